```python
import math
import jax, jax.numpy as jnp
from jax import lax
import numpy as np

D_MODEL = 1024
BATCH = 8
SEQ = 2048
DEPTH = 2

N_MIXERS = 2
N_SSD_LAYERS = (DEPTH + 1) // 2
N_CONV_LAYERS = DEPTH // 2
NORM_EPS = 1e-5
ADA_MODS = 6

SSD_EXPAND = 2
SSD_D_INNER = SSD_EXPAND * D_MODEL
SSD_HEAD_DIM = 64
SSD_N_HEADS = SSD_D_INNER // SSD_HEAD_DIM
SSD_N_GROUPS = 4
SSD_HEADS_PER_GROUP = SSD_N_HEADS // SSD_N_GROUPS
SSD_D_STATE = 128
SSD_CONV_K = 4
SSD_CHUNK = 128
SSD_CONV_DIM = SSD_D_INNER + 2 * SSD_N_GROUPS * SSD_D_STATE
SSD_IN_DIM = SSD_D_INNER + SSD_CONV_DIM + SSD_N_HEADS
DT_MIN = 1e-3
DT_MAX = 1e-1

SC_WIDTH = D_MODEL
SC_CONV_K = 3

D_FF = 4 * D_MODEL

kernel_name = "hybrid_ssd_shortconv_adaln_trunk"


def rmsnorm(x, g, eps=NORM_EPS):
    xf = x.astype(jnp.float32)
    y = xf * lax.rsqrt(jnp.mean(xf * xf, axis=-1, keepdims=True) + eps)
    return (y * g.astype(jnp.float32)).astype(x.dtype)


def causal_dwconv(x, w, b=None):
    k, ch = w.shape
    out = lax.conv_general_dilated(
        x, w[:, None, :].astype(x.dtype), window_strides=(1,), padding=[(k - 1, 0)],
        dimension_numbers=("NWC", "WIO", "NWC"), feature_group_count=ch)
    if b is not None:
        out = out + b.astype(x.dtype)
    return out


def ssd_chunked(xs, dt, A, Bm, Cm):
    b, L, g, r, p = xs.shape
    n = Bm.shape[-1]
    nc = L // SSD_CHUNK
    xs = xs.astype(jnp.float32).reshape(b, nc, SSD_CHUNK, g, r, p)
    dt = dt.reshape(b, nc, SSD_CHUNK, g, r)
    Bc = Bm.astype(jnp.float32).reshape(b, nc, SSD_CHUNK, g, n)
    Cc = Cm.astype(jnp.float32).reshape(b, nc, SSD_CHUNK, g, n)
    X = xs * dt[..., None]
    Acs = jnp.cumsum(dt * A, axis=2)

    causal = jnp.tril(jnp.ones((SSD_CHUNK, SSD_CHUNK), dtype=bool))[:, :, None, None]
    seg = Acs[:, :, :, None] - Acs[:, :, None, :]
    Lmat = jnp.exp(jnp.where(causal, seg, -jnp.inf))
    scores = jnp.einsum("bclgn,bcsgn->bclsg", Cc, Bc)
    M = scores[..., None] * Lmat
    y_diag = jnp.einsum("bclsgr,bcsgrp->bclgrp", M, X)

    decay_states = jnp.exp(Acs[:, :, -1:] - Acs)
    states = jnp.einsum("bcsgn,bcsgrp->bcgrpn", Bc, X * decay_states[..., None])

    chunk_decay = jnp.exp(Acs[:, :, -1])

    def step(carry, inp):
        st, dec = inp
        return carry * dec[..., None, None] + st, carry

    init = jnp.zeros((b, g, r, p, n), dtype=states.dtype)
    _, prev = lax.scan(step, init, (jnp.moveaxis(states, 1, 0), jnp.moveaxis(chunk_decay, 1, 0)))
    prev = jnp.moveaxis(prev, 0, 1)

    y_off = jnp.einsum("bclgn,bcgrpn->bclgrp", Cc, prev) * jnp.exp(Acs)[..., None]
    return (y_diag + y_off).reshape(b, L, g, r, p)


def ssd_mixer(h, in_w, conv_w, conv_b, dt_bias, A_log, D_skip, norm_w, out_w):
    b, L, _ = h.shape
    G, R, P, N = SSD_N_GROUPS, SSD_HEADS_PER_GROUP, SSD_HEAD_DIM, SSD_D_STATE
    zxbcdt = h @ in_w
    z = zxbcdt[..., :SSD_D_INNER]
    xBC = zxbcdt[..., SSD_D_INNER:SSD_D_INNER + SSD_CONV_DIM]
    dt_raw = zxbcdt[..., SSD_D_INNER + SSD_CONV_DIM:]
    xBC = jax.nn.silu(causal_dwconv(xBC, conv_w, conv_b))
    xs = xBC[..., :SSD_D_INNER].reshape(b, L, G, R, P)
    Bm = xBC[..., SSD_D_INNER:SSD_D_INNER + G * N].reshape(b, L, G, N)
    Cm = xBC[..., SSD_D_INNER + G * N:].reshape(b, L, G, N)
    dt = jax.nn.softplus(dt_raw.astype(jnp.float32) + dt_bias.astype(jnp.float32)).reshape(b, L, G, R)
    A = -jnp.exp(A_log.astype(jnp.float32)).reshape(G, R)
    y = ssd_chunked(xs, dt, A, Bm, Cm)
    y = y + D_skip.astype(jnp.float32).reshape(G, R, 1) * xs.astype(jnp.float32)
    yg = y.reshape(b, L, SSD_D_INNER) * jax.nn.silu(z.astype(jnp.float32))
    yg = yg.reshape(b, L, G, SSD_D_INNER // G)
    yg = yg * lax.rsqrt(jnp.mean(yg * yg, axis=-1, keepdims=True) + NORM_EPS)
    yg = yg.reshape(b, L, SSD_D_INNER) * norm_w.astype(jnp.float32)
    return yg.astype(h.dtype) @ out_w


def short_conv_mixer(h, in_w, conv_w, out_w):
    proj = h @ in_w
    Bg, Cg, xv = jnp.split(proj, 3, axis=-1)
    y = Bg * causal_dwconv(Cg * xv, conv_w)
    return y @ out_w


def sqrelu_mlp(h, up_w, down_w):
    a = jax.nn.relu(h @ up_w)
    return (a * a) @ down_w


def _fwd_setup_inputs(seed: int = 0) -> dict:
    key = jax.random.key(seed)
    ks = jax.random.split(key, 24)
    f32 = jnp.float32
    D = D_MODEL
    nrm = lambda k, shape, s: jax.random.normal(k, shape, f32) * s
    x = jax.random.normal(ks[0], (BATCH, SEQ, D), f32)
    c = jax.random.normal(ks[1], (BATCH, D), f32)
    ada_w = nrm(ks[2], (DEPTH, D, ADA_MODS * D), 0.5 * D ** -0.5)
    ada_b = nrm(ks[3], (DEPTH, ADA_MODS * D), 0.02)
    mix_norm_w = 1.0 + nrm(ks[4], (DEPTH, D), 0.02)
    mlp_norm_w = 1.0 + nrm(ks[5], (DEPTH, D), 0.02)
    mlp_up = nrm(ks[6], (DEPTH, D, D_FF), D ** -0.5)
    mlp_down = nrm(ks[7], (DEPTH, D_FF, D), D_FF ** -0.5)
    ssd_in_w = nrm(ks[8], (N_SSD_LAYERS, D, SSD_IN_DIM), D ** -0.5)
    ssd_conv_w = nrm(ks[9], (N_SSD_LAYERS, SSD_CONV_K, SSD_CONV_DIM), SSD_CONV_K ** -0.5)
    ssd_conv_b = nrm(ks[10], (N_SSD_LAYERS, SSD_CONV_DIM), 0.02)
    u = jax.random.uniform(ks[11], (N_SSD_LAYERS, SSD_N_HEADS), f32)
    dt0 = jnp.exp(u * (math.log(DT_MAX) - math.log(DT_MIN)) + math.log(DT_MIN))
    ssd_dt_bias = dt0 + jnp.log(-jnp.expm1(-dt0))
    ssd_A_log = jnp.log(jax.random.uniform(ks[12], (N_SSD_LAYERS, SSD_N_HEADS), f32, 1.0, 16.0))
    ssd_D = 1.0 + nrm(ks[13], (N_SSD_LAYERS, SSD_N_HEADS), 0.02)
    ssd_norm_w = 1.0 + nrm(ks[14], (N_SSD_LAYERS, SSD_D_INNER), 0.02)
    ssd_out_w = nrm(ks[15], (N_SSD_LAYERS, SSD_D_INNER, D), SSD_D_INNER ** -0.5)
    sc_in_w = nrm(ks[16], (N_CONV_LAYERS, D, 3 * SC_WIDTH), D ** -0.5)
    sc_conv_w = nrm(ks[17], (N_CONV_LAYERS, SC_CONV_K, SC_WIDTH), SC_CONV_K ** -0.5)
    sc_out_w = nrm(ks[18], (N_CONV_LAYERS, SC_WIDTH, D), SC_WIDTH ** -0.5)
    final_norm_w = 1.0 + nrm(ks[19], (D,), 0.02)
    return {"x": x, "c": c, "ada_w": ada_w, "ada_b": ada_b,
            "mix_norm_w": mix_norm_w, "mlp_norm_w": mlp_norm_w,
            "mlp_up": mlp_up, "mlp_down": mlp_down,
            "ssd_in_w": ssd_in_w, "ssd_conv_w": ssd_conv_w, "ssd_conv_b": ssd_conv_b,
            "ssd_dt_bias": ssd_dt_bias, "ssd_A_log": ssd_A_log, "ssd_D": ssd_D,
            "ssd_norm_w": ssd_norm_w, "ssd_out_w": ssd_out_w,
            "sc_in_w": sc_in_w, "sc_conv_w": sc_conv_w, "sc_out_w": sc_out_w,
            "final_norm_w": final_norm_w}


def _fwd_reference(x, c, ada_w, ada_b, mix_norm_w, mlp_norm_w, mlp_up, mlp_down,
              ssd_in_w, ssd_conv_w, ssd_conv_b, ssd_dt_bias, ssd_A_log, ssd_D,
              ssd_norm_w, ssd_out_w, sc_in_w, sc_conv_w, sc_out_w, final_norm_w):
    cond = jax.nn.silu(c.astype(x.dtype))
    for i in range(DEPTH):
        mod = cond @ ada_w[i] + ada_b[i]
        sh_m, sc_m, g_m, sh_f, sc_f, g_f = [m[:, None, :] for m in jnp.split(mod, ADA_MODS, axis=-1)]
        h = rmsnorm(x, mix_norm_w[i]) * (1.0 + sc_m) + sh_m
        j = i // N_MIXERS
        if i % N_MIXERS == 0:
            y = ssd_mixer(h, ssd_in_w[j], ssd_conv_w[j], ssd_conv_b[j], ssd_dt_bias[j],
                          ssd_A_log[j], ssd_D[j], ssd_norm_w[j], ssd_out_w[j])
        else:
            y = short_conv_mixer(h, sc_in_w[j], sc_conv_w[j], sc_out_w[j])
        x = x + g_m * y
        h = rmsnorm(x, mlp_norm_w[i]) * (1.0 + sc_f) + sh_f
        x = x + g_f * sqrelu_mlp(h, mlp_up[i], mlp_down[i])
    return rmsnorm(x, final_norm_w)


import jax as _jax
import jax.numpy as _jnp

TWIN_FORMAT = 'train_step'
FWD_PARAMS = ['x', 'c', 'ada_w', 'ada_b', 'mix_norm_w', 'mlp_norm_w', 'mlp_up', 'mlp_down', 'ssd_in_w', 'ssd_conv_w', 'ssd_conv_b', 'ssd_dt_bias', 'ssd_A_log', 'ssd_D', 'ssd_norm_w', 'ssd_out_w', 'sc_in_w', 'sc_conv_w', 'sc_out_w', 'final_norm_w']
TWIN_WEIGHTS = ['ada_w', 'ada_b', 'mix_norm_w', 'mlp_norm_w', 'mlp_up', 'mlp_down', 'ssd_in_w', 'ssd_conv_w', 'ssd_conv_b', 'ssd_dt_bias', 'ssd_A_log', 'ssd_D', 'ssd_norm_w', 'ssd_out_w', 'sc_in_w', 'sc_conv_w', 'sc_out_w', 'final_norm_w']
TWIN_DIFF_INPUT = 'x'
TWIN_INPUTS = ['x', 'c', 'ada_w', 'ada_b', 'mix_norm_w', 'mlp_norm_w', 'mlp_up', 'mlp_down', 'ssd_in_w', 'ssd_conv_w', 'ssd_conv_b', 'ssd_dt_bias', 'ssd_A_log', 'ssd_D', 'ssd_norm_w', 'ssd_out_w', 'sc_in_w', 'sc_conv_w', 'sc_out_w', 'final_norm_w', 'loss_target', 'm_ada_w', 'm_ada_b', 'm_mix_norm_w', 'm_mlp_norm_w', 'm_mlp_up', 'm_mlp_down', 'm_ssd_in_w', 'm_ssd_conv_w', 'm_ssd_conv_b', 'm_ssd_dt_bias', 'm_ssd_A_log', 'm_ssd_D', 'm_ssd_norm_w', 'm_ssd_out_w', 'm_sc_in_w', 'm_sc_conv_w', 'm_sc_out_w', 'm_final_norm_w', 'v_ada_w', 'v_ada_b', 'v_mix_norm_w', 'v_mlp_norm_w', 'v_mlp_up', 'v_mlp_down', 'v_ssd_in_w', 'v_ssd_conv_w', 'v_ssd_conv_b', 'v_ssd_dt_bias', 'v_ssd_A_log', 'v_ssd_D', 'v_ssd_norm_w', 'v_ssd_out_w', 'v_sc_in_w', 'v_sc_conv_w', 'v_sc_out_w', 'v_final_norm_w']
TWIN_OUTPUTS = ['loss', 'grad_x', 'grad_ada_w', 'grad_ada_b', 'grad_mix_norm_w', 'grad_mlp_norm_w', 'grad_mlp_up', 'grad_mlp_down', 'grad_ssd_in_w', 'grad_ssd_conv_w', 'grad_ssd_conv_b', 'grad_ssd_dt_bias', 'grad_ssd_A_log', 'grad_ssd_D', 'grad_ssd_norm_w', 'grad_ssd_out_w', 'grad_sc_in_w', 'grad_sc_conv_w', 'grad_sc_out_w', 'grad_final_norm_w', 'delta_ada_w', 'delta_ada_b', 'delta_mix_norm_w', 'delta_mlp_norm_w', 'delta_mlp_up', 'delta_mlp_down', 'delta_ssd_in_w', 'delta_ssd_conv_w', 'delta_ssd_conv_b', 'delta_ssd_dt_bias', 'delta_ssd_A_log', 'delta_ssd_D', 'delta_ssd_norm_w', 'delta_ssd_out_w', 'delta_sc_in_w', 'delta_sc_conv_w', 'delta_sc_out_w', 'delta_final_norm_w', 'new_m_ada_w', 'new_m_ada_b', 'new_m_mix_norm_w', 'new_m_mlp_norm_w', 'new_m_mlp_up', 'new_m_mlp_down', 'new_m_ssd_in_w', 'new_m_ssd_conv_w', 'new_m_ssd_conv_b', 'new_m_ssd_dt_bias', 'new_m_ssd_A_log', 'new_m_ssd_D', 'new_m_ssd_norm_w', 'new_m_ssd_out_w', 'new_m_sc_in_w', 'new_m_sc_conv_w', 'new_m_sc_out_w', 'new_m_final_norm_w', 'new_v_ada_w', 'new_v_ada_b', 'new_v_mix_norm_w', 'new_v_mlp_norm_w', 'new_v_mlp_up', 'new_v_mlp_down', 'new_v_ssd_in_w', 'new_v_ssd_conv_w', 'new_v_ssd_conv_b', 'new_v_ssd_dt_bias', 'new_v_ssd_A_log', 'new_v_ssd_D', 'new_v_ssd_norm_w', 'new_v_ssd_out_w', 'new_v_sc_in_w', 'new_v_sc_conv_w', 'new_v_sc_out_w', 'new_v_final_norm_w']
TWIN_LEAF_KINDS = {'loss': 'loss', 'grad_x': 'grad_x', 'grad_ada_w': 'grad_w', 'grad_ada_b': 'grad_w', 'grad_mix_norm_w': 'grad_w', 'grad_mlp_norm_w': 'grad_w', 'grad_mlp_up': 'grad_w', 'grad_mlp_down': 'grad_w', 'grad_ssd_in_w': 'grad_w', 'grad_ssd_conv_w': 'grad_w', 'grad_ssd_conv_b': 'grad_w', 'grad_ssd_dt_bias': 'grad_w', 'grad_ssd_A_log': 'grad_w', 'grad_ssd_D': 'grad_w', 'grad_ssd_norm_w': 'grad_w', 'grad_ssd_out_w': 'grad_w', 'grad_sc_in_w': 'grad_w', 'grad_sc_conv_w': 'grad_w', 'grad_sc_out_w': 'grad_w', 'grad_final_norm_w': 'grad_w', 'delta_ada_w': 'delta_w', 'delta_ada_b': 'delta_w', 'delta_mix_norm_w': 'delta_w', 'delta_mlp_norm_w': 'delta_w', 'delta_mlp_up': 'delta_w', 'delta_mlp_down': 'delta_w', 'delta_ssd_in_w': 'delta_w', 'delta_ssd_conv_w': 'delta_w', 'delta_ssd_conv_b': 'delta_w', 'delta_ssd_dt_bias': 'delta_w', 'delta_ssd_A_log': 'delta_w', 'delta_ssd_D': 'delta_w', 'delta_ssd_norm_w': 'delta_w', 'delta_ssd_out_w': 'delta_w', 'delta_sc_in_w': 'delta_w', 'delta_sc_conv_w': 'delta_w', 'delta_sc_out_w': 'delta_w', 'delta_final_norm_w': 'delta_w', 'new_m_ada_w': 'new_m', 'new_m_ada_b': 'new_m', 'new_m_mix_norm_w': 'new_m', 'new_m_mlp_norm_w': 'new_m', 'new_m_mlp_up': 'new_m', 'new_m_mlp_down': 'new_m', 'new_m_ssd_in_w': 'new_m', 'new_m_ssd_conv_w': 'new_m', 'new_m_ssd_conv_b': 'new_m', 'new_m_ssd_dt_bias': 'new_m', 'new_m_ssd_A_log': 'new_m', 'new_m_ssd_D': 'new_m', 'new_m_ssd_norm_w': 'new_m', 'new_m_ssd_out_w': 'new_m', 'new_m_sc_in_w': 'new_m', 'new_m_sc_conv_w': 'new_m', 'new_m_sc_out_w': 'new_m', 'new_m_final_norm_w': 'new_m', 'new_v_ada_w': 'new_v', 'new_v_ada_b': 'new_v', 'new_v_mix_norm_w': 'new_v', 'new_v_mlp_norm_w': 'new_v', 'new_v_mlp_up': 'new_v', 'new_v_mlp_down': 'new_v', 'new_v_ssd_in_w': 'new_v', 'new_v_ssd_conv_w': 'new_v', 'new_v_ssd_conv_b': 'new_v', 'new_v_ssd_dt_bias': 'new_v', 'new_v_ssd_A_log': 'new_v', 'new_v_ssd_D': 'new_v', 'new_v_ssd_norm_w': 'new_v', 'new_v_ssd_out_w': 'new_v', 'new_v_sc_in_w': 'new_v', 'new_v_sc_conv_w': 'new_v', 'new_v_sc_out_w': 'new_v', 'new_v_final_norm_w': 'new_v'}


def _forward(args):
    return _fwd_reference(*[args[k] for k in FWD_PARAMS])


def _output_shape():
    out = _jax.eval_shape(lambda: _forward(_fwd_setup_inputs(0)))
    return out.shape, out.dtype

N_MICROBATCH = 1
ADAM_LR = 0.001
ADAM_B1 = 0.9
ADAM_B2 = 0.999
ADAM_EPS = 1e-08
ADAM_WD = 0.01
ADAM_STEP = 10
PER_EXAMPLE_BATCH_AXIS = {'x': 0, 'c': 0, 'loss_target': 0}
SHARED_INPUTS = []
_WEIGHT_DTYPES = {'ada_w': _jnp.float32, 'ada_b': _jnp.float32, 'mix_norm_w': _jnp.float32, 'mlp_norm_w': _jnp.float32, 'mlp_up': _jnp.float32, 'mlp_down': _jnp.float32, 'ssd_in_w': _jnp.float32, 'ssd_conv_w': _jnp.float32, 'ssd_conv_b': _jnp.float32, 'ssd_dt_bias': _jnp.float32, 'ssd_A_log': _jnp.float32, 'ssd_D': _jnp.float32, 'ssd_norm_w': _jnp.float32, 'ssd_out_w': _jnp.float32, 'sc_in_w': _jnp.float32, 'sc_conv_w': _jnp.float32, 'sc_out_w': _jnp.float32, 'final_norm_w': _jnp.float32}
MOMENT_SCALE = {'ada_w': 6.687647e-02, 'ada_b': 1.125716e-01, 'mix_norm_w': 6.926307e-02, 'mlp_norm_w': 5.456199e-02, 'mlp_up': 2.847945e-02, 'mlp_down': 5.035954e-02, 'ssd_in_w': 2.856388e-02, 'ssd_conv_w': 2.574947e-02, 'ssd_conv_b': 3.079816e-02, 'ssd_dt_bias': 6.850148e-02, 'ssd_A_log': 1.099661e-01, 'ssd_D': 1.735072e-01, 'ssd_norm_w': 3.017696e-02, 'ssd_out_w': 4.172926e-02, 'sc_in_w': 4.311908e-02, 'sc_conv_w': 4.318673e-02, 'sc_out_w': 4.313199e-02, 'final_norm_w': 1.622687e+01}


def _to_microbatches(a, axis):
    t = _jnp.moveaxis(a, axis, 0)
    t = t.reshape((N_MICROBATCH, t.shape[0] // N_MICROBATCH) + t.shape[1:])
    return _jnp.moveaxis(t, 1, axis + 1)


def setup_inputs(seed: int = 0) -> dict:
    inp = _fwd_setup_inputs(seed)
    key = _jax.random.fold_in(_jax.random.key(seed), 7919)
    shape, _ = _output_shape()
    out = dict(inp)
    out["loss_target"] = _jax.random.normal(_jax.random.fold_in(key, 0), shape, _jnp.float32)
    for i, name in enumerate(TWIN_WEIGHTS):
        w = inp[name].astype(_jnp.float32)
        if MOMENT_SCALE is None:
            s = _jnp.sqrt(_jnp.mean(_jnp.square(w)) + 1e-30)
        else:
            s = MOMENT_SCALE[name]
        km, kv = _jax.random.split(_jax.random.fold_in(key, i + 1))
        out[name] = w
        out["m_" + name] = s * _jax.random.normal(km, w.shape, _jnp.float32)
        out["v_" + name] = (s * s) * _jax.random.uniform(kv, w.shape, _jnp.float32, 0.5, 1.5)
    if N_MICROBATCH > 1:
        for name, axis in PER_EXAMPLE_BATCH_AXIS.items():
            out[name] = _to_microbatches(out[name], axis)
    return {'x': out['x'], 'c': out['c'], 'ada_w': out['ada_w'], 'ada_b': out['ada_b'], 'mix_norm_w': out['mix_norm_w'], 'mlp_norm_w': out['mlp_norm_w'], 'mlp_up': out['mlp_up'], 'mlp_down': out['mlp_down'], 'ssd_in_w': out['ssd_in_w'], 'ssd_conv_w': out['ssd_conv_w'], 'ssd_conv_b': out['ssd_conv_b'], 'ssd_dt_bias': out['ssd_dt_bias'], 'ssd_A_log': out['ssd_A_log'], 'ssd_D': out['ssd_D'], 'ssd_norm_w': out['ssd_norm_w'], 'ssd_out_w': out['ssd_out_w'], 'sc_in_w': out['sc_in_w'], 'sc_conv_w': out['sc_conv_w'], 'sc_out_w': out['sc_out_w'], 'final_norm_w': out['final_norm_w'], 'loss_target': out['loss_target'], 'm_ada_w': out['m_ada_w'], 'm_ada_b': out['m_ada_b'], 'm_mix_norm_w': out['m_mix_norm_w'], 'm_mlp_norm_w': out['m_mlp_norm_w'], 'm_mlp_up': out['m_mlp_up'], 'm_mlp_down': out['m_mlp_down'], 'm_ssd_in_w': out['m_ssd_in_w'], 'm_ssd_conv_w': out['m_ssd_conv_w'], 'm_ssd_conv_b': out['m_ssd_conv_b'], 'm_ssd_dt_bias': out['m_ssd_dt_bias'], 'm_ssd_A_log': out['m_ssd_A_log'], 'm_ssd_D': out['m_ssd_D'], 'm_ssd_norm_w': out['m_ssd_norm_w'], 'm_ssd_out_w': out['m_ssd_out_w'], 'm_sc_in_w': out['m_sc_in_w'], 'm_sc_conv_w': out['m_sc_conv_w'], 'm_sc_out_w': out['m_sc_out_w'], 'm_final_norm_w': out['m_final_norm_w'], 'v_ada_w': out['v_ada_w'], 'v_ada_b': out['v_ada_b'], 'v_mix_norm_w': out['v_mix_norm_w'], 'v_mlp_norm_w': out['v_mlp_norm_w'], 'v_mlp_up': out['v_mlp_up'], 'v_mlp_down': out['v_mlp_down'], 'v_ssd_in_w': out['v_ssd_in_w'], 'v_ssd_conv_w': out['v_ssd_conv_w'], 'v_ssd_conv_b': out['v_ssd_conv_b'], 'v_ssd_dt_bias': out['v_ssd_dt_bias'], 'v_ssd_A_log': out['v_ssd_A_log'], 'v_ssd_D': out['v_ssd_D'], 'v_ssd_norm_w': out['v_ssd_norm_w'], 'v_ssd_out_w': out['v_ssd_out_w'], 'v_sc_in_w': out['v_sc_in_w'], 'v_sc_conv_w': out['v_sc_conv_w'], 'v_sc_out_w': out['v_sc_out_w'], 'v_final_norm_w': out['v_final_norm_w']}


def _loss(weights, diff, rest, loss_target):
    with _jax.named_scope("forward"):
        args = {**rest, TWIN_DIFF_INPUT: diff, **{k: w.astype(_WEIGHT_DTYPES[k]) for k, w in weights.items()}}
        y = _forward(args)
    with _jax.named_scope("loss_head"):
        err = _jnp.square(y.astype(_jnp.float32) - loss_target)
        return 0.5 * _jnp.sum(_jnp.mean(err, axis=-1)) if err.ndim else 0.5 * err


def _adamw(w, g, m, v):
    m = ADAM_B1 * m + (1.0 - ADAM_B1) * g
    v = ADAM_B2 * v + (1.0 - ADAM_B2) * _jnp.square(g)
    m_hat = m / (1.0 - ADAM_B1 ** ADAM_STEP)
    v_hat = v / (1.0 - ADAM_B2 ** ADAM_STEP)
    delta = -ADAM_LR * (m_hat / (_jnp.sqrt(v_hat) + ADAM_EPS) + ADAM_WD * w)
    return delta, m, v


def reference(x, c, ada_w, ada_b, mix_norm_w, mlp_norm_w, mlp_up, mlp_down, ssd_in_w, ssd_conv_w, ssd_conv_b, ssd_dt_bias, ssd_A_log, ssd_D, ssd_norm_w, ssd_out_w, sc_in_w, sc_conv_w, sc_out_w, final_norm_w, loss_target, m_ada_w, m_ada_b, m_mix_norm_w, m_mlp_norm_w, m_mlp_up, m_mlp_down, m_ssd_in_w, m_ssd_conv_w, m_ssd_conv_b, m_ssd_dt_bias, m_ssd_A_log, m_ssd_D, m_ssd_norm_w, m_ssd_out_w, m_sc_in_w, m_sc_conv_w, m_sc_out_w, m_final_norm_w, v_ada_w, v_ada_b, v_mix_norm_w, v_mlp_norm_w, v_mlp_up, v_mlp_down, v_ssd_in_w, v_ssd_conv_w, v_ssd_conv_b, v_ssd_dt_bias, v_ssd_A_log, v_ssd_D, v_ssd_norm_w, v_ssd_out_w, v_sc_in_w, v_sc_conv_w, v_sc_out_w, v_final_norm_w):
    given = dict(x=x, c=c, ada_w=ada_w, ada_b=ada_b, mix_norm_w=mix_norm_w, mlp_norm_w=mlp_norm_w, mlp_up=mlp_up, mlp_down=mlp_down, ssd_in_w=ssd_in_w, ssd_conv_w=ssd_conv_w, ssd_conv_b=ssd_conv_b, ssd_dt_bias=ssd_dt_bias, ssd_A_log=ssd_A_log, ssd_D=ssd_D, ssd_norm_w=ssd_norm_w, ssd_out_w=ssd_out_w, sc_in_w=sc_in_w, sc_conv_w=sc_conv_w, sc_out_w=sc_out_w, final_norm_w=final_norm_w, loss_target=loss_target, m_ada_w=m_ada_w, m_ada_b=m_ada_b, m_mix_norm_w=m_mix_norm_w, m_mlp_norm_w=m_mlp_norm_w, m_mlp_up=m_mlp_up, m_mlp_down=m_mlp_down, m_ssd_in_w=m_ssd_in_w, m_ssd_conv_w=m_ssd_conv_w, m_ssd_conv_b=m_ssd_conv_b, m_ssd_dt_bias=m_ssd_dt_bias, m_ssd_A_log=m_ssd_A_log, m_ssd_D=m_ssd_D, m_ssd_norm_w=m_ssd_norm_w, m_ssd_out_w=m_ssd_out_w, m_sc_in_w=m_sc_in_w, m_sc_conv_w=m_sc_conv_w, m_sc_out_w=m_sc_out_w, m_final_norm_w=m_final_norm_w, v_ada_w=v_ada_w, v_ada_b=v_ada_b, v_mix_norm_w=v_mix_norm_w, v_mlp_norm_w=v_mlp_norm_w, v_mlp_up=v_mlp_up, v_mlp_down=v_mlp_down, v_ssd_in_w=v_ssd_in_w, v_ssd_conv_w=v_ssd_conv_w, v_ssd_conv_b=v_ssd_conv_b, v_ssd_dt_bias=v_ssd_dt_bias, v_ssd_A_log=v_ssd_A_log, v_ssd_D=v_ssd_D, v_ssd_norm_w=v_ssd_norm_w, v_ssd_out_w=v_ssd_out_w, v_sc_in_w=v_sc_in_w, v_sc_conv_w=v_sc_conv_w, v_sc_out_w=v_sc_out_w, v_final_norm_w=v_final_norm_w)
    weights = {n: given[n] for n in TWIN_WEIGHTS}
    shared = {n: given[n] for n in SHARED_INPUTS}
    per_example = {n: given[n] for n in ['x', 'c']}
    grad_fn = _jax.value_and_grad(_loss, argnums=(0, 1))

    def one_microbatch(ex, loss_target):
        ex = dict(ex)
        diff = ex.pop(TWIN_DIFF_INPUT)
        return grad_fn(weights, diff, {**shared, **ex}, loss_target)

    if N_MICROBATCH == 1:
        loss, (grad_w, grad_x) = one_microbatch(per_example, given["loss_target"])
    else:
        def body(carry, xs):
            loss_sum, grad_sum = carry
            l_k, (gw_k, gx_k) = one_microbatch(xs[0], xs[1])
            with _jax.named_scope("update"):
                return (loss_sum + l_k, _jax.tree.map(_jnp.add, grad_sum, gw_k)), gx_k

        init = (_jnp.zeros((), _jnp.float32), _jax.tree.map(_jnp.zeros_like, weights))
        (loss, grad_w), grad_x = _jax.lax.scan(body, init, (per_example, given["loss_target"]))
    with _jax.named_scope("update"):
        delta_w, new_m, new_v = {}, {}, {}
        for n in TWIN_WEIGHTS:
            delta_w[n], new_m[n], new_v[n] = _adamw(weights[n], grad_w[n], given["m_" + n], given["v_" + n])
    return (loss, grad_x, *[grad_w[n] for n in TWIN_WEIGHTS], *[delta_w[n] for n in TWIN_WEIGHTS],
            *[new_m[n] for n in TWIN_WEIGHTS], *[new_v[n] for n in TWIN_WEIGHTS])
```

```python
import functools

import jax
import jax.numpy as jnp
from jax import lax
from jax.experimental import pallas as pl
from jax.experimental.pallas import tpu as pltpu

F32 = jnp.float32
BF16 = jnp.bfloat16
MESH = pl.DeviceIdType.MESH
HIGHEST = lax.Precision.HIGHEST

D = 1024
DFF = 4096
DI = 2048
NH = 32
HP = 64
NG = 4
NS = 128
CH = 128
CONVD = DI + 2 * NG * NS
ZX = DI + CONVD
GW = NG * NS
LANES = 128
N_CHIPS = 4
N_DEV = 8
EPS = 1e-5
ADAM_LR, ADAM_B1, ADAM_B2, ADAM_EPS, ADAM_WD, ADAM_STEP = 1e-3, 0.9, 0.999, 1e-8, 0.01, 10
VMEM_LIMIT = 48 * 1024 * 1024


def _params(sem=None):
    return pltpu.CompilerParams(dimension_semantics=sem, vmem_limit_bytes=VMEM_LIMIT)


def _sigmoid(v):
    return 1.0 / (1.0 + jnp.exp(-v))


def _dot(a, b, dims=((1,), (0,)), precision=None):
    return lax.dot_general(a, b, (dims, ((), ())), preferred_element_type=F32, precision=precision)


def _dot_nt(a, b):
    return _dot(a, b, ((1,), (1,)))


def _dot_tn(a, b):
    return _dot(a, b, ((0,), (0,)))


def _matmul(a, b, *, name, trans_b=False, tm=512, tn=512, extras=(), epi=None, out_dtypes=(F32,), a_silu=False):
    M, K = a.shape
    N = b.shape[0] if trans_b else b.shape[1]
    tm, tn = min(tm, M), min(tn, N)
    assert M % tm == 0 and N % tn == 0, (name, M, N, tm, tn)
    n_ex = len(extras)

    def body(*refs):
        a_ref, b_ref = refs[:2]
        av = a_ref[...]
        if a_silu:
            av = av * _sigmoid(av)
        acc = _dot(av.astype(BF16), b_ref[...].astype(BF16), ((1,), (1,)) if trans_b else ((1,), (0,)))
        res = epi(acc, *[r[...] for r in refs[2:2 + n_ex]]) if epi is not None else (acc,)
        for o_ref, r in zip(refs[2 + n_ex:], res, strict=True):
            o_ref[...] = r.astype(o_ref.dtype)

    in_specs = [pl.BlockSpec((tm, K), lambda i, j: (i, 0)),
                pl.BlockSpec((tn, K), lambda i, j: (j, 0)) if trans_b else pl.BlockSpec((K, tn), lambda i, j: (0, j))]
    for e in extras:
        in_specs.append(pl.BlockSpec((1, tn), lambda i, j: (0, j)) if e.shape[0] == 1 and M != 1
                        else pl.BlockSpec((tm, tn), lambda i, j: (i, j)))
    outs = pl.pallas_call(
        body, grid=(M // tm, N // tn), in_specs=in_specs,
        out_specs=[pl.BlockSpec((tm, tn), lambda i, j: (i, j)) for _ in out_dtypes],
        out_shape=[jax.ShapeDtypeStruct((M, N), dt) for dt in out_dtypes],
        compiler_params=_params(("parallel", "parallel")), name=name)(a, b, *extras)
    return outs if len(out_dtypes) > 1 else outs[0]


def _matmul_tn(a, b, *, name, tm=512, tn=512, a_silu=False):
    T, M = a.shape
    N = b.shape[1]
    tm, tn = min(tm, M), min(tn, N)
    assert M % tm == 0 and N % tn == 0, (name, M, N, tm, tn)

    def body(a_ref, b_ref, o_ref):
        av = a_ref[...]
        if a_silu:
            av = av * _sigmoid(av)
        o_ref[...] = _dot_tn(av.astype(BF16), b_ref[...].astype(BF16))

    return pl.pallas_call(
        body, grid=(M // tm, N // tn),
        in_specs=[pl.BlockSpec((T, tm), lambda i, j: (0, i)), pl.BlockSpec((T, tn), lambda i, j: (0, j))],
        out_specs=pl.BlockSpec((tm, tn), lambda i, j: (i, j)),
        out_shape=jax.ShapeDtypeStruct((M, N), F32),
        compiler_params=_params(("parallel", "parallel")), name=name)(a, b)


def _modnorm_fwd(x, nw, sc, sh, *, name):
    L = x.shape[0]
    tm = min(L, 512)

    def body(x_ref, nw_ref, sc_ref, sh_ref, h_ref):
        xv = x_ref[...]
        r = lax.rsqrt(jnp.mean(xv * xv, axis=-1, keepdims=True) + EPS)
        h_ref[...] = ((xv * r * nw_ref[...]) * (1.0 + sc_ref[...]) + sh_ref[...]).astype(BF16)

    row = pl.BlockSpec((tm, D), lambda i: (i, 0))
    vec = pl.BlockSpec((1, D), lambda i: (0, 0))
    return pl.pallas_call(body, grid=(L // tm,), in_specs=[row, vec, vec, vec], out_specs=row,
                          out_shape=jax.ShapeDtypeStruct((L, D), BF16),
                          compiler_params=_params(("parallel",)), name=name)(x, nw, sc, sh)


def _modnorm_bwd(x, dh, dxo, nw, sc, *, name):
    L = x.shape[0]
    tm = min(L, 256)

    def body(x_ref, dh_ref, dxo_ref, nw_ref, sc_ref, dx_ref, s_ref):
        @pl.when(pl.program_id(0) == 0)
        def _():
            s_ref[...] = jnp.zeros_like(s_ref)

        xv, dhv = x_ref[...], dh_ref[...]
        r = lax.rsqrt(jnp.mean(xv * xv, axis=-1, keepdims=True) + EPS)
        xhat = xv * r
        dxhat = dhv * (nw_ref[...] * (1.0 + sc_ref[...]))
        dx_ref[...] = dxo_ref[...] + r * (dxhat - xhat * jnp.mean(dxhat * xhat, axis=-1, keepdims=True))
        s_ref[0:1, :] += jnp.sum(dhv * xhat, axis=0, keepdims=True) * (1.0 + sc_ref[...])
        s_ref[1:2, :] += jnp.sum(dhv * xhat, axis=0, keepdims=True) * nw_ref[...]
        s_ref[2:3, :] += jnp.sum(dhv, axis=0, keepdims=True)

    row = pl.BlockSpec((tm, D), lambda i: (i, 0))
    vec = pl.BlockSpec((1, D), lambda i: (0, 0))
    return pl.pallas_call(body, grid=(L // tm,), in_specs=[row, row, row, vec, vec],
                          out_specs=[row, pl.BlockSpec((8, D), lambda i: (0, 0))],
                          out_shape=[jax.ShapeDtypeStruct((L, D), F32), jax.ShapeDtypeStruct((8, D), F32)],
                          compiler_params=_params(("arbitrary",)), name=name)(x, dh, dxo, nw, sc)


def _gate_bwd(dxo, y, g, *, name):
    L = dxo.shape[0]
    tm = min(L, 512)

    def body(dxo_ref, y_ref, g_ref, dy_ref, s_ref):
        @pl.when(pl.program_id(0) == 0)
        def _():
            s_ref[...] = jnp.zeros_like(s_ref)

        dv = dxo_ref[...]
        dy_ref[...] = (dv * g_ref[...]).astype(BF16)
        s_ref[0:1, :] += jnp.sum(dv * y_ref[...], axis=0, keepdims=True)

    row = pl.BlockSpec((tm, D), lambda i: (i, 0))
    return pl.pallas_call(body, grid=(L // tm,), in_specs=[row, row, pl.BlockSpec((1, D), lambda i: (0, 0))],
                          out_specs=[row, pl.BlockSpec((8, D), lambda i: (0, 0))],
                          out_shape=[jax.ShapeDtypeStruct((L, D), BF16), jax.ShapeDtypeStruct((8, D), F32)],
                          compiler_params=_params(("arbitrary",)), name=name)(dxo, y, g)


def _final_loss(x, fw, tgt, *, name):
    L = x.shape[0]
    tm = min(L, 256)

    def body(x_ref, fw_ref, t_ref, dx_ref, s_ref):
        @pl.when(pl.program_id(0) == 0)
        def _():
            s_ref[...] = jnp.zeros_like(s_ref)

        xv = x_ref[...]
        r = lax.rsqrt(jnp.mean(xv * xv, axis=-1, keepdims=True) + EPS)
        xhat = xv * r
        diff = xhat * fw_ref[...] - t_ref[...]
        dout = diff * (1.0 / D)
        dxhat = dout * fw_ref[...]
        dx_ref[...] = r * (dxhat - xhat * jnp.mean(dxhat * xhat, axis=-1, keepdims=True))
        s_ref[0:1, :] += jnp.sum(dout * xhat, axis=0, keepdims=True)
        s_ref[1:2, :] += jnp.zeros((1, D), F32) + 0.5 * jnp.sum(jnp.sum(diff * diff, axis=-1, keepdims=True) * (1.0 / D))

    row = pl.BlockSpec((tm, D), lambda i: (i, 0))
    return pl.pallas_call(body, grid=(L // tm,), in_specs=[row, pl.BlockSpec((1, D), lambda i: (0, 0)), row],
                          out_specs=[row, pl.BlockSpec((8, D), lambda i: (0, 0))],
                          out_shape=[jax.ShapeDtypeStruct((L, D), F32), jax.ShapeDtypeStruct((8, D), F32)],
                          compiler_params=_params(("arbitrary",)), name=name)(x, fw, tgt)


def _shift_down(v, j):
    if j == 0:
        return v
    row = lax.broadcasted_iota(jnp.int32, v.shape, 0)
    return jnp.where(row >= j, pltpu.roll(v, j, 0), 0.0)


def _shift_up(v, j):
    if j == 0:
        return v
    n = v.shape[0]
    row = lax.broadcasted_iota(jnp.int32, v.shape, 0)
    return jnp.where(row < n - j, pltpu.roll(v, n - j, 0), 0.0)


def _ssd_conv_fwd(zx, w, b, *, name):
    L = zx.shape[0]
    cb = 256
    k = w.shape[0]

    def body(x_ref, w_ref, b_ref, o_ref):
        xv = x_ref[...]
        pre = b_ref[...] + xv * w_ref[k - 1:k, :]
        for j in range(1, k):
            pre = pre + _shift_down(xv, j) * w_ref[k - 1 - j:k - j, :]
        o_ref[...] = pre * _sigmoid(pre)

    return pl.pallas_call(
        body, grid=(CONVD // cb,),
        in_specs=[pl.BlockSpec((L, cb), lambda i: (0, i + DI // cb)), pl.BlockSpec((k, cb), lambda i: (0, i)),
                  pl.BlockSpec((1, cb), lambda i: (0, i))],
        out_specs=pl.BlockSpec((L, cb), lambda i: (0, i)), out_shape=jax.ShapeDtypeStruct((L, CONVD), F32),
        compiler_params=_params(("parallel",)), name=name)(zx, w, b)


def _ssd_conv_bwd(zx, dact, w, b, *, name):
    L = zx.shape[0]
    cb = 256
    k = w.shape[0]

    def body(x_ref, da_ref, w_ref, b_ref, dx_ref, s_ref):
        xv = x_ref[...]
        sh = [_shift_down(xv, j) for j in range(k)]
        pre = b_ref[...] + sh[0] * w_ref[k - 1:k, :]
        for j in range(1, k):
            pre = pre + sh[j] * w_ref[k - 1 - j:k - j, :]
        s = _sigmoid(pre)
        dpre = da_ref[...] * (s * (1.0 + pre * (1.0 - s)))
        dx = dpre * w_ref[k - 1:k, :]
        for j in range(1, k):
            dx = dx + _shift_up(dpre, j) * w_ref[k - 1 - j:k - j, :]
        dx_ref[...] = dx.astype(BF16)
        s_ref[...] = jnp.zeros_like(s_ref)
        for j in range(k):
            s_ref[k - 1 - j:k - j, :] = jnp.sum(dpre * sh[j], axis=0, keepdims=True)
        s_ref[k:k + 1, :] = jnp.sum(dpre, axis=0, keepdims=True)

    return pl.pallas_call(
        body, grid=(CONVD // cb,),
        in_specs=[pl.BlockSpec((L, cb), lambda i: (0, i + DI // cb)), pl.BlockSpec((L, cb), lambda i: (0, i)),
                  pl.BlockSpec((k, cb), lambda i: (0, i)), pl.BlockSpec((1, cb), lambda i: (0, i))],
        out_specs=[pl.BlockSpec((L, cb), lambda i: (0, i)), pl.BlockSpec((8, cb), lambda i: (0, i))],
        out_shape=[jax.ShapeDtypeStruct((L, CONVD), BF16), jax.ShapeDtypeStruct((8, CONVD), F32)],
        compiler_params=_params(("parallel",)), name=name)(zx, dact, w, b)


def _sc_fwd(proj, w, *, name):
    L = proj.shape[0]
    cb = 256
    nb = D // cb
    k = w.shape[0]

    def body(b_ref, c_ref, x_ref, w_ref, o_ref):
        u = c_ref[...] * x_ref[...]
        v = u * w_ref[k - 1:k, :]
        for j in range(1, k):
            v = v + _shift_down(u, j) * w_ref[k - 1 - j:k - j, :]
        o_ref[...] = (b_ref[...] * v).astype(BF16)

    return pl.pallas_call(
        body, grid=(nb,),
        in_specs=[pl.BlockSpec((L, cb), lambda i: (0, i)), pl.BlockSpec((L, cb), lambda i: (0, i + nb)),
                  pl.BlockSpec((L, cb), lambda i: (0, i + 2 * nb)), pl.BlockSpec((k, cb), lambda i: (0, i))],
        out_specs=pl.BlockSpec((L, cb), lambda i: (0, i)), out_shape=jax.ShapeDtypeStruct((L, D), BF16),
        compiler_params=_params(("parallel",)), name=name)(proj, proj, proj, w)


def _sc_bwd(proj, dyv, w, *, name):
    L = proj.shape[0]
    cb = 256
    nb = D // cb
    k = w.shape[0]

    def body(b_ref, c_ref, x_ref, dy_ref, w_ref, db_ref, dc_ref, dx_ref, s_ref):
        cv, xv = c_ref[...], x_ref[...]
        u = cv * xv
        sh = [_shift_down(u, j) for j in range(k)]
        v = sh[0] * w_ref[k - 1:k, :]
        for j in range(1, k):
            v = v + sh[j] * w_ref[k - 1 - j:k - j, :]
        dyv_ = dy_ref[...]
        db_ref[...] = (dyv_ * v).astype(BF16)
        dv = dyv_ * b_ref[...]
        du = dv * w_ref[k - 1:k, :]
        for j in range(1, k):
            du = du + _shift_up(dv, j) * w_ref[k - 1 - j:k - j, :]
        dc_ref[...] = (du * xv).astype(BF16)
        dx_ref[...] = (du * cv).astype(BF16)
        s_ref[...] = jnp.zeros_like(s_ref)
        for j in range(k):
            s_ref[k - 1 - j:k - j, :] = jnp.sum(dv * sh[j], axis=0, keepdims=True)

    blk = pl.BlockSpec((L, cb), lambda i: (0, i))
    return pl.pallas_call(
        body, grid=(nb,),
        in_specs=[blk, pl.BlockSpec((L, cb), lambda i: (0, i + nb)), pl.BlockSpec((L, cb), lambda i: (0, i + 2 * nb)),
                  blk, pl.BlockSpec((k, cb), lambda i: (0, i))],
        out_specs=[blk, blk, blk, pl.BlockSpec((8, cb), lambda i: (0, i))],
        out_shape=[jax.ShapeDtypeStruct((L, D), BF16)] * 3 + [jax.ShapeDtypeStruct((8, D), F32)],
        compiler_params=_params(("parallel",)), name=name)(proj, proj, proj, dyv, w)


def _ssd_chunk_terms(dtr, prm):
    lane = lax.broadcasted_iota(jnp.int32, (CH, LANES), 1)
    valid = lane < NH
    xdt = dtr + prm[0:1, :]
    dt = jnp.where(valid, jnp.maximum(xdt, 0.0) + jnp.log1p(jnp.exp(-jnp.abs(xdt))), 0.0)
    A = -jnp.exp(prm[1:2, :])
    ri = lax.broadcasted_iota(jnp.int32, (CH, CH), 0)
    ci = lax.broadcasted_iota(jnp.int32, (CH, CH), 1)
    cs = _dot((ri >= ci).astype(F32), dt * A, precision=HIGHEST)
    last = cs[CH - 1:CH, :]
    ex = (lax.broadcasted_iota(jnp.int32, (LANES, DI), 1) // HP == lax.broadcasted_iota(jnp.int32, (LANES, DI), 0)).astype(F32)
    return dict(valid=valid, xdt=xdt, dt=dt, A=A, cs=cs, csT=cs.T, last=last, ri=ri, ci=ci, ex=ex)


def _expand(v, ex):
    if v.shape[0] == 1:
        return _dot(jnp.broadcast_to(v, (8, LANES)), ex, precision=HIGHEST)[0:1, :]
    return _dot(v, ex, precision=HIGHEST)


def _head_sum(v, ex):
    if v.shape[0] == 1:
        return _dot(jnp.broadcast_to(v, (8, DI)), ex, ((1,), (1,)), precision=HIGHEST)[0:1, :]
    return _dot(v, ex, ((1,), (1,)), precision=HIGHEST)


def _ssd_fwd(xbc, dtr, prm, *, name):
    L = xbc.shape[0]
    nc = L // CH

    def body(xbc_ref, dtr_ref, prm_ref, y_ref, sp_ref, st_ref):
        @pl.when(pl.program_id(0) == 0)
        def _():
            st_ref[...] = jnp.zeros_like(st_ref)

        prm_v = prm_ref[...]
        t = _ssd_chunk_terms(dtr_ref[...], prm_v)
        cs, csT, ex, causal = t["cs"], t["csT"], t["ex"], t["ri"] >= t["ci"]
        xs = xbc_ref[:, 0:DI]
        X = xs * _expand(t["dt"], ex)
        Xb = X.astype(BF16)
        Xd = (X * _expand(jnp.exp(t["last"] - cs), ex)).astype(BF16)
        Ex = _expand(jnp.exp(cs), ex)
        cdx = _expand(jnp.exp(t["last"]), ex)
        dskx = _expand(prm_v[2:3, :], ex)
        lane = lax.broadcasted_iota(jnp.int32, (CH, LANES), 1)
        sp_ref[0] = st_ref[...]
        for g in range(NG):
            Bg = xbc_ref[:, DI + g * NS:DI + (g + 1) * NS].astype(BF16)
            Cg = xbc_ref[:, DI + GW + g * NS:DI + GW + (g + 1) * NS].astype(BF16)
            G = _dot_nt(Cg, Bg)
            Sg = st_ref[:, g * GW:(g + 1) * GW]
            yoff = _dot(Cg, Sg.astype(BF16)) * Ex[:, g * GW:(g + 1) * GW]
            for j in range(GW // LANES):
                lo = g * GW + j * LANES
                Xp = Xb[:, lo:lo + LANES]
                yd = []
                for h in (lo // HP, lo // HP + 1):
                    seg = cs[:, h:h + 1] - csT[h:h + 1, :]
                    yd.append(_dot((G * jnp.where(causal, jnp.exp(seg), 0.0)).astype(BF16), Xp))
                y_ref[:, lo:lo + LANES] = (jnp.where(lane < HP, yd[0], yd[1]) + yoff[:, j * LANES:(j + 1) * LANES]
                                           + dskx[:, lo:lo + LANES] * xs[:, lo:lo + LANES])
            st_ref[:, g * GW:(g + 1) * GW] = Sg * cdx[:, g * GW:(g + 1) * GW] + _dot_tn(Bg, Xd[:, g * GW:(g + 1) * GW])

    return pl.pallas_call(
        body, grid=(nc,),
        in_specs=[pl.BlockSpec((CH, CONVD), lambda c: (c, 0)), pl.BlockSpec((CH, LANES), lambda c: (c, 0)),
                  pl.BlockSpec((8, LANES), lambda c: (0, 0))],
        out_specs=[pl.BlockSpec((CH, DI), lambda c: (c, 0)), pl.BlockSpec((1, NS, DI), lambda c: (c, 0, 0))],
        out_shape=[jax.ShapeDtypeStruct((L, DI), F32), jax.ShapeDtypeStruct((nc, NS, DI), F32)],
        scratch_shapes=[pltpu.VMEM((NS, DI), F32)],
        compiler_params=_params(("arbitrary",)), name=name)(xbc, dtr, prm)


def _ssd_bwd(xbc, dtr, prm, dy, sprev, *, name):
    L = xbc.shape[0]
    nc = L // CH

    def body(xbc_ref, dtr_ref, prm_ref, dy_ref, sp_ref, dxbc_ref, ddtr_ref, s_ref, dst_ref, dx_scr, de_scr, dd_scr):
        step = pl.program_id(0)

        @pl.when(step == 0)
        def _():
            dst_ref[...] = jnp.zeros_like(dst_ref)
            s_ref[...] = jnp.zeros_like(s_ref)

        prm_v = prm_ref[...]
        t = _ssd_chunk_terms(dtr_ref[...], prm_v)
        cs, csT, ex, ri, ci = t["cs"], t["csT"], t["ex"], t["ri"], t["ci"]
        E = jnp.exp(cs)
        dec = jnp.exp(t["last"] - cs)
        cd = jnp.exp(t["last"])
        xs = xbc_ref[:, 0:DI]
        dtx = _expand(t["dt"], ex)
        X = xs * dtx
        Xb = X.astype(BF16)
        decx = _expand(dec, ex)
        Xd = (X * decx).astype(BF16)
        Ex = _expand(E, ex)
        cdx = _expand(cd, ex)
        dskx = _expand(prm_v[2:3, :], ex)
        lane = lax.broadcasted_iota(jnp.int32, (CH, LANES), 1)
        dcs = jnp.zeros((CH, LANES), F32)
        dcd_x = []
        for g in range(NG):
            gs = slice(g * GW, (g + 1) * GW)
            Bg = xbc_ref[:, DI + g * NS:DI + (g + 1) * NS].astype(BF16)
            Cg = xbc_ref[:, DI + GW + g * NS:DI + GW + (g + 1) * NS].astype(BF16)
            G = _dot_nt(Cg, Bg)
            GT = _dot_nt(Bg, Cg)
            Sg = sp_ref[0, :, gs]
            Sgb = Sg.astype(BF16)
            dyg = dy_ref[:, gs]
            de_scr[:, gs] = dyg * _dot(Cg, Sgb)
            dYo = (Ex[:, gs] * dyg).astype(BF16)
            dC = _dot_nt(dYo, Sgb)
            dS_in = _dot_tn(Cg, dYo)
            dStg = dst_ref[:, gs]
            dStb = dStg.astype(BF16)
            dXd = _dot(Bg, dStb)
            dB = _dot_nt(Xd[:, gs], dStb)
            dd_scr[:, gs] = dXd * X[:, gs]
            dXst = dXd * decx[:, gs]
            dG = jnp.zeros((CH, CH), F32)
            dGT = jnp.zeros((CH, CH), F32)
            for j in range(GW // LANES):
                lo = g * GW + j * LANES
                Xp = Xb[:, lo:lo + LANES]
                dyp = dy_ref[:, lo:lo + LANES]
                dXp = dXst[:, j * LANES:(j + 1) * LANES]
                for k, h in enumerate((lo // HP, lo // HP + 1)):
                    dyh = jnp.where((lane < HP) if k == 0 else (lane >= HP), dyp, 0.0).astype(BF16)
                    seg = cs[:, h:h + 1] - csT[h:h + 1, :]
                    Lm = jnp.where(ri >= ci, jnp.exp(seg), 0.0)
                    LmT = jnp.where(ci >= ri, jnp.exp(-seg), 0.0)
                    dM = _dot_nt(dyh, Xp)
                    dMT = _dot_nt(Xp, dyh)
                    MT = GT * LmT
                    rs = jnp.sum(dM * (G * Lm), axis=1, keepdims=True) - jnp.sum(dMT * MT, axis=1, keepdims=True)
                    dcs = dcs + jnp.where(lane == h, rs, 0.0)
                    dG = dG + dM * Lm
                    dGT = dGT + dMT * LmT
                    dXp = dXp + _dot(MT.astype(BF16), dyh)
                dx_scr[:, lo:lo + LANES] = dXp
            dxbc_ref[:, DI + g * NS:DI + (g + 1) * NS] = dB + _dot(dGT.astype(BF16), Cg)
            dxbc_ref[:, DI + GW + g * NS:DI + GW + (g + 1) * NS] = dC + _dot(dG.astype(BF16), Bg)
            dcd_x.append(jnp.sum(dStg * Sg, axis=0, keepdims=True))
            dst_ref[:, gs] = dStg * cdx[:, gs] + dS_in
        dX = dx_scr[...]
        dy = dy_ref[...]
        ddec = _head_sum(dd_scr[...], ex)
        dcd = _head_sum(jnp.concatenate(dcd_x, axis=1), ex)
        dcs = dcs + _head_sum(de_scr[...], ex) * E - ddec * dec
        row = lax.broadcasted_iota(jnp.int32, (CH, LANES), 0)
        dcs = dcs + jnp.where(row == CH - 1, jnp.sum(ddec * dec, axis=0, keepdims=True) + dcd * cd, 0.0)
        da = _dot((ci >= ri).astype(F32), dcs, precision=HIGHEST)
        ddt = da * t["A"] + _head_sum(dX * xs, ex)
        ddtr = jnp.where(t["valid"], ddt * _sigmoid(t["xdt"]), 0.0)
        ddtr_ref[...] = ddtr
        dxbc_ref[:, 0:DI] = dX * dtx + dskx * dy
        s_ref[0:1, :] += jnp.sum(da * t["dt"], axis=0, keepdims=True)
        s_ref[1:2, :] += _head_sum(jnp.sum(dy * xs, axis=0, keepdims=True), ex)
        s_ref[2:3, :] += jnp.sum(ddtr, axis=0, keepdims=True)

        @pl.when(step == nc - 1)
        def _():
            s_ref[0:1, :] = s_ref[0:1, :] * t["A"]

    rev = lambda c: (nc - 1 - c, 0)
    return pl.pallas_call(
        body, grid=(nc,),
        in_specs=[pl.BlockSpec((CH, CONVD), rev), pl.BlockSpec((CH, LANES), rev), pl.BlockSpec((8, LANES), lambda c: (0, 0)),
                  pl.BlockSpec((CH, DI), rev), pl.BlockSpec((1, NS, DI), lambda c: (nc - 1 - c, 0, 0))],
        out_specs=[pl.BlockSpec((CH, CONVD), rev), pl.BlockSpec((CH, LANES), rev), pl.BlockSpec((8, LANES), lambda c: (0, 0))],
        out_shape=[jax.ShapeDtypeStruct((L, CONVD), F32), jax.ShapeDtypeStruct((L, LANES), F32),
                   jax.ShapeDtypeStruct((8, LANES), F32)],
        scratch_shapes=[pltpu.VMEM((NS, DI), F32), pltpu.VMEM((CH, DI), F32), pltpu.VMEM((CH, DI), F32),
                        pltpu.VMEM((CH, DI), F32)],
        compiler_params=_params(("arbitrary",)), name=name)(xbc, dtr, prm, dy, sprev)


def _gnorm_fwd(y, zx, nw, *, name):
    L = y.shape[0]
    tm = min(L, 256)

    def body(y_ref, z_ref, nw_ref, o_ref):
        z = z_ref[...]
        yg = y_ref[...] * (z * _sigmoid(z))
        for g in range(NG):
            v = yg[:, g * GW:(g + 1) * GW]
            r = lax.rsqrt(jnp.mean(v * v, axis=-1, keepdims=True) + EPS)
            o_ref[:, g * GW:(g + 1) * GW] = (v * r * nw_ref[:, g * GW:(g + 1) * GW]).astype(BF16)

    row = pl.BlockSpec((tm, DI), lambda i: (i, 0))
    return pl.pallas_call(body, grid=(L // tm,), in_specs=[row, row, pl.BlockSpec((1, DI), lambda i: (0, 0))],
                          out_specs=row, out_shape=jax.ShapeDtypeStruct((L, DI), BF16),
                          compiler_params=_params(("parallel",)), name=name)(y, zx, nw)


def _gnorm_bwd(y, zx, nw, dyn, *, name):
    L = y.shape[0]
    tm = min(L, 256)

    def body(y_ref, z_ref, nw_ref, dyn_ref, dy_ref, dz_ref, s_ref):
        @pl.when(pl.program_id(0) == 0)
        def _():
            s_ref[...] = jnp.zeros_like(s_ref)

        z, yv = z_ref[...], y_ref[...]
        sz = _sigmoid(z)
        gate = z * sz
        dgate_dz = sz * (1.0 + z * (1.0 - sz))
        for g in range(NG):
            gs = slice(g * GW, (g + 1) * GW)
            v = yv[:, gs] * gate[:, gs]
            r = lax.rsqrt(jnp.mean(v * v, axis=-1, keepdims=True) + EPS)
            vhat = v * r
            dn = dyn_ref[:, gs]
            s_ref[0:1, gs] += jnp.sum(dn * vhat, axis=0, keepdims=True)
            dvhat = dn * nw_ref[:, gs]
            dv = r * (dvhat - vhat * jnp.mean(dvhat * vhat, axis=-1, keepdims=True))
            dy_ref[:, gs] = dv * gate[:, gs]
            dz_ref[:, gs] = (dv * yv[:, gs] * dgate_dz[:, gs]).astype(BF16)

    row = pl.BlockSpec((tm, DI), lambda i: (i, 0))
    return pl.pallas_call(body, grid=(L // tm,), in_specs=[row, row, pl.BlockSpec((1, DI), lambda i: (0, 0)), row],
                          out_specs=[row, row, pl.BlockSpec((8, DI), lambda i: (0, 0))],
                          out_shape=[jax.ShapeDtypeStruct((L, DI), F32), jax.ShapeDtypeStruct((L, DI), BF16),
                                     jax.ShapeDtypeStruct((8, DI), F32)],
                          compiler_params=_params(("arbitrary",)), name=name)(y, zx, nw, dyn)


def _adamw(w, g, m, v, *, name):
    R, C = w.shape
    tr = R
    while tr * C > 256 * 1024 and tr % 16 == 0:
        tr //= 2

    def body(w_ref, g_ref, m_ref, v_ref, d_ref, mo_ref, vo_ref):
        gv = g_ref[...]
        mn = ADAM_B1 * m_ref[...] + (1.0 - ADAM_B1) * gv
        vn = ADAM_B2 * v_ref[...] + (1.0 - ADAM_B2) * (gv * gv)
        m_hat = mn / (1.0 - ADAM_B1 ** ADAM_STEP)
        v_hat = vn / (1.0 - ADAM_B2 ** ADAM_STEP)
        d_ref[...] = -ADAM_LR * (m_hat / (jnp.sqrt(v_hat) + ADAM_EPS) + ADAM_WD * w_ref[...])
        mo_ref[...] = mn
        vo_ref[...] = vn

    blk = pl.BlockSpec((tr, C), lambda i: (i, 0))
    return pl.pallas_call(body, grid=(R // tr,), in_specs=[blk] * 4, out_specs=[blk] * 3,
                          out_shape=[jax.ShapeDtypeStruct((R, C), F32)] * 3,
                          compiler_params=_params(("parallel",)), name=name)(w, g, m, v)


def _residual(acc, xv, gv):
    return xv + gv * acc, acc


def _relu2(acc):
    a = jnp.maximum(acc, 0.0)
    return a, a * a


def _mlp_fwd(x, mod, nw, up, down, tag):
    sh, sc, g = mod
    h = _modnorm_fwd(x, nw, sc, sh, name=tag + "_norm")
    a, act = _matmul(h, up, epi=_relu2, out_dtypes=(BF16, BF16), name=tag + "_up")
    xn, y = _matmul(act, down, extras=(x, g), epi=_residual, out_dtypes=(F32, F32), name=tag + "_down")
    return xn, (x, h, a, act, y)


def _mlp_bwd(dxo, saved, mod, nw, up, down, tag):
    x, h, a, act, y = saved
    sh, sc, g = mod
    dy, gsum = _gate_bwd(dxo, y, g, name=tag + "_dgate")
    du = _matmul(dy, down, trans_b=True, extras=(a,), epi=lambda acc, av: (acc * (2.0 * av.astype(F32)),),
                 out_dtypes=(BF16,), name=tag + "_dact")
    d_down = _matmul_tn(act, dy, name=tag + "_ddown")
    dh = _matmul(du, up, trans_b=True, name=tag + "_dh")
    d_up = _matmul_tn(h, du, name=tag + "_dup")
    dx, nsum = _modnorm_bwd(x, dh, dxo, nw, sc, name=tag + "_dnorm")
    return dx, d_up, d_down, nsum[0:1], jnp.concatenate([nsum[2:3], nsum[1:2], gsum[0:1]], axis=0)


def _ssd_layer_fwd(x, mod, nw, w_zx, w_dt, conv_w, conv_b, prm, gn_w, out_w, tag):
    sh, sc, g = mod
    h = _modnorm_fwd(x, nw, sc, sh, name=tag + "_norm")
    zx = _matmul(h, w_zx, name=tag + "_in")
    dtr = _matmul(h, w_dt, name=tag + "_in_dt")
    xbc = _ssd_conv_fwd(zx, conv_w, conv_b, name=tag + "_conv")
    y, sprev = _ssd_fwd(xbc, dtr, prm, name=tag + "_scan")
    yn = _gnorm_fwd(y, zx, gn_w, name=tag + "_gnorm")
    xn, yo = _matmul(yn, out_w, extras=(x, g), epi=_residual, out_dtypes=(F32, F32), name=tag + "_out")
    return xn, (x, h, zx, dtr, xbc, y, sprev, yn, yo)


def _ssd_layer_bwd(dxo, saved, mod, nw, w_zx, w_dt, conv_w, conv_b, prm, gn_w, out_w, tag):
    x, h, zx, dtr, xbc, y, sprev, yn, yo = saved
    sh, sc, g = mod
    dyo, gsum = _gate_bwd(dxo, yo, g, name=tag + "_dgate")
    dyn = _matmul(dyo, out_w, trans_b=True, name=tag + "_dyn")
    d_out_w = _matmul_tn(yn, dyo, name=tag + "_dout")
    dy, dz, gnsum = _gnorm_bwd(y, zx, gn_w, dyn, name=tag + "_dgnorm")
    dxbc, ddtr, ssum = _ssd_bwd(xbc, dtr, prm, dy, sprev, name=tag + "_dscan")
    draw, csum = _ssd_conv_bwd(zx, dxbc, conv_w, conv_b, name=tag + "_dconv")
    dzx = jnp.concatenate([dz, draw], axis=1)
    dh_dt = _matmul(ddtr, w_dt, trans_b=True, name=tag + "_dh_dt")
    dh = _matmul(dzx, w_zx, trans_b=True, extras=(dh_dt,), epi=lambda acc, e: (acc + e,), name=tag + "_dh")
    d_w_zx = _matmul_tn(h, dzx, name=tag + "_din")
    d_w_dt = _matmul_tn(h, ddtr, name=tag + "_din_dt")
    dx, nsum = _modnorm_bwd(x, dh, dxo, nw, sc, name=tag + "_dnorm")
    d_in_w = jnp.concatenate([d_w_zx, d_w_dt[:, :NH]], axis=1)
    dmod = jnp.concatenate([nsum[2:3], nsum[1:2], gsum[0:1]], axis=0)
    small = dict(conv_w=csum[0:4], conv_b=csum[4:5], dt_bias=ssum[2:3, :NH], A_log=ssum[0:1, :NH], D=ssum[1:2, :NH],
                 norm_w=gnsum[0:1])
    return dx, d_in_w, d_out_w, nsum[0:1], dmod, small


def _sc_layer_fwd(x, mod, nw, in_w, conv_w, out_w, tag):
    sh, sc, g = mod
    h = _modnorm_fwd(x, nw, sc, sh, name=tag + "_norm")
    proj = _matmul(h, in_w, name=tag + "_in")
    yv = _sc_fwd(proj, conv_w, name=tag + "_conv")
    xn, yo = _matmul(yv, out_w, extras=(x, g), epi=_residual, out_dtypes=(F32, F32), name=tag + "_out")
    return xn, (x, h, proj, yv, yo)


def _sc_layer_bwd(dxo, saved, mod, nw, in_w, conv_w, out_w, tag):
    x, h, proj, yv, yo = saved
    sh, sc, g = mod
    dyo, gsum = _gate_bwd(dxo, yo, g, name=tag + "_dgate")
    dyv = _matmul(dyo, out_w, trans_b=True, name=tag + "_dyv")
    d_out_w = _matmul_tn(yv, dyo, name=tag + "_dout")
    db, dc, dxv, csum = _sc_bwd(proj, dyv, conv_w, name=tag + "_dconv")
    dproj = jnp.concatenate([db, dc, dxv], axis=1)
    dh = _matmul(dproj, in_w, trans_b=True, name=tag + "_dh")
    d_in_w = _matmul_tn(h, dproj, name=tag + "_din")
    dx, nsum = _modnorm_bwd(x, dh, dxo, nw, sc, name=tag + "_dnorm")
    dmod = jnp.concatenate([nsum[2:3], nsum[1:2], gsum[0:1]], axis=0)
    return dx, d_in_w, d_out_w, nsum[0:1], dmod, csum[0:3]


def _local_step(x, tgt, mod, p):
    m = [[mod[i, j:j + 1] for j in range(6)] for i in range(2)]
    row = lambda v: v.reshape(1, -1)
    prm = jnp.zeros((8, LANES), F32).at[0, :NH].set(p["ssd_dt_bias"][0]).at[1, :NH].set(p["ssd_A_log"][0]).at[2, :NH].set(p["ssd_D"][0])
    ssd_args = (p["w_zx"], p["w_dt"], p["ssd_conv_w"][0], p["ssd_conv_b"], prm, p["ssd_norm_w"], p["ssd_out_w"])
    sc_args = (p["sc_in_w"], p["sc_conv_w"][0], p["sc_out_w"])
    mix_nw = [row(p["mix_norm_w"][i]) for i in range(2)]
    mlp_nw = [row(p["mlp_norm_w"][i]) for i in range(2)]

    x1, s_ssd = _ssd_layer_fwd(x, m[0][0:3], mix_nw[0], *ssd_args, tag="ssd")
    x2, s_mlp0 = _mlp_fwd(x1, m[0][3:6], mlp_nw[0], p["mlp_up"][0], p["mlp_down"][0], "mlp0")
    x3, s_sc = _sc_layer_fwd(x2, m[1][0:3], mix_nw[1], *sc_args, tag="sc")
    x4, s_mlp1 = _mlp_fwd(x3, m[1][3:6], mlp_nw[1], p["mlp_up"][1], p["mlp_down"][1], "mlp1")
    dx4, fsum = _final_loss(x4, row(p["final_norm_w"]), tgt, name="final_loss")
    dx3, d_up1, d_down1, d_mlpn1, dmod_f1 = _mlp_bwd(dx4, s_mlp1, m[1][3:6], mlp_nw[1], p["mlp_up"][1], p["mlp_down"][1], "mlp1")
    dx2, d_sc_in, d_sc_out, d_mixn1, dmod_m1, d_sc_conv = _sc_layer_bwd(dx3, s_sc, m[1][0:3], mix_nw[1], *sc_args, tag="sc")
    dx1, d_up0, d_down0, d_mlpn0, dmod_f0 = _mlp_bwd(dx2, s_mlp0, m[0][3:6], mlp_nw[0], p["mlp_up"][0], p["mlp_down"][0], "mlp0")
    dx0, d_ssd_in, d_ssd_out, d_mixn0, dmod_m0, ssd_small = _ssd_layer_bwd(dx1, s_ssd, m[0][0:3], mix_nw[0], *ssd_args, tag="ssd")

    big = dict(mlp_up=[d_up0, d_up1], mlp_down=[d_down0, d_down1], ssd_in_w=d_ssd_in, ssd_out_w=d_ssd_out,
               sc_in_w=d_sc_in, sc_out_w=d_sc_out)
    pad = lambda v: jnp.pad(v, ((0, 0), (0, D - v.shape[1])))
    small = jnp.concatenate([
        dmod_m0, dmod_f0, dmod_m1, dmod_f1,
        d_mixn0, d_mixn1, d_mlpn0, d_mlpn1,
        ssd_small["conv_w"].reshape(12, D), ssd_small["conv_b"].reshape(3, D), ssd_small["norm_w"].reshape(2, D),
        fsum[0:1], d_sc_conv,
        pad(jnp.concatenate([ssd_small["dt_bias"], ssd_small["A_log"], ssd_small["D"]], axis=1)),
        fsum[1:2], jnp.zeros((1, D), F32)], axis=0)
    return dx0, big, small


SMALL_ROWS = 40
HALF_ROWS = 3472
PACK_ROWS = 6920
ADD_ROWS = 496
ANY = pl.BlockSpec(memory_space=pl.ANY)


def _all_gather_rows(blk, *, name):
    m_per, n = blk.shape

    def body(x_ref, out_ref, send_sems, recv_sems, local_sem):
        x, y, c = lax.axis_index("x"), lax.axis_index("y"), lax.axis_index("c")
        me, sibling = (x, y, c), (x, y, 1 - c)
        chips = [(1 - x, y), (x, 1 - y), (1 - x, 1 - y)]

        def rows(px, py, pc):
            return out_ref.at[pl.ds((4 * px + 2 * py + pc) * m_per, m_per), :]

        def copy(k, block, to, src=None):
            return pltpu.make_async_remote_copy(src_ref=rows(*block) if src is None else src, dst_ref=rows(*block),
                                                send_sem=send_sems.at[k], recv_sem=recv_sems.at[k], device_id=to,
                                                device_id_type=MESH)

        mine = pltpu.make_async_copy(x_ref, rows(*me), local_sem)
        mine.start()
        first = [copy(0, me, sibling, src=x_ref)] + [copy(1 + j, me, (*chip, c), src=x_ref) for j, chip in enumerate(chips)]
        for cp in first:
            cp.start()
        passed = [copy(4 + j, (*chip, c), sibling) for j, chip in enumerate(chips)]
        for j, chip in enumerate(chips):
            copy(1 + j, (*chip, c), me).wait_recv()
            passed[j].start()
        copy(0, sibling, me).wait_recv()
        for j, chip in enumerate(chips):
            copy(4 + j, (*chip, 1 - c), me).wait_recv()
        for cp in first + passed:
            cp.wait_send()
        mine.wait()

    return pl.pallas_call(
        body, out_shape=jax.ShapeDtypeStruct((N_DEV * m_per, n), blk.dtype),
        in_specs=[pl.BlockSpec(memory_space=pltpu.VMEM)], out_specs=pl.BlockSpec(memory_space=pltpu.VMEM),
        scratch_shapes=[pltpu.SemaphoreType.DMA((7,)), pltpu.SemaphoreType.DMA((7,)), pltpu.SemaphoreType.DMA],
        name=name)(blk)


def _gather_weights(wp, *, name):
    _, hr, n = wp.shape

    def body(w_ref, out_ref, send_sems, recv_sems, local_sem):
        x, y, c = lax.axis_index("x"), lax.axis_index("y"), lax.axis_index("c")
        me, sibling = (x, y, c), (x, y, 1 - c)
        chips = [(1 - x, y), (x, 1 - y), (1 - x, 1 - y)]

        def slot(px, py, pc):
            return out_ref.at[2 * px + py, pc]

        def copy(k, block, to, src=None):
            return pltpu.make_async_remote_copy(src_ref=slot(*block) if src is None else src, dst_ref=slot(*block),
                                                send_sem=send_sems.at[k], recv_sem=recv_sems.at[k], device_id=to,
                                                device_id_type=MESH)

        src = w_ref.at[c]
        mine = pltpu.make_async_copy(src, slot(*me), local_sem)
        mine.start()
        first = [copy(0, me, sibling, src=src)] + [copy(1 + j, me, (*chip, c), src=src) for j, chip in enumerate(chips)]
        for cp in first:
            cp.start()
        passed = [copy(4 + j, (*chip, c), sibling) for j, chip in enumerate(chips)]
        for j, chip in enumerate(chips):
            copy(1 + j, (*chip, c), me).wait_recv()
            passed[j].start()
        copy(0, sibling, me).wait_recv()
        for j, chip in enumerate(chips):
            copy(4 + j, (*chip, 1 - c), me).wait_recv()
        for cp in first + passed:
            cp.wait_send()
        mine.wait()

    return pl.pallas_call(
        body, out_shape=jax.ShapeDtypeStruct((N_CHIPS, 2, hr, n), wp.dtype), in_specs=[ANY], out_specs=ANY,
        scratch_shapes=[pltpu.SemaphoreType.DMA((7,)), pltpu.SemaphoreType.DMA((7,)), pltpu.SemaphoreType.DMA],
        name=name)(wp)


def _swap_half_with_sibling(gp, *, name):
    _, nk, hr, n = gp.shape

    def body(g_ref, out_ref, send_sem, recv_sem):
        x, y, c = lax.axis_index("x"), lax.axis_index("y"), lax.axis_index("c")
        cp = pltpu.make_async_remote_copy(src_ref=g_ref.at[1 - c], dst_ref=out_ref, send_sem=send_sem, recv_sem=recv_sem,
                                          device_id=(x, y, 1 - c), device_id_type=MESH)
        cp.start()
        cp.wait()

    return pl.pallas_call(body, out_shape=jax.ShapeDtypeStruct((nk, hr, n), gp.dtype), in_specs=[ANY], out_specs=ANY,
                          scratch_shapes=[pltpu.SemaphoreType.DMA, pltpu.SemaphoreType.DMA], name=name)(gp)


def _send_to_owners(h, *, name):
    _, hr, n = h.shape

    def body(h_ref, out_ref, send_sems, recv_sems):
        x, y, c = lax.axis_index("x"), lax.axis_index("y"), lax.axis_index("c")
        chips = [(1 - x, y), (x, 1 - y), (1 - x, 1 - y)]
        copies = [pltpu.make_async_remote_copy(src_ref=h_ref.at[2 * cx + cy], dst_ref=out_ref.at[j], send_sem=send_sems.at[j],
                                               recv_sem=recv_sems.at[j], device_id=(cx, cy, c), device_id_type=MESH)
                  for j, (cx, cy) in enumerate(chips)]
        for cp in copies:
            cp.start()
        for cp in copies:
            cp.wait()

    return pl.pallas_call(body, out_shape=jax.ShapeDtypeStruct((3, hr, n), h.dtype), in_specs=[ANY], out_specs=ANY,
                          scratch_shapes=[pltpu.SemaphoreType.DMA((3,)), pltpu.SemaphoreType.DMA((3,))], name=name)(h)


def _swap_result_with_sibling(t, *, name):
    hr, n = t.shape

    def body(t_ref, out_ref, send_sem, recv_sem, local_sem):
        x, y, c = lax.axis_index("x"), lax.axis_index("y"), lax.axis_index("c")
        mine = pltpu.make_async_copy(t_ref, out_ref.at[c], local_sem)
        mine.start()
        cp = pltpu.make_async_remote_copy(src_ref=t_ref, dst_ref=out_ref.at[c], send_sem=send_sem, recv_sem=recv_sem,
                                          device_id=(x, y, 1 - c), device_id_type=MESH)
        cp.start()
        cp.wait()
        mine.wait()

    return pl.pallas_call(body, out_shape=jax.ShapeDtypeStruct((2, hr, n), t.dtype), in_specs=[ANY], out_specs=ANY,
                          scratch_shapes=[pltpu.SemaphoreType.DMA, pltpu.SemaphoreType.DMA, pltpu.SemaphoreType.DMA],
                          name=name)(t)


def _add_sibling_half(gp, recv, core, *, name):
    _, nk, hr, n = gp.shape

    def body(c_ref, a_ref, b_ref, o_ref):
        o_ref[...] = (a_ref[...].astype(F32) + b_ref[...].astype(F32)).astype(BF16)

    grid_spec = pltpu.PrefetchScalarGridSpec(
        num_scalar_prefetch=1, grid=(nk, hr // ADD_ROWS),
        in_specs=[pl.BlockSpec((None, None, ADD_ROWS, n), lambda k, i, c_ref: (c_ref[0], k, i, 0)),
                  pl.BlockSpec((None, ADD_ROWS, n), lambda k, i, c_ref: (k, i, 0))],
        out_specs=pl.BlockSpec((None, ADD_ROWS, n), lambda k, i, c_ref: (k, i, 0)))
    return pl.pallas_call(body, grid_spec=grid_spec, out_shape=jax.ShapeDtypeStruct((nk, hr, n), BF16),
                          compiler_params=_params(("parallel", "parallel")), name=name)(core, gp, recv)


def _add_chip_sums(h, recv, chip, *, name):
    _, hr, n = h.shape

    def body(k_ref, a_ref, b_ref, o_ref):
        o_ref[...] = ((a_ref[...].astype(F32) + b_ref[0].astype(F32)) + b_ref[1].astype(F32)) + b_ref[2].astype(F32)

    grid_spec = pltpu.PrefetchScalarGridSpec(
        num_scalar_prefetch=1, grid=(hr // ADD_ROWS,),
        in_specs=[pl.BlockSpec((None, ADD_ROWS, n), lambda i, k_ref: (k_ref[0], i, 0)),
                  pl.BlockSpec((3, ADD_ROWS, n), lambda i, k_ref: (0, i, 0))],
        out_specs=pl.BlockSpec((ADD_ROWS, n), lambda i, k_ref: (i, 0)))
    return pl.pallas_call(body, grid_spec=grid_spec, out_shape=jax.ShapeDtypeStruct((hr, n), F32),
                          compiler_params=_params(("parallel",)), name=name)(chip, h, recv)


def _sum_devices(g, *, name):
    nd, r, n = g.shape

    def body(g_ref, o_ref):
        acc = g_ref[0]
        for i in range(1, nd):
            acc = acc + g_ref[i]
        o_ref[...] = acc

    return pl.pallas_call(body, out_shape=jax.ShapeDtypeStruct((r, n), F32), name=name)(g)


def _pack_shards(up, down, ssd_in, ssd_out, sc_in, sc_out):
    rows = jnp.concatenate([v.astype(BF16).reshape(-1, D) for v in (up, down, ssd_in, ssd_out, sc_in, sc_out)], axis=0)
    assert rows.shape[0] == PACK_ROWS, rows.shape
    return jnp.pad(rows, ((0, 2 * HALF_ROWS - PACK_ROWS), (0, 0))).reshape(2, HALF_ROWS, D)


_PACK_SPLITS = (2048, 4096, 5384, 5896, 6664, 6920)
_SHARD_SHAPES = ((2, 1024, 1024), (2, 1024, 1024), (1, 1024, 1288), (1, 512, 1024), (1, 1024, 768), (1, 256, 1024))


def _unpack_rows(rows, lead=()):
    out, lo = [], 0
    for hi, shp in zip(_PACK_SPLITS, _SHARD_SHAPES, strict=True):
        out.append(rows[..., lo:hi, :].reshape(lead + shp))
        lo = hi
    return out


def kernel(x, c, ada_w, ada_b, mix_norm_w, mlp_norm_w, mlp_up, mlp_down, ssd_in_w, ssd_conv_w, ssd_conv_b, ssd_dt_bias, ssd_A_log, ssd_D, ssd_norm_w, ssd_out_w, sc_in_w, sc_conv_w, sc_out_w, final_norm_w, loss_target, m_ada_w, m_ada_b, m_mix_norm_w, m_mlp_norm_w, m_mlp_up, m_mlp_down, m_ssd_in_w, m_ssd_conv_w, m_ssd_conv_b, m_ssd_dt_bias, m_ssd_A_log, m_ssd_D, m_ssd_norm_w, m_ssd_out_w, m_sc_in_w, m_sc_conv_w, m_sc_out_w, m_final_norm_w, v_ada_w, v_ada_b, v_mix_norm_w, v_mlp_norm_w, v_mlp_up, v_mlp_down, v_ssd_in_w, v_ssd_conv_w, v_ssd_conv_b, v_ssd_dt_bias, v_ssd_A_log, v_ssd_D, v_ssd_norm_w, v_ssd_out_w, v_sc_in_w, v_sc_conv_w, v_sc_out_w, v_final_norm_w):
    xi, yi, ci = lax.axis_index("x"), lax.axis_index("y"), lax.axis_index("c")
    chip = 2 * xi + yi
    dev = 2 * chip + ci
    n_ada = ada_w.shape[2]

    wg = _gather_weights(_pack_shards(mlp_up, mlp_down, ssd_in_w, ssd_out_w, sc_in_w, sc_out_w), name="gather_weights")
    up_k, down_k, ssd_in_k, ssd_out_k, sc_in_k, sc_out_k = _unpack_rows(wg.reshape(N_CHIPS, 2 * HALF_ROWS, D), (N_CHIPS,))
    cols = lambda v: jnp.moveaxis(v, 0, -2).reshape(v.shape[1:-1] + (N_CHIPS * v.shape[-1],))
    rows = lambda v: jnp.moveaxis(v, 0, -3).reshape(v.shape[1:-2] + (N_CHIPS * v.shape[-2], v.shape[-1]))
    ssd_in_full = cols(ssd_in_k)[0]
    p = dict(mlp_up=cols(up_k), mlp_down=rows(down_k), w_zx=ssd_in_full[:, :ZX],
             w_dt=jnp.pad(ssd_in_full[:, ZX:], ((0, 0), (0, LANES - NH))), ssd_out_w=rows(ssd_out_k)[0],
             sc_in_w=cols(sc_in_k)[0], sc_out_w=rows(sc_out_k)[0], mix_norm_w=mix_norm_w, mlp_norm_w=mlp_norm_w,
             ssd_conv_b=ssd_conv_b, ssd_dt_bias=ssd_dt_bias, ssd_A_log=ssd_A_log, ssd_D=ssd_D, ssd_norm_w=ssd_norm_w,
             final_norm_w=final_norm_w)

    conv_rows = (ssd_conv_w.size + sc_conv_w.size) // 256
    blk0 = jnp.concatenate([c, jnp.zeros((7, D), F32)], axis=0)
    c_all = _all_gather_rows(blk0, name="gather_cond").reshape(N_DEV, 8, D)[:, 0]
    blk1 = jnp.concatenate([ssd_conv_w.reshape(-1, 256), sc_conv_w.reshape(-1, 256), jnp.zeros((1, 256), F32)], axis=0)
    conv_all = _all_gather_rows(blk1, name="gather_conv").reshape(N_DEV, conv_rows + 1, 256)[0::2]
    p["ssd_conv_w"] = jnp.moveaxis(conv_all[:, 0:12].reshape(N_CHIPS, 4, 768), 0, 1).reshape(1, 4, CONVD)
    p["sc_conv_w"] = jnp.moveaxis(conv_all[:, 12:15].reshape(N_CHIPS, 3, 256), 0, 1).reshape(1, 3, D)
    mod_shard = [_matmul(c_all, ada_w[i], a_silu=True, extras=(lax.dynamic_slice(ada_b, (i, chip * n_ada), (1, n_ada)),),
                         epi=lambda acc, b: (acc + b,), name=f"ada_mod{i}") for i in range(2)]
    mod_all = _all_gather_rows(jnp.concatenate(mod_shard, axis=0), name="gather_mod")
    mod_all = mod_all.reshape(N_DEV, 2, N_DEV, n_ada)[0::2]
    mod = jnp.moveaxis(lax.dynamic_index_in_dim(mod_all, dev, axis=2, keepdims=False), 0, 1).reshape(2, 6, D)

    grad_x, big, small = _local_step(x[0], loss_target[0], mod, p)

    small_all = _all_gather_rows(small, name="gather_small").reshape(N_DEV, SMALL_ROWS, D)
    tot = _sum_devices(small_all, name="sum_small")
    loss = tot[38, 0]
    g_ada_b = tot[0:12].reshape(2, 6 * D)
    g_mix_norm, g_mlp_norm = tot[12:14], tot[14:16]
    g_ssd_conv_w = lax.dynamic_slice(tot[16:28].reshape(1, 4, CONVD), (0, 0, chip * 768), (1, 4, 768))
    g_ssd_conv_b, g_ssd_norm = tot[28:31].reshape(1, CONVD), tot[31:33].reshape(1, DI)
    g_final = tot[33]
    g_sc_conv_w = lax.dynamic_slice(tot[34:37].reshape(1, 3, D), (0, 0, chip * 256), (1, 3, 256))
    g_dt_bias, g_a_log, g_d = tot[37:38, 0:NH], tot[37:38, NH:2 * NH], tot[37:38, 2 * NH:3 * NH]
    c_pad = jnp.concatenate([c_all, jnp.zeros((8, D), F32)], axis=0)
    g_ada_w = []
    for i in range(2):
        dm = lax.dynamic_slice(small_all[:, 6 * i:6 * i + 6].reshape(N_DEV, 6 * D), (0, chip * n_ada), (N_DEV, n_ada))
        g_ada_w.append(_matmul_tn(c_pad, jnp.concatenate([dm, jnp.zeros_like(dm)], axis=0), a_silu=True, name=f"ada_dw{i}"))
    g_ada_w = jnp.stack(g_ada_w)

    def chip_rows(k):
        return _pack_shards(jnp.stack([g[:, k * 1024:(k + 1) * 1024] for g in big["mlp_up"]]),
                            jnp.stack([g[k * 1024:(k + 1) * 1024] for g in big["mlp_down"]]),
                            big["ssd_in_w"][:, k * 1288:(k + 1) * 1288], big["ssd_out_w"][k * 512:(k + 1) * 512],
                            big["sc_in_w"][:, k * 768:(k + 1) * 768], big["sc_out_w"][k * 256:(k + 1) * 256])

    gp = jnp.stack([chip_rows(k) for k in range(N_CHIPS)], axis=1)
    core = ci.reshape(1).astype(jnp.int32)
    sib = _swap_half_with_sibling(gp, name="rs_sibling")
    h = _add_sibling_half(gp, sib, core, name="rs_add_sibling")
    others = _send_to_owners(h, name="rs_owners")
    t = _add_chip_sums(h, others, chip.reshape(1).astype(jnp.int32), name="rs_add_chips")
    gsh = _swap_result_with_sibling(t, name="rs_result").reshape(2 * HALF_ROWS, D)
    g_up, g_down, g_ssd_in, g_ssd_out, g_sc_in, g_sc_out = _unpack_rows(gsh)

    grads = dict(ada_w=g_ada_w, ada_b=g_ada_b, mix_norm_w=g_mix_norm, mlp_norm_w=g_mlp_norm, mlp_up=g_up, mlp_down=g_down,
                 ssd_in_w=g_ssd_in, ssd_conv_w=g_ssd_conv_w, ssd_conv_b=g_ssd_conv_b, ssd_dt_bias=g_dt_bias,
                 ssd_A_log=g_a_log, ssd_D=g_d, ssd_norm_w=g_ssd_norm, ssd_out_w=g_ssd_out, sc_in_w=g_sc_in,
                 sc_conv_w=g_sc_conv_w, sc_out_w=g_sc_out, final_norm_w=g_final)
    weights = dict(ada_w=(ada_w, m_ada_w, v_ada_w), ada_b=(ada_b, m_ada_b, v_ada_b),
                   mix_norm_w=(mix_norm_w, m_mix_norm_w, v_mix_norm_w), mlp_norm_w=(mlp_norm_w, m_mlp_norm_w, v_mlp_norm_w),
                   mlp_up=(mlp_up, m_mlp_up, v_mlp_up), mlp_down=(mlp_down, m_mlp_down, v_mlp_down),
                   ssd_in_w=(ssd_in_w, m_ssd_in_w, v_ssd_in_w), ssd_conv_w=(ssd_conv_w, m_ssd_conv_w, v_ssd_conv_w),
                   ssd_conv_b=(ssd_conv_b, m_ssd_conv_b, v_ssd_conv_b), ssd_dt_bias=(ssd_dt_bias, m_ssd_dt_bias, v_ssd_dt_bias),
                   ssd_A_log=(ssd_A_log, m_ssd_A_log, v_ssd_A_log), ssd_D=(ssd_D, m_ssd_D, v_ssd_D),
                   ssd_norm_w=(ssd_norm_w, m_ssd_norm_w, v_ssd_norm_w), ssd_out_w=(ssd_out_w, m_ssd_out_w, v_ssd_out_w),
                   sc_in_w=(sc_in_w, m_sc_in_w, v_sc_in_w), sc_conv_w=(sc_conv_w, m_sc_conv_w, v_sc_conv_w),
                   sc_out_w=(sc_out_w, m_sc_out_w, v_sc_out_w), final_norm_w=(final_norm_w, m_final_norm_w, v_final_norm_w))
    g_out, d_out, m_out, v_out = [], [], [], []
    for nm, (w, m, v) in weights.items():
        shp = w.shape
        two_d = (-1, shp[-1]) if w.ndim > 1 else (1, -1)
        g = grads[nm].reshape(shp)
        dl, mn, vn = _adamw(w.reshape(two_d), g.reshape(two_d), m.reshape(two_d), v.reshape(two_d), name="adamw_" + nm)
        g_out.append(g)
        d_out.append(dl.reshape(shp))
        m_out.append(mn.reshape(shp))
        v_out.append(vn.reshape(shp))
    return (loss, grad_x[None], *g_out, *d_out, *m_out, *v_out)
```

```python
import functools

import jax
import jax.numpy as jnp
from jax import lax
from jax.experimental import pallas as pl
from jax.experimental.pallas import tpu as pltpu

F32 = jnp.float32
BF16 = jnp.bfloat16
MESH = pl.DeviceIdType.MESH
HIGHEST = lax.Precision.HIGHEST

D = 1024
DFF = 4096
DI = 2048
NH = 32
HP = 64
NG = 4
NS = 128
CH = 128
CONVD = DI + 2 * NG * NS
ZX = DI + CONVD
GW = NG * NS
LANES = 128
N_CHIPS = 4
N_DEV = 8
EPS = 1e-5
ADAM_LR, ADAM_B1, ADAM_B2, ADAM_EPS, ADAM_WD, ADAM_STEP = 1e-3, 0.9, 0.999, 1e-8, 0.01, 10
VMEM_LIMIT = 48 * 1024 * 1024


def _params(sem=None):
    return pltpu.CompilerParams(dimension_semantics=sem, vmem_limit_bytes=VMEM_LIMIT)


def _sigmoid(v):
    return 1.0 / (1.0 + jnp.exp(-v))


def _dot(a, b, dims=((1,), (0,)), precision=None):
    return lax.dot_general(a, b, (dims, ((), ())), preferred_element_type=F32, precision=precision)


def _dot_nt(a, b):
    return _dot(a, b, ((1,), (1,)))


def _dot_tn(a, b):
    return _dot(a, b, ((0,), (0,)))


def _matmul(a, b, *, name, trans_b=False, tm=512, tn=512, extras=(), epi=None, out_dtypes=(F32,), a_silu=False):
    M, K = a.shape
    N = b.shape[0] if trans_b else b.shape[1]
    tm, tn = min(tm, M), min(tn, N)
    assert M % tm == 0 and N % tn == 0, (name, M, N, tm, tn)
    n_ex = len(extras)

    def body(*refs):
        a_ref, b_ref = refs[:2]
        av = a_ref[...]
        if a_silu:
            av = av * _sigmoid(av)
        acc = _dot(av.astype(BF16), b_ref[...].astype(BF16), ((1,), (1,)) if trans_b else ((1,), (0,)))
        res = epi(acc, *[r[...] for r in refs[2:2 + n_ex]]) if epi is not None else (acc,)
        for o_ref, r in zip(refs[2 + n_ex:], res, strict=True):
            o_ref[...] = r.astype(o_ref.dtype)

    in_specs = [pl.BlockSpec((tm, K), lambda i, j: (i, 0)),
                pl.BlockSpec((tn, K), lambda i, j: (j, 0)) if trans_b else pl.BlockSpec((K, tn), lambda i, j: (0, j))]
    for e in extras:
        in_specs.append(pl.BlockSpec((1, tn), lambda i, j: (0, j)) if e.shape[0] == 1 and M != 1
                        else pl.BlockSpec((tm, tn), lambda i, j: (i, j)))
    outs = pl.pallas_call(
        body, grid=(M // tm, N // tn), in_specs=in_specs,
        out_specs=[pl.BlockSpec((tm, tn), lambda i, j: (i, j)) for _ in out_dtypes],
        out_shape=[jax.ShapeDtypeStruct((M, N), dt) for dt in out_dtypes],
        compiler_params=_params(("parallel", "parallel")), name=name)(a, b, *extras)
    return outs if len(out_dtypes) > 1 else outs[0]


def _matmul_tn(a, b, *, name, tm=512, tn=512, a_silu=False):
    T, M = a.shape
    N = b.shape[1]
    tm, tn = min(tm, M), min(tn, N)
    assert M % tm == 0 and N % tn == 0, (name, M, N, tm, tn)

    def body(a_ref, b_ref, o_ref):
        av = a_ref[...]
        if a_silu:
            av = av * _sigmoid(av)
        o_ref[...] = _dot_tn(av.astype(BF16), b_ref[...].astype(BF16))

    return pl.pallas_call(
        body, grid=(M // tm, N // tn),
        in_specs=[pl.BlockSpec((T, tm), lambda i, j: (0, i)), pl.BlockSpec((T, tn), lambda i, j: (0, j))],
        out_specs=pl.BlockSpec((tm, tn), lambda i, j: (i, j)),
        out_shape=jax.ShapeDtypeStruct((M, N), F32),
        compiler_params=_params(("parallel", "parallel")), name=name)(a, b)


def _modnorm_fwd(x, nw, sc, sh, *, name):
    L = x.shape[0]
    tm = min(L, 512)

    def body(x_ref, nw_ref, sc_ref, sh_ref, h_ref):
        xv = x_ref[...]
        r = lax.rsqrt(jnp.mean(xv * xv, axis=-1, keepdims=True) + EPS)
        h_ref[...] = ((xv * r * nw_ref[...]) * (1.0 + sc_ref[...]) + sh_ref[...]).astype(BF16)

    row = pl.BlockSpec((tm, D), lambda i: (i, 0))
    vec = pl.BlockSpec((1, D), lambda i: (0, 0))
    return pl.pallas_call(body, grid=(L // tm,), in_specs=[row, vec, vec, vec], out_specs=row,
                          out_shape=jax.ShapeDtypeStruct((L, D), BF16),
                          compiler_params=_params(("parallel",)), name=name)(x, nw, sc, sh)


def _modnorm_bwd(x, dh, dxo, nw, sc, *, name):
    L = x.shape[0]
    tm = min(L, 256)

    def body(x_ref, dh_ref, dxo_ref, nw_ref, sc_ref, dx_ref, s_ref):
        @pl.when(pl.program_id(0) == 0)
        def _():
            s_ref[...] = jnp.zeros_like(s_ref)

        xv, dhv = x_ref[...], dh_ref[...]
        r = lax.rsqrt(jnp.mean(xv * xv, axis=-1, keepdims=True) + EPS)
        xhat = xv * r
        dxhat = dhv * (nw_ref[...] * (1.0 + sc_ref[...]))
        dx_ref[...] = dxo_ref[...] + r * (dxhat - xhat * jnp.mean(dxhat * xhat, axis=-1, keepdims=True))
        s_ref[0:1, :] += jnp.sum(dhv * xhat, axis=0, keepdims=True) * (1.0 + sc_ref[...])
        s_ref[1:2, :] += jnp.sum(dhv * xhat, axis=0, keepdims=True) * nw_ref[...]
        s_ref[2:3, :] += jnp.sum(dhv, axis=0, keepdims=True)

    row = pl.BlockSpec((tm, D), lambda i: (i, 0))
    vec = pl.BlockSpec((1, D), lambda i: (0, 0))
    return pl.pallas_call(body, grid=(L // tm,), in_specs=[row, row, row, vec, vec],
                          out_specs=[row, pl.BlockSpec((8, D), lambda i: (0, 0))],
                          out_shape=[jax.ShapeDtypeStruct((L, D), F32), jax.ShapeDtypeStruct((8, D), F32)],
                          compiler_params=_params(("arbitrary",)), name=name)(x, dh, dxo, nw, sc)


def _gate_bwd(dxo, y, g, *, name):
    L = dxo.shape[0]
    tm = min(L, 512)

    def body(dxo_ref, y_ref, g_ref, dy_ref, s_ref):
        @pl.when(pl.program_id(0) == 0)
        def _():
            s_ref[...] = jnp.zeros_like(s_ref)

        dv = dxo_ref[...]
        dy_ref[...] = (dv * g_ref[...]).astype(BF16)
        s_ref[0:1, :] += jnp.sum(dv * y_ref[...], axis=0, keepdims=True)

    row = pl.BlockSpec((tm, D), lambda i: (i, 0))
    return pl.pallas_call(body, grid=(L // tm,), in_specs=[row, row, pl.BlockSpec((1, D), lambda i: (0, 0))],
                          out_specs=[row, pl.BlockSpec((8, D), lambda i: (0, 0))],
                          out_shape=[jax.ShapeDtypeStruct((L, D), BF16), jax.ShapeDtypeStruct((8, D), F32)],
                          compiler_params=_params(("arbitrary",)), name=name)(dxo, y, g)


def _final_loss(x, fw, tgt, *, name):
    L = x.shape[0]
    tm = min(L, 256)

    def body(x_ref, fw_ref, t_ref, dx_ref, s_ref):
        @pl.when(pl.program_id(0) == 0)
        def _():
            s_ref[...] = jnp.zeros_like(s_ref)

        xv = x_ref[...]
        r = lax.rsqrt(jnp.mean(xv * xv, axis=-1, keepdims=True) + EPS)
        xhat = xv * r
        diff = xhat * fw_ref[...] - t_ref[...]
        dout = diff * (1.0 / D)
        dxhat = dout * fw_ref[...]
        dx_ref[...] = r * (dxhat - xhat * jnp.mean(dxhat * xhat, axis=-1, keepdims=True))
        s_ref[0:1, :] += jnp.sum(dout * xhat, axis=0, keepdims=True)
        s_ref[1:2, :] += jnp.zeros((1, D), F32) + 0.5 * jnp.sum(jnp.sum(diff * diff, axis=-1, keepdims=True) * (1.0 / D))

    row = pl.BlockSpec((tm, D), lambda i: (i, 0))
    return pl.pallas_call(body, grid=(L // tm,), in_specs=[row, pl.BlockSpec((1, D), lambda i: (0, 0)), row],
                          out_specs=[row, pl.BlockSpec((8, D), lambda i: (0, 0))],
                          out_shape=[jax.ShapeDtypeStruct((L, D), F32), jax.ShapeDtypeStruct((8, D), F32)],
                          compiler_params=_params(("arbitrary",)), name=name)(x, fw, tgt)


def _shift_down(v, j):
    if j == 0:
        return v
    row = lax.broadcasted_iota(jnp.int32, v.shape, 0)
    return jnp.where(row >= j, pltpu.roll(v, j, 0), 0.0)


def _shift_up(v, j):
    if j == 0:
        return v
    n = v.shape[0]
    row = lax.broadcasted_iota(jnp.int32, v.shape, 0)
    return jnp.where(row < n - j, pltpu.roll(v, n - j, 0), 0.0)


def _ssd_conv_fwd(zx, w, b, *, name):
    L = zx.shape[0]
    cb = 256
    k = w.shape[0]

    def body(x_ref, w_ref, b_ref, o_ref):
        xv = x_ref[...]
        pre = b_ref[...] + xv * w_ref[k - 1:k, :]
        for j in range(1, k):
            pre = pre + _shift_down(xv, j) * w_ref[k - 1 - j:k - j, :]
        o_ref[...] = pre * _sigmoid(pre)

    return pl.pallas_call(
        body, grid=(CONVD // cb,),
        in_specs=[pl.BlockSpec((L, cb), lambda i: (0, i + DI // cb)), pl.BlockSpec((k, cb), lambda i: (0, i)),
                  pl.BlockSpec((1, cb), lambda i: (0, i))],
        out_specs=pl.BlockSpec((L, cb), lambda i: (0, i)), out_shape=jax.ShapeDtypeStruct((L, CONVD), F32),
        compiler_params=_params(("parallel",)), name=name)(zx, w, b)


def _ssd_conv_bwd(zx, dact, w, b, *, name):
    L = zx.shape[0]
    cb = 256
    k = w.shape[0]

    def body(x_ref, da_ref, w_ref, b_ref, dx_ref, s_ref):
        xv = x_ref[...]
        sh = [_shift_down(xv, j) for j in range(k)]
        pre = b_ref[...] + sh[0] * w_ref[k - 1:k, :]
        for j in range(1, k):
            pre = pre + sh[j] * w_ref[k - 1 - j:k - j, :]
        s = _sigmoid(pre)
        dpre = da_ref[...] * (s * (1.0 + pre * (1.0 - s)))
        dx = dpre * w_ref[k - 1:k, :]
        for j in range(1, k):
            dx = dx + _shift_up(dpre, j) * w_ref[k - 1 - j:k - j, :]
        dx_ref[...] = dx.astype(BF16)
        s_ref[...] = jnp.zeros_like(s_ref)
        for j in range(k):
            s_ref[k - 1 - j:k - j, :] = jnp.sum(dpre * sh[j], axis=0, keepdims=True)
        s_ref[k:k + 1, :] = jnp.sum(dpre, axis=0, keepdims=True)

    return pl.pallas_call(
        body, grid=(CONVD // cb,),
        in_specs=[pl.BlockSpec((L, cb), lambda i: (0, i + DI // cb)), pl.BlockSpec((L, cb), lambda i: (0, i)),
                  pl.BlockSpec((k, cb), lambda i: (0, i)), pl.BlockSpec((1, cb), lambda i: (0, i))],
        out_specs=[pl.BlockSpec((L, cb), lambda i: (0, i)), pl.BlockSpec((8, cb), lambda i: (0, i))],
        out_shape=[jax.ShapeDtypeStruct((L, CONVD), BF16), jax.ShapeDtypeStruct((8, CONVD), F32)],
        compiler_params=_params(("parallel",)), name=name)(zx, dact, w, b)


def _sc_fwd(proj, w, *, name):
    L = proj.shape[0]
    cb = 256
    nb = D // cb
    k = w.shape[0]

    def body(b_ref, c_ref, x_ref, w_ref, o_ref):
        u = c_ref[...] * x_ref[...]
        v = u * w_ref[k - 1:k, :]
        for j in range(1, k):
            v = v + _shift_down(u, j) * w_ref[k - 1 - j:k - j, :]
        o_ref[...] = (b_ref[...] * v).astype(BF16)

    return pl.pallas_call(
        body, grid=(nb,),
        in_specs=[pl.BlockSpec((L, cb), lambda i: (0, i)), pl.BlockSpec((L, cb), lambda i: (0, i + nb)),
                  pl.BlockSpec((L, cb), lambda i: (0, i + 2 * nb)), pl.BlockSpec((k, cb), lambda i: (0, i))],
        out_specs=pl.BlockSpec((L, cb), lambda i: (0, i)), out_shape=jax.ShapeDtypeStruct((L, D), BF16),
        compiler_params=_params(("parallel",)), name=name)(proj, proj, proj, w)


def _sc_bwd(proj, dyv, w, *, name):
    L = proj.shape[0]
    cb = 256
    nb = D // cb
    k = w.shape[0]

    def body(b_ref, c_ref, x_ref, dy_ref, w_ref, db_ref, dc_ref, dx_ref, s_ref):
        cv, xv = c_ref[...], x_ref[...]
        u = cv * xv
        sh = [_shift_down(u, j) for j in range(k)]
        v = sh[0] * w_ref[k - 1:k, :]
        for j in range(1, k):
            v = v + sh[j] * w_ref[k - 1 - j:k - j, :]
        dyv_ = dy_ref[...]
        db_ref[...] = (dyv_ * v).astype(BF16)
        dv = dyv_ * b_ref[...]
        du = dv * w_ref[k - 1:k, :]
        for j in range(1, k):
            du = du + _shift_up(dv, j) * w_ref[k - 1 - j:k - j, :]
        dc_ref[...] = (du * xv).astype(BF16)
        dx_ref[...] = (du * cv).astype(BF16)
        s_ref[...] = jnp.zeros_like(s_ref)
        for j in range(k):
            s_ref[k - 1 - j:k - j, :] = jnp.sum(dv * sh[j], axis=0, keepdims=True)

    blk = pl.BlockSpec((L, cb), lambda i: (0, i))
    return pl.pallas_call(
        body, grid=(nb,),
        in_specs=[blk, pl.BlockSpec((L, cb), lambda i: (0, i + nb)), pl.BlockSpec((L, cb), lambda i: (0, i + 2 * nb)),
                  blk, pl.BlockSpec((k, cb), lambda i: (0, i))],
        out_specs=[blk, blk, blk, pl.BlockSpec((8, cb), lambda i: (0, i))],
        out_shape=[jax.ShapeDtypeStruct((L, D), BF16)] * 3 + [jax.ShapeDtypeStruct((8, D), F32)],
        compiler_params=_params(("parallel",)), name=name)(proj, proj, proj, dyv, w)


def _ssd_chunk_terms(dtr, prm):
    lane = lax.broadcasted_iota(jnp.int32, (CH, LANES), 1)
    valid = lane < NH
    xdt = dtr + prm[0:1, :]
    dt = jnp.where(valid, jnp.maximum(xdt, 0.0) + jnp.log1p(jnp.exp(-jnp.abs(xdt))), 0.0)
    A = -jnp.exp(prm[1:2, :])
    ri = lax.broadcasted_iota(jnp.int32, (CH, CH), 0)
    ci = lax.broadcasted_iota(jnp.int32, (CH, CH), 1)
    cs = _dot((ri >= ci).astype(F32), dt * A, precision=HIGHEST)
    last = cs[CH - 1:CH, :]
    ex = (lax.broadcasted_iota(jnp.int32, (LANES, DI), 1) // HP == lax.broadcasted_iota(jnp.int32, (LANES, DI), 0)).astype(F32)
    return dict(valid=valid, xdt=xdt, dt=dt, A=A, cs=cs, csT=cs.T, last=last, ri=ri, ci=ci, ex=ex)


def _expand(v, ex):
    if v.shape[0] == 1:
        return _dot(jnp.broadcast_to(v, (8, LANES)), ex, precision=HIGHEST)[0:1, :]
    return _dot(v, ex, precision=HIGHEST)


def _head_sum(v, ex):
    if v.shape[0] == 1:
        return _dot(jnp.broadcast_to(v, (8, DI)), ex, ((1,), (1,)), precision=HIGHEST)[0:1, :]
    return _dot(v, ex, ((1,), (1,)), precision=HIGHEST)


def _ssd_fwd(xbc, dtr, prm, *, name):
    L = xbc.shape[0]
    nc = L // CH

    def body(xbc_ref, dtr_ref, prm_ref, y_ref, sp_ref, st_ref):
        @pl.when(pl.program_id(0) == 0)
        def _():
            st_ref[...] = jnp.zeros_like(st_ref)

        prm_v = prm_ref[...]
        t = _ssd_chunk_terms(dtr_ref[...], prm_v)
        cs, csT, ex, causal = t["cs"], t["csT"], t["ex"], t["ri"] >= t["ci"]
        xs = xbc_ref[:, 0:DI]
        X = xs * _expand(t["dt"], ex)
        Xb = X.astype(BF16)
        Xd = (X * _expand(jnp.exp(t["last"] - cs), ex)).astype(BF16)
        Ex = _expand(jnp.exp(cs), ex)
        cdx = _expand(jnp.exp(t["last"]), ex)
        dskx = _expand(prm_v[2:3, :], ex)
        lane = lax.broadcasted_iota(jnp.int32, (CH, LANES), 1)
        sp_ref[0] = st_ref[...]
        for g in range(NG):
            Bg = xbc_ref[:, DI + g * NS:DI + (g + 1) * NS].astype(BF16)
            Cg = xbc_ref[:, DI + GW + g * NS:DI + GW + (g + 1) * NS].astype(BF16)
            G = _dot_nt(Cg, Bg)
            Sg = st_ref[:, g * GW:(g + 1) * GW]
            yoff = _dot(Cg, Sg.astype(BF16)) * Ex[:, g * GW:(g + 1) * GW]
            for j in range(GW // LANES):
                lo = g * GW + j * LANES
                Xp = Xb[:, lo:lo + LANES]
                yd = []
                for h in (lo // HP, lo // HP + 1):
                    seg = cs[:, h:h + 1] - csT[h:h + 1, :]
                    yd.append(_dot((G * jnp.where(causal, jnp.exp(seg), 0.0)).astype(BF16), Xp))
                y_ref[:, lo:lo + LANES] = (jnp.where(lane < HP, yd[0], yd[1]) + yoff[:, j * LANES:(j + 1) * LANES]
                                           + dskx[:, lo:lo + LANES] * xs[:, lo:lo + LANES])
            st_ref[:, g * GW:(g + 1) * GW] = Sg * cdx[:, g * GW:(g + 1) * GW] + _dot_tn(Bg, Xd[:, g * GW:(g + 1) * GW])

    return pl.pallas_call(
        body, grid=(nc,),
        in_specs=[pl.BlockSpec((CH, CONVD), lambda c: (c, 0)), pl.BlockSpec((CH, LANES), lambda c: (c, 0)),
                  pl.BlockSpec((8, LANES), lambda c: (0, 0))],
        out_specs=[pl.BlockSpec((CH, DI), lambda c: (c, 0)), pl.BlockSpec((1, NS, DI), lambda c: (c, 0, 0))],
        out_shape=[jax.ShapeDtypeStruct((L, DI), F32), jax.ShapeDtypeStruct((nc, NS, DI), F32)],
        scratch_shapes=[pltpu.VMEM((NS, DI), F32)],
        compiler_params=_params(("arbitrary",)), name=name)(xbc, dtr, prm)


def _ssd_bwd(xbc, dtr, prm, dy, sprev, *, name):
    L = xbc.shape[0]
    nc = L // CH

    def body(xbc_ref, dtr_ref, prm_ref, dy_ref, sp_ref, dxbc_ref, ddtr_ref, s_ref, dst_ref, dx_scr, de_scr, dd_scr):
        step = pl.program_id(0)

        @pl.when(step == 0)
        def _():
            dst_ref[...] = jnp.zeros_like(dst_ref)
            s_ref[...] = jnp.zeros_like(s_ref)

        prm_v = prm_ref[...]
        t = _ssd_chunk_terms(dtr_ref[...], prm_v)
        cs, csT, ex, ri, ci = t["cs"], t["csT"], t["ex"], t["ri"], t["ci"]
        E = jnp.exp(cs)
        dec = jnp.exp(t["last"] - cs)
        cd = jnp.exp(t["last"])
        xs = xbc_ref[:, 0:DI]
        dtx = _expand(t["dt"], ex)
        X = xs * dtx
        Xb = X.astype(BF16)
        decx = _expand(dec, ex)
        Xd = (X * decx).astype(BF16)
        Ex = _expand(E, ex)
        cdx = _expand(cd, ex)
        dskx = _expand(prm_v[2:3, :], ex)
        lane = lax.broadcasted_iota(jnp.int32, (CH, LANES), 1)
        dcs = jnp.zeros((CH, LANES), F32)
        dcd_x = []
        for g in range(NG):
            gs = slice(g * GW, (g + 1) * GW)
            Bg = xbc_ref[:, DI + g * NS:DI + (g + 1) * NS].astype(BF16)
            Cg = xbc_ref[:, DI + GW + g * NS:DI + GW + (g + 1) * NS].astype(BF16)
            G = _dot_nt(Cg, Bg)
            GT = _dot_nt(Bg, Cg)
            Sg = sp_ref[0, :, gs]
            Sgb = Sg.astype(BF16)
            dyg = dy_ref[:, gs]
            de_scr[:, gs] = dyg * _dot(Cg, Sgb)
            dYo = (Ex[:, gs] * dyg).astype(BF16)
            dC = _dot_nt(dYo, Sgb)
            dS_in = _dot_tn(Cg, dYo)
            dStg = dst_ref[:, gs]
            dStb = dStg.astype(BF16)
            dXd = _dot(Bg, dStb)
            dB = _dot_nt(Xd[:, gs], dStb)
            dd_scr[:, gs] = dXd * X[:, gs]
            dXst = dXd * decx[:, gs]
            dG = jnp.zeros((CH, CH), F32)
            dGT = jnp.zeros((CH, CH), F32)
            for j in range(GW // LANES):
                lo = g * GW + j * LANES
                Xp = Xb[:, lo:lo + LANES]
                dyp = dy_ref[:, lo:lo + LANES]
                dXp = dXst[:, j * LANES:(j + 1) * LANES]
                for k, h in enumerate((lo // HP, lo // HP + 1)):
                    dyh = jnp.where((lane < HP) if k == 0 else (lane >= HP), dyp, 0.0).astype(BF16)
                    seg = cs[:, h:h + 1] - csT[h:h + 1, :]
                    Lm = jnp.where(ri >= ci, jnp.exp(seg), 0.0)
                    LmT = jnp.where(ci >= ri, jnp.exp(-seg), 0.0)
                    dM = _dot_nt(dyh, Xp)
                    dMT = _dot_nt(Xp, dyh)
                    MT = GT * LmT
                    rs = jnp.sum(dM * (G * Lm), axis=1, keepdims=True) - jnp.sum(dMT * MT, axis=1, keepdims=True)
                    dcs = dcs + jnp.where(lane == h, rs, 0.0)
                    dG = dG + dM * Lm
                    dGT = dGT + dMT * LmT
                    dXp = dXp + _dot(MT.astype(BF16), dyh)
                dx_scr[:, lo:lo + LANES] = dXp
            dxbc_ref[:, DI + g * NS:DI + (g + 1) * NS] = dB + _dot(dGT.astype(BF16), Cg)
            dxbc_ref[:, DI + GW + g * NS:DI + GW + (g + 1) * NS] = dC + _dot(dG.astype(BF16), Bg)
            dcd_x.append(jnp.sum(dStg * Sg, axis=0, keepdims=True))
            dst_ref[:, gs] = dStg * cdx[:, gs] + dS_in
        dX = dx_scr[...]
        dy = dy_ref[...]
        ddec = _head_sum(dd_scr[...], ex)
        dcd = _head_sum(jnp.concatenate(dcd_x, axis=1), ex)
        dcs = dcs + _head_sum(de_scr[...], ex) * E - ddec * dec
        row = lax.broadcasted_iota(jnp.int32, (CH, LANES), 0)
        dcs = dcs + jnp.where(row == CH - 1, jnp.sum(ddec * dec, axis=0, keepdims=True) + dcd * cd, 0.0)
        da = _dot((ci >= ri).astype(F32), dcs, precision=HIGHEST)
        ddt = da * t["A"] + _head_sum(dX * xs, ex)
        ddtr = jnp.where(t["valid"], ddt * _sigmoid(t["xdt"]), 0.0)
        ddtr_ref[...] = ddtr
        dxbc_ref[:, 0:DI] = dX * dtx + dskx * dy
        s_ref[0:1, :] += jnp.sum(da * t["dt"], axis=0, keepdims=True)
        s_ref[1:2, :] += _head_sum(jnp.sum(dy * xs, axis=0, keepdims=True), ex)
        s_ref[2:3, :] += jnp.sum(ddtr, axis=0, keepdims=True)

        @pl.when(step == nc - 1)
        def _():
            s_ref[0:1, :] = s_ref[0:1, :] * t["A"]

    rev = lambda c: (nc - 1 - c, 0)
    return pl.pallas_call(
        body, grid=(nc,),
        in_specs=[pl.BlockSpec((CH, CONVD), rev), pl.BlockSpec((CH, LANES), rev), pl.BlockSpec((8, LANES), lambda c: (0, 0)),
                  pl.BlockSpec((CH, DI), rev), pl.BlockSpec((1, NS, DI), lambda c: (nc - 1 - c, 0, 0))],
        out_specs=[pl.BlockSpec((CH, CONVD), rev), pl.BlockSpec((CH, LANES), rev), pl.BlockSpec((8, LANES), lambda c: (0, 0))],
        out_shape=[jax.ShapeDtypeStruct((L, CONVD), F32), jax.ShapeDtypeStruct((L, LANES), F32),
                   jax.ShapeDtypeStruct((8, LANES), F32)],
        scratch_shapes=[pltpu.VMEM((NS, DI), F32), pltpu.VMEM((CH, DI), F32), pltpu.VMEM((CH, DI), F32),
                        pltpu.VMEM((CH, DI), F32)],
        compiler_params=_params(("arbitrary",)), name=name)(xbc, dtr, prm, dy, sprev)


def _gnorm_fwd(y, zx, nw, *, name):
    L = y.shape[0]
    tm = min(L, 256)

    def body(y_ref, z_ref, nw_ref, o_ref):
        z = z_ref[...]
        yg = y_ref[...] * (z * _sigmoid(z))
        for g in range(NG):
            v = yg[:, g * GW:(g + 1) * GW]
            r = lax.rsqrt(jnp.mean(v * v, axis=-1, keepdims=True) + EPS)
            o_ref[:, g * GW:(g + 1) * GW] = (v * r * nw_ref[:, g * GW:(g + 1) * GW]).astype(BF16)

    row = pl.BlockSpec((tm, DI), lambda i: (i, 0))
    return pl.pallas_call(body, grid=(L // tm,), in_specs=[row, row, pl.BlockSpec((1, DI), lambda i: (0, 0))],
                          out_specs=row, out_shape=jax.ShapeDtypeStruct((L, DI), BF16),
                          compiler_params=_params(("parallel",)), name=name)(y, zx, nw)


def _gnorm_bwd(y, zx, nw, dyn, *, name):
    L = y.shape[0]
    tm = min(L, 256)

    def body(y_ref, z_ref, nw_ref, dyn_ref, dy_ref, dz_ref, s_ref):
        @pl.when(pl.program_id(0) == 0)
        def _():
            s_ref[...] = jnp.zeros_like(s_ref)

        z, yv = z_ref[...], y_ref[...]
        sz = _sigmoid(z)
        gate = z * sz
        dgate_dz = sz * (1.0 + z * (1.0 - sz))
        for g in range(NG):
            gs = slice(g * GW, (g + 1) * GW)
            v = yv[:, gs] * gate[:, gs]
            r = lax.rsqrt(jnp.mean(v * v, axis=-1, keepdims=True) + EPS)
            vhat = v * r
            dn = dyn_ref[:, gs]
            s_ref[0:1, gs] += jnp.sum(dn * vhat, axis=0, keepdims=True)
            dvhat = dn * nw_ref[:, gs]
            dv = r * (dvhat - vhat * jnp.mean(dvhat * vhat, axis=-1, keepdims=True))
            dy_ref[:, gs] = dv * gate[:, gs]
            dz_ref[:, gs] = (dv * yv[:, gs] * dgate_dz[:, gs]).astype(BF16)

    row = pl.BlockSpec((tm, DI), lambda i: (i, 0))
    return pl.pallas_call(body, grid=(L // tm,), in_specs=[row, row, pl.BlockSpec((1, DI), lambda i: (0, 0)), row],
                          out_specs=[row, row, pl.BlockSpec((8, DI), lambda i: (0, 0))],
                          out_shape=[jax.ShapeDtypeStruct((L, DI), F32), jax.ShapeDtypeStruct((L, DI), BF16),
                                     jax.ShapeDtypeStruct((8, DI), F32)],
                          compiler_params=_params(("arbitrary",)), name=name)(y, zx, nw, dyn)


def _adamw(w, g, m, v, *, name):
    R, C = w.shape
    tr = R
    while tr * C > 256 * 1024 and tr % 16 == 0:
        tr //= 2

    def body(w_ref, g_ref, m_ref, v_ref, d_ref, mo_ref, vo_ref):
        gv = g_ref[...]
        mn = ADAM_B1 * m_ref[...] + (1.0 - ADAM_B1) * gv
        vn = ADAM_B2 * v_ref[...] + (1.0 - ADAM_B2) * (gv * gv)
        m_hat = mn / (1.0 - ADAM_B1 ** ADAM_STEP)
        v_hat = vn / (1.0 - ADAM_B2 ** ADAM_STEP)
        d_ref[...] = -ADAM_LR * (m_hat / (jnp.sqrt(v_hat) + ADAM_EPS) + ADAM_WD * w_ref[...])
        mo_ref[...] = mn
        vo_ref[...] = vn

    blk = pl.BlockSpec((tr, C), lambda i: (i, 0))
    return pl.pallas_call(body, grid=(R // tr,), in_specs=[blk] * 4, out_specs=[blk] * 3,
                          out_shape=[jax.ShapeDtypeStruct((R, C), F32)] * 3,
                          compiler_params=_params(("parallel",)), name=name)(w, g, m, v)


def _residual(acc, xv, gv):
    return xv + gv * acc, acc


def _relu2(acc):
    a = jnp.maximum(acc, 0.0)
    return a, a * a


def _mlp_fwd(x, mod, nw, up, down, tag):
    sh, sc, g = mod
    h = _modnorm_fwd(x, nw, sc, sh, name=tag + "_norm")
    a, act = _matmul(h, up, epi=_relu2, out_dtypes=(BF16, BF16), name=tag + "_up")
    xn, y = _matmul(act, down, extras=(x, g), epi=_residual, out_dtypes=(F32, F32), name=tag + "_down")
    return xn, (x, h, a, act, y)


def _mlp_bwd(dxo, saved, mod, nw, up, down, tag):
    x, h, a, act, y = saved
    sh, sc, g = mod
    dy, gsum = _gate_bwd(dxo, y, g, name=tag + "_dgate")
    du = _matmul(dy, down, trans_b=True, extras=(a,), epi=lambda acc, av: (acc * (2.0 * av.astype(F32)),),
                 out_dtypes=(BF16,), name=tag + "_dact")
    d_down = _matmul_tn(act, dy, name=tag + "_ddown")
    dh = _matmul(du, up, trans_b=True, name=tag + "_dh")
    d_up = _matmul_tn(h, du, name=tag + "_dup")
    dx, nsum = _modnorm_bwd(x, dh, dxo, nw, sc, name=tag + "_dnorm")
    return dx, d_up, d_down, nsum[0:1], jnp.concatenate([nsum[2:3], nsum[1:2], gsum[0:1]], axis=0)


def _ssd_layer_fwd(x, mod, nw, w_zx, w_dt, conv_w, conv_b, prm, gn_w, out_w, tag):
    sh, sc, g = mod
    h = _modnorm_fwd(x, nw, sc, sh, name=tag + "_norm")
    zx = _matmul(h, w_zx, name=tag + "_in")
    dtr = _matmul(h, w_dt, name=tag + "_in_dt")
    xbc = _ssd_conv_fwd(zx, conv_w, conv_b, name=tag + "_conv")
    y, sprev = _ssd_fwd(xbc, dtr, prm, name=tag + "_scan")
    yn = _gnorm_fwd(y, zx, gn_w, name=tag + "_gnorm")
    xn, yo = _matmul(yn, out_w, extras=(x, g), epi=_residual, out_dtypes=(F32, F32), name=tag + "_out")
    return xn, (x, h, zx, dtr, xbc, y, sprev, yn, yo)


def _ssd_layer_bwd(dxo, saved, mod, nw, w_zx, w_dt, conv_w, conv_b, prm, gn_w, out_w, tag):
    x, h, zx, dtr, xbc, y, sprev, yn, yo = saved
    sh, sc, g = mod
    dyo, gsum = _gate_bwd(dxo, yo, g, name=tag + "_dgate")
    dyn = _matmul(dyo, out_w, trans_b=True, name=tag + "_dyn")
    d_out_w = _matmul_tn(yn, dyo, name=tag + "_dout")
    dy, dz, gnsum = _gnorm_bwd(y, zx, gn_w, dyn, name=tag + "_dgnorm")
    dxbc, ddtr, ssum = _ssd_bwd(xbc, dtr, prm, dy, sprev, name=tag + "_dscan")
    draw, csum = _ssd_conv_bwd(zx, dxbc, conv_w, conv_b, name=tag + "_dconv")
    dzx = jnp.concatenate([dz, draw], axis=1)
    dh_dt = _matmul(ddtr, w_dt, trans_b=True, name=tag + "_dh_dt")
    dh = _matmul(dzx, w_zx, trans_b=True, extras=(dh_dt,), epi=lambda acc, e: (acc + e,), name=tag + "_dh")
    d_w_zx = _matmul_tn(h, dzx, name=tag + "_din")
    d_w_dt = _matmul_tn(h, ddtr, name=tag + "_din_dt")
    dx, nsum = _modnorm_bwd(x, dh, dxo, nw, sc, name=tag + "_dnorm")
    d_in_w = jnp.concatenate([d_w_zx, d_w_dt[:, :NH]], axis=1)
    dmod = jnp.concatenate([nsum[2:3], nsum[1:2], gsum[0:1]], axis=0)
    small = dict(conv_w=csum[0:4], conv_b=csum[4:5], dt_bias=ssum[2:3, :NH], A_log=ssum[0:1, :NH], D=ssum[1:2, :NH],
                 norm_w=gnsum[0:1])
    return dx, d_in_w, d_out_w, nsum[0:1], dmod, small


def _sc_layer_fwd(x, mod, nw, in_w, conv_w, out_w, tag):
    sh, sc, g = mod
    h = _modnorm_fwd(x, nw, sc, sh, name=tag + "_norm")
    proj = _matmul(h, in_w, name=tag + "_in")
    yv = _sc_fwd(proj, conv_w, name=tag + "_conv")
    xn, yo = _matmul(yv, out_w, extras=(x, g), epi=_residual, out_dtypes=(F32, F32), name=tag + "_out")
    return xn, (x, h, proj, yv, yo)


def _sc_layer_bwd(dxo, saved, mod, nw, in_w, conv_w, out_w, tag):
    x, h, proj, yv, yo = saved
    sh, sc, g = mod
    dyo, gsum = _gate_bwd(dxo, yo, g, name=tag + "_dgate")
    dyv = _matmul(dyo, out_w, trans_b=True, name=tag + "_dyv")
    d_out_w = _matmul_tn(yv, dyo, name=tag + "_dout")
    db, dc, dxv, csum = _sc_bwd(proj, dyv, conv_w, name=tag + "_dconv")
    dproj = jnp.concatenate([db, dc, dxv], axis=1)
    dh = _matmul(dproj, in_w, trans_b=True, name=tag + "_dh")
    d_in_w = _matmul_tn(h, dproj, name=tag + "_din")
    dx, nsum = _modnorm_bwd(x, dh, dxo, nw, sc, name=tag + "_dnorm")
    dmod = jnp.concatenate([nsum[2:3], nsum[1:2], gsum[0:1]], axis=0)
    return dx, d_in_w, d_out_w, nsum[0:1], dmod, csum[0:3]


def _local_step(x, tgt, mod, p):
    m = [[mod[i, j:j + 1] for j in range(6)] for i in range(2)]
    row = lambda v: v.reshape(1, -1)
    prm = jnp.zeros((8, LANES), F32).at[0, :NH].set(p["ssd_dt_bias"][0]).at[1, :NH].set(p["ssd_A_log"][0]).at[2, :NH].set(p["ssd_D"][0])
    ssd_args = (p["w_zx"], p["w_dt"], p["ssd_conv_w"][0], p["ssd_conv_b"], prm, p["ssd_norm_w"], p["ssd_out_w"])
    sc_args = (p["sc_in_w"], p["sc_conv_w"][0], p["sc_out_w"])
    mix_nw = [row(p["mix_norm_w"][i]) for i in range(2)]
    mlp_nw = [row(p["mlp_norm_w"][i]) for i in range(2)]

    x1, s_ssd = _ssd_layer_fwd(x, m[0][0:3], mix_nw[0], *ssd_args, tag="ssd")
    x2, s_mlp0 = _mlp_fwd(x1, m[0][3:6], mlp_nw[0], p["mlp_up"][0], p["mlp_down"][0], "mlp0")
    x3, s_sc = _sc_layer_fwd(x2, m[1][0:3], mix_nw[1], *sc_args, tag="sc")
    x4, s_mlp1 = _mlp_fwd(x3, m[1][3:6], mlp_nw[1], p["mlp_up"][1], p["mlp_down"][1], "mlp1")
    dx4, fsum = _final_loss(x4, row(p["final_norm_w"]), tgt, name="final_loss")
    dx3, d_up1, d_down1, d_mlpn1, dmod_f1 = _mlp_bwd(dx4, s_mlp1, m[1][3:6], mlp_nw[1], p["mlp_up"][1], p["mlp_down"][1], "mlp1")
    dx2, d_sc_in, d_sc_out, d_mixn1, dmod_m1, d_sc_conv = _sc_layer_bwd(dx3, s_sc, m[1][0:3], mix_nw[1], *sc_args, tag="sc")
    dx1, d_up0, d_down0, d_mlpn0, dmod_f0 = _mlp_bwd(dx2, s_mlp0, m[0][3:6], mlp_nw[0], p["mlp_up"][0], p["mlp_down"][0], "mlp0")
    dx0, d_ssd_in, d_ssd_out, d_mixn0, dmod_m0, ssd_small = _ssd_layer_bwd(dx1, s_ssd, m[0][0:3], mix_nw[0], *ssd_args, tag="ssd")

    big = dict(mlp_up=[d_up0, d_up1], mlp_down=[d_down0, d_down1], ssd_in_w=d_ssd_in, ssd_out_w=d_ssd_out,
               sc_in_w=d_sc_in, sc_out_w=d_sc_out)
    pad = lambda v: jnp.pad(v, ((0, 0), (0, D - v.shape[1])))
    small = jnp.concatenate([
        dmod_m0, dmod_f0, dmod_m1, dmod_f1,
        d_mixn0, d_mixn1, d_mlpn0, d_mlpn1,
        ssd_small["conv_w"].reshape(12, D), ssd_small["conv_b"].reshape(3, D), ssd_small["norm_w"].reshape(2, D),
        fsum[0:1], d_sc_conv,
        pad(jnp.concatenate([ssd_small["dt_bias"], ssd_small["A_log"], ssd_small["D"]], axis=1)),
        fsum[1:2], jnp.zeros((1, D), F32)], axis=0)
    return dx0, big, small


SMALL_ROWS = 40
HALF_ROWS = 3472
PACK_ROWS = 6920
ADD_ROWS = 496
ANY = pl.BlockSpec(memory_space=pl.ANY)


def _all_gather_rows(blk, *, name):
    m_per, n = blk.shape

    def body(x_ref, out_ref, send_sems, recv_sems, local_sem):
        x, y, c = lax.axis_index("x"), lax.axis_index("y"), lax.axis_index("c")
        me, sibling = (x, y, c), (x, y, 1 - c)
        chips = [(1 - x, y), (x, 1 - y), (1 - x, 1 - y)]

        def rows(px, py, pc):
            return out_ref.at[pl.ds((4 * px + 2 * py + pc) * m_per, m_per), :]

        def copy(k, block, to, src=None):
            return pltpu.make_async_remote_copy(src_ref=rows(*block) if src is None else src, dst_ref=rows(*block),
                                                send_sem=send_sems.at[k], recv_sem=recv_sems.at[k], device_id=to,
                                                device_id_type=MESH)

        mine = pltpu.make_async_copy(x_ref, rows(*me), local_sem)
        mine.start()
        first = [copy(0, me, sibling, src=x_ref)] + [copy(1 + j, me, (*chip, c), src=x_ref) for j, chip in enumerate(chips)]
        for cp in first:
            cp.start()
        passed = [copy(4 + j, (*chip, c), sibling) for j, chip in enumerate(chips)]
        for j, chip in enumerate(chips):
            copy(1 + j, (*chip, c), me).wait_recv()
            passed[j].start()
        copy(0, sibling, me).wait_recv()
        for j, chip in enumerate(chips):
            copy(4 + j, (*chip, 1 - c), me).wait_recv()
        for cp in first + passed:
            cp.wait_send()
        mine.wait()

    return pl.pallas_call(
        body, out_shape=jax.ShapeDtypeStruct((N_DEV * m_per, n), blk.dtype),
        in_specs=[pl.BlockSpec(memory_space=pltpu.VMEM)], out_specs=pl.BlockSpec(memory_space=pltpu.VMEM),
        scratch_shapes=[pltpu.SemaphoreType.DMA((7,)), pltpu.SemaphoreType.DMA((7,)), pltpu.SemaphoreType.DMA],
        name=name)(blk)


def _gather_weights(wp, *, name):
    _, hr, n = wp.shape

    def body(w_ref, out_ref, send_sems, recv_sems, local_sem):
        x, y, c = lax.axis_index("x"), lax.axis_index("y"), lax.axis_index("c")
        me, sibling = (x, y, c), (x, y, 1 - c)
        chips = [(1 - x, y), (x, 1 - y), (1 - x, 1 - y)]

        def slot(px, py, pc):
            return out_ref.at[2 * px + py, pc]

        def copy(k, block, to, src=None):
            return pltpu.make_async_remote_copy(src_ref=slot(*block) if src is None else src, dst_ref=slot(*block),
                                                send_sem=send_sems.at[k], recv_sem=recv_sems.at[k], device_id=to,
                                                device_id_type=MESH)

        src = w_ref.at[c]
        mine = pltpu.make_async_copy(src, slot(*me), local_sem)
        mine.start()
        first = [copy(0, me, sibling, src=src)] + [copy(1 + j, me, (*chip, c), src=src) for j, chip in enumerate(chips)]
        for cp in first:
            cp.start()
        passed = [copy(4 + j, (*chip, c), sibling) for j, chip in enumerate(chips)]
        for j, chip in enumerate(chips):
            copy(1 + j, (*chip, c), me).wait_recv()
            passed[j].start()
        copy(0, sibling, me).wait_recv()
        for j, chip in enumerate(chips):
            copy(4 + j, (*chip, 1 - c), me).wait_recv()
        for cp in first + passed:
            cp.wait_send()
        mine.wait()

    return pl.pallas_call(
        body, out_shape=jax.ShapeDtypeStruct((N_CHIPS, 2, hr, n), wp.dtype), in_specs=[ANY], out_specs=ANY,
        scratch_shapes=[pltpu.SemaphoreType.DMA((7,)), pltpu.SemaphoreType.DMA((7,)), pltpu.SemaphoreType.DMA],
        name=name)(wp)


def _swap_half_with_sibling(gp, *, name):
    _, nk, hr, n = gp.shape

    def body(g_ref, out_ref, send_sem, recv_sem):
        x, y, c = lax.axis_index("x"), lax.axis_index("y"), lax.axis_index("c")
        cp = pltpu.make_async_remote_copy(src_ref=g_ref.at[1 - c], dst_ref=out_ref, send_sem=send_sem, recv_sem=recv_sem,
                                          device_id=(x, y, 1 - c), device_id_type=MESH)
        cp.start()
        cp.wait()

    return pl.pallas_call(body, out_shape=jax.ShapeDtypeStruct((nk, hr, n), gp.dtype), in_specs=[ANY], out_specs=ANY,
                          scratch_shapes=[pltpu.SemaphoreType.DMA, pltpu.SemaphoreType.DMA], name=name)(gp)


def _send_to_owners(h, *, name):
    _, hr, n = h.shape

    def body(h_ref, out_ref, send_sems, recv_sems):
        x, y, c = lax.axis_index("x"), lax.axis_index("y"), lax.axis_index("c")
        chips = [(1 - x, y), (x, 1 - y), (1 - x, 1 - y)]
        copies = [pltpu.make_async_remote_copy(src_ref=h_ref.at[2 * cx + cy], dst_ref=out_ref.at[j], send_sem=send_sems.at[j],
                                               recv_sem=recv_sems.at[j], device_id=(cx, cy, c), device_id_type=MESH)
                  for j, (cx, cy) in enumerate(chips)]
        for cp in copies:
            cp.start()
        for cp in copies:
            cp.wait()

    return pl.pallas_call(body, out_shape=jax.ShapeDtypeStruct((3, hr, n), h.dtype), in_specs=[ANY], out_specs=ANY,
                          scratch_shapes=[pltpu.SemaphoreType.DMA((3,)), pltpu.SemaphoreType.DMA((3,))], name=name)(h)


def _swap_result_with_sibling(t2, *, name):
    _, hr, n = t2.shape

    def body(t_ref, out_ref, send_sem, recv_sem):
        del t_ref
        x, y, c = lax.axis_index("x"), lax.axis_index("y"), lax.axis_index("c")
        cp = pltpu.make_async_remote_copy(src_ref=out_ref.at[c], dst_ref=out_ref.at[c], send_sem=send_sem, recv_sem=recv_sem,
                                          device_id=(x, y, 1 - c), device_id_type=MESH)
        cp.start()
        cp.wait()

    return pl.pallas_call(body, out_shape=jax.ShapeDtypeStruct(t2.shape, t2.dtype), in_specs=[ANY], out_specs=ANY,
                          scratch_shapes=[pltpu.SemaphoreType.DMA, pltpu.SemaphoreType.DMA],
                          input_output_aliases={0: 0}, name=name)(t2)


def _add_sibling_half(gp, recv, core, *, name):
    _, nk, hr, n = gp.shape

    def body(c_ref, a_ref, b_ref, o_ref):
        o_ref[...] = (a_ref[...].astype(F32) + b_ref[...].astype(F32)).astype(BF16)

    grid_spec = pltpu.PrefetchScalarGridSpec(
        num_scalar_prefetch=1, grid=(nk, hr // ADD_ROWS),
        in_specs=[pl.BlockSpec((None, None, ADD_ROWS, n), lambda k, i, c_ref: (c_ref[0], k, i, 0)),
                  pl.BlockSpec((None, ADD_ROWS, n), lambda k, i, c_ref: (k, i, 0))],
        out_specs=pl.BlockSpec((None, ADD_ROWS, n), lambda k, i, c_ref: (k, i, 0)))
    return pl.pallas_call(body, grid_spec=grid_spec, out_shape=jax.ShapeDtypeStruct((nk, hr, n), BF16),
                          compiler_params=_params(("parallel", "parallel")), name=name)(core, gp, recv)


def _add_chip_sums(h, recv, chip_core, *, name):
    _, hr, n = h.shape

    def body(k_ref, a_ref, b_ref, o_ref):
        o_ref[...] = ((a_ref[...].astype(F32) + b_ref[0].astype(F32)) + b_ref[1].astype(F32)) + b_ref[2].astype(F32)

    grid_spec = pltpu.PrefetchScalarGridSpec(
        num_scalar_prefetch=1, grid=(hr // ADD_ROWS,),
        in_specs=[pl.BlockSpec((None, ADD_ROWS, n), lambda i, k_ref: (k_ref[0], i, 0)),
                  pl.BlockSpec((3, ADD_ROWS, n), lambda i, k_ref: (0, i, 0))],
        out_specs=pl.BlockSpec((None, ADD_ROWS, n), lambda i, k_ref: (k_ref[1], i, 0)))
    return pl.pallas_call(body, grid_spec=grid_spec, out_shape=jax.ShapeDtypeStruct((2, hr, n), F32),
                          compiler_params=_params(("parallel",)), name=name)(chip_core, h, recv)


def _sum_devices(g, *, name):
    nd, r, n = g.shape

    def body(g_ref, o_ref):
        acc = g_ref[0]
        for i in range(1, nd):
            acc = acc + g_ref[i]
        o_ref[...] = acc

    return pl.pallas_call(body, out_shape=jax.ShapeDtypeStruct((r, n), F32), name=name)(g)


def _pack_shards(up, down, ssd_in, ssd_out, sc_in, sc_out):
    rows = jnp.concatenate([v.astype(BF16).reshape(-1, D) for v in (up, down, ssd_in, ssd_out, sc_in, sc_out)], axis=0)
    assert rows.shape[0] == PACK_ROWS, rows.shape
    return jnp.pad(rows, ((0, 2 * HALF_ROWS - PACK_ROWS), (0, 0))).reshape(2, HALF_ROWS, D)


_PACK_SPLITS = (2048, 4096, 5384, 5896, 6664, 6920)
_SHARD_SHAPES = ((2, 1024, 1024), (2, 1024, 1024), (1, 1024, 1288), (1, 512, 1024), (1, 1024, 768), (1, 256, 1024))


def _unpack_rows(rows, lead=()):
    out, lo = [], 0
    for hi, shp in zip(_PACK_SPLITS, _SHARD_SHAPES, strict=True):
        out.append(rows[..., lo:hi, :].reshape(lead + shp))
        lo = hi
    return out


def kernel(x, c, ada_w, ada_b, mix_norm_w, mlp_norm_w, mlp_up, mlp_down, ssd_in_w, ssd_conv_w, ssd_conv_b, ssd_dt_bias, ssd_A_log, ssd_D, ssd_norm_w, ssd_out_w, sc_in_w, sc_conv_w, sc_out_w, final_norm_w, loss_target, m_ada_w, m_ada_b, m_mix_norm_w, m_mlp_norm_w, m_mlp_up, m_mlp_down, m_ssd_in_w, m_ssd_conv_w, m_ssd_conv_b, m_ssd_dt_bias, m_ssd_A_log, m_ssd_D, m_ssd_norm_w, m_ssd_out_w, m_sc_in_w, m_sc_conv_w, m_sc_out_w, m_final_norm_w, v_ada_w, v_ada_b, v_mix_norm_w, v_mlp_norm_w, v_mlp_up, v_mlp_down, v_ssd_in_w, v_ssd_conv_w, v_ssd_conv_b, v_ssd_dt_bias, v_ssd_A_log, v_ssd_D, v_ssd_norm_w, v_ssd_out_w, v_sc_in_w, v_sc_conv_w, v_sc_out_w, v_final_norm_w):
    xi, yi, ci = lax.axis_index("x"), lax.axis_index("y"), lax.axis_index("c")
    chip = 2 * xi + yi
    dev = 2 * chip + ci
    n_ada = ada_w.shape[2]

    wg = _gather_weights(_pack_shards(mlp_up, mlp_down, ssd_in_w, ssd_out_w, sc_in_w, sc_out_w), name="gather_weights")
    up_k, down_k, ssd_in_k, ssd_out_k, sc_in_k, sc_out_k = _unpack_rows(wg.reshape(N_CHIPS, 2 * HALF_ROWS, D), (N_CHIPS,))
    cols = lambda v: jnp.moveaxis(v, 0, -2).reshape(v.shape[1:-1] + (N_CHIPS * v.shape[-1],))
    rows = lambda v: jnp.moveaxis(v, 0, -3).reshape(v.shape[1:-2] + (N_CHIPS * v.shape[-2], v.shape[-1]))
    ssd_in_full = cols(ssd_in_k)[0]
    p = dict(mlp_up=cols(up_k), mlp_down=rows(down_k), w_zx=ssd_in_full[:, :ZX],
             w_dt=jnp.pad(ssd_in_full[:, ZX:], ((0, 0), (0, LANES - NH))), ssd_out_w=rows(ssd_out_k)[0],
             sc_in_w=cols(sc_in_k)[0], sc_out_w=rows(sc_out_k)[0], mix_norm_w=mix_norm_w, mlp_norm_w=mlp_norm_w,
             ssd_conv_b=ssd_conv_b, ssd_dt_bias=ssd_dt_bias, ssd_A_log=ssd_A_log, ssd_D=ssd_D, ssd_norm_w=ssd_norm_w,
             final_norm_w=final_norm_w)

    conv_rows = (ssd_conv_w.size + sc_conv_w.size) // 256
    blk0 = jnp.concatenate([c, jnp.zeros((7, D), F32)], axis=0)
    c_all = _all_gather_rows(blk0, name="gather_cond").reshape(N_DEV, 8, D)[:, 0]
    blk1 = jnp.concatenate([ssd_conv_w.reshape(-1, 256), sc_conv_w.reshape(-1, 256), jnp.zeros((1, 256), F32)], axis=0)
    conv_all = _all_gather_rows(blk1, name="gather_conv").reshape(N_DEV, conv_rows + 1, 256)[0::2]
    p["ssd_conv_w"] = jnp.moveaxis(conv_all[:, 0:12].reshape(N_CHIPS, 4, 768), 0, 1).reshape(1, 4, CONVD)
    p["sc_conv_w"] = jnp.moveaxis(conv_all[:, 12:15].reshape(N_CHIPS, 3, 256), 0, 1).reshape(1, 3, D)
    mod_shard = [_matmul(c_all, ada_w[i], a_silu=True, extras=(lax.dynamic_slice(ada_b, (i, chip * n_ada), (1, n_ada)),),
                         epi=lambda acc, b: (acc + b,), name=f"ada_mod{i}") for i in range(2)]
    mod_all = _all_gather_rows(jnp.concatenate(mod_shard, axis=0), name="gather_mod")
    mod_all = mod_all.reshape(N_DEV, 2, N_DEV, n_ada)[0::2]
    mod = jnp.moveaxis(lax.dynamic_index_in_dim(mod_all, dev, axis=2, keepdims=False), 0, 1).reshape(2, 6, D)

    grad_x, big, small = _local_step(x[0], loss_target[0], mod, p)

    small_all = _all_gather_rows(small, name="gather_small").reshape(N_DEV, SMALL_ROWS, D)
    tot = _sum_devices(small_all, name="sum_small")
    loss = tot[38, 0]
    g_ada_b = tot[0:12].reshape(2, 6 * D)
    g_mix_norm, g_mlp_norm = tot[12:14], tot[14:16]
    g_ssd_conv_w = lax.dynamic_slice(tot[16:28].reshape(1, 4, CONVD), (0, 0, chip * 768), (1, 4, 768))
    g_ssd_conv_b, g_ssd_norm = tot[28:31].reshape(1, CONVD), tot[31:33].reshape(1, DI)
    g_final = tot[33]
    g_sc_conv_w = lax.dynamic_slice(tot[34:37].reshape(1, 3, D), (0, 0, chip * 256), (1, 3, 256))
    g_dt_bias, g_a_log, g_d = tot[37:38, 0:NH], tot[37:38, NH:2 * NH], tot[37:38, 2 * NH:3 * NH]
    c_pad = jnp.concatenate([c_all, jnp.zeros((8, D), F32)], axis=0)
    g_ada_w = []
    for i in range(2):
        dm = lax.dynamic_slice(small_all[:, 6 * i:6 * i + 6].reshape(N_DEV, 6 * D), (0, chip * n_ada), (N_DEV, n_ada))
        g_ada_w.append(_matmul_tn(c_pad, jnp.concatenate([dm, jnp.zeros_like(dm)], axis=0), a_silu=True, name=f"ada_dw{i}"))
    g_ada_w = jnp.stack(g_ada_w)

    def chip_rows(k):
        return _pack_shards(jnp.stack([g[:, k * 1024:(k + 1) * 1024] for g in big["mlp_up"]]),
                            jnp.stack([g[k * 1024:(k + 1) * 1024] for g in big["mlp_down"]]),
                            big["ssd_in_w"][:, k * 1288:(k + 1) * 1288], big["ssd_out_w"][k * 512:(k + 1) * 512],
                            big["sc_in_w"][:, k * 768:(k + 1) * 768], big["sc_out_w"][k * 256:(k + 1) * 256])

    gp = jnp.stack([chip_rows(k) for k in range(N_CHIPS)], axis=1)
    core = ci.reshape(1).astype(jnp.int32)
    sib = _swap_half_with_sibling(gp, name="rs_sibling")
    h = _add_sibling_half(gp, sib, core, name="rs_add_sibling")
    others = _send_to_owners(h, name="rs_owners")
    t2 = _add_chip_sums(h, others, jnp.stack([chip, ci]).astype(jnp.int32), name="rs_add_chips")
    gsh = _swap_result_with_sibling(t2, name="rs_result").reshape(2 * HALF_ROWS, D)
    g_up, g_down, g_ssd_in, g_ssd_out, g_sc_in, g_sc_out = _unpack_rows(gsh)

    grads = dict(ada_w=g_ada_w, ada_b=g_ada_b, mix_norm_w=g_mix_norm, mlp_norm_w=g_mlp_norm, mlp_up=g_up, mlp_down=g_down,
                 ssd_in_w=g_ssd_in, ssd_conv_w=g_ssd_conv_w, ssd_conv_b=g_ssd_conv_b, ssd_dt_bias=g_dt_bias,
                 ssd_A_log=g_a_log, ssd_D=g_d, ssd_norm_w=g_ssd_norm, ssd_out_w=g_ssd_out, sc_in_w=g_sc_in,
                 sc_conv_w=g_sc_conv_w, sc_out_w=g_sc_out, final_norm_w=g_final)
    weights = dict(ada_w=(ada_w, m_ada_w, v_ada_w), ada_b=(ada_b, m_ada_b, v_ada_b),
                   mix_norm_w=(mix_norm_w, m_mix_norm_w, v_mix_norm_w), mlp_norm_w=(mlp_norm_w, m_mlp_norm_w, v_mlp_norm_w),
                   mlp_up=(mlp_up, m_mlp_up, v_mlp_up), mlp_down=(mlp_down, m_mlp_down, v_mlp_down),
                   ssd_in_w=(ssd_in_w, m_ssd_in_w, v_ssd_in_w), ssd_conv_w=(ssd_conv_w, m_ssd_conv_w, v_ssd_conv_w),
                   ssd_conv_b=(ssd_conv_b, m_ssd_conv_b, v_ssd_conv_b), ssd_dt_bias=(ssd_dt_bias, m_ssd_dt_bias, v_ssd_dt_bias),
                   ssd_A_log=(ssd_A_log, m_ssd_A_log, v_ssd_A_log), ssd_D=(ssd_D, m_ssd_D, v_ssd_D),
                   ssd_norm_w=(ssd_norm_w, m_ssd_norm_w, v_ssd_norm_w), ssd_out_w=(ssd_out_w, m_ssd_out_w, v_ssd_out_w),
                   sc_in_w=(sc_in_w, m_sc_in_w, v_sc_in_w), sc_conv_w=(sc_conv_w, m_sc_conv_w, v_sc_conv_w),
                   sc_out_w=(sc_out_w, m_sc_out_w, v_sc_out_w), final_norm_w=(final_norm_w, m_final_norm_w, v_final_norm_w))
    g_out, d_out, m_out, v_out = [], [], [], []
    for nm, (w, m, v) in weights.items():
        shp = w.shape
        two_d = (-1, shp[-1]) if w.ndim > 1 else (1, -1)
        g = grads[nm].reshape(shp)
        dl, mn, vn = _adamw(w.reshape(two_d), g.reshape(two_d), m.reshape(two_d), v.reshape(two_d), name="adamw_" + nm)
        g_out.append(g)
        d_out.append(dl.reshape(shp))
        m_out.append(mn.reshape(shp))
        v_out.append(vn.reshape(shp))
    return (loss, grad_x[None], *g_out, *d_out, *m_out, *v_out)
```

```python
import jax
import jax.numpy as jnp
from jax import lax
from jax.experimental import pallas as pl
from jax.experimental.pallas import tpu as pltpu

F32 = jnp.float32
BF16 = jnp.bfloat16
MESH = pl.DeviceIdType.MESH
HIGHEST = lax.Precision.HIGHEST

D = 1024
DFF = 4096
DI = 2048
NH = 32
HP = 64
NG = 4
NS = 128
CH = 128
CONVD = DI + 2 * NG * NS
ZX = DI + CONVD
GW = NG * NS
LANES = 128
N_CHIPS = 4
N_DEV = 8
EPS = 1e-5
ADAM_LR, ADAM_B1, ADAM_B2, ADAM_EPS, ADAM_WD, ADAM_STEP = 1e-3, 0.9, 0.999, 1e-8, 0.01, 10
VMEM_LIMIT = 48 * 1024 * 1024
ANY = pl.BlockSpec(memory_space=pl.ANY)

UP_ROW, DOWN_ROW, SSD_OUT_ROW, SC_OUT_ROW, W_ROWS = 0, 2048, 4096, 4608, 4864
SSD_IN_SHARD = 1288
SC_IN_SHARD = 768


def _params(sem=None):
    return pltpu.CompilerParams(dimension_semantics=sem, vmem_limit_bytes=VMEM_LIMIT)


def _sigmoid(v):
    return 1.0 / (1.0 + jnp.exp(-v))


def _dot(a, b, dims=((1,), (0,)), precision=None):
    return lax.dot_general(a, b, (dims, ((), ())), preferred_element_type=F32, precision=precision)


def _dot_nt(a, b):
    return _dot(a, b, ((1,), (1,)))


def _dot_tn(a, b):
    return _dot(a, b, ((0,), (0,)))


def _nn(av, bv):
    return _dot(av.astype(BF16), bv.astype(BF16))


def _nt(av, bv):
    return _dot_nt(av.astype(BF16), bv.astype(BF16))


def _nn_split(av, bv):
    r = bv.shape[1]
    acc = _dot(av[:, 0:r].astype(BF16), bv[0])
    for s in range(1, bv.shape[0]):
        acc = acc + _dot(av[:, s * r:(s + 1) * r].astype(BF16), bv[s])
    return acc


def _nt_split(av, bv):
    kc = bv.shape[2]
    acc = _dot_nt(av[:, 0:kc].astype(BF16), bv[0])
    for s in range(1, bv.shape[0]):
        acc = acc + _dot_nt(av[:, s * kc:(s + 1) * kc].astype(BF16), bv[s])
    return acc


def _nt_sc_in(av, bv):
    q = 256
    acc = None
    for i in range(3 * D // q):
        a_blk = av[i // 4][:, (i % 4) * q:(i % 4 + 1) * q]
        b_blk = bv[i // 3][:, (i % 3) * q:(i % 3 + 1) * q]
        t = _dot_nt(a_blk, b_blk)
        acc = t if acc is None else acc + t
    return acc


def _matmul(a, b, *, name, n, contract=_nn, a_spec=None, b_spec=None, tm=512, tn=512, extras=(), epi=None,
            out_dtypes=(F32,), a_silu=False):
    M = a.shape[-2]
    tm, tn = min(tm, M), min(tn, n)
    assert M % tm == 0 and n % tn == 0, (name, M, n, tm, tn)
    n_ex = len(extras)
    if a_spec is None:
        a_spec = pl.BlockSpec((tm, a.shape[1]), lambda i, j: (i, 0))
    if b_spec is None:
        b_spec = (pl.BlockSpec((tn, b.shape[1]), lambda i, j: (j, 0)) if contract is _nt
                  else pl.BlockSpec((b.shape[0], tn), lambda i, j: (0, j)))

    def body(*refs):
        av = refs[0][...]
        if a_silu:
            av = av * _sigmoid(av)
        acc = contract(av, refs[1][...])
        res = epi(acc, *[r[...] for r in refs[2:2 + n_ex]]) if epi is not None else (acc,)
        for o_ref, r in zip(refs[2 + n_ex:], res, strict=True):
            o_ref[...] = r.astype(o_ref.dtype)

    in_specs = [a_spec, b_spec]
    for e in extras:
        in_specs.append(pl.BlockSpec((1, tn), lambda i, j: (0, j)) if e.shape[0] == 1 and M != 1
                        else pl.BlockSpec((tm, tn), lambda i, j: (i, j)))
    outs = pl.pallas_call(
        body, grid=(M // tm, n // tn), in_specs=in_specs,
        out_specs=[pl.BlockSpec((tm, tn), lambda i, j: (i, j)) for _ in out_dtypes],
        out_shape=[jax.ShapeDtypeStruct((M, n), dt) for dt in out_dtypes],
        compiler_params=_params(("parallel", "parallel")), name=name)(a, b, *extras)
    return outs if len(out_dtypes) > 1 else outs[0]


def _matmul_tn(a, b, *, name, m, n, tm=512, tn=512, a_spec=None, b_spec=None, out_spec=None, out_struct=None, into=None,
               a_silu=False):
    T = a.shape[-2]
    tm, tn = min(tm, m), min(tn, n)
    assert m % tm == 0 and n % tn == 0, (name, m, n, tm, tn)
    if a_spec is None:
        a_spec = pl.BlockSpec((T, tm), lambda i, j: (0, i))
    if b_spec is None:
        b_spec = pl.BlockSpec((T, tn), lambda i, j: (0, j))
    if out_spec is None:
        out_spec, out_struct = pl.BlockSpec((tm, tn), lambda i, j: (i, j)), jax.ShapeDtypeStruct((m, n), F32)

    def body(a_ref, b_ref, *rest):
        av = a_ref[...]
        if a_silu:
            av = av * _sigmoid(av)
        rest[-1][...] = _dot_tn(av.astype(BF16), b_ref[...].astype(BF16)).astype(rest[-1].dtype)

    args, in_specs, alias = [a, b], [a_spec, b_spec], {}
    if into is not None:
        args, in_specs, alias = args + [into], in_specs + [ANY], {2: 0}
    return pl.pallas_call(body, grid=(m // tm, n // tn), in_specs=in_specs, out_specs=out_spec, out_shape=out_struct,
                          input_output_aliases=alias, compiler_params=_params(("parallel", "parallel")), name=name)(*args)


def _modnorm_fwd(x, nw, sc, sh, *, name):
    L = x.shape[0]
    tm = min(L, 512)

    def body(x_ref, nw_ref, sc_ref, sh_ref, h_ref):
        xv = x_ref[...]
        r = lax.rsqrt(jnp.mean(xv * xv, axis=-1, keepdims=True) + EPS)
        h_ref[...] = ((xv * r * nw_ref[...]) * (1.0 + sc_ref[...]) + sh_ref[...]).astype(BF16)

    row = pl.BlockSpec((tm, D), lambda i: (i, 0))
    vec = pl.BlockSpec((1, D), lambda i: (0, 0))
    return pl.pallas_call(body, grid=(L // tm,), in_specs=[row, vec, vec, vec], out_specs=row,
                          out_shape=jax.ShapeDtypeStruct((L, D), BF16),
                          compiler_params=_params(("parallel",)), name=name)(x, nw, sc, sh)


def _modnorm_bwd(x, dh, dxo, nw, sc, gsum, *, name):
    L = x.shape[0]
    tm = min(L, 256)

    def body(x_ref, dh_ref, dxo_ref, nw_ref, sc_ref, g_ref, dx_ref, s_ref):
        @pl.when(pl.program_id(0) == 0)
        def _():
            s_ref[...] = g_ref[...]

        xv, dhv = x_ref[...], dh_ref[...]
        r = lax.rsqrt(jnp.mean(xv * xv, axis=-1, keepdims=True) + EPS)
        xhat = xv * r
        dxhat = dhv * (nw_ref[...] * (1.0 + sc_ref[...]))
        dx_ref[...] = dxo_ref[...] + r * (dxhat - xhat * jnp.mean(dxhat * xhat, axis=-1, keepdims=True))
        s_ref[1:2, :] += jnp.sum(dhv * xhat, axis=0, keepdims=True) * (1.0 + sc_ref[...])
        s_ref[2:3, :] += jnp.sum(dhv * xhat, axis=0, keepdims=True) * nw_ref[...]
        s_ref[3:4, :] += jnp.sum(dhv, axis=0, keepdims=True)

    row = pl.BlockSpec((tm, D), lambda i: (i, 0))
    vec = pl.BlockSpec((1, D), lambda i: (0, 0))
    blk = pl.BlockSpec((8, D), lambda i: (0, 0))
    return pl.pallas_call(body, grid=(L // tm,), in_specs=[row, row, row, vec, vec, blk], out_specs=[row, blk],
                          out_shape=[jax.ShapeDtypeStruct((L, D), F32), jax.ShapeDtypeStruct((8, D), F32)],
                          compiler_params=_params(("arbitrary",)), name=name)(x, dh, dxo, nw, sc, gsum)


def _gate_bwd(dxo, y, g, *, name):
    L = dxo.shape[0]
    tm = min(L, 512)

    def body(dxo_ref, y_ref, g_ref, dy_ref, s_ref):
        @pl.when(pl.program_id(0) == 0)
        def _():
            s_ref[...] = jnp.zeros_like(s_ref)

        dv = dxo_ref[...]
        dy_ref[...] = (dv * g_ref[...]).astype(BF16)
        s_ref[0:1, :] += jnp.sum(dv * y_ref[...], axis=0, keepdims=True)

    row = pl.BlockSpec((tm, D), lambda i: (i, 0))
    return pl.pallas_call(body, grid=(L // tm,), in_specs=[row, row, pl.BlockSpec((1, D), lambda i: (0, 0))],
                          out_specs=[row, pl.BlockSpec((8, D), lambda i: (0, 0))],
                          out_shape=[jax.ShapeDtypeStruct((L, D), BF16), jax.ShapeDtypeStruct((8, D), F32)],
                          compiler_params=_params(("arbitrary",)), name=name)(dxo, y, g)


def _final_loss(x, fw, tgt, *, name):
    L = x.shape[0]
    tm = min(L, 256)

    def body(x_ref, fw_ref, t_ref, dx_ref, s_ref):
        @pl.when(pl.program_id(0) == 0)
        def _():
            s_ref[...] = jnp.zeros_like(s_ref)

        xv = x_ref[...]
        r = lax.rsqrt(jnp.mean(xv * xv, axis=-1, keepdims=True) + EPS)
        xhat = xv * r
        diff = xhat * fw_ref[...] - t_ref[...]
        dout = diff * (1.0 / D)
        dxhat = dout * fw_ref[...]
        dx_ref[...] = r * (dxhat - xhat * jnp.mean(dxhat * xhat, axis=-1, keepdims=True))
        s_ref[0:1, :] += jnp.sum(dout * xhat, axis=0, keepdims=True)
        s_ref[1:2, :] += jnp.zeros((1, D), F32) + 0.5 * jnp.sum(jnp.sum(diff * diff, axis=-1, keepdims=True) * (1.0 / D))

    row = pl.BlockSpec((tm, D), lambda i: (i, 0))
    return pl.pallas_call(body, grid=(L // tm,), in_specs=[row, pl.BlockSpec((1, D), lambda i: (0, 0)), row],
                          out_specs=[row, pl.BlockSpec((8, D), lambda i: (0, 0))],
                          out_shape=[jax.ShapeDtypeStruct((L, D), F32), jax.ShapeDtypeStruct((8, D), F32)],
                          compiler_params=_params(("arbitrary",)), name=name)(x, fw, tgt)


def _shift_down(v, j):
    if j == 0:
        return v
    row = lax.broadcasted_iota(jnp.int32, v.shape, 0)
    return jnp.where(row >= j, pltpu.roll(v, j, 0), 0.0)


def _shift_up(v, j):
    if j == 0:
        return v
    n = v.shape[0]
    row = lax.broadcasted_iota(jnp.int32, v.shape, 0)
    return jnp.where(row < n - j, pltpu.roll(v, n - j, 0), 0.0)


def _ssd_conv_fwd(zx, w, b, *, name):
    L = zx.shape[0]
    cb = 256
    k = w.shape[0]

    def body(x_ref, w_ref, b_ref, o_ref):
        xv = x_ref[...]
        pre = b_ref[...] + xv * w_ref[k - 1:k, :]
        for j in range(1, k):
            pre = pre + _shift_down(xv, j) * w_ref[k - 1 - j:k - j, :]
        o_ref[...] = pre * _sigmoid(pre)

    return pl.pallas_call(
        body, grid=(CONVD // cb,),
        in_specs=[pl.BlockSpec((L, cb), lambda i: (0, i + DI // cb)), pl.BlockSpec((k, cb), lambda i: (0, i)),
                  pl.BlockSpec((1, cb), lambda i: (0, i))],
        out_specs=pl.BlockSpec((L, cb), lambda i: (0, i)), out_shape=jax.ShapeDtypeStruct((L, CONVD), F32),
        compiler_params=_params(("parallel",)), name=name)(zx, w, b)


def _ssd_conv_bwd(zx, dact, w, b, dzx, *, name):
    L = zx.shape[0]
    cb = 256
    k = w.shape[0]

    def body(x_ref, da_ref, w_ref, b_ref, _, dx_ref, s_ref):
        xv = x_ref[...]
        sh = [_shift_down(xv, j) for j in range(k)]
        pre = b_ref[...] + sh[0] * w_ref[k - 1:k, :]
        for j in range(1, k):
            pre = pre + sh[j] * w_ref[k - 1 - j:k - j, :]
        s = _sigmoid(pre)
        dpre = da_ref[...] * (s * (1.0 + pre * (1.0 - s)))
        dx = dpre * w_ref[k - 1:k, :]
        for j in range(1, k):
            dx = dx + _shift_up(dpre, j) * w_ref[k - 1 - j:k - j, :]
        dx_ref[...] = dx.astype(BF16)
        s_ref[...] = jnp.zeros_like(s_ref)
        for j in range(k):
            s_ref[k - 1 - j:k - j, :] = jnp.sum(dpre * sh[j], axis=0, keepdims=True)
        s_ref[k:k + 1, :] = jnp.sum(dpre, axis=0, keepdims=True)

    return pl.pallas_call(
        body, grid=(CONVD // cb,),
        in_specs=[pl.BlockSpec((L, cb), lambda i: (0, i + DI // cb)), pl.BlockSpec((L, cb), lambda i: (0, i)),
                  pl.BlockSpec((k, cb), lambda i: (0, i)), pl.BlockSpec((1, cb), lambda i: (0, i)), ANY],
        out_specs=[pl.BlockSpec((L, cb), lambda i: (0, i + DI // cb)), pl.BlockSpec((8, cb), lambda i: (0, i))],
        out_shape=[jax.ShapeDtypeStruct((L, ZX), BF16), jax.ShapeDtypeStruct((8, CONVD), F32)],
        input_output_aliases={4: 0}, compiler_params=_params(("parallel",)), name=name)(zx, dact, w, b, dzx)


def _sc_fwd(proj, w, *, name):
    L = proj.shape[0]
    cb = 256
    nb = D // cb
    k = w.shape[0]

    def body(b_ref, c_ref, x_ref, w_ref, o_ref):
        u = c_ref[...] * x_ref[...]
        v = u * w_ref[k - 1:k, :]
        for j in range(1, k):
            v = v + _shift_down(u, j) * w_ref[k - 1 - j:k - j, :]
        o_ref[...] = (b_ref[...] * v).astype(BF16)

    return pl.pallas_call(
        body, grid=(nb,),
        in_specs=[pl.BlockSpec((L, cb), lambda i: (0, i)), pl.BlockSpec((L, cb), lambda i: (0, i + nb)),
                  pl.BlockSpec((L, cb), lambda i: (0, i + 2 * nb)), pl.BlockSpec((k, cb), lambda i: (0, i))],
        out_specs=pl.BlockSpec((L, cb), lambda i: (0, i)), out_shape=jax.ShapeDtypeStruct((L, D), BF16),
        compiler_params=_params(("parallel",)), name=name)(proj, proj, proj, w)


def _sc_bwd(proj, dyv, w, *, name):
    L = proj.shape[0]
    cb = 256
    nb = D // cb
    k = w.shape[0]

    def body(b_ref, c_ref, x_ref, dy_ref, w_ref, dp_ref, s_ref):
        cv, xv = c_ref[...], x_ref[...]
        u = cv * xv
        sh = [_shift_down(u, j) for j in range(k)]
        v = sh[0] * w_ref[k - 1:k, :]
        for j in range(1, k):
            v = v + sh[j] * w_ref[k - 1 - j:k - j, :]
        dyv_ = dy_ref[...]
        dp_ref[0] = (dyv_ * v).astype(BF16)
        dv = dyv_ * b_ref[...]
        du = dv * w_ref[k - 1:k, :]
        for j in range(1, k):
            du = du + _shift_up(dv, j) * w_ref[k - 1 - j:k - j, :]
        dp_ref[1] = (du * xv).astype(BF16)
        dp_ref[2] = (du * cv).astype(BF16)
        s_ref[...] = jnp.zeros_like(s_ref)
        for j in range(k):
            s_ref[k - 1 - j:k - j, :] = jnp.sum(dv * sh[j], axis=0, keepdims=True)

    blk = pl.BlockSpec((L, cb), lambda i: (0, i))
    return pl.pallas_call(
        body, grid=(nb,),
        in_specs=[blk, pl.BlockSpec((L, cb), lambda i: (0, i + nb)), pl.BlockSpec((L, cb), lambda i: (0, i + 2 * nb)),
                  blk, pl.BlockSpec((k, cb), lambda i: (0, i))],
        out_specs=[pl.BlockSpec((3, L, cb), lambda i: (0, 0, i)), pl.BlockSpec((8, cb), lambda i: (0, i))],
        out_shape=[jax.ShapeDtypeStruct((3, L, D), BF16), jax.ShapeDtypeStruct((8, D), F32)],
        compiler_params=_params(("parallel",)), name=name)(proj, proj, proj, dyv, w)


def _ssd_chunk_terms(dtr, prm):
    lane = lax.broadcasted_iota(jnp.int32, (CH, LANES), 1)
    valid = lane < NH
    xdt = dtr + prm[0:1, :]
    dt = jnp.where(valid, jnp.maximum(xdt, 0.0) + jnp.log1p(jnp.exp(-jnp.abs(xdt))), 0.0)
    A = -jnp.exp(prm[1:2, :])
    ri = lax.broadcasted_iota(jnp.int32, (CH, CH), 0)
    ci = lax.broadcasted_iota(jnp.int32, (CH, CH), 1)
    cs = _dot((ri >= ci).astype(F32), dt * A, precision=HIGHEST)
    last = cs[CH - 1:CH, :]
    ex = (lax.broadcasted_iota(jnp.int32, (LANES, DI), 1) // HP == lax.broadcasted_iota(jnp.int32, (LANES, DI), 0)).astype(F32)
    return dict(valid=valid, xdt=xdt, dt=dt, A=A, cs=cs, csT=cs.T, last=last, ri=ri, ci=ci, ex=ex)


def _expand(v, ex):
    if v.shape[0] == 1:
        return _dot(jnp.broadcast_to(v, (8, LANES)), ex, precision=HIGHEST)[0:1, :]
    return _dot(v, ex, precision=HIGHEST)


def _head_sum(v, ex):
    if v.shape[0] == 1:
        return _dot(jnp.broadcast_to(v, (8, DI)), ex, ((1,), (1,)), precision=HIGHEST)[0:1, :]
    return _dot(v, ex, ((1,), (1,)), precision=HIGHEST)


def _ssd_fwd(xbc, dtr, prm, *, name):
    L = xbc.shape[0]
    nc = L // CH

    def body(xbc_ref, dtr_ref, prm_ref, y_ref, sp_ref, st_ref):
        @pl.when(pl.program_id(0) == 0)
        def _():
            st_ref[...] = jnp.zeros_like(st_ref)

        prm_v = prm_ref[...]
        t = _ssd_chunk_terms(dtr_ref[...], prm_v)
        cs, csT, ex, causal = t["cs"], t["csT"], t["ex"], t["ri"] >= t["ci"]
        xs = xbc_ref[:, 0:DI]
        X = xs * _expand(t["dt"], ex)
        Xb = X.astype(BF16)
        Xd = (X * _expand(jnp.exp(t["last"] - cs), ex)).astype(BF16)
        Ex = _expand(jnp.exp(cs), ex)
        cdx = _expand(jnp.exp(t["last"]), ex)
        dskx = _expand(prm_v[2:3, :], ex)
        lane = lax.broadcasted_iota(jnp.int32, (CH, LANES), 1)
        sp_ref[0] = st_ref[...]
        for g in range(NG):
            Bg = xbc_ref[:, DI + g * NS:DI + (g + 1) * NS].astype(BF16)
            Cg = xbc_ref[:, DI + GW + g * NS:DI + GW + (g + 1) * NS].astype(BF16)
            G = _dot_nt(Cg, Bg)
            Sg = st_ref[:, g * GW:(g + 1) * GW]
            yoff = _dot(Cg, Sg.astype(BF16)) * Ex[:, g * GW:(g + 1) * GW]
            for j in range(GW // LANES):
                lo = g * GW + j * LANES
                Xp = Xb[:, lo:lo + LANES]
                yd = []
                for h in (lo // HP, lo // HP + 1):
                    seg = cs[:, h:h + 1] - csT[h:h + 1, :]
                    yd.append(_dot((G * jnp.where(causal, jnp.exp(seg), 0.0)).astype(BF16), Xp))
                y_ref[:, lo:lo + LANES] = (jnp.where(lane < HP, yd[0], yd[1]) + yoff[:, j * LANES:(j + 1) * LANES]
                                           + dskx[:, lo:lo + LANES] * xs[:, lo:lo + LANES])
            st_ref[:, g * GW:(g + 1) * GW] = Sg * cdx[:, g * GW:(g + 1) * GW] + _dot_tn(Bg, Xd[:, g * GW:(g + 1) * GW])

    return pl.pallas_call(
        body, grid=(nc,),
        in_specs=[pl.BlockSpec((CH, CONVD), lambda c: (c, 0)), pl.BlockSpec((CH, LANES), lambda c: (c, 0)),
                  pl.BlockSpec((8, LANES), lambda c: (0, 0))],
        out_specs=[pl.BlockSpec((CH, DI), lambda c: (c, 0)), pl.BlockSpec((1, NS, DI), lambda c: (c, 0, 0))],
        out_shape=[jax.ShapeDtypeStruct((L, DI), F32), jax.ShapeDtypeStruct((nc, NS, DI), F32)],
        scratch_shapes=[pltpu.VMEM((NS, DI), F32)],
        compiler_params=_params(("arbitrary",)), name=name)(xbc, dtr, prm)


def _ssd_bwd(xbc, dtr, prm, dy, sprev, *, name):
    L = xbc.shape[0]
    nc = L // CH

    def body(xbc_ref, dtr_ref, prm_ref, dy_ref, sp_ref, dxbc_ref, ddtr_ref, s_ref, dst_ref, dx_scr, de_scr, dd_scr):
        step = pl.program_id(0)

        @pl.when(step == 0)
        def _():
            dst_ref[...] = jnp.zeros_like(dst_ref)
            s_ref[...] = jnp.zeros_like(s_ref)

        prm_v = prm_ref[...]
        t = _ssd_chunk_terms(dtr_ref[...], prm_v)
        cs, csT, ex, ri, ci = t["cs"], t["csT"], t["ex"], t["ri"], t["ci"]
        E = jnp.exp(cs)
        dec = jnp.exp(t["last"] - cs)
        cd = jnp.exp(t["last"])
        xs = xbc_ref[:, 0:DI]
        dtx = _expand(t["dt"], ex)
        X = xs * dtx
        Xb = X.astype(BF16)
        decx = _expand(dec, ex)
        Xd = (X * decx).astype(BF16)
        Ex = _expand(E, ex)
        cdx = _expand(cd, ex)
        dskx = _expand(prm_v[2:3, :], ex)
        lane = lax.broadcasted_iota(jnp.int32, (CH, LANES), 1)
        dcs = jnp.zeros((CH, LANES), F32)
        dcd_x = []
        for g in range(NG):
            gs = slice(g * GW, (g + 1) * GW)
            Bg = xbc_ref[:, DI + g * NS:DI + (g + 1) * NS].astype(BF16)
            Cg = xbc_ref[:, DI + GW + g * NS:DI + GW + (g + 1) * NS].astype(BF16)
            G = _dot_nt(Cg, Bg)
            GT = _dot_nt(Bg, Cg)
            Sg = sp_ref[0, :, gs]
            Sgb = Sg.astype(BF16)
            dyg = dy_ref[:, gs]
            de_scr[:, gs] = dyg * _dot(Cg, Sgb)
            dYo = (Ex[:, gs] * dyg).astype(BF16)
            dC = _dot_nt(dYo, Sgb)
            dS_in = _dot_tn(Cg, dYo)
            dStg = dst_ref[:, gs]
            dStb = dStg.astype(BF16)
            dXd = _dot(Bg, dStb)
            dB = _dot_nt(Xd[:, gs], dStb)
            dd_scr[:, gs] = dXd * X[:, gs]
            dXst = dXd * decx[:, gs]
            dG = jnp.zeros((CH, CH), F32)
            dGT = jnp.zeros((CH, CH), F32)
            for j in range(GW // LANES):
                lo = g * GW + j * LANES
                Xp = Xb[:, lo:lo + LANES]
                dyp = dy_ref[:, lo:lo + LANES]
                dXp = dXst[:, j * LANES:(j + 1) * LANES]
                for k, h in enumerate((lo // HP, lo // HP + 1)):
                    dyh = jnp.where((lane < HP) if k == 0 else (lane >= HP), dyp, 0.0).astype(BF16)
                    seg = cs[:, h:h + 1] - csT[h:h + 1, :]
                    Lm = jnp.where(ri >= ci, jnp.exp(seg), 0.0)
                    LmT = jnp.where(ci >= ri, jnp.exp(-seg), 0.0)
                    dM = _dot_nt(dyh, Xp)
                    dMT = _dot_nt(Xp, dyh)
                    MT = GT * LmT
                    rs = jnp.sum(dM * (G * Lm), axis=1, keepdims=True) - jnp.sum(dMT * MT, axis=1, keepdims=True)
                    dcs = dcs + jnp.where(lane == h, rs, 0.0)
                    dG = dG + dM * Lm
                    dGT = dGT + dMT * LmT
                    dXp = dXp + _dot(MT.astype(BF16), dyh)
                dx_scr[:, lo:lo + LANES] = dXp
            dxbc_ref[:, DI + g * NS:DI + (g + 1) * NS] = dB + _dot(dGT.astype(BF16), Cg)
            dxbc_ref[:, DI + GW + g * NS:DI + GW + (g + 1) * NS] = dC + _dot(dG.astype(BF16), Bg)
            dcd_x.append(jnp.sum(dStg * Sg, axis=0, keepdims=True))
            dst_ref[:, gs] = dStg * cdx[:, gs] + dS_in
        dX = dx_scr[...]
        dy = dy_ref[...]
        ddec = _head_sum(dd_scr[...], ex)
        dcd = _head_sum(jnp.concatenate(dcd_x, axis=1), ex)
        dcs = dcs + _head_sum(de_scr[...], ex) * E - ddec * dec
        row = lax.broadcasted_iota(jnp.int32, (CH, LANES), 0)
        dcs = dcs + jnp.where(row == CH - 1, jnp.sum(ddec * dec, axis=0, keepdims=True) + dcd * cd, 0.0)
        da = _dot((ci >= ri).astype(F32), dcs, precision=HIGHEST)
        ddt = da * t["A"] + _head_sum(dX * xs, ex)
        ddtr = jnp.where(t["valid"], ddt * _sigmoid(t["xdt"]), 0.0)
        ddtr_ref[...] = ddtr
        dxbc_ref[:, 0:DI] = dX * dtx + dskx * dy
        s_ref[0:1, :] += jnp.sum(da * t["dt"], axis=0, keepdims=True)
        s_ref[1:2, :] += _head_sum(jnp.sum(dy * xs, axis=0, keepdims=True), ex)
        s_ref[2:3, :] += jnp.sum(ddtr, axis=0, keepdims=True)

        @pl.when(step == nc - 1)
        def _():
            s_ref[0:1, :] = s_ref[0:1, :] * t["A"]

    rev = lambda c: (nc - 1 - c, 0)
    return pl.pallas_call(
        body, grid=(nc,),
        in_specs=[pl.BlockSpec((CH, CONVD), rev), pl.BlockSpec((CH, LANES), rev), pl.BlockSpec((8, LANES), lambda c: (0, 0)),
                  pl.BlockSpec((CH, DI), rev), pl.BlockSpec((1, NS, DI), lambda c: (nc - 1 - c, 0, 0))],
        out_specs=[pl.BlockSpec((CH, CONVD), rev), pl.BlockSpec((CH, LANES), rev), pl.BlockSpec((8, LANES), lambda c: (0, 0))],
        out_shape=[jax.ShapeDtypeStruct((L, CONVD), F32), jax.ShapeDtypeStruct((L, LANES), F32),
                   jax.ShapeDtypeStruct((8, LANES), F32)],
        scratch_shapes=[pltpu.VMEM((NS, DI), F32), pltpu.VMEM((CH, DI), F32), pltpu.VMEM((CH, DI), F32),
                        pltpu.VMEM((CH, DI), F32)],
        compiler_params=_params(("arbitrary",)), name=name)(xbc, dtr, prm, dy, sprev)


def _gnorm_fwd(y, zx, nw, *, name):
    L = y.shape[0]
    tm = min(L, 256)

    def body(y_ref, z_ref, nw_ref, o_ref):
        z = z_ref[...]
        yg = y_ref[...] * (z * _sigmoid(z))
        for g in range(NG):
            v = yg[:, g * GW:(g + 1) * GW]
            r = lax.rsqrt(jnp.mean(v * v, axis=-1, keepdims=True) + EPS)
            o_ref[:, g * GW:(g + 1) * GW] = (v * r * nw_ref[:, g * GW:(g + 1) * GW]).astype(BF16)

    row = pl.BlockSpec((tm, DI), lambda i: (i, 0))
    return pl.pallas_call(body, grid=(L // tm,), in_specs=[row, row, pl.BlockSpec((1, DI), lambda i: (0, 0))],
                          out_specs=row, out_shape=jax.ShapeDtypeStruct((L, DI), BF16),
                          compiler_params=_params(("parallel",)), name=name)(y, zx, nw)


def _gnorm_bwd(y, zx, nw, dyn, *, name):
    L = y.shape[0]
    tm = min(L, 256)

    def body(y_ref, z_ref, nw_ref, dyn_ref, dy_ref, dz_ref, s_ref):
        @pl.when(pl.program_id(0) == 0)
        def _():
            s_ref[...] = jnp.zeros_like(s_ref)

        z, yv = z_ref[...], y_ref[...]
        sz = _sigmoid(z)
        gate = z * sz
        dgate_dz = sz * (1.0 + z * (1.0 - sz))
        for g in range(NG):
            gs = slice(g * GW, (g + 1) * GW)
            v = yv[:, gs] * gate[:, gs]
            r = lax.rsqrt(jnp.mean(v * v, axis=-1, keepdims=True) + EPS)
            vhat = v * r
            dn = dyn_ref[:, gs]
            s_ref[0:1, gs] += jnp.sum(dn * vhat, axis=0, keepdims=True)
            dvhat = dn * nw_ref[:, gs]
            dv = r * (dvhat - vhat * jnp.mean(dvhat * vhat, axis=-1, keepdims=True))
            dy_ref[:, gs] = dv * gate[:, gs]
            dz_ref[:, gs] = (dv * yv[:, gs] * dgate_dz[:, gs]).astype(BF16)

    row = pl.BlockSpec((tm, DI), lambda i: (i, 0))
    return pl.pallas_call(body, grid=(L // tm,), in_specs=[row, row, pl.BlockSpec((1, DI), lambda i: (0, 0)), row],
                          out_specs=[row, row, pl.BlockSpec((8, DI), lambda i: (0, 0))],
                          out_shape=[jax.ShapeDtypeStruct((L, DI), F32), jax.ShapeDtypeStruct((L, ZX), BF16),
                                     jax.ShapeDtypeStruct((8, DI), F32)],
                          compiler_params=_params(("arbitrary",)), name=name)(y, zx, nw, dyn)


def _adamw(w, g, m, v, *, name, g_row=0, emit_g=False):
    R, C = w.shape
    tr = R
    while tr * C > 256 * 1024 and tr % 16 == 0:
        tr //= 2
    assert g_row % tr == 0, (name, g_row, tr)

    def body(w_ref, g_ref, m_ref, v_ref, *outs):
        gv = g_ref[...]
        mn = ADAM_B1 * m_ref[...] + (1.0 - ADAM_B1) * gv
        vn = ADAM_B2 * v_ref[...] + (1.0 - ADAM_B2) * (gv * gv)
        m_hat = mn / (1.0 - ADAM_B1 ** ADAM_STEP)
        v_hat = vn / (1.0 - ADAM_B2 ** ADAM_STEP)
        d_ref, mo_ref, vo_ref = outs[-3:]
        d_ref[...] = -ADAM_LR * (m_hat / (jnp.sqrt(v_hat) + ADAM_EPS) + ADAM_WD * w_ref[...])
        mo_ref[...] = mn
        vo_ref[...] = vn
        if emit_g:
            outs[0][...] = gv

    blk = pl.BlockSpec((tr, C), lambda i: (i, 0))
    n_out = 4 if emit_g else 3
    return pl.pallas_call(body, grid=(R // tr,),
                          in_specs=[blk, pl.BlockSpec((tr, C), lambda i: (i + g_row // tr, 0)), blk, blk],
                          out_specs=[blk] * n_out, out_shape=[jax.ShapeDtypeStruct((R, C), F32)] * n_out,
                          compiler_params=_params(("parallel",)), name=name)(w, g, m, v)


def _residual(acc, xv, gv):
    return xv + gv * acc, acc


def _relu2(acc):
    a = jnp.maximum(acc, 0.0)
    return a, a * a


def _w1_struct():
    return jax.ShapeDtypeStruct((N_CHIPS, W_ROWS, D), BF16)


def _mlp_fwd(x, mod, nw, w1, i, tag):
    sh, sc, g = mod
    h = _modnorm_fwd(x, nw, sc, sh, name=tag + "_norm")
    a, act = _matmul(h, w1, n=DFF, b_spec=pl.BlockSpec((None, D, 512), lambda mi, j: (j // 2, UP_ROW // D + i, j % 2)),
                     epi=_relu2, out_dtypes=(BF16, BF16), name=tag + "_up")
    xn, y = _matmul(act, w1, n=D, contract=_nn_split,
                    b_spec=pl.BlockSpec((N_CHIPS, D, 512), lambda mi, j: (0, DOWN_ROW // D + i, j)),
                    extras=(x, g), epi=_residual, out_dtypes=(F32, F32), name=tag + "_down")
    return xn, (x, h, a, act, y)


def _mlp_bwd(dxo, saved, mod, nw, w1, g1, i, tag):
    x, h, a, act, y = saved
    sh, sc, g = mod
    dy, gsum = _gate_bwd(dxo, y, g, name=tag + "_dgate")
    du = _matmul(dy, w1, n=DFF, contract=_nt,
                 b_spec=pl.BlockSpec((None, 512, D), lambda mi, j: (j // 2, (DOWN_ROW + i * D) // 512 + j % 2, 0)),
                 extras=(a,), epi=lambda acc, av: (acc * (2.0 * av.astype(F32)),), out_dtypes=(BF16,), name=tag + "_dact")
    g1 = _matmul_tn(act, dy, m=DFF, n=D, into=g1, out_struct=_w1_struct(),
                    out_spec=pl.BlockSpec((None, 512, 512), lambda mi, j: (mi // 2, (DOWN_ROW + i * D) // 512 + mi % 2, j)),
                    name=tag + "_ddown")
    dh = _matmul(du, w1, n=D, contract=_nt_split,
                 b_spec=pl.BlockSpec((N_CHIPS, 512, D), lambda mi, j: (0, (UP_ROW + i * D) // 512 + j, 0)), name=tag + "_dh")
    g1 = _matmul_tn(h, du, m=D, n=DFF, into=g1, out_struct=_w1_struct(),
                    out_spec=pl.BlockSpec((None, 512, 512), lambda mi, j: (j // 2, (UP_ROW + i * D) // 512 + mi, j % 2)),
                    name=tag + "_dup")
    dx, sums = _modnorm_bwd(x, dh, dxo, nw, sc, gsum, name=tag + "_dnorm")
    return dx, g1, sums


def _ssd_layer_fwd(x, mod, nw, w_zx, w_dt, conv_w, conv_b, prm, gn_w, w1, tag):
    sh, sc, g = mod
    h = _modnorm_fwd(x, nw, sc, sh, name=tag + "_norm")
    zx = _matmul(h, w_zx, n=ZX, name=tag + "_in")
    dtr = _matmul(h, w_dt, n=LANES, name=tag + "_in_dt")
    xbc = _ssd_conv_fwd(zx, conv_w, conv_b, name=tag + "_conv")
    y, sprev = _ssd_fwd(xbc, dtr, prm, name=tag + "_scan")
    yn = _gnorm_fwd(y, zx, gn_w, name=tag + "_gnorm")
    xn, yo = _matmul(yn, w1, n=D, contract=_nn_split,
                     b_spec=pl.BlockSpec((N_CHIPS, 512, 512), lambda mi, j: (0, SSD_OUT_ROW // 512, j)),
                     extras=(x, g), epi=_residual, out_dtypes=(F32, F32), name=tag + "_out")
    return xn, (x, h, zx, dtr, xbc, y, sprev, yn, yo)


def _ssd_layer_bwd(dxo, saved, mod, nw, w_zx, w_dt, conv_w, conv_b, prm, gn_w, w1, g1, tag):
    x, h, zx, dtr, xbc, y, sprev, yn, yo = saved
    sh, sc, g = mod
    dyo, gsum = _gate_bwd(dxo, yo, g, name=tag + "_dgate")
    dyn = _matmul(dyo, w1, n=DI, contract=_nt, b_spec=pl.BlockSpec((None, 512, D), lambda mi, j: (j, SSD_OUT_ROW // 512, 0)),
                  name=tag + "_dyn")
    g1 = _matmul_tn(yn, dyo, m=DI, n=D, into=g1, out_struct=_w1_struct(),
                    out_spec=pl.BlockSpec((None, 512, 512), lambda mi, j: (mi, SSD_OUT_ROW // 512, j)), name=tag + "_dout")
    dy, dzx, gnsum = _gnorm_bwd(y, zx, gn_w, dyn, name=tag + "_dgnorm")
    dxbc, ddtr, ssum = _ssd_bwd(xbc, dtr, prm, dy, sprev, name=tag + "_dscan")
    dzx, csum = _ssd_conv_bwd(zx, dxbc, conv_w, conv_b, dzx, name=tag + "_dconv")
    dh_dt = _matmul(ddtr, w_dt, n=D, contract=_nt, name=tag + "_dh_dt")
    dh = _matmul(dzx, w_zx, n=D, contract=_nt, extras=(dh_dt,), epi=lambda acc, e: (acc + e,), name=tag + "_dh")
    d_w_zx = _matmul_tn(h, dzx, m=D, n=ZX, name=tag + "_din")
    d_w_dt = _matmul_tn(h, ddtr, m=D, n=LANES, name=tag + "_din_dt")
    dx, sums = _modnorm_bwd(x, dh, dxo, nw, sc, gsum, name=tag + "_dnorm")
    return dx, g1, d_w_zx, d_w_dt, sums, csum, gnsum, ssum


def _sc_layer_fwd(x, mod, nw, w_sc_in, conv_w, w1, tag):
    sh, sc, g = mod
    h = _modnorm_fwd(x, nw, sc, sh, name=tag + "_norm")
    proj = _matmul(h, w_sc_in, n=3 * D, tn=256, b_spec=pl.BlockSpec((None, D, 256), lambda mi, j: (j // 3, 0, j % 3)),
                   name=tag + "_in")
    yv = _sc_fwd(proj, conv_w, name=tag + "_conv")
    xn, yo = _matmul(yv, w1, n=D, contract=_nn_split,
                     b_spec=pl.BlockSpec((N_CHIPS, 256, 512), lambda mi, j: (0, SC_OUT_ROW // 256, j)),
                     extras=(x, g), epi=_residual, out_dtypes=(F32, F32), name=tag + "_out")
    return xn, (x, h, proj, yv, yo)


def _sc_layer_bwd(dxo, saved, mod, nw, w_sc_in, conv_w, w1, g1, tag):
    x, h, proj, yv, yo = saved
    sh, sc, g = mod
    L = x.shape[0]
    dyo, gsum = _gate_bwd(dxo, yo, g, name=tag + "_dgate")
    dyv = _matmul(dyo, w1, n=D, tn=256, contract=_nt,
                  b_spec=pl.BlockSpec((None, 256, D), lambda mi, j: (j, SC_OUT_ROW // 256, 0)), name=tag + "_dyv")
    g1 = _matmul_tn(yv, dyo, m=D, n=D, tm=256, into=g1, out_struct=_w1_struct(),
                    out_spec=pl.BlockSpec((None, 256, 512), lambda mi, j: (mi, SC_OUT_ROW // 256, j)), name=tag + "_dout")
    dproj, csum = _sc_bwd(proj, dyv, conv_w, name=tag + "_dconv")
    tm = min(L, 512)
    dh = _matmul(dproj, w_sc_in, n=D, contract=_nt_sc_in, a_spec=pl.BlockSpec((3, tm, D), lambda mi, j: (0, mi, 0)),
                 b_spec=pl.BlockSpec((N_CHIPS, 512, SC_IN_SHARD), lambda mi, j: (0, j, 0)), name=tag + "_dh")
    g_sc_in = _matmul_tn(h, dproj, m=D, n=3 * D, tn=256, b_spec=pl.BlockSpec((None, L, 256), lambda mi, j: (j // 4, 0, j % 4)),
                         out_spec=pl.BlockSpec((None, 512, 256), lambda mi, j: (j // 3, mi, j % 3)),
                         out_struct=jax.ShapeDtypeStruct((N_CHIPS, D, SC_IN_SHARD), BF16), name=tag + "_din")
    dx, sums = _modnorm_bwd(x, dh, dxo, nw, sc, gsum, name=tag + "_dnorm")
    return dx, g1, g_sc_in, sums, csum


SUB_ROW = (0, 8, 16, 24)
SSD_CONV_ROW, GNORM_ROW, FINAL_ROW, SC_CONV_ROW, HEAD_ROW, SMALL_ROWS = 32, 56, 72, 80, 88, 96


def _local_step(x, tgt, mod, p):
    m = [[mod[i, j:j + 1] for j in range(6)] for i in range(2)]
    row = lambda v: v.reshape(1, -1)
    w1, w_sc_in = p["w1"], p["w_sc_in"]
    prm = jnp.pad(jnp.concatenate([p["ssd_dt_bias"], p["ssd_A_log"], p["ssd_D"], jnp.zeros((5, NH), F32)], axis=0),
                  ((0, 0), (0, LANES - NH)))
    ssd_args = (p["w_zx"], p["w_dt"], p["ssd_conv_w"][0], p["ssd_conv_b"], prm, p["ssd_norm_w"], w1)
    sc_args = (w_sc_in, p["sc_conv_w"][0], w1)
    mix_nw = [row(p["mix_norm_w"][i]) for i in range(2)]
    mlp_nw = [row(p["mlp_norm_w"][i]) for i in range(2)]

    x1, s_ssd = _ssd_layer_fwd(x, m[0][0:3], mix_nw[0], *ssd_args, tag="ssd")
    x2, s_mlp0 = _mlp_fwd(x1, m[0][3:6], mlp_nw[0], w1, 0, "mlp0")
    x3, s_sc = _sc_layer_fwd(x2, m[1][0:3], mix_nw[1], *sc_args, tag="sc")
    x4, s_mlp1 = _mlp_fwd(x3, m[1][3:6], mlp_nw[1], w1, 1, "mlp1")
    dx4, fsum = _final_loss(x4, row(p["final_norm_w"]), tgt, name="final_loss")
    dx3, g1, sum_mlp1 = _mlp_bwd(dx4, s_mlp1, m[1][3:6], mlp_nw[1], w1, None, 1, "mlp1")
    dx2, g1, g_sc_in, sum_sc, sc_csum = _sc_layer_bwd(dx3, s_sc, m[1][0:3], mix_nw[1], *sc_args, g1, tag="sc")
    dx1, g1, sum_mlp0 = _mlp_bwd(dx2, s_mlp0, m[0][3:6], mlp_nw[0], w1, g1, 0, "mlp0")
    dx0, g1, d_w_zx, d_w_dt, sum_ssd, csum, gnsum, ssum = _ssd_layer_bwd(dx1, s_ssd, m[0][0:3], mix_nw[0], *ssd_args, g1,
                                                                         tag="ssd")
    small = jnp.concatenate([sum_ssd, sum_mlp0, sum_sc, sum_mlp1, csum.reshape(24, D), gnsum.reshape(16, D), fsum, sc_csum,
                             jnp.pad(ssum, ((0, 0), (0, D - LANES)))], axis=0)
    return dx0, g1, g_sc_in, d_w_zx, d_w_dt, small


def _all_gather_rows(blk, *, name):
    m_per, n = blk.shape

    def body(x_ref, out_ref, send_sems, recv_sems, local_sem):
        x, y, c = lax.axis_index("x"), lax.axis_index("y"), lax.axis_index("c")
        me, sibling = (x, y, c), (x, y, 1 - c)
        chips = [(1 - x, y), (x, 1 - y), (1 - x, 1 - y)]

        def rows(px, py, pc):
            return out_ref.at[pl.ds((4 * px + 2 * py + pc) * m_per, m_per), :]

        def copy(k, block, to, src=None):
            return pltpu.make_async_remote_copy(src_ref=rows(*block) if src is None else src, dst_ref=rows(*block),
                                                send_sem=send_sems.at[k], recv_sem=recv_sems.at[k], device_id=to,
                                                device_id_type=MESH)

        mine = pltpu.make_async_copy(x_ref, rows(*me), local_sem)
        mine.start()
        first = [copy(0, me, sibling, src=x_ref)] + [copy(1 + j, me, (*chip, c), src=x_ref) for j, chip in enumerate(chips)]
        for cp in first:
            cp.start()
        passed = [copy(4 + j, (*chip, c), sibling) for j, chip in enumerate(chips)]
        for j, chip in enumerate(chips):
            copy(1 + j, (*chip, c), me).wait_recv()
            passed[j].start()
        copy(0, sibling, me).wait_recv()
        for j, chip in enumerate(chips):
            copy(4 + j, (*chip, 1 - c), me).wait_recv()
        for cp in first + passed:
            cp.wait_send()
        mine.wait()

    return pl.pallas_call(
        body, out_shape=jax.ShapeDtypeStruct((N_DEV * m_per, n), blk.dtype),
        in_specs=[pl.BlockSpec(memory_space=pltpu.VMEM)], out_specs=pl.BlockSpec(memory_space=pltpu.VMEM),
        scratch_shapes=[pltpu.SemaphoreType.DMA((7,)), pltpu.SemaphoreType.DMA((7,)), pltpu.SemaphoreType.DMA],
        name=name)(blk)


def _half(ref, chip, c):
    hr = ref.shape[1] // 2
    return ref.at[chip, pl.ds(c * hr, hr), :]


def _gather_stacked(bufs, *, name):
    nb = len(bufs)

    def body(*refs):
        outs, send_sems, recv_sems = refs[nb:2 * nb], refs[2 * nb], refs[2 * nb + 1]
        x, y, c = lax.axis_index("x"), lax.axis_index("y"), lax.axis_index("c")
        sibling = (x, y, 1 - c)
        chips = [(1 - x, y), (x, 1 - y), (1 - x, 1 - y)]

        def copy(b, k, chip, pc, to):
            piece = _half(outs[b], 2 * chip[0] + chip[1], pc)
            return pltpu.make_async_remote_copy(src_ref=piece, dst_ref=piece, send_sem=send_sems.at[7 * b + k],
                                                recv_sem=recv_sems.at[7 * b + k], device_id=to, device_id_type=MESH)

        first = []
        for b in range(nb):
            first += [copy(b, 0, (x, y), c, sibling)] + [copy(b, 1 + j, (x, y), c, (*chip, c)) for j, chip in enumerate(chips)]
        for cp in first:
            cp.start()
        passed = []
        for j, chip in enumerate(chips):
            for b in range(nb):
                copy(b, 1 + j, chip, c, (x, y, c)).wait_recv()
                passed.append(copy(b, 4 + j, chip, c, sibling))
                passed[-1].start()
        for b in range(nb):
            copy(b, 0, (x, y), 1 - c, (x, y, c)).wait_recv()
            for j, chip in enumerate(chips):
                copy(b, 4 + j, chip, 1 - c, (x, y, c)).wait_recv()
        for cp in first + passed:
            cp.wait_send()

    return pl.pallas_call(
        body, out_shape=[jax.ShapeDtypeStruct(b.shape, b.dtype) for b in bufs], in_specs=[ANY] * nb, out_specs=[ANY] * nb,
        scratch_shapes=[pltpu.SemaphoreType.DMA((7 * nb,)), pltpu.SemaphoreType.DMA((7 * nb,))],
        input_output_aliases={b: b for b in range(nb)}, name=name)(*bufs)


def _swap_halves_with_sibling(bufs, *, name):
    nb = len(bufs)

    def body(*refs):
        ins, outs, send_sems, recv_sems = refs[:nb], refs[nb:2 * nb], refs[2 * nb], refs[2 * nb + 1]
        x, y, c = lax.axis_index("x"), lax.axis_index("y"), lax.axis_index("c")
        copies = []
        for b in range(nb):
            hr = ins[b].shape[1] // 2
            copies.append(pltpu.make_async_remote_copy(
                src_ref=ins[b].at[:, pl.ds((1 - c) * hr, hr), :], dst_ref=outs[b], send_sem=send_sems.at[b],
                recv_sem=recv_sems.at[b], device_id=(x, y, 1 - c), device_id_type=MESH))
        for cp in copies:
            cp.start()
        for cp in copies:
            cp.wait()

    return pl.pallas_call(
        body, out_shape=[jax.ShapeDtypeStruct((b.shape[0], b.shape[1] // 2, b.shape[2]), b.dtype) for b in bufs],
        in_specs=[ANY] * nb, out_specs=[ANY] * nb,
        scratch_shapes=[pltpu.SemaphoreType.DMA((nb,)), pltpu.SemaphoreType.DMA((nb,))], name=name)(*bufs)


def _send_to_owners(hs, *, name):
    nb = len(hs)

    def body(*refs):
        ins, outs, send_sems, recv_sems = refs[:nb], refs[nb:2 * nb], refs[2 * nb], refs[2 * nb + 1]
        x, y, c = lax.axis_index("x"), lax.axis_index("y"), lax.axis_index("c")
        chips = [(1 - x, y), (x, 1 - y), (1 - x, 1 - y)]
        copies = [pltpu.make_async_remote_copy(src_ref=ins[b].at[2 * cx + cy], dst_ref=outs[b].at[j],
                                               send_sem=send_sems.at[3 * b + j], recv_sem=recv_sems.at[3 * b + j],
                                               device_id=(cx, cy, c), device_id_type=MESH)
                  for b in range(nb) for j, (cx, cy) in enumerate(chips)]
        for cp in copies:
            cp.start()
        for cp in copies:
            cp.wait()

    return pl.pallas_call(
        body, out_shape=[jax.ShapeDtypeStruct((3,) + h.shape[1:], h.dtype) for h in hs], in_specs=[ANY] * nb,
        out_specs=[ANY] * nb, scratch_shapes=[pltpu.SemaphoreType.DMA((3 * nb,)), pltpu.SemaphoreType.DMA((3 * nb,))],
        name=name)(*hs)


def _swap_results_with_sibling(ts, *, name):
    nb = len(ts)

    def body(*refs):
        outs, send_sems, recv_sems = refs[nb:2 * nb], refs[2 * nb], refs[2 * nb + 1]
        x, y, c = lax.axis_index("x"), lax.axis_index("y"), lax.axis_index("c")
        copies = [pltpu.make_async_remote_copy(src_ref=outs[b].at[c], dst_ref=outs[b].at[c], send_sem=send_sems.at[b],
                                               recv_sem=recv_sems.at[b], device_id=(x, y, 1 - c), device_id_type=MESH)
                  for b in range(nb)]
        for cp in copies:
            cp.start()
        for cp in copies:
            cp.wait()

    return pl.pallas_call(
        body, out_shape=[jax.ShapeDtypeStruct(t.shape, t.dtype) for t in ts], in_specs=[ANY] * nb, out_specs=[ANY] * nb,
        scratch_shapes=[pltpu.SemaphoreType.DMA((nb,)), pltpu.SemaphoreType.DMA((nb,))],
        input_output_aliases={b: b for b in range(nb)}, name=name)(*ts)


def _row_tile(rows, cols):
    best = 16
    for t in range(16, rows + 1, 16):
        if rows % t == 0 and t * cols <= 640 * 1024:
            best = t
    assert rows % best == 0, (rows, cols)
    return best


def _add_sibling_half(g, recv, core, *, name):
    nk, r, n = g.shape
    hr = r // 2
    tr = _row_tile(hr, n)

    def body(c_ref, a_ref, b_ref, o_ref):
        o_ref[...] = (a_ref[...].astype(F32) + b_ref[...].astype(F32)).astype(BF16)

    grid_spec = pltpu.PrefetchScalarGridSpec(
        num_scalar_prefetch=1, grid=(nk, hr // tr),
        in_specs=[pl.BlockSpec((None, tr, n), lambda k, i, c_ref: (k, c_ref[0] * (hr // tr) + i, 0)),
                  pl.BlockSpec((None, tr, n), lambda k, i, c_ref: (k, i, 0))],
        out_specs=pl.BlockSpec((None, tr, n), lambda k, i, c_ref: (k, i, 0)))
    return pl.pallas_call(body, grid_spec=grid_spec, out_shape=jax.ShapeDtypeStruct((nk, hr, n), BF16),
                          compiler_params=_params(("parallel", "parallel")), name=name)(core, g, recv)


def _add_chip_sums(h, recv, chip_core, *, name):
    _, hr, n = h.shape
    tr = _row_tile(hr, n)

    def body(k_ref, a_ref, b_ref, o_ref):
        o_ref[...] = ((a_ref[...].astype(F32) + b_ref[0].astype(F32)) + b_ref[1].astype(F32)) + b_ref[2].astype(F32)

    grid_spec = pltpu.PrefetchScalarGridSpec(
        num_scalar_prefetch=1, grid=(hr // tr,),
        in_specs=[pl.BlockSpec((None, tr, n), lambda i, k_ref: (k_ref[0], i, 0)),
                  pl.BlockSpec((3, tr, n), lambda i, k_ref: (0, i, 0))],
        out_specs=pl.BlockSpec((None, tr, n), lambda i, k_ref: (k_ref[1], i, 0)))
    return pl.pallas_call(body, grid_spec=grid_spec, out_shape=jax.ShapeDtypeStruct((2, hr, n), F32),
                          compiler_params=_params(("parallel",)), name=name)(chip_core, h, recv)


def _sum_devices(g, *, name):
    nd, r, n = g.shape

    def body(g_ref, o_ref):
        acc = g_ref[0]
        for i in range(1, nd):
            acc = acc + g_ref[i]
        o_ref[...] = acc

    return pl.pallas_call(body, out_shape=jax.ShapeDtypeStruct((r, n), F32), name=name)(g)


def _own_slot(shard, chip):
    return lax.dynamic_update_slice(jnp.zeros((N_CHIPS,) + shard.shape, BF16), shard[None], (chip, 0, 0))


def kernel(x, c, ada_w, ada_b, mix_norm_w, mlp_norm_w, mlp_up, mlp_down, ssd_in_w, ssd_conv_w, ssd_conv_b, ssd_dt_bias, ssd_A_log, ssd_D, ssd_norm_w, ssd_out_w, sc_in_w, sc_conv_w, sc_out_w, final_norm_w, loss_target, m_ada_w, m_ada_b, m_mix_norm_w, m_mlp_norm_w, m_mlp_up, m_mlp_down, m_ssd_in_w, m_ssd_conv_w, m_ssd_conv_b, m_ssd_dt_bias, m_ssd_A_log, m_ssd_D, m_ssd_norm_w, m_ssd_out_w, m_sc_in_w, m_sc_conv_w, m_sc_out_w, m_final_norm_w, v_ada_w, v_ada_b, v_mix_norm_w, v_mlp_norm_w, v_mlp_up, v_mlp_down, v_ssd_in_w, v_ssd_conv_w, v_ssd_conv_b, v_ssd_dt_bias, v_ssd_A_log, v_ssd_D, v_ssd_norm_w, v_ssd_out_w, v_sc_in_w, v_sc_conv_w, v_sc_out_w, v_final_norm_w):
    xi, yi, ci = lax.axis_index("x"), lax.axis_index("y"), lax.axis_index("c")
    chip = 2 * xi + yi
    dev = 2 * chip + ci
    n_ada = ada_w.shape[2]

    rows1 = jnp.concatenate([mlp_up.reshape(-1, D), mlp_down.reshape(-1, D), ssd_out_w[0], sc_out_w[0]], axis=0).astype(BF16)
    w1, w_ssd_in, w_sc_in = _gather_stacked(
        [_own_slot(rows1, chip), _own_slot(ssd_in_w[0].astype(BF16), chip), _own_slot(sc_in_w[0].astype(BF16), chip)],
        name="gather_weights")
    ssd_in_full = jnp.moveaxis(w_ssd_in, 0, 1).reshape(D, N_CHIPS * SSD_IN_SHARD)
    p = dict(w1=w1, w_sc_in=w_sc_in, w_zx=ssd_in_full[:, :ZX], w_dt=jnp.pad(ssd_in_full[:, ZX:], ((0, 0), (0, LANES - NH))),
             mix_norm_w=mix_norm_w, mlp_norm_w=mlp_norm_w, ssd_conv_b=ssd_conv_b, ssd_dt_bias=ssd_dt_bias,
             ssd_A_log=ssd_A_log, ssd_D=ssd_D, ssd_norm_w=ssd_norm_w, final_norm_w=final_norm_w)

    conv_flat = jnp.concatenate([ssd_conv_w.reshape(-1), sc_conv_w.reshape(-1), jnp.zeros((256,), F32)]).reshape(4, D)
    blk0 = jnp.concatenate([c, conv_flat, jnp.zeros((3, D), F32)], axis=0)
    got0 = _all_gather_rows(blk0, name="gather_cond").reshape(N_DEV, 8, D)
    c_all = got0[:, 0]
    conv_all = got0[0::2, 1:5].reshape(N_CHIPS, 4 * D)
    p["ssd_conv_w"] = jnp.moveaxis(conv_all[:, :4 * 768].reshape(N_CHIPS, 4, 768), 0, 1).reshape(1, 4, CONVD)
    p["sc_conv_w"] = jnp.moveaxis(conv_all[:, 4 * 768:4 * 768 + 3 * 256].reshape(N_CHIPS, 3, 256), 0, 1).reshape(1, 3, D)
    mod_shard = [_matmul(c_all, ada_w[i], n=n_ada, a_silu=True,
                         extras=(lax.dynamic_slice(ada_b, (i, chip * n_ada), (1, n_ada)),),
                         epi=lambda acc, b: (acc + b,), name=f"ada_mod{i}") for i in range(2)]
    mod_all = _all_gather_rows(jnp.concatenate(mod_shard, axis=0), name="gather_mod")
    mod_all = mod_all.reshape(N_DEV, 2, N_DEV, n_ada)[0::2]
    mod = jnp.moveaxis(lax.dynamic_index_in_dim(mod_all, dev, axis=2, keepdims=False), 0, 1).reshape(2, 6, D)

    grad_x, g1, g_sc_in, d_w_zx, d_w_dt, small = _local_step(x[0], loss_target[0], mod, p)

    small_all = _all_gather_rows(small, name="gather_small").reshape(N_DEV, SMALL_ROWS, D)
    tot = _sum_devices(small_all, name="sum_small")
    loss = tot[FINAL_ROW + 1, 0]
    mod_rows = [r + o for r in SUB_ROW for o in (3, 2, 0)]
    g_ada_b = jnp.stack([tot[r] for r in mod_rows]).reshape(2, 6 * D)
    g_mix_norm = jnp.stack([tot[SUB_ROW[0] + 1], tot[SUB_ROW[2] + 1]])
    g_mlp_norm = jnp.stack([tot[SUB_ROW[1] + 1], tot[SUB_ROW[3] + 1]])
    conv_sums = tot[SSD_CONV_ROW:SSD_CONV_ROW + 24].reshape(8, CONVD)
    g_ssd_conv_w = lax.dynamic_slice(conv_sums, (0, chip * 768), (4, 768))[None]
    g_ssd_conv_b = conv_sums[4:5]
    g_ssd_norm = tot[GNORM_ROW:GNORM_ROW + 2].reshape(1, DI)
    g_final = tot[FINAL_ROW]
    g_sc_conv_w = lax.dynamic_slice(tot[SC_CONV_ROW:SC_CONV_ROW + 3], (0, chip * 256), (3, 256))[None]
    g_a_log, g_d, g_dt_bias = (tot[HEAD_ROW + r:HEAD_ROW + r + 1, 0:NH] for r in range(3))
    c_pad = jnp.concatenate([c_all, jnp.zeros((8, D), F32)], axis=0)
    dmod_all = jnp.stack([small_all[:, r] for r in mod_rows], axis=1).reshape(N_DEV, 2, 6 * D)
    g_ada_w = []
    for i in range(2):
        dm = lax.dynamic_slice(dmod_all[:, i], (0, chip * n_ada), (N_DEV, n_ada))
        g_ada_w.append(_matmul_tn(c_pad, jnp.concatenate([dm, jnp.zeros_like(dm)], axis=0), m=D, n=n_ada, a_silu=True,
                                  name=f"ada_dw{i}"))
    g_ada_w = jnp.stack(g_ada_w)

    def ssd_in_owner(k):
        lo, hi = k * SSD_IN_SHARD, (k + 1) * SSD_IN_SHARD
        if hi <= ZX:
            return d_w_zx[:, lo:hi]
        return jnp.concatenate([d_w_zx[:, lo:], d_w_dt[:, :hi - ZX]], axis=1)

    g_ssd_in = jnp.stack([ssd_in_owner(k) for k in range(N_CHIPS)]).astype(BF16)
    gbufs = [g1, g_ssd_in, g_sc_in]
    core = ci.reshape(1).astype(jnp.int32)
    chip_core = jnp.stack([chip, ci]).astype(jnp.int32)
    sib = _swap_halves_with_sibling(gbufs, name="rs_sibling")
    hs = [_add_sibling_half(g, s, core, name=f"rs_add_sibling{b}") for b, (g, s) in enumerate(zip(gbufs, sib))]
    others = _send_to_owners(hs, name="rs_owners")
    ts = [_add_chip_sums(h, o, chip_core, name=f"rs_add_chips{b}") for b, (h, o) in enumerate(zip(hs, others))]
    t1, t_ssd_in, t_sc_in = [t.reshape(-1, t.shape[2]) for t in _swap_results_with_sibling(ts, name="rs_result")]

    big = dict(mlp_up=(t1, UP_ROW), mlp_down=(t1, DOWN_ROW), ssd_out_w=(t1, SSD_OUT_ROW), sc_out_w=(t1, SC_OUT_ROW),
               ssd_in_w=(t_ssd_in, 0), sc_in_w=(t_sc_in, 0))
    grads = dict(ada_w=g_ada_w, ada_b=g_ada_b, mix_norm_w=g_mix_norm, mlp_norm_w=g_mlp_norm, ssd_conv_w=g_ssd_conv_w,
                 ssd_conv_b=g_ssd_conv_b, ssd_dt_bias=g_dt_bias, ssd_A_log=g_a_log, ssd_D=g_d, ssd_norm_w=g_ssd_norm,
                 sc_conv_w=g_sc_conv_w, final_norm_w=g_final)
    weights = dict(ada_w=(ada_w, m_ada_w, v_ada_w), ada_b=(ada_b, m_ada_b, v_ada_b),
                   mix_norm_w=(mix_norm_w, m_mix_norm_w, v_mix_norm_w), mlp_norm_w=(mlp_norm_w, m_mlp_norm_w, v_mlp_norm_w),
                   mlp_up=(mlp_up, m_mlp_up, v_mlp_up), mlp_down=(mlp_down, m_mlp_down, v_mlp_down),
                   ssd_in_w=(ssd_in_w, m_ssd_in_w, v_ssd_in_w), ssd_conv_w=(ssd_conv_w, m_ssd_conv_w, v_ssd_conv_w),
                   ssd_conv_b=(ssd_conv_b, m_ssd_conv_b, v_ssd_conv_b), ssd_dt_bias=(ssd_dt_bias, m_ssd_dt_bias, v_ssd_dt_bias),
                   ssd_A_log=(ssd_A_log, m_ssd_A_log, v_ssd_A_log), ssd_D=(ssd_D, m_ssd_D, v_ssd_D),
                   ssd_norm_w=(ssd_norm_w, m_ssd_norm_w, v_ssd_norm_w), ssd_out_w=(ssd_out_w, m_ssd_out_w, v_ssd_out_w),
                   sc_in_w=(sc_in_w, m_sc_in_w, v_sc_in_w), sc_conv_w=(sc_conv_w, m_sc_conv_w, v_sc_conv_w),
                   sc_out_w=(sc_out_w, m_sc_out_w, v_sc_out_w), final_norm_w=(final_norm_w, m_final_norm_w, v_final_norm_w))
    g_out, d_out, m_out, v_out = [], [], [], []
    for nm, (w, m, v) in weights.items():
        shp = w.shape
        two_d = (-1, shp[-1]) if w.ndim > 1 else (1, -1)
        if nm in big:
            gbuf, g_row = big[nm]
            g, dl, mn, vn = _adamw(w.reshape(two_d), gbuf, m.reshape(two_d), v.reshape(two_d), g_row=g_row, emit_g=True,
                                   name="adamw_" + nm)
        else:
            g = grads[nm]
            dl, mn, vn = _adamw(w.reshape(two_d), g.reshape(two_d), m.reshape(two_d), v.reshape(two_d), name="adamw_" + nm)
        g_out.append(g.reshape(shp))
        d_out.append(dl.reshape(shp))
        m_out.append(mn.reshape(shp))
        v_out.append(vn.reshape(shp))
    return (loss, grad_x[None], *g_out, *d_out, *m_out, *v_out)
```

```python
import jax
import jax.numpy as jnp
from jax import lax
from jax.experimental import pallas as pl
from jax.experimental.pallas import tpu as pltpu

F32 = jnp.float32
BF16 = jnp.bfloat16
MESH = pl.DeviceIdType.MESH
HIGHEST = lax.Precision.HIGHEST

D = 1024
DFF = 4096
DI = 2048
NH = 32
HP = 64
NG = 4
NS = 128
CH = 128
CONVD = DI + 2 * NG * NS
ZX = DI + CONVD
GW = NG * NS
LANES = 128
N_CHIPS = 4
N_DEV = 8
EPS = 1e-5
ADAM_LR, ADAM_B1, ADAM_B2, ADAM_EPS, ADAM_WD, ADAM_STEP = 1e-3, 0.9, 0.999, 1e-8, 0.01, 10
VMEM_LIMIT = 48 * 1024 * 1024
ANY = pl.BlockSpec(memory_space=pl.ANY)
HBM = pl.BlockSpec(memory_space=pltpu.HBM)
SEM = pl.BlockSpec(memory_space=pltpu.SEMAPHORE)

SSD_IN_SHARD = 1288
SC_IN_SHARD = 768


def _params(sem=None):
    return pltpu.CompilerParams(dimension_semantics=sem, vmem_limit_bytes=VMEM_LIMIT)


def _sigmoid(v):
    return 1.0 / (1.0 + jnp.exp(-v))


def _dot(a, b, dims=((1,), (0,)), precision=None):
    return lax.dot_general(a, b, (dims, ((), ())), preferred_element_type=F32, precision=precision)


def _dot_nt(a, b):
    return _dot(a, b, ((1,), (1,)))


def _dot_tn(a, b):
    return _dot(a, b, ((0,), (0,)))


def _nn(av, bv):
    return _dot(av.astype(BF16), bv.astype(BF16))


def _nt(av, bv):
    return _dot_nt(av.astype(BF16), bv.astype(BF16))


def _nn_split(av, bv):
    r = bv.shape[1]
    acc = _dot(av[:, 0:r].astype(BF16), bv[0])
    for s in range(1, bv.shape[0]):
        acc = acc + _dot(av[:, s * r:(s + 1) * r].astype(BF16), bv[s])
    return acc


def _nt_split(av, bv):
    kc = bv.shape[2]
    acc = _dot_nt(av[:, 0:kc].astype(BF16), bv[0])
    for s in range(1, bv.shape[0]):
        acc = acc + _dot_nt(av[:, s * kc:(s + 1) * kc].astype(BF16), bv[s])
    return acc


def _nt_sc_in(av, bv):
    q = 256
    acc = None
    for i in range(3 * D // q):
        a_blk = av[i // 4][:, (i % 4) * q:(i % 4 + 1) * q]
        b_blk = bv[i // 3][:, (i % 3) * q:(i % 3 + 1) * q]
        t = _dot_nt(a_blk, b_blk)
        acc = t if acc is None else acc + t
    return acc


def _matmul(a, b, *, name, n, contract=_nn, a_spec=None, b_spec=None, tm=512, tn=512, extras=(), epi=None,
            out_dtypes=(F32,), a_silu=False):
    M = a.shape[-2]
    tm, tn = min(tm, M), min(tn, n)
    assert M % tm == 0 and n % tn == 0, (name, M, n, tm, tn)
    n_ex = len(extras)
    if a_spec is None:
        a_spec = pl.BlockSpec((tm, a.shape[1]), lambda i, j: (i, 0))
    if b_spec is None:
        b_spec = (pl.BlockSpec((tn, b.shape[1]), lambda i, j: (j, 0)) if contract is _nt
                  else pl.BlockSpec((b.shape[0], tn), lambda i, j: (0, j)))

    def body(*refs):
        av = refs[0][...]
        if a_silu:
            av = av * _sigmoid(av)
        acc = contract(av, refs[1][...])
        res = epi(acc, *[r[...] for r in refs[2:2 + n_ex]]) if epi is not None else (acc,)
        for o_ref, r in zip(refs[2 + n_ex:], res, strict=True):
            o_ref[...] = r.astype(o_ref.dtype)

    in_specs = [a_spec, b_spec]
    for e in extras:
        in_specs.append(pl.BlockSpec((1, tn), lambda i, j: (0, j)) if e.shape[0] == 1 and M != 1
                        else pl.BlockSpec((tm, tn), lambda i, j: (i, j)))
    outs = pl.pallas_call(
        body, grid=(M // tm, n // tn), in_specs=in_specs,
        out_specs=[pl.BlockSpec((tm, tn), lambda i, j: (i, j)) for _ in out_dtypes],
        out_shape=[jax.ShapeDtypeStruct((M, n), dt) for dt in out_dtypes],
        compiler_params=_params(("parallel", "parallel")), name=name)(a, b, *extras)
    return outs if len(out_dtypes) > 1 else outs[0]


def _matmul_tn(a, b, *, name, m, n, tm=512, tn=512, a_spec=None, b_spec=None, out_spec=None, out_struct=None, into=None,
               a_silu=False):
    T = a.shape[-2]
    tm, tn = min(tm, m), min(tn, n)
    assert m % tm == 0 and n % tn == 0, (name, m, n, tm, tn)
    if a_spec is None:
        a_spec = pl.BlockSpec((T, tm), lambda i, j: (0, i))
    if b_spec is None:
        b_spec = pl.BlockSpec((T, tn), lambda i, j: (0, j))
    if out_spec is None:
        out_spec, out_struct = pl.BlockSpec((tm, tn), lambda i, j: (i, j)), jax.ShapeDtypeStruct((m, n), F32)

    def body(a_ref, b_ref, *rest):
        av = a_ref[...]
        if a_silu:
            av = av * _sigmoid(av)
        rest[-1][...] = _dot_tn(av.astype(BF16), b_ref[...].astype(BF16)).astype(rest[-1].dtype)

    args, in_specs, alias = [a, b], [a_spec, b_spec], {}
    if into is not None:
        args, in_specs, alias = args + [into], in_specs + [ANY], {2: 0}
    return pl.pallas_call(body, grid=(m // tm, n // tn), in_specs=in_specs, out_specs=out_spec, out_shape=out_struct,
                          input_output_aliases=alias, compiler_params=_params(("parallel", "parallel")), name=name)(*args)


def _modnorm_fwd(x, nw, sc, sh, *, name):
    L = x.shape[0]
    tm = min(L, 512)

    def body(x_ref, nw_ref, sc_ref, sh_ref, h_ref):
        xv = x_ref[...]
        r = lax.rsqrt(jnp.mean(xv * xv, axis=-1, keepdims=True) + EPS)
        h_ref[...] = ((xv * r * nw_ref[...]) * (1.0 + sc_ref[...]) + sh_ref[...]).astype(BF16)

    row = pl.BlockSpec((tm, D), lambda i: (i, 0))
    vec = pl.BlockSpec((1, D), lambda i: (0, 0))
    return pl.pallas_call(body, grid=(L // tm,), in_specs=[row, vec, vec, vec], out_specs=row,
                          out_shape=jax.ShapeDtypeStruct((L, D), BF16),
                          compiler_params=_params(("parallel",)), name=name)(x, nw, sc, sh)


def _modnorm_bwd(x, dh, dxo, nw, sc, gsum, *, name):
    L = x.shape[0]
    tm = min(L, 256)

    def body(x_ref, dh_ref, dxo_ref, nw_ref, sc_ref, g_ref, dx_ref, s_ref):
        @pl.when(pl.program_id(0) == 0)
        def _():
            s_ref[...] = g_ref[...]

        xv, dhv = x_ref[...], dh_ref[...]
        r = lax.rsqrt(jnp.mean(xv * xv, axis=-1, keepdims=True) + EPS)
        xhat = xv * r
        dxhat = dhv * (nw_ref[...] * (1.0 + sc_ref[...]))
        dx_ref[...] = dxo_ref[...] + r * (dxhat - xhat * jnp.mean(dxhat * xhat, axis=-1, keepdims=True))
        s_ref[1:2, :] += jnp.sum(dhv * xhat, axis=0, keepdims=True) * (1.0 + sc_ref[...])
        s_ref[2:3, :] += jnp.sum(dhv * xhat, axis=0, keepdims=True) * nw_ref[...]
        s_ref[3:4, :] += jnp.sum(dhv, axis=0, keepdims=True)

    row = pl.BlockSpec((tm, D), lambda i: (i, 0))
    vec = pl.BlockSpec((1, D), lambda i: (0, 0))
    blk = pl.BlockSpec((8, D), lambda i: (0, 0))
    return pl.pallas_call(body, grid=(L // tm,), in_specs=[row, row, row, vec, vec, blk], out_specs=[row, blk],
                          out_shape=[jax.ShapeDtypeStruct((L, D), F32), jax.ShapeDtypeStruct((8, D), F32)],
                          compiler_params=_params(("arbitrary",)), name=name)(x, dh, dxo, nw, sc, gsum)


def _gate_bwd(dxo, y, g, *, name):
    L = dxo.shape[0]
    tm = min(L, 512)

    def body(dxo_ref, y_ref, g_ref, dy_ref, s_ref):
        @pl.when(pl.program_id(0) == 0)
        def _():
            s_ref[...] = jnp.zeros_like(s_ref)

        dv = dxo_ref[...]
        dy_ref[...] = (dv * g_ref[...]).astype(BF16)
        s_ref[0:1, :] += jnp.sum(dv * y_ref[...], axis=0, keepdims=True)

    row = pl.BlockSpec((tm, D), lambda i: (i, 0))
    return pl.pallas_call(body, grid=(L // tm,), in_specs=[row, row, pl.BlockSpec((1, D), lambda i: (0, 0))],
                          out_specs=[row, pl.BlockSpec((8, D), lambda i: (0, 0))],
                          out_shape=[jax.ShapeDtypeStruct((L, D), BF16), jax.ShapeDtypeStruct((8, D), F32)],
                          compiler_params=_params(("arbitrary",)), name=name)(dxo, y, g)


def _final_loss(x, fw, tgt, *, name):
    L = x.shape[0]
    tm = min(L, 256)

    def body(x_ref, fw_ref, t_ref, dx_ref, s_ref):
        @pl.when(pl.program_id(0) == 0)
        def _():
            s_ref[...] = jnp.zeros_like(s_ref)

        xv = x_ref[...]
        r = lax.rsqrt(jnp.mean(xv * xv, axis=-1, keepdims=True) + EPS)
        xhat = xv * r
        diff = xhat * fw_ref[...] - t_ref[...]
        dout = diff * (1.0 / D)
        dxhat = dout * fw_ref[...]
        dx_ref[...] = r * (dxhat - xhat * jnp.mean(dxhat * xhat, axis=-1, keepdims=True))
        s_ref[0:1, :] += jnp.sum(dout * xhat, axis=0, keepdims=True)
        s_ref[1:2, :] += jnp.zeros((1, D), F32) + 0.5 * jnp.sum(jnp.sum(diff * diff, axis=-1, keepdims=True) * (1.0 / D))

    row = pl.BlockSpec((tm, D), lambda i: (i, 0))
    return pl.pallas_call(body, grid=(L // tm,), in_specs=[row, pl.BlockSpec((1, D), lambda i: (0, 0)), row],
                          out_specs=[row, pl.BlockSpec((8, D), lambda i: (0, 0))],
                          out_shape=[jax.ShapeDtypeStruct((L, D), F32), jax.ShapeDtypeStruct((8, D), F32)],
                          compiler_params=_params(("arbitrary",)), name=name)(x, fw, tgt)


def _shift_down(v, j):
    if j == 0:
        return v
    row = lax.broadcasted_iota(jnp.int32, v.shape, 0)
    return jnp.where(row >= j, pltpu.roll(v, j, 0), 0.0)


def _shift_up(v, j):
    if j == 0:
        return v
    n = v.shape[0]
    row = lax.broadcasted_iota(jnp.int32, v.shape, 0)
    return jnp.where(row < n - j, pltpu.roll(v, n - j, 0), 0.0)


def _ssd_conv_fwd(zx, w, b, *, name):
    L = zx.shape[0]
    cb = 256
    k = w.shape[0]

    def body(x_ref, w_ref, b_ref, o_ref):
        xv = x_ref[...]
        pre = b_ref[...] + xv * w_ref[k - 1:k, :]
        for j in range(1, k):
            pre = pre + _shift_down(xv, j) * w_ref[k - 1 - j:k - j, :]
        o_ref[...] = pre * _sigmoid(pre)

    return pl.pallas_call(
        body, grid=(CONVD // cb,),
        in_specs=[pl.BlockSpec((L, cb), lambda i: (0, i + DI // cb)), pl.BlockSpec((k, cb), lambda i: (0, i)),
                  pl.BlockSpec((1, cb), lambda i: (0, i))],
        out_specs=pl.BlockSpec((L, cb), lambda i: (0, i)), out_shape=jax.ShapeDtypeStruct((L, CONVD), F32),
        compiler_params=_params(("parallel",)), name=name)(zx, w, b)


def _ssd_conv_bwd(zx, dact, w, b, dzx, *, name):
    L = zx.shape[0]
    cb = 256
    k = w.shape[0]

    def body(x_ref, da_ref, w_ref, b_ref, _, dx_ref, s_ref):
        xv = x_ref[...]
        sh = [_shift_down(xv, j) for j in range(k)]
        pre = b_ref[...] + sh[0] * w_ref[k - 1:k, :]
        for j in range(1, k):
            pre = pre + sh[j] * w_ref[k - 1 - j:k - j, :]
        s = _sigmoid(pre)
        dpre = da_ref[...] * (s * (1.0 + pre * (1.0 - s)))
        dx = dpre * w_ref[k - 1:k, :]
        for j in range(1, k):
            dx = dx + _shift_up(dpre, j) * w_ref[k - 1 - j:k - j, :]
        dx_ref[...] = dx.astype(BF16)
        s_ref[...] = jnp.zeros_like(s_ref)
        for j in range(k):
            s_ref[k - 1 - j:k - j, :] = jnp.sum(dpre * sh[j], axis=0, keepdims=True)
        s_ref[k:k + 1, :] = jnp.sum(dpre, axis=0, keepdims=True)

    return pl.pallas_call(
        body, grid=(CONVD // cb,),
        in_specs=[pl.BlockSpec((L, cb), lambda i: (0, i + DI // cb)), pl.BlockSpec((L, cb), lambda i: (0, i)),
                  pl.BlockSpec((k, cb), lambda i: (0, i)), pl.BlockSpec((1, cb), lambda i: (0, i)), ANY],
        out_specs=[pl.BlockSpec((L, cb), lambda i: (0, i + DI // cb)), pl.BlockSpec((8, cb), lambda i: (0, i))],
        out_shape=[jax.ShapeDtypeStruct((L, ZX), BF16), jax.ShapeDtypeStruct((8, CONVD), F32)],
        input_output_aliases={4: 0}, compiler_params=_params(("parallel",)), name=name)(zx, dact, w, b, dzx)


def _sc_fwd(proj, w, *, name):
    L = proj.shape[0]
    cb = 256
    nb = D // cb
    k = w.shape[0]

    def body(b_ref, c_ref, x_ref, w_ref, o_ref):
        u = c_ref[...] * x_ref[...]
        v = u * w_ref[k - 1:k, :]
        for j in range(1, k):
            v = v + _shift_down(u, j) * w_ref[k - 1 - j:k - j, :]
        o_ref[...] = (b_ref[...] * v).astype(BF16)

    return pl.pallas_call(
        body, grid=(nb,),
        in_specs=[pl.BlockSpec((L, cb), lambda i: (0, i)), pl.BlockSpec((L, cb), lambda i: (0, i + nb)),
                  pl.BlockSpec((L, cb), lambda i: (0, i + 2 * nb)), pl.BlockSpec((k, cb), lambda i: (0, i))],
        out_specs=pl.BlockSpec((L, cb), lambda i: (0, i)), out_shape=jax.ShapeDtypeStruct((L, D), BF16),
        compiler_params=_params(("parallel",)), name=name)(proj, proj, proj, w)


def _sc_bwd(proj, dyv, w, *, name):
    L = proj.shape[0]
    cb = 256
    nb = D // cb
    k = w.shape[0]

    def body(b_ref, c_ref, x_ref, dy_ref, w_ref, dp_ref, s_ref):
        cv, xv = c_ref[...], x_ref[...]
        u = cv * xv
        sh = [_shift_down(u, j) for j in range(k)]
        v = sh[0] * w_ref[k - 1:k, :]
        for j in range(1, k):
            v = v + sh[j] * w_ref[k - 1 - j:k - j, :]
        dyv_ = dy_ref[...]
        dp_ref[0] = (dyv_ * v).astype(BF16)
        dv = dyv_ * b_ref[...]
        du = dv * w_ref[k - 1:k, :]
        for j in range(1, k):
            du = du + _shift_up(dv, j) * w_ref[k - 1 - j:k - j, :]
        dp_ref[1] = (du * xv).astype(BF16)
        dp_ref[2] = (du * cv).astype(BF16)
        s_ref[...] = jnp.zeros_like(s_ref)
        for j in range(k):
            s_ref[k - 1 - j:k - j, :] = jnp.sum(dv * sh[j], axis=0, keepdims=True)

    blk = pl.BlockSpec((L, cb), lambda i: (0, i))
    return pl.pallas_call(
        body, grid=(nb,),
        in_specs=[blk, pl.BlockSpec((L, cb), lambda i: (0, i + nb)), pl.BlockSpec((L, cb), lambda i: (0, i + 2 * nb)),
                  blk, pl.BlockSpec((k, cb), lambda i: (0, i))],
        out_specs=[pl.BlockSpec((3, L, cb), lambda i: (0, 0, i)), pl.BlockSpec((8, cb), lambda i: (0, i))],
        out_shape=[jax.ShapeDtypeStruct((3, L, D), BF16), jax.ShapeDtypeStruct((8, D), F32)],
        compiler_params=_params(("parallel",)), name=name)(proj, proj, proj, dyv, w)


def _ssd_chunk_terms(dtr, prm):
    lane = lax.broadcasted_iota(jnp.int32, (CH, LANES), 1)
    valid = lane < NH
    xdt = dtr + prm[0:1, :]
    dt = jnp.where(valid, jnp.maximum(xdt, 0.0) + jnp.log1p(jnp.exp(-jnp.abs(xdt))), 0.0)
    A = -jnp.exp(prm[1:2, :])
    ri = lax.broadcasted_iota(jnp.int32, (CH, CH), 0)
    ci = lax.broadcasted_iota(jnp.int32, (CH, CH), 1)
    cs = _dot((ri >= ci).astype(F32), dt * A, precision=HIGHEST)
    last = cs[CH - 1:CH, :]
    ex = (lax.broadcasted_iota(jnp.int32, (LANES, DI), 1) // HP == lax.broadcasted_iota(jnp.int32, (LANES, DI), 0)).astype(F32)
    return dict(valid=valid, xdt=xdt, dt=dt, A=A, cs=cs, csT=cs.T, last=last, ri=ri, ci=ci, ex=ex)


def _expand(v, ex):
    if v.shape[0] == 1:
        return _dot(jnp.broadcast_to(v, (8, LANES)), ex, precision=HIGHEST)[0:1, :]
    return _dot(v, ex, precision=HIGHEST)


def _head_sum(v, ex):
    if v.shape[0] == 1:
        return _dot(jnp.broadcast_to(v, (8, DI)), ex, ((1,), (1,)), precision=HIGHEST)[0:1, :]
    return _dot(v, ex, ((1,), (1,)), precision=HIGHEST)


def _ssd_fwd(xbc, dtr, prm, *, name):
    L = xbc.shape[0]
    nc = L // CH

    def body(xbc_ref, dtr_ref, prm_ref, y_ref, sp_ref, st_ref):
        @pl.when(pl.program_id(0) == 0)
        def _():
            st_ref[...] = jnp.zeros_like(st_ref)

        prm_v = prm_ref[...]
        t = _ssd_chunk_terms(dtr_ref[...], prm_v)
        cs, csT, ex, causal = t["cs"], t["csT"], t["ex"], t["ri"] >= t["ci"]
        xs = xbc_ref[:, 0:DI]
        X = xs * _expand(t["dt"], ex)
        Xb = X.astype(BF16)
        Xd = (X * _expand(jnp.exp(t["last"] - cs), ex)).astype(BF16)
        Ex = _expand(jnp.exp(cs), ex)
        cdx = _expand(jnp.exp(t["last"]), ex)
        dskx = _expand(prm_v[2:3, :], ex)
        lane = lax.broadcasted_iota(jnp.int32, (CH, LANES), 1)
        sp_ref[0] = st_ref[...]
        for g in range(NG):
            Bg = xbc_ref[:, DI + g * NS:DI + (g + 1) * NS].astype(BF16)
            Cg = xbc_ref[:, DI + GW + g * NS:DI + GW + (g + 1) * NS].astype(BF16)
            G = _dot_nt(Cg, Bg)
            Sg = st_ref[:, g * GW:(g + 1) * GW]
            yoff = _dot(Cg, Sg.astype(BF16)) * Ex[:, g * GW:(g + 1) * GW]
            for j in range(GW // LANES):
                lo = g * GW + j * LANES
                Xp = Xb[:, lo:lo + LANES]
                yd = []
                for h in (lo // HP, lo // HP + 1):
                    seg = cs[:, h:h + 1] - csT[h:h + 1, :]
                    yd.append(_dot((G * jnp.where(causal, jnp.exp(seg), 0.0)).astype(BF16), Xp))
                y_ref[:, lo:lo + LANES] = (jnp.where(lane < HP, yd[0], yd[1]) + yoff[:, j * LANES:(j + 1) * LANES]
                                           + dskx[:, lo:lo + LANES] * xs[:, lo:lo + LANES])
            st_ref[:, g * GW:(g + 1) * GW] = Sg * cdx[:, g * GW:(g + 1) * GW] + _dot_tn(Bg, Xd[:, g * GW:(g + 1) * GW])

    return pl.pallas_call(
        body, grid=(nc,),
        in_specs=[pl.BlockSpec((CH, CONVD), lambda c: (c, 0)), pl.BlockSpec((CH, LANES), lambda c: (c, 0)),
                  pl.BlockSpec((8, LANES), lambda c: (0, 0))],
        out_specs=[pl.BlockSpec((CH, DI), lambda c: (c, 0)), pl.BlockSpec((1, NS, DI), lambda c: (c, 0, 0))],
        out_shape=[jax.ShapeDtypeStruct((L, DI), F32), jax.ShapeDtypeStruct((nc, NS, DI), F32)],
        scratch_shapes=[pltpu.VMEM((NS, DI), F32)],
        compiler_params=_params(("arbitrary",)), name=name)(xbc, dtr, prm)


def _ssd_bwd(xbc, dtr, prm, dy, sprev, *, name):
    L = xbc.shape[0]
    nc = L // CH

    def body(xbc_ref, dtr_ref, prm_ref, dy_ref, sp_ref, dxbc_ref, ddtr_ref, s_ref, dst_ref, dx_scr, de_scr, dd_scr):
        step = pl.program_id(0)

        @pl.when(step == 0)
        def _():
            dst_ref[...] = jnp.zeros_like(dst_ref)
            s_ref[...] = jnp.zeros_like(s_ref)

        prm_v = prm_ref[...]
        t = _ssd_chunk_terms(dtr_ref[...], prm_v)
        cs, csT, ex, ri, ci = t["cs"], t["csT"], t["ex"], t["ri"], t["ci"]
        E = jnp.exp(cs)
        dec = jnp.exp(t["last"] - cs)
        cd = jnp.exp(t["last"])
        xs = xbc_ref[:, 0:DI]
        dtx = _expand(t["dt"], ex)
        X = xs * dtx
        Xb = X.astype(BF16)
        decx = _expand(dec, ex)
        Xd = (X * decx).astype(BF16)
        Ex = _expand(E, ex)
        cdx = _expand(cd, ex)
        dskx = _expand(prm_v[2:3, :], ex)
        lane = lax.broadcasted_iota(jnp.int32, (CH, LANES), 1)
        dcs = jnp.zeros((CH, LANES), F32)
        dcd_x = []
        for g in range(NG):
            gs = slice(g * GW, (g + 1) * GW)
            Bg = xbc_ref[:, DI + g * NS:DI + (g + 1) * NS].astype(BF16)
            Cg = xbc_ref[:, DI + GW + g * NS:DI + GW + (g + 1) * NS].astype(BF16)
            G = _dot_nt(Cg, Bg)
            GT = _dot_nt(Bg, Cg)
            Sg = sp_ref[0, :, gs]
            Sgb = Sg.astype(BF16)
            dyg = dy_ref[:, gs]
            de_scr[:, gs] = dyg * _dot(Cg, Sgb)
            dYo = (Ex[:, gs] * dyg).astype(BF16)
            dC = _dot_nt(dYo, Sgb)
            dS_in = _dot_tn(Cg, dYo)
            dStg = dst_ref[:, gs]
            dStb = dStg.astype(BF16)
            dXd = _dot(Bg, dStb)
            dB = _dot_nt(Xd[:, gs], dStb)
            dd_scr[:, gs] = dXd * X[:, gs]
            dXst = dXd * decx[:, gs]
            dG = jnp.zeros((CH, CH), F32)
            dGT = jnp.zeros((CH, CH), F32)
            for j in range(GW // LANES):
                lo = g * GW + j * LANES
                Xp = Xb[:, lo:lo + LANES]
                dyp = dy_ref[:, lo:lo + LANES]
                dXp = dXst[:, j * LANES:(j + 1) * LANES]
                for k, h in enumerate((lo // HP, lo // HP + 1)):
                    dyh = jnp.where((lane < HP) if k == 0 else (lane >= HP), dyp, 0.0).astype(BF16)
                    seg = cs[:, h:h + 1] - csT[h:h + 1, :]
                    Lm = jnp.where(ri >= ci, jnp.exp(seg), 0.0)
                    LmT = jnp.where(ci >= ri, jnp.exp(-seg), 0.0)
                    dM = _dot_nt(dyh, Xp)
                    dMT = _dot_nt(Xp, dyh)
                    MT = GT * LmT
                    rs = jnp.sum(dM * (G * Lm), axis=1, keepdims=True) - jnp.sum(dMT * MT, axis=1, keepdims=True)
                    dcs = dcs + jnp.where(lane == h, rs, 0.0)
                    dG = dG + dM * Lm
                    dGT = dGT + dMT * LmT
                    dXp = dXp + _dot(MT.astype(BF16), dyh)
                dx_scr[:, lo:lo + LANES] = dXp
            dxbc_ref[:, DI + g * NS:DI + (g + 1) * NS] = dB + _dot(dGT.astype(BF16), Cg)
            dxbc_ref[:, DI + GW + g * NS:DI + GW + (g + 1) * NS] = dC + _dot(dG.astype(BF16), Bg)
            dcd_x.append(jnp.sum(dStg * Sg, axis=0, keepdims=True))
            dst_ref[:, gs] = dStg * cdx[:, gs] + dS_in
        dX = dx_scr[...]
        dy = dy_ref[...]
        ddec = _head_sum(dd_scr[...], ex)
        dcd = _head_sum(jnp.concatenate(dcd_x, axis=1), ex)
        dcs = dcs + _head_sum(de_scr[...], ex) * E - ddec * dec
        row = lax.broadcasted_iota(jnp.int32, (CH, LANES), 0)
        dcs = dcs + jnp.where(row == CH - 1, jnp.sum(ddec * dec, axis=0, keepdims=True) + dcd * cd, 0.0)
        da = _dot((ci >= ri).astype(F32), dcs, precision=HIGHEST)
        ddt = da * t["A"] + _head_sum(dX * xs, ex)
        ddtr = jnp.where(t["valid"], ddt * _sigmoid(t["xdt"]), 0.0)
        ddtr_ref[...] = ddtr
        dxbc_ref[:, 0:DI] = dX * dtx + dskx * dy
        s_ref[0:1, :] += jnp.sum(da * t["dt"], axis=0, keepdims=True)
        s_ref[1:2, :] += _head_sum(jnp.sum(dy * xs, axis=0, keepdims=True), ex)
        s_ref[2:3, :] += jnp.sum(ddtr, axis=0, keepdims=True)

        @pl.when(step == nc - 1)
        def _():
            s_ref[0:1, :] = s_ref[0:1, :] * t["A"]

    rev = lambda c: (nc - 1 - c, 0)
    return pl.pallas_call(
        body, grid=(nc,),
        in_specs=[pl.BlockSpec((CH, CONVD), rev), pl.BlockSpec((CH, LANES), rev), pl.BlockSpec((8, LANES), lambda c: (0, 0)),
                  pl.BlockSpec((CH, DI), rev), pl.BlockSpec((1, NS, DI), lambda c: (nc - 1 - c, 0, 0))],
        out_specs=[pl.BlockSpec((CH, CONVD), rev), pl.BlockSpec((CH, LANES), rev), pl.BlockSpec((8, LANES), lambda c: (0, 0))],
        out_shape=[jax.ShapeDtypeStruct((L, CONVD), F32), jax.ShapeDtypeStruct((L, LANES), F32),
                   jax.ShapeDtypeStruct((8, LANES), F32)],
        scratch_shapes=[pltpu.VMEM((NS, DI), F32), pltpu.VMEM((CH, DI), F32), pltpu.VMEM((CH, DI), F32),
                        pltpu.VMEM((CH, DI), F32)],
        compiler_params=_params(("arbitrary",)), name=name)(xbc, dtr, prm, dy, sprev)


def _gnorm_fwd(y, zx, nw, *, name):
    L = y.shape[0]
    tm = min(L, 256)

    def body(y_ref, z_ref, nw_ref, o_ref):
        z = z_ref[...]
        yg = y_ref[...] * (z * _sigmoid(z))
        for g in range(NG):
            v = yg[:, g * GW:(g + 1) * GW]
            r = lax.rsqrt(jnp.mean(v * v, axis=-1, keepdims=True) + EPS)
            o_ref[:, g * GW:(g + 1) * GW] = (v * r * nw_ref[:, g * GW:(g + 1) * GW]).astype(BF16)

    row = pl.BlockSpec((tm, DI), lambda i: (i, 0))
    return pl.pallas_call(body, grid=(L // tm,), in_specs=[row, row, pl.BlockSpec((1, DI), lambda i: (0, 0))],
                          out_specs=row, out_shape=jax.ShapeDtypeStruct((L, DI), BF16),
                          compiler_params=_params(("parallel",)), name=name)(y, zx, nw)


def _gnorm_bwd(y, zx, nw, dyn, *, name):
    L = y.shape[0]
    tm = min(L, 256)

    def body(y_ref, z_ref, nw_ref, dyn_ref, dy_ref, dz_ref, s_ref):
        @pl.when(pl.program_id(0) == 0)
        def _():
            s_ref[...] = jnp.zeros_like(s_ref)

        z, yv = z_ref[...], y_ref[...]
        sz = _sigmoid(z)
        gate = z * sz
        dgate_dz = sz * (1.0 + z * (1.0 - sz))
        for g in range(NG):
            gs = slice(g * GW, (g + 1) * GW)
            v = yv[:, gs] * gate[:, gs]
            r = lax.rsqrt(jnp.mean(v * v, axis=-1, keepdims=True) + EPS)
            vhat = v * r
            dn = dyn_ref[:, gs]
            s_ref[0:1, gs] += jnp.sum(dn * vhat, axis=0, keepdims=True)
            dvhat = dn * nw_ref[:, gs]
            dv = r * (dvhat - vhat * jnp.mean(dvhat * vhat, axis=-1, keepdims=True))
            dy_ref[:, gs] = dv * gate[:, gs]
            dz_ref[:, gs] = (dv * yv[:, gs] * dgate_dz[:, gs]).astype(BF16)

    row = pl.BlockSpec((tm, DI), lambda i: (i, 0))
    return pl.pallas_call(body, grid=(L // tm,), in_specs=[row, row, pl.BlockSpec((1, DI), lambda i: (0, 0)), row],
                          out_specs=[row, row, pl.BlockSpec((8, DI), lambda i: (0, 0))],
                          out_shape=[jax.ShapeDtypeStruct((L, DI), F32), jax.ShapeDtypeStruct((L, ZX), BF16),
                                     jax.ShapeDtypeStruct((8, DI), F32)],
                          compiler_params=_params(("arbitrary",)), name=name)(y, zx, nw, dyn)


def _adamw(w, g, m, v, *, name, g_row=0, w_row=0, rows=None, into=None, emit_g=False):
    R, C = w.shape
    rows = R if rows is None else rows
    tr = rows
    while tr * C > 256 * 1024 and tr % 16 == 0:
        tr //= 2
    assert g_row % tr == 0 and w_row % tr == 0, (name, g_row, w_row, tr)
    n_out = 4 if emit_g else 3

    def body(w_ref, g_ref, m_ref, v_ref, *rest):
        outs = rest[-n_out:]
        gv = g_ref[...]
        mn = ADAM_B1 * m_ref[...] + (1.0 - ADAM_B1) * gv
        vn = ADAM_B2 * v_ref[...] + (1.0 - ADAM_B2) * (gv * gv)
        m_hat = mn / (1.0 - ADAM_B1 ** ADAM_STEP)
        v_hat = vn / (1.0 - ADAM_B2 ** ADAM_STEP)
        d_ref, mo_ref, vo_ref = outs[-3:]
        d_ref[...] = -ADAM_LR * (m_hat / (jnp.sqrt(v_hat) + ADAM_EPS) + ADAM_WD * w_ref[...])
        mo_ref[...] = mn
        vo_ref[...] = vn
        if emit_g:
            outs[0][...] = gv

    blk = pl.BlockSpec((tr, C), lambda i: (i + w_row // tr, 0))
    args, in_specs, alias = [w, g, m, v], [blk, pl.BlockSpec((tr, C), lambda i: (i + g_row // tr, 0)), blk, blk], {}
    if into is not None:
        args, in_specs, alias = args + list(into), in_specs + [ANY] * n_out, {4 + k: k for k in range(n_out)}
    return pl.pallas_call(body, grid=(rows // tr,), in_specs=in_specs, out_specs=[blk] * n_out,
                          out_shape=[jax.ShapeDtypeStruct((R, C), F32)] * n_out, input_output_aliases=alias,
                          compiler_params=_params(("parallel",)), name=name)(*args)


def _residual(acc, xv, gv):
    return xv + gv * acc, acc


def _relu2(acc):
    a = jnp.maximum(acc, 0.0)
    return a, a * a


def _like(buf):
    return jax.ShapeDtypeStruct(buf.shape, buf.dtype)


def _mlp_fwd(x, mod, nw, wb, up_row, down_row, tag):
    sh, sc, g = mod
    h = _modnorm_fwd(x, nw, sc, sh, name=tag + "_norm")
    a, act = _matmul(h, wb, n=DFF, b_spec=pl.BlockSpec((None, D, 512), lambda mi, j: (j // 2, up_row // D, j % 2)),
                     epi=_relu2, out_dtypes=(BF16, BF16), name=tag + "_up")
    xn, y = _matmul(act, wb, n=D, contract=_nn_split,
                    b_spec=pl.BlockSpec((N_CHIPS, D, 512), lambda mi, j: (0, down_row // D, j)),
                    extras=(x, g), epi=_residual, out_dtypes=(F32, F32), name=tag + "_down")
    return xn, (x, h, a, act, y)


def _mlp_bwd(dxo, saved, mod, nw, wb, gb, up_row, down_row, tag):
    x, h, a, act, y = saved
    sh, sc, g = mod
    dy, gsum = _gate_bwd(dxo, y, g, name=tag + "_dgate")
    du = _matmul(dy, wb, n=DFF, contract=_nt,
                 b_spec=pl.BlockSpec((None, 512, D), lambda mi, j: (j // 2, down_row // 512 + j % 2, 0)),
                 extras=(a,), epi=lambda acc, av: (acc * (2.0 * av.astype(F32)),), out_dtypes=(BF16,), name=tag + "_dact")
    gb = _matmul_tn(act, dy, m=DFF, n=D, into=gb, out_struct=_like(wb),
                    out_spec=pl.BlockSpec((None, 512, 512), lambda mi, j: (mi // 2, down_row // 512 + mi % 2, j)),
                    name=tag + "_ddown")
    dh = _matmul(du, wb, n=D, contract=_nt_split,
                 b_spec=pl.BlockSpec((N_CHIPS, 512, D), lambda mi, j: (0, up_row // 512 + j, 0)), name=tag + "_dh")
    gb = _matmul_tn(h, du, m=D, n=DFF, into=gb, out_struct=_like(wb),
                    out_spec=pl.BlockSpec((None, 512, 512), lambda mi, j: (j // 2, up_row // 512 + mi, j % 2)),
                    name=tag + "_dup")
    dx, sums = _modnorm_bwd(x, dh, dxo, nw, sc, gsum, name=tag + "_dnorm")
    return dx, gb, sums


def _ssd_fwd_scan(x, mod, nw, w_zx, w_dt, conv_w, conv_b, prm, tag):
    sh, sc, g = mod
    h = _modnorm_fwd(x, nw, sc, sh, name=tag + "_norm")
    zx = _matmul(h, w_zx, n=ZX, name=tag + "_in")
    dtr = _matmul(h, w_dt, n=LANES, name=tag + "_in_dt")
    xbc = _ssd_conv_fwd(zx, conv_w, conv_b, name=tag + "_conv")
    y, sprev = _ssd_fwd(xbc, dtr, prm, name=tag + "_scan")
    return h, zx, dtr, xbc, y, sprev


def _ssd_fwd_out(x, mod, scan, gn_w, w_out, tag):
    sh, sc, g = mod
    h, zx, dtr, xbc, y, sprev = scan
    yn = _gnorm_fwd(y, zx, gn_w, name=tag + "_gnorm")
    xn, yo = _matmul(yn, w_out, n=D, contract=_nn_split, b_spec=pl.BlockSpec((N_CHIPS, 512, 512), lambda mi, j: (0, 0, j)),
                     extras=(x, g), epi=_residual, out_dtypes=(F32, F32), name=tag + "_out")
    return xn, (x, h, zx, dtr, xbc, y, sprev, yn, yo)


def _ssd_bwd_out(dxo, saved, mod, w_out, tag):
    x, h, zx, dtr, xbc, y, sprev, yn, yo = saved
    sh, sc, g = mod
    dyo, gsum = _gate_bwd(dxo, yo, g, name=tag + "_dgate")
    dyn = _matmul(dyo, w_out, n=DI, contract=_nt, b_spec=pl.BlockSpec((None, 512, D), lambda mi, j: (j, 0, 0)),
                  name=tag + "_dyn")
    g_out = _matmul_tn(yn, dyo, m=DI, n=D, out_struct=_like(w_out),
                       out_spec=pl.BlockSpec((None, 512, 512), lambda mi, j: (mi, 0, j)), name=tag + "_dout")
    return dyn, g_out, gsum


def _ssd_bwd_rest(dxo, dyn, gsum, saved, mod, nw, w_zx, w_dt, conv_w, conv_b, prm, gn_w, tag):
    x, h, zx, dtr, xbc, y, sprev, yn, yo = saved
    sh, sc, g = mod
    dy, dzx, gnsum = _gnorm_bwd(y, zx, gn_w, dyn, name=tag + "_dgnorm")
    dxbc, ddtr, ssum = _ssd_bwd(xbc, dtr, prm, dy, sprev, name=tag + "_dscan")
    dzx, csum = _ssd_conv_bwd(zx, dxbc, conv_w, conv_b, dzx, name=tag + "_dconv")
    dh_dt = _matmul(ddtr, w_dt, n=D, contract=_nt, name=tag + "_dh_dt")
    dh = _matmul(dzx, w_zx, n=D, contract=_nt, extras=(dh_dt,), epi=lambda acc, e: (acc + e,), name=tag + "_dh")
    d_w_zx = _matmul_tn(h, dzx, m=D, n=ZX, name=tag + "_din")
    d_w_dt = _matmul_tn(h, ddtr, m=D, n=LANES, name=tag + "_din_dt")
    dx, sums = _modnorm_bwd(x, dh, dxo, nw, sc, gsum, name=tag + "_dnorm")
    return dx, d_w_zx, d_w_dt, sums, csum, gnsum, ssum


def _sc_layer_fwd(x, mod, nw, w_sc_in, conv_w, wb, out_row, tag):
    sh, sc, g = mod
    h = _modnorm_fwd(x, nw, sc, sh, name=tag + "_norm")
    proj = _matmul(h, w_sc_in, n=3 * D, tn=256, b_spec=pl.BlockSpec((None, D, 256), lambda mi, j: (j // 3, 0, j % 3)),
                   name=tag + "_in")
    yv = _sc_fwd(proj, conv_w, name=tag + "_conv")
    xn, yo = _matmul(yv, wb, n=D, contract=_nn_split,
                     b_spec=pl.BlockSpec((N_CHIPS, 256, 512), lambda mi, j: (0, out_row // 256, j)),
                     extras=(x, g), epi=_residual, out_dtypes=(F32, F32), name=tag + "_out")
    return xn, (x, h, proj, yv, yo)


def _sc_layer_bwd(dxo, saved, mod, nw, w_sc_in, conv_w, wb, gb, out_row, tag):
    x, h, proj, yv, yo = saved
    sh, sc, g = mod
    L = x.shape[0]
    dyo, gsum = _gate_bwd(dxo, yo, g, name=tag + "_dgate")
    dyv = _matmul(dyo, wb, n=D, tn=256, contract=_nt,
                  b_spec=pl.BlockSpec((None, 256, D), lambda mi, j: (j, out_row // 256, 0)), name=tag + "_dyv")
    gb = _matmul_tn(yv, dyo, m=D, n=D, tm=256, into=gb, out_struct=_like(wb),
                    out_spec=pl.BlockSpec((None, 256, 512), lambda mi, j: (mi, out_row // 256, j)), name=tag + "_dout")
    dproj, csum = _sc_bwd(proj, dyv, conv_w, name=tag + "_dconv")
    tm = min(L, 512)
    dh = _matmul(dproj, w_sc_in, n=D, contract=_nt_sc_in, a_spec=pl.BlockSpec((3, tm, D), lambda mi, j: (0, mi, 0)),
                 b_spec=pl.BlockSpec((N_CHIPS, 512, SC_IN_SHARD), lambda mi, j: (0, j, 0)), name=tag + "_dh")
    g_sc_in = _matmul_tn(h, dproj, m=D, n=3 * D, tn=256, b_spec=pl.BlockSpec((None, L, 256), lambda mi, j: (j // 4, 0, j % 4)),
                         out_spec=pl.BlockSpec((None, 512, 256), lambda mi, j: (j // 3, mi, j % 3)),
                         out_struct=jax.ShapeDtypeStruct((N_CHIPS, D, SC_IN_SHARD), BF16), name=tag + "_din")
    dx, sums = _modnorm_bwd(x, dh, dxo, nw, sc, gsum, name=tag + "_dnorm")
    return dx, gb, g_sc_in, sums, csum


SUB_ROW = (0, 8, 16, 24)
SSD_CONV_ROW, GNORM_ROW, FINAL_ROW, SC_CONV_ROW, HEAD_ROW, SMALL_ROWS = 32, 56, 72, 80, 88, 96


def _all_gather_rows(blk, *, name):
    m_per, n = blk.shape

    def body(x_ref, out_ref, send_sems, recv_sems, local_sem):
        x, y, c = lax.axis_index("x"), lax.axis_index("y"), lax.axis_index("c")
        me, sibling = (x, y, c), (x, y, 1 - c)
        chips = [(1 - x, y), (x, 1 - y), (1 - x, 1 - y)]

        def rows(px, py, pc):
            return out_ref.at[pl.ds((4 * px + 2 * py + pc) * m_per, m_per), :]

        def copy(k, block, to, src=None):
            return pltpu.make_async_remote_copy(src_ref=rows(*block) if src is None else src, dst_ref=rows(*block),
                                                send_sem=send_sems.at[k], recv_sem=recv_sems.at[k], device_id=to,
                                                device_id_type=MESH)

        mine = pltpu.make_async_copy(x_ref, rows(*me), local_sem)
        mine.start()
        first = [copy(0, me, sibling, src=x_ref)] + [copy(1 + j, me, (*chip, c), src=x_ref) for j, chip in enumerate(chips)]
        for cp in first:
            cp.start()
        passed = [copy(4 + j, (*chip, c), sibling) for j, chip in enumerate(chips)]
        for j, chip in enumerate(chips):
            copy(1 + j, (*chip, c), me).wait_recv()
            passed[j].start()
        copy(0, sibling, me).wait_recv()
        for j, chip in enumerate(chips):
            copy(4 + j, (*chip, 1 - c), me).wait_recv()
        for cp in first + passed:
            cp.wait_send()
        mine.wait()

    return pl.pallas_call(
        body, out_shape=jax.ShapeDtypeStruct((N_DEV * m_per, n), blk.dtype),
        in_specs=[pl.BlockSpec(memory_space=pltpu.VMEM)], out_specs=pl.BlockSpec(memory_space=pltpu.VMEM),
        scratch_shapes=[pltpu.SemaphoreType.DMA((7,)), pltpu.SemaphoreType.DMA((7,)), pltpu.SemaphoreType.DMA],
        name=name)(blk)


def _half(ref, chip, c):
    hr = ref.shape[1] // 2
    return ref.at[chip, pl.ds(c * hr, hr), :]


def _gather_copy(bufs, sends, recvs, b, k, chip, pc, to):
    piece = _half(bufs[b], 2 * chip[0] + chip[1], pc)
    return pltpu.make_async_remote_copy(src_ref=piece, dst_ref=piece, send_sem=sends.at[4 * b + k], recv_sem=recvs.at[4 * b + k],
                                        device_id=to, device_id_type=MESH)


def _split_call(body, bufs, sems_in, n_sems, *, name, after=(), token=False):
    nb, na, starts = len(bufs), len(after), not sems_in

    def wrapped(*refs):
        sems = refs[nb + na:nb + na + 2] if starts else refs[nb:nb + 2]
        body(refs[:nb], sems[0], sems[1])
        if token:
            refs[-1][...] = jnp.zeros_like(refs[-1])

    out_shape = [pltpu.SemaphoreType.DMA((n_sems,)) for _ in range(2 if starts else 0)]
    out_specs = [SEM] * len(out_shape) + [HBM] * nb
    alias = {b: len(out_shape) + b for b in range(nb)}
    out_shape += [pltpu.HBM(b.shape, b.dtype) for b in bufs]
    if token:
        out_shape.append(jax.ShapeDtypeStruct((8, LANES), F32))
        out_specs.append(pl.BlockSpec(memory_space=pltpu.VMEM))
    return pl.pallas_call(
        wrapped, out_shape=out_shape, in_specs=[HBM] * nb + [SEM] * len(sems_in) + [ANY] * na, out_specs=out_specs,
        input_output_aliases=alias,
        compiler_params=pltpu.CompilerParams(has_side_effects=pltpu.SideEffectType.DATAFLOW_SIDE_EFFECTING),
        name=name)(*[pltpu.with_memory_space_constraint(b, pltpu.HBM) for b in bufs], *sems_in, *after)


def _gather_start(bufs, *, name, after=()):
    nb = len(bufs)

    def body(ins, sends, recvs):
        x, y, c = lax.axis_index("x"), lax.axis_index("y"), lax.axis_index("c")
        chips = [(1 - x, y), (x, 1 - y), (1 - x, 1 - y)]
        for b in range(nb):
            _gather_copy(ins, sends, recvs, b, 0, (x, y), c, (x, y, 1 - c)).start()
            for j, chip in enumerate(chips):
                _gather_copy(ins, sends, recvs, b, 1 + j, (x, y), c, (*chip, c)).start()

    out = _split_call(body, bufs, (), 4 * nb, name=name, after=after, token=True)
    return (out[0], out[1], out[2:2 + nb]), out[-1]


def _gather_wait_first(flight, *, name, after=()):
    sends, recvs, bufs = flight
    nb = len(bufs)

    def body(ins, sends_, recvs_):
        x, y, c = lax.axis_index("x"), lax.axis_index("y"), lax.axis_index("c")
        chips = [(1 - x, y), (x, 1 - y), (1 - x, 1 - y)]
        for b in range(nb):
            _gather_copy(ins, sends_, recvs_, b, 0, (x, y), c, (x, y, 1 - c)).wait_send()
            _gather_copy(ins, sends_, recvs_, b, 0, (x, y), 1 - c, (x, y, c)).wait_recv()
            for j, chip in enumerate(chips):
                _gather_copy(ins, sends_, recvs_, b, 1 + j, (x, y), c, (*chip, c)).wait_send()
                _gather_copy(ins, sends_, recvs_, b, 1 + j, chip, c, (x, y, c)).wait_recv()

    return _split_call(body, bufs, (sends, recvs), 4 * nb, name=name, after=after)


def _gather_forward(bufs, *, name):
    nb = len(bufs)

    def body(ins, sends, recvs):
        x, y, c = lax.axis_index("x"), lax.axis_index("y"), lax.axis_index("c")
        chips = [(1 - x, y), (x, 1 - y), (1 - x, 1 - y)]
        for b in range(nb):
            for j, chip in enumerate(chips):
                _gather_copy(ins, sends, recvs, b, 1 + j, chip, c, (x, y, 1 - c)).start()

    out = _split_call(body, bufs, (), 4 * nb, name=name)
    return out[0], out[1], out[2:2 + nb]


def _gather_wait_forward(flight, *, name, after=()):
    sends, recvs, bufs = flight
    nb = len(bufs)

    def body(ins, sends_, recvs_):
        x, y, c = lax.axis_index("x"), lax.axis_index("y"), lax.axis_index("c")
        chips = [(1 - x, y), (x, 1 - y), (1 - x, 1 - y)]
        for b in range(nb):
            for j, chip in enumerate(chips):
                _gather_copy(ins, sends_, recvs_, b, 1 + j, chip, c, (x, y, 1 - c)).wait_send()
                _gather_copy(ins, sends_, recvs_, b, 1 + j, chip, 1 - c, (x, y, c)).wait_recv()

    return _split_call(body, bufs, (sends, recvs), 4 * nb, name=name, after=after)


def _owner_copies(hs, lands, sends, recvs):
    x, y, c = lax.axis_index("x"), lax.axis_index("y"), lax.axis_index("c")
    chips = [(1 - x, y), (x, 1 - y), (1 - x, 1 - y)]
    return [pltpu.make_async_remote_copy(src_ref=hs[b].at[2 * cx + cy], dst_ref=lands[b].at[j], send_sem=sends.at[3 * b + j],
                                         recv_sem=recvs.at[3 * b + j], device_id=(cx, cy, c), device_id_type=MESH)
            for b in range(len(hs)) for j, (cx, cy) in enumerate(chips)]


def _owners_start(hs, *, name):
    nb = len(hs)
    lands = [lax.empty((3,) + h.shape[1:], h.dtype) for h in hs]

    def body(refs, sends, recvs):
        for cp in _owner_copies(refs[:nb], refs[nb:], sends, recvs):
            cp.start()

    out = _split_call(body, list(hs) + lands, (), 3 * nb, name=name, token=True)
    return (out[0], out[1], out[2:2 + 2 * nb]), out[-1]


def _owners_wait(flight, *, name, after=()):
    sends, recvs, bufs = flight
    nb = len(bufs) // 2

    def body(refs, sends_, recvs_):
        for cp in _owner_copies(refs[:nb], refs[nb:], sends_, recvs_):
            cp.wait()

    return _split_call(body, bufs, (sends, recvs), 3 * nb, name=name, after=after)[nb:]


def _swap_halves_with_sibling(bufs, *, name):
    nb = len(bufs)

    def body(*refs):
        ins, outs, send_sems, recv_sems = refs[:nb], refs[nb:2 * nb], refs[2 * nb], refs[2 * nb + 1]
        x, y, c = lax.axis_index("x"), lax.axis_index("y"), lax.axis_index("c")
        copies = []
        for b in range(nb):
            hr = ins[b].shape[1] // 2
            copies.append(pltpu.make_async_remote_copy(
                src_ref=ins[b].at[:, pl.ds((1 - c) * hr, hr), :], dst_ref=outs[b], send_sem=send_sems.at[b],
                recv_sem=recv_sems.at[b], device_id=(x, y, 1 - c), device_id_type=MESH))
        for cp in copies:
            cp.start()
        for cp in copies:
            cp.wait()

    return pl.pallas_call(
        body, out_shape=[jax.ShapeDtypeStruct((b.shape[0], b.shape[1] // 2, b.shape[2]), b.dtype) for b in bufs],
        in_specs=[ANY] * nb, out_specs=[ANY] * nb,
        scratch_shapes=[pltpu.SemaphoreType.DMA((nb,)), pltpu.SemaphoreType.DMA((nb,))], name=name)(*bufs)


def _swap_results_with_sibling(ts, *, name):
    nb = len(ts)

    def body(*refs):
        outs, send_sems, recv_sems = refs[nb:2 * nb], refs[2 * nb], refs[2 * nb + 1]
        x, y, c = lax.axis_index("x"), lax.axis_index("y"), lax.axis_index("c")
        copies = [pltpu.make_async_remote_copy(src_ref=outs[b].at[c], dst_ref=outs[b].at[c], send_sem=send_sems.at[b],
                                               recv_sem=recv_sems.at[b], device_id=(x, y, 1 - c), device_id_type=MESH)
                  for b in range(nb)]
        for cp in copies:
            cp.start()
        for cp in copies:
            cp.wait()

    return pl.pallas_call(
        body, out_shape=[jax.ShapeDtypeStruct(t.shape, t.dtype) for t in ts], in_specs=[ANY] * nb, out_specs=[ANY] * nb,
        scratch_shapes=[pltpu.SemaphoreType.DMA((nb,)), pltpu.SemaphoreType.DMA((nb,))],
        input_output_aliases={b: b for b in range(nb)}, name=name)(*ts)


def _row_tile(rows, cols):
    best = 16
    for t in range(16, rows + 1, 16):
        if rows % t == 0 and t * cols <= 640 * 1024:
            best = t
    assert rows % best == 0, (rows, cols)
    return best


def _add_sibling_half(g, recv, core, *, name):
    nk, r, n = g.shape
    hr = r // 2
    tr = _row_tile(hr, n)

    def body(c_ref, a_ref, b_ref, o_ref):
        o_ref[...] = (a_ref[...].astype(F32) + b_ref[...].astype(F32)).astype(BF16)

    grid_spec = pltpu.PrefetchScalarGridSpec(
        num_scalar_prefetch=1, grid=(nk, hr // tr),
        in_specs=[pl.BlockSpec((None, tr, n), lambda k, i, c_ref: (k, c_ref[0] * (hr // tr) + i, 0)),
                  pl.BlockSpec((None, tr, n), lambda k, i, c_ref: (k, i, 0))],
        out_specs=pl.BlockSpec((None, tr, n), lambda k, i, c_ref: (k, i, 0)))
    return pl.pallas_call(body, grid_spec=grid_spec, out_shape=jax.ShapeDtypeStruct((nk, hr, n), BF16),
                          compiler_params=_params(("parallel", "parallel")), name=name)(core, g, recv)


def _add_chip_sums(h, recv, chip_core, *, name):
    _, hr, n = h.shape
    tr = _row_tile(hr, n)

    def body(k_ref, a_ref, b_ref, o_ref):
        o_ref[...] = ((a_ref[...].astype(F32) + b_ref[0].astype(F32)) + b_ref[1].astype(F32)) + b_ref[2].astype(F32)

    grid_spec = pltpu.PrefetchScalarGridSpec(
        num_scalar_prefetch=1, grid=(hr // tr,),
        in_specs=[pl.BlockSpec((None, tr, n), lambda i, k_ref: (k_ref[0], i, 0)),
                  pl.BlockSpec((3, tr, n), lambda i, k_ref: (0, i, 0))],
        out_specs=pl.BlockSpec((None, tr, n), lambda i, k_ref: (k_ref[1], i, 0)))
    return pl.pallas_call(body, grid_spec=grid_spec, out_shape=jax.ShapeDtypeStruct((2, hr, n), F32),
                          compiler_params=_params(("parallel",)), name=name)(chip_core, h, recv)


def _sum_devices(g, *, name):
    nd, r, n = g.shape

    def body(g_ref, o_ref):
        acc = g_ref[0]
        for i in range(1, nd):
            acc = acc + g_ref[i]
        o_ref[...] = acc

    return pl.pallas_call(body, out_shape=jax.ShapeDtypeStruct((r, n), F32), name=name)(g)


def _own_slot(shard, chip):
    return lax.dynamic_update_slice(jnp.zeros((N_CHIPS,) + shard.shape, BF16), shard[None], (chip, 0, 0))


def kernel(x, c, ada_w, ada_b, mix_norm_w, mlp_norm_w, mlp_up, mlp_down, ssd_in_w, ssd_conv_w, ssd_conv_b, ssd_dt_bias, ssd_A_log, ssd_D, ssd_norm_w, ssd_out_w, sc_in_w, sc_conv_w, sc_out_w, final_norm_w, loss_target, m_ada_w, m_ada_b, m_mix_norm_w, m_mlp_norm_w, m_mlp_up, m_mlp_down, m_ssd_in_w, m_ssd_conv_w, m_ssd_conv_b, m_ssd_dt_bias, m_ssd_A_log, m_ssd_D, m_ssd_norm_w, m_ssd_out_w, m_sc_in_w, m_sc_conv_w, m_sc_out_w, m_final_norm_w, v_ada_w, v_ada_b, v_mix_norm_w, v_mlp_norm_w, v_mlp_up, v_mlp_down, v_ssd_in_w, v_ssd_conv_w, v_ssd_conv_b, v_ssd_dt_bias, v_ssd_A_log, v_ssd_D, v_ssd_norm_w, v_ssd_out_w, v_sc_in_w, v_sc_conv_w, v_sc_out_w, v_final_norm_w):
    xi, yi, ci = lax.axis_index("x"), lax.axis_index("y"), lax.axis_index("c")
    chip = 2 * xi + yi
    dev = 2 * chip + ci
    n_ada = ada_w.shape[2]

    bf = lambda v: v.astype(BF16)
    up_row, down_row, sc_out_row = 0, D, 2 * D
    a_bufs = [_own_slot(bf(ssd_in_w[0]), chip), _own_slot(bf(ssd_out_w[0]), chip)]
    b_bufs = [_own_slot(bf(jnp.concatenate([mlp_up[0], mlp_down[0]], axis=0)), chip)]
    c_bufs = [_own_slot(bf(sc_in_w[0]), chip), _own_slot(bf(jnp.concatenate([mlp_up[1], mlp_down[1], sc_out_w[0]], axis=0)), chip)]
    fly_a, tok = _gather_start(a_bufs, name="gather_a_start")
    fly_b, tok = _gather_start(b_bufs, name="gather_b_start", after=(tok,))
    fly_c, tok = _gather_start(c_bufs, name="gather_c_start", after=(tok,))

    conv_flat = jnp.concatenate([ssd_conv_w.reshape(-1), sc_conv_w.reshape(-1), jnp.zeros((256,), F32)]).reshape(4, D)
    blk0 = jnp.concatenate([c + tok[0:1, 0:1], conv_flat, jnp.zeros((3, D), F32)], axis=0)
    got0 = _all_gather_rows(blk0, name="gather_cond").reshape(N_DEV, 8, D)
    c_all = got0[:, 0]
    conv_all = got0[0::2, 1:5].reshape(N_CHIPS, 4 * D)
    ssd_conv = jnp.moveaxis(conv_all[:, :4 * 768].reshape(N_CHIPS, 4, 768), 0, 1).reshape(4, CONVD)
    sc_conv = jnp.moveaxis(conv_all[:, 4 * 768:4 * 768 + 3 * 256].reshape(N_CHIPS, 3, 256), 0, 1).reshape(3, D)
    mod_shard = [_matmul(c_all, ada_w[i], n=n_ada, a_silu=True,
                         extras=(lax.dynamic_slice(ada_b, (i, chip * n_ada), (1, n_ada)),),
                         epi=lambda acc, b: (acc + b,), name=f"ada_mod{i}") for i in range(2)]
    mod_all = _all_gather_rows(jnp.concatenate(mod_shard, axis=0), name="gather_mod")
    mod_all = mod_all.reshape(N_DEV, 2, N_DEV, n_ada)[0::2]
    mod = jnp.moveaxis(lax.dynamic_index_in_dim(mod_all, dev, axis=2, keepdims=False), 0, 1).reshape(2, 6, D)
    mods = [[mod[i, j:j + 1] for j in range(6)] for i in range(2)]

    row = lambda v: v.reshape(1, -1)
    xs, tgt = x[0], loss_target[0]
    prm = jnp.pad(jnp.concatenate([ssd_dt_bias, ssd_A_log, ssd_D, jnp.zeros((5, NH), F32)], axis=0), ((0, 0), (0, LANES - NH)))
    mix_nw = [row(mix_norm_w[i]) for i in range(2)]
    mlp_nw = [row(mlp_norm_w[i]) for i in range(2)]
    a_bufs = _gather_wait_first(fly_a, name="gather_a_landed", after=(mod,))
    w_ssd_in, w_ssd_out = _gather_wait_forward(_gather_forward(a_bufs, name="gather_a_pass"), name="gather_a_done")
    ssd_in_full = jnp.moveaxis(w_ssd_in, 0, 1).reshape(D, N_CHIPS * SSD_IN_SHARD)
    w_zx, w_dt = ssd_in_full[:, :ZX], jnp.pad(ssd_in_full[:, ZX:], ((0, 0), (0, LANES - NH)))
    scan = _ssd_fwd_scan(xs, mods[0][0:3], mix_nw[0], w_zx, w_dt, ssd_conv, ssd_conv_b, prm, "ssd")
    fly_b = _gather_forward(_gather_wait_first(fly_b, name="gather_b_landed", after=(scan[4],)), name="gather_b_pass")
    x1, s_ssd = _ssd_fwd_out(xs, mods[0][0:3], scan, ssd_norm_w, w_ssd_out, "ssd")
    (w_b,) = _gather_wait_forward(fly_b, name="gather_b_done", after=(x1,))
    x2, s_mlp0 = _mlp_fwd(x1, mods[0][3:6], mlp_nw[0], w_b, up_row, down_row, "mlp0")
    c_bufs = _gather_wait_first(fly_c, name="gather_c_landed", after=(x2,))
    w_sc_in, w_c = _gather_wait_forward(_gather_forward(c_bufs, name="gather_c_pass"), name="gather_c_done")
    x3, s_sc = _sc_layer_fwd(x2, mods[1][0:3], mix_nw[1], w_sc_in, sc_conv, w_c, sc_out_row, "sc")
    x4, s_mlp1 = _mlp_fwd(x3, mods[1][3:6], mlp_nw[1], w_c, up_row, down_row, "mlp1")

    core = ci.reshape(1).astype(jnp.int32)
    chip_core = jnp.stack([chip, ci]).astype(jnp.int32)

    def reduce_start(gbufs, tag):
        sib = _swap_halves_with_sibling(gbufs, name=tag + "_sibling")
        hs = [_add_sibling_half(g, s, core, name=f"{tag}_add_sibling{b}") for b, (g, s) in enumerate(zip(gbufs, sib))]
        return _owners_start(hs, name=tag + "_owners_start")

    def reduce_finish(flight, tag, after):
        nb = len(flight[2]) // 2
        lands = _owners_wait(flight, name=tag + "_owners_landed", after=after)
        ts = [_add_chip_sums(h, o, chip_core, name=f"{tag}_add_chips{b}") for b, (h, o) in enumerate(zip(flight[2][:nb], lands))]
        return [t.reshape(-1, t.shape[2]) for t in _swap_results_with_sibling(ts, name=tag + "_result")]

    dx4, fsum = _final_loss(x4, row(final_norm_w), tgt, name="final_loss")
    dx3, g_c, sum_mlp1 = _mlp_bwd(dx4, s_mlp1, mods[1][3:6], mlp_nw[1], w_c, None, up_row, down_row, "mlp1")
    dx2, g_c, g_sc_in, sum_sc, sc_csum = _sc_layer_bwd(dx3, s_sc, mods[1][0:3], mix_nw[1], w_sc_in, sc_conv, w_c, g_c,
                                                       sc_out_row, "sc")
    dx1, g_b, sum_mlp0 = _mlp_bwd(dx2, s_mlp0, mods[0][3:6], mlp_nw[0], w_b, None, up_row, down_row, "mlp0")
    dyn, g_ssd_out, gsum_ssd = _ssd_bwd_out(dx1, s_ssd, mods[0][0:3], w_ssd_out, "ssd")
    fly_1, tok = reduce_start([g_c, g_sc_in, g_b, g_ssd_out], "rs1")
    grad_x, d_w_zx, d_w_dt, sum_ssd, csum, gnsum, ssum = _ssd_bwd_rest(
        dx1, dyn, gsum_ssd, s_ssd, mods[0][0:3], mix_nw[0], w_zx, w_dt, ssd_conv, ssd_conv_b, prm,
        ssd_norm_w + tok[0:1, 0:1], "ssd")
    t_c, t_sc_in, t_b, t_ssd_out = reduce_finish(fly_1, "rs1", (grad_x,))

    def ssd_in_owner(k):
        lo, hi = k * SSD_IN_SHARD, (k + 1) * SSD_IN_SHARD
        if hi <= ZX:
            return d_w_zx[:, lo:hi]
        return jnp.concatenate([d_w_zx[:, lo:], d_w_dt[:, :hi - ZX]], axis=1)

    fly_2, tok = reduce_start([jnp.stack([ssd_in_owner(k) for k in range(N_CHIPS)]).astype(BF16)], "rs2")

    small = jnp.concatenate([sum_ssd + tok[0:1, 0:1], sum_mlp0, sum_sc, sum_mlp1, csum.reshape(24, D), gnsum.reshape(16, D),
                             fsum, sc_csum, jnp.pad(ssum, ((0, 0), (0, D - LANES)))], axis=0)
    small_all = _all_gather_rows(small, name="gather_small").reshape(N_DEV, SMALL_ROWS, D)
    tot = _sum_devices(small_all, name="sum_small")
    loss = tot[FINAL_ROW + 1, 0]
    mod_rows = [r + o for r in SUB_ROW for o in (3, 2, 0)]
    g_ada_b = jnp.stack([tot[r] for r in mod_rows]).reshape(2, 6 * D)
    g_mix_norm = jnp.stack([tot[SUB_ROW[0] + 1], tot[SUB_ROW[2] + 1]])
    g_mlp_norm = jnp.stack([tot[SUB_ROW[1] + 1], tot[SUB_ROW[3] + 1]])
    conv_sums = tot[SSD_CONV_ROW:SSD_CONV_ROW + 24].reshape(8, CONVD)
    g_ssd_conv_w = lax.dynamic_slice(conv_sums, (0, chip * 768), (4, 768))[None]
    g_ssd_conv_b = conv_sums[4:5]
    g_ssd_norm = tot[GNORM_ROW:GNORM_ROW + 2].reshape(1, DI)
    g_final = tot[FINAL_ROW]
    g_sc_conv_w = lax.dynamic_slice(tot[SC_CONV_ROW:SC_CONV_ROW + 3], (0, chip * 256), (3, 256))[None]
    g_a_log, g_d, g_dt_bias = (tot[HEAD_ROW + r:HEAD_ROW + r + 1, 0:NH] for r in range(3))
    c_pad = jnp.concatenate([c_all, jnp.zeros((8, D), F32)], axis=0)
    dmod_all = jnp.stack([small_all[:, r] for r in mod_rows], axis=1).reshape(N_DEV, 2, 6 * D)
    g_ada_w = []
    for i in range(2):
        dm = lax.dynamic_slice(dmod_all[:, i], (0, chip * n_ada), (N_DEV, n_ada))
        g_ada_w.append(_matmul_tn(c_pad, jnp.concatenate([dm, jnp.zeros_like(dm)], axis=0), m=D, n=n_ada, a_silu=True,
                                  name=f"ada_dw{i}"))
    g_ada_w = jnp.stack(g_ada_w)

    big = dict(mlp_up=[(t_b, up_row), (t_c, up_row)], mlp_down=[(t_b, down_row), (t_c, down_row)],
               ssd_out_w=[(t_ssd_out, 0)], sc_out_w=[(t_c, sc_out_row)], sc_in_w=[(t_sc_in, 0)], ssd_in_w=None)
    grads = dict(ada_w=g_ada_w, ada_b=g_ada_b, mix_norm_w=g_mix_norm, mlp_norm_w=g_mlp_norm, ssd_conv_w=g_ssd_conv_w,
                 ssd_conv_b=g_ssd_conv_b, ssd_dt_bias=g_dt_bias, ssd_A_log=g_a_log, ssd_D=g_d, ssd_norm_w=g_ssd_norm,
                 sc_conv_w=g_sc_conv_w, final_norm_w=g_final)
    weights = dict(ada_w=(ada_w, m_ada_w, v_ada_w), ada_b=(ada_b, m_ada_b, v_ada_b),
                   mix_norm_w=(mix_norm_w, m_mix_norm_w, v_mix_norm_w), mlp_norm_w=(mlp_norm_w, m_mlp_norm_w, v_mlp_norm_w),
                   mlp_up=(mlp_up, m_mlp_up, v_mlp_up), mlp_down=(mlp_down, m_mlp_down, v_mlp_down),
                   ssd_in_w=(ssd_in_w, m_ssd_in_w, v_ssd_in_w), ssd_conv_w=(ssd_conv_w, m_ssd_conv_w, v_ssd_conv_w),
                   ssd_conv_b=(ssd_conv_b, m_ssd_conv_b, v_ssd_conv_b), ssd_dt_bias=(ssd_dt_bias, m_ssd_dt_bias, v_ssd_dt_bias),
                   ssd_A_log=(ssd_A_log, m_ssd_A_log, v_ssd_A_log), ssd_D=(ssd_D, m_ssd_D, v_ssd_D),
                   ssd_norm_w=(ssd_norm_w, m_ssd_norm_w, v_ssd_norm_w), ssd_out_w=(ssd_out_w, m_ssd_out_w, v_ssd_out_w),
                   sc_in_w=(sc_in_w, m_sc_in_w, v_sc_in_w), sc_conv_w=(sc_conv_w, m_sc_conv_w, v_sc_conv_w),
                   sc_out_w=(sc_out_w, m_sc_out_w, v_sc_out_w), final_norm_w=(final_norm_w, m_final_norm_w, v_final_norm_w))
    def step(nm, parts):
        w, m, v = (t.reshape(-1, t.shape[-1]) for t in weights[nm])
        rows, outs = w.shape[0] // len(parts), None
        for i, (gbuf, g_row) in enumerate(parts):
            outs = _adamw(w, gbuf, m, v, g_row=g_row, w_row=i * rows, rows=rows, into=outs, emit_g=True, name=f"adamw_{nm}{i}")
        return outs

    res = {}
    for nm, (w, m, v) in weights.items():
        two_d = (-1, w.shape[-1]) if w.ndim > 1 else (1, -1)
        if nm not in big:
            res[nm] = (grads[nm], *_adamw(w.reshape(two_d), grads[nm].reshape(two_d), m.reshape(two_d), v.reshape(two_d),
                                          name="adamw_" + nm))
        elif big[nm] is not None:
            res[nm] = step(nm, big[nm])
    (t_ssd_in,) = reduce_finish(fly_2, "rs2", (res["sc_out_w"][1],))
    res["ssd_in_w"] = step("ssd_in_w", [(t_ssd_in, 0)])
    outs = [[res[nm][k].reshape(weights[nm][0].shape) for nm in weights] for k in range(4)]
    return (loss, grad_x[None], *outs[0], *outs[1], *outs[2], *outs[3])
```

```python
import jax
import jax.numpy as jnp
from jax import lax
from jax.experimental import pallas as pl
from jax.experimental.pallas import tpu as pltpu

F32 = jnp.float32
BF16 = jnp.bfloat16
MESH = pl.DeviceIdType.MESH
HIGHEST = lax.Precision.HIGHEST

D = 1024
DFF = 4096
DI = 2048
NH = 32
HP = 64
NG = 4
NS = 128
CH = 128
CONVD = DI + 2 * NG * NS
ZX = DI + CONVD
GW = NG * NS
LANES = 128
N_CHIPS = 4
N_DEV = 8
EPS = 1e-5
ADAM_LR, ADAM_B1, ADAM_B2, ADAM_EPS, ADAM_WD, ADAM_STEP = 1e-3, 0.9, 0.999, 1e-8, 0.01, 10
VMEM_LIMIT = 48 * 1024 * 1024
TM_ALL = 2048
TM_HALF = 1024
ANY = pl.BlockSpec(memory_space=pl.ANY)
HBM = pl.BlockSpec(memory_space=pltpu.HBM)
SEM = pl.BlockSpec(memory_space=pltpu.SEMAPHORE)

SSD_IN_SHARD = 1288
SC_IN_SHARD = 768


def _params(sem=None):
    return pltpu.CompilerParams(dimension_semantics=sem, vmem_limit_bytes=VMEM_LIMIT)


def _sigmoid(v):
    return 1.0 / (1.0 + jnp.exp(-v))


def _dot(a, b, dims=((1,), (0,)), precision=None):
    return lax.dot_general(a, b, (dims, ((), ())), preferred_element_type=F32, precision=precision)


def _dot_nt(a, b):
    return _dot(a, b, ((1,), (1,)))


def _dot_tn(a, b):
    return _dot(a, b, ((0,), (0,)))


def _nn(av, bv):
    return _dot(av.astype(BF16), bv.astype(BF16))


def _nt(av, bv):
    return _dot_nt(av.astype(BF16), bv.astype(BF16))


def _nn_split(av, bv):
    r = bv.shape[1]
    acc = _dot(av[:, 0:r].astype(BF16), bv[0])
    for s in range(1, bv.shape[0]):
        acc = acc + _dot(av[:, s * r:(s + 1) * r].astype(BF16), bv[s])
    return acc


def _nt_split(av, bv):
    kc = bv.shape[2]
    acc = _dot_nt(av[:, 0:kc].astype(BF16), bv[0])
    for s in range(1, bv.shape[0]):
        acc = acc + _dot_nt(av[:, s * kc:(s + 1) * kc].astype(BF16), bv[s])
    return acc


def _nt_sc_in(av, bv):
    q = 256
    acc = None
    for i in range(3 * D // q):
        a_blk = av[i // 4][:, (i % 4) * q:(i % 4 + 1) * q]
        b_blk = bv[i // 3][:, (i % 3) * q:(i % 3 + 1) * q]
        t = _dot_nt(a_blk, b_blk)
        acc = t if acc is None else acc + t
    return acc


def _matmul(a, b, *, name, n, contract=_nn, a_spec=None, b_spec=None, tm=512, tn=512, extras=(), epi=None,
            out_dtypes=(F32,), a_silu=False):
    M = a.shape[-2]
    tm, tn = min(tm, M), min(tn, n)
    assert M % tm == 0 and n % tn == 0, (name, M, n, tm, tn)
    n_ex = len(extras)
    if a_spec is None:
        a_spec = pl.BlockSpec((tm, a.shape[1]), lambda i, j: (i, 0))
    if b_spec is None:
        b_spec = (pl.BlockSpec((tn, b.shape[1]), lambda i, j: (j, 0)) if contract is _nt
                  else pl.BlockSpec((b.shape[0], tn), lambda i, j: (0, j)))

    def body(*refs):
        av = refs[0][...]
        if a_silu:
            av = av * _sigmoid(av)
        acc = contract(av, refs[1][...])
        res = epi(acc, *[r[...] for r in refs[2:2 + n_ex]]) if epi is not None else (acc,)
        for o_ref, r in zip(refs[2 + n_ex:], res, strict=True):
            o_ref[...] = r.astype(o_ref.dtype)

    in_specs = [a_spec, b_spec]
    for e in extras:
        in_specs.append(pl.BlockSpec((1, tn), lambda i, j: (0, j)) if e.shape[0] == 1 and M != 1
                        else pl.BlockSpec((tm, tn), lambda i, j: (i, j)))
    outs = pl.pallas_call(
        body, grid=(M // tm, n // tn), in_specs=in_specs,
        out_specs=[pl.BlockSpec((tm, tn), lambda i, j: (i, j)) for _ in out_dtypes],
        out_shape=[jax.ShapeDtypeStruct((M, n), dt) for dt in out_dtypes],
        compiler_params=_params(("parallel", "parallel")), name=name)(a, b, *extras)
    return outs if len(out_dtypes) > 1 else outs[0]


def _matmul_tn(a, b, *, name, m, n, tm=512, tn=512, a_spec=None, b_spec=None, out_spec=None, out_struct=None, into=None,
               a_silu=False):
    T = a.shape[-2]
    tm, tn = min(tm, m), min(tn, n)
    assert m % tm == 0 and n % tn == 0, (name, m, n, tm, tn)
    if a_spec is None:
        a_spec = pl.BlockSpec((T, tm), lambda i, j: (0, i))
    if b_spec is None:
        b_spec = pl.BlockSpec((T, tn), lambda i, j: (0, j))
    if out_spec is None:
        out_spec, out_struct = pl.BlockSpec((tm, tn), lambda i, j: (i, j)), jax.ShapeDtypeStruct((m, n), F32)

    def body(a_ref, b_ref, *rest):
        av = a_ref[...]
        if a_silu:
            av = av * _sigmoid(av)
        rest[-1][...] = _dot_tn(av.astype(BF16), b_ref[...].astype(BF16)).astype(rest[-1].dtype)

    args, in_specs, alias = [a, b], [a_spec, b_spec], {}
    if into is not None:
        args, in_specs, alias = args + [into], in_specs + [ANY], {2: 0}
    return pl.pallas_call(body, grid=(m // tm, n // tn), in_specs=in_specs, out_specs=out_spec, out_shape=out_struct,
                          input_output_aliases=alias, compiler_params=_params(("parallel", "parallel")), name=name)(*args)


def _modnorm_fwd(x, nw, sc, sh, *, name):
    L = x.shape[0]
    tm = min(L, 512)

    def body(x_ref, nw_ref, sc_ref, sh_ref, h_ref):
        xv = x_ref[...]
        r = lax.rsqrt(jnp.mean(xv * xv, axis=-1, keepdims=True) + EPS)
        h_ref[...] = ((xv * r * nw_ref[...]) * (1.0 + sc_ref[...]) + sh_ref[...]).astype(BF16)

    row = pl.BlockSpec((tm, D), lambda i: (i, 0))
    vec = pl.BlockSpec((1, D), lambda i: (0, 0))
    return pl.pallas_call(body, grid=(L // tm,), in_specs=[row, vec, vec, vec], out_specs=row,
                          out_shape=jax.ShapeDtypeStruct((L, D), BF16),
                          compiler_params=_params(("parallel",)), name=name)(x, nw, sc, sh)


def _modnorm_bwd(x, dh, dxo, nw, sc, gsum, *, name):
    L = x.shape[0]
    tm = min(L, 256)

    def body(x_ref, dh_ref, dxo_ref, nw_ref, sc_ref, g_ref, dx_ref, s_ref):
        @pl.when(pl.program_id(0) == 0)
        def _():
            s_ref[...] = g_ref[...]

        xv, dhv = x_ref[...], dh_ref[...]
        r = lax.rsqrt(jnp.mean(xv * xv, axis=-1, keepdims=True) + EPS)
        xhat = xv * r
        dxhat = dhv * (nw_ref[...] * (1.0 + sc_ref[...]))
        dx_ref[...] = dxo_ref[...] + r * (dxhat - xhat * jnp.mean(dxhat * xhat, axis=-1, keepdims=True))
        s_ref[1:2, :] += jnp.sum(dhv * xhat, axis=0, keepdims=True) * (1.0 + sc_ref[...])
        s_ref[2:3, :] += jnp.sum(dhv * xhat, axis=0, keepdims=True) * nw_ref[...]
        s_ref[3:4, :] += jnp.sum(dhv, axis=0, keepdims=True)

    row = pl.BlockSpec((tm, D), lambda i: (i, 0))
    vec = pl.BlockSpec((1, D), lambda i: (0, 0))
    blk = pl.BlockSpec((8, D), lambda i: (0, 0))
    return pl.pallas_call(body, grid=(L // tm,), in_specs=[row, row, row, vec, vec, blk], out_specs=[row, blk],
                          out_shape=[jax.ShapeDtypeStruct((L, D), F32), jax.ShapeDtypeStruct((8, D), F32)],
                          compiler_params=_params(("arbitrary",)), name=name)(x, dh, dxo, nw, sc, gsum)


def _gate_bwd(dxo, y, g, *, name):
    L = dxo.shape[0]
    tm = min(L, 512)

    def body(dxo_ref, y_ref, g_ref, dy_ref, s_ref):
        @pl.when(pl.program_id(0) == 0)
        def _():
            s_ref[...] = jnp.zeros_like(s_ref)

        dv = dxo_ref[...]
        dy_ref[...] = (dv * g_ref[...]).astype(BF16)
        s_ref[0:1, :] += jnp.sum(dv * y_ref[...], axis=0, keepdims=True)

    row = pl.BlockSpec((tm, D), lambda i: (i, 0))
    return pl.pallas_call(body, grid=(L // tm,), in_specs=[row, row, pl.BlockSpec((1, D), lambda i: (0, 0))],
                          out_specs=[row, pl.BlockSpec((8, D), lambda i: (0, 0))],
                          out_shape=[jax.ShapeDtypeStruct((L, D), BF16), jax.ShapeDtypeStruct((8, D), F32)],
                          compiler_params=_params(("arbitrary",)), name=name)(dxo, y, g)


def _final_loss(x, fw, tgt, *, name):
    L = x.shape[0]
    tm = min(L, 256)

    def body(x_ref, fw_ref, t_ref, dx_ref, s_ref):
        @pl.when(pl.program_id(0) == 0)
        def _():
            s_ref[...] = jnp.zeros_like(s_ref)

        xv = x_ref[...]
        r = lax.rsqrt(jnp.mean(xv * xv, axis=-1, keepdims=True) + EPS)
        xhat = xv * r
        diff = xhat * fw_ref[...] - t_ref[...]
        dout = diff * (1.0 / D)
        dxhat = dout * fw_ref[...]
        dx_ref[...] = r * (dxhat - xhat * jnp.mean(dxhat * xhat, axis=-1, keepdims=True))
        s_ref[0:1, :] += jnp.sum(dout * xhat, axis=0, keepdims=True)
        s_ref[1:2, :] += jnp.zeros((1, D), F32) + 0.5 * jnp.sum(jnp.sum(diff * diff, axis=-1, keepdims=True) * (1.0 / D))

    row = pl.BlockSpec((tm, D), lambda i: (i, 0))
    return pl.pallas_call(body, grid=(L // tm,), in_specs=[row, pl.BlockSpec((1, D), lambda i: (0, 0)), row],
                          out_specs=[row, pl.BlockSpec((8, D), lambda i: (0, 0))],
                          out_shape=[jax.ShapeDtypeStruct((L, D), F32), jax.ShapeDtypeStruct((8, D), F32)],
                          compiler_params=_params(("arbitrary",)), name=name)(x, fw, tgt)


def _shift_down(v, j):
    if j == 0:
        return v
    row = lax.broadcasted_iota(jnp.int32, v.shape, 0)
    return jnp.where(row >= j, pltpu.roll(v, j, 0), 0.0)


def _shift_up(v, j):
    if j == 0:
        return v
    n = v.shape[0]
    row = lax.broadcasted_iota(jnp.int32, v.shape, 0)
    return jnp.where(row < n - j, pltpu.roll(v, n - j, 0), 0.0)


def _ssd_conv_fwd(zx, w, b, *, name):
    L = zx.shape[0]
    cb = 256
    k = w.shape[0]

    def body(x_ref, w_ref, b_ref, o_ref):
        xv = x_ref[...]
        pre = b_ref[...] + xv * w_ref[k - 1:k, :]
        for j in range(1, k):
            pre = pre + _shift_down(xv, j) * w_ref[k - 1 - j:k - j, :]
        o_ref[...] = pre * _sigmoid(pre)

    return pl.pallas_call(
        body, grid=(CONVD // cb,),
        in_specs=[pl.BlockSpec((L, cb), lambda i: (0, i + DI // cb)), pl.BlockSpec((k, cb), lambda i: (0, i)),
                  pl.BlockSpec((1, cb), lambda i: (0, i))],
        out_specs=pl.BlockSpec((L, cb), lambda i: (0, i)), out_shape=jax.ShapeDtypeStruct((L, CONVD), F32),
        compiler_params=_params(("parallel",)), name=name)(zx, w, b)


def _ssd_conv_bwd(zx, dact, w, b, dzx, *, name):
    L = zx.shape[0]
    cb = 256
    k = w.shape[0]

    def body(x_ref, da_ref, w_ref, b_ref, _, dx_ref, s_ref):
        xv = x_ref[...]
        sh = [_shift_down(xv, j) for j in range(k)]
        pre = b_ref[...] + sh[0] * w_ref[k - 1:k, :]
        for j in range(1, k):
            pre = pre + sh[j] * w_ref[k - 1 - j:k - j, :]
        s = _sigmoid(pre)
        dpre = da_ref[...] * (s * (1.0 + pre * (1.0 - s)))
        dx = dpre * w_ref[k - 1:k, :]
        for j in range(1, k):
            dx = dx + _shift_up(dpre, j) * w_ref[k - 1 - j:k - j, :]
        dx_ref[...] = dx.astype(BF16)
        s_ref[...] = jnp.zeros_like(s_ref)
        for j in range(k):
            s_ref[k - 1 - j:k - j, :] = jnp.sum(dpre * sh[j], axis=0, keepdims=True)
        s_ref[k:k + 1, :] = jnp.sum(dpre, axis=0, keepdims=True)

    return pl.pallas_call(
        body, grid=(CONVD // cb,),
        in_specs=[pl.BlockSpec((L, cb), lambda i: (0, i + DI // cb)), pl.BlockSpec((L, cb), lambda i: (0, i)),
                  pl.BlockSpec((k, cb), lambda i: (0, i)), pl.BlockSpec((1, cb), lambda i: (0, i)), ANY],
        out_specs=[pl.BlockSpec((L, cb), lambda i: (0, i + DI // cb)), pl.BlockSpec((8, cb), lambda i: (0, i))],
        out_shape=[jax.ShapeDtypeStruct((L, ZX), BF16), jax.ShapeDtypeStruct((8, CONVD), F32)],
        input_output_aliases={4: 0}, compiler_params=_params(("parallel",)), name=name)(zx, dact, w, b, dzx)


def _sc_fwd(proj, w, *, name):
    L = proj.shape[0]
    cb = 256
    nb = D // cb
    k = w.shape[0]

    def body(b_ref, c_ref, x_ref, w_ref, o_ref):
        u = c_ref[...] * x_ref[...]
        v = u * w_ref[k - 1:k, :]
        for j in range(1, k):
            v = v + _shift_down(u, j) * w_ref[k - 1 - j:k - j, :]
        o_ref[...] = (b_ref[...] * v).astype(BF16)

    return pl.pallas_call(
        body, grid=(nb,),
        in_specs=[pl.BlockSpec((L, cb), lambda i: (0, i)), pl.BlockSpec((L, cb), lambda i: (0, i + nb)),
                  pl.BlockSpec((L, cb), lambda i: (0, i + 2 * nb)), pl.BlockSpec((k, cb), lambda i: (0, i))],
        out_specs=pl.BlockSpec((L, cb), lambda i: (0, i)), out_shape=jax.ShapeDtypeStruct((L, D), BF16),
        compiler_params=_params(("parallel",)), name=name)(proj, proj, proj, w)


def _sc_bwd(proj, dyv, w, *, name):
    L = proj.shape[0]
    cb = 256
    nb = D // cb
    k = w.shape[0]

    def body(b_ref, c_ref, x_ref, dy_ref, w_ref, dp_ref, s_ref):
        cv, xv = c_ref[...], x_ref[...]
        u = cv * xv
        sh = [_shift_down(u, j) for j in range(k)]
        v = sh[0] * w_ref[k - 1:k, :]
        for j in range(1, k):
            v = v + sh[j] * w_ref[k - 1 - j:k - j, :]
        dyv_ = dy_ref[...]
        dp_ref[0] = (dyv_ * v).astype(BF16)
        dv = dyv_ * b_ref[...]
        du = dv * w_ref[k - 1:k, :]
        for j in range(1, k):
            du = du + _shift_up(dv, j) * w_ref[k - 1 - j:k - j, :]
        dp_ref[1] = (du * xv).astype(BF16)
        dp_ref[2] = (du * cv).astype(BF16)
        s_ref[...] = jnp.zeros_like(s_ref)
        for j in range(k):
            s_ref[k - 1 - j:k - j, :] = jnp.sum(dv * sh[j], axis=0, keepdims=True)

    blk = pl.BlockSpec((L, cb), lambda i: (0, i))
    return pl.pallas_call(
        body, grid=(nb,),
        in_specs=[blk, pl.BlockSpec((L, cb), lambda i: (0, i + nb)), pl.BlockSpec((L, cb), lambda i: (0, i + 2 * nb)),
                  blk, pl.BlockSpec((k, cb), lambda i: (0, i))],
        out_specs=[pl.BlockSpec((3, L, cb), lambda i: (0, 0, i)), pl.BlockSpec((8, cb), lambda i: (0, i))],
        out_shape=[jax.ShapeDtypeStruct((3, L, D), BF16), jax.ShapeDtypeStruct((8, D), F32)],
        compiler_params=_params(("parallel",)), name=name)(proj, proj, proj, dyv, w)


def _ssd_chunk_terms(dtr, prm):
    lane = lax.broadcasted_iota(jnp.int32, (CH, LANES), 1)
    valid = lane < NH
    xdt = dtr + prm[0:1, :]
    dt = jnp.where(valid, jnp.maximum(xdt, 0.0) + jnp.log1p(jnp.exp(-jnp.abs(xdt))), 0.0)
    A = -jnp.exp(prm[1:2, :])
    ri = lax.broadcasted_iota(jnp.int32, (CH, CH), 0)
    ci = lax.broadcasted_iota(jnp.int32, (CH, CH), 1)
    cs = _dot((ri >= ci).astype(F32), dt * A, precision=HIGHEST)
    last = cs[CH - 1:CH, :]
    ex = (lax.broadcasted_iota(jnp.int32, (LANES, DI), 1) // HP == lax.broadcasted_iota(jnp.int32, (LANES, DI), 0)).astype(F32)
    return dict(valid=valid, xdt=xdt, dt=dt, A=A, cs=cs, csT=cs.T, last=last, ri=ri, ci=ci, ex=ex)


def _expand(v, ex):
    if v.shape[0] == 1:
        return _dot(jnp.broadcast_to(v, (8, LANES)), ex, precision=HIGHEST)[0:1, :]
    return _dot(v, ex, precision=HIGHEST)


def _head_sum(v, ex):
    if v.shape[0] == 1:
        return _dot(jnp.broadcast_to(v, (8, DI)), ex, ((1,), (1,)), precision=HIGHEST)[0:1, :]
    return _dot(v, ex, ((1,), (1,)), precision=HIGHEST)


def _ssd_fwd(xbc, dtr, prm, *, name):
    L = xbc.shape[0]
    nc = L // CH

    def body(xbc_ref, dtr_ref, prm_ref, y_ref, sp_ref, st_ref):
        @pl.when(pl.program_id(0) == 0)
        def _():
            st_ref[...] = jnp.zeros_like(st_ref)

        prm_v = prm_ref[...]
        t = _ssd_chunk_terms(dtr_ref[...], prm_v)
        cs, csT, ex, causal = t["cs"], t["csT"], t["ex"], t["ri"] >= t["ci"]
        xs = xbc_ref[:, 0:DI]
        X = xs * _expand(t["dt"], ex)
        Xb = X.astype(BF16)
        Xd = (X * _expand(jnp.exp(t["last"] - cs), ex)).astype(BF16)
        Ex = _expand(jnp.exp(cs), ex)
        cdx = _expand(jnp.exp(t["last"]), ex)
        dskx = _expand(prm_v[2:3, :], ex)
        lane = lax.broadcasted_iota(jnp.int32, (CH, LANES), 1)
        sp_ref[0] = st_ref[...]
        for g in range(NG):
            Bg = xbc_ref[:, DI + g * NS:DI + (g + 1) * NS].astype(BF16)
            Cg = xbc_ref[:, DI + GW + g * NS:DI + GW + (g + 1) * NS].astype(BF16)
            G = _dot_nt(Cg, Bg)
            Sg = st_ref[:, g * GW:(g + 1) * GW]
            yoff = _dot(Cg, Sg.astype(BF16)) * Ex[:, g * GW:(g + 1) * GW]
            for j in range(GW // LANES):
                lo = g * GW + j * LANES
                Xp = Xb[:, lo:lo + LANES]
                yd = []
                for h in (lo // HP, lo // HP + 1):
                    seg = cs[:, h:h + 1] - csT[h:h + 1, :]
                    yd.append(_dot((G * jnp.where(causal, jnp.exp(seg), 0.0)).astype(BF16), Xp))
                y_ref[:, lo:lo + LANES] = (jnp.where(lane < HP, yd[0], yd[1]) + yoff[:, j * LANES:(j + 1) * LANES]
                                           + dskx[:, lo:lo + LANES] * xs[:, lo:lo + LANES])
            st_ref[:, g * GW:(g + 1) * GW] = Sg * cdx[:, g * GW:(g + 1) * GW] + _dot_tn(Bg, Xd[:, g * GW:(g + 1) * GW])

    return pl.pallas_call(
        body, grid=(nc,),
        in_specs=[pl.BlockSpec((CH, CONVD), lambda c: (c, 0)), pl.BlockSpec((CH, LANES), lambda c: (c, 0)),
                  pl.BlockSpec((8, LANES), lambda c: (0, 0))],
        out_specs=[pl.BlockSpec((CH, DI), lambda c: (c, 0)), pl.BlockSpec((1, NS, DI), lambda c: (c, 0, 0))],
        out_shape=[jax.ShapeDtypeStruct((L, DI), F32), jax.ShapeDtypeStruct((nc, NS, DI), F32)],
        scratch_shapes=[pltpu.VMEM((NS, DI), F32)],
        compiler_params=_params(("arbitrary",)), name=name)(xbc, dtr, prm)


def _ssd_bwd(xbc, dtr, prm, dy, sprev, *, name):
    L = xbc.shape[0]
    nc = L // CH

    def body(xbc_ref, dtr_ref, prm_ref, dy_ref, sp_ref, dxbc_ref, ddtr_ref, s_ref, dst_ref, dx_scr, de_scr, dd_scr):
        step = pl.program_id(0)

        @pl.when(step == 0)
        def _():
            dst_ref[...] = jnp.zeros_like(dst_ref)
            s_ref[...] = jnp.zeros_like(s_ref)

        prm_v = prm_ref[...]
        t = _ssd_chunk_terms(dtr_ref[...], prm_v)
        cs, csT, ex, ri, ci = t["cs"], t["csT"], t["ex"], t["ri"], t["ci"]
        E = jnp.exp(cs)
        dec = jnp.exp(t["last"] - cs)
        cd = jnp.exp(t["last"])
        xs = xbc_ref[:, 0:DI]
        dtx = _expand(t["dt"], ex)
        X = xs * dtx
        Xb = X.astype(BF16)
        decx = _expand(dec, ex)
        Xd = (X * decx).astype(BF16)
        Ex = _expand(E, ex)
        cdx = _expand(cd, ex)
        dskx = _expand(prm_v[2:3, :], ex)
        lane = lax.broadcasted_iota(jnp.int32, (CH, LANES), 1)
        dcs = jnp.zeros((CH, LANES), F32)
        dcd_x = []
        for g in range(NG):
            gs = slice(g * GW, (g + 1) * GW)
            Bg = xbc_ref[:, DI + g * NS:DI + (g + 1) * NS].astype(BF16)
            Cg = xbc_ref[:, DI + GW + g * NS:DI + GW + (g + 1) * NS].astype(BF16)
            G = _dot_nt(Cg, Bg)
            GT = _dot_nt(Bg, Cg)
            Sg = sp_ref[0, :, gs]
            Sgb = Sg.astype(BF16)
            dyg = dy_ref[:, gs]
            de_scr[:, gs] = dyg * _dot(Cg, Sgb)
            dYo = (Ex[:, gs] * dyg).astype(BF16)
            dC = _dot_nt(dYo, Sgb)
            dS_in = _dot_tn(Cg, dYo)
            dStg = dst_ref[:, gs]
            dStb = dStg.astype(BF16)
            dXd = _dot(Bg, dStb)
            dB = _dot_nt(Xd[:, gs], dStb)
            dd_scr[:, gs] = dXd * X[:, gs]
            dXst = dXd * decx[:, gs]
            dG = jnp.zeros((CH, CH), F32)
            dGT = jnp.zeros((CH, CH), F32)
            for j in range(GW // LANES):
                lo = g * GW + j * LANES
                Xp = Xb[:, lo:lo + LANES]
                dyp = dy_ref[:, lo:lo + LANES]
                dXp = dXst[:, j * LANES:(j + 1) * LANES]
                for k, h in enumerate((lo // HP, lo // HP + 1)):
                    dyh = jnp.where((lane < HP) if k == 0 else (lane >= HP), dyp, 0.0).astype(BF16)
                    seg = cs[:, h:h + 1] - csT[h:h + 1, :]
                    Lm = jnp.where(ri >= ci, jnp.exp(seg), 0.0)
                    LmT = jnp.where(ci >= ri, jnp.exp(-seg), 0.0)
                    dM = _dot_nt(dyh, Xp)
                    dMT = _dot_nt(Xp, dyh)
                    MT = GT * LmT
                    rs = jnp.sum(dM * (G * Lm), axis=1, keepdims=True) - jnp.sum(dMT * MT, axis=1, keepdims=True)
                    dcs = dcs + jnp.where(lane == h, rs, 0.0)
                    dG = dG + dM * Lm
                    dGT = dGT + dMT * LmT
                    dXp = dXp + _dot(MT.astype(BF16), dyh)
                dx_scr[:, lo:lo + LANES] = dXp
            dxbc_ref[:, DI + g * NS:DI + (g + 1) * NS] = dB + _dot(dGT.astype(BF16), Cg)
            dxbc_ref[:, DI + GW + g * NS:DI + GW + (g + 1) * NS] = dC + _dot(dG.astype(BF16), Bg)
            dcd_x.append(jnp.sum(dStg * Sg, axis=0, keepdims=True))
            dst_ref[:, gs] = dStg * cdx[:, gs] + dS_in
        dX = dx_scr[...]
        dy = dy_ref[...]
        ddec = _head_sum(dd_scr[...], ex)
        dcd = _head_sum(jnp.concatenate(dcd_x, axis=1), ex)
        dcs = dcs + _head_sum(de_scr[...], ex) * E - ddec * dec
        row = lax.broadcasted_iota(jnp.int32, (CH, LANES), 0)
        dcs = dcs + jnp.where(row == CH - 1, jnp.sum(ddec * dec, axis=0, keepdims=True) + dcd * cd, 0.0)
        da = _dot((ci >= ri).astype(F32), dcs, precision=HIGHEST)
        ddt = da * t["A"] + _head_sum(dX * xs, ex)
        ddtr = jnp.where(t["valid"], ddt * _sigmoid(t["xdt"]), 0.0)
        ddtr_ref[...] = ddtr
        dxbc_ref[:, 0:DI] = dX * dtx + dskx * dy
        s_ref[0:1, :] += jnp.sum(da * t["dt"], axis=0, keepdims=True)
        s_ref[1:2, :] += _head_sum(jnp.sum(dy * xs, axis=0, keepdims=True), ex)
        s_ref[2:3, :] += jnp.sum(ddtr, axis=0, keepdims=True)

        @pl.when(step == nc - 1)
        def _():
            s_ref[0:1, :] = s_ref[0:1, :] * t["A"]

    rev = lambda c: (nc - 1 - c, 0)
    return pl.pallas_call(
        body, grid=(nc,),
        in_specs=[pl.BlockSpec((CH, CONVD), rev), pl.BlockSpec((CH, LANES), rev), pl.BlockSpec((8, LANES), lambda c: (0, 0)),
                  pl.BlockSpec((CH, DI), rev), pl.BlockSpec((1, NS, DI), lambda c: (nc - 1 - c, 0, 0))],
        out_specs=[pl.BlockSpec((CH, CONVD), rev), pl.BlockSpec((CH, LANES), rev), pl.BlockSpec((8, LANES), lambda c: (0, 0))],
        out_shape=[jax.ShapeDtypeStruct((L, CONVD), F32), jax.ShapeDtypeStruct((L, LANES), F32),
                   jax.ShapeDtypeStruct((8, LANES), F32)],
        scratch_shapes=[pltpu.VMEM((NS, DI), F32), pltpu.VMEM((CH, DI), F32), pltpu.VMEM((CH, DI), F32),
                        pltpu.VMEM((CH, DI), F32)],
        compiler_params=_params(("arbitrary",)), name=name)(xbc, dtr, prm, dy, sprev)


def _gnorm_fwd(y, zx, nw, *, name):
    L = y.shape[0]
    tm = min(L, 256)

    def body(y_ref, z_ref, nw_ref, o_ref):
        z = z_ref[...]
        yg = y_ref[...] * (z * _sigmoid(z))
        for g in range(NG):
            v = yg[:, g * GW:(g + 1) * GW]
            r = lax.rsqrt(jnp.mean(v * v, axis=-1, keepdims=True) + EPS)
            o_ref[:, g * GW:(g + 1) * GW] = (v * r * nw_ref[:, g * GW:(g + 1) * GW]).astype(BF16)

    row = pl.BlockSpec((tm, DI), lambda i: (i, 0))
    return pl.pallas_call(body, grid=(L // tm,), in_specs=[row, row, pl.BlockSpec((1, DI), lambda i: (0, 0))],
                          out_specs=row, out_shape=jax.ShapeDtypeStruct((L, DI), BF16),
                          compiler_params=_params(("parallel",)), name=name)(y, zx, nw)


def _gnorm_bwd(y, zx, nw, dyn, *, name):
    L = y.shape[0]
    tm = min(L, 256)

    def body(y_ref, z_ref, nw_ref, dyn_ref, dy_ref, dz_ref, s_ref):
        @pl.when(pl.program_id(0) == 0)
        def _():
            s_ref[...] = jnp.zeros_like(s_ref)

        z, yv = z_ref[...], y_ref[...]
        sz = _sigmoid(z)
        gate = z * sz
        dgate_dz = sz * (1.0 + z * (1.0 - sz))
        for g in range(NG):
            gs = slice(g * GW, (g + 1) * GW)
            v = yv[:, gs] * gate[:, gs]
            r = lax.rsqrt(jnp.mean(v * v, axis=-1, keepdims=True) + EPS)
            vhat = v * r
            dn = dyn_ref[:, gs]
            s_ref[0:1, gs] += jnp.sum(dn * vhat, axis=0, keepdims=True)
            dvhat = dn * nw_ref[:, gs]
            dv = r * (dvhat - vhat * jnp.mean(dvhat * vhat, axis=-1, keepdims=True))
            dy_ref[:, gs] = dv * gate[:, gs]
            dz_ref[:, gs] = (dv * yv[:, gs] * dgate_dz[:, gs]).astype(BF16)

    row = pl.BlockSpec((tm, DI), lambda i: (i, 0))
    return pl.pallas_call(body, grid=(L // tm,), in_specs=[row, row, pl.BlockSpec((1, DI), lambda i: (0, 0)), row],
                          out_specs=[row, row, pl.BlockSpec((8, DI), lambda i: (0, 0))],
                          out_shape=[jax.ShapeDtypeStruct((L, DI), F32), jax.ShapeDtypeStruct((L, ZX), BF16),
                                     jax.ShapeDtypeStruct((8, DI), F32)],
                          compiler_params=_params(("arbitrary",)), name=name)(y, zx, nw, dyn)


def _adamw(w, g, m, v, *, name, g_row=0, w_row=0, rows=None, into=None, emit_g=False):
    R, C = w.shape
    rows = R if rows is None else rows
    tr = rows
    while tr * C > 256 * 1024 and tr % 16 == 0:
        tr //= 2
    assert g_row % tr == 0 and w_row % tr == 0, (name, g_row, w_row, tr)
    n_out = 4 if emit_g else 3

    def body(w_ref, g_ref, m_ref, v_ref, *rest):
        outs = rest[-n_out:]
        gv = g_ref[...]
        mn = ADAM_B1 * m_ref[...] + (1.0 - ADAM_B1) * gv
        vn = ADAM_B2 * v_ref[...] + (1.0 - ADAM_B2) * (gv * gv)
        m_hat = mn / (1.0 - ADAM_B1 ** ADAM_STEP)
        v_hat = vn / (1.0 - ADAM_B2 ** ADAM_STEP)
        d_ref, mo_ref, vo_ref = outs[-3:]
        d_ref[...] = -ADAM_LR * (m_hat / (jnp.sqrt(v_hat) + ADAM_EPS) + ADAM_WD * w_ref[...])
        mo_ref[...] = mn
        vo_ref[...] = vn
        if emit_g:
            outs[0][...] = gv

    blk = pl.BlockSpec((tr, C), lambda i: (i + w_row // tr, 0))
    args, in_specs, alias = [w, g, m, v], [blk, pl.BlockSpec((tr, C), lambda i: (i + g_row // tr, 0)), blk, blk], {}
    if into is not None:
        args, in_specs, alias = args + list(into), in_specs + [ANY] * n_out, {4 + k: k for k in range(n_out)}
    return pl.pallas_call(body, grid=(rows // tr,), in_specs=in_specs, out_specs=[blk] * n_out,
                          out_shape=[jax.ShapeDtypeStruct((R, C), F32)] * n_out, input_output_aliases=alias,
                          compiler_params=_params(("parallel",)), name=name)(*args)


def _residual(acc, xv, gv):
    return xv + gv * acc, acc


def _relu2(acc):
    a = jnp.maximum(acc, 0.0)
    return a, a * a


def _like(buf):
    return jax.ShapeDtypeStruct(buf.shape, buf.dtype)


def _mlp_fwd(x, mod, nw, wb, up_row, down_row, tag):
    sh, sc, g = mod
    h = _modnorm_fwd(x, nw, sc, sh, name=tag + "_norm")
    a, act = _matmul(h, wb, n=DFF, tm=TM_ALL, b_spec=pl.BlockSpec((None, D, 512), lambda mi, j: (j // 2, up_row // D, j % 2)),
                     epi=_relu2, out_dtypes=(BF16, BF16), name=tag + "_up")
    xn, y = _matmul(act, wb, n=D, tm=TM_HALF, contract=_nn_split,
                    b_spec=pl.BlockSpec((N_CHIPS, D, 512), lambda mi, j: (0, down_row // D, j)),
                    extras=(x, g), epi=_residual, out_dtypes=(F32, F32), name=tag + "_down")
    return xn, (x, h, a, act, y)


def _mlp_bwd(dxo, saved, mod, nw, wb, gb, up_row, down_row, tag):
    x, h, a, act, y = saved
    sh, sc, g = mod
    dy, gsum = _gate_bwd(dxo, y, g, name=tag + "_dgate")
    du = _matmul(dy, wb, n=DFF, tm=TM_ALL, contract=_nt,
                 b_spec=pl.BlockSpec((None, 512, D), lambda mi, j: (j // 2, down_row // 512 + j % 2, 0)),
                 extras=(a,), epi=lambda acc, av: (acc * (2.0 * av.astype(F32)),), out_dtypes=(BF16,), name=tag + "_dact")
    gb = _matmul_tn(act, dy, m=DFF, n=D, tm=D, tn=D, into=gb, out_struct=_like(wb),
                    out_spec=pl.BlockSpec((None, D, D), lambda mi, j: (mi, down_row // D, 0)), name=tag + "_ddown")
    dh = _matmul(du, wb, n=D, tm=TM_HALF, contract=_nt_split,
                 b_spec=pl.BlockSpec((N_CHIPS, 512, D), lambda mi, j: (0, up_row // 512 + j, 0)), name=tag + "_dh")
    gb = _matmul_tn(h, du, m=D, n=DFF, tm=D, into=gb, out_struct=_like(wb),
                    out_spec=pl.BlockSpec((None, D, 512), lambda mi, j: (j // 2, up_row // D, j % 2)), name=tag + "_dup")
    dx, sums = _modnorm_bwd(x, dh, dxo, nw, sc, gsum, name=tag + "_dnorm")
    return dx, gb, sums


def _ssd_fwd_scan(x, mod, nw, w_zx, w_dt, conv_w, conv_b, prm, tag):
    sh, sc, g = mod
    h = _modnorm_fwd(x, nw, sc, sh, name=tag + "_norm")
    zx = _matmul(h, w_zx, n=ZX, tm=TM_ALL, name=tag + "_in")
    dtr = _matmul(h, w_dt, n=LANES, tm=TM_ALL, name=tag + "_in_dt")
    xbc = _ssd_conv_fwd(zx, conv_w, conv_b, name=tag + "_conv")
    y, sprev = _ssd_fwd(xbc, dtr, prm, name=tag + "_scan")
    return h, zx, dtr, xbc, y, sprev


def _ssd_fwd_out(x, mod, scan, gn_w, w_out, tag):
    sh, sc, g = mod
    h, zx, dtr, xbc, y, sprev = scan
    yn = _gnorm_fwd(y, zx, gn_w, name=tag + "_gnorm")
    xn, yo = _matmul(yn, w_out, n=D, tm=TM_HALF, contract=_nn_split,
                     b_spec=pl.BlockSpec((N_CHIPS, 512, 512), lambda mi, j: (0, 0, j)),
                     extras=(x, g), epi=_residual, out_dtypes=(F32, F32), name=tag + "_out")
    return xn, (x, h, zx, dtr, xbc, y, sprev, yn, yo)


def _ssd_bwd_out(dxo, saved, mod, w_out, tag):
    x, h, zx, dtr, xbc, y, sprev, yn, yo = saved
    sh, sc, g = mod
    dyo, gsum = _gate_bwd(dxo, yo, g, name=tag + "_dgate")
    dyn = _matmul(dyo, w_out, n=DI, tm=TM_ALL, contract=_nt, b_spec=pl.BlockSpec((None, 512, D), lambda mi, j: (j, 0, 0)),
                  name=tag + "_dyn")
    g_out = _matmul_tn(yn, dyo, m=DI, n=D, tn=D, out_struct=_like(w_out),
                       out_spec=pl.BlockSpec((None, 512, D), lambda mi, j: (mi, 0, 0)), name=tag + "_dout")
    return dyn, g_out, gsum


def _ssd_bwd_rest(dxo, dyn, gsum, saved, mod, nw, w_zx, w_dt, conv_w, conv_b, prm, gn_w, tag):
    x, h, zx, dtr, xbc, y, sprev, yn, yo = saved
    sh, sc, g = mod
    dy, dzx, gnsum = _gnorm_bwd(y, zx, gn_w, dyn, name=tag + "_dgnorm")
    dxbc, ddtr, ssum = _ssd_bwd(xbc, dtr, prm, dy, sprev, name=tag + "_dscan")
    dzx, csum = _ssd_conv_bwd(zx, dxbc, conv_w, conv_b, dzx, name=tag + "_dconv")
    dh_dt = _matmul(ddtr, w_dt, n=D, tm=TM_ALL, contract=_nt, name=tag + "_dh_dt")
    dh = _matmul(dzx, w_zx, n=D, tm=TM_HALF, contract=_nt, extras=(dh_dt,), epi=lambda acc, e: (acc + e,), name=tag + "_dh")
    d_w_zx = _matmul_tn(h, dzx, m=D, n=ZX, tm=D, name=tag + "_din")
    d_w_dt = _matmul_tn(h, ddtr, m=D, n=LANES, tm=D, name=tag + "_din_dt")
    dx, sums = _modnorm_bwd(x, dh, dxo, nw, sc, gsum, name=tag + "_dnorm")
    return dx, d_w_zx, d_w_dt, sums, csum, gnsum, ssum


def _sc_layer_fwd(x, mod, nw, w_sc_in, conv_w, wb, out_row, tag):
    sh, sc, g = mod
    h = _modnorm_fwd(x, nw, sc, sh, name=tag + "_norm")
    proj = _matmul(h, w_sc_in, n=3 * D, tm=TM_ALL, tn=256, b_spec=pl.BlockSpec((None, D, 256), lambda mi, j: (j // 3, 0, j % 3)),
                   name=tag + "_in")
    yv = _sc_fwd(proj, conv_w, name=tag + "_conv")
    xn, yo = _matmul(yv, wb, n=D, tm=TM_HALF, contract=_nn_split,
                     b_spec=pl.BlockSpec((N_CHIPS, 256, 512), lambda mi, j: (0, out_row // 256, j)),
                     extras=(x, g), epi=_residual, out_dtypes=(F32, F32), name=tag + "_out")
    return xn, (x, h, proj, yv, yo)


def _sc_layer_bwd(dxo, saved, mod, nw, w_sc_in, conv_w, wb, gb, out_row, tag):
    x, h, proj, yv, yo = saved
    sh, sc, g = mod
    L = x.shape[0]
    dyo, gsum = _gate_bwd(dxo, yo, g, name=tag + "_dgate")
    dyv = _matmul(dyo, wb, n=D, tm=TM_ALL, tn=256, contract=_nt,
                  b_spec=pl.BlockSpec((None, 256, D), lambda mi, j: (j, out_row // 256, 0)), name=tag + "_dyv")
    gb = _matmul_tn(yv, dyo, m=D, n=D, tm=256, tn=D, into=gb, out_struct=_like(wb),
                    out_spec=pl.BlockSpec((None, 256, D), lambda mi, j: (mi, out_row // 256, 0)), name=tag + "_dout")
    dproj, csum = _sc_bwd(proj, dyv, conv_w, name=tag + "_dconv")
    tm = min(L, TM_HALF)
    dh = _matmul(dproj, w_sc_in, n=D, tm=tm, contract=_nt_sc_in, a_spec=pl.BlockSpec((3, tm, D), lambda mi, j: (0, mi, 0)),
                 b_spec=pl.BlockSpec((N_CHIPS, 512, SC_IN_SHARD), lambda mi, j: (0, j, 0)), name=tag + "_dh")
    g_sc_in = _matmul_tn(h, dproj, m=D, n=3 * D, tm=D, tn=256, b_spec=pl.BlockSpec((None, L, 256), lambda mi, j: (j // 4, 0, j % 4)),
                         out_spec=pl.BlockSpec((None, D, 256), lambda mi, j: (j // 3, 0, j % 3)),
                         out_struct=jax.ShapeDtypeStruct((N_CHIPS, D, SC_IN_SHARD), BF16), name=tag + "_din")
    dx, sums = _modnorm_bwd(x, dh, dxo, nw, sc, gsum, name=tag + "_dnorm")
    return dx, gb, g_sc_in, sums, csum


SUB_ROW = (0, 8, 16, 24)
SSD_CONV_ROW, GNORM_ROW, FINAL_ROW, SC_CONV_ROW, HEAD_ROW, SMALL_ROWS = 32, 56, 72, 80, 88, 96


def _all_gather_rows(blk, *, name):
    m_per, n = blk.shape

    def body(x_ref, out_ref, send_sems, recv_sems, local_sem):
        x, y, c = lax.axis_index("x"), lax.axis_index("y"), lax.axis_index("c")
        me, sibling = (x, y, c), (x, y, 1 - c)
        chips = [(1 - x, y), (x, 1 - y), (1 - x, 1 - y)]

        def rows(px, py, pc):
            return out_ref.at[pl.ds((4 * px + 2 * py + pc) * m_per, m_per), :]

        def copy(k, block, to, src=None):
            return pltpu.make_async_remote_copy(src_ref=rows(*block) if src is None else src, dst_ref=rows(*block),
                                                send_sem=send_sems.at[k], recv_sem=recv_sems.at[k], device_id=to,
                                                device_id_type=MESH)

        mine = pltpu.make_async_copy(x_ref, rows(*me), local_sem)
        mine.start()
        first = [copy(0, me, sibling, src=x_ref)] + [copy(1 + j, me, (*chip, c), src=x_ref) for j, chip in enumerate(chips)]
        for cp in first:
            cp.start()
        passed = [copy(4 + j, (*chip, c), sibling) for j, chip in enumerate(chips)]
        for j, chip in enumerate(chips):
            copy(1 + j, (*chip, c), me).wait_recv()
            passed[j].start()
        copy(0, sibling, me).wait_recv()
        for j, chip in enumerate(chips):
            copy(4 + j, (*chip, 1 - c), me).wait_recv()
        for cp in first + passed:
            cp.wait_send()
        mine.wait()

    return pl.pallas_call(
        body, out_shape=jax.ShapeDtypeStruct((N_DEV * m_per, n), blk.dtype),
        in_specs=[pl.BlockSpec(memory_space=pltpu.VMEM)], out_specs=pl.BlockSpec(memory_space=pltpu.VMEM),
        scratch_shapes=[pltpu.SemaphoreType.DMA((7,)), pltpu.SemaphoreType.DMA((7,)), pltpu.SemaphoreType.DMA],
        name=name)(blk)


def _half(ref, chip, c):
    hr = ref.shape[1] // 2
    return ref.at[chip, pl.ds(c * hr, hr), :]


def _gather_copy(bufs, sends, recvs, b, k, chip, pc, to):
    piece = _half(bufs[b], 2 * chip[0] + chip[1], pc)
    return pltpu.make_async_remote_copy(src_ref=piece, dst_ref=piece, send_sem=sends.at[4 * b + k], recv_sem=recvs.at[4 * b + k],
                                        device_id=to, device_id_type=MESH)


def _split_call(body, bufs, sems_in, n_sems, *, name, after=(), token=False):
    nb, na, starts = len(bufs), len(after), not sems_in

    def wrapped(*refs):
        sems = refs[nb + na:nb + na + 2] if starts else refs[nb:nb + 2]
        body(refs[:nb], sems[0], sems[1])
        if token:
            refs[-1][...] = jnp.zeros_like(refs[-1])

    out_shape = [pltpu.SemaphoreType.DMA((n_sems,)) for _ in range(2 if starts else 0)]
    out_specs = [SEM] * len(out_shape) + [HBM] * nb
    alias = {b: len(out_shape) + b for b in range(nb)}
    out_shape += [pltpu.HBM(b.shape, b.dtype) for b in bufs]
    if token:
        out_shape.append(jax.ShapeDtypeStruct((8, LANES), F32))
        out_specs.append(pl.BlockSpec(memory_space=pltpu.VMEM))
    return pl.pallas_call(
        wrapped, out_shape=out_shape, in_specs=[HBM] * nb + [SEM] * len(sems_in) + [ANY] * na, out_specs=out_specs,
        input_output_aliases=alias,
        compiler_params=pltpu.CompilerParams(has_side_effects=pltpu.SideEffectType.DATAFLOW_SIDE_EFFECTING),
        name=name)(*[pltpu.with_memory_space_constraint(b, pltpu.HBM) for b in bufs], *sems_in, *after)


def _gather_start(bufs, *, name, after=()):
    nb = len(bufs)

    def body(ins, sends, recvs):
        x, y, c = lax.axis_index("x"), lax.axis_index("y"), lax.axis_index("c")
        chips = [(1 - x, y), (x, 1 - y), (1 - x, 1 - y)]
        for b in range(nb):
            _gather_copy(ins, sends, recvs, b, 0, (x, y), c, (x, y, 1 - c)).start()
            for j, chip in enumerate(chips):
                _gather_copy(ins, sends, recvs, b, 1 + j, (x, y), c, (*chip, c)).start()

    out = _split_call(body, bufs, (), 4 * nb, name=name, after=after, token=True)
    return (out[0], out[1], out[2:2 + nb]), out[-1]


def _gather_wait_first(flight, *, name, after=()):
    sends, recvs, bufs = flight
    nb = len(bufs)

    def body(ins, sends_, recvs_):
        x, y, c = lax.axis_index("x"), lax.axis_index("y"), lax.axis_index("c")
        chips = [(1 - x, y), (x, 1 - y), (1 - x, 1 - y)]
        for b in range(nb):
            _gather_copy(ins, sends_, recvs_, b, 0, (x, y), c, (x, y, 1 - c)).wait_send()
            _gather_copy(ins, sends_, recvs_, b, 0, (x, y), 1 - c, (x, y, c)).wait_recv()
            for j, chip in enumerate(chips):
                _gather_copy(ins, sends_, recvs_, b, 1 + j, (x, y), c, (*chip, c)).wait_send()
                _gather_copy(ins, sends_, recvs_, b, 1 + j, chip, c, (x, y, c)).wait_recv()

    return _split_call(body, bufs, (sends, recvs), 4 * nb, name=name, after=after)


def _gather_forward(bufs, *, name):
    nb = len(bufs)

    def body(ins, sends, recvs):
        x, y, c = lax.axis_index("x"), lax.axis_index("y"), lax.axis_index("c")
        chips = [(1 - x, y), (x, 1 - y), (1 - x, 1 - y)]
        for b in range(nb):
            for j, chip in enumerate(chips):
                _gather_copy(ins, sends, recvs, b, 1 + j, chip, c, (x, y, 1 - c)).start()

    out = _split_call(body, bufs, (), 4 * nb, name=name)
    return out[0], out[1], out[2:2 + nb]


def _gather_wait_forward(flight, *, name, after=()):
    sends, recvs, bufs = flight
    nb = len(bufs)

    def body(ins, sends_, recvs_):
        x, y, c = lax.axis_index("x"), lax.axis_index("y"), lax.axis_index("c")
        chips = [(1 - x, y), (x, 1 - y), (1 - x, 1 - y)]
        for b in range(nb):
            for j, chip in enumerate(chips):
                _gather_copy(ins, sends_, recvs_, b, 1 + j, chip, c, (x, y, 1 - c)).wait_send()
                _gather_copy(ins, sends_, recvs_, b, 1 + j, chip, 1 - c, (x, y, c)).wait_recv()

    return _split_call(body, bufs, (sends, recvs), 4 * nb, name=name, after=after)


def _owner_copies(hs, lands, sends, recvs):
    x, y, c = lax.axis_index("x"), lax.axis_index("y"), lax.axis_index("c")
    chips = [(1 - x, y), (x, 1 - y), (1 - x, 1 - y)]
    return [pltpu.make_async_remote_copy(src_ref=hs[b].at[2 * cx + cy], dst_ref=lands[b].at[j], send_sem=sends.at[3 * b + j],
                                         recv_sem=recvs.at[3 * b + j], device_id=(cx, cy, c), device_id_type=MESH)
            for b in range(len(hs)) for j, (cx, cy) in enumerate(chips)]


def _owners_start(hs, *, name):
    nb = len(hs)
    lands = [lax.empty((3,) + h.shape[1:], h.dtype) for h in hs]

    def body(refs, sends, recvs):
        for cp in _owner_copies(refs[:nb], refs[nb:], sends, recvs):
            cp.start()

    out = _split_call(body, list(hs) + lands, (), 3 * nb, name=name, token=True)
    return (out[0], out[1], out[2:2 + 2 * nb]), out[-1]


def _owners_wait(flight, *, name, after=()):
    sends, recvs, bufs = flight
    nb = len(bufs) // 2

    def body(refs, sends_, recvs_):
        for cp in _owner_copies(refs[:nb], refs[nb:], sends_, recvs_):
            cp.wait()

    return _split_call(body, bufs, (sends, recvs), 3 * nb, name=name, after=after)[nb:]


def _swap_halves_with_sibling(bufs, *, name, after=()):
    nb, na = len(bufs), len(after)

    def body(*refs):
        ins, outs, send_sems, recv_sems = refs[:nb], refs[nb + na:2 * nb + na], refs[2 * nb + na], refs[2 * nb + na + 1]
        x, y, c = lax.axis_index("x"), lax.axis_index("y"), lax.axis_index("c")
        copies = []
        for b in range(nb):
            hr = ins[b].shape[1] // 2
            copies.append(pltpu.make_async_remote_copy(
                src_ref=ins[b].at[:, pl.ds((1 - c) * hr, hr), :], dst_ref=outs[b], send_sem=send_sems.at[b],
                recv_sem=recv_sems.at[b], device_id=(x, y, 1 - c), device_id_type=MESH))
        for cp in copies:
            cp.start()
        for cp in copies:
            cp.wait()

    return pl.pallas_call(
        body, out_shape=[jax.ShapeDtypeStruct((b.shape[0], b.shape[1] // 2, b.shape[2]), b.dtype) for b in bufs],
        in_specs=[ANY] * (nb + na), out_specs=[ANY] * nb,
        scratch_shapes=[pltpu.SemaphoreType.DMA((nb,)), pltpu.SemaphoreType.DMA((nb,))], name=name)(*bufs, *after)


def _swap_results_with_sibling(ts, *, name):
    nb = len(ts)

    def body(*refs):
        outs, send_sems, recv_sems = refs[nb:2 * nb], refs[2 * nb], refs[2 * nb + 1]
        x, y, c = lax.axis_index("x"), lax.axis_index("y"), lax.axis_index("c")
        copies = [pltpu.make_async_remote_copy(src_ref=outs[b].at[c], dst_ref=outs[b].at[c], send_sem=send_sems.at[b],
                                               recv_sem=recv_sems.at[b], device_id=(x, y, 1 - c), device_id_type=MESH)
                  for b in range(nb)]
        for cp in copies:
            cp.start()
        for cp in copies:
            cp.wait()

    return pl.pallas_call(
        body, out_shape=[jax.ShapeDtypeStruct(t.shape, t.dtype) for t in ts], in_specs=[ANY] * nb, out_specs=[ANY] * nb,
        scratch_shapes=[pltpu.SemaphoreType.DMA((nb,)), pltpu.SemaphoreType.DMA((nb,))],
        input_output_aliases={b: b for b in range(nb)}, name=name)(*ts)


def _row_tile(rows, cols):
    best = 16
    for t in range(16, rows + 1, 16):
        if rows % t == 0 and t * cols <= 640 * 1024:
            best = t
    assert rows % best == 0, (rows, cols)
    return best


def _add_sibling_half(g, recv, core, *, name):
    nk, r, n = g.shape
    hr = r // 2
    tr = _row_tile(hr, n)

    def body(c_ref, a_ref, b_ref, o_ref):
        o_ref[...] = (a_ref[...].astype(F32) + b_ref[...].astype(F32)).astype(BF16)

    grid_spec = pltpu.PrefetchScalarGridSpec(
        num_scalar_prefetch=1, grid=(nk, hr // tr),
        in_specs=[pl.BlockSpec((None, tr, n), lambda k, i, c_ref: (k, c_ref[0] * (hr // tr) + i, 0)),
                  pl.BlockSpec((None, tr, n), lambda k, i, c_ref: (k, i, 0))],
        out_specs=pl.BlockSpec((None, tr, n), lambda k, i, c_ref: (k, i, 0)))
    return pl.pallas_call(body, grid_spec=grid_spec, out_shape=jax.ShapeDtypeStruct((nk, hr, n), BF16),
                          compiler_params=_params(("parallel", "parallel")), name=name)(core, g, recv)


def _add_chip_sums(h, recv, chip_core, *, name):
    _, hr, n = h.shape
    tr = _row_tile(hr, n)

    def body(k_ref, a_ref, b_ref, o_ref):
        o_ref[...] = ((a_ref[...].astype(F32) + b_ref[0].astype(F32)) + b_ref[1].astype(F32)) + b_ref[2].astype(F32)

    grid_spec = pltpu.PrefetchScalarGridSpec(
        num_scalar_prefetch=1, grid=(hr // tr,),
        in_specs=[pl.BlockSpec((None, tr, n), lambda i, k_ref: (k_ref[0], i, 0)),
                  pl.BlockSpec((3, tr, n), lambda i, k_ref: (0, i, 0))],
        out_specs=pl.BlockSpec((None, tr, n), lambda i, k_ref: (k_ref[1], i, 0)))
    return pl.pallas_call(body, grid_spec=grid_spec, out_shape=jax.ShapeDtypeStruct((2, hr, n), F32),
                          compiler_params=_params(("parallel",)), name=name)(chip_core, h, recv)


def _sum_devices(g, *, name):
    nd, r, n = g.shape

    def body(g_ref, o_ref):
        acc = g_ref[0]
        for i in range(1, nd):
            acc = acc + g_ref[i]
        o_ref[...] = acc

    return pl.pallas_call(body, out_shape=jax.ShapeDtypeStruct((r, n), F32), name=name)(g)


def _own_slot(shard, chip):
    return lax.dynamic_update_slice(jnp.zeros((N_CHIPS,) + shard.shape, BF16), shard[None], (chip, 0, 0))


def kernel(x, c, ada_w, ada_b, mix_norm_w, mlp_norm_w, mlp_up, mlp_down, ssd_in_w, ssd_conv_w, ssd_conv_b, ssd_dt_bias, ssd_A_log, ssd_D, ssd_norm_w, ssd_out_w, sc_in_w, sc_conv_w, sc_out_w, final_norm_w, loss_target, m_ada_w, m_ada_b, m_mix_norm_w, m_mlp_norm_w, m_mlp_up, m_mlp_down, m_ssd_in_w, m_ssd_conv_w, m_ssd_conv_b, m_ssd_dt_bias, m_ssd_A_log, m_ssd_D, m_ssd_norm_w, m_ssd_out_w, m_sc_in_w, m_sc_conv_w, m_sc_out_w, m_final_norm_w, v_ada_w, v_ada_b, v_mix_norm_w, v_mlp_norm_w, v_mlp_up, v_mlp_down, v_ssd_in_w, v_ssd_conv_w, v_ssd_conv_b, v_ssd_dt_bias, v_ssd_A_log, v_ssd_D, v_ssd_norm_w, v_ssd_out_w, v_sc_in_w, v_sc_conv_w, v_sc_out_w, v_final_norm_w):
    xi, yi, ci = lax.axis_index("x"), lax.axis_index("y"), lax.axis_index("c")
    chip = 2 * xi + yi
    dev = 2 * chip + ci
    n_ada = ada_w.shape[2]

    conv_flat = jnp.concatenate([ssd_conv_w.reshape(-1), sc_conv_w.reshape(-1), jnp.zeros((256,), F32)]).reshape(4, D)
    blk0 = jnp.concatenate([c, conv_flat, jnp.zeros((3, D), F32)], axis=0)
    got0 = _all_gather_rows(blk0, name="gather_cond").reshape(N_DEV, 8, D)
    c_all = got0[:, 0]
    conv_all = got0[0::2, 1:5].reshape(N_CHIPS, 4 * D)
    ssd_conv = jnp.moveaxis(conv_all[:, :4 * 768].reshape(N_CHIPS, 4, 768), 0, 1).reshape(4, CONVD)
    sc_conv = jnp.moveaxis(conv_all[:, 4 * 768:4 * 768 + 3 * 256].reshape(N_CHIPS, 3, 256), 0, 1).reshape(3, D)
    mod_shard = [_matmul(c_all, ada_w[i], n=n_ada, a_silu=True,
                         extras=(lax.dynamic_slice(ada_b, (i, chip * n_ada), (1, n_ada)),),
                         epi=lambda acc, b: (acc + b,), name=f"ada_mod{i}") for i in range(2)]
    mod_all = _all_gather_rows(jnp.concatenate(mod_shard, axis=0), name="gather_mod")
    mod_all = mod_all.reshape(N_DEV, 2, N_DEV, n_ada)[0::2]
    mod = jnp.moveaxis(lax.dynamic_index_in_dim(mod_all, dev, axis=2, keepdims=False), 0, 1).reshape(2, 6, D)
    mods = [[mod[i, j:j + 1] for j in range(6)] for i in range(2)]

    bf = lambda v: v.astype(BF16)
    up_row, down_row, sc_out_row = 0, D, 2 * D
    a_bufs = [_own_slot(bf(ssd_in_w[0]), chip)]
    b_bufs = [_own_slot(bf(ssd_out_w[0]), chip), _own_slot(bf(jnp.concatenate([mlp_up[0], mlp_down[0]], axis=0)), chip)]
    c_bufs = [_own_slot(bf(sc_in_w[0]), chip), _own_slot(bf(jnp.concatenate([mlp_up[1], mlp_down[1], sc_out_w[0]], axis=0)), chip)]
    fly_a, tok = _gather_start(a_bufs, name="gather_a_start", after=(mod,))
    fly_b, tok = _gather_start(b_bufs, name="gather_b_start", after=(tok,))
    fly_c, tok = _gather_start(c_bufs, name="gather_c_start", after=(tok,))

    row = lambda v: v.reshape(1, -1)
    xs, tgt = x[0], loss_target[0]
    prm = jnp.pad(jnp.concatenate([ssd_dt_bias, ssd_A_log, ssd_D, jnp.zeros((5, NH), F32)], axis=0), ((0, 0), (0, LANES - NH)))
    mix_nw = [row(mix_norm_w[i]) for i in range(2)]
    mlp_nw = [row(mlp_norm_w[i]) for i in range(2)]
    a_bufs = _gather_wait_first(fly_a, name="gather_a_landed", after=(tok,))
    (w_ssd_in,) = _gather_wait_forward(_gather_forward(a_bufs, name="gather_a_pass"), name="gather_a_done")
    ssd_in_full = jnp.moveaxis(w_ssd_in, 0, 1).reshape(D, N_CHIPS * SSD_IN_SHARD)
    w_zx, w_dt = ssd_in_full[:, :ZX], jnp.pad(ssd_in_full[:, ZX:], ((0, 0), (0, LANES - NH)))
    scan = _ssd_fwd_scan(xs, mods[0][0:3], mix_nw[0], w_zx, w_dt, ssd_conv, ssd_conv_b, prm, "ssd")
    fly_b = _gather_forward(_gather_wait_first(fly_b, name="gather_b_landed", after=(scan[3],)), name="gather_b_pass")
    w_ssd_out, w_b = _gather_wait_forward(fly_b, name="gather_b_done", after=(scan[4],))
    x1, s_ssd = _ssd_fwd_out(xs, mods[0][0:3], scan, ssd_norm_w, w_ssd_out, "ssd")
    x2, s_mlp0 = _mlp_fwd(x1, mods[0][3:6], mlp_nw[0], w_b, up_row, down_row, "mlp0")
    c_bufs = _gather_wait_first(fly_c, name="gather_c_landed", after=(x2,))
    w_sc_in, w_c = _gather_wait_forward(_gather_forward(c_bufs, name="gather_c_pass"), name="gather_c_done")
    x3, s_sc = _sc_layer_fwd(x2, mods[1][0:3], mix_nw[1], w_sc_in, sc_conv, w_c, sc_out_row, "sc")
    x4, s_mlp1 = _mlp_fwd(x3, mods[1][3:6], mlp_nw[1], w_c, up_row, down_row, "mlp1")

    core = ci.reshape(1).astype(jnp.int32)
    chip_core = jnp.stack([chip, ci]).astype(jnp.int32)

    def reduce_start(gbufs, tag, after=()):
        sib = _swap_halves_with_sibling(gbufs, name=tag + "_sibling", after=after)
        hs = [_add_sibling_half(g, s, core, name=f"{tag}_add_sibling{b}") for b, (g, s) in enumerate(zip(gbufs, sib))]
        return _owners_start(hs, name=tag + "_owners_start")

    def reduce_finish(flight, tag, after):
        nb = len(flight[2]) // 2
        lands = _owners_wait(flight, name=tag + "_owners_landed", after=after)
        ts = [_add_chip_sums(h, o, chip_core, name=f"{tag}_add_chips{b}") for b, (h, o) in enumerate(zip(flight[2][:nb], lands))]
        return [t.reshape(-1, t.shape[2]) for t in _swap_results_with_sibling(ts, name=tag + "_result")]

    dx4, fsum = _final_loss(x4, row(final_norm_w), tgt, name="final_loss")
    dx3, g_c, sum_mlp1 = _mlp_bwd(dx4, s_mlp1, mods[1][3:6], mlp_nw[1], w_c, None, up_row, down_row, "mlp1")
    dx2, g_c, g_sc_in, sum_sc, sc_csum = _sc_layer_bwd(dx3, s_sc, mods[1][0:3], mix_nw[1], w_sc_in, sc_conv, w_c, g_c,
                                                       sc_out_row, "sc")
    dx1, g_b, sum_mlp0 = _mlp_bwd(dx2, s_mlp0, mods[0][3:6], mlp_nw[0], w_b, None, up_row, down_row, "mlp0")
    dyn, g_ssd_out, gsum_ssd = _ssd_bwd_out(dx1, s_ssd, mods[0][0:3], w_ssd_out, "ssd")
    fly_1, tok = reduce_start([g_c, g_sc_in, g_b, g_ssd_out], "rs1")
    grad_x, d_w_zx, d_w_dt, sum_ssd, csum, gnsum, ssum = _ssd_bwd_rest(
        dx1, dyn, gsum_ssd, s_ssd, mods[0][0:3], mix_nw[0], w_zx, w_dt, ssd_conv, ssd_conv_b, prm,
        ssd_norm_w + tok[0:1, 0:1], "ssd")
    t_c, t_sc_in, t_b, t_ssd_out = reduce_finish(fly_1, "rs1", (grad_x,))

    def ssd_in_owner(k):
        lo, hi = k * SSD_IN_SHARD, (k + 1) * SSD_IN_SHARD
        if hi <= ZX:
            return d_w_zx[:, lo:hi]
        return jnp.concatenate([d_w_zx[:, lo:], d_w_dt[:, :hi - ZX]], axis=1)

    small = jnp.concatenate([sum_ssd, sum_mlp0, sum_sc, sum_mlp1, csum.reshape(24, D), gnsum.reshape(16, D), fsum, sc_csum,
                             jnp.pad(ssum, ((0, 0), (0, D - LANES)))], axis=0)
    small_all = _all_gather_rows(small, name="gather_small").reshape(N_DEV, SMALL_ROWS, D)
    fly_2, tok = reduce_start([jnp.stack([ssd_in_owner(k) for k in range(N_CHIPS)]).astype(BF16)], "rs2", (small_all,))
    small_all = small_all + tok[0:1, 0:1]
    tot = _sum_devices(small_all, name="sum_small")
    loss = tot[FINAL_ROW + 1, 0]
    mod_rows = [r + o for r in SUB_ROW for o in (3, 2, 0)]
    g_ada_b = jnp.stack([tot[r] for r in mod_rows]).reshape(2, 6 * D)
    g_mix_norm = jnp.stack([tot[SUB_ROW[0] + 1], tot[SUB_ROW[2] + 1]])
    g_mlp_norm = jnp.stack([tot[SUB_ROW[1] + 1], tot[SUB_ROW[3] + 1]])
    conv_sums = tot[SSD_CONV_ROW:SSD_CONV_ROW + 24].reshape(8, CONVD)
    g_ssd_conv_w = lax.dynamic_slice(conv_sums, (0, chip * 768), (4, 768))[None]
    g_ssd_conv_b = conv_sums[4:5]
    g_ssd_norm = tot[GNORM_ROW:GNORM_ROW + 2].reshape(1, DI)
    g_final = tot[FINAL_ROW]
    g_sc_conv_w = lax.dynamic_slice(tot[SC_CONV_ROW:SC_CONV_ROW + 3], (0, chip * 256), (3, 256))[None]
    g_a_log, g_d, g_dt_bias = (tot[HEAD_ROW + r:HEAD_ROW + r + 1, 0:NH] for r in range(3))
    c_pad = jnp.concatenate([c_all, jnp.zeros((8, D), F32)], axis=0)
    dmod_all = jnp.stack([small_all[:, r] for r in mod_rows], axis=1).reshape(N_DEV, 2, 6 * D)
    g_ada_w = []
    for i in range(2):
        dm = lax.dynamic_slice(dmod_all[:, i], (0, chip * n_ada), (N_DEV, n_ada))
        g_ada_w.append(_matmul_tn(c_pad, jnp.concatenate([dm, jnp.zeros_like(dm)], axis=0), m=D, n=n_ada, a_silu=True,
                                  name=f"ada_dw{i}"))
    g_ada_w = jnp.stack(g_ada_w)

    big = dict(mlp_up=[(t_b, up_row), (t_c, up_row)], mlp_down=[(t_b, down_row), (t_c, down_row)],
               ssd_out_w=[(t_ssd_out, 0)], sc_out_w=[(t_c, sc_out_row)], sc_in_w=[(t_sc_in, 0)], ssd_in_w=None)
    grads = dict(ada_w=g_ada_w, ada_b=g_ada_b, mix_norm_w=g_mix_norm, mlp_norm_w=g_mlp_norm, ssd_conv_w=g_ssd_conv_w,
                 ssd_conv_b=g_ssd_conv_b, ssd_dt_bias=g_dt_bias, ssd_A_log=g_a_log, ssd_D=g_d, ssd_norm_w=g_ssd_norm,
                 sc_conv_w=g_sc_conv_w, final_norm_w=g_final)
    weights = dict(ada_w=(ada_w, m_ada_w, v_ada_w), ada_b=(ada_b, m_ada_b, v_ada_b),
                   mix_norm_w=(mix_norm_w, m_mix_norm_w, v_mix_norm_w), mlp_norm_w=(mlp_norm_w, m_mlp_norm_w, v_mlp_norm_w),
                   mlp_up=(mlp_up, m_mlp_up, v_mlp_up), mlp_down=(mlp_down, m_mlp_down, v_mlp_down),
                   ssd_in_w=(ssd_in_w, m_ssd_in_w, v_ssd_in_w), ssd_conv_w=(ssd_conv_w, m_ssd_conv_w, v_ssd_conv_w),
                   ssd_conv_b=(ssd_conv_b, m_ssd_conv_b, v_ssd_conv_b), ssd_dt_bias=(ssd_dt_bias, m_ssd_dt_bias, v_ssd_dt_bias),
                   ssd_A_log=(ssd_A_log, m_ssd_A_log, v_ssd_A_log), ssd_D=(ssd_D, m_ssd_D, v_ssd_D),
                   ssd_norm_w=(ssd_norm_w, m_ssd_norm_w, v_ssd_norm_w), ssd_out_w=(ssd_out_w, m_ssd_out_w, v_ssd_out_w),
                   sc_in_w=(sc_in_w, m_sc_in_w, v_sc_in_w), sc_conv_w=(sc_conv_w, m_sc_conv_w, v_sc_conv_w),
                   sc_out_w=(sc_out_w, m_sc_out_w, v_sc_out_w), final_norm_w=(final_norm_w, m_final_norm_w, v_final_norm_w))
    def step(nm, parts):
        w, m, v = (t.reshape(-1, t.shape[-1]) for t in weights[nm])
        rows, outs = w.shape[0] // len(parts), None
        for i, (gbuf, g_row) in enumerate(parts):
            outs = _adamw(w, gbuf, m, v, g_row=g_row, w_row=i * rows, rows=rows, into=outs, emit_g=True, name=f"adamw_{nm}{i}")
        return outs

    res = {}
    for nm, (w, m, v) in weights.items():
        two_d = (-1, w.shape[-1]) if w.ndim > 1 else (1, -1)
        if nm not in big:
            res[nm] = (grads[nm], *_adamw(w.reshape(two_d), grads[nm].reshape(two_d), m.reshape(two_d), v.reshape(two_d),
                                          name="adamw_" + nm))
        elif big[nm] is not None:
            res[nm] = step(nm, big[nm])
    (t_ssd_in,) = reduce_finish(fly_2, "rs2", (res["sc_out_w"][1],))
    res["ssd_in_w"] = step("ssd_in_w", [(t_ssd_in, 0)])
    outs = [[res[nm][k].reshape(weights[nm][0].shape) for nm in weights] for k in range(4)]
    return (loss, grad_x[None], *outs[0], *outs[1], *outs[2], *outs[3])
```

```python
import jax
import jax.numpy as jnp
from jax import lax
from jax.experimental import pallas as pl
from jax.experimental.pallas import tpu as pltpu

F32 = jnp.float32
BF16 = jnp.bfloat16
MESH = pl.DeviceIdType.MESH

D = 1024
DFF = 4096
DI = 2048
NH = 32
HP = 64
NG = 4
NS = 128
CH = 128
CONVD = DI + 2 * NG * NS
ZX = DI + CONVD
GW = NG * NS
LANES = 128
N_CHIPS = 4
N_DEV = 8
EPS = 1e-5
ADAM_LR, ADAM_B1, ADAM_B2, ADAM_EPS, ADAM_WD, ADAM_STEP = 1e-3, 0.9, 0.999, 1e-8, 0.01, 10
VMEM_LIMIT = 48 * 1024 * 1024
TM_ALL = 2048
TM_HALF = 1024
ANY = pl.BlockSpec(memory_space=pl.ANY)
HBM = pl.BlockSpec(memory_space=pltpu.HBM)
SEM = pl.BlockSpec(memory_space=pltpu.SEMAPHORE)

SSD_IN_SHARD = 1288
SC_IN_SHARD = 768


def _params(sem=None):
    return pltpu.CompilerParams(dimension_semantics=sem, vmem_limit_bytes=VMEM_LIMIT)


def _sigmoid(v):
    return 1.0 / (1.0 + jnp.exp(-v))


def _dot(a, b, dims=((1,), (0,)), precision=None):
    return lax.dot_general(a, b, (dims, ((), ())), preferred_element_type=F32, precision=precision)


def _dot_nt(a, b):
    return _dot(a, b, ((1,), (1,)))


def _dot_tn(a, b):
    return _dot(a, b, ((0,), (0,)))


def _nn(av, bv):
    return _dot(av.astype(BF16), bv.astype(BF16))


def _nt(av, bv):
    return _dot_nt(av.astype(BF16), bv.astype(BF16))


def _nn_split(av, bv):
    r = bv.shape[1]
    acc = _dot(av[:, 0:r].astype(BF16), bv[0])
    for s in range(1, bv.shape[0]):
        acc = acc + _dot(av[:, s * r:(s + 1) * r].astype(BF16), bv[s])
    return acc


def _nt_split(av, bv):
    kc = bv.shape[2]
    acc = _dot_nt(av[:, 0:kc].astype(BF16), bv[0])
    for s in range(1, bv.shape[0]):
        acc = acc + _dot_nt(av[:, s * kc:(s + 1) * kc].astype(BF16), bv[s])
    return acc


def _nt_sc_in(av, bv):
    q = 256
    acc = None
    for i in range(3 * D // q):
        a_blk = av[i // 4][:, (i % 4) * q:(i % 4 + 1) * q]
        b_blk = bv[i // 3][:, (i % 3) * q:(i % 3 + 1) * q]
        t = _dot_nt(a_blk, b_blk)
        acc = t if acc is None else acc + t
    return acc


def _matmul(a, b, *, name, n, contract=_nn, a_spec=None, b_spec=None, tm=512, tn=512, extras=(), epi=None,
            out_dtypes=(F32,), a_silu=False):
    M = a.shape[-2]
    tm, tn = min(tm, M), min(tn, n)
    assert M % tm == 0 and n % tn == 0, (name, M, n, tm, tn)
    n_ex = len(extras)
    if a_spec is None:
        a_spec = pl.BlockSpec((tm, a.shape[1]), lambda i, j: (i, 0))
    if b_spec is None:
        b_spec = (pl.BlockSpec((tn, b.shape[1]), lambda i, j: (j, 0)) if contract is _nt
                  else pl.BlockSpec((b.shape[0], tn), lambda i, j: (0, j)))

    def body(*refs):
        av = refs[0][...]
        if a_silu:
            av = av * _sigmoid(av)
        acc = contract(av, refs[1][...])
        res = epi(acc, *[r[...] for r in refs[2:2 + n_ex]]) if epi is not None else (acc,)
        for o_ref, r in zip(refs[2 + n_ex:], res, strict=True):
            o_ref[...] = r.astype(o_ref.dtype)

    in_specs = [a_spec, b_spec]
    for e in extras:
        in_specs.append(pl.BlockSpec((1, tn), lambda i, j: (0, j)) if e.shape[0] == 1 and M != 1
                        else pl.BlockSpec((tm, tn), lambda i, j: (i, j)))
    outs = pl.pallas_call(
        body, grid=(M // tm, n // tn), in_specs=in_specs,
        out_specs=[pl.BlockSpec((tm, tn), lambda i, j: (i, j)) for _ in out_dtypes],
        out_shape=[jax.ShapeDtypeStruct((M, n), dt) for dt in out_dtypes],
        compiler_params=_params(("parallel", "parallel")), name=name)(a, b, *extras)
    return outs if len(out_dtypes) > 1 else outs[0]


def _matmul_tn(a, b, *, name, m, n, tm=512, tn=512, a_spec=None, b_spec=None, out_spec=None, out_struct=None, into=None,
               a_silu=False):
    T = a.shape[-2]
    tm, tn = min(tm, m), min(tn, n)
    assert m % tm == 0 and n % tn == 0, (name, m, n, tm, tn)
    if a_spec is None:
        a_spec = pl.BlockSpec((T, tm), lambda i, j: (0, i))
    if b_spec is None:
        b_spec = pl.BlockSpec((T, tn), lambda i, j: (0, j))
    if out_spec is None:
        out_spec, out_struct = pl.BlockSpec((tm, tn), lambda i, j: (i, j)), jax.ShapeDtypeStruct((m, n), F32)

    def body(a_ref, b_ref, *rest):
        av = a_ref[...]
        if a_silu:
            av = av * _sigmoid(av)
        rest[-1][...] = _dot_tn(av.astype(BF16), b_ref[...].astype(BF16)).astype(rest[-1].dtype)

    args, in_specs, alias = [a, b], [a_spec, b_spec], {}
    if into is not None:
        args, in_specs, alias = args + [into], in_specs + [ANY], {2: 0}
    return pl.pallas_call(body, grid=(m // tm, n // tn), in_specs=in_specs, out_specs=out_spec, out_shape=out_struct,
                          input_output_aliases=alias, compiler_params=_params(("parallel", "parallel")), name=name)(*args)


def _modnorm_fwd(x, nw, sc, sh, *, name):
    L = x.shape[0]
    tm = min(L, 512)

    def body(x_ref, nw_ref, sc_ref, sh_ref, h_ref):
        xv = x_ref[...]
        r = lax.rsqrt(jnp.mean(xv * xv, axis=-1, keepdims=True) + EPS)
        h_ref[...] = ((xv * r * nw_ref[...]) * (1.0 + sc_ref[...]) + sh_ref[...]).astype(BF16)

    row = pl.BlockSpec((tm, D), lambda i: (i, 0))
    vec = pl.BlockSpec((1, D), lambda i: (0, 0))
    return pl.pallas_call(body, grid=(L // tm,), in_specs=[row, vec, vec, vec], out_specs=row,
                          out_shape=jax.ShapeDtypeStruct((L, D), BF16),
                          compiler_params=_params(("parallel",)), name=name)(x, nw, sc, sh)


def _modnorm_bwd(x, dh, dxo, nw, sc, gsum, *, name):
    L = x.shape[0]
    tm = min(L, 256)

    def body(x_ref, dh_ref, dxo_ref, nw_ref, sc_ref, g_ref, dx_ref, s_ref):
        @pl.when(pl.program_id(0) == 0)
        def _():
            s_ref[...] = g_ref[...]

        xv, dhv = x_ref[...], dh_ref[...]
        r = lax.rsqrt(jnp.mean(xv * xv, axis=-1, keepdims=True) + EPS)
        xhat = xv * r
        dxhat = dhv * (nw_ref[...] * (1.0 + sc_ref[...]))
        dx_ref[...] = dxo_ref[...] + r * (dxhat - xhat * jnp.mean(dxhat * xhat, axis=-1, keepdims=True))
        s_ref[1:2, :] += jnp.sum(dhv * xhat, axis=0, keepdims=True) * (1.0 + sc_ref[...])
        s_ref[2:3, :] += jnp.sum(dhv * xhat, axis=0, keepdims=True) * nw_ref[...]
        s_ref[3:4, :] += jnp.sum(dhv, axis=0, keepdims=True)

    row = pl.BlockSpec((tm, D), lambda i: (i, 0))
    vec = pl.BlockSpec((1, D), lambda i: (0, 0))
    blk = pl.BlockSpec((8, D), lambda i: (0, 0))
    return pl.pallas_call(body, grid=(L // tm,), in_specs=[row, row, row, vec, vec, blk], out_specs=[row, blk],
                          out_shape=[jax.ShapeDtypeStruct((L, D), F32), jax.ShapeDtypeStruct((8, D), F32)],
                          compiler_params=_params(("arbitrary",)), name=name)(x, dh, dxo, nw, sc, gsum)


def _gate_bwd(dxo, y, g, *, name):
    L = dxo.shape[0]
    tm = min(L, 512)

    def body(dxo_ref, y_ref, g_ref, dy_ref, s_ref):
        @pl.when(pl.program_id(0) == 0)
        def _():
            s_ref[...] = jnp.zeros_like(s_ref)

        dv = dxo_ref[...]
        dy_ref[...] = (dv * g_ref[...]).astype(BF16)
        s_ref[0:1, :] += jnp.sum(dv * y_ref[...], axis=0, keepdims=True)

    row = pl.BlockSpec((tm, D), lambda i: (i, 0))
    return pl.pallas_call(body, grid=(L // tm,), in_specs=[row, row, pl.BlockSpec((1, D), lambda i: (0, 0))],
                          out_specs=[row, pl.BlockSpec((8, D), lambda i: (0, 0))],
                          out_shape=[jax.ShapeDtypeStruct((L, D), BF16), jax.ShapeDtypeStruct((8, D), F32)],
                          compiler_params=_params(("arbitrary",)), name=name)(dxo, y, g)


def _final_loss(x, fw, tgt, *, name):
    L = x.shape[0]
    tm = min(L, 256)

    def body(x_ref, fw_ref, t_ref, dx_ref, s_ref):
        @pl.when(pl.program_id(0) == 0)
        def _():
            s_ref[...] = jnp.zeros_like(s_ref)

        xv = x_ref[...]
        r = lax.rsqrt(jnp.mean(xv * xv, axis=-1, keepdims=True) + EPS)
        xhat = xv * r
        diff = xhat * fw_ref[...] - t_ref[...]
        dout = diff * (1.0 / D)
        dxhat = dout * fw_ref[...]
        dx_ref[...] = r * (dxhat - xhat * jnp.mean(dxhat * xhat, axis=-1, keepdims=True))
        s_ref[0:1, :] += jnp.sum(dout * xhat, axis=0, keepdims=True)
        s_ref[1:2, :] += jnp.zeros((1, D), F32) + 0.5 * jnp.sum(jnp.sum(diff * diff, axis=-1, keepdims=True) * (1.0 / D))

    row = pl.BlockSpec((tm, D), lambda i: (i, 0))
    return pl.pallas_call(body, grid=(L // tm,), in_specs=[row, pl.BlockSpec((1, D), lambda i: (0, 0)), row],
                          out_specs=[row, pl.BlockSpec((8, D), lambda i: (0, 0))],
                          out_shape=[jax.ShapeDtypeStruct((L, D), F32), jax.ShapeDtypeStruct((8, D), F32)],
                          compiler_params=_params(("arbitrary",)), name=name)(x, fw, tgt)


def _shift_down(v, j):
    if j == 0:
        return v
    row = lax.broadcasted_iota(jnp.int32, v.shape, 0)
    return jnp.where(row >= j, pltpu.roll(v, j, 0), 0.0)


def _shift_up(v, j):
    if j == 0:
        return v
    n = v.shape[0]
    row = lax.broadcasted_iota(jnp.int32, v.shape, 0)
    return jnp.where(row < n - j, pltpu.roll(v, n - j, 0), 0.0)


def _ssd_conv_fwd(zx, w, b, *, name):
    L = zx.shape[0]
    cb = 256
    k = w.shape[0]

    def body(x_ref, w_ref, b_ref, o_ref):
        xv = x_ref[...]
        pre = b_ref[...] + xv * w_ref[k - 1:k, :]
        for j in range(1, k):
            pre = pre + _shift_down(xv, j) * w_ref[k - 1 - j:k - j, :]
        o_ref[...] = pre * _sigmoid(pre)

    return pl.pallas_call(
        body, grid=(CONVD // cb,),
        in_specs=[pl.BlockSpec((L, cb), lambda i: (0, i + DI // cb)), pl.BlockSpec((k, cb), lambda i: (0, i)),
                  pl.BlockSpec((1, cb), lambda i: (0, i))],
        out_specs=pl.BlockSpec((L, cb), lambda i: (0, i)), out_shape=jax.ShapeDtypeStruct((L, CONVD), F32),
        compiler_params=_params(("parallel",)), name=name)(zx, w, b)


def _ssd_conv_bwd(zx, dact, w, b, dzx, *, name):
    L = zx.shape[0]
    cb = 256
    k = w.shape[0]

    def body(x_ref, da_ref, w_ref, b_ref, _, dx_ref, s_ref):
        xv = x_ref[...]
        sh = [_shift_down(xv, j) for j in range(k)]
        pre = b_ref[...] + sh[0] * w_ref[k - 1:k, :]
        for j in range(1, k):
            pre = pre + sh[j] * w_ref[k - 1 - j:k - j, :]
        s = _sigmoid(pre)
        dpre = da_ref[...] * (s * (1.0 + pre * (1.0 - s)))
        dx = dpre * w_ref[k - 1:k, :]
        for j in range(1, k):
            dx = dx + _shift_up(dpre, j) * w_ref[k - 1 - j:k - j, :]
        dx_ref[...] = dx.astype(BF16)
        s_ref[...] = jnp.zeros_like(s_ref)
        for j in range(k):
            s_ref[k - 1 - j:k - j, :] = jnp.sum(dpre * sh[j], axis=0, keepdims=True)
        s_ref[k:k + 1, :] = jnp.sum(dpre, axis=0, keepdims=True)

    return pl.pallas_call(
        body, grid=(CONVD // cb,),
        in_specs=[pl.BlockSpec((L, cb), lambda i: (0, i + DI // cb)), pl.BlockSpec((L, cb), lambda i: (0, i)),
                  pl.BlockSpec((k, cb), lambda i: (0, i)), pl.BlockSpec((1, cb), lambda i: (0, i)), ANY],
        out_specs=[pl.BlockSpec((L, cb), lambda i: (0, i + DI // cb)), pl.BlockSpec((8, cb), lambda i: (0, i))],
        out_shape=[jax.ShapeDtypeStruct((L, ZX), BF16), jax.ShapeDtypeStruct((8, CONVD), F32)],
        input_output_aliases={4: 0}, compiler_params=_params(("parallel",)), name=name)(zx, dact, w, b, dzx)


def _sc_fwd(proj, w, *, name):
    L = proj.shape[0]
    cb = 256
    nb = D // cb
    k = w.shape[0]

    def body(b_ref, c_ref, x_ref, w_ref, o_ref):
        u = c_ref[...] * x_ref[...]
        v = u * w_ref[k - 1:k, :]
        for j in range(1, k):
            v = v + _shift_down(u, j) * w_ref[k - 1 - j:k - j, :]
        o_ref[...] = (b_ref[...] * v).astype(BF16)

    return pl.pallas_call(
        body, grid=(nb,),
        in_specs=[pl.BlockSpec((L, cb), lambda i: (0, i)), pl.BlockSpec((L, cb), lambda i: (0, i + nb)),
                  pl.BlockSpec((L, cb), lambda i: (0, i + 2 * nb)), pl.BlockSpec((k, cb), lambda i: (0, i))],
        out_specs=pl.BlockSpec((L, cb), lambda i: (0, i)), out_shape=jax.ShapeDtypeStruct((L, D), BF16),
        compiler_params=_params(("parallel",)), name=name)(proj, proj, proj, w)


def _sc_bwd(proj, dyv, w, *, name):
    L = proj.shape[0]
    cb = 256
    nb = D // cb
    k = w.shape[0]

    def body(b_ref, c_ref, x_ref, dy_ref, w_ref, dp_ref, s_ref):
        cv, xv = c_ref[...], x_ref[...]
        u = cv * xv
        sh = [_shift_down(u, j) for j in range(k)]
        v = sh[0] * w_ref[k - 1:k, :]
        for j in range(1, k):
            v = v + sh[j] * w_ref[k - 1 - j:k - j, :]
        dyv_ = dy_ref[...]
        dp_ref[0] = (dyv_ * v).astype(BF16)
        dv = dyv_ * b_ref[...]
        du = dv * w_ref[k - 1:k, :]
        for j in range(1, k):
            du = du + _shift_up(dv, j) * w_ref[k - 1 - j:k - j, :]
        dp_ref[1] = (du * xv).astype(BF16)
        dp_ref[2] = (du * cv).astype(BF16)
        s_ref[...] = jnp.zeros_like(s_ref)
        for j in range(k):
            s_ref[k - 1 - j:k - j, :] = jnp.sum(dv * sh[j], axis=0, keepdims=True)

    blk = pl.BlockSpec((L, cb), lambda i: (0, i))
    return pl.pallas_call(
        body, grid=(nb,),
        in_specs=[blk, pl.BlockSpec((L, cb), lambda i: (0, i + nb)), pl.BlockSpec((L, cb), lambda i: (0, i + 2 * nb)),
                  blk, pl.BlockSpec((k, cb), lambda i: (0, i))],
        out_specs=[pl.BlockSpec((3, L, cb), lambda i: (0, 0, i)), pl.BlockSpec((8, cb), lambda i: (0, i))],
        out_shape=[jax.ShapeDtypeStruct((3, L, D), BF16), jax.ShapeDtypeStruct((8, D), F32)],
        compiler_params=_params(("parallel",)), name=name)(proj, proj, proj, dyv, w)


def _pieces(v, n):
    out, rest = [], v
    for _ in range(n):
        out.append(rest.astype(BF16))
        rest = rest - out[-1].astype(F32)
    return out


def _cumsum_rows(mask, v):
    m = mask.astype(BF16)
    return _dot(jnp.concatenate([m, m, m], axis=1), jnp.concatenate(_pieces(v, 3), axis=0))


def _ssd_chunk_terms(dtr, prm):
    lane = lax.broadcasted_iota(jnp.int32, (CH, LANES), 1)
    valid = lane < NH
    xdt = dtr + prm[0:1, :]
    dt = jnp.where(valid, jnp.maximum(xdt, 0.0) + jnp.log1p(jnp.exp(-jnp.abs(xdt))), 0.0)
    A = -jnp.exp(prm[1:2, :])
    ri = lax.broadcasted_iota(jnp.int32, (CH, CH), 0)
    ci = lax.broadcasted_iota(jnp.int32, (CH, CH), 1)
    cs = _cumsum_rows(ri >= ci, dt * A)
    last = cs[CH - 1:CH, :]
    spread = (lax.broadcasted_iota(jnp.int32, (2 * LANES, DI), 1) // HP
              == lax.broadcasted_iota(jnp.int32, (2 * LANES, DI), 0) % LANES).astype(BF16)
    gather = ((lax.broadcasted_iota(jnp.int32, (LANES, 2 * DI), 1) % DI) // HP
              == lax.broadcasted_iota(jnp.int32, (LANES, 2 * DI), 0)).astype(BF16)
    return dict(valid=valid, xdt=xdt, dt=dt, A=A, cs=cs, csT=cs.T, last=last, ri=ri, ci=ci, ex=(spread, gather))


def _expand(v, ex):
    if v.shape[0] == 1:
        return _expand(jnp.broadcast_to(v, (8, LANES)), ex)[0:1, :]
    return _dot(jnp.concatenate(_pieces(v, 2), axis=1), ex[0])


def _head_sum(v, ex):
    if v.shape[0] == 1:
        return _head_sum(jnp.broadcast_to(v, (8, DI)), ex)[0:1, :]
    return _dot_nt(jnp.concatenate(_pieces(v, 2), axis=1), ex[1])


def _ssd_fwd(xbc, dtr, prm, *, name):
    L = xbc.shape[0]
    nc = L // CH

    def body(xbc_ref, dtr_ref, prm_ref, y_ref, sp_ref, st_ref):
        @pl.when(pl.program_id(0) == 0)
        def _():
            st_ref[...] = jnp.zeros_like(st_ref)

        prm_v = prm_ref[...]
        t = _ssd_chunk_terms(dtr_ref[...], prm_v)
        cs, csT, ex, causal = t["cs"], t["csT"], t["ex"], t["ri"] >= t["ci"]
        xs = xbc_ref[:, 0:DI]
        X = xs * _expand(t["dt"], ex)
        Xb = X.astype(BF16)
        Xd = (X * _expand(jnp.exp(t["last"] - cs), ex)).astype(BF16)
        Ex = _expand(jnp.exp(cs), ex)
        cdx = _expand(jnp.exp(t["last"]), ex)
        dskx = _expand(prm_v[2:3, :], ex)
        lane = lax.broadcasted_iota(jnp.int32, (CH, LANES), 1)
        sp_ref[0] = st_ref[...]
        for g in range(NG):
            Bg = xbc_ref[:, DI + g * NS:DI + (g + 1) * NS].astype(BF16)
            Cg = xbc_ref[:, DI + GW + g * NS:DI + GW + (g + 1) * NS].astype(BF16)
            G = _dot_nt(Cg, Bg)
            Sg = st_ref[:, g * GW:(g + 1) * GW]
            yoff = _dot(Cg, Sg.astype(BF16)) * Ex[:, g * GW:(g + 1) * GW]
            for j in range(GW // LANES):
                lo = g * GW + j * LANES
                Xp = Xb[:, lo:lo + LANES]
                yd = []
                for h in (lo // HP, lo // HP + 1):
                    seg = cs[:, h:h + 1] - csT[h:h + 1, :]
                    yd.append(_dot((G * jnp.where(causal, jnp.exp(seg), 0.0)).astype(BF16), Xp))
                y_ref[:, lo:lo + LANES] = (jnp.where(lane < HP, yd[0], yd[1]) + yoff[:, j * LANES:(j + 1) * LANES]
                                           + dskx[:, lo:lo + LANES] * xs[:, lo:lo + LANES])
            st_ref[:, g * GW:(g + 1) * GW] = Sg * cdx[:, g * GW:(g + 1) * GW] + _dot_tn(Bg, Xd[:, g * GW:(g + 1) * GW])

    return pl.pallas_call(
        body, grid=(nc,),
        in_specs=[pl.BlockSpec((CH, CONVD), lambda c: (c, 0)), pl.BlockSpec((CH, LANES), lambda c: (c, 0)),
                  pl.BlockSpec((8, LANES), lambda c: (0, 0))],
        out_specs=[pl.BlockSpec((CH, DI), lambda c: (c, 0)), pl.BlockSpec((1, NS, DI), lambda c: (c, 0, 0))],
        out_shape=[jax.ShapeDtypeStruct((L, DI), F32), jax.ShapeDtypeStruct((nc, NS, DI), F32)],
        scratch_shapes=[pltpu.VMEM((NS, DI), F32)],
        compiler_params=_params(("arbitrary",)), name=name)(xbc, dtr, prm)


def _ssd_bwd(xbc, dtr, prm, dy, sprev, *, name):
    L = xbc.shape[0]
    nc = L // CH

    def body(xbc_ref, dtr_ref, prm_ref, dy_ref, sp_ref, dxbc_ref, ddtr_ref, s_ref, dst_ref, dx_scr, de_scr, dd_scr):
        step = pl.program_id(0)

        @pl.when(step == 0)
        def _():
            dst_ref[...] = jnp.zeros_like(dst_ref)
            s_ref[...] = jnp.zeros_like(s_ref)

        prm_v = prm_ref[...]
        t = _ssd_chunk_terms(dtr_ref[...], prm_v)
        cs, csT, ex, ri, ci = t["cs"], t["csT"], t["ex"], t["ri"], t["ci"]
        E = jnp.exp(cs)
        dec = jnp.exp(t["last"] - cs)
        cd = jnp.exp(t["last"])
        xs = xbc_ref[:, 0:DI]
        dtx = _expand(t["dt"], ex)
        X = xs * dtx
        Xb = X.astype(BF16)
        decx = _expand(dec, ex)
        Xd = (X * decx).astype(BF16)
        Ex = _expand(E, ex)
        cdx = _expand(cd, ex)
        dskx = _expand(prm_v[2:3, :], ex)
        lane = lax.broadcasted_iota(jnp.int32, (CH, LANES), 1)
        dcs = jnp.zeros((CH, LANES), F32)
        dcd_x = []
        for g in range(NG):
            gs = slice(g * GW, (g + 1) * GW)
            Bg = xbc_ref[:, DI + g * NS:DI + (g + 1) * NS].astype(BF16)
            Cg = xbc_ref[:, DI + GW + g * NS:DI + GW + (g + 1) * NS].astype(BF16)
            G = _dot_nt(Cg, Bg)
            GT = _dot_nt(Bg, Cg)
            Sg = sp_ref[0, :, gs]
            Sgb = Sg.astype(BF16)
            dyg = dy_ref[:, gs]
            de_scr[:, gs] = dyg * _dot(Cg, Sgb)
            dYo = (Ex[:, gs] * dyg).astype(BF16)
            dC = _dot_nt(dYo, Sgb)
            dS_in = _dot_tn(Cg, dYo)
            dStg = dst_ref[:, gs]
            dStb = dStg.astype(BF16)
            dXd = _dot(Bg, dStb)
            dB = _dot_nt(Xd[:, gs], dStb)
            dd_scr[:, gs] = dXd * X[:, gs]
            dXst = dXd * decx[:, gs]
            dG = jnp.zeros((CH, CH), F32)
            dGT = jnp.zeros((CH, CH), F32)
            for j in range(GW // LANES):
                lo = g * GW + j * LANES
                Xp = Xb[:, lo:lo + LANES]
                dyp = dy_ref[:, lo:lo + LANES]
                dXp = dXst[:, j * LANES:(j + 1) * LANES]
                for k, h in enumerate((lo // HP, lo // HP + 1)):
                    dyh = jnp.where((lane < HP) if k == 0 else (lane >= HP), dyp, 0.0).astype(BF16)
                    seg = cs[:, h:h + 1] - csT[h:h + 1, :]
                    Lm = jnp.where(ri >= ci, jnp.exp(seg), 0.0)
                    LmT = jnp.where(ci >= ri, jnp.exp(-seg), 0.0)
                    dM = _dot_nt(dyh, Xp)
                    dMT = _dot_nt(Xp, dyh)
                    MT = GT * LmT
                    rs = jnp.sum(dM * (G * Lm), axis=1, keepdims=True) - jnp.sum(dMT * MT, axis=1, keepdims=True)
                    dcs = dcs + jnp.where(lane == h, rs, 0.0)
                    dG = dG + dM * Lm
                    dGT = dGT + dMT * LmT
                    dXp = dXp + _dot(MT.astype(BF16), dyh)
                dx_scr[:, lo:lo + LANES] = dXp
            dxbc_ref[:, DI + g * NS:DI + (g + 1) * NS] = dB + _dot(dGT.astype(BF16), Cg)
            dxbc_ref[:, DI + GW + g * NS:DI + GW + (g + 1) * NS] = dC + _dot(dG.astype(BF16), Bg)
            dcd_x.append(jnp.sum(dStg * Sg, axis=0, keepdims=True))
            dst_ref[:, gs] = dStg * cdx[:, gs] + dS_in
        dX = dx_scr[...]
        dy = dy_ref[...]
        ddec = _head_sum(dd_scr[...], ex)
        dcd = _head_sum(jnp.concatenate(dcd_x, axis=1), ex)
        dcs = dcs + _head_sum(de_scr[...], ex) * E - ddec * dec
        row = lax.broadcasted_iota(jnp.int32, (CH, LANES), 0)
        dcs = dcs + jnp.where(row == CH - 1, jnp.sum(ddec * dec, axis=0, keepdims=True) + dcd * cd, 0.0)
        da = _cumsum_rows(ci >= ri, dcs)
        ddt = da * t["A"] + _head_sum(dX * xs, ex)
        ddtr = jnp.where(t["valid"], ddt * _sigmoid(t["xdt"]), 0.0)
        ddtr_ref[...] = ddtr
        dxbc_ref[:, 0:DI] = dX * dtx + dskx * dy
        s_ref[0:1, :] += jnp.sum(da * t["dt"], axis=0, keepdims=True)
        s_ref[1:2, :] += _head_sum(jnp.sum(dy * xs, axis=0, keepdims=True), ex)
        s_ref[2:3, :] += jnp.sum(ddtr, axis=0, keepdims=True)

        @pl.when(step == nc - 1)
        def _():
            s_ref[0:1, :] = s_ref[0:1, :] * t["A"]

    rev = lambda c: (nc - 1 - c, 0)
    return pl.pallas_call(
        body, grid=(nc,),
        in_specs=[pl.BlockSpec((CH, CONVD), rev), pl.BlockSpec((CH, LANES), rev), pl.BlockSpec((8, LANES), lambda c: (0, 0)),
                  pl.BlockSpec((CH, DI), rev), pl.BlockSpec((1, NS, DI), lambda c: (nc - 1 - c, 0, 0))],
        out_specs=[pl.BlockSpec((CH, CONVD), rev), pl.BlockSpec((CH, LANES), rev), pl.BlockSpec((8, LANES), lambda c: (0, 0))],
        out_shape=[jax.ShapeDtypeStruct((L, CONVD), F32), jax.ShapeDtypeStruct((L, LANES), F32),
                   jax.ShapeDtypeStruct((8, LANES), F32)],
        scratch_shapes=[pltpu.VMEM((NS, DI), F32), pltpu.VMEM((CH, DI), F32), pltpu.VMEM((CH, DI), F32),
                        pltpu.VMEM((CH, DI), F32)],
        compiler_params=_params(("arbitrary",)), name=name)(xbc, dtr, prm, dy, sprev)


def _gnorm_fwd(y, zx, nw, *, name):
    L = y.shape[0]
    tm = min(L, 256)

    def body(y_ref, z_ref, nw_ref, o_ref):
        z = z_ref[...]
        yg = y_ref[...] * (z * _sigmoid(z))
        for g in range(NG):
            v = yg[:, g * GW:(g + 1) * GW]
            r = lax.rsqrt(jnp.mean(v * v, axis=-1, keepdims=True) + EPS)
            o_ref[:, g * GW:(g + 1) * GW] = (v * r * nw_ref[:, g * GW:(g + 1) * GW]).astype(BF16)

    row = pl.BlockSpec((tm, DI), lambda i: (i, 0))
    return pl.pallas_call(body, grid=(L // tm,), in_specs=[row, row, pl.BlockSpec((1, DI), lambda i: (0, 0))],
                          out_specs=row, out_shape=jax.ShapeDtypeStruct((L, DI), BF16),
                          compiler_params=_params(("parallel",)), name=name)(y, zx, nw)


def _gnorm_bwd(y, zx, nw, dyn, *, name):
    L = y.shape[0]
    tm = min(L, 256)

    def body(y_ref, z_ref, nw_ref, dyn_ref, dy_ref, dz_ref, s_ref):
        @pl.when(pl.program_id(0) == 0)
        def _():
            s_ref[...] = jnp.zeros_like(s_ref)

        z, yv = z_ref[...], y_ref[...]
        sz = _sigmoid(z)
        gate = z * sz
        dgate_dz = sz * (1.0 + z * (1.0 - sz))
        for g in range(NG):
            gs = slice(g * GW, (g + 1) * GW)
            v = yv[:, gs] * gate[:, gs]
            r = lax.rsqrt(jnp.mean(v * v, axis=-1, keepdims=True) + EPS)
            vhat = v * r
            dn = dyn_ref[:, gs]
            s_ref[0:1, gs] += jnp.sum(dn * vhat, axis=0, keepdims=True)
            dvhat = dn * nw_ref[:, gs]
            dv = r * (dvhat - vhat * jnp.mean(dvhat * vhat, axis=-1, keepdims=True))
            dy_ref[:, gs] = dv * gate[:, gs]
            dz_ref[:, gs] = (dv * yv[:, gs] * dgate_dz[:, gs]).astype(BF16)

    row = pl.BlockSpec((tm, DI), lambda i: (i, 0))
    return pl.pallas_call(body, grid=(L // tm,), in_specs=[row, row, pl.BlockSpec((1, DI), lambda i: (0, 0)), row],
                          out_specs=[row, row, pl.BlockSpec((8, DI), lambda i: (0, 0))],
                          out_shape=[jax.ShapeDtypeStruct((L, DI), F32), jax.ShapeDtypeStruct((L, ZX), BF16),
                                     jax.ShapeDtypeStruct((8, DI), F32)],
                          compiler_params=_params(("arbitrary",)), name=name)(y, zx, nw, dyn)


def _adamw(w, g, m, v, *, name, g_row=0, w_row=0, rows=None, into=None, emit_g=False):
    R, C = w.shape
    rows = R if rows is None else rows
    tr = rows
    while tr * C > 256 * 1024 and tr % 16 == 0:
        tr //= 2
    assert g_row % tr == 0 and w_row % tr == 0, (name, g_row, w_row, tr)
    n_out = 4 if emit_g else 3

    def body(w_ref, g_ref, m_ref, v_ref, *rest):
        outs = rest[-n_out:]
        gv = g_ref[...]
        mn = ADAM_B1 * m_ref[...] + (1.0 - ADAM_B1) * gv
        vn = ADAM_B2 * v_ref[...] + (1.0 - ADAM_B2) * (gv * gv)
        m_hat = mn / (1.0 - ADAM_B1 ** ADAM_STEP)
        v_hat = vn / (1.0 - ADAM_B2 ** ADAM_STEP)
        d_ref, mo_ref, vo_ref = outs[-3:]
        d_ref[...] = -ADAM_LR * (m_hat / (jnp.sqrt(v_hat) + ADAM_EPS) + ADAM_WD * w_ref[...])
        mo_ref[...] = mn
        vo_ref[...] = vn
        if emit_g:
            outs[0][...] = gv

    blk = pl.BlockSpec((tr, C), lambda i: (i + w_row // tr, 0))
    args, in_specs, alias = [w, g, m, v], [blk, pl.BlockSpec((tr, C), lambda i: (i + g_row // tr, 0)), blk, blk], {}
    if into is not None:
        args, in_specs, alias = args + list(into), in_specs + [ANY] * n_out, {4 + k: k for k in range(n_out)}
    return pl.pallas_call(body, grid=(rows // tr,), in_specs=in_specs, out_specs=[blk] * n_out,
                          out_shape=[jax.ShapeDtypeStruct((R, C), F32)] * n_out, input_output_aliases=alias,
                          compiler_params=_params(("parallel",)), name=name)(*args)


def _residual(acc, xv, gv):
    return xv + gv * acc, acc


def _relu2(acc):
    a = jnp.maximum(acc, 0.0)
    return a, a * a


def _like(buf):
    return jax.ShapeDtypeStruct(buf.shape, buf.dtype)


def _mlp_fwd(x, mod, nw, wb, up_row, down_row, tag):
    sh, sc, g = mod
    h = _modnorm_fwd(x, nw, sc, sh, name=tag + "_norm")
    a, act = _matmul(h, wb, n=DFF, tm=TM_ALL, b_spec=pl.BlockSpec((None, D, 512), lambda mi, j: (j // 2, up_row // D, j % 2)),
                     epi=_relu2, out_dtypes=(BF16, BF16), name=tag + "_up")
    xn, y = _matmul(act, wb, n=D, tm=TM_HALF, contract=_nn_split,
                    b_spec=pl.BlockSpec((N_CHIPS, D, 512), lambda mi, j: (0, down_row // D, j)),
                    extras=(x, g), epi=_residual, out_dtypes=(F32, F32), name=tag + "_down")
    return xn, (x, h, a, act, y)


def _mlp_bwd(dxo, saved, mod, nw, wb, gb, up_row, down_row, tag):
    x, h, a, act, y = saved
    sh, sc, g = mod
    dy, gsum = _gate_bwd(dxo, y, g, name=tag + "_dgate")
    du = _matmul(dy, wb, n=DFF, tm=TM_ALL, contract=_nt,
                 b_spec=pl.BlockSpec((None, 512, D), lambda mi, j: (j // 2, down_row // 512 + j % 2, 0)),
                 extras=(a,), epi=lambda acc, av: (acc * (2.0 * av.astype(F32)),), out_dtypes=(BF16,), name=tag + "_dact")
    gb = _matmul_tn(act, dy, m=DFF, n=D, tm=D, tn=D, into=gb, out_struct=_like(wb),
                    out_spec=pl.BlockSpec((None, D, D), lambda mi, j: (mi, down_row // D, 0)), name=tag + "_ddown")
    dh = _matmul(du, wb, n=D, tm=TM_HALF, contract=_nt_split,
                 b_spec=pl.BlockSpec((N_CHIPS, 512, D), lambda mi, j: (0, up_row // 512 + j, 0)), name=tag + "_dh")
    gb = _matmul_tn(h, du, m=D, n=DFF, tm=D, into=gb, out_struct=_like(wb),
                    out_spec=pl.BlockSpec((None, D, 512), lambda mi, j: (j // 2, up_row // D, j % 2)), name=tag + "_dup")
    dx, sums = _modnorm_bwd(x, dh, dxo, nw, sc, gsum, name=tag + "_dnorm")
    return dx, gb, sums


def _ssd_fwd_scan(x, mod, nw, w_zx, w_dt, conv_w, conv_b, prm, tag):
    sh, sc, g = mod
    h = _modnorm_fwd(x, nw, sc, sh, name=tag + "_norm")
    zx = _matmul(h, w_zx, n=ZX, tm=TM_ALL, name=tag + "_in")
    dtr = _matmul(h, w_dt, n=LANES, tm=TM_ALL, name=tag + "_in_dt")
    xbc = _ssd_conv_fwd(zx, conv_w, conv_b, name=tag + "_conv")
    y, sprev = _ssd_fwd(xbc, dtr, prm, name=tag + "_scan")
    return h, zx, dtr, xbc, y, sprev


def _ssd_fwd_out(x, mod, scan, gn_w, w_out, tag):
    sh, sc, g = mod
    h, zx, dtr, xbc, y, sprev = scan
    yn = _gnorm_fwd(y, zx, gn_w, name=tag + "_gnorm")
    xn, yo = _matmul(yn, w_out, n=D, tm=TM_HALF, contract=_nn_split,
                     b_spec=pl.BlockSpec((N_CHIPS, 512, 512), lambda mi, j: (0, 0, j)),
                     extras=(x, g), epi=_residual, out_dtypes=(F32, F32), name=tag + "_out")
    return xn, (x, h, zx, dtr, xbc, y, sprev, yn, yo)


def _ssd_bwd_out(dxo, saved, mod, w_out, tag):
    x, h, zx, dtr, xbc, y, sprev, yn, yo = saved
    sh, sc, g = mod
    dyo, gsum = _gate_bwd(dxo, yo, g, name=tag + "_dgate")
    dyn = _matmul(dyo, w_out, n=DI, tm=TM_ALL, contract=_nt, b_spec=pl.BlockSpec((None, 512, D), lambda mi, j: (j, 0, 0)),
                  name=tag + "_dyn")
    g_out = _matmul_tn(yn, dyo, m=DI, n=D, tn=D, out_struct=_like(w_out),
                       out_spec=pl.BlockSpec((None, 512, D), lambda mi, j: (mi, 0, 0)), name=tag + "_dout")
    return dyn, g_out, gsum


def _ssd_bwd_rest(dxo, dyn, gsum, saved, mod, nw, w_zx, w_dt, conv_w, conv_b, prm, gn_w, tag):
    x, h, zx, dtr, xbc, y, sprev, yn, yo = saved
    sh, sc, g = mod
    dy, dzx, gnsum = _gnorm_bwd(y, zx, gn_w, dyn, name=tag + "_dgnorm")
    dxbc, ddtr, ssum = _ssd_bwd(xbc, dtr, prm, dy, sprev, name=tag + "_dscan")
    dzx, csum = _ssd_conv_bwd(zx, dxbc, conv_w, conv_b, dzx, name=tag + "_dconv")
    dh_dt = _matmul(ddtr, w_dt, n=D, tm=TM_ALL, contract=_nt, name=tag + "_dh_dt")
    dh = _matmul(dzx, w_zx, n=D, tm=TM_HALF, contract=_nt, extras=(dh_dt,), epi=lambda acc, e: (acc + e,), name=tag + "_dh")
    d_w_zx = _matmul_tn(h, dzx, m=D, n=ZX, tm=D, name=tag + "_din")
    d_w_dt = _matmul_tn(h, ddtr, m=D, n=LANES, tm=D, name=tag + "_din_dt")
    dx, sums = _modnorm_bwd(x, dh, dxo, nw, sc, gsum, name=tag + "_dnorm")
    return dx, d_w_zx, d_w_dt, sums, csum, gnsum, ssum


def _sc_layer_fwd(x, mod, nw, w_sc_in, conv_w, wb, out_row, tag):
    sh, sc, g = mod
    h = _modnorm_fwd(x, nw, sc, sh, name=tag + "_norm")
    proj = _matmul(h, w_sc_in, n=3 * D, tm=TM_ALL, tn=256, b_spec=pl.BlockSpec((None, D, 256), lambda mi, j: (j // 3, 0, j % 3)),
                   name=tag + "_in")
    yv = _sc_fwd(proj, conv_w, name=tag + "_conv")
    xn, yo = _matmul(yv, wb, n=D, tm=TM_HALF, contract=_nn_split,
                     b_spec=pl.BlockSpec((N_CHIPS, 256, 512), lambda mi, j: (0, out_row // 256, j)),
                     extras=(x, g), epi=_residual, out_dtypes=(F32, F32), name=tag + "_out")
    return xn, (x, h, proj, yv, yo)


def _sc_layer_bwd(dxo, saved, mod, nw, w_sc_in, conv_w, wb, gb, out_row, tag):
    x, h, proj, yv, yo = saved
    sh, sc, g = mod
    L = x.shape[0]
    dyo, gsum = _gate_bwd(dxo, yo, g, name=tag + "_dgate")
    dyv = _matmul(dyo, wb, n=D, tm=TM_ALL, tn=256, contract=_nt,
                  b_spec=pl.BlockSpec((None, 256, D), lambda mi, j: (j, out_row // 256, 0)), name=tag + "_dyv")
    gb = _matmul_tn(yv, dyo, m=D, n=D, tm=256, tn=D, into=gb, out_struct=_like(wb),
                    out_spec=pl.BlockSpec((None, 256, D), lambda mi, j: (mi, out_row // 256, 0)), name=tag + "_dout")
    dproj, csum = _sc_bwd(proj, dyv, conv_w, name=tag + "_dconv")
    tm = min(L, TM_HALF)
    dh = _matmul(dproj, w_sc_in, n=D, tm=tm, contract=_nt_sc_in, a_spec=pl.BlockSpec((3, tm, D), lambda mi, j: (0, mi, 0)),
                 b_spec=pl.BlockSpec((N_CHIPS, 512, SC_IN_SHARD), lambda mi, j: (0, j, 0)), name=tag + "_dh")
    g_sc_in = _matmul_tn(h, dproj, m=D, n=3 * D, tm=D, tn=256, b_spec=pl.BlockSpec((None, L, 256), lambda mi, j: (j // 4, 0, j % 4)),
                         out_spec=pl.BlockSpec((None, D, 256), lambda mi, j: (j // 3, 0, j % 3)),
                         out_struct=jax.ShapeDtypeStruct((N_CHIPS, D, SC_IN_SHARD), BF16), name=tag + "_din")
    dx, sums = _modnorm_bwd(x, dh, dxo, nw, sc, gsum, name=tag + "_dnorm")
    return dx, gb, g_sc_in, sums, csum


SUB_ROW = (0, 8, 16, 24)
SSD_CONV_ROW, GNORM_ROW, FINAL_ROW, SC_CONV_ROW, HEAD_ROW, SMALL_ROWS = 32, 56, 72, 80, 88, 96


def _all_gather_rows(blk, *, name):
    m_per, n = blk.shape

    def body(x_ref, out_ref, send_sems, recv_sems, local_sem):
        x, y, c = lax.axis_index("x"), lax.axis_index("y"), lax.axis_index("c")
        me, sibling = (x, y, c), (x, y, 1 - c)
        chips = [(1 - x, y), (x, 1 - y), (1 - x, 1 - y)]

        def rows(px, py, pc):
            return out_ref.at[pl.ds((4 * px + 2 * py + pc) * m_per, m_per), :]

        def copy(k, block, to, src=None):
            return pltpu.make_async_remote_copy(src_ref=rows(*block) if src is None else src, dst_ref=rows(*block),
                                                send_sem=send_sems.at[k], recv_sem=recv_sems.at[k], device_id=to,
                                                device_id_type=MESH)

        mine = pltpu.make_async_copy(x_ref, rows(*me), local_sem)
        mine.start()
        first = [copy(0, me, sibling, src=x_ref)] + [copy(1 + j, me, (*chip, c), src=x_ref) for j, chip in enumerate(chips)]
        for cp in first:
            cp.start()
        passed = [copy(4 + j, (*chip, c), sibling) for j, chip in enumerate(chips)]
        for j, chip in enumerate(chips):
            copy(1 + j, (*chip, c), me).wait_recv()
            passed[j].start()
        copy(0, sibling, me).wait_recv()
        for j, chip in enumerate(chips):
            copy(4 + j, (*chip, 1 - c), me).wait_recv()
        for cp in first + passed:
            cp.wait_send()
        mine.wait()

    return pl.pallas_call(
        body, out_shape=jax.ShapeDtypeStruct((N_DEV * m_per, n), blk.dtype),
        in_specs=[pl.BlockSpec(memory_space=pltpu.VMEM)], out_specs=pl.BlockSpec(memory_space=pltpu.VMEM),
        scratch_shapes=[pltpu.SemaphoreType.DMA((7,)), pltpu.SemaphoreType.DMA((7,)), pltpu.SemaphoreType.DMA],
        name=name)(blk)


def _half(ref, chip, c):
    hr = ref.shape[1] // 2
    return ref.at[chip, pl.ds(c * hr, hr), :]


def _gather_copy(bufs, sends, recvs, b, k, chip, pc, to):
    piece = _half(bufs[b], 2 * chip[0] + chip[1], pc)
    return pltpu.make_async_remote_copy(src_ref=piece, dst_ref=piece, send_sem=sends.at[4 * b + k], recv_sem=recvs.at[4 * b + k],
                                        device_id=to, device_id_type=MESH)


def _split_call(body, bufs, sems_in, n_sems, *, name, after=(), token=False):
    nb, na, starts = len(bufs), len(after), not sems_in

    def wrapped(*refs):
        sems = refs[nb + na:nb + na + 2] if starts else refs[nb:nb + 2]
        body(refs[:nb], sems[0], sems[1])
        if token:
            refs[-1][...] = jnp.zeros_like(refs[-1])

    out_shape = [pltpu.SemaphoreType.DMA((n_sems,)) for _ in range(2 if starts else 0)]
    out_specs = [SEM] * len(out_shape) + [HBM] * nb
    alias = {b: len(out_shape) + b for b in range(nb)}
    out_shape += [pltpu.HBM(b.shape, b.dtype) for b in bufs]
    if token:
        out_shape.append(jax.ShapeDtypeStruct((8, LANES), F32))
        out_specs.append(pl.BlockSpec(memory_space=pltpu.VMEM))
    return pl.pallas_call(
        wrapped, out_shape=out_shape, in_specs=[HBM] * nb + [SEM] * len(sems_in) + [ANY] * na, out_specs=out_specs,
        input_output_aliases=alias,
        compiler_params=pltpu.CompilerParams(has_side_effects=pltpu.SideEffectType.DATAFLOW_SIDE_EFFECTING),
        name=name)(*[pltpu.with_memory_space_constraint(b, pltpu.HBM) for b in bufs], *sems_in, *after)


def _gather_start(bufs, *, name, after=()):
    nb = len(bufs)

    def body(ins, sends, recvs):
        x, y, c = lax.axis_index("x"), lax.axis_index("y"), lax.axis_index("c")
        chips = [(1 - x, y), (x, 1 - y), (1 - x, 1 - y)]
        for b in range(nb):
            _gather_copy(ins, sends, recvs, b, 0, (x, y), c, (x, y, 1 - c)).start()
            for j, chip in enumerate(chips):
                _gather_copy(ins, sends, recvs, b, 1 + j, (x, y), c, (*chip, c)).start()

    out = _split_call(body, bufs, (), 4 * nb, name=name, after=after, token=True)
    return (out[0], out[1], out[2:2 + nb]), out[-1]


def _gather_wait_first(flight, *, name, after=()):
    sends, recvs, bufs = flight
    nb = len(bufs)

    def body(ins, sends_, recvs_):
        x, y, c = lax.axis_index("x"), lax.axis_index("y"), lax.axis_index("c")
        chips = [(1 - x, y), (x, 1 - y), (1 - x, 1 - y)]
        for b in range(nb):
            _gather_copy(ins, sends_, recvs_, b, 0, (x, y), c, (x, y, 1 - c)).wait_send()
            _gather_copy(ins, sends_, recvs_, b, 0, (x, y), 1 - c, (x, y, c)).wait_recv()
            for j, chip in enumerate(chips):
                _gather_copy(ins, sends_, recvs_, b, 1 + j, (x, y), c, (*chip, c)).wait_send()
                _gather_copy(ins, sends_, recvs_, b, 1 + j, chip, c, (x, y, c)).wait_recv()

    return _split_call(body, bufs, (sends, recvs), 4 * nb, name=name, after=after)


def _gather_forward(bufs, *, name):
    nb = len(bufs)

    def body(ins, sends, recvs):
        x, y, c = lax.axis_index("x"), lax.axis_index("y"), lax.axis_index("c")
        chips = [(1 - x, y), (x, 1 - y), (1 - x, 1 - y)]
        for b in range(nb):
            for j, chip in enumerate(chips):
                _gather_copy(ins, sends, recvs, b, 1 + j, chip, c, (x, y, 1 - c)).start()

    out = _split_call(body, bufs, (), 4 * nb, name=name)
    return out[0], out[1], out[2:2 + nb]


def _gather_wait_forward(flight, *, name, after=()):
    sends, recvs, bufs = flight
    nb = len(bufs)

    def body(ins, sends_, recvs_):
        x, y, c = lax.axis_index("x"), lax.axis_index("y"), lax.axis_index("c")
        chips = [(1 - x, y), (x, 1 - y), (1 - x, 1 - y)]
        for b in range(nb):
            for j, chip in enumerate(chips):
                _gather_copy(ins, sends_, recvs_, b, 1 + j, chip, c, (x, y, 1 - c)).wait_send()
                _gather_copy(ins, sends_, recvs_, b, 1 + j, chip, 1 - c, (x, y, c)).wait_recv()

    return _split_call(body, bufs, (sends, recvs), 4 * nb, name=name, after=after)


def _owner_copies(hs, lands, sends, recvs):
    x, y, c = lax.axis_index("x"), lax.axis_index("y"), lax.axis_index("c")
    chips = [(1 - x, y), (x, 1 - y), (1 - x, 1 - y)]
    return [pltpu.make_async_remote_copy(src_ref=hs[b].at[2 * cx + cy], dst_ref=lands[b].at[j], send_sem=sends.at[3 * b + j],
                                         recv_sem=recvs.at[3 * b + j], device_id=(cx, cy, c), device_id_type=MESH)
            for b in range(len(hs)) for j, (cx, cy) in enumerate(chips)]


def _owners_start(hs, *, name):
    nb = len(hs)
    lands = [lax.empty((3,) + h.shape[1:], h.dtype) for h in hs]

    def body(refs, sends, recvs):
        for cp in _owner_copies(refs[:nb], refs[nb:], sends, recvs):
            cp.start()

    out = _split_call(body, list(hs) + lands, (), 3 * nb, name=name, token=True)
    return (out[0], out[1], out[2:2 + 2 * nb]), out[-1]


def _owners_wait(flight, *, name, after=()):
    sends, recvs, bufs = flight
    nb = len(bufs) // 2

    def body(refs, sends_, recvs_):
        for cp in _owner_copies(refs[:nb], refs[nb:], sends_, recvs_):
            cp.wait()

    return _split_call(body, bufs, (sends, recvs), 3 * nb, name=name, after=after)[nb:]


def _swap_halves_with_sibling(bufs, *, name, after=()):
    nb, na = len(bufs), len(after)

    def body(*refs):
        ins, outs, send_sems, recv_sems = refs[:nb], refs[nb + na:2 * nb + na], refs[2 * nb + na], refs[2 * nb + na + 1]
        x, y, c = lax.axis_index("x"), lax.axis_index("y"), lax.axis_index("c")
        copies = []
        for b in range(nb):
            hr = ins[b].shape[1] // 2
            copies.append(pltpu.make_async_remote_copy(
                src_ref=ins[b].at[:, pl.ds((1 - c) * hr, hr), :], dst_ref=outs[b], send_sem=send_sems.at[b],
                recv_sem=recv_sems.at[b], device_id=(x, y, 1 - c), device_id_type=MESH))
        for cp in copies:
            cp.start()
        for cp in copies:
            cp.wait()

    return pl.pallas_call(
        body, out_shape=[jax.ShapeDtypeStruct((b.shape[0], b.shape[1] // 2, b.shape[2]), b.dtype) for b in bufs],
        in_specs=[ANY] * (nb + na), out_specs=[ANY] * nb,
        scratch_shapes=[pltpu.SemaphoreType.DMA((nb,)), pltpu.SemaphoreType.DMA((nb,))], name=name)(*bufs, *after)


def _swap_results_with_sibling(ts, *, name):
    nb = len(ts)

    def body(*refs):
        outs, send_sems, recv_sems = refs[nb:2 * nb], refs[2 * nb], refs[2 * nb + 1]
        x, y, c = lax.axis_index("x"), lax.axis_index("y"), lax.axis_index("c")
        copies = [pltpu.make_async_remote_copy(src_ref=outs[b].at[c], dst_ref=outs[b].at[c], send_sem=send_sems.at[b],
                                               recv_sem=recv_sems.at[b], device_id=(x, y, 1 - c), device_id_type=MESH)
                  for b in range(nb)]
        for cp in copies:
            cp.start()
        for cp in copies:
            cp.wait()

    return pl.pallas_call(
        body, out_shape=[jax.ShapeDtypeStruct(t.shape, t.dtype) for t in ts], in_specs=[ANY] * nb, out_specs=[ANY] * nb,
        scratch_shapes=[pltpu.SemaphoreType.DMA((nb,)), pltpu.SemaphoreType.DMA((nb,))],
        input_output_aliases={b: b for b in range(nb)}, name=name)(*ts)


def _row_tile(rows, cols):
    best = 16
    for t in range(16, rows + 1, 16):
        if rows % t == 0 and t * cols <= 640 * 1024:
            best = t
    assert rows % best == 0, (rows, cols)
    return best


def _add_sibling_half(g, recv, core, *, name):
    nk, r, n = g.shape
    hr = r // 2
    tr = _row_tile(hr, n)

    def body(c_ref, a_ref, b_ref, o_ref):
        o_ref[...] = (a_ref[...].astype(F32) + b_ref[...].astype(F32)).astype(BF16)

    grid_spec = pltpu.PrefetchScalarGridSpec(
        num_scalar_prefetch=1, grid=(nk, hr // tr),
        in_specs=[pl.BlockSpec((None, tr, n), lambda k, i, c_ref: (k, c_ref[0] * (hr // tr) + i, 0)),
                  pl.BlockSpec((None, tr, n), lambda k, i, c_ref: (k, i, 0))],
        out_specs=pl.BlockSpec((None, tr, n), lambda k, i, c_ref: (k, i, 0)))
    return pl.pallas_call(body, grid_spec=grid_spec, out_shape=jax.ShapeDtypeStruct((nk, hr, n), BF16),
                          compiler_params=_params(("parallel", "parallel")), name=name)(core, g, recv)


def _add_chip_sums(h, recv, chip_core, *, name):
    _, hr, n = h.shape
    tr = _row_tile(hr, n)

    def body(k_ref, a_ref, b_ref, o_ref):
        o_ref[...] = ((a_ref[...].astype(F32) + b_ref[0].astype(F32)) + b_ref[1].astype(F32)) + b_ref[2].astype(F32)

    grid_spec = pltpu.PrefetchScalarGridSpec(
        num_scalar_prefetch=1, grid=(hr // tr,),
        in_specs=[pl.BlockSpec((None, tr, n), lambda i, k_ref: (k_ref[0], i, 0)),
                  pl.BlockSpec((3, tr, n), lambda i, k_ref: (0, i, 0))],
        out_specs=pl.BlockSpec((None, tr, n), lambda i, k_ref: (k_ref[1], i, 0)))
    return pl.pallas_call(body, grid_spec=grid_spec, out_shape=jax.ShapeDtypeStruct((2, hr, n), F32),
                          compiler_params=_params(("parallel",)), name=name)(chip_core, h, recv)


def _sum_devices(g, *, name):
    nd, r, n = g.shape

    def body(g_ref, o_ref):
        acc = g_ref[0]
        for i in range(1, nd):
            acc = acc + g_ref[i]
        o_ref[...] = acc

    return pl.pallas_call(body, out_shape=jax.ShapeDtypeStruct((r, n), F32), name=name)(g)


def _own_slot(shard, chip):
    return lax.dynamic_update_slice(jnp.zeros((N_CHIPS,) + shard.shape, BF16), shard[None], (chip, 0, 0))


def kernel(x, c, ada_w, ada_b, mix_norm_w, mlp_norm_w, mlp_up, mlp_down, ssd_in_w, ssd_conv_w, ssd_conv_b, ssd_dt_bias, ssd_A_log, ssd_D, ssd_norm_w, ssd_out_w, sc_in_w, sc_conv_w, sc_out_w, final_norm_w, loss_target, m_ada_w, m_ada_b, m_mix_norm_w, m_mlp_norm_w, m_mlp_up, m_mlp_down, m_ssd_in_w, m_ssd_conv_w, m_ssd_conv_b, m_ssd_dt_bias, m_ssd_A_log, m_ssd_D, m_ssd_norm_w, m_ssd_out_w, m_sc_in_w, m_sc_conv_w, m_sc_out_w, m_final_norm_w, v_ada_w, v_ada_b, v_mix_norm_w, v_mlp_norm_w, v_mlp_up, v_mlp_down, v_ssd_in_w, v_ssd_conv_w, v_ssd_conv_b, v_ssd_dt_bias, v_ssd_A_log, v_ssd_D, v_ssd_norm_w, v_ssd_out_w, v_sc_in_w, v_sc_conv_w, v_sc_out_w, v_final_norm_w):
    xi, yi, ci = lax.axis_index("x"), lax.axis_index("y"), lax.axis_index("c")
    chip = 2 * xi + yi
    dev = 2 * chip + ci
    n_ada = ada_w.shape[2]

    conv_flat = jnp.concatenate([ssd_conv_w.reshape(-1), sc_conv_w.reshape(-1), jnp.zeros((256,), F32)]).reshape(4, D)
    blk0 = jnp.concatenate([c, conv_flat, jnp.zeros((3, D), F32)], axis=0)
    got0 = _all_gather_rows(blk0, name="gather_cond").reshape(N_DEV, 8, D)
    c_all = got0[:, 0]
    conv_all = got0[0::2, 1:5].reshape(N_CHIPS, 4 * D)
    ssd_conv = jnp.moveaxis(conv_all[:, :4 * 768].reshape(N_CHIPS, 4, 768), 0, 1).reshape(4, CONVD)
    sc_conv = jnp.moveaxis(conv_all[:, 4 * 768:4 * 768 + 3 * 256].reshape(N_CHIPS, 3, 256), 0, 1).reshape(3, D)
    mod_shard = [_matmul(c_all, ada_w[i], n=n_ada, a_silu=True,
                         extras=(lax.dynamic_slice(ada_b, (i, chip * n_ada), (1, n_ada)),),
                         epi=lambda acc, b: (acc + b,), name=f"ada_mod{i}") for i in range(2)]
    mod_all = _all_gather_rows(jnp.concatenate(mod_shard, axis=0), name="gather_mod")
    mod_all = mod_all.reshape(N_DEV, 2, N_DEV, n_ada)[0::2]
    mod = jnp.moveaxis(lax.dynamic_index_in_dim(mod_all, dev, axis=2, keepdims=False), 0, 1).reshape(2, 6, D)
    mods = [[mod[i, j:j + 1] for j in range(6)] for i in range(2)]

    bf = lambda v: v.astype(BF16)
    up_row, down_row, sc_out_row = 0, D, 2 * D
    a_bufs = [_own_slot(bf(ssd_in_w[0]), chip)]
    b_bufs = [_own_slot(bf(ssd_out_w[0]), chip), _own_slot(bf(jnp.concatenate([mlp_up[0], mlp_down[0]], axis=0)), chip)]
    c_bufs = [_own_slot(bf(sc_in_w[0]), chip), _own_slot(bf(jnp.concatenate([mlp_up[1], mlp_down[1], sc_out_w[0]], axis=0)), chip)]
    fly_a, tok = _gather_start(a_bufs, name="gather_a_start", after=(mod,))
    fly_b, tok = _gather_start(b_bufs, name="gather_b_start", after=(tok,))
    fly_c, tok = _gather_start(c_bufs, name="gather_c_start", after=(tok,))

    row = lambda v: v.reshape(1, -1)
    xs, tgt = x[0], loss_target[0]
    prm = jnp.pad(jnp.concatenate([ssd_dt_bias, ssd_A_log, ssd_D, jnp.zeros((5, NH), F32)], axis=0), ((0, 0), (0, LANES - NH)))
    mix_nw = [row(mix_norm_w[i]) for i in range(2)]
    mlp_nw = [row(mlp_norm_w[i]) for i in range(2)]
    a_bufs = _gather_wait_first(fly_a, name="gather_a_landed", after=(tok,))
    (w_ssd_in,) = _gather_wait_forward(_gather_forward(a_bufs, name="gather_a_pass"), name="gather_a_done")
    ssd_in_full = jnp.moveaxis(w_ssd_in, 0, 1).reshape(D, N_CHIPS * SSD_IN_SHARD)
    w_zx, w_dt = ssd_in_full[:, :ZX], jnp.pad(ssd_in_full[:, ZX:], ((0, 0), (0, LANES - NH)))
    scan = _ssd_fwd_scan(xs, mods[0][0:3], mix_nw[0], w_zx, w_dt, ssd_conv, ssd_conv_b, prm, "ssd")
    fly_b = _gather_forward(_gather_wait_first(fly_b, name="gather_b_landed", after=(scan[3],)), name="gather_b_pass")
    w_ssd_out, w_b = _gather_wait_forward(fly_b, name="gather_b_done", after=(scan[4],))
    x1, s_ssd = _ssd_fwd_out(xs, mods[0][0:3], scan, ssd_norm_w, w_ssd_out, "ssd")
    x2, s_mlp0 = _mlp_fwd(x1, mods[0][3:6], mlp_nw[0], w_b, up_row, down_row, "mlp0")
    c_bufs = _gather_wait_first(fly_c, name="gather_c_landed", after=(x2,))
    w_sc_in, w_c = _gather_wait_forward(_gather_forward(c_bufs, name="gather_c_pass"), name="gather_c_done")
    x3, s_sc = _sc_layer_fwd(x2, mods[1][0:3], mix_nw[1], w_sc_in, sc_conv, w_c, sc_out_row, "sc")
    x4, s_mlp1 = _mlp_fwd(x3, mods[1][3:6], mlp_nw[1], w_c, up_row, down_row, "mlp1")

    core = ci.reshape(1).astype(jnp.int32)
    chip_core = jnp.stack([chip, ci]).astype(jnp.int32)

    def reduce_start(gbufs, tag, after=()):
        sib = _swap_halves_with_sibling(gbufs, name=tag + "_sibling", after=after)
        hs = [_add_sibling_half(g, s, core, name=f"{tag}_add_sibling{b}") for b, (g, s) in enumerate(zip(gbufs, sib))]
        return _owners_start(hs, name=tag + "_owners_start")

    def reduce_finish(flight, tag, after):
        nb = len(flight[2]) // 2
        lands = _owners_wait(flight, name=tag + "_owners_landed", after=after)
        ts = [_add_chip_sums(h, o, chip_core, name=f"{tag}_add_chips{b}") for b, (h, o) in enumerate(zip(flight[2][:nb], lands))]
        return [t.reshape(-1, t.shape[2]) for t in _swap_results_with_sibling(ts, name=tag + "_result")]

    dx4, fsum = _final_loss(x4, row(final_norm_w), tgt, name="final_loss")
    dx3, g_c, sum_mlp1 = _mlp_bwd(dx4, s_mlp1, mods[1][3:6], mlp_nw[1], w_c, None, up_row, down_row, "mlp1")
    dx2, g_c, g_sc_in, sum_sc, sc_csum = _sc_layer_bwd(dx3, s_sc, mods[1][0:3], mix_nw[1], w_sc_in, sc_conv, w_c, g_c,
                                                       sc_out_row, "sc")
    dx1, g_b, sum_mlp0 = _mlp_bwd(dx2, s_mlp0, mods[0][3:6], mlp_nw[0], w_b, None, up_row, down_row, "mlp0")
    dyn, g_ssd_out, gsum_ssd = _ssd_bwd_out(dx1, s_ssd, mods[0][0:3], w_ssd_out, "ssd")
    fly_1, tok = reduce_start([g_c, g_sc_in, g_b, g_ssd_out], "rs1")
    grad_x, d_w_zx, d_w_dt, sum_ssd, csum, gnsum, ssum = _ssd_bwd_rest(
        dx1, dyn, gsum_ssd, s_ssd, mods[0][0:3], mix_nw[0], w_zx, w_dt, ssd_conv, ssd_conv_b, prm,
        ssd_norm_w + tok[0:1, 0:1], "ssd")
    t_c, t_sc_in, t_b, t_ssd_out = reduce_finish(fly_1, "rs1", (grad_x,))

    def ssd_in_owner(k):
        lo, hi = k * SSD_IN_SHARD, (k + 1) * SSD_IN_SHARD
        if hi <= ZX:
            return d_w_zx[:, lo:hi]
        return jnp.concatenate([d_w_zx[:, lo:], d_w_dt[:, :hi - ZX]], axis=1)

    small = jnp.concatenate([sum_ssd, sum_mlp0, sum_sc, sum_mlp1, csum.reshape(24, D), gnsum.reshape(16, D), fsum, sc_csum,
                             jnp.pad(ssum, ((0, 0), (0, D - LANES)))], axis=0)
    small_all = _all_gather_rows(small, name="gather_small").reshape(N_DEV, SMALL_ROWS, D)
    fly_2, tok = reduce_start([jnp.stack([ssd_in_owner(k) for k in range(N_CHIPS)]).astype(BF16)], "rs2", (small_all,))
    small_all = small_all + tok[0:1, 0:1]
    tot = _sum_devices(small_all, name="sum_small")
    loss = tot[FINAL_ROW + 1, 0]
    mod_rows = [r + o for r in SUB_ROW for o in (3, 2, 0)]
    g_ada_b = jnp.stack([tot[r] for r in mod_rows]).reshape(2, 6 * D)
    g_mix_norm = jnp.stack([tot[SUB_ROW[0] + 1], tot[SUB_ROW[2] + 1]])
    g_mlp_norm = jnp.stack([tot[SUB_ROW[1] + 1], tot[SUB_ROW[3] + 1]])
    conv_sums = tot[SSD_CONV_ROW:SSD_CONV_ROW + 24].reshape(8, CONVD)
    g_ssd_conv_w = lax.dynamic_slice(conv_sums, (0, chip * 768), (4, 768))[None]
    g_ssd_conv_b = conv_sums[4:5]
    g_ssd_norm = tot[GNORM_ROW:GNORM_ROW + 2].reshape(1, DI)
    g_final = tot[FINAL_ROW]
    g_sc_conv_w = lax.dynamic_slice(tot[SC_CONV_ROW:SC_CONV_ROW + 3], (0, chip * 256), (3, 256))[None]
    g_a_log, g_d, g_dt_bias = (tot[HEAD_ROW + r:HEAD_ROW + r + 1, 0:NH] for r in range(3))
    c_pad = jnp.concatenate([c_all, jnp.zeros((8, D), F32)], axis=0)
    dmod_all = jnp.stack([small_all[:, r] for r in mod_rows], axis=1).reshape(N_DEV, 2, 6 * D)
    g_ada_w = []
    for i in range(2):
        dm = lax.dynamic_slice(dmod_all[:, i], (0, chip * n_ada), (N_DEV, n_ada))
        g_ada_w.append(_matmul_tn(c_pad, jnp.concatenate([dm, jnp.zeros_like(dm)], axis=0), m=D, n=n_ada, a_silu=True,
                                  name=f"ada_dw{i}"))
    g_ada_w = jnp.stack(g_ada_w)

    big = dict(mlp_up=[(t_b, up_row), (t_c, up_row)], mlp_down=[(t_b, down_row), (t_c, down_row)],
               ssd_out_w=[(t_ssd_out, 0)], sc_out_w=[(t_c, sc_out_row)], sc_in_w=[(t_sc_in, 0)], ssd_in_w=None)
    grads = dict(ada_w=g_ada_w, ada_b=g_ada_b, mix_norm_w=g_mix_norm, mlp_norm_w=g_mlp_norm, ssd_conv_w=g_ssd_conv_w,
                 ssd_conv_b=g_ssd_conv_b, ssd_dt_bias=g_dt_bias, ssd_A_log=g_a_log, ssd_D=g_d, ssd_norm_w=g_ssd_norm,
                 sc_conv_w=g_sc_conv_w, final_norm_w=g_final)
    weights = dict(ada_w=(ada_w, m_ada_w, v_ada_w), ada_b=(ada_b, m_ada_b, v_ada_b),
                   mix_norm_w=(mix_norm_w, m_mix_norm_w, v_mix_norm_w), mlp_norm_w=(mlp_norm_w, m_mlp_norm_w, v_mlp_norm_w),
                   mlp_up=(mlp_up, m_mlp_up, v_mlp_up), mlp_down=(mlp_down, m_mlp_down, v_mlp_down),
                   ssd_in_w=(ssd_in_w, m_ssd_in_w, v_ssd_in_w), ssd_conv_w=(ssd_conv_w, m_ssd_conv_w, v_ssd_conv_w),
                   ssd_conv_b=(ssd_conv_b, m_ssd_conv_b, v_ssd_conv_b), ssd_dt_bias=(ssd_dt_bias, m_ssd_dt_bias, v_ssd_dt_bias),
                   ssd_A_log=(ssd_A_log, m_ssd_A_log, v_ssd_A_log), ssd_D=(ssd_D, m_ssd_D, v_ssd_D),
                   ssd_norm_w=(ssd_norm_w, m_ssd_norm_w, v_ssd_norm_w), ssd_out_w=(ssd_out_w, m_ssd_out_w, v_ssd_out_w),
                   sc_in_w=(sc_in_w, m_sc_in_w, v_sc_in_w), sc_conv_w=(sc_conv_w, m_sc_conv_w, v_sc_conv_w),
                   sc_out_w=(sc_out_w, m_sc_out_w, v_sc_out_w), final_norm_w=(final_norm_w, m_final_norm_w, v_final_norm_w))
    def step(nm, parts):
        w, m, v = (t.reshape(-1, t.shape[-1]) for t in weights[nm])
        rows, outs = w.shape[0] // len(parts), None
        for i, (gbuf, g_row) in enumerate(parts):
            outs = _adamw(w, gbuf, m, v, g_row=g_row, w_row=i * rows, rows=rows, into=outs, emit_g=True, name=f"adamw_{nm}{i}")
        return outs

    res = {}
    for nm, (w, m, v) in weights.items():
        two_d = (-1, w.shape[-1]) if w.ndim > 1 else (1, -1)
        if nm not in big:
            res[nm] = (grads[nm], *_adamw(w.reshape(two_d), grads[nm].reshape(two_d), m.reshape(two_d), v.reshape(two_d),
                                          name="adamw_" + nm))
        elif big[nm] is not None:
            res[nm] = step(nm, big[nm])
    (t_ssd_in,) = reduce_finish(fly_2, "rs2", (res["sc_out_w"][1],))
    res["ssd_in_w"] = step("ssd_in_w", [(t_ssd_in, 0)])
    outs = [[res[nm][k].reshape(weights[nm][0].shape) for nm in weights] for k in range(4)]
    return (loss, grad_x[None], *outs[0], *outs[1], *outs[2], *outs[3])
```

```python
import jax
import jax.numpy as jnp
from jax import lax
from jax.experimental import pallas as pl
from jax.experimental.pallas import tpu as pltpu

F32 = jnp.float32
BF16 = jnp.bfloat16
MESH = pl.DeviceIdType.MESH

D = 1024
DFF = 4096
DI = 2048
NH = 32
HP = 64
NG = 4
NS = 128
CH = 128
CONVD = DI + 2 * NG * NS
ZX = DI + CONVD
GW = NG * NS
LANES = 128
N_CHIPS = 4
N_DEV = 8
EPS = 1e-5
ADAM_LR, ADAM_B1, ADAM_B2, ADAM_EPS, ADAM_WD, ADAM_STEP = 1e-3, 0.9, 0.999, 1e-8, 0.01, 10
VMEM_LIMIT = 48 * 1024 * 1024
TM_ALL = 2048
TM_HALF = 1024
ANY = pl.BlockSpec(memory_space=pl.ANY)
HBM = pl.BlockSpec(memory_space=pltpu.HBM)
SEM = pl.BlockSpec(memory_space=pltpu.SEMAPHORE)

SSD_IN_SHARD = 1288
SC_IN_SHARD = 768


def _params(sem=None):
    return pltpu.CompilerParams(dimension_semantics=sem, vmem_limit_bytes=VMEM_LIMIT)


def _sigmoid(v):
    return 1.0 / (1.0 + jnp.exp(-v))


def _dot(a, b, dims=((1,), (0,)), precision=None):
    return lax.dot_general(a, b, (dims, ((), ())), preferred_element_type=F32, precision=precision)


def _dot_nt(a, b):
    return _dot(a, b, ((1,), (1,)))


def _dot_tn(a, b):
    return _dot(a, b, ((0,), (0,)))


def _nn(av, bv):
    return _dot(av.astype(BF16), bv.astype(BF16))


def _nt(av, bv):
    return _dot_nt(av.astype(BF16), bv.astype(BF16))


def _nn_split(av, bv):
    return _dot(av.astype(BF16), bv.reshape(-1, bv.shape[2]))


def _nn_split_sq(av, bv):
    af = av.astype(F32)
    return _nn_split(af * af, bv)


def _nt_split(av, bv):
    kc = bv.shape[2]
    acc = _dot_nt(av[:, 0:kc].astype(BF16), bv[0])
    for s in range(1, bv.shape[0]):
        acc = acc + _dot_nt(av[:, s * kc:(s + 1) * kc].astype(BF16), bv[s])
    return acc


def _nt_sc_in(av, bv):
    q = 256
    acc = None
    for i in range(3 * D // q):
        a_blk = av[i // 4][:, (i % 4) * q:(i % 4 + 1) * q]
        b_blk = bv[i // 3][:, (i % 3) * q:(i % 3 + 1) * q]
        t = _dot_nt(a_blk, b_blk)
        acc = t if acc is None else acc + t
    return acc


def _matmul(a, b, *, name, n, contract=_nn, a_spec=None, b_spec=None, tm=512, tn=512, extras=(), epi=None,
            out_dtypes=(F32,), a_silu=False):
    M = a.shape[-2]
    tm, tn = min(tm, M), min(tn, n)
    assert M % tm == 0 and n % tn == 0, (name, M, n, tm, tn)
    n_ex = len(extras)
    if a_spec is None:
        a_spec = pl.BlockSpec((tm, a.shape[1]), lambda i, j: (i, 0))
    if b_spec is None:
        b_spec = (pl.BlockSpec((tn, b.shape[1]), lambda i, j: (j, 0)) if contract is _nt
                  else pl.BlockSpec((b.shape[0], tn), lambda i, j: (0, j)))

    def body(*refs):
        av = refs[0][...]
        if a_silu:
            av = av * _sigmoid(av)
        acc = contract(av, refs[1][...])
        res = epi(acc, *[r[...] for r in refs[2:2 + n_ex]]) if epi is not None else (acc,)
        for o_ref, r in zip(refs[2 + n_ex:], res, strict=True):
            o_ref[...] = r.astype(o_ref.dtype)

    in_specs = [a_spec, b_spec]
    for e in extras:
        in_specs.append(pl.BlockSpec((1, tn), lambda i, j: (0, j)) if e.shape[0] == 1 and M != 1
                        else pl.BlockSpec((tm, tn), lambda i, j: (i, j)))
    outs = pl.pallas_call(
        body, grid=(M // tm, n // tn), in_specs=in_specs,
        out_specs=[pl.BlockSpec((tm, tn), lambda i, j: (i, j)) for _ in out_dtypes],
        out_shape=[jax.ShapeDtypeStruct((M, n), dt) for dt in out_dtypes],
        compiler_params=_params(("parallel", "parallel")), name=name)(a, b, *extras)
    return outs if len(out_dtypes) > 1 else outs[0]


def _matmul_tn(a, b, *, name, m, n, tm=512, tn=512, a_spec=None, b_spec=None, out_spec=None, out_struct=None, into=None,
               a_silu=False, a_square=False):
    T = a.shape[-2]
    tm, tn = min(tm, m), min(tn, n)
    assert m % tm == 0 and n % tn == 0, (name, m, n, tm, tn)
    if a_spec is None:
        a_spec = pl.BlockSpec((T, tm), lambda i, j: (0, i))
    if b_spec is None:
        b_spec = pl.BlockSpec((T, tn), lambda i, j: (0, j))
    if out_spec is None:
        out_spec, out_struct = pl.BlockSpec((tm, tn), lambda i, j: (i, j)), jax.ShapeDtypeStruct((m, n), F32)

    def body(a_ref, b_ref, *rest):
        av = a_ref[...]
        if a_silu:
            av = av * _sigmoid(av)
        if a_square:
            av = av.astype(F32) * av.astype(F32)
        rest[-1][...] = _dot_tn(av.astype(BF16), b_ref[...].astype(BF16)).astype(rest[-1].dtype)

    args, in_specs, alias = [a, b], [a_spec, b_spec], {}
    if into is not None:
        args, in_specs, alias = args + [into], in_specs + [ANY], {2: 0}
    return pl.pallas_call(body, grid=(m // tm, n // tn), in_specs=in_specs, out_specs=out_spec, out_shape=out_struct,
                          input_output_aliases=alias, compiler_params=_params(("parallel", "parallel")), name=name)(*args)


def _modnorm_fwd(x, nw, sc, sh, *, name):
    L = x.shape[0]
    tm = min(L, 512)

    def body(x_ref, nw_ref, sc_ref, sh_ref, h_ref):
        xv = x_ref[...]
        r = lax.rsqrt(jnp.mean(xv * xv, axis=-1, keepdims=True) + EPS)
        h_ref[...] = ((xv * r * nw_ref[...]) * (1.0 + sc_ref[...]) + sh_ref[...]).astype(BF16)

    row = pl.BlockSpec((tm, D), lambda i: (i, 0))
    vec = pl.BlockSpec((1, D), lambda i: (0, 0))
    return pl.pallas_call(body, grid=(L // tm,), in_specs=[row, vec, vec, vec], out_specs=row,
                          out_shape=jax.ShapeDtypeStruct((L, D), BF16),
                          compiler_params=_params(("parallel",)), name=name)(x, nw, sc, sh)


def _modnorm_bwd(x, dh, dxo, nw, sc, gsum, *, name):
    L = x.shape[0]
    tm = min(L, 256)

    def body(x_ref, dh_ref, dxo_ref, nw_ref, sc_ref, g_ref, dx_ref, s_ref):
        @pl.when(pl.program_id(0) == 0)
        def _():
            s_ref[...] = g_ref[...]

        xv, dhv = x_ref[...], dh_ref[...]
        r = lax.rsqrt(jnp.mean(xv * xv, axis=-1, keepdims=True) + EPS)
        xhat = xv * r
        dxhat = dhv * (nw_ref[...] * (1.0 + sc_ref[...]))
        dx_ref[...] = dxo_ref[...] + r * (dxhat - xhat * jnp.mean(dxhat * xhat, axis=-1, keepdims=True))
        s_ref[1:2, :] += jnp.sum(dhv * xhat, axis=0, keepdims=True) * (1.0 + sc_ref[...])
        s_ref[2:3, :] += jnp.sum(dhv * xhat, axis=0, keepdims=True) * nw_ref[...]
        s_ref[3:4, :] += jnp.sum(dhv, axis=0, keepdims=True)

    row = pl.BlockSpec((tm, D), lambda i: (i, 0))
    vec = pl.BlockSpec((1, D), lambda i: (0, 0))
    blk = pl.BlockSpec((8, D), lambda i: (0, 0))
    return pl.pallas_call(body, grid=(L // tm,), in_specs=[row, row, row, vec, vec, blk], out_specs=[row, blk],
                          out_shape=[jax.ShapeDtypeStruct((L, D), F32), jax.ShapeDtypeStruct((8, D), F32)],
                          compiler_params=_params(("arbitrary",)), name=name)(x, dh, dxo, nw, sc, gsum)


def _gate_bwd(dxo, y, g, *, name):
    L = dxo.shape[0]
    tm = min(L, 512)

    def body(dxo_ref, y_ref, g_ref, dy_ref, s_ref):
        @pl.when(pl.program_id(0) == 0)
        def _():
            s_ref[...] = jnp.zeros_like(s_ref)

        dv = dxo_ref[...]
        dy_ref[...] = (dv * g_ref[...]).astype(BF16)
        s_ref[0:1, :] += jnp.sum(dv * y_ref[...], axis=0, keepdims=True)

    row = pl.BlockSpec((tm, D), lambda i: (i, 0))
    return pl.pallas_call(body, grid=(L // tm,), in_specs=[row, row, pl.BlockSpec((1, D), lambda i: (0, 0))],
                          out_specs=[row, pl.BlockSpec((8, D), lambda i: (0, 0))],
                          out_shape=[jax.ShapeDtypeStruct((L, D), BF16), jax.ShapeDtypeStruct((8, D), F32)],
                          compiler_params=_params(("arbitrary",)), name=name)(dxo, y, g)


def _final_loss(x, fw, tgt, *, name):
    L = x.shape[0]
    tm = min(L, 256)

    def body(x_ref, fw_ref, t_ref, dx_ref, s_ref):
        @pl.when(pl.program_id(0) == 0)
        def _():
            s_ref[...] = jnp.zeros_like(s_ref)

        xv = x_ref[...]
        r = lax.rsqrt(jnp.mean(xv * xv, axis=-1, keepdims=True) + EPS)
        xhat = xv * r
        diff = xhat * fw_ref[...] - t_ref[...]
        dout = diff * (1.0 / D)
        dxhat = dout * fw_ref[...]
        dx_ref[...] = r * (dxhat - xhat * jnp.mean(dxhat * xhat, axis=-1, keepdims=True))
        s_ref[0:1, :] += jnp.sum(dout * xhat, axis=0, keepdims=True)
        s_ref[1:2, :] += jnp.zeros((1, D), F32) + 0.5 * jnp.sum(jnp.sum(diff * diff, axis=-1, keepdims=True) * (1.0 / D))

    row = pl.BlockSpec((tm, D), lambda i: (i, 0))
    return pl.pallas_call(body, grid=(L // tm,), in_specs=[row, pl.BlockSpec((1, D), lambda i: (0, 0)), row],
                          out_specs=[row, pl.BlockSpec((8, D), lambda i: (0, 0))],
                          out_shape=[jax.ShapeDtypeStruct((L, D), F32), jax.ShapeDtypeStruct((8, D), F32)],
                          compiler_params=_params(("arbitrary",)), name=name)(x, fw, tgt)


def _shift_down(v, j):
    if j == 0:
        return v
    row = lax.broadcasted_iota(jnp.int32, v.shape, 0)
    return jnp.where(row >= j, pltpu.roll(v, j, 0), 0.0)


def _shift_up(v, j):
    if j == 0:
        return v
    n = v.shape[0]
    row = lax.broadcasted_iota(jnp.int32, v.shape, 0)
    return jnp.where(row < n - j, pltpu.roll(v, n - j, 0), 0.0)


def _ssd_conv_fwd(zx, w, b, *, name):
    L = zx.shape[0]
    cb = 256
    k = w.shape[0]

    def body(x_ref, w_ref, b_ref, o_ref):
        xv = x_ref[...]
        pre = b_ref[...] + xv * w_ref[k - 1:k, :]
        for j in range(1, k):
            pre = pre + _shift_down(xv, j) * w_ref[k - 1 - j:k - j, :]
        o_ref[...] = pre * _sigmoid(pre)

    return pl.pallas_call(
        body, grid=(CONVD // cb,),
        in_specs=[pl.BlockSpec((L, cb), lambda i: (0, i + DI // cb)), pl.BlockSpec((k, cb), lambda i: (0, i)),
                  pl.BlockSpec((1, cb), lambda i: (0, i))],
        out_specs=pl.BlockSpec((L, cb), lambda i: (0, i)), out_shape=jax.ShapeDtypeStruct((L, CONVD), F32),
        compiler_params=_params(("parallel",)), name=name)(zx, w, b)


def _ssd_conv_bwd(zx, dact, w, b, dzx, *, name):
    L = zx.shape[0]
    cb = 256
    k = w.shape[0]

    def body(x_ref, da_ref, w_ref, b_ref, _, dx_ref, s_ref):
        xv = x_ref[...]
        sh = [_shift_down(xv, j) for j in range(k)]
        pre = b_ref[...] + sh[0] * w_ref[k - 1:k, :]
        for j in range(1, k):
            pre = pre + sh[j] * w_ref[k - 1 - j:k - j, :]
        s = _sigmoid(pre)
        dpre = da_ref[...] * (s * (1.0 + pre * (1.0 - s)))
        dx = dpre * w_ref[k - 1:k, :]
        for j in range(1, k):
            dx = dx + _shift_up(dpre, j) * w_ref[k - 1 - j:k - j, :]
        dx_ref[...] = dx.astype(BF16)
        s_ref[...] = jnp.zeros_like(s_ref)
        for j in range(k):
            s_ref[k - 1 - j:k - j, :] = jnp.sum(dpre * sh[j], axis=0, keepdims=True)
        s_ref[k:k + 1, :] = jnp.sum(dpre, axis=0, keepdims=True)

    return pl.pallas_call(
        body, grid=(CONVD // cb,),
        in_specs=[pl.BlockSpec((L, cb), lambda i: (0, i + DI // cb)), pl.BlockSpec((L, cb), lambda i: (0, i)),
                  pl.BlockSpec((k, cb), lambda i: (0, i)), pl.BlockSpec((1, cb), lambda i: (0, i)), ANY],
        out_specs=[pl.BlockSpec((L, cb), lambda i: (0, i + DI // cb)), pl.BlockSpec((8, cb), lambda i: (0, i))],
        out_shape=[jax.ShapeDtypeStruct((L, ZX), BF16), jax.ShapeDtypeStruct((8, CONVD), F32)],
        input_output_aliases={4: 0}, compiler_params=_params(("parallel",)), name=name)(zx, dact, w, b, dzx)


def _sc_fwd(proj, w, *, name):
    L = proj.shape[0]
    cb = 256
    nb = D // cb
    k = w.shape[0]

    def body(b_ref, c_ref, x_ref, w_ref, o_ref):
        u = c_ref[...] * x_ref[...]
        v = u * w_ref[k - 1:k, :]
        for j in range(1, k):
            v = v + _shift_down(u, j) * w_ref[k - 1 - j:k - j, :]
        o_ref[...] = (b_ref[...] * v).astype(BF16)

    return pl.pallas_call(
        body, grid=(nb,),
        in_specs=[pl.BlockSpec((L, cb), lambda i: (0, i)), pl.BlockSpec((L, cb), lambda i: (0, i + nb)),
                  pl.BlockSpec((L, cb), lambda i: (0, i + 2 * nb)), pl.BlockSpec((k, cb), lambda i: (0, i))],
        out_specs=pl.BlockSpec((L, cb), lambda i: (0, i)), out_shape=jax.ShapeDtypeStruct((L, D), BF16),
        compiler_params=_params(("parallel",)), name=name)(proj, proj, proj, w)


def _sc_bwd(proj, dyv, w, *, name):
    L = proj.shape[0]
    cb = 256
    nb = D // cb
    k = w.shape[0]

    def body(b_ref, c_ref, x_ref, dy_ref, w_ref, dp_ref, s_ref):
        cv, xv = c_ref[...], x_ref[...]
        u = cv * xv
        sh = [_shift_down(u, j) for j in range(k)]
        v = sh[0] * w_ref[k - 1:k, :]
        for j in range(1, k):
            v = v + sh[j] * w_ref[k - 1 - j:k - j, :]
        dyv_ = dy_ref[...]
        dp_ref[0] = (dyv_ * v).astype(BF16)
        dv = dyv_ * b_ref[...]
        du = dv * w_ref[k - 1:k, :]
        for j in range(1, k):
            du = du + _shift_up(dv, j) * w_ref[k - 1 - j:k - j, :]
        dp_ref[1] = (du * xv).astype(BF16)
        dp_ref[2] = (du * cv).astype(BF16)
        s_ref[...] = jnp.zeros_like(s_ref)
        for j in range(k):
            s_ref[k - 1 - j:k - j, :] = jnp.sum(dv * sh[j], axis=0, keepdims=True)

    blk = pl.BlockSpec((L, cb), lambda i: (0, i))
    return pl.pallas_call(
        body, grid=(nb,),
        in_specs=[blk, pl.BlockSpec((L, cb), lambda i: (0, i + nb)), pl.BlockSpec((L, cb), lambda i: (0, i + 2 * nb)),
                  blk, pl.BlockSpec((k, cb), lambda i: (0, i))],
        out_specs=[pl.BlockSpec((3, L, cb), lambda i: (0, 0, i)), pl.BlockSpec((8, cb), lambda i: (0, i))],
        out_shape=[jax.ShapeDtypeStruct((3, L, D), BF16), jax.ShapeDtypeStruct((8, D), F32)],
        compiler_params=_params(("parallel",)), name=name)(proj, proj, proj, dyv, w)


def _pieces(v, n):
    out, rest = [], v
    for _ in range(n):
        out.append(rest.astype(BF16))
        rest = rest - out[-1].astype(F32)
    return out


def _cumsum_rows(mask, v):
    m = mask.astype(BF16)
    return _dot(jnp.concatenate([m, m, m], axis=1), jnp.concatenate(_pieces(v, 3), axis=0))


def _ssd_chunk_terms(dtr, prm):
    lane = lax.broadcasted_iota(jnp.int32, (CH, LANES), 1)
    valid = lane < NH
    xdt = dtr + prm[0:1, :]
    dt = jnp.where(valid, jnp.maximum(xdt, 0.0) + jnp.log1p(jnp.exp(-jnp.abs(xdt))), 0.0)
    A = -jnp.exp(prm[1:2, :])
    ri = lax.broadcasted_iota(jnp.int32, (CH, CH), 0)
    ci = lax.broadcasted_iota(jnp.int32, (CH, CH), 1)
    cs = _cumsum_rows(ri >= ci, dt * A)
    last = cs[CH - 1:CH, :]
    spread = (lax.broadcasted_iota(jnp.int32, (2 * LANES, DI), 1) // HP
              == lax.broadcasted_iota(jnp.int32, (2 * LANES, DI), 0) % LANES).astype(BF16)
    gather = ((lax.broadcasted_iota(jnp.int32, (LANES, 2 * DI), 1) % DI) // HP
              == lax.broadcasted_iota(jnp.int32, (LANES, 2 * DI), 0)).astype(BF16)
    return dict(valid=valid, xdt=xdt, dt=dt, A=A, cs=cs, csT=cs.T, last=last, ri=ri, ci=ci, ex=(spread, gather))


def _expand(v, ex):
    if v.shape[0] == 1:
        return _expand(jnp.broadcast_to(v, (8, LANES)), ex)[0:1, :]
    return _dot(jnp.concatenate(_pieces(v, 2), axis=1), ex[0])


def _head_sum(v, ex):
    if v.shape[0] == 1:
        return _head_sum(jnp.broadcast_to(v, (8, DI)), ex)[0:1, :]
    return _dot_nt(jnp.concatenate(_pieces(v, 2), axis=1), ex[1])


def _ssd_fwd(xbc, dtr, prm, *, name):
    L = xbc.shape[0]
    nc = L // CH

    def body(xbc_ref, dtr_ref, prm_ref, y_ref, sp_ref, st_ref):
        @pl.when(pl.program_id(0) == 0)
        def _():
            st_ref[...] = jnp.zeros_like(st_ref)

        prm_v = prm_ref[...]
        t = _ssd_chunk_terms(dtr_ref[...], prm_v)
        cs, csT, ex, causal = t["cs"], t["csT"], t["ex"], t["ri"] >= t["ci"]
        xs = xbc_ref[:, 0:DI]
        X = xs * _expand(t["dt"], ex)
        Xb = X.astype(BF16)
        Xd = (X * _expand(jnp.exp(t["last"] - cs), ex)).astype(BF16)
        Ex = _expand(jnp.exp(cs), ex)
        cdx = _expand(jnp.exp(t["last"]), ex)
        dskx = _expand(prm_v[2:3, :], ex)
        lane = lax.broadcasted_iota(jnp.int32, (CH, LANES), 1)
        sp_ref[0] = st_ref[...]
        for g in range(NG):
            Bg = xbc_ref[:, DI + g * NS:DI + (g + 1) * NS].astype(BF16)
            Cg = xbc_ref[:, DI + GW + g * NS:DI + GW + (g + 1) * NS].astype(BF16)
            G = _dot_nt(Cg, Bg)
            Sg = st_ref[:, g * GW:(g + 1) * GW]
            yoff = _dot(Cg, Sg.astype(BF16)) * Ex[:, g * GW:(g + 1) * GW]
            for j in range(GW // LANES):
                lo = g * GW + j * LANES
                Xp = Xb[:, lo:lo + LANES]
                yd = []
                for h in (lo // HP, lo // HP + 1):
                    seg = cs[:, h:h + 1] - csT[h:h + 1, :]
                    yd.append(_dot((G * jnp.where(causal, jnp.exp(seg), 0.0)).astype(BF16), Xp))
                y_ref[:, lo:lo + LANES] = (jnp.where(lane < HP, yd[0], yd[1]) + yoff[:, j * LANES:(j + 1) * LANES]
                                           + dskx[:, lo:lo + LANES] * xs[:, lo:lo + LANES])
            st_ref[:, g * GW:(g + 1) * GW] = Sg * cdx[:, g * GW:(g + 1) * GW] + _dot_tn(Bg, Xd[:, g * GW:(g + 1) * GW])

    return pl.pallas_call(
        body, grid=(nc,),
        in_specs=[pl.BlockSpec((CH, CONVD), lambda c: (c, 0)), pl.BlockSpec((CH, LANES), lambda c: (c, 0)),
                  pl.BlockSpec((8, LANES), lambda c: (0, 0))],
        out_specs=[pl.BlockSpec((CH, DI), lambda c: (c, 0)), pl.BlockSpec((1, NS, DI), lambda c: (c, 0, 0))],
        out_shape=[jax.ShapeDtypeStruct((L, DI), F32), jax.ShapeDtypeStruct((nc, NS, DI), F32)],
        scratch_shapes=[pltpu.VMEM((NS, DI), F32)],
        compiler_params=_params(("arbitrary",)), name=name)(xbc, dtr, prm)


def _ssd_bwd(xbc, dtr, prm, dy, sprev, *, name):
    L = xbc.shape[0]
    nc = L // CH

    def body(xbc_ref, dtr_ref, prm_ref, dy_ref, sp_ref, dxbc_ref, ddtr_ref, s_ref, dst_ref, dx_scr, de_scr, dd_scr):
        step = pl.program_id(0)

        @pl.when(step == 0)
        def _():
            dst_ref[...] = jnp.zeros_like(dst_ref)
            s_ref[...] = jnp.zeros_like(s_ref)

        prm_v = prm_ref[...]
        t = _ssd_chunk_terms(dtr_ref[...], prm_v)
        cs, csT, ex, ri, ci = t["cs"], t["csT"], t["ex"], t["ri"], t["ci"]
        E = jnp.exp(cs)
        dec = jnp.exp(t["last"] - cs)
        cd = jnp.exp(t["last"])
        xs = xbc_ref[:, 0:DI]
        dtx = _expand(t["dt"], ex)
        X = xs * dtx
        Xb = X.astype(BF16)
        decx = _expand(dec, ex)
        Xd = (X * decx).astype(BF16)
        Ex = _expand(E, ex)
        cdx = _expand(cd, ex)
        dskx = _expand(prm_v[2:3, :], ex)
        lane = lax.broadcasted_iota(jnp.int32, (CH, LANES), 1)
        dcs = jnp.zeros((CH, LANES), F32)
        dcd_x = []
        for g in range(NG):
            gs = slice(g * GW, (g + 1) * GW)
            Bg = xbc_ref[:, DI + g * NS:DI + (g + 1) * NS].astype(BF16)
            Cg = xbc_ref[:, DI + GW + g * NS:DI + GW + (g + 1) * NS].astype(BF16)
            G = _dot_nt(Cg, Bg)
            GT = _dot_nt(Bg, Cg)
            Sg = sp_ref[0, :, gs]
            Sgb = Sg.astype(BF16)
            dyg = dy_ref[:, gs]
            de_scr[:, gs] = dyg * _dot(Cg, Sgb)
            dYo = (Ex[:, gs] * dyg).astype(BF16)
            dC = _dot_nt(dYo, Sgb)
            dS_in = _dot_tn(Cg, dYo)
            dStg = dst_ref[:, gs]
            dStb = dStg.astype(BF16)
            dXd = _dot(Bg, dStb)
            dB = _dot_nt(Xd[:, gs], dStb)
            dd_scr[:, gs] = dXd * X[:, gs]
            dXst = dXd * decx[:, gs]
            dG = jnp.zeros((CH, CH), F32)
            dGT = jnp.zeros((CH, CH), F32)
            for j in range(GW // LANES):
                lo = g * GW + j * LANES
                Xp = Xb[:, lo:lo + LANES]
                dyp = dy_ref[:, lo:lo + LANES]
                dXp = dXst[:, j * LANES:(j + 1) * LANES]
                for k, h in enumerate((lo // HP, lo // HP + 1)):
                    dyh = jnp.where((lane < HP) if k == 0 else (lane >= HP), dyp, 0.0).astype(BF16)
                    seg = cs[:, h:h + 1] - csT[h:h + 1, :]
                    Lm = jnp.where(ri >= ci, jnp.exp(seg), 0.0)
                    LmT = jnp.where(ci >= ri, jnp.exp(-seg), 0.0)
                    dM = _dot_nt(dyh, Xp)
                    dMT = _dot_nt(Xp, dyh)
                    MT = GT * LmT
                    rs = jnp.sum(dM * (G * Lm), axis=1, keepdims=True) - jnp.sum(dMT * MT, axis=1, keepdims=True)
                    dcs = dcs + jnp.where(lane == h, rs, 0.0)
                    dG = dG + dM * Lm
                    dGT = dGT + dMT * LmT
                    dXp = dXp + _dot(MT.astype(BF16), dyh)
                dx_scr[:, lo:lo + LANES] = dXp
            dxbc_ref[:, DI + g * NS:DI + (g + 1) * NS] = dB + _dot(dGT.astype(BF16), Cg)
            dxbc_ref[:, DI + GW + g * NS:DI + GW + (g + 1) * NS] = dC + _dot(dG.astype(BF16), Bg)
            dcd_x.append(jnp.sum(dStg * Sg, axis=0, keepdims=True))
            dst_ref[:, gs] = dStg * cdx[:, gs] + dS_in
        dX = dx_scr[...]
        dy = dy_ref[...]
        ddec = _head_sum(dd_scr[...], ex)
        dcd = _head_sum(jnp.concatenate(dcd_x, axis=1), ex)
        dcs = dcs + _head_sum(de_scr[...], ex) * E - ddec * dec
        row = lax.broadcasted_iota(jnp.int32, (CH, LANES), 0)
        dcs = dcs + jnp.where(row == CH - 1, jnp.sum(ddec * dec, axis=0, keepdims=True) + dcd * cd, 0.0)
        da = _cumsum_rows(ci >= ri, dcs)
        ddt = da * t["A"] + _head_sum(dX * xs, ex)
        ddtr = jnp.where(t["valid"], ddt * _sigmoid(t["xdt"]), 0.0)
        ddtr_ref[...] = ddtr
        dxbc_ref[:, 0:DI] = dX * dtx + dskx * dy
        s_ref[0:1, :] += jnp.sum(da * t["dt"], axis=0, keepdims=True)
        s_ref[1:2, :] += _head_sum(jnp.sum(dy * xs, axis=0, keepdims=True), ex)
        s_ref[2:3, :] += jnp.sum(ddtr, axis=0, keepdims=True)

        @pl.when(step == nc - 1)
        def _():
            s_ref[0:1, :] = s_ref[0:1, :] * t["A"]

    rev = lambda c: (nc - 1 - c, 0)
    return pl.pallas_call(
        body, grid=(nc,),
        in_specs=[pl.BlockSpec((CH, CONVD), rev), pl.BlockSpec((CH, LANES), rev), pl.BlockSpec((8, LANES), lambda c: (0, 0)),
                  pl.BlockSpec((CH, DI), rev), pl.BlockSpec((1, NS, DI), lambda c: (nc - 1 - c, 0, 0))],
        out_specs=[pl.BlockSpec((CH, CONVD), rev), pl.BlockSpec((CH, LANES), rev), pl.BlockSpec((8, LANES), lambda c: (0, 0))],
        out_shape=[jax.ShapeDtypeStruct((L, CONVD), F32), jax.ShapeDtypeStruct((L, LANES), F32),
                   jax.ShapeDtypeStruct((8, LANES), F32)],
        scratch_shapes=[pltpu.VMEM((NS, DI), F32), pltpu.VMEM((CH, DI), F32), pltpu.VMEM((CH, DI), F32),
                        pltpu.VMEM((CH, DI), F32)],
        compiler_params=_params(("arbitrary",)), name=name)(xbc, dtr, prm, dy, sprev)


def _gnorm_fwd(y, zx, nw, *, name):
    L = y.shape[0]
    tm = min(L, 256)

    def body(y_ref, z_ref, nw_ref, o_ref):
        z = z_ref[...]
        yg = y_ref[...] * (z * _sigmoid(z))
        for g in range(NG):
            v = yg[:, g * GW:(g + 1) * GW]
            r = lax.rsqrt(jnp.mean(v * v, axis=-1, keepdims=True) + EPS)
            o_ref[:, g * GW:(g + 1) * GW] = (v * r * nw_ref[:, g * GW:(g + 1) * GW]).astype(BF16)

    row = pl.BlockSpec((tm, DI), lambda i: (i, 0))
    return pl.pallas_call(body, grid=(L // tm,), in_specs=[row, row, pl.BlockSpec((1, DI), lambda i: (0, 0))],
                          out_specs=row, out_shape=jax.ShapeDtypeStruct((L, DI), BF16),
                          compiler_params=_params(("parallel",)), name=name)(y, zx, nw)


def _gnorm_bwd(y, zx, nw, dyn, *, name):
    L = y.shape[0]
    tm = min(L, 256)

    def body(y_ref, z_ref, nw_ref, dyn_ref, dy_ref, dz_ref, s_ref):
        @pl.when(pl.program_id(0) == 0)
        def _():
            s_ref[...] = jnp.zeros_like(s_ref)

        z, yv = z_ref[...], y_ref[...]
        sz = _sigmoid(z)
        gate = z * sz
        dgate_dz = sz * (1.0 + z * (1.0 - sz))
        for g in range(NG):
            gs = slice(g * GW, (g + 1) * GW)
            v = yv[:, gs] * gate[:, gs]
            r = lax.rsqrt(jnp.mean(v * v, axis=-1, keepdims=True) + EPS)
            vhat = v * r
            dn = dyn_ref[:, gs]
            s_ref[0:1, gs] += jnp.sum(dn * vhat, axis=0, keepdims=True)
            dvhat = dn * nw_ref[:, gs]
            dv = r * (dvhat - vhat * jnp.mean(dvhat * vhat, axis=-1, keepdims=True))
            dy_ref[:, gs] = dv * gate[:, gs]
            dz_ref[:, gs] = (dv * yv[:, gs] * dgate_dz[:, gs]).astype(BF16)

    row = pl.BlockSpec((tm, DI), lambda i: (i, 0))
    return pl.pallas_call(body, grid=(L // tm,), in_specs=[row, row, pl.BlockSpec((1, DI), lambda i: (0, 0)), row],
                          out_specs=[row, row, pl.BlockSpec((8, DI), lambda i: (0, 0))],
                          out_shape=[jax.ShapeDtypeStruct((L, DI), F32), jax.ShapeDtypeStruct((L, ZX), BF16),
                                     jax.ShapeDtypeStruct((8, DI), F32)],
                          compiler_params=_params(("arbitrary",)), name=name)(y, zx, nw, dyn)


def _adamw(w, g, m, v, *, name, g_row=0, w_row=0, rows=None, into=None, emit_g=False):
    R, C = w.shape
    rows = R if rows is None else rows
    tr = rows
    while tr * C > 256 * 1024 and tr % 16 == 0:
        tr //= 2
    assert g_row % tr == 0 and w_row % tr == 0, (name, g_row, w_row, tr)
    n_out = 4 if emit_g else 3

    def body(w_ref, g_ref, m_ref, v_ref, *rest):
        outs = rest[-n_out:]
        gv = g_ref[...]
        mn = ADAM_B1 * m_ref[...] + (1.0 - ADAM_B1) * gv
        vn = ADAM_B2 * v_ref[...] + (1.0 - ADAM_B2) * (gv * gv)
        m_hat = mn / (1.0 - ADAM_B1 ** ADAM_STEP)
        v_hat = vn / (1.0 - ADAM_B2 ** ADAM_STEP)
        d_ref, mo_ref, vo_ref = outs[-3:]
        d_ref[...] = -ADAM_LR * (m_hat / (jnp.sqrt(v_hat) + ADAM_EPS) + ADAM_WD * w_ref[...])
        mo_ref[...] = mn
        vo_ref[...] = vn
        if emit_g:
            outs[0][...] = gv

    blk = pl.BlockSpec((tr, C), lambda i: (i + w_row // tr, 0))
    args, in_specs, alias = [w, g, m, v], [blk, pl.BlockSpec((tr, C), lambda i: (i + g_row // tr, 0)), blk, blk], {}
    if into is not None:
        args, in_specs, alias = args + list(into), in_specs + [ANY] * n_out, {4 + k: k for k in range(n_out)}
    return pl.pallas_call(body, grid=(rows // tr,), in_specs=in_specs, out_specs=[blk] * n_out,
                          out_shape=[jax.ShapeDtypeStruct((R, C), F32)] * n_out, input_output_aliases=alias,
                          compiler_params=_params(("parallel",)), name=name)(*args)


def _residual(acc, xv, gv):
    return xv + gv * acc, acc


def _like(buf):
    return jax.ShapeDtypeStruct(buf.shape, buf.dtype)


def _mlp_fwd(x, mod, nw, wb, up_row, down_row, tag):
    sh, sc, g = mod
    h = _modnorm_fwd(x, nw, sc, sh, name=tag + "_norm")
    a = _matmul(h, wb, n=DFF, tm=TM_ALL, b_spec=pl.BlockSpec((None, D, 512), lambda mi, j: (j // 2, up_row // D, j % 2)),
                epi=lambda acc: (jnp.maximum(acc, 0.0),), out_dtypes=(BF16,), name=tag + "_up")
    xn, y = _matmul(a, wb, n=D, tm=TM_HALF, contract=_nn_split_sq,
                    b_spec=pl.BlockSpec((N_CHIPS, D, 512), lambda mi, j: (0, down_row // D, j)),
                    extras=(x, g), epi=_residual, out_dtypes=(F32, F32), name=tag + "_down")
    return xn, (x, h, a, y)


def _mlp_bwd(dxo, saved, mod, nw, wb, gb, up_row, down_row, tag):
    x, h, a, y = saved
    sh, sc, g = mod
    dy, gsum = _gate_bwd(dxo, y, g, name=tag + "_dgate")
    du = _matmul(dy, wb, n=DFF, tm=TM_ALL, contract=_nt,
                 b_spec=pl.BlockSpec((None, 512, D), lambda mi, j: (j // 2, down_row // 512 + j % 2, 0)),
                 extras=(a,), epi=lambda acc, av: (acc * (2.0 * av.astype(F32)),), out_dtypes=(BF16,), name=tag + "_dact")
    gb = _matmul_tn(a, dy, m=DFF, n=D, tm=D, tn=D, a_square=True, into=gb, out_struct=_like(wb),
                    out_spec=pl.BlockSpec((None, D, D), lambda mi, j: (mi, down_row // D, 0)), name=tag + "_ddown")
    dh = _matmul(du, wb, n=D, tm=TM_HALF, contract=_nt_split,
                 b_spec=pl.BlockSpec((N_CHIPS, 512, D), lambda mi, j: (0, up_row // 512 + j, 0)), name=tag + "_dh")
    gb = _matmul_tn(h, du, m=D, n=DFF, tm=D, into=gb, out_struct=_like(wb),
                    out_spec=pl.BlockSpec((None, D, 512), lambda mi, j: (j // 2, up_row // D, j % 2)), name=tag + "_dup")
    dx, sums = _modnorm_bwd(x, dh, dxo, nw, sc, gsum, name=tag + "_dnorm")
    return dx, gb, sums


def _ssd_fwd_scan(x, mod, nw, w_zx, w_dt, conv_w, conv_b, prm, tag):
    sh, sc, g = mod
    h = _modnorm_fwd(x, nw, sc, sh, name=tag + "_norm")
    zx = _matmul(h, w_zx, n=ZX, tm=TM_ALL, name=tag + "_in")
    dtr = _matmul(h, w_dt, n=LANES, tm=TM_ALL, name=tag + "_in_dt")
    xbc = _ssd_conv_fwd(zx, conv_w, conv_b, name=tag + "_conv")
    y, sprev = _ssd_fwd(xbc, dtr, prm, name=tag + "_scan")
    return h, zx, dtr, xbc, y, sprev


def _ssd_fwd_out(x, mod, scan, gn_w, w_out, tag):
    sh, sc, g = mod
    h, zx, dtr, xbc, y, sprev = scan
    yn = _gnorm_fwd(y, zx, gn_w, name=tag + "_gnorm")
    xn, yo = _matmul(yn, w_out, n=D, tm=TM_HALF, contract=_nn_split,
                     b_spec=pl.BlockSpec((N_CHIPS, 512, 512), lambda mi, j: (0, 0, j)),
                     extras=(x, g), epi=_residual, out_dtypes=(F32, F32), name=tag + "_out")
    return xn, (x, h, zx, dtr, xbc, y, sprev, yn, yo)


def _ssd_bwd_out(dxo, saved, mod, w_out, tag):
    x, h, zx, dtr, xbc, y, sprev, yn, yo = saved
    sh, sc, g = mod
    dyo, gsum = _gate_bwd(dxo, yo, g, name=tag + "_dgate")
    dyn = _matmul(dyo, w_out, n=DI, tm=TM_ALL, contract=_nt, b_spec=pl.BlockSpec((None, 512, D), lambda mi, j: (j, 0, 0)),
                  name=tag + "_dyn")
    g_out = _matmul_tn(yn, dyo, m=DI, n=D, tn=D, out_struct=_like(w_out),
                       out_spec=pl.BlockSpec((None, 512, D), lambda mi, j: (mi, 0, 0)), name=tag + "_dout")
    return dyn, g_out, gsum


def _ssd_bwd_rest(dxo, dy, dzx, gsum, saved, mod, nw, w_zx, w_dt, conv_w, conv_b, prm, tag):
    x, h, zx, dtr, xbc, y, sprev, yn, yo = saved
    sh, sc, g = mod
    dxbc, ddtr, ssum = _ssd_bwd(xbc, dtr, prm, dy, sprev, name=tag + "_dscan")
    dzx, csum = _ssd_conv_bwd(zx, dxbc, conv_w, conv_b, dzx, name=tag + "_dconv")
    dh_dt = _matmul(ddtr, w_dt, n=D, tm=TM_ALL, contract=_nt, name=tag + "_dh_dt")
    dh = _matmul(dzx, w_zx, n=D, tm=TM_HALF, contract=_nt, extras=(dh_dt,), epi=lambda acc, e: (acc + e,), name=tag + "_dh")
    d_w_zx = _matmul_tn(h, dzx, m=D, n=ZX, tm=D, name=tag + "_din")
    d_w_dt = _matmul_tn(h, ddtr, m=D, n=LANES, tm=D, name=tag + "_din_dt")
    dx, sums = _modnorm_bwd(x, dh, dxo, nw, sc, gsum, name=tag + "_dnorm")
    return dx, d_w_zx, d_w_dt, sums, csum, ssum


def _sc_layer_fwd(x, mod, nw, w_sc_in, conv_w, wb, out_row, tag):
    sh, sc, g = mod
    h = _modnorm_fwd(x, nw, sc, sh, name=tag + "_norm")
    proj = _matmul(h, w_sc_in, n=3 * D, tm=TM_ALL, tn=256, b_spec=pl.BlockSpec((None, D, 256), lambda mi, j: (j // 3, 0, j % 3)),
                   name=tag + "_in")
    yv = _sc_fwd(proj, conv_w, name=tag + "_conv")
    xn, yo = _matmul(yv, wb, n=D, tm=TM_HALF, contract=_nn_split,
                     b_spec=pl.BlockSpec((N_CHIPS, 256, 512), lambda mi, j: (0, out_row // 256, j)),
                     extras=(x, g), epi=_residual, out_dtypes=(F32, F32), name=tag + "_out")
    return xn, (x, h, proj, yv, yo)


def _sc_layer_bwd(dxo, saved, mod, nw, w_sc_in, conv_w, wb, gb, out_row, tag):
    x, h, proj, yv, yo = saved
    sh, sc, g = mod
    L = x.shape[0]
    dyo, gsum = _gate_bwd(dxo, yo, g, name=tag + "_dgate")
    dyv = _matmul(dyo, wb, n=D, tm=TM_ALL, tn=256, contract=_nt,
                  b_spec=pl.BlockSpec((None, 256, D), lambda mi, j: (j, out_row // 256, 0)), name=tag + "_dyv")
    gb = _matmul_tn(yv, dyo, m=D, n=D, tm=256, tn=D, into=gb, out_struct=_like(wb),
                    out_spec=pl.BlockSpec((None, 256, D), lambda mi, j: (mi, out_row // 256, 0)), name=tag + "_dout")
    dproj, csum = _sc_bwd(proj, dyv, conv_w, name=tag + "_dconv")
    tm = min(L, TM_HALF)
    dh = _matmul(dproj, w_sc_in, n=D, tm=tm, contract=_nt_sc_in, a_spec=pl.BlockSpec((3, tm, D), lambda mi, j: (0, mi, 0)),
                 b_spec=pl.BlockSpec((N_CHIPS, 512, SC_IN_SHARD), lambda mi, j: (0, j, 0)), name=tag + "_dh")
    g_sc_in = _matmul_tn(h, dproj, m=D, n=3 * D, tm=D, tn=256, b_spec=pl.BlockSpec((None, L, 256), lambda mi, j: (j // 4, 0, j % 4)),
                         out_spec=pl.BlockSpec((None, D, 256), lambda mi, j: (j // 3, 0, j % 3)),
                         out_struct=jax.ShapeDtypeStruct((N_CHIPS, D, SC_IN_SHARD), BF16), name=tag + "_din")
    dx, sums = _modnorm_bwd(x, dh, dxo, nw, sc, gsum, name=tag + "_dnorm")
    return dx, gb, g_sc_in, sums, csum


SUB_ROW = (0, 8, 16, 24)
SSD_CONV_ROW, GNORM_ROW, FINAL_ROW, SC_CONV_ROW, HEAD_ROW, SMALL_ROWS = 32, 56, 72, 80, 88, 96


def _all_gather_rows(blk, *, name):
    m_per, n = blk.shape

    def body(x_ref, out_ref, send_sems, recv_sems, local_sem):
        x, y, c = lax.axis_index("x"), lax.axis_index("y"), lax.axis_index("c")
        me, sibling = (x, y, c), (x, y, 1 - c)
        chips = [(1 - x, y), (x, 1 - y), (1 - x, 1 - y)]

        def rows(px, py, pc):
            return out_ref.at[pl.ds((4 * px + 2 * py + pc) * m_per, m_per), :]

        def copy(k, block, to, src=None):
            return pltpu.make_async_remote_copy(src_ref=rows(*block) if src is None else src, dst_ref=rows(*block),
                                                send_sem=send_sems.at[k], recv_sem=recv_sems.at[k], device_id=to,
                                                device_id_type=MESH)

        mine = pltpu.make_async_copy(x_ref, rows(*me), local_sem)
        mine.start()
        first = [copy(0, me, sibling, src=x_ref)] + [copy(1 + j, me, (*chip, c), src=x_ref) for j, chip in enumerate(chips)]
        for cp in first:
            cp.start()
        passed = [copy(4 + j, (*chip, c), sibling) for j, chip in enumerate(chips)]
        for j, chip in enumerate(chips):
            copy(1 + j, (*chip, c), me).wait_recv()
            passed[j].start()
        copy(0, sibling, me).wait_recv()
        for j, chip in enumerate(chips):
            copy(4 + j, (*chip, 1 - c), me).wait_recv()
        for cp in first + passed:
            cp.wait_send()
        mine.wait()

    return pl.pallas_call(
        body, out_shape=jax.ShapeDtypeStruct((N_DEV * m_per, n), blk.dtype),
        in_specs=[pl.BlockSpec(memory_space=pltpu.VMEM)], out_specs=pl.BlockSpec(memory_space=pltpu.VMEM),
        scratch_shapes=[pltpu.SemaphoreType.DMA((7,)), pltpu.SemaphoreType.DMA((7,)), pltpu.SemaphoreType.DMA],
        name=name)(blk)


def _half(ref, chip, c):
    hr = ref.shape[1] // 2
    return ref.at[chip, pl.ds(c * hr, hr), :]


def _gather_copy(bufs, sends, recvs, b, k, chip, pc, to):
    piece = _half(bufs[b], 2 * chip[0] + chip[1], pc)
    return pltpu.make_async_remote_copy(src_ref=piece, dst_ref=piece, send_sem=sends.at[4 * b + k], recv_sem=recvs.at[4 * b + k],
                                        device_id=to, device_id_type=MESH)


def _split_call(body, bufs, sems_in, n_sems, *, name, after=(), token=False):
    nb, na, starts = len(bufs), len(after), not sems_in

    def wrapped(*refs):
        sems = refs[nb + na:nb + na + 2] if starts else refs[nb:nb + 2]
        body(refs[:nb], sems[0], sems[1])
        if token:
            refs[-1][...] = jnp.zeros_like(refs[-1])

    out_shape = [pltpu.SemaphoreType.DMA((n_sems,)) for _ in range(2 if starts else 0)]
    out_specs = [SEM] * len(out_shape) + [HBM] * nb
    alias = {b: len(out_shape) + b for b in range(nb)}
    out_shape += [pltpu.HBM(b.shape, b.dtype) for b in bufs]
    if token:
        out_shape.append(jax.ShapeDtypeStruct((8, LANES), F32))
        out_specs.append(pl.BlockSpec(memory_space=pltpu.VMEM))
    return pl.pallas_call(
        wrapped, out_shape=out_shape, in_specs=[HBM] * nb + [SEM] * len(sems_in) + [ANY] * na, out_specs=out_specs,
        input_output_aliases=alias,
        compiler_params=pltpu.CompilerParams(has_side_effects=pltpu.SideEffectType.DATAFLOW_SIDE_EFFECTING),
        name=name)(*[pltpu.with_memory_space_constraint(b, pltpu.HBM) for b in bufs], *sems_in, *after)


def _gather_start(bufs, *, name, after=()):
    nb = len(bufs)

    def body(ins, sends, recvs):
        x, y, c = lax.axis_index("x"), lax.axis_index("y"), lax.axis_index("c")
        chips = [(1 - x, y), (x, 1 - y), (1 - x, 1 - y)]
        for b in range(nb):
            _gather_copy(ins, sends, recvs, b, 0, (x, y), c, (x, y, 1 - c)).start()
            for j, chip in enumerate(chips):
                _gather_copy(ins, sends, recvs, b, 1 + j, (x, y), c, (*chip, c)).start()

    out = _split_call(body, bufs, (), 4 * nb, name=name, after=after, token=True)
    return (out[0], out[1], out[2:2 + nb]), out[-1]


def _gather_wait_first(flight, *, name, after=()):
    sends, recvs, bufs = flight
    nb = len(bufs)

    def body(ins, sends_, recvs_):
        x, y, c = lax.axis_index("x"), lax.axis_index("y"), lax.axis_index("c")
        chips = [(1 - x, y), (x, 1 - y), (1 - x, 1 - y)]
        for b in range(nb):
            _gather_copy(ins, sends_, recvs_, b, 0, (x, y), c, (x, y, 1 - c)).wait_send()
            _gather_copy(ins, sends_, recvs_, b, 0, (x, y), 1 - c, (x, y, c)).wait_recv()
            for j, chip in enumerate(chips):
                _gather_copy(ins, sends_, recvs_, b, 1 + j, (x, y), c, (*chip, c)).wait_send()
                _gather_copy(ins, sends_, recvs_, b, 1 + j, chip, c, (x, y, c)).wait_recv()

    return _split_call(body, bufs, (sends, recvs), 4 * nb, name=name, after=after)


def _gather_forward(bufs, *, name):
    nb = len(bufs)

    def body(ins, sends, recvs):
        x, y, c = lax.axis_index("x"), lax.axis_index("y"), lax.axis_index("c")
        chips = [(1 - x, y), (x, 1 - y), (1 - x, 1 - y)]
        for b in range(nb):
            for j, chip in enumerate(chips):
                _gather_copy(ins, sends, recvs, b, 1 + j, chip, c, (x, y, 1 - c)).start()

    out = _split_call(body, bufs, (), 4 * nb, name=name)
    return out[0], out[1], out[2:2 + nb]


def _gather_wait_forward(flight, *, name, after=()):
    sends, recvs, bufs = flight
    nb = len(bufs)

    def body(ins, sends_, recvs_):
        x, y, c = lax.axis_index("x"), lax.axis_index("y"), lax.axis_index("c")
        chips = [(1 - x, y), (x, 1 - y), (1 - x, 1 - y)]
        for b in range(nb):
            for j, chip in enumerate(chips):
                _gather_copy(ins, sends_, recvs_, b, 1 + j, chip, c, (x, y, 1 - c)).wait_send()
                _gather_copy(ins, sends_, recvs_, b, 1 + j, chip, 1 - c, (x, y, c)).wait_recv()

    return _split_call(body, bufs, (sends, recvs), 4 * nb, name=name, after=after)


def _owner_copies(hs, lands, sends, recvs):
    x, y, c = lax.axis_index("x"), lax.axis_index("y"), lax.axis_index("c")
    chips = [(1 - x, y), (x, 1 - y), (1 - x, 1 - y)]
    return [pltpu.make_async_remote_copy(src_ref=hs[b].at[2 * cx + cy], dst_ref=lands[b].at[j], send_sem=sends.at[3 * b + j],
                                         recv_sem=recvs.at[3 * b + j], device_id=(cx, cy, c), device_id_type=MESH)
            for b in range(len(hs)) for j, (cx, cy) in enumerate(chips)]


def _owners_start(hs, *, name):
    nb = len(hs)
    lands = [lax.empty((3,) + h.shape[1:], h.dtype) for h in hs]

    def body(refs, sends, recvs):
        for cp in _owner_copies(refs[:nb], refs[nb:], sends, recvs):
            cp.start()

    out = _split_call(body, list(hs) + lands, (), 3 * nb, name=name, token=True)
    return (out[0], out[1], out[2:2 + 2 * nb]), out[-1]


def _owners_wait(flight, *, name, after=()):
    sends, recvs, bufs = flight
    nb = len(bufs) // 2

    def body(refs, sends_, recvs_):
        for cp in _owner_copies(refs[:nb], refs[nb:], sends_, recvs_):
            cp.wait()

    return _split_call(body, bufs, (sends, recvs), 3 * nb, name=name, after=after)[nb:]


def _sibling_copies(gs, lands, sends, recvs):
    x, y, c = lax.axis_index("x"), lax.axis_index("y"), lax.axis_index("c")
    copies = []
    for b in range(len(gs)):
        hr = gs[b].shape[1] // 2
        copies.append(pltpu.make_async_remote_copy(
            src_ref=gs[b].at[:, pl.ds((1 - c) * hr, hr), :], dst_ref=lands[b], send_sem=sends.at[b], recv_sem=recvs.at[b],
            device_id=(x, y, 1 - c), device_id_type=MESH))
    return copies


def _sibling_start(gs, *, name, after=()):
    nb = len(gs)
    lands = [lax.empty((g.shape[0], g.shape[1] // 2, g.shape[2]), g.dtype) for g in gs]

    def body(refs, sends, recvs):
        for cp in _sibling_copies(refs[:nb], refs[nb:], sends, recvs):
            cp.start()

    out = _split_call(body, list(gs) + lands, (), nb, name=name, after=after, token=True)
    return (out[0], out[1], out[2:2 + 2 * nb]), out[-1]


def _sibling_wait(flight, *, name, after=()):
    sends, recvs, bufs = flight
    nb = len(bufs) // 2

    def body(refs, sends_, recvs_):
        for cp in _sibling_copies(refs[:nb], refs[nb:], sends_, recvs_):
            cp.wait()

    out = _split_call(body, bufs, (sends, recvs), nb, name=name, after=after)
    return out[:nb], out[nb:]


def _result_copies(ts, sends, recvs):
    x, y, c = lax.axis_index("x"), lax.axis_index("y"), lax.axis_index("c")
    return [pltpu.make_async_remote_copy(src_ref=ts[b].at[c], dst_ref=ts[b].at[c], send_sem=sends.at[b], recv_sem=recvs.at[b],
                                         device_id=(x, y, 1 - c), device_id_type=MESH) for b in range(len(ts))]


def _result_start(ts, *, name):
    def body(refs, sends, recvs):
        for cp in _result_copies(refs, sends, recvs):
            cp.start()

    out = _split_call(body, ts, (), len(ts), name=name, token=True)
    return (out[0], out[1], out[2:2 + len(ts)]), out[-1]


def _result_wait(flight, *, name, after=()):
    sends, recvs, bufs = flight

    def body(refs, sends_, recvs_):
        for cp in _result_copies(refs, sends_, recvs_):
            cp.wait()

    return _split_call(body, bufs, (sends, recvs), len(bufs), name=name, after=after)


def _row_tile(rows, cols):
    best = 16
    for t in range(16, rows + 1, 16):
        if rows % t == 0 and t * cols <= 640 * 1024:
            best = t
    assert rows % best == 0, (rows, cols)
    return best


def _add_sibling_half(g, recv, core, *, name):
    nk, r, n = g.shape
    hr = r // 2
    tr = _row_tile(hr, n)

    def body(c_ref, a_ref, b_ref, o_ref):
        o_ref[...] = (a_ref[...].astype(F32) + b_ref[...].astype(F32)).astype(BF16)

    grid_spec = pltpu.PrefetchScalarGridSpec(
        num_scalar_prefetch=1, grid=(nk, hr // tr),
        in_specs=[pl.BlockSpec((None, tr, n), lambda k, i, c_ref: (k, c_ref[0] * (hr // tr) + i, 0)),
                  pl.BlockSpec((None, tr, n), lambda k, i, c_ref: (k, i, 0))],
        out_specs=pl.BlockSpec((None, tr, n), lambda k, i, c_ref: (k, i, 0)))
    return pl.pallas_call(body, grid_spec=grid_spec, out_shape=jax.ShapeDtypeStruct((nk, hr, n), BF16),
                          compiler_params=_params(("parallel", "parallel")), name=name)(core, g, recv)


def _add_chip_sums(h, recv, chip_core, *, name):
    _, hr, n = h.shape
    tr = _row_tile(hr, n)

    def body(k_ref, a_ref, b_ref, o_ref):
        o_ref[...] = ((a_ref[...].astype(F32) + b_ref[0].astype(F32)) + b_ref[1].astype(F32)) + b_ref[2].astype(F32)

    grid_spec = pltpu.PrefetchScalarGridSpec(
        num_scalar_prefetch=1, grid=(hr // tr,),
        in_specs=[pl.BlockSpec((None, tr, n), lambda i, k_ref: (k_ref[0], i, 0)),
                  pl.BlockSpec((3, tr, n), lambda i, k_ref: (0, i, 0))],
        out_specs=pl.BlockSpec((None, tr, n), lambda i, k_ref: (k_ref[1], i, 0)))
    return pl.pallas_call(body, grid_spec=grid_spec, out_shape=jax.ShapeDtypeStruct((2, hr, n), F32),
                          compiler_params=_params(("parallel",)), name=name)(chip_core, h, recv)


def _sum_devices(g, *, name):
    nd, r, n = g.shape

    def body(g_ref, o_ref):
        acc = g_ref[0]
        for i in range(1, nd):
            acc = acc + g_ref[i]
        o_ref[...] = acc

    return pl.pallas_call(body, out_shape=jax.ShapeDtypeStruct((r, n), F32), name=name)(g)


def _own_slot(shard, chip):
    return lax.dynamic_update_slice(jnp.zeros((N_CHIPS,) + shard.shape, BF16), shard[None], (chip, 0, 0))


def kernel(x, c, ada_w, ada_b, mix_norm_w, mlp_norm_w, mlp_up, mlp_down, ssd_in_w, ssd_conv_w, ssd_conv_b, ssd_dt_bias, ssd_A_log, ssd_D, ssd_norm_w, ssd_out_w, sc_in_w, sc_conv_w, sc_out_w, final_norm_w, loss_target, m_ada_w, m_ada_b, m_mix_norm_w, m_mlp_norm_w, m_mlp_up, m_mlp_down, m_ssd_in_w, m_ssd_conv_w, m_ssd_conv_b, m_ssd_dt_bias, m_ssd_A_log, m_ssd_D, m_ssd_norm_w, m_ssd_out_w, m_sc_in_w, m_sc_conv_w, m_sc_out_w, m_final_norm_w, v_ada_w, v_ada_b, v_mix_norm_w, v_mlp_norm_w, v_mlp_up, v_mlp_down, v_ssd_in_w, v_ssd_conv_w, v_ssd_conv_b, v_ssd_dt_bias, v_ssd_A_log, v_ssd_D, v_ssd_norm_w, v_ssd_out_w, v_sc_in_w, v_sc_conv_w, v_sc_out_w, v_final_norm_w):
    xi, yi, ci = lax.axis_index("x"), lax.axis_index("y"), lax.axis_index("c")
    chip = 2 * xi + yi
    dev = 2 * chip + ci
    n_ada = ada_w.shape[2]

    conv_flat = jnp.concatenate([ssd_conv_w.reshape(-1), sc_conv_w.reshape(-1), jnp.zeros((256,), F32)]).reshape(4, D)
    blk0 = jnp.concatenate([c, conv_flat, jnp.zeros((3, D), F32)], axis=0)
    got0 = _all_gather_rows(blk0, name="gather_cond").reshape(N_DEV, 8, D)
    c_all = got0[:, 0]
    conv_all = got0[0::2, 1:5].reshape(N_CHIPS, 4 * D)
    ssd_conv = jnp.moveaxis(conv_all[:, :4 * 768].reshape(N_CHIPS, 4, 768), 0, 1).reshape(4, CONVD)
    sc_conv = jnp.moveaxis(conv_all[:, 4 * 768:4 * 768 + 3 * 256].reshape(N_CHIPS, 3, 256), 0, 1).reshape(3, D)
    mod_shard = [_matmul(c_all, ada_w[i], n=n_ada, a_silu=True,
                         extras=(lax.dynamic_slice(ada_b, (i, chip * n_ada), (1, n_ada)),),
                         epi=lambda acc, b: (acc + b,), name=f"ada_mod{i}") for i in range(2)]
    mod_all = _all_gather_rows(jnp.concatenate(mod_shard, axis=0), name="gather_mod")
    mod_all = mod_all.reshape(N_DEV, 2, N_DEV, n_ada)[0::2]
    mod = jnp.moveaxis(lax.dynamic_index_in_dim(mod_all, dev, axis=2, keepdims=False), 0, 1).reshape(2, 6, D)
    mods = [[mod[i, j:j + 1] for j in range(6)] for i in range(2)]

    bf = lambda v: v.astype(BF16)
    up_row, down_row, sc_out_row = 0, D, 2 * D
    a_bufs = [_own_slot(bf(ssd_in_w[0]), chip)]
    b_bufs = [_own_slot(bf(ssd_out_w[0]), chip), _own_slot(bf(jnp.concatenate([mlp_up[0], mlp_down[0]], axis=0)), chip)]
    c_bufs = [_own_slot(bf(sc_in_w[0]), chip), _own_slot(bf(jnp.concatenate([mlp_up[1], mlp_down[1], sc_out_w[0]], axis=0)), chip)]
    fly_a, tok = _gather_start(a_bufs, name="gather_a_start", after=(mod,))
    fly_b, tok = _gather_start(b_bufs, name="gather_b_start", after=(tok,))
    fly_c, tok = _gather_start(c_bufs, name="gather_c_start", after=(tok,))

    row = lambda v: v.reshape(1, -1)
    xs, tgt = x[0], loss_target[0]
    prm = jnp.pad(jnp.concatenate([ssd_dt_bias, ssd_A_log, ssd_D, jnp.zeros((5, NH), F32)], axis=0), ((0, 0), (0, LANES - NH)))
    mix_nw = [row(mix_norm_w[i]) for i in range(2)]
    mlp_nw = [row(mlp_norm_w[i]) for i in range(2)]
    a_bufs = _gather_wait_first(fly_a, name="gather_a_landed", after=(tok,))
    (w_ssd_in,) = _gather_wait_forward(_gather_forward(a_bufs, name="gather_a_pass"), name="gather_a_done")
    ssd_in_full = jnp.moveaxis(w_ssd_in, 0, 1).reshape(D, N_CHIPS * SSD_IN_SHARD)
    w_zx, w_dt = ssd_in_full[:, :ZX], jnp.pad(ssd_in_full[:, ZX:], ((0, 0), (0, LANES - NH)))
    scan = _ssd_fwd_scan(xs, mods[0][0:3], mix_nw[0], w_zx, w_dt, ssd_conv, ssd_conv_b, prm, "ssd")
    fly_b = _gather_forward(_gather_wait_first(fly_b, name="gather_b_landed", after=(scan[3],)), name="gather_b_pass")
    w_ssd_out, w_b = _gather_wait_forward(fly_b, name="gather_b_done", after=(scan[4],))
    x1, s_ssd = _ssd_fwd_out(xs, mods[0][0:3], scan, ssd_norm_w, w_ssd_out, "ssd")
    x2, s_mlp0 = _mlp_fwd(x1, mods[0][3:6], mlp_nw[0], w_b, up_row, down_row, "mlp0")
    c_bufs = _gather_wait_first(fly_c, name="gather_c_landed", after=(x2,))
    w_sc_in, w_c = _gather_wait_forward(_gather_forward(c_bufs, name="gather_c_pass"), name="gather_c_done")
    x3, s_sc = _sc_layer_fwd(x2, mods[1][0:3], mix_nw[1], w_sc_in, sc_conv, w_c, sc_out_row, "sc")
    x4, s_mlp1 = _mlp_fwd(x3, mods[1][3:6], mlp_nw[1], w_c, up_row, down_row, "mlp1")

    core = ci.reshape(1).astype(jnp.int32)
    chip_core = jnp.stack([chip, ci]).astype(jnp.int32)

    def reduce_swap(gbufs, tag, after=()):
        return _sibling_start(gbufs, name=tag + "_sibling_start", after=after)

    def reduce_send(flight, tag, after):
        gs, sib = _sibling_wait(flight, name=tag + "_sibling_landed", after=after)
        hs = [_add_sibling_half(g, s, core, name=f"{tag}_add_sibling{b}") for b, (g, s) in enumerate(zip(gs, sib))]
        return _owners_start(hs, name=tag + "_owners_start")

    def reduce_sum(flight, tag, after):
        nb = len(flight[2]) // 2
        lands = _owners_wait(flight, name=tag + "_owners_landed", after=after)
        ts = [_add_chip_sums(h, o, chip_core, name=f"{tag}_add_chips{b}") for b, (h, o) in enumerate(zip(flight[2][:nb], lands))]
        return _result_start(ts, name=tag + "_result_start")

    def reduce_done(flight, tag, after=()):
        return [t.reshape(-1, t.shape[2]) for t in _result_wait(flight, name=tag + "_result_landed", after=after)]

    dx4, fsum = _final_loss(x4, row(final_norm_w), tgt, name="final_loss")
    dx3, g_c, sum_mlp1 = _mlp_bwd(dx4, s_mlp1, mods[1][3:6], mlp_nw[1], w_c, None, up_row, down_row, "mlp1")
    dx2, g_c, g_sc_in, sum_sc, sc_csum = _sc_layer_bwd(dx3, s_sc, mods[1][0:3], mix_nw[1], w_sc_in, sc_conv, w_c, g_c,
                                                       sc_out_row, "sc")
    dx1, g_b, sum_mlp0 = _mlp_bwd(dx2, s_mlp0, mods[0][3:6], mlp_nw[0], w_b, None, up_row, down_row, "mlp0")
    dyn, g_ssd_out, gsum_ssd = _ssd_bwd_out(dx1, s_ssd, mods[0][0:3], w_ssd_out, "ssd")
    fly_1, tok = reduce_swap([g_c, g_sc_in, g_b, g_ssd_out], "rs1")
    dy, dzx, gnsum = _gnorm_bwd(s_ssd[5], s_ssd[2], ssd_norm_w + tok[0:1, 0:1], dyn, name="ssd_dgnorm")
    fly_1, tok = reduce_send(fly_1, "rs1", (dy,))
    grad_x, d_w_zx, d_w_dt, sum_ssd, csum, ssum = _ssd_bwd_rest(
        dx1, dy, dzx, gsum_ssd, s_ssd, mods[0][0:3], mix_nw[0], w_zx, w_dt, ssd_conv, ssd_conv_b, prm + tok[0:1, 0:1], "ssd")
    fly_1, tok = reduce_sum(fly_1, "rs1", (grad_x,))

    def ssd_in_owner(k):
        lo, hi = k * SSD_IN_SHARD, (k + 1) * SSD_IN_SHARD
        if hi <= ZX:
            return d_w_zx[:, lo:hi]
        return jnp.concatenate([d_w_zx[:, lo:], d_w_dt[:, :hi - ZX]], axis=1)

    small = jnp.concatenate([sum_ssd + tok[0:1, 0:1], sum_mlp0, sum_sc, sum_mlp1, csum.reshape(24, D), gnsum.reshape(16, D),
                             fsum, sc_csum, jnp.pad(ssum, ((0, 0), (0, D - LANES)))], axis=0)
    small_all = _all_gather_rows(small, name="gather_small").reshape(N_DEV, SMALL_ROWS, D)
    fly_2, tok = reduce_swap([jnp.stack([ssd_in_owner(k) for k in range(N_CHIPS)]).astype(BF16)], "rs2", (small_all,))
    fly_2, tok = reduce_send(fly_2, "rs2", (tok,))
    t_c, t_sc_in, t_b, t_ssd_out = reduce_done(fly_1, "rs1", (tok,))
    small_all = small_all + tok[0:1, 0:1]
    tot = _sum_devices(small_all, name="sum_small")
    loss = tot[FINAL_ROW + 1, 0]
    mod_rows = [r + o for r in SUB_ROW for o in (3, 2, 0)]
    g_ada_b = jnp.stack([tot[r] for r in mod_rows]).reshape(2, 6 * D)
    g_mix_norm = jnp.stack([tot[SUB_ROW[0] + 1], tot[SUB_ROW[2] + 1]])
    g_mlp_norm = jnp.stack([tot[SUB_ROW[1] + 1], tot[SUB_ROW[3] + 1]])
    conv_sums = tot[SSD_CONV_ROW:SSD_CONV_ROW + 24].reshape(8, CONVD)
    g_ssd_conv_w = lax.dynamic_slice(conv_sums, (0, chip * 768), (4, 768))[None]
    g_ssd_conv_b = conv_sums[4:5]
    g_ssd_norm = tot[GNORM_ROW:GNORM_ROW + 2].reshape(1, DI)
    g_final = tot[FINAL_ROW]
    g_sc_conv_w = lax.dynamic_slice(tot[SC_CONV_ROW:SC_CONV_ROW + 3], (0, chip * 256), (3, 256))[None]
    g_a_log, g_d, g_dt_bias = (tot[HEAD_ROW + r:HEAD_ROW + r + 1, 0:NH] for r in range(3))
    c_pad = jnp.concatenate([c_all, jnp.zeros((8, D), F32)], axis=0)
    dmod_all = jnp.stack([small_all[:, r] for r in mod_rows], axis=1).reshape(N_DEV, 2, 6 * D)
    g_ada_w = []
    for i in range(2):
        dm = lax.dynamic_slice(dmod_all[:, i], (0, chip * n_ada), (N_DEV, n_ada))
        g_ada_w.append(_matmul_tn(c_pad, jnp.concatenate([dm, jnp.zeros_like(dm)], axis=0), m=D, n=n_ada, a_silu=True,
                                  name=f"ada_dw{i}"))
    g_ada_w = jnp.stack(g_ada_w)

    big = dict(mlp_up=[(t_b, up_row), (t_c, up_row)], mlp_down=[(t_b, down_row), (t_c, down_row)],
               ssd_out_w=[(t_ssd_out, 0)], sc_out_w=[(t_c, sc_out_row)], sc_in_w=[(t_sc_in, 0)], ssd_in_w=None)
    grads = dict(ada_w=g_ada_w, ada_b=g_ada_b, mix_norm_w=g_mix_norm, mlp_norm_w=g_mlp_norm, ssd_conv_w=g_ssd_conv_w,
                 ssd_conv_b=g_ssd_conv_b, ssd_dt_bias=g_dt_bias, ssd_A_log=g_a_log, ssd_D=g_d, ssd_norm_w=g_ssd_norm,
                 sc_conv_w=g_sc_conv_w, final_norm_w=g_final)
    weights = dict(ada_w=(ada_w, m_ada_w, v_ada_w), ada_b=(ada_b, m_ada_b, v_ada_b),
                   mix_norm_w=(mix_norm_w, m_mix_norm_w, v_mix_norm_w), mlp_norm_w=(mlp_norm_w, m_mlp_norm_w, v_mlp_norm_w),
                   mlp_up=(mlp_up, m_mlp_up, v_mlp_up), mlp_down=(mlp_down, m_mlp_down, v_mlp_down),
                   ssd_in_w=(ssd_in_w, m_ssd_in_w, v_ssd_in_w), ssd_conv_w=(ssd_conv_w, m_ssd_conv_w, v_ssd_conv_w),
                   ssd_conv_b=(ssd_conv_b, m_ssd_conv_b, v_ssd_conv_b), ssd_dt_bias=(ssd_dt_bias, m_ssd_dt_bias, v_ssd_dt_bias),
                   ssd_A_log=(ssd_A_log, m_ssd_A_log, v_ssd_A_log), ssd_D=(ssd_D, m_ssd_D, v_ssd_D),
                   ssd_norm_w=(ssd_norm_w, m_ssd_norm_w, v_ssd_norm_w), ssd_out_w=(ssd_out_w, m_ssd_out_w, v_ssd_out_w),
                   sc_in_w=(sc_in_w, m_sc_in_w, v_sc_in_w), sc_conv_w=(sc_conv_w, m_sc_conv_w, v_sc_conv_w),
                   sc_out_w=(sc_out_w, m_sc_out_w, v_sc_out_w), final_norm_w=(final_norm_w, m_final_norm_w, v_final_norm_w))
    def step(nm, parts):
        w, m, v = (t.reshape(-1, t.shape[-1]) for t in weights[nm])
        rows, outs = w.shape[0] // len(parts), None
        for i, (gbuf, g_row) in enumerate(parts):
            outs = _adamw(w, gbuf, m, v, g_row=g_row, w_row=i * rows, rows=rows, into=outs, emit_g=True, name=f"adamw_{nm}{i}")
        return outs

    res = {}
    for nm, (w, m, v) in weights.items():
        two_d = (-1, w.shape[-1]) if w.ndim > 1 else (1, -1)
        if nm not in big:
            res[nm] = (grads[nm], *_adamw(w.reshape(two_d), grads[nm].reshape(two_d), m.reshape(two_d), v.reshape(two_d),
                                          name="adamw_" + nm))
        elif big[nm] is not None:
            res[nm] = step(nm, big[nm])
    fly_2, tok = reduce_sum(fly_2, "rs2", (res["sc_out_w"][1],))
    (t_ssd_in,) = reduce_done(fly_2, "rs2", (tok,))
    res["ssd_in_w"] = step("ssd_in_w", [(t_ssd_in, 0)])
    outs = [[res[nm][k].reshape(weights[nm][0].shape) for nm in weights] for k in range(4)]
    return (loss, grad_x[None], *outs[0], *outs[1], *outs[2], *outs[3])
```

```python
import jax
import jax.numpy as jnp
from jax import lax
from jax.experimental import pallas as pl
from jax.experimental.pallas import tpu as pltpu

F32 = jnp.float32
BF16 = jnp.bfloat16
MESH = pl.DeviceIdType.MESH

D = 1024
DFF = 4096
DI = 2048
NH = 32
HP = 64
NG = 4
NS = 128
CH = 128
CONVD = DI + 2 * NG * NS
ZX = DI + CONVD
GW = NG * NS
LANES = 128
N_CHIPS = 4
N_DEV = 8
EPS = 1e-5
ADAM_LR, ADAM_B1, ADAM_B2, ADAM_EPS, ADAM_WD, ADAM_STEP = 1e-3, 0.9, 0.999, 1e-8, 0.01, 10
VMEM_LIMIT = 48 * 1024 * 1024
TM_ALL = 2048
TM_HALF = 1024
ANY = pl.BlockSpec(memory_space=pl.ANY)
HBM = pl.BlockSpec(memory_space=pltpu.HBM)
SEM = pl.BlockSpec(memory_space=pltpu.SEMAPHORE)

SSD_IN_SHARD = 1288
SC_IN_SHARD = 768


def _params(sem=None):
    return pltpu.CompilerParams(dimension_semantics=sem, vmem_limit_bytes=VMEM_LIMIT)


def _sigmoid(v):
    return 1.0 / (1.0 + jnp.exp(-v))


def _dot(a, b, dims=((1,), (0,)), precision=None):
    return lax.dot_general(a, b, (dims, ((), ())), preferred_element_type=F32, precision=precision)


def _dot_nt(a, b):
    return _dot(a, b, ((1,), (1,)))


def _dot_tn(a, b):
    return _dot(a, b, ((0,), (0,)))


def _nn(av, bv):
    return _dot(av.astype(BF16), bv.astype(BF16))


def _nt(av, bv):
    return _dot_nt(av.astype(BF16), bv.astype(BF16))


def _nn_split(av, bv):
    return _dot(av.astype(BF16), bv.reshape(-1, bv.shape[2]))


def _nn_split_sq(av, bv):
    af = av.astype(F32)
    return _nn_split(af * af, bv)


def _nt_split(av, bv):
    kc = bv.shape[2]
    acc = _dot_nt(av[:, 0:kc].astype(BF16), bv[0])
    for s in range(1, bv.shape[0]):
        acc = acc + _dot_nt(av[:, s * kc:(s + 1) * kc].astype(BF16), bv[s])
    return acc


def _nt_sc_in(av, bv):
    q = 256
    acc = None
    for i in range(3 * D // q):
        a_blk = av[i // 4][:, (i % 4) * q:(i % 4 + 1) * q]
        b_blk = bv[i // 3][:, (i % 3) * q:(i % 3 + 1) * q]
        t = _dot_nt(a_blk, b_blk)
        acc = t if acc is None else acc + t
    return acc


def _matmul(a, b, *, name, n, contract=_nn, a_spec=None, b_spec=None, tm=512, tn=512, extras=(), epi=None,
            out_dtypes=(F32,), a_silu=False):
    M = a.shape[-2]
    tm, tn = min(tm, M), min(tn, n)
    assert M % tm == 0 and n % tn == 0, (name, M, n, tm, tn)
    n_ex = len(extras)
    if a_spec is None:
        a_spec = pl.BlockSpec((tm, a.shape[1]), lambda i, j: (i, 0))
    if b_spec is None:
        b_spec = (pl.BlockSpec((tn, b.shape[1]), lambda i, j: (j, 0)) if contract is _nt
                  else pl.BlockSpec((b.shape[0], tn), lambda i, j: (0, j)))

    def body(*refs):
        av = refs[0][...]
        if a_silu:
            av = av * _sigmoid(av)
        acc = contract(av, refs[1][...])
        res = epi(acc, *[r[...] for r in refs[2:2 + n_ex]]) if epi is not None else (acc,)
        for o_ref, r in zip(refs[2 + n_ex:], res, strict=True):
            o_ref[...] = r.astype(o_ref.dtype)

    in_specs = [a_spec, b_spec]
    for e in extras:
        in_specs.append(pl.BlockSpec((1, tn), lambda i, j: (0, j)) if e.shape[0] == 1 and M != 1
                        else pl.BlockSpec((tm, tn), lambda i, j: (i, j)))
    outs = pl.pallas_call(
        body, grid=(M // tm, n // tn), in_specs=in_specs,
        out_specs=[pl.BlockSpec((tm, tn), lambda i, j: (i, j)) for _ in out_dtypes],
        out_shape=[jax.ShapeDtypeStruct((M, n), dt) for dt in out_dtypes],
        compiler_params=_params(("parallel", "parallel")), name=name)(a, b, *extras)
    return outs if len(out_dtypes) > 1 else outs[0]


def _matmul_tn(a, b, *, name, m, n, tm=512, tn=512, a_spec=None, b_spec=None, out_spec=None, out_struct=None, into=None,
               a_silu=False, a_square=False):
    T = a.shape[-2]
    tm, tn = min(tm, m), min(tn, n)
    assert m % tm == 0 and n % tn == 0, (name, m, n, tm, tn)
    if a_spec is None:
        a_spec = pl.BlockSpec((T, tm), lambda i, j: (0, i))
    if b_spec is None:
        b_spec = pl.BlockSpec((T, tn), lambda i, j: (0, j))
    if out_spec is None:
        out_spec, out_struct = pl.BlockSpec((tm, tn), lambda i, j: (i, j)), jax.ShapeDtypeStruct((m, n), F32)

    def body(a_ref, b_ref, *rest):
        av = a_ref[...]
        if a_silu:
            av = av * _sigmoid(av)
        if a_square:
            av = av.astype(F32) * av.astype(F32)
        rest[-1][...] = _dot_tn(av.astype(BF16), b_ref[...].astype(BF16)).astype(rest[-1].dtype)

    args, in_specs, alias = [a, b], [a_spec, b_spec], {}
    if into is not None:
        args, in_specs, alias = args + [into], in_specs + [ANY], {2: 0}
    return pl.pallas_call(body, grid=(m // tm, n // tn), in_specs=in_specs, out_specs=out_spec, out_shape=out_struct,
                          input_output_aliases=alias, compiler_params=_params(("parallel", "parallel")), name=name)(*args)


def _modnorm_fwd(x, nw, sc, sh, *, name):
    L = x.shape[0]
    tm = min(L, 512)

    def body(x_ref, nw_ref, sc_ref, sh_ref, h_ref):
        xv = x_ref[...]
        r = lax.rsqrt(jnp.mean(xv * xv, axis=-1, keepdims=True) + EPS)
        h_ref[...] = ((xv * r * nw_ref[...]) * (1.0 + sc_ref[...]) + sh_ref[...]).astype(BF16)

    row = pl.BlockSpec((tm, D), lambda i: (i, 0))
    vec = pl.BlockSpec((1, D), lambda i: (0, 0))
    return pl.pallas_call(body, grid=(L // tm,), in_specs=[row, vec, vec, vec], out_specs=row,
                          out_shape=jax.ShapeDtypeStruct((L, D), BF16),
                          compiler_params=_params(("parallel",)), name=name)(x, nw, sc, sh)


def _modnorm_bwd(x, dh, dxo, nw, sc, gsum, *, name):
    L = x.shape[0]
    tm = min(L, 256)

    def body(x_ref, dh_ref, dxo_ref, nw_ref, sc_ref, g_ref, dx_ref, s_ref):
        @pl.when(pl.program_id(0) == 0)
        def _():
            s_ref[...] = g_ref[...]

        xv, dhv = x_ref[...], dh_ref[...]
        r = lax.rsqrt(jnp.mean(xv * xv, axis=-1, keepdims=True) + EPS)
        xhat = xv * r
        dxhat = dhv * (nw_ref[...] * (1.0 + sc_ref[...]))
        dx_ref[...] = dxo_ref[...] + r * (dxhat - xhat * jnp.mean(dxhat * xhat, axis=-1, keepdims=True))
        s_ref[1:2, :] += jnp.sum(dhv * xhat, axis=0, keepdims=True) * (1.0 + sc_ref[...])
        s_ref[2:3, :] += jnp.sum(dhv * xhat, axis=0, keepdims=True) * nw_ref[...]
        s_ref[3:4, :] += jnp.sum(dhv, axis=0, keepdims=True)

    row = pl.BlockSpec((tm, D), lambda i: (i, 0))
    vec = pl.BlockSpec((1, D), lambda i: (0, 0))
    blk = pl.BlockSpec((8, D), lambda i: (0, 0))
    return pl.pallas_call(body, grid=(L // tm,), in_specs=[row, row, row, vec, vec, blk], out_specs=[row, blk],
                          out_shape=[jax.ShapeDtypeStruct((L, D), F32), jax.ShapeDtypeStruct((8, D), F32)],
                          compiler_params=_params(("arbitrary",)), name=name)(x, dh, dxo, nw, sc, gsum)


def _gate_bwd(dxo, y, g, *, name):
    L = dxo.shape[0]
    tm = min(L, 512)

    def body(dxo_ref, y_ref, g_ref, dy_ref, s_ref):
        @pl.when(pl.program_id(0) == 0)
        def _():
            s_ref[...] = jnp.zeros_like(s_ref)

        dv = dxo_ref[...]
        dy_ref[...] = (dv * g_ref[...]).astype(BF16)
        s_ref[0:1, :] += jnp.sum(dv * y_ref[...], axis=0, keepdims=True)

    row = pl.BlockSpec((tm, D), lambda i: (i, 0))
    return pl.pallas_call(body, grid=(L // tm,), in_specs=[row, row, pl.BlockSpec((1, D), lambda i: (0, 0))],
                          out_specs=[row, pl.BlockSpec((8, D), lambda i: (0, 0))],
                          out_shape=[jax.ShapeDtypeStruct((L, D), BF16), jax.ShapeDtypeStruct((8, D), F32)],
                          compiler_params=_params(("arbitrary",)), name=name)(dxo, y, g)


def _final_loss(x, fw, tgt, *, name):
    L = x.shape[0]
    tm = min(L, 256)

    def body(x_ref, fw_ref, t_ref, dx_ref, s_ref):
        @pl.when(pl.program_id(0) == 0)
        def _():
            s_ref[...] = jnp.zeros_like(s_ref)

        xv = x_ref[...]
        r = lax.rsqrt(jnp.mean(xv * xv, axis=-1, keepdims=True) + EPS)
        xhat = xv * r
        diff = xhat * fw_ref[...] - t_ref[...]
        dout = diff * (1.0 / D)
        dxhat = dout * fw_ref[...]
        dx_ref[...] = r * (dxhat - xhat * jnp.mean(dxhat * xhat, axis=-1, keepdims=True))
        s_ref[0:1, :] += jnp.sum(dout * xhat, axis=0, keepdims=True)
        s_ref[1:2, :] += jnp.zeros((1, D), F32) + 0.5 * jnp.sum(jnp.sum(diff * diff, axis=-1, keepdims=True) * (1.0 / D))

    row = pl.BlockSpec((tm, D), lambda i: (i, 0))
    return pl.pallas_call(body, grid=(L // tm,), in_specs=[row, pl.BlockSpec((1, D), lambda i: (0, 0)), row],
                          out_specs=[row, pl.BlockSpec((8, D), lambda i: (0, 0))],
                          out_shape=[jax.ShapeDtypeStruct((L, D), F32), jax.ShapeDtypeStruct((8, D), F32)],
                          compiler_params=_params(("arbitrary",)), name=name)(x, fw, tgt)


def _shift_down(v, j):
    if j == 0:
        return v
    row = lax.broadcasted_iota(jnp.int32, v.shape, 0)
    return jnp.where(row >= j, pltpu.roll(v, j, 0), 0.0)


def _shift_up(v, j):
    if j == 0:
        return v
    n = v.shape[0]
    row = lax.broadcasted_iota(jnp.int32, v.shape, 0)
    return jnp.where(row < n - j, pltpu.roll(v, n - j, 0), 0.0)


def _ssd_conv_fwd(zx, w, b, *, name):
    L = zx.shape[0]
    cb = 256
    k = w.shape[0]

    def body(x_ref, w_ref, b_ref, o_ref):
        xv = x_ref[...].astype(F32)
        pre = b_ref[...] + xv * w_ref[k - 1:k, :]
        for j in range(1, k):
            pre = pre + _shift_down(xv, j) * w_ref[k - 1 - j:k - j, :]
        o_ref[...] = (pre * _sigmoid(pre)).astype(BF16)

    return pl.pallas_call(
        body, grid=(CONVD // cb,),
        in_specs=[pl.BlockSpec((L, cb), lambda i: (0, i + DI // cb)), pl.BlockSpec((k, cb), lambda i: (0, i)),
                  pl.BlockSpec((1, cb), lambda i: (0, i))],
        out_specs=pl.BlockSpec((L, cb), lambda i: (0, i)), out_shape=jax.ShapeDtypeStruct((L, CONVD), BF16),
        compiler_params=_params(("parallel",)), name=name)(zx, w, b)


def _ssd_conv_bwd(zx, dact, w, b, dzx, *, name):
    L = zx.shape[0]
    cb = 256
    k = w.shape[0]

    def body(x_ref, da_ref, w_ref, b_ref, _, dx_ref, s_ref):
        xv = x_ref[...].astype(F32)
        sh = [_shift_down(xv, j) for j in range(k)]
        pre = b_ref[...] + sh[0] * w_ref[k - 1:k, :]
        for j in range(1, k):
            pre = pre + sh[j] * w_ref[k - 1 - j:k - j, :]
        s = _sigmoid(pre)
        dpre = da_ref[...].astype(F32) * (s * (1.0 + pre * (1.0 - s)))
        dx = dpre * w_ref[k - 1:k, :]
        for j in range(1, k):
            dx = dx + _shift_up(dpre, j) * w_ref[k - 1 - j:k - j, :]
        dx_ref[...] = dx.astype(BF16)
        s_ref[...] = jnp.zeros_like(s_ref)
        for j in range(k):
            s_ref[k - 1 - j:k - j, :] = jnp.sum(dpre * sh[j], axis=0, keepdims=True)
        s_ref[k:k + 1, :] = jnp.sum(dpre, axis=0, keepdims=True)

    return pl.pallas_call(
        body, grid=(CONVD // cb,),
        in_specs=[pl.BlockSpec((L, cb), lambda i: (0, i + DI // cb)), pl.BlockSpec((L, cb), lambda i: (0, i)),
                  pl.BlockSpec((k, cb), lambda i: (0, i)), pl.BlockSpec((1, cb), lambda i: (0, i)), ANY],
        out_specs=[pl.BlockSpec((L, cb), lambda i: (0, i + DI // cb)), pl.BlockSpec((8, cb), lambda i: (0, i))],
        out_shape=[jax.ShapeDtypeStruct((L, ZX), BF16), jax.ShapeDtypeStruct((8, CONVD), F32)],
        input_output_aliases={4: 0}, compiler_params=_params(("parallel",)), name=name)(zx, dact, w, b, dzx)


def _sc_fwd(proj, w, *, name):
    L = proj.shape[0]
    cb = 256
    nb = D // cb
    k = w.shape[0]

    def body(b_ref, c_ref, x_ref, w_ref, o_ref):
        u = c_ref[...] * x_ref[...]
        v = u * w_ref[k - 1:k, :]
        for j in range(1, k):
            v = v + _shift_down(u, j) * w_ref[k - 1 - j:k - j, :]
        o_ref[...] = (b_ref[...] * v).astype(BF16)

    return pl.pallas_call(
        body, grid=(nb,),
        in_specs=[pl.BlockSpec((L, cb), lambda i: (0, i)), pl.BlockSpec((L, cb), lambda i: (0, i + nb)),
                  pl.BlockSpec((L, cb), lambda i: (0, i + 2 * nb)), pl.BlockSpec((k, cb), lambda i: (0, i))],
        out_specs=pl.BlockSpec((L, cb), lambda i: (0, i)), out_shape=jax.ShapeDtypeStruct((L, D), BF16),
        compiler_params=_params(("parallel",)), name=name)(proj, proj, proj, w)


def _sc_bwd(proj, dyv, w, *, name):
    L = proj.shape[0]
    cb = 256
    nb = D // cb
    k = w.shape[0]

    def body(b_ref, c_ref, x_ref, dy_ref, w_ref, dp_ref, s_ref):
        cv, xv = c_ref[...], x_ref[...]
        u = cv * xv
        sh = [_shift_down(u, j) for j in range(k)]
        v = sh[0] * w_ref[k - 1:k, :]
        for j in range(1, k):
            v = v + sh[j] * w_ref[k - 1 - j:k - j, :]
        dyv_ = dy_ref[...]
        dp_ref[0] = (dyv_ * v).astype(BF16)
        dv = dyv_ * b_ref[...]
        du = dv * w_ref[k - 1:k, :]
        for j in range(1, k):
            du = du + _shift_up(dv, j) * w_ref[k - 1 - j:k - j, :]
        dp_ref[1] = (du * xv).astype(BF16)
        dp_ref[2] = (du * cv).astype(BF16)
        s_ref[...] = jnp.zeros_like(s_ref)
        for j in range(k):
            s_ref[k - 1 - j:k - j, :] = jnp.sum(dv * sh[j], axis=0, keepdims=True)

    blk = pl.BlockSpec((L, cb), lambda i: (0, i))
    return pl.pallas_call(
        body, grid=(nb,),
        in_specs=[blk, pl.BlockSpec((L, cb), lambda i: (0, i + nb)), pl.BlockSpec((L, cb), lambda i: (0, i + 2 * nb)),
                  blk, pl.BlockSpec((k, cb), lambda i: (0, i))],
        out_specs=[pl.BlockSpec((3, L, cb), lambda i: (0, 0, i)), pl.BlockSpec((8, cb), lambda i: (0, i))],
        out_shape=[jax.ShapeDtypeStruct((3, L, D), BF16), jax.ShapeDtypeStruct((8, D), F32)],
        compiler_params=_params(("parallel",)), name=name)(proj, proj, proj, dyv, w)


def _pieces(v, n):
    out, rest = [], v
    for _ in range(n):
        out.append(rest.astype(BF16))
        rest = rest - out[-1].astype(F32)
    return out


def _cumsum_rows(mask, v):
    m = mask.astype(BF16)
    return _dot(jnp.concatenate([m, m, m], axis=1), jnp.concatenate(_pieces(v, 3), axis=0))


def _ssd_chunk_terms(dtr, prm):
    lane = lax.broadcasted_iota(jnp.int32, (CH, LANES), 1)
    valid = lane < NH
    xdt = dtr + prm[0:1, :]
    dt = jnp.where(valid, jnp.maximum(xdt, 0.0) + jnp.log1p(jnp.exp(-jnp.abs(xdt))), 0.0)
    A = -jnp.exp(prm[1:2, :])
    ri = lax.broadcasted_iota(jnp.int32, (CH, CH), 0)
    ci = lax.broadcasted_iota(jnp.int32, (CH, CH), 1)
    cs = _cumsum_rows(ri >= ci, dt * A)
    last = cs[CH - 1:CH, :]
    spread = (lax.broadcasted_iota(jnp.int32, (2 * LANES, DI), 1) // HP
              == lax.broadcasted_iota(jnp.int32, (2 * LANES, DI), 0) % LANES).astype(BF16)
    gather = ((lax.broadcasted_iota(jnp.int32, (LANES, 2 * DI), 1) % DI) // HP
              == lax.broadcasted_iota(jnp.int32, (LANES, 2 * DI), 0)).astype(BF16)
    return dict(valid=valid, xdt=xdt, dt=dt, A=A, cs=cs, csT=cs.T, last=last, ri=ri, ci=ci, ex=(spread, gather))


def _expand(v, ex):
    if v.shape[0] == 1:
        return _expand(jnp.broadcast_to(v, (8, LANES)), ex)[0:1, :]
    return _dot(jnp.concatenate(_pieces(v, 2), axis=1), ex[0])


def _head_sum(v, ex):
    if v.shape[0] == 1:
        return _head_sum(jnp.broadcast_to(v, (8, DI)), ex)[0:1, :]
    return _dot_nt(jnp.concatenate(_pieces(v, 2), axis=1), ex[1])


def _ssd_fwd(xbc, dtr, prm, *, name):
    L = xbc.shape[0]
    nc = L // CH

    def body(xbc_ref, dtr_ref, prm_ref, y_ref, sp_ref, st_ref):
        @pl.when(pl.program_id(0) == 0)
        def _():
            st_ref[...] = jnp.zeros_like(st_ref)

        prm_v = prm_ref[...]
        t = _ssd_chunk_terms(dtr_ref[...], prm_v)
        cs, csT, ex, causal = t["cs"], t["csT"], t["ex"], t["ri"] >= t["ci"]
        xs = xbc_ref[:, 0:DI].astype(F32)
        X = xs * _expand(t["dt"], ex)
        Xb = X.astype(BF16)
        Xd = (X * _expand(jnp.exp(t["last"] - cs), ex)).astype(BF16)
        Ex = _expand(jnp.exp(cs), ex)
        cdx = _expand(jnp.exp(t["last"]), ex)
        dskx = _expand(prm_v[2:3, :], ex)
        lane = lax.broadcasted_iota(jnp.int32, (CH, LANES), 1)
        sp_ref[0] = st_ref[...]
        for g in range(NG):
            Bg = xbc_ref[:, DI + g * NS:DI + (g + 1) * NS].astype(BF16)
            Cg = xbc_ref[:, DI + GW + g * NS:DI + GW + (g + 1) * NS].astype(BF16)
            G = _dot_nt(Cg, Bg)
            Sg = st_ref[:, g * GW:(g + 1) * GW]
            yoff = _dot(Cg, Sg.astype(BF16)) * Ex[:, g * GW:(g + 1) * GW]
            for j in range(GW // LANES):
                lo = g * GW + j * LANES
                Xp = Xb[:, lo:lo + LANES]
                yd = []
                for h in (lo // HP, lo // HP + 1):
                    seg = cs[:, h:h + 1] - csT[h:h + 1, :]
                    yd.append(_dot((G * jnp.where(causal, jnp.exp(seg), 0.0)).astype(BF16), Xp))
                y_ref[:, lo:lo + LANES] = (jnp.where(lane < HP, yd[0], yd[1]) + yoff[:, j * LANES:(j + 1) * LANES]
                                           + dskx[:, lo:lo + LANES] * xs[:, lo:lo + LANES]).astype(BF16)
            st_ref[:, g * GW:(g + 1) * GW] = Sg * cdx[:, g * GW:(g + 1) * GW] + _dot_tn(Bg, Xd[:, g * GW:(g + 1) * GW])

    return pl.pallas_call(
        body, grid=(nc,),
        in_specs=[pl.BlockSpec((CH, CONVD), lambda c: (c, 0)), pl.BlockSpec((CH, LANES), lambda c: (c, 0)),
                  pl.BlockSpec((8, LANES), lambda c: (0, 0))],
        out_specs=[pl.BlockSpec((CH, DI), lambda c: (c, 0)), pl.BlockSpec((1, NS, DI), lambda c: (c, 0, 0))],
        out_shape=[jax.ShapeDtypeStruct((L, DI), BF16), jax.ShapeDtypeStruct((nc, NS, DI), F32)],
        scratch_shapes=[pltpu.VMEM((NS, DI), F32)],
        compiler_params=_params(("arbitrary",)), name=name)(xbc, dtr, prm)


def _ssd_bwd(xbc, dtr, prm, dy, sprev, *, name):
    L = xbc.shape[0]
    nc = L // CH

    def body(xbc_ref, dtr_ref, prm_ref, dy_ref, sp_ref, dxbc_ref, ddtr_ref, s_ref, dst_ref, dx_scr, de_scr, dd_scr):
        step = pl.program_id(0)

        @pl.when(step == 0)
        def _():
            dst_ref[...] = jnp.zeros_like(dst_ref)
            s_ref[...] = jnp.zeros_like(s_ref)

        prm_v = prm_ref[...]
        t = _ssd_chunk_terms(dtr_ref[...], prm_v)
        cs, csT, ex, ri, ci = t["cs"], t["csT"], t["ex"], t["ri"], t["ci"]
        E = jnp.exp(cs)
        dec = jnp.exp(t["last"] - cs)
        cd = jnp.exp(t["last"])
        xs = xbc_ref[:, 0:DI].astype(F32)
        dtx = _expand(t["dt"], ex)
        X = xs * dtx
        Xb = X.astype(BF16)
        decx = _expand(dec, ex)
        Xd = (X * decx).astype(BF16)
        Ex = _expand(E, ex)
        cdx = _expand(cd, ex)
        dskx = _expand(prm_v[2:3, :], ex)
        lane = lax.broadcasted_iota(jnp.int32, (CH, LANES), 1)
        dcs = jnp.zeros((CH, LANES), F32)
        dcd_x = []
        for g in range(NG):
            gs = slice(g * GW, (g + 1) * GW)
            Bg = xbc_ref[:, DI + g * NS:DI + (g + 1) * NS].astype(BF16)
            Cg = xbc_ref[:, DI + GW + g * NS:DI + GW + (g + 1) * NS].astype(BF16)
            G = _dot_nt(Cg, Bg)
            GT = _dot_nt(Bg, Cg)
            Sg = sp_ref[0, :, gs]
            Sgb = Sg.astype(BF16)
            dyg = dy_ref[:, gs]
            de_scr[:, gs] = dyg * _dot(Cg, Sgb)
            dYo = (Ex[:, gs] * dyg).astype(BF16)
            dC = _dot_nt(dYo, Sgb)
            dS_in = _dot_tn(Cg, dYo)
            dStg = dst_ref[:, gs]
            dStb = dStg.astype(BF16)
            dXd = _dot(Bg, dStb)
            dB = _dot_nt(Xd[:, gs], dStb)
            dd_scr[:, gs] = dXd * X[:, gs]
            dXst = dXd * decx[:, gs]
            dG = jnp.zeros((CH, CH), F32)
            dGT = jnp.zeros((CH, CH), F32)
            for j in range(GW // LANES):
                lo = g * GW + j * LANES
                Xp = Xb[:, lo:lo + LANES]
                dyp = dy_ref[:, lo:lo + LANES]
                dXp = dXst[:, j * LANES:(j + 1) * LANES]
                for k, h in enumerate((lo // HP, lo // HP + 1)):
                    dyh = jnp.where((lane < HP) if k == 0 else (lane >= HP), dyp, 0.0).astype(BF16)
                    seg = cs[:, h:h + 1] - csT[h:h + 1, :]
                    Lm = jnp.where(ri >= ci, jnp.exp(seg), 0.0)
                    LmT = jnp.where(ci >= ri, jnp.exp(-seg), 0.0)
                    dM = _dot_nt(dyh, Xp)
                    dMT = _dot_nt(Xp, dyh)
                    MT = GT * LmT
                    rs = jnp.sum(dM * (G * Lm), axis=1, keepdims=True) - jnp.sum(dMT * MT, axis=1, keepdims=True)
                    dcs = dcs + jnp.where(lane == h, rs, 0.0)
                    dG = dG + dM * Lm
                    dGT = dGT + dMT * LmT
                    dXp = dXp + _dot(MT.astype(BF16), dyh)
                dx_scr[:, lo:lo + LANES] = dXp
            dxbc_ref[:, DI + g * NS:DI + (g + 1) * NS] = (dB + _dot(dGT.astype(BF16), Cg)).astype(BF16)
            dxbc_ref[:, DI + GW + g * NS:DI + GW + (g + 1) * NS] = (dC + _dot(dG.astype(BF16), Bg)).astype(BF16)
            dcd_x.append(jnp.sum(dStg * Sg, axis=0, keepdims=True))
            dst_ref[:, gs] = dStg * cdx[:, gs] + dS_in
        dX = dx_scr[...]
        dy = dy_ref[...]
        ddec = _head_sum(dd_scr[...], ex)
        dcd = _head_sum(jnp.concatenate(dcd_x, axis=1), ex)
        dcs = dcs + _head_sum(de_scr[...], ex) * E - ddec * dec
        row = lax.broadcasted_iota(jnp.int32, (CH, LANES), 0)
        dcs = dcs + jnp.where(row == CH - 1, jnp.sum(ddec * dec, axis=0, keepdims=True) + dcd * cd, 0.0)
        da = _cumsum_rows(ci >= ri, dcs)
        ddt = da * t["A"] + _head_sum(dX * xs, ex)
        ddtr = jnp.where(t["valid"], ddt * _sigmoid(t["xdt"]), 0.0)
        ddtr_ref[...] = ddtr
        dxbc_ref[:, 0:DI] = (dX * dtx + dskx * dy).astype(BF16)
        s_ref[0:1, :] += jnp.sum(da * t["dt"], axis=0, keepdims=True)
        s_ref[1:2, :] += _head_sum(jnp.sum(dy * xs, axis=0, keepdims=True), ex)
        s_ref[2:3, :] += jnp.sum(ddtr, axis=0, keepdims=True)

        @pl.when(step == nc - 1)
        def _():
            s_ref[0:1, :] = s_ref[0:1, :] * t["A"]

    rev = lambda c: (nc - 1 - c, 0)
    return pl.pallas_call(
        body, grid=(nc,),
        in_specs=[pl.BlockSpec((CH, CONVD), rev), pl.BlockSpec((CH, LANES), rev), pl.BlockSpec((8, LANES), lambda c: (0, 0)),
                  pl.BlockSpec((CH, DI), rev), pl.BlockSpec((1, NS, DI), lambda c: (nc - 1 - c, 0, 0))],
        out_specs=[pl.BlockSpec((CH, CONVD), rev), pl.BlockSpec((CH, LANES), rev), pl.BlockSpec((8, LANES), lambda c: (0, 0))],
        out_shape=[jax.ShapeDtypeStruct((L, CONVD), BF16), jax.ShapeDtypeStruct((L, LANES), F32),
                   jax.ShapeDtypeStruct((8, LANES), F32)],
        scratch_shapes=[pltpu.VMEM((NS, DI), F32), pltpu.VMEM((CH, DI), F32), pltpu.VMEM((CH, DI), F32),
                        pltpu.VMEM((CH, DI), F32)],
        compiler_params=_params(("arbitrary",)), name=name)(xbc, dtr, prm, dy, sprev)


def _gnorm_fwd(y, zx, nw, *, name):
    L = y.shape[0]
    tm = min(L, 256)

    def body(y_ref, z_ref, nw_ref, o_ref):
        z = z_ref[...].astype(F32)
        yg = y_ref[...].astype(F32) * (z * _sigmoid(z))
        for g in range(NG):
            v = yg[:, g * GW:(g + 1) * GW]
            r = lax.rsqrt(jnp.mean(v * v, axis=-1, keepdims=True) + EPS)
            o_ref[:, g * GW:(g + 1) * GW] = (v * r * nw_ref[:, g * GW:(g + 1) * GW]).astype(BF16)

    row = pl.BlockSpec((tm, DI), lambda i: (i, 0))
    return pl.pallas_call(body, grid=(L // tm,), in_specs=[row, row, pl.BlockSpec((1, DI), lambda i: (0, 0))],
                          out_specs=row, out_shape=jax.ShapeDtypeStruct((L, DI), BF16),
                          compiler_params=_params(("parallel",)), name=name)(y, zx, nw)


def _gnorm_bwd(y, zx, nw, dyn, *, name):
    L = y.shape[0]
    tm = min(L, 256)

    def body(y_ref, z_ref, nw_ref, dyn_ref, dy_ref, dz_ref, s_ref):
        @pl.when(pl.program_id(0) == 0)
        def _():
            s_ref[...] = jnp.zeros_like(s_ref)

        z, yv = z_ref[...].astype(F32), y_ref[...].astype(F32)
        sz = _sigmoid(z)
        gate = z * sz
        dgate_dz = sz * (1.0 + z * (1.0 - sz))
        for g in range(NG):
            gs = slice(g * GW, (g + 1) * GW)
            v = yv[:, gs] * gate[:, gs]
            r = lax.rsqrt(jnp.mean(v * v, axis=-1, keepdims=True) + EPS)
            vhat = v * r
            dn = dyn_ref[:, gs]
            s_ref[0:1, gs] += jnp.sum(dn * vhat, axis=0, keepdims=True)
            dvhat = dn * nw_ref[:, gs]
            dv = r * (dvhat - vhat * jnp.mean(dvhat * vhat, axis=-1, keepdims=True))
            dy_ref[:, gs] = dv * gate[:, gs]
            dz_ref[:, gs] = (dv * yv[:, gs] * dgate_dz[:, gs]).astype(BF16)

    row = pl.BlockSpec((tm, DI), lambda i: (i, 0))
    return pl.pallas_call(body, grid=(L // tm,), in_specs=[row, row, pl.BlockSpec((1, DI), lambda i: (0, 0)), row],
                          out_specs=[row, row, pl.BlockSpec((8, DI), lambda i: (0, 0))],
                          out_shape=[jax.ShapeDtypeStruct((L, DI), F32), jax.ShapeDtypeStruct((L, ZX), BF16),
                                     jax.ShapeDtypeStruct((8, DI), F32)],
                          compiler_params=_params(("arbitrary",)), name=name)(y, zx, nw, dyn)


def _adamw(w, g, m, v, *, name, g_row=0, w_row=0, rows=None, into=None, emit_g=False):
    R, C = w.shape
    rows = R if rows is None else rows
    tr = rows
    while tr * C > 256 * 1024 and tr % 16 == 0:
        tr //= 2
    assert g_row % tr == 0 and w_row % tr == 0, (name, g_row, w_row, tr)
    n_out = 4 if emit_g else 3

    def body(w_ref, g_ref, m_ref, v_ref, *rest):
        outs = rest[-n_out:]
        gv = g_ref[...]
        mn = ADAM_B1 * m_ref[...] + (1.0 - ADAM_B1) * gv
        vn = ADAM_B2 * v_ref[...] + (1.0 - ADAM_B2) * (gv * gv)
        m_hat = mn / (1.0 - ADAM_B1 ** ADAM_STEP)
        v_hat = vn / (1.0 - ADAM_B2 ** ADAM_STEP)
        d_ref, mo_ref, vo_ref = outs[-3:]
        d_ref[...] = -ADAM_LR * (m_hat / (jnp.sqrt(v_hat) + ADAM_EPS) + ADAM_WD * w_ref[...])
        mo_ref[...] = mn
        vo_ref[...] = vn
        if emit_g:
            outs[0][...] = gv

    blk = pl.BlockSpec((tr, C), lambda i: (i + w_row // tr, 0))
    args, in_specs, alias = [w, g, m, v], [blk, pl.BlockSpec((tr, C), lambda i: (i + g_row // tr, 0)), blk, blk], {}
    if into is not None:
        args, in_specs, alias = args + list(into), in_specs + [ANY] * n_out, {4 + k: k for k in range(n_out)}
    return pl.pallas_call(body, grid=(rows // tr,), in_specs=in_specs, out_specs=[blk] * n_out,
                          out_shape=[jax.ShapeDtypeStruct((R, C), F32)] * n_out, input_output_aliases=alias,
                          compiler_params=_params(("parallel",)), name=name)(*args)


def _residual(acc, xv, gv):
    return xv + gv * acc, acc


def _like(buf):
    return jax.ShapeDtypeStruct(buf.shape, buf.dtype)


def _mlp_fwd(x, mod, nw, wb, up_row, down_row, tag):
    sh, sc, g = mod
    h = _modnorm_fwd(x, nw, sc, sh, name=tag + "_norm")
    a = _matmul(h, wb, n=DFF, tm=TM_ALL, b_spec=pl.BlockSpec((None, D, 512), lambda mi, j: (j // 2, up_row // D, j % 2)),
                epi=lambda acc: (jnp.maximum(acc, 0.0),), out_dtypes=(BF16,), name=tag + "_up")
    xn, y = _matmul(a, wb, n=D, tm=TM_HALF, contract=_nn_split_sq,
                    b_spec=pl.BlockSpec((N_CHIPS, D, 512), lambda mi, j: (0, down_row // D, j)),
                    extras=(x, g), epi=_residual, out_dtypes=(F32, F32), name=tag + "_down")
    return xn, (x, h, a, y)


def _mlp_bwd(dxo, saved, mod, nw, wb, gb, up_row, down_row, tag):
    x, h, a, y = saved
    sh, sc, g = mod
    dy, gsum = _gate_bwd(dxo, y, g, name=tag + "_dgate")
    du = _matmul(dy, wb, n=DFF, tm=TM_ALL, contract=_nt,
                 b_spec=pl.BlockSpec((None, 512, D), lambda mi, j: (j // 2, down_row // 512 + j % 2, 0)),
                 extras=(a,), epi=lambda acc, av: (acc * (2.0 * av.astype(F32)),), out_dtypes=(BF16,), name=tag + "_dact")
    gb = _matmul_tn(a, dy, m=DFF, n=D, tm=D, tn=D, a_square=True, into=gb, out_struct=_like(wb),
                    out_spec=pl.BlockSpec((None, D, D), lambda mi, j: (mi, down_row // D, 0)), name=tag + "_ddown")
    dh = _matmul(du, wb, n=D, tm=TM_HALF, contract=_nt_split,
                 b_spec=pl.BlockSpec((N_CHIPS, 512, D), lambda mi, j: (0, up_row // 512 + j, 0)), name=tag + "_dh")
    gb = _matmul_tn(h, du, m=D, n=DFF, tm=D, into=gb, out_struct=_like(wb),
                    out_spec=pl.BlockSpec((None, D, 512), lambda mi, j: (j // 2, up_row // D, j % 2)), name=tag + "_dup")
    dx, sums = _modnorm_bwd(x, dh, dxo, nw, sc, gsum, name=tag + "_dnorm")
    return dx, gb, sums


def _ssd_fwd_scan(x, mod, nw, w_zx, w_dt, conv_w, conv_b, prm, tag):
    sh, sc, g = mod
    h = _modnorm_fwd(x, nw, sc, sh, name=tag + "_norm")
    zx = _matmul(h, w_zx, n=ZX, tm=TM_ALL, out_dtypes=(BF16,), name=tag + "_in")
    dtr = _matmul(h, w_dt, n=LANES, tm=TM_ALL, name=tag + "_in_dt")
    xbc = _ssd_conv_fwd(zx, conv_w, conv_b, name=tag + "_conv")
    y, sprev = _ssd_fwd(xbc, dtr, prm, name=tag + "_scan")
    return h, zx, dtr, xbc, y, sprev


def _ssd_fwd_out(x, mod, scan, gn_w, w_out, tag):
    sh, sc, g = mod
    h, zx, dtr, xbc, y, sprev = scan
    yn = _gnorm_fwd(y, zx, gn_w, name=tag + "_gnorm")
    xn, yo = _matmul(yn, w_out, n=D, tm=TM_HALF, contract=_nn_split,
                     b_spec=pl.BlockSpec((N_CHIPS, 512, 512), lambda mi, j: (0, 0, j)),
                     extras=(x, g), epi=_residual, out_dtypes=(F32, F32), name=tag + "_out")
    return xn, (x, h, zx, dtr, xbc, y, sprev, yn, yo)


def _ssd_bwd_out(dxo, saved, mod, w_out, tag):
    x, h, zx, dtr, xbc, y, sprev, yn, yo = saved
    sh, sc, g = mod
    dyo, gsum = _gate_bwd(dxo, yo, g, name=tag + "_dgate")
    dyn = _matmul(dyo, w_out, n=DI, tm=TM_ALL, contract=_nt, b_spec=pl.BlockSpec((None, 512, D), lambda mi, j: (j, 0, 0)),
                  name=tag + "_dyn")
    g_out = _matmul_tn(yn, dyo, m=DI, n=D, tn=D, out_struct=_like(w_out),
                       out_spec=pl.BlockSpec((None, 512, D), lambda mi, j: (mi, 0, 0)), name=tag + "_dout")
    return dyn, g_out, gsum


def _ssd_bwd_rest(dxo, dy, dzx, gsum, saved, mod, nw, w_zx, w_dt, conv_w, conv_b, prm, tag):
    x, h, zx, dtr, xbc, y, sprev, yn, yo = saved
    sh, sc, g = mod
    dxbc, ddtr, ssum = _ssd_bwd(xbc, dtr, prm, dy, sprev, name=tag + "_dscan")
    dzx, csum = _ssd_conv_bwd(zx, dxbc, conv_w, conv_b, dzx, name=tag + "_dconv")
    dh_dt = _matmul(ddtr, w_dt, n=D, tm=TM_ALL, contract=_nt, name=tag + "_dh_dt")
    dh = _matmul(dzx, w_zx, n=D, tm=TM_HALF, contract=_nt, extras=(dh_dt,), epi=lambda acc, e: (acc + e,), name=tag + "_dh")
    d_w_zx = _matmul_tn(h, dzx, m=D, n=ZX, tm=D, name=tag + "_din")
    d_w_dt = _matmul_tn(h, ddtr, m=D, n=LANES, tm=D, name=tag + "_din_dt")
    dx, sums = _modnorm_bwd(x, dh, dxo, nw, sc, gsum, name=tag + "_dnorm")
    return dx, d_w_zx, d_w_dt, sums, csum, ssum


def _sc_layer_fwd(x, mod, nw, w_sc_in, conv_w, wb, out_row, tag):
    sh, sc, g = mod
    h = _modnorm_fwd(x, nw, sc, sh, name=tag + "_norm")
    proj = _matmul(h, w_sc_in, n=3 * D, tm=TM_ALL, tn=256, b_spec=pl.BlockSpec((None, D, 256), lambda mi, j: (j // 3, 0, j % 3)),
                   name=tag + "_in")
    yv = _sc_fwd(proj, conv_w, name=tag + "_conv")
    xn, yo = _matmul(yv, wb, n=D, tm=TM_HALF, contract=_nn_split,
                     b_spec=pl.BlockSpec((N_CHIPS, 256, 512), lambda mi, j: (0, out_row // 256, j)),
                     extras=(x, g), epi=_residual, out_dtypes=(F32, F32), name=tag + "_out")
    return xn, (x, h, proj, yv, yo)


def _sc_layer_bwd(dxo, saved, mod, nw, w_sc_in, conv_w, wb, gb, out_row, tag):
    x, h, proj, yv, yo = saved
    sh, sc, g = mod
    L = x.shape[0]
    dyo, gsum = _gate_bwd(dxo, yo, g, name=tag + "_dgate")
    dyv = _matmul(dyo, wb, n=D, tm=TM_ALL, tn=256, contract=_nt,
                  b_spec=pl.BlockSpec((None, 256, D), lambda mi, j: (j, out_row // 256, 0)), name=tag + "_dyv")
    gb = _matmul_tn(yv, dyo, m=D, n=D, tm=256, tn=D, into=gb, out_struct=_like(wb),
                    out_spec=pl.BlockSpec((None, 256, D), lambda mi, j: (mi, out_row // 256, 0)), name=tag + "_dout")
    dproj, csum = _sc_bwd(proj, dyv, conv_w, name=tag + "_dconv")
    tm = min(L, TM_HALF)
    dh = _matmul(dproj, w_sc_in, n=D, tm=tm, contract=_nt_sc_in, a_spec=pl.BlockSpec((3, tm, D), lambda mi, j: (0, mi, 0)),
                 b_spec=pl.BlockSpec((N_CHIPS, 512, SC_IN_SHARD), lambda mi, j: (0, j, 0)), name=tag + "_dh")
    g_sc_in = _matmul_tn(h, dproj, m=D, n=3 * D, tm=D, tn=256, b_spec=pl.BlockSpec((None, L, 256), lambda mi, j: (j // 4, 0, j % 4)),
                         out_spec=pl.BlockSpec((None, D, 256), lambda mi, j: (j // 3, 0, j % 3)),
                         out_struct=jax.ShapeDtypeStruct((N_CHIPS, D, SC_IN_SHARD), BF16), name=tag + "_din")
    dx, sums = _modnorm_bwd(x, dh, dxo, nw, sc, gsum, name=tag + "_dnorm")
    return dx, gb, g_sc_in, sums, csum


SUB_ROW = (0, 8, 16, 24)
SSD_CONV_ROW, GNORM_ROW, FINAL_ROW, SC_CONV_ROW, HEAD_ROW, SMALL_ROWS = 32, 56, 72, 80, 88, 96


def _all_gather_rows(blk, *, name):
    m_per, n = blk.shape

    def body(x_ref, out_ref, send_sems, recv_sems, local_sem):
        x, y, c = lax.axis_index("x"), lax.axis_index("y"), lax.axis_index("c")
        me, sibling = (x, y, c), (x, y, 1 - c)
        chips = [(1 - x, y), (x, 1 - y), (1 - x, 1 - y)]

        def rows(px, py, pc):
            return out_ref.at[pl.ds((4 * px + 2 * py + pc) * m_per, m_per), :]

        def copy(k, block, to, src=None):
            return pltpu.make_async_remote_copy(src_ref=rows(*block) if src is None else src, dst_ref=rows(*block),
                                                send_sem=send_sems.at[k], recv_sem=recv_sems.at[k], device_id=to,
                                                device_id_type=MESH)

        mine = pltpu.make_async_copy(x_ref, rows(*me), local_sem)
        mine.start()
        first = [copy(0, me, sibling, src=x_ref)] + [copy(1 + j, me, (*chip, c), src=x_ref) for j, chip in enumerate(chips)]
        for cp in first:
            cp.start()
        passed = [copy(4 + j, (*chip, c), sibling) for j, chip in enumerate(chips)]
        for j, chip in enumerate(chips):
            copy(1 + j, (*chip, c), me).wait_recv()
            passed[j].start()
        copy(0, sibling, me).wait_recv()
        for j, chip in enumerate(chips):
            copy(4 + j, (*chip, 1 - c), me).wait_recv()
        for cp in first + passed:
            cp.wait_send()
        mine.wait()

    return pl.pallas_call(
        body, out_shape=jax.ShapeDtypeStruct((N_DEV * m_per, n), blk.dtype),
        in_specs=[pl.BlockSpec(memory_space=pltpu.VMEM)], out_specs=pl.BlockSpec(memory_space=pltpu.VMEM),
        scratch_shapes=[pltpu.SemaphoreType.DMA((7,)), pltpu.SemaphoreType.DMA((7,)), pltpu.SemaphoreType.DMA],
        name=name)(blk)


def _half(ref, chip, c):
    hr = ref.shape[1] // 2
    return ref.at[chip, pl.ds(c * hr, hr), :]


def _gather_copy(bufs, sends, recvs, b, k, chip, pc, to):
    piece = _half(bufs[b], 2 * chip[0] + chip[1], pc)
    return pltpu.make_async_remote_copy(src_ref=piece, dst_ref=piece, send_sem=sends.at[4 * b + k], recv_sem=recvs.at[4 * b + k],
                                        device_id=to, device_id_type=MESH)


def _split_call(body, bufs, sems_in, n_sems, *, name, after=(), token=False):
    nb, na, starts = len(bufs), len(after), not sems_in

    def wrapped(*refs):
        sems = refs[nb + na:nb + na + 2] if starts else refs[nb:nb + 2]
        body(refs[:nb], sems[0], sems[1])
        if token:
            refs[-1][...] = jnp.zeros_like(refs[-1])

    out_shape = [pltpu.SemaphoreType.DMA((n_sems,)) for _ in range(2 if starts else 0)]
    out_specs = [SEM] * len(out_shape) + [HBM] * nb
    alias = {b: len(out_shape) + b for b in range(nb)}
    out_shape += [pltpu.HBM(b.shape, b.dtype) for b in bufs]
    if token:
        out_shape.append(jax.ShapeDtypeStruct((8, LANES), F32))
        out_specs.append(pl.BlockSpec(memory_space=pltpu.VMEM))
    return pl.pallas_call(
        wrapped, out_shape=out_shape, in_specs=[HBM] * nb + [SEM] * len(sems_in) + [ANY] * na, out_specs=out_specs,
        input_output_aliases=alias,
        compiler_params=pltpu.CompilerParams(has_side_effects=pltpu.SideEffectType.DATAFLOW_SIDE_EFFECTING),
        name=name)(*[pltpu.with_memory_space_constraint(b, pltpu.HBM) for b in bufs], *sems_in, *after)


def _gather_start(bufs, *, name, after=()):
    nb = len(bufs)

    def body(ins, sends, recvs):
        x, y, c = lax.axis_index("x"), lax.axis_index("y"), lax.axis_index("c")
        chips = [(1 - x, y), (x, 1 - y), (1 - x, 1 - y)]
        for b in range(nb):
            _gather_copy(ins, sends, recvs, b, 0, (x, y), c, (x, y, 1 - c)).start()
            for j, chip in enumerate(chips):
                _gather_copy(ins, sends, recvs, b, 1 + j, (x, y), c, (*chip, c)).start()

    out = _split_call(body, bufs, (), 4 * nb, name=name, after=after, token=True)
    return (out[0], out[1], out[2:2 + nb]), out[-1]


def _gather_wait_first(flight, *, name, after=()):
    sends, recvs, bufs = flight
    nb = len(bufs)

    def body(ins, sends_, recvs_):
        x, y, c = lax.axis_index("x"), lax.axis_index("y"), lax.axis_index("c")
        chips = [(1 - x, y), (x, 1 - y), (1 - x, 1 - y)]
        for b in range(nb):
            _gather_copy(ins, sends_, recvs_, b, 0, (x, y), c, (x, y, 1 - c)).wait_send()
            _gather_copy(ins, sends_, recvs_, b, 0, (x, y), 1 - c, (x, y, c)).wait_recv()
            for j, chip in enumerate(chips):
                _gather_copy(ins, sends_, recvs_, b, 1 + j, (x, y), c, (*chip, c)).wait_send()
                _gather_copy(ins, sends_, recvs_, b, 1 + j, chip, c, (x, y, c)).wait_recv()

    return _split_call(body, bufs, (sends, recvs), 4 * nb, name=name, after=after)


def _gather_forward(bufs, *, name):
    nb = len(bufs)

    def body(ins, sends, recvs):
        x, y, c = lax.axis_index("x"), lax.axis_index("y"), lax.axis_index("c")
        chips = [(1 - x, y), (x, 1 - y), (1 - x, 1 - y)]
        for b in range(nb):
            for j, chip in enumerate(chips):
                _gather_copy(ins, sends, recvs, b, 1 + j, chip, c, (x, y, 1 - c)).start()

    out = _split_call(body, bufs, (), 4 * nb, name=name)
    return out[0], out[1], out[2:2 + nb]


def _gather_wait_forward(flight, *, name, after=()):
    sends, recvs, bufs = flight
    nb = len(bufs)

    def body(ins, sends_, recvs_):
        x, y, c = lax.axis_index("x"), lax.axis_index("y"), lax.axis_index("c")
        chips = [(1 - x, y), (x, 1 - y), (1 - x, 1 - y)]
        for b in range(nb):
            for j, chip in enumerate(chips):
                _gather_copy(ins, sends_, recvs_, b, 1 + j, chip, c, (x, y, 1 - c)).wait_send()
                _gather_copy(ins, sends_, recvs_, b, 1 + j, chip, 1 - c, (x, y, c)).wait_recv()

    return _split_call(body, bufs, (sends, recvs), 4 * nb, name=name, after=after)


def _owner_copies(hs, lands, sends, recvs):
    x, y, c = lax.axis_index("x"), lax.axis_index("y"), lax.axis_index("c")
    chips = [(1 - x, y), (x, 1 - y), (1 - x, 1 - y)]
    return [pltpu.make_async_remote_copy(src_ref=hs[b].at[2 * cx + cy], dst_ref=lands[b].at[j], send_sem=sends.at[3 * b + j],
                                         recv_sem=recvs.at[3 * b + j], device_id=(cx, cy, c), device_id_type=MESH)
            for b in range(len(hs)) for j, (cx, cy) in enumerate(chips)]


def _owners_start(hs, *, name):
    nb = len(hs)
    lands = [lax.empty((3,) + h.shape[1:], h.dtype) for h in hs]

    def body(refs, sends, recvs):
        for cp in _owner_copies(refs[:nb], refs[nb:], sends, recvs):
            cp.start()

    out = _split_call(body, list(hs) + lands, (), 3 * nb, name=name, token=True)
    return (out[0], out[1], out[2:2 + 2 * nb]), out[-1]


def _owners_wait(flight, *, name, after=()):
    sends, recvs, bufs = flight
    nb = len(bufs) // 2

    def body(refs, sends_, recvs_):
        for cp in _owner_copies(refs[:nb], refs[nb:], sends_, recvs_):
            cp.wait()

    return _split_call(body, bufs, (sends, recvs), 3 * nb, name=name, after=after)[nb:]


def _sibling_copies(gs, lands, sends, recvs):
    x, y, c = lax.axis_index("x"), lax.axis_index("y"), lax.axis_index("c")
    copies = []
    for b in range(len(gs)):
        hr = gs[b].shape[1] // 2
        copies.append(pltpu.make_async_remote_copy(
            src_ref=gs[b].at[:, pl.ds((1 - c) * hr, hr), :], dst_ref=lands[b], send_sem=sends.at[b], recv_sem=recvs.at[b],
            device_id=(x, y, 1 - c), device_id_type=MESH))
    return copies


def _sibling_start(gs, *, name, after=()):
    nb = len(gs)
    lands = [lax.empty((g.shape[0], g.shape[1] // 2, g.shape[2]), g.dtype) for g in gs]

    def body(refs, sends, recvs):
        for cp in _sibling_copies(refs[:nb], refs[nb:], sends, recvs):
            cp.start()

    out = _split_call(body, list(gs) + lands, (), nb, name=name, after=after, token=True)
    return (out[0], out[1], out[2:2 + 2 * nb]), out[-1]


def _sibling_wait(flight, *, name, after=()):
    sends, recvs, bufs = flight
    nb = len(bufs) // 2

    def body(refs, sends_, recvs_):
        for cp in _sibling_copies(refs[:nb], refs[nb:], sends_, recvs_):
            cp.wait()

    out = _split_call(body, bufs, (sends, recvs), nb, name=name, after=after)
    return out[:nb], out[nb:]


def _result_copies(ts, sends, recvs):
    x, y, c = lax.axis_index("x"), lax.axis_index("y"), lax.axis_index("c")
    return [pltpu.make_async_remote_copy(src_ref=ts[b].at[c], dst_ref=ts[b].at[c], send_sem=sends.at[b], recv_sem=recvs.at[b],
                                         device_id=(x, y, 1 - c), device_id_type=MESH) for b in range(len(ts))]


def _result_start(ts, *, name):
    def body(refs, sends, recvs):
        for cp in _result_copies(refs, sends, recvs):
            cp.start()

    out = _split_call(body, ts, (), len(ts), name=name, token=True)
    return (out[0], out[1], out[2:2 + len(ts)]), out[-1]


def _result_wait(flight, *, name, after=()):
    sends, recvs, bufs = flight

    def body(refs, sends_, recvs_):
        for cp in _result_copies(refs, sends_, recvs_):
            cp.wait()

    return _split_call(body, bufs, (sends, recvs), len(bufs), name=name, after=after)


def _row_tile(rows, cols):
    best = 16
    for t in range(16, rows + 1, 16):
        if rows % t == 0 and t * cols <= 640 * 1024:
            best = t
    assert rows % best == 0, (rows, cols)
    return best


def _add_sibling_half(g, recv, core, *, name):
    nk, r, n = g.shape
    hr = r // 2
    tr = _row_tile(hr, n)

    def body(c_ref, a_ref, b_ref, o_ref):
        o_ref[...] = (a_ref[...].astype(F32) + b_ref[...].astype(F32)).astype(BF16)

    grid_spec = pltpu.PrefetchScalarGridSpec(
        num_scalar_prefetch=1, grid=(nk, hr // tr),
        in_specs=[pl.BlockSpec((None, tr, n), lambda k, i, c_ref: (k, c_ref[0] * (hr // tr) + i, 0)),
                  pl.BlockSpec((None, tr, n), lambda k, i, c_ref: (k, i, 0))],
        out_specs=pl.BlockSpec((None, tr, n), lambda k, i, c_ref: (k, i, 0)))
    return pl.pallas_call(body, grid_spec=grid_spec, out_shape=jax.ShapeDtypeStruct((nk, hr, n), BF16),
                          compiler_params=_params(("parallel", "parallel")), name=name)(core, g, recv)


def _add_chip_sums(h, recv, chip_core, *, name):
    _, hr, n = h.shape
    tr = _row_tile(hr, n)

    def body(k_ref, a_ref, b_ref, o_ref):
        o_ref[...] = ((a_ref[...].astype(F32) + b_ref[0].astype(F32)) + b_ref[1].astype(F32)) + b_ref[2].astype(F32)

    grid_spec = pltpu.PrefetchScalarGridSpec(
        num_scalar_prefetch=1, grid=(hr // tr,),
        in_specs=[pl.BlockSpec((None, tr, n), lambda i, k_ref: (k_ref[0], i, 0)),
                  pl.BlockSpec((3, tr, n), lambda i, k_ref: (0, i, 0))],
        out_specs=pl.BlockSpec((None, tr, n), lambda i, k_ref: (k_ref[1], i, 0)))
    return pl.pallas_call(body, grid_spec=grid_spec, out_shape=jax.ShapeDtypeStruct((2, hr, n), F32),
                          compiler_params=_params(("parallel",)), name=name)(chip_core, h, recv)


def _sum_devices(g, *, name):
    nd, r, n = g.shape

    def body(g_ref, o_ref):
        acc = g_ref[0]
        for i in range(1, nd):
            acc = acc + g_ref[i]
        o_ref[...] = acc

    return pl.pallas_call(body, out_shape=jax.ShapeDtypeStruct((r, n), F32), name=name)(g)


def _own_slot(shard, chip):
    return lax.dynamic_update_slice(jnp.zeros((N_CHIPS,) + shard.shape, BF16), shard[None], (chip, 0, 0))


def kernel(x, c, ada_w, ada_b, mix_norm_w, mlp_norm_w, mlp_up, mlp_down, ssd_in_w, ssd_conv_w, ssd_conv_b, ssd_dt_bias, ssd_A_log, ssd_D, ssd_norm_w, ssd_out_w, sc_in_w, sc_conv_w, sc_out_w, final_norm_w, loss_target, m_ada_w, m_ada_b, m_mix_norm_w, m_mlp_norm_w, m_mlp_up, m_mlp_down, m_ssd_in_w, m_ssd_conv_w, m_ssd_conv_b, m_ssd_dt_bias, m_ssd_A_log, m_ssd_D, m_ssd_norm_w, m_ssd_out_w, m_sc_in_w, m_sc_conv_w, m_sc_out_w, m_final_norm_w, v_ada_w, v_ada_b, v_mix_norm_w, v_mlp_norm_w, v_mlp_up, v_mlp_down, v_ssd_in_w, v_ssd_conv_w, v_ssd_conv_b, v_ssd_dt_bias, v_ssd_A_log, v_ssd_D, v_ssd_norm_w, v_ssd_out_w, v_sc_in_w, v_sc_conv_w, v_sc_out_w, v_final_norm_w):
    xi, yi, ci = lax.axis_index("x"), lax.axis_index("y"), lax.axis_index("c")
    chip = 2 * xi + yi
    dev = 2 * chip + ci
    n_ada = ada_w.shape[2]

    conv_flat = jnp.concatenate([ssd_conv_w.reshape(-1), sc_conv_w.reshape(-1), jnp.zeros((256,), F32)]).reshape(4, D)
    blk0 = jnp.concatenate([c, conv_flat, jnp.zeros((3, D), F32)], axis=0)
    got0 = _all_gather_rows(blk0, name="gather_cond").reshape(N_DEV, 8, D)
    c_all = got0[:, 0]
    conv_all = got0[0::2, 1:5].reshape(N_CHIPS, 4 * D)
    ssd_conv = jnp.moveaxis(conv_all[:, :4 * 768].reshape(N_CHIPS, 4, 768), 0, 1).reshape(4, CONVD)
    sc_conv = jnp.moveaxis(conv_all[:, 4 * 768:4 * 768 + 3 * 256].reshape(N_CHIPS, 3, 256), 0, 1).reshape(3, D)
    mod_shard = [_matmul(c_all, ada_w[i], n=n_ada, a_silu=True,
                         extras=(lax.dynamic_slice(ada_b, (i, chip * n_ada), (1, n_ada)),),
                         epi=lambda acc, b: (acc + b,), name=f"ada_mod{i}") for i in range(2)]
    mod_all = _all_gather_rows(jnp.concatenate(mod_shard, axis=0), name="gather_mod")
    mod_all = mod_all.reshape(N_DEV, 2, N_DEV, n_ada)[0::2]
    mod = jnp.moveaxis(lax.dynamic_index_in_dim(mod_all, dev, axis=2, keepdims=False), 0, 1).reshape(2, 6, D)
    mods = [[mod[i, j:j + 1] for j in range(6)] for i in range(2)]

    bf = lambda v: v.astype(BF16)
    up_row, down_row, sc_out_row = 0, D, 2 * D
    a_bufs = [_own_slot(bf(ssd_in_w[0]), chip)]
    b_bufs = [_own_slot(bf(ssd_out_w[0]), chip), _own_slot(bf(jnp.concatenate([mlp_up[0], mlp_down[0]], axis=0)), chip)]
    c_bufs = [_own_slot(bf(sc_in_w[0]), chip), _own_slot(bf(jnp.concatenate([mlp_up[1], mlp_down[1], sc_out_w[0]], axis=0)), chip)]
    fly_a, tok = _gather_start(a_bufs, name="gather_a_start", after=(mod,))
    fly_b, tok = _gather_start(b_bufs, name="gather_b_start", after=(tok,))
    fly_c, tok = _gather_start(c_bufs, name="gather_c_start", after=(tok,))

    row = lambda v: v.reshape(1, -1)
    xs, tgt = x[0], loss_target[0]
    prm = jnp.pad(jnp.concatenate([ssd_dt_bias, ssd_A_log, ssd_D, jnp.zeros((5, NH), F32)], axis=0), ((0, 0), (0, LANES - NH)))
    mix_nw = [row(mix_norm_w[i]) for i in range(2)]
    mlp_nw = [row(mlp_norm_w[i]) for i in range(2)]
    a_bufs = _gather_wait_first(fly_a, name="gather_a_landed", after=(tok,))
    (w_ssd_in,) = _gather_wait_forward(_gather_forward(a_bufs, name="gather_a_pass"), name="gather_a_done")
    ssd_in_full = jnp.moveaxis(w_ssd_in, 0, 1).reshape(D, N_CHIPS * SSD_IN_SHARD)
    w_zx, w_dt = ssd_in_full[:, :ZX], jnp.pad(ssd_in_full[:, ZX:], ((0, 0), (0, LANES - NH)))
    scan = _ssd_fwd_scan(xs, mods[0][0:3], mix_nw[0], w_zx, w_dt, ssd_conv, ssd_conv_b, prm, "ssd")
    fly_b = _gather_forward(_gather_wait_first(fly_b, name="gather_b_landed", after=(scan[3],)), name="gather_b_pass")
    w_ssd_out, w_b = _gather_wait_forward(fly_b, name="gather_b_done", after=(scan[4],))
    x1, s_ssd = _ssd_fwd_out(xs, mods[0][0:3], scan, ssd_norm_w, w_ssd_out, "ssd")
    x2, s_mlp0 = _mlp_fwd(x1, mods[0][3:6], mlp_nw[0], w_b, up_row, down_row, "mlp0")
    c_bufs = _gather_wait_first(fly_c, name="gather_c_landed", after=(x2,))
    w_sc_in, w_c = _gather_wait_forward(_gather_forward(c_bufs, name="gather_c_pass"), name="gather_c_done")
    x3, s_sc = _sc_layer_fwd(x2, mods[1][0:3], mix_nw[1], w_sc_in, sc_conv, w_c, sc_out_row, "sc")
    x4, s_mlp1 = _mlp_fwd(x3, mods[1][3:6], mlp_nw[1], w_c, up_row, down_row, "mlp1")

    core = ci.reshape(1).astype(jnp.int32)
    chip_core = jnp.stack([chip, ci]).astype(jnp.int32)

    def reduce_swap(gbufs, tag, after=()):
        return _sibling_start(gbufs, name=tag + "_sibling_start", after=after)

    def reduce_send(flight, tag, after):
        gs, sib = _sibling_wait(flight, name=tag + "_sibling_landed", after=after)
        hs = [_add_sibling_half(g, s, core, name=f"{tag}_add_sibling{b}") for b, (g, s) in enumerate(zip(gs, sib))]
        return _owners_start(hs, name=tag + "_owners_start")

    def reduce_sum(flight, tag, after):
        nb = len(flight[2]) // 2
        lands = _owners_wait(flight, name=tag + "_owners_landed", after=after)
        ts = [_add_chip_sums(h, o, chip_core, name=f"{tag}_add_chips{b}") for b, (h, o) in enumerate(zip(flight[2][:nb], lands))]
        return _result_start(ts, name=tag + "_result_start")

    def reduce_done(flight, tag, after=()):
        return [t.reshape(-1, t.shape[2]) for t in _result_wait(flight, name=tag + "_result_landed", after=after)]

    dx4, fsum = _final_loss(x4, row(final_norm_w), tgt, name="final_loss")
    dx3, g_c, sum_mlp1 = _mlp_bwd(dx4, s_mlp1, mods[1][3:6], mlp_nw[1], w_c, None, up_row, down_row, "mlp1")
    dx2, g_c, g_sc_in, sum_sc, sc_csum = _sc_layer_bwd(dx3, s_sc, mods[1][0:3], mix_nw[1], w_sc_in, sc_conv, w_c, g_c,
                                                       sc_out_row, "sc")
    dx1, g_b, sum_mlp0 = _mlp_bwd(dx2, s_mlp0, mods[0][3:6], mlp_nw[0], w_b, None, up_row, down_row, "mlp0")
    dyn, g_ssd_out, gsum_ssd = _ssd_bwd_out(dx1, s_ssd, mods[0][0:3], w_ssd_out, "ssd")
    fly_1, tok = reduce_swap([g_c, g_sc_in, g_b, g_ssd_out], "rs1")
    dy, dzx, gnsum = _gnorm_bwd(s_ssd[5], s_ssd[2], ssd_norm_w + tok[0:1, 0:1], dyn, name="ssd_dgnorm")
    fly_1, tok = reduce_send(fly_1, "rs1", (dy,))
    grad_x, d_w_zx, d_w_dt, sum_ssd, csum, ssum = _ssd_bwd_rest(
        dx1, dy, dzx, gsum_ssd, s_ssd, mods[0][0:3], mix_nw[0], w_zx, w_dt, ssd_conv, ssd_conv_b, prm + tok[0:1, 0:1], "ssd")
    fly_1, tok = reduce_sum(fly_1, "rs1", (grad_x,))

    def ssd_in_owner(k):
        lo, hi = k * SSD_IN_SHARD, (k + 1) * SSD_IN_SHARD
        if hi <= ZX:
            return d_w_zx[:, lo:hi]
        return jnp.concatenate([d_w_zx[:, lo:], d_w_dt[:, :hi - ZX]], axis=1)

    small = jnp.concatenate([sum_ssd + tok[0:1, 0:1], sum_mlp0, sum_sc, sum_mlp1, csum.reshape(24, D), gnsum.reshape(16, D),
                             fsum, sc_csum, jnp.pad(ssum, ((0, 0), (0, D - LANES)))], axis=0)
    small_all = _all_gather_rows(small, name="gather_small").reshape(N_DEV, SMALL_ROWS, D)
    fly_2, tok = reduce_swap([jnp.stack([ssd_in_owner(k) for k in range(N_CHIPS)]).astype(BF16)], "rs2", (small_all,))
    fly_2, tok = reduce_send(fly_2, "rs2", (tok,))
    t_c, t_sc_in, t_b, t_ssd_out = reduce_done(fly_1, "rs1", (tok,))
    small_all = small_all + tok[0:1, 0:1]
    tot = _sum_devices(small_all, name="sum_small")
    loss = tot[FINAL_ROW + 1, 0]
    mod_rows = [r + o for r in SUB_ROW for o in (3, 2, 0)]
    g_ada_b = jnp.stack([tot[r] for r in mod_rows]).reshape(2, 6 * D)
    g_mix_norm = jnp.stack([tot[SUB_ROW[0] + 1], tot[SUB_ROW[2] + 1]])
    g_mlp_norm = jnp.stack([tot[SUB_ROW[1] + 1], tot[SUB_ROW[3] + 1]])
    conv_sums = tot[SSD_CONV_ROW:SSD_CONV_ROW + 24].reshape(8, CONVD)
    g_ssd_conv_w = lax.dynamic_slice(conv_sums, (0, chip * 768), (4, 768))[None]
    g_ssd_conv_b = conv_sums[4:5]
    g_ssd_norm = tot[GNORM_ROW:GNORM_ROW + 2].reshape(1, DI)
    g_final = tot[FINAL_ROW]
    g_sc_conv_w = lax.dynamic_slice(tot[SC_CONV_ROW:SC_CONV_ROW + 3], (0, chip * 256), (3, 256))[None]
    g_a_log, g_d, g_dt_bias = (tot[HEAD_ROW + r:HEAD_ROW + r + 1, 0:NH] for r in range(3))
    c_pad = jnp.concatenate([c_all, jnp.zeros((8, D), F32)], axis=0)
    dmod_all = jnp.stack([small_all[:, r] for r in mod_rows], axis=1).reshape(N_DEV, 2, 6 * D)
    g_ada_w = []
    for i in range(2):
        dm = lax.dynamic_slice(dmod_all[:, i], (0, chip * n_ada), (N_DEV, n_ada))
        g_ada_w.append(_matmul_tn(c_pad, jnp.concatenate([dm, jnp.zeros_like(dm)], axis=0), m=D, n=n_ada, a_silu=True,
                                  name=f"ada_dw{i}"))
    g_ada_w = jnp.stack(g_ada_w)

    big = dict(mlp_up=[(t_b, up_row), (t_c, up_row)], mlp_down=[(t_b, down_row), (t_c, down_row)],
               ssd_out_w=[(t_ssd_out, 0)], sc_out_w=[(t_c, sc_out_row)], sc_in_w=[(t_sc_in, 0)], ssd_in_w=None)
    grads = dict(ada_w=g_ada_w, ada_b=g_ada_b, mix_norm_w=g_mix_norm, mlp_norm_w=g_mlp_norm, ssd_conv_w=g_ssd_conv_w,
                 ssd_conv_b=g_ssd_conv_b, ssd_dt_bias=g_dt_bias, ssd_A_log=g_a_log, ssd_D=g_d, ssd_norm_w=g_ssd_norm,
                 sc_conv_w=g_sc_conv_w, final_norm_w=g_final)
    weights = dict(ada_w=(ada_w, m_ada_w, v_ada_w), ada_b=(ada_b, m_ada_b, v_ada_b),
                   mix_norm_w=(mix_norm_w, m_mix_norm_w, v_mix_norm_w), mlp_norm_w=(mlp_norm_w, m_mlp_norm_w, v_mlp_norm_w),
                   mlp_up=(mlp_up, m_mlp_up, v_mlp_up), mlp_down=(mlp_down, m_mlp_down, v_mlp_down),
                   ssd_in_w=(ssd_in_w, m_ssd_in_w, v_ssd_in_w), ssd_conv_w=(ssd_conv_w, m_ssd_conv_w, v_ssd_conv_w),
                   ssd_conv_b=(ssd_conv_b, m_ssd_conv_b, v_ssd_conv_b), ssd_dt_bias=(ssd_dt_bias, m_ssd_dt_bias, v_ssd_dt_bias),
                   ssd_A_log=(ssd_A_log, m_ssd_A_log, v_ssd_A_log), ssd_D=(ssd_D, m_ssd_D, v_ssd_D),
                   ssd_norm_w=(ssd_norm_w, m_ssd_norm_w, v_ssd_norm_w), ssd_out_w=(ssd_out_w, m_ssd_out_w, v_ssd_out_w),
                   sc_in_w=(sc_in_w, m_sc_in_w, v_sc_in_w), sc_conv_w=(sc_conv_w, m_sc_conv_w, v_sc_conv_w),
                   sc_out_w=(sc_out_w, m_sc_out_w, v_sc_out_w), final_norm_w=(final_norm_w, m_final_norm_w, v_final_norm_w))
    def step(nm, parts):
        w, m, v = (t.reshape(-1, t.shape[-1]) for t in weights[nm])
        rows, outs = w.shape[0] // len(parts), None
        for i, (gbuf, g_row) in enumerate(parts):
            outs = _adamw(w, gbuf, m, v, g_row=g_row, w_row=i * rows, rows=rows, into=outs, emit_g=True, name=f"adamw_{nm}{i}")
        return outs

    res = {}
    for nm, (w, m, v) in weights.items():
        two_d = (-1, w.shape[-1]) if w.ndim > 1 else (1, -1)
        if nm not in big:
            res[nm] = (grads[nm], *_adamw(w.reshape(two_d), grads[nm].reshape(two_d), m.reshape(two_d), v.reshape(two_d),
                                          name="adamw_" + nm))
        elif big[nm] is not None:
            res[nm] = step(nm, big[nm])
    fly_2, tok = reduce_sum(fly_2, "rs2", tuple(r[1] for r in res.values()))
    (t_ssd_in,) = reduce_done(fly_2, "rs2", (tok,))
    res["ssd_in_w"] = step("ssd_in_w", [(t_ssd_in, 0)])
    outs = [[res[nm][k].reshape(weights[nm][0].shape) for nm in weights] for k in range(4)]
    return (loss, grad_x[None], *outs[0], *outs[1], *outs[2], *outs[3])
```

```python
import jax
import jax.numpy as jnp
from jax import lax
from jax.experimental import pallas as pl
from jax.experimental.pallas import tpu as pltpu

F32 = jnp.float32
BF16 = jnp.bfloat16
MESH = pl.DeviceIdType.MESH

D = 1024
DFF = 4096
DI = 2048
NH = 32
HP = 64
NG = 4
NS = 128
CH = 128
CONVD = DI + 2 * NG * NS
ZX = DI + CONVD
GW = NG * NS
LANES = 128
N_CHIPS = 4
N_DEV = 8
EPS = 1e-5
ADAM_LR, ADAM_B1, ADAM_B2, ADAM_EPS, ADAM_WD, ADAM_STEP = 1e-3, 0.9, 0.999, 1e-8, 0.01, 10
VMEM_LIMIT = 48 * 1024 * 1024
TM_ALL = 2048
TM_HALF = 1024
ANY = pl.BlockSpec(memory_space=pl.ANY)
HBM = pl.BlockSpec(memory_space=pltpu.HBM)
SEM = pl.BlockSpec(memory_space=pltpu.SEMAPHORE)

SSD_IN_SHARD = 1288
SC_IN_SHARD = 768


def _params(sem=None):
    return pltpu.CompilerParams(dimension_semantics=sem, vmem_limit_bytes=VMEM_LIMIT)


def _sigmoid(v):
    return 1.0 / (1.0 + jnp.exp(-v))


def _dot(a, b, dims=((1,), (0,)), precision=None):
    return lax.dot_general(a, b, (dims, ((), ())), preferred_element_type=F32, precision=precision)


def _dot_nt(a, b):
    return _dot(a, b, ((1,), (1,)))


def _dot_tn(a, b):
    return _dot(a, b, ((0,), (0,)))


def _nn(av, bv):
    return _dot(av.astype(BF16), bv.astype(BF16))


def _nt(av, bv):
    return _dot_nt(av.astype(BF16), bv.astype(BF16))


def _nn_split(av, bv):
    return _dot(av.astype(BF16), bv.reshape(-1, bv.shape[2]))


def _nn_split_sq(av, bv):
    af = av.astype(F32)
    return _nn_split(af * af, bv)


def _nt_split(av, bv):
    kc = bv.shape[2]
    acc = _dot_nt(av[:, 0:kc].astype(BF16), bv[0])
    for s in range(1, bv.shape[0]):
        acc = acc + _dot_nt(av[:, s * kc:(s + 1) * kc].astype(BF16), bv[s])
    return acc


def _nt_sc_in(av, bv):
    q = 256
    acc = None
    for i in range(3 * D // q):
        a_blk = av[i // 4][:, (i % 4) * q:(i % 4 + 1) * q]
        b_blk = bv[i // 3][:, (i % 3) * q:(i % 3 + 1) * q]
        t = _dot_nt(a_blk, b_blk)
        acc = t if acc is None else acc + t
    return acc


def _matmul(a, b, *, name, n, contract=_nn, a_spec=None, b_spec=None, tm=512, tn=512, extras=(), epi=None,
            out_dtypes=(F32,), a_silu=False):
    M = a.shape[-2]
    tm, tn = min(tm, M), min(tn, n)
    assert M % tm == 0 and n % tn == 0, (name, M, n, tm, tn)
    n_ex = len(extras)
    if a_spec is None:
        a_spec = pl.BlockSpec((tm, a.shape[1]), lambda i, j: (i, 0))
    if b_spec is None:
        b_spec = (pl.BlockSpec((tn, b.shape[1]), lambda i, j: (j, 0)) if contract is _nt
                  else pl.BlockSpec((b.shape[0], tn), lambda i, j: (0, j)))

    def body(*refs):
        av = refs[0][...]
        if a_silu:
            av = av * _sigmoid(av)
        acc = contract(av, refs[1][...])
        res = epi(acc, *[r[...] for r in refs[2:2 + n_ex]]) if epi is not None else (acc,)
        for o_ref, r in zip(refs[2 + n_ex:], res, strict=True):
            o_ref[...] = r.astype(o_ref.dtype)

    in_specs = [a_spec, b_spec]
    for e in extras:
        in_specs.append(pl.BlockSpec((1, tn), lambda i, j: (0, j)) if e.shape[0] == 1 and M != 1
                        else pl.BlockSpec((tm, tn), lambda i, j: (i, j)))
    outs = pl.pallas_call(
        body, grid=(M // tm, n // tn), in_specs=in_specs,
        out_specs=[pl.BlockSpec((tm, tn), lambda i, j: (i, j)) for _ in out_dtypes],
        out_shape=[jax.ShapeDtypeStruct((M, n), dt) for dt in out_dtypes],
        compiler_params=_params(("parallel", "parallel")), name=name)(a, b, *extras)
    return outs if len(out_dtypes) > 1 else outs[0]


def _matmul_tn(a, b, *, name, m, n, tm=512, tn=512, a_spec=None, b_spec=None, out_spec=None, out_struct=None, into=None,
               a_silu=False, a_square=False):
    T = a.shape[-2]
    tm, tn = min(tm, m), min(tn, n)
    assert m % tm == 0 and n % tn == 0, (name, m, n, tm, tn)
    if a_spec is None:
        a_spec = pl.BlockSpec((T, tm), lambda i, j: (0, i))
    if b_spec is None:
        b_spec = pl.BlockSpec((T, tn), lambda i, j: (0, j))
    if out_spec is None:
        out_spec, out_struct = pl.BlockSpec((tm, tn), lambda i, j: (i, j)), jax.ShapeDtypeStruct((m, n), F32)

    def body(a_ref, b_ref, *rest):
        av = a_ref[...]
        if a_silu:
            av = av * _sigmoid(av)
        if a_square:
            av = av.astype(F32) * av.astype(F32)
        rest[-1][...] = _dot_tn(av.astype(BF16), b_ref[...].astype(BF16)).astype(rest[-1].dtype)

    args, in_specs, alias = [a, b], [a_spec, b_spec], {}
    if into is not None:
        args, in_specs, alias = args + [into], in_specs + [ANY], {2: 0}
    return pl.pallas_call(body, grid=(m // tm, n // tn), in_specs=in_specs, out_specs=out_spec, out_shape=out_struct,
                          input_output_aliases=alias, compiler_params=_params(("parallel", "parallel")), name=name)(*args)


def _modnorm_fwd(x, nw, sc, sh, *, name):
    L = x.shape[0]
    tm = min(L, 512)

    def body(x_ref, nw_ref, sc_ref, sh_ref, h_ref):
        xv = x_ref[...]
        r = lax.rsqrt(jnp.mean(xv * xv, axis=-1, keepdims=True) + EPS)
        h_ref[...] = ((xv * r * nw_ref[...]) * (1.0 + sc_ref[...]) + sh_ref[...]).astype(BF16)

    row = pl.BlockSpec((tm, D), lambda i: (i, 0))
    vec = pl.BlockSpec((1, D), lambda i: (0, 0))
    return pl.pallas_call(body, grid=(L // tm,), in_specs=[row, vec, vec, vec], out_specs=row,
                          out_shape=jax.ShapeDtypeStruct((L, D), BF16),
                          compiler_params=_params(("parallel",)), name=name)(x, nw, sc, sh)


def _modnorm_bwd(x, dh, dxo, nw, sc, gsum, *, name):
    L = x.shape[0]
    tm = min(L, 256)

    def body(x_ref, dh_ref, dxo_ref, nw_ref, sc_ref, g_ref, dx_ref, s_ref):
        @pl.when(pl.program_id(0) == 0)
        def _():
            s_ref[...] = g_ref[...]

        xv, dhv = x_ref[...], dh_ref[...]
        r = lax.rsqrt(jnp.mean(xv * xv, axis=-1, keepdims=True) + EPS)
        xhat = xv * r
        dxhat = dhv * (nw_ref[...] * (1.0 + sc_ref[...]))
        dx_ref[...] = dxo_ref[...] + r * (dxhat - xhat * jnp.mean(dxhat * xhat, axis=-1, keepdims=True))
        s_ref[1:2, :] += jnp.sum(dhv * xhat, axis=0, keepdims=True) * (1.0 + sc_ref[...])
        s_ref[2:3, :] += jnp.sum(dhv * xhat, axis=0, keepdims=True) * nw_ref[...]
        s_ref[3:4, :] += jnp.sum(dhv, axis=0, keepdims=True)

    row = pl.BlockSpec((tm, D), lambda i: (i, 0))
    vec = pl.BlockSpec((1, D), lambda i: (0, 0))
    blk = pl.BlockSpec((8, D), lambda i: (0, 0))
    return pl.pallas_call(body, grid=(L // tm,), in_specs=[row, row, row, vec, vec, blk], out_specs=[row, blk],
                          out_shape=[jax.ShapeDtypeStruct((L, D), F32), jax.ShapeDtypeStruct((8, D), F32)],
                          compiler_params=_params(("arbitrary",)), name=name)(x, dh, dxo, nw, sc, gsum)


def _gate_bwd(dxo, y, g, *, name):
    L = dxo.shape[0]
    tm = min(L, 512)

    def body(dxo_ref, y_ref, g_ref, dy_ref, s_ref):
        @pl.when(pl.program_id(0) == 0)
        def _():
            s_ref[...] = jnp.zeros_like(s_ref)

        dv = dxo_ref[...]
        dy_ref[...] = (dv * g_ref[...]).astype(BF16)
        s_ref[0:1, :] += jnp.sum(dv * y_ref[...], axis=0, keepdims=True)

    row = pl.BlockSpec((tm, D), lambda i: (i, 0))
    return pl.pallas_call(body, grid=(L // tm,), in_specs=[row, row, pl.BlockSpec((1, D), lambda i: (0, 0))],
                          out_specs=[row, pl.BlockSpec((8, D), lambda i: (0, 0))],
                          out_shape=[jax.ShapeDtypeStruct((L, D), BF16), jax.ShapeDtypeStruct((8, D), F32)],
                          compiler_params=_params(("arbitrary",)), name=name)(dxo, y, g)


def _final_loss(x, fw, tgt, *, name):
    L = x.shape[0]
    tm = min(L, 256)

    def body(x_ref, fw_ref, t_ref, dx_ref, s_ref):
        @pl.when(pl.program_id(0) == 0)
        def _():
            s_ref[...] = jnp.zeros_like(s_ref)

        xv = x_ref[...]
        r = lax.rsqrt(jnp.mean(xv * xv, axis=-1, keepdims=True) + EPS)
        xhat = xv * r
        diff = xhat * fw_ref[...] - t_ref[...]
        dout = diff * (1.0 / D)
        dxhat = dout * fw_ref[...]
        dx_ref[...] = r * (dxhat - xhat * jnp.mean(dxhat * xhat, axis=-1, keepdims=True))
        s_ref[0:1, :] += jnp.sum(dout * xhat, axis=0, keepdims=True)
        s_ref[1:2, :] += jnp.zeros((1, D), F32) + 0.5 * jnp.sum(jnp.sum(diff * diff, axis=-1, keepdims=True) * (1.0 / D))

    row = pl.BlockSpec((tm, D), lambda i: (i, 0))
    return pl.pallas_call(body, grid=(L // tm,), in_specs=[row, pl.BlockSpec((1, D), lambda i: (0, 0)), row],
                          out_specs=[row, pl.BlockSpec((8, D), lambda i: (0, 0))],
                          out_shape=[jax.ShapeDtypeStruct((L, D), F32), jax.ShapeDtypeStruct((8, D), F32)],
                          compiler_params=_params(("arbitrary",)), name=name)(x, fw, tgt)


def _shift_down(v, j):
    if j == 0:
        return v
    row = lax.broadcasted_iota(jnp.int32, v.shape, 0)
    return jnp.where(row >= j, pltpu.roll(v, j, 0), 0.0)


def _shift_up(v, j):
    if j == 0:
        return v
    n = v.shape[0]
    row = lax.broadcasted_iota(jnp.int32, v.shape, 0)
    return jnp.where(row < n - j, pltpu.roll(v, n - j, 0), 0.0)


def _ssd_conv_fwd(zx, w, b, *, name):
    L = zx.shape[0]
    cb = 256
    k = w.shape[0]

    def body(x_ref, w_ref, b_ref, o_ref):
        xv = x_ref[...].astype(F32)
        pre = b_ref[...] + xv * w_ref[k - 1:k, :]
        for j in range(1, k):
            pre = pre + _shift_down(xv, j) * w_ref[k - 1 - j:k - j, :]
        o_ref[...] = (pre * _sigmoid(pre)).astype(BF16)

    return pl.pallas_call(
        body, grid=(CONVD // cb,),
        in_specs=[pl.BlockSpec((L, cb), lambda i: (0, i + DI // cb)), pl.BlockSpec((k, cb), lambda i: (0, i)),
                  pl.BlockSpec((1, cb), lambda i: (0, i))],
        out_specs=pl.BlockSpec((L, cb), lambda i: (0, i)), out_shape=jax.ShapeDtypeStruct((L, CONVD), BF16),
        compiler_params=_params(("parallel",)), name=name)(zx, w, b)


def _ssd_conv_bwd(zx, dact, w, b, dzx, *, name):
    L = zx.shape[0]
    cb = 256
    k = w.shape[0]

    def body(x_ref, da_ref, w_ref, b_ref, _, dx_ref, s_ref):
        xv = x_ref[...].astype(F32)
        sh = [_shift_down(xv, j) for j in range(k)]
        pre = b_ref[...] + sh[0] * w_ref[k - 1:k, :]
        for j in range(1, k):
            pre = pre + sh[j] * w_ref[k - 1 - j:k - j, :]
        s = _sigmoid(pre)
        dpre = da_ref[...].astype(F32) * (s * (1.0 + pre * (1.0 - s)))
        dx = dpre * w_ref[k - 1:k, :]
        for j in range(1, k):
            dx = dx + _shift_up(dpre, j) * w_ref[k - 1 - j:k - j, :]
        dx_ref[...] = dx.astype(BF16)
        s_ref[...] = jnp.zeros_like(s_ref)
        for j in range(k):
            s_ref[k - 1 - j:k - j, :] = jnp.sum(dpre * sh[j], axis=0, keepdims=True)
        s_ref[k:k + 1, :] = jnp.sum(dpre, axis=0, keepdims=True)

    return pl.pallas_call(
        body, grid=(CONVD // cb,),
        in_specs=[pl.BlockSpec((L, cb), lambda i: (0, i + DI // cb)), pl.BlockSpec((L, cb), lambda i: (0, i)),
                  pl.BlockSpec((k, cb), lambda i: (0, i)), pl.BlockSpec((1, cb), lambda i: (0, i)), ANY],
        out_specs=[pl.BlockSpec((L, cb), lambda i: (0, i + DI // cb)), pl.BlockSpec((8, cb), lambda i: (0, i))],
        out_shape=[jax.ShapeDtypeStruct((L, ZX), BF16), jax.ShapeDtypeStruct((8, CONVD), F32)],
        input_output_aliases={4: 0}, compiler_params=_params(("parallel",)), name=name)(zx, dact, w, b, dzx)


def _sc_fwd(proj, w, *, name):
    L = proj.shape[0]
    cb = 256
    nb = D // cb
    k = w.shape[0]

    def body(b_ref, c_ref, x_ref, w_ref, o_ref):
        u = c_ref[...] * x_ref[...]
        v = u * w_ref[k - 1:k, :]
        for j in range(1, k):
            v = v + _shift_down(u, j) * w_ref[k - 1 - j:k - j, :]
        o_ref[...] = (b_ref[...] * v).astype(BF16)

    return pl.pallas_call(
        body, grid=(nb,),
        in_specs=[pl.BlockSpec((L, cb), lambda i: (0, i)), pl.BlockSpec((L, cb), lambda i: (0, i + nb)),
                  pl.BlockSpec((L, cb), lambda i: (0, i + 2 * nb)), pl.BlockSpec((k, cb), lambda i: (0, i))],
        out_specs=pl.BlockSpec((L, cb), lambda i: (0, i)), out_shape=jax.ShapeDtypeStruct((L, D), BF16),
        compiler_params=_params(("parallel",)), name=name)(proj, proj, proj, w)


def _sc_bwd(proj, dyv, w, *, name):
    L = proj.shape[0]
    cb = 256
    nb = D // cb
    k = w.shape[0]

    def body(b_ref, c_ref, x_ref, dy_ref, w_ref, dp_ref, s_ref):
        cv, xv = c_ref[...], x_ref[...]
        u = cv * xv
        sh = [_shift_down(u, j) for j in range(k)]
        v = sh[0] * w_ref[k - 1:k, :]
        for j in range(1, k):
            v = v + sh[j] * w_ref[k - 1 - j:k - j, :]
        dyv_ = dy_ref[...]
        dp_ref[0] = (dyv_ * v).astype(BF16)
        dv = dyv_ * b_ref[...]
        du = dv * w_ref[k - 1:k, :]
        for j in range(1, k):
            du = du + _shift_up(dv, j) * w_ref[k - 1 - j:k - j, :]
        dp_ref[1] = (du * xv).astype(BF16)
        dp_ref[2] = (du * cv).astype(BF16)
        s_ref[...] = jnp.zeros_like(s_ref)
        for j in range(k):
            s_ref[k - 1 - j:k - j, :] = jnp.sum(dv * sh[j], axis=0, keepdims=True)

    blk = pl.BlockSpec((L, cb), lambda i: (0, i))
    return pl.pallas_call(
        body, grid=(nb,),
        in_specs=[blk, pl.BlockSpec((L, cb), lambda i: (0, i + nb)), pl.BlockSpec((L, cb), lambda i: (0, i + 2 * nb)),
                  blk, pl.BlockSpec((k, cb), lambda i: (0, i))],
        out_specs=[pl.BlockSpec((3, L, cb), lambda i: (0, 0, i)), pl.BlockSpec((8, cb), lambda i: (0, i))],
        out_shape=[jax.ShapeDtypeStruct((3, L, D), BF16), jax.ShapeDtypeStruct((8, D), F32)],
        compiler_params=_params(("parallel",)), name=name)(proj, proj, proj, dyv, w)


def _pieces(v, n):
    out, rest = [], v
    for _ in range(n):
        out.append(rest.astype(BF16))
        rest = rest - out[-1].astype(F32)
    return out


def _cumsum_rows(mask, v):
    m = mask.astype(BF16)
    return _dot(jnp.concatenate([m, m, m], axis=1), jnp.concatenate(_pieces(v, 3), axis=0))


def _ssd_chunk_terms(dtr, prm):
    lane = lax.broadcasted_iota(jnp.int32, (CH, LANES), 1)
    valid = lane < NH
    xdt = dtr + prm[0:1, :]
    dt = jnp.where(valid, jnp.maximum(xdt, 0.0) + jnp.log1p(jnp.exp(-jnp.abs(xdt))), 0.0)
    A = -jnp.exp(prm[1:2, :])
    ri = lax.broadcasted_iota(jnp.int32, (CH, CH), 0)
    ci = lax.broadcasted_iota(jnp.int32, (CH, CH), 1)
    cs = _cumsum_rows(ri >= ci, dt * A)
    last = cs[CH - 1:CH, :]
    spread = (lax.broadcasted_iota(jnp.int32, (2 * LANES, DI), 1) // HP
              == lax.broadcasted_iota(jnp.int32, (2 * LANES, DI), 0) % LANES).astype(BF16)
    gather = ((lax.broadcasted_iota(jnp.int32, (LANES, 2 * DI), 1) % DI) // HP
              == lax.broadcasted_iota(jnp.int32, (LANES, 2 * DI), 0)).astype(BF16)
    return dict(valid=valid, xdt=xdt, dt=dt, A=A, cs=cs, csT=cs.T, last=last, ri=ri, ci=ci, ex=(spread, gather))


def _expand(v, ex):
    if v.shape[0] == 1:
        return _expand(jnp.broadcast_to(v, (8, LANES)), ex)[0:1, :]
    return _dot(jnp.concatenate(_pieces(v, 2), axis=1), ex[0])


def _head_sum(v, ex):
    if v.shape[0] == 1:
        return _head_sum(jnp.broadcast_to(v, (8, DI)), ex)[0:1, :]
    return _dot_nt(jnp.concatenate(_pieces(v, 2), axis=1), ex[1])


def _ssd_fwd(xbc, dtr, prm, *, name):
    L = xbc.shape[0]
    nc = L // CH

    def body(xbc_ref, dtr_ref, prm_ref, y_ref, sp_ref, st_ref):
        @pl.when(pl.program_id(0) == 0)
        def _():
            st_ref[...] = jnp.zeros_like(st_ref)

        prm_v = prm_ref[...]
        t = _ssd_chunk_terms(dtr_ref[...], prm_v)
        cs, csT, ex, causal = t["cs"], t["csT"], t["ex"], t["ri"] >= t["ci"]
        xs = xbc_ref[:, 0:DI].astype(F32)
        X = xs * _expand(t["dt"], ex)
        Xb = X.astype(BF16)
        Xd = (X * _expand(jnp.exp(t["last"] - cs), ex)).astype(BF16)
        Ex = _expand(jnp.exp(cs), ex)
        cdx = _expand(jnp.exp(t["last"]), ex)
        dskx = _expand(prm_v[2:3, :], ex)
        lane = lax.broadcasted_iota(jnp.int32, (CH, LANES), 1)
        sp_ref[0] = st_ref[...]
        for g in range(NG):
            Bg = xbc_ref[:, DI + g * NS:DI + (g + 1) * NS].astype(BF16)
            Cg = xbc_ref[:, DI + GW + g * NS:DI + GW + (g + 1) * NS].astype(BF16)
            G = _dot_nt(Cg, Bg)
            Sg = st_ref[:, g * GW:(g + 1) * GW]
            yoff = _dot(Cg, Sg.astype(BF16)) * Ex[:, g * GW:(g + 1) * GW]
            for j in range(GW // LANES):
                lo = g * GW + j * LANES
                Xp = Xb[:, lo:lo + LANES]
                yd = []
                for h in (lo // HP, lo // HP + 1):
                    seg = cs[:, h:h + 1] - csT[h:h + 1, :]
                    yd.append(_dot((G * jnp.where(causal, jnp.exp(seg), 0.0)).astype(BF16), Xp))
                y_ref[:, lo:lo + LANES] = (jnp.where(lane < HP, yd[0], yd[1]) + yoff[:, j * LANES:(j + 1) * LANES]
                                           + dskx[:, lo:lo + LANES] * xs[:, lo:lo + LANES]).astype(BF16)
            st_ref[:, g * GW:(g + 1) * GW] = Sg * cdx[:, g * GW:(g + 1) * GW] + _dot_tn(Bg, Xd[:, g * GW:(g + 1) * GW])

    return pl.pallas_call(
        body, grid=(nc,),
        in_specs=[pl.BlockSpec((CH, CONVD), lambda c: (c, 0)), pl.BlockSpec((CH, LANES), lambda c: (c, 0)),
                  pl.BlockSpec((8, LANES), lambda c: (0, 0))],
        out_specs=[pl.BlockSpec((CH, DI), lambda c: (c, 0)), pl.BlockSpec((1, NS, DI), lambda c: (c, 0, 0))],
        out_shape=[jax.ShapeDtypeStruct((L, DI), BF16), jax.ShapeDtypeStruct((nc, NS, DI), F32)],
        scratch_shapes=[pltpu.VMEM((NS, DI), F32)],
        compiler_params=_params(("arbitrary",)), name=name)(xbc, dtr, prm)


def _ssd_bwd(xbc, dtr, prm, dy, sprev, *, name):
    L = xbc.shape[0]
    nc = L // CH

    def body(xbc_ref, dtr_ref, prm_ref, dy_ref, sp_ref, dxbc_ref, ddtr_ref, s_ref, dst_ref, dx_scr, de_scr, dd_scr):
        step = pl.program_id(0)

        @pl.when(step == 0)
        def _():
            dst_ref[...] = jnp.zeros_like(dst_ref)
            s_ref[...] = jnp.zeros_like(s_ref)

        prm_v = prm_ref[...]
        t = _ssd_chunk_terms(dtr_ref[...], prm_v)
        cs, csT, ex, ri, ci = t["cs"], t["csT"], t["ex"], t["ri"], t["ci"]
        E = jnp.exp(cs)
        dec = jnp.exp(t["last"] - cs)
        cd = jnp.exp(t["last"])
        xs = xbc_ref[:, 0:DI].astype(F32)
        dtx = _expand(t["dt"], ex)
        X = xs * dtx
        Xb = X.astype(BF16)
        decx = _expand(dec, ex)
        Xd = (X * decx).astype(BF16)
        Ex = _expand(E, ex)
        cdx = _expand(cd, ex)
        dskx = _expand(prm_v[2:3, :], ex)
        lane = lax.broadcasted_iota(jnp.int32, (CH, LANES), 1)
        dcs = jnp.zeros((CH, LANES), F32)
        dcd_x = []
        for g in range(NG):
            gs = slice(g * GW, (g + 1) * GW)
            Bg = xbc_ref[:, DI + g * NS:DI + (g + 1) * NS].astype(BF16)
            Cg = xbc_ref[:, DI + GW + g * NS:DI + GW + (g + 1) * NS].astype(BF16)
            G = _dot_nt(Cg, Bg)
            GT = _dot_nt(Bg, Cg)
            Sg = sp_ref[0, :, gs]
            Sgb = Sg.astype(BF16)
            dyg = dy_ref[:, gs]
            de_scr[:, gs] = dyg * _dot(Cg, Sgb)
            dYo = (Ex[:, gs] * dyg).astype(BF16)
            dC = _dot_nt(dYo, Sgb)
            dS_in = _dot_tn(Cg, dYo)
            dStg = dst_ref[:, gs]
            dStb = dStg.astype(BF16)
            dXd = _dot(Bg, dStb)
            dB = _dot_nt(Xd[:, gs], dStb)
            dd_scr[:, gs] = dXd * X[:, gs]
            dXst = dXd * decx[:, gs]
            dG = jnp.zeros((CH, CH), F32)
            dGT = jnp.zeros((CH, CH), F32)
            for j in range(GW // LANES):
                lo = g * GW + j * LANES
                Xp = Xb[:, lo:lo + LANES]
                dyp = dy_ref[:, lo:lo + LANES]
                dXp = dXst[:, j * LANES:(j + 1) * LANES]
                for k, h in enumerate((lo // HP, lo // HP + 1)):
                    dyh = jnp.where((lane < HP) if k == 0 else (lane >= HP), dyp, 0.0).astype(BF16)
                    seg = cs[:, h:h + 1] - csT[h:h + 1, :]
                    Lm = jnp.where(ri >= ci, jnp.exp(seg), 0.0)
                    LmT = jnp.where(ci >= ri, jnp.exp(-seg), 0.0)
                    dM = _dot_nt(dyh, Xp)
                    dMT = _dot_nt(Xp, dyh)
                    MT = GT * LmT
                    rs = jnp.sum(dM * (G * Lm), axis=1, keepdims=True) - jnp.sum(dMT * MT, axis=1, keepdims=True)
                    dcs = dcs + jnp.where(lane == h, rs, 0.0)
                    dG = dG + dM * Lm
                    dGT = dGT + dMT * LmT
                    dXp = dXp + _dot(MT.astype(BF16), dyh)
                dx_scr[:, lo:lo + LANES] = dXp
            dxbc_ref[:, DI + g * NS:DI + (g + 1) * NS] = (dB + _dot(dGT.astype(BF16), Cg)).astype(BF16)
            dxbc_ref[:, DI + GW + g * NS:DI + GW + (g + 1) * NS] = (dC + _dot(dG.astype(BF16), Bg)).astype(BF16)
            dcd_x.append(jnp.sum(dStg * Sg, axis=0, keepdims=True))
            dst_ref[:, gs] = dStg * cdx[:, gs] + dS_in
        dX = dx_scr[...]
        dy = dy_ref[...]
        ddec = _head_sum(dd_scr[...], ex)
        dcd = _head_sum(jnp.concatenate(dcd_x, axis=1), ex)
        dcs = dcs + _head_sum(de_scr[...], ex) * E - ddec * dec
        row = lax.broadcasted_iota(jnp.int32, (CH, LANES), 0)
        dcs = dcs + jnp.where(row == CH - 1, jnp.sum(ddec * dec, axis=0, keepdims=True) + dcd * cd, 0.0)
        da = _cumsum_rows(ci >= ri, dcs)
        ddt = da * t["A"] + _head_sum(dX * xs, ex)
        ddtr = jnp.where(t["valid"], ddt * _sigmoid(t["xdt"]), 0.0)
        ddtr_ref[...] = ddtr
        dxbc_ref[:, 0:DI] = (dX * dtx + dskx * dy).astype(BF16)
        s_ref[0:1, :] += jnp.sum(da * t["dt"], axis=0, keepdims=True)
        s_ref[1:2, :] += _head_sum(jnp.sum(dy * xs, axis=0, keepdims=True), ex)
        s_ref[2:3, :] += jnp.sum(ddtr, axis=0, keepdims=True)

        @pl.when(step == nc - 1)
        def _():
            s_ref[0:1, :] = s_ref[0:1, :] * t["A"]

    rev = lambda c: (nc - 1 - c, 0)
    return pl.pallas_call(
        body, grid=(nc,),
        in_specs=[pl.BlockSpec((CH, CONVD), rev), pl.BlockSpec((CH, LANES), rev), pl.BlockSpec((8, LANES), lambda c: (0, 0)),
                  pl.BlockSpec((CH, DI), rev), pl.BlockSpec((1, NS, DI), lambda c: (nc - 1 - c, 0, 0))],
        out_specs=[pl.BlockSpec((CH, CONVD), rev), pl.BlockSpec((CH, LANES), rev), pl.BlockSpec((8, LANES), lambda c: (0, 0))],
        out_shape=[jax.ShapeDtypeStruct((L, CONVD), BF16), jax.ShapeDtypeStruct((L, LANES), F32),
                   jax.ShapeDtypeStruct((8, LANES), F32)],
        scratch_shapes=[pltpu.VMEM((NS, DI), F32), pltpu.VMEM((CH, DI), F32), pltpu.VMEM((CH, DI), F32),
                        pltpu.VMEM((CH, DI), F32)],
        compiler_params=_params(("arbitrary",)), name=name)(xbc, dtr, prm, dy, sprev)


def _gnorm_fwd(y, zx, nw, *, name):
    L = y.shape[0]
    tm = min(L, 256)

    def body(y_ref, z_ref, nw_ref, o_ref):
        z = z_ref[...].astype(F32)
        yg = y_ref[...].astype(F32) * (z * _sigmoid(z))
        for g in range(NG):
            v = yg[:, g * GW:(g + 1) * GW]
            r = lax.rsqrt(jnp.mean(v * v, axis=-1, keepdims=True) + EPS)
            o_ref[:, g * GW:(g + 1) * GW] = (v * r * nw_ref[:, g * GW:(g + 1) * GW]).astype(BF16)

    row = pl.BlockSpec((tm, DI), lambda i: (i, 0))
    return pl.pallas_call(body, grid=(L // tm,), in_specs=[row, row, pl.BlockSpec((1, DI), lambda i: (0, 0))],
                          out_specs=row, out_shape=jax.ShapeDtypeStruct((L, DI), BF16),
                          compiler_params=_params(("parallel",)), name=name)(y, zx, nw)


def _gnorm_bwd(y, zx, nw, dyn, *, name):
    L = y.shape[0]
    tm = min(L, 256)

    def body(y_ref, z_ref, nw_ref, dyn_ref, dy_ref, dz_ref, s_ref):
        @pl.when(pl.program_id(0) == 0)
        def _():
            s_ref[...] = jnp.zeros_like(s_ref)

        z, yv = z_ref[...].astype(F32), y_ref[...].astype(F32)
        sz = _sigmoid(z)
        gate = z * sz
        dgate_dz = sz * (1.0 + z * (1.0 - sz))
        for g in range(NG):
            gs = slice(g * GW, (g + 1) * GW)
            v = yv[:, gs] * gate[:, gs]
            r = lax.rsqrt(jnp.mean(v * v, axis=-1, keepdims=True) + EPS)
            vhat = v * r
            dn = dyn_ref[:, gs]
            s_ref[0:1, gs] += jnp.sum(dn * vhat, axis=0, keepdims=True)
            dvhat = dn * nw_ref[:, gs]
            dv = r * (dvhat - vhat * jnp.mean(dvhat * vhat, axis=-1, keepdims=True))
            dy_ref[:, gs] = dv * gate[:, gs]
            dz_ref[:, gs] = (dv * yv[:, gs] * dgate_dz[:, gs]).astype(BF16)

    row = pl.BlockSpec((tm, DI), lambda i: (i, 0))
    return pl.pallas_call(body, grid=(L // tm,), in_specs=[row, row, pl.BlockSpec((1, DI), lambda i: (0, 0)), row],
                          out_specs=[row, row, pl.BlockSpec((8, DI), lambda i: (0, 0))],
                          out_shape=[jax.ShapeDtypeStruct((L, DI), F32), jax.ShapeDtypeStruct((L, ZX), BF16),
                                     jax.ShapeDtypeStruct((8, DI), F32)],
                          compiler_params=_params(("arbitrary",)), name=name)(y, zx, nw, dyn)


def _adamw(w, g, m, v, *, name, g_row=0, w_row=0, rows=None, into=None, emit_g=False):
    lead = w.ndim == 3
    R, C = w.shape[-2:]
    rows = R if rows is None else rows
    tr = rows
    while tr * C > 256 * 1024 and tr % 16 == 0:
        tr //= 2
    assert g_row % tr == 0 and w_row % tr == 0, (name, g_row, w_row, tr)
    n_out = 4 if emit_g else 3

    def body(w_ref, g_ref, m_ref, v_ref, *rest):
        outs = rest[-n_out:]
        gv = g_ref[...]
        mn = ADAM_B1 * m_ref[...] + (1.0 - ADAM_B1) * gv
        vn = ADAM_B2 * v_ref[...] + (1.0 - ADAM_B2) * (gv * gv)
        m_hat = mn / (1.0 - ADAM_B1 ** ADAM_STEP)
        v_hat = vn / (1.0 - ADAM_B2 ** ADAM_STEP)
        d_ref, mo_ref, vo_ref = outs[-3:]
        d_ref[...] = -ADAM_LR * (m_hat / (jnp.sqrt(v_hat) + ADAM_EPS) + ADAM_WD * w_ref[...])
        mo_ref[...] = mn
        vo_ref[...] = vn
        if emit_g:
            outs[0][...] = gv

    blk = (pl.BlockSpec((None, tr, C), lambda i: (0, i + w_row // tr, 0)) if lead
           else pl.BlockSpec((tr, C), lambda i: (i + w_row // tr, 0)))
    args, in_specs, alias = [w, g, m, v], [blk, pl.BlockSpec((tr, C), lambda i: (i + g_row // tr, 0)), blk, blk], {}
    if into is not None:
        args, in_specs, alias = args + list(into), in_specs + [ANY] * n_out, {4 + k: k for k in range(n_out)}
    return pl.pallas_call(body, grid=(rows // tr,), in_specs=in_specs, out_specs=[blk] * n_out,
                          out_shape=[jax.ShapeDtypeStruct(w.shape, F32)] * n_out, input_output_aliases=alias,
                          compiler_params=_params(("parallel",)), name=name)(*args)


def _residual(acc, xv, gv):
    return xv + gv * acc, acc


def _like(buf):
    return jax.ShapeDtypeStruct(buf.shape, buf.dtype)


def _mlp_fwd(x, mod, nw, wb, up_row, down_row, tag):
    sh, sc, g = mod
    h = _modnorm_fwd(x, nw, sc, sh, name=tag + "_norm")
    a = _matmul(h, wb, n=DFF, tm=TM_ALL, b_spec=pl.BlockSpec((None, D, 512), lambda mi, j: (j // 2, up_row // D, j % 2)),
                epi=lambda acc: (jnp.maximum(acc, 0.0),), out_dtypes=(BF16,), name=tag + "_up")
    xn, y = _matmul(a, wb, n=D, tm=TM_HALF, contract=_nn_split_sq,
                    b_spec=pl.BlockSpec((N_CHIPS, D, 512), lambda mi, j: (0, down_row // D, j)),
                    extras=(x, g), epi=_residual, out_dtypes=(F32, F32), name=tag + "_down")
    return xn, (x, h, a, y)


def _mlp_bwd(dxo, saved, mod, nw, wb, gb, up_row, down_row, tag):
    x, h, a, y = saved
    sh, sc, g = mod
    dy, gsum = _gate_bwd(dxo, y, g, name=tag + "_dgate")
    du = _matmul(dy, wb, n=DFF, tm=TM_ALL, contract=_nt,
                 b_spec=pl.BlockSpec((None, 512, D), lambda mi, j: (j // 2, down_row // 512 + j % 2, 0)),
                 extras=(a,), epi=lambda acc, av: (acc * (2.0 * av.astype(F32)),), out_dtypes=(BF16,), name=tag + "_dact")
    gb = _matmul_tn(a, dy, m=DFF, n=D, tm=D, tn=D, a_square=True, into=gb, out_struct=_like(wb),
                    out_spec=pl.BlockSpec((None, D, D), lambda mi, j: (mi, down_row // D, 0)), name=tag + "_ddown")
    dh = _matmul(du, wb, n=D, tm=TM_HALF, contract=_nt_split,
                 b_spec=pl.BlockSpec((N_CHIPS, 512, D), lambda mi, j: (0, up_row // 512 + j, 0)), name=tag + "_dh")
    gb = _matmul_tn(h, du, m=D, n=DFF, tm=D, into=gb, out_struct=_like(wb),
                    out_spec=pl.BlockSpec((None, D, 512), lambda mi, j: (j // 2, up_row // D, j % 2)), name=tag + "_dup")
    dx, sums = _modnorm_bwd(x, dh, dxo, nw, sc, gsum, name=tag + "_dnorm")
    return dx, gb, sums


def _ssd_fwd_scan(x, mod, nw, w_zx, w_dt, conv_w, conv_b, prm, tag):
    sh, sc, g = mod
    h = _modnorm_fwd(x, nw, sc, sh, name=tag + "_norm")
    zx = _matmul(h, w_zx, n=ZX, tm=TM_ALL, out_dtypes=(BF16,), name=tag + "_in")
    dtr = _matmul(h, w_dt, n=LANES, tm=TM_ALL, name=tag + "_in_dt")
    xbc = _ssd_conv_fwd(zx, conv_w, conv_b, name=tag + "_conv")
    y, sprev = _ssd_fwd(xbc, dtr, prm, name=tag + "_scan")
    return h, zx, dtr, xbc, y, sprev


def _ssd_fwd_out(x, mod, scan, gn_w, w_out, tag):
    sh, sc, g = mod
    h, zx, dtr, xbc, y, sprev = scan
    yn = _gnorm_fwd(y, zx, gn_w, name=tag + "_gnorm")
    xn, yo = _matmul(yn, w_out, n=D, tm=TM_HALF, contract=_nn_split,
                     b_spec=pl.BlockSpec((N_CHIPS, 512, 512), lambda mi, j: (0, 0, j)),
                     extras=(x, g), epi=_residual, out_dtypes=(F32, F32), name=tag + "_out")
    return xn, (x, h, zx, dtr, xbc, y, sprev, yn, yo)


def _ssd_bwd_out(dxo, saved, mod, w_out, tag):
    x, h, zx, dtr, xbc, y, sprev, yn, yo = saved
    sh, sc, g = mod
    dyo, gsum = _gate_bwd(dxo, yo, g, name=tag + "_dgate")
    dyn = _matmul(dyo, w_out, n=DI, tm=TM_ALL, contract=_nt, b_spec=pl.BlockSpec((None, 512, D), lambda mi, j: (j, 0, 0)),
                  name=tag + "_dyn")
    g_out = _matmul_tn(yn, dyo, m=DI, n=D, tn=D, out_struct=_like(w_out),
                       out_spec=pl.BlockSpec((None, 512, D), lambda mi, j: (mi, 0, 0)), name=tag + "_dout")
    return dyn, g_out, gsum


def _ssd_bwd_rest(dxo, dy, dzx, gsum, saved, mod, nw, w_zx, w_dt, conv_w, conv_b, prm, tag):
    x, h, zx, dtr, xbc, y, sprev, yn, yo = saved
    sh, sc, g = mod
    dxbc, ddtr, ssum = _ssd_bwd(xbc, dtr, prm, dy, sprev, name=tag + "_dscan")
    dzx, csum = _ssd_conv_bwd(zx, dxbc, conv_w, conv_b, dzx, name=tag + "_dconv")
    dh_dt = _matmul(ddtr, w_dt, n=D, tm=TM_ALL, contract=_nt, name=tag + "_dh_dt")
    dh = _matmul(dzx, w_zx, n=D, tm=TM_HALF, contract=_nt, extras=(dh_dt,), epi=lambda acc, e: (acc + e,), name=tag + "_dh")
    d_w_zx = _matmul_tn(h, dzx, m=D, n=ZX, tm=D, name=tag + "_din")
    d_w_dt = _matmul_tn(h, ddtr, m=D, n=LANES, tm=D, name=tag + "_din_dt")
    dx, sums = _modnorm_bwd(x, dh, dxo, nw, sc, gsum, name=tag + "_dnorm")
    return dx, d_w_zx, d_w_dt, sums, csum, ssum


def _sc_layer_fwd(x, mod, nw, w_sc_in, conv_w, wb, out_row, tag):
    sh, sc, g = mod
    h = _modnorm_fwd(x, nw, sc, sh, name=tag + "_norm")
    proj = _matmul(h, w_sc_in, n=3 * D, tm=TM_ALL, tn=256, b_spec=pl.BlockSpec((None, D, 256), lambda mi, j: (j // 3, 0, j % 3)),
                   name=tag + "_in")
    yv = _sc_fwd(proj, conv_w, name=tag + "_conv")
    xn, yo = _matmul(yv, wb, n=D, tm=TM_HALF, contract=_nn_split,
                     b_spec=pl.BlockSpec((N_CHIPS, 256, 512), lambda mi, j: (0, out_row // 256, j)),
                     extras=(x, g), epi=_residual, out_dtypes=(F32, F32), name=tag + "_out")
    return xn, (x, h, proj, yv, yo)


def _sc_layer_bwd(dxo, saved, mod, nw, w_sc_in, conv_w, wb, gb, out_row, tag):
    x, h, proj, yv, yo = saved
    sh, sc, g = mod
    L = x.shape[0]
    dyo, gsum = _gate_bwd(dxo, yo, g, name=tag + "_dgate")
    dyv = _matmul(dyo, wb, n=D, tm=TM_ALL, tn=256, contract=_nt,
                  b_spec=pl.BlockSpec((None, 256, D), lambda mi, j: (j, out_row // 256, 0)), name=tag + "_dyv")
    gb = _matmul_tn(yv, dyo, m=D, n=D, tm=256, tn=D, into=gb, out_struct=_like(wb),
                    out_spec=pl.BlockSpec((None, 256, D), lambda mi, j: (mi, out_row // 256, 0)), name=tag + "_dout")
    dproj, csum = _sc_bwd(proj, dyv, conv_w, name=tag + "_dconv")
    tm = min(L, TM_HALF)
    dh = _matmul(dproj, w_sc_in, n=D, tm=tm, contract=_nt_sc_in, a_spec=pl.BlockSpec((3, tm, D), lambda mi, j: (0, mi, 0)),
                 b_spec=pl.BlockSpec((N_CHIPS, 512, SC_IN_SHARD), lambda mi, j: (0, j, 0)), name=tag + "_dh")
    g_sc_in = _matmul_tn(h, dproj, m=D, n=3 * D, tm=D, tn=256, b_spec=pl.BlockSpec((None, L, 256), lambda mi, j: (j // 4, 0, j % 4)),
                         out_spec=pl.BlockSpec((None, D, 256), lambda mi, j: (j // 3, 0, j % 3)),
                         out_struct=jax.ShapeDtypeStruct((N_CHIPS, D, SC_IN_SHARD), BF16), name=tag + "_din")
    dx, sums = _modnorm_bwd(x, dh, dxo, nw, sc, gsum, name=tag + "_dnorm")
    return dx, gb, g_sc_in, sums, csum


SUB_ROW = (0, 8, 16, 24)
SSD_CONV_ROW, GNORM_ROW, FINAL_ROW, SC_CONV_ROW, HEAD_ROW, SMALL_ROWS = 32, 56, 72, 80, 88, 96


def _all_gather_rows(blk, *, name):
    m_per, n = blk.shape

    def body(x_ref, out_ref, send_sems, recv_sems, local_sem):
        x, y, c = lax.axis_index("x"), lax.axis_index("y"), lax.axis_index("c")
        me, sibling = (x, y, c), (x, y, 1 - c)
        chips = [(1 - x, y), (x, 1 - y), (1 - x, 1 - y)]

        def rows(px, py, pc):
            return out_ref.at[pl.ds((4 * px + 2 * py + pc) * m_per, m_per), :]

        def copy(k, block, to, src=None):
            return pltpu.make_async_remote_copy(src_ref=rows(*block) if src is None else src, dst_ref=rows(*block),
                                                send_sem=send_sems.at[k], recv_sem=recv_sems.at[k], device_id=to,
                                                device_id_type=MESH)

        mine = pltpu.make_async_copy(x_ref, rows(*me), local_sem)
        mine.start()
        first = [copy(0, me, sibling, src=x_ref)] + [copy(1 + j, me, (*chip, c), src=x_ref) for j, chip in enumerate(chips)]
        for cp in first:
            cp.start()
        passed = [copy(4 + j, (*chip, c), sibling) for j, chip in enumerate(chips)]
        for j, chip in enumerate(chips):
            copy(1 + j, (*chip, c), me).wait_recv()
            passed[j].start()
        copy(0, sibling, me).wait_recv()
        for j, chip in enumerate(chips):
            copy(4 + j, (*chip, 1 - c), me).wait_recv()
        for cp in first + passed:
            cp.wait_send()
        mine.wait()

    return pl.pallas_call(
        body, out_shape=jax.ShapeDtypeStruct((N_DEV * m_per, n), blk.dtype),
        in_specs=[pl.BlockSpec(memory_space=pltpu.VMEM)], out_specs=pl.BlockSpec(memory_space=pltpu.VMEM),
        scratch_shapes=[pltpu.SemaphoreType.DMA((7,)), pltpu.SemaphoreType.DMA((7,)), pltpu.SemaphoreType.DMA],
        name=name)(blk)


def _half(ref, chip, c):
    hr = ref.shape[1] // 2
    return ref.at[chip, pl.ds(c * hr, hr), :]


def _gather_copy(bufs, sends, recvs, b, k, chip, pc, to):
    piece = _half(bufs[b], 2 * chip[0] + chip[1], pc)
    return pltpu.make_async_remote_copy(src_ref=piece, dst_ref=piece, send_sem=sends.at[4 * b + k], recv_sem=recvs.at[4 * b + k],
                                        device_id=to, device_id_type=MESH)


def _split_call(body, bufs, sems_in, n_sems, *, name, after=(), token=False):
    nb, na, starts = len(bufs), len(after), not sems_in

    def wrapped(*refs):
        sems = refs[nb + na:nb + na + 2] if starts else refs[nb:nb + 2]
        body(refs[:nb], sems[0], sems[1])
        if token:
            refs[-1][...] = jnp.zeros_like(refs[-1])

    out_shape = [pltpu.SemaphoreType.DMA((n_sems,)) for _ in range(2 if starts else 0)]
    out_specs = [SEM] * len(out_shape) + [ANY] * nb
    alias = {b: len(out_shape) + b for b in range(nb)}
    out_shape += [jax.ShapeDtypeStruct(b.shape, b.dtype) for b in bufs]
    if token:
        out_shape.append(jax.ShapeDtypeStruct((8, LANES), F32))
        out_specs.append(pl.BlockSpec(memory_space=pltpu.VMEM))
    return pl.pallas_call(
        wrapped, out_shape=out_shape, in_specs=[ANY] * nb + [SEM] * len(sems_in) + [ANY] * na, out_specs=out_specs,
        input_output_aliases=alias,
        compiler_params=pltpu.CompilerParams(has_side_effects=pltpu.SideEffectType.DATAFLOW_SIDE_EFFECTING),
        name=name)(*bufs, *sems_in, *after)


def _gather_start(bufs, *, name, after=()):
    nb = len(bufs)

    def body(ins, sends, recvs):
        x, y, c = lax.axis_index("x"), lax.axis_index("y"), lax.axis_index("c")
        chips = [(1 - x, y), (x, 1 - y), (1 - x, 1 - y)]
        for b in range(nb):
            _gather_copy(ins, sends, recvs, b, 0, (x, y), c, (x, y, 1 - c)).start()
            for j, chip in enumerate(chips):
                _gather_copy(ins, sends, recvs, b, 1 + j, (x, y), c, (*chip, c)).start()

    out = _split_call(body, bufs, (), 4 * nb, name=name, after=after, token=True)
    return (out[0], out[1], out[2:2 + nb]), out[-1]


def _gather_wait_first(flight, *, name, after=()):
    sends, recvs, bufs = flight
    nb = len(bufs)

    def body(ins, sends_, recvs_):
        x, y, c = lax.axis_index("x"), lax.axis_index("y"), lax.axis_index("c")
        chips = [(1 - x, y), (x, 1 - y), (1 - x, 1 - y)]
        for b in range(nb):
            _gather_copy(ins, sends_, recvs_, b, 0, (x, y), c, (x, y, 1 - c)).wait_send()
            _gather_copy(ins, sends_, recvs_, b, 0, (x, y), 1 - c, (x, y, c)).wait_recv()
            for j, chip in enumerate(chips):
                _gather_copy(ins, sends_, recvs_, b, 1 + j, (x, y), c, (*chip, c)).wait_send()
                _gather_copy(ins, sends_, recvs_, b, 1 + j, chip, c, (x, y, c)).wait_recv()

    return _split_call(body, bufs, (sends, recvs), 4 * nb, name=name, after=after)


def _gather_forward(bufs, *, name):
    nb = len(bufs)

    def body(ins, sends, recvs):
        x, y, c = lax.axis_index("x"), lax.axis_index("y"), lax.axis_index("c")
        chips = [(1 - x, y), (x, 1 - y), (1 - x, 1 - y)]
        for b in range(nb):
            for j, chip in enumerate(chips):
                _gather_copy(ins, sends, recvs, b, 1 + j, chip, c, (x, y, 1 - c)).start()

    out = _split_call(body, bufs, (), 4 * nb, name=name)
    return out[0], out[1], out[2:2 + nb]


def _gather_wait_forward(flight, *, name, after=()):
    sends, recvs, bufs = flight
    nb = len(bufs)

    def body(ins, sends_, recvs_):
        x, y, c = lax.axis_index("x"), lax.axis_index("y"), lax.axis_index("c")
        chips = [(1 - x, y), (x, 1 - y), (1 - x, 1 - y)]
        for b in range(nb):
            for j, chip in enumerate(chips):
                _gather_copy(ins, sends_, recvs_, b, 1 + j, chip, c, (x, y, 1 - c)).wait_send()
                _gather_copy(ins, sends_, recvs_, b, 1 + j, chip, 1 - c, (x, y, c)).wait_recv()

    return _split_call(body, bufs, (sends, recvs), 4 * nb, name=name, after=after)


def _owner_copies(hs, lands, sends, recvs):
    x, y, c = lax.axis_index("x"), lax.axis_index("y"), lax.axis_index("c")
    chips = [(1 - x, y), (x, 1 - y), (1 - x, 1 - y)]
    return [pltpu.make_async_remote_copy(src_ref=hs[b].at[2 * cx + cy], dst_ref=lands[b].at[j], send_sem=sends.at[3 * b + j],
                                         recv_sem=recvs.at[3 * b + j], device_id=(cx, cy, c), device_id_type=MESH)
            for b in range(len(hs)) for j, (cx, cy) in enumerate(chips)]


def _owners_start(hs, *, name):
    nb = len(hs)
    lands = [lax.empty((3,) + h.shape[1:], h.dtype) for h in hs]

    def body(refs, sends, recvs):
        for cp in _owner_copies(refs[:nb], refs[nb:], sends, recvs):
            cp.start()

    out = _split_call(body, list(hs) + lands, (), 3 * nb, name=name, token=True)
    return (out[0], out[1], out[2:2 + 2 * nb]), out[-1]


def _owners_wait(flight, *, name, after=()):
    sends, recvs, bufs = flight
    nb = len(bufs) // 2

    def body(refs, sends_, recvs_):
        for cp in _owner_copies(refs[:nb], refs[nb:], sends_, recvs_):
            cp.wait()

    return _split_call(body, bufs, (sends, recvs), 3 * nb, name=name, after=after)[nb:]


def _sibling_copies(gs, lands, sends, recvs):
    x, y, c = lax.axis_index("x"), lax.axis_index("y"), lax.axis_index("c")
    copies = []
    for b in range(len(gs)):
        hr = gs[b].shape[1] // 2
        copies.append(pltpu.make_async_remote_copy(
            src_ref=gs[b].at[:, pl.ds((1 - c) * hr, hr), :], dst_ref=lands[b], send_sem=sends.at[b], recv_sem=recvs.at[b],
            device_id=(x, y, 1 - c), device_id_type=MESH))
    return copies


def _sibling_start(gs, *, name, after=()):
    nb = len(gs)
    lands = [lax.empty((g.shape[0], g.shape[1] // 2, g.shape[2]), g.dtype) for g in gs]

    def body(refs, sends, recvs):
        for cp in _sibling_copies(refs[:nb], refs[nb:], sends, recvs):
            cp.start()

    out = _split_call(body, list(gs) + lands, (), nb, name=name, after=after, token=True)
    return (out[0], out[1], out[2:2 + 2 * nb]), out[-1]


def _sibling_wait(flight, *, name, after=()):
    sends, recvs, bufs = flight
    nb = len(bufs) // 2

    def body(refs, sends_, recvs_):
        for cp in _sibling_copies(refs[:nb], refs[nb:], sends_, recvs_):
            cp.wait()

    out = _split_call(body, bufs, (sends, recvs), nb, name=name, after=after)
    return out[:nb], out[nb:]


def _result_copies(ts, sends, recvs):
    x, y, c = lax.axis_index("x"), lax.axis_index("y"), lax.axis_index("c")
    return [pltpu.make_async_remote_copy(src_ref=ts[b].at[c], dst_ref=ts[b].at[c], send_sem=sends.at[b], recv_sem=recvs.at[b],
                                         device_id=(x, y, 1 - c), device_id_type=MESH) for b in range(len(ts))]


def _result_start(ts, *, name):
    def body(refs, sends, recvs):
        for cp in _result_copies(refs, sends, recvs):
            cp.start()

    out = _split_call(body, ts, (), len(ts), name=name, token=True)
    return (out[0], out[1], out[2:2 + len(ts)]), out[-1]


def _result_wait(flight, *, name, after=()):
    sends, recvs, bufs = flight

    def body(refs, sends_, recvs_):
        for cp in _result_copies(refs, sends_, recvs_):
            cp.wait()

    return _split_call(body, bufs, (sends, recvs), len(bufs), name=name, after=after)


def _row_tile(rows, cols):
    best = 16
    for t in range(16, rows + 1, 16):
        if rows % t == 0 and t * cols <= 640 * 1024:
            best = t
    assert rows % best == 0, (rows, cols)
    return best


def _add_sibling_half(g, recv, core, *, name):
    nk, r, n = g.shape
    hr = r // 2
    tr = _row_tile(hr, n)

    def body(c_ref, a_ref, b_ref, o_ref):
        o_ref[...] = (a_ref[...].astype(F32) + b_ref[...].astype(F32)).astype(BF16)

    grid_spec = pltpu.PrefetchScalarGridSpec(
        num_scalar_prefetch=1, grid=(nk, hr // tr),
        in_specs=[pl.BlockSpec((None, tr, n), lambda k, i, c_ref: (k, c_ref[0] * (hr // tr) + i, 0)),
                  pl.BlockSpec((None, tr, n), lambda k, i, c_ref: (k, i, 0))],
        out_specs=pl.BlockSpec((None, tr, n), lambda k, i, c_ref: (k, i, 0)))
    return pl.pallas_call(body, grid_spec=grid_spec, out_shape=jax.ShapeDtypeStruct((nk, hr, n), BF16),
                          compiler_params=_params(("parallel", "parallel")), name=name)(core, g, recv)


def _add_chip_sums(h, recv, chip_core, *, name):
    _, hr, n = h.shape
    tr = _row_tile(hr, n)

    def body(k_ref, a_ref, b_ref, o_ref):
        o_ref[...] = ((a_ref[...].astype(F32) + b_ref[0].astype(F32)) + b_ref[1].astype(F32)) + b_ref[2].astype(F32)

    grid_spec = pltpu.PrefetchScalarGridSpec(
        num_scalar_prefetch=1, grid=(hr // tr,),
        in_specs=[pl.BlockSpec((None, tr, n), lambda i, k_ref: (k_ref[0], i, 0)),
                  pl.BlockSpec((3, tr, n), lambda i, k_ref: (0, i, 0))],
        out_specs=pl.BlockSpec((None, tr, n), lambda i, k_ref: (k_ref[1], i, 0)))
    return pl.pallas_call(body, grid_spec=grid_spec, out_shape=jax.ShapeDtypeStruct((2, hr, n), F32),
                          compiler_params=_params(("parallel",)), name=name)(chip_core, h, recv)


def _sum_devices(g, *, name):
    nd, r, n = g.shape

    def body(g_ref, o_ref):
        acc = g_ref[0]
        for i in range(1, nd):
            acc = acc + g_ref[i]
        o_ref[...] = acc

    return pl.pallas_call(body, out_shape=jax.ShapeDtypeStruct((r, n), F32), name=name)(g)


def _own_slot(shard, chip):
    return lax.dynamic_update_slice(jnp.zeros((N_CHIPS,) + shard.shape, BF16), shard[None], (chip, 0, 0))


def kernel(x, c, ada_w, ada_b, mix_norm_w, mlp_norm_w, mlp_up, mlp_down, ssd_in_w, ssd_conv_w, ssd_conv_b, ssd_dt_bias, ssd_A_log, ssd_D, ssd_norm_w, ssd_out_w, sc_in_w, sc_conv_w, sc_out_w, final_norm_w, loss_target, m_ada_w, m_ada_b, m_mix_norm_w, m_mlp_norm_w, m_mlp_up, m_mlp_down, m_ssd_in_w, m_ssd_conv_w, m_ssd_conv_b, m_ssd_dt_bias, m_ssd_A_log, m_ssd_D, m_ssd_norm_w, m_ssd_out_w, m_sc_in_w, m_sc_conv_w, m_sc_out_w, m_final_norm_w, v_ada_w, v_ada_b, v_mix_norm_w, v_mlp_norm_w, v_mlp_up, v_mlp_down, v_ssd_in_w, v_ssd_conv_w, v_ssd_conv_b, v_ssd_dt_bias, v_ssd_A_log, v_ssd_D, v_ssd_norm_w, v_ssd_out_w, v_sc_in_w, v_sc_conv_w, v_sc_out_w, v_final_norm_w):
    xi, yi, ci = lax.axis_index("x"), lax.axis_index("y"), lax.axis_index("c")
    chip = 2 * xi + yi
    dev = 2 * chip + ci
    n_ada = ada_w.shape[2]

    conv_flat = jnp.concatenate([ssd_conv_w.reshape(-1), sc_conv_w.reshape(-1), jnp.zeros((256,), F32)]).reshape(4, D)
    blk0 = jnp.concatenate([c, conv_flat, jnp.zeros((3, D), F32)], axis=0)
    got0 = _all_gather_rows(blk0, name="gather_cond").reshape(N_DEV, 8, D)
    c_all = got0[:, 0]
    conv_all = got0[0::2, 1:5].reshape(N_CHIPS, 4 * D)
    ssd_conv = jnp.moveaxis(conv_all[:, :4 * 768].reshape(N_CHIPS, 4, 768), 0, 1).reshape(4, CONVD)
    sc_conv = jnp.moveaxis(conv_all[:, 4 * 768:4 * 768 + 3 * 256].reshape(N_CHIPS, 3, 256), 0, 1).reshape(3, D)
    mod_shard = [_matmul(c_all, ada_w, n=n_ada, a_silu=True, b_spec=pl.BlockSpec((None, D, 512), lambda mi, j, i=i: (i, 0, j)),
                         extras=(lax.dynamic_slice(ada_b, (i, chip * n_ada), (1, n_ada)),),
                         epi=lambda acc, b: (acc + b,), name=f"ada_mod{i}") for i in range(2)]
    mod_all = _all_gather_rows(jnp.concatenate(mod_shard, axis=0), name="gather_mod")
    mod_all = mod_all.reshape(N_DEV, 2, N_DEV, n_ada)[0::2]
    mod = jnp.moveaxis(lax.dynamic_index_in_dim(mod_all, dev, axis=2, keepdims=False), 0, 1).reshape(2, 6, D)
    mods = [[mod[i, j:j + 1] for j in range(6)] for i in range(2)]

    bf = lambda v: v.astype(BF16)
    up_row, down_row, sc_out_row = 0, D, 2 * D
    a_bufs = [_own_slot(bf(ssd_in_w[0]), chip)]
    b_bufs = [_own_slot(bf(ssd_out_w[0]), chip), _own_slot(bf(jnp.concatenate([mlp_up[0], mlp_down[0]], axis=0)), chip)]
    c_bufs = [_own_slot(bf(sc_in_w[0]), chip), _own_slot(bf(jnp.concatenate([mlp_up[1], mlp_down[1], sc_out_w[0]], axis=0)), chip)]
    fly_a, tok = _gather_start(a_bufs, name="gather_a_start", after=(mod,))
    fly_b, tok = _gather_start(b_bufs, name="gather_b_start", after=(tok,))
    fly_c, tok = _gather_start(c_bufs, name="gather_c_start", after=(tok,))

    row = lambda v: v.reshape(1, -1)
    xs, tgt = x[0], loss_target[0]
    prm = jnp.pad(jnp.concatenate([ssd_dt_bias, ssd_A_log, ssd_D, jnp.zeros((5, NH), F32)], axis=0), ((0, 0), (0, LANES - NH)))
    mix_nw = [row(mix_norm_w[i]) for i in range(2)]
    mlp_nw = [row(mlp_norm_w[i]) for i in range(2)]
    a_bufs = _gather_wait_first(fly_a, name="gather_a_landed", after=(tok,))
    (w_ssd_in,) = _gather_wait_forward(_gather_forward(a_bufs, name="gather_a_pass"), name="gather_a_done")
    ssd_in_full = jnp.moveaxis(w_ssd_in, 0, 1).reshape(D, N_CHIPS * SSD_IN_SHARD)
    w_zx, w_dt = ssd_in_full[:, :ZX], jnp.pad(ssd_in_full[:, ZX:], ((0, 0), (0, LANES - NH)))
    scan = _ssd_fwd_scan(xs, mods[0][0:3], mix_nw[0], w_zx, w_dt, ssd_conv, ssd_conv_b, prm, "ssd")
    fly_b = _gather_forward(_gather_wait_first(fly_b, name="gather_b_landed", after=(scan[3],)), name="gather_b_pass")
    w_ssd_out, w_b = _gather_wait_forward(fly_b, name="gather_b_done", after=(scan[4],))
    x1, s_ssd = _ssd_fwd_out(xs, mods[0][0:3], scan, ssd_norm_w, w_ssd_out, "ssd")
    x2, s_mlp0 = _mlp_fwd(x1, mods[0][3:6], mlp_nw[0], w_b, up_row, down_row, "mlp0")
    c_bufs = _gather_wait_first(fly_c, name="gather_c_landed", after=(x2,))
    w_sc_in, w_c = _gather_wait_forward(_gather_forward(c_bufs, name="gather_c_pass"), name="gather_c_done")
    x3, s_sc = _sc_layer_fwd(x2, mods[1][0:3], mix_nw[1], w_sc_in, sc_conv, w_c, sc_out_row, "sc")
    x4, s_mlp1 = _mlp_fwd(x3, mods[1][3:6], mlp_nw[1], w_c, up_row, down_row, "mlp1")

    core = ci.reshape(1).astype(jnp.int32)
    chip_core = jnp.stack([chip, ci]).astype(jnp.int32)

    def reduce_swap(gbufs, tag, after=()):
        return _sibling_start(gbufs, name=tag + "_sibling_start", after=after)

    def reduce_send(flight, tag, after):
        gs, sib = _sibling_wait(flight, name=tag + "_sibling_landed", after=after)
        hs = [_add_sibling_half(g, s, core, name=f"{tag}_add_sibling{b}") for b, (g, s) in enumerate(zip(gs, sib))]
        return _owners_start(hs, name=tag + "_owners_start")

    def reduce_sum(flight, tag, after):
        nb = len(flight[2]) // 2
        lands = _owners_wait(flight, name=tag + "_owners_landed", after=after)
        ts = [_add_chip_sums(h, o, chip_core, name=f"{tag}_add_chips{b}") for b, (h, o) in enumerate(zip(flight[2][:nb], lands))]
        return _result_start(ts, name=tag + "_result_start")

    def reduce_done(flight, tag, after=()):
        return [t.reshape(-1, t.shape[2]) for t in _result_wait(flight, name=tag + "_result_landed", after=after)]

    dx4, fsum = _final_loss(x4, row(final_norm_w), tgt, name="final_loss")
    dx3, g_c, sum_mlp1 = _mlp_bwd(dx4, s_mlp1, mods[1][3:6], mlp_nw[1], w_c, None, up_row, down_row, "mlp1")
    dx2, g_c, g_sc_in, sum_sc, sc_csum = _sc_layer_bwd(dx3, s_sc, mods[1][0:3], mix_nw[1], w_sc_in, sc_conv, w_c, g_c,
                                                       sc_out_row, "sc")
    dx1, g_b, sum_mlp0 = _mlp_bwd(dx2, s_mlp0, mods[0][3:6], mlp_nw[0], w_b, None, up_row, down_row, "mlp0")
    dyn, g_ssd_out, gsum_ssd = _ssd_bwd_out(dx1, s_ssd, mods[0][0:3], w_ssd_out, "ssd")
    fly_1, tok = reduce_swap([g_c, g_sc_in, g_b, g_ssd_out], "rs1")
    dy, dzx, gnsum = _gnorm_bwd(s_ssd[5], s_ssd[2], ssd_norm_w + tok[0:1, 0:1], dyn, name="ssd_dgnorm")
    fly_1, tok = reduce_send(fly_1, "rs1", (dy,))
    grad_x, d_w_zx, d_w_dt, sum_ssd, csum, ssum = _ssd_bwd_rest(
        dx1, dy, dzx, gsum_ssd, s_ssd, mods[0][0:3], mix_nw[0], w_zx, w_dt, ssd_conv, ssd_conv_b, prm + tok[0:1, 0:1], "ssd")
    fly_1, tok = reduce_sum(fly_1, "rs1", (grad_x,))

    def ssd_in_owner(k):
        lo, hi = k * SSD_IN_SHARD, (k + 1) * SSD_IN_SHARD
        if hi <= ZX:
            return d_w_zx[:, lo:hi]
        return jnp.concatenate([d_w_zx[:, lo:], d_w_dt[:, :hi - ZX]], axis=1)

    small = jnp.concatenate([sum_ssd + tok[0:1, 0:1], sum_mlp0, sum_sc, sum_mlp1, csum.reshape(24, D), gnsum.reshape(16, D),
                             fsum, sc_csum, jnp.pad(ssum, ((0, 0), (0, D - LANES)))], axis=0)
    small_all = _all_gather_rows(small, name="gather_small").reshape(N_DEV, SMALL_ROWS, D)
    fly_2, tok = reduce_swap([jnp.stack([ssd_in_owner(k) for k in range(N_CHIPS)]).astype(BF16)], "rs2", (small_all,))
    fly_2, tok = reduce_send(fly_2, "rs2", (tok,))
    t_c, t_sc_in, t_b, t_ssd_out = reduce_done(fly_1, "rs1", (tok,))
    small_all = small_all + tok[0:1, 0:1]
    tot = _sum_devices(small_all, name="sum_small")
    loss = tot[FINAL_ROW + 1, 0]
    mod_rows = [r + o for r in SUB_ROW for o in (3, 2, 0)]
    g_ada_b = jnp.stack([tot[r] for r in mod_rows]).reshape(2, 6 * D)
    g_mix_norm = jnp.stack([tot[SUB_ROW[0] + 1], tot[SUB_ROW[2] + 1]])
    g_mlp_norm = jnp.stack([tot[SUB_ROW[1] + 1], tot[SUB_ROW[3] + 1]])
    conv_sums = tot[SSD_CONV_ROW:SSD_CONV_ROW + 24].reshape(8, CONVD)
    g_ssd_conv_w = lax.dynamic_slice(conv_sums, (0, chip * 768), (4, 768))[None]
    g_ssd_conv_b = conv_sums[4:5]
    g_ssd_norm = tot[GNORM_ROW:GNORM_ROW + 2].reshape(1, DI)
    g_final = tot[FINAL_ROW]
    g_sc_conv_w = lax.dynamic_slice(tot[SC_CONV_ROW:SC_CONV_ROW + 3], (0, chip * 256), (3, 256))[None]
    g_a_log, g_d, g_dt_bias = (tot[HEAD_ROW + r:HEAD_ROW + r + 1, 0:NH] for r in range(3))
    c_pad = jnp.concatenate([c_all, jnp.zeros((8, D), F32)], axis=0)
    dmod_all = jnp.stack([small_all[:, r] for r in mod_rows], axis=1).reshape(N_DEV, 2, 6 * D)
    g_ada_w = []
    for i in range(2):
        dm = lax.dynamic_slice(dmod_all[:, i], (0, chip * n_ada), (N_DEV, n_ada))
        g_ada_w.append(_matmul_tn(c_pad, jnp.concatenate([dm, jnp.zeros_like(dm)], axis=0), m=D, n=n_ada, a_silu=True,
                                  name=f"ada_dw{i}"))

    big = dict(ada_w=[(g, 0) for g in g_ada_w], mlp_up=[(t_b, up_row), (t_c, up_row)], mlp_down=[(t_b, down_row), (t_c, down_row)],
               ssd_out_w=[(t_ssd_out, 0)], sc_out_w=[(t_c, sc_out_row)], sc_in_w=[(t_sc_in, 0)], ssd_in_w=None)
    grads = dict(ada_b=g_ada_b, mix_norm_w=g_mix_norm, mlp_norm_w=g_mlp_norm, ssd_conv_w=g_ssd_conv_w,
                 ssd_conv_b=g_ssd_conv_b, ssd_dt_bias=g_dt_bias, ssd_A_log=g_a_log, ssd_D=g_d, ssd_norm_w=g_ssd_norm,
                 sc_conv_w=g_sc_conv_w, final_norm_w=g_final)
    weights = dict(ada_w=(ada_w, m_ada_w, v_ada_w), ada_b=(ada_b, m_ada_b, v_ada_b),
                   mix_norm_w=(mix_norm_w, m_mix_norm_w, v_mix_norm_w), mlp_norm_w=(mlp_norm_w, m_mlp_norm_w, v_mlp_norm_w),
                   mlp_up=(mlp_up, m_mlp_up, v_mlp_up), mlp_down=(mlp_down, m_mlp_down, v_mlp_down),
                   ssd_in_w=(ssd_in_w, m_ssd_in_w, v_ssd_in_w), ssd_conv_w=(ssd_conv_w, m_ssd_conv_w, v_ssd_conv_w),
                   ssd_conv_b=(ssd_conv_b, m_ssd_conv_b, v_ssd_conv_b), ssd_dt_bias=(ssd_dt_bias, m_ssd_dt_bias, v_ssd_dt_bias),
                   ssd_A_log=(ssd_A_log, m_ssd_A_log, v_ssd_A_log), ssd_D=(ssd_D, m_ssd_D, v_ssd_D),
                   ssd_norm_w=(ssd_norm_w, m_ssd_norm_w, v_ssd_norm_w), ssd_out_w=(ssd_out_w, m_ssd_out_w, v_ssd_out_w),
                   sc_in_w=(sc_in_w, m_sc_in_w, v_sc_in_w), sc_conv_w=(sc_conv_w, m_sc_conv_w, v_sc_conv_w),
                   sc_out_w=(sc_out_w, m_sc_out_w, v_sc_out_w), final_norm_w=(final_norm_w, m_final_norm_w, v_final_norm_w))
    def step(nm, parts):
        w, m, v = (t if t.shape[0] == 1 else t.reshape(-1, t.shape[-1]) for t in weights[nm])
        rows, outs = w.shape[-2] // len(parts), None
        for i, (gbuf, g_row) in enumerate(parts):
            outs = _adamw(w, gbuf, m, v, g_row=g_row, w_row=i * rows, rows=rows, into=outs, emit_g=True, name=f"adamw_{nm}{i}")
        return outs

    res = {}
    for nm, (w, m, v) in weights.items():
        two_d = (-1, w.shape[-1]) if w.ndim > 1 else (1, -1)
        if nm not in big:
            res[nm] = (grads[nm], *_adamw(w.reshape(two_d), grads[nm].reshape(two_d), m.reshape(two_d), v.reshape(two_d),
                                          name="adamw_" + nm))
        elif big[nm] is not None:
            res[nm] = step(nm, big[nm])
    fly_2, tok = reduce_sum(fly_2, "rs2", tuple(r[1] for r in res.values()))
    (t_ssd_in,) = reduce_done(fly_2, "rs2", (tok,))
    res["ssd_in_w"] = step("ssd_in_w", [(t_ssd_in, 0)])
    outs = [[res[nm][k].reshape(weights[nm][0].shape) for nm in weights] for k in range(4)]
    return (loss, grad_x[None], *outs[0], *outs[1], *outs[2], *outs[3])
```

```python
import jax
import jax.numpy as jnp
from jax import lax
from jax.experimental import pallas as pl
from jax.experimental.pallas import tpu as pltpu

F32 = jnp.float32
BF16 = jnp.bfloat16
MESH = pl.DeviceIdType.MESH

D = 1024
DFF = 4096
DI = 2048
NH = 32
HP = 64
NG = 4
NS = 128
CH = 128
CONVD = DI + 2 * NG * NS
ZX = DI + CONVD
GW = NG * NS
LANES = 128
N_CHIPS = 4
N_DEV = 8
EPS = 1e-5
ADAM_LR, ADAM_B1, ADAM_B2, ADAM_EPS, ADAM_WD, ADAM_STEP = 1e-3, 0.9, 0.999, 1e-8, 0.01, 10
VMEM_LIMIT = 48 * 1024 * 1024
TM_ALL = 2048
TM_HALF = 1024
ANY = pl.BlockSpec(memory_space=pl.ANY)
SEM = pl.BlockSpec(memory_space=pltpu.SEMAPHORE)

SSD_IN_SHARD = 1288
SC_IN_SHARD = 768


def _params(sem=None):
    return pltpu.CompilerParams(dimension_semantics=sem, vmem_limit_bytes=VMEM_LIMIT)


def _sigmoid(v):
    return 1.0 / (1.0 + jnp.exp(-v))


def _dot(a, b, dims=((1,), (0,)), precision=None):
    return lax.dot_general(a, b, (dims, ((), ())), preferred_element_type=F32, precision=precision)


def _dot_nt(a, b):
    return _dot(a, b, ((1,), (1,)))


def _dot_tn(a, b):
    return _dot(a, b, ((0,), (0,)))


def _nn(av, bv):
    return _dot(av.astype(BF16), bv.astype(BF16))


def _nt(av, bv):
    return _dot_nt(av.astype(BF16), bv.astype(BF16))


def _nn_split(av, bv):
    return _dot(av.astype(BF16), bv.reshape(-1, bv.shape[2]))


def _nn_split_sq(av, bv):
    af = av.astype(F32)
    return _nn_split(af * af, bv)


def _nt_split(av, bv):
    kc = bv.shape[2]
    acc = _dot_nt(av[:, 0:kc].astype(BF16), bv[0])
    for s in range(1, bv.shape[0]):
        acc = acc + _dot_nt(av[:, s * kc:(s + 1) * kc].astype(BF16), bv[s])
    return acc


def _nt_sc_in(av, bv):
    q = 256
    acc = None
    for i in range(3 * D // q):
        a_blk = av[i // 4][:, (i % 4) * q:(i % 4 + 1) * q]
        b_blk = bv[i // 3][:, (i % 3) * q:(i % 3 + 1) * q]
        t = _dot_nt(a_blk, b_blk)
        acc = t if acc is None else acc + t
    return acc


def _matmul(a, b, *, name, n, contract=_nn, a_spec=None, b_spec=None, tm=512, tn=512, extras=(), epi=None,
            out_dtypes=(F32,), a_silu=False):
    M = a.shape[-2]
    tm, tn = min(tm, M), min(tn, n)
    assert M % tm == 0 and n % tn == 0, (name, M, n, tm, tn)
    n_ex = len(extras)
    if a_spec is None:
        a_spec = pl.BlockSpec((tm, a.shape[1]), lambda i, j: (i, 0))
    if b_spec is None:
        b_spec = (pl.BlockSpec((tn, b.shape[1]), lambda i, j: (j, 0)) if contract is _nt
                  else pl.BlockSpec((b.shape[0], tn), lambda i, j: (0, j)))

    def body(*refs):
        av = refs[0][...]
        if a_silu:
            av = av * _sigmoid(av)
        acc = contract(av, refs[1][...])
        res = epi(acc, *[r[...] for r in refs[2:2 + n_ex]]) if epi is not None else (acc,)
        for o_ref, r in zip(refs[2 + n_ex:], res, strict=True):
            o_ref[...] = r.astype(o_ref.dtype)

    in_specs = [a_spec, b_spec]
    for e in extras:
        in_specs.append(pl.BlockSpec((1, tn), lambda i, j: (0, j)) if e.shape[0] == 1 and M != 1
                        else pl.BlockSpec((tm, tn), lambda i, j: (i, j)))
    outs = pl.pallas_call(
        body, grid=(M // tm, n // tn), in_specs=in_specs,
        out_specs=[pl.BlockSpec((tm, tn), lambda i, j: (i, j)) for _ in out_dtypes],
        out_shape=[jax.ShapeDtypeStruct((M, n), dt) for dt in out_dtypes],
        compiler_params=_params(("parallel", "parallel")), name=name)(a, b, *extras)
    return outs if len(out_dtypes) > 1 else outs[0]


def _matmul_tn(a, b, *, name, m, n, tm=512, tn=512, a_spec=None, b_spec=None, out_spec=None, out_struct=None, into=None,
               a_silu=False, a_square=False):
    T = a.shape[-2]
    tm, tn = min(tm, m), min(tn, n)
    assert m % tm == 0 and n % tn == 0, (name, m, n, tm, tn)
    if a_spec is None:
        a_spec = pl.BlockSpec((T, tm), lambda i, j: (0, i))
    if b_spec is None:
        b_spec = pl.BlockSpec((T, tn), lambda i, j: (0, j))
    if out_spec is None:
        out_spec, out_struct = pl.BlockSpec((tm, tn), lambda i, j: (i, j)), jax.ShapeDtypeStruct((m, n), F32)

    def body(a_ref, b_ref, *rest):
        av = a_ref[...]
        if a_silu:
            av = av * _sigmoid(av)
        if a_square:
            av = av.astype(F32) * av.astype(F32)
        rest[-1][...] = _dot_tn(av.astype(BF16), b_ref[...].astype(BF16)).astype(rest[-1].dtype)

    args, in_specs, alias = [a, b], [a_spec, b_spec], {}
    if into is not None:
        args, in_specs, alias = args + [into], in_specs + [ANY], {2: 0}
    return pl.pallas_call(body, grid=(m // tm, n // tn), in_specs=in_specs, out_specs=out_spec, out_shape=out_struct,
                          input_output_aliases=alias, compiler_params=_params(("parallel", "parallel")), name=name)(*args)


def _modnorm_fwd(x, nw, sc, sh, *, name):
    L = x.shape[0]
    tm = min(L, 512)

    def body(x_ref, nw_ref, sc_ref, sh_ref, h_ref):
        xv = x_ref[...]
        r = lax.rsqrt(jnp.mean(xv * xv, axis=-1, keepdims=True) + EPS)
        h_ref[...] = ((xv * r * nw_ref[...]) * (1.0 + sc_ref[...]) + sh_ref[...]).astype(BF16)

    row = pl.BlockSpec((tm, D), lambda i: (i, 0))
    vec = pl.BlockSpec((1, D), lambda i: (0, 0))
    return pl.pallas_call(body, grid=(L // tm,), in_specs=[row, vec, vec, vec], out_specs=row,
                          out_shape=jax.ShapeDtypeStruct((L, D), BF16),
                          compiler_params=_params(("parallel",)), name=name)(x, nw, sc, sh)


def _modnorm_bwd(x, dh, dxo, nw, sc, gsum, *, name):
    L = x.shape[0]
    tm = min(L, 256)

    def body(x_ref, dh_ref, dxo_ref, nw_ref, sc_ref, g_ref, dx_ref, s_ref):
        @pl.when(pl.program_id(0) == 0)
        def _():
            s_ref[...] = g_ref[...]

        xv, dhv = x_ref[...], dh_ref[...]
        r = lax.rsqrt(jnp.mean(xv * xv, axis=-1, keepdims=True) + EPS)
        xhat = xv * r
        dxhat = dhv * (nw_ref[...] * (1.0 + sc_ref[...]))
        dx_ref[...] = dxo_ref[...] + r * (dxhat - xhat * jnp.mean(dxhat * xhat, axis=-1, keepdims=True))
        s_ref[1:2, :] += jnp.sum(dhv * xhat, axis=0, keepdims=True) * (1.0 + sc_ref[...])
        s_ref[2:3, :] += jnp.sum(dhv * xhat, axis=0, keepdims=True) * nw_ref[...]
        s_ref[3:4, :] += jnp.sum(dhv, axis=0, keepdims=True)

    row = pl.BlockSpec((tm, D), lambda i: (i, 0))
    vec = pl.BlockSpec((1, D), lambda i: (0, 0))
    blk = pl.BlockSpec((8, D), lambda i: (0, 0))
    return pl.pallas_call(body, grid=(L // tm,), in_specs=[row, row, row, vec, vec, blk], out_specs=[row, blk],
                          out_shape=[jax.ShapeDtypeStruct((L, D), F32), jax.ShapeDtypeStruct((8, D), F32)],
                          compiler_params=_params(("arbitrary",)), name=name)(x, dh, dxo, nw, sc, gsum)


def _gate_bwd(dxo, y, g, *, name):
    L = dxo.shape[0]
    tm = min(L, 512)

    def body(dxo_ref, y_ref, g_ref, dy_ref, s_ref):
        @pl.when(pl.program_id(0) == 0)
        def _():
            s_ref[...] = jnp.zeros_like(s_ref)

        dv = dxo_ref[...]
        dy_ref[...] = (dv * g_ref[...]).astype(BF16)
        s_ref[0:1, :] += jnp.sum(dv * y_ref[...], axis=0, keepdims=True)

    row = pl.BlockSpec((tm, D), lambda i: (i, 0))
    return pl.pallas_call(body, grid=(L // tm,), in_specs=[row, row, pl.BlockSpec((1, D), lambda i: (0, 0))],
                          out_specs=[row, pl.BlockSpec((8, D), lambda i: (0, 0))],
                          out_shape=[jax.ShapeDtypeStruct((L, D), BF16), jax.ShapeDtypeStruct((8, D), F32)],
                          compiler_params=_params(("arbitrary",)), name=name)(dxo, y, g)


def _final_loss(x, fw, tgt, *, name):
    L = x.shape[0]
    tm = min(L, 256)

    def body(x_ref, fw_ref, t_ref, dx_ref, s_ref):
        @pl.when(pl.program_id(0) == 0)
        def _():
            s_ref[...] = jnp.zeros_like(s_ref)

        xv = x_ref[...]
        r = lax.rsqrt(jnp.mean(xv * xv, axis=-1, keepdims=True) + EPS)
        xhat = xv * r
        diff = xhat * fw_ref[...] - t_ref[...]
        dout = diff * (1.0 / D)
        dxhat = dout * fw_ref[...]
        dx_ref[...] = r * (dxhat - xhat * jnp.mean(dxhat * xhat, axis=-1, keepdims=True))
        s_ref[0:1, :] += jnp.sum(dout * xhat, axis=0, keepdims=True)
        s_ref[1:2, :] += jnp.zeros((1, D), F32) + 0.5 * jnp.sum(jnp.sum(diff * diff, axis=-1, keepdims=True) * (1.0 / D))

    row = pl.BlockSpec((tm, D), lambda i: (i, 0))
    return pl.pallas_call(body, grid=(L // tm,), in_specs=[row, pl.BlockSpec((1, D), lambda i: (0, 0)), row],
                          out_specs=[row, pl.BlockSpec((8, D), lambda i: (0, 0))],
                          out_shape=[jax.ShapeDtypeStruct((L, D), F32), jax.ShapeDtypeStruct((8, D), F32)],
                          compiler_params=_params(("arbitrary",)), name=name)(x, fw, tgt)


def _shift_down(v, j):
    if j == 0:
        return v
    row = lax.broadcasted_iota(jnp.int32, v.shape, 0)
    return jnp.where(row >= j, pltpu.roll(v, j, 0), 0.0)


def _shift_up(v, j):
    if j == 0:
        return v
    n = v.shape[0]
    row = lax.broadcasted_iota(jnp.int32, v.shape, 0)
    return jnp.where(row < n - j, pltpu.roll(v, n - j, 0), 0.0)


def _ssd_conv_fwd(zx, w, b, *, name):
    L = zx.shape[0]
    cb = 256
    k = w.shape[0]

    def body(x_ref, w_ref, b_ref, o_ref):
        xv = x_ref[...].astype(F32)
        pre = b_ref[...] + xv * w_ref[k - 1:k, :]
        for j in range(1, k):
            pre = pre + _shift_down(xv, j) * w_ref[k - 1 - j:k - j, :]
        o_ref[...] = (pre * _sigmoid(pre)).astype(BF16)

    return pl.pallas_call(
        body, grid=(CONVD // cb,),
        in_specs=[pl.BlockSpec((L, cb), lambda i: (0, i + DI // cb)), pl.BlockSpec((k, cb), lambda i: (0, i)),
                  pl.BlockSpec((1, cb), lambda i: (0, i))],
        out_specs=pl.BlockSpec((L, cb), lambda i: (0, i)), out_shape=jax.ShapeDtypeStruct((L, CONVD), BF16),
        compiler_params=_params(("parallel",)), name=name)(zx, w, b)


def _ssd_conv_bwd(zx, dact, w, b, dzx, *, name):
    L = zx.shape[0]
    cb = 256
    k = w.shape[0]

    def body(x_ref, da_ref, w_ref, b_ref, _, dx_ref, s_ref):
        xv = x_ref[...].astype(F32)
        sh = [_shift_down(xv, j) for j in range(k)]
        pre = b_ref[...] + sh[0] * w_ref[k - 1:k, :]
        for j in range(1, k):
            pre = pre + sh[j] * w_ref[k - 1 - j:k - j, :]
        s = _sigmoid(pre)
        dpre = da_ref[...].astype(F32) * (s * (1.0 + pre * (1.0 - s)))
        dx = dpre * w_ref[k - 1:k, :]
        for j in range(1, k):
            dx = dx + _shift_up(dpre, j) * w_ref[k - 1 - j:k - j, :]
        dx_ref[...] = dx.astype(BF16)
        s_ref[...] = jnp.zeros_like(s_ref)
        for j in range(k):
            s_ref[k - 1 - j:k - j, :] = jnp.sum(dpre * sh[j], axis=0, keepdims=True)
        s_ref[k:k + 1, :] = jnp.sum(dpre, axis=0, keepdims=True)

    return pl.pallas_call(
        body, grid=(CONVD // cb,),
        in_specs=[pl.BlockSpec((L, cb), lambda i: (0, i + DI // cb)), pl.BlockSpec((L, cb), lambda i: (0, i)),
                  pl.BlockSpec((k, cb), lambda i: (0, i)), pl.BlockSpec((1, cb), lambda i: (0, i)), ANY],
        out_specs=[pl.BlockSpec((L, cb), lambda i: (0, i + DI // cb)), pl.BlockSpec((8, cb), lambda i: (0, i))],
        out_shape=[jax.ShapeDtypeStruct((L, ZX), BF16), jax.ShapeDtypeStruct((8, CONVD), F32)],
        input_output_aliases={4: 0}, compiler_params=_params(("parallel",)), name=name)(zx, dact, w, b, dzx)


def _sc_fwd(proj, w, *, name):
    L = proj.shape[0]
    cb = 256
    nb = D // cb
    k = w.shape[0]

    def body(b_ref, c_ref, x_ref, w_ref, o_ref):
        u = c_ref[...] * x_ref[...]
        v = u * w_ref[k - 1:k, :]
        for j in range(1, k):
            v = v + _shift_down(u, j) * w_ref[k - 1 - j:k - j, :]
        o_ref[...] = (b_ref[...] * v).astype(BF16)

    return pl.pallas_call(
        body, grid=(nb,),
        in_specs=[pl.BlockSpec((L, cb), lambda i: (0, i)), pl.BlockSpec((L, cb), lambda i: (0, i + nb)),
                  pl.BlockSpec((L, cb), lambda i: (0, i + 2 * nb)), pl.BlockSpec((k, cb), lambda i: (0, i))],
        out_specs=pl.BlockSpec((L, cb), lambda i: (0, i)), out_shape=jax.ShapeDtypeStruct((L, D), BF16),
        compiler_params=_params(("parallel",)), name=name)(proj, proj, proj, w)


def _sc_bwd(proj, dyv, w, *, name):
    L = proj.shape[0]
    cb = 256
    nb = D // cb
    k = w.shape[0]

    def body(b_ref, c_ref, x_ref, dy_ref, w_ref, dp_ref, s_ref):
        cv, xv = c_ref[...], x_ref[...]
        u = cv * xv
        sh = [_shift_down(u, j) for j in range(k)]
        v = sh[0] * w_ref[k - 1:k, :]
        for j in range(1, k):
            v = v + sh[j] * w_ref[k - 1 - j:k - j, :]
        dyv_ = dy_ref[...]
        dp_ref[0] = (dyv_ * v).astype(BF16)
        dv = dyv_ * b_ref[...]
        du = dv * w_ref[k - 1:k, :]
        for j in range(1, k):
            du = du + _shift_up(dv, j) * w_ref[k - 1 - j:k - j, :]
        dp_ref[1] = (du * xv).astype(BF16)
        dp_ref[2] = (du * cv).astype(BF16)
        s_ref[...] = jnp.zeros_like(s_ref)
        for j in range(k):
            s_ref[k - 1 - j:k - j, :] = jnp.sum(dv * sh[j], axis=0, keepdims=True)

    blk = pl.BlockSpec((L, cb), lambda i: (0, i))
    return pl.pallas_call(
        body, grid=(nb,),
        in_specs=[blk, pl.BlockSpec((L, cb), lambda i: (0, i + nb)), pl.BlockSpec((L, cb), lambda i: (0, i + 2 * nb)),
                  blk, pl.BlockSpec((k, cb), lambda i: (0, i))],
        out_specs=[pl.BlockSpec((3, L, cb), lambda i: (0, 0, i)), pl.BlockSpec((8, cb), lambda i: (0, i))],
        out_shape=[jax.ShapeDtypeStruct((3, L, D), BF16), jax.ShapeDtypeStruct((8, D), F32)],
        compiler_params=_params(("parallel",)), name=name)(proj, proj, proj, dyv, w)


def _pieces(v, n):
    out, rest = [], v
    for _ in range(n):
        out.append(rest.astype(BF16))
        rest = rest - out[-1].astype(F32)
    return out


def _cumsum_rows(mask, v):
    m = mask.astype(BF16)
    return _dot(jnp.concatenate([m, m, m], axis=1), jnp.concatenate(_pieces(v, 3), axis=0))


def _ssd_chunk_terms(dtr, prm):
    lane = lax.broadcasted_iota(jnp.int32, (CH, LANES), 1)
    valid = lane < NH
    xdt = dtr + prm[0:1, :]
    dt = jnp.where(valid, jnp.maximum(xdt, 0.0) + jnp.log1p(jnp.exp(-jnp.abs(xdt))), 0.0)
    A = -jnp.exp(prm[1:2, :])
    ri = lax.broadcasted_iota(jnp.int32, (CH, CH), 0)
    ci = lax.broadcasted_iota(jnp.int32, (CH, CH), 1)
    cs = _cumsum_rows(ri >= ci, dt * A)
    last = cs[CH - 1:CH, :]
    spread = (lax.broadcasted_iota(jnp.int32, (2 * LANES, DI), 1) // HP
              == lax.broadcasted_iota(jnp.int32, (2 * LANES, DI), 0) % LANES).astype(BF16)
    gather = ((lax.broadcasted_iota(jnp.int32, (LANES, 2 * DI), 1) % DI) // HP
              == lax.broadcasted_iota(jnp.int32, (LANES, 2 * DI), 0)).astype(BF16)
    return dict(valid=valid, xdt=xdt, dt=dt, A=A, cs=cs, csT=cs.T, last=last, ri=ri, ci=ci, ex=(spread, gather))


def _expand(v, ex):
    if v.shape[0] == 1:
        return _expand(jnp.broadcast_to(v, (8, LANES)), ex)[0:1, :]
    return _dot(jnp.concatenate(_pieces(v, 2), axis=1), ex[0])


def _head_sum(v, ex):
    if v.shape[0] == 1:
        return _head_sum(jnp.broadcast_to(v, (8, DI)), ex)[0:1, :]
    return _dot_nt(jnp.concatenate(_pieces(v, 2), axis=1), ex[1])


def _ssd_fwd(xbc, dtr, prm, *, name):
    L = xbc.shape[0]
    nc = L // CH

    def body(xbc_ref, dtr_ref, prm_ref, y_ref, sp_ref, st_ref):
        @pl.when(pl.program_id(0) == 0)
        def _():
            st_ref[...] = jnp.zeros_like(st_ref)

        prm_v = prm_ref[...]
        t = _ssd_chunk_terms(dtr_ref[...], prm_v)
        cs, csT, ex, causal = t["cs"], t["csT"], t["ex"], t["ri"] >= t["ci"]
        xs = xbc_ref[:, 0:DI].astype(F32)
        X = xs * _expand(t["dt"], ex)
        Xb = X.astype(BF16)
        Xd = (X * _expand(jnp.exp(t["last"] - cs), ex)).astype(BF16)
        Ex = _expand(jnp.exp(cs), ex)
        cdx = _expand(jnp.exp(t["last"]), ex)
        dskx = _expand(prm_v[2:3, :], ex)
        lane = lax.broadcasted_iota(jnp.int32, (CH, LANES), 1)
        sp_ref[0] = st_ref[...]
        for g in range(NG):
            Bg = xbc_ref[:, DI + g * NS:DI + (g + 1) * NS].astype(BF16)
            Cg = xbc_ref[:, DI + GW + g * NS:DI + GW + (g + 1) * NS].astype(BF16)
            G = _dot_nt(Cg, Bg)
            Sg = st_ref[:, g * GW:(g + 1) * GW]
            yoff = _dot(Cg, Sg.astype(BF16)) * Ex[:, g * GW:(g + 1) * GW]
            for j in range(GW // LANES):
                lo = g * GW + j * LANES
                Xp = Xb[:, lo:lo + LANES]
                yd = []
                for h in (lo // HP, lo // HP + 1):
                    seg = cs[:, h:h + 1] - csT[h:h + 1, :]
                    yd.append(_dot((G * jnp.where(causal, jnp.exp(seg), 0.0)).astype(BF16), Xp))
                y_ref[:, lo:lo + LANES] = (jnp.where(lane < HP, yd[0], yd[1]) + yoff[:, j * LANES:(j + 1) * LANES]
                                           + dskx[:, lo:lo + LANES] * xs[:, lo:lo + LANES]).astype(BF16)
            st_ref[:, g * GW:(g + 1) * GW] = Sg * cdx[:, g * GW:(g + 1) * GW] + _dot_tn(Bg, Xd[:, g * GW:(g + 1) * GW])

    return pl.pallas_call(
        body, grid=(nc,),
        in_specs=[pl.BlockSpec((CH, CONVD), lambda c: (c, 0)), pl.BlockSpec((CH, LANES), lambda c: (c, 0)),
                  pl.BlockSpec((8, LANES), lambda c: (0, 0))],
        out_specs=[pl.BlockSpec((CH, DI), lambda c: (c, 0)), pl.BlockSpec((1, NS, DI), lambda c: (c, 0, 0))],
        out_shape=[jax.ShapeDtypeStruct((L, DI), BF16), jax.ShapeDtypeStruct((nc, NS, DI), F32)],
        scratch_shapes=[pltpu.VMEM((NS, DI), F32)],
        compiler_params=_params(("arbitrary",)), name=name)(xbc, dtr, prm)


def _ssd_bwd(xbc, dtr, prm, dy, sprev, *, name):
    L = xbc.shape[0]
    nc = L // CH

    def body(xbc_ref, dtr_ref, prm_ref, dy_ref, sp_ref, dxbc_ref, ddtr_ref, s_ref, dst_ref, dx_scr, de_scr, dd_scr):
        step = pl.program_id(0)

        @pl.when(step == 0)
        def _():
            dst_ref[...] = jnp.zeros_like(dst_ref)
            s_ref[...] = jnp.zeros_like(s_ref)

        prm_v = prm_ref[...]
        t = _ssd_chunk_terms(dtr_ref[...], prm_v)
        cs, csT, ex, ri, ci = t["cs"], t["csT"], t["ex"], t["ri"], t["ci"]
        E = jnp.exp(cs)
        dec = jnp.exp(t["last"] - cs)
        cd = jnp.exp(t["last"])
        xs = xbc_ref[:, 0:DI].astype(F32)
        dtx = _expand(t["dt"], ex)
        X = xs * dtx
        Xb = X.astype(BF16)
        decx = _expand(dec, ex)
        Xd = (X * decx).astype(BF16)
        Ex = _expand(E, ex)
        cdx = _expand(cd, ex)
        dskx = _expand(prm_v[2:3, :], ex)
        lane = lax.broadcasted_iota(jnp.int32, (CH, LANES), 1)
        dcs = jnp.zeros((CH, LANES), F32)
        dcd_x = []
        for g in range(NG):
            gs = slice(g * GW, (g + 1) * GW)
            Bg = xbc_ref[:, DI + g * NS:DI + (g + 1) * NS].astype(BF16)
            Cg = xbc_ref[:, DI + GW + g * NS:DI + GW + (g + 1) * NS].astype(BF16)
            G = _dot_nt(Cg, Bg)
            GT = _dot_nt(Bg, Cg)
            Sg = sp_ref[0, :, gs]
            Sgb = Sg.astype(BF16)
            dyg = dy_ref[:, gs]
            de_scr[:, gs] = dyg * _dot(Cg, Sgb)
            dYo = (Ex[:, gs] * dyg).astype(BF16)
            dC = _dot_nt(dYo, Sgb)
            dS_in = _dot_tn(Cg, dYo)
            dStg = dst_ref[:, gs]
            dStb = dStg.astype(BF16)
            dXd = _dot(Bg, dStb)
            dB = _dot_nt(Xd[:, gs], dStb)
            dd_scr[:, gs] = dXd * X[:, gs]
            dXst = dXd * decx[:, gs]
            dG = jnp.zeros((CH, CH), F32)
            dGT = jnp.zeros((CH, CH), F32)
            for j in range(GW // LANES):
                lo = g * GW + j * LANES
                Xp = Xb[:, lo:lo + LANES]
                dyp = dy_ref[:, lo:lo + LANES]
                dXp = dXst[:, j * LANES:(j + 1) * LANES]
                for k, h in enumerate((lo // HP, lo // HP + 1)):
                    dyh = jnp.where((lane < HP) if k == 0 else (lane >= HP), dyp, 0.0).astype(BF16)
                    seg = cs[:, h:h + 1] - csT[h:h + 1, :]
                    Lm = jnp.where(ri >= ci, jnp.exp(seg), 0.0)
                    LmT = jnp.where(ci >= ri, jnp.exp(-seg), 0.0)
                    dM = _dot_nt(dyh, Xp)
                    dMT = _dot_nt(Xp, dyh)
                    MT = GT * LmT
                    rs = jnp.sum(dM * (G * Lm), axis=1, keepdims=True) - jnp.sum(dMT * MT, axis=1, keepdims=True)
                    dcs = dcs + jnp.where(lane == h, rs, 0.0)
                    dG = dG + dM * Lm
                    dGT = dGT + dMT * LmT
                    dXp = dXp + _dot(MT.astype(BF16), dyh)
                dx_scr[:, lo:lo + LANES] = dXp
            dxbc_ref[:, DI + g * NS:DI + (g + 1) * NS] = (dB + _dot(dGT.astype(BF16), Cg)).astype(BF16)
            dxbc_ref[:, DI + GW + g * NS:DI + GW + (g + 1) * NS] = (dC + _dot(dG.astype(BF16), Bg)).astype(BF16)
            dcd_x.append(jnp.sum(dStg * Sg, axis=0, keepdims=True))
            dst_ref[:, gs] = dStg * cdx[:, gs] + dS_in
        dX = dx_scr[...]
        dy = dy_ref[...]
        ddec = _head_sum(dd_scr[...], ex)
        dcd = _head_sum(jnp.concatenate(dcd_x, axis=1), ex)
        dcs = dcs + _head_sum(de_scr[...], ex) * E - ddec * dec
        row = lax.broadcasted_iota(jnp.int32, (CH, LANES), 0)
        dcs = dcs + jnp.where(row == CH - 1, jnp.sum(ddec * dec, axis=0, keepdims=True) + dcd * cd, 0.0)
        da = _cumsum_rows(ci >= ri, dcs)
        ddt = da * t["A"] + _head_sum(dX * xs, ex)
        ddtr = jnp.where(t["valid"], ddt * _sigmoid(t["xdt"]), 0.0)
        ddtr_ref[...] = ddtr
        dxbc_ref[:, 0:DI] = (dX * dtx + dskx * dy).astype(BF16)
        s_ref[0:1, :] += jnp.sum(da * t["dt"], axis=0, keepdims=True)
        s_ref[1:2, :] += _head_sum(jnp.sum(dy * xs, axis=0, keepdims=True), ex)
        s_ref[2:3, :] += jnp.sum(ddtr, axis=0, keepdims=True)

        @pl.when(step == nc - 1)
        def _():
            s_ref[0:1, :] = s_ref[0:1, :] * t["A"]

    rev = lambda c: (nc - 1 - c, 0)
    return pl.pallas_call(
        body, grid=(nc,),
        in_specs=[pl.BlockSpec((CH, CONVD), rev), pl.BlockSpec((CH, LANES), rev), pl.BlockSpec((8, LANES), lambda c: (0, 0)),
                  pl.BlockSpec((CH, DI), rev), pl.BlockSpec((1, NS, DI), lambda c: (nc - 1 - c, 0, 0))],
        out_specs=[pl.BlockSpec((CH, CONVD), rev), pl.BlockSpec((CH, LANES), rev), pl.BlockSpec((8, LANES), lambda c: (0, 0))],
        out_shape=[jax.ShapeDtypeStruct((L, CONVD), BF16), jax.ShapeDtypeStruct((L, LANES), F32),
                   jax.ShapeDtypeStruct((8, LANES), F32)],
        scratch_shapes=[pltpu.VMEM((NS, DI), F32), pltpu.VMEM((CH, DI), F32), pltpu.VMEM((CH, DI), F32),
                        pltpu.VMEM((CH, DI), F32)],
        compiler_params=_params(("arbitrary",)), name=name)(xbc, dtr, prm, dy, sprev)


def _gnorm_fwd(y, zx, nw, *, name):
    L = y.shape[0]
    tm = min(L, 256)

    def body(y_ref, z_ref, nw_ref, o_ref):
        z = z_ref[...].astype(F32)
        yg = y_ref[...].astype(F32) * (z * _sigmoid(z))
        for g in range(NG):
            v = yg[:, g * GW:(g + 1) * GW]
            r = lax.rsqrt(jnp.mean(v * v, axis=-1, keepdims=True) + EPS)
            o_ref[:, g * GW:(g + 1) * GW] = (v * r * nw_ref[:, g * GW:(g + 1) * GW]).astype(BF16)

    row = pl.BlockSpec((tm, DI), lambda i: (i, 0))
    return pl.pallas_call(body, grid=(L // tm,), in_specs=[row, row, pl.BlockSpec((1, DI), lambda i: (0, 0))],
                          out_specs=row, out_shape=jax.ShapeDtypeStruct((L, DI), BF16),
                          compiler_params=_params(("parallel",)), name=name)(y, zx, nw)


def _gnorm_bwd(y, zx, nw, dyn, *, name):
    L = y.shape[0]
    tm = min(L, 256)

    def body(y_ref, z_ref, nw_ref, dyn_ref, dy_ref, dz_ref, s_ref):
        @pl.when(pl.program_id(0) == 0)
        def _():
            s_ref[...] = jnp.zeros_like(s_ref)

        z, yv = z_ref[...].astype(F32), y_ref[...].astype(F32)
        sz = _sigmoid(z)
        gate = z * sz
        dgate_dz = sz * (1.0 + z * (1.0 - sz))
        for g in range(NG):
            gs = slice(g * GW, (g + 1) * GW)
            v = yv[:, gs] * gate[:, gs]
            r = lax.rsqrt(jnp.mean(v * v, axis=-1, keepdims=True) + EPS)
            vhat = v * r
            dn = dyn_ref[:, gs]
            s_ref[0:1, gs] += jnp.sum(dn * vhat, axis=0, keepdims=True)
            dvhat = dn * nw_ref[:, gs]
            dv = r * (dvhat - vhat * jnp.mean(dvhat * vhat, axis=-1, keepdims=True))
            dy_ref[:, gs] = dv * gate[:, gs]
            dz_ref[:, gs] = (dv * yv[:, gs] * dgate_dz[:, gs]).astype(BF16)

    row = pl.BlockSpec((tm, DI), lambda i: (i, 0))
    return pl.pallas_call(body, grid=(L // tm,), in_specs=[row, row, pl.BlockSpec((1, DI), lambda i: (0, 0)), row],
                          out_specs=[row, row, pl.BlockSpec((8, DI), lambda i: (0, 0))],
                          out_shape=[jax.ShapeDtypeStruct((L, DI), F32), jax.ShapeDtypeStruct((L, ZX), BF16),
                                     jax.ShapeDtypeStruct((8, DI), F32)],
                          compiler_params=_params(("arbitrary",)), name=name)(y, zx, nw, dyn)


def _adamw(w, g, m, v, *, name, g_row=0, w_row=0, rows=None, into=None, emit_g=False):
    lead = w.ndim == 3
    R, C = w.shape[-2:]
    rows = R if rows is None else rows
    tr = max([t for t in range(8, rows + 1, 8) if rows % t == 0 and t * C <= 256 * 1024], default=rows)
    assert g_row % tr == 0 and w_row % tr == 0, (name, g_row, w_row, tr)
    n_out = 4 if emit_g else 3

    def body(w_ref, g_ref, m_ref, v_ref, *rest):
        outs = rest[-n_out:]
        gv = g_ref[...]
        mn = ADAM_B1 * m_ref[...] + (1.0 - ADAM_B1) * gv
        vn = ADAM_B2 * v_ref[...] + (1.0 - ADAM_B2) * (gv * gv)
        m_hat = mn / (1.0 - ADAM_B1 ** ADAM_STEP)
        v_hat = vn / (1.0 - ADAM_B2 ** ADAM_STEP)
        d_ref, mo_ref, vo_ref = outs[-3:]
        d_ref[...] = -ADAM_LR * (m_hat / (jnp.sqrt(v_hat) + ADAM_EPS) + ADAM_WD * w_ref[...])
        mo_ref[...] = mn
        vo_ref[...] = vn
        if emit_g:
            outs[0][...] = gv

    blk = (pl.BlockSpec((None, tr, C), lambda i: (0, i + w_row // tr, 0)) if lead
           else pl.BlockSpec((tr, C), lambda i: (i + w_row // tr, 0)))
    args, in_specs, alias = [w, g, m, v], [blk, pl.BlockSpec((tr, C), lambda i: (i + g_row // tr, 0)), blk, blk], {}
    if into is not None:
        args, in_specs, alias = args + list(into), in_specs + [ANY] * n_out, {4 + k: k for k in range(n_out)}
    return pl.pallas_call(body, grid=(rows // tr,), in_specs=in_specs, out_specs=[blk] * n_out,
                          out_shape=[jax.ShapeDtypeStruct(w.shape, F32)] * n_out, input_output_aliases=alias,
                          compiler_params=_params(("parallel",)), name=name)(*args)


def _residual(acc, xv, gv):
    return xv + gv * acc, acc


def _like(buf):
    return jax.ShapeDtypeStruct(buf.shape, buf.dtype)


def _mlp_fwd(x, mod, nw, wb, up_row, down_row, tag):
    sh, sc, g = mod
    h = _modnorm_fwd(x, nw, sc, sh, name=tag + "_norm")
    a = _matmul(h, wb, n=DFF, tm=TM_ALL, b_spec=pl.BlockSpec((None, D, 512), lambda mi, j: (j // 2, up_row // D, j % 2)),
                epi=lambda acc: (jnp.maximum(acc, 0.0),), out_dtypes=(BF16,), name=tag + "_up")
    xn, y = _matmul(a, wb, n=D, tm=TM_HALF, contract=_nn_split_sq,
                    b_spec=pl.BlockSpec((N_CHIPS, D, 512), lambda mi, j: (0, down_row // D, j)),
                    extras=(x, g), epi=_residual, out_dtypes=(F32, F32), name=tag + "_down")
    return xn, (x, h, a, y)


def _mlp_bwd(dxo, saved, mod, nw, wb, gb, up_row, down_row, tag):
    x, h, a, y = saved
    sh, sc, g = mod
    dy, gsum = _gate_bwd(dxo, y, g, name=tag + "_dgate")
    du = _matmul(dy, wb, n=DFF, tm=TM_ALL, contract=_nt,
                 b_spec=pl.BlockSpec((None, 512, D), lambda mi, j: (j // 2, down_row // 512 + j % 2, 0)),
                 extras=(a,), epi=lambda acc, av: (acc * (2.0 * av.astype(F32)),), out_dtypes=(BF16,), name=tag + "_dact")
    gb = _matmul_tn(a, dy, m=DFF, n=D, tm=D, tn=D, a_square=True, into=gb, out_struct=_like(wb),
                    out_spec=pl.BlockSpec((None, D, D), lambda mi, j: (mi, down_row // D, 0)), name=tag + "_ddown")
    dh = _matmul(du, wb, n=D, tm=TM_HALF, contract=_nt_split,
                 b_spec=pl.BlockSpec((N_CHIPS, 512, D), lambda mi, j: (0, up_row // 512 + j, 0)), name=tag + "_dh")
    gb = _matmul_tn(h, du, m=D, n=DFF, tm=D, into=gb, out_struct=_like(wb),
                    out_spec=pl.BlockSpec((None, D, 512), lambda mi, j: (j // 2, up_row // D, j % 2)), name=tag + "_dup")
    dx, sums = _modnorm_bwd(x, dh, dxo, nw, sc, gsum, name=tag + "_dnorm")
    return dx, gb, sums


def _ssd_fwd_scan(x, mod, nw, w_zx, w_dt, conv_w, conv_b, prm, tag):
    sh, sc, g = mod
    h = _modnorm_fwd(x, nw, sc, sh, name=tag + "_norm")
    zx = _matmul(h, w_zx, n=ZX, tm=TM_ALL, out_dtypes=(BF16,), name=tag + "_in")
    dtr = _matmul(h, w_dt, n=LANES, tm=TM_ALL, name=tag + "_in_dt")
    xbc = _ssd_conv_fwd(zx, conv_w, conv_b, name=tag + "_conv")
    y, sprev = _ssd_fwd(xbc, dtr, prm, name=tag + "_scan")
    return h, zx, dtr, xbc, y, sprev


def _ssd_fwd_out(x, mod, scan, gn_w, w_out, tag):
    sh, sc, g = mod
    h, zx, dtr, xbc, y, sprev = scan
    yn = _gnorm_fwd(y, zx, gn_w, name=tag + "_gnorm")
    xn, yo = _matmul(yn, w_out, n=D, tm=TM_HALF, contract=_nn_split,
                     b_spec=pl.BlockSpec((N_CHIPS, 512, 512), lambda mi, j: (0, 0, j)),
                     extras=(x, g), epi=_residual, out_dtypes=(F32, F32), name=tag + "_out")
    return xn, (x, h, zx, dtr, xbc, y, sprev, yn, yo)


def _ssd_bwd_out(dxo, saved, mod, w_out, tag):
    x, h, zx, dtr, xbc, y, sprev, yn, yo = saved
    sh, sc, g = mod
    dyo, gsum = _gate_bwd(dxo, yo, g, name=tag + "_dgate")
    dyn = _matmul(dyo, w_out, n=DI, tm=TM_ALL, contract=_nt, b_spec=pl.BlockSpec((None, 512, D), lambda mi, j: (j, 0, 0)),
                  name=tag + "_dyn")
    g_out = _matmul_tn(yn, dyo, m=DI, n=D, tn=D, out_struct=_like(w_out),
                       out_spec=pl.BlockSpec((None, 512, D), lambda mi, j: (mi, 0, 0)), name=tag + "_dout")
    return dyn, g_out, gsum


def _ssd_bwd_rest(dxo, dy, dzx, gsum, saved, mod, nw, w_zx, w_dt, conv_w, conv_b, prm, tag):
    x, h, zx, dtr, xbc, y, sprev, yn, yo = saved
    sh, sc, g = mod
    dxbc, ddtr, ssum = _ssd_bwd(xbc, dtr, prm, dy, sprev, name=tag + "_dscan")
    dzx, csum = _ssd_conv_bwd(zx, dxbc, conv_w, conv_b, dzx, name=tag + "_dconv")
    dh_dt = _matmul(ddtr, w_dt, n=D, tm=TM_ALL, contract=_nt, name=tag + "_dh_dt")
    dh = _matmul(dzx, w_zx, n=D, tm=TM_HALF, contract=_nt, extras=(dh_dt,), epi=lambda acc, e: (acc + e,), name=tag + "_dh")
    d_w_zx = _matmul_tn(h, dzx, m=D, n=ZX, tm=D, name=tag + "_din")
    d_w_dt = _matmul_tn(h, ddtr, m=D, n=LANES, tm=D, name=tag + "_din_dt")
    dx, sums = _modnorm_bwd(x, dh, dxo, nw, sc, gsum, name=tag + "_dnorm")
    return dx, d_w_zx, d_w_dt, sums, csum, ssum


def _sc_layer_fwd(x, mod, nw, w_sc_in, conv_w, wb, out_row, tag):
    sh, sc, g = mod
    h = _modnorm_fwd(x, nw, sc, sh, name=tag + "_norm")
    proj = _matmul(h, w_sc_in, n=3 * D, tm=TM_ALL, tn=256, b_spec=pl.BlockSpec((None, D, 256), lambda mi, j: (j // 3, 0, j % 3)),
                   name=tag + "_in")
    yv = _sc_fwd(proj, conv_w, name=tag + "_conv")
    xn, yo = _matmul(yv, wb, n=D, tm=TM_HALF, contract=_nn_split,
                     b_spec=pl.BlockSpec((N_CHIPS, 256, 512), lambda mi, j: (0, out_row // 256, j)),
                     extras=(x, g), epi=_residual, out_dtypes=(F32, F32), name=tag + "_out")
    return xn, (x, h, proj, yv, yo)


def _sc_layer_bwd(dxo, saved, mod, nw, w_sc_in, conv_w, wb, gb, out_row, tag):
    x, h, proj, yv, yo = saved
    sh, sc, g = mod
    L = x.shape[0]
    dyo, gsum = _gate_bwd(dxo, yo, g, name=tag + "_dgate")
    dyv = _matmul(dyo, wb, n=D, tm=TM_ALL, tn=256, contract=_nt,
                  b_spec=pl.BlockSpec((None, 256, D), lambda mi, j: (j, out_row // 256, 0)), name=tag + "_dyv")
    gb = _matmul_tn(yv, dyo, m=D, n=D, tm=256, tn=D, into=gb, out_struct=_like(wb),
                    out_spec=pl.BlockSpec((None, 256, D), lambda mi, j: (mi, out_row // 256, 0)), name=tag + "_dout")
    dproj, csum = _sc_bwd(proj, dyv, conv_w, name=tag + "_dconv")
    tm = min(L, TM_HALF)
    dh = _matmul(dproj, w_sc_in, n=D, tm=tm, contract=_nt_sc_in, a_spec=pl.BlockSpec((3, tm, D), lambda mi, j: (0, mi, 0)),
                 b_spec=pl.BlockSpec((N_CHIPS, 512, SC_IN_SHARD), lambda mi, j: (0, j, 0)), name=tag + "_dh")
    g_sc_in = _matmul_tn(h, dproj, m=D, n=3 * D, tm=D, tn=256, b_spec=pl.BlockSpec((None, L, 256), lambda mi, j: (j // 4, 0, j % 4)),
                         out_spec=pl.BlockSpec((None, D, 256), lambda mi, j: (j // 3, 0, j % 3)),
                         out_struct=jax.ShapeDtypeStruct((N_CHIPS, D, SC_IN_SHARD), BF16), name=tag + "_din")
    dx, sums = _modnorm_bwd(x, dh, dxo, nw, sc, gsum, name=tag + "_dnorm")
    return dx, gb, g_sc_in, sums, csum


SUB_ROW = (0, 8, 16, 24)
SSD_CONV_ROW, GNORM_ROW, FINAL_ROW, SC_CONV_ROW, HEAD_ROW, SMALL_ROWS = 32, 56, 72, 80, 88, 96


def _all_gather_rows(blk, *, name):
    m_per, n = blk.shape

    def body(x_ref, out_ref, send_sems, recv_sems, local_sem):
        x, y, c = lax.axis_index("x"), lax.axis_index("y"), lax.axis_index("c")
        me, sibling = (x, y, c), (x, y, 1 - c)
        chips = [(1 - x, y), (x, 1 - y), (1 - x, 1 - y)]

        def rows(px, py, pc):
            return out_ref.at[pl.ds((4 * px + 2 * py + pc) * m_per, m_per), :]

        def copy(k, block, to, src=None):
            return pltpu.make_async_remote_copy(src_ref=rows(*block) if src is None else src, dst_ref=rows(*block),
                                                send_sem=send_sems.at[k], recv_sem=recv_sems.at[k], device_id=to,
                                                device_id_type=MESH)

        mine = pltpu.make_async_copy(x_ref, rows(*me), local_sem)
        mine.start()
        first = [copy(0, me, sibling, src=x_ref)] + [copy(1 + j, me, (*chip, c), src=x_ref) for j, chip in enumerate(chips)]
        for cp in first:
            cp.start()
        passed = [copy(4 + j, (*chip, c), sibling) for j, chip in enumerate(chips)]
        for j, chip in enumerate(chips):
            copy(1 + j, (*chip, c), me).wait_recv()
            passed[j].start()
        copy(0, sibling, me).wait_recv()
        for j, chip in enumerate(chips):
            copy(4 + j, (*chip, 1 - c), me).wait_recv()
        for cp in first + passed:
            cp.wait_send()
        mine.wait()

    return pl.pallas_call(
        body, out_shape=jax.ShapeDtypeStruct((N_DEV * m_per, n), blk.dtype),
        in_specs=[pl.BlockSpec(memory_space=pltpu.VMEM)], out_specs=pl.BlockSpec(memory_space=pltpu.VMEM),
        scratch_shapes=[pltpu.SemaphoreType.DMA((7,)), pltpu.SemaphoreType.DMA((7,)), pltpu.SemaphoreType.DMA],
        name=name)(blk)


def _half(ref, chip, c):
    hr = ref.shape[1] // 2
    return ref.at[chip, pl.ds(c * hr, hr), :]


def _gather_copy(bufs, sends, recvs, b, k, chip, pc, to):
    piece = _half(bufs[b], 2 * chip[0] + chip[1], pc)
    return pltpu.make_async_remote_copy(src_ref=piece, dst_ref=piece, send_sem=sends.at[4 * b + k], recv_sem=recvs.at[4 * b + k],
                                        device_id=to, device_id_type=MESH)


def _split_call(body, bufs, sems_in, n_sems, *, name, after=(), token=False):
    nb, na, starts = len(bufs), len(after), not sems_in

    def wrapped(*refs):
        sems = refs[nb + na:nb + na + 2] if starts else refs[nb:nb + 2]
        body(refs[:nb], sems[0], sems[1])
        if token:
            refs[-1][...] = jnp.zeros_like(refs[-1])

    out_shape = [pltpu.SemaphoreType.DMA((n_sems,)) for _ in range(2 if starts else 0)]
    out_specs = [SEM] * len(out_shape) + [ANY] * nb
    alias = {b: len(out_shape) + b for b in range(nb)}
    out_shape += [jax.ShapeDtypeStruct(b.shape, b.dtype) for b in bufs]
    if token:
        out_shape.append(jax.ShapeDtypeStruct((8, LANES), F32))
        out_specs.append(pl.BlockSpec(memory_space=pltpu.VMEM))
    return pl.pallas_call(
        wrapped, out_shape=out_shape, in_specs=[ANY] * nb + [SEM] * len(sems_in) + [ANY] * na, out_specs=out_specs,
        input_output_aliases=alias,
        compiler_params=pltpu.CompilerParams(has_side_effects=pltpu.SideEffectType.DATAFLOW_SIDE_EFFECTING),
        name=name)(*bufs, *sems_in, *after)


def _gather_start(bufs, *, name, after=()):
    nb = len(bufs)

    def body(ins, sends, recvs):
        x, y, c = lax.axis_index("x"), lax.axis_index("y"), lax.axis_index("c")
        chips = [(1 - x, y), (x, 1 - y), (1 - x, 1 - y)]
        for b in range(nb):
            _gather_copy(ins, sends, recvs, b, 0, (x, y), c, (x, y, 1 - c)).start()
            for j, chip in enumerate(chips):
                _gather_copy(ins, sends, recvs, b, 1 + j, (x, y), c, (*chip, c)).start()

    out = _split_call(body, bufs, (), 4 * nb, name=name, after=after, token=True)
    return (out[0], out[1], out[2:2 + nb]), out[-1]


def _gather_wait_first(flight, *, name, after=()):
    sends, recvs, bufs = flight
    nb = len(bufs)

    def body(ins, sends_, recvs_):
        x, y, c = lax.axis_index("x"), lax.axis_index("y"), lax.axis_index("c")
        chips = [(1 - x, y), (x, 1 - y), (1 - x, 1 - y)]
        for b in range(nb):
            _gather_copy(ins, sends_, recvs_, b, 0, (x, y), c, (x, y, 1 - c)).wait_send()
            _gather_copy(ins, sends_, recvs_, b, 0, (x, y), 1 - c, (x, y, c)).wait_recv()
            for j, chip in enumerate(chips):
                _gather_copy(ins, sends_, recvs_, b, 1 + j, (x, y), c, (*chip, c)).wait_send()
                _gather_copy(ins, sends_, recvs_, b, 1 + j, chip, c, (x, y, c)).wait_recv()

    return _split_call(body, bufs, (sends, recvs), 4 * nb, name=name, after=after)


def _gather_forward(bufs, *, name):
    nb = len(bufs)

    def body(ins, sends, recvs):
        x, y, c = lax.axis_index("x"), lax.axis_index("y"), lax.axis_index("c")
        chips = [(1 - x, y), (x, 1 - y), (1 - x, 1 - y)]
        for b in range(nb):
            for j, chip in enumerate(chips):
                _gather_copy(ins, sends, recvs, b, 1 + j, chip, c, (x, y, 1 - c)).start()

    out = _split_call(body, bufs, (), 4 * nb, name=name)
    return out[0], out[1], out[2:2 + nb]


def _gather_wait_forward(flight, *, name, after=()):
    sends, recvs, bufs = flight
    nb = len(bufs)

    def body(ins, sends_, recvs_):
        x, y, c = lax.axis_index("x"), lax.axis_index("y"), lax.axis_index("c")
        chips = [(1 - x, y), (x, 1 - y), (1 - x, 1 - y)]
        for b in range(nb):
            for j, chip in enumerate(chips):
                _gather_copy(ins, sends_, recvs_, b, 1 + j, chip, c, (x, y, 1 - c)).wait_send()
                _gather_copy(ins, sends_, recvs_, b, 1 + j, chip, 1 - c, (x, y, c)).wait_recv()

    return _split_call(body, bufs, (sends, recvs), 4 * nb, name=name, after=after)


def _owner_copies(hs, lands, sends, recvs):
    x, y, c = lax.axis_index("x"), lax.axis_index("y"), lax.axis_index("c")
    chips = [(1 - x, y), (x, 1 - y), (1 - x, 1 - y)]
    return [pltpu.make_async_remote_copy(src_ref=hs[b].at[2 * cx + cy], dst_ref=lands[b].at[j], send_sem=sends.at[3 * b + j],
                                         recv_sem=recvs.at[3 * b + j], device_id=(cx, cy, c), device_id_type=MESH)
            for b in range(len(hs)) for j, (cx, cy) in enumerate(chips)]


def _owners_start(hs, *, name):
    nb = len(hs)
    lands = [lax.empty((3,) + h.shape[1:], h.dtype) for h in hs]

    def body(refs, sends, recvs):
        for cp in _owner_copies(refs[:nb], refs[nb:], sends, recvs):
            cp.start()

    out = _split_call(body, list(hs) + lands, (), 3 * nb, name=name, token=True)
    return (out[0], out[1], out[2:2 + 2 * nb]), out[-1]


def _owners_wait(flight, *, name, after=()):
    sends, recvs, bufs = flight
    nb = len(bufs) // 2

    def body(refs, sends_, recvs_):
        for cp in _owner_copies(refs[:nb], refs[nb:], sends_, recvs_):
            cp.wait()

    return _split_call(body, bufs, (sends, recvs), 3 * nb, name=name, after=after)[nb:]


def _sibling_copies(gs, lands, sends, recvs):
    x, y, c = lax.axis_index("x"), lax.axis_index("y"), lax.axis_index("c")
    copies = []
    for b in range(len(gs)):
        hr = gs[b].shape[1] // 2
        copies.append(pltpu.make_async_remote_copy(
            src_ref=gs[b].at[:, pl.ds((1 - c) * hr, hr), :], dst_ref=lands[b], send_sem=sends.at[b], recv_sem=recvs.at[b],
            device_id=(x, y, 1 - c), device_id_type=MESH))
    return copies


def _sibling_start(gs, *, name, after=()):
    nb = len(gs)
    lands = [lax.empty((g.shape[0], g.shape[1] // 2, g.shape[2]), g.dtype) for g in gs]

    def body(refs, sends, recvs):
        for cp in _sibling_copies(refs[:nb], refs[nb:], sends, recvs):
            cp.start()

    out = _split_call(body, list(gs) + lands, (), nb, name=name, after=after, token=True)
    return (out[0], out[1], out[2:2 + 2 * nb]), out[-1]


def _sibling_wait(flight, *, name, after=()):
    sends, recvs, bufs = flight
    nb = len(bufs) // 2

    def body(refs, sends_, recvs_):
        for cp in _sibling_copies(refs[:nb], refs[nb:], sends_, recvs_):
            cp.wait()

    out = _split_call(body, bufs, (sends, recvs), nb, name=name, after=after)
    return out[:nb], out[nb:]


def _result_copies(ts, sends, recvs):
    x, y, c = lax.axis_index("x"), lax.axis_index("y"), lax.axis_index("c")
    return [pltpu.make_async_remote_copy(src_ref=ts[b].at[c], dst_ref=ts[b].at[c], send_sem=sends.at[b], recv_sem=recvs.at[b],
                                         device_id=(x, y, 1 - c), device_id_type=MESH) for b in range(len(ts))]


def _result_start(ts, *, name):
    def body(refs, sends, recvs):
        for cp in _result_copies(refs, sends, recvs):
            cp.start()

    out = _split_call(body, ts, (), len(ts), name=name, token=True)
    return (out[0], out[1], out[2:2 + len(ts)]), out[-1]


def _result_wait(flight, *, name, after=()):
    sends, recvs, bufs = flight

    def body(refs, sends_, recvs_):
        for cp in _result_copies(refs, sends_, recvs_):
            cp.wait()

    return _split_call(body, bufs, (sends, recvs), len(bufs), name=name, after=after)


def _row_tile(rows, cols):
    best = 16
    for t in range(16, rows + 1, 16):
        if rows % t == 0 and t * cols <= 640 * 1024:
            best = t
    assert rows % best == 0, (rows, cols)
    return best


def _add_sibling_half(g, recv, core, *, name):
    nk, r, n = g.shape
    hr = r // 2
    tr = _row_tile(hr, n)

    def body(c_ref, a_ref, b_ref, o_ref):
        o_ref[...] = (a_ref[...].astype(F32) + b_ref[...].astype(F32)).astype(BF16)

    grid_spec = pltpu.PrefetchScalarGridSpec(
        num_scalar_prefetch=1, grid=(nk, hr // tr),
        in_specs=[pl.BlockSpec((None, tr, n), lambda k, i, c_ref: (k, c_ref[0] * (hr // tr) + i, 0)),
                  pl.BlockSpec((None, tr, n), lambda k, i, c_ref: (k, i, 0))],
        out_specs=pl.BlockSpec((None, tr, n), lambda k, i, c_ref: (k, i, 0)))
    return pl.pallas_call(body, grid_spec=grid_spec, out_shape=jax.ShapeDtypeStruct((nk, hr, n), BF16),
                          compiler_params=_params(("parallel", "parallel")), name=name)(core, g, recv)


def _add_chip_sums(h, recv, chip_core, *, name):
    _, hr, n = h.shape
    tr = _row_tile(hr, n)

    def body(k_ref, a_ref, b_ref, o_ref):
        o_ref[...] = ((a_ref[...].astype(F32) + b_ref[0].astype(F32)) + b_ref[1].astype(F32)) + b_ref[2].astype(F32)

    grid_spec = pltpu.PrefetchScalarGridSpec(
        num_scalar_prefetch=1, grid=(hr // tr,),
        in_specs=[pl.BlockSpec((None, tr, n), lambda i, k_ref: (k_ref[0], i, 0)),
                  pl.BlockSpec((3, tr, n), lambda i, k_ref: (0, i, 0))],
        out_specs=pl.BlockSpec((None, tr, n), lambda i, k_ref: (k_ref[1], i, 0)))
    return pl.pallas_call(body, grid_spec=grid_spec, out_shape=jax.ShapeDtypeStruct((2, hr, n), F32),
                          compiler_params=_params(("parallel",)), name=name)(chip_core, h, recv)


def _sum_devices(g, *, name):
    nd, r, n = g.shape

    def body(g_ref, o_ref):
        acc = g_ref[0]
        for i in range(1, nd):
            acc = acc + g_ref[i]
        o_ref[...] = acc

    return pl.pallas_call(body, out_shape=jax.ShapeDtypeStruct((r, n), F32), name=name)(g)


def _own_slot(shard, chip):
    return lax.dynamic_update_slice(jnp.zeros((N_CHIPS,) + shard.shape, BF16), shard[None], (chip, 0, 0))


def kernel(x, c, ada_w, ada_b, mix_norm_w, mlp_norm_w, mlp_up, mlp_down, ssd_in_w, ssd_conv_w, ssd_conv_b, ssd_dt_bias, ssd_A_log, ssd_D, ssd_norm_w, ssd_out_w, sc_in_w, sc_conv_w, sc_out_w, final_norm_w, loss_target, m_ada_w, m_ada_b, m_mix_norm_w, m_mlp_norm_w, m_mlp_up, m_mlp_down, m_ssd_in_w, m_ssd_conv_w, m_ssd_conv_b, m_ssd_dt_bias, m_ssd_A_log, m_ssd_D, m_ssd_norm_w, m_ssd_out_w, m_sc_in_w, m_sc_conv_w, m_sc_out_w, m_final_norm_w, v_ada_w, v_ada_b, v_mix_norm_w, v_mlp_norm_w, v_mlp_up, v_mlp_down, v_ssd_in_w, v_ssd_conv_w, v_ssd_conv_b, v_ssd_dt_bias, v_ssd_A_log, v_ssd_D, v_ssd_norm_w, v_ssd_out_w, v_sc_in_w, v_sc_conv_w, v_sc_out_w, v_final_norm_w):
    xi, yi, ci = lax.axis_index("x"), lax.axis_index("y"), lax.axis_index("c")
    chip = 2 * xi + yi
    dev = 2 * chip + ci
    n_ada = ada_w.shape[2]

    conv_flat = jnp.concatenate([ssd_conv_w.reshape(-1), sc_conv_w.reshape(-1), jnp.zeros((256,), F32)]).reshape(4, D)
    blk0 = jnp.concatenate([c, conv_flat, jnp.zeros((3, D), F32)], axis=0)
    got0 = _all_gather_rows(blk0, name="gather_cond").reshape(N_DEV, 8, D)
    c_all = got0[:, 0]
    conv_all = got0[0::2, 1:5].reshape(N_CHIPS, 4 * D)
    ssd_conv = jnp.moveaxis(conv_all[:, :4 * 768].reshape(N_CHIPS, 4, 768), 0, 1).reshape(4, CONVD)
    sc_conv = jnp.moveaxis(conv_all[:, 4 * 768:4 * 768 + 3 * 256].reshape(N_CHIPS, 3, 256), 0, 1).reshape(3, D)
    mod_shard = [_matmul(c_all, ada_w, n=n_ada, a_silu=True, b_spec=pl.BlockSpec((None, D, 512), lambda mi, j, i=i: (i, 0, j)),
                         extras=(lax.dynamic_slice(ada_b, (i, chip * n_ada), (1, n_ada)),),
                         epi=lambda acc, b: (acc + b,), name=f"ada_mod{i}") for i in range(2)]
    mod_all = _all_gather_rows(jnp.concatenate(mod_shard, axis=0), name="gather_mod")
    mod_all = mod_all.reshape(N_DEV, 2, N_DEV, n_ada)[0::2]
    mod = jnp.moveaxis(lax.dynamic_index_in_dim(mod_all, dev, axis=2, keepdims=False), 0, 1).reshape(2, 6, D)
    mods = [[mod[i, j:j + 1] for j in range(6)] for i in range(2)]

    bf = lambda v: v.astype(BF16)
    up_row, down_row, sc_out_row = 0, D, 2 * D
    a_bufs = [_own_slot(bf(ssd_in_w[0]), chip)]
    b_bufs = [_own_slot(bf(ssd_out_w[0]), chip), _own_slot(bf(jnp.concatenate([mlp_up[0], mlp_down[0]], axis=0)), chip)]
    c_bufs = [_own_slot(bf(sc_in_w[0]), chip), _own_slot(bf(jnp.concatenate([mlp_up[1], mlp_down[1], sc_out_w[0]], axis=0)), chip)]
    fly_a, tok = _gather_start(a_bufs, name="gather_a_start", after=(mod,))
    fly_b, tok = _gather_start(b_bufs, name="gather_b_start", after=(tok,))
    fly_c, tok = _gather_start(c_bufs, name="gather_c_start", after=(tok,))

    row = lambda v: v.reshape(1, -1)
    xs, tgt = x[0], loss_target[0]
    prm = jnp.pad(jnp.concatenate([ssd_dt_bias, ssd_A_log, ssd_D, jnp.zeros((5, NH), F32)], axis=0), ((0, 0), (0, LANES - NH)))
    mix_nw = [row(mix_norm_w[i]) for i in range(2)]
    mlp_nw = [row(mlp_norm_w[i]) for i in range(2)]
    a_bufs = _gather_wait_first(fly_a, name="gather_a_landed", after=(tok,))
    (w_ssd_in,) = _gather_wait_forward(_gather_forward(a_bufs, name="gather_a_pass"), name="gather_a_done")
    ssd_in_full = jnp.moveaxis(w_ssd_in, 0, 1).reshape(D, N_CHIPS * SSD_IN_SHARD)
    w_zx, w_dt = ssd_in_full[:, :ZX], jnp.pad(ssd_in_full[:, ZX:], ((0, 0), (0, LANES - NH)))
    scan = _ssd_fwd_scan(xs, mods[0][0:3], mix_nw[0], w_zx, w_dt, ssd_conv, ssd_conv_b, prm, "ssd")
    fly_b = _gather_forward(_gather_wait_first(fly_b, name="gather_b_landed", after=(scan[3],)), name="gather_b_pass")
    w_ssd_out, w_b = _gather_wait_forward(fly_b, name="gather_b_done", after=(scan[4],))
    x1, s_ssd = _ssd_fwd_out(xs, mods[0][0:3], scan, ssd_norm_w, w_ssd_out, "ssd")
    x2, s_mlp0 = _mlp_fwd(x1, mods[0][3:6], mlp_nw[0], w_b, up_row, down_row, "mlp0")
    c_bufs = _gather_wait_first(fly_c, name="gather_c_landed", after=(x2,))
    w_sc_in, w_c = _gather_wait_forward(_gather_forward(c_bufs, name="gather_c_pass"), name="gather_c_done")
    x3, s_sc = _sc_layer_fwd(x2, mods[1][0:3], mix_nw[1], w_sc_in, sc_conv, w_c, sc_out_row, "sc")
    x4, s_mlp1 = _mlp_fwd(x3, mods[1][3:6], mlp_nw[1], w_c, up_row, down_row, "mlp1")

    core = ci.reshape(1).astype(jnp.int32)
    chip_core = jnp.stack([chip, ci]).astype(jnp.int32)

    def reduce_swap(gbufs, tag, after=()):
        return _sibling_start(gbufs, name=tag + "_sibling_start", after=after)

    def reduce_send(flight, tag, after):
        gs, sib = _sibling_wait(flight, name=tag + "_sibling_landed", after=after)
        hs = [_add_sibling_half(g, s, core, name=f"{tag}_add_sibling{b}") for b, (g, s) in enumerate(zip(gs, sib))]
        return _owners_start(hs, name=tag + "_owners_start")

    def reduce_sum(flight, tag, after):
        nb = len(flight[2]) // 2
        lands = _owners_wait(flight, name=tag + "_owners_landed", after=after)
        ts = [_add_chip_sums(h, o, chip_core, name=f"{tag}_add_chips{b}") for b, (h, o) in enumerate(zip(flight[2][:nb], lands))]
        return _result_start(ts, name=tag + "_result_start")

    def reduce_done(flight, tag, after=()):
        return [t.reshape(-1, t.shape[2]) for t in _result_wait(flight, name=tag + "_result_landed", after=after)]

    dx4, fsum = _final_loss(x4, row(final_norm_w), tgt, name="final_loss")
    dx3, g_c, sum_mlp1 = _mlp_bwd(dx4, s_mlp1, mods[1][3:6], mlp_nw[1], w_c, None, up_row, down_row, "mlp1")
    dx2, g_c, g_sc_in, sum_sc, sc_csum = _sc_layer_bwd(dx3, s_sc, mods[1][0:3], mix_nw[1], w_sc_in, sc_conv, w_c, g_c,
                                                       sc_out_row, "sc")
    dx1, g_b, sum_mlp0 = _mlp_bwd(dx2, s_mlp0, mods[0][3:6], mlp_nw[0], w_b, None, up_row, down_row, "mlp0")
    dyn, g_ssd_out, gsum_ssd = _ssd_bwd_out(dx1, s_ssd, mods[0][0:3], w_ssd_out, "ssd")
    fly_1, tok = reduce_swap([g_c, g_sc_in, g_b, g_ssd_out], "rs1")
    dy, dzx, gnsum = _gnorm_bwd(s_ssd[5], s_ssd[2], ssd_norm_w + tok[0:1, 0:1], dyn, name="ssd_dgnorm")
    fly_1, tok = reduce_send(fly_1, "rs1", (dy,))
    grad_x, d_w_zx, d_w_dt, sum_ssd, csum, ssum = _ssd_bwd_rest(
        dx1, dy, dzx, gsum_ssd, s_ssd, mods[0][0:3], mix_nw[0], w_zx, w_dt, ssd_conv, ssd_conv_b, prm + tok[0:1, 0:1], "ssd")
    fly_1, tok = reduce_sum(fly_1, "rs1", (grad_x,))

    def ssd_in_owner(k):
        lo, hi = k * SSD_IN_SHARD, (k + 1) * SSD_IN_SHARD
        if hi <= ZX:
            return d_w_zx[:, lo:hi]
        return jnp.concatenate([d_w_zx[:, lo:], d_w_dt[:, :hi - ZX]], axis=1)

    small = jnp.concatenate([sum_ssd + tok[0:1, 0:1], sum_mlp0, sum_sc, sum_mlp1, csum.reshape(24, D), gnsum.reshape(16, D),
                             fsum, sc_csum, jnp.pad(ssum, ((0, 0), (0, D - LANES)))], axis=0)
    small_all = _all_gather_rows(small, name="gather_small").reshape(N_DEV, SMALL_ROWS, D)
    fly_2, tok = reduce_swap([jnp.stack([ssd_in_owner(k) for k in range(N_CHIPS)]).astype(BF16)], "rs2", (small_all,))
    fly_2, tok = reduce_send(fly_2, "rs2", (tok,))
    t_c, t_sc_in, t_b, t_ssd_out = reduce_done(fly_1, "rs1", (tok,))
    small_all = small_all + tok[0:1, 0:1]
    tot = _sum_devices(small_all, name="sum_small")
    loss = tot[FINAL_ROW + 1, 0]
    mod_rows = [r + o for r in SUB_ROW for o in (3, 2, 0)]
    g_ada_b = jnp.stack([tot[r] for r in mod_rows]).reshape(2, 6 * D)
    g_mix_norm = jnp.stack([tot[SUB_ROW[0] + 1], tot[SUB_ROW[2] + 1]])
    g_mlp_norm = jnp.stack([tot[SUB_ROW[1] + 1], tot[SUB_ROW[3] + 1]])
    conv_sums = tot[SSD_CONV_ROW:SSD_CONV_ROW + 24].reshape(8, CONVD)
    g_ssd_conv_w = lax.dynamic_slice(conv_sums, (0, chip * 768), (4, 768))[None]
    g_ssd_conv_b = conv_sums[4:5]
    g_ssd_norm = tot[GNORM_ROW:GNORM_ROW + 2].reshape(1, DI)
    g_final = tot[FINAL_ROW]
    g_sc_conv_w = lax.dynamic_slice(tot[SC_CONV_ROW:SC_CONV_ROW + 3], (0, chip * 256), (3, 256))[None]
    g_a_log, g_d, g_dt_bias = (tot[HEAD_ROW + r:HEAD_ROW + r + 1, 0:NH] for r in range(3))
    c_pad = jnp.concatenate([c_all, jnp.zeros((8, D), F32)], axis=0)
    dmod_all = jnp.stack([small_all[:, r] for r in mod_rows], axis=1).reshape(N_DEV, 2, 6 * D)
    g_ada_w = []
    for i in range(2):
        dm = lax.dynamic_slice(dmod_all[:, i], (0, chip * n_ada), (N_DEV, n_ada))
        g_ada_w.append(_matmul_tn(c_pad, jnp.concatenate([dm, jnp.zeros_like(dm)], axis=0), m=D, n=n_ada, a_silu=True,
                                  name=f"ada_dw{i}"))

    big = dict(ada_w=[(g, 0) for g in g_ada_w], mlp_up=[(t_b, up_row), (t_c, up_row)], mlp_down=[(t_b, down_row), (t_c, down_row)],
               ssd_out_w=[(t_ssd_out, 0)], sc_out_w=[(t_c, sc_out_row)], sc_in_w=[(t_sc_in, 0)], ssd_in_w=None)
    grads = dict(ada_b=g_ada_b, mix_norm_w=g_mix_norm, mlp_norm_w=g_mlp_norm, ssd_conv_w=g_ssd_conv_w,
                 ssd_conv_b=g_ssd_conv_b, ssd_dt_bias=g_dt_bias, ssd_A_log=g_a_log, ssd_D=g_d, ssd_norm_w=g_ssd_norm,
                 sc_conv_w=g_sc_conv_w, final_norm_w=g_final)
    weights = dict(ada_w=(ada_w, m_ada_w, v_ada_w), ada_b=(ada_b, m_ada_b, v_ada_b),
                   mix_norm_w=(mix_norm_w, m_mix_norm_w, v_mix_norm_w), mlp_norm_w=(mlp_norm_w, m_mlp_norm_w, v_mlp_norm_w),
                   mlp_up=(mlp_up, m_mlp_up, v_mlp_up), mlp_down=(mlp_down, m_mlp_down, v_mlp_down),
                   ssd_in_w=(ssd_in_w, m_ssd_in_w, v_ssd_in_w), ssd_conv_w=(ssd_conv_w, m_ssd_conv_w, v_ssd_conv_w),
                   ssd_conv_b=(ssd_conv_b, m_ssd_conv_b, v_ssd_conv_b), ssd_dt_bias=(ssd_dt_bias, m_ssd_dt_bias, v_ssd_dt_bias),
                   ssd_A_log=(ssd_A_log, m_ssd_A_log, v_ssd_A_log), ssd_D=(ssd_D, m_ssd_D, v_ssd_D),
                   ssd_norm_w=(ssd_norm_w, m_ssd_norm_w, v_ssd_norm_w), ssd_out_w=(ssd_out_w, m_ssd_out_w, v_ssd_out_w),
                   sc_in_w=(sc_in_w, m_sc_in_w, v_sc_in_w), sc_conv_w=(sc_conv_w, m_sc_conv_w, v_sc_conv_w),
                   sc_out_w=(sc_out_w, m_sc_out_w, v_sc_out_w), final_norm_w=(final_norm_w, m_final_norm_w, v_final_norm_w))
    def step(nm, parts):
        w, m, v = (t if t.shape[0] == 1 else t.reshape(-1, t.shape[-1]) for t in weights[nm])
        rows, outs = w.shape[-2] // len(parts), None
        for i, (gbuf, g_row) in enumerate(parts):
            outs = _adamw(w, gbuf, m, v, g_row=g_row, w_row=i * rows, rows=rows, into=outs, emit_g=True, name=f"adamw_{nm}{i}")
        return outs

    res = {}
    for nm, (w, m, v) in weights.items():
        two_d = (-1, w.shape[-1]) if w.ndim > 1 else (1, -1)
        if nm not in big:
            res[nm] = (grads[nm], *_adamw(w.reshape(two_d), grads[nm].reshape(two_d), m.reshape(two_d), v.reshape(two_d),
                                          name="adamw_" + nm))
        elif big[nm] is not None:
            res[nm] = step(nm, big[nm])
    fly_2, tok = reduce_sum(fly_2, "rs2", tuple(r[1] for r in res.values()))
    (t_ssd_in,) = reduce_done(fly_2, "rs2", (tok,))
    w_t, m_t, v_t = (jnp.swapaxes(t[0], 0, 1) for t in weights["ssd_in_w"])
    res["ssd_in_w"] = [jnp.swapaxes(o, 0, 1) for o in _adamw(w_t, t_ssd_in.T, m_t, v_t, emit_g=True, name="adamw_ssd_in_w")]
    outs = [[res[nm][k].reshape(weights[nm][0].shape) for nm in weights] for k in range(4)]
    return (loss, grad_x[None], *outs[0], *outs[1], *outs[2], *outs[3])
```

```python
import jax
import jax.numpy as jnp
from jax import lax
from jax.experimental import pallas as pl
from jax.experimental.pallas import tpu as pltpu

F32 = jnp.float32
BF16 = jnp.bfloat16
MESH = pl.DeviceIdType.MESH

D = 1024
DFF = 4096
DI = 2048
NH = 32
HP = 64
NG = 4
NS = 128
CH = 128
CONVD = DI + 2 * NG * NS
ZX = DI + CONVD
GW = NG * NS
LANES = 128
N_CHIPS = 4
N_DEV = 8
EPS = 1e-5
ADAM_LR, ADAM_B1, ADAM_B2, ADAM_EPS, ADAM_WD, ADAM_STEP = 1e-3, 0.9, 0.999, 1e-8, 0.01, 10
VMEM_LIMIT = 48 * 1024 * 1024
TM_ALL = 2048
TM_HALF = 1024
ANY = pl.BlockSpec(memory_space=pl.ANY)
SEM = pl.BlockSpec(memory_space=pltpu.SEMAPHORE)

SSD_IN_SHARD = 1288
SC_IN_SHARD = 768


def _params(sem=None):
    return pltpu.CompilerParams(dimension_semantics=sem, vmem_limit_bytes=VMEM_LIMIT)


def _sigmoid(v):
    return 1.0 / (1.0 + jnp.exp(-v))


def _dot(a, b, dims=((1,), (0,)), precision=None):
    return lax.dot_general(a, b, (dims, ((), ())), preferred_element_type=F32, precision=precision)


def _dot_nt(a, b):
    return _dot(a, b, ((1,), (1,)))


def _dot_tn(a, b):
    return _dot(a, b, ((0,), (0,)))


def _nn(av, bv):
    return _dot(av.astype(BF16), bv.astype(BF16))


def _nt(av, bv):
    return _dot_nt(av.astype(BF16), bv.astype(BF16))


def _nn_split(av, bv):
    return _dot(av.astype(BF16), bv.reshape(-1, bv.shape[2]))


def _nn_split_sq(av, bv):
    af = av.astype(F32)
    return _nn_split(af * af, bv)


def _nt_split(av, bv):
    kc = bv.shape[2]
    acc = _dot_nt(av[:, 0:kc].astype(BF16), bv[0])
    for s in range(1, bv.shape[0]):
        acc = acc + _dot_nt(av[:, s * kc:(s + 1) * kc].astype(BF16), bv[s])
    return acc


def _nt_sc_in(av, bv):
    q = 256
    acc = None
    for i in range(3 * D // q):
        a_blk = av[i // 4][:, (i % 4) * q:(i % 4 + 1) * q]
        b_blk = bv[i // 3][:, (i % 3) * q:(i % 3 + 1) * q]
        t = _dot_nt(a_blk, b_blk)
        acc = t if acc is None else acc + t
    return acc


def _matmul(a, b, *, name, n, contract=_nn, a_spec=None, b_spec=None, tm=512, tn=512, extras=(), epi=None,
            out_dtypes=(F32,), a_silu=False):
    M = a.shape[-2]
    tm, tn = min(tm, M), min(tn, n)
    assert M % tm == 0 and n % tn == 0, (name, M, n, tm, tn)
    n_ex = len(extras)
    if a_spec is None:
        a_spec = pl.BlockSpec((tm, a.shape[1]), lambda i, j: (i, 0))
    if b_spec is None:
        b_spec = (pl.BlockSpec((tn, b.shape[1]), lambda i, j: (j, 0)) if contract is _nt
                  else pl.BlockSpec((b.shape[0], tn), lambda i, j: (0, j)))

    def body(*refs):
        av = refs[0][...]
        if a_silu:
            av = av * _sigmoid(av)
        acc = contract(av, refs[1][...])
        res = epi(acc, *[r[...] for r in refs[2:2 + n_ex]]) if epi is not None else (acc,)
        for o_ref, r in zip(refs[2 + n_ex:], res, strict=True):
            o_ref[...] = r.astype(o_ref.dtype)

    in_specs = [a_spec, b_spec]
    for e in extras:
        in_specs.append(pl.BlockSpec((1, tn), lambda i, j: (0, j)) if e.shape[0] == 1 and M != 1
                        else pl.BlockSpec((tm, tn), lambda i, j: (i, j)))
    outs = pl.pallas_call(
        body, grid=(M // tm, n // tn), in_specs=in_specs,
        out_specs=[pl.BlockSpec((tm, tn), lambda i, j: (i, j)) for _ in out_dtypes],
        out_shape=[jax.ShapeDtypeStruct((M, n), dt) for dt in out_dtypes],
        compiler_params=_params(("parallel", "parallel")), name=name)(a, b, *extras)
    return outs if len(out_dtypes) > 1 else outs[0]


def _matmul_tn(a, b, *, name, m, n, tm=512, tn=512, a_spec=None, b_spec=None, out_spec=None, out_struct=None, into=None,
               a_silu=False, a_square=False):
    T = a.shape[-2]
    tm, tn = min(tm, m), min(tn, n)
    assert m % tm == 0 and n % tn == 0, (name, m, n, tm, tn)
    if a_spec is None:
        a_spec = pl.BlockSpec((T, tm), lambda i, j: (0, i))
    if b_spec is None:
        b_spec = pl.BlockSpec((T, tn), lambda i, j: (0, j))
    if out_spec is None:
        out_spec, out_struct = pl.BlockSpec((tm, tn), lambda i, j: (i, j)), jax.ShapeDtypeStruct((m, n), F32)

    def body(a_ref, b_ref, *rest):
        av = a_ref[...]
        if a_silu:
            av = av * _sigmoid(av)
        if a_square:
            av = av.astype(F32) * av.astype(F32)
        rest[-1][...] = _dot_tn(av.astype(BF16), b_ref[...].astype(BF16)).astype(rest[-1].dtype)

    args, in_specs, alias = [a, b], [a_spec, b_spec], {}
    if into is not None:
        args, in_specs, alias = args + [into], in_specs + [ANY], {2: 0}
    return pl.pallas_call(body, grid=(m // tm, n // tn), in_specs=in_specs, out_specs=out_spec, out_shape=out_struct,
                          input_output_aliases=alias, compiler_params=_params(("parallel", "parallel")), name=name)(*args)


def _modnorm_fwd(x, nw, sc, sh, *, name):
    L = x.shape[0]
    tm = min(L, 512)

    def body(x_ref, nw_ref, sc_ref, sh_ref, h_ref):
        xv = x_ref[...]
        r = lax.rsqrt(jnp.mean(xv * xv, axis=-1, keepdims=True) + EPS)
        h_ref[...] = ((xv * r * nw_ref[...]) * (1.0 + sc_ref[...]) + sh_ref[...]).astype(BF16)

    row = pl.BlockSpec((tm, D), lambda i: (i, 0))
    vec = pl.BlockSpec((1, D), lambda i: (0, 0))
    return pl.pallas_call(body, grid=(L // tm,), in_specs=[row, vec, vec, vec], out_specs=row,
                          out_shape=jax.ShapeDtypeStruct((L, D), BF16),
                          compiler_params=_params(("parallel",)), name=name)(x, nw, sc, sh)


def _modnorm_bwd(x, dh, dxo, nw, sc, gsum, *, name):
    L = x.shape[0]
    tm = min(L, 256)

    def body(x_ref, dh_ref, dxo_ref, nw_ref, sc_ref, g_ref, dx_ref, s_ref):
        @pl.when(pl.program_id(0) == 0)
        def _():
            s_ref[...] = g_ref[...]

        xv, dhv = x_ref[...], dh_ref[...]
        r = lax.rsqrt(jnp.mean(xv * xv, axis=-1, keepdims=True) + EPS)
        xhat = xv * r
        dxhat = dhv * (nw_ref[...] * (1.0 + sc_ref[...]))
        dx_ref[...] = dxo_ref[...] + r * (dxhat - xhat * jnp.mean(dxhat * xhat, axis=-1, keepdims=True))
        s_ref[1:2, :] += jnp.sum(dhv * xhat, axis=0, keepdims=True) * (1.0 + sc_ref[...])
        s_ref[2:3, :] += jnp.sum(dhv * xhat, axis=0, keepdims=True) * nw_ref[...]
        s_ref[3:4, :] += jnp.sum(dhv, axis=0, keepdims=True)

    row = pl.BlockSpec((tm, D), lambda i: (i, 0))
    vec = pl.BlockSpec((1, D), lambda i: (0, 0))
    blk = pl.BlockSpec((8, D), lambda i: (0, 0))
    return pl.pallas_call(body, grid=(L // tm,), in_specs=[row, row, row, vec, vec, blk], out_specs=[row, blk],
                          out_shape=[jax.ShapeDtypeStruct((L, D), F32), jax.ShapeDtypeStruct((8, D), F32)],
                          compiler_params=_params(("arbitrary",)), name=name)(x, dh, dxo, nw, sc, gsum)


def _gate_bwd(dxo, y, g, *, name):
    L = dxo.shape[0]
    tm = min(L, 512)

    def body(dxo_ref, y_ref, g_ref, dy_ref, s_ref):
        @pl.when(pl.program_id(0) == 0)
        def _():
            s_ref[...] = jnp.zeros_like(s_ref)

        dv = dxo_ref[...]
        dy_ref[...] = (dv * g_ref[...]).astype(BF16)
        s_ref[0:1, :] += jnp.sum(dv * y_ref[...], axis=0, keepdims=True)

    row = pl.BlockSpec((tm, D), lambda i: (i, 0))
    return pl.pallas_call(body, grid=(L // tm,), in_specs=[row, row, pl.BlockSpec((1, D), lambda i: (0, 0))],
                          out_specs=[row, pl.BlockSpec((8, D), lambda i: (0, 0))],
                          out_shape=[jax.ShapeDtypeStruct((L, D), BF16), jax.ShapeDtypeStruct((8, D), F32)],
                          compiler_params=_params(("arbitrary",)), name=name)(dxo, y, g)


def _final_loss(x, fw, tgt, *, name):
    L = x.shape[0]
    tm = min(L, 256)

    def body(x_ref, fw_ref, t_ref, dx_ref, s_ref):
        @pl.when(pl.program_id(0) == 0)
        def _():
            s_ref[...] = jnp.zeros_like(s_ref)

        xv = x_ref[...]
        r = lax.rsqrt(jnp.mean(xv * xv, axis=-1, keepdims=True) + EPS)
        xhat = xv * r
        diff = xhat * fw_ref[...] - t_ref[...]
        dout = diff * (1.0 / D)
        dxhat = dout * fw_ref[...]
        dx_ref[...] = r * (dxhat - xhat * jnp.mean(dxhat * xhat, axis=-1, keepdims=True))
        s_ref[0:1, :] += jnp.sum(dout * xhat, axis=0, keepdims=True)
        s_ref[1:2, :] += jnp.zeros((1, D), F32) + 0.5 * jnp.sum(jnp.sum(diff * diff, axis=-1, keepdims=True) * (1.0 / D))

    row = pl.BlockSpec((tm, D), lambda i: (i, 0))
    return pl.pallas_call(body, grid=(L // tm,), in_specs=[row, pl.BlockSpec((1, D), lambda i: (0, 0)), row],
                          out_specs=[row, pl.BlockSpec((8, D), lambda i: (0, 0))],
                          out_shape=[jax.ShapeDtypeStruct((L, D), F32), jax.ShapeDtypeStruct((8, D), F32)],
                          compiler_params=_params(("arbitrary",)), name=name)(x, fw, tgt)


def _shift_down(v, j):
    if j == 0:
        return v
    row = lax.broadcasted_iota(jnp.int32, v.shape, 0)
    return jnp.where(row >= j, pltpu.roll(v, j, 0), 0.0)


def _shift_up(v, j):
    if j == 0:
        return v
    n = v.shape[0]
    row = lax.broadcasted_iota(jnp.int32, v.shape, 0)
    return jnp.where(row < n - j, pltpu.roll(v, n - j, 0), 0.0)


def _ssd_conv_fwd(zx, w, b, *, name):
    L = zx.shape[0]
    cb = 256
    k = w.shape[0]

    def body(x_ref, w_ref, b_ref, o_ref):
        xv = x_ref[...].astype(F32)
        pre = b_ref[...] + xv * w_ref[k - 1:k, :]
        for j in range(1, k):
            pre = pre + _shift_down(xv, j) * w_ref[k - 1 - j:k - j, :]
        o_ref[...] = (pre * _sigmoid(pre)).astype(BF16)

    return pl.pallas_call(
        body, grid=(CONVD // cb,),
        in_specs=[pl.BlockSpec((L, cb), lambda i: (0, i + DI // cb)), pl.BlockSpec((k, cb), lambda i: (0, i)),
                  pl.BlockSpec((1, cb), lambda i: (0, i))],
        out_specs=pl.BlockSpec((L, cb), lambda i: (0, i)), out_shape=jax.ShapeDtypeStruct((L, CONVD), BF16),
        compiler_params=_params(("parallel",)), name=name)(zx, w, b)


def _ssd_conv_bwd(zx, dact, w, b, dzx, *, name):
    L = zx.shape[0]
    cb = 256
    k = w.shape[0]

    def body(x_ref, da_ref, w_ref, b_ref, _, dx_ref, s_ref):
        xv = x_ref[...].astype(F32)
        sh = [_shift_down(xv, j) for j in range(k)]
        pre = b_ref[...] + sh[0] * w_ref[k - 1:k, :]
        for j in range(1, k):
            pre = pre + sh[j] * w_ref[k - 1 - j:k - j, :]
        s = _sigmoid(pre)
        dpre = da_ref[...].astype(F32) * (s * (1.0 + pre * (1.0 - s)))
        dx = dpre * w_ref[k - 1:k, :]
        for j in range(1, k):
            dx = dx + _shift_up(dpre, j) * w_ref[k - 1 - j:k - j, :]
        dx_ref[...] = dx.astype(BF16)
        s_ref[...] = jnp.zeros_like(s_ref)
        for j in range(k):
            s_ref[k - 1 - j:k - j, :] = jnp.sum(dpre * sh[j], axis=0, keepdims=True)
        s_ref[k:k + 1, :] = jnp.sum(dpre, axis=0, keepdims=True)

    return pl.pallas_call(
        body, grid=(CONVD // cb,),
        in_specs=[pl.BlockSpec((L, cb), lambda i: (0, i + DI // cb)), pl.BlockSpec((L, cb), lambda i: (0, i)),
                  pl.BlockSpec((k, cb), lambda i: (0, i)), pl.BlockSpec((1, cb), lambda i: (0, i)), ANY],
        out_specs=[pl.BlockSpec((L, cb), lambda i: (0, i + DI // cb)), pl.BlockSpec((8, cb), lambda i: (0, i))],
        out_shape=[jax.ShapeDtypeStruct((L, ZX), BF16), jax.ShapeDtypeStruct((8, CONVD), F32)],
        input_output_aliases={4: 0}, compiler_params=_params(("parallel",)), name=name)(zx, dact, w, b, dzx)


def _sc_fwd(proj, w, *, name):
    L = proj.shape[0]
    cb = 256
    nb = D // cb
    k = w.shape[0]

    def body(b_ref, c_ref, x_ref, w_ref, o_ref):
        u = c_ref[...] * x_ref[...]
        v = u * w_ref[k - 1:k, :]
        for j in range(1, k):
            v = v + _shift_down(u, j) * w_ref[k - 1 - j:k - j, :]
        o_ref[...] = (b_ref[...] * v).astype(BF16)

    return pl.pallas_call(
        body, grid=(nb,),
        in_specs=[pl.BlockSpec((L, cb), lambda i: (0, i)), pl.BlockSpec((L, cb), lambda i: (0, i + nb)),
                  pl.BlockSpec((L, cb), lambda i: (0, i + 2 * nb)), pl.BlockSpec((k, cb), lambda i: (0, i))],
        out_specs=pl.BlockSpec((L, cb), lambda i: (0, i)), out_shape=jax.ShapeDtypeStruct((L, D), BF16),
        compiler_params=_params(("parallel",)), name=name)(proj, proj, proj, w)


def _sc_bwd(proj, dyv, w, *, name):
    L = proj.shape[0]
    cb = 256
    nb = D // cb
    k = w.shape[0]

    def body(b_ref, c_ref, x_ref, dy_ref, w_ref, dp_ref, s_ref):
        cv, xv = c_ref[...], x_ref[...]
        u = cv * xv
        sh = [_shift_down(u, j) for j in range(k)]
        v = sh[0] * w_ref[k - 1:k, :]
        for j in range(1, k):
            v = v + sh[j] * w_ref[k - 1 - j:k - j, :]
        dyv_ = dy_ref[...]
        dp_ref[0] = (dyv_ * v).astype(BF16)
        dv = dyv_ * b_ref[...]
        du = dv * w_ref[k - 1:k, :]
        for j in range(1, k):
            du = du + _shift_up(dv, j) * w_ref[k - 1 - j:k - j, :]
        dp_ref[1] = (du * xv).astype(BF16)
        dp_ref[2] = (du * cv).astype(BF16)
        s_ref[...] = jnp.zeros_like(s_ref)
        for j in range(k):
            s_ref[k - 1 - j:k - j, :] = jnp.sum(dv * sh[j], axis=0, keepdims=True)

    blk = pl.BlockSpec((L, cb), lambda i: (0, i))
    return pl.pallas_call(
        body, grid=(nb,),
        in_specs=[blk, pl.BlockSpec((L, cb), lambda i: (0, i + nb)), pl.BlockSpec((L, cb), lambda i: (0, i + 2 * nb)),
                  blk, pl.BlockSpec((k, cb), lambda i: (0, i))],
        out_specs=[pl.BlockSpec((3, L, cb), lambda i: (0, 0, i)), pl.BlockSpec((8, cb), lambda i: (0, i))],
        out_shape=[jax.ShapeDtypeStruct((3, L, D), BF16), jax.ShapeDtypeStruct((8, D), F32)],
        compiler_params=_params(("parallel",)), name=name)(proj, proj, proj, dyv, w)


def _pieces(v, n):
    out, rest = [], v
    for _ in range(n):
        out.append(rest.astype(BF16))
        rest = rest - out[-1].astype(F32)
    return out


def _cumsum_rows(mask, v):
    m = mask.astype(BF16)
    return _dot(jnp.concatenate([m, m, m], axis=1), jnp.concatenate(_pieces(v, 3), axis=0))


def _ssd_chunk_terms(dtr, prm):
    lane = lax.broadcasted_iota(jnp.int32, (CH, LANES), 1)
    valid = lane < NH
    xdt = dtr + prm[0:1, :]
    dt = jnp.where(valid, jnp.maximum(xdt, 0.0) + jnp.log1p(jnp.exp(-jnp.abs(xdt))), 0.0)
    A = -jnp.exp(prm[1:2, :])
    ri = lax.broadcasted_iota(jnp.int32, (CH, CH), 0)
    ci = lax.broadcasted_iota(jnp.int32, (CH, CH), 1)
    cs = _cumsum_rows(ri >= ci, dt * A)
    last = cs[CH - 1:CH, :]
    spread = (lax.broadcasted_iota(jnp.int32, (2 * LANES, DI), 1) // HP
              == lax.broadcasted_iota(jnp.int32, (2 * LANES, DI), 0) % LANES).astype(BF16)
    gather = ((lax.broadcasted_iota(jnp.int32, (LANES, 2 * DI), 1) % DI) // HP
              == lax.broadcasted_iota(jnp.int32, (LANES, 2 * DI), 0)).astype(BF16)
    return dict(valid=valid, xdt=xdt, dt=dt, A=A, cs=cs, csT=cs.T, last=last, ri=ri, ci=ci, ex=(spread, gather))


def _expand(v, ex):
    if v.shape[0] == 1:
        return _expand(jnp.broadcast_to(v, (8, LANES)), ex)[0:1, :]
    return _dot(jnp.concatenate(_pieces(v, 2), axis=1), ex[0])


def _head_sum(v, ex):
    if v.shape[0] == 1:
        return _head_sum(jnp.broadcast_to(v, (8, DI)), ex)[0:1, :]
    return _dot_nt(jnp.concatenate(_pieces(v, 2), axis=1), ex[1])


def _ssd_fwd(xbc, dtr, prm, *, name):
    L = xbc.shape[0]
    nc = L // CH

    def body(xbc_ref, dtr_ref, prm_ref, y_ref, sp_ref, st_ref):
        @pl.when(pl.program_id(0) == 0)
        def _():
            st_ref[...] = jnp.zeros_like(st_ref)

        prm_v = prm_ref[...]
        t = _ssd_chunk_terms(dtr_ref[...], prm_v)
        cs, csT, ex, causal = t["cs"], t["csT"], t["ex"], t["ri"] >= t["ci"]
        xs = xbc_ref[:, 0:DI].astype(F32)
        X = xs * _expand(t["dt"], ex)
        Xb = X.astype(BF16)
        Xd = (X * _expand(jnp.exp(t["last"] - cs), ex)).astype(BF16)
        Ex = _expand(jnp.exp(cs), ex)
        cdx = _expand(jnp.exp(t["last"]), ex)
        dskx = _expand(prm_v[2:3, :], ex)
        lane = lax.broadcasted_iota(jnp.int32, (CH, LANES), 1)
        sp_ref[0] = st_ref[...]
        for g in range(NG):
            Bg = xbc_ref[:, DI + g * NS:DI + (g + 1) * NS].astype(BF16)
            Cg = xbc_ref[:, DI + GW + g * NS:DI + GW + (g + 1) * NS].astype(BF16)
            G = _dot_nt(Cg, Bg)
            Sg = st_ref[:, g * GW:(g + 1) * GW]
            yoff = _dot(Cg, Sg.astype(BF16)) * Ex[:, g * GW:(g + 1) * GW]
            for j in range(GW // LANES):
                lo = g * GW + j * LANES
                Xp = Xb[:, lo:lo + LANES]
                yd = []
                for h in (lo // HP, lo // HP + 1):
                    seg = cs[:, h:h + 1] - csT[h:h + 1, :]
                    yd.append(_dot((G * jnp.where(causal, jnp.exp(seg), 0.0)).astype(BF16), Xp))
                y_ref[:, lo:lo + LANES] = (jnp.where(lane < HP, yd[0], yd[1]) + yoff[:, j * LANES:(j + 1) * LANES]
                                           + dskx[:, lo:lo + LANES] * xs[:, lo:lo + LANES]).astype(BF16)
            st_ref[:, g * GW:(g + 1) * GW] = Sg * cdx[:, g * GW:(g + 1) * GW] + _dot_tn(Bg, Xd[:, g * GW:(g + 1) * GW])

    return pl.pallas_call(
        body, grid=(nc,),
        in_specs=[pl.BlockSpec((CH, CONVD), lambda c: (c, 0)), pl.BlockSpec((CH, LANES), lambda c: (c, 0)),
                  pl.BlockSpec((8, LANES), lambda c: (0, 0))],
        out_specs=[pl.BlockSpec((CH, DI), lambda c: (c, 0)), pl.BlockSpec((1, NS, DI), lambda c: (c, 0, 0))],
        out_shape=[jax.ShapeDtypeStruct((L, DI), BF16), jax.ShapeDtypeStruct((nc, NS, DI), F32)],
        scratch_shapes=[pltpu.VMEM((NS, DI), F32)],
        compiler_params=_params(("arbitrary",)), name=name)(xbc, dtr, prm)


def _ssd_bwd(xbc, dtr, prm, dy, sprev, *, name):
    L = xbc.shape[0]
    nc = L // CH

    def body(xbc_ref, dtr_ref, prm_ref, dy_ref, sp_ref, dxbc_ref, ddtr_ref, s_ref, dst_ref, dx_scr, de_scr, dd_scr):
        step = pl.program_id(0)

        @pl.when(step == 0)
        def _():
            dst_ref[...] = jnp.zeros_like(dst_ref)
            s_ref[...] = jnp.zeros_like(s_ref)

        prm_v = prm_ref[...]
        t = _ssd_chunk_terms(dtr_ref[...], prm_v)
        cs, csT, ex, ri, ci = t["cs"], t["csT"], t["ex"], t["ri"], t["ci"]
        E = jnp.exp(cs)
        dec = jnp.exp(t["last"] - cs)
        cd = jnp.exp(t["last"])
        xs = xbc_ref[:, 0:DI].astype(F32)
        dtx = _expand(t["dt"], ex)
        X = xs * dtx
        Xb = X.astype(BF16)
        decx = _expand(dec, ex)
        Xd = (X * decx).astype(BF16)
        Ex = _expand(E, ex)
        cdx = _expand(cd, ex)
        dskx = _expand(prm_v[2:3, :], ex)
        lane = lax.broadcasted_iota(jnp.int32, (CH, LANES), 1)
        dcs = jnp.zeros((CH, LANES), F32)
        dcd_x = []
        for g in range(NG):
            gs = slice(g * GW, (g + 1) * GW)
            Bg = xbc_ref[:, DI + g * NS:DI + (g + 1) * NS].astype(BF16)
            Cg = xbc_ref[:, DI + GW + g * NS:DI + GW + (g + 1) * NS].astype(BF16)
            G = _dot_nt(Cg, Bg)
            GT = _dot_nt(Bg, Cg)
            Sg = sp_ref[0, :, gs]
            Sgb = Sg.astype(BF16)
            dyg = dy_ref[:, gs]
            de_scr[:, gs] = dyg * _dot(Cg, Sgb)
            dYo = (Ex[:, gs] * dyg).astype(BF16)
            dC = _dot_nt(dYo, Sgb)
            dS_in = _dot_tn(Cg, dYo)
            dStg = dst_ref[:, gs]
            dStb = dStg.astype(BF16)
            dXd = _dot(Bg, dStb)
            dB = _dot_nt(Xd[:, gs], dStb)
            dd_scr[:, gs] = dXd * X[:, gs]
            dXst = dXd * decx[:, gs]
            dG = jnp.zeros((CH, CH), F32)
            dGT = jnp.zeros((CH, CH), F32)
            for j in range(GW // LANES):
                lo = g * GW + j * LANES
                Xp = Xb[:, lo:lo + LANES]
                dyp = dy_ref[:, lo:lo + LANES]
                dXp = dXst[:, j * LANES:(j + 1) * LANES]
                for k, h in enumerate((lo // HP, lo // HP + 1)):
                    dyh = jnp.where((lane < HP) if k == 0 else (lane >= HP), dyp, 0.0).astype(BF16)
                    seg = cs[:, h:h + 1] - csT[h:h + 1, :]
                    Lm = jnp.where(ri >= ci, jnp.exp(seg), 0.0)
                    LmT = jnp.where(ci >= ri, jnp.exp(-seg), 0.0)
                    dM = _dot_nt(dyh, Xp)
                    dMT = _dot_nt(Xp, dyh)
                    MT = GT * LmT
                    rs = jnp.sum(dM * (G * Lm), axis=1, keepdims=True) - jnp.sum(dMT * MT, axis=1, keepdims=True)
                    dcs = dcs + jnp.where(lane == h, rs, 0.0)
                    dG = dG + dM * Lm
                    dGT = dGT + dMT * LmT
                    dXp = dXp + _dot(MT.astype(BF16), dyh)
                dx_scr[:, lo:lo + LANES] = dXp
            dxbc_ref[:, DI + g * NS:DI + (g + 1) * NS] = (dB + _dot(dGT.astype(BF16), Cg)).astype(BF16)
            dxbc_ref[:, DI + GW + g * NS:DI + GW + (g + 1) * NS] = (dC + _dot(dG.astype(BF16), Bg)).astype(BF16)
            dcd_x.append(jnp.sum(dStg * Sg, axis=0, keepdims=True))
            dst_ref[:, gs] = dStg * cdx[:, gs] + dS_in
        dX = dx_scr[...]
        dy = dy_ref[...]
        ddec = _head_sum(dd_scr[...], ex)
        dcd = _head_sum(jnp.concatenate(dcd_x, axis=1), ex)
        dcs = dcs + _head_sum(de_scr[...], ex) * E - ddec * dec
        row = lax.broadcasted_iota(jnp.int32, (CH, LANES), 0)
        dcs = dcs + jnp.where(row == CH - 1, jnp.sum(ddec * dec, axis=0, keepdims=True) + dcd * cd, 0.0)
        da = _cumsum_rows(ci >= ri, dcs)
        ddt = da * t["A"] + _head_sum(dX * xs, ex)
        ddtr = jnp.where(t["valid"], ddt * _sigmoid(t["xdt"]), 0.0)
        ddtr_ref[...] = ddtr
        dxbc_ref[:, 0:DI] = (dX * dtx + dskx * dy).astype(BF16)
        s_ref[0:1, :] += jnp.sum(da * t["dt"], axis=0, keepdims=True)
        s_ref[1:2, :] += _head_sum(jnp.sum(dy * xs, axis=0, keepdims=True), ex)
        s_ref[2:3, :] += jnp.sum(ddtr, axis=0, keepdims=True)

        @pl.when(step == nc - 1)
        def _():
            s_ref[0:1, :] = s_ref[0:1, :] * t["A"]

    rev = lambda c: (nc - 1 - c, 0)
    return pl.pallas_call(
        body, grid=(nc,),
        in_specs=[pl.BlockSpec((CH, CONVD), rev), pl.BlockSpec((CH, LANES), rev), pl.BlockSpec((8, LANES), lambda c: (0, 0)),
                  pl.BlockSpec((CH, DI), rev), pl.BlockSpec((1, NS, DI), lambda c: (nc - 1 - c, 0, 0))],
        out_specs=[pl.BlockSpec((CH, CONVD), rev), pl.BlockSpec((CH, LANES), rev), pl.BlockSpec((8, LANES), lambda c: (0, 0))],
        out_shape=[jax.ShapeDtypeStruct((L, CONVD), BF16), jax.ShapeDtypeStruct((L, LANES), F32),
                   jax.ShapeDtypeStruct((8, LANES), F32)],
        scratch_shapes=[pltpu.VMEM((NS, DI), F32), pltpu.VMEM((CH, DI), F32), pltpu.VMEM((CH, DI), F32),
                        pltpu.VMEM((CH, DI), F32)],
        compiler_params=_params(("arbitrary",)), name=name)(xbc, dtr, prm, dy, sprev)


def _gnorm_fwd(y, zx, nw, *, name):
    L = y.shape[0]
    tm = min(L, 256)

    def body(y_ref, z_ref, nw_ref, o_ref):
        z = z_ref[...].astype(F32)
        yg = y_ref[...].astype(F32) * (z * _sigmoid(z))
        for g in range(NG):
            v = yg[:, g * GW:(g + 1) * GW]
            r = lax.rsqrt(jnp.mean(v * v, axis=-1, keepdims=True) + EPS)
            o_ref[:, g * GW:(g + 1) * GW] = (v * r * nw_ref[:, g * GW:(g + 1) * GW]).astype(BF16)

    row = pl.BlockSpec((tm, DI), lambda i: (i, 0))
    return pl.pallas_call(body, grid=(L // tm,), in_specs=[row, row, pl.BlockSpec((1, DI), lambda i: (0, 0))],
                          out_specs=row, out_shape=jax.ShapeDtypeStruct((L, DI), BF16),
                          compiler_params=_params(("parallel",)), name=name)(y, zx, nw)


def _gnorm_bwd(y, zx, nw, dyn, *, name):
    L = y.shape[0]
    tm = min(L, 256)

    def body(y_ref, z_ref, nw_ref, dyn_ref, dy_ref, dz_ref, s_ref):
        @pl.when(pl.program_id(0) == 0)
        def _():
            s_ref[...] = jnp.zeros_like(s_ref)

        z, yv = z_ref[...].astype(F32), y_ref[...].astype(F32)
        sz = _sigmoid(z)
        gate = z * sz
        dgate_dz = sz * (1.0 + z * (1.0 - sz))
        for g in range(NG):
            gs = slice(g * GW, (g + 1) * GW)
            v = yv[:, gs] * gate[:, gs]
            r = lax.rsqrt(jnp.mean(v * v, axis=-1, keepdims=True) + EPS)
            vhat = v * r
            dn = dyn_ref[:, gs]
            s_ref[0:1, gs] += jnp.sum(dn * vhat, axis=0, keepdims=True)
            dvhat = dn * nw_ref[:, gs]
            dv = r * (dvhat - vhat * jnp.mean(dvhat * vhat, axis=-1, keepdims=True))
            dy_ref[:, gs] = dv * gate[:, gs]
            dz_ref[:, gs] = (dv * yv[:, gs] * dgate_dz[:, gs]).astype(BF16)

    row = pl.BlockSpec((tm, DI), lambda i: (i, 0))
    return pl.pallas_call(body, grid=(L // tm,), in_specs=[row, row, pl.BlockSpec((1, DI), lambda i: (0, 0)), row],
                          out_specs=[row, row, pl.BlockSpec((8, DI), lambda i: (0, 0))],
                          out_shape=[jax.ShapeDtypeStruct((L, DI), F32), jax.ShapeDtypeStruct((L, ZX), BF16),
                                     jax.ShapeDtypeStruct((8, DI), F32)],
                          compiler_params=_params(("arbitrary",)), name=name)(y, zx, nw, dyn)


def _adamw(w, g, m, v, *, name, g_row=0, w_row=0, rows=None, into=None, emit_g=False):
    lead = w.ndim == 3
    R, C = w.shape[-2:]
    rows = R if rows is None else rows
    tr = max([t for t in range(8, rows + 1, 8) if rows % t == 0 and t * C <= 256 * 1024], default=rows)
    assert g_row % tr == 0 and w_row % tr == 0, (name, g_row, w_row, tr)
    n_out = 4 if emit_g else 3

    def body(w_ref, g_ref, m_ref, v_ref, *rest):
        outs = rest[-n_out:]
        gv = g_ref[...]
        mn = ADAM_B1 * m_ref[...] + (1.0 - ADAM_B1) * gv
        vn = ADAM_B2 * v_ref[...] + (1.0 - ADAM_B2) * (gv * gv)
        m_hat = mn / (1.0 - ADAM_B1 ** ADAM_STEP)
        v_hat = vn / (1.0 - ADAM_B2 ** ADAM_STEP)
        d_ref, mo_ref, vo_ref = outs[-3:]
        d_ref[...] = -ADAM_LR * (m_hat / (jnp.sqrt(v_hat) + ADAM_EPS) + ADAM_WD * w_ref[...])
        mo_ref[...] = mn
        vo_ref[...] = vn
        if emit_g:
            outs[0][...] = gv

    blk = (pl.BlockSpec((None, tr, C), lambda i: (0, i + w_row // tr, 0)) if lead
           else pl.BlockSpec((tr, C), lambda i: (i + w_row // tr, 0)))
    args, in_specs, alias = [w, g, m, v], [blk, pl.BlockSpec((tr, C), lambda i: (i + g_row // tr, 0)), blk, blk], {}
    if into is not None:
        args, in_specs, alias = args + list(into), in_specs + [ANY] * n_out, {4 + k: k for k in range(n_out)}
    return pl.pallas_call(body, grid=(rows // tr,), in_specs=in_specs, out_specs=[blk] * n_out,
                          out_shape=[jax.ShapeDtypeStruct(w.shape, F32)] * n_out, input_output_aliases=alias,
                          compiler_params=_params(("parallel",)), name=name)(*args)


def _residual(acc, xv, gv):
    return xv + gv * acc, acc


def _like(buf):
    return jax.ShapeDtypeStruct(buf.shape, buf.dtype)


def _mlp_fwd(x, mod, nw, wb, up_row, down_row, tag):
    sh, sc, g = mod
    h = _modnorm_fwd(x, nw, sc, sh, name=tag + "_norm")
    a = _matmul(h, wb, n=DFF, tm=TM_ALL, b_spec=pl.BlockSpec((None, D, 512), lambda mi, j: (j // 2, up_row // D, j % 2)),
                epi=lambda acc: (jnp.maximum(acc, 0.0),), out_dtypes=(BF16,), name=tag + "_up")
    xn, y = _matmul(a, wb, n=D, tm=TM_HALF, contract=_nn_split_sq,
                    b_spec=pl.BlockSpec((N_CHIPS, D, 512), lambda mi, j: (0, down_row // D, j)),
                    extras=(x, g), epi=_residual, out_dtypes=(F32, F32), name=tag + "_down")
    return xn, (x, h, a, y)


def _mlp_bwd(dxo, saved, mod, nw, wb, gb, up_row, down_row, tag):
    x, h, a, y = saved
    sh, sc, g = mod
    dy, gsum = _gate_bwd(dxo, y, g, name=tag + "_dgate")
    du = _matmul(dy, wb, n=DFF, tm=TM_ALL, contract=_nt,
                 b_spec=pl.BlockSpec((None, 512, D), lambda mi, j: (j // 2, down_row // 512 + j % 2, 0)),
                 extras=(a,), epi=lambda acc, av: (acc * (2.0 * av.astype(F32)),), out_dtypes=(BF16,), name=tag + "_dact")
    gb = _matmul_tn(a, dy, m=DFF, n=D, tm=D, tn=D, a_square=True, into=gb, out_struct=_like(wb),
                    out_spec=pl.BlockSpec((None, D, D), lambda mi, j: (mi, down_row // D, 0)), name=tag + "_ddown")
    dh = _matmul(du, wb, n=D, tm=TM_HALF, contract=_nt_split,
                 b_spec=pl.BlockSpec((N_CHIPS, 512, D), lambda mi, j: (0, up_row // 512 + j, 0)), name=tag + "_dh")
    gb = _matmul_tn(h, du, m=D, n=DFF, tm=D, into=gb, out_struct=_like(wb),
                    out_spec=pl.BlockSpec((None, D, 512), lambda mi, j: (j // 2, up_row // D, j % 2)), name=tag + "_dup")
    dx, sums = _modnorm_bwd(x, dh, dxo, nw, sc, gsum, name=tag + "_dnorm")
    return dx, gb, sums


def _ssd_fwd_scan(x, mod, nw, w_zx, w_dt, conv_w, conv_b, prm, tag):
    sh, sc, g = mod
    h = _modnorm_fwd(x, nw, sc, sh, name=tag + "_norm")
    zx = _matmul(h, w_zx, n=ZX, tm=TM_ALL, out_dtypes=(BF16,), name=tag + "_in")
    dtr = _matmul(h, w_dt, n=LANES, tm=TM_ALL, name=tag + "_in_dt")
    xbc = _ssd_conv_fwd(zx, conv_w, conv_b, name=tag + "_conv")
    y, sprev = _ssd_fwd(xbc, dtr, prm, name=tag + "_scan")
    return h, zx, dtr, xbc, y, sprev


def _ssd_fwd_out(x, mod, scan, gn_w, w_out, tag):
    sh, sc, g = mod
    h, zx, dtr, xbc, y, sprev = scan
    yn = _gnorm_fwd(y, zx, gn_w, name=tag + "_gnorm")
    xn, yo = _matmul(yn, w_out, n=D, tm=TM_HALF, contract=_nn_split,
                     b_spec=pl.BlockSpec((N_CHIPS, 512, 512), lambda mi, j: (0, 0, j)),
                     extras=(x, g), epi=_residual, out_dtypes=(F32, F32), name=tag + "_out")
    return xn, (x, h, zx, dtr, xbc, y, sprev, yn, yo)


def _ssd_bwd_out(dxo, saved, mod, w_out, tag):
    x, h, zx, dtr, xbc, y, sprev, yn, yo = saved
    sh, sc, g = mod
    dyo, gsum = _gate_bwd(dxo, yo, g, name=tag + "_dgate")
    dyn = _matmul(dyo, w_out, n=DI, tm=TM_ALL, contract=_nt, b_spec=pl.BlockSpec((None, 512, D), lambda mi, j: (j, 0, 0)),
                  name=tag + "_dyn")
    g_out = _matmul_tn(yn, dyo, m=DI, n=D, tn=D, out_struct=_like(w_out),
                       out_spec=pl.BlockSpec((None, 512, D), lambda mi, j: (mi, 0, 0)), name=tag + "_dout")
    return dyn, g_out, gsum


def _ssd_bwd_rest(dxo, dy, dzx, gsum, saved, mod, nw, w_zx, w_dt, conv_w, conv_b, prm, tag):
    x, h, zx, dtr, xbc, y, sprev, yn, yo = saved
    sh, sc, g = mod
    dxbc, ddtr, ssum = _ssd_bwd(xbc, dtr, prm, dy, sprev, name=tag + "_dscan")
    dzx, csum = _ssd_conv_bwd(zx, dxbc, conv_w, conv_b, dzx, name=tag + "_dconv")
    dh_dt = _matmul(ddtr, w_dt, n=D, tm=TM_ALL, contract=_nt, name=tag + "_dh_dt")
    dh = _matmul(dzx, w_zx, n=D, tm=TM_HALF, contract=_nt, extras=(dh_dt,), epi=lambda acc, e: (acc + e,), name=tag + "_dh")
    d_w_zx = _matmul_tn(h, dzx, m=D, n=ZX, tm=D, name=tag + "_din")
    d_w_dt = _matmul_tn(h, ddtr, m=D, n=LANES, tm=D, name=tag + "_din_dt")
    dx, sums = _modnorm_bwd(x, dh, dxo, nw, sc, gsum, name=tag + "_dnorm")
    return dx, d_w_zx, d_w_dt, sums, csum, ssum


def _sc_layer_fwd(x, mod, nw, w_sc_in, conv_w, wb, out_row, tag):
    sh, sc, g = mod
    h = _modnorm_fwd(x, nw, sc, sh, name=tag + "_norm")
    proj = _matmul(h, w_sc_in, n=3 * D, tm=TM_ALL, tn=256, b_spec=pl.BlockSpec((None, D, 256), lambda mi, j: (j // 3, 0, j % 3)),
                   name=tag + "_in")
    yv = _sc_fwd(proj, conv_w, name=tag + "_conv")
    xn, yo = _matmul(yv, wb, n=D, tm=TM_HALF, contract=_nn_split,
                     b_spec=pl.BlockSpec((N_CHIPS, 256, 512), lambda mi, j: (0, out_row // 256, j)),
                     extras=(x, g), epi=_residual, out_dtypes=(F32, F32), name=tag + "_out")
    return xn, (x, h, proj, yv, yo)


def _sc_layer_bwd(dxo, saved, mod, nw, w_sc_in, conv_w, wb, gb, out_row, tag):
    x, h, proj, yv, yo = saved
    sh, sc, g = mod
    L = x.shape[0]
    dyo, gsum = _gate_bwd(dxo, yo, g, name=tag + "_dgate")
    dyv = _matmul(dyo, wb, n=D, tm=TM_ALL, tn=256, contract=_nt,
                  b_spec=pl.BlockSpec((None, 256, D), lambda mi, j: (j, out_row // 256, 0)), name=tag + "_dyv")
    gb = _matmul_tn(yv, dyo, m=D, n=D, tm=256, tn=D, into=gb, out_struct=_like(wb),
                    out_spec=pl.BlockSpec((None, 256, D), lambda mi, j: (mi, out_row // 256, 0)), name=tag + "_dout")
    dproj, csum = _sc_bwd(proj, dyv, conv_w, name=tag + "_dconv")
    tm = min(L, TM_HALF)
    dh = _matmul(dproj, w_sc_in, n=D, tm=tm, contract=_nt_sc_in, a_spec=pl.BlockSpec((3, tm, D), lambda mi, j: (0, mi, 0)),
                 b_spec=pl.BlockSpec((N_CHIPS, 512, SC_IN_SHARD), lambda mi, j: (0, j, 0)), name=tag + "_dh")
    g_sc_in = _matmul_tn(h, dproj, m=D, n=3 * D, tm=D, tn=256, b_spec=pl.BlockSpec((None, L, 256), lambda mi, j: (j // 4, 0, j % 4)),
                         out_spec=pl.BlockSpec((None, D, 256), lambda mi, j: (j // 3, 0, j % 3)),
                         out_struct=jax.ShapeDtypeStruct((N_CHIPS, D, SC_IN_SHARD), BF16), name=tag + "_din")
    dx, sums = _modnorm_bwd(x, dh, dxo, nw, sc, gsum, name=tag + "_dnorm")
    return dx, gb, g_sc_in, sums, csum


SUB_ROW = (0, 8, 16, 24)
SSD_CONV_ROW, GNORM_ROW, FINAL_ROW, SC_CONV_ROW, HEAD_ROW, SMALL_ROWS = 32, 56, 72, 80, 88, 96


def _all_gather_rows(blk, *, name):
    m_per, n = blk.shape

    def body(x_ref, out_ref, send_sems, recv_sems, local_sem):
        x, y, c = lax.axis_index("x"), lax.axis_index("y"), lax.axis_index("c")
        me, sibling = (x, y, c), (x, y, 1 - c)
        chips = [(1 - x, y), (x, 1 - y), (1 - x, 1 - y)]

        def rows(px, py, pc):
            return out_ref.at[pl.ds((4 * px + 2 * py + pc) * m_per, m_per), :]

        def copy(k, block, to, src=None):
            return pltpu.make_async_remote_copy(src_ref=rows(*block) if src is None else src, dst_ref=rows(*block),
                                                send_sem=send_sems.at[k], recv_sem=recv_sems.at[k], device_id=to,
                                                device_id_type=MESH)

        mine = pltpu.make_async_copy(x_ref, rows(*me), local_sem)
        mine.start()
        first = [copy(0, me, sibling, src=x_ref)] + [copy(1 + j, me, (*chip, c), src=x_ref) for j, chip in enumerate(chips)]
        for cp in first:
            cp.start()
        passed = [copy(4 + j, (*chip, c), sibling) for j, chip in enumerate(chips)]
        for j, chip in enumerate(chips):
            copy(1 + j, (*chip, c), me).wait_recv()
            passed[j].start()
        copy(0, sibling, me).wait_recv()
        for j, chip in enumerate(chips):
            copy(4 + j, (*chip, 1 - c), me).wait_recv()
        for cp in first + passed:
            cp.wait_send()
        mine.wait()

    return pl.pallas_call(
        body, out_shape=jax.ShapeDtypeStruct((N_DEV * m_per, n), blk.dtype),
        in_specs=[pl.BlockSpec(memory_space=pltpu.VMEM)], out_specs=pl.BlockSpec(memory_space=pltpu.VMEM),
        scratch_shapes=[pltpu.SemaphoreType.DMA((7,)), pltpu.SemaphoreType.DMA((7,)), pltpu.SemaphoreType.DMA],
        name=name)(blk)


def _half(ref, chip, c):
    hr = ref.shape[1] // 2
    return ref.at[chip, pl.ds(c * hr, hr), :]


def _gather_copy(bufs, sends, recvs, b, k, chip, pc, to):
    piece = _half(bufs[b], 2 * chip[0] + chip[1], pc)
    return pltpu.make_async_remote_copy(src_ref=piece, dst_ref=piece, send_sem=sends.at[4 * b + k], recv_sem=recvs.at[4 * b + k],
                                        device_id=to, device_id_type=MESH)


def _split_call(body, bufs, sems_in, n_sems, *, name, after=(), token=False, lands=()):
    nb, na, nl, starts = len(bufs), len(after), len(lands), not sems_in

    def wrapped(*refs):
        sems = refs[nb + na:nb + na + 2] if starts else refs[nb:nb + 2]
        made = refs[nb + na + 2 + nb:nb + na + 2 + nb + nl] if starts else ()
        body(tuple(refs[:nb]) + tuple(made), sems[0], sems[1])
        if token:
            refs[-1][...] = jnp.zeros_like(refs[-1])

    out_shape = [pltpu.SemaphoreType.DMA((n_sems,)) for _ in range(2 if starts else 0)]
    out_specs = [SEM] * len(out_shape) + [ANY] * (nb + nl)
    alias = {b: len(out_shape) + b for b in range(nb)}
    out_shape += [jax.ShapeDtypeStruct(b.shape, b.dtype) for b in bufs] + list(lands)
    if token:
        out_shape.append(jax.ShapeDtypeStruct((8, LANES), F32))
        out_specs.append(pl.BlockSpec(memory_space=pltpu.VMEM))
    return pl.pallas_call(
        wrapped, out_shape=out_shape, in_specs=[ANY] * nb + [SEM] * len(sems_in) + [ANY] * na, out_specs=out_specs,
        input_output_aliases=alias,
        compiler_params=pltpu.CompilerParams(has_side_effects=pltpu.SideEffectType.DATAFLOW_SIDE_EFFECTING),
        name=name)(*bufs, *sems_in, *after)


def _gather_start(bufs, *, name, after=()):
    nb = len(bufs)

    def body(ins, sends, recvs):
        x, y, c = lax.axis_index("x"), lax.axis_index("y"), lax.axis_index("c")
        chips = [(1 - x, y), (x, 1 - y), (1 - x, 1 - y)]
        for b in range(nb):
            _gather_copy(ins, sends, recvs, b, 0, (x, y), c, (x, y, 1 - c)).start()
            for j, chip in enumerate(chips):
                _gather_copy(ins, sends, recvs, b, 1 + j, (x, y), c, (*chip, c)).start()

    out = _split_call(body, bufs, (), 4 * nb, name=name, after=after, token=True)
    return (out[0], out[1], out[2:2 + nb]), out[-1]


def _gather_wait_first(flight, *, name, after=()):
    sends, recvs, bufs = flight
    nb = len(bufs)

    def body(ins, sends_, recvs_):
        x, y, c = lax.axis_index("x"), lax.axis_index("y"), lax.axis_index("c")
        chips = [(1 - x, y), (x, 1 - y), (1 - x, 1 - y)]
        for b in range(nb):
            _gather_copy(ins, sends_, recvs_, b, 0, (x, y), c, (x, y, 1 - c)).wait_send()
            _gather_copy(ins, sends_, recvs_, b, 0, (x, y), 1 - c, (x, y, c)).wait_recv()
            for j, chip in enumerate(chips):
                _gather_copy(ins, sends_, recvs_, b, 1 + j, (x, y), c, (*chip, c)).wait_send()
                _gather_copy(ins, sends_, recvs_, b, 1 + j, chip, c, (x, y, c)).wait_recv()

    return _split_call(body, bufs, (sends, recvs), 4 * nb, name=name, after=after)


def _gather_forward(bufs, *, name):
    nb = len(bufs)

    def body(ins, sends, recvs):
        x, y, c = lax.axis_index("x"), lax.axis_index("y"), lax.axis_index("c")
        chips = [(1 - x, y), (x, 1 - y), (1 - x, 1 - y)]
        for b in range(nb):
            for j, chip in enumerate(chips):
                _gather_copy(ins, sends, recvs, b, 1 + j, chip, c, (x, y, 1 - c)).start()

    out = _split_call(body, bufs, (), 4 * nb, name=name)
    return out[0], out[1], out[2:2 + nb]


def _gather_wait_forward(flight, *, name, after=()):
    sends, recvs, bufs = flight
    nb = len(bufs)

    def body(ins, sends_, recvs_):
        x, y, c = lax.axis_index("x"), lax.axis_index("y"), lax.axis_index("c")
        chips = [(1 - x, y), (x, 1 - y), (1 - x, 1 - y)]
        for b in range(nb):
            for j, chip in enumerate(chips):
                _gather_copy(ins, sends_, recvs_, b, 1 + j, chip, c, (x, y, 1 - c)).wait_send()
                _gather_copy(ins, sends_, recvs_, b, 1 + j, chip, 1 - c, (x, y, c)).wait_recv()

    return _split_call(body, bufs, (sends, recvs), 4 * nb, name=name, after=after)


def _owner_copies(hs, lands, sends, recvs):
    x, y, c = lax.axis_index("x"), lax.axis_index("y"), lax.axis_index("c")
    chips = [(1 - x, y), (x, 1 - y), (1 - x, 1 - y)]
    return [pltpu.make_async_remote_copy(src_ref=hs[b].at[2 * cx + cy], dst_ref=lands[b].at[j], send_sem=sends.at[3 * b + j],
                                         recv_sem=recvs.at[3 * b + j], device_id=(cx, cy, c), device_id_type=MESH)
            for b in range(len(hs)) for j, (cx, cy) in enumerate(chips)]


def _owners_start(hs, *, name):
    nb = len(hs)
    lands = [jax.ShapeDtypeStruct((3,) + h.shape[1:], h.dtype) for h in hs]

    def body(refs, sends, recvs):
        for cp in _owner_copies(refs[:nb], refs[nb:], sends, recvs):
            cp.start()

    out = _split_call(body, list(hs), (), 3 * nb, name=name, token=True, lands=lands)
    return (out[0], out[1], out[2:2 + 2 * nb]), out[-1]


def _owners_wait(flight, *, name, after=()):
    sends, recvs, bufs = flight
    nb = len(bufs) // 2

    def body(refs, sends_, recvs_):
        for cp in _owner_copies(refs[:nb], refs[nb:], sends_, recvs_):
            cp.wait()

    out = _split_call(body, bufs, (sends, recvs), 3 * nb, name=name, after=after)
    return out[:nb], out[nb:]


def _sibling_copies(gs, lands, sends, recvs):
    x, y, c = lax.axis_index("x"), lax.axis_index("y"), lax.axis_index("c")
    copies = []
    for b in range(len(gs)):
        hr = gs[b].shape[1] // 2
        copies.append(pltpu.make_async_remote_copy(
            src_ref=gs[b].at[:, pl.ds((1 - c) * hr, hr), :], dst_ref=lands[b], send_sem=sends.at[b], recv_sem=recvs.at[b],
            device_id=(x, y, 1 - c), device_id_type=MESH))
    return copies


def _sibling_start(gs, *, name, after=()):
    nb = len(gs)
    lands = [jax.ShapeDtypeStruct((g.shape[0], g.shape[1] // 2, g.shape[2]), g.dtype) for g in gs]

    def body(refs, sends, recvs):
        for cp in _sibling_copies(refs[:nb], refs[nb:], sends, recvs):
            cp.start()

    out = _split_call(body, list(gs), (), nb, name=name, after=after, token=True, lands=lands)
    return (out[0], out[1], out[2:2 + 2 * nb]), out[-1]


def _sibling_wait(flight, *, name, after=()):
    sends, recvs, bufs = flight
    nb = len(bufs) // 2

    def body(refs, sends_, recvs_):
        for cp in _sibling_copies(refs[:nb], refs[nb:], sends_, recvs_):
            cp.wait()

    out = _split_call(body, bufs, (sends, recvs), nb, name=name, after=after)
    return out[:nb], out[nb:]


def _result_copies(ts, sends, recvs):
    x, y, c = lax.axis_index("x"), lax.axis_index("y"), lax.axis_index("c")
    return [pltpu.make_async_remote_copy(src_ref=ts[b].at[c], dst_ref=ts[b].at[c], send_sem=sends.at[b], recv_sem=recvs.at[b],
                                         device_id=(x, y, 1 - c), device_id_type=MESH) for b in range(len(ts))]


def _result_start(ts, *, name):
    def body(refs, sends, recvs):
        for cp in _result_copies(refs, sends, recvs):
            cp.start()

    out = _split_call(body, ts, (), len(ts), name=name, token=True)
    return (out[0], out[1], out[2:2 + len(ts)]), out[-1]


def _result_wait(flight, *, name, after=()):
    sends, recvs, bufs = flight

    def body(refs, sends_, recvs_):
        for cp in _result_copies(refs, sends_, recvs_):
            cp.wait()

    return _split_call(body, bufs, (sends, recvs), len(bufs), name=name, after=after)


def _row_tile(rows, cols):
    best = 16
    for t in range(16, rows + 1, 16):
        if rows % t == 0 and t * cols <= 640 * 1024:
            best = t
    assert rows % best == 0, (rows, cols)
    return best


def _add_sibling_half(g, recv, core, *, name):
    nk, r, n = g.shape
    hr = r // 2
    tr = _row_tile(hr, n)

    def body(c_ref, a_ref, b_ref, o_ref):
        o_ref[...] = (a_ref[...].astype(F32) + b_ref[...].astype(F32)).astype(BF16)

    grid_spec = pltpu.PrefetchScalarGridSpec(
        num_scalar_prefetch=1, grid=(nk, hr // tr),
        in_specs=[pl.BlockSpec((None, tr, n), lambda k, i, c_ref: (k, c_ref[0] * (hr // tr) + i, 0)),
                  pl.BlockSpec((None, tr, n), lambda k, i, c_ref: (k, i, 0))],
        out_specs=pl.BlockSpec((None, tr, n), lambda k, i, c_ref: (k, i, 0)))
    return pl.pallas_call(body, grid_spec=grid_spec, out_shape=jax.ShapeDtypeStruct((nk, hr, n), BF16),
                          compiler_params=_params(("parallel", "parallel")), name=name)(core, g, recv)


def _add_chip_sums(h, recv, chip_core, *, name):
    _, hr, n = h.shape
    tr = _row_tile(hr, n)

    def body(k_ref, a_ref, b_ref, o_ref):
        o_ref[...] = ((a_ref[...].astype(F32) + b_ref[0].astype(F32)) + b_ref[1].astype(F32)) + b_ref[2].astype(F32)

    grid_spec = pltpu.PrefetchScalarGridSpec(
        num_scalar_prefetch=1, grid=(hr // tr,),
        in_specs=[pl.BlockSpec((None, tr, n), lambda i, k_ref: (k_ref[0], i, 0)),
                  pl.BlockSpec((3, tr, n), lambda i, k_ref: (0, i, 0))],
        out_specs=pl.BlockSpec((None, tr, n), lambda i, k_ref: (k_ref[1], i, 0)))
    return pl.pallas_call(body, grid_spec=grid_spec, out_shape=jax.ShapeDtypeStruct((2, hr, n), F32),
                          compiler_params=_params(("parallel",)), name=name)(chip_core, h, recv)


def _sum_devices(g, *, name):
    nd, r, n = g.shape

    def body(g_ref, o_ref):
        acc = g_ref[0]
        for i in range(1, nd):
            acc = acc + g_ref[i]
        o_ref[...] = acc

    return pl.pallas_call(body, out_shape=jax.ShapeDtypeStruct((r, n), F32), name=name)(g)


def _own_slot(shard, chip):
    return lax.dynamic_update_slice(jnp.zeros((N_CHIPS,) + shard.shape, BF16), shard[None], (chip, 0, 0))


def kernel(x, c, ada_w, ada_b, mix_norm_w, mlp_norm_w, mlp_up, mlp_down, ssd_in_w, ssd_conv_w, ssd_conv_b, ssd_dt_bias, ssd_A_log, ssd_D, ssd_norm_w, ssd_out_w, sc_in_w, sc_conv_w, sc_out_w, final_norm_w, loss_target, m_ada_w, m_ada_b, m_mix_norm_w, m_mlp_norm_w, m_mlp_up, m_mlp_down, m_ssd_in_w, m_ssd_conv_w, m_ssd_conv_b, m_ssd_dt_bias, m_ssd_A_log, m_ssd_D, m_ssd_norm_w, m_ssd_out_w, m_sc_in_w, m_sc_conv_w, m_sc_out_w, m_final_norm_w, v_ada_w, v_ada_b, v_mix_norm_w, v_mlp_norm_w, v_mlp_up, v_mlp_down, v_ssd_in_w, v_ssd_conv_w, v_ssd_conv_b, v_ssd_dt_bias, v_ssd_A_log, v_ssd_D, v_ssd_norm_w, v_ssd_out_w, v_sc_in_w, v_sc_conv_w, v_sc_out_w, v_final_norm_w):
    xi, yi, ci = lax.axis_index("x"), lax.axis_index("y"), lax.axis_index("c")
    chip = 2 * xi + yi
    dev = 2 * chip + ci
    n_ada = ada_w.shape[2]

    conv_flat = jnp.concatenate([ssd_conv_w.reshape(-1), sc_conv_w.reshape(-1), jnp.zeros((256,), F32)]).reshape(4, D)
    blk0 = jnp.concatenate([c, conv_flat, jnp.zeros((3, D), F32)], axis=0)
    got0 = _all_gather_rows(blk0, name="gather_cond").reshape(N_DEV, 8, D)
    c_all = got0[:, 0]
    conv_all = got0[0::2, 1:5].reshape(N_CHIPS, 4 * D)
    ssd_conv = jnp.moveaxis(conv_all[:, :4 * 768].reshape(N_CHIPS, 4, 768), 0, 1).reshape(4, CONVD)
    sc_conv = jnp.moveaxis(conv_all[:, 4 * 768:4 * 768 + 3 * 256].reshape(N_CHIPS, 3, 256), 0, 1).reshape(3, D)
    mod_shard = [_matmul(c_all, ada_w, n=n_ada, a_silu=True, b_spec=pl.BlockSpec((None, D, 512), lambda mi, j, i=i: (i, 0, j)),
                         extras=(lax.dynamic_slice(ada_b, (i, chip * n_ada), (1, n_ada)),),
                         epi=lambda acc, b: (acc + b,), name=f"ada_mod{i}") for i in range(2)]
    mod_all = _all_gather_rows(jnp.concatenate(mod_shard, axis=0), name="gather_mod")
    mod_all = mod_all.reshape(N_DEV, 2, N_DEV, n_ada)[0::2]
    mod = jnp.moveaxis(lax.dynamic_index_in_dim(mod_all, dev, axis=2, keepdims=False), 0, 1).reshape(2, 6, D)
    mods = [[mod[i, j:j + 1] for j in range(6)] for i in range(2)]

    bf = lambda v: v.astype(BF16)
    up_row, down_row, sc_out_row = 0, D, 2 * D
    a_bufs = [_own_slot(bf(ssd_in_w[0]), chip)]
    b_bufs = [_own_slot(bf(ssd_out_w[0]), chip), _own_slot(bf(jnp.concatenate([mlp_up[0], mlp_down[0]], axis=0)), chip)]
    c_bufs = [_own_slot(bf(sc_in_w[0]), chip), _own_slot(bf(jnp.concatenate([mlp_up[1], mlp_down[1], sc_out_w[0]], axis=0)), chip)]
    fly_a, tok = _gather_start(a_bufs, name="gather_a_start", after=(mod,))
    fly_b, tok = _gather_start(b_bufs, name="gather_b_start", after=(tok,))
    fly_c, tok = _gather_start(c_bufs, name="gather_c_start", after=(tok,))

    row = lambda v: v.reshape(1, -1)
    xs, tgt = x[0], loss_target[0]
    prm = jnp.pad(jnp.concatenate([ssd_dt_bias, ssd_A_log, ssd_D, jnp.zeros((5, NH), F32)], axis=0), ((0, 0), (0, LANES - NH)))
    mix_nw = [row(mix_norm_w[i]) for i in range(2)]
    mlp_nw = [row(mlp_norm_w[i]) for i in range(2)]
    a_bufs = _gather_wait_first(fly_a, name="gather_a_landed", after=(tok,))
    (w_ssd_in,) = _gather_wait_forward(_gather_forward(a_bufs, name="gather_a_pass"), name="gather_a_done")
    ssd_in_full = jnp.moveaxis(w_ssd_in, 0, 1).reshape(D, N_CHIPS * SSD_IN_SHARD)
    w_zx, w_dt = ssd_in_full[:, :ZX], jnp.pad(ssd_in_full[:, ZX:], ((0, 0), (0, LANES - NH)))
    scan = _ssd_fwd_scan(xs, mods[0][0:3], mix_nw[0], w_zx, w_dt, ssd_conv, ssd_conv_b, prm, "ssd")
    fly_b = _gather_forward(_gather_wait_first(fly_b, name="gather_b_landed", after=(scan[3],)), name="gather_b_pass")
    w_ssd_out, w_b = _gather_wait_forward(fly_b, name="gather_b_done", after=(scan[4],))
    x1, s_ssd = _ssd_fwd_out(xs, mods[0][0:3], scan, ssd_norm_w, w_ssd_out, "ssd")
    x2, s_mlp0 = _mlp_fwd(x1, mods[0][3:6], mlp_nw[0], w_b, up_row, down_row, "mlp0")
    c_bufs = _gather_wait_first(fly_c, name="gather_c_landed", after=(x2,))
    w_sc_in, w_c = _gather_wait_forward(_gather_forward(c_bufs, name="gather_c_pass"), name="gather_c_done")
    x3, s_sc = _sc_layer_fwd(x2, mods[1][0:3], mix_nw[1], w_sc_in, sc_conv, w_c, sc_out_row, "sc")
    x4, s_mlp1 = _mlp_fwd(x3, mods[1][3:6], mlp_nw[1], w_c, up_row, down_row, "mlp1")

    core = ci.reshape(1).astype(jnp.int32)
    chip_core = jnp.stack([chip, ci]).astype(jnp.int32)

    def reduce_swap(gbufs, tag, after=()):
        return _sibling_start(gbufs, name=tag + "_sibling_start", after=after)

    def reduce_send(flight, tag, after):
        gs, sib = _sibling_wait(flight, name=tag + "_sibling_landed", after=after)
        hs = [_add_sibling_half(g, s, core, name=f"{tag}_add_sibling{b}") for b, (g, s) in enumerate(zip(gs, sib))]
        return _owners_start(hs, name=tag + "_owners_start")

    def reduce_sum(flight, tag, after):
        hs, lands = _owners_wait(flight, name=tag + "_owners_landed", after=after)
        ts = [_add_chip_sums(h, o, chip_core, name=f"{tag}_add_chips{b}") for b, (h, o) in enumerate(zip(hs, lands))]
        return _result_start(ts, name=tag + "_result_start")

    def reduce_done(flight, tag, after=()):
        return [t.reshape(-1, t.shape[2]) for t in _result_wait(flight, name=tag + "_result_landed", after=after)]

    dx4, fsum = _final_loss(x4, row(final_norm_w), tgt, name="final_loss")
    dx3, g_c, sum_mlp1 = _mlp_bwd(dx4, s_mlp1, mods[1][3:6], mlp_nw[1], w_c, None, up_row, down_row, "mlp1")
    dx2, g_c, g_sc_in, sum_sc, sc_csum = _sc_layer_bwd(dx3, s_sc, mods[1][0:3], mix_nw[1], w_sc_in, sc_conv, w_c, g_c,
                                                       sc_out_row, "sc")
    dx1, g_b, sum_mlp0 = _mlp_bwd(dx2, s_mlp0, mods[0][3:6], mlp_nw[0], w_b, None, up_row, down_row, "mlp0")
    dyn, g_ssd_out, gsum_ssd = _ssd_bwd_out(dx1, s_ssd, mods[0][0:3], w_ssd_out, "ssd")
    fly_1, tok = reduce_swap([g_c, g_sc_in, g_b, g_ssd_out], "rs1")
    dy, dzx, gnsum = _gnorm_bwd(s_ssd[5], s_ssd[2], ssd_norm_w + tok[0:1, 0:1], dyn, name="ssd_dgnorm")
    fly_1, tok = reduce_send(fly_1, "rs1", (dy,))
    grad_x, d_w_zx, d_w_dt, sum_ssd, csum, ssum = _ssd_bwd_rest(
        dx1, dy, dzx, gsum_ssd, s_ssd, mods[0][0:3], mix_nw[0], w_zx, w_dt, ssd_conv, ssd_conv_b, prm + tok[0:1, 0:1], "ssd")
    fly_1, tok = reduce_sum(fly_1, "rs1", (grad_x,))

    def ssd_in_owner(k):
        lo, hi = k * SSD_IN_SHARD, (k + 1) * SSD_IN_SHARD
        if hi <= ZX:
            return d_w_zx[:, lo:hi]
        return jnp.concatenate([d_w_zx[:, lo:], d_w_dt[:, :hi - ZX]], axis=1)

    small = jnp.concatenate([sum_ssd + tok[0:1, 0:1], sum_mlp0, sum_sc, sum_mlp1, csum.reshape(24, D), gnsum.reshape(16, D),
                             fsum, sc_csum, jnp.pad(ssum, ((0, 0), (0, D - LANES)))], axis=0)
    small_all = _all_gather_rows(small, name="gather_small").reshape(N_DEV, SMALL_ROWS, D)
    fly_2, tok = reduce_swap([jnp.stack([ssd_in_owner(k) for k in range(N_CHIPS)]).astype(BF16)], "rs2", (small_all,))
    fly_2, tok = reduce_send(fly_2, "rs2", (tok,))
    t_c, t_sc_in, t_b, t_ssd_out = reduce_done(fly_1, "rs1", (tok,))
    small_all = small_all + tok[0:1, 0:1]
    tot = _sum_devices(small_all, name="sum_small")
    loss = tot[FINAL_ROW + 1, 0]
    mod_rows = [r + o for r in SUB_ROW for o in (3, 2, 0)]
    g_ada_b = jnp.stack([tot[r] for r in mod_rows]).reshape(2, 6 * D)
    g_mix_norm = jnp.stack([tot[SUB_ROW[0] + 1], tot[SUB_ROW[2] + 1]])
    g_mlp_norm = jnp.stack([tot[SUB_ROW[1] + 1], tot[SUB_ROW[3] + 1]])
    conv_sums = tot[SSD_CONV_ROW:SSD_CONV_ROW + 24].reshape(8, CONVD)
    g_ssd_conv_w = lax.dynamic_slice(conv_sums, (0, chip * 768), (4, 768))[None]
    g_ssd_conv_b = conv_sums[4:5]
    g_ssd_norm = tot[GNORM_ROW:GNORM_ROW + 2].reshape(1, DI)
    g_final = tot[FINAL_ROW]
    g_sc_conv_w = lax.dynamic_slice(tot[SC_CONV_ROW:SC_CONV_ROW + 3], (0, chip * 256), (3, 256))[None]
    g_a_log, g_d, g_dt_bias = (tot[HEAD_ROW + r:HEAD_ROW + r + 1, 0:NH] for r in range(3))
    c_pad = jnp.concatenate([c_all, jnp.zeros((8, D), F32)], axis=0)
    dmod_all = jnp.stack([small_all[:, r] for r in mod_rows], axis=1).reshape(N_DEV, 2, 6 * D)
    g_ada_w = []
    for i in range(2):
        dm = lax.dynamic_slice(dmod_all[:, i], (0, chip * n_ada), (N_DEV, n_ada))
        g_ada_w.append(_matmul_tn(c_pad, jnp.concatenate([dm, jnp.zeros_like(dm)], axis=0), m=D, n=n_ada, a_silu=True,
                                  name=f"ada_dw{i}"))

    big = dict(ada_w=[(g, 0) for g in g_ada_w], mlp_up=[(t_b, up_row), (t_c, up_row)], mlp_down=[(t_b, down_row), (t_c, down_row)],
               ssd_out_w=[(t_ssd_out, 0)], sc_out_w=[(t_c, sc_out_row)], sc_in_w=[(t_sc_in, 0)], ssd_in_w=None)
    grads = dict(ada_b=g_ada_b, mix_norm_w=g_mix_norm, mlp_norm_w=g_mlp_norm, ssd_conv_w=g_ssd_conv_w,
                 ssd_conv_b=g_ssd_conv_b, ssd_dt_bias=g_dt_bias, ssd_A_log=g_a_log, ssd_D=g_d, ssd_norm_w=g_ssd_norm,
                 sc_conv_w=g_sc_conv_w, final_norm_w=g_final)
    weights = dict(ada_w=(ada_w, m_ada_w, v_ada_w), ada_b=(ada_b, m_ada_b, v_ada_b),
                   mix_norm_w=(mix_norm_w, m_mix_norm_w, v_mix_norm_w), mlp_norm_w=(mlp_norm_w, m_mlp_norm_w, v_mlp_norm_w),
                   mlp_up=(mlp_up, m_mlp_up, v_mlp_up), mlp_down=(mlp_down, m_mlp_down, v_mlp_down),
                   ssd_in_w=(ssd_in_w, m_ssd_in_w, v_ssd_in_w), ssd_conv_w=(ssd_conv_w, m_ssd_conv_w, v_ssd_conv_w),
                   ssd_conv_b=(ssd_conv_b, m_ssd_conv_b, v_ssd_conv_b), ssd_dt_bias=(ssd_dt_bias, m_ssd_dt_bias, v_ssd_dt_bias),
                   ssd_A_log=(ssd_A_log, m_ssd_A_log, v_ssd_A_log), ssd_D=(ssd_D, m_ssd_D, v_ssd_D),
                   ssd_norm_w=(ssd_norm_w, m_ssd_norm_w, v_ssd_norm_w), ssd_out_w=(ssd_out_w, m_ssd_out_w, v_ssd_out_w),
                   sc_in_w=(sc_in_w, m_sc_in_w, v_sc_in_w), sc_conv_w=(sc_conv_w, m_sc_conv_w, v_sc_conv_w),
                   sc_out_w=(sc_out_w, m_sc_out_w, v_sc_out_w), final_norm_w=(final_norm_w, m_final_norm_w, v_final_norm_w))
    def step(nm, parts):
        w, m, v = (t if t.shape[0] == 1 else t.reshape(-1, t.shape[-1]) for t in weights[nm])
        rows, outs = w.shape[-2] // len(parts), None
        for i, (gbuf, g_row) in enumerate(parts):
            outs = _adamw(w, gbuf, m, v, g_row=g_row, w_row=i * rows, rows=rows, into=outs, emit_g=True, name=f"adamw_{nm}{i}")
        return outs

    res = {}
    for nm, (w, m, v) in weights.items():
        two_d = (-1, w.shape[-1]) if w.ndim > 1 else (1, -1)
        if nm not in big:
            res[nm] = (grads[nm], *_adamw(w.reshape(two_d), grads[nm].reshape(two_d), m.reshape(two_d), v.reshape(two_d),
                                          name="adamw_" + nm))
        elif big[nm] is not None:
            res[nm] = step(nm, big[nm])
    fly_2, tok = reduce_sum(fly_2, "rs2", tuple(r[1] for r in res.values()))
    (t_ssd_in,) = reduce_done(fly_2, "rs2", (tok,))
    w_t, m_t, v_t = (jnp.swapaxes(t[0], 0, 1) for t in weights["ssd_in_w"])
    res["ssd_in_w"] = [jnp.swapaxes(o, 0, 1) for o in _adamw(w_t, t_ssd_in.T, m_t, v_t, emit_g=True, name="adamw_ssd_in_w")]
    outs = [[res[nm][k].reshape(weights[nm][0].shape) for nm in weights] for k in range(4)]
    return (loss, grad_x[None], *outs[0], *outs[1], *outs[2], *outs[3])
```

```python
import jax
import jax.numpy as jnp
from jax import lax
from jax.experimental import pallas as pl
from jax.experimental.pallas import tpu as pltpu

F32 = jnp.float32
BF16 = jnp.bfloat16
MESH = pl.DeviceIdType.MESH

D = 1024
DFF = 4096
DI = 2048
NH = 32
HP = 64
NG = 4
NS = 128
CH = 128
CONVD = DI + 2 * NG * NS
ZX = DI + CONVD
GW = NG * NS
LANES = 128
N_CHIPS = 4
N_DEV = 8
EPS = 1e-5
ADAM_LR, ADAM_B1, ADAM_B2, ADAM_EPS, ADAM_WD, ADAM_STEP = 1e-3, 0.9, 0.999, 1e-8, 0.01, 10
VMEM_LIMIT = 48 * 1024 * 1024
TM_ALL = 2048
TM_HALF = 1024
ANY = pl.BlockSpec(memory_space=pl.ANY)
SEM = pl.BlockSpec(memory_space=pltpu.SEMAPHORE)

SSD_IN_SHARD = 1288
SC_IN_SHARD = 768


def _params(sem=None):
    return pltpu.CompilerParams(dimension_semantics=sem, vmem_limit_bytes=VMEM_LIMIT)


def _sigmoid(v):
    return 1.0 / (1.0 + jnp.exp(-v))


def _dot(a, b, dims=((1,), (0,)), precision=None):
    return lax.dot_general(a, b, (dims, ((), ())), preferred_element_type=F32, precision=precision)


def _dot_nt(a, b):
    return _dot(a, b, ((1,), (1,)))


def _dot_tn(a, b):
    return _dot(a, b, ((0,), (0,)))


def _nn(av, bv):
    return _dot(av.astype(BF16), bv.astype(BF16))


def _nt(av, bv):
    return _dot_nt(av.astype(BF16), bv.astype(BF16))


def _nn_split(av, bv):
    return _dot(av.astype(BF16), bv.reshape(-1, bv.shape[2]))


def _nn_split_sq(av, bv):
    af = av.astype(F32)
    return _nn_split(af * af, bv)


def _nt_split(av, bv):
    kc = bv.shape[2]
    acc = _dot_nt(av[:, 0:kc].astype(BF16), bv[0])
    for s in range(1, bv.shape[0]):
        acc = acc + _dot_nt(av[:, s * kc:(s + 1) * kc].astype(BF16), bv[s])
    return acc


def _nt_sc_in(av, bv):
    q = 256
    acc = None
    for i in range(3 * D // q):
        a_blk = av[i // 4][:, (i % 4) * q:(i % 4 + 1) * q]
        b_blk = bv[i // 3][:, (i % 3) * q:(i % 3 + 1) * q]
        t = _dot_nt(a_blk, b_blk)
        acc = t if acc is None else acc + t
    return acc


def _matmul(a, b, *, name, n, contract=_nn, a_spec=None, b_spec=None, tm=512, tn=512, extras=(), epi=None,
            out_dtypes=(F32,), a_silu=False):
    M = a.shape[-2]
    tm, tn = min(tm, M), min(tn, n)
    assert M % tm == 0 and n % tn == 0, (name, M, n, tm, tn)
    n_ex = len(extras)
    if a_spec is None:
        a_spec = pl.BlockSpec((tm, a.shape[1]), lambda i, j: (i, 0))
    if b_spec is None:
        b_spec = (pl.BlockSpec((tn, b.shape[1]), lambda i, j: (j, 0)) if contract is _nt
                  else pl.BlockSpec((b.shape[0], tn), lambda i, j: (0, j)))

    def body(*refs):
        av = refs[0][...]
        if a_silu:
            av = av * _sigmoid(av)
        acc = contract(av, refs[1][...])
        res = epi(acc, *[r[...] for r in refs[2:2 + n_ex]]) if epi is not None else (acc,)
        for o_ref, r in zip(refs[2 + n_ex:], res, strict=True):
            o_ref[...] = r.astype(o_ref.dtype)

    in_specs = [a_spec, b_spec]
    for e in extras:
        in_specs.append(pl.BlockSpec((1, tn), lambda i, j: (0, j)) if e.shape[0] == 1 and M != 1
                        else pl.BlockSpec((tm, tn), lambda i, j: (i, j)))
    outs = pl.pallas_call(
        body, grid=(M // tm, n // tn), in_specs=in_specs,
        out_specs=[pl.BlockSpec((tm, tn), lambda i, j: (i, j)) for _ in out_dtypes],
        out_shape=[jax.ShapeDtypeStruct((M, n), dt) for dt in out_dtypes],
        compiler_params=_params(("parallel", "parallel")), name=name)(a, b, *extras)
    return outs if len(out_dtypes) > 1 else outs[0]


def _matmul_tn(a, b, *, name, m, n, tm=512, tn=512, a_spec=None, b_spec=None, out_spec=None, out_struct=None, into=None,
               a_silu=False, a_square=False):
    T = a.shape[-2]
    tm, tn = min(tm, m), min(tn, n)
    assert m % tm == 0 and n % tn == 0, (name, m, n, tm, tn)
    if a_spec is None:
        a_spec = pl.BlockSpec((T, tm), lambda i, j: (0, i))
    if b_spec is None:
        b_spec = pl.BlockSpec((T, tn), lambda i, j: (0, j))
    if out_spec is None:
        out_spec, out_struct = pl.BlockSpec((tm, tn), lambda i, j: (i, j)), jax.ShapeDtypeStruct((m, n), F32)

    def body(a_ref, b_ref, *rest):
        av = a_ref[...]
        if a_silu:
            av = av * _sigmoid(av)
        if a_square:
            av = av.astype(F32) * av.astype(F32)
        rest[-1][...] = _dot_tn(av.astype(BF16), b_ref[...].astype(BF16)).astype(rest[-1].dtype)

    args, in_specs, alias = [a, b], [a_spec, b_spec], {}
    if into is not None:
        args, in_specs, alias = args + [into], in_specs + [ANY], {2: 0}
    return pl.pallas_call(body, grid=(m // tm, n // tn), in_specs=in_specs, out_specs=out_spec, out_shape=out_struct,
                          input_output_aliases=alias, compiler_params=_params(("parallel", "parallel")), name=name)(*args)


def _modnorm_fwd(x, nw, sc, sh, *, name):
    L = x.shape[0]
    tm = min(L, 512)

    def body(x_ref, nw_ref, sc_ref, sh_ref, h_ref):
        xv = x_ref[...]
        r = lax.rsqrt(jnp.mean(xv * xv, axis=-1, keepdims=True) + EPS)
        h_ref[...] = ((xv * r * nw_ref[...]) * (1.0 + sc_ref[...]) + sh_ref[...]).astype(BF16)

    row = pl.BlockSpec((tm, D), lambda i: (i, 0))
    vec = pl.BlockSpec((1, D), lambda i: (0, 0))
    return pl.pallas_call(body, grid=(L // tm,), in_specs=[row, vec, vec, vec], out_specs=row,
                          out_shape=jax.ShapeDtypeStruct((L, D), BF16),
                          compiler_params=_params(("parallel",)), name=name)(x, nw, sc, sh)


def _modnorm_bwd(x, dh, dxo, nw, sc, gsum, *, name):
    L = x.shape[0]
    tm = min(L, 256)

    def body(x_ref, dh_ref, dxo_ref, nw_ref, sc_ref, g_ref, dx_ref, s_ref):
        @pl.when(pl.program_id(0) == 0)
        def _():
            s_ref[...] = g_ref[...]

        xv, dhv = x_ref[...], dh_ref[...]
        r = lax.rsqrt(jnp.mean(xv * xv, axis=-1, keepdims=True) + EPS)
        xhat = xv * r
        dxhat = dhv * (nw_ref[...] * (1.0 + sc_ref[...]))
        dx_ref[...] = dxo_ref[...] + r * (dxhat - xhat * jnp.mean(dxhat * xhat, axis=-1, keepdims=True))
        s_ref[1:2, :] += jnp.sum(dhv * xhat, axis=0, keepdims=True) * (1.0 + sc_ref[...])
        s_ref[2:3, :] += jnp.sum(dhv * xhat, axis=0, keepdims=True) * nw_ref[...]
        s_ref[3:4, :] += jnp.sum(dhv, axis=0, keepdims=True)

    row = pl.BlockSpec((tm, D), lambda i: (i, 0))
    vec = pl.BlockSpec((1, D), lambda i: (0, 0))
    blk = pl.BlockSpec((8, D), lambda i: (0, 0))
    return pl.pallas_call(body, grid=(L // tm,), in_specs=[row, row, row, vec, vec, blk], out_specs=[row, blk],
                          out_shape=[jax.ShapeDtypeStruct((L, D), F32), jax.ShapeDtypeStruct((8, D), F32)],
                          compiler_params=_params(("arbitrary",)), name=name)(x, dh, dxo, nw, sc, gsum)


def _gate_bwd(dxo, y, g, *, name):
    L = dxo.shape[0]
    tm = min(L, 512)

    def body(dxo_ref, y_ref, g_ref, dy_ref, s_ref):
        @pl.when(pl.program_id(0) == 0)
        def _():
            s_ref[...] = jnp.zeros_like(s_ref)

        dv = dxo_ref[...]
        dy_ref[...] = (dv * g_ref[...]).astype(BF16)
        s_ref[0:1, :] += jnp.sum(dv * y_ref[...].astype(F32), axis=0, keepdims=True)

    row = pl.BlockSpec((tm, D), lambda i: (i, 0))
    return pl.pallas_call(body, grid=(L // tm,), in_specs=[row, row, pl.BlockSpec((1, D), lambda i: (0, 0))],
                          out_specs=[row, pl.BlockSpec((8, D), lambda i: (0, 0))],
                          out_shape=[jax.ShapeDtypeStruct((L, D), BF16), jax.ShapeDtypeStruct((8, D), F32)],
                          compiler_params=_params(("arbitrary",)), name=name)(dxo, y, g)


def _final_loss(x, fw, tgt, *, name):
    L = x.shape[0]
    tm = min(L, 256)

    def body(x_ref, fw_ref, t_ref, dx_ref, s_ref):
        @pl.when(pl.program_id(0) == 0)
        def _():
            s_ref[...] = jnp.zeros_like(s_ref)

        xv = x_ref[...]
        r = lax.rsqrt(jnp.mean(xv * xv, axis=-1, keepdims=True) + EPS)
        xhat = xv * r
        diff = xhat * fw_ref[...] - t_ref[...]
        dout = diff * (1.0 / D)
        dxhat = dout * fw_ref[...]
        dx_ref[...] = r * (dxhat - xhat * jnp.mean(dxhat * xhat, axis=-1, keepdims=True))
        s_ref[0:1, :] += jnp.sum(dout * xhat, axis=0, keepdims=True)
        s_ref[1:2, :] += jnp.zeros((1, D), F32) + 0.5 * jnp.sum(jnp.sum(diff * diff, axis=-1, keepdims=True) * (1.0 / D))

    row = pl.BlockSpec((tm, D), lambda i: (i, 0))
    return pl.pallas_call(body, grid=(L // tm,), in_specs=[row, pl.BlockSpec((1, D), lambda i: (0, 0)), row],
                          out_specs=[row, pl.BlockSpec((8, D), lambda i: (0, 0))],
                          out_shape=[jax.ShapeDtypeStruct((L, D), F32), jax.ShapeDtypeStruct((8, D), F32)],
                          compiler_params=_params(("arbitrary",)), name=name)(x, fw, tgt)


def _shift_down(v, j):
    if j == 0:
        return v
    rolled = pltpu.roll(v, j, 0)
    row = lax.broadcasted_iota(jnp.int32, (8, v.shape[1]), 0)
    return jnp.concatenate([jnp.where(row >= j, rolled[0:8], 0.0), rolled[8:]], axis=0)


def _shift_up(v, j):
    if j == 0:
        return v
    n = v.shape[0]
    rolled = pltpu.roll(v, n - j, 0)
    row = lax.broadcasted_iota(jnp.int32, (8, v.shape[1]), 0)
    return jnp.concatenate([rolled[:n - 8], jnp.where(row < 8 - j, rolled[n - 8:], 0.0)], axis=0)


def _ssd_conv_fwd(zx, w, b, *, name):
    L = zx.shape[0]
    cb = 256
    k = w.shape[0]

    def body(x_ref, w_ref, b_ref, o_ref):
        xv = x_ref[...].astype(F32)
        pre = b_ref[...] + xv * w_ref[k - 1:k, :]
        for j in range(1, k):
            pre = pre + _shift_down(xv, j) * w_ref[k - 1 - j:k - j, :]
        o_ref[...] = (pre * _sigmoid(pre)).astype(BF16)

    return pl.pallas_call(
        body, grid=(CONVD // cb,),
        in_specs=[pl.BlockSpec((L, cb), lambda i: (0, i + DI // cb)), pl.BlockSpec((k, cb), lambda i: (0, i)),
                  pl.BlockSpec((1, cb), lambda i: (0, i))],
        out_specs=pl.BlockSpec((L, cb), lambda i: (0, i)), out_shape=jax.ShapeDtypeStruct((L, CONVD), BF16),
        compiler_params=_params(("parallel",)), name=name)(zx, w, b)


def _ssd_conv_bwd(zx, dact, w, b, dzx, *, name):
    L = zx.shape[0]
    cb = 256
    k = w.shape[0]

    def body(x_ref, da_ref, w_ref, b_ref, _, dx_ref, s_ref):
        xv = x_ref[...].astype(F32)
        sh = [_shift_down(xv, j) for j in range(k)]
        pre = b_ref[...] + sh[0] * w_ref[k - 1:k, :]
        for j in range(1, k):
            pre = pre + sh[j] * w_ref[k - 1 - j:k - j, :]
        s = _sigmoid(pre)
        dpre = da_ref[...].astype(F32) * (s * (1.0 + pre * (1.0 - s)))
        dx = dpre * w_ref[k - 1:k, :]
        for j in range(1, k):
            dx = dx + _shift_up(dpre, j) * w_ref[k - 1 - j:k - j, :]
        dx_ref[...] = dx.astype(BF16)
        s_ref[...] = jnp.zeros_like(s_ref)
        for j in range(k):
            s_ref[k - 1 - j:k - j, :] = jnp.sum(dpre * sh[j], axis=0, keepdims=True)
        s_ref[k:k + 1, :] = jnp.sum(dpre, axis=0, keepdims=True)

    return pl.pallas_call(
        body, grid=(CONVD // cb,),
        in_specs=[pl.BlockSpec((L, cb), lambda i: (0, i + DI // cb)), pl.BlockSpec((L, cb), lambda i: (0, i)),
                  pl.BlockSpec((k, cb), lambda i: (0, i)), pl.BlockSpec((1, cb), lambda i: (0, i)), ANY],
        out_specs=[pl.BlockSpec((L, cb), lambda i: (0, i + DI // cb)), pl.BlockSpec((8, cb), lambda i: (0, i))],
        out_shape=[jax.ShapeDtypeStruct((L, ZX), BF16), jax.ShapeDtypeStruct((8, CONVD), F32)],
        input_output_aliases={4: 0}, compiler_params=_params(("parallel",)), name=name)(zx, dact, w, b, dzx)


def _sc_fwd(proj, w, *, name):
    L = proj.shape[0]
    cb = 256
    nb = D // cb
    k = w.shape[0]

    def body(b_ref, c_ref, x_ref, w_ref, o_ref):
        u = c_ref[...].astype(F32) * x_ref[...].astype(F32)
        v = u * w_ref[k - 1:k, :]
        for j in range(1, k):
            v = v + _shift_down(u, j) * w_ref[k - 1 - j:k - j, :]
        o_ref[...] = (b_ref[...].astype(F32) * v).astype(BF16)

    return pl.pallas_call(
        body, grid=(nb,),
        in_specs=[pl.BlockSpec((L, cb), lambda i: (0, i)), pl.BlockSpec((L, cb), lambda i: (0, i + nb)),
                  pl.BlockSpec((L, cb), lambda i: (0, i + 2 * nb)), pl.BlockSpec((k, cb), lambda i: (0, i))],
        out_specs=pl.BlockSpec((L, cb), lambda i: (0, i)), out_shape=jax.ShapeDtypeStruct((L, D), BF16),
        compiler_params=_params(("parallel",)), name=name)(proj, proj, proj, w)


def _sc_bwd(proj, dyv, w, *, name):
    L = proj.shape[0]
    cb = 256
    nb = D // cb
    k = w.shape[0]

    def body(b_ref, c_ref, x_ref, dy_ref, w_ref, dp_ref, s_ref):
        cv, xv = c_ref[...].astype(F32), x_ref[...].astype(F32)
        u = cv * xv
        sh = [_shift_down(u, j) for j in range(k)]
        v = sh[0] * w_ref[k - 1:k, :]
        for j in range(1, k):
            v = v + sh[j] * w_ref[k - 1 - j:k - j, :]
        dyv_ = dy_ref[...]
        dp_ref[0] = (dyv_ * v).astype(BF16)
        dv = dyv_ * b_ref[...].astype(F32)
        du = dv * w_ref[k - 1:k, :]
        for j in range(1, k):
            du = du + _shift_up(dv, j) * w_ref[k - 1 - j:k - j, :]
        dp_ref[1] = (du * xv).astype(BF16)
        dp_ref[2] = (du * cv).astype(BF16)
        s_ref[...] = jnp.zeros_like(s_ref)
        for j in range(k):
            s_ref[k - 1 - j:k - j, :] = jnp.sum(dv * sh[j], axis=0, keepdims=True)

    blk = pl.BlockSpec((L, cb), lambda i: (0, i))
    return pl.pallas_call(
        body, grid=(nb,),
        in_specs=[blk, pl.BlockSpec((L, cb), lambda i: (0, i + nb)), pl.BlockSpec((L, cb), lambda i: (0, i + 2 * nb)),
                  blk, pl.BlockSpec((k, cb), lambda i: (0, i))],
        out_specs=[pl.BlockSpec((3, L, cb), lambda i: (0, 0, i)), pl.BlockSpec((8, cb), lambda i: (0, i))],
        out_shape=[jax.ShapeDtypeStruct((3, L, D), BF16), jax.ShapeDtypeStruct((8, D), F32)],
        compiler_params=_params(("parallel",)), name=name)(proj, proj, proj, dyv, w)


def _pieces(v, n):
    out, rest = [], v
    for _ in range(n):
        out.append(rest.astype(BF16))
        rest = rest - out[-1].astype(F32)
    return out


def _cumsum_rows(mask, v):
    m = mask.astype(BF16)
    return _dot(jnp.concatenate([m, m, m], axis=1), jnp.concatenate(_pieces(v, 3), axis=0))


def _ssd_chunk_terms(dtr, prm):
    lane = lax.broadcasted_iota(jnp.int32, (CH, LANES), 1)
    valid = lane < NH
    xdt = dtr + prm[0:1, :]
    dt = jnp.where(valid, jnp.maximum(xdt, 0.0) + jnp.log1p(jnp.exp(-jnp.abs(xdt))), 0.0)
    A = -jnp.exp(prm[1:2, :])
    ri = lax.broadcasted_iota(jnp.int32, (CH, CH), 0)
    ci = lax.broadcasted_iota(jnp.int32, (CH, CH), 1)
    cs = _cumsum_rows(ri >= ci, dt * A)
    last = cs[CH - 1:CH, :]
    spread = (lax.broadcasted_iota(jnp.int32, (2 * LANES, DI), 1) // HP
              == lax.broadcasted_iota(jnp.int32, (2 * LANES, DI), 0) % LANES).astype(BF16)
    gather = ((lax.broadcasted_iota(jnp.int32, (LANES, 2 * DI), 1) % DI) // HP
              == lax.broadcasted_iota(jnp.int32, (LANES, 2 * DI), 0)).astype(BF16)
    return dict(valid=valid, xdt=xdt, dt=dt, A=A, cs=cs, csT=cs.T, last=last, ri=ri, ci=ci, ex=(spread, gather))


def _expand(v, ex):
    if v.shape[0] == 1:
        return _expand(jnp.broadcast_to(v, (8, LANES)), ex)[0:1, :]
    return _dot(jnp.concatenate(_pieces(v, 2), axis=1), ex[0])


def _head_sum(v, ex):
    if v.shape[0] == 1:
        return _head_sum(jnp.broadcast_to(v, (8, DI)), ex)[0:1, :]
    return _dot_nt(jnp.concatenate(_pieces(v, 2), axis=1), ex[1])


def _ssd_fwd(xbc, dtr, prm, *, name):
    L = xbc.shape[0]
    nc = L // CH

    def body(xbc_ref, dtr_ref, prm_ref, y_ref, sp_ref, st_ref):
        @pl.when(pl.program_id(0) == 0)
        def _():
            st_ref[...] = jnp.zeros_like(st_ref)

        prm_v = prm_ref[...]
        t = _ssd_chunk_terms(dtr_ref[...], prm_v)
        cs, csT, ex, causal = t["cs"], t["csT"], t["ex"], t["ri"] >= t["ci"]
        xs = xbc_ref[:, 0:DI].astype(F32)
        X = xs * _expand(t["dt"], ex)
        Xb = X.astype(BF16)
        Xd = (X * _expand(jnp.exp(t["last"] - cs), ex)).astype(BF16)
        Ex = _expand(jnp.exp(cs), ex)
        cdx = _expand(jnp.exp(t["last"]), ex)
        dskx = _expand(prm_v[2:3, :], ex)
        lane = lax.broadcasted_iota(jnp.int32, (CH, LANES), 1)
        sp_ref[0] = st_ref[...]
        for g in range(NG):
            Bg = xbc_ref[:, DI + g * NS:DI + (g + 1) * NS].astype(BF16)
            Cg = xbc_ref[:, DI + GW + g * NS:DI + GW + (g + 1) * NS].astype(BF16)
            G = _dot_nt(Cg, Bg)
            Sg = st_ref[:, g * GW:(g + 1) * GW]
            yoff = _dot(Cg, Sg.astype(BF16)) * Ex[:, g * GW:(g + 1) * GW]
            for j in range(GW // LANES):
                lo = g * GW + j * LANES
                Xp = Xb[:, lo:lo + LANES]
                yd = []
                for h in (lo // HP, lo // HP + 1):
                    seg = cs[:, h:h + 1] - csT[h:h + 1, :]
                    yd.append(_dot((G * jnp.where(causal, jnp.exp(seg), 0.0)).astype(BF16), Xp))
                y_ref[:, lo:lo + LANES] = (jnp.where(lane < HP, yd[0], yd[1]) + yoff[:, j * LANES:(j + 1) * LANES]
                                           + dskx[:, lo:lo + LANES] * xs[:, lo:lo + LANES]).astype(BF16)
            st_ref[:, g * GW:(g + 1) * GW] = Sg * cdx[:, g * GW:(g + 1) * GW] + _dot_tn(Bg, Xd[:, g * GW:(g + 1) * GW])

    return pl.pallas_call(
        body, grid=(nc,),
        in_specs=[pl.BlockSpec((CH, CONVD), lambda c: (c, 0)), pl.BlockSpec((CH, LANES), lambda c: (c, 0)),
                  pl.BlockSpec((8, LANES), lambda c: (0, 0))],
        out_specs=[pl.BlockSpec((CH, DI), lambda c: (c, 0)), pl.BlockSpec((1, NS, DI), lambda c: (c, 0, 0))],
        out_shape=[jax.ShapeDtypeStruct((L, DI), BF16), jax.ShapeDtypeStruct((nc, NS, DI), F32)],
        scratch_shapes=[pltpu.VMEM((NS, DI), F32)],
        compiler_params=_params(("arbitrary",)), name=name)(xbc, dtr, prm)


def _ssd_bwd(xbc, dtr, prm, dy, sprev, *, name):
    L = xbc.shape[0]
    nc = L // CH

    def body(xbc_ref, dtr_ref, prm_ref, dy_ref, sp_ref, dxbc_ref, ddtr_ref, s_ref, dst_ref, dx_scr, de_scr, dd_scr):
        step = pl.program_id(0)

        @pl.when(step == 0)
        def _():
            dst_ref[...] = jnp.zeros_like(dst_ref)
            s_ref[...] = jnp.zeros_like(s_ref)

        prm_v = prm_ref[...]
        t = _ssd_chunk_terms(dtr_ref[...], prm_v)
        cs, csT, ex, ri, ci = t["cs"], t["csT"], t["ex"], t["ri"], t["ci"]
        E = jnp.exp(cs)
        dec = jnp.exp(t["last"] - cs)
        cd = jnp.exp(t["last"])
        xs = xbc_ref[:, 0:DI].astype(F32)
        dtx = _expand(t["dt"], ex)
        X = xs * dtx
        Xb = X.astype(BF16)
        decx = _expand(dec, ex)
        Xd = (X * decx).astype(BF16)
        Ex = _expand(E, ex)
        cdx = _expand(cd, ex)
        dskx = _expand(prm_v[2:3, :], ex)
        lane = lax.broadcasted_iota(jnp.int32, (CH, LANES), 1)
        dcs = jnp.zeros((CH, LANES), F32)
        dcd_x = []
        for g in range(NG):
            gs = slice(g * GW, (g + 1) * GW)
            Bg = xbc_ref[:, DI + g * NS:DI + (g + 1) * NS].astype(BF16)
            Cg = xbc_ref[:, DI + GW + g * NS:DI + GW + (g + 1) * NS].astype(BF16)
            G = _dot_nt(Cg, Bg)
            GT = _dot_nt(Bg, Cg)
            Sg = sp_ref[0, :, gs]
            Sgb = Sg.astype(BF16)
            dyg = dy_ref[:, gs]
            de_scr[:, gs] = dyg * _dot(Cg, Sgb)
            dYo = (Ex[:, gs] * dyg).astype(BF16)
            dC = _dot_nt(dYo, Sgb)
            dS_in = _dot_tn(Cg, dYo)
            dStg = dst_ref[:, gs]
            dStb = dStg.astype(BF16)
            dXd = _dot(Bg, dStb)
            dB = _dot_nt(Xd[:, gs], dStb)
            dd_scr[:, gs] = dXd * X[:, gs]
            dXst = dXd * decx[:, gs]
            dG = jnp.zeros((CH, CH), F32)
            dGT = jnp.zeros((CH, CH), F32)
            for j in range(GW // LANES):
                lo = g * GW + j * LANES
                Xp = Xb[:, lo:lo + LANES]
                dyp = dy_ref[:, lo:lo + LANES]
                dXp = dXst[:, j * LANES:(j + 1) * LANES]
                for k, h in enumerate((lo // HP, lo // HP + 1)):
                    dyh = jnp.where((lane < HP) if k == 0 else (lane >= HP), dyp, 0.0).astype(BF16)
                    seg = cs[:, h:h + 1] - csT[h:h + 1, :]
                    Lm = jnp.where(ri >= ci, jnp.exp(seg), 0.0)
                    LmT = jnp.where(ci >= ri, jnp.exp(-seg), 0.0)
                    dM = _dot_nt(dyh, Xp)
                    dMT = _dot_nt(Xp, dyh)
                    MT = GT * LmT
                    rs = jnp.sum(dM * (G * Lm), axis=1, keepdims=True) - jnp.sum(dMT * MT, axis=1, keepdims=True)
                    dcs = dcs + jnp.where(lane == h, rs, 0.0)
                    dG = dG + dM * Lm
                    dGT = dGT + dMT * LmT
                    dXp = dXp + _dot(MT.astype(BF16), dyh)
                dx_scr[:, lo:lo + LANES] = dXp
            dxbc_ref[:, DI + g * NS:DI + (g + 1) * NS] = (dB + _dot(dGT.astype(BF16), Cg)).astype(BF16)
            dxbc_ref[:, DI + GW + g * NS:DI + GW + (g + 1) * NS] = (dC + _dot(dG.astype(BF16), Bg)).astype(BF16)
            dcd_x.append(jnp.sum(dStg * Sg, axis=0, keepdims=True))
            dst_ref[:, gs] = dStg * cdx[:, gs] + dS_in
        dX = dx_scr[...]
        dy = dy_ref[...]
        ddec = _head_sum(dd_scr[...], ex)
        dcd = _head_sum(jnp.concatenate(dcd_x, axis=1), ex)
        dcs = dcs + _head_sum(de_scr[...], ex) * E - ddec * dec
        row = lax.broadcasted_iota(jnp.int32, (CH, LANES), 0)
        dcs = dcs + jnp.where(row == CH - 1, jnp.sum(ddec * dec, axis=0, keepdims=True) + dcd * cd, 0.0)
        da = _cumsum_rows(ci >= ri, dcs)
        ddt = da * t["A"] + _head_sum(dX * xs, ex)
        ddtr = jnp.where(t["valid"], ddt * _sigmoid(t["xdt"]), 0.0)
        ddtr_ref[...] = ddtr
        dxbc_ref[:, 0:DI] = (dX * dtx + dskx * dy).astype(BF16)
        s_ref[0:1, :] += jnp.sum(da * t["dt"], axis=0, keepdims=True)
        s_ref[1:2, :] += _head_sum(jnp.sum(dy * xs, axis=0, keepdims=True), ex)
        s_ref[2:3, :] += jnp.sum(ddtr, axis=0, keepdims=True)

        @pl.when(step == nc - 1)
        def _():
            s_ref[0:1, :] = s_ref[0:1, :] * t["A"]

    rev = lambda c: (nc - 1 - c, 0)
    return pl.pallas_call(
        body, grid=(nc,),
        in_specs=[pl.BlockSpec((CH, CONVD), rev), pl.BlockSpec((CH, LANES), rev), pl.BlockSpec((8, LANES), lambda c: (0, 0)),
                  pl.BlockSpec((CH, DI), rev), pl.BlockSpec((1, NS, DI), lambda c: (nc - 1 - c, 0, 0))],
        out_specs=[pl.BlockSpec((CH, CONVD), rev), pl.BlockSpec((CH, LANES), rev), pl.BlockSpec((8, LANES), lambda c: (0, 0))],
        out_shape=[jax.ShapeDtypeStruct((L, CONVD), BF16), jax.ShapeDtypeStruct((L, LANES), F32),
                   jax.ShapeDtypeStruct((8, LANES), F32)],
        scratch_shapes=[pltpu.VMEM((NS, DI), F32), pltpu.VMEM((CH, DI), F32), pltpu.VMEM((CH, DI), F32),
                        pltpu.VMEM((CH, DI), F32)],
        compiler_params=_params(("arbitrary",)), name=name)(xbc, dtr, prm, dy, sprev)


def _gnorm_fwd(y, zx, nw, *, name):
    L = y.shape[0]
    tm = min(L, 256)

    def body(y_ref, z_ref, nw_ref, o_ref):
        z = z_ref[...].astype(F32)
        yg = y_ref[...].astype(F32) * (z * _sigmoid(z))
        for g in range(NG):
            v = yg[:, g * GW:(g + 1) * GW]
            r = lax.rsqrt(jnp.mean(v * v, axis=-1, keepdims=True) + EPS)
            o_ref[:, g * GW:(g + 1) * GW] = (v * r * nw_ref[:, g * GW:(g + 1) * GW]).astype(BF16)

    row = pl.BlockSpec((tm, DI), lambda i: (i, 0))
    return pl.pallas_call(body, grid=(L // tm,), in_specs=[row, row, pl.BlockSpec((1, DI), lambda i: (0, 0))],
                          out_specs=row, out_shape=jax.ShapeDtypeStruct((L, DI), BF16),
                          compiler_params=_params(("parallel",)), name=name)(y, zx, nw)


def _gnorm_bwd(y, zx, nw, dyn, *, name):
    L = y.shape[0]
    tm = min(L, 256)

    def body(y_ref, z_ref, nw_ref, dyn_ref, dy_ref, dz_ref, s_ref):
        @pl.when(pl.program_id(0) == 0)
        def _():
            s_ref[...] = jnp.zeros_like(s_ref)

        z, yv = z_ref[...].astype(F32), y_ref[...].astype(F32)
        sz = _sigmoid(z)
        gate = z * sz
        dgate_dz = sz * (1.0 + z * (1.0 - sz))
        for g in range(NG):
            gs = slice(g * GW, (g + 1) * GW)
            v = yv[:, gs] * gate[:, gs]
            r = lax.rsqrt(jnp.mean(v * v, axis=-1, keepdims=True) + EPS)
            vhat = v * r
            dn = dyn_ref[:, gs].astype(F32)
            s_ref[0:1, gs] += jnp.sum(dn * vhat, axis=0, keepdims=True)
            dvhat = dn * nw_ref[:, gs]
            dv = r * (dvhat - vhat * jnp.mean(dvhat * vhat, axis=-1, keepdims=True))
            dy_ref[:, gs] = dv * gate[:, gs]
            dz_ref[:, gs] = (dv * yv[:, gs] * dgate_dz[:, gs]).astype(BF16)

    row = pl.BlockSpec((tm, DI), lambda i: (i, 0))
    return pl.pallas_call(body, grid=(L // tm,), in_specs=[row, row, pl.BlockSpec((1, DI), lambda i: (0, 0)), row],
                          out_specs=[row, row, pl.BlockSpec((8, DI), lambda i: (0, 0))],
                          out_shape=[jax.ShapeDtypeStruct((L, DI), F32), jax.ShapeDtypeStruct((L, ZX), BF16),
                                     jax.ShapeDtypeStruct((8, DI), F32)],
                          compiler_params=_params(("arbitrary",)), name=name)(y, zx, nw, dyn)


def _adamw(w, g, m, v, *, name, g_row=0, w_row=0, rows=None, into=None, emit_g=False):
    lead = w.ndim == 3
    R, C = w.shape[-2:]
    rows = R if rows is None else rows
    tr = max([t for t in range(8, rows + 1, 8) if rows % t == 0 and t * C <= 256 * 1024], default=rows)
    assert g_row % tr == 0 and w_row % tr == 0, (name, g_row, w_row, tr)
    n_out = 4 if emit_g else 3

    def body(w_ref, g_ref, m_ref, v_ref, *rest):
        outs = rest[-n_out:]
        gv = g_ref[...]
        mn = ADAM_B1 * m_ref[...] + (1.0 - ADAM_B1) * gv
        vn = ADAM_B2 * v_ref[...] + (1.0 - ADAM_B2) * (gv * gv)
        m_hat = mn / (1.0 - ADAM_B1 ** ADAM_STEP)
        v_hat = vn / (1.0 - ADAM_B2 ** ADAM_STEP)
        d_ref, mo_ref, vo_ref = outs[-3:]
        d_ref[...] = -ADAM_LR * (m_hat / (jnp.sqrt(v_hat) + ADAM_EPS) + ADAM_WD * w_ref[...])
        mo_ref[...] = mn
        vo_ref[...] = vn
        if emit_g:
            outs[0][...] = gv

    blk = (pl.BlockSpec((None, tr, C), lambda i: (0, i + w_row // tr, 0)) if lead
           else pl.BlockSpec((tr, C), lambda i: (i + w_row // tr, 0)))
    args, in_specs, alias = [w, g, m, v], [blk, pl.BlockSpec((tr, C), lambda i: (i + g_row // tr, 0)), blk, blk], {}
    if into is not None:
        args, in_specs, alias = args + list(into), in_specs + [ANY] * n_out, {4 + k: k for k in range(n_out)}
    return pl.pallas_call(body, grid=(rows // tr,), in_specs=in_specs, out_specs=[blk] * n_out,
                          out_shape=[jax.ShapeDtypeStruct(w.shape, F32)] * n_out, input_output_aliases=alias,
                          compiler_params=_params(("parallel",)), name=name)(*args)


def _residual(acc, xv, gv):
    return xv + gv * acc, acc


def _like(buf):
    return jax.ShapeDtypeStruct(buf.shape, buf.dtype)


def _mlp_fwd(x, mod, nw, wb, up_row, down_row, tag):
    sh, sc, g = mod
    h = _modnorm_fwd(x, nw, sc, sh, name=tag + "_norm")
    a = _matmul(h, wb, n=DFF, tm=TM_ALL, b_spec=pl.BlockSpec((None, D, 512), lambda mi, j: (j // 2, up_row // D, j % 2)),
                epi=lambda acc: (jnp.maximum(acc, 0.0),), out_dtypes=(BF16,), name=tag + "_up")
    xn, y = _matmul(a, wb, n=D, tm=TM_HALF, contract=_nn_split_sq,
                    b_spec=pl.BlockSpec((N_CHIPS, D, 512), lambda mi, j: (0, down_row // D, j)),
                    extras=(x, g), epi=_residual, out_dtypes=(F32, BF16), name=tag + "_down")
    return xn, (x, h, a, y)


def _mlp_bwd(dxo, saved, mod, nw, wb, gb, up_row, down_row, tag):
    x, h, a, y = saved
    sh, sc, g = mod
    dy, gsum = _gate_bwd(dxo, y, g, name=tag + "_dgate")
    du = _matmul(dy, wb, n=DFF, tm=TM_ALL, contract=_nt,
                 b_spec=pl.BlockSpec((None, 512, D), lambda mi, j: (j // 2, down_row // 512 + j % 2, 0)),
                 extras=(a,), epi=lambda acc, av: (acc * (2.0 * av.astype(F32)),), out_dtypes=(BF16,), name=tag + "_dact")
    gb = _matmul_tn(a, dy, m=DFF, n=D, tm=D, tn=D, a_square=True, into=gb, out_struct=_like(wb),
                    out_spec=pl.BlockSpec((None, D, D), lambda mi, j: (mi, down_row // D, 0)), name=tag + "_ddown")
    dh = _matmul(du, wb, n=D, tm=TM_HALF, contract=_nt_split,
                 b_spec=pl.BlockSpec((N_CHIPS, 512, D), lambda mi, j: (0, up_row // 512 + j, 0)), name=tag + "_dh")
    gb = _matmul_tn(h, du, m=D, n=DFF, tm=D, into=gb, out_struct=_like(wb),
                    out_spec=pl.BlockSpec((None, D, 512), lambda mi, j: (j // 2, up_row // D, j % 2)), name=tag + "_dup")
    dx, sums = _modnorm_bwd(x, dh, dxo, nw, sc, gsum, name=tag + "_dnorm")
    return dx, gb, sums


def _ssd_fwd_scan(x, mod, nw, w_zx, w_dt, conv_w, conv_b, prm, tag):
    sh, sc, g = mod
    h = _modnorm_fwd(x, nw, sc, sh, name=tag + "_norm")
    zx = _matmul(h, w_zx, n=ZX, tm=TM_ALL, out_dtypes=(BF16,), name=tag + "_in")
    dtr = _matmul(h, w_dt, n=LANES, tm=TM_ALL, name=tag + "_in_dt")
    xbc = _ssd_conv_fwd(zx, conv_w, conv_b, name=tag + "_conv")
    y, sprev = _ssd_fwd(xbc, dtr, prm, name=tag + "_scan")
    return h, zx, dtr, xbc, y, sprev


def _ssd_fwd_out(x, mod, scan, gn_w, w_out, tag):
    sh, sc, g = mod
    h, zx, dtr, xbc, y, sprev = scan
    yn = _gnorm_fwd(y, zx, gn_w, name=tag + "_gnorm")
    xn, yo = _matmul(yn, w_out, n=D, tm=TM_HALF, contract=_nn_split,
                     b_spec=pl.BlockSpec((N_CHIPS, 512, 512), lambda mi, j: (0, 0, j)),
                     extras=(x, g), epi=_residual, out_dtypes=(F32, BF16), name=tag + "_out")
    return xn, (x, h, zx, dtr, xbc, y, sprev, yn, yo)


def _ssd_bwd_out(dxo, saved, mod, w_out, tag):
    x, h, zx, dtr, xbc, y, sprev, yn, yo = saved
    sh, sc, g = mod
    dyo, gsum = _gate_bwd(dxo, yo, g, name=tag + "_dgate")
    dyn = _matmul(dyo, w_out, n=DI, tm=TM_ALL, contract=_nt, b_spec=pl.BlockSpec((None, 512, D), lambda mi, j: (j, 0, 0)),
                  out_dtypes=(BF16,), name=tag + "_dyn")
    g_out = _matmul_tn(yn, dyo, m=DI, n=D, tn=D, out_struct=_like(w_out),
                       out_spec=pl.BlockSpec((None, 512, D), lambda mi, j: (mi, 0, 0)), name=tag + "_dout")
    return dyn, g_out, gsum


def _ssd_bwd_rest(dxo, dy, dzx, gsum, saved, mod, nw, w_zx, w_dt, conv_w, conv_b, prm, tag):
    x, h, zx, dtr, xbc, y, sprev, yn, yo = saved
    sh, sc, g = mod
    dxbc, ddtr, ssum = _ssd_bwd(xbc, dtr, prm, dy, sprev, name=tag + "_dscan")
    dzx, csum = _ssd_conv_bwd(zx, dxbc, conv_w, conv_b, dzx, name=tag + "_dconv")
    dh_dt = _matmul(ddtr, w_dt, n=D, tm=TM_ALL, contract=_nt, name=tag + "_dh_dt")
    dh = _matmul(dzx, w_zx, n=D, tm=TM_HALF, contract=_nt, extras=(dh_dt,), epi=lambda acc, e: (acc + e,), name=tag + "_dh")
    d_w_zx = _matmul_tn(h, dzx, m=D, n=ZX, tm=D, name=tag + "_din")
    d_w_dt = _matmul_tn(h, ddtr, m=D, n=LANES, tm=D, name=tag + "_din_dt")
    dx, sums = _modnorm_bwd(x, dh, dxo, nw, sc, gsum, name=tag + "_dnorm")
    return dx, d_w_zx, d_w_dt, sums, csum, ssum


def _sc_layer_fwd(x, mod, nw, w_sc_in, conv_w, wb, out_row, tag):
    sh, sc, g = mod
    h = _modnorm_fwd(x, nw, sc, sh, name=tag + "_norm")
    proj = _matmul(h, w_sc_in, n=3 * D, tm=TM_ALL, tn=256, out_dtypes=(BF16,),
                   b_spec=pl.BlockSpec((None, D, 256), lambda mi, j: (j // 3, 0, j % 3)),
                   name=tag + "_in")
    yv = _sc_fwd(proj, conv_w, name=tag + "_conv")
    xn, yo = _matmul(yv, wb, n=D, tm=TM_HALF, contract=_nn_split,
                     b_spec=pl.BlockSpec((N_CHIPS, 256, 512), lambda mi, j: (0, out_row // 256, j)),
                     extras=(x, g), epi=_residual, out_dtypes=(F32, BF16), name=tag + "_out")
    return xn, (x, h, proj, yv, yo)


def _sc_layer_bwd(dxo, saved, mod, nw, w_sc_in, conv_w, wb, gb, out_row, tag):
    x, h, proj, yv, yo = saved
    sh, sc, g = mod
    L = x.shape[0]
    dyo, gsum = _gate_bwd(dxo, yo, g, name=tag + "_dgate")
    dyv = _matmul(dyo, wb, n=D, tm=TM_ALL, tn=256, contract=_nt,
                  b_spec=pl.BlockSpec((None, 256, D), lambda mi, j: (j, out_row // 256, 0)), name=tag + "_dyv")
    gb = _matmul_tn(yv, dyo, m=D, n=D, tm=256, tn=D, into=gb, out_struct=_like(wb),
                    out_spec=pl.BlockSpec((None, 256, D), lambda mi, j: (mi, out_row // 256, 0)), name=tag + "_dout")
    dproj, csum = _sc_bwd(proj, dyv, conv_w, name=tag + "_dconv")
    tm = min(L, TM_HALF)
    dh = _matmul(dproj, w_sc_in, n=D, tm=tm, contract=_nt_sc_in, a_spec=pl.BlockSpec((3, tm, D), lambda mi, j: (0, mi, 0)),
                 b_spec=pl.BlockSpec((N_CHIPS, 512, SC_IN_SHARD), lambda mi, j: (0, j, 0)), name=tag + "_dh")
    g_sc_in = _matmul_tn(h, dproj, m=D, n=3 * D, tm=D, tn=256, b_spec=pl.BlockSpec((None, L, 256), lambda mi, j: (j // 4, 0, j % 4)),
                         out_spec=pl.BlockSpec((None, D, 256), lambda mi, j: (j // 3, 0, j % 3)),
                         out_struct=jax.ShapeDtypeStruct((N_CHIPS, D, SC_IN_SHARD), BF16), name=tag + "_din")
    dx, sums = _modnorm_bwd(x, dh, dxo, nw, sc, gsum, name=tag + "_dnorm")
    return dx, gb, g_sc_in, sums, csum


SUB_ROW = (0, 8, 16, 24)
SSD_CONV_ROW, GNORM_ROW, FINAL_ROW, SC_CONV_ROW, HEAD_ROW, SMALL_ROWS = 32, 56, 72, 80, 88, 96


def _all_gather_rows(blk, *, name):
    m_per, n = blk.shape

    def body(x_ref, out_ref, send_sems, recv_sems, local_sem):
        x, y, c = lax.axis_index("x"), lax.axis_index("y"), lax.axis_index("c")
        me, sibling = (x, y, c), (x, y, 1 - c)
        chips = [(1 - x, y), (x, 1 - y), (1 - x, 1 - y)]

        def rows(px, py, pc):
            return out_ref.at[pl.ds((4 * px + 2 * py + pc) * m_per, m_per), :]

        def copy(k, block, to, src=None):
            return pltpu.make_async_remote_copy(src_ref=rows(*block) if src is None else src, dst_ref=rows(*block),
                                                send_sem=send_sems.at[k], recv_sem=recv_sems.at[k], device_id=to,
                                                device_id_type=MESH)

        mine = pltpu.make_async_copy(x_ref, rows(*me), local_sem)
        mine.start()
        first = [copy(0, me, sibling, src=x_ref)] + [copy(1 + j, me, (*chip, c), src=x_ref) for j, chip in enumerate(chips)]
        for cp in first:
            cp.start()
        passed = [copy(4 + j, (*chip, c), sibling) for j, chip in enumerate(chips)]
        for j, chip in enumerate(chips):
            copy(1 + j, (*chip, c), me).wait_recv()
            passed[j].start()
        copy(0, sibling, me).wait_recv()
        for j, chip in enumerate(chips):
            copy(4 + j, (*chip, 1 - c), me).wait_recv()
        for cp in first + passed:
            cp.wait_send()
        mine.wait()

    return pl.pallas_call(
        body, out_shape=jax.ShapeDtypeStruct((N_DEV * m_per, n), blk.dtype),
        in_specs=[pl.BlockSpec(memory_space=pltpu.VMEM)], out_specs=pl.BlockSpec(memory_space=pltpu.VMEM),
        scratch_shapes=[pltpu.SemaphoreType.DMA((7,)), pltpu.SemaphoreType.DMA((7,)), pltpu.SemaphoreType.DMA],
        name=name)(blk)


def _half(ref, chip, c):
    hr = ref.shape[1] // 2
    return ref.at[chip, pl.ds(c * hr, hr), :]


def _gather_copy(bufs, sends, recvs, b, k, chip, pc, to):
    piece = _half(bufs[b], 2 * chip[0] + chip[1], pc)
    return pltpu.make_async_remote_copy(src_ref=piece, dst_ref=piece, send_sem=sends.at[4 * b + k], recv_sem=recvs.at[4 * b + k],
                                        device_id=to, device_id_type=MESH)


def _split_call(body, bufs, sems_in, n_sems, *, name, after=(), token=False, lands=()):
    nb, na, nl, starts = len(bufs), len(after), len(lands), not sems_in

    def wrapped(*refs):
        sems = refs[nb + na:nb + na + 2] if starts else refs[nb:nb + 2]
        made = refs[nb + na + 2 + nb:nb + na + 2 + nb + nl] if starts else ()
        body(tuple(refs[:nb]) + tuple(made), sems[0], sems[1])
        if token:
            refs[-1][...] = jnp.zeros_like(refs[-1])

    out_shape = [pltpu.SemaphoreType.DMA((n_sems,)) for _ in range(2 if starts else 0)]
    out_specs = [SEM] * len(out_shape) + [ANY] * (nb + nl)
    alias = {b: len(out_shape) + b for b in range(nb)}
    out_shape += [jax.ShapeDtypeStruct(b.shape, b.dtype) for b in bufs] + list(lands)
    if token:
        out_shape.append(jax.ShapeDtypeStruct((8, LANES), F32))
        out_specs.append(pl.BlockSpec(memory_space=pltpu.VMEM))
    return pl.pallas_call(
        wrapped, out_shape=out_shape, in_specs=[ANY] * nb + [SEM] * len(sems_in) + [ANY] * na, out_specs=out_specs,
        input_output_aliases=alias,
        compiler_params=pltpu.CompilerParams(has_side_effects=pltpu.SideEffectType.DATAFLOW_SIDE_EFFECTING),
        name=name)(*bufs, *sems_in, *after)


def _gather_start(bufs, *, name, after=()):
    nb = len(bufs)

    def body(ins, sends, recvs):
        x, y, c = lax.axis_index("x"), lax.axis_index("y"), lax.axis_index("c")
        chips = [(1 - x, y), (x, 1 - y), (1 - x, 1 - y)]
        for b in range(nb):
            _gather_copy(ins, sends, recvs, b, 0, (x, y), c, (x, y, 1 - c)).start()
            for j, chip in enumerate(chips):
                _gather_copy(ins, sends, recvs, b, 1 + j, (x, y), c, (*chip, c)).start()

    out = _split_call(body, bufs, (), 4 * nb, name=name, after=after, token=True)
    return (out[0], out[1], out[2:2 + nb]), out[-1]


def _gather_wait_first(flight, *, name, after=()):
    sends, recvs, bufs = flight
    nb = len(bufs)

    def body(ins, sends_, recvs_):
        x, y, c = lax.axis_index("x"), lax.axis_index("y"), lax.axis_index("c")
        chips = [(1 - x, y), (x, 1 - y), (1 - x, 1 - y)]
        for b in range(nb):
            _gather_copy(ins, sends_, recvs_, b, 0, (x, y), c, (x, y, 1 - c)).wait_send()
            _gather_copy(ins, sends_, recvs_, b, 0, (x, y), 1 - c, (x, y, c)).wait_recv()
            for j, chip in enumerate(chips):
                _gather_copy(ins, sends_, recvs_, b, 1 + j, (x, y), c, (*chip, c)).wait_send()
                _gather_copy(ins, sends_, recvs_, b, 1 + j, chip, c, (x, y, c)).wait_recv()

    return _split_call(body, bufs, (sends, recvs), 4 * nb, name=name, after=after)


def _gather_forward(bufs, *, name):
    nb = len(bufs)

    def body(ins, sends, recvs):
        x, y, c = lax.axis_index("x"), lax.axis_index("y"), lax.axis_index("c")
        chips = [(1 - x, y), (x, 1 - y), (1 - x, 1 - y)]
        for b in range(nb):
            for j, chip in enumerate(chips):
                _gather_copy(ins, sends, recvs, b, 1 + j, chip, c, (x, y, 1 - c)).start()

    out = _split_call(body, bufs, (), 4 * nb, name=name)
    return out[0], out[1], out[2:2 + nb]


def _gather_wait_forward(flight, *, name, after=()):
    sends, recvs, bufs = flight
    nb = len(bufs)

    def body(ins, sends_, recvs_):
        x, y, c = lax.axis_index("x"), lax.axis_index("y"), lax.axis_index("c")
        chips = [(1 - x, y), (x, 1 - y), (1 - x, 1 - y)]
        for b in range(nb):
            for j, chip in enumerate(chips):
                _gather_copy(ins, sends_, recvs_, b, 1 + j, chip, c, (x, y, 1 - c)).wait_send()
                _gather_copy(ins, sends_, recvs_, b, 1 + j, chip, 1 - c, (x, y, c)).wait_recv()

    return _split_call(body, bufs, (sends, recvs), 4 * nb, name=name, after=after)


def _owner_copies(hs, lands, sends, recvs):
    x, y, c = lax.axis_index("x"), lax.axis_index("y"), lax.axis_index("c")
    chips = [(1 - x, y), (x, 1 - y), (1 - x, 1 - y)]
    return [pltpu.make_async_remote_copy(src_ref=hs[b].at[2 * cx + cy], dst_ref=lands[b].at[j], send_sem=sends.at[3 * b + j],
                                         recv_sem=recvs.at[3 * b + j], device_id=(cx, cy, c), device_id_type=MESH)
            for b in range(len(hs)) for j, (cx, cy) in enumerate(chips)]


def _owners_start(hs, *, name):
    nb = len(hs)
    lands = [jax.ShapeDtypeStruct((3,) + h.shape[1:], h.dtype) for h in hs]

    def body(refs, sends, recvs):
        for cp in _owner_copies(refs[:nb], refs[nb:], sends, recvs):
            cp.start()

    out = _split_call(body, list(hs), (), 3 * nb, name=name, token=True, lands=lands)
    return (out[0], out[1], out[2:2 + 2 * nb]), out[-1]


def _owners_wait(flight, *, name, after=()):
    sends, recvs, bufs = flight
    nb = len(bufs) // 2

    def body(refs, sends_, recvs_):
        for cp in _owner_copies(refs[:nb], refs[nb:], sends_, recvs_):
            cp.wait()

    out = _split_call(body, bufs, (sends, recvs), 3 * nb, name=name, after=after)
    return out[:nb], out[nb:]


def _sibling_copies(gs, lands, sends, recvs):
    x, y, c = lax.axis_index("x"), lax.axis_index("y"), lax.axis_index("c")
    copies = []
    for b in range(len(gs)):
        hr = gs[b].shape[1] // 2
        copies.append(pltpu.make_async_remote_copy(
            src_ref=gs[b].at[:, pl.ds((1 - c) * hr, hr), :], dst_ref=lands[b], send_sem=sends.at[b], recv_sem=recvs.at[b],
            device_id=(x, y, 1 - c), device_id_type=MESH))
    return copies


def _sibling_start(gs, *, name, after=()):
    nb = len(gs)
    lands = [jax.ShapeDtypeStruct((g.shape[0], g.shape[1] // 2, g.shape[2]), g.dtype) for g in gs]

    def body(refs, sends, recvs):
        for cp in _sibling_copies(refs[:nb], refs[nb:], sends, recvs):
            cp.start()

    out = _split_call(body, list(gs), (), nb, name=name, after=after, token=True, lands=lands)
    return (out[0], out[1], out[2:2 + 2 * nb]), out[-1]


def _sibling_wait(flight, *, name, after=()):
    sends, recvs, bufs = flight
    nb = len(bufs) // 2

    def body(refs, sends_, recvs_):
        for cp in _sibling_copies(refs[:nb], refs[nb:], sends_, recvs_):
            cp.wait()

    out = _split_call(body, bufs, (sends, recvs), nb, name=name, after=after)
    return out[:nb], out[nb:]


def _result_copies(ts, sends, recvs):
    x, y, c = lax.axis_index("x"), lax.axis_index("y"), lax.axis_index("c")
    return [pltpu.make_async_remote_copy(src_ref=ts[b].at[c], dst_ref=ts[b].at[c], send_sem=sends.at[b], recv_sem=recvs.at[b],
                                         device_id=(x, y, 1 - c), device_id_type=MESH) for b in range(len(ts))]


def _result_start(ts, *, name):
    def body(refs, sends, recvs):
        for cp in _result_copies(refs, sends, recvs):
            cp.start()

    out = _split_call(body, ts, (), len(ts), name=name, token=True)
    return (out[0], out[1], out[2:2 + len(ts)]), out[-1]


def _result_wait(flight, *, name, after=()):
    sends, recvs, bufs = flight

    def body(refs, sends_, recvs_):
        for cp in _result_copies(refs, sends_, recvs_):
            cp.wait()

    return _split_call(body, bufs, (sends, recvs), len(bufs), name=name, after=after)


def _row_tile(rows, cols):
    best = 16
    for t in range(16, rows + 1, 16):
        if rows % t == 0 and t * cols <= 640 * 1024:
            best = t
    assert rows % best == 0, (rows, cols)
    return best


def _add_sibling_half(g, recv, core, *, name):
    nk, r, n = g.shape
    hr = r // 2
    tr = _row_tile(hr, n)

    def body(c_ref, a_ref, b_ref, o_ref):
        o_ref[...] = (a_ref[...].astype(F32) + b_ref[...].astype(F32)).astype(BF16)

    grid_spec = pltpu.PrefetchScalarGridSpec(
        num_scalar_prefetch=1, grid=(nk, hr // tr),
        in_specs=[pl.BlockSpec((None, tr, n), lambda k, i, c_ref: (k, c_ref[0] * (hr // tr) + i, 0)),
                  pl.BlockSpec((None, tr, n), lambda k, i, c_ref: (k, i, 0))],
        out_specs=pl.BlockSpec((None, tr, n), lambda k, i, c_ref: (k, i, 0)))
    return pl.pallas_call(body, grid_spec=grid_spec, out_shape=jax.ShapeDtypeStruct((nk, hr, n), BF16),
                          compiler_params=_params(("parallel", "parallel")), name=name)(core, g, recv)


def _add_chip_sums(h, recv, chip_core, *, name):
    _, hr, n = h.shape
    tr = _row_tile(hr, n)

    def body(k_ref, a_ref, b_ref, o_ref):
        o_ref[...] = ((a_ref[...].astype(F32) + b_ref[0].astype(F32)) + b_ref[1].astype(F32)) + b_ref[2].astype(F32)

    grid_spec = pltpu.PrefetchScalarGridSpec(
        num_scalar_prefetch=1, grid=(hr // tr,),
        in_specs=[pl.BlockSpec((None, tr, n), lambda i, k_ref: (k_ref[0], i, 0)),
                  pl.BlockSpec((3, tr, n), lambda i, k_ref: (0, i, 0))],
        out_specs=pl.BlockSpec((None, tr, n), lambda i, k_ref: (k_ref[1], i, 0)))
    return pl.pallas_call(body, grid_spec=grid_spec, out_shape=jax.ShapeDtypeStruct((2, hr, n), F32),
                          compiler_params=_params(("parallel",)), name=name)(chip_core, h, recv)


def _sum_devices(g, *, name):
    nd, r, n = g.shape

    def body(g_ref, o_ref):
        acc = g_ref[0]
        for i in range(1, nd):
            acc = acc + g_ref[i]
        o_ref[...] = acc

    return pl.pallas_call(body, out_shape=jax.ShapeDtypeStruct((r, n), F32), name=name)(g)


def _own_slot(shard, chip):
    return lax.dynamic_update_slice(jnp.zeros((N_CHIPS,) + shard.shape, BF16), shard[None], (chip, 0, 0))


def kernel(x, c, ada_w, ada_b, mix_norm_w, mlp_norm_w, mlp_up, mlp_down, ssd_in_w, ssd_conv_w, ssd_conv_b, ssd_dt_bias, ssd_A_log, ssd_D, ssd_norm_w, ssd_out_w, sc_in_w, sc_conv_w, sc_out_w, final_norm_w, loss_target, m_ada_w, m_ada_b, m_mix_norm_w, m_mlp_norm_w, m_mlp_up, m_mlp_down, m_ssd_in_w, m_ssd_conv_w, m_ssd_conv_b, m_ssd_dt_bias, m_ssd_A_log, m_ssd_D, m_ssd_norm_w, m_ssd_out_w, m_sc_in_w, m_sc_conv_w, m_sc_out_w, m_final_norm_w, v_ada_w, v_ada_b, v_mix_norm_w, v_mlp_norm_w, v_mlp_up, v_mlp_down, v_ssd_in_w, v_ssd_conv_w, v_ssd_conv_b, v_ssd_dt_bias, v_ssd_A_log, v_ssd_D, v_ssd_norm_w, v_ssd_out_w, v_sc_in_w, v_sc_conv_w, v_sc_out_w, v_final_norm_w):
    xi, yi, ci = lax.axis_index("x"), lax.axis_index("y"), lax.axis_index("c")
    chip = 2 * xi + yi
    dev = 2 * chip + ci
    n_ada = ada_w.shape[2]

    conv_flat = jnp.concatenate([ssd_conv_w.reshape(-1), sc_conv_w.reshape(-1), jnp.zeros((256,), F32)]).reshape(4, D)
    blk0 = jnp.concatenate([c, conv_flat, jnp.zeros((3, D), F32)], axis=0)
    got0 = _all_gather_rows(blk0, name="gather_cond").reshape(N_DEV, 8, D)
    c_all = got0[:, 0]
    conv_all = got0[0::2, 1:5].reshape(N_CHIPS, 4 * D)
    ssd_conv = jnp.moveaxis(conv_all[:, :4 * 768].reshape(N_CHIPS, 4, 768), 0, 1).reshape(4, CONVD)
    sc_conv = jnp.moveaxis(conv_all[:, 4 * 768:4 * 768 + 3 * 256].reshape(N_CHIPS, 3, 256), 0, 1).reshape(3, D)
    mod_shard = [_matmul(c_all, ada_w, n=n_ada, a_silu=True, b_spec=pl.BlockSpec((None, D, 512), lambda mi, j, i=i: (i, 0, j)),
                         extras=(lax.dynamic_slice(ada_b, (i, chip * n_ada), (1, n_ada)),),
                         epi=lambda acc, b: (acc + b,), name=f"ada_mod{i}") for i in range(2)]
    mod_all = _all_gather_rows(jnp.concatenate(mod_shard, axis=0), name="gather_mod")
    mod_all = mod_all.reshape(N_DEV, 2, N_DEV, n_ada)[0::2]
    mod = jnp.moveaxis(lax.dynamic_index_in_dim(mod_all, dev, axis=2, keepdims=False), 0, 1).reshape(2, 6, D)
    mods = [[mod[i, j:j + 1] for j in range(6)] for i in range(2)]

    bf = lambda v: v.astype(BF16)
    up_row, down_row, sc_out_row = 0, D, 2 * D
    a_bufs = [_own_slot(bf(ssd_in_w[0]), chip)]
    b_bufs = [_own_slot(bf(ssd_out_w[0]), chip), _own_slot(bf(jnp.concatenate([mlp_up[0], mlp_down[0]], axis=0)), chip)]
    c_bufs = [_own_slot(bf(sc_in_w[0]), chip), _own_slot(bf(jnp.concatenate([mlp_up[1], mlp_down[1], sc_out_w[0]], axis=0)), chip)]
    fly_a, tok = _gather_start(a_bufs, name="gather_a_start", after=(mod,))
    fly_b, tok = _gather_start(b_bufs, name="gather_b_start", after=(tok,))
    fly_c, tok = _gather_start(c_bufs, name="gather_c_start", after=(tok,))

    row = lambda v: v.reshape(1, -1)
    xs, tgt = x[0], loss_target[0]
    prm = jnp.pad(jnp.concatenate([ssd_dt_bias, ssd_A_log, ssd_D, jnp.zeros((5, NH), F32)], axis=0), ((0, 0), (0, LANES - NH)))
    mix_nw = [row(mix_norm_w[i]) for i in range(2)]
    mlp_nw = [row(mlp_norm_w[i]) for i in range(2)]
    a_bufs = _gather_wait_first(fly_a, name="gather_a_landed", after=(tok,))
    (w_ssd_in,) = _gather_wait_forward(_gather_forward(a_bufs, name="gather_a_pass"), name="gather_a_done")
    ssd_in_full = jnp.moveaxis(w_ssd_in, 0, 1).reshape(D, N_CHIPS * SSD_IN_SHARD)
    w_zx, w_dt = ssd_in_full[:, :ZX], jnp.pad(ssd_in_full[:, ZX:], ((0, 0), (0, LANES - NH)))
    scan = _ssd_fwd_scan(xs, mods[0][0:3], mix_nw[0], w_zx, w_dt, ssd_conv, ssd_conv_b, prm, "ssd")
    fly_b = _gather_forward(_gather_wait_first(fly_b, name="gather_b_landed", after=(scan[3],)), name="gather_b_pass")
    w_ssd_out, w_b = _gather_wait_forward(fly_b, name="gather_b_done", after=(scan[4],))
    x1, s_ssd = _ssd_fwd_out(xs, mods[0][0:3], scan, ssd_norm_w, w_ssd_out, "ssd")
    x2, s_mlp0 = _mlp_fwd(x1, mods[0][3:6], mlp_nw[0], w_b, up_row, down_row, "mlp0")
    c_bufs = _gather_wait_first(fly_c, name="gather_c_landed", after=(x2,))
    w_sc_in, w_c = _gather_wait_forward(_gather_forward(c_bufs, name="gather_c_pass"), name="gather_c_done")
    x3, s_sc = _sc_layer_fwd(x2, mods[1][0:3], mix_nw[1], w_sc_in, sc_conv, w_c, sc_out_row, "sc")
    x4, s_mlp1 = _mlp_fwd(x3, mods[1][3:6], mlp_nw[1], w_c, up_row, down_row, "mlp1")

    core = ci.reshape(1).astype(jnp.int32)
    chip_core = jnp.stack([chip, ci]).astype(jnp.int32)

    def reduce_swap(gbufs, tag, after=()):
        return _sibling_start(gbufs, name=tag + "_sibling_start", after=after)

    def reduce_send(flight, tag, after):
        gs, sib = _sibling_wait(flight, name=tag + "_sibling_landed", after=after)
        hs = [_add_sibling_half(g, s, core, name=f"{tag}_add_sibling{b}") for b, (g, s) in enumerate(zip(gs, sib))]
        return _owners_start(hs, name=tag + "_owners_start")

    def reduce_sum(flight, tag, after):
        hs, lands = _owners_wait(flight, name=tag + "_owners_landed", after=after)
        ts = [_add_chip_sums(h, o, chip_core, name=f"{tag}_add_chips{b}") for b, (h, o) in enumerate(zip(hs, lands))]
        return _result_start(ts, name=tag + "_result_start")

    def reduce_done(flight, tag, after=()):
        return [t.reshape(-1, t.shape[2]) for t in _result_wait(flight, name=tag + "_result_landed", after=after)]

    dx4, fsum = _final_loss(x4, row(final_norm_w), tgt, name="final_loss")
    dx3, g_c, sum_mlp1 = _mlp_bwd(dx4, s_mlp1, mods[1][3:6], mlp_nw[1], w_c, None, up_row, down_row, "mlp1")
    dx2, g_c, g_sc_in, sum_sc, sc_csum = _sc_layer_bwd(dx3, s_sc, mods[1][0:3], mix_nw[1], w_sc_in, sc_conv, w_c, g_c,
                                                       sc_out_row, "sc")
    dx1, g_b, sum_mlp0 = _mlp_bwd(dx2, s_mlp0, mods[0][3:6], mlp_nw[0], w_b, None, up_row, down_row, "mlp0")
    dyn, g_ssd_out, gsum_ssd = _ssd_bwd_out(dx1, s_ssd, mods[0][0:3], w_ssd_out, "ssd")
    fly_1, tok = reduce_swap([g_c, g_sc_in, g_b, g_ssd_out], "rs1")
    dy, dzx, gnsum = _gnorm_bwd(s_ssd[5], s_ssd[2], ssd_norm_w + tok[0:1, 0:1], dyn, name="ssd_dgnorm")
    fly_1, tok = reduce_send(fly_1, "rs1", (dy,))
    grad_x, d_w_zx, d_w_dt, sum_ssd, csum, ssum = _ssd_bwd_rest(
        dx1, dy, dzx, gsum_ssd, s_ssd, mods[0][0:3], mix_nw[0], w_zx, w_dt, ssd_conv, ssd_conv_b, prm + tok[0:1, 0:1], "ssd")
    fly_1, tok = reduce_sum(fly_1, "rs1", (grad_x,))

    def ssd_in_owner(k):
        lo, hi = k * SSD_IN_SHARD, (k + 1) * SSD_IN_SHARD
        if hi <= ZX:
            return d_w_zx[:, lo:hi]
        return jnp.concatenate([d_w_zx[:, lo:], d_w_dt[:, :hi - ZX]], axis=1)

    small = jnp.concatenate([sum_ssd + tok[0:1, 0:1], sum_mlp0, sum_sc, sum_mlp1, csum.reshape(24, D), gnsum.reshape(16, D),
                             fsum, sc_csum, jnp.pad(ssum, ((0, 0), (0, D - LANES)))], axis=0)
    small_all = _all_gather_rows(small, name="gather_small").reshape(N_DEV, SMALL_ROWS, D)
    fly_2, tok = reduce_swap([jnp.stack([ssd_in_owner(k) for k in range(N_CHIPS)]).astype(BF16)], "rs2", (small_all,))
    fly_2, tok = reduce_send(fly_2, "rs2", (tok,))
    t_c, t_sc_in, t_b, t_ssd_out = reduce_done(fly_1, "rs1", (tok,))
    small_all = small_all + tok[0:1, 0:1]
    tot = _sum_devices(small_all, name="sum_small")
    loss = tot[FINAL_ROW + 1, 0]
    mod_rows = [r + o for r in SUB_ROW for o in (3, 2, 0)]
    g_ada_b = jnp.stack([tot[r] for r in mod_rows]).reshape(2, 6 * D)
    g_mix_norm = jnp.stack([tot[SUB_ROW[0] + 1], tot[SUB_ROW[2] + 1]])
    g_mlp_norm = jnp.stack([tot[SUB_ROW[1] + 1], tot[SUB_ROW[3] + 1]])
    conv_sums = tot[SSD_CONV_ROW:SSD_CONV_ROW + 24].reshape(8, CONVD)
    g_ssd_conv_w = lax.dynamic_slice(conv_sums, (0, chip * 768), (4, 768))[None]
    g_ssd_conv_b = conv_sums[4:5]
    g_ssd_norm = tot[GNORM_ROW:GNORM_ROW + 2].reshape(1, DI)
    g_final = tot[FINAL_ROW]
    g_sc_conv_w = lax.dynamic_slice(tot[SC_CONV_ROW:SC_CONV_ROW + 3], (0, chip * 256), (3, 256))[None]
    g_a_log, g_d, g_dt_bias = (tot[HEAD_ROW + r:HEAD_ROW + r + 1, 0:NH] for r in range(3))
    c_pad = jnp.concatenate([c_all, jnp.zeros((8, D), F32)], axis=0)
    dmod_all = jnp.stack([small_all[:, r] for r in mod_rows], axis=1).reshape(N_DEV, 2, 6 * D)
    g_ada_w = []
    for i in range(2):
        dm = lax.dynamic_slice(dmod_all[:, i], (0, chip * n_ada), (N_DEV, n_ada))
        g_ada_w.append(_matmul_tn(c_pad, jnp.concatenate([dm, jnp.zeros_like(dm)], axis=0), m=D, n=n_ada, a_silu=True,
                                  name=f"ada_dw{i}"))

    big = dict(ada_w=[(g, 0) for g in g_ada_w], mlp_up=[(t_b, up_row), (t_c, up_row)], mlp_down=[(t_b, down_row), (t_c, down_row)],
               ssd_out_w=[(t_ssd_out, 0)], sc_out_w=[(t_c, sc_out_row)], sc_in_w=[(t_sc_in, 0)], ssd_in_w=None)
    grads = dict(ada_b=g_ada_b, mix_norm_w=g_mix_norm, mlp_norm_w=g_mlp_norm, ssd_conv_w=g_ssd_conv_w,
                 ssd_conv_b=g_ssd_conv_b, ssd_dt_bias=g_dt_bias, ssd_A_log=g_a_log, ssd_D=g_d, ssd_norm_w=g_ssd_norm,
                 sc_conv_w=g_sc_conv_w, final_norm_w=g_final)
    weights = dict(ada_w=(ada_w, m_ada_w, v_ada_w), ada_b=(ada_b, m_ada_b, v_ada_b),
                   mix_norm_w=(mix_norm_w, m_mix_norm_w, v_mix_norm_w), mlp_norm_w=(mlp_norm_w, m_mlp_norm_w, v_mlp_norm_w),
                   mlp_up=(mlp_up, m_mlp_up, v_mlp_up), mlp_down=(mlp_down, m_mlp_down, v_mlp_down),
                   ssd_in_w=(ssd_in_w, m_ssd_in_w, v_ssd_in_w), ssd_conv_w=(ssd_conv_w, m_ssd_conv_w, v_ssd_conv_w),
                   ssd_conv_b=(ssd_conv_b, m_ssd_conv_b, v_ssd_conv_b), ssd_dt_bias=(ssd_dt_bias, m_ssd_dt_bias, v_ssd_dt_bias),
                   ssd_A_log=(ssd_A_log, m_ssd_A_log, v_ssd_A_log), ssd_D=(ssd_D, m_ssd_D, v_ssd_D),
                   ssd_norm_w=(ssd_norm_w, m_ssd_norm_w, v_ssd_norm_w), ssd_out_w=(ssd_out_w, m_ssd_out_w, v_ssd_out_w),
                   sc_in_w=(sc_in_w, m_sc_in_w, v_sc_in_w), sc_conv_w=(sc_conv_w, m_sc_conv_w, v_sc_conv_w),
                   sc_out_w=(sc_out_w, m_sc_out_w, v_sc_out_w), final_norm_w=(final_norm_w, m_final_norm_w, v_final_norm_w))
    def step(nm, parts):
        w, m, v = (t if t.shape[0] == 1 else t.reshape(-1, t.shape[-1]) for t in weights[nm])
        rows, outs = w.shape[-2] // len(parts), None
        for i, (gbuf, g_row) in enumerate(parts):
            outs = _adamw(w, gbuf, m, v, g_row=g_row, w_row=i * rows, rows=rows, into=outs, emit_g=True, name=f"adamw_{nm}{i}")
        return outs

    res = {}
    for nm, (w, m, v) in weights.items():
        two_d = (-1, w.shape[-1]) if w.ndim > 1 else (1, -1)
        if nm not in big:
            res[nm] = (grads[nm], *_adamw(w.reshape(two_d), grads[nm].reshape(two_d), m.reshape(two_d), v.reshape(two_d),
                                          name="adamw_" + nm))
        elif big[nm] is not None:
            res[nm] = step(nm, big[nm])
    fly_2, tok = reduce_sum(fly_2, "rs2", tuple(r[1] for r in res.values()))
    (t_ssd_in,) = reduce_done(fly_2, "rs2", (tok,))
    w_t, m_t, v_t = (jnp.swapaxes(t[0], 0, 1) for t in weights["ssd_in_w"])
    res["ssd_in_w"] = [jnp.swapaxes(o, 0, 1) for o in _adamw(w_t, t_ssd_in.T, m_t, v_t, emit_g=True, name="adamw_ssd_in_w")]
    outs = [[res[nm][k].reshape(weights[nm][0].shape) for nm in weights] for k in range(4)]
    return (loss, grad_x[None], *outs[0], *outs[1], *outs[2], *outs[3])
```

```python
import jax
import jax.numpy as jnp
from jax import lax
from jax.experimental import pallas as pl
from jax.experimental.pallas import tpu as pltpu

F32 = jnp.float32
BF16 = jnp.bfloat16
MESH = pl.DeviceIdType.MESH

D = 1024
DFF = 4096
DI = 2048
NH = 32
HP = 64
NG = 4
NS = 128
CH = 128
CONVD = DI + 2 * NG * NS
ZX = DI + CONVD
GW = NG * NS
LANES = 128
N_CHIPS = 4
N_DEV = 8
EPS = 1e-5
ADAM_LR, ADAM_B1, ADAM_B2, ADAM_EPS, ADAM_WD, ADAM_STEP = 1e-3, 0.9, 0.999, 1e-8, 0.01, 10
VMEM_LIMIT = 48 * 1024 * 1024
TM_ALL = 2048
TM_HALF = 1024
ANY = pl.BlockSpec(memory_space=pl.ANY)
SEM = pl.BlockSpec(memory_space=pltpu.SEMAPHORE)

SSD_IN_SHARD = 1288
SC_IN_SHARD = 768


def _params(sem=None):
    return pltpu.CompilerParams(dimension_semantics=sem, vmem_limit_bytes=VMEM_LIMIT)


def _sigmoid(v):
    return 0.5 * jnp.tanh(0.5 * v) + 0.5


def _dot(a, b, dims=((1,), (0,)), precision=None):
    return lax.dot_general(a, b, (dims, ((), ())), preferred_element_type=F32, precision=precision)


def _dot_nt(a, b):
    return _dot(a, b, ((1,), (1,)))


def _dot_tn(a, b):
    return _dot(a, b, ((0,), (0,)))


def _nn(av, bv):
    return _dot(av.astype(BF16), bv.astype(BF16))


def _nt(av, bv):
    return _dot_nt(av.astype(BF16), bv.astype(BF16))


def _nn_split(av, bv):
    return _dot(av.astype(BF16), bv.reshape(-1, bv.shape[2]))


def _nn_split_sq(av, bv):
    af = av.astype(F32)
    return _nn_split(af * af, bv)


def _nt_split(av, bv):
    kc = bv.shape[2]
    acc = _dot_nt(av[:, 0:kc].astype(BF16), bv[0])
    for s in range(1, bv.shape[0]):
        acc = acc + _dot_nt(av[:, s * kc:(s + 1) * kc].astype(BF16), bv[s])
    return acc


def _nt_sc_in(av, bv):
    q = 256
    acc = None
    for i in range(3 * D // q):
        a_blk = av[i // 4][:, (i % 4) * q:(i % 4 + 1) * q]
        b_blk = bv[i // 3][:, (i % 3) * q:(i % 3 + 1) * q]
        t = _dot_nt(a_blk, b_blk)
        acc = t if acc is None else acc + t
    return acc


def _matmul(a, b, *, name, n, contract=_nn, a_spec=None, b_spec=None, tm=512, tn=512, extras=(), epi=None,
            out_dtypes=(F32,), a_silu=False):
    M = a.shape[-2]
    tm, tn = min(tm, M), min(tn, n)
    assert M % tm == 0 and n % tn == 0, (name, M, n, tm, tn)
    n_ex = len(extras)
    if a_spec is None:
        a_spec = pl.BlockSpec((tm, a.shape[1]), lambda i, j: (i, 0))
    if b_spec is None:
        b_spec = (pl.BlockSpec((tn, b.shape[1]), lambda i, j: (j, 0)) if contract is _nt
                  else pl.BlockSpec((b.shape[0], tn), lambda i, j: (0, j)))

    def body(*refs):
        av = refs[0][...]
        if a_silu:
            av = av * _sigmoid(av)
        acc = contract(av, refs[1][...])
        res = epi(acc, *[r[...] for r in refs[2:2 + n_ex]]) if epi is not None else (acc,)
        for o_ref, r in zip(refs[2 + n_ex:], res, strict=True):
            o_ref[...] = r.astype(o_ref.dtype)

    in_specs = [a_spec, b_spec]
    for e in extras:
        in_specs.append(pl.BlockSpec((1, tn), lambda i, j: (0, j)) if e.shape[0] == 1 and M != 1
                        else pl.BlockSpec((tm, tn), lambda i, j: (i, j)))
    outs = pl.pallas_call(
        body, grid=(M // tm, n // tn), in_specs=in_specs,
        out_specs=[pl.BlockSpec((tm, tn), lambda i, j: (i, j)) for _ in out_dtypes],
        out_shape=[jax.ShapeDtypeStruct((M, n), dt) for dt in out_dtypes],
        compiler_params=_params(("parallel", "parallel")), name=name)(a, b, *extras)
    return outs if len(out_dtypes) > 1 else outs[0]


def _matmul_tn(a, b, *, name, m, n, tm=512, tn=512, a_spec=None, b_spec=None, out_spec=None, out_struct=None, into=None,
               a_silu=False, a_square=False):
    T = a.shape[-2]
    tm, tn = min(tm, m), min(tn, n)
    assert m % tm == 0 and n % tn == 0, (name, m, n, tm, tn)
    if a_spec is None:
        a_spec = pl.BlockSpec((T, tm), lambda i, j: (0, i))
    if b_spec is None:
        b_spec = pl.BlockSpec((T, tn), lambda i, j: (0, j))
    if out_spec is None:
        out_spec, out_struct = pl.BlockSpec((tm, tn), lambda i, j: (i, j)), jax.ShapeDtypeStruct((m, n), F32)

    def body(a_ref, b_ref, *rest):
        av = a_ref[...]
        if a_silu:
            av = av * _sigmoid(av)
        if a_square:
            av = av.astype(F32) * av.astype(F32)
        rest[-1][...] = _dot_tn(av.astype(BF16), b_ref[...].astype(BF16)).astype(rest[-1].dtype)

    args, in_specs, alias = [a, b], [a_spec, b_spec], {}
    if into is not None:
        args, in_specs, alias = args + [into], in_specs + [ANY], {2: 0}
    return pl.pallas_call(body, grid=(m // tm, n // tn), in_specs=in_specs, out_specs=out_spec, out_shape=out_struct,
                          input_output_aliases=alias, compiler_params=_params(("parallel", "parallel")), name=name)(*args)


def _modnorm_fwd(x, nw, sc, sh, *, name):
    L = x.shape[0]
    tm = min(L, 512)

    def body(x_ref, nw_ref, sc_ref, sh_ref, h_ref):
        xv = x_ref[...]
        r = lax.rsqrt(jnp.mean(xv * xv, axis=-1, keepdims=True) + EPS)
        h_ref[...] = ((xv * r * nw_ref[...]) * (1.0 + sc_ref[...]) + sh_ref[...]).astype(BF16)

    row = pl.BlockSpec((tm, D), lambda i: (i, 0))
    vec = pl.BlockSpec((1, D), lambda i: (0, 0))
    return pl.pallas_call(body, grid=(L // tm,), in_specs=[row, vec, vec, vec], out_specs=row,
                          out_shape=jax.ShapeDtypeStruct((L, D), BF16),
                          compiler_params=_params(("parallel",)), name=name)(x, nw, sc, sh)


def _gate_outputs(dx, below_refs, dy_ref, gs_ref):
    g_ref, y_ref = below_refs
    dy_ref[...] = (dx * g_ref[...]).astype(BF16)
    gs_ref[0:1, :] += jnp.sum(dx * y_ref[...].astype(F32), axis=0, keepdims=True)


def _modnorm_bwd(x, dh, dxo, nw, sc, gsum, below, *, name):
    L = x.shape[0]
    tm = min(L, 256)
    nb = 0 if below is None else 2

    def body(x_ref, dh_ref, dxo_ref, nw_ref, sc_ref, g_ref, *rest):
        dx_ref, s_ref = rest[nb:nb + 2]

        @pl.when(pl.program_id(0) == 0)
        def _():
            s_ref[...] = g_ref[...]
            if nb:
                rest[-1][...] = jnp.zeros_like(rest[-1])

        xv, dhv = x_ref[...], dh_ref[...]
        r = lax.rsqrt(jnp.mean(xv * xv, axis=-1, keepdims=True) + EPS)
        xhat = xv * r
        dxhat = dhv * (nw_ref[...] * (1.0 + sc_ref[...]))
        dx = dxo_ref[...] + r * (dxhat - xhat * jnp.mean(dxhat * xhat, axis=-1, keepdims=True))
        dx_ref[...] = dx
        s_ref[1:2, :] += jnp.sum(dhv * xhat, axis=0, keepdims=True) * (1.0 + sc_ref[...])
        s_ref[2:3, :] += jnp.sum(dhv * xhat, axis=0, keepdims=True) * nw_ref[...]
        s_ref[3:4, :] += jnp.sum(dhv, axis=0, keepdims=True)
        if nb:
            _gate_outputs(dx, rest[:nb], rest[-2], rest[-1])

    row = pl.BlockSpec((tm, D), lambda i: (i, 0))
    vec = pl.BlockSpec((1, D), lambda i: (0, 0))
    blk = pl.BlockSpec((8, D), lambda i: (0, 0))
    in_specs, out_specs = [row, row, row, vec, vec, blk], [row, blk]
    out_shape = [jax.ShapeDtypeStruct((L, D), F32), jax.ShapeDtypeStruct((8, D), F32)]
    if nb:
        in_specs, out_specs = in_specs + [vec, row], out_specs + [row, blk]
        out_shape += [jax.ShapeDtypeStruct((L, D), BF16), jax.ShapeDtypeStruct((8, D), F32)]
    return pl.pallas_call(body, grid=(L // tm,), in_specs=in_specs, out_specs=out_specs, out_shape=out_shape,
                          compiler_params=_params(("arbitrary",)), name=name)(x, dh, dxo, nw, sc, gsum, *(below or ()))


def _final_loss(x, fw, tgt, below, *, name):
    L = x.shape[0]
    tm = min(L, 256)

    def body(x_ref, fw_ref, t_ref, g_ref, y_ref, dx_ref, s_ref, dy_ref, gs_ref):
        @pl.when(pl.program_id(0) == 0)
        def _():
            s_ref[...] = jnp.zeros_like(s_ref)
            gs_ref[...] = jnp.zeros_like(gs_ref)

        xv = x_ref[...]
        r = lax.rsqrt(jnp.mean(xv * xv, axis=-1, keepdims=True) + EPS)
        xhat = xv * r
        diff = xhat * fw_ref[...] - t_ref[...]
        dout = diff * (1.0 / D)
        dxhat = dout * fw_ref[...]
        dx = r * (dxhat - xhat * jnp.mean(dxhat * xhat, axis=-1, keepdims=True))
        dx_ref[...] = dx
        s_ref[0:1, :] += jnp.sum(dout * xhat, axis=0, keepdims=True)
        s_ref[1:2, :] += jnp.zeros((1, D), F32) + 0.5 * jnp.sum(jnp.sum(diff * diff, axis=-1, keepdims=True) * (1.0 / D))
        _gate_outputs(dx, (g_ref, y_ref), dy_ref, gs_ref)

    row = pl.BlockSpec((tm, D), lambda i: (i, 0))
    vec = pl.BlockSpec((1, D), lambda i: (0, 0))
    blk = pl.BlockSpec((8, D), lambda i: (0, 0))
    return pl.pallas_call(body, grid=(L // tm,), in_specs=[row, vec, row, vec, row], out_specs=[row, blk, row, blk],
                          out_shape=[jax.ShapeDtypeStruct((L, D), F32), jax.ShapeDtypeStruct((8, D), F32),
                                     jax.ShapeDtypeStruct((L, D), BF16), jax.ShapeDtypeStruct((8, D), F32)],
                          compiler_params=_params(("arbitrary",)), name=name)(x, fw, tgt, *below)


def _shift_down(v, j):
    if j == 0:
        return v
    rolled = pltpu.roll(v, j, 0)
    row = lax.broadcasted_iota(jnp.int32, (8, v.shape[1]), 0)
    return jnp.concatenate([jnp.where(row >= j, rolled[0:8], 0.0), rolled[8:]], axis=0)


def _shift_up(v, j):
    if j == 0:
        return v
    n = v.shape[0]
    rolled = pltpu.roll(v, n - j, 0)
    row = lax.broadcasted_iota(jnp.int32, (8, v.shape[1]), 0)
    return jnp.concatenate([rolled[:n - 8], jnp.where(row < 8 - j, rolled[n - 8:], 0.0)], axis=0)


def _ssd_conv_fwd(zx, w, b, *, name):
    L = zx.shape[0]
    cb = 256
    k = w.shape[0]

    def body(x_ref, w_ref, b_ref, o_ref):
        xv = x_ref[...].astype(F32)
        pre = b_ref[...] + xv * w_ref[k - 1:k, :]
        for j in range(1, k):
            pre = pre + _shift_down(xv, j) * w_ref[k - 1 - j:k - j, :]
        o_ref[...] = (pre * _sigmoid(pre)).astype(BF16)

    return pl.pallas_call(
        body, grid=(CONVD // cb,),
        in_specs=[pl.BlockSpec((L, cb), lambda i: (0, i + DI // cb)), pl.BlockSpec((k, cb), lambda i: (0, i)),
                  pl.BlockSpec((1, cb), lambda i: (0, i))],
        out_specs=pl.BlockSpec((L, cb), lambda i: (0, i)), out_shape=jax.ShapeDtypeStruct((L, CONVD), BF16),
        compiler_params=_params(("parallel",)), name=name)(zx, w, b)


def _ssd_conv_bwd(zx, dact, w, b, dzx, *, name):
    L = zx.shape[0]
    cb = 256
    k = w.shape[0]

    def body(x_ref, da_ref, w_ref, b_ref, _, dx_ref, s_ref):
        xv = x_ref[...].astype(F32)
        sh = [_shift_down(xv, j) for j in range(k)]
        pre = b_ref[...] + sh[0] * w_ref[k - 1:k, :]
        for j in range(1, k):
            pre = pre + sh[j] * w_ref[k - 1 - j:k - j, :]
        s = _sigmoid(pre)
        dpre = da_ref[...].astype(F32) * (s * (1.0 + pre * (1.0 - s)))
        dx = dpre * w_ref[k - 1:k, :]
        for j in range(1, k):
            dx = dx + _shift_up(dpre, j) * w_ref[k - 1 - j:k - j, :]
        dx_ref[...] = dx.astype(BF16)
        s_ref[...] = jnp.zeros_like(s_ref)
        for j in range(k):
            s_ref[k - 1 - j:k - j, :] = jnp.sum(dpre * sh[j], axis=0, keepdims=True)
        s_ref[k:k + 1, :] = jnp.sum(dpre, axis=0, keepdims=True)

    return pl.pallas_call(
        body, grid=(CONVD // cb,),
        in_specs=[pl.BlockSpec((L, cb), lambda i: (0, i + DI // cb)), pl.BlockSpec((L, cb), lambda i: (0, i)),
                  pl.BlockSpec((k, cb), lambda i: (0, i)), pl.BlockSpec((1, cb), lambda i: (0, i)), ANY],
        out_specs=[pl.BlockSpec((L, cb), lambda i: (0, i + DI // cb)), pl.BlockSpec((8, cb), lambda i: (0, i))],
        out_shape=[jax.ShapeDtypeStruct((L, ZX), BF16), jax.ShapeDtypeStruct((8, CONVD), F32)],
        input_output_aliases={4: 0}, compiler_params=_params(("parallel",)), name=name)(zx, dact, w, b, dzx)


def _sc_fwd(proj, w, *, name):
    L = proj.shape[0]
    cb = 256
    nb = D // cb
    k = w.shape[0]

    def body(b_ref, c_ref, x_ref, w_ref, o_ref):
        u = c_ref[...].astype(F32) * x_ref[...].astype(F32)
        v = u * w_ref[k - 1:k, :]
        for j in range(1, k):
            v = v + _shift_down(u, j) * w_ref[k - 1 - j:k - j, :]
        o_ref[...] = (b_ref[...].astype(F32) * v).astype(BF16)

    return pl.pallas_call(
        body, grid=(nb,),
        in_specs=[pl.BlockSpec((L, cb), lambda i: (0, i)), pl.BlockSpec((L, cb), lambda i: (0, i + nb)),
                  pl.BlockSpec((L, cb), lambda i: (0, i + 2 * nb)), pl.BlockSpec((k, cb), lambda i: (0, i))],
        out_specs=pl.BlockSpec((L, cb), lambda i: (0, i)), out_shape=jax.ShapeDtypeStruct((L, D), BF16),
        compiler_params=_params(("parallel",)), name=name)(proj, proj, proj, w)


def _sc_bwd(proj, dyv, w, *, name):
    L = proj.shape[0]
    cb = 256
    nb = D // cb
    k = w.shape[0]

    def body(b_ref, c_ref, x_ref, dy_ref, w_ref, dp_ref, s_ref):
        cv, xv = c_ref[...].astype(F32), x_ref[...].astype(F32)
        u = cv * xv
        sh = [_shift_down(u, j) for j in range(k)]
        v = sh[0] * w_ref[k - 1:k, :]
        for j in range(1, k):
            v = v + sh[j] * w_ref[k - 1 - j:k - j, :]
        dyv_ = dy_ref[...]
        dp_ref[0] = (dyv_ * v).astype(BF16)
        dv = dyv_ * b_ref[...].astype(F32)
        du = dv * w_ref[k - 1:k, :]
        for j in range(1, k):
            du = du + _shift_up(dv, j) * w_ref[k - 1 - j:k - j, :]
        dp_ref[1] = (du * xv).astype(BF16)
        dp_ref[2] = (du * cv).astype(BF16)
        s_ref[...] = jnp.zeros_like(s_ref)
        for j in range(k):
            s_ref[k - 1 - j:k - j, :] = jnp.sum(dv * sh[j], axis=0, keepdims=True)

    blk = pl.BlockSpec((L, cb), lambda i: (0, i))
    return pl.pallas_call(
        body, grid=(nb,),
        in_specs=[blk, pl.BlockSpec((L, cb), lambda i: (0, i + nb)), pl.BlockSpec((L, cb), lambda i: (0, i + 2 * nb)),
                  blk, pl.BlockSpec((k, cb), lambda i: (0, i))],
        out_specs=[pl.BlockSpec((3, L, cb), lambda i: (0, 0, i)), pl.BlockSpec((8, cb), lambda i: (0, i))],
        out_shape=[jax.ShapeDtypeStruct((3, L, D), BF16), jax.ShapeDtypeStruct((8, D), F32)],
        compiler_params=_params(("parallel",)), name=name)(proj, proj, proj, dyv, w)


def _pieces(v, n):
    out, rest = [], v
    for _ in range(n):
        out.append(rest.astype(BF16))
        rest = rest - out[-1].astype(F32)
    return out


def _cumsum_rows(mask, v):
    m = mask.astype(BF16)
    return _dot(jnp.concatenate([m, m, m], axis=1), jnp.concatenate(_pieces(v, 3), axis=0))


def _ssd_chunk_terms(dtr, prm):
    lane = lax.broadcasted_iota(jnp.int32, (CH, LANES), 1)
    valid = lane < NH
    xdt = dtr + prm[0:1, :]
    dt = jnp.where(valid, jnp.maximum(xdt, 0.0) + jnp.log1p(jnp.exp(-jnp.abs(xdt))), 0.0)
    A = -jnp.exp(prm[1:2, :])
    ri = lax.broadcasted_iota(jnp.int32, (CH, CH), 0)
    ci = lax.broadcasted_iota(jnp.int32, (CH, CH), 1)
    cs = _cumsum_rows(ri >= ci, dt * A)
    last = cs[CH - 1:CH, :]
    spread = (lax.broadcasted_iota(jnp.int32, (2 * LANES, DI), 1) // HP
              == lax.broadcasted_iota(jnp.int32, (2 * LANES, DI), 0) % LANES).astype(BF16)
    gather = ((lax.broadcasted_iota(jnp.int32, (LANES, 2 * DI), 1) % DI) // HP
              == lax.broadcasted_iota(jnp.int32, (LANES, 2 * DI), 0)).astype(BF16)
    return dict(valid=valid, xdt=xdt, dt=dt, A=A, cs=cs, csT=cs.T, last=last, ri=ri, ci=ci, ex=(spread, gather))


def _expand(v, ex):
    if v.shape[0] == 1:
        return _expand(jnp.broadcast_to(v, (8, LANES)), ex)[0:1, :]
    return _dot(jnp.concatenate(_pieces(v, 2), axis=1), ex[0])


def _head_sum(v, ex):
    if v.shape[0] == 1:
        return _head_sum(jnp.broadcast_to(v, (8, DI)), ex)[0:1, :]
    return _dot_nt(jnp.concatenate(_pieces(v, 2), axis=1), ex[1])


def _ssd_fwd(xbc, dtr, prm, *, name):
    L = xbc.shape[0]
    nc = L // CH

    def body(xbc_ref, dtr_ref, prm_ref, y_ref, sp_ref, st_ref):
        @pl.when(pl.program_id(0) == 0)
        def _():
            st_ref[...] = jnp.zeros_like(st_ref)

        prm_v = prm_ref[...]
        t = _ssd_chunk_terms(dtr_ref[...], prm_v)
        cs, csT, ex, causal = t["cs"], t["csT"], t["ex"], t["ri"] >= t["ci"]
        xs = xbc_ref[:, 0:DI].astype(F32)
        X = xs * _expand(t["dt"], ex)
        Xb = X.astype(BF16)
        Xd = (X * _expand(jnp.exp(t["last"] - cs), ex)).astype(BF16)
        Ex = _expand(jnp.exp(cs), ex)
        cdx = _expand(jnp.exp(t["last"]), ex)
        dskx = _expand(prm_v[2:3, :], ex)
        lane = lax.broadcasted_iota(jnp.int32, (CH, LANES), 1)
        sp_ref[0] = st_ref[...]
        for g in range(NG):
            Bg = xbc_ref[:, DI + g * NS:DI + (g + 1) * NS].astype(BF16)
            Cg = xbc_ref[:, DI + GW + g * NS:DI + GW + (g + 1) * NS].astype(BF16)
            G = _dot_nt(Cg, Bg)
            Sg = st_ref[:, g * GW:(g + 1) * GW]
            yoff = _dot(Cg, Sg.astype(BF16)) * Ex[:, g * GW:(g + 1) * GW]
            for j in range(GW // LANES):
                lo = g * GW + j * LANES
                Xp = Xb[:, lo:lo + LANES]
                yd = []
                for h in (lo // HP, lo // HP + 1):
                    seg = cs[:, h:h + 1] - csT[h:h + 1, :]
                    yd.append(_dot((G * jnp.where(causal, jnp.exp(seg), 0.0)).astype(BF16), Xp))
                y_ref[:, lo:lo + LANES] = (jnp.where(lane < HP, yd[0], yd[1]) + yoff[:, j * LANES:(j + 1) * LANES]
                                           + dskx[:, lo:lo + LANES] * xs[:, lo:lo + LANES]).astype(BF16)
            st_ref[:, g * GW:(g + 1) * GW] = Sg * cdx[:, g * GW:(g + 1) * GW] + _dot_tn(Bg, Xd[:, g * GW:(g + 1) * GW])

    return pl.pallas_call(
        body, grid=(nc,),
        in_specs=[pl.BlockSpec((CH, CONVD), lambda c: (c, 0)), pl.BlockSpec((CH, LANES), lambda c: (c, 0)),
                  pl.BlockSpec((8, LANES), lambda c: (0, 0))],
        out_specs=[pl.BlockSpec((CH, DI), lambda c: (c, 0)), pl.BlockSpec((1, NS, DI), lambda c: (c, 0, 0))],
        out_shape=[jax.ShapeDtypeStruct((L, DI), BF16), jax.ShapeDtypeStruct((nc, NS, DI), F32)],
        scratch_shapes=[pltpu.VMEM((NS, DI), F32)],
        compiler_params=_params(("arbitrary",)), name=name)(xbc, dtr, prm)


def _ssd_bwd(xbc, dtr, prm, dy, sprev, *, name):
    L = xbc.shape[0]
    nc = L // CH

    def body(xbc_ref, dtr_ref, prm_ref, dy_ref, sp_ref, dxbc_ref, ddtr_ref, s_ref, dst_ref, dx_scr, de_scr, dd_scr):
        step = pl.program_id(0)

        @pl.when(step == 0)
        def _():
            dst_ref[...] = jnp.zeros_like(dst_ref)
            s_ref[...] = jnp.zeros_like(s_ref)

        prm_v = prm_ref[...]
        t = _ssd_chunk_terms(dtr_ref[...], prm_v)
        cs, csT, ex, ri, ci = t["cs"], t["csT"], t["ex"], t["ri"], t["ci"]
        E = jnp.exp(cs)
        dec = jnp.exp(t["last"] - cs)
        cd = jnp.exp(t["last"])
        xs = xbc_ref[:, 0:DI].astype(F32)
        dtx = _expand(t["dt"], ex)
        X = xs * dtx
        Xb = X.astype(BF16)
        decx = _expand(dec, ex)
        Xd = (X * decx).astype(BF16)
        Ex = _expand(E, ex)
        cdx = _expand(cd, ex)
        dskx = _expand(prm_v[2:3, :], ex)
        lane = lax.broadcasted_iota(jnp.int32, (CH, LANES), 1)
        dcs = jnp.zeros((CH, LANES), F32)
        dcd_x = []
        for g in range(NG):
            gs = slice(g * GW, (g + 1) * GW)
            Bg = xbc_ref[:, DI + g * NS:DI + (g + 1) * NS].astype(BF16)
            Cg = xbc_ref[:, DI + GW + g * NS:DI + GW + (g + 1) * NS].astype(BF16)
            G = _dot_nt(Cg, Bg)
            GT = _dot_nt(Bg, Cg)
            Sg = sp_ref[0, :, gs]
            Sgb = Sg.astype(BF16)
            dyg = dy_ref[:, gs]
            de_scr[:, gs] = dyg * _dot(Cg, Sgb)
            dYo = (Ex[:, gs] * dyg).astype(BF16)
            dC = _dot_nt(dYo, Sgb)
            dS_in = _dot_tn(Cg, dYo)
            dStg = dst_ref[:, gs]
            dStb = dStg.astype(BF16)
            dXd = _dot(Bg, dStb)
            dB = _dot_nt(Xd[:, gs], dStb)
            dd_scr[:, gs] = dXd * X[:, gs]
            dXst = dXd * decx[:, gs]
            dG = jnp.zeros((CH, CH), F32)
            dGT = jnp.zeros((CH, CH), F32)
            for j in range(GW // LANES):
                lo = g * GW + j * LANES
                Xp = Xb[:, lo:lo + LANES]
                dyp = dy_ref[:, lo:lo + LANES]
                dXp = dXst[:, j * LANES:(j + 1) * LANES]
                for k, h in enumerate((lo // HP, lo // HP + 1)):
                    dyh = jnp.where((lane < HP) if k == 0 else (lane >= HP), dyp, 0.0).astype(BF16)
                    seg = cs[:, h:h + 1] - csT[h:h + 1, :]
                    Lm = jnp.where(ri >= ci, jnp.exp(seg), 0.0)
                    LmT = jnp.where(ci >= ri, jnp.exp(-seg), 0.0)
                    dM = _dot_nt(dyh, Xp)
                    dMT = _dot_nt(Xp, dyh)
                    MT = GT * LmT
                    rs = jnp.sum(dM * (G * Lm), axis=1, keepdims=True) - jnp.sum(dMT * MT, axis=1, keepdims=True)
                    dcs = dcs + jnp.where(lane == h, rs, 0.0)
                    dG = dG + dM * Lm
                    dGT = dGT + dMT * LmT
                    dXp = dXp + _dot(MT.astype(BF16), dyh)
                dx_scr[:, lo:lo + LANES] = dXp
            dxbc_ref[:, DI + g * NS:DI + (g + 1) * NS] = (dB + _dot(dGT.astype(BF16), Cg)).astype(BF16)
            dxbc_ref[:, DI + GW + g * NS:DI + GW + (g + 1) * NS] = (dC + _dot(dG.astype(BF16), Bg)).astype(BF16)
            dcd_x.append(jnp.sum(dStg * Sg, axis=0, keepdims=True))
            dst_ref[:, gs] = dStg * cdx[:, gs] + dS_in
        dX = dx_scr[...]
        dy = dy_ref[...]
        ddec = _head_sum(dd_scr[...], ex)
        dcd = _head_sum(jnp.concatenate(dcd_x, axis=1), ex)
        dcs = dcs + _head_sum(de_scr[...], ex) * E - ddec * dec
        row = lax.broadcasted_iota(jnp.int32, (CH, LANES), 0)
        dcs = dcs + jnp.where(row == CH - 1, jnp.sum(ddec * dec, axis=0, keepdims=True) + dcd * cd, 0.0)
        da = _cumsum_rows(ci >= ri, dcs)
        ddt = da * t["A"] + _head_sum(dX * xs, ex)
        ddtr = jnp.where(t["valid"], ddt * _sigmoid(t["xdt"]), 0.0)
        ddtr_ref[...] = ddtr
        dxbc_ref[:, 0:DI] = (dX * dtx + dskx * dy).astype(BF16)
        s_ref[0:1, :] += jnp.sum(da * t["dt"], axis=0, keepdims=True)
        s_ref[1:2, :] += _head_sum(jnp.sum(dy * xs, axis=0, keepdims=True), ex)
        s_ref[2:3, :] += jnp.sum(ddtr, axis=0, keepdims=True)

        @pl.when(step == nc - 1)
        def _():
            s_ref[0:1, :] = s_ref[0:1, :] * t["A"]

    rev = lambda c: (nc - 1 - c, 0)
    return pl.pallas_call(
        body, grid=(nc,),
        in_specs=[pl.BlockSpec((CH, CONVD), rev), pl.BlockSpec((CH, LANES), rev), pl.BlockSpec((8, LANES), lambda c: (0, 0)),
                  pl.BlockSpec((CH, DI), rev), pl.BlockSpec((1, NS, DI), lambda c: (nc - 1 - c, 0, 0))],
        out_specs=[pl.BlockSpec((CH, CONVD), rev), pl.BlockSpec((CH, LANES), rev), pl.BlockSpec((8, LANES), lambda c: (0, 0))],
        out_shape=[jax.ShapeDtypeStruct((L, CONVD), BF16), jax.ShapeDtypeStruct((L, LANES), F32),
                   jax.ShapeDtypeStruct((8, LANES), F32)],
        scratch_shapes=[pltpu.VMEM((NS, DI), F32), pltpu.VMEM((CH, DI), F32), pltpu.VMEM((CH, DI), F32),
                        pltpu.VMEM((CH, DI), F32)],
        compiler_params=_params(("arbitrary",)), name=name)(xbc, dtr, prm, dy, sprev)


def _gnorm_fwd(y, zx, nw, *, name):
    L = y.shape[0]
    tm = min(L, 256)

    def body(y_ref, z_ref, nw_ref, o_ref):
        z = z_ref[...].astype(F32)
        yg = y_ref[...].astype(F32) * (z * _sigmoid(z))
        for g in range(NG):
            v = yg[:, g * GW:(g + 1) * GW]
            r = lax.rsqrt(jnp.mean(v * v, axis=-1, keepdims=True) + EPS)
            o_ref[:, g * GW:(g + 1) * GW] = (v * r * nw_ref[:, g * GW:(g + 1) * GW]).astype(BF16)

    row = pl.BlockSpec((tm, DI), lambda i: (i, 0))
    return pl.pallas_call(body, grid=(L // tm,), in_specs=[row, row, pl.BlockSpec((1, DI), lambda i: (0, 0))],
                          out_specs=row, out_shape=jax.ShapeDtypeStruct((L, DI), BF16),
                          compiler_params=_params(("parallel",)), name=name)(y, zx, nw)


def _gnorm_bwd(y, zx, nw, dyn, *, name):
    L = y.shape[0]
    tm = min(L, 256)

    def body(y_ref, z_ref, nw_ref, dyn_ref, dy_ref, dz_ref, s_ref):
        @pl.when(pl.program_id(0) == 0)
        def _():
            s_ref[...] = jnp.zeros_like(s_ref)

        z, yv = z_ref[...].astype(F32), y_ref[...].astype(F32)
        sz = _sigmoid(z)
        gate = z * sz
        dgate_dz = sz * (1.0 + z * (1.0 - sz))
        for g in range(NG):
            gs = slice(g * GW, (g + 1) * GW)
            v = yv[:, gs] * gate[:, gs]
            r = lax.rsqrt(jnp.mean(v * v, axis=-1, keepdims=True) + EPS)
            vhat = v * r
            dn = dyn_ref[:, gs].astype(F32)
            s_ref[0:1, gs] += jnp.sum(dn * vhat, axis=0, keepdims=True)
            dvhat = dn * nw_ref[:, gs]
            dv = r * (dvhat - vhat * jnp.mean(dvhat * vhat, axis=-1, keepdims=True))
            dy_ref[:, gs] = dv * gate[:, gs]
            dz_ref[:, gs] = (dv * yv[:, gs] * dgate_dz[:, gs]).astype(BF16)

    row = pl.BlockSpec((tm, DI), lambda i: (i, 0))
    return pl.pallas_call(body, grid=(L // tm,), in_specs=[row, row, pl.BlockSpec((1, DI), lambda i: (0, 0)), row],
                          out_specs=[row, row, pl.BlockSpec((8, DI), lambda i: (0, 0))],
                          out_shape=[jax.ShapeDtypeStruct((L, DI), F32), jax.ShapeDtypeStruct((L, ZX), BF16),
                                     jax.ShapeDtypeStruct((8, DI), F32)],
                          compiler_params=_params(("arbitrary",)), name=name)(y, zx, nw, dyn)


def _adamw(w, g, m, v, *, name, g_row=0, w_row=0, rows=None, into=None, emit_g=False):
    lead = w.ndim == 3
    R, C = w.shape[-2:]
    rows = R if rows is None else rows
    tr = max([t for t in range(8, rows + 1, 8) if rows % t == 0 and t * C <= 256 * 1024], default=rows)
    assert g_row % tr == 0 and w_row % tr == 0, (name, g_row, w_row, tr)
    n_out = 4 if emit_g else 3

    def body(w_ref, g_ref, m_ref, v_ref, *rest):
        outs = rest[-n_out:]
        gv = g_ref[...]
        mn = ADAM_B1 * m_ref[...] + (1.0 - ADAM_B1) * gv
        vn = ADAM_B2 * v_ref[...] + (1.0 - ADAM_B2) * (gv * gv)
        m_hat = mn / (1.0 - ADAM_B1 ** ADAM_STEP)
        v_hat = vn / (1.0 - ADAM_B2 ** ADAM_STEP)
        d_ref, mo_ref, vo_ref = outs[-3:]
        d_ref[...] = -ADAM_LR * (m_hat / (jnp.sqrt(v_hat) + ADAM_EPS) + ADAM_WD * w_ref[...])
        mo_ref[...] = mn
        vo_ref[...] = vn
        if emit_g:
            outs[0][...] = gv

    blk = (pl.BlockSpec((None, tr, C), lambda i: (0, i + w_row // tr, 0)) if lead
           else pl.BlockSpec((tr, C), lambda i: (i + w_row // tr, 0)))
    args, in_specs, alias = [w, g, m, v], [blk, pl.BlockSpec((tr, C), lambda i: (i + g_row // tr, 0)), blk, blk], {}
    if into is not None:
        args, in_specs, alias = args + list(into), in_specs + [ANY] * n_out, {4 + k: k for k in range(n_out)}
    return pl.pallas_call(body, grid=(rows // tr,), in_specs=in_specs, out_specs=[blk] * n_out,
                          out_shape=[jax.ShapeDtypeStruct(w.shape, F32)] * n_out, input_output_aliases=alias,
                          compiler_params=_params(("parallel",)), name=name)(*args)


def _residual(acc, xv, gv):
    return xv + gv * acc, acc


def _like(buf):
    return jax.ShapeDtypeStruct(buf.shape, buf.dtype)


def _mlp_fwd(x, mod, nw, wb, up_row, down_row, tag):
    sh, sc, g = mod
    h = _modnorm_fwd(x, nw, sc, sh, name=tag + "_norm")
    a = _matmul(h, wb, n=DFF, tm=TM_ALL, b_spec=pl.BlockSpec((None, D, 512), lambda mi, j: (j // 2, up_row // D, j % 2)),
                epi=lambda acc: (jnp.maximum(acc, 0.0),), out_dtypes=(BF16,), name=tag + "_up")
    xn, y = _matmul(a, wb, n=D, tm=TM_HALF, contract=_nn_split_sq,
                    b_spec=pl.BlockSpec((N_CHIPS, D, 512), lambda mi, j: (0, down_row // D, j)),
                    extras=(x, g), epi=_residual, out_dtypes=(F32, BF16), name=tag + "_down")
    return xn, (x, h, a, y)


def _mlp_bwd(dxo, dy, gsum, saved, mod, nw, wb, gb, up_row, down_row, below, tag):
    x, h, a, y = saved
    sh, sc, g = mod
    du = _matmul(dy, wb, n=DFF, tm=TM_ALL, contract=_nt,
                 b_spec=pl.BlockSpec((None, 512, D), lambda mi, j: (j // 2, down_row // 512 + j % 2, 0)),
                 extras=(a,), epi=lambda acc, av: (acc * (2.0 * av.astype(F32)),), out_dtypes=(BF16,), name=tag + "_dact")
    gb = _matmul_tn(a, dy, m=DFF, n=D, tm=D, tn=D, a_square=True, into=gb, out_struct=_like(wb),
                    out_spec=pl.BlockSpec((None, D, D), lambda mi, j: (mi, down_row // D, 0)), name=tag + "_ddown")
    dh = _matmul(du, wb, n=D, tm=TM_HALF, contract=_nt_split,
                 b_spec=pl.BlockSpec((N_CHIPS, 512, D), lambda mi, j: (0, up_row // 512 + j, 0)), name=tag + "_dh")
    gb = _matmul_tn(h, du, m=D, n=DFF, tm=D, into=gb, out_struct=_like(wb),
                    out_spec=pl.BlockSpec((None, D, 512), lambda mi, j: (j // 2, up_row // D, j % 2)), name=tag + "_dup")
    dx, sums, *nxt = _modnorm_bwd(x, dh, dxo, nw, sc, gsum, below, name=tag + "_dnorm")
    return dx, gb, sums, *nxt


def _ssd_fwd_scan(x, mod, nw, w_zx, w_dt, conv_w, conv_b, prm, tag):
    sh, sc, g = mod
    h = _modnorm_fwd(x, nw, sc, sh, name=tag + "_norm")
    zx = _matmul(h, w_zx, n=ZX, tm=TM_ALL, out_dtypes=(BF16,), name=tag + "_in")
    dtr = _matmul(h, w_dt, n=LANES, tm=TM_ALL, name=tag + "_in_dt")
    xbc = _ssd_conv_fwd(zx, conv_w, conv_b, name=tag + "_conv")
    y, sprev = _ssd_fwd(xbc, dtr, prm, name=tag + "_scan")
    return h, zx, dtr, xbc, y, sprev


def _ssd_fwd_out(x, mod, scan, gn_w, w_out, tag):
    sh, sc, g = mod
    h, zx, dtr, xbc, y, sprev = scan
    yn = _gnorm_fwd(y, zx, gn_w, name=tag + "_gnorm")
    xn, yo = _matmul(yn, w_out, n=D, tm=TM_HALF, contract=_nn_split,
                     b_spec=pl.BlockSpec((N_CHIPS, 512, 512), lambda mi, j: (0, 0, j)),
                     extras=(x, g), epi=_residual, out_dtypes=(F32, BF16), name=tag + "_out")
    return xn, (x, h, zx, dtr, xbc, y, sprev, yn, yo)


def _ssd_bwd_out(dyo, saved, w_out, tag):
    x, h, zx, dtr, xbc, y, sprev, yn, yo = saved
    dyn = _matmul(dyo, w_out, n=DI, tm=TM_ALL, contract=_nt, b_spec=pl.BlockSpec((None, 512, D), lambda mi, j: (j, 0, 0)),
                  out_dtypes=(BF16,), name=tag + "_dyn")
    g_out = _matmul_tn(yn, dyo, m=DI, n=D, tn=D, out_struct=_like(w_out),
                       out_spec=pl.BlockSpec((None, 512, D), lambda mi, j: (mi, 0, 0)), name=tag + "_dout")
    return dyn, g_out


def _ssd_bwd_rest(dxo, dy, dzx, gsum, saved, mod, nw, w_zx, w_dt, conv_w, conv_b, prm, tag):
    x, h, zx, dtr, xbc, y, sprev, yn, yo = saved
    sh, sc, g = mod
    dxbc, ddtr, ssum = _ssd_bwd(xbc, dtr, prm, dy, sprev, name=tag + "_dscan")
    dzx, csum = _ssd_conv_bwd(zx, dxbc, conv_w, conv_b, dzx, name=tag + "_dconv")
    dh_dt = _matmul(ddtr, w_dt, n=D, tm=TM_ALL, contract=_nt, name=tag + "_dh_dt")
    dh = _matmul(dzx, w_zx, n=D, tm=TM_HALF, contract=_nt, extras=(dh_dt,), epi=lambda acc, e: (acc + e,), name=tag + "_dh")
    d_w_zx = _matmul_tn(h, dzx, m=D, n=ZX, tm=D, name=tag + "_din")
    d_w_dt = _matmul_tn(h, ddtr, m=D, n=LANES, tm=D, name=tag + "_din_dt")
    dx, sums = _modnorm_bwd(x, dh, dxo, nw, sc, gsum, None, name=tag + "_dnorm")
    return dx, d_w_zx, d_w_dt, sums, csum, ssum


def _sc_layer_fwd(x, mod, nw, w_sc_in, conv_w, wb, out_row, tag):
    sh, sc, g = mod
    h = _modnorm_fwd(x, nw, sc, sh, name=tag + "_norm")
    proj = _matmul(h, w_sc_in, n=3 * D, tm=TM_ALL, tn=256, out_dtypes=(BF16,),
                   b_spec=pl.BlockSpec((None, D, 256), lambda mi, j: (j // 3, 0, j % 3)),
                   name=tag + "_in")
    yv = _sc_fwd(proj, conv_w, name=tag + "_conv")
    xn, yo = _matmul(yv, wb, n=D, tm=TM_HALF, contract=_nn_split,
                     b_spec=pl.BlockSpec((N_CHIPS, 256, 512), lambda mi, j: (0, out_row // 256, j)),
                     extras=(x, g), epi=_residual, out_dtypes=(F32, BF16), name=tag + "_out")
    return xn, (x, h, proj, yv, yo)


def _sc_layer_bwd(dxo, dyo, gsum, saved, mod, nw, w_sc_in, conv_w, wb, gb, out_row, below, tag):
    x, h, proj, yv, yo = saved
    sh, sc, g = mod
    L = x.shape[0]
    dyv = _matmul(dyo, wb, n=D, tm=TM_ALL, tn=256, contract=_nt,
                  b_spec=pl.BlockSpec((None, 256, D), lambda mi, j: (j, out_row // 256, 0)), name=tag + "_dyv")
    gb = _matmul_tn(yv, dyo, m=D, n=D, tm=256, tn=D, into=gb, out_struct=_like(wb),
                    out_spec=pl.BlockSpec((None, 256, D), lambda mi, j: (mi, out_row // 256, 0)), name=tag + "_dout")
    dproj, csum = _sc_bwd(proj, dyv, conv_w, name=tag + "_dconv")
    tm = min(L, TM_HALF)
    dh = _matmul(dproj, w_sc_in, n=D, tm=tm, contract=_nt_sc_in, a_spec=pl.BlockSpec((3, tm, D), lambda mi, j: (0, mi, 0)),
                 b_spec=pl.BlockSpec((N_CHIPS, 512, SC_IN_SHARD), lambda mi, j: (0, j, 0)), name=tag + "_dh")
    g_sc_in = _matmul_tn(h, dproj, m=D, n=3 * D, tm=D, tn=256, b_spec=pl.BlockSpec((None, L, 256), lambda mi, j: (j // 4, 0, j % 4)),
                         out_spec=pl.BlockSpec((None, D, 256), lambda mi, j: (j // 3, 0, j % 3)),
                         out_struct=jax.ShapeDtypeStruct((N_CHIPS, D, SC_IN_SHARD), BF16), name=tag + "_din")
    dx, sums, *nxt = _modnorm_bwd(x, dh, dxo, nw, sc, gsum, below, name=tag + "_dnorm")
    return dx, gb, g_sc_in, sums, csum, *nxt


SUB_ROW = (0, 8, 16, 24)
SSD_CONV_ROW, GNORM_ROW, FINAL_ROW, SC_CONV_ROW, HEAD_ROW, SMALL_ROWS = 32, 56, 72, 80, 88, 96


def _all_gather_rows(blk, *, name):
    m_per, n = blk.shape

    def body(x_ref, out_ref, send_sems, recv_sems, local_sem):
        x, y, c = lax.axis_index("x"), lax.axis_index("y"), lax.axis_index("c")
        me, sibling = (x, y, c), (x, y, 1 - c)
        chips = [(1 - x, y), (x, 1 - y), (1 - x, 1 - y)]

        def rows(px, py, pc):
            return out_ref.at[pl.ds((4 * px + 2 * py + pc) * m_per, m_per), :]

        def copy(k, block, to, src=None):
            return pltpu.make_async_remote_copy(src_ref=rows(*block) if src is None else src, dst_ref=rows(*block),
                                                send_sem=send_sems.at[k], recv_sem=recv_sems.at[k], device_id=to,
                                                device_id_type=MESH)

        mine = pltpu.make_async_copy(x_ref, rows(*me), local_sem)
        mine.start()
        first = [copy(0, me, sibling, src=x_ref)] + [copy(1 + j, me, (*chip, c), src=x_ref) for j, chip in enumerate(chips)]
        for cp in first:
            cp.start()
        passed = [copy(4 + j, (*chip, c), sibling) for j, chip in enumerate(chips)]
        for j, chip in enumerate(chips):
            copy(1 + j, (*chip, c), me).wait_recv()
            passed[j].start()
        copy(0, sibling, me).wait_recv()
        for j, chip in enumerate(chips):
            copy(4 + j, (*chip, 1 - c), me).wait_recv()
        for cp in first + passed:
            cp.wait_send()
        mine.wait()

    return pl.pallas_call(
        body, out_shape=jax.ShapeDtypeStruct((N_DEV * m_per, n), blk.dtype),
        in_specs=[pl.BlockSpec(memory_space=pltpu.VMEM)], out_specs=pl.BlockSpec(memory_space=pltpu.VMEM),
        scratch_shapes=[pltpu.SemaphoreType.DMA((7,)), pltpu.SemaphoreType.DMA((7,)), pltpu.SemaphoreType.DMA],
        name=name)(blk)


def _half(ref, chip, c):
    hr = ref.shape[1] // 2
    return ref.at[chip, pl.ds(c * hr, hr), :]


def _gather_copy(bufs, sends, recvs, b, k, chip, pc, to):
    piece = _half(bufs[b], 2 * chip[0] + chip[1], pc)
    return pltpu.make_async_remote_copy(src_ref=piece, dst_ref=piece, send_sem=sends.at[4 * b + k], recv_sem=recvs.at[4 * b + k],
                                        device_id=to, device_id_type=MESH)


def _split_call(body, bufs, sems_in, n_sems, *, name, after=(), token=False, lands=()):
    nb, na, nl, starts = len(bufs), len(after), len(lands), not sems_in

    def wrapped(*refs):
        sems = refs[nb + na:nb + na + 2] if starts else refs[nb:nb + 2]
        made = refs[nb + na + 2 + nb:nb + na + 2 + nb + nl] if starts else ()
        body(tuple(refs[:nb]) + tuple(made), sems[0], sems[1])
        if token:
            refs[-1][...] = jnp.zeros_like(refs[-1])

    out_shape = [pltpu.SemaphoreType.DMA((n_sems,)) for _ in range(2 if starts else 0)]
    out_specs = [SEM] * len(out_shape) + [ANY] * (nb + nl)
    alias = {b: len(out_shape) + b for b in range(nb)}
    out_shape += [jax.ShapeDtypeStruct(b.shape, b.dtype) for b in bufs] + list(lands)
    if token:
        out_shape.append(jax.ShapeDtypeStruct((8, LANES), F32))
        out_specs.append(pl.BlockSpec(memory_space=pltpu.VMEM))
    return pl.pallas_call(
        wrapped, out_shape=out_shape, in_specs=[ANY] * nb + [SEM] * len(sems_in) + [ANY] * na, out_specs=out_specs,
        input_output_aliases=alias,
        compiler_params=pltpu.CompilerParams(has_side_effects=pltpu.SideEffectType.DATAFLOW_SIDE_EFFECTING),
        name=name)(*bufs, *sems_in, *after)


def _gather_start(bufs, *, name, after=()):
    nb = len(bufs)

    def body(ins, sends, recvs):
        x, y, c = lax.axis_index("x"), lax.axis_index("y"), lax.axis_index("c")
        chips = [(1 - x, y), (x, 1 - y), (1 - x, 1 - y)]
        for b in range(nb):
            _gather_copy(ins, sends, recvs, b, 0, (x, y), c, (x, y, 1 - c)).start()
            for j, chip in enumerate(chips):
                _gather_copy(ins, sends, recvs, b, 1 + j, (x, y), c, (*chip, c)).start()

    out = _split_call(body, bufs, (), 4 * nb, name=name, after=after, token=True)
    return (out[0], out[1], out[2:2 + nb]), out[-1]


def _gather_wait_first(flight, *, name, after=()):
    sends, recvs, bufs = flight
    nb = len(bufs)

    def body(ins, sends_, recvs_):
        x, y, c = lax.axis_index("x"), lax.axis_index("y"), lax.axis_index("c")
        chips = [(1 - x, y), (x, 1 - y), (1 - x, 1 - y)]
        for b in range(nb):
            _gather_copy(ins, sends_, recvs_, b, 0, (x, y), c, (x, y, 1 - c)).wait_send()
            _gather_copy(ins, sends_, recvs_, b, 0, (x, y), 1 - c, (x, y, c)).wait_recv()
            for j, chip in enumerate(chips):
                _gather_copy(ins, sends_, recvs_, b, 1 + j, (x, y), c, (*chip, c)).wait_send()
                _gather_copy(ins, sends_, recvs_, b, 1 + j, chip, c, (x, y, c)).wait_recv()

    return _split_call(body, bufs, (sends, recvs), 4 * nb, name=name, after=after)


def _gather_forward(bufs, *, name):
    nb = len(bufs)

    def body(ins, sends, recvs):
        x, y, c = lax.axis_index("x"), lax.axis_index("y"), lax.axis_index("c")
        chips = [(1 - x, y), (x, 1 - y), (1 - x, 1 - y)]
        for b in range(nb):
            for j, chip in enumerate(chips):
                _gather_copy(ins, sends, recvs, b, 1 + j, chip, c, (x, y, 1 - c)).start()

    out = _split_call(body, bufs, (), 4 * nb, name=name)
    return out[0], out[1], out[2:2 + nb]


def _gather_wait_forward(flight, *, name, after=()):
    sends, recvs, bufs = flight
    nb = len(bufs)

    def body(ins, sends_, recvs_):
        x, y, c = lax.axis_index("x"), lax.axis_index("y"), lax.axis_index("c")
        chips = [(1 - x, y), (x, 1 - y), (1 - x, 1 - y)]
        for b in range(nb):
            for j, chip in enumerate(chips):
                _gather_copy(ins, sends_, recvs_, b, 1 + j, chip, c, (x, y, 1 - c)).wait_send()
                _gather_copy(ins, sends_, recvs_, b, 1 + j, chip, 1 - c, (x, y, c)).wait_recv()

    return _split_call(body, bufs, (sends, recvs), 4 * nb, name=name, after=after)


def _owner_copies(hs, lands, sends, recvs):
    x, y, c = lax.axis_index("x"), lax.axis_index("y"), lax.axis_index("c")
    chips = [(1 - x, y), (x, 1 - y), (1 - x, 1 - y)]
    return [pltpu.make_async_remote_copy(src_ref=hs[b].at[2 * cx + cy], dst_ref=lands[b].at[j], send_sem=sends.at[3 * b + j],
                                         recv_sem=recvs.at[3 * b + j], device_id=(cx, cy, c), device_id_type=MESH)
            for b in range(len(hs)) for j, (cx, cy) in enumerate(chips)]


def _owners_start(hs, *, name):
    nb = len(hs)
    lands = [jax.ShapeDtypeStruct((3,) + h.shape[1:], h.dtype) for h in hs]

    def body(refs, sends, recvs):
        for cp in _owner_copies(refs[:nb], refs[nb:], sends, recvs):
            cp.start()

    out = _split_call(body, list(hs), (), 3 * nb, name=name, token=True, lands=lands)
    return (out[0], out[1], out[2:2 + 2 * nb]), out[-1]


def _owners_wait(flight, *, name, after=()):
    sends, recvs, bufs = flight
    nb = len(bufs) // 2

    def body(refs, sends_, recvs_):
        for cp in _owner_copies(refs[:nb], refs[nb:], sends_, recvs_):
            cp.wait()

    out = _split_call(body, bufs, (sends, recvs), 3 * nb, name=name, after=after)
    return out[:nb], out[nb:]


def _sibling_copies(gs, lands, sends, recvs):
    x, y, c = lax.axis_index("x"), lax.axis_index("y"), lax.axis_index("c")
    copies = []
    for b in range(len(gs)):
        hr = gs[b].shape[1] // 2
        copies.append(pltpu.make_async_remote_copy(
            src_ref=gs[b].at[:, pl.ds((1 - c) * hr, hr), :], dst_ref=lands[b], send_sem=sends.at[b], recv_sem=recvs.at[b],
            device_id=(x, y, 1 - c), device_id_type=MESH))
    return copies


def _sibling_start(gs, *, name, after=()):
    nb = len(gs)
    lands = [jax.ShapeDtypeStruct((g.shape[0], g.shape[1] // 2, g.shape[2]), g.dtype) for g in gs]

    def body(refs, sends, recvs):
        for cp in _sibling_copies(refs[:nb], refs[nb:], sends, recvs):
            cp.start()

    out = _split_call(body, list(gs), (), nb, name=name, after=after, token=True, lands=lands)
    return (out[0], out[1], out[2:2 + 2 * nb]), out[-1]


def _sibling_wait(flight, *, name, after=()):
    sends, recvs, bufs = flight
    nb = len(bufs) // 2

    def body(refs, sends_, recvs_):
        for cp in _sibling_copies(refs[:nb], refs[nb:], sends_, recvs_):
            cp.wait()

    out = _split_call(body, bufs, (sends, recvs), nb, name=name, after=after)
    return out[:nb], out[nb:]


def _result_copies(ts, sends, recvs):
    x, y, c = lax.axis_index("x"), lax.axis_index("y"), lax.axis_index("c")
    return [pltpu.make_async_remote_copy(src_ref=ts[b].at[c], dst_ref=ts[b].at[c], send_sem=sends.at[b], recv_sem=recvs.at[b],
                                         device_id=(x, y, 1 - c), device_id_type=MESH) for b in range(len(ts))]


def _result_start(ts, *, name):
    def body(refs, sends, recvs):
        for cp in _result_copies(refs, sends, recvs):
            cp.start()

    out = _split_call(body, ts, (), len(ts), name=name, token=True)
    return (out[0], out[1], out[2:2 + len(ts)]), out[-1]


def _result_wait(flight, *, name, after=()):
    sends, recvs, bufs = flight

    def body(refs, sends_, recvs_):
        for cp in _result_copies(refs, sends_, recvs_):
            cp.wait()

    return _split_call(body, bufs, (sends, recvs), len(bufs), name=name, after=after)


def _row_tile(rows, cols):
    best = 16
    for t in range(16, rows + 1, 16):
        if rows % t == 0 and t * cols <= 640 * 1024:
            best = t
    assert rows % best == 0, (rows, cols)
    return best


def _add_sibling_half(g, recv, core, *, name):
    nk, r, n = g.shape
    hr = r // 2
    tr = _row_tile(hr, n)

    def body(c_ref, a_ref, b_ref, o_ref):
        o_ref[...] = (a_ref[...].astype(F32) + b_ref[...].astype(F32)).astype(BF16)

    grid_spec = pltpu.PrefetchScalarGridSpec(
        num_scalar_prefetch=1, grid=(nk, hr // tr),
        in_specs=[pl.BlockSpec((None, tr, n), lambda k, i, c_ref: (k, c_ref[0] * (hr // tr) + i, 0)),
                  pl.BlockSpec((None, tr, n), lambda k, i, c_ref: (k, i, 0))],
        out_specs=pl.BlockSpec((None, tr, n), lambda k, i, c_ref: (k, i, 0)))
    return pl.pallas_call(body, grid_spec=grid_spec, out_shape=jax.ShapeDtypeStruct((nk, hr, n), BF16),
                          compiler_params=_params(("parallel", "parallel")), name=name)(core, g, recv)


def _add_chip_sums(h, recv, chip_core, *, name):
    _, hr, n = h.shape
    tr = _row_tile(hr, n)

    def body(k_ref, a_ref, b_ref, o_ref):
        o_ref[...] = ((a_ref[...].astype(F32) + b_ref[0].astype(F32)) + b_ref[1].astype(F32)) + b_ref[2].astype(F32)

    grid_spec = pltpu.PrefetchScalarGridSpec(
        num_scalar_prefetch=1, grid=(hr // tr,),
        in_specs=[pl.BlockSpec((None, tr, n), lambda i, k_ref: (k_ref[0], i, 0)),
                  pl.BlockSpec((3, tr, n), lambda i, k_ref: (0, i, 0))],
        out_specs=pl.BlockSpec((None, tr, n), lambda i, k_ref: (k_ref[1], i, 0)))
    return pl.pallas_call(body, grid_spec=grid_spec, out_shape=jax.ShapeDtypeStruct((2, hr, n), F32),
                          compiler_params=_params(("parallel",)), name=name)(chip_core, h, recv)


def _sum_devices(g, *, name):
    nd, r, n = g.shape

    def body(g_ref, o_ref):
        acc = g_ref[0]
        for i in range(1, nd):
            acc = acc + g_ref[i]
        o_ref[...] = acc

    return pl.pallas_call(body, out_shape=jax.ShapeDtypeStruct((r, n), F32), name=name)(g)


def _own_slot(shard, chip):
    return lax.dynamic_update_slice(jnp.zeros((N_CHIPS,) + shard.shape, BF16), shard[None], (chip, 0, 0))


def kernel(x, c, ada_w, ada_b, mix_norm_w, mlp_norm_w, mlp_up, mlp_down, ssd_in_w, ssd_conv_w, ssd_conv_b, ssd_dt_bias, ssd_A_log, ssd_D, ssd_norm_w, ssd_out_w, sc_in_w, sc_conv_w, sc_out_w, final_norm_w, loss_target, m_ada_w, m_ada_b, m_mix_norm_w, m_mlp_norm_w, m_mlp_up, m_mlp_down, m_ssd_in_w, m_ssd_conv_w, m_ssd_conv_b, m_ssd_dt_bias, m_ssd_A_log, m_ssd_D, m_ssd_norm_w, m_ssd_out_w, m_sc_in_w, m_sc_conv_w, m_sc_out_w, m_final_norm_w, v_ada_w, v_ada_b, v_mix_norm_w, v_mlp_norm_w, v_mlp_up, v_mlp_down, v_ssd_in_w, v_ssd_conv_w, v_ssd_conv_b, v_ssd_dt_bias, v_ssd_A_log, v_ssd_D, v_ssd_norm_w, v_ssd_out_w, v_sc_in_w, v_sc_conv_w, v_sc_out_w, v_final_norm_w):
    xi, yi, ci = lax.axis_index("x"), lax.axis_index("y"), lax.axis_index("c")
    chip = 2 * xi + yi
    dev = 2 * chip + ci
    n_ada = ada_w.shape[2]

    conv_flat = jnp.concatenate([ssd_conv_w.reshape(-1), sc_conv_w.reshape(-1), jnp.zeros((256,), F32)]).reshape(4, D)
    blk0 = jnp.concatenate([c, conv_flat, jnp.zeros((3, D), F32)], axis=0)
    got0 = _all_gather_rows(blk0, name="gather_cond").reshape(N_DEV, 8, D)
    c_all = got0[:, 0]
    conv_all = got0[0::2, 1:5].reshape(N_CHIPS, 4 * D)
    ssd_conv = jnp.moveaxis(conv_all[:, :4 * 768].reshape(N_CHIPS, 4, 768), 0, 1).reshape(4, CONVD)
    sc_conv = jnp.moveaxis(conv_all[:, 4 * 768:4 * 768 + 3 * 256].reshape(N_CHIPS, 3, 256), 0, 1).reshape(3, D)
    mod_shard = [_matmul(c_all, ada_w, n=n_ada, a_silu=True, b_spec=pl.BlockSpec((None, D, 512), lambda mi, j, i=i: (i, 0, j)),
                         extras=(lax.dynamic_slice(ada_b, (i, chip * n_ada), (1, n_ada)),),
                         epi=lambda acc, b: (acc + b,), name=f"ada_mod{i}") for i in range(2)]
    mod_all = _all_gather_rows(jnp.concatenate(mod_shard, axis=0), name="gather_mod")
    mod_all = mod_all.reshape(N_DEV, 2, N_DEV, n_ada)[0::2]
    mod = jnp.moveaxis(lax.dynamic_index_in_dim(mod_all, dev, axis=2, keepdims=False), 0, 1).reshape(2, 6, D)
    mods = [[mod[i, j:j + 1] for j in range(6)] for i in range(2)]

    bf = lambda v: v.astype(BF16)
    up_row, down_row, sc_out_row = 0, D, 2 * D
    a_bufs = [_own_slot(bf(ssd_in_w[0]), chip)]
    b_bufs = [_own_slot(bf(ssd_out_w[0]), chip), _own_slot(bf(jnp.concatenate([mlp_up[0], mlp_down[0]], axis=0)), chip)]
    c_bufs = [_own_slot(bf(sc_in_w[0]), chip), _own_slot(bf(jnp.concatenate([mlp_up[1], mlp_down[1], sc_out_w[0]], axis=0)), chip)]
    fly_a, tok = _gather_start(a_bufs, name="gather_a_start", after=(mod,))
    fly_b, tok = _gather_start(b_bufs, name="gather_b_start", after=(tok,))
    fly_c, tok = _gather_start(c_bufs, name="gather_c_start", after=(tok,))

    row = lambda v: v.reshape(1, -1)
    xs, tgt = x[0], loss_target[0]
    prm = jnp.pad(jnp.concatenate([ssd_dt_bias, ssd_A_log, ssd_D, jnp.zeros((5, NH), F32)], axis=0), ((0, 0), (0, LANES - NH)))
    mix_nw = [row(mix_norm_w[i]) for i in range(2)]
    mlp_nw = [row(mlp_norm_w[i]) for i in range(2)]
    a_bufs = _gather_wait_first(fly_a, name="gather_a_landed", after=(tok,))
    (w_ssd_in,) = _gather_wait_forward(_gather_forward(a_bufs, name="gather_a_pass"), name="gather_a_done")
    ssd_in_full = jnp.moveaxis(w_ssd_in, 0, 1).reshape(D, N_CHIPS * SSD_IN_SHARD)
    w_zx, w_dt = ssd_in_full[:, :ZX], jnp.pad(ssd_in_full[:, ZX:], ((0, 0), (0, LANES - NH)))
    scan = _ssd_fwd_scan(xs, mods[0][0:3], mix_nw[0], w_zx, w_dt, ssd_conv, ssd_conv_b, prm, "ssd")
    fly_b = _gather_forward(_gather_wait_first(fly_b, name="gather_b_landed", after=(scan[3],)), name="gather_b_pass")
    w_ssd_out, w_b = _gather_wait_forward(fly_b, name="gather_b_done", after=(scan[4],))
    x1, s_ssd = _ssd_fwd_out(xs, mods[0][0:3], scan, ssd_norm_w, w_ssd_out, "ssd")
    x2, s_mlp0 = _mlp_fwd(x1, mods[0][3:6], mlp_nw[0], w_b, up_row, down_row, "mlp0")
    c_bufs = _gather_wait_first(fly_c, name="gather_c_landed", after=(x2,))
    w_sc_in, w_c = _gather_wait_forward(_gather_forward(c_bufs, name="gather_c_pass"), name="gather_c_done")
    x3, s_sc = _sc_layer_fwd(x2, mods[1][0:3], mix_nw[1], w_sc_in, sc_conv, w_c, sc_out_row, "sc")
    x4, s_mlp1 = _mlp_fwd(x3, mods[1][3:6], mlp_nw[1], w_c, up_row, down_row, "mlp1")

    core = ci.reshape(1).astype(jnp.int32)
    chip_core = jnp.stack([chip, ci]).astype(jnp.int32)

    def reduce_swap(gbufs, tag, after=()):
        return _sibling_start(gbufs, name=tag + "_sibling_start", after=after)

    def reduce_send(flight, tag, after):
        gs, sib = _sibling_wait(flight, name=tag + "_sibling_landed", after=after)
        hs = [_add_sibling_half(g, s, core, name=f"{tag}_add_sibling{b}") for b, (g, s) in enumerate(zip(gs, sib))]
        return _owners_start(hs, name=tag + "_owners_start")

    def reduce_sum(flight, tag, after):
        hs, lands = _owners_wait(flight, name=tag + "_owners_landed", after=after)
        ts = [_add_chip_sums(h, o, chip_core, name=f"{tag}_add_chips{b}") for b, (h, o) in enumerate(zip(hs, lands))]
        return _result_start(ts, name=tag + "_result_start")

    def reduce_done(flight, tag, after=()):
        return [t.reshape(-1, t.shape[2]) for t in _result_wait(flight, name=tag + "_result_landed", after=after)]

    dx4, fsum, dy, gs = _final_loss(x4, row(final_norm_w), tgt, (mods[1][5], s_mlp1[3]), name="final_loss")
    dx3, g_c, sum_mlp1, dy, gs = _mlp_bwd(dx4, dy, gs, s_mlp1, mods[1][3:6], mlp_nw[1], w_c, None, up_row, down_row,
                                          (mods[1][2], s_sc[4]), "mlp1")
    dx2, g_c, g_sc_in, sum_sc, sc_csum, dy, gs = _sc_layer_bwd(dx3, dy, gs, s_sc, mods[1][0:3], mix_nw[1], w_sc_in, sc_conv,
                                                               w_c, g_c, sc_out_row, (mods[0][5], s_mlp0[3]), "sc")
    dx1, g_b, sum_mlp0, dy, gsum_ssd = _mlp_bwd(dx2, dy, gs, s_mlp0, mods[0][3:6], mlp_nw[0], w_b, None, up_row, down_row,
                                                (mods[0][2], s_ssd[8]), "mlp0")
    dyn, g_ssd_out = _ssd_bwd_out(dy, s_ssd, w_ssd_out, "ssd")
    fly_1, tok = reduce_swap([g_c, g_sc_in, g_b, g_ssd_out], "rs1")
    dy, dzx, gnsum = _gnorm_bwd(s_ssd[5], s_ssd[2], ssd_norm_w + tok[0:1, 0:1], dyn, name="ssd_dgnorm")
    fly_1, tok = reduce_send(fly_1, "rs1", (dy,))
    grad_x, d_w_zx, d_w_dt, sum_ssd, csum, ssum = _ssd_bwd_rest(
        dx1, dy, dzx, gsum_ssd, s_ssd, mods[0][0:3], mix_nw[0], w_zx, w_dt, ssd_conv, ssd_conv_b, prm + tok[0:1, 0:1], "ssd")
    fly_1, tok = reduce_sum(fly_1, "rs1", (grad_x,))

    def ssd_in_owner(k):
        lo, hi = k * SSD_IN_SHARD, (k + 1) * SSD_IN_SHARD
        if hi <= ZX:
            return d_w_zx[:, lo:hi]
        return jnp.concatenate([d_w_zx[:, lo:], d_w_dt[:, :hi - ZX]], axis=1)

    small = jnp.concatenate([sum_ssd + tok[0:1, 0:1], sum_mlp0, sum_sc, sum_mlp1, csum.reshape(24, D), gnsum.reshape(16, D),
                             fsum, sc_csum, jnp.pad(ssum, ((0, 0), (0, D - LANES)))], axis=0)
    small_all = _all_gather_rows(small, name="gather_small").reshape(N_DEV, SMALL_ROWS, D)
    fly_2, tok = reduce_swap([jnp.stack([ssd_in_owner(k) for k in range(N_CHIPS)]).astype(BF16)], "rs2", (small_all,))
    fly_2, tok = reduce_send(fly_2, "rs2", (tok,))
    t_c, t_sc_in, t_b, t_ssd_out = reduce_done(fly_1, "rs1", (tok,))
    small_all = small_all + tok[0:1, 0:1]
    tot = _sum_devices(small_all, name="sum_small")
    loss = tot[FINAL_ROW + 1, 0]
    mod_rows = [r + o for r in SUB_ROW for o in (3, 2, 0)]
    g_ada_b = jnp.stack([tot[r] for r in mod_rows]).reshape(2, 6 * D)
    g_mix_norm = jnp.stack([tot[SUB_ROW[0] + 1], tot[SUB_ROW[2] + 1]])
    g_mlp_norm = jnp.stack([tot[SUB_ROW[1] + 1], tot[SUB_ROW[3] + 1]])
    conv_sums = tot[SSD_CONV_ROW:SSD_CONV_ROW + 24].reshape(8, CONVD)
    g_ssd_conv_w = lax.dynamic_slice(conv_sums, (0, chip * 768), (4, 768))[None]
    g_ssd_conv_b = conv_sums[4:5]
    g_ssd_norm = tot[GNORM_ROW:GNORM_ROW + 2].reshape(1, DI)
    g_final = tot[FINAL_ROW]
    g_sc_conv_w = lax.dynamic_slice(tot[SC_CONV_ROW:SC_CONV_ROW + 3], (0, chip * 256), (3, 256))[None]
    g_a_log, g_d, g_dt_bias = (tot[HEAD_ROW + r:HEAD_ROW + r + 1, 0:NH] for r in range(3))
    c_pad = jnp.concatenate([c_all, jnp.zeros((8, D), F32)], axis=0)
    dmod_all = jnp.stack([small_all[:, r] for r in mod_rows], axis=1).reshape(N_DEV, 2, 6 * D)
    g_ada_w = []
    for i in range(2):
        dm = lax.dynamic_slice(dmod_all[:, i], (0, chip * n_ada), (N_DEV, n_ada))
        g_ada_w.append(_matmul_tn(c_pad, jnp.concatenate([dm, jnp.zeros_like(dm)], axis=0), m=D, n=n_ada, a_silu=True,
                                  name=f"ada_dw{i}"))

    big = dict(ada_w=[(g, 0) for g in g_ada_w], mlp_up=[(t_b, up_row), (t_c, up_row)], mlp_down=[(t_b, down_row), (t_c, down_row)],
               ssd_out_w=[(t_ssd_out, 0)], sc_out_w=[(t_c, sc_out_row)], sc_in_w=[(t_sc_in, 0)], ssd_in_w=None)
    grads = dict(ada_b=g_ada_b, mix_norm_w=g_mix_norm, mlp_norm_w=g_mlp_norm, ssd_conv_w=g_ssd_conv_w,
                 ssd_conv_b=g_ssd_conv_b, ssd_dt_bias=g_dt_bias, ssd_A_log=g_a_log, ssd_D=g_d, ssd_norm_w=g_ssd_norm,
                 sc_conv_w=g_sc_conv_w, final_norm_w=g_final)
    weights = dict(ada_w=(ada_w, m_ada_w, v_ada_w), ada_b=(ada_b, m_ada_b, v_ada_b),
                   mix_norm_w=(mix_norm_w, m_mix_norm_w, v_mix_norm_w), mlp_norm_w=(mlp_norm_w, m_mlp_norm_w, v_mlp_norm_w),
                   mlp_up=(mlp_up, m_mlp_up, v_mlp_up), mlp_down=(mlp_down, m_mlp_down, v_mlp_down),
                   ssd_in_w=(ssd_in_w, m_ssd_in_w, v_ssd_in_w), ssd_conv_w=(ssd_conv_w, m_ssd_conv_w, v_ssd_conv_w),
                   ssd_conv_b=(ssd_conv_b, m_ssd_conv_b, v_ssd_conv_b), ssd_dt_bias=(ssd_dt_bias, m_ssd_dt_bias, v_ssd_dt_bias),
                   ssd_A_log=(ssd_A_log, m_ssd_A_log, v_ssd_A_log), ssd_D=(ssd_D, m_ssd_D, v_ssd_D),
                   ssd_norm_w=(ssd_norm_w, m_ssd_norm_w, v_ssd_norm_w), ssd_out_w=(ssd_out_w, m_ssd_out_w, v_ssd_out_w),
                   sc_in_w=(sc_in_w, m_sc_in_w, v_sc_in_w), sc_conv_w=(sc_conv_w, m_sc_conv_w, v_sc_conv_w),
                   sc_out_w=(sc_out_w, m_sc_out_w, v_sc_out_w), final_norm_w=(final_norm_w, m_final_norm_w, v_final_norm_w))
    def step(nm, parts):
        w, m, v = (t if t.shape[0] == 1 else t.reshape(-1, t.shape[-1]) for t in weights[nm])
        rows, outs = w.shape[-2] // len(parts), None
        for i, (gbuf, g_row) in enumerate(parts):
            outs = _adamw(w, gbuf, m, v, g_row=g_row, w_row=i * rows, rows=rows, into=outs, emit_g=True, name=f"adamw_{nm}{i}")
        return outs

    res = {}
    for nm, (w, m, v) in weights.items():
        two_d = (-1, w.shape[-1]) if w.ndim > 1 else (1, -1)
        if nm not in big:
            res[nm] = (grads[nm], *_adamw(w.reshape(two_d), grads[nm].reshape(two_d), m.reshape(two_d), v.reshape(two_d),
                                          name="adamw_" + nm))
        elif big[nm] is not None:
            res[nm] = step(nm, big[nm])
    fly_2, tok = reduce_sum(fly_2, "rs2", tuple(r[1] for r in res.values()))
    (t_ssd_in,) = reduce_done(fly_2, "rs2", (tok,))
    w_t, m_t, v_t = (jnp.swapaxes(t[0], 0, 1) for t in weights["ssd_in_w"])
    res["ssd_in_w"] = [jnp.swapaxes(o, 0, 1) for o in _adamw(w_t, t_ssd_in.T, m_t, v_t, emit_g=True, name="adamw_ssd_in_w")]
    outs = [[res[nm][k].reshape(weights[nm][0].shape) for nm in weights] for k in range(4)]
    return (loss, grad_x[None], *outs[0], *outs[1], *outs[2], *outs[3])
```

```python
import jax
import jax.numpy as jnp
from jax import lax
from jax.experimental import pallas as pl
from jax.experimental.pallas import tpu as pltpu

F32 = jnp.float32
BF16 = jnp.bfloat16
MESH = pl.DeviceIdType.MESH

D = 1024
DFF = 4096
DI = 2048
NH = 32
HP = 64
NG = 4
NS = 128
CH = 128
CONVD = DI + 2 * NG * NS
ZX = DI + CONVD
GW = NG * NS
LANES = 128
N_CHIPS = 4
N_DEV = 8
EPS = 1e-5
ADAM_LR, ADAM_B1, ADAM_B2, ADAM_EPS, ADAM_WD, ADAM_STEP = 1e-3, 0.9, 0.999, 1e-8, 0.01, 10
VMEM_LIMIT = 48 * 1024 * 1024
TM_ALL = 2048
TM_HALF = 1024
ANY = pl.BlockSpec(memory_space=pl.ANY)
SEM = pl.BlockSpec(memory_space=pltpu.SEMAPHORE)

SSD_IN_SHARD = 1288
SC_IN_SHARD = 768


def _params(sem=None):
    return pltpu.CompilerParams(dimension_semantics=sem, vmem_limit_bytes=VMEM_LIMIT)


def _sigmoid(v):
    return 0.5 * jnp.tanh(0.5 * v) + 0.5


def _dot(a, b, dims=((1,), (0,)), precision=None):
    return lax.dot_general(a, b, (dims, ((), ())), preferred_element_type=F32, precision=precision)


def _dot_nt(a, b):
    return _dot(a, b, ((1,), (1,)))


def _dot_tn(a, b):
    return _dot(a, b, ((0,), (0,)))


def _nn(av, bv):
    return _dot(av.astype(BF16), bv.astype(BF16))


def _nt(av, bv):
    return _dot_nt(av.astype(BF16), bv.astype(BF16))


def _nn_split(av, bv):
    return _dot(av.astype(BF16), bv.reshape(-1, bv.shape[2]))


def _nn_split_sq(av, bv):
    af = av.astype(F32)
    return _nn_split(af * af, bv)


def _nt_split(av, bv):
    kc = bv.shape[2]
    acc = _dot_nt(av[:, 0:kc].astype(BF16), bv[0])
    for s in range(1, bv.shape[0]):
        acc = acc + _dot_nt(av[:, s * kc:(s + 1) * kc].astype(BF16), bv[s])
    return acc


def _nt_sc_in(av, bv):
    q = 256
    acc = None
    for i in range(3 * D // q):
        a_blk = av[i // 4][:, (i % 4) * q:(i % 4 + 1) * q]
        b_blk = bv[i // 3][:, (i % 3) * q:(i % 3 + 1) * q]
        t = _dot_nt(a_blk, b_blk)
        acc = t if acc is None else acc + t
    return acc


def _matmul(a, b, *, name, n, contract=_nn, a_spec=None, b_spec=None, tm=512, tn=512, extras=(), epi=None,
            out_dtypes=(F32,), a_silu=False):
    M = a.shape[-2]
    tm, tn = min(tm, M), min(tn, n)
    assert M % tm == 0 and n % tn == 0, (name, M, n, tm, tn)
    n_ex = len(extras)
    if a_spec is None:
        a_spec = pl.BlockSpec((tm, a.shape[1]), lambda i, j: (i, 0))
    if b_spec is None:
        b_spec = (pl.BlockSpec((tn, b.shape[1]), lambda i, j: (j, 0)) if contract is _nt
                  else pl.BlockSpec((b.shape[0], tn), lambda i, j: (0, j)))

    def body(*refs):
        av = refs[0][...]
        if a_silu:
            av = av * _sigmoid(av)
        acc = contract(av, refs[1][...])
        res = epi(acc, *[r[...] for r in refs[2:2 + n_ex]]) if epi is not None else (acc,)
        for o_ref, r in zip(refs[2 + n_ex:], res, strict=True):
            o_ref[...] = r.astype(o_ref.dtype)

    in_specs = [a_spec, b_spec]
    for e in extras:
        in_specs.append(pl.BlockSpec((1, tn), lambda i, j: (0, j)) if e.shape[0] == 1 and M != 1
                        else pl.BlockSpec((tm, tn), lambda i, j: (i, j)))
    outs = pl.pallas_call(
        body, grid=(M // tm, n // tn), in_specs=in_specs,
        out_specs=[pl.BlockSpec((tm, tn), lambda i, j: (i, j)) for _ in out_dtypes],
        out_shape=[jax.ShapeDtypeStruct((M, n), dt) for dt in out_dtypes],
        compiler_params=_params(("parallel", "parallel")), name=name)(a, b, *extras)
    return outs if len(out_dtypes) > 1 else outs[0]


def _matmul_tn(a, b, *, name, m, n, tm=512, tn=512, a_spec=None, b_spec=None, out_spec=None, out_struct=None, into=None,
               a_silu=False, a_square=False):
    T = a.shape[-2]
    tm, tn = min(tm, m), min(tn, n)
    assert m % tm == 0 and n % tn == 0, (name, m, n, tm, tn)
    if a_spec is None:
        a_spec = pl.BlockSpec((T, tm), lambda i, j: (0, i))
    if b_spec is None:
        b_spec = pl.BlockSpec((T, tn), lambda i, j: (0, j))
    if out_spec is None:
        out_spec, out_struct = pl.BlockSpec((tm, tn), lambda i, j: (i, j)), jax.ShapeDtypeStruct((m, n), F32)

    def body(a_ref, b_ref, *rest):
        av = a_ref[...]
        if a_silu:
            av = av * _sigmoid(av)
        if a_square:
            av = av.astype(F32) * av.astype(F32)
        rest[-1][...] = _dot_tn(av.astype(BF16), b_ref[...].astype(BF16)).astype(rest[-1].dtype)

    args, in_specs, alias = [a, b], [a_spec, b_spec], {}
    if into is not None:
        args, in_specs, alias = args + [into], in_specs + [ANY], {2: 0}
    return pl.pallas_call(body, grid=(m // tm, n // tn), in_specs=in_specs, out_specs=out_spec, out_shape=out_struct,
                          input_output_aliases=alias, compiler_params=_params(("parallel", "parallel")), name=name)(*args)


def _modnorm_fwd(x, nw, sc, sh, *, name):
    L = x.shape[0]
    tm = min(L, 512)

    def body(x_ref, nw_ref, sc_ref, sh_ref, h_ref):
        xv = x_ref[...]
        r = lax.rsqrt(jnp.mean(xv * xv, axis=-1, keepdims=True) + EPS)
        h_ref[...] = ((xv * r * nw_ref[...]) * (1.0 + sc_ref[...]) + sh_ref[...]).astype(BF16)

    row = pl.BlockSpec((tm, D), lambda i: (i, 0))
    vec = pl.BlockSpec((1, D), lambda i: (0, 0))
    return pl.pallas_call(body, grid=(L // tm,), in_specs=[row, vec, vec, vec], out_specs=row,
                          out_shape=jax.ShapeDtypeStruct((L, D), BF16),
                          compiler_params=_params(("parallel",)), name=name)(x, nw, sc, sh)


def _gate_outputs(dx, below_refs, dy_ref, gs_ref):
    g_ref, y_ref = below_refs
    dy_ref[...] = (dx * g_ref[...]).astype(BF16)
    gs_ref[0:1, :] += jnp.sum(dx * y_ref[...].astype(F32), axis=0, keepdims=True)


def _modnorm_bwd(x, dh, dxo, nw, sc, gsum, below, *, name):
    L = x.shape[0]
    tm = min(L, 256)
    nb = 0 if below is None else 2

    def body(x_ref, dh_ref, dxo_ref, nw_ref, sc_ref, g_ref, *rest):
        dx_ref, s_ref = rest[nb:nb + 2]

        @pl.when(pl.program_id(0) == 0)
        def _():
            s_ref[...] = g_ref[...]
            if nb:
                rest[-1][...] = jnp.zeros_like(rest[-1])

        xv, dhv = x_ref[...], dh_ref[...]
        r = lax.rsqrt(jnp.mean(xv * xv, axis=-1, keepdims=True) + EPS)
        xhat = xv * r
        dxhat = dhv * (nw_ref[...] * (1.0 + sc_ref[...]))
        dx = dxo_ref[...] + r * (dxhat - xhat * jnp.mean(dxhat * xhat, axis=-1, keepdims=True))
        dx_ref[...] = dx
        s_ref[1:2, :] += jnp.sum(dhv * xhat, axis=0, keepdims=True) * (1.0 + sc_ref[...])
        s_ref[2:3, :] += jnp.sum(dhv * xhat, axis=0, keepdims=True) * nw_ref[...]
        s_ref[3:4, :] += jnp.sum(dhv, axis=0, keepdims=True)
        if nb:
            _gate_outputs(dx, rest[:nb], rest[-2], rest[-1])

    row = pl.BlockSpec((tm, D), lambda i: (i, 0))
    vec = pl.BlockSpec((1, D), lambda i: (0, 0))
    blk = pl.BlockSpec((8, D), lambda i: (0, 0))
    in_specs, out_specs = [row, row, row, vec, vec, blk], [row, blk]
    out_shape = [jax.ShapeDtypeStruct((L, D), F32), jax.ShapeDtypeStruct((8, D), F32)]
    if nb:
        in_specs, out_specs = in_specs + [vec, row], out_specs + [row, blk]
        out_shape += [jax.ShapeDtypeStruct((L, D), BF16), jax.ShapeDtypeStruct((8, D), F32)]
    return pl.pallas_call(body, grid=(L // tm,), in_specs=in_specs, out_specs=out_specs, out_shape=out_shape,
                          compiler_params=_params(("arbitrary",)), name=name)(x, dh, dxo, nw, sc, gsum, *(below or ()))


def _final_loss(x, fw, tgt, below, *, name):
    L = x.shape[0]
    tm = min(L, 256)

    def body(x_ref, fw_ref, t_ref, g_ref, y_ref, dx_ref, s_ref, dy_ref, gs_ref):
        @pl.when(pl.program_id(0) == 0)
        def _():
            s_ref[...] = jnp.zeros_like(s_ref)
            gs_ref[...] = jnp.zeros_like(gs_ref)

        xv = x_ref[...]
        r = lax.rsqrt(jnp.mean(xv * xv, axis=-1, keepdims=True) + EPS)
        xhat = xv * r
        diff = xhat * fw_ref[...] - t_ref[...]
        dout = diff * (1.0 / D)
        dxhat = dout * fw_ref[...]
        dx = r * (dxhat - xhat * jnp.mean(dxhat * xhat, axis=-1, keepdims=True))
        dx_ref[...] = dx
        s_ref[0:1, :] += jnp.sum(dout * xhat, axis=0, keepdims=True)
        s_ref[1:2, :] += jnp.zeros((1, D), F32) + 0.5 * jnp.sum(jnp.sum(diff * diff, axis=-1, keepdims=True) * (1.0 / D))
        _gate_outputs(dx, (g_ref, y_ref), dy_ref, gs_ref)

    row = pl.BlockSpec((tm, D), lambda i: (i, 0))
    vec = pl.BlockSpec((1, D), lambda i: (0, 0))
    blk = pl.BlockSpec((8, D), lambda i: (0, 0))
    return pl.pallas_call(body, grid=(L // tm,), in_specs=[row, vec, row, vec, row], out_specs=[row, blk, row, blk],
                          out_shape=[jax.ShapeDtypeStruct((L, D), F32), jax.ShapeDtypeStruct((8, D), F32),
                                     jax.ShapeDtypeStruct((L, D), BF16), jax.ShapeDtypeStruct((8, D), F32)],
                          compiler_params=_params(("arbitrary",)), name=name)(x, fw, tgt, *below)


def _shift_down(v, j):
    if j == 0:
        return v
    rolled = pltpu.roll(v, j, 0)
    row = lax.broadcasted_iota(jnp.int32, (8, v.shape[1]), 0)
    return jnp.concatenate([jnp.where(row >= j, rolled[0:8], 0.0), rolled[8:]], axis=0)


def _shift_up(v, j):
    if j == 0:
        return v
    n = v.shape[0]
    rolled = pltpu.roll(v, n - j, 0)
    row = lax.broadcasted_iota(jnp.int32, (8, v.shape[1]), 0)
    return jnp.concatenate([rolled[:n - 8], jnp.where(row < 8 - j, rolled[n - 8:], 0.0)], axis=0)


def _ssd_conv_fwd(zx, w, b, *, name):
    L = zx.shape[0]
    cb = 256
    k = w.shape[0]

    def body(x_ref, w_ref, b_ref, o_ref):
        xv = x_ref[...].astype(F32)
        pre = b_ref[...] + xv * w_ref[k - 1:k, :]
        for j in range(1, k):
            pre = pre + _shift_down(xv, j) * w_ref[k - 1 - j:k - j, :]
        o_ref[...] = (pre * _sigmoid(pre)).astype(BF16)

    return pl.pallas_call(
        body, grid=(CONVD // cb,),
        in_specs=[pl.BlockSpec((L, cb), lambda i: (0, i + DI // cb)), pl.BlockSpec((k, cb), lambda i: (0, i)),
                  pl.BlockSpec((1, cb), lambda i: (0, i))],
        out_specs=pl.BlockSpec((L, cb), lambda i: (0, i)), out_shape=jax.ShapeDtypeStruct((L, CONVD), BF16),
        compiler_params=_params(("parallel",)), name=name)(zx, w, b)


def _ssd_conv_bwd(zx, dact, w, b, dzx, *, name):
    L = zx.shape[0]
    cb = 256
    k = w.shape[0]

    def body(x_ref, da_ref, w_ref, b_ref, _, dx_ref, s_ref):
        xv = x_ref[...].astype(F32)
        sh = [_shift_down(xv, j) for j in range(k)]
        pre = b_ref[...] + sh[0] * w_ref[k - 1:k, :]
        for j in range(1, k):
            pre = pre + sh[j] * w_ref[k - 1 - j:k - j, :]
        s = _sigmoid(pre)
        dpre = da_ref[...].astype(F32) * (s * (1.0 + pre * (1.0 - s)))
        dx = dpre * w_ref[k - 1:k, :]
        for j in range(1, k):
            dx = dx + _shift_up(dpre, j) * w_ref[k - 1 - j:k - j, :]
        dx_ref[...] = dx.astype(BF16)
        s_ref[...] = jnp.zeros_like(s_ref)
        for j in range(k):
            s_ref[k - 1 - j:k - j, :] = jnp.sum(dpre * sh[j], axis=0, keepdims=True)
        s_ref[k:k + 1, :] = jnp.sum(dpre, axis=0, keepdims=True)

    return pl.pallas_call(
        body, grid=(CONVD // cb,),
        in_specs=[pl.BlockSpec((L, cb), lambda i: (0, i + DI // cb)), pl.BlockSpec((L, cb), lambda i: (0, i)),
                  pl.BlockSpec((k, cb), lambda i: (0, i)), pl.BlockSpec((1, cb), lambda i: (0, i)), ANY],
        out_specs=[pl.BlockSpec((L, cb), lambda i: (0, i + DI // cb)), pl.BlockSpec((8, cb), lambda i: (0, i))],
        out_shape=[jax.ShapeDtypeStruct((L, ZX), BF16), jax.ShapeDtypeStruct((8, CONVD), F32)],
        input_output_aliases={4: 0}, compiler_params=_params(("parallel",)), name=name)(zx, dact, w, b, dzx)


def _sc_fwd(proj, w, *, name):
    L = proj.shape[0]
    cb = 256
    nb = D // cb
    k = w.shape[0]

    def body(b_ref, c_ref, x_ref, w_ref, o_ref):
        u = c_ref[...].astype(F32) * x_ref[...].astype(F32)
        v = u * w_ref[k - 1:k, :]
        for j in range(1, k):
            v = v + _shift_down(u, j) * w_ref[k - 1 - j:k - j, :]
        o_ref[...] = (b_ref[...].astype(F32) * v).astype(BF16)

    return pl.pallas_call(
        body, grid=(nb,),
        in_specs=[pl.BlockSpec((L, cb), lambda i: (0, i)), pl.BlockSpec((L, cb), lambda i: (0, i + nb)),
                  pl.BlockSpec((L, cb), lambda i: (0, i + 2 * nb)), pl.BlockSpec((k, cb), lambda i: (0, i))],
        out_specs=pl.BlockSpec((L, cb), lambda i: (0, i)), out_shape=jax.ShapeDtypeStruct((L, D), BF16),
        compiler_params=_params(("parallel",)), name=name)(proj, proj, proj, w)


def _sc_bwd(proj, dyv, w, *, name):
    L = proj.shape[0]
    cb = 256
    nb = D // cb
    k = w.shape[0]

    def body(b_ref, c_ref, x_ref, dy_ref, w_ref, dp_ref, s_ref):
        cv, xv = c_ref[...].astype(F32), x_ref[...].astype(F32)
        u = cv * xv
        sh = [_shift_down(u, j) for j in range(k)]
        v = sh[0] * w_ref[k - 1:k, :]
        for j in range(1, k):
            v = v + sh[j] * w_ref[k - 1 - j:k - j, :]
        dyv_ = dy_ref[...]
        dp_ref[0] = (dyv_ * v).astype(BF16)
        dv = dyv_ * b_ref[...].astype(F32)
        du = dv * w_ref[k - 1:k, :]
        for j in range(1, k):
            du = du + _shift_up(dv, j) * w_ref[k - 1 - j:k - j, :]
        dp_ref[1] = (du * xv).astype(BF16)
        dp_ref[2] = (du * cv).astype(BF16)
        s_ref[...] = jnp.zeros_like(s_ref)
        for j in range(k):
            s_ref[k - 1 - j:k - j, :] = jnp.sum(dv * sh[j], axis=0, keepdims=True)

    blk = pl.BlockSpec((L, cb), lambda i: (0, i))
    return pl.pallas_call(
        body, grid=(nb,),
        in_specs=[blk, pl.BlockSpec((L, cb), lambda i: (0, i + nb)), pl.BlockSpec((L, cb), lambda i: (0, i + 2 * nb)),
                  blk, pl.BlockSpec((k, cb), lambda i: (0, i))],
        out_specs=[pl.BlockSpec((3, L, cb), lambda i: (0, 0, i)), pl.BlockSpec((8, cb), lambda i: (0, i))],
        out_shape=[jax.ShapeDtypeStruct((3, L, D), BF16), jax.ShapeDtypeStruct((8, D), F32)],
        compiler_params=_params(("parallel",)), name=name)(proj, proj, proj, dyv, w)


def _pieces(v, n):
    out, rest = [], v
    for _ in range(n):
        out.append(rest.astype(BF16))
        rest = rest - out[-1].astype(F32)
    return out


def _cumsum_rows(mask, v):
    m = mask.astype(BF16)
    return _dot(jnp.concatenate([m, m, m], axis=1), jnp.concatenate(_pieces(v, 3), axis=0))


def _ssd_chunk_terms(dtr, prm):
    lane = lax.broadcasted_iota(jnp.int32, (CH, LANES), 1)
    valid = lane < NH
    xdt = dtr + prm[0:1, :]
    dt = jnp.where(valid, jnp.maximum(xdt, 0.0) + jnp.log1p(jnp.exp(-jnp.abs(xdt))), 0.0)
    A = -jnp.exp(prm[1:2, :])
    ri = lax.broadcasted_iota(jnp.int32, (CH, CH), 0)
    ci = lax.broadcasted_iota(jnp.int32, (CH, CH), 1)
    cs = _cumsum_rows(ri >= ci, dt * A)
    last = cs[CH - 1:CH, :]
    spread = (lax.broadcasted_iota(jnp.int32, (2 * LANES, DI), 1) // HP
              == lax.broadcasted_iota(jnp.int32, (2 * LANES, DI), 0) % LANES).astype(BF16)
    gather = ((lax.broadcasted_iota(jnp.int32, (LANES, 2 * DI), 1) % DI) // HP
              == lax.broadcasted_iota(jnp.int32, (LANES, 2 * DI), 0)).astype(BF16)
    return dict(valid=valid, xdt=xdt, dt=dt, A=A, cs=cs, csT=cs.T, last=last, ri=ri, ci=ci, ex=(spread, gather))


def _expand(v, ex):
    if v.shape[0] == 1:
        return _expand(jnp.broadcast_to(v, (8, LANES)), ex)[0:1, :]
    return _dot(jnp.concatenate(_pieces(v, 2), axis=1), ex[0])


def _head_sum(v, ex):
    if v.shape[0] == 1:
        return _head_sum(jnp.broadcast_to(v, (8, DI)), ex)[0:1, :]
    return _dot_nt(jnp.concatenate(_pieces(v, 2), axis=1), ex[1])


def _ssd_fwd(xbc, dtr, prm, *, name):
    L = xbc.shape[0]
    nc = L // CH

    def body(xbc_ref, dtr_ref, prm_ref, y_ref, sp_ref, st_ref):
        @pl.when(pl.program_id(0) == 0)
        def _():
            st_ref[...] = jnp.zeros_like(st_ref)

        prm_v = prm_ref[...]
        t = _ssd_chunk_terms(dtr_ref[...], prm_v)
        cs, csT, ex, causal = t["cs"], t["csT"], t["ex"], t["ri"] >= t["ci"]
        xs = xbc_ref[:, 0:DI].astype(F32)
        X = xs * _expand(t["dt"], ex)
        Xb = X.astype(BF16)
        Xd = (X * _expand(jnp.exp(t["last"] - cs), ex)).astype(BF16)
        Ex = _expand(jnp.exp(cs), ex)
        cdx = _expand(jnp.exp(t["last"]), ex)
        dskx = _expand(prm_v[2:3, :], ex)
        lane = lax.broadcasted_iota(jnp.int32, (CH, LANES), 1)
        sp_ref[0] = st_ref[...]
        for g in range(NG):
            Bg = xbc_ref[:, DI + g * NS:DI + (g + 1) * NS].astype(BF16)
            Cg = xbc_ref[:, DI + GW + g * NS:DI + GW + (g + 1) * NS].astype(BF16)
            G = _dot_nt(Cg, Bg)
            Sg = st_ref[:, g * GW:(g + 1) * GW]
            yoff = _dot(Cg, Sg.astype(BF16)) * Ex[:, g * GW:(g + 1) * GW]
            for j in range(GW // LANES):
                lo = g * GW + j * LANES
                Xp = Xb[:, lo:lo + LANES]
                yd = []
                for h in (lo // HP, lo // HP + 1):
                    seg = cs[:, h:h + 1] - csT[h:h + 1, :]
                    yd.append(_dot((G * jnp.where(causal, jnp.exp(seg), 0.0)).astype(BF16), Xp))
                y_ref[:, lo:lo + LANES] = (jnp.where(lane < HP, yd[0], yd[1]) + yoff[:, j * LANES:(j + 1) * LANES]
                                           + dskx[:, lo:lo + LANES] * xs[:, lo:lo + LANES]).astype(BF16)
            st_ref[:, g * GW:(g + 1) * GW] = Sg * cdx[:, g * GW:(g + 1) * GW] + _dot_tn(Bg, Xd[:, g * GW:(g + 1) * GW])

    return pl.pallas_call(
        body, grid=(nc,),
        in_specs=[pl.BlockSpec((CH, CONVD), lambda c: (c, 0)), pl.BlockSpec((CH, LANES), lambda c: (c, 0)),
                  pl.BlockSpec((8, LANES), lambda c: (0, 0))],
        out_specs=[pl.BlockSpec((CH, DI), lambda c: (c, 0)), pl.BlockSpec((1, NS, DI), lambda c: (c, 0, 0))],
        out_shape=[jax.ShapeDtypeStruct((L, DI), BF16), jax.ShapeDtypeStruct((nc, NS, DI), F32)],
        scratch_shapes=[pltpu.VMEM((NS, DI), F32)],
        compiler_params=_params(("arbitrary",)), name=name)(xbc, dtr, prm)


def _ssd_bwd(xbc, dtr, prm, dy, sprev, *, name):
    L = xbc.shape[0]
    nc = L // CH

    def body(xbc_ref, dtr_ref, prm_ref, dy_ref, sp_ref, dxbc_ref, ddtr_ref, s_ref, dst_ref, dx_scr, de_scr, dd_scr):
        step = pl.program_id(0)

        @pl.when(step == 0)
        def _():
            dst_ref[...] = jnp.zeros_like(dst_ref)
            s_ref[...] = jnp.zeros_like(s_ref)

        prm_v = prm_ref[...]
        t = _ssd_chunk_terms(dtr_ref[...], prm_v)
        cs, csT, ex, ri, ci = t["cs"], t["csT"], t["ex"], t["ri"], t["ci"]
        E = jnp.exp(cs)
        dec = jnp.exp(t["last"] - cs)
        cd = jnp.exp(t["last"])
        xs = xbc_ref[:, 0:DI].astype(F32)
        dtx = _expand(t["dt"], ex)
        X = xs * dtx
        Xb = X.astype(BF16)
        decx = _expand(dec, ex)
        Xd = (X * decx).astype(BF16)
        Ex = _expand(E, ex)
        cdx = _expand(cd, ex)
        dskx = _expand(prm_v[2:3, :], ex)
        lane = lax.broadcasted_iota(jnp.int32, (CH, LANES), 1)
        dcs = jnp.zeros((CH, LANES), F32)
        dcd_x = []
        for g in range(NG):
            gs = slice(g * GW, (g + 1) * GW)
            Bg = xbc_ref[:, DI + g * NS:DI + (g + 1) * NS].astype(BF16)
            Cg = xbc_ref[:, DI + GW + g * NS:DI + GW + (g + 1) * NS].astype(BF16)
            G = _dot_nt(Cg, Bg)
            GT = _dot_nt(Bg, Cg)
            Sg = sp_ref[0, :, gs]
            Sgb = Sg.astype(BF16)
            dyg = dy_ref[:, gs]
            de_scr[:, gs] = dyg * _dot(Cg, Sgb)
            dYo = (Ex[:, gs] * dyg).astype(BF16)
            dC = _dot_nt(dYo, Sgb)
            dS_in = _dot_tn(Cg, dYo)
            dStg = dst_ref[:, gs]
            dStb = dStg.astype(BF16)
            dXd = _dot(Bg, dStb)
            dB = _dot_nt(Xd[:, gs], dStb)
            dd_scr[:, gs] = dXd * X[:, gs]
            dXst = dXd * decx[:, gs]
            dG = jnp.zeros((CH, CH), F32)
            dGT = jnp.zeros((CH, CH), F32)
            for j in range(GW // LANES):
                lo = g * GW + j * LANES
                Xp = Xb[:, lo:lo + LANES]
                dyp = dy_ref[:, lo:lo + LANES]
                dXp = dXst[:, j * LANES:(j + 1) * LANES]
                for k, h in enumerate((lo // HP, lo // HP + 1)):
                    dyh = jnp.where((lane < HP) if k == 0 else (lane >= HP), dyp, 0.0).astype(BF16)
                    seg = cs[:, h:h + 1] - csT[h:h + 1, :]
                    Lm = jnp.where(ri >= ci, jnp.exp(seg), 0.0)
                    LmT = jnp.where(ci >= ri, jnp.exp(-seg), 0.0)
                    dM = _dot_nt(dyh, Xp)
                    dMT = _dot_nt(Xp, dyh)
                    MT = GT * LmT
                    rs = jnp.sum(dM * (G * Lm), axis=1, keepdims=True) - jnp.sum(dMT * MT, axis=1, keepdims=True)
                    dcs = dcs + jnp.where(lane == h, rs, 0.0)
                    dG = dG + dM * Lm
                    dGT = dGT + dMT * LmT
                    dXp = dXp + _dot(MT.astype(BF16), dyh)
                dx_scr[:, lo:lo + LANES] = dXp
            dxbc_ref[:, DI + g * NS:DI + (g + 1) * NS] = (dB + _dot(dGT.astype(BF16), Cg)).astype(BF16)
            dxbc_ref[:, DI + GW + g * NS:DI + GW + (g + 1) * NS] = (dC + _dot(dG.astype(BF16), Bg)).astype(BF16)
            dcd_x.append(jnp.sum(dStg * Sg, axis=0, keepdims=True))
            dst_ref[:, gs] = dStg * cdx[:, gs] + dS_in
        dX = dx_scr[...]
        dy = dy_ref[...]
        ddec = _head_sum(dd_scr[...], ex)
        dcd = _head_sum(jnp.concatenate(dcd_x, axis=1), ex)
        dcs = dcs + _head_sum(de_scr[...], ex) * E - ddec * dec
        row = lax.broadcasted_iota(jnp.int32, (CH, LANES), 0)
        dcs = dcs + jnp.where(row == CH - 1, jnp.sum(ddec * dec, axis=0, keepdims=True) + dcd * cd, 0.0)
        da = _cumsum_rows(ci >= ri, dcs)
        ddt = da * t["A"] + _head_sum(dX * xs, ex)
        ddtr = jnp.where(t["valid"], ddt * _sigmoid(t["xdt"]), 0.0)
        ddtr_ref[...] = ddtr
        dxbc_ref[:, 0:DI] = (dX * dtx + dskx * dy).astype(BF16)
        s_ref[0:1, :] += jnp.sum(da * t["dt"], axis=0, keepdims=True)
        s_ref[1:2, :] += _head_sum(jnp.sum(dy * xs, axis=0, keepdims=True), ex)
        s_ref[2:3, :] += jnp.sum(ddtr, axis=0, keepdims=True)

        @pl.when(step == nc - 1)
        def _():
            s_ref[0:1, :] = s_ref[0:1, :] * t["A"]

    rev = lambda c: (nc - 1 - c, 0)
    return pl.pallas_call(
        body, grid=(nc,),
        in_specs=[pl.BlockSpec((CH, CONVD), rev), pl.BlockSpec((CH, LANES), rev), pl.BlockSpec((8, LANES), lambda c: (0, 0)),
                  pl.BlockSpec((CH, DI), rev), pl.BlockSpec((1, NS, DI), lambda c: (nc - 1 - c, 0, 0))],
        out_specs=[pl.BlockSpec((CH, CONVD), rev), pl.BlockSpec((CH, LANES), rev), pl.BlockSpec((8, LANES), lambda c: (0, 0))],
        out_shape=[jax.ShapeDtypeStruct((L, CONVD), BF16), jax.ShapeDtypeStruct((L, LANES), F32),
                   jax.ShapeDtypeStruct((8, LANES), F32)],
        scratch_shapes=[pltpu.VMEM((NS, DI), F32), pltpu.VMEM((CH, DI), F32), pltpu.VMEM((CH, DI), F32),
                        pltpu.VMEM((CH, DI), F32)],
        compiler_params=_params(("arbitrary",)), name=name)(xbc, dtr, prm, dy, sprev)


def _gnorm_fwd(y, zx, nw, *, name):
    L = y.shape[0]
    tm = min(L, 256)

    def body(y_ref, z_ref, nw_ref, o_ref):
        z = z_ref[...].astype(F32)
        yg = y_ref[...].astype(F32) * (z * _sigmoid(z))
        for g in range(NG):
            v = yg[:, g * GW:(g + 1) * GW]
            r = lax.rsqrt(jnp.mean(v * v, axis=-1, keepdims=True) + EPS)
            o_ref[:, g * GW:(g + 1) * GW] = (v * r * nw_ref[:, g * GW:(g + 1) * GW]).astype(BF16)

    row = pl.BlockSpec((tm, DI), lambda i: (i, 0))
    return pl.pallas_call(body, grid=(L // tm,), in_specs=[row, row, pl.BlockSpec((1, DI), lambda i: (0, 0))],
                          out_specs=row, out_shape=jax.ShapeDtypeStruct((L, DI), BF16),
                          compiler_params=_params(("parallel",)), name=name)(y, zx, nw)


def _gnorm_bwd(y, zx, nw, dyn, *, name):
    L = y.shape[0]
    tm = min(L, 256)

    def body(y_ref, z_ref, nw_ref, dyn_ref, dy_ref, dz_ref, s_ref):
        @pl.when(pl.program_id(0) == 0)
        def _():
            s_ref[...] = jnp.zeros_like(s_ref)

        z, yv = z_ref[...].astype(F32), y_ref[...].astype(F32)
        sz = _sigmoid(z)
        gate = z * sz
        dgate_dz = sz * (1.0 + z * (1.0 - sz))
        for g in range(NG):
            gs = slice(g * GW, (g + 1) * GW)
            v = yv[:, gs] * gate[:, gs]
            r = lax.rsqrt(jnp.mean(v * v, axis=-1, keepdims=True) + EPS)
            vhat = v * r
            dn = dyn_ref[:, gs].astype(F32)
            s_ref[0:1, gs] += jnp.sum(dn * vhat, axis=0, keepdims=True)
            dvhat = dn * nw_ref[:, gs]
            dv = r * (dvhat - vhat * jnp.mean(dvhat * vhat, axis=-1, keepdims=True))
            dy_ref[:, gs] = dv * gate[:, gs]
            dz_ref[:, gs] = (dv * yv[:, gs] * dgate_dz[:, gs]).astype(BF16)

    row = pl.BlockSpec((tm, DI), lambda i: (i, 0))
    return pl.pallas_call(body, grid=(L // tm,), in_specs=[row, row, pl.BlockSpec((1, DI), lambda i: (0, 0)), row],
                          out_specs=[row, row, pl.BlockSpec((8, DI), lambda i: (0, 0))],
                          out_shape=[jax.ShapeDtypeStruct((L, DI), F32), jax.ShapeDtypeStruct((L, ZX), BF16),
                                     jax.ShapeDtypeStruct((8, DI), F32)],
                          compiler_params=_params(("arbitrary",)), name=name)(y, zx, nw, dyn)


def _adamw(w, g, m, v, *, name, g_row=0, w_row=0, rows=None, into=None, emit_g=False):
    lead = w.ndim == 3
    R, C = w.shape[-2:]
    rows = R if rows is None else rows
    tr = max([t for t in range(8, rows + 1, 8) if rows % t == 0 and t * C <= 256 * 1024], default=rows)
    assert g_row % tr == 0 and w_row % tr == 0, (name, g_row, w_row, tr)
    n_out = 4 if emit_g else 3

    def body(w_ref, g_ref, m_ref, v_ref, *rest):
        outs = rest[-n_out:]
        gv = g_ref[...]
        mn = ADAM_B1 * m_ref[...] + (1.0 - ADAM_B1) * gv
        vn = ADAM_B2 * v_ref[...] + (1.0 - ADAM_B2) * (gv * gv)
        m_hat = mn / (1.0 - ADAM_B1 ** ADAM_STEP)
        v_hat = vn / (1.0 - ADAM_B2 ** ADAM_STEP)
        d_ref, mo_ref, vo_ref = outs[-3:]
        d_ref[...] = -ADAM_LR * (m_hat / (jnp.sqrt(v_hat) + ADAM_EPS) + ADAM_WD * w_ref[...])
        mo_ref[...] = mn
        vo_ref[...] = vn
        if emit_g:
            outs[0][...] = gv

    blk = (pl.BlockSpec((None, tr, C), lambda i: (0, i + w_row // tr, 0)) if lead
           else pl.BlockSpec((tr, C), lambda i: (i + w_row // tr, 0)))
    args, in_specs, alias = [w, g, m, v], [blk, pl.BlockSpec((tr, C), lambda i: (i + g_row // tr, 0)), blk, blk], {}
    if into is not None:
        args, in_specs, alias = args + list(into), in_specs + [ANY] * n_out, {4 + k: k for k in range(n_out)}
    return pl.pallas_call(body, grid=(rows // tr,), in_specs=in_specs, out_specs=[blk] * n_out,
                          out_shape=[jax.ShapeDtypeStruct(w.shape, F32)] * n_out, input_output_aliases=alias,
                          compiler_params=_params(("parallel",)), name=name)(*args)


def _residual(acc, xv, gv):
    return xv + gv * acc, acc


def _like(buf):
    return jax.ShapeDtypeStruct(buf.shape, buf.dtype)


def _mlp_fwd(x, mod, nw, wb, up_row, down_row, tag):
    sh, sc, g = mod
    h = _modnorm_fwd(x, nw, sc, sh, name=tag + "_norm")
    a = _matmul(h, wb, n=DFF, tm=TM_ALL, b_spec=pl.BlockSpec((None, D, 512), lambda mi, j: (j // 2, up_row // D, j % 2)),
                epi=lambda acc: (jnp.maximum(acc, 0.0),), out_dtypes=(BF16,), name=tag + "_up")
    xn, y = _matmul(a, wb, n=D, tm=TM_HALF, contract=_nn_split_sq,
                    b_spec=pl.BlockSpec((N_CHIPS, D, 512), lambda mi, j: (0, down_row // D, j)),
                    extras=(x, g), epi=_residual, out_dtypes=(F32, BF16), name=tag + "_down")
    return xn, (x, h, a, y)


def _mlp_bwd(dxo, dy, gsum, saved, mod, nw, wb, gb, up_row, down_row, below, tag):
    x, h, a, y = saved
    sh, sc, g = mod
    du = _matmul(dy, wb, n=DFF, tm=TM_ALL, contract=_nt,
                 b_spec=pl.BlockSpec((None, 512, D), lambda mi, j: (j // 2, down_row // 512 + j % 2, 0)),
                 extras=(a,), epi=lambda acc, av: (acc * (2.0 * av.astype(F32)),), out_dtypes=(BF16,), name=tag + "_dact")
    gb = _matmul_tn(a, dy, m=DFF, n=D, tm=D, tn=D, a_square=True, into=gb, out_struct=_like(wb),
                    out_spec=pl.BlockSpec((None, D, D), lambda mi, j: (mi, down_row // D, 0)), name=tag + "_ddown")
    dh = _matmul(du, wb, n=D, tm=TM_HALF, contract=_nt_split,
                 b_spec=pl.BlockSpec((N_CHIPS, 512, D), lambda mi, j: (0, up_row // 512 + j, 0)), name=tag + "_dh")
    gb = _matmul_tn(h, du, m=D, n=DFF, tm=D, into=gb, out_struct=_like(wb),
                    out_spec=pl.BlockSpec((None, D, 512), lambda mi, j: (j // 2, up_row // D, j % 2)), name=tag + "_dup")
    dx, sums, *nxt = _modnorm_bwd(x, dh, dxo, nw, sc, gsum, below, name=tag + "_dnorm")
    return dx, gb, sums, *nxt


def _ssd_fwd_scan(x, mod, nw, w_zx, w_dt, conv_w, conv_b, prm, tag):
    sh, sc, g = mod
    h = _modnorm_fwd(x, nw, sc, sh, name=tag + "_norm")
    zx = _matmul(h, w_zx, n=ZX, tm=TM_ALL, out_dtypes=(BF16,), name=tag + "_in")
    dtr = _matmul(h, w_dt, n=LANES, tm=TM_ALL, name=tag + "_in_dt")
    xbc = _ssd_conv_fwd(zx, conv_w, conv_b, name=tag + "_conv")
    y, sprev = _ssd_fwd(xbc, dtr, prm, name=tag + "_scan")
    return h, zx, dtr, xbc, y, sprev


def _ssd_fwd_out(x, mod, scan, gn_w, w_out, tag):
    sh, sc, g = mod
    h, zx, dtr, xbc, y, sprev = scan
    yn = _gnorm_fwd(y, zx, gn_w, name=tag + "_gnorm")
    xn, yo = _matmul(yn, w_out, n=D, tm=TM_HALF, contract=_nn_split,
                     b_spec=pl.BlockSpec((N_CHIPS, 512, 512), lambda mi, j: (0, 0, j)),
                     extras=(x, g), epi=_residual, out_dtypes=(F32, BF16), name=tag + "_out")
    return xn, (x, h, zx, dtr, xbc, y, sprev, yn, yo)


def _ssd_bwd_out(dyo, saved, w_out, tag):
    x, h, zx, dtr, xbc, y, sprev, yn, yo = saved
    dyn = _matmul(dyo, w_out, n=DI, tm=TM_ALL, contract=_nt, b_spec=pl.BlockSpec((None, 512, D), lambda mi, j: (j, 0, 0)),
                  out_dtypes=(BF16,), name=tag + "_dyn")
    g_out = _matmul_tn(yn, dyo, m=DI, n=D, tn=D, out_struct=_like(w_out),
                       out_spec=pl.BlockSpec((None, 512, D), lambda mi, j: (mi, 0, 0)), name=tag + "_dout")
    return dyn, g_out


def _ssd_bwd_rest(dxo, dy, dzx, gsum, saved, mod, nw, w_zx, w_dt, conv_w, conv_b, prm, tag):
    x, h, zx, dtr, xbc, y, sprev, yn, yo = saved
    sh, sc, g = mod
    dxbc, ddtr, ssum = _ssd_bwd(xbc, dtr, prm, dy, sprev, name=tag + "_dscan")
    dzx, csum = _ssd_conv_bwd(zx, dxbc, conv_w, conv_b, dzx, name=tag + "_dconv")
    dh_dt = _matmul(ddtr, w_dt, n=D, tm=TM_ALL, contract=_nt, name=tag + "_dh_dt")
    dh = _matmul(dzx, w_zx, n=D, tm=TM_HALF, contract=_nt, extras=(dh_dt,), epi=lambda acc, e: (acc + e,), name=tag + "_dh")
    d_w_zx = _matmul_tn(h, dzx, m=D, n=ZX, tm=D, name=tag + "_din")
    d_w_dt = _matmul_tn(h, ddtr, m=D, n=LANES, tm=D, name=tag + "_din_dt")
    dx, sums = _modnorm_bwd(x, dh, dxo, nw, sc, gsum, None, name=tag + "_dnorm")
    return dx, d_w_zx, d_w_dt, sums, csum, ssum


def _sc_layer_fwd(x, mod, nw, w_sc_in, conv_w, wb, out_row, tag):
    sh, sc, g = mod
    h = _modnorm_fwd(x, nw, sc, sh, name=tag + "_norm")
    proj = _matmul(h, w_sc_in, n=3 * D, tm=TM_ALL, tn=256, out_dtypes=(BF16,),
                   b_spec=pl.BlockSpec((None, D, 256), lambda mi, j: (j // 3, 0, j % 3)),
                   name=tag + "_in")
    yv = _sc_fwd(proj, conv_w, name=tag + "_conv")
    xn, yo = _matmul(yv, wb, n=D, tm=TM_HALF, contract=_nn_split,
                     b_spec=pl.BlockSpec((N_CHIPS, 256, 512), lambda mi, j: (0, out_row // 256, j)),
                     extras=(x, g), epi=_residual, out_dtypes=(F32, BF16), name=tag + "_out")
    return xn, (x, h, proj, yv, yo)


def _sc_layer_bwd(dxo, dyo, gsum, saved, mod, nw, w_sc_in, conv_w, wb, gb, out_row, below, tag):
    x, h, proj, yv, yo = saved
    sh, sc, g = mod
    L = x.shape[0]
    dyv = _matmul(dyo, wb, n=D, tm=TM_ALL, tn=256, contract=_nt,
                  b_spec=pl.BlockSpec((None, 256, D), lambda mi, j: (j, out_row // 256, 0)), name=tag + "_dyv")
    gb = _matmul_tn(yv, dyo, m=D, n=D, tm=256, tn=D, into=gb, out_struct=_like(wb),
                    out_spec=pl.BlockSpec((None, 256, D), lambda mi, j: (mi, out_row // 256, 0)), name=tag + "_dout")
    dproj, csum = _sc_bwd(proj, dyv, conv_w, name=tag + "_dconv")
    tm = min(L, TM_HALF)
    dh = _matmul(dproj, w_sc_in, n=D, tm=tm, contract=_nt_sc_in, a_spec=pl.BlockSpec((3, tm, D), lambda mi, j: (0, mi, 0)),
                 b_spec=pl.BlockSpec((N_CHIPS, 512, SC_IN_SHARD), lambda mi, j: (0, j, 0)), name=tag + "_dh")
    g_sc_in = _matmul_tn(h, dproj, m=D, n=3 * D, tm=D, tn=256, b_spec=pl.BlockSpec((None, L, 256), lambda mi, j: (j // 4, 0, j % 4)),
                         out_spec=pl.BlockSpec((None, D, 256), lambda mi, j: (j // 3, 0, j % 3)),
                         out_struct=jax.ShapeDtypeStruct((N_CHIPS, D, SC_IN_SHARD), BF16), name=tag + "_din")
    dx, sums, *nxt = _modnorm_bwd(x, dh, dxo, nw, sc, gsum, below, name=tag + "_dnorm")
    return dx, gb, g_sc_in, sums, csum, *nxt


SUB_ROW = (0, 8, 16, 24)
SSD_CONV_ROW, GNORM_ROW, FINAL_ROW, SC_CONV_ROW, HEAD_ROW, SMALL_ROWS = 32, 56, 72, 80, 88, 96


def _all_gather_rows(blk, *, name):
    m_per, n = blk.shape

    def body(x_ref, out_ref, send_sems, recv_sems, local_sem):
        x, y, c = lax.axis_index("x"), lax.axis_index("y"), lax.axis_index("c")
        me, sibling = (x, y, c), (x, y, 1 - c)
        chips = [(1 - x, y), (x, 1 - y), (1 - x, 1 - y)]

        def rows(px, py, pc):
            return out_ref.at[pl.ds((4 * px + 2 * py + pc) * m_per, m_per), :]

        def copy(k, block, to, src=None):
            return pltpu.make_async_remote_copy(src_ref=rows(*block) if src is None else src, dst_ref=rows(*block),
                                                send_sem=send_sems.at[k], recv_sem=recv_sems.at[k], device_id=to,
                                                device_id_type=MESH)

        mine = pltpu.make_async_copy(x_ref, rows(*me), local_sem)
        mine.start()
        first = [copy(0, me, sibling, src=x_ref)] + [copy(1 + j, me, (*chip, c), src=x_ref) for j, chip in enumerate(chips)]
        for cp in first:
            cp.start()
        passed = [copy(4 + j, (*chip, c), sibling) for j, chip in enumerate(chips)]
        for j, chip in enumerate(chips):
            copy(1 + j, (*chip, c), me).wait_recv()
            passed[j].start()
        copy(0, sibling, me).wait_recv()
        for j, chip in enumerate(chips):
            copy(4 + j, (*chip, 1 - c), me).wait_recv()
        for cp in first + passed:
            cp.wait_send()
        mine.wait()

    return pl.pallas_call(
        body, out_shape=jax.ShapeDtypeStruct((N_DEV * m_per, n), blk.dtype),
        in_specs=[pl.BlockSpec(memory_space=pltpu.VMEM)], out_specs=pl.BlockSpec(memory_space=pltpu.VMEM),
        scratch_shapes=[pltpu.SemaphoreType.DMA((7,)), pltpu.SemaphoreType.DMA((7,)), pltpu.SemaphoreType.DMA],
        name=name)(blk)


def _half(ref, chip, c):
    hr = ref.shape[1] // 2
    return ref.at[chip, pl.ds(c * hr, hr), :]


def _gather_copy(bufs, sends, recvs, b, k, chip, pc, to):
    piece = _half(bufs[b], 2 * chip[0] + chip[1], pc)
    return pltpu.make_async_remote_copy(src_ref=piece, dst_ref=piece, send_sem=sends.at[4 * b + k], recv_sem=recvs.at[4 * b + k],
                                        device_id=to, device_id_type=MESH)


def _split_call(body, bufs, sems_in, n_sems, *, name, after=(), token=False, lands=()):
    nb, na, nl, starts = len(bufs), len(after), len(lands), not sems_in

    def wrapped(*refs):
        sems = refs[nb + na:nb + na + 2] if starts else refs[nb:nb + 2]
        made = refs[nb + na + 2 + nb:nb + na + 2 + nb + nl] if starts else ()
        body(tuple(refs[:nb]) + tuple(made), sems[0], sems[1])
        if token:
            refs[-1][...] = jnp.zeros_like(refs[-1])

    out_shape = [pltpu.SemaphoreType.DMA((n_sems,)) for _ in range(2 if starts else 0)]
    out_specs = [SEM] * len(out_shape) + [ANY] * (nb + nl)
    alias = {b: len(out_shape) + b for b in range(nb)}
    out_shape += [jax.ShapeDtypeStruct(b.shape, b.dtype) for b in bufs] + list(lands)
    if token:
        out_shape.append(jax.ShapeDtypeStruct((8, LANES), F32))
        out_specs.append(pl.BlockSpec(memory_space=pltpu.VMEM))
    return pl.pallas_call(
        wrapped, out_shape=out_shape, in_specs=[ANY] * nb + [SEM] * len(sems_in) + [ANY] * na, out_specs=out_specs,
        input_output_aliases=alias,
        compiler_params=pltpu.CompilerParams(has_side_effects=pltpu.SideEffectType.DATAFLOW_SIDE_EFFECTING),
        name=name)(*bufs, *sems_in, *after)


def _gather_start(bufs, *, name, after=()):
    nb = len(bufs)

    def body(ins, sends, recvs):
        x, y, c = lax.axis_index("x"), lax.axis_index("y"), lax.axis_index("c")
        chips = [(1 - x, y), (x, 1 - y), (1 - x, 1 - y)]
        for b in range(nb):
            _gather_copy(ins, sends, recvs, b, 0, (x, y), c, (x, y, 1 - c)).start()
            for j, chip in enumerate(chips):
                _gather_copy(ins, sends, recvs, b, 1 + j, (x, y), c, (*chip, c)).start()

    out = _split_call(body, bufs, (), 4 * nb, name=name, after=after, token=True)
    return (out[0], out[1], out[2:2 + nb]), out[-1]


def _gather_wait_first(flight, *, name, after=()):
    sends, recvs, bufs = flight
    nb = len(bufs)

    def body(ins, sends_, recvs_):
        x, y, c = lax.axis_index("x"), lax.axis_index("y"), lax.axis_index("c")
        chips = [(1 - x, y), (x, 1 - y), (1 - x, 1 - y)]
        for b in range(nb):
            _gather_copy(ins, sends_, recvs_, b, 0, (x, y), c, (x, y, 1 - c)).wait_send()
            _gather_copy(ins, sends_, recvs_, b, 0, (x, y), 1 - c, (x, y, c)).wait_recv()
            for j, chip in enumerate(chips):
                _gather_copy(ins, sends_, recvs_, b, 1 + j, (x, y), c, (*chip, c)).wait_send()
                _gather_copy(ins, sends_, recvs_, b, 1 + j, chip, c, (x, y, c)).wait_recv()

    return _split_call(body, bufs, (sends, recvs), 4 * nb, name=name, after=after)


def _gather_forward(bufs, *, name):
    nb = len(bufs)

    def body(ins, sends, recvs):
        x, y, c = lax.axis_index("x"), lax.axis_index("y"), lax.axis_index("c")
        chips = [(1 - x, y), (x, 1 - y), (1 - x, 1 - y)]
        for b in range(nb):
            for j, chip in enumerate(chips):
                _gather_copy(ins, sends, recvs, b, 1 + j, chip, c, (x, y, 1 - c)).start()

    out = _split_call(body, bufs, (), 4 * nb, name=name)
    return out[0], out[1], out[2:2 + nb]


def _gather_wait_forward(flight, *, name, after=()):
    sends, recvs, bufs = flight
    nb = len(bufs)

    def body(ins, sends_, recvs_):
        x, y, c = lax.axis_index("x"), lax.axis_index("y"), lax.axis_index("c")
        chips = [(1 - x, y), (x, 1 - y), (1 - x, 1 - y)]
        for b in range(nb):
            for j, chip in enumerate(chips):
                _gather_copy(ins, sends_, recvs_, b, 1 + j, chip, c, (x, y, 1 - c)).wait_send()
                _gather_copy(ins, sends_, recvs_, b, 1 + j, chip, 1 - c, (x, y, c)).wait_recv()

    return _split_call(body, bufs, (sends, recvs), 4 * nb, name=name, after=after)


def _owner_copies(hs, lands, sends, recvs):
    x, y, c = lax.axis_index("x"), lax.axis_index("y"), lax.axis_index("c")
    chips = [(1 - x, y), (x, 1 - y), (1 - x, 1 - y)]
    return [pltpu.make_async_remote_copy(src_ref=hs[b].at[2 * cx + cy], dst_ref=lands[b].at[j], send_sem=sends.at[3 * b + j],
                                         recv_sem=recvs.at[3 * b + j], device_id=(cx, cy, c), device_id_type=MESH)
            for b in range(len(hs)) for j, (cx, cy) in enumerate(chips)]


def _owners_start(hs, *, name):
    nb = len(hs)
    lands = [jax.ShapeDtypeStruct((3,) + h.shape[1:], h.dtype) for h in hs]

    def body(refs, sends, recvs):
        for cp in _owner_copies(refs[:nb], refs[nb:], sends, recvs):
            cp.start()

    out = _split_call(body, list(hs), (), 3 * nb, name=name, token=True, lands=lands)
    return (out[0], out[1], out[2:2 + 2 * nb]), out[-1]


def _owners_wait(flight, *, name, after=()):
    sends, recvs, bufs = flight
    nb = len(bufs) // 2

    def body(refs, sends_, recvs_):
        for cp in _owner_copies(refs[:nb], refs[nb:], sends_, recvs_):
            cp.wait()

    out = _split_call(body, bufs, (sends, recvs), 3 * nb, name=name, after=after)
    return out[:nb], out[nb:]


def _sibling_copies(gs, lands, sends, recvs):
    x, y, c = lax.axis_index("x"), lax.axis_index("y"), lax.axis_index("c")
    copies = []
    for b in range(len(gs)):
        hr = gs[b].shape[1] // 2
        copies.append(pltpu.make_async_remote_copy(
            src_ref=gs[b].at[:, pl.ds((1 - c) * hr, hr), :], dst_ref=lands[b], send_sem=sends.at[b], recv_sem=recvs.at[b],
            device_id=(x, y, 1 - c), device_id_type=MESH))
    return copies


def _sibling_start(gs, *, name, after=()):
    nb = len(gs)
    lands = [jax.ShapeDtypeStruct((g.shape[0], g.shape[1] // 2, g.shape[2]), g.dtype) for g in gs]

    def body(refs, sends, recvs):
        for cp in _sibling_copies(refs[:nb], refs[nb:], sends, recvs):
            cp.start()

    out = _split_call(body, list(gs), (), nb, name=name, after=after, token=True, lands=lands)
    return (out[0], out[1], out[2:2 + 2 * nb]), out[-1]


def _sibling_wait(flight, *, name, after=()):
    sends, recvs, bufs = flight
    nb = len(bufs) // 2

    def body(refs, sends_, recvs_):
        for cp in _sibling_copies(refs[:nb], refs[nb:], sends_, recvs_):
            cp.wait()

    out = _split_call(body, bufs, (sends, recvs), nb, name=name, after=after)
    return out[:nb], out[nb:]


def _result_copies(ts, sends, recvs):
    x, y, c = lax.axis_index("x"), lax.axis_index("y"), lax.axis_index("c")
    return [pltpu.make_async_remote_copy(src_ref=ts[b].at[c], dst_ref=ts[b].at[c], send_sem=sends.at[b], recv_sem=recvs.at[b],
                                         device_id=(x, y, 1 - c), device_id_type=MESH) for b in range(len(ts))]


def _result_start(ts, *, name):
    def body(refs, sends, recvs):
        for cp in _result_copies(refs, sends, recvs):
            cp.start()

    out = _split_call(body, ts, (), len(ts), name=name, token=True)
    return (out[0], out[1], out[2:2 + len(ts)]), out[-1]


def _result_wait(flight, *, name, after=()):
    sends, recvs, bufs = flight

    def body(refs, sends_, recvs_):
        for cp in _result_copies(refs, sends_, recvs_):
            cp.wait()

    return _split_call(body, bufs, (sends, recvs), len(bufs), name=name, after=after)


def _row_tile(rows, cols):
    best = 16
    for t in range(16, rows + 1, 16):
        if rows % t == 0 and t * cols <= 640 * 1024:
            best = t
    assert rows % best == 0, (rows, cols)
    return best


def _add_sibling_half(g, recv, core, *, name):
    nk, r, n = g.shape
    hr = r // 2
    tr = _row_tile(hr, n)

    def body(c_ref, a_ref, b_ref, o_ref):
        o_ref[...] = (a_ref[...].astype(F32) + b_ref[...].astype(F32)).astype(BF16)

    grid_spec = pltpu.PrefetchScalarGridSpec(
        num_scalar_prefetch=1, grid=(nk, hr // tr),
        in_specs=[pl.BlockSpec((None, tr, n), lambda k, i, c_ref: (k, c_ref[0] * (hr // tr) + i, 0)),
                  pl.BlockSpec((None, tr, n), lambda k, i, c_ref: (k, i, 0))],
        out_specs=pl.BlockSpec((None, tr, n), lambda k, i, c_ref: (k, i, 0)))
    return pl.pallas_call(body, grid_spec=grid_spec, out_shape=jax.ShapeDtypeStruct((nk, hr, n), BF16),
                          compiler_params=_params(("parallel", "parallel")), name=name)(core, g, recv)


def _add_chip_sums(h, recv, chip_core, *, name):
    _, hr, n = h.shape
    tr = _row_tile(hr, n)

    def body(k_ref, a_ref, b_ref, o_ref):
        o_ref[...] = ((a_ref[...].astype(F32) + b_ref[0].astype(F32)) + b_ref[1].astype(F32)) + b_ref[2].astype(F32)

    grid_spec = pltpu.PrefetchScalarGridSpec(
        num_scalar_prefetch=1, grid=(hr // tr,),
        in_specs=[pl.BlockSpec((None, tr, n), lambda i, k_ref: (k_ref[0], i, 0)),
                  pl.BlockSpec((3, tr, n), lambda i, k_ref: (0, i, 0))],
        out_specs=pl.BlockSpec((None, tr, n), lambda i, k_ref: (k_ref[1], i, 0)))
    return pl.pallas_call(body, grid_spec=grid_spec, out_shape=jax.ShapeDtypeStruct((2, hr, n), F32),
                          compiler_params=_params(("parallel",)), name=name)(chip_core, h, recv)


def _sum_devices(g, *, name):
    nd, r, n = g.shape

    def body(g_ref, o_ref):
        acc = g_ref[0]
        for i in range(1, nd):
            acc = acc + g_ref[i]
        o_ref[...] = acc

    return pl.pallas_call(body, out_shape=jax.ShapeDtypeStruct((r, n), F32), name=name)(g)


def _own_slot(parts, chip, *, name):
    rows, cols = sum(w.shape[1] for w, _ in parts), parts[0][0].shape[2]
    buf, row0 = None, 0
    for p, (w, idx) in enumerate(parts):
        r = w.shape[1]
        tr = min(r, 256)
        assert r % tr == 0 and row0 % tr == 0, (name, r, row0)

        def body(chip_ref, w_ref, *rest):
            rest[-1][...] = w_ref[...].astype(BF16)

        grid_spec = pltpu.PrefetchScalarGridSpec(
            num_scalar_prefetch=1, grid=(r // tr,),
            in_specs=[pl.BlockSpec((None, tr, cols), lambda i, c_ref, idx=idx: (idx, i, 0))] + ([] if buf is None else [ANY]),
            out_specs=pl.BlockSpec((None, tr, cols), lambda i, c_ref, row0=row0, tr=tr: (c_ref[0], row0 // tr + i, 0)))
        buf = pl.pallas_call(body, grid_spec=grid_spec, out_shape=jax.ShapeDtypeStruct((N_CHIPS, rows, cols), BF16),
                             input_output_aliases={} if buf is None else {2: 0}, compiler_params=_params(("parallel",)),
                             name=f"{name}{p}")(chip, w, *(() if buf is None else (buf,)))
        row0 += r
    return buf


def kernel(x, c, ada_w, ada_b, mix_norm_w, mlp_norm_w, mlp_up, mlp_down, ssd_in_w, ssd_conv_w, ssd_conv_b, ssd_dt_bias, ssd_A_log, ssd_D, ssd_norm_w, ssd_out_w, sc_in_w, sc_conv_w, sc_out_w, final_norm_w, loss_target, m_ada_w, m_ada_b, m_mix_norm_w, m_mlp_norm_w, m_mlp_up, m_mlp_down, m_ssd_in_w, m_ssd_conv_w, m_ssd_conv_b, m_ssd_dt_bias, m_ssd_A_log, m_ssd_D, m_ssd_norm_w, m_ssd_out_w, m_sc_in_w, m_sc_conv_w, m_sc_out_w, m_final_norm_w, v_ada_w, v_ada_b, v_mix_norm_w, v_mlp_norm_w, v_mlp_up, v_mlp_down, v_ssd_in_w, v_ssd_conv_w, v_ssd_conv_b, v_ssd_dt_bias, v_ssd_A_log, v_ssd_D, v_ssd_norm_w, v_ssd_out_w, v_sc_in_w, v_sc_conv_w, v_sc_out_w, v_final_norm_w):
    xi, yi, ci = lax.axis_index("x"), lax.axis_index("y"), lax.axis_index("c")
    chip = 2 * xi + yi
    dev = 2 * chip + ci
    n_ada = ada_w.shape[2]

    conv_flat = jnp.concatenate([ssd_conv_w.reshape(-1), sc_conv_w.reshape(-1), jnp.zeros((256,), F32)]).reshape(4, D)
    blk0 = jnp.concatenate([c, conv_flat, jnp.zeros((3, D), F32)], axis=0)
    got0 = _all_gather_rows(blk0, name="gather_cond").reshape(N_DEV, 8, D)
    c_all = got0[:, 0]
    conv_all = got0[0::2, 1:5].reshape(N_CHIPS, 4 * D)
    ssd_conv = jnp.moveaxis(conv_all[:, :4 * 768].reshape(N_CHIPS, 4, 768), 0, 1).reshape(4, CONVD)
    sc_conv = jnp.moveaxis(conv_all[:, 4 * 768:4 * 768 + 3 * 256].reshape(N_CHIPS, 3, 256), 0, 1).reshape(3, D)
    mod_shard = [_matmul(c_all, ada_w, n=n_ada, a_silu=True, b_spec=pl.BlockSpec((None, D, 512), lambda mi, j, i=i: (i, 0, j)),
                         extras=(lax.dynamic_slice(ada_b, (i, chip * n_ada), (1, n_ada)),),
                         epi=lambda acc, b: (acc + b,), name=f"ada_mod{i}") for i in range(2)]
    mod_all = _all_gather_rows(jnp.concatenate(mod_shard, axis=0), name="gather_mod")
    mod_all = mod_all.reshape(N_DEV, 2, N_DEV, n_ada)[0::2]
    mod = jnp.moveaxis(lax.dynamic_index_in_dim(mod_all, dev, axis=2, keepdims=False), 0, 1).reshape(2, 6, D)
    mods = [[mod[i, j:j + 1] for j in range(6)] for i in range(2)]

    up_row, down_row, sc_out_row = 0, D, 2 * D
    chip1 = chip.reshape(1).astype(jnp.int32)
    a_bufs = [_own_slot([(ssd_in_w, 0)], chip1, name="slot_ssd_in")]
    b_bufs = [_own_slot([(ssd_out_w, 0)], chip1, name="slot_ssd_out"),
              _own_slot([(mlp_up, 0), (mlp_down, 0)], chip1, name="slot_mlp0_")]
    c_bufs = [_own_slot([(sc_in_w, 0)], chip1, name="slot_sc_in"),
              _own_slot([(mlp_up, 1), (mlp_down, 1), (sc_out_w, 0)], chip1, name="slot_layer1_")]
    fly_a, tok = _gather_start(a_bufs, name="gather_a_start", after=(mod,))
    fly_b, tok = _gather_start(b_bufs, name="gather_b_start", after=(tok,))
    fly_c, tok = _gather_start(c_bufs, name="gather_c_start", after=(tok,))

    row = lambda v: v.reshape(1, -1)
    xs, tgt = x[0], loss_target[0]
    prm = jnp.pad(jnp.concatenate([ssd_dt_bias, ssd_A_log, ssd_D, jnp.zeros((5, NH), F32)], axis=0), ((0, 0), (0, LANES - NH)))
    mix_nw = [row(mix_norm_w[i]) for i in range(2)]
    mlp_nw = [row(mlp_norm_w[i]) for i in range(2)]
    a_bufs = _gather_wait_first(fly_a, name="gather_a_landed", after=(tok,))
    (w_ssd_in,) = _gather_wait_forward(_gather_forward(a_bufs, name="gather_a_pass"), name="gather_a_done")
    ssd_in_full = jnp.moveaxis(w_ssd_in, 0, 1).reshape(D, N_CHIPS * SSD_IN_SHARD)
    w_zx, w_dt = ssd_in_full[:, :ZX], jnp.pad(ssd_in_full[:, ZX:], ((0, 0), (0, LANES - NH)))
    scan = _ssd_fwd_scan(xs, mods[0][0:3], mix_nw[0], w_zx, w_dt, ssd_conv, ssd_conv_b, prm, "ssd")
    fly_b = _gather_forward(_gather_wait_first(fly_b, name="gather_b_landed", after=(scan[3],)), name="gather_b_pass")
    w_ssd_out, w_b = _gather_wait_forward(fly_b, name="gather_b_done", after=(scan[4],))
    x1, s_ssd = _ssd_fwd_out(xs, mods[0][0:3], scan, ssd_norm_w, w_ssd_out, "ssd")
    x2, s_mlp0 = _mlp_fwd(x1, mods[0][3:6], mlp_nw[0], w_b, up_row, down_row, "mlp0")
    c_bufs = _gather_wait_first(fly_c, name="gather_c_landed", after=(x2,))
    w_sc_in, w_c = _gather_wait_forward(_gather_forward(c_bufs, name="gather_c_pass"), name="gather_c_done")
    x3, s_sc = _sc_layer_fwd(x2, mods[1][0:3], mix_nw[1], w_sc_in, sc_conv, w_c, sc_out_row, "sc")
    x4, s_mlp1 = _mlp_fwd(x3, mods[1][3:6], mlp_nw[1], w_c, up_row, down_row, "mlp1")

    core = ci.reshape(1).astype(jnp.int32)
    chip_core = jnp.stack([chip, ci]).astype(jnp.int32)

    def reduce_swap(gbufs, tag, after=()):
        return _sibling_start(gbufs, name=tag + "_sibling_start", after=after)

    def reduce_send(flight, tag, after):
        gs, sib = _sibling_wait(flight, name=tag + "_sibling_landed", after=after)
        hs = [_add_sibling_half(g, s, core, name=f"{tag}_add_sibling{b}") for b, (g, s) in enumerate(zip(gs, sib))]
        return _owners_start(hs, name=tag + "_owners_start")

    def reduce_sum(flight, tag, after):
        hs, lands = _owners_wait(flight, name=tag + "_owners_landed", after=after)
        ts = [_add_chip_sums(h, o, chip_core, name=f"{tag}_add_chips{b}") for b, (h, o) in enumerate(zip(hs, lands))]
        return _result_start(ts, name=tag + "_result_start")

    def reduce_done(flight, tag, after=()):
        return [t.reshape(-1, t.shape[2]) for t in _result_wait(flight, name=tag + "_result_landed", after=after)]

    dx4, fsum, dy, gs = _final_loss(x4, row(final_norm_w), tgt, (mods[1][5], s_mlp1[3]), name="final_loss")
    dx3, g_c, sum_mlp1, dy, gs = _mlp_bwd(dx4, dy, gs, s_mlp1, mods[1][3:6], mlp_nw[1], w_c, None, up_row, down_row,
                                          (mods[1][2], s_sc[4]), "mlp1")
    dx2, g_c, g_sc_in, sum_sc, sc_csum, dy, gs = _sc_layer_bwd(dx3, dy, gs, s_sc, mods[1][0:3], mix_nw[1], w_sc_in, sc_conv,
                                                               w_c, g_c, sc_out_row, (mods[0][5], s_mlp0[3]), "sc")
    dx1, g_b, sum_mlp0, dy, gsum_ssd = _mlp_bwd(dx2, dy, gs, s_mlp0, mods[0][3:6], mlp_nw[0], w_b, None, up_row, down_row,
                                                (mods[0][2], s_ssd[8]), "mlp0")
    dyn, g_ssd_out = _ssd_bwd_out(dy, s_ssd, w_ssd_out, "ssd")
    fly_1, tok = reduce_swap([g_c, g_sc_in, g_b, g_ssd_out], "rs1")
    dy, dzx, gnsum = _gnorm_bwd(s_ssd[5], s_ssd[2], ssd_norm_w + tok[0:1, 0:1], dyn, name="ssd_dgnorm")
    fly_1, tok = reduce_send(fly_1, "rs1", (dy,))
    grad_x, d_w_zx, d_w_dt, sum_ssd, csum, ssum = _ssd_bwd_rest(
        dx1, dy, dzx, gsum_ssd, s_ssd, mods[0][0:3], mix_nw[0], w_zx, w_dt, ssd_conv, ssd_conv_b, prm + tok[0:1, 0:1], "ssd")
    fly_1, tok = reduce_sum(fly_1, "rs1", (grad_x,))

    def ssd_in_owner(k):
        lo, hi = k * SSD_IN_SHARD, (k + 1) * SSD_IN_SHARD
        if hi <= ZX:
            return d_w_zx[:, lo:hi]
        return jnp.concatenate([d_w_zx[:, lo:], d_w_dt[:, :hi - ZX]], axis=1)

    small = jnp.concatenate([sum_ssd + tok[0:1, 0:1], sum_mlp0, sum_sc, sum_mlp1, csum.reshape(24, D), gnsum.reshape(16, D),
                             fsum, sc_csum, jnp.pad(ssum, ((0, 0), (0, D - LANES)))], axis=0)
    small_all = _all_gather_rows(small, name="gather_small").reshape(N_DEV, SMALL_ROWS, D)
    fly_2, tok = reduce_swap([jnp.stack([ssd_in_owner(k) for k in range(N_CHIPS)]).astype(BF16)], "rs2", (small_all,))
    fly_2, tok = reduce_send(fly_2, "rs2", (tok,))
    t_c, t_sc_in, t_b, t_ssd_out = reduce_done(fly_1, "rs1", (tok,))
    small_all = small_all + tok[0:1, 0:1]
    tot = _sum_devices(small_all, name="sum_small")
    loss = tot[FINAL_ROW + 1, 0]
    mod_rows = [r + o for r in SUB_ROW for o in (3, 2, 0)]
    g_ada_b = jnp.stack([tot[r] for r in mod_rows]).reshape(2, 6 * D)
    g_mix_norm = jnp.stack([tot[SUB_ROW[0] + 1], tot[SUB_ROW[2] + 1]])
    g_mlp_norm = jnp.stack([tot[SUB_ROW[1] + 1], tot[SUB_ROW[3] + 1]])
    conv_sums = tot[SSD_CONV_ROW:SSD_CONV_ROW + 24].reshape(8, CONVD)
    g_ssd_conv_w = lax.dynamic_slice(conv_sums, (0, chip * 768), (4, 768))[None]
    g_ssd_conv_b = conv_sums[4:5]
    g_ssd_norm = tot[GNORM_ROW:GNORM_ROW + 2].reshape(1, DI)
    g_final = tot[FINAL_ROW]
    g_sc_conv_w = lax.dynamic_slice(tot[SC_CONV_ROW:SC_CONV_ROW + 3], (0, chip * 256), (3, 256))[None]
    g_a_log, g_d, g_dt_bias = (tot[HEAD_ROW + r:HEAD_ROW + r + 1, 0:NH] for r in range(3))
    c_pad = jnp.concatenate([c_all, jnp.zeros((8, D), F32)], axis=0)
    dmod_all = jnp.stack([small_all[:, r] for r in mod_rows], axis=1).reshape(N_DEV, 2, 6 * D)
    g_ada_w = []
    for i in range(2):
        dm = lax.dynamic_slice(dmod_all[:, i], (0, chip * n_ada), (N_DEV, n_ada))
        g_ada_w.append(_matmul_tn(c_pad, jnp.concatenate([dm, jnp.zeros_like(dm)], axis=0), m=D, n=n_ada, a_silu=True,
                                  name=f"ada_dw{i}"))

    big = dict(ada_w=[(g, 0) for g in g_ada_w], mlp_up=[(t_b, up_row), (t_c, up_row)], mlp_down=[(t_b, down_row), (t_c, down_row)],
               ssd_out_w=[(t_ssd_out, 0)], sc_out_w=[(t_c, sc_out_row)], sc_in_w=[(t_sc_in, 0)], ssd_in_w=None)
    grads = dict(ada_b=g_ada_b, mix_norm_w=g_mix_norm, mlp_norm_w=g_mlp_norm, ssd_conv_w=g_ssd_conv_w,
                 ssd_conv_b=g_ssd_conv_b, ssd_dt_bias=g_dt_bias, ssd_A_log=g_a_log, ssd_D=g_d, ssd_norm_w=g_ssd_norm,
                 sc_conv_w=g_sc_conv_w, final_norm_w=g_final)
    weights = dict(ada_w=(ada_w, m_ada_w, v_ada_w), ada_b=(ada_b, m_ada_b, v_ada_b),
                   mix_norm_w=(mix_norm_w, m_mix_norm_w, v_mix_norm_w), mlp_norm_w=(mlp_norm_w, m_mlp_norm_w, v_mlp_norm_w),
                   mlp_up=(mlp_up, m_mlp_up, v_mlp_up), mlp_down=(mlp_down, m_mlp_down, v_mlp_down),
                   ssd_in_w=(ssd_in_w, m_ssd_in_w, v_ssd_in_w), ssd_conv_w=(ssd_conv_w, m_ssd_conv_w, v_ssd_conv_w),
                   ssd_conv_b=(ssd_conv_b, m_ssd_conv_b, v_ssd_conv_b), ssd_dt_bias=(ssd_dt_bias, m_ssd_dt_bias, v_ssd_dt_bias),
                   ssd_A_log=(ssd_A_log, m_ssd_A_log, v_ssd_A_log), ssd_D=(ssd_D, m_ssd_D, v_ssd_D),
                   ssd_norm_w=(ssd_norm_w, m_ssd_norm_w, v_ssd_norm_w), ssd_out_w=(ssd_out_w, m_ssd_out_w, v_ssd_out_w),
                   sc_in_w=(sc_in_w, m_sc_in_w, v_sc_in_w), sc_conv_w=(sc_conv_w, m_sc_conv_w, v_sc_conv_w),
                   sc_out_w=(sc_out_w, m_sc_out_w, v_sc_out_w), final_norm_w=(final_norm_w, m_final_norm_w, v_final_norm_w))
    def step(nm, parts):
        w, m, v = (t if t.shape[0] == 1 else t.reshape(-1, t.shape[-1]) for t in weights[nm])
        rows, outs = w.shape[-2] // len(parts), None
        for i, (gbuf, g_row) in enumerate(parts):
            outs = _adamw(w, gbuf, m, v, g_row=g_row, w_row=i * rows, rows=rows, into=outs, emit_g=True, name=f"adamw_{nm}{i}")
        return outs

    res = {}
    for nm, (w, m, v) in weights.items():
        two_d = (-1, w.shape[-1]) if w.ndim > 1 else (1, -1)
        if nm not in big:
            res[nm] = (grads[nm], *_adamw(w.reshape(two_d), grads[nm].reshape(two_d), m.reshape(two_d), v.reshape(two_d),
                                          name="adamw_" + nm))
        elif big[nm] is not None:
            res[nm] = step(nm, big[nm])
    fly_2, tok = reduce_sum(fly_2, "rs2", tuple(r[1] for r in res.values()))
    (t_ssd_in,) = reduce_done(fly_2, "rs2", (tok,))
    w_t, m_t, v_t = (jnp.swapaxes(t[0], 0, 1) for t in weights["ssd_in_w"])
    res["ssd_in_w"] = [jnp.swapaxes(o, 0, 1) for o in _adamw(w_t, t_ssd_in.T, m_t, v_t, emit_g=True, name="adamw_ssd_in_w")]
    outs = [[res[nm][k].reshape(weights[nm][0].shape) for nm in weights] for k in range(4)]
    return (loss, grad_x[None], *outs[0], *outs[1], *outs[2], *outs[3])
```

```python
import jax
import jax.numpy as jnp
from jax import lax
from jax.experimental import pallas as pl
from jax.experimental.pallas import tpu as pltpu

F32 = jnp.float32
BF16 = jnp.bfloat16
MESH = pl.DeviceIdType.MESH

D = 1024
DFF = 4096
DI = 2048
NH = 32
HP = 64
NG = 4
NS = 128
CH = 128
CONVD = DI + 2 * NG * NS
ZX = DI + CONVD
GW = NG * NS
LANES = 128
N_CHIPS = 4
N_DEV = 8
EPS = 1e-5
ADAM_LR, ADAM_B1, ADAM_B2, ADAM_EPS, ADAM_WD, ADAM_STEP = 1e-3, 0.9, 0.999, 1e-8, 0.01, 10
VMEM_LIMIT = 48 * 1024 * 1024
TM_ALL = 2048
TM_HALF = 1024
ANY = pl.BlockSpec(memory_space=pl.ANY)
SEM = pl.BlockSpec(memory_space=pltpu.SEMAPHORE)

SSD_IN_SHARD = 1288
SC_IN_SHARD = 768


def _params(sem=None):
    return pltpu.CompilerParams(dimension_semantics=sem, vmem_limit_bytes=VMEM_LIMIT)


def _sigmoid(v):
    return 0.5 * jnp.tanh(0.5 * v) + 0.5


def _dot(a, b, dims=((1,), (0,)), precision=None):
    return lax.dot_general(a, b, (dims, ((), ())), preferred_element_type=F32, precision=precision)


def _dot_nt(a, b):
    return _dot(a, b, ((1,), (1,)))


def _dot_tn(a, b):
    return _dot(a, b, ((0,), (0,)))


def _nn(av, bv):
    return _dot(av.astype(BF16), bv.astype(BF16))


def _nt(av, bv):
    return _dot_nt(av.astype(BF16), bv.astype(BF16))


def _nn_split(av, bv):
    return _dot(av.astype(BF16), bv.reshape(-1, bv.shape[2]))


def _nn_split_sq(av, bv):
    af = av.astype(F32)
    return _nn_split(af * af, bv)


def _nt_split(av, bv):
    kc = bv.shape[2]
    acc = _dot_nt(av[:, 0:kc].astype(BF16), bv[0])
    for s in range(1, bv.shape[0]):
        acc = acc + _dot_nt(av[:, s * kc:(s + 1) * kc].astype(BF16), bv[s])
    return acc


def _nt_sc_in(av, bv):
    q = 256
    acc = None
    for i in range(3 * D // q):
        a_blk = av[i // 4][:, (i % 4) * q:(i % 4 + 1) * q]
        b_blk = bv[i // 3][:, (i % 3) * q:(i % 3 + 1) * q]
        t = _dot_nt(a_blk, b_blk)
        acc = t if acc is None else acc + t
    return acc


def _matmul(a, b, *, name, n, contract=_nn, a_spec=None, b_spec=None, tm=512, tn=512, extras=(), epi=None,
            out_dtypes=(F32,), a_silu=False):
    M = a.shape[-2]
    tm, tn = min(tm, M), min(tn, n)
    assert M % tm == 0 and n % tn == 0, (name, M, n, tm, tn)
    n_ex = len(extras)
    if a_spec is None:
        a_spec = pl.BlockSpec((tm, a.shape[1]), lambda i, j: (i, 0))
    if b_spec is None:
        b_spec = (pl.BlockSpec((tn, b.shape[1]), lambda i, j: (j, 0)) if contract is _nt
                  else pl.BlockSpec((b.shape[0], tn), lambda i, j: (0, j)))

    def body(*refs):
        av = refs[0][...]
        if a_silu:
            av = av * _sigmoid(av)
        acc = contract(av, refs[1][...])
        res = epi(acc, *[r[...] for r in refs[2:2 + n_ex]]) if epi is not None else (acc,)
        for o_ref, r in zip(refs[2 + n_ex:], res, strict=True):
            o_ref[...] = r.astype(o_ref.dtype)

    in_specs = [a_spec, b_spec]
    for e in extras:
        in_specs.append(pl.BlockSpec((1, tn), lambda i, j: (0, j)) if e.shape[0] == 1 and M != 1
                        else pl.BlockSpec((tm, tn), lambda i, j: (i, j)))
    outs = pl.pallas_call(
        body, grid=(M // tm, n // tn), in_specs=in_specs,
        out_specs=[pl.BlockSpec((tm, tn), lambda i, j: (i, j)) for _ in out_dtypes],
        out_shape=[jax.ShapeDtypeStruct((M, n), dt) for dt in out_dtypes],
        compiler_params=_params(("parallel", "parallel")), name=name)(a, b, *extras)
    return outs if len(out_dtypes) > 1 else outs[0]


def _matmul_tn(a, b, *, name, m, n, tm=512, tn=512, a_spec=None, b_spec=None, out_spec=None, out_struct=None, into=None,
               a_silu=False, a_square=False):
    T = a.shape[-2]
    tm, tn = min(tm, m), min(tn, n)
    assert m % tm == 0 and n % tn == 0, (name, m, n, tm, tn)
    if a_spec is None:
        a_spec = pl.BlockSpec((T, tm), lambda i, j: (0, i))
    if b_spec is None:
        b_spec = pl.BlockSpec((T, tn), lambda i, j: (0, j))
    if out_spec is None:
        out_spec, out_struct = pl.BlockSpec((tm, tn), lambda i, j: (i, j)), jax.ShapeDtypeStruct((m, n), F32)

    def body(a_ref, b_ref, *rest):
        av = a_ref[...]
        if a_silu:
            av = av * _sigmoid(av)
        if a_square:
            av = av.astype(F32) * av.astype(F32)
        rest[-1][...] = _dot_tn(av.astype(BF16), b_ref[...].astype(BF16)).astype(rest[-1].dtype)

    args, in_specs, alias = [a, b], [a_spec, b_spec], {}
    if into is not None:
        args, in_specs, alias = args + [into], in_specs + [ANY], {2: 0}
    return pl.pallas_call(body, grid=(m // tm, n // tn), in_specs=in_specs, out_specs=out_spec, out_shape=out_struct,
                          input_output_aliases=alias, compiler_params=_params(("parallel", "parallel")), name=name)(*args)


def _modnorm_fwd(x, nw, sc, sh, *, name):
    L = x.shape[0]
    tm = min(L, 512)

    def body(x_ref, nw_ref, sc_ref, sh_ref, h_ref):
        xv = x_ref[...]
        r = lax.rsqrt(jnp.mean(xv * xv, axis=-1, keepdims=True) + EPS)
        h_ref[...] = ((xv * r * nw_ref[...]) * (1.0 + sc_ref[...]) + sh_ref[...]).astype(BF16)

    row = pl.BlockSpec((tm, D), lambda i: (i, 0))
    vec = pl.BlockSpec((1, D), lambda i: (0, 0))
    return pl.pallas_call(body, grid=(L // tm,), in_specs=[row, vec, vec, vec], out_specs=row,
                          out_shape=jax.ShapeDtypeStruct((L, D), BF16),
                          compiler_params=_params(("parallel",)), name=name)(x, nw, sc, sh)


def _gate_outputs(dx, below_refs, dy_ref, gs_ref):
    g_ref, y_ref = below_refs
    dy_ref[...] = (dx * g_ref[...]).astype(BF16)
    gs_ref[0:1, :] += jnp.sum(dx * y_ref[...].astype(F32), axis=0, keepdims=True)


def _modnorm_bwd(x, dh, dxo, nw, sc, gsum, below, *, name):
    L = x.shape[0]
    tm = min(L, 256)
    nb = 0 if below is None else 2

    def body(x_ref, dh_ref, dxo_ref, nw_ref, sc_ref, g_ref, *rest):
        dx_ref, s_ref = rest[nb:nb + 2]

        @pl.when(pl.program_id(0) == 0)
        def _():
            s_ref[...] = g_ref[...]
            if nb:
                rest[-1][...] = jnp.zeros_like(rest[-1])

        xv, dhv = x_ref[...], dh_ref[...]
        r = lax.rsqrt(jnp.mean(xv * xv, axis=-1, keepdims=True) + EPS)
        xhat = xv * r
        dxhat = dhv * (nw_ref[...] * (1.0 + sc_ref[...]))
        dx = dxo_ref[...] + r * (dxhat - xhat * jnp.mean(dxhat * xhat, axis=-1, keepdims=True))
        dx_ref[...] = dx
        s_ref[1:2, :] += jnp.sum(dhv * xhat, axis=0, keepdims=True) * (1.0 + sc_ref[...])
        s_ref[2:3, :] += jnp.sum(dhv * xhat, axis=0, keepdims=True) * nw_ref[...]
        s_ref[3:4, :] += jnp.sum(dhv, axis=0, keepdims=True)
        if nb:
            _gate_outputs(dx, rest[:nb], rest[-2], rest[-1])

    row = pl.BlockSpec((tm, D), lambda i: (i, 0))
    vec = pl.BlockSpec((1, D), lambda i: (0, 0))
    blk = pl.BlockSpec((8, D), lambda i: (0, 0))
    in_specs, out_specs = [row, row, row, vec, vec, blk], [row, blk]
    out_shape = [jax.ShapeDtypeStruct((L, D), F32), jax.ShapeDtypeStruct((8, D), F32)]
    if nb:
        in_specs, out_specs = in_specs + [vec, row], out_specs + [row, blk]
        out_shape += [jax.ShapeDtypeStruct((L, D), BF16), jax.ShapeDtypeStruct((8, D), F32)]
    return pl.pallas_call(body, grid=(L // tm,), in_specs=in_specs, out_specs=out_specs, out_shape=out_shape,
                          compiler_params=_params(("arbitrary",)), name=name)(x, dh, dxo, nw, sc, gsum, *(below or ()))


def _final_loss(x, fw, tgt, below, *, name):
    L = x.shape[0]
    tm = min(L, 256)

    def body(x_ref, fw_ref, t_ref, g_ref, y_ref, dx_ref, s_ref, dy_ref, gs_ref):
        @pl.when(pl.program_id(0) == 0)
        def _():
            s_ref[...] = jnp.zeros_like(s_ref)
            gs_ref[...] = jnp.zeros_like(gs_ref)

        xv = x_ref[...]
        r = lax.rsqrt(jnp.mean(xv * xv, axis=-1, keepdims=True) + EPS)
        xhat = xv * r
        diff = xhat * fw_ref[...] - t_ref[...]
        dout = diff * (1.0 / D)
        dxhat = dout * fw_ref[...]
        dx = r * (dxhat - xhat * jnp.mean(dxhat * xhat, axis=-1, keepdims=True))
        dx_ref[...] = dx
        s_ref[0:1, :] += jnp.sum(dout * xhat, axis=0, keepdims=True)
        s_ref[1:2, :] += jnp.zeros((1, D), F32) + 0.5 * jnp.sum(jnp.sum(diff * diff, axis=-1, keepdims=True) * (1.0 / D))
        _gate_outputs(dx, (g_ref, y_ref), dy_ref, gs_ref)

    row = pl.BlockSpec((tm, D), lambda i: (i, 0))
    vec = pl.BlockSpec((1, D), lambda i: (0, 0))
    blk = pl.BlockSpec((8, D), lambda i: (0, 0))
    return pl.pallas_call(body, grid=(L // tm,), in_specs=[row, vec, row, vec, row], out_specs=[row, blk, row, blk],
                          out_shape=[jax.ShapeDtypeStruct((L, D), F32), jax.ShapeDtypeStruct((8, D), F32),
                                     jax.ShapeDtypeStruct((L, D), BF16), jax.ShapeDtypeStruct((8, D), F32)],
                          compiler_params=_params(("arbitrary",)), name=name)(x, fw, tgt, *below)


def _shift_down(v, j):
    if j == 0:
        return v
    rolled = pltpu.roll(v, j, 0)
    row = lax.broadcasted_iota(jnp.int32, (8, v.shape[1]), 0)
    return jnp.concatenate([jnp.where(row >= j, rolled[0:8], 0.0), rolled[8:]], axis=0)


def _shift_up(v, j):
    if j == 0:
        return v
    n = v.shape[0]
    rolled = pltpu.roll(v, n - j, 0)
    row = lax.broadcasted_iota(jnp.int32, (8, v.shape[1]), 0)
    return jnp.concatenate([rolled[:n - 8], jnp.where(row < 8 - j, rolled[n - 8:], 0.0)], axis=0)


def _ssd_conv_fwd(zx, w, b, *, name):
    L = zx.shape[0]
    cb = 256
    k = w.shape[0]

    def body(x_ref, w_ref, b_ref, o_ref, p_ref):
        xv = x_ref[...].astype(F32)
        pre = b_ref[...] + xv * w_ref[k - 1:k, :]
        for j in range(1, k):
            pre = pre + _shift_down(xv, j) * w_ref[k - 1 - j:k - j, :]
        o_ref[...] = (pre * _sigmoid(pre)).astype(BF16)
        p_ref[...] = pre.astype(BF16)

    blk = pl.BlockSpec((L, cb), lambda i: (0, i))
    return pl.pallas_call(
        body, grid=(CONVD // cb,),
        in_specs=[pl.BlockSpec((L, cb), lambda i: (0, i + DI // cb)), pl.BlockSpec((k, cb), lambda i: (0, i)),
                  pl.BlockSpec((1, cb), lambda i: (0, i))],
        out_specs=[blk, blk], out_shape=[jax.ShapeDtypeStruct((L, CONVD), BF16)] * 2,
        compiler_params=_params(("parallel",)), name=name)(zx, w, b)


def _ssd_conv_bwd(zx, pre, dact, w, dzx, *, name):
    L = zx.shape[0]
    cb = 256
    k = w.shape[0]

    def body(x_ref, p_ref, da_ref, w_ref, _, dx_ref, s_ref):
        xv, pv = x_ref[...].astype(F32), p_ref[...].astype(F32)
        s = _sigmoid(pv)
        dpre = da_ref[...].astype(F32) * (s * (1.0 + pv * (1.0 - s)))
        s_ref[...] = jnp.zeros_like(s_ref)
        s_ref[k:k + 1, :] = jnp.sum(dpre, axis=0, keepdims=True)
        s_ref[k - 1:k, :] = jnp.sum(dpre * xv, axis=0, keepdims=True)
        dx = dpre * w_ref[k - 1:k, :]
        for j in range(1, k):
            later = _shift_up(dpre, j)
            dx = dx + later * w_ref[k - 1 - j:k - j, :]
            s_ref[k - 1 - j:k - j, :] = jnp.sum(later * xv, axis=0, keepdims=True)
        dx_ref[...] = dx.astype(BF16)

    blk = pl.BlockSpec((L, cb), lambda i: (0, i))
    return pl.pallas_call(
        body, grid=(CONVD // cb,),
        in_specs=[pl.BlockSpec((L, cb), lambda i: (0, i + DI // cb)), blk, blk, pl.BlockSpec((k, cb), lambda i: (0, i)), ANY],
        out_specs=[pl.BlockSpec((L, cb), lambda i: (0, i + DI // cb)), pl.BlockSpec((8, cb), lambda i: (0, i))],
        out_shape=[jax.ShapeDtypeStruct((L, ZX), BF16), jax.ShapeDtypeStruct((8, CONVD), F32)],
        input_output_aliases={4: 0}, compiler_params=_params(("parallel",)), name=name)(zx, pre, dact, w, dzx)


def _sc_fwd(proj, w, *, name):
    L = proj.shape[0]
    cb = 256
    nb = D // cb
    k = w.shape[0]

    def body(b_ref, c_ref, x_ref, w_ref, o_ref, v_ref):
        u = c_ref[...].astype(F32) * x_ref[...].astype(F32)
        v = u * w_ref[k - 1:k, :]
        for j in range(1, k):
            v = v + _shift_down(u, j) * w_ref[k - 1 - j:k - j, :]
        o_ref[...] = (b_ref[...].astype(F32) * v).astype(BF16)
        v_ref[...] = v.astype(BF16)

    blk = pl.BlockSpec((L, cb), lambda i: (0, i))
    return pl.pallas_call(
        body, grid=(nb,),
        in_specs=[blk, pl.BlockSpec((L, cb), lambda i: (0, i + nb)), pl.BlockSpec((L, cb), lambda i: (0, i + 2 * nb)),
                  pl.BlockSpec((k, cb), lambda i: (0, i))],
        out_specs=[blk, blk], out_shape=[jax.ShapeDtypeStruct((L, D), BF16)] * 2,
        compiler_params=_params(("parallel",)), name=name)(proj, proj, proj, w)


def _sc_bwd(proj, v, dyv, w, *, name):
    L = proj.shape[0]
    cb = 256
    nb = D // cb
    k = w.shape[0]

    def body(b_ref, c_ref, x_ref, v_ref, dy_ref, w_ref, dp_ref, s_ref):
        cv, xv = c_ref[...].astype(F32), x_ref[...].astype(F32)
        u = cv * xv
        dyv_ = dy_ref[...]
        dp_ref[0] = (dyv_ * v_ref[...].astype(F32)).astype(BF16)
        dv = dyv_ * b_ref[...].astype(F32)
        s_ref[...] = jnp.zeros_like(s_ref)
        s_ref[k - 1:k, :] = jnp.sum(dv * u, axis=0, keepdims=True)
        du = dv * w_ref[k - 1:k, :]
        for j in range(1, k):
            later = _shift_up(dv, j)
            du = du + later * w_ref[k - 1 - j:k - j, :]
            s_ref[k - 1 - j:k - j, :] = jnp.sum(later * u, axis=0, keepdims=True)
        dp_ref[1] = (du * xv).astype(BF16)
        dp_ref[2] = (du * cv).astype(BF16)

    blk = pl.BlockSpec((L, cb), lambda i: (0, i))
    return pl.pallas_call(
        body, grid=(nb,),
        in_specs=[blk, pl.BlockSpec((L, cb), lambda i: (0, i + nb)), pl.BlockSpec((L, cb), lambda i: (0, i + 2 * nb)),
                  blk, blk, pl.BlockSpec((k, cb), lambda i: (0, i))],
        out_specs=[pl.BlockSpec((3, L, cb), lambda i: (0, 0, i)), pl.BlockSpec((8, cb), lambda i: (0, i))],
        out_shape=[jax.ShapeDtypeStruct((3, L, D), BF16), jax.ShapeDtypeStruct((8, D), F32)],
        compiler_params=_params(("parallel",)), name=name)(proj, proj, proj, v, dyv, w)


def _pieces(v, n):
    out, rest = [], v
    for _ in range(n):
        out.append(rest.astype(BF16))
        rest = rest - out[-1].astype(F32)
    return out


def _cumsum_rows(mask, v):
    m = mask.astype(BF16)
    return _dot(jnp.concatenate([m, m, m], axis=1), jnp.concatenate(_pieces(v, 3), axis=0))


def _ssd_chunk_terms(dtr, prm):
    lane = lax.broadcasted_iota(jnp.int32, (CH, LANES), 1)
    valid = lane < NH
    xdt = dtr + prm[0:1, :]
    dt = jnp.where(valid, jnp.maximum(xdt, 0.0) + jnp.log1p(jnp.exp(-jnp.abs(xdt))), 0.0)
    A = -jnp.exp(prm[1:2, :])
    ri = lax.broadcasted_iota(jnp.int32, (CH, CH), 0)
    ci = lax.broadcasted_iota(jnp.int32, (CH, CH), 1)
    cs = _cumsum_rows(ri >= ci, dt * A)
    last = cs[CH - 1:CH, :]
    spread = (lax.broadcasted_iota(jnp.int32, (2 * LANES, DI), 1) // HP
              == lax.broadcasted_iota(jnp.int32, (2 * LANES, DI), 0) % LANES).astype(BF16)
    gather = ((lax.broadcasted_iota(jnp.int32, (LANES, 2 * DI), 1) % DI) // HP
              == lax.broadcasted_iota(jnp.int32, (LANES, 2 * DI), 0)).astype(BF16)
    return dict(valid=valid, xdt=xdt, dt=dt, A=A, cs=cs, csT=cs.T, last=last, ri=ri, ci=ci, ex=(spread, gather))


def _expand(v, ex):
    if v.shape[0] == 1:
        return _expand(jnp.broadcast_to(v, (8, LANES)), ex)[0:1, :]
    return _dot(jnp.concatenate(_pieces(v, 2), axis=1), ex[0])


def _head_sum(v, ex):
    if v.shape[0] == 1:
        return _head_sum(jnp.broadcast_to(v, (8, DI)), ex)[0:1, :]
    return _dot_nt(jnp.concatenate(_pieces(v, 2), axis=1), ex[1])


def _ssd_fwd(xbc, dtr, prm, *, name):
    L = xbc.shape[0]
    nc = L // CH

    def body(xbc_ref, dtr_ref, prm_ref, y_ref, sp_ref, st_ref):
        @pl.when(pl.program_id(0) == 0)
        def _():
            st_ref[...] = jnp.zeros_like(st_ref)

        prm_v = prm_ref[...]
        t = _ssd_chunk_terms(dtr_ref[...], prm_v)
        cs, csT, ex, causal = t["cs"], t["csT"], t["ex"], t["ri"] >= t["ci"]
        xs = xbc_ref[:, 0:DI].astype(F32)
        X = xs * _expand(t["dt"], ex)
        Xb = X.astype(BF16)
        Xd = (X * _expand(jnp.exp(t["last"] - cs), ex)).astype(BF16)
        Ex = _expand(jnp.exp(cs), ex)
        cdx = _expand(jnp.exp(t["last"]), ex)
        dskx = _expand(prm_v[2:3, :], ex)
        lane = lax.broadcasted_iota(jnp.int32, (CH, LANES), 1)
        sp_ref[0] = st_ref[...]
        for g in range(NG):
            Bg = xbc_ref[:, DI + g * NS:DI + (g + 1) * NS].astype(BF16)
            Cg = xbc_ref[:, DI + GW + g * NS:DI + GW + (g + 1) * NS].astype(BF16)
            G = _dot_nt(Cg, Bg)
            Sg = st_ref[:, g * GW:(g + 1) * GW]
            yoff = _dot(Cg, Sg.astype(BF16)) * Ex[:, g * GW:(g + 1) * GW]
            for j in range(GW // LANES):
                lo = g * GW + j * LANES
                Xp = Xb[:, lo:lo + LANES]
                yd = []
                for h in (lo // HP, lo // HP + 1):
                    seg = cs[:, h:h + 1] - csT[h:h + 1, :]
                    yd.append(_dot((G * jnp.where(causal, jnp.exp(seg), 0.0)).astype(BF16), Xp))
                y_ref[:, lo:lo + LANES] = (jnp.where(lane < HP, yd[0], yd[1]) + yoff[:, j * LANES:(j + 1) * LANES]
                                           + dskx[:, lo:lo + LANES] * xs[:, lo:lo + LANES]).astype(BF16)
            st_ref[:, g * GW:(g + 1) * GW] = Sg * cdx[:, g * GW:(g + 1) * GW] + _dot_tn(Bg, Xd[:, g * GW:(g + 1) * GW])

    return pl.pallas_call(
        body, grid=(nc,),
        in_specs=[pl.BlockSpec((CH, CONVD), lambda c: (c, 0)), pl.BlockSpec((CH, LANES), lambda c: (c, 0)),
                  pl.BlockSpec((8, LANES), lambda c: (0, 0))],
        out_specs=[pl.BlockSpec((CH, DI), lambda c: (c, 0)), pl.BlockSpec((1, NS, DI), lambda c: (c, 0, 0))],
        out_shape=[jax.ShapeDtypeStruct((L, DI), BF16), jax.ShapeDtypeStruct((nc, NS, DI), F32)],
        scratch_shapes=[pltpu.VMEM((NS, DI), F32)],
        compiler_params=_params(("arbitrary",)), name=name)(xbc, dtr, prm)


def _ssd_bwd(xbc, dtr, prm, dy, sprev, *, name):
    L = xbc.shape[0]
    nc = L // CH

    def body(xbc_ref, dtr_ref, prm_ref, dy_ref, sp_ref, dxbc_ref, ddtr_ref, s_ref, dst_ref, dx_scr, de_scr, dd_scr):
        step = pl.program_id(0)

        @pl.when(step == 0)
        def _():
            dst_ref[...] = jnp.zeros_like(dst_ref)
            s_ref[...] = jnp.zeros_like(s_ref)

        prm_v = prm_ref[...]
        t = _ssd_chunk_terms(dtr_ref[...], prm_v)
        cs, csT, ex, ri, ci = t["cs"], t["csT"], t["ex"], t["ri"], t["ci"]
        E = jnp.exp(cs)
        dec = jnp.exp(t["last"] - cs)
        cd = jnp.exp(t["last"])
        xs = xbc_ref[:, 0:DI].astype(F32)
        dtx = _expand(t["dt"], ex)
        X = xs * dtx
        Xb = X.astype(BF16)
        decx = _expand(dec, ex)
        Xd = (X * decx).astype(BF16)
        Ex = _expand(E, ex)
        cdx = _expand(cd, ex)
        dskx = _expand(prm_v[2:3, :], ex)
        lane = lax.broadcasted_iota(jnp.int32, (CH, LANES), 1)
        dcs = jnp.zeros((CH, LANES), F32)
        dcd_x = []
        for g in range(NG):
            gs = slice(g * GW, (g + 1) * GW)
            Bg = xbc_ref[:, DI + g * NS:DI + (g + 1) * NS].astype(BF16)
            Cg = xbc_ref[:, DI + GW + g * NS:DI + GW + (g + 1) * NS].astype(BF16)
            G = _dot_nt(Cg, Bg)
            GT = _dot_nt(Bg, Cg)
            Sg = sp_ref[0, :, gs]
            Sgb = Sg.astype(BF16)
            dyg = dy_ref[:, gs]
            de_scr[:, gs] = dyg * _dot(Cg, Sgb)
            dYo = (Ex[:, gs] * dyg).astype(BF16)
            dC = _dot_nt(dYo, Sgb)
            dS_in = _dot_tn(Cg, dYo)
            dStg = dst_ref[:, gs]
            dStb = dStg.astype(BF16)
            dXd = _dot(Bg, dStb)
            dB = _dot_nt(Xd[:, gs], dStb)
            dd_scr[:, gs] = dXd * X[:, gs]
            dXst = dXd * decx[:, gs]
            dG = jnp.zeros((CH, CH), F32)
            dGT = jnp.zeros((CH, CH), F32)
            for j in range(GW // LANES):
                lo = g * GW + j * LANES
                Xp = Xb[:, lo:lo + LANES]
                dyp = dy_ref[:, lo:lo + LANES]
                dXp = dXst[:, j * LANES:(j + 1) * LANES]
                for k, h in enumerate((lo // HP, lo // HP + 1)):
                    dyh = jnp.where((lane < HP) if k == 0 else (lane >= HP), dyp, 0.0).astype(BF16)
                    seg = cs[:, h:h + 1] - csT[h:h + 1, :]
                    Lm = jnp.where(ri >= ci, jnp.exp(seg), 0.0)
                    LmT = jnp.where(ci >= ri, jnp.exp(-seg), 0.0)
                    dM = _dot_nt(dyh, Xp)
                    dMT = _dot_nt(Xp, dyh)
                    MT = GT * LmT
                    rs = jnp.sum(dM * (G * Lm), axis=1, keepdims=True) - jnp.sum(dMT * MT, axis=1, keepdims=True)
                    dcs = dcs + jnp.where(lane == h, rs, 0.0)
                    dG = dG + dM * Lm
                    dGT = dGT + dMT * LmT
                    dXp = dXp + _dot(MT.astype(BF16), dyh)
                dx_scr[:, lo:lo + LANES] = dXp
            dxbc_ref[:, DI + g * NS:DI + (g + 1) * NS] = (dB + _dot(dGT.astype(BF16), Cg)).astype(BF16)
            dxbc_ref[:, DI + GW + g * NS:DI + GW + (g + 1) * NS] = (dC + _dot(dG.astype(BF16), Bg)).astype(BF16)
            dcd_x.append(jnp.sum(dStg * Sg, axis=0, keepdims=True))
            dst_ref[:, gs] = dStg * cdx[:, gs] + dS_in
        dX = dx_scr[...]
        dy = dy_ref[...]
        ddec = _head_sum(dd_scr[...], ex)
        dcd = _head_sum(jnp.concatenate(dcd_x, axis=1), ex)
        dcs = dcs + _head_sum(de_scr[...], ex) * E - ddec * dec
        row = lax.broadcasted_iota(jnp.int32, (CH, LANES), 0)
        dcs = dcs + jnp.where(row == CH - 1, jnp.sum(ddec * dec, axis=0, keepdims=True) + dcd * cd, 0.0)
        da = _cumsum_rows(ci >= ri, dcs)
        ddt = da * t["A"] + _head_sum(dX * xs, ex)
        ddtr = jnp.where(t["valid"], ddt * _sigmoid(t["xdt"]), 0.0)
        ddtr_ref[...] = ddtr
        dxbc_ref[:, 0:DI] = (dX * dtx + dskx * dy).astype(BF16)
        s_ref[0:1, :] += jnp.sum(da * t["dt"], axis=0, keepdims=True)
        s_ref[1:2, :] += _head_sum(jnp.sum(dy * xs, axis=0, keepdims=True), ex)
        s_ref[2:3, :] += jnp.sum(ddtr, axis=0, keepdims=True)

        @pl.when(step == nc - 1)
        def _():
            s_ref[0:1, :] = s_ref[0:1, :] * t["A"]

    rev = lambda c: (nc - 1 - c, 0)
    return pl.pallas_call(
        body, grid=(nc,),
        in_specs=[pl.BlockSpec((CH, CONVD), rev), pl.BlockSpec((CH, LANES), rev), pl.BlockSpec((8, LANES), lambda c: (0, 0)),
                  pl.BlockSpec((CH, DI), rev), pl.BlockSpec((1, NS, DI), lambda c: (nc - 1 - c, 0, 0))],
        out_specs=[pl.BlockSpec((CH, CONVD), rev), pl.BlockSpec((CH, LANES), rev), pl.BlockSpec((8, LANES), lambda c: (0, 0))],
        out_shape=[jax.ShapeDtypeStruct((L, CONVD), BF16), jax.ShapeDtypeStruct((L, LANES), F32),
                   jax.ShapeDtypeStruct((8, LANES), F32)],
        scratch_shapes=[pltpu.VMEM((NS, DI), F32), pltpu.VMEM((CH, DI), F32), pltpu.VMEM((CH, DI), F32),
                        pltpu.VMEM((CH, DI), F32)],
        compiler_params=_params(("arbitrary",)), name=name)(xbc, dtr, prm, dy, sprev)


def _gnorm_fwd(y, zx, nw, *, name):
    L = y.shape[0]
    tm = min(L, 256)

    def body(y_ref, z_ref, nw_ref, o_ref):
        z = z_ref[...].astype(F32)
        yg = y_ref[...].astype(F32) * (z * _sigmoid(z))
        for g in range(NG):
            v = yg[:, g * GW:(g + 1) * GW]
            r = lax.rsqrt(jnp.mean(v * v, axis=-1, keepdims=True) + EPS)
            o_ref[:, g * GW:(g + 1) * GW] = (v * r * nw_ref[:, g * GW:(g + 1) * GW]).astype(BF16)

    row = pl.BlockSpec((tm, DI), lambda i: (i, 0))
    return pl.pallas_call(body, grid=(L // tm,), in_specs=[row, row, pl.BlockSpec((1, DI), lambda i: (0, 0))],
                          out_specs=row, out_shape=jax.ShapeDtypeStruct((L, DI), BF16),
                          compiler_params=_params(("parallel",)), name=name)(y, zx, nw)


def _gnorm_bwd(y, zx, nw, dyn, *, name):
    L = y.shape[0]
    tm = min(L, 256)

    def body(y_ref, z_ref, nw_ref, dyn_ref, dy_ref, dz_ref, s_ref):
        @pl.when(pl.program_id(0) == 0)
        def _():
            s_ref[...] = jnp.zeros_like(s_ref)

        z, yv = z_ref[...].astype(F32), y_ref[...].astype(F32)
        sz = _sigmoid(z)
        gate = z * sz
        dgate_dz = sz * (1.0 + z * (1.0 - sz))
        for g in range(NG):
            gs = slice(g * GW, (g + 1) * GW)
            v = yv[:, gs] * gate[:, gs]
            r = lax.rsqrt(jnp.mean(v * v, axis=-1, keepdims=True) + EPS)
            vhat = v * r
            dn = dyn_ref[:, gs].astype(F32)
            s_ref[0:1, gs] += jnp.sum(dn * vhat, axis=0, keepdims=True)
            dvhat = dn * nw_ref[:, gs]
            dv = r * (dvhat - vhat * jnp.mean(dvhat * vhat, axis=-1, keepdims=True))
            dy_ref[:, gs] = dv * gate[:, gs]
            dz_ref[:, gs] = (dv * yv[:, gs] * dgate_dz[:, gs]).astype(BF16)

    row = pl.BlockSpec((tm, DI), lambda i: (i, 0))
    return pl.pallas_call(body, grid=(L // tm,), in_specs=[row, row, pl.BlockSpec((1, DI), lambda i: (0, 0)), row],
                          out_specs=[row, row, pl.BlockSpec((8, DI), lambda i: (0, 0))],
                          out_shape=[jax.ShapeDtypeStruct((L, DI), F32), jax.ShapeDtypeStruct((L, ZX), BF16),
                                     jax.ShapeDtypeStruct((8, DI), F32)],
                          compiler_params=_params(("arbitrary",)), name=name)(y, zx, nw, dyn)


def _adamw(w, g, m, v, *, name, g_row=0, w_row=0, rows=None, into=None, emit_g=False):
    lead = w.ndim == 3
    R, C = w.shape[-2:]
    rows = R if rows is None else rows
    tr = max([t for t in range(8, rows + 1, 8) if rows % t == 0 and t * C <= 256 * 1024], default=rows)
    assert g_row % tr == 0 and w_row % tr == 0, (name, g_row, w_row, tr)
    n_out = 4 if emit_g else 3

    def body(w_ref, g_ref, m_ref, v_ref, *rest):
        outs = rest[-n_out:]
        gv = g_ref[...]
        mn = ADAM_B1 * m_ref[...] + (1.0 - ADAM_B1) * gv
        vn = ADAM_B2 * v_ref[...] + (1.0 - ADAM_B2) * (gv * gv)
        m_hat = mn / (1.0 - ADAM_B1 ** ADAM_STEP)
        v_hat = vn / (1.0 - ADAM_B2 ** ADAM_STEP)
        d_ref, mo_ref, vo_ref = outs[-3:]
        d_ref[...] = -ADAM_LR * (m_hat / (jnp.sqrt(v_hat) + ADAM_EPS) + ADAM_WD * w_ref[...])
        mo_ref[...] = mn
        vo_ref[...] = vn
        if emit_g:
            outs[0][...] = gv

    blk = (pl.BlockSpec((None, tr, C), lambda i: (0, i + w_row // tr, 0)) if lead
           else pl.BlockSpec((tr, C), lambda i: (i + w_row // tr, 0)))
    args, in_specs, alias = [w, g, m, v], [blk, pl.BlockSpec((tr, C), lambda i: (i + g_row // tr, 0)), blk, blk], {}
    if into is not None:
        args, in_specs, alias = args + list(into), in_specs + [ANY] * n_out, {4 + k: k for k in range(n_out)}
    return pl.pallas_call(body, grid=(rows // tr,), in_specs=in_specs, out_specs=[blk] * n_out,
                          out_shape=[jax.ShapeDtypeStruct(w.shape, F32)] * n_out, input_output_aliases=alias,
                          compiler_params=_params(("parallel",)), name=name)(*args)


def _residual(acc, xv, gv):
    return xv + gv * acc, acc


def _like(buf):
    return jax.ShapeDtypeStruct(buf.shape, buf.dtype)


def _mlp_fwd(x, mod, nw, wb, up_row, down_row, tag):
    sh, sc, g = mod
    h = _modnorm_fwd(x, nw, sc, sh, name=tag + "_norm")
    a = _matmul(h, wb, n=DFF, tm=TM_ALL, b_spec=pl.BlockSpec((None, D, 512), lambda mi, j: (j // 2, up_row // D, j % 2)),
                epi=lambda acc: (jnp.maximum(acc, 0.0),), out_dtypes=(BF16,), name=tag + "_up")
    xn, y = _matmul(a, wb, n=D, tm=TM_HALF, contract=_nn_split_sq,
                    b_spec=pl.BlockSpec((N_CHIPS, D, 512), lambda mi, j: (0, down_row // D, j)),
                    extras=(x, g), epi=_residual, out_dtypes=(F32, BF16), name=tag + "_down")
    return xn, (x, h, a, y)


def _mlp_bwd(dxo, dy, gsum, saved, mod, nw, wb, gb, up_row, down_row, below, tag):
    x, h, a, y = saved
    sh, sc, g = mod
    du = _matmul(dy, wb, n=DFF, tm=TM_ALL, contract=_nt,
                 b_spec=pl.BlockSpec((None, 512, D), lambda mi, j: (j // 2, down_row // 512 + j % 2, 0)),
                 extras=(a,), epi=lambda acc, av: (acc * (2.0 * av.astype(F32)),), out_dtypes=(BF16,), name=tag + "_dact")
    gb = _matmul_tn(a, dy, m=DFF, n=D, tm=D, tn=D, a_square=True, into=gb, out_struct=_like(wb),
                    out_spec=pl.BlockSpec((None, D, D), lambda mi, j: (mi, down_row // D, 0)), name=tag + "_ddown")
    dh = _matmul(du, wb, n=D, tm=TM_HALF, contract=_nt_split,
                 b_spec=pl.BlockSpec((N_CHIPS, 512, D), lambda mi, j: (0, up_row // 512 + j, 0)), name=tag + "_dh")
    gb = _matmul_tn(h, du, m=D, n=DFF, tm=D, into=gb, out_struct=_like(wb),
                    out_spec=pl.BlockSpec((None, D, 512), lambda mi, j: (j // 2, up_row // D, j % 2)), name=tag + "_dup")
    dx, sums, *nxt = _modnorm_bwd(x, dh, dxo, nw, sc, gsum, below, name=tag + "_dnorm")
    return dx, gb, sums, *nxt


def _ssd_fwd_scan(x, mod, nw, w_zx, w_dt, conv_w, conv_b, prm, tag):
    sh, sc, g = mod
    h = _modnorm_fwd(x, nw, sc, sh, name=tag + "_norm")
    zx = _matmul(h, w_zx, n=ZX, tm=TM_ALL, out_dtypes=(BF16,), name=tag + "_in")
    dtr = _matmul(h, w_dt, n=LANES, tm=TM_ALL, name=tag + "_in_dt")
    xbc, pre = _ssd_conv_fwd(zx, conv_w, conv_b, name=tag + "_conv")
    y, sprev = _ssd_fwd(xbc, dtr, prm, name=tag + "_scan")
    return h, zx, dtr, xbc, y, sprev, pre


def _ssd_fwd_out(x, mod, scan, gn_w, w_out, tag):
    sh, sc, g = mod
    h, zx, dtr, xbc, y, sprev, pre = scan
    yn = _gnorm_fwd(y, zx, gn_w, name=tag + "_gnorm")
    xn, yo = _matmul(yn, w_out, n=D, tm=TM_HALF, contract=_nn_split,
                     b_spec=pl.BlockSpec((N_CHIPS, 512, 512), lambda mi, j: (0, 0, j)),
                     extras=(x, g), epi=_residual, out_dtypes=(F32, BF16), name=tag + "_out")
    return xn, (x, h, zx, dtr, xbc, y, sprev, yn, yo, pre)


def _ssd_bwd_out(dyo, saved, w_out, tag):
    x, h, zx, dtr, xbc, y, sprev, yn, yo, pre = saved
    dyn = _matmul(dyo, w_out, n=DI, tm=TM_ALL, contract=_nt, b_spec=pl.BlockSpec((None, 512, D), lambda mi, j: (j, 0, 0)),
                  out_dtypes=(BF16,), name=tag + "_dyn")
    g_out = _matmul_tn(yn, dyo, m=DI, n=D, tn=D, out_struct=_like(w_out),
                       out_spec=pl.BlockSpec((None, 512, D), lambda mi, j: (mi, 0, 0)), name=tag + "_dout")
    return dyn, g_out


def _ssd_bwd_rest(dxo, dy, dzx, gsum, saved, mod, nw, w_zx, w_dt, conv_w, prm, tag):
    x, h, zx, dtr, xbc, y, sprev, yn, yo, pre = saved
    sh, sc, g = mod
    dxbc, ddtr, ssum = _ssd_bwd(xbc, dtr, prm, dy, sprev, name=tag + "_dscan")
    dzx, csum = _ssd_conv_bwd(zx, pre, dxbc, conv_w, dzx, name=tag + "_dconv")
    dh_dt = _matmul(ddtr, w_dt, n=D, tm=TM_ALL, contract=_nt, name=tag + "_dh_dt")
    dh = _matmul(dzx, w_zx, n=D, tm=TM_HALF, contract=_nt, extras=(dh_dt,), epi=lambda acc, e: (acc + e,), name=tag + "_dh")
    d_w_zx = _matmul_tn(h, dzx, m=D, n=ZX, tm=D, name=tag + "_din")
    d_w_dt = _matmul_tn(h, ddtr, m=D, n=LANES, tm=D, name=tag + "_din_dt")
    dx, sums = _modnorm_bwd(x, dh, dxo, nw, sc, gsum, None, name=tag + "_dnorm")
    return dx, d_w_zx, d_w_dt, sums, csum, ssum


def _sc_layer_fwd(x, mod, nw, w_sc_in, conv_w, wb, out_row, tag):
    sh, sc, g = mod
    h = _modnorm_fwd(x, nw, sc, sh, name=tag + "_norm")
    proj = _matmul(h, w_sc_in, n=3 * D, tm=TM_ALL, tn=256, out_dtypes=(BF16,),
                   b_spec=pl.BlockSpec((None, D, 256), lambda mi, j: (j // 3, 0, j % 3)),
                   name=tag + "_in")
    yv, v = _sc_fwd(proj, conv_w, name=tag + "_conv")
    xn, yo = _matmul(yv, wb, n=D, tm=TM_HALF, contract=_nn_split,
                     b_spec=pl.BlockSpec((N_CHIPS, 256, 512), lambda mi, j: (0, out_row // 256, j)),
                     extras=(x, g), epi=_residual, out_dtypes=(F32, BF16), name=tag + "_out")
    return xn, (x, h, proj, yv, yo, v)


def _sc_layer_bwd(dxo, dyo, gsum, saved, mod, nw, w_sc_in, conv_w, wb, gb, out_row, below, tag):
    x, h, proj, yv, yo, v = saved
    sh, sc, g = mod
    L = x.shape[0]
    dyv = _matmul(dyo, wb, n=D, tm=TM_ALL, tn=256, contract=_nt,
                  b_spec=pl.BlockSpec((None, 256, D), lambda mi, j: (j, out_row // 256, 0)), name=tag + "_dyv")
    gb = _matmul_tn(yv, dyo, m=D, n=D, tm=256, tn=D, into=gb, out_struct=_like(wb),
                    out_spec=pl.BlockSpec((None, 256, D), lambda mi, j: (mi, out_row // 256, 0)), name=tag + "_dout")
    dproj, csum = _sc_bwd(proj, v, dyv, conv_w, name=tag + "_dconv")
    tm = min(L, TM_HALF)
    dh = _matmul(dproj, w_sc_in, n=D, tm=tm, contract=_nt_sc_in, a_spec=pl.BlockSpec((3, tm, D), lambda mi, j: (0, mi, 0)),
                 b_spec=pl.BlockSpec((N_CHIPS, 512, SC_IN_SHARD), lambda mi, j: (0, j, 0)), name=tag + "_dh")
    g_sc_in = _matmul_tn(h, dproj, m=D, n=3 * D, tm=D, tn=256, b_spec=pl.BlockSpec((None, L, 256), lambda mi, j: (j // 4, 0, j % 4)),
                         out_spec=pl.BlockSpec((None, D, 256), lambda mi, j: (j // 3, 0, j % 3)),
                         out_struct=jax.ShapeDtypeStruct((N_CHIPS, D, SC_IN_SHARD), BF16), name=tag + "_din")
    dx, sums, *nxt = _modnorm_bwd(x, dh, dxo, nw, sc, gsum, below, name=tag + "_dnorm")
    return dx, gb, g_sc_in, sums, csum, *nxt


SUB_ROW = (0, 8, 16, 24)
SSD_CONV_ROW, GNORM_ROW, FINAL_ROW, SC_CONV_ROW, HEAD_ROW, SMALL_ROWS = 32, 56, 72, 80, 88, 96


def _all_gather_rows(blk, *, name):
    m_per, n = blk.shape

    def body(x_ref, out_ref, send_sems, recv_sems, local_sem):
        x, y, c = lax.axis_index("x"), lax.axis_index("y"), lax.axis_index("c")
        me, sibling = (x, y, c), (x, y, 1 - c)
        chips = [(1 - x, y), (x, 1 - y), (1 - x, 1 - y)]

        def rows(px, py, pc):
            return out_ref.at[pl.ds((4 * px + 2 * py + pc) * m_per, m_per), :]

        def copy(k, block, to, src=None):
            return pltpu.make_async_remote_copy(src_ref=rows(*block) if src is None else src, dst_ref=rows(*block),
                                                send_sem=send_sems.at[k], recv_sem=recv_sems.at[k], device_id=to,
                                                device_id_type=MESH)

        mine = pltpu.make_async_copy(x_ref, rows(*me), local_sem)
        mine.start()
        first = [copy(0, me, sibling, src=x_ref)] + [copy(1 + j, me, (*chip, c), src=x_ref) for j, chip in enumerate(chips)]
        for cp in first:
            cp.start()
        passed = [copy(4 + j, (*chip, c), sibling) for j, chip in enumerate(chips)]
        for j, chip in enumerate(chips):
            copy(1 + j, (*chip, c), me).wait_recv()
            passed[j].start()
        copy(0, sibling, me).wait_recv()
        for j, chip in enumerate(chips):
            copy(4 + j, (*chip, 1 - c), me).wait_recv()
        for cp in first + passed:
            cp.wait_send()
        mine.wait()

    return pl.pallas_call(
        body, out_shape=jax.ShapeDtypeStruct((N_DEV * m_per, n), blk.dtype),
        in_specs=[pl.BlockSpec(memory_space=pltpu.VMEM)], out_specs=pl.BlockSpec(memory_space=pltpu.VMEM),
        scratch_shapes=[pltpu.SemaphoreType.DMA((7,)), pltpu.SemaphoreType.DMA((7,)), pltpu.SemaphoreType.DMA],
        name=name)(blk)


def _half(ref, chip, c):
    hr = ref.shape[1] // 2
    return ref.at[chip, pl.ds(c * hr, hr), :]


def _gather_copy(bufs, sends, recvs, b, k, chip, pc, to):
    piece = _half(bufs[b], 2 * chip[0] + chip[1], pc)
    return pltpu.make_async_remote_copy(src_ref=piece, dst_ref=piece, send_sem=sends.at[4 * b + k], recv_sem=recvs.at[4 * b + k],
                                        device_id=to, device_id_type=MESH)


def _split_call(body, bufs, sems_in, n_sems, *, name, after=(), token=False, lands=()):
    nb, na, nl, starts = len(bufs), len(after), len(lands), not sems_in

    def wrapped(*refs):
        sems = refs[nb + na:nb + na + 2] if starts else refs[nb:nb + 2]
        made = refs[nb + na + 2 + nb:nb + na + 2 + nb + nl] if starts else ()
        body(tuple(refs[:nb]) + tuple(made), sems[0], sems[1])
        if token:
            refs[-1][...] = jnp.zeros_like(refs[-1])

    out_shape = [pltpu.SemaphoreType.DMA((n_sems,)) for _ in range(2 if starts else 0)]
    out_specs = [SEM] * len(out_shape) + [ANY] * (nb + nl)
    alias = {b: len(out_shape) + b for b in range(nb)}
    out_shape += [jax.ShapeDtypeStruct(b.shape, b.dtype) for b in bufs] + list(lands)
    if token:
        out_shape.append(jax.ShapeDtypeStruct((8, LANES), F32))
        out_specs.append(pl.BlockSpec(memory_space=pltpu.VMEM))
    return pl.pallas_call(
        wrapped, out_shape=out_shape, in_specs=[ANY] * nb + [SEM] * len(sems_in) + [ANY] * na, out_specs=out_specs,
        input_output_aliases=alias,
        compiler_params=pltpu.CompilerParams(has_side_effects=pltpu.SideEffectType.DATAFLOW_SIDE_EFFECTING),
        name=name)(*bufs, *sems_in, *after)


def _gather_start(bufs, *, name, after=()):
    nb = len(bufs)

    def body(ins, sends, recvs):
        x, y, c = lax.axis_index("x"), lax.axis_index("y"), lax.axis_index("c")
        chips = [(1 - x, y), (x, 1 - y), (1 - x, 1 - y)]
        for b in range(nb):
            _gather_copy(ins, sends, recvs, b, 0, (x, y), c, (x, y, 1 - c)).start()
            for j, chip in enumerate(chips):
                _gather_copy(ins, sends, recvs, b, 1 + j, (x, y), c, (*chip, c)).start()

    out = _split_call(body, bufs, (), 4 * nb, name=name, after=after, token=True)
    return (out[0], out[1], out[2:2 + nb]), out[-1]


def _gather_wait_first(flight, *, name, after=()):
    sends, recvs, bufs = flight
    nb = len(bufs)

    def body(ins, sends_, recvs_):
        x, y, c = lax.axis_index("x"), lax.axis_index("y"), lax.axis_index("c")
        chips = [(1 - x, y), (x, 1 - y), (1 - x, 1 - y)]
        for b in range(nb):
            _gather_copy(ins, sends_, recvs_, b, 0, (x, y), c, (x, y, 1 - c)).wait_send()
            _gather_copy(ins, sends_, recvs_, b, 0, (x, y), 1 - c, (x, y, c)).wait_recv()
            for j, chip in enumerate(chips):
                _gather_copy(ins, sends_, recvs_, b, 1 + j, (x, y), c, (*chip, c)).wait_send()
                _gather_copy(ins, sends_, recvs_, b, 1 + j, chip, c, (x, y, c)).wait_recv()

    return _split_call(body, bufs, (sends, recvs), 4 * nb, name=name, after=after)


def _gather_forward(bufs, *, name):
    nb = len(bufs)

    def body(ins, sends, recvs):
        x, y, c = lax.axis_index("x"), lax.axis_index("y"), lax.axis_index("c")
        chips = [(1 - x, y), (x, 1 - y), (1 - x, 1 - y)]
        for b in range(nb):
            for j, chip in enumerate(chips):
                _gather_copy(ins, sends, recvs, b, 1 + j, chip, c, (x, y, 1 - c)).start()

    out = _split_call(body, bufs, (), 4 * nb, name=name)
    return out[0], out[1], out[2:2 + nb]


def _gather_wait_forward(flight, *, name, after=()):
    sends, recvs, bufs = flight
    nb = len(bufs)

    def body(ins, sends_, recvs_):
        x, y, c = lax.axis_index("x"), lax.axis_index("y"), lax.axis_index("c")
        chips = [(1 - x, y), (x, 1 - y), (1 - x, 1 - y)]
        for b in range(nb):
            for j, chip in enumerate(chips):
                _gather_copy(ins, sends_, recvs_, b, 1 + j, chip, c, (x, y, 1 - c)).wait_send()
                _gather_copy(ins, sends_, recvs_, b, 1 + j, chip, 1 - c, (x, y, c)).wait_recv()

    return _split_call(body, bufs, (sends, recvs), 4 * nb, name=name, after=after)


def _owner_copies(hs, lands, sends, recvs):
    x, y, c = lax.axis_index("x"), lax.axis_index("y"), lax.axis_index("c")
    chips = [(1 - x, y), (x, 1 - y), (1 - x, 1 - y)]
    return [pltpu.make_async_remote_copy(src_ref=hs[b].at[2 * cx + cy], dst_ref=lands[b].at[j], send_sem=sends.at[3 * b + j],
                                         recv_sem=recvs.at[3 * b + j], device_id=(cx, cy, c), device_id_type=MESH)
            for b in range(len(hs)) for j, (cx, cy) in enumerate(chips)]


def _owners_start(hs, *, name):
    nb = len(hs)
    lands = [jax.ShapeDtypeStruct((3,) + h.shape[1:], h.dtype) for h in hs]

    def body(refs, sends, recvs):
        for cp in _owner_copies(refs[:nb], refs[nb:], sends, recvs):
            cp.start()

    out = _split_call(body, list(hs), (), 3 * nb, name=name, token=True, lands=lands)
    return (out[0], out[1], out[2:2 + 2 * nb]), out[-1]


def _owners_wait(flight, *, name, after=()):
    sends, recvs, bufs = flight
    nb = len(bufs) // 2

    def body(refs, sends_, recvs_):
        for cp in _owner_copies(refs[:nb], refs[nb:], sends_, recvs_):
            cp.wait()

    out = _split_call(body, bufs, (sends, recvs), 3 * nb, name=name, after=after)
    return out[:nb], out[nb:]


def _sibling_copies(gs, lands, sends, recvs):
    x, y, c = lax.axis_index("x"), lax.axis_index("y"), lax.axis_index("c")
    copies = []
    for b in range(len(gs)):
        hr = gs[b].shape[1] // 2
        copies.append(pltpu.make_async_remote_copy(
            src_ref=gs[b].at[:, pl.ds((1 - c) * hr, hr), :], dst_ref=lands[b], send_sem=sends.at[b], recv_sem=recvs.at[b],
            device_id=(x, y, 1 - c), device_id_type=MESH))
    return copies


def _sibling_start(gs, *, name, after=()):
    nb = len(gs)
    lands = [jax.ShapeDtypeStruct((g.shape[0], g.shape[1] // 2, g.shape[2]), g.dtype) for g in gs]

    def body(refs, sends, recvs):
        for cp in _sibling_copies(refs[:nb], refs[nb:], sends, recvs):
            cp.start()

    out = _split_call(body, list(gs), (), nb, name=name, after=after, token=True, lands=lands)
    return (out[0], out[1], out[2:2 + 2 * nb]), out[-1]


def _sibling_wait(flight, *, name, after=()):
    sends, recvs, bufs = flight
    nb = len(bufs) // 2

    def body(refs, sends_, recvs_):
        for cp in _sibling_copies(refs[:nb], refs[nb:], sends_, recvs_):
            cp.wait()

    out = _split_call(body, bufs, (sends, recvs), nb, name=name, after=after)
    return out[:nb], out[nb:]


def _result_copies(ts, sends, recvs):
    x, y, c = lax.axis_index("x"), lax.axis_index("y"), lax.axis_index("c")
    return [pltpu.make_async_remote_copy(src_ref=ts[b].at[c], dst_ref=ts[b].at[c], send_sem=sends.at[b], recv_sem=recvs.at[b],
                                         device_id=(x, y, 1 - c), device_id_type=MESH) for b in range(len(ts))]


def _result_start(ts, *, name):
    def body(refs, sends, recvs):
        for cp in _result_copies(refs, sends, recvs):
            cp.start()

    out = _split_call(body, ts, (), len(ts), name=name, token=True)
    return (out[0], out[1], out[2:2 + len(ts)]), out[-1]


def _result_wait(flight, *, name, after=()):
    sends, recvs, bufs = flight

    def body(refs, sends_, recvs_):
        for cp in _result_copies(refs, sends_, recvs_):
            cp.wait()

    return _split_call(body, bufs, (sends, recvs), len(bufs), name=name, after=after)


def _row_tile(rows, cols):
    best = 16
    for t in range(16, rows + 1, 16):
        if rows % t == 0 and t * cols <= 640 * 1024:
            best = t
    assert rows % best == 0, (rows, cols)
    return best


def _add_sibling_half(g, recv, core, *, name):
    nk, r, n = g.shape
    hr = r // 2
    tr = _row_tile(hr, n)

    def body(c_ref, a_ref, b_ref, o_ref):
        o_ref[...] = (a_ref[...].astype(F32) + b_ref[...].astype(F32)).astype(BF16)

    grid_spec = pltpu.PrefetchScalarGridSpec(
        num_scalar_prefetch=1, grid=(nk, hr // tr),
        in_specs=[pl.BlockSpec((None, tr, n), lambda k, i, c_ref: (k, c_ref[0] * (hr // tr) + i, 0)),
                  pl.BlockSpec((None, tr, n), lambda k, i, c_ref: (k, i, 0))],
        out_specs=pl.BlockSpec((None, tr, n), lambda k, i, c_ref: (k, i, 0)))
    return pl.pallas_call(body, grid_spec=grid_spec, out_shape=jax.ShapeDtypeStruct((nk, hr, n), BF16),
                          compiler_params=_params(("parallel", "parallel")), name=name)(core, g, recv)


def _add_chip_sums(h, recv, chip_core, *, name):
    _, hr, n = h.shape
    tr = _row_tile(hr, n)

    def body(k_ref, a_ref, b_ref, o_ref):
        o_ref[...] = ((a_ref[...].astype(F32) + b_ref[0].astype(F32)) + b_ref[1].astype(F32)) + b_ref[2].astype(F32)

    grid_spec = pltpu.PrefetchScalarGridSpec(
        num_scalar_prefetch=1, grid=(hr // tr,),
        in_specs=[pl.BlockSpec((None, tr, n), lambda i, k_ref: (k_ref[0], i, 0)),
                  pl.BlockSpec((3, tr, n), lambda i, k_ref: (0, i, 0))],
        out_specs=pl.BlockSpec((None, tr, n), lambda i, k_ref: (k_ref[1], i, 0)))
    return pl.pallas_call(body, grid_spec=grid_spec, out_shape=jax.ShapeDtypeStruct((2, hr, n), F32),
                          compiler_params=_params(("parallel",)), name=name)(chip_core, h, recv)


def _sum_devices(g, *, name):
    nd, r, n = g.shape

    def body(g_ref, o_ref):
        acc = g_ref[0]
        for i in range(1, nd):
            acc = acc + g_ref[i]
        o_ref[...] = acc

    return pl.pallas_call(body, out_shape=jax.ShapeDtypeStruct((r, n), F32), name=name)(g)


def _own_slot(parts, chip, *, name):
    rows, cols = sum(w.shape[1] for w, _ in parts), parts[0][0].shape[2]
    buf, row0 = None, 0
    for p, (w, idx) in enumerate(parts):
        r = w.shape[1]
        tr = min(r, 256)
        assert r % tr == 0 and row0 % tr == 0, (name, r, row0)

        def body(chip_ref, w_ref, *rest):
            rest[-1][...] = w_ref[...].astype(BF16)

        grid_spec = pltpu.PrefetchScalarGridSpec(
            num_scalar_prefetch=1, grid=(r // tr,),
            in_specs=[pl.BlockSpec((None, tr, cols), lambda i, c_ref, idx=idx: (idx, i, 0))] + ([] if buf is None else [ANY]),
            out_specs=pl.BlockSpec((None, tr, cols), lambda i, c_ref, row0=row0, tr=tr: (c_ref[0], row0 // tr + i, 0)))
        buf = pl.pallas_call(body, grid_spec=grid_spec, out_shape=jax.ShapeDtypeStruct((N_CHIPS, rows, cols), BF16),
                             input_output_aliases={} if buf is None else {2: 0}, compiler_params=_params(("parallel",)),
                             name=f"{name}{p}")(chip, w, *(() if buf is None else (buf,)))
        row0 += r
    return buf


def kernel(x, c, ada_w, ada_b, mix_norm_w, mlp_norm_w, mlp_up, mlp_down, ssd_in_w, ssd_conv_w, ssd_conv_b, ssd_dt_bias, ssd_A_log, ssd_D, ssd_norm_w, ssd_out_w, sc_in_w, sc_conv_w, sc_out_w, final_norm_w, loss_target, m_ada_w, m_ada_b, m_mix_norm_w, m_mlp_norm_w, m_mlp_up, m_mlp_down, m_ssd_in_w, m_ssd_conv_w, m_ssd_conv_b, m_ssd_dt_bias, m_ssd_A_log, m_ssd_D, m_ssd_norm_w, m_ssd_out_w, m_sc_in_w, m_sc_conv_w, m_sc_out_w, m_final_norm_w, v_ada_w, v_ada_b, v_mix_norm_w, v_mlp_norm_w, v_mlp_up, v_mlp_down, v_ssd_in_w, v_ssd_conv_w, v_ssd_conv_b, v_ssd_dt_bias, v_ssd_A_log, v_ssd_D, v_ssd_norm_w, v_ssd_out_w, v_sc_in_w, v_sc_conv_w, v_sc_out_w, v_final_norm_w):
    xi, yi, ci = lax.axis_index("x"), lax.axis_index("y"), lax.axis_index("c")
    chip = 2 * xi + yi
    dev = 2 * chip + ci
    n_ada = ada_w.shape[2]

    conv_flat = jnp.concatenate([ssd_conv_w.reshape(-1), sc_conv_w.reshape(-1), jnp.zeros((256,), F32)]).reshape(4, D)
    blk0 = jnp.concatenate([c, conv_flat, jnp.zeros((3, D), F32)], axis=0)
    got0 = _all_gather_rows(blk0, name="gather_cond").reshape(N_DEV, 8, D)
    c_all = got0[:, 0]
    conv_all = got0[0::2, 1:5].reshape(N_CHIPS, 4 * D)
    ssd_conv = jnp.moveaxis(conv_all[:, :4 * 768].reshape(N_CHIPS, 4, 768), 0, 1).reshape(4, CONVD)
    sc_conv = jnp.moveaxis(conv_all[:, 4 * 768:4 * 768 + 3 * 256].reshape(N_CHIPS, 3, 256), 0, 1).reshape(3, D)
    mod_shard = [_matmul(c_all, ada_w, n=n_ada, a_silu=True, b_spec=pl.BlockSpec((None, D, 512), lambda mi, j, i=i: (i, 0, j)),
                         extras=(lax.dynamic_slice(ada_b, (i, chip * n_ada), (1, n_ada)),),
                         epi=lambda acc, b: (acc + b,), name=f"ada_mod{i}") for i in range(2)]
    mod_all = _all_gather_rows(jnp.concatenate(mod_shard, axis=0), name="gather_mod")
    mod_all = mod_all.reshape(N_DEV, 2, N_DEV, n_ada)[0::2]
    mod = jnp.moveaxis(lax.dynamic_index_in_dim(mod_all, dev, axis=2, keepdims=False), 0, 1).reshape(2, 6, D)
    mods = [[mod[i, j:j + 1] for j in range(6)] for i in range(2)]

    up_row, down_row, sc_out_row = 0, D, 2 * D
    chip1 = chip.reshape(1).astype(jnp.int32)
    a_bufs = [_own_slot([(ssd_in_w, 0)], chip1, name="slot_ssd_in")]
    b_bufs = [_own_slot([(ssd_out_w, 0)], chip1, name="slot_ssd_out"),
              _own_slot([(mlp_up, 0), (mlp_down, 0)], chip1, name="slot_mlp0_")]
    c_bufs = [_own_slot([(sc_in_w, 0)], chip1, name="slot_sc_in"),
              _own_slot([(mlp_up, 1), (mlp_down, 1), (sc_out_w, 0)], chip1, name="slot_layer1_")]
    fly_a, tok = _gather_start(a_bufs, name="gather_a_start", after=(mod,))
    fly_b, tok = _gather_start(b_bufs, name="gather_b_start", after=(tok,))
    fly_c, tok = _gather_start(c_bufs, name="gather_c_start", after=(tok,))

    row = lambda v: v.reshape(1, -1)
    xs, tgt = x[0], loss_target[0]
    prm = jnp.pad(jnp.concatenate([ssd_dt_bias, ssd_A_log, ssd_D, jnp.zeros((5, NH), F32)], axis=0), ((0, 0), (0, LANES - NH)))
    mix_nw = [row(mix_norm_w[i]) for i in range(2)]
    mlp_nw = [row(mlp_norm_w[i]) for i in range(2)]
    a_bufs = _gather_wait_first(fly_a, name="gather_a_landed", after=(tok,))
    (w_ssd_in,) = _gather_wait_forward(_gather_forward(a_bufs, name="gather_a_pass"), name="gather_a_done")
    ssd_in_full = jnp.moveaxis(w_ssd_in, 0, 1).reshape(D, N_CHIPS * SSD_IN_SHARD)
    w_zx, w_dt = ssd_in_full[:, :ZX], jnp.pad(ssd_in_full[:, ZX:], ((0, 0), (0, LANES - NH)))
    scan = _ssd_fwd_scan(xs, mods[0][0:3], mix_nw[0], w_zx, w_dt, ssd_conv, ssd_conv_b, prm, "ssd")
    fly_b = _gather_forward(_gather_wait_first(fly_b, name="gather_b_landed", after=(scan[3],)), name="gather_b_pass")
    w_ssd_out, w_b = _gather_wait_forward(fly_b, name="gather_b_done", after=(scan[4],))
    x1, s_ssd = _ssd_fwd_out(xs, mods[0][0:3], scan, ssd_norm_w, w_ssd_out, "ssd")
    x2, s_mlp0 = _mlp_fwd(x1, mods[0][3:6], mlp_nw[0], w_b, up_row, down_row, "mlp0")
    c_bufs = _gather_wait_first(fly_c, name="gather_c_landed", after=(x2,))
    w_sc_in, w_c = _gather_wait_forward(_gather_forward(c_bufs, name="gather_c_pass"), name="gather_c_done")
    x3, s_sc = _sc_layer_fwd(x2, mods[1][0:3], mix_nw[1], w_sc_in, sc_conv, w_c, sc_out_row, "sc")
    x4, s_mlp1 = _mlp_fwd(x3, mods[1][3:6], mlp_nw[1], w_c, up_row, down_row, "mlp1")

    core = ci.reshape(1).astype(jnp.int32)
    chip_core = jnp.stack([chip, ci]).astype(jnp.int32)

    def reduce_swap(gbufs, tag, after=()):
        return _sibling_start(gbufs, name=tag + "_sibling_start", after=after)

    def reduce_send(flight, tag, after):
        gs, sib = _sibling_wait(flight, name=tag + "_sibling_landed", after=after)
        hs = [_add_sibling_half(g, s, core, name=f"{tag}_add_sibling{b}") for b, (g, s) in enumerate(zip(gs, sib))]
        return _owners_start(hs, name=tag + "_owners_start")

    def reduce_sum(flight, tag, after):
        hs, lands = _owners_wait(flight, name=tag + "_owners_landed", after=after)
        ts = [_add_chip_sums(h, o, chip_core, name=f"{tag}_add_chips{b}") for b, (h, o) in enumerate(zip(hs, lands))]
        return _result_start(ts, name=tag + "_result_start")

    def reduce_done(flight, tag, after=()):
        return [t.reshape(-1, t.shape[2]) for t in _result_wait(flight, name=tag + "_result_landed", after=after)]

    dx4, fsum, dy, gs = _final_loss(x4, row(final_norm_w), tgt, (mods[1][5], s_mlp1[3]), name="final_loss")
    dx3, g_c, sum_mlp1, dy, gs = _mlp_bwd(dx4, dy, gs, s_mlp1, mods[1][3:6], mlp_nw[1], w_c, None, up_row, down_row,
                                          (mods[1][2], s_sc[4]), "mlp1")
    dx2, g_c, g_sc_in, sum_sc, sc_csum, dy, gs = _sc_layer_bwd(dx3, dy, gs, s_sc, mods[1][0:3], mix_nw[1], w_sc_in, sc_conv,
                                                               w_c, g_c, sc_out_row, (mods[0][5], s_mlp0[3]), "sc")
    dx1, g_b, sum_mlp0, dy, gsum_ssd = _mlp_bwd(dx2, dy, gs, s_mlp0, mods[0][3:6], mlp_nw[0], w_b, None, up_row, down_row,
                                                (mods[0][2], s_ssd[8]), "mlp0")
    dyn, g_ssd_out = _ssd_bwd_out(dy, s_ssd, w_ssd_out, "ssd")
    fly_1, tok = reduce_swap([g_c, g_sc_in, g_b, g_ssd_out], "rs1")
    dy, dzx, gnsum = _gnorm_bwd(s_ssd[5], s_ssd[2], ssd_norm_w + tok[0:1, 0:1], dyn, name="ssd_dgnorm")
    fly_1, tok = reduce_send(fly_1, "rs1", (dy,))
    grad_x, d_w_zx, d_w_dt, sum_ssd, csum, ssum = _ssd_bwd_rest(
        dx1, dy, dzx, gsum_ssd, s_ssd, mods[0][0:3], mix_nw[0], w_zx, w_dt, ssd_conv, prm + tok[0:1, 0:1], "ssd")
    fly_1, tok = reduce_sum(fly_1, "rs1", (grad_x,))

    def ssd_in_owner(k):
        lo, hi = k * SSD_IN_SHARD, (k + 1) * SSD_IN_SHARD
        if hi <= ZX:
            return d_w_zx[:, lo:hi]
        return jnp.concatenate([d_w_zx[:, lo:], d_w_dt[:, :hi - ZX]], axis=1)

    small = jnp.concatenate([sum_ssd + tok[0:1, 0:1], sum_mlp0, sum_sc, sum_mlp1, csum.reshape(24, D), gnsum.reshape(16, D),
                             fsum, sc_csum, jnp.pad(ssum, ((0, 0), (0, D - LANES)))], axis=0)
    small_all = _all_gather_rows(small, name="gather_small").reshape(N_DEV, SMALL_ROWS, D)
    fly_2, tok = reduce_swap([jnp.stack([ssd_in_owner(k) for k in range(N_CHIPS)]).astype(BF16)], "rs2", (small_all,))
    fly_2, tok = reduce_send(fly_2, "rs2", (tok,))
    t_c, t_sc_in, t_b, t_ssd_out = reduce_done(fly_1, "rs1", (tok,))
    small_all = small_all + tok[0:1, 0:1]
    tot = _sum_devices(small_all, name="sum_small")
    loss = tot[FINAL_ROW + 1, 0]
    mod_rows = [r + o for r in SUB_ROW for o in (3, 2, 0)]
    g_ada_b = jnp.stack([tot[r] for r in mod_rows]).reshape(2, 6 * D)
    g_mix_norm = jnp.stack([tot[SUB_ROW[0] + 1], tot[SUB_ROW[2] + 1]])
    g_mlp_norm = jnp.stack([tot[SUB_ROW[1] + 1], tot[SUB_ROW[3] + 1]])
    conv_sums = tot[SSD_CONV_ROW:SSD_CONV_ROW + 24].reshape(8, CONVD)
    g_ssd_conv_w = lax.dynamic_slice(conv_sums, (0, chip * 768), (4, 768))[None]
    g_ssd_conv_b = conv_sums[4:5]
    g_ssd_norm = tot[GNORM_ROW:GNORM_ROW + 2].reshape(1, DI)
    g_final = tot[FINAL_ROW]
    g_sc_conv_w = lax.dynamic_slice(tot[SC_CONV_ROW:SC_CONV_ROW + 3], (0, chip * 256), (3, 256))[None]
    g_a_log, g_d, g_dt_bias = (tot[HEAD_ROW + r:HEAD_ROW + r + 1, 0:NH] for r in range(3))
    c_pad = jnp.concatenate([c_all, jnp.zeros((8, D), F32)], axis=0)
    dmod_all = jnp.stack([small_all[:, r] for r in mod_rows], axis=1).reshape(N_DEV, 2, 6 * D)
    g_ada_w = []
    for i in range(2):
        dm = lax.dynamic_slice(dmod_all[:, i], (0, chip * n_ada), (N_DEV, n_ada))
        g_ada_w.append(_matmul_tn(c_pad, jnp.concatenate([dm, jnp.zeros_like(dm)], axis=0), m=D, n=n_ada, a_silu=True,
                                  name=f"ada_dw{i}"))

    big = dict(ada_w=[(g, 0) for g in g_ada_w], mlp_up=[(t_b, up_row), (t_c, up_row)], mlp_down=[(t_b, down_row), (t_c, down_row)],
               ssd_out_w=[(t_ssd_out, 0)], sc_out_w=[(t_c, sc_out_row)], sc_in_w=[(t_sc_in, 0)], ssd_in_w=None)
    grads = dict(ada_b=g_ada_b, mix_norm_w=g_mix_norm, mlp_norm_w=g_mlp_norm, ssd_conv_w=g_ssd_conv_w,
                 ssd_conv_b=g_ssd_conv_b, ssd_dt_bias=g_dt_bias, ssd_A_log=g_a_log, ssd_D=g_d, ssd_norm_w=g_ssd_norm,
                 sc_conv_w=g_sc_conv_w, final_norm_w=g_final)
    weights = dict(ada_w=(ada_w, m_ada_w, v_ada_w), ada_b=(ada_b, m_ada_b, v_ada_b),
                   mix_norm_w=(mix_norm_w, m_mix_norm_w, v_mix_norm_w), mlp_norm_w=(mlp_norm_w, m_mlp_norm_w, v_mlp_norm_w),
                   mlp_up=(mlp_up, m_mlp_up, v_mlp_up), mlp_down=(mlp_down, m_mlp_down, v_mlp_down),
                   ssd_in_w=(ssd_in_w, m_ssd_in_w, v_ssd_in_w), ssd_conv_w=(ssd_conv_w, m_ssd_conv_w, v_ssd_conv_w),
                   ssd_conv_b=(ssd_conv_b, m_ssd_conv_b, v_ssd_conv_b), ssd_dt_bias=(ssd_dt_bias, m_ssd_dt_bias, v_ssd_dt_bias),
                   ssd_A_log=(ssd_A_log, m_ssd_A_log, v_ssd_A_log), ssd_D=(ssd_D, m_ssd_D, v_ssd_D),
                   ssd_norm_w=(ssd_norm_w, m_ssd_norm_w, v_ssd_norm_w), ssd_out_w=(ssd_out_w, m_ssd_out_w, v_ssd_out_w),
                   sc_in_w=(sc_in_w, m_sc_in_w, v_sc_in_w), sc_conv_w=(sc_conv_w, m_sc_conv_w, v_sc_conv_w),
                   sc_out_w=(sc_out_w, m_sc_out_w, v_sc_out_w), final_norm_w=(final_norm_w, m_final_norm_w, v_final_norm_w))
    def step(nm, parts):
        w, m, v = (t if t.shape[0] == 1 else t.reshape(-1, t.shape[-1]) for t in weights[nm])
        rows, outs = w.shape[-2] // len(parts), None
        for i, (gbuf, g_row) in enumerate(parts):
            outs = _adamw(w, gbuf, m, v, g_row=g_row, w_row=i * rows, rows=rows, into=outs, emit_g=True, name=f"adamw_{nm}{i}")
        return outs

    res = {}
    for nm, (w, m, v) in weights.items():
        two_d = (-1, w.shape[-1]) if w.ndim > 1 else (1, -1)
        if nm not in big:
            res[nm] = (grads[nm], *_adamw(w.reshape(two_d), grads[nm].reshape(two_d), m.reshape(two_d), v.reshape(two_d),
                                          name="adamw_" + nm))
        elif big[nm] is not None:
            res[nm] = step(nm, big[nm])
    fly_2, tok = reduce_sum(fly_2, "rs2", tuple(r[1] for r in res.values()))
    (t_ssd_in,) = reduce_done(fly_2, "rs2", (tok,))
    w_t, m_t, v_t = (jnp.swapaxes(t[0], 0, 1) for t in weights["ssd_in_w"])
    res["ssd_in_w"] = [jnp.swapaxes(o, 0, 1) for o in _adamw(w_t, t_ssd_in.T, m_t, v_t, emit_g=True, name="adamw_ssd_in_w")]
    outs = [[res[nm][k].reshape(weights[nm][0].shape) for nm in weights] for k in range(4)]
    return (loss, grad_x[None], *outs[0], *outs[1], *outs[2], *outs[3])
```

```python
import jax
import jax.numpy as jnp
from jax import lax
from jax.experimental import pallas as pl
from jax.experimental.pallas import tpu as pltpu

F32 = jnp.float32
BF16 = jnp.bfloat16
MESH = pl.DeviceIdType.MESH

D = 1024
DFF = 4096
DI = 2048
NH = 32
HP = 64
NG = 4
NS = 128
CH = 128
CONVD = DI + 2 * NG * NS
ZX = DI + CONVD
GW = NG * NS
LANES = 128
N_CHIPS = 4
N_DEV = 8
EPS = 1e-5
ADAM_LR, ADAM_B1, ADAM_B2, ADAM_EPS, ADAM_WD, ADAM_STEP = 1e-3, 0.9, 0.999, 1e-8, 0.01, 10
VMEM_LIMIT = 48 * 1024 * 1024
TM_ALL = 2048
TM_HALF = 1024
ANY = pl.BlockSpec(memory_space=pl.ANY)
SEM = pl.BlockSpec(memory_space=pltpu.SEMAPHORE)

SSD_IN_SHARD = 1288
SC_IN_SHARD = 768


def _params(sem=None):
    return pltpu.CompilerParams(dimension_semantics=sem, vmem_limit_bytes=VMEM_LIMIT)


def _sigmoid(v):
    return 0.5 * jnp.tanh(0.5 * v) + 0.5


def _dot(a, b, dims=((1,), (0,)), precision=None):
    return lax.dot_general(a, b, (dims, ((), ())), preferred_element_type=F32, precision=precision)


def _dot_nt(a, b):
    return _dot(a, b, ((1,), (1,)))


def _dot_tn(a, b):
    return _dot(a, b, ((0,), (0,)))


def _nn(av, bv):
    return _dot(av.astype(BF16), bv.astype(BF16))


def _nt(av, bv):
    return _dot_nt(av.astype(BF16), bv.astype(BF16))


def _nn_split(av, bv):
    return _dot(av.astype(BF16), bv.reshape(-1, bv.shape[2]))


def _nn_split_sq(av, bv):
    af = av.astype(F32)
    return _nn_split(af * af, bv)


def _nt_split(av, bv):
    kc = bv.shape[2]
    acc = _dot_nt(av[:, 0:kc].astype(BF16), bv[0])
    for s in range(1, bv.shape[0]):
        acc = acc + _dot_nt(av[:, s * kc:(s + 1) * kc].astype(BF16), bv[s])
    return acc


def _nt_sc_in(av, bv):
    q = 256
    acc = None
    for i in range(3 * D // q):
        a_blk = av[i // 4][:, (i % 4) * q:(i % 4 + 1) * q]
        b_blk = bv[i // 3][:, (i % 3) * q:(i % 3 + 1) * q]
        t = _dot_nt(a_blk, b_blk)
        acc = t if acc is None else acc + t
    return acc


def _matmul(a, b, *, name, n, contract=_nn, a_spec=None, b_spec=None, tm=512, tn=512, extras=(), epi=None,
            out_dtypes=(F32,), a_silu=False):
    M = a.shape[-2]
    tm, tn = min(tm, M), min(tn, n)
    assert M % tm == 0 and n % tn == 0, (name, M, n, tm, tn)
    n_ex = len(extras)
    if a_spec is None:
        a_spec = pl.BlockSpec((tm, a.shape[1]), lambda i, j: (i, 0))
    if b_spec is None:
        b_spec = (pl.BlockSpec((tn, b.shape[1]), lambda i, j: (j, 0)) if contract is _nt
                  else pl.BlockSpec((b.shape[0], tn), lambda i, j: (0, j)))

    def body(*refs):
        av = refs[0][...]
        if a_silu:
            av = av * _sigmoid(av)
        acc = contract(av, refs[1][...])
        res = epi(acc, *[r[...] for r in refs[2:2 + n_ex]]) if epi is not None else (acc,)
        for o_ref, r in zip(refs[2 + n_ex:], res, strict=True):
            o_ref[...] = r.astype(o_ref.dtype)

    in_specs = [a_spec, b_spec]
    for e in extras:
        in_specs.append(pl.BlockSpec((1, tn), lambda i, j: (0, j)) if e.shape[0] == 1 and M != 1
                        else pl.BlockSpec((tm, tn), lambda i, j: (i, j)))
    outs = pl.pallas_call(
        body, grid=(M // tm, n // tn), in_specs=in_specs,
        out_specs=[pl.BlockSpec((tm, tn), lambda i, j: (i, j)) for _ in out_dtypes],
        out_shape=[jax.ShapeDtypeStruct((M, n), dt) for dt in out_dtypes],
        compiler_params=_params(("parallel", "parallel")), name=name)(a, b, *extras)
    return outs if len(out_dtypes) > 1 else outs[0]


def _matmul_tn(a, b, *, name, m, n, tm=512, tn=512, a_spec=None, b_spec=None, out_spec=None, out_struct=None, into=None,
               a_silu=False, a_square=False):
    T = a.shape[-2]
    tm, tn = min(tm, m), min(tn, n)
    assert m % tm == 0 and n % tn == 0, (name, m, n, tm, tn)
    if a_spec is None:
        a_spec = pl.BlockSpec((T, tm), lambda i, j: (0, i))
    if b_spec is None:
        b_spec = pl.BlockSpec((T, tn), lambda i, j: (0, j))
    if out_spec is None:
        out_spec, out_struct = pl.BlockSpec((tm, tn), lambda i, j: (i, j)), jax.ShapeDtypeStruct((m, n), F32)

    def body(a_ref, b_ref, *rest):
        av = a_ref[...]
        if a_silu:
            av = av * _sigmoid(av)
        if a_square:
            av = av.astype(F32) * av.astype(F32)
        rest[-1][...] = _dot_tn(av.astype(BF16), b_ref[...].astype(BF16)).astype(rest[-1].dtype)

    args, in_specs, alias = [a, b], [a_spec, b_spec], {}
    if into is not None:
        args, in_specs, alias = args + [into], in_specs + [ANY], {2: 0}
    return pl.pallas_call(body, grid=(m // tm, n // tn), in_specs=in_specs, out_specs=out_spec, out_shape=out_struct,
                          input_output_aliases=alias, compiler_params=_params(("parallel", "parallel")), name=name)(*args)


def _modnorm_fwd(x, nw, sc, sh, *, name):
    L = x.shape[0]
    tm = min(L, 512)

    def body(x_ref, nw_ref, sc_ref, sh_ref, h_ref):
        xv = x_ref[...]
        r = lax.rsqrt(jnp.mean(xv * xv, axis=-1, keepdims=True) + EPS)
        h_ref[...] = ((xv * r * nw_ref[...]) * (1.0 + sc_ref[...]) + sh_ref[...]).astype(BF16)

    row = pl.BlockSpec((tm, D), lambda i: (i, 0))
    vec = pl.BlockSpec((1, D), lambda i: (0, 0))
    return pl.pallas_call(body, grid=(L // tm,), in_specs=[row, vec, vec, vec], out_specs=row,
                          out_shape=jax.ShapeDtypeStruct((L, D), BF16),
                          compiler_params=_params(("parallel",)), name=name)(x, nw, sc, sh)


def _gate_outputs(dx, below_refs, dy_ref, gs_ref):
    g_ref, y_ref = below_refs
    dy_ref[...] = (dx * g_ref[...]).astype(BF16)
    gs_ref[0:1, :] += jnp.sum(dx * y_ref[...].astype(F32), axis=0, keepdims=True)


def _modnorm_bwd(x, dh, dxo, nw, sc, gsum, below, *, name):
    L = x.shape[0]
    tm = min(L, 256)
    nb = 0 if below is None else 2

    def body(x_ref, dh_ref, dxo_ref, nw_ref, sc_ref, g_ref, *rest):
        dx_ref, s_ref = rest[nb:nb + 2]

        @pl.when(pl.program_id(0) == 0)
        def _():
            s_ref[...] = g_ref[...]
            if nb:
                rest[-1][...] = jnp.zeros_like(rest[-1])

        xv, dhv = x_ref[...], dh_ref[...]
        r = lax.rsqrt(jnp.mean(xv * xv, axis=-1, keepdims=True) + EPS)
        xhat = xv * r
        dxhat = dhv * (nw_ref[...] * (1.0 + sc_ref[...]))
        dx = dxo_ref[...] + r * (dxhat - xhat * jnp.mean(dxhat * xhat, axis=-1, keepdims=True))
        dx_ref[...] = dx
        s_ref[1:2, :] += jnp.sum(dhv * xhat, axis=0, keepdims=True) * (1.0 + sc_ref[...])
        s_ref[2:3, :] += jnp.sum(dhv * xhat, axis=0, keepdims=True) * nw_ref[...]
        s_ref[3:4, :] += jnp.sum(dhv, axis=0, keepdims=True)
        if nb:
            _gate_outputs(dx, rest[:nb], rest[-2], rest[-1])

    row = pl.BlockSpec((tm, D), lambda i: (i, 0))
    vec = pl.BlockSpec((1, D), lambda i: (0, 0))
    blk = pl.BlockSpec((8, D), lambda i: (0, 0))
    in_specs, out_specs = [row, row, row, vec, vec, blk], [row, blk]
    out_shape = [jax.ShapeDtypeStruct((L, D), F32), jax.ShapeDtypeStruct((8, D), F32)]
    if nb:
        in_specs, out_specs = in_specs + [vec, row], out_specs + [row, blk]
        out_shape += [jax.ShapeDtypeStruct((L, D), BF16), jax.ShapeDtypeStruct((8, D), F32)]
    return pl.pallas_call(body, grid=(L // tm,), in_specs=in_specs, out_specs=out_specs, out_shape=out_shape,
                          compiler_params=_params(("arbitrary",)), name=name)(x, dh, dxo, nw, sc, gsum, *(below or ()))


def _final_loss(x, fw, tgt, below, *, name):
    L = x.shape[0]
    tm = min(L, 256)

    def body(x_ref, fw_ref, t_ref, g_ref, y_ref, dx_ref, s_ref, dy_ref, gs_ref):
        @pl.when(pl.program_id(0) == 0)
        def _():
            s_ref[...] = jnp.zeros_like(s_ref)
            gs_ref[...] = jnp.zeros_like(gs_ref)

        xv = x_ref[...]
        r = lax.rsqrt(jnp.mean(xv * xv, axis=-1, keepdims=True) + EPS)
        xhat = xv * r
        diff = xhat * fw_ref[...] - t_ref[...]
        dout = diff * (1.0 / D)
        dxhat = dout * fw_ref[...]
        dx = r * (dxhat - xhat * jnp.mean(dxhat * xhat, axis=-1, keepdims=True))
        dx_ref[...] = dx
        s_ref[0:1, :] += jnp.sum(dout * xhat, axis=0, keepdims=True)
        s_ref[1:2, :] += jnp.zeros((1, D), F32) + 0.5 * jnp.sum(jnp.sum(diff * diff, axis=-1, keepdims=True) * (1.0 / D))
        _gate_outputs(dx, (g_ref, y_ref), dy_ref, gs_ref)

    row = pl.BlockSpec((tm, D), lambda i: (i, 0))
    vec = pl.BlockSpec((1, D), lambda i: (0, 0))
    blk = pl.BlockSpec((8, D), lambda i: (0, 0))
    return pl.pallas_call(body, grid=(L // tm,), in_specs=[row, vec, row, vec, row], out_specs=[row, blk, row, blk],
                          out_shape=[jax.ShapeDtypeStruct((L, D), F32), jax.ShapeDtypeStruct((8, D), F32),
                                     jax.ShapeDtypeStruct((L, D), BF16), jax.ShapeDtypeStruct((8, D), F32)],
                          compiler_params=_params(("arbitrary",)), name=name)(x, fw, tgt, *below)


def _shift_down(v, j):
    if j == 0:
        return v
    rolled = pltpu.roll(v, j, 0)
    row = lax.broadcasted_iota(jnp.int32, (8, v.shape[1]), 0)
    return jnp.concatenate([jnp.where(row >= j, rolled[0:8], 0.0), rolled[8:]], axis=0)


def _shift_up(v, j):
    if j == 0:
        return v
    n = v.shape[0]
    rolled = pltpu.roll(v, n - j, 0)
    row = lax.broadcasted_iota(jnp.int32, (8, v.shape[1]), 0)
    return jnp.concatenate([rolled[:n - 8], jnp.where(row < 8 - j, rolled[n - 8:], 0.0)], axis=0)


def _ssd_conv_fwd(zx, w, b, *, name):
    L = zx.shape[0]
    cb = 256
    k = w.shape[0]

    def body(x_ref, w_ref, b_ref, o_ref, p_ref):
        xv = x_ref[...].astype(F32)
        pre = b_ref[...] + xv * w_ref[k - 1:k, :]
        for j in range(1, k):
            pre = pre + _shift_down(xv, j) * w_ref[k - 1 - j:k - j, :]
        o_ref[...] = (pre * _sigmoid(pre)).astype(BF16)
        p_ref[...] = pre.astype(BF16)

    blk = pl.BlockSpec((L, cb), lambda i: (0, i))
    return pl.pallas_call(
        body, grid=(CONVD // cb,),
        in_specs=[pl.BlockSpec((L, cb), lambda i: (0, i + DI // cb)), pl.BlockSpec((k, cb), lambda i: (0, i)),
                  pl.BlockSpec((1, cb), lambda i: (0, i))],
        out_specs=[blk, blk], out_shape=[jax.ShapeDtypeStruct((L, CONVD), BF16)] * 2,
        compiler_params=_params(("parallel",)), name=name)(zx, w, b)


def _ssd_conv_bwd(zx, pre, dact, w, dzx, *, name):
    L = zx.shape[0]
    cb = 256
    k = w.shape[0]

    def body(x_ref, p_ref, da_ref, w_ref, _, dx_ref, s_ref):
        xv, pv = x_ref[...].astype(F32), p_ref[...].astype(F32)
        s = _sigmoid(pv)
        dpre = da_ref[...].astype(F32) * (s * (1.0 + pv * (1.0 - s)))
        s_ref[...] = jnp.zeros_like(s_ref)
        s_ref[k:k + 1, :] = jnp.sum(dpre, axis=0, keepdims=True)
        s_ref[k - 1:k, :] = jnp.sum(dpre * xv, axis=0, keepdims=True)
        dx = dpre * w_ref[k - 1:k, :]
        for j in range(1, k):
            later = _shift_up(dpre, j)
            dx = dx + later * w_ref[k - 1 - j:k - j, :]
            s_ref[k - 1 - j:k - j, :] = jnp.sum(later * xv, axis=0, keepdims=True)
        dx_ref[...] = dx.astype(BF16)

    blk = pl.BlockSpec((L, cb), lambda i: (0, i))
    return pl.pallas_call(
        body, grid=(CONVD // cb,),
        in_specs=[pl.BlockSpec((L, cb), lambda i: (0, i + DI // cb)), blk, blk, pl.BlockSpec((k, cb), lambda i: (0, i)), ANY],
        out_specs=[pl.BlockSpec((L, cb), lambda i: (0, i + DI // cb)), pl.BlockSpec((8, cb), lambda i: (0, i))],
        out_shape=[jax.ShapeDtypeStruct((L, ZX), BF16), jax.ShapeDtypeStruct((8, CONVD), F32)],
        input_output_aliases={4: 0}, compiler_params=_params(("parallel",)), name=name)(zx, pre, dact, w, dzx)


def _sc_fwd(proj, w, *, name):
    L = proj.shape[0]
    cb = 256
    nb = D // cb
    k = w.shape[0]

    def body(b_ref, c_ref, x_ref, w_ref, o_ref, v_ref):
        u = c_ref[...].astype(F32) * x_ref[...].astype(F32)
        v = u * w_ref[k - 1:k, :]
        for j in range(1, k):
            v = v + _shift_down(u, j) * w_ref[k - 1 - j:k - j, :]
        o_ref[...] = (b_ref[...].astype(F32) * v).astype(BF16)
        v_ref[...] = v.astype(BF16)

    blk = pl.BlockSpec((L, cb), lambda i: (0, i))
    return pl.pallas_call(
        body, grid=(nb,),
        in_specs=[blk, pl.BlockSpec((L, cb), lambda i: (0, i + nb)), pl.BlockSpec((L, cb), lambda i: (0, i + 2 * nb)),
                  pl.BlockSpec((k, cb), lambda i: (0, i))],
        out_specs=[blk, blk], out_shape=[jax.ShapeDtypeStruct((L, D), BF16)] * 2,
        compiler_params=_params(("parallel",)), name=name)(proj, proj, proj, w)


def _sc_bwd(proj, v, dyv, w, *, name):
    L = proj.shape[0]
    cb = 256
    nb = D // cb
    k = w.shape[0]

    def body(b_ref, c_ref, x_ref, v_ref, dy_ref, w_ref, dp_ref, s_ref):
        cv, xv = c_ref[...].astype(F32), x_ref[...].astype(F32)
        u = cv * xv
        dyv_ = dy_ref[...]
        dp_ref[0] = (dyv_ * v_ref[...].astype(F32)).astype(BF16)
        dv = dyv_ * b_ref[...].astype(F32)
        s_ref[...] = jnp.zeros_like(s_ref)
        s_ref[k - 1:k, :] = jnp.sum(dv * u, axis=0, keepdims=True)
        du = dv * w_ref[k - 1:k, :]
        for j in range(1, k):
            later = _shift_up(dv, j)
            du = du + later * w_ref[k - 1 - j:k - j, :]
            s_ref[k - 1 - j:k - j, :] = jnp.sum(later * u, axis=0, keepdims=True)
        dp_ref[1] = (du * xv).astype(BF16)
        dp_ref[2] = (du * cv).astype(BF16)

    blk = pl.BlockSpec((L, cb), lambda i: (0, i))
    return pl.pallas_call(
        body, grid=(nb,),
        in_specs=[blk, pl.BlockSpec((L, cb), lambda i: (0, i + nb)), pl.BlockSpec((L, cb), lambda i: (0, i + 2 * nb)),
                  blk, blk, pl.BlockSpec((k, cb), lambda i: (0, i))],
        out_specs=[pl.BlockSpec((3, L, cb), lambda i: (0, 0, i)), pl.BlockSpec((8, cb), lambda i: (0, i))],
        out_shape=[jax.ShapeDtypeStruct((3, L, D), BF16), jax.ShapeDtypeStruct((8, D), F32)],
        compiler_params=_params(("parallel",)), name=name)(proj, proj, proj, v, dyv, w)


def _pieces(v, n):
    out, rest = [], v
    for _ in range(n):
        out.append(rest.astype(BF16))
        rest = rest - out[-1].astype(F32)
    return out


def _cumsum_rows(mask, v):
    m = mask.astype(BF16)
    return _dot(jnp.concatenate([m, m, m], axis=1), jnp.concatenate(_pieces(v, 3), axis=0))


def _ssd_chunk_terms(dtr, prm):
    lane = lax.broadcasted_iota(jnp.int32, (CH, LANES), 1)
    valid = lane < NH
    xdt = dtr + prm[0:1, :]
    dt = jnp.where(valid, jnp.maximum(xdt, 0.0) + jnp.log1p(jnp.exp(-jnp.abs(xdt))), 0.0)
    A = -jnp.exp(prm[1:2, :])
    ri = lax.broadcasted_iota(jnp.int32, (CH, CH), 0)
    ci = lax.broadcasted_iota(jnp.int32, (CH, CH), 1)
    cs = _cumsum_rows(ri >= ci, dt * A)
    last = cs[CH - 1:CH, :]
    spread = (lax.broadcasted_iota(jnp.int32, (2 * LANES, DI), 1) // HP
              == lax.broadcasted_iota(jnp.int32, (2 * LANES, DI), 0) % LANES).astype(BF16)
    gather = ((lax.broadcasted_iota(jnp.int32, (LANES, 2 * DI), 1) % DI) // HP
              == lax.broadcasted_iota(jnp.int32, (LANES, 2 * DI), 0)).astype(BF16)
    return dict(valid=valid, xdt=xdt, dt=dt, A=A, cs=cs, csT=cs.T, last=last, ri=ri, ci=ci, ex=(spread, gather))


def _expand(v, ex):
    if v.shape[0] == 1:
        return _expand(jnp.broadcast_to(v, (8, LANES)), ex)[0:1, :]
    return _dot(jnp.concatenate(_pieces(v, 2), axis=1), ex[0])


def _head_sum(v, ex):
    if v.shape[0] == 1:
        return _head_sum(jnp.broadcast_to(v, (8, DI)), ex)[0:1, :]
    return _dot_nt(jnp.concatenate(_pieces(v, 2), axis=1), ex[1])


def _ssd_fwd(xbc, dtr, prm, *, name):
    L = xbc.shape[0]
    nc = L // CH

    def body(xbc_ref, dtr_ref, prm_ref, y_ref, sp_ref, st_ref):
        @pl.when(pl.program_id(0) == 0)
        def _():
            st_ref[...] = jnp.zeros_like(st_ref)

        prm_v = prm_ref[...]
        t = _ssd_chunk_terms(dtr_ref[...], prm_v)
        cs, csT, ex, causal = t["cs"], t["csT"], t["ex"], t["ri"] >= t["ci"]
        xs = xbc_ref[:, 0:DI].astype(F32)
        X = xs * _expand(t["dt"], ex)
        Xb = X.astype(BF16)
        Xd = (X * _expand(jnp.exp(t["last"] - cs), ex)).astype(BF16)
        Ex = _expand(jnp.exp(cs), ex)
        cdx = _expand(jnp.exp(t["last"]), ex)
        dskx = _expand(prm_v[2:3, :], ex)
        lane = lax.broadcasted_iota(jnp.int32, (CH, LANES), 1)
        sp_ref[0] = st_ref[...]
        for g in range(NG):
            Bg = xbc_ref[:, DI + g * NS:DI + (g + 1) * NS].astype(BF16)
            Cg = xbc_ref[:, DI + GW + g * NS:DI + GW + (g + 1) * NS].astype(BF16)
            G = _dot_nt(Cg, Bg)
            Sg = st_ref[:, g * GW:(g + 1) * GW]
            yoff = _dot(Cg, Sg.astype(BF16)) * Ex[:, g * GW:(g + 1) * GW]
            for j in range(GW // LANES):
                lo = g * GW + j * LANES
                Xp = Xb[:, lo:lo + LANES]
                yd = []
                for h in (lo // HP, lo // HP + 1):
                    seg = cs[:, h:h + 1] - csT[h:h + 1, :]
                    yd.append(_dot((G * jnp.where(causal, jnp.exp(seg), 0.0)).astype(BF16), Xp))
                y_ref[:, lo:lo + LANES] = (jnp.where(lane < HP, yd[0], yd[1]) + yoff[:, j * LANES:(j + 1) * LANES]
                                           + dskx[:, lo:lo + LANES] * xs[:, lo:lo + LANES]).astype(BF16)
            st_ref[:, g * GW:(g + 1) * GW] = Sg * cdx[:, g * GW:(g + 1) * GW] + _dot_tn(Bg, Xd[:, g * GW:(g + 1) * GW])

    return pl.pallas_call(
        body, grid=(nc,),
        in_specs=[pl.BlockSpec((CH, CONVD), lambda c: (c, 0)), pl.BlockSpec((CH, LANES), lambda c: (c, 0)),
                  pl.BlockSpec((8, LANES), lambda c: (0, 0))],
        out_specs=[pl.BlockSpec((CH, DI), lambda c: (c, 0)), pl.BlockSpec((1, NS, DI), lambda c: (c, 0, 0))],
        out_shape=[jax.ShapeDtypeStruct((L, DI), BF16), jax.ShapeDtypeStruct((nc, NS, DI), F32)],
        scratch_shapes=[pltpu.VMEM((NS, DI), F32)],
        compiler_params=_params(("arbitrary",)), name=name)(xbc, dtr, prm)


def _ssd_bwd(xbc, dtr, prm, dy, sprev, *, name):
    L = xbc.shape[0]
    nc = L // CH

    def body(xbc_ref, dtr_ref, prm_ref, dy_ref, sp_ref, dxbc_ref, ddtr_ref, s_ref, dst_ref, dx_scr, de_scr, dd_scr):
        step = pl.program_id(0)

        @pl.when(step == 0)
        def _():
            dst_ref[...] = jnp.zeros_like(dst_ref)
            s_ref[...] = jnp.zeros_like(s_ref)

        prm_v = prm_ref[...]
        t = _ssd_chunk_terms(dtr_ref[...], prm_v)
        cs, csT, ex, ri, ci = t["cs"], t["csT"], t["ex"], t["ri"], t["ci"]
        E = jnp.exp(cs)
        dec = jnp.exp(t["last"] - cs)
        cd = jnp.exp(t["last"])
        xs = xbc_ref[:, 0:DI].astype(F32)
        dtx = _expand(t["dt"], ex)
        X = xs * dtx
        Xb = X.astype(BF16)
        decx = _expand(dec, ex)
        Xd = (X * decx).astype(BF16)
        Ex = _expand(E, ex)
        cdx = _expand(cd, ex)
        dskx = _expand(prm_v[2:3, :], ex)
        lane = lax.broadcasted_iota(jnp.int32, (CH, LANES), 1)
        dcs = jnp.zeros((CH, LANES), F32)
        dcd_x = []
        for g in range(NG):
            gs = slice(g * GW, (g + 1) * GW)
            Bg = xbc_ref[:, DI + g * NS:DI + (g + 1) * NS].astype(BF16)
            Cg = xbc_ref[:, DI + GW + g * NS:DI + GW + (g + 1) * NS].astype(BF16)
            G = _dot_nt(Cg, Bg)
            GT = _dot_nt(Bg, Cg)
            Sg = sp_ref[0, :, gs]
            Sgb = Sg.astype(BF16)
            dyg = dy_ref[:, gs]
            de_scr[:, gs] = dyg * _dot(Cg, Sgb)
            dYo = (Ex[:, gs] * dyg).astype(BF16)
            dC = _dot_nt(dYo, Sgb)
            dS_in = _dot_tn(Cg, dYo)
            dStg = dst_ref[:, gs]
            dStb = dStg.astype(BF16)
            dXd = _dot(Bg, dStb)
            dB = _dot_nt(Xd[:, gs], dStb)
            dd_scr[:, gs] = dXd * X[:, gs]
            dXst = dXd * decx[:, gs]
            dG = jnp.zeros((CH, CH), F32)
            dGT = jnp.zeros((CH, CH), F32)
            for j in range(GW // LANES):
                lo = g * GW + j * LANES
                Xp = Xb[:, lo:lo + LANES]
                dyp = dy_ref[:, lo:lo + LANES]
                dXp = dXst[:, j * LANES:(j + 1) * LANES]
                for k, h in enumerate((lo // HP, lo // HP + 1)):
                    dyh = jnp.where((lane < HP) if k == 0 else (lane >= HP), dyp, 0.0).astype(BF16)
                    seg = cs[:, h:h + 1] - csT[h:h + 1, :]
                    Lm = jnp.where(ri >= ci, jnp.exp(seg), 0.0)
                    LmT = jnp.where(ci >= ri, jnp.exp(-seg), 0.0)
                    dM = _dot_nt(dyh, Xp)
                    dMT = _dot_nt(Xp, dyh)
                    MT = GT * LmT
                    rs = jnp.sum(dM * (G * Lm), axis=1, keepdims=True) - jnp.sum(dMT * MT, axis=1, keepdims=True)
                    dcs = dcs + jnp.where(lane == h, rs, 0.0)
                    dG = dG + dM * Lm
                    dGT = dGT + dMT * LmT
                    dXp = dXp + _dot(MT.astype(BF16), dyh)
                dx_scr[:, lo:lo + LANES] = dXp
            dxbc_ref[:, DI + g * NS:DI + (g + 1) * NS] = (dB + _dot(dGT.astype(BF16), Cg)).astype(BF16)
            dxbc_ref[:, DI + GW + g * NS:DI + GW + (g + 1) * NS] = (dC + _dot(dG.astype(BF16), Bg)).astype(BF16)
            dcd_x.append(jnp.sum(dStg * Sg, axis=0, keepdims=True))
            dst_ref[:, gs] = dStg * cdx[:, gs] + dS_in
        dX = dx_scr[...]
        dy = dy_ref[...]
        ddec = _head_sum(dd_scr[...], ex)
        dcd = _head_sum(jnp.concatenate(dcd_x, axis=1), ex)
        dcs = dcs + _head_sum(de_scr[...], ex) * E - ddec * dec
        row = lax.broadcasted_iota(jnp.int32, (CH, LANES), 0)
        dcs = dcs + jnp.where(row == CH - 1, jnp.sum(ddec * dec, axis=0, keepdims=True) + dcd * cd, 0.0)
        da = _cumsum_rows(ci >= ri, dcs)
        ddt = da * t["A"] + _head_sum(dX * xs, ex)
        ddtr = jnp.where(t["valid"], ddt * _sigmoid(t["xdt"]), 0.0)
        ddtr_ref[...] = ddtr
        dxbc_ref[:, 0:DI] = (dX * dtx + dskx * dy).astype(BF16)
        s_ref[0:1, :] += jnp.sum(da * t["dt"], axis=0, keepdims=True)
        s_ref[1:2, :] += _head_sum(jnp.sum(dy * xs, axis=0, keepdims=True), ex)
        s_ref[2:3, :] += jnp.sum(ddtr, axis=0, keepdims=True)

        @pl.when(step == nc - 1)
        def _():
            s_ref[0:1, :] = s_ref[0:1, :] * t["A"]

    rev = lambda c: (nc - 1 - c, 0)
    return pl.pallas_call(
        body, grid=(nc,),
        in_specs=[pl.BlockSpec((CH, CONVD), rev), pl.BlockSpec((CH, LANES), rev), pl.BlockSpec((8, LANES), lambda c: (0, 0)),
                  pl.BlockSpec((CH, DI), rev), pl.BlockSpec((1, NS, DI), lambda c: (nc - 1 - c, 0, 0))],
        out_specs=[pl.BlockSpec((CH, CONVD), rev), pl.BlockSpec((CH, LANES), rev), pl.BlockSpec((8, LANES), lambda c: (0, 0))],
        out_shape=[jax.ShapeDtypeStruct((L, CONVD), BF16), jax.ShapeDtypeStruct((L, LANES), F32),
                   jax.ShapeDtypeStruct((8, LANES), F32)],
        scratch_shapes=[pltpu.VMEM((NS, DI), F32), pltpu.VMEM((CH, DI), F32), pltpu.VMEM((CH, DI), F32),
                        pltpu.VMEM((CH, DI), F32)],
        compiler_params=_params(("arbitrary",)), name=name)(xbc, dtr, prm, dy, sprev)


def _gnorm_fwd(y, zx, nw, *, name):
    L = y.shape[0]
    tm = min(L, 256)

    def body(y_ref, z_ref, nw_ref, o_ref):
        z = z_ref[...].astype(F32)
        yg = y_ref[...].astype(F32) * (z * _sigmoid(z))
        for g in range(NG):
            v = yg[:, g * GW:(g + 1) * GW]
            r = lax.rsqrt(jnp.mean(v * v, axis=-1, keepdims=True) + EPS)
            o_ref[:, g * GW:(g + 1) * GW] = (v * r * nw_ref[:, g * GW:(g + 1) * GW]).astype(BF16)

    row = pl.BlockSpec((tm, DI), lambda i: (i, 0))
    return pl.pallas_call(body, grid=(L // tm,), in_specs=[row, row, pl.BlockSpec((1, DI), lambda i: (0, 0))],
                          out_specs=row, out_shape=jax.ShapeDtypeStruct((L, DI), BF16),
                          compiler_params=_params(("parallel",)), name=name)(y, zx, nw)


def _gnorm_bwd(y, zx, nw, dyn, *, name):
    L = y.shape[0]
    tm = min(L, 256)

    def body(y_ref, z_ref, nw_ref, dyn_ref, dy_ref, dz_ref, s_ref):
        @pl.when(pl.program_id(0) == 0)
        def _():
            s_ref[...] = jnp.zeros_like(s_ref)

        z, yv = z_ref[...].astype(F32), y_ref[...].astype(F32)
        sz = _sigmoid(z)
        gate = z * sz
        dgate_dz = sz * (1.0 + z * (1.0 - sz))
        for g in range(NG):
            gs = slice(g * GW, (g + 1) * GW)
            v = yv[:, gs] * gate[:, gs]
            r = lax.rsqrt(jnp.mean(v * v, axis=-1, keepdims=True) + EPS)
            vhat = v * r
            dn = dyn_ref[:, gs].astype(F32)
            s_ref[0:1, gs] += jnp.sum(dn * vhat, axis=0, keepdims=True)
            dvhat = dn * nw_ref[:, gs]
            dv = r * (dvhat - vhat * jnp.mean(dvhat * vhat, axis=-1, keepdims=True))
            dy_ref[:, gs] = dv * gate[:, gs]
            dz_ref[:, gs] = (dv * yv[:, gs] * dgate_dz[:, gs]).astype(BF16)

    row = pl.BlockSpec((tm, DI), lambda i: (i, 0))
    return pl.pallas_call(body, grid=(L // tm,), in_specs=[row, row, pl.BlockSpec((1, DI), lambda i: (0, 0)), row],
                          out_specs=[row, row, pl.BlockSpec((8, DI), lambda i: (0, 0))],
                          out_shape=[jax.ShapeDtypeStruct((L, DI), F32), jax.ShapeDtypeStruct((L, ZX), BF16),
                                     jax.ShapeDtypeStruct((8, DI), F32)],
                          compiler_params=_params(("arbitrary",)), name=name)(y, zx, nw, dyn)


def _adamw(w, g, m, v, *, name, g_row=0, w_row=0, rows=None, into=None, emit_g=False):
    lead = w.ndim == 3
    R, C = w.shape[-2:]
    rows = R if rows is None else rows
    tr = max([t for t in range(8, rows + 1, 8) if rows % t == 0 and t * C <= 256 * 1024], default=rows)
    assert g_row % tr == 0 and w_row % tr == 0, (name, g_row, w_row, tr)
    n_out = 4 if emit_g else 3

    def body(w_ref, g_ref, m_ref, v_ref, *rest):
        outs = rest[-n_out:]
        gv = g_ref[...]
        mn = ADAM_B1 * m_ref[...] + (1.0 - ADAM_B1) * gv
        vn = ADAM_B2 * v_ref[...] + (1.0 - ADAM_B2) * (gv * gv)
        m_hat = mn / (1.0 - ADAM_B1 ** ADAM_STEP)
        v_hat = vn / (1.0 - ADAM_B2 ** ADAM_STEP)
        d_ref, mo_ref, vo_ref = outs[-3:]
        d_ref[...] = -ADAM_LR * (m_hat / (jnp.sqrt(v_hat) + ADAM_EPS) + ADAM_WD * w_ref[...])
        mo_ref[...] = mn
        vo_ref[...] = vn
        if emit_g:
            outs[0][...] = gv

    blk = (pl.BlockSpec((None, tr, C), lambda i: (0, i + w_row // tr, 0)) if lead
           else pl.BlockSpec((tr, C), lambda i: (i + w_row // tr, 0)))
    args, in_specs, alias = [w, g, m, v], [blk, pl.BlockSpec((tr, C), lambda i: (i + g_row // tr, 0)), blk, blk], {}
    if into is not None:
        args, in_specs, alias = args + list(into), in_specs + [ANY] * n_out, {4 + k: k for k in range(n_out)}
    return pl.pallas_call(body, grid=(rows // tr,), in_specs=in_specs, out_specs=[blk] * n_out,
                          out_shape=[jax.ShapeDtypeStruct(w.shape, F32)] * n_out, input_output_aliases=alias,
                          compiler_params=_params(("parallel",)), name=name)(*args)


def _residual(acc, xv, gv):
    return xv + gv * acc, acc


def _like(buf):
    return jax.ShapeDtypeStruct(buf.shape, buf.dtype)


def _mlp_fwd(x, mod, nw, wb, up_row, down_row, tag):
    sh, sc, g = mod
    h = _modnorm_fwd(x, nw, sc, sh, name=tag + "_norm")
    a = _matmul(h, wb, n=DFF, tm=TM_ALL, b_spec=pl.BlockSpec((None, D, 512), lambda mi, j: (j // 2, up_row // D, j % 2)),
                epi=lambda acc: (jnp.maximum(acc, 0.0),), out_dtypes=(BF16,), name=tag + "_up")
    xn, y = _matmul(a, wb, n=D, tm=TM_HALF, contract=_nn_split_sq,
                    b_spec=pl.BlockSpec((N_CHIPS, D, 512), lambda mi, j: (0, down_row // D, j)),
                    extras=(x, g), epi=_residual, out_dtypes=(F32, BF16), name=tag + "_down")
    return xn, (x, h, a, y)


def _mlp_bwd(dxo, dy, gsum, saved, mod, nw, wb, gb, up_row, down_row, below, tag, midway=None):
    x, h, a, y = saved
    sh, sc, g = mod
    du = _matmul(dy, wb, n=DFF, tm=TM_ALL, contract=_nt,
                 b_spec=pl.BlockSpec((None, 512, D), lambda mi, j: (j // 2, down_row // 512 + j % 2, 0)),
                 extras=(a,), epi=lambda acc, av: (acc * (2.0 * av.astype(F32)),), out_dtypes=(BF16,), name=tag + "_dact")
    if midway is not None:
        nw = nw + midway(du)[0:1, 0:1]
    gb = _matmul_tn(a, dy, m=DFF, n=D, tm=D, tn=D, a_square=True, into=gb, out_struct=_like(wb),
                    out_spec=pl.BlockSpec((None, D, D), lambda mi, j: (mi, down_row // D, 0)), name=tag + "_ddown")
    dh = _matmul(du, wb, n=D, tm=TM_HALF, contract=_nt_split,
                 b_spec=pl.BlockSpec((N_CHIPS, 512, D), lambda mi, j: (0, up_row // 512 + j, 0)), name=tag + "_dh")
    gb = _matmul_tn(h, du, m=D, n=DFF, tm=D, into=gb, out_struct=_like(wb),
                    out_spec=pl.BlockSpec((None, D, 512), lambda mi, j: (j // 2, up_row // D, j % 2)), name=tag + "_dup")
    dx, sums, *nxt = _modnorm_bwd(x, dh, dxo, nw, sc, gsum, below, name=tag + "_dnorm")
    return dx, gb, sums, *nxt


def _ssd_fwd_scan(x, mod, nw, w_zx, w_dt, conv_w, conv_b, prm, tag):
    sh, sc, g = mod
    h = _modnorm_fwd(x, nw, sc, sh, name=tag + "_norm")
    zx = _matmul(h, w_zx, n=ZX, tm=TM_ALL, out_dtypes=(BF16,), name=tag + "_in")
    dtr = _matmul(h, w_dt, n=LANES, tm=TM_ALL, name=tag + "_in_dt")
    xbc, pre = _ssd_conv_fwd(zx, conv_w, conv_b, name=tag + "_conv")
    y, sprev = _ssd_fwd(xbc, dtr, prm, name=tag + "_scan")
    return h, zx, dtr, xbc, y, sprev, pre


def _ssd_fwd_out(x, mod, scan, gn_w, w_out, tag):
    sh, sc, g = mod
    h, zx, dtr, xbc, y, sprev, pre = scan
    yn = _gnorm_fwd(y, zx, gn_w, name=tag + "_gnorm")
    xn, yo = _matmul(yn, w_out, n=D, tm=TM_HALF, contract=_nn_split,
                     b_spec=pl.BlockSpec((N_CHIPS, 512, 512), lambda mi, j: (0, 0, j)),
                     extras=(x, g), epi=_residual, out_dtypes=(F32, BF16), name=tag + "_out")
    return xn, (x, h, zx, dtr, xbc, y, sprev, yn, yo, pre)


def _ssd_bwd_out(dyo, saved, w_out, tag):
    x, h, zx, dtr, xbc, y, sprev, yn, yo, pre = saved
    dyn = _matmul(dyo, w_out, n=DI, tm=TM_ALL, contract=_nt, b_spec=pl.BlockSpec((None, 512, D), lambda mi, j: (j, 0, 0)),
                  out_dtypes=(BF16,), name=tag + "_dyn")
    g_out = _matmul_tn(yn, dyo, m=DI, n=D, tn=D, out_struct=_like(w_out),
                       out_spec=pl.BlockSpec((None, 512, D), lambda mi, j: (mi, 0, 0)), name=tag + "_dout")
    return dyn, g_out


def _ssd_bwd_rest(dxo, dy, dzx, gsum, saved, mod, nw, w_zx, w_dt, conv_w, prm, tag):
    x, h, zx, dtr, xbc, y, sprev, yn, yo, pre = saved
    sh, sc, g = mod
    dxbc, ddtr, ssum = _ssd_bwd(xbc, dtr, prm, dy, sprev, name=tag + "_dscan")
    dzx, csum = _ssd_conv_bwd(zx, pre, dxbc, conv_w, dzx, name=tag + "_dconv")
    dh_dt = _matmul(ddtr, w_dt, n=D, tm=TM_ALL, contract=_nt, name=tag + "_dh_dt")
    dh = _matmul(dzx, w_zx, n=D, tm=TM_HALF, contract=_nt, extras=(dh_dt,), epi=lambda acc, e: (acc + e,), name=tag + "_dh")
    d_w_zx = _matmul_tn(h, dzx, m=D, n=ZX, tm=D, name=tag + "_din")
    d_w_dt = _matmul_tn(h, ddtr, m=D, n=LANES, tm=D, name=tag + "_din_dt")
    dx, sums = _modnorm_bwd(x, dh, dxo, nw, sc, gsum, None, name=tag + "_dnorm")
    return dx, d_w_zx, d_w_dt, sums, csum, ssum


def _sc_layer_fwd(x, mod, nw, w_sc_in, conv_w, wb, out_row, tag):
    sh, sc, g = mod
    h = _modnorm_fwd(x, nw, sc, sh, name=tag + "_norm")
    proj = _matmul(h, w_sc_in, n=3 * D, tm=TM_ALL, tn=256, out_dtypes=(BF16,),
                   b_spec=pl.BlockSpec((None, D, 256), lambda mi, j: (j // 3, 0, j % 3)),
                   name=tag + "_in")
    yv, v = _sc_fwd(proj, conv_w, name=tag + "_conv")
    xn, yo = _matmul(yv, wb, n=D, tm=TM_HALF, contract=_nn_split,
                     b_spec=pl.BlockSpec((N_CHIPS, 256, 512), lambda mi, j: (0, out_row // 256, j)),
                     extras=(x, g), epi=_residual, out_dtypes=(F32, BF16), name=tag + "_out")
    return xn, (x, h, proj, yv, yo, v)


def _sc_layer_bwd(dxo, dyo, gsum, saved, mod, nw, w_sc_in, conv_w, wb, gb, out_row, below, tag):
    x, h, proj, yv, yo, v = saved
    sh, sc, g = mod
    L = x.shape[0]
    dyv = _matmul(dyo, wb, n=D, tm=TM_ALL, tn=256, contract=_nt,
                  b_spec=pl.BlockSpec((None, 256, D), lambda mi, j: (j, out_row // 256, 0)), name=tag + "_dyv")
    gb = _matmul_tn(yv, dyo, m=D, n=D, tm=256, tn=D, into=gb, out_struct=_like(wb),
                    out_spec=pl.BlockSpec((None, 256, D), lambda mi, j: (mi, out_row // 256, 0)), name=tag + "_dout")
    dproj, csum = _sc_bwd(proj, v, dyv, conv_w, name=tag + "_dconv")
    tm = min(L, TM_HALF)
    dh = _matmul(dproj, w_sc_in, n=D, tm=tm, contract=_nt_sc_in, a_spec=pl.BlockSpec((3, tm, D), lambda mi, j: (0, mi, 0)),
                 b_spec=pl.BlockSpec((N_CHIPS, 512, SC_IN_SHARD), lambda mi, j: (0, j, 0)), name=tag + "_dh")
    g_sc_in = _matmul_tn(h, dproj, m=D, n=3 * D, tm=D, tn=256, b_spec=pl.BlockSpec((None, L, 256), lambda mi, j: (j // 4, 0, j % 4)),
                         out_spec=pl.BlockSpec((None, D, 256), lambda mi, j: (j // 3, 0, j % 3)),
                         out_struct=jax.ShapeDtypeStruct((N_CHIPS, D, SC_IN_SHARD), BF16), name=tag + "_din")
    dx, sums, *nxt = _modnorm_bwd(x, dh, dxo, nw, sc, gsum, below, name=tag + "_dnorm")
    return dx, gb, g_sc_in, sums, csum, *nxt


SUB_ROW = (0, 8, 16, 24)
SSD_CONV_ROW, GNORM_ROW, FINAL_ROW, SC_CONV_ROW, HEAD_ROW, SMALL_ROWS = 32, 56, 72, 80, 88, 96


def _all_gather_rows(blk, *, name):
    m_per, n = blk.shape

    def body(x_ref, out_ref, send_sems, recv_sems, local_sem):
        x, y, c = lax.axis_index("x"), lax.axis_index("y"), lax.axis_index("c")
        me, sibling = (x, y, c), (x, y, 1 - c)
        chips = [(1 - x, y), (x, 1 - y), (1 - x, 1 - y)]

        def rows(px, py, pc):
            return out_ref.at[pl.ds((4 * px + 2 * py + pc) * m_per, m_per), :]

        def copy(k, block, to, src=None):
            return pltpu.make_async_remote_copy(src_ref=rows(*block) if src is None else src, dst_ref=rows(*block),
                                                send_sem=send_sems.at[k], recv_sem=recv_sems.at[k], device_id=to,
                                                device_id_type=MESH)

        mine = pltpu.make_async_copy(x_ref, rows(*me), local_sem)
        mine.start()
        first = [copy(0, me, sibling, src=x_ref)] + [copy(1 + j, me, (*chip, c), src=x_ref) for j, chip in enumerate(chips)]
        for cp in first:
            cp.start()
        passed = [copy(4 + j, (*chip, c), sibling) for j, chip in enumerate(chips)]
        for j, chip in enumerate(chips):
            copy(1 + j, (*chip, c), me).wait_recv()
            passed[j].start()
        copy(0, sibling, me).wait_recv()
        for j, chip in enumerate(chips):
            copy(4 + j, (*chip, 1 - c), me).wait_recv()
        for cp in first + passed:
            cp.wait_send()
        mine.wait()

    return pl.pallas_call(
        body, out_shape=jax.ShapeDtypeStruct((N_DEV * m_per, n), blk.dtype),
        in_specs=[pl.BlockSpec(memory_space=pltpu.VMEM)], out_specs=pl.BlockSpec(memory_space=pltpu.VMEM),
        scratch_shapes=[pltpu.SemaphoreType.DMA((7,)), pltpu.SemaphoreType.DMA((7,)), pltpu.SemaphoreType.DMA],
        name=name)(blk)


def _half(ref, chip, c):
    hr = ref.shape[1] // 2
    return ref.at[chip, pl.ds(c * hr, hr), :]


def _gather_copy(bufs, sends, recvs, b, k, chip, pc, to):
    piece = _half(bufs[b], 2 * chip[0] + chip[1], pc)
    return pltpu.make_async_remote_copy(src_ref=piece, dst_ref=piece, send_sem=sends.at[4 * b + k], recv_sem=recvs.at[4 * b + k],
                                        device_id=to, device_id_type=MESH)


def _split_call(body, bufs, sems_in, n_sems, *, name, after=(), token=False, lands=()):
    nb, na, nl, starts = len(bufs), len(after), len(lands), not sems_in

    def wrapped(*refs):
        sems = refs[nb + na:nb + na + 2] if starts else refs[nb:nb + 2]
        made = refs[nb + na + 2 + nb:nb + na + 2 + nb + nl] if starts else ()
        body(tuple(refs[:nb]) + tuple(made), sems[0], sems[1])
        if token:
            refs[-1][...] = jnp.zeros_like(refs[-1])

    out_shape = [pltpu.SemaphoreType.DMA((n_sems,)) for _ in range(2 if starts else 0)]
    out_specs = [SEM] * len(out_shape) + [ANY] * (nb + nl)
    alias = {b: len(out_shape) + b for b in range(nb)}
    out_shape += [jax.ShapeDtypeStruct(b.shape, b.dtype) for b in bufs] + list(lands)
    if token:
        out_shape.append(jax.ShapeDtypeStruct((8, LANES), F32))
        out_specs.append(pl.BlockSpec(memory_space=pltpu.VMEM))
    return pl.pallas_call(
        wrapped, out_shape=out_shape, in_specs=[ANY] * nb + [SEM] * len(sems_in) + [ANY] * na, out_specs=out_specs,
        input_output_aliases=alias,
        compiler_params=pltpu.CompilerParams(has_side_effects=pltpu.SideEffectType.DATAFLOW_SIDE_EFFECTING),
        name=name)(*bufs, *sems_in, *after)


def _gather_start(bufs, *, name, after=()):
    nb = len(bufs)

    def body(ins, sends, recvs):
        x, y, c = lax.axis_index("x"), lax.axis_index("y"), lax.axis_index("c")
        chips = [(1 - x, y), (x, 1 - y), (1 - x, 1 - y)]
        for b in range(nb):
            _gather_copy(ins, sends, recvs, b, 0, (x, y), c, (x, y, 1 - c)).start()
            for j, chip in enumerate(chips):
                _gather_copy(ins, sends, recvs, b, 1 + j, (x, y), c, (*chip, c)).start()

    out = _split_call(body, bufs, (), 4 * nb, name=name, after=after, token=True)
    return (out[0], out[1], out[2:2 + nb]), out[-1]


def _gather_wait_first(flight, *, name, after=()):
    sends, recvs, bufs = flight
    nb = len(bufs)

    def body(ins, sends_, recvs_):
        x, y, c = lax.axis_index("x"), lax.axis_index("y"), lax.axis_index("c")
        chips = [(1 - x, y), (x, 1 - y), (1 - x, 1 - y)]
        for b in range(nb):
            _gather_copy(ins, sends_, recvs_, b, 0, (x, y), c, (x, y, 1 - c)).wait_send()
            _gather_copy(ins, sends_, recvs_, b, 0, (x, y), 1 - c, (x, y, c)).wait_recv()
            for j, chip in enumerate(chips):
                _gather_copy(ins, sends_, recvs_, b, 1 + j, (x, y), c, (*chip, c)).wait_send()
                _gather_copy(ins, sends_, recvs_, b, 1 + j, chip, c, (x, y, c)).wait_recv()

    return _split_call(body, bufs, (sends, recvs), 4 * nb, name=name, after=after)


def _gather_forward(bufs, *, name):
    nb = len(bufs)

    def body(ins, sends, recvs):
        x, y, c = lax.axis_index("x"), lax.axis_index("y"), lax.axis_index("c")
        chips = [(1 - x, y), (x, 1 - y), (1 - x, 1 - y)]
        for b in range(nb):
            for j, chip in enumerate(chips):
                _gather_copy(ins, sends, recvs, b, 1 + j, chip, c, (x, y, 1 - c)).start()

    out = _split_call(body, bufs, (), 4 * nb, name=name)
    return out[0], out[1], out[2:2 + nb]


def _gather_wait_forward(flight, *, name, after=()):
    sends, recvs, bufs = flight
    nb = len(bufs)

    def body(ins, sends_, recvs_):
        x, y, c = lax.axis_index("x"), lax.axis_index("y"), lax.axis_index("c")
        chips = [(1 - x, y), (x, 1 - y), (1 - x, 1 - y)]
        for b in range(nb):
            for j, chip in enumerate(chips):
                _gather_copy(ins, sends_, recvs_, b, 1 + j, chip, c, (x, y, 1 - c)).wait_send()
                _gather_copy(ins, sends_, recvs_, b, 1 + j, chip, 1 - c, (x, y, c)).wait_recv()

    return _split_call(body, bufs, (sends, recvs), 4 * nb, name=name, after=after)


def _owner_copies(hs, lands, sends, recvs):
    x, y, c = lax.axis_index("x"), lax.axis_index("y"), lax.axis_index("c")
    chips = [(1 - x, y), (x, 1 - y), (1 - x, 1 - y)]
    return [pltpu.make_async_remote_copy(src_ref=hs[b].at[2 * cx + cy], dst_ref=lands[b].at[j], send_sem=sends.at[3 * b + j],
                                         recv_sem=recvs.at[3 * b + j], device_id=(cx, cy, c), device_id_type=MESH)
            for b in range(len(hs)) for j, (cx, cy) in enumerate(chips)]


def _owners_start(hs, *, name):
    nb = len(hs)
    lands = [jax.ShapeDtypeStruct((3,) + h.shape[1:], h.dtype) for h in hs]

    def body(refs, sends, recvs):
        for cp in _owner_copies(refs[:nb], refs[nb:], sends, recvs):
            cp.start()

    out = _split_call(body, list(hs), (), 3 * nb, name=name, token=True, lands=lands)
    return (out[0], out[1], out[2:2 + 2 * nb]), out[-1]


def _owners_wait(flight, *, name, after=()):
    sends, recvs, bufs = flight
    nb = len(bufs) // 2

    def body(refs, sends_, recvs_):
        for cp in _owner_copies(refs[:nb], refs[nb:], sends_, recvs_):
            cp.wait()

    out = _split_call(body, bufs, (sends, recvs), 3 * nb, name=name, after=after)
    return out[:nb], out[nb:]


def _sibling_copies(gs, lands, sends, recvs):
    x, y, c = lax.axis_index("x"), lax.axis_index("y"), lax.axis_index("c")
    copies = []
    for b in range(len(gs)):
        hr = gs[b].shape[1] // 2
        copies.append(pltpu.make_async_remote_copy(
            src_ref=gs[b].at[:, pl.ds((1 - c) * hr, hr), :], dst_ref=lands[b], send_sem=sends.at[b], recv_sem=recvs.at[b],
            device_id=(x, y, 1 - c), device_id_type=MESH))
    return copies


def _sibling_start(gs, *, name, after=()):
    nb = len(gs)
    lands = [jax.ShapeDtypeStruct((g.shape[0], g.shape[1] // 2, g.shape[2]), g.dtype) for g in gs]

    def body(refs, sends, recvs):
        for cp in _sibling_copies(refs[:nb], refs[nb:], sends, recvs):
            cp.start()

    out = _split_call(body, list(gs), (), nb, name=name, after=after, token=True, lands=lands)
    return (out[0], out[1], out[2:2 + 2 * nb]), out[-1]


def _sibling_wait(flight, *, name, after=()):
    sends, recvs, bufs = flight
    nb = len(bufs) // 2

    def body(refs, sends_, recvs_):
        for cp in _sibling_copies(refs[:nb], refs[nb:], sends_, recvs_):
            cp.wait()

    out = _split_call(body, bufs, (sends, recvs), nb, name=name, after=after)
    return out[:nb], out[nb:]


def _result_copies(ts, sends, recvs):
    x, y, c = lax.axis_index("x"), lax.axis_index("y"), lax.axis_index("c")
    return [pltpu.make_async_remote_copy(src_ref=ts[b].at[c], dst_ref=ts[b].at[c], send_sem=sends.at[b], recv_sem=recvs.at[b],
                                         device_id=(x, y, 1 - c), device_id_type=MESH) for b in range(len(ts))]


def _result_start(ts, *, name):
    def body(refs, sends, recvs):
        for cp in _result_copies(refs, sends, recvs):
            cp.start()

    out = _split_call(body, ts, (), len(ts), name=name, token=True)
    return (out[0], out[1], out[2:2 + len(ts)]), out[-1]


def _result_wait(flight, *, name, after=()):
    sends, recvs, bufs = flight

    def body(refs, sends_, recvs_):
        for cp in _result_copies(refs, sends_, recvs_):
            cp.wait()

    return _split_call(body, bufs, (sends, recvs), len(bufs), name=name, after=after)


def _row_tile(rows, cols):
    best = 16
    for t in range(16, rows + 1, 16):
        if rows % t == 0 and t * cols <= 640 * 1024:
            best = t
    assert rows % best == 0, (rows, cols)
    return best


def _add_sibling_half(g, recv, core, *, name):
    nk, r, n = g.shape
    hr = r // 2
    tr = _row_tile(hr, n)

    def body(c_ref, a_ref, b_ref, o_ref):
        o_ref[...] = (a_ref[...].astype(F32) + b_ref[...].astype(F32)).astype(BF16)

    grid_spec = pltpu.PrefetchScalarGridSpec(
        num_scalar_prefetch=1, grid=(nk, hr // tr),
        in_specs=[pl.BlockSpec((None, tr, n), lambda k, i, c_ref: (k, c_ref[0] * (hr // tr) + i, 0)),
                  pl.BlockSpec((None, tr, n), lambda k, i, c_ref: (k, i, 0))],
        out_specs=pl.BlockSpec((None, tr, n), lambda k, i, c_ref: (k, i, 0)))
    return pl.pallas_call(body, grid_spec=grid_spec, out_shape=jax.ShapeDtypeStruct((nk, hr, n), BF16),
                          compiler_params=_params(("parallel", "parallel")), name=name)(core, g, recv)


def _add_chip_sums(h, recv, chip_core, *, name):
    _, hr, n = h.shape
    tr = _row_tile(hr, n)

    def body(k_ref, a_ref, b_ref, o_ref):
        o_ref[...] = ((a_ref[...].astype(F32) + b_ref[0].astype(F32)) + b_ref[1].astype(F32)) + b_ref[2].astype(F32)

    grid_spec = pltpu.PrefetchScalarGridSpec(
        num_scalar_prefetch=1, grid=(hr // tr,),
        in_specs=[pl.BlockSpec((None, tr, n), lambda i, k_ref: (k_ref[0], i, 0)),
                  pl.BlockSpec((3, tr, n), lambda i, k_ref: (0, i, 0))],
        out_specs=pl.BlockSpec((None, tr, n), lambda i, k_ref: (k_ref[1], i, 0)))
    return pl.pallas_call(body, grid_spec=grid_spec, out_shape=jax.ShapeDtypeStruct((2, hr, n), F32),
                          compiler_params=_params(("parallel",)), name=name)(chip_core, h, recv)


def _sum_devices(g, *, name):
    nd, r, n = g.shape

    def body(g_ref, o_ref):
        acc = g_ref[0]
        for i in range(1, nd):
            acc = acc + g_ref[i]
        o_ref[...] = acc

    return pl.pallas_call(body, out_shape=jax.ShapeDtypeStruct((r, n), F32), name=name)(g)


def _own_slot(parts, chip, *, name):
    rows, cols = sum(w.shape[1] for w, _ in parts), parts[0][0].shape[2]
    buf, row0 = None, 0
    for p, (w, idx) in enumerate(parts):
        r = w.shape[1]
        tr = min(r, 256)
        assert r % tr == 0 and row0 % tr == 0, (name, r, row0)

        def body(chip_ref, w_ref, *rest):
            rest[-1][...] = w_ref[...].astype(BF16)

        grid_spec = pltpu.PrefetchScalarGridSpec(
            num_scalar_prefetch=1, grid=(r // tr,),
            in_specs=[pl.BlockSpec((None, tr, cols), lambda i, c_ref, idx=idx: (idx, i, 0))] + ([] if buf is None else [ANY]),
            out_specs=pl.BlockSpec((None, tr, cols), lambda i, c_ref, row0=row0, tr=tr: (c_ref[0], row0 // tr + i, 0)))
        buf = pl.pallas_call(body, grid_spec=grid_spec, out_shape=jax.ShapeDtypeStruct((N_CHIPS, rows, cols), BF16),
                             input_output_aliases={} if buf is None else {2: 0}, compiler_params=_params(("parallel",)),
                             name=f"{name}{p}")(chip, w, *(() if buf is None else (buf,)))
        row0 += r
    return buf


def kernel(x, c, ada_w, ada_b, mix_norm_w, mlp_norm_w, mlp_up, mlp_down, ssd_in_w, ssd_conv_w, ssd_conv_b, ssd_dt_bias, ssd_A_log, ssd_D, ssd_norm_w, ssd_out_w, sc_in_w, sc_conv_w, sc_out_w, final_norm_w, loss_target, m_ada_w, m_ada_b, m_mix_norm_w, m_mlp_norm_w, m_mlp_up, m_mlp_down, m_ssd_in_w, m_ssd_conv_w, m_ssd_conv_b, m_ssd_dt_bias, m_ssd_A_log, m_ssd_D, m_ssd_norm_w, m_ssd_out_w, m_sc_in_w, m_sc_conv_w, m_sc_out_w, m_final_norm_w, v_ada_w, v_ada_b, v_mix_norm_w, v_mlp_norm_w, v_mlp_up, v_mlp_down, v_ssd_in_w, v_ssd_conv_w, v_ssd_conv_b, v_ssd_dt_bias, v_ssd_A_log, v_ssd_D, v_ssd_norm_w, v_ssd_out_w, v_sc_in_w, v_sc_conv_w, v_sc_out_w, v_final_norm_w):
    xi, yi, ci = lax.axis_index("x"), lax.axis_index("y"), lax.axis_index("c")
    chip = 2 * xi + yi
    dev = 2 * chip + ci
    n_ada = ada_w.shape[2]

    conv_flat = jnp.concatenate([ssd_conv_w.reshape(-1), sc_conv_w.reshape(-1), jnp.zeros((256,), F32)]).reshape(4, D)
    blk0 = jnp.concatenate([c, conv_flat, jnp.zeros((3, D), F32)], axis=0)
    got0 = _all_gather_rows(blk0, name="gather_cond").reshape(N_DEV, 8, D)
    c_all = got0[:, 0]
    conv_all = got0[0::2, 1:5].reshape(N_CHIPS, 4 * D)
    ssd_conv = jnp.moveaxis(conv_all[:, :4 * 768].reshape(N_CHIPS, 4, 768), 0, 1).reshape(4, CONVD)
    sc_conv = jnp.moveaxis(conv_all[:, 4 * 768:4 * 768 + 3 * 256].reshape(N_CHIPS, 3, 256), 0, 1).reshape(3, D)
    mod_shard = [_matmul(c_all, ada_w, n=n_ada, a_silu=True, b_spec=pl.BlockSpec((None, D, 512), lambda mi, j, i=i: (i, 0, j)),
                         extras=(lax.dynamic_slice(ada_b, (i, chip * n_ada), (1, n_ada)),),
                         epi=lambda acc, b: (acc + b,), name=f"ada_mod{i}") for i in range(2)]
    mod_all = _all_gather_rows(jnp.concatenate(mod_shard, axis=0), name="gather_mod")
    mod_all = mod_all.reshape(N_DEV, 2, N_DEV, n_ada)[0::2]
    mod = jnp.moveaxis(lax.dynamic_index_in_dim(mod_all, dev, axis=2, keepdims=False), 0, 1).reshape(2, 6, D)
    mods = [[mod[i, j:j + 1] for j in range(6)] for i in range(2)]

    up_row, down_row, sc_out_row = 0, D, 2 * D
    chip1 = chip.reshape(1).astype(jnp.int32)
    a_bufs = [_own_slot([(ssd_in_w, 0)], chip1, name="slot_ssd_in")]
    b_bufs = [_own_slot([(ssd_out_w, 0)], chip1, name="slot_ssd_out"),
              _own_slot([(mlp_up, 0), (mlp_down, 0)], chip1, name="slot_mlp0_")]
    c_bufs = [_own_slot([(sc_in_w, 0)], chip1, name="slot_sc_in"),
              _own_slot([(mlp_up, 1), (mlp_down, 1), (sc_out_w, 0)], chip1, name="slot_layer1_")]
    fly_a, tok = _gather_start(a_bufs, name="gather_a_start", after=(mod,))
    fly_b, tok = _gather_start(b_bufs, name="gather_b_start", after=(tok,))
    fly_c, tok = _gather_start(c_bufs, name="gather_c_start", after=(tok,))

    row = lambda v: v.reshape(1, -1)
    xs, tgt = x[0], loss_target[0]
    prm = jnp.pad(jnp.concatenate([ssd_dt_bias, ssd_A_log, ssd_D, jnp.zeros((5, NH), F32)], axis=0), ((0, 0), (0, LANES - NH)))
    mix_nw = [row(mix_norm_w[i]) for i in range(2)]
    mlp_nw = [row(mlp_norm_w[i]) for i in range(2)]
    a_bufs = _gather_wait_first(fly_a, name="gather_a_landed", after=(tok,))
    (w_ssd_in,) = _gather_wait_forward(_gather_forward(a_bufs, name="gather_a_pass"), name="gather_a_done")
    ssd_in_full = jnp.moveaxis(w_ssd_in, 0, 1).reshape(D, N_CHIPS * SSD_IN_SHARD)
    w_zx, w_dt = ssd_in_full[:, :ZX], jnp.pad(ssd_in_full[:, ZX:], ((0, 0), (0, LANES - NH)))
    scan = _ssd_fwd_scan(xs, mods[0][0:3], mix_nw[0], w_zx, w_dt, ssd_conv, ssd_conv_b, prm, "ssd")
    fly_b = _gather_forward(_gather_wait_first(fly_b, name="gather_b_landed", after=(scan[3],)), name="gather_b_pass")
    w_ssd_out, w_b = _gather_wait_forward(fly_b, name="gather_b_done", after=(scan[4],))
    x1, s_ssd = _ssd_fwd_out(xs, mods[0][0:3], scan, ssd_norm_w, w_ssd_out, "ssd")
    x2, s_mlp0 = _mlp_fwd(x1, mods[0][3:6], mlp_nw[0], w_b, up_row, down_row, "mlp0")
    c_bufs = _gather_wait_first(fly_c, name="gather_c_landed", after=(x2,))
    w_sc_in, w_c = _gather_wait_forward(_gather_forward(c_bufs, name="gather_c_pass"), name="gather_c_done")
    x3, s_sc = _sc_layer_fwd(x2, mods[1][0:3], mix_nw[1], w_sc_in, sc_conv, w_c, sc_out_row, "sc")
    x4, s_mlp1 = _mlp_fwd(x3, mods[1][3:6], mlp_nw[1], w_c, up_row, down_row, "mlp1")

    core = ci.reshape(1).astype(jnp.int32)
    chip_core = jnp.stack([chip, ci]).astype(jnp.int32)

    def reduce_swap(gbufs, tag, after=()):
        return _sibling_start(gbufs, name=tag + "_sibling_start", after=after)

    def reduce_send(flight, tag, after):
        gs, sib = _sibling_wait(flight, name=tag + "_sibling_landed", after=after)
        hs = [_add_sibling_half(g, s, core, name=f"{tag}_add_sibling{b}") for b, (g, s) in enumerate(zip(gs, sib))]
        return _owners_start(hs, name=tag + "_owners_start")

    def reduce_sum(flight, tag, after):
        hs, lands = _owners_wait(flight, name=tag + "_owners_landed", after=after)
        ts = [_add_chip_sums(h, o, chip_core, name=f"{tag}_add_chips{b}") for b, (h, o) in enumerate(zip(hs, lands))]
        return _result_start(ts, name=tag + "_result_start")

    def reduce_done(flight, tag, after=()):
        return [t.reshape(-1, t.shape[2]) for t in _result_wait(flight, name=tag + "_result_landed", after=after)]

    dx4, fsum, dy, gs = _final_loss(x4, row(final_norm_w), tgt, (mods[1][5], s_mlp1[3]), name="final_loss")
    dx3, g_c, sum_mlp1, dy, gs = _mlp_bwd(dx4, dy, gs, s_mlp1, mods[1][3:6], mlp_nw[1], w_c, None, up_row, down_row,
                                          (mods[1][2], s_sc[4]), "mlp1")
    dx2, g_c, g_sc_in, sum_sc, sc_csum, dy, gs = _sc_layer_bwd(dx3, dy, gs, s_sc, mods[1][0:3], mix_nw[1], w_sc_in, sc_conv,
                                                               w_c, g_c, sc_out_row, (mods[0][5], s_mlp0[3]), "sc")
    fly_1, tok = reduce_swap([g_c, g_sc_in], "rs1", (dx2,))
    sent_1 = []

    def send_1(du):
        sent_1.extend(reduce_send(fly_1, "rs1", (du,)))
        return sent_1[1]

    dx1, g_b, sum_mlp0, dy, gsum_ssd = _mlp_bwd(dx2, dy, gs, s_mlp0, mods[0][3:6], mlp_nw[0] + tok[0:1, 0:1], w_b, None,
                                                up_row, down_row, (mods[0][2], s_ssd[8]), "mlp0", midway=send_1)
    fly_1 = sent_1[0]
    dyn, g_ssd_out = _ssd_bwd_out(dy, s_ssd, w_ssd_out, "ssd")
    fly_2, tok = reduce_swap([g_b, g_ssd_out], "rs2")
    dy, dzx, gnsum = _gnorm_bwd(s_ssd[5], s_ssd[2], ssd_norm_w + tok[0:1, 0:1], dyn, name="ssd_dgnorm")
    fly_2, tok = reduce_send(fly_2, "rs2", (dy,))
    grad_x, d_w_zx, d_w_dt, sum_ssd, csum, ssum = _ssd_bwd_rest(
        dx1, dy, dzx, gsum_ssd, s_ssd, mods[0][0:3], mix_nw[0], w_zx, w_dt, ssd_conv, prm + tok[0:1, 0:1], "ssd")
    fly_1, tok = reduce_sum(fly_1, "rs1", (grad_x,))
    fly_2, tok = reduce_sum(fly_2, "rs2", (tok,))

    def ssd_in_owner(k):
        lo, hi = k * SSD_IN_SHARD, (k + 1) * SSD_IN_SHARD
        if hi <= ZX:
            return d_w_zx[:, lo:hi]
        return jnp.concatenate([d_w_zx[:, lo:], d_w_dt[:, :hi - ZX]], axis=1)

    small = jnp.concatenate([sum_ssd + tok[0:1, 0:1], sum_mlp0, sum_sc, sum_mlp1, csum.reshape(24, D), gnsum.reshape(16, D),
                             fsum, sc_csum, jnp.pad(ssum, ((0, 0), (0, D - LANES)))], axis=0)
    small_all = _all_gather_rows(small, name="gather_small").reshape(N_DEV, SMALL_ROWS, D)
    fly_3, tok = reduce_swap([jnp.stack([ssd_in_owner(k) for k in range(N_CHIPS)]).astype(BF16)], "rs3", (small_all,))
    fly_3, tok = reduce_send(fly_3, "rs3", (tok,))
    t_c, t_sc_in = reduce_done(fly_1, "rs1", (tok,))
    t_b, t_ssd_out = reduce_done(fly_2, "rs2", (tok,))
    small_all = small_all + tok[0:1, 0:1]
    tot = _sum_devices(small_all, name="sum_small")
    loss = tot[FINAL_ROW + 1, 0]
    mod_rows = [r + o for r in SUB_ROW for o in (3, 2, 0)]
    g_ada_b = jnp.stack([tot[r] for r in mod_rows]).reshape(2, 6 * D)
    g_mix_norm = jnp.stack([tot[SUB_ROW[0] + 1], tot[SUB_ROW[2] + 1]])
    g_mlp_norm = jnp.stack([tot[SUB_ROW[1] + 1], tot[SUB_ROW[3] + 1]])
    conv_sums = tot[SSD_CONV_ROW:SSD_CONV_ROW + 24].reshape(8, CONVD)
    g_ssd_conv_w = lax.dynamic_slice(conv_sums, (0, chip * 768), (4, 768))[None]
    g_ssd_conv_b = conv_sums[4:5]
    g_ssd_norm = tot[GNORM_ROW:GNORM_ROW + 2].reshape(1, DI)
    g_final = tot[FINAL_ROW]
    g_sc_conv_w = lax.dynamic_slice(tot[SC_CONV_ROW:SC_CONV_ROW + 3], (0, chip * 256), (3, 256))[None]
    g_a_log, g_d, g_dt_bias = (tot[HEAD_ROW + r:HEAD_ROW + r + 1, 0:NH] for r in range(3))
    c_pad = jnp.concatenate([c_all, jnp.zeros((8, D), F32)], axis=0)
    dmod_all = jnp.stack([small_all[:, r] for r in mod_rows], axis=1).reshape(N_DEV, 2, 6 * D)
    g_ada_w = []
    for i in range(2):
        dm = lax.dynamic_slice(dmod_all[:, i], (0, chip * n_ada), (N_DEV, n_ada))
        g_ada_w.append(_matmul_tn(c_pad, jnp.concatenate([dm, jnp.zeros_like(dm)], axis=0), m=D, n=n_ada, a_silu=True,
                                  name=f"ada_dw{i}"))

    big = dict(ada_w=[(g, 0) for g in g_ada_w], mlp_up=[(t_b, up_row), (t_c, up_row)], mlp_down=[(t_b, down_row), (t_c, down_row)],
               ssd_out_w=[(t_ssd_out, 0)], sc_out_w=[(t_c, sc_out_row)], sc_in_w=[(t_sc_in, 0)], ssd_in_w=None)
    grads = dict(ada_b=g_ada_b, mix_norm_w=g_mix_norm, mlp_norm_w=g_mlp_norm, ssd_conv_w=g_ssd_conv_w,
                 ssd_conv_b=g_ssd_conv_b, ssd_dt_bias=g_dt_bias, ssd_A_log=g_a_log, ssd_D=g_d, ssd_norm_w=g_ssd_norm,
                 sc_conv_w=g_sc_conv_w, final_norm_w=g_final)
    weights = dict(ada_w=(ada_w, m_ada_w, v_ada_w), ada_b=(ada_b, m_ada_b, v_ada_b),
                   mix_norm_w=(mix_norm_w, m_mix_norm_w, v_mix_norm_w), mlp_norm_w=(mlp_norm_w, m_mlp_norm_w, v_mlp_norm_w),
                   mlp_up=(mlp_up, m_mlp_up, v_mlp_up), mlp_down=(mlp_down, m_mlp_down, v_mlp_down),
                   ssd_in_w=(ssd_in_w, m_ssd_in_w, v_ssd_in_w), ssd_conv_w=(ssd_conv_w, m_ssd_conv_w, v_ssd_conv_w),
                   ssd_conv_b=(ssd_conv_b, m_ssd_conv_b, v_ssd_conv_b), ssd_dt_bias=(ssd_dt_bias, m_ssd_dt_bias, v_ssd_dt_bias),
                   ssd_A_log=(ssd_A_log, m_ssd_A_log, v_ssd_A_log), ssd_D=(ssd_D, m_ssd_D, v_ssd_D),
                   ssd_norm_w=(ssd_norm_w, m_ssd_norm_w, v_ssd_norm_w), ssd_out_w=(ssd_out_w, m_ssd_out_w, v_ssd_out_w),
                   sc_in_w=(sc_in_w, m_sc_in_w, v_sc_in_w), sc_conv_w=(sc_conv_w, m_sc_conv_w, v_sc_conv_w),
                   sc_out_w=(sc_out_w, m_sc_out_w, v_sc_out_w), final_norm_w=(final_norm_w, m_final_norm_w, v_final_norm_w))
    def step(nm, parts):
        w, m, v = (t if t.shape[0] == 1 else t.reshape(-1, t.shape[-1]) for t in weights[nm])
        rows, outs = w.shape[-2] // len(parts), None
        for i, (gbuf, g_row) in enumerate(parts):
            outs = _adamw(w, gbuf, m, v, g_row=g_row, w_row=i * rows, rows=rows, into=outs, emit_g=True, name=f"adamw_{nm}{i}")
        return outs

    res = {}
    for nm, (w, m, v) in weights.items():
        two_d = (-1, w.shape[-1]) if w.ndim > 1 else (1, -1)
        if nm not in big:
            res[nm] = (grads[nm], *_adamw(w.reshape(two_d), grads[nm].reshape(two_d), m.reshape(two_d), v.reshape(two_d),
                                          name="adamw_" + nm))
        elif big[nm] is not None:
            res[nm] = step(nm, big[nm])
    fly_3, tok = reduce_sum(fly_3, "rs3", tuple(r[1] for r in res.values()))
    (t_ssd_in,) = reduce_done(fly_3, "rs3", (tok,))
    w_t, m_t, v_t = (jnp.swapaxes(t[0], 0, 1) for t in weights["ssd_in_w"])
    res["ssd_in_w"] = [jnp.swapaxes(o, 0, 1) for o in _adamw(w_t, t_ssd_in.T, m_t, v_t, emit_g=True, name="adamw_ssd_in_w")]
    outs = [[res[nm][k].reshape(weights[nm][0].shape) for nm in weights] for k in range(4)]
    return (loss, grad_x[None], *outs[0], *outs[1], *outs[2], *outs[3])
```

```python
import jax
import jax.numpy as jnp
from jax import lax
from jax.experimental import pallas as pl
from jax.experimental.pallas import tpu as pltpu

F32 = jnp.float32
BF16 = jnp.bfloat16
MESH = pl.DeviceIdType.MESH

D = 1024
DFF = 4096
DI = 2048
NH = 32
HP = 64
NG = 4
NS = 128
CH = 128
CONVD = DI + 2 * NG * NS
ZX = DI + CONVD
GW = NG * NS
LANES = 128
N_CHIPS = 4
N_DEV = 8
EPS = 1e-5
ADAM_LR, ADAM_B1, ADAM_B2, ADAM_EPS, ADAM_WD, ADAM_STEP = 1e-3, 0.9, 0.999, 1e-8, 0.01, 10
VMEM_LIMIT = 48 * 1024 * 1024
TM_ALL = 2048
TM_HALF = 1024
ANY = pl.BlockSpec(memory_space=pl.ANY)
SEM = pl.BlockSpec(memory_space=pltpu.SEMAPHORE)

SSD_IN_SHARD = 1288
SC_IN_SHARD = 768


def _params(sem=None):
    return pltpu.CompilerParams(dimension_semantics=sem, vmem_limit_bytes=VMEM_LIMIT)


def _sigmoid(v):
    return 0.5 * jnp.tanh(0.5 * v) + 0.5


def _dot(a, b, dims=((1,), (0,)), precision=None):
    return lax.dot_general(a, b, (dims, ((), ())), preferred_element_type=F32, precision=precision)


def _dot_nt(a, b):
    return _dot(a, b, ((1,), (1,)))


def _dot_tn(a, b):
    return _dot(a, b, ((0,), (0,)))


def _nn(av, bv):
    return _dot(av.astype(BF16), bv.astype(BF16))


def _nt(av, bv):
    return _dot_nt(av.astype(BF16), bv.astype(BF16))


def _nn_split(av, bv):
    return _dot(av.astype(BF16), bv.reshape(-1, bv.shape[2]))


def _nn_split_sq(av, bv):
    af = av.astype(F32)
    return _nn_split(af * af, bv)


def _nt_split(av, bv):
    kc = bv.shape[2]
    acc = _dot_nt(av[:, 0:kc].astype(BF16), bv[0])
    for s in range(1, bv.shape[0]):
        acc = acc + _dot_nt(av[:, s * kc:(s + 1) * kc].astype(BF16), bv[s])
    return acc


def _nt_sc_in(av, bv):
    q = 256
    acc = None
    for i in range(3 * D // q):
        a_blk = av[i // 4][:, (i % 4) * q:(i % 4 + 1) * q]
        b_blk = bv[i // 3][:, (i % 3) * q:(i % 3 + 1) * q]
        t = _dot_nt(a_blk, b_blk)
        acc = t if acc is None else acc + t
    return acc


def _matmul(a, b, *, name, n, contract=_nn, a_spec=None, b_spec=None, tm=512, tn=512, extras=(), epi=None,
            out_dtypes=(F32,), a_silu=False):
    M = a.shape[-2]
    tm, tn = min(tm, M), min(tn, n)
    assert M % tm == 0 and n % tn == 0, (name, M, n, tm, tn)
    n_ex = len(extras)
    if a_spec is None:
        a_spec = pl.BlockSpec((tm, a.shape[1]), lambda i, j: (i, 0))
    if b_spec is None:
        b_spec = (pl.BlockSpec((tn, b.shape[1]), lambda i, j: (j, 0)) if contract is _nt
                  else pl.BlockSpec((b.shape[0], tn), lambda i, j: (0, j)))

    def body(*refs):
        av = refs[0][...]
        if a_silu:
            av = av * _sigmoid(av)
        acc = contract(av, refs[1][...])
        res = epi(acc, *[r[...] for r in refs[2:2 + n_ex]]) if epi is not None else (acc,)
        for o_ref, r in zip(refs[2 + n_ex:], res, strict=True):
            o_ref[...] = r.astype(o_ref.dtype)

    in_specs = [a_spec, b_spec]
    for e in extras:
        in_specs.append(pl.BlockSpec((1, tn), lambda i, j: (0, j)) if e.shape[0] == 1 and M != 1
                        else pl.BlockSpec((tm, tn), lambda i, j: (i, j)))
    outs = pl.pallas_call(
        body, grid=(M // tm, n // tn), in_specs=in_specs,
        out_specs=[pl.BlockSpec((tm, tn), lambda i, j: (i, j)) for _ in out_dtypes],
        out_shape=[jax.ShapeDtypeStruct((M, n), dt) for dt in out_dtypes],
        compiler_params=_params(("parallel", "parallel")), name=name)(a, b, *extras)
    return outs if len(out_dtypes) > 1 else outs[0]


def _matmul_tn(a, b, *, name, m, n, tm=512, tn=512, a_spec=None, b_spec=None, out_spec=None, out_struct=None, into=None,
               a_silu=False, a_square=False):
    T = a.shape[-2]
    tm, tn = min(tm, m), min(tn, n)
    assert m % tm == 0 and n % tn == 0, (name, m, n, tm, tn)
    if a_spec is None:
        a_spec = pl.BlockSpec((T, tm), lambda i, j: (0, i))
    if b_spec is None:
        b_spec = pl.BlockSpec((T, tn), lambda i, j: (0, j))
    if out_spec is None:
        out_spec, out_struct = pl.BlockSpec((tm, tn), lambda i, j: (i, j)), jax.ShapeDtypeStruct((m, n), F32)

    def body(a_ref, b_ref, *rest):
        av = a_ref[...]
        if a_silu:
            av = av * _sigmoid(av)
        if a_square:
            av = av.astype(F32) * av.astype(F32)
        rest[-1][...] = _dot_tn(av.astype(BF16), b_ref[...].astype(BF16)).astype(rest[-1].dtype)

    args, in_specs, alias = [a, b], [a_spec, b_spec], {}
    if into is not None:
        args, in_specs, alias = args + [into], in_specs + [ANY], {2: 0}
    return pl.pallas_call(body, grid=(m // tm, n // tn), in_specs=in_specs, out_specs=out_spec, out_shape=out_struct,
                          input_output_aliases=alias, compiler_params=_params(("parallel", "parallel")), name=name)(*args)


def _modnorm_fwd(x, nw, sc, sh, *, name):
    L = x.shape[0]
    tm = min(L, 512)

    def body(x_ref, nw_ref, sc_ref, sh_ref, h_ref):
        xv = x_ref[...]
        r = lax.rsqrt(jnp.mean(xv * xv, axis=-1, keepdims=True) + EPS)
        h_ref[...] = ((xv * r * nw_ref[...]) * (1.0 + sc_ref[...]) + sh_ref[...]).astype(BF16)

    row = pl.BlockSpec((tm, D), lambda i: (i, 0))
    vec = pl.BlockSpec((1, D), lambda i: (0, 0))
    return pl.pallas_call(body, grid=(L // tm,), in_specs=[row, vec, vec, vec], out_specs=row,
                          out_shape=jax.ShapeDtypeStruct((L, D), BF16),
                          compiler_params=_params(("parallel",)), name=name)(x, nw, sc, sh)


def _gate_outputs(dx, below_refs, dy_ref, gs_ref):
    g_ref, y_ref = below_refs
    dy_ref[...] = (dx * g_ref[...]).astype(BF16)
    gs_ref[0:1, :] += jnp.sum(dx * y_ref[...].astype(F32), axis=0, keepdims=True)


def _modnorm_bwd(x, dh, dxo, nw, sc, gsum, below, *, name):
    L = x.shape[0]
    tm = min(L, 256)
    nb = 0 if below is None else 2

    def body(x_ref, dh_ref, dxo_ref, nw_ref, sc_ref, g_ref, *rest):
        dx_ref, s_ref = rest[nb:nb + 2]

        @pl.when(pl.program_id(0) == 0)
        def _():
            s_ref[...] = g_ref[...]
            if nb:
                rest[-1][...] = jnp.zeros_like(rest[-1])

        xv, dhv = x_ref[...], dh_ref[...]
        r = lax.rsqrt(jnp.mean(xv * xv, axis=-1, keepdims=True) + EPS)
        xhat = xv * r
        dxhat = dhv * (nw_ref[...] * (1.0 + sc_ref[...]))
        dx = dxo_ref[...] + r * (dxhat - xhat * jnp.mean(dxhat * xhat, axis=-1, keepdims=True))
        dx_ref[...] = dx
        s_ref[1:2, :] += jnp.sum(dhv * xhat, axis=0, keepdims=True) * (1.0 + sc_ref[...])
        s_ref[2:3, :] += jnp.sum(dhv * xhat, axis=0, keepdims=True) * nw_ref[...]
        s_ref[3:4, :] += jnp.sum(dhv, axis=0, keepdims=True)
        if nb:
            _gate_outputs(dx, rest[:nb], rest[-2], rest[-1])

    row = pl.BlockSpec((tm, D), lambda i: (i, 0))
    vec = pl.BlockSpec((1, D), lambda i: (0, 0))
    blk = pl.BlockSpec((8, D), lambda i: (0, 0))
    in_specs, out_specs = [row, row, row, vec, vec, blk], [row, blk]
    out_shape = [jax.ShapeDtypeStruct((L, D), F32), jax.ShapeDtypeStruct((8, D), F32)]
    if nb:
        in_specs, out_specs = in_specs + [vec, row], out_specs + [row, blk]
        out_shape += [jax.ShapeDtypeStruct((L, D), BF16), jax.ShapeDtypeStruct((8, D), F32)]
    return pl.pallas_call(body, grid=(L // tm,), in_specs=in_specs, out_specs=out_specs, out_shape=out_shape,
                          compiler_params=_params(("arbitrary",)), name=name)(x, dh, dxo, nw, sc, gsum, *(below or ()))


def _final_loss(x, fw, tgt, below, *, name):
    L = x.shape[0]
    tm = min(L, 256)

    def body(x_ref, fw_ref, t_ref, g_ref, y_ref, dx_ref, s_ref, dy_ref, gs_ref):
        @pl.when(pl.program_id(0) == 0)
        def _():
            s_ref[...] = jnp.zeros_like(s_ref)
            gs_ref[...] = jnp.zeros_like(gs_ref)

        xv = x_ref[...]
        r = lax.rsqrt(jnp.mean(xv * xv, axis=-1, keepdims=True) + EPS)
        xhat = xv * r
        diff = xhat * fw_ref[...] - t_ref[...]
        dout = diff * (1.0 / D)
        dxhat = dout * fw_ref[...]
        dx = r * (dxhat - xhat * jnp.mean(dxhat * xhat, axis=-1, keepdims=True))
        dx_ref[...] = dx
        s_ref[0:1, :] += jnp.sum(dout * xhat, axis=0, keepdims=True)
        s_ref[1:2, :] += jnp.zeros((1, D), F32) + 0.5 * jnp.sum(jnp.sum(diff * diff, axis=-1, keepdims=True) * (1.0 / D))
        _gate_outputs(dx, (g_ref, y_ref), dy_ref, gs_ref)

    row = pl.BlockSpec((tm, D), lambda i: (i, 0))
    vec = pl.BlockSpec((1, D), lambda i: (0, 0))
    blk = pl.BlockSpec((8, D), lambda i: (0, 0))
    return pl.pallas_call(body, grid=(L // tm,), in_specs=[row, vec, row, vec, row], out_specs=[row, blk, row, blk],
                          out_shape=[jax.ShapeDtypeStruct((L, D), F32), jax.ShapeDtypeStruct((8, D), F32),
                                     jax.ShapeDtypeStruct((L, D), BF16), jax.ShapeDtypeStruct((8, D), F32)],
                          compiler_params=_params(("arbitrary",)), name=name)(x, fw, tgt, *below)


def _shift_down(v, j):
    if j == 0:
        return v
    rolled = pltpu.roll(v, j, 0)
    row = lax.broadcasted_iota(jnp.int32, (8, v.shape[1]), 0)
    return jnp.concatenate([jnp.where(row >= j, rolled[0:8], 0.0), rolled[8:]], axis=0)


def _shift_up(v, j):
    if j == 0:
        return v
    n = v.shape[0]
    rolled = pltpu.roll(v, n - j, 0)
    row = lax.broadcasted_iota(jnp.int32, (8, v.shape[1]), 0)
    return jnp.concatenate([rolled[:n - 8], jnp.where(row < 8 - j, rolled[n - 8:], 0.0)], axis=0)


def _ssd_conv_fwd(zx, w, b, *, name):
    L = zx.shape[0]
    cb = 256
    k = w.shape[0]

    def body(x_ref, w_ref, b_ref, o_ref, p_ref):
        xv = x_ref[...].astype(F32)
        pre = b_ref[...] + xv * w_ref[k - 1:k, :]
        for j in range(1, k):
            pre = pre + _shift_down(xv, j) * w_ref[k - 1 - j:k - j, :]
        o_ref[...] = (pre * _sigmoid(pre)).astype(BF16)
        p_ref[...] = pre.astype(BF16)

    blk = pl.BlockSpec((L, cb), lambda i: (0, i))
    return pl.pallas_call(
        body, grid=(CONVD // cb,),
        in_specs=[pl.BlockSpec((L, cb), lambda i: (0, i + DI // cb)), pl.BlockSpec((k, cb), lambda i: (0, i)),
                  pl.BlockSpec((1, cb), lambda i: (0, i))],
        out_specs=[blk, blk], out_shape=[jax.ShapeDtypeStruct((L, CONVD), BF16)] * 2,
        compiler_params=_params(("parallel",)), name=name)(zx, w, b)


def _ssd_conv_bwd(zx, pre, dact, w, dzx, *, name):
    L = zx.shape[0]
    cb = 256
    k = w.shape[0]

    def body(x_ref, p_ref, da_ref, w_ref, _, dx_ref, s_ref):
        xv, pv = x_ref[...].astype(F32), p_ref[...].astype(F32)
        s = _sigmoid(pv)
        dpre = da_ref[...].astype(F32) * (s * (1.0 + pv * (1.0 - s)))
        s_ref[...] = jnp.zeros_like(s_ref)
        s_ref[k:k + 1, :] = jnp.sum(dpre, axis=0, keepdims=True)
        s_ref[k - 1:k, :] = jnp.sum(dpre * xv, axis=0, keepdims=True)
        dx = dpre * w_ref[k - 1:k, :]
        for j in range(1, k):
            later = _shift_up(dpre, j)
            dx = dx + later * w_ref[k - 1 - j:k - j, :]
            s_ref[k - 1 - j:k - j, :] = jnp.sum(later * xv, axis=0, keepdims=True)
        dx_ref[...] = dx.astype(BF16)

    blk = pl.BlockSpec((L, cb), lambda i: (0, i))
    return pl.pallas_call(
        body, grid=(CONVD // cb,),
        in_specs=[pl.BlockSpec((L, cb), lambda i: (0, i + DI // cb)), blk, blk, pl.BlockSpec((k, cb), lambda i: (0, i)), ANY],
        out_specs=[pl.BlockSpec((L, cb), lambda i: (0, i + DI // cb)), pl.BlockSpec((8, cb), lambda i: (0, i))],
        out_shape=[jax.ShapeDtypeStruct((L, ZX), BF16), jax.ShapeDtypeStruct((8, CONVD), F32)],
        input_output_aliases={4: 0}, compiler_params=_params(("parallel",)), name=name)(zx, pre, dact, w, dzx)


def _sc_fwd(proj, w, *, name):
    L = proj.shape[0]
    cb = 256
    nb = D // cb
    k = w.shape[0]

    def body(b_ref, c_ref, x_ref, w_ref, o_ref, v_ref):
        u = c_ref[...].astype(F32) * x_ref[...].astype(F32)
        v = u * w_ref[k - 1:k, :]
        for j in range(1, k):
            v = v + _shift_down(u, j) * w_ref[k - 1 - j:k - j, :]
        o_ref[...] = (b_ref[...].astype(F32) * v).astype(BF16)
        v_ref[...] = v.astype(BF16)

    blk = pl.BlockSpec((L, cb), lambda i: (0, i))
    return pl.pallas_call(
        body, grid=(nb,),
        in_specs=[blk, pl.BlockSpec((L, cb), lambda i: (0, i + nb)), pl.BlockSpec((L, cb), lambda i: (0, i + 2 * nb)),
                  pl.BlockSpec((k, cb), lambda i: (0, i))],
        out_specs=[blk, blk], out_shape=[jax.ShapeDtypeStruct((L, D), BF16)] * 2,
        compiler_params=_params(("parallel",)), name=name)(proj, proj, proj, w)


def _sc_bwd(proj, v, dyv, w, *, name):
    L = proj.shape[0]
    cb = 256
    nb = D // cb
    k = w.shape[0]

    def body(b_ref, c_ref, x_ref, v_ref, dy_ref, w_ref, dp_ref, s_ref):
        cv, xv = c_ref[...].astype(F32), x_ref[...].astype(F32)
        u = cv * xv
        dyv_ = dy_ref[...]
        dp_ref[0] = (dyv_ * v_ref[...].astype(F32)).astype(BF16)
        dv = dyv_ * b_ref[...].astype(F32)
        s_ref[...] = jnp.zeros_like(s_ref)
        s_ref[k - 1:k, :] = jnp.sum(dv * u, axis=0, keepdims=True)
        du = dv * w_ref[k - 1:k, :]
        for j in range(1, k):
            later = _shift_up(dv, j)
            du = du + later * w_ref[k - 1 - j:k - j, :]
            s_ref[k - 1 - j:k - j, :] = jnp.sum(later * u, axis=0, keepdims=True)
        dp_ref[1] = (du * xv).astype(BF16)
        dp_ref[2] = (du * cv).astype(BF16)

    blk = pl.BlockSpec((L, cb), lambda i: (0, i))
    return pl.pallas_call(
        body, grid=(nb,),
        in_specs=[blk, pl.BlockSpec((L, cb), lambda i: (0, i + nb)), pl.BlockSpec((L, cb), lambda i: (0, i + 2 * nb)),
                  blk, blk, pl.BlockSpec((k, cb), lambda i: (0, i))],
        out_specs=[pl.BlockSpec((3, L, cb), lambda i: (0, 0, i)), pl.BlockSpec((8, cb), lambda i: (0, i))],
        out_shape=[jax.ShapeDtypeStruct((3, L, D), BF16), jax.ShapeDtypeStruct((8, D), F32)],
        compiler_params=_params(("parallel",)), name=name)(proj, proj, proj, v, dyv, w)


def _pieces(v, n):
    out, rest = [], v
    for _ in range(n):
        out.append(rest.astype(BF16))
        rest = rest - out[-1].astype(F32)
    return out


def _cumsum_rows(mask, v):
    m = mask.astype(BF16)
    return _dot(jnp.concatenate([m, m, m], axis=1), jnp.concatenate(_pieces(v, 3), axis=0))


def _ssd_chunk_terms(dtr, prm):
    lane = lax.broadcasted_iota(jnp.int32, (CH, LANES), 1)
    valid = lane < NH
    xdt = dtr + prm[0:1, :]
    dt = jnp.where(valid, jnp.maximum(xdt, 0.0) + jnp.log1p(jnp.exp(-jnp.abs(xdt))), 0.0)
    A = -jnp.exp(prm[1:2, :])
    ri = lax.broadcasted_iota(jnp.int32, (CH, CH), 0)
    ci = lax.broadcasted_iota(jnp.int32, (CH, CH), 1)
    cs = _cumsum_rows(ri >= ci, dt * A)
    last = cs[CH - 1:CH, :]
    spread = (lax.broadcasted_iota(jnp.int32, (2 * LANES, DI), 1) // HP
              == lax.broadcasted_iota(jnp.int32, (2 * LANES, DI), 0) % LANES).astype(BF16)
    gather = ((lax.broadcasted_iota(jnp.int32, (LANES, 2 * DI), 1) % DI) // HP
              == lax.broadcasted_iota(jnp.int32, (LANES, 2 * DI), 0)).astype(BF16)
    return dict(valid=valid, xdt=xdt, dt=dt, A=A, cs=cs, csT=cs.T, last=last, ri=ri, ci=ci, ex=(spread, gather))


def _expand(v, ex):
    if v.shape[0] == 1:
        return _expand(jnp.broadcast_to(v, (8, LANES)), ex)[0:1, :]
    return _dot(jnp.concatenate(_pieces(v, 2), axis=1), ex[0])


def _head_sum(v, ex):
    if v.shape[0] == 1:
        return _head_sum(jnp.broadcast_to(v, (8, DI)), ex)[0:1, :]
    return _dot_nt(jnp.concatenate(_pieces(v, 2), axis=1), ex[1])


def _ssd_fwd(xbc, dtr, prm, *, name):
    L = xbc.shape[0]
    nc = L // CH

    def body(xbc_ref, dtr_ref, prm_ref, y_ref, sp_ref, st_ref):
        @pl.when(pl.program_id(0) == 0)
        def _():
            st_ref[...] = jnp.zeros_like(st_ref)

        prm_v = prm_ref[...]
        t = _ssd_chunk_terms(dtr_ref[...], prm_v)
        cs, csT, ex, causal = t["cs"], t["csT"], t["ex"], t["ri"] >= t["ci"]
        xs = xbc_ref[:, 0:DI].astype(F32)
        X = xs * _expand(t["dt"], ex)
        Xb = X.astype(BF16)
        Xd = (X * _expand(jnp.exp(t["last"] - cs), ex)).astype(BF16)
        Ex = _expand(jnp.exp(cs), ex)
        cdx = _expand(jnp.exp(t["last"]), ex)
        dskx = _expand(prm_v[2:3, :], ex)
        lane = lax.broadcasted_iota(jnp.int32, (CH, LANES), 1)
        sp_ref[0] = st_ref[...]
        for g in range(NG):
            Bg = xbc_ref[:, DI + g * NS:DI + (g + 1) * NS].astype(BF16)
            Cg = xbc_ref[:, DI + GW + g * NS:DI + GW + (g + 1) * NS].astype(BF16)
            G = _dot_nt(Cg, Bg)
            Sg = st_ref[:, g * GW:(g + 1) * GW]
            yoff = _dot(Cg, Sg.astype(BF16)) * Ex[:, g * GW:(g + 1) * GW]
            for j in range(GW // LANES):
                lo = g * GW + j * LANES
                Xp = Xb[:, lo:lo + LANES]
                yd = []
                for h in (lo // HP, lo // HP + 1):
                    seg = cs[:, h:h + 1] - csT[h:h + 1, :]
                    yd.append(_dot((G * jnp.where(causal, jnp.exp(seg), 0.0)).astype(BF16), Xp))
                y_ref[:, lo:lo + LANES] = (jnp.where(lane < HP, yd[0], yd[1]) + yoff[:, j * LANES:(j + 1) * LANES]
                                           + dskx[:, lo:lo + LANES] * xs[:, lo:lo + LANES]).astype(BF16)
            st_ref[:, g * GW:(g + 1) * GW] = Sg * cdx[:, g * GW:(g + 1) * GW] + _dot_tn(Bg, Xd[:, g * GW:(g + 1) * GW])

    return pl.pallas_call(
        body, grid=(nc,),
        in_specs=[pl.BlockSpec((CH, CONVD), lambda c: (c, 0)), pl.BlockSpec((CH, LANES), lambda c: (c, 0)),
                  pl.BlockSpec((8, LANES), lambda c: (0, 0))],
        out_specs=[pl.BlockSpec((CH, DI), lambda c: (c, 0)), pl.BlockSpec((1, NS, DI), lambda c: (c, 0, 0))],
        out_shape=[jax.ShapeDtypeStruct((L, DI), BF16), jax.ShapeDtypeStruct((nc, NS, DI), F32)],
        scratch_shapes=[pltpu.VMEM((NS, DI), F32)],
        compiler_params=_params(("arbitrary",)), name=name)(xbc, dtr, prm)


def _ssd_bwd(xbc, dtr, prm, dy, sprev, *, name):
    L = xbc.shape[0]
    nc = L // CH

    def body(xbc_ref, dtr_ref, prm_ref, dy_ref, sp_ref, dxbc_ref, ddtr_ref, s_ref, dst_ref, dx_scr, de_scr, dd_scr):
        step = pl.program_id(0)

        @pl.when(step == 0)
        def _():
            dst_ref[...] = jnp.zeros_like(dst_ref)
            s_ref[...] = jnp.zeros_like(s_ref)

        prm_v = prm_ref[...]
        t = _ssd_chunk_terms(dtr_ref[...], prm_v)
        cs, csT, ex, ri, ci = t["cs"], t["csT"], t["ex"], t["ri"], t["ci"]
        E = jnp.exp(cs)
        dec = jnp.exp(t["last"] - cs)
        cd = jnp.exp(t["last"])
        xs = xbc_ref[:, 0:DI].astype(F32)
        dtx = _expand(t["dt"], ex)
        X = xs * dtx
        Xb = X.astype(BF16)
        decx = _expand(dec, ex)
        Xd = (X * decx).astype(BF16)
        Ex = _expand(E, ex)
        cdx = _expand(cd, ex)
        dskx = _expand(prm_v[2:3, :], ex)
        lane = lax.broadcasted_iota(jnp.int32, (CH, LANES), 1)
        dcs = jnp.zeros((CH, LANES), F32)
        dcd_x = []
        for g in range(NG):
            gs = slice(g * GW, (g + 1) * GW)
            Bg = xbc_ref[:, DI + g * NS:DI + (g + 1) * NS].astype(BF16)
            Cg = xbc_ref[:, DI + GW + g * NS:DI + GW + (g + 1) * NS].astype(BF16)
            G = _dot_nt(Cg, Bg)
            GT = _dot_nt(Bg, Cg)
            Sg = sp_ref[0, :, gs]
            Sgb = Sg.astype(BF16)
            dyg = dy_ref[:, gs]
            de_scr[:, gs] = dyg * _dot(Cg, Sgb)
            dYo = (Ex[:, gs] * dyg).astype(BF16)
            dC = _dot_nt(dYo, Sgb)
            dS_in = _dot_tn(Cg, dYo)
            dStg = dst_ref[:, gs]
            dStb = dStg.astype(BF16)
            dXd = _dot(Bg, dStb)
            dB = _dot_nt(Xd[:, gs], dStb)
            dd_scr[:, gs] = dXd * X[:, gs]
            dXst = dXd * decx[:, gs]
            dG = jnp.zeros((CH, CH), F32)
            dGT = jnp.zeros((CH, CH), F32)
            for j in range(GW // LANES):
                lo = g * GW + j * LANES
                Xp = Xb[:, lo:lo + LANES]
                dyp = dy_ref[:, lo:lo + LANES]
                dXp = dXst[:, j * LANES:(j + 1) * LANES]
                for k, h in enumerate((lo // HP, lo // HP + 1)):
                    dyh = jnp.where((lane < HP) if k == 0 else (lane >= HP), dyp, 0.0).astype(BF16)
                    seg = cs[:, h:h + 1] - csT[h:h + 1, :]
                    Lm = jnp.where(ri >= ci, jnp.exp(seg), 0.0)
                    LmT = jnp.where(ci >= ri, jnp.exp(-seg), 0.0)
                    dM = _dot_nt(dyh, Xp)
                    dMT = _dot_nt(Xp, dyh)
                    MT = GT * LmT
                    rs = jnp.sum(dM * (G * Lm), axis=1, keepdims=True) - jnp.sum(dMT * MT, axis=1, keepdims=True)
                    dcs = dcs + jnp.where(lane == h, rs, 0.0)
                    dG = dG + dM * Lm
                    dGT = dGT + dMT * LmT
                    dXp = dXp + _dot(MT.astype(BF16), dyh)
                dx_scr[:, lo:lo + LANES] = dXp
            dxbc_ref[:, DI + g * NS:DI + (g + 1) * NS] = (dB + _dot(dGT.astype(BF16), Cg)).astype(BF16)
            dxbc_ref[:, DI + GW + g * NS:DI + GW + (g + 1) * NS] = (dC + _dot(dG.astype(BF16), Bg)).astype(BF16)
            dcd_x.append(jnp.sum(dStg * Sg, axis=0, keepdims=True))
            dst_ref[:, gs] = dStg * cdx[:, gs] + dS_in
        dX = dx_scr[...]
        dy = dy_ref[...]
        ddec = _head_sum(dd_scr[...], ex)
        dcd = _head_sum(jnp.concatenate(dcd_x, axis=1), ex)
        dcs = dcs + _head_sum(de_scr[...], ex) * E - ddec * dec
        row = lax.broadcasted_iota(jnp.int32, (CH, LANES), 0)
        dcs = dcs + jnp.where(row == CH - 1, jnp.sum(ddec * dec, axis=0, keepdims=True) + dcd * cd, 0.0)
        da = _cumsum_rows(ci >= ri, dcs)
        ddt = da * t["A"] + _head_sum(dX * xs, ex)
        ddtr = jnp.where(t["valid"], ddt * _sigmoid(t["xdt"]), 0.0)
        ddtr_ref[...] = ddtr
        dxbc_ref[:, 0:DI] = (dX * dtx + dskx * dy).astype(BF16)
        s_ref[0:1, :] += jnp.sum(da * t["dt"], axis=0, keepdims=True)
        s_ref[1:2, :] += _head_sum(jnp.sum(dy * xs, axis=0, keepdims=True), ex)
        s_ref[2:3, :] += jnp.sum(ddtr, axis=0, keepdims=True)

        @pl.when(step == nc - 1)
        def _():
            s_ref[0:1, :] = s_ref[0:1, :] * t["A"]

    rev = lambda c: (nc - 1 - c, 0)
    return pl.pallas_call(
        body, grid=(nc,),
        in_specs=[pl.BlockSpec((CH, CONVD), rev), pl.BlockSpec((CH, LANES), rev), pl.BlockSpec((8, LANES), lambda c: (0, 0)),
                  pl.BlockSpec((CH, DI), rev), pl.BlockSpec((1, NS, DI), lambda c: (nc - 1 - c, 0, 0))],
        out_specs=[pl.BlockSpec((CH, CONVD), rev), pl.BlockSpec((CH, LANES), rev), pl.BlockSpec((8, LANES), lambda c: (0, 0))],
        out_shape=[jax.ShapeDtypeStruct((L, CONVD), BF16), jax.ShapeDtypeStruct((L, LANES), F32),
                   jax.ShapeDtypeStruct((8, LANES), F32)],
        scratch_shapes=[pltpu.VMEM((NS, DI), F32), pltpu.VMEM((CH, DI), F32), pltpu.VMEM((CH, DI), F32),
                        pltpu.VMEM((CH, DI), F32)],
        compiler_params=_params(("arbitrary",)), name=name)(xbc, dtr, prm, dy, sprev)


def _gnorm_fwd(y, zx, nw, *, name):
    L = y.shape[0]
    tm = min(L, 256)

    def body(y_ref, z_ref, nw_ref, o_ref):
        z = z_ref[...].astype(F32)
        yg = y_ref[...].astype(F32) * (z * _sigmoid(z))
        for g in range(NG):
            v = yg[:, g * GW:(g + 1) * GW]
            r = lax.rsqrt(jnp.mean(v * v, axis=-1, keepdims=True) + EPS)
            o_ref[:, g * GW:(g + 1) * GW] = (v * r * nw_ref[:, g * GW:(g + 1) * GW]).astype(BF16)

    row = pl.BlockSpec((tm, DI), lambda i: (i, 0))
    return pl.pallas_call(body, grid=(L // tm,), in_specs=[row, row, pl.BlockSpec((1, DI), lambda i: (0, 0))],
                          out_specs=row, out_shape=jax.ShapeDtypeStruct((L, DI), BF16),
                          compiler_params=_params(("parallel",)), name=name)(y, zx, nw)


def _gnorm_bwd(y, zx, nw, dyn, *, name):
    L = y.shape[0]
    tm = min(L, 256)

    def body(y_ref, z_ref, nw_ref, dyn_ref, dy_ref, dz_ref, s_ref):
        @pl.when(pl.program_id(0) == 0)
        def _():
            s_ref[...] = jnp.zeros_like(s_ref)

        z, yv = z_ref[...].astype(F32), y_ref[...].astype(F32)
        sz = _sigmoid(z)
        gate = z * sz
        dgate_dz = sz * (1.0 + z * (1.0 - sz))
        for g in range(NG):
            gs = slice(g * GW, (g + 1) * GW)
            v = yv[:, gs] * gate[:, gs]
            r = lax.rsqrt(jnp.mean(v * v, axis=-1, keepdims=True) + EPS)
            vhat = v * r
            dn = dyn_ref[:, gs].astype(F32)
            s_ref[0:1, gs] += jnp.sum(dn * vhat, axis=0, keepdims=True)
            dvhat = dn * nw_ref[:, gs]
            dv = r * (dvhat - vhat * jnp.mean(dvhat * vhat, axis=-1, keepdims=True))
            dy_ref[:, gs] = dv * gate[:, gs]
            dz_ref[:, gs] = (dv * yv[:, gs] * dgate_dz[:, gs]).astype(BF16)

    row = pl.BlockSpec((tm, DI), lambda i: (i, 0))
    return pl.pallas_call(body, grid=(L // tm,), in_specs=[row, row, pl.BlockSpec((1, DI), lambda i: (0, 0)), row],
                          out_specs=[row, row, pl.BlockSpec((8, DI), lambda i: (0, 0))],
                          out_shape=[jax.ShapeDtypeStruct((L, DI), F32), jax.ShapeDtypeStruct((L, ZX), BF16),
                                     jax.ShapeDtypeStruct((8, DI), F32)],
                          compiler_params=_params(("arbitrary",)), name=name)(y, zx, nw, dyn)


def _adamw(w, g, m, v, *, name, g_row=0, w_row=0, rows=None, into=None, emit_g=False):
    lead = w.ndim == 3
    R, C = w.shape[-2:]
    rows = R if rows is None else rows
    tr = max([t for t in range(8, rows + 1, 8) if rows % t == 0 and t * C <= 256 * 1024], default=rows)
    assert g_row % tr == 0 and w_row % tr == 0, (name, g_row, w_row, tr)
    n_out = 4 if emit_g else 3

    def body(w_ref, g_ref, m_ref, v_ref, *rest):
        outs = rest[-n_out:]
        gv = g_ref[...]
        mn = ADAM_B1 * m_ref[...] + (1.0 - ADAM_B1) * gv
        vn = ADAM_B2 * v_ref[...] + (1.0 - ADAM_B2) * (gv * gv)
        m_hat = mn / (1.0 - ADAM_B1 ** ADAM_STEP)
        v_hat = vn / (1.0 - ADAM_B2 ** ADAM_STEP)
        d_ref, mo_ref, vo_ref = outs[-3:]
        d_ref[...] = -ADAM_LR * (m_hat / (jnp.sqrt(v_hat) + ADAM_EPS) + ADAM_WD * w_ref[...])
        mo_ref[...] = mn
        vo_ref[...] = vn
        if emit_g:
            outs[0][...] = gv

    blk = (pl.BlockSpec((None, tr, C), lambda i: (0, i + w_row // tr, 0)) if lead
           else pl.BlockSpec((tr, C), lambda i: (i + w_row // tr, 0)))
    args, in_specs, alias = [w, g, m, v], [blk, pl.BlockSpec((tr, C), lambda i: (i + g_row // tr, 0)), blk, blk], {}
    if into is not None:
        args, in_specs, alias = args + list(into), in_specs + [ANY] * n_out, {4 + k: k for k in range(n_out)}
    return pl.pallas_call(body, grid=(rows // tr,), in_specs=in_specs, out_specs=[blk] * n_out,
                          out_shape=[jax.ShapeDtypeStruct(w.shape, F32)] * n_out, input_output_aliases=alias,
                          compiler_params=_params(("parallel",)), name=name)(*args)


def _residual(acc, xv, gv):
    return xv + gv * acc, acc


def _like(buf):
    return jax.ShapeDtypeStruct(buf.shape, buf.dtype)


def _mlp_fwd(x, mod, nw, wb, up_row, down_row, tag):
    sh, sc, g = mod
    h = _modnorm_fwd(x, nw, sc, sh, name=tag + "_norm")
    a = _matmul(h, wb, n=DFF, tm=TM_ALL, b_spec=pl.BlockSpec((None, D, 512), lambda mi, j: (j // 2, up_row // D, j % 2)),
                epi=lambda acc: (jnp.maximum(acc, 0.0),), out_dtypes=(BF16,), name=tag + "_up")
    xn, y = _matmul(a, wb, n=D, tm=TM_HALF, contract=_nn_split_sq,
                    b_spec=pl.BlockSpec((N_CHIPS, D, 512), lambda mi, j: (0, down_row // D, j)),
                    extras=(x, g), epi=_residual, out_dtypes=(F32, BF16), name=tag + "_down")
    return xn, (x, h, a, y)


def _mlp_bwd(dxo, dy, gsum, saved, mod, nw, wb, gb, up_row, down_row, below, tag):
    x, h, a, y = saved
    sh, sc, g = mod
    du = _matmul(dy, wb, n=DFF, tm=TM_ALL, contract=_nt,
                 b_spec=pl.BlockSpec((None, 512, D), lambda mi, j: (j // 2, down_row // 512 + j % 2, 0)),
                 extras=(a,), epi=lambda acc, av: (acc * (2.0 * av.astype(F32)),), out_dtypes=(BF16,), name=tag + "_dact")
    gb = _matmul_tn(a, dy, m=DFF, n=D, tm=D, tn=D, a_square=True, into=gb, out_struct=_like(wb),
                    out_spec=pl.BlockSpec((None, D, D), lambda mi, j: (mi, down_row // D, 0)), name=tag + "_ddown")
    dh = _matmul(du, wb, n=D, tm=TM_HALF, contract=_nt_split,
                 b_spec=pl.BlockSpec((N_CHIPS, 512, D), lambda mi, j: (0, up_row // 512 + j, 0)), name=tag + "_dh")
    gb = _matmul_tn(h, du, m=D, n=DFF, tm=D, into=gb, out_struct=_like(wb),
                    out_spec=pl.BlockSpec((None, D, 512), lambda mi, j: (j // 2, up_row // D, j % 2)), name=tag + "_dup")
    dx, sums, *nxt = _modnorm_bwd(x, dh, dxo, nw, sc, gsum, below, name=tag + "_dnorm")
    return dx, gb, sums, *nxt


def _ssd_fwd_scan(x, mod, nw, w_in_t, w_dt_t, conv_w, conv_b, prm, tag):
    sh, sc, g = mod
    h = _modnorm_fwd(x, nw, sc, sh, name=tag + "_norm")
    zx = _matmul(h, w_in_t, n=ZX, tm=TM_ALL, contract=_nt, out_dtypes=(BF16,), name=tag + "_in")
    dtr = _matmul(h, w_dt_t, n=LANES, tm=TM_ALL, contract=_nt, name=tag + "_in_dt")
    xbc, pre = _ssd_conv_fwd(zx, conv_w, conv_b, name=tag + "_conv")
    y, sprev = _ssd_fwd(xbc, dtr, prm, name=tag + "_scan")
    return h, zx, dtr, xbc, y, sprev, pre


def _ssd_fwd_out(x, mod, scan, gn_w, w_out, tag):
    sh, sc, g = mod
    h, zx, dtr, xbc, y, sprev, pre = scan
    yn = _gnorm_fwd(y, zx, gn_w, name=tag + "_gnorm")
    xn, yo = _matmul(yn, w_out, n=D, tm=TM_HALF, contract=_nn_split,
                     b_spec=pl.BlockSpec((N_CHIPS, 512, 512), lambda mi, j: (0, 0, j)),
                     extras=(x, g), epi=_residual, out_dtypes=(F32, BF16), name=tag + "_out")
    return xn, (x, h, zx, dtr, xbc, y, sprev, yn, yo, pre)


def _ssd_bwd_out(dyo, saved, w_out, tag):
    x, h, zx, dtr, xbc, y, sprev, yn, yo, pre = saved
    dyn = _matmul(dyo, w_out, n=DI, tm=TM_ALL, contract=_nt, b_spec=pl.BlockSpec((None, 512, D), lambda mi, j: (j, 0, 0)),
                  out_dtypes=(BF16,), name=tag + "_dyn")
    g_out = _matmul_tn(yn, dyo, m=DI, n=D, tn=D, out_struct=_like(w_out),
                       out_spec=pl.BlockSpec((None, 512, D), lambda mi, j: (mi, 0, 0)), name=tag + "_dout")
    return dyn, g_out


def _ssd_bwd_rest(dxo, dy, dzx, gsum, saved, mod, nw, w_in_t, w_dt_t, conv_w, prm, tag):
    x, h, zx, dtr, xbc, y, sprev, yn, yo, pre = saved
    sh, sc, g = mod
    dxbc, ddtr, ssum = _ssd_bwd(xbc, dtr, prm, dy, sprev, name=tag + "_dscan")
    dzx, csum = _ssd_conv_bwd(zx, pre, dxbc, conv_w, dzx, name=tag + "_dconv")
    dh_dt = _matmul(ddtr, w_dt_t, n=D, tm=TM_ALL, name=tag + "_dh_dt")
    dh = _matmul(dzx, w_in_t, n=D, tm=TM_HALF, b_spec=pl.BlockSpec((ZX, 512), lambda mi, j: (0, j)), extras=(dh_dt,),
                 epi=lambda acc, e: (acc + e,), name=tag + "_dh")
    d_w_zx = _matmul_tn(h, dzx, m=D, n=ZX, tm=D, name=tag + "_din")
    d_w_dt = _matmul_tn(h, ddtr, m=D, n=LANES, tm=D, name=tag + "_din_dt")
    dx, sums = _modnorm_bwd(x, dh, dxo, nw, sc, gsum, None, name=tag + "_dnorm")
    return dx, d_w_zx, d_w_dt, sums, csum, ssum


def _sc_layer_fwd(x, mod, nw, w_sc_in, conv_w, wb, out_row, tag):
    sh, sc, g = mod
    h = _modnorm_fwd(x, nw, sc, sh, name=tag + "_norm")
    proj = _matmul(h, w_sc_in, n=3 * D, tm=TM_ALL, tn=256, out_dtypes=(BF16,),
                   b_spec=pl.BlockSpec((None, D, 256), lambda mi, j: (j // 3, 0, j % 3)),
                   name=tag + "_in")
    yv, v = _sc_fwd(proj, conv_w, name=tag + "_conv")
    xn, yo = _matmul(yv, wb, n=D, tm=TM_HALF, contract=_nn_split,
                     b_spec=pl.BlockSpec((N_CHIPS, 256, 512), lambda mi, j: (0, out_row // 256, j)),
                     extras=(x, g), epi=_residual, out_dtypes=(F32, BF16), name=tag + "_out")
    return xn, (x, h, proj, yv, yo, v)


def _sc_layer_bwd(dxo, dyo, gsum, saved, mod, nw, w_sc_in, conv_w, wb, gb, out_row, below, tag):
    x, h, proj, yv, yo, v = saved
    sh, sc, g = mod
    L = x.shape[0]
    dyv = _matmul(dyo, wb, n=D, tm=TM_ALL, tn=256, contract=_nt,
                  b_spec=pl.BlockSpec((None, 256, D), lambda mi, j: (j, out_row // 256, 0)), name=tag + "_dyv")
    gb = _matmul_tn(yv, dyo, m=D, n=D, tm=256, tn=D, into=gb, out_struct=_like(wb),
                    out_spec=pl.BlockSpec((None, 256, D), lambda mi, j: (mi, out_row // 256, 0)), name=tag + "_dout")
    dproj, csum = _sc_bwd(proj, v, dyv, conv_w, name=tag + "_dconv")
    tm = min(L, TM_HALF)
    dh = _matmul(dproj, w_sc_in, n=D, tm=tm, contract=_nt_sc_in, a_spec=pl.BlockSpec((3, tm, D), lambda mi, j: (0, mi, 0)),
                 b_spec=pl.BlockSpec((N_CHIPS, 512, SC_IN_SHARD), lambda mi, j: (0, j, 0)), name=tag + "_dh")
    g_sc_in = _matmul_tn(h, dproj, m=D, n=3 * D, tm=D, tn=256, b_spec=pl.BlockSpec((None, L, 256), lambda mi, j: (j // 4, 0, j % 4)),
                         out_spec=pl.BlockSpec((None, D, 256), lambda mi, j: (j // 3, 0, j % 3)),
                         out_struct=jax.ShapeDtypeStruct((N_CHIPS, D, SC_IN_SHARD), BF16), name=tag + "_din")
    dx, sums, *nxt = _modnorm_bwd(x, dh, dxo, nw, sc, gsum, below, name=tag + "_dnorm")
    return dx, gb, g_sc_in, sums, csum, *nxt


SUB_ROW = (0, 8, 16, 24)
SSD_CONV_ROW, GNORM_ROW, FINAL_ROW, SC_CONV_ROW, HEAD_ROW, SMALL_ROWS = 32, 56, 72, 80, 88, 96


def _all_gather_rows(blk, *, name):
    m_per, n = blk.shape

    def body(x_ref, out_ref, send_sems, recv_sems, local_sem):
        x, y, c = lax.axis_index("x"), lax.axis_index("y"), lax.axis_index("c")
        me, sibling = (x, y, c), (x, y, 1 - c)
        chips = [(1 - x, y), (x, 1 - y), (1 - x, 1 - y)]

        def rows(px, py, pc):
            return out_ref.at[pl.ds((4 * px + 2 * py + pc) * m_per, m_per), :]

        def copy(k, block, to, src=None):
            return pltpu.make_async_remote_copy(src_ref=rows(*block) if src is None else src, dst_ref=rows(*block),
                                                send_sem=send_sems.at[k], recv_sem=recv_sems.at[k], device_id=to,
                                                device_id_type=MESH)

        mine = pltpu.make_async_copy(x_ref, rows(*me), local_sem)
        mine.start()
        first = [copy(0, me, sibling, src=x_ref)] + [copy(1 + j, me, (*chip, c), src=x_ref) for j, chip in enumerate(chips)]
        for cp in first:
            cp.start()
        passed = [copy(4 + j, (*chip, c), sibling) for j, chip in enumerate(chips)]
        for j, chip in enumerate(chips):
            copy(1 + j, (*chip, c), me).wait_recv()
            passed[j].start()
        copy(0, sibling, me).wait_recv()
        for j, chip in enumerate(chips):
            copy(4 + j, (*chip, 1 - c), me).wait_recv()
        for cp in first + passed:
            cp.wait_send()
        mine.wait()

    return pl.pallas_call(
        body, out_shape=jax.ShapeDtypeStruct((N_DEV * m_per, n), blk.dtype),
        in_specs=[pl.BlockSpec(memory_space=pltpu.VMEM)], out_specs=pl.BlockSpec(memory_space=pltpu.VMEM),
        scratch_shapes=[pltpu.SemaphoreType.DMA((7,)), pltpu.SemaphoreType.DMA((7,)), pltpu.SemaphoreType.DMA],
        name=name)(blk)


def _half(ref, chip, c):
    r, n = ref.shape[1:]
    if r % 32 == 0:
        return ref.at[chip, pl.ds(c * (r // 2), r // 2), :]
    assert n % 256 == 0, ref.shape
    return ref.at[chip, :, pl.ds(c * (n // 2), n // 2)]


def _gather_copy(bufs, sends, recvs, b, k, chip, pc, to):
    piece = _half(bufs[b], 2 * chip[0] + chip[1], pc)
    return pltpu.make_async_remote_copy(src_ref=piece, dst_ref=piece, send_sem=sends.at[4 * b + k], recv_sem=recvs.at[4 * b + k],
                                        device_id=to, device_id_type=MESH)


def _split_call(body, bufs, sems_in, n_sems, *, name, after=(), token=False, lands=()):
    nb, na, nl, starts = len(bufs), len(after), len(lands), not sems_in

    def wrapped(*refs):
        sems = refs[nb + na:nb + na + 2] if starts else refs[nb:nb + 2]
        made = refs[nb + na + 2 + nb:nb + na + 2 + nb + nl] if starts else ()
        body(tuple(refs[:nb]) + tuple(made), sems[0], sems[1])
        if token:
            refs[-1][...] = jnp.zeros_like(refs[-1])

    out_shape = [pltpu.SemaphoreType.DMA((n_sems,)) for _ in range(2 if starts else 0)]
    out_specs = [SEM] * len(out_shape) + [ANY] * (nb + nl)
    alias = {b: len(out_shape) + b for b in range(nb)}
    out_shape += [jax.ShapeDtypeStruct(b.shape, b.dtype) for b in bufs] + list(lands)
    if token:
        out_shape.append(jax.ShapeDtypeStruct((8, LANES), F32))
        out_specs.append(pl.BlockSpec(memory_space=pltpu.VMEM))
    return pl.pallas_call(
        wrapped, out_shape=out_shape, in_specs=[ANY] * nb + [SEM] * len(sems_in) + [ANY] * na, out_specs=out_specs,
        input_output_aliases=alias,
        compiler_params=pltpu.CompilerParams(has_side_effects=pltpu.SideEffectType.DATAFLOW_SIDE_EFFECTING),
        name=name)(*bufs, *sems_in, *after)


def _gather_start(bufs, *, name, after=()):
    nb = len(bufs)

    def body(ins, sends, recvs):
        x, y, c = lax.axis_index("x"), lax.axis_index("y"), lax.axis_index("c")
        chips = [(1 - x, y), (x, 1 - y), (1 - x, 1 - y)]
        for b in range(nb):
            _gather_copy(ins, sends, recvs, b, 0, (x, y), c, (x, y, 1 - c)).start()
            for j, chip in enumerate(chips):
                _gather_copy(ins, sends, recvs, b, 1 + j, (x, y), c, (*chip, c)).start()

    out = _split_call(body, bufs, (), 4 * nb, name=name, after=after, token=True)
    return (out[0], out[1], out[2:2 + nb]), out[-1]


def _gather_wait_first(flight, *, name, after=()):
    sends, recvs, bufs = flight
    nb = len(bufs)

    def body(ins, sends_, recvs_):
        x, y, c = lax.axis_index("x"), lax.axis_index("y"), lax.axis_index("c")
        chips = [(1 - x, y), (x, 1 - y), (1 - x, 1 - y)]
        for b in range(nb):
            _gather_copy(ins, sends_, recvs_, b, 0, (x, y), c, (x, y, 1 - c)).wait_send()
            _gather_copy(ins, sends_, recvs_, b, 0, (x, y), 1 - c, (x, y, c)).wait_recv()
            for j, chip in enumerate(chips):
                _gather_copy(ins, sends_, recvs_, b, 1 + j, (x, y), c, (*chip, c)).wait_send()
                _gather_copy(ins, sends_, recvs_, b, 1 + j, chip, c, (x, y, c)).wait_recv()

    return _split_call(body, bufs, (sends, recvs), 4 * nb, name=name, after=after)


def _gather_forward(bufs, *, name):
    nb = len(bufs)

    def body(ins, sends, recvs):
        x, y, c = lax.axis_index("x"), lax.axis_index("y"), lax.axis_index("c")
        chips = [(1 - x, y), (x, 1 - y), (1 - x, 1 - y)]
        for b in range(nb):
            for j, chip in enumerate(chips):
                _gather_copy(ins, sends, recvs, b, 1 + j, chip, c, (x, y, 1 - c)).start()

    out = _split_call(body, bufs, (), 4 * nb, name=name)
    return out[0], out[1], out[2:2 + nb]


def _gather_wait_forward(flight, *, name, after=()):
    sends, recvs, bufs = flight
    nb = len(bufs)

    def body(ins, sends_, recvs_):
        x, y, c = lax.axis_index("x"), lax.axis_index("y"), lax.axis_index("c")
        chips = [(1 - x, y), (x, 1 - y), (1 - x, 1 - y)]
        for b in range(nb):
            for j, chip in enumerate(chips):
                _gather_copy(ins, sends_, recvs_, b, 1 + j, chip, c, (x, y, 1 - c)).wait_send()
                _gather_copy(ins, sends_, recvs_, b, 1 + j, chip, 1 - c, (x, y, c)).wait_recv()

    return _split_call(body, bufs, (sends, recvs), 4 * nb, name=name, after=after)


def _owner_copies(hs, lands, sends, recvs):
    x, y, c = lax.axis_index("x"), lax.axis_index("y"), lax.axis_index("c")
    chips = [(1 - x, y), (x, 1 - y), (1 - x, 1 - y)]
    return [pltpu.make_async_remote_copy(src_ref=hs[b].at[2 * cx + cy], dst_ref=lands[b].at[j], send_sem=sends.at[3 * b + j],
                                         recv_sem=recvs.at[3 * b + j], device_id=(cx, cy, c), device_id_type=MESH)
            for b in range(len(hs)) for j, (cx, cy) in enumerate(chips)]


def _owners_start(hs, *, name):
    nb = len(hs)
    lands = [jax.ShapeDtypeStruct((3,) + h.shape[1:], h.dtype) for h in hs]

    def body(refs, sends, recvs):
        for cp in _owner_copies(refs[:nb], refs[nb:], sends, recvs):
            cp.start()

    out = _split_call(body, list(hs), (), 3 * nb, name=name, token=True, lands=lands)
    return (out[0], out[1], out[2:2 + 2 * nb]), out[-1]


def _owners_wait(flight, *, name, after=()):
    sends, recvs, bufs = flight
    nb = len(bufs) // 2

    def body(refs, sends_, recvs_):
        for cp in _owner_copies(refs[:nb], refs[nb:], sends_, recvs_):
            cp.wait()

    out = _split_call(body, bufs, (sends, recvs), 3 * nb, name=name, after=after)
    return out[:nb], out[nb:]


def _sibling_copies(gs, lands, sends, recvs):
    x, y, c = lax.axis_index("x"), lax.axis_index("y"), lax.axis_index("c")
    copies = []
    for b in range(len(gs)):
        hr = gs[b].shape[1] // 2
        copies.append(pltpu.make_async_remote_copy(
            src_ref=gs[b].at[:, pl.ds((1 - c) * hr, hr), :], dst_ref=lands[b], send_sem=sends.at[b], recv_sem=recvs.at[b],
            device_id=(x, y, 1 - c), device_id_type=MESH))
    return copies


def _sibling_start(gs, *, name, after=()):
    nb = len(gs)
    lands = [jax.ShapeDtypeStruct((g.shape[0], g.shape[1] // 2, g.shape[2]), g.dtype) for g in gs]

    def body(refs, sends, recvs):
        for cp in _sibling_copies(refs[:nb], refs[nb:], sends, recvs):
            cp.start()

    out = _split_call(body, list(gs), (), nb, name=name, after=after, token=True, lands=lands)
    return (out[0], out[1], out[2:2 + 2 * nb]), out[-1]


def _sibling_wait(flight, *, name, after=()):
    sends, recvs, bufs = flight
    nb = len(bufs) // 2

    def body(refs, sends_, recvs_):
        for cp in _sibling_copies(refs[:nb], refs[nb:], sends_, recvs_):
            cp.wait()

    out = _split_call(body, bufs, (sends, recvs), nb, name=name, after=after)
    return out[:nb], out[nb:]


def _result_copies(ts, sends, recvs):
    x, y, c = lax.axis_index("x"), lax.axis_index("y"), lax.axis_index("c")
    return [pltpu.make_async_remote_copy(src_ref=ts[b].at[c], dst_ref=ts[b].at[c], send_sem=sends.at[b], recv_sem=recvs.at[b],
                                         device_id=(x, y, 1 - c), device_id_type=MESH) for b in range(len(ts))]


def _result_start(ts, *, name):
    def body(refs, sends, recvs):
        for cp in _result_copies(refs, sends, recvs):
            cp.start()

    out = _split_call(body, ts, (), len(ts), name=name, token=True)
    return (out[0], out[1], out[2:2 + len(ts)]), out[-1]


def _result_wait(flight, *, name, after=()):
    sends, recvs, bufs = flight

    def body(refs, sends_, recvs_):
        for cp in _result_copies(refs, sends_, recvs_):
            cp.wait()

    return _split_call(body, bufs, (sends, recvs), len(bufs), name=name, after=after)


def _row_tile(rows, cols):
    best = 16
    for t in range(16, rows + 1, 16):
        if rows % t == 0 and t * cols <= 640 * 1024:
            best = t
    assert rows % best == 0, (rows, cols)
    return best


def _add_sibling_half(g, recv, core, *, name):
    nk, r, n = g.shape
    hr = r // 2
    tr = _row_tile(hr, n)

    def body(c_ref, a_ref, b_ref, o_ref):
        o_ref[...] = (a_ref[...].astype(F32) + b_ref[...].astype(F32)).astype(BF16)

    grid_spec = pltpu.PrefetchScalarGridSpec(
        num_scalar_prefetch=1, grid=(nk, hr // tr),
        in_specs=[pl.BlockSpec((None, tr, n), lambda k, i, c_ref: (k, c_ref[0] * (hr // tr) + i, 0)),
                  pl.BlockSpec((None, tr, n), lambda k, i, c_ref: (k, i, 0))],
        out_specs=pl.BlockSpec((None, tr, n), lambda k, i, c_ref: (k, i, 0)))
    return pl.pallas_call(body, grid_spec=grid_spec, out_shape=jax.ShapeDtypeStruct((nk, hr, n), BF16),
                          compiler_params=_params(("parallel", "parallel")), name=name)(core, g, recv)


def _add_chip_sums(h, recv, chip_core, *, name):
    _, hr, n = h.shape
    tr = _row_tile(hr, n)

    def body(k_ref, a_ref, b_ref, o_ref):
        o_ref[...] = ((a_ref[...].astype(F32) + b_ref[0].astype(F32)) + b_ref[1].astype(F32)) + b_ref[2].astype(F32)

    grid_spec = pltpu.PrefetchScalarGridSpec(
        num_scalar_prefetch=1, grid=(hr // tr,),
        in_specs=[pl.BlockSpec((None, tr, n), lambda i, k_ref: (k_ref[0], i, 0)),
                  pl.BlockSpec((3, tr, n), lambda i, k_ref: (0, i, 0))],
        out_specs=pl.BlockSpec((None, tr, n), lambda i, k_ref: (k_ref[1], i, 0)))
    return pl.pallas_call(body, grid_spec=grid_spec, out_shape=jax.ShapeDtypeStruct((2, hr, n), F32),
                          compiler_params=_params(("parallel",)), name=name)(chip_core, h, recv)


def _sum_devices(g, *, name):
    nd, r, n = g.shape

    def body(g_ref, o_ref):
        acc = g_ref[0]
        for i in range(1, nd):
            acc = acc + g_ref[i]
        o_ref[...] = acc

    return pl.pallas_call(body, out_shape=jax.ShapeDtypeStruct((r, n), F32), name=name)(g)


def _own_slot(parts, chip, *, name):
    rows, cols = sum(w.shape[1] for w, _ in parts), parts[0][0].shape[2]
    buf, row0 = None, 0
    for p, (w, idx) in enumerate(parts):
        r = w.shape[1]
        tr = 256 if r % 256 == 0 else r
        assert row0 % tr == 0, (name, r, row0)

        def body(chip_ref, w_ref, *rest):
            rest[-1][...] = w_ref[...].astype(BF16)

        grid_spec = pltpu.PrefetchScalarGridSpec(
            num_scalar_prefetch=1, grid=(r // tr,),
            in_specs=[pl.BlockSpec((None, tr, cols), lambda i, c_ref, idx=idx: (idx, i, 0))] + ([] if buf is None else [ANY]),
            out_specs=pl.BlockSpec((None, tr, cols), lambda i, c_ref, row0=row0, tr=tr: (c_ref[0], row0 // tr + i, 0)))
        buf = pl.pallas_call(body, grid_spec=grid_spec, out_shape=jax.ShapeDtypeStruct((N_CHIPS, rows, cols), BF16),
                             input_output_aliases={} if buf is None else {2: 0}, compiler_params=_params(("parallel",)),
                             name=f"{name}{p}")(chip, w, *(() if buf is None else (buf,)))
        row0 += r
    return buf


def kernel(x, c, ada_w, ada_b, mix_norm_w, mlp_norm_w, mlp_up, mlp_down, ssd_in_w, ssd_conv_w, ssd_conv_b, ssd_dt_bias, ssd_A_log, ssd_D, ssd_norm_w, ssd_out_w, sc_in_w, sc_conv_w, sc_out_w, final_norm_w, loss_target, m_ada_w, m_ada_b, m_mix_norm_w, m_mlp_norm_w, m_mlp_up, m_mlp_down, m_ssd_in_w, m_ssd_conv_w, m_ssd_conv_b, m_ssd_dt_bias, m_ssd_A_log, m_ssd_D, m_ssd_norm_w, m_ssd_out_w, m_sc_in_w, m_sc_conv_w, m_sc_out_w, m_final_norm_w, v_ada_w, v_ada_b, v_mix_norm_w, v_mlp_norm_w, v_mlp_up, v_mlp_down, v_ssd_in_w, v_ssd_conv_w, v_ssd_conv_b, v_ssd_dt_bias, v_ssd_A_log, v_ssd_D, v_ssd_norm_w, v_ssd_out_w, v_sc_in_w, v_sc_conv_w, v_sc_out_w, v_final_norm_w):
    xi, yi, ci = lax.axis_index("x"), lax.axis_index("y"), lax.axis_index("c")
    chip = 2 * xi + yi
    dev = 2 * chip + ci
    n_ada = ada_w.shape[2]

    conv_flat = jnp.concatenate([ssd_conv_w.reshape(-1), sc_conv_w.reshape(-1), jnp.zeros((256,), F32)]).reshape(4, D)
    blk0 = jnp.concatenate([c, conv_flat, jnp.zeros((3, D), F32)], axis=0)
    got0 = _all_gather_rows(blk0, name="gather_cond").reshape(N_DEV, 8, D)
    c_all = got0[:, 0]
    conv_all = got0[0::2, 1:5].reshape(N_CHIPS, 4 * D)
    ssd_conv = jnp.moveaxis(conv_all[:, :4 * 768].reshape(N_CHIPS, 4, 768), 0, 1).reshape(4, CONVD)
    sc_conv = jnp.moveaxis(conv_all[:, 4 * 768:4 * 768 + 3 * 256].reshape(N_CHIPS, 3, 256), 0, 1).reshape(3, D)
    mod_shard = [_matmul(c_all, ada_w, n=n_ada, a_silu=True, b_spec=pl.BlockSpec((None, D, 512), lambda mi, j, i=i: (i, 0, j)),
                         extras=(lax.dynamic_slice(ada_b, (i, chip * n_ada), (1, n_ada)),),
                         epi=lambda acc, b: (acc + b,), name=f"ada_mod{i}") for i in range(2)]
    mod_all = _all_gather_rows(jnp.concatenate(mod_shard, axis=0), name="gather_mod")
    mod_all = mod_all.reshape(N_DEV, 2, N_DEV, n_ada)[0::2]
    mod = jnp.moveaxis(lax.dynamic_index_in_dim(mod_all, dev, axis=2, keepdims=False), 0, 1).reshape(2, 6, D)
    mods = [[mod[i, j:j + 1] for j in range(6)] for i in range(2)]

    up_row, down_row, sc_out_row = 0, D, 2 * D
    chip1 = chip.reshape(1).astype(jnp.int32)
    a_bufs = [_own_slot([(jnp.swapaxes(ssd_in_w, 1, 2), 0)], chip1, name="slot_ssd_in")]
    b_bufs = [_own_slot([(ssd_out_w, 0)], chip1, name="slot_ssd_out"),
              _own_slot([(mlp_up, 0), (mlp_down, 0)], chip1, name="slot_mlp0_")]
    c_bufs = [_own_slot([(sc_in_w, 0)], chip1, name="slot_sc_in"),
              _own_slot([(mlp_up, 1), (mlp_down, 1), (sc_out_w, 0)], chip1, name="slot_layer1_")]
    fly_a, tok = _gather_start(a_bufs, name="gather_a_start", after=(mod,))
    fly_b, tok = _gather_start(b_bufs, name="gather_b_start", after=(tok,))
    fly_c, tok = _gather_start(c_bufs, name="gather_c_start", after=(tok,))

    row = lambda v: v.reshape(1, -1)
    xs, tgt = x[0], loss_target[0]
    prm = jnp.pad(jnp.concatenate([ssd_dt_bias, ssd_A_log, ssd_D, jnp.zeros((5, NH), F32)], axis=0), ((0, 0), (0, LANES - NH)))
    mix_nw = [row(mix_norm_w[i]) for i in range(2)]
    mlp_nw = [row(mlp_norm_w[i]) for i in range(2)]
    a_bufs = _gather_wait_first(fly_a, name="gather_a_landed", after=(tok,))
    (w_ssd_in,) = _gather_wait_forward(_gather_forward(a_bufs, name="gather_a_pass"), name="gather_a_done")
    w_in_t = w_ssd_in.reshape(N_CHIPS * SSD_IN_SHARD, D)
    w_dt_t = jnp.pad(w_in_t[ZX:], ((0, LANES - NH), (0, 0)))
    scan = _ssd_fwd_scan(xs, mods[0][0:3], mix_nw[0], w_in_t, w_dt_t, ssd_conv, ssd_conv_b, prm, "ssd")
    fly_b = _gather_forward(_gather_wait_first(fly_b, name="gather_b_landed", after=(scan[3],)), name="gather_b_pass")
    w_ssd_out, w_b = _gather_wait_forward(fly_b, name="gather_b_done", after=(scan[4],))
    x1, s_ssd = _ssd_fwd_out(xs, mods[0][0:3], scan, ssd_norm_w, w_ssd_out, "ssd")
    x2, s_mlp0 = _mlp_fwd(x1, mods[0][3:6], mlp_nw[0], w_b, up_row, down_row, "mlp0")
    c_bufs = _gather_wait_first(fly_c, name="gather_c_landed", after=(x2,))
    w_sc_in, w_c = _gather_wait_forward(_gather_forward(c_bufs, name="gather_c_pass"), name="gather_c_done")
    x3, s_sc = _sc_layer_fwd(x2, mods[1][0:3], mix_nw[1], w_sc_in, sc_conv, w_c, sc_out_row, "sc")
    x4, s_mlp1 = _mlp_fwd(x3, mods[1][3:6], mlp_nw[1], w_c, up_row, down_row, "mlp1")

    core = ci.reshape(1).astype(jnp.int32)
    chip_core = jnp.stack([chip, ci]).astype(jnp.int32)

    def reduce_swap(gbufs, tag, after=()):
        return _sibling_start(gbufs, name=tag + "_sibling_start", after=after)

    def reduce_send(flight, tag, after):
        gs, sib = _sibling_wait(flight, name=tag + "_sibling_landed", after=after)
        hs = [_add_sibling_half(g, s, core, name=f"{tag}_add_sibling{b}") for b, (g, s) in enumerate(zip(gs, sib))]
        return _owners_start(hs, name=tag + "_owners_start")

    def reduce_sum(flight, tag, after):
        hs, lands = _owners_wait(flight, name=tag + "_owners_landed", after=after)
        ts = [_add_chip_sums(h, o, chip_core, name=f"{tag}_add_chips{b}") for b, (h, o) in enumerate(zip(hs, lands))]
        return _result_start(ts, name=tag + "_result_start")

    def reduce_done(flight, tag, after=()):
        return [t.reshape(-1, t.shape[2]) for t in _result_wait(flight, name=tag + "_result_landed", after=after)]

    dx4, fsum, dy, gs = _final_loss(x4, row(final_norm_w), tgt, (mods[1][5], s_mlp1[3]), name="final_loss")
    dx3, g_c, sum_mlp1, dy, gs = _mlp_bwd(dx4, dy, gs, s_mlp1, mods[1][3:6], mlp_nw[1], w_c, None, up_row, down_row,
                                          (mods[1][2], s_sc[4]), "mlp1")
    dx2, g_c, g_sc_in, sum_sc, sc_csum, dy, gs = _sc_layer_bwd(dx3, dy, gs, s_sc, mods[1][0:3], mix_nw[1], w_sc_in, sc_conv,
                                                               w_c, g_c, sc_out_row, (mods[0][5], s_mlp0[3]), "sc")
    dx1, g_b, sum_mlp0, dy, gsum_ssd = _mlp_bwd(dx2, dy, gs, s_mlp0, mods[0][3:6], mlp_nw[0], w_b, None, up_row, down_row,
                                                (mods[0][2], s_ssd[8]), "mlp0")
    dyn, g_ssd_out = _ssd_bwd_out(dy, s_ssd, w_ssd_out, "ssd")
    fly_1, tok = reduce_swap([g_c, g_sc_in, g_b, g_ssd_out], "rs1")
    dy, dzx, gnsum = _gnorm_bwd(s_ssd[5], s_ssd[2], ssd_norm_w + tok[0:1, 0:1], dyn, name="ssd_dgnorm")
    fly_1, tok = reduce_send(fly_1, "rs1", (dy,))
    grad_x, d_w_zx, d_w_dt, sum_ssd, csum, ssum = _ssd_bwd_rest(
        dx1, dy, dzx, gsum_ssd, s_ssd, mods[0][0:3], mix_nw[0], w_in_t, w_dt_t, ssd_conv, prm + tok[0:1, 0:1], "ssd")
    fly_1, tok = reduce_sum(fly_1, "rs1", (grad_x,))

    def ssd_in_owner(k):
        lo, hi = k * SSD_IN_SHARD, (k + 1) * SSD_IN_SHARD
        if hi <= ZX:
            return d_w_zx[:, lo:hi]
        return jnp.concatenate([d_w_zx[:, lo:], d_w_dt[:, :hi - ZX]], axis=1)

    small = jnp.concatenate([sum_ssd + tok[0:1, 0:1], sum_mlp0, sum_sc, sum_mlp1, csum.reshape(24, D), gnsum.reshape(16, D),
                             fsum, sc_csum, jnp.pad(ssum, ((0, 0), (0, D - LANES)))], axis=0)
    small_all = _all_gather_rows(small, name="gather_small").reshape(N_DEV, SMALL_ROWS, D)
    fly_2, tok = reduce_swap([jnp.stack([ssd_in_owner(k) for k in range(N_CHIPS)]).astype(BF16)], "rs2", (small_all,))
    fly_2, tok = reduce_send(fly_2, "rs2", (tok,))
    t_c, t_sc_in, t_b, t_ssd_out = reduce_done(fly_1, "rs1", (tok,))
    small_all = small_all + tok[0:1, 0:1]
    tot = _sum_devices(small_all, name="sum_small")
    loss = tot[FINAL_ROW + 1, 0]
    mod_rows = [r + o for r in SUB_ROW for o in (3, 2, 0)]
    g_ada_b = jnp.stack([tot[r] for r in mod_rows]).reshape(2, 6 * D)
    g_mix_norm = jnp.stack([tot[SUB_ROW[0] + 1], tot[SUB_ROW[2] + 1]])
    g_mlp_norm = jnp.stack([tot[SUB_ROW[1] + 1], tot[SUB_ROW[3] + 1]])
    conv_sums = tot[SSD_CONV_ROW:SSD_CONV_ROW + 24].reshape(8, CONVD)
    g_ssd_conv_w = lax.dynamic_slice(conv_sums, (0, chip * 768), (4, 768))[None]
    g_ssd_conv_b = conv_sums[4:5]
    g_ssd_norm = tot[GNORM_ROW:GNORM_ROW + 2].reshape(1, DI)
    g_final = tot[FINAL_ROW]
    g_sc_conv_w = lax.dynamic_slice(tot[SC_CONV_ROW:SC_CONV_ROW + 3], (0, chip * 256), (3, 256))[None]
    g_a_log, g_d, g_dt_bias = (tot[HEAD_ROW + r:HEAD_ROW + r + 1, 0:NH] for r in range(3))
    c_pad = jnp.concatenate([c_all, jnp.zeros((8, D), F32)], axis=0)
    dmod_all = jnp.stack([small_all[:, r] for r in mod_rows], axis=1).reshape(N_DEV, 2, 6 * D)
    g_ada_w = []
    for i in range(2):
        dm = lax.dynamic_slice(dmod_all[:, i], (0, chip * n_ada), (N_DEV, n_ada))
        g_ada_w.append(_matmul_tn(c_pad, jnp.concatenate([dm, jnp.zeros_like(dm)], axis=0), m=D, n=n_ada, a_silu=True,
                                  name=f"ada_dw{i}"))

    big = dict(ada_w=[(g, 0) for g in g_ada_w], mlp_up=[(t_b, up_row), (t_c, up_row)], mlp_down=[(t_b, down_row), (t_c, down_row)],
               ssd_out_w=[(t_ssd_out, 0)], sc_out_w=[(t_c, sc_out_row)], sc_in_w=[(t_sc_in, 0)], ssd_in_w=None)
    grads = dict(ada_b=g_ada_b, mix_norm_w=g_mix_norm, mlp_norm_w=g_mlp_norm, ssd_conv_w=g_ssd_conv_w,
                 ssd_conv_b=g_ssd_conv_b, ssd_dt_bias=g_dt_bias, ssd_A_log=g_a_log, ssd_D=g_d, ssd_norm_w=g_ssd_norm,
                 sc_conv_w=g_sc_conv_w, final_norm_w=g_final)
    weights = dict(ada_w=(ada_w, m_ada_w, v_ada_w), ada_b=(ada_b, m_ada_b, v_ada_b),
                   mix_norm_w=(mix_norm_w, m_mix_norm_w, v_mix_norm_w), mlp_norm_w=(mlp_norm_w, m_mlp_norm_w, v_mlp_norm_w),
                   mlp_up=(mlp_up, m_mlp_up, v_mlp_up), mlp_down=(mlp_down, m_mlp_down, v_mlp_down),
                   ssd_in_w=(ssd_in_w, m_ssd_in_w, v_ssd_in_w), ssd_conv_w=(ssd_conv_w, m_ssd_conv_w, v_ssd_conv_w),
                   ssd_conv_b=(ssd_conv_b, m_ssd_conv_b, v_ssd_conv_b), ssd_dt_bias=(ssd_dt_bias, m_ssd_dt_bias, v_ssd_dt_bias),
                   ssd_A_log=(ssd_A_log, m_ssd_A_log, v_ssd_A_log), ssd_D=(ssd_D, m_ssd_D, v_ssd_D),
                   ssd_norm_w=(ssd_norm_w, m_ssd_norm_w, v_ssd_norm_w), ssd_out_w=(ssd_out_w, m_ssd_out_w, v_ssd_out_w),
                   sc_in_w=(sc_in_w, m_sc_in_w, v_sc_in_w), sc_conv_w=(sc_conv_w, m_sc_conv_w, v_sc_conv_w),
                   sc_out_w=(sc_out_w, m_sc_out_w, v_sc_out_w), final_norm_w=(final_norm_w, m_final_norm_w, v_final_norm_w))
    def step(nm, parts):
        w, m, v = (t if t.shape[0] == 1 else t.reshape(-1, t.shape[-1]) for t in weights[nm])
        rows, outs = w.shape[-2] // len(parts), None
        for i, (gbuf, g_row) in enumerate(parts):
            outs = _adamw(w, gbuf, m, v, g_row=g_row, w_row=i * rows, rows=rows, into=outs, emit_g=True, name=f"adamw_{nm}{i}")
        return outs

    res = {}
    for nm, (w, m, v) in weights.items():
        two_d = (-1, w.shape[-1]) if w.ndim > 1 else (1, -1)
        if nm not in big:
            res[nm] = (grads[nm], *_adamw(w.reshape(two_d), grads[nm].reshape(two_d), m.reshape(two_d), v.reshape(two_d),
                                          name="adamw_" + nm))
        elif big[nm] is not None:
            res[nm] = step(nm, big[nm])
    fly_2, tok = reduce_sum(fly_2, "rs2", tuple(r[1] for r in res.values()))
    (t_ssd_in,) = reduce_done(fly_2, "rs2", (tok,))
    w_t, m_t, v_t = (jnp.swapaxes(t[0], 0, 1) for t in weights["ssd_in_w"])
    res["ssd_in_w"] = [jnp.swapaxes(o, 0, 1) for o in _adamw(w_t, t_ssd_in.T, m_t, v_t, emit_g=True, name="adamw_ssd_in_w")]
    outs = [[res[nm][k].reshape(weights[nm][0].shape) for nm in weights] for k in range(4)]
    return (loss, grad_x[None], *outs[0], *outs[1], *outs[2], *outs[3])
```

```python
import jax
import jax.numpy as jnp
from jax import lax
from jax.experimental import pallas as pl
from jax.experimental.pallas import tpu as pltpu

F32 = jnp.float32
BF16 = jnp.bfloat16
MESH = pl.DeviceIdType.MESH

D = 1024
DFF = 4096
DI = 2048
NH = 32
HP = 64
NG = 4
NS = 128
CH = 128
CONVD = DI + 2 * NG * NS
ZX = DI + CONVD
GW = NG * NS
LANES = 128
N_CHIPS = 4
N_DEV = 8
EPS = 1e-5
ADAM_LR, ADAM_B1, ADAM_B2, ADAM_EPS, ADAM_WD, ADAM_STEP = 1e-3, 0.9, 0.999, 1e-8, 0.01, 10
VMEM_LIMIT = 48 * 1024 * 1024
TM_ALL = 2048
TM_HALF = 1024
ANY = pl.BlockSpec(memory_space=pl.ANY)
SEM = pl.BlockSpec(memory_space=pltpu.SEMAPHORE)

SSD_IN_SHARD = 1288
SC_IN_SHARD = 768


def _params(sem=None):
    return pltpu.CompilerParams(dimension_semantics=sem, vmem_limit_bytes=VMEM_LIMIT)


def _sigmoid(v):
    return 0.5 * jnp.tanh(0.5 * v) + 0.5


def _dot(a, b, dims=((1,), (0,)), precision=None):
    return lax.dot_general(a, b, (dims, ((), ())), preferred_element_type=F32, precision=precision)


def _dot_nt(a, b):
    return _dot(a, b, ((1,), (1,)))


def _dot_tn(a, b):
    return _dot(a, b, ((0,), (0,)))


def _nn(av, bv):
    return _dot(av.astype(BF16), bv.astype(BF16))


def _nt(av, bv):
    return _dot_nt(av.astype(BF16), bv.astype(BF16))


def _nn_split(av, bv):
    return _dot(av.astype(BF16), bv.reshape(-1, bv.shape[2]))


def _nn_split_sq(av, bv):
    af = av.astype(F32)
    return _nn_split(af * af, bv)


def _nt_split(av, bv):
    kc = bv.shape[2]
    acc = _dot_nt(av[:, 0:kc].astype(BF16), bv[0])
    for s in range(1, bv.shape[0]):
        acc = acc + _dot_nt(av[:, s * kc:(s + 1) * kc].astype(BF16), bv[s])
    return acc


def _nt_sc_in(av, bv):
    q = 256
    acc = None
    for i in range(3 * D // q):
        a_blk = av[i // 4][:, (i % 4) * q:(i % 4 + 1) * q]
        b_blk = bv[i // 3][:, (i % 3) * q:(i % 3 + 1) * q]
        t = _dot_nt(a_blk, b_blk)
        acc = t if acc is None else acc + t
    return acc


def _matmul(a, b, *, name, n, contract=_nn, a_spec=None, b_spec=None, tm=512, tn=512, extras=(), epi=None,
            out_dtypes=(F32,), a_silu=False):
    M = a.shape[-2]
    tm, tn = min(tm, M), min(tn, n)
    assert M % tm == 0 and n % tn == 0, (name, M, n, tm, tn)
    n_ex = len(extras)
    if a_spec is None:
        a_spec = pl.BlockSpec((tm, a.shape[1]), lambda i, j: (i, 0))
    if b_spec is None:
        b_spec = (pl.BlockSpec((tn, b.shape[1]), lambda i, j: (j, 0)) if contract is _nt
                  else pl.BlockSpec((b.shape[0], tn), lambda i, j: (0, j)))

    def body(*refs):
        av = refs[0][...]
        if a_silu:
            av = av * _sigmoid(av)
        acc = contract(av, refs[1][...])
        res = epi(acc, *[r[...] for r in refs[2:2 + n_ex]]) if epi is not None else (acc,)
        for o_ref, r in zip(refs[2 + n_ex:], res, strict=True):
            o_ref[...] = r.astype(o_ref.dtype)

    in_specs = [a_spec, b_spec]
    for e in extras:
        in_specs.append(pl.BlockSpec((1, tn), lambda i, j: (0, j)) if e.shape[0] == 1 and M != 1
                        else pl.BlockSpec((tm, tn), lambda i, j: (i, j)))
    outs = pl.pallas_call(
        body, grid=(M // tm, n // tn), in_specs=in_specs,
        out_specs=[pl.BlockSpec((tm, tn), lambda i, j: (i, j)) for _ in out_dtypes],
        out_shape=[jax.ShapeDtypeStruct((M, n), dt) for dt in out_dtypes],
        compiler_params=_params(("parallel", "parallel")), name=name)(a, b, *extras)
    return outs if len(out_dtypes) > 1 else outs[0]


def _matmul_tn(a, b, *, name, m, n, tm=512, tn=512, a_spec=None, b_spec=None, out_spec=None, out_struct=None, into=None,
               a_silu=False, a_square=False):
    T = a.shape[-2]
    tm, tn = min(tm, m), min(tn, n)
    assert m % tm == 0 and n % tn == 0, (name, m, n, tm, tn)
    if a_spec is None:
        a_spec = pl.BlockSpec((T, tm), lambda i, j: (0, i))
    if b_spec is None:
        b_spec = pl.BlockSpec((T, tn), lambda i, j: (0, j))
    if out_spec is None:
        out_spec, out_struct = pl.BlockSpec((tm, tn), lambda i, j: (i, j)), jax.ShapeDtypeStruct((m, n), F32)

    def body(a_ref, b_ref, *rest):
        av = a_ref[...]
        if a_silu:
            av = av * _sigmoid(av)
        if a_square:
            av = av.astype(F32) * av.astype(F32)
        rest[-1][...] = _dot_tn(av.astype(BF16), b_ref[...].astype(BF16)).astype(rest[-1].dtype)

    args, in_specs, alias = [a, b], [a_spec, b_spec], {}
    if into is not None:
        args, in_specs, alias = args + [into], in_specs + [ANY], {2: 0}
    return pl.pallas_call(body, grid=(m // tm, n // tn), in_specs=in_specs, out_specs=out_spec, out_shape=out_struct,
                          input_output_aliases=alias, compiler_params=_params(("parallel", "parallel")), name=name)(*args)


def _modnorm_fwd(x, nw, sc, sh, *, name):
    L = x.shape[0]
    tm = min(L, 512)

    def body(x_ref, nw_ref, sc_ref, sh_ref, h_ref):
        xv = x_ref[...]
        r = lax.rsqrt(jnp.mean(xv * xv, axis=-1, keepdims=True) + EPS)
        h_ref[...] = ((xv * r * nw_ref[...]) * (1.0 + sc_ref[...]) + sh_ref[...]).astype(BF16)

    row = pl.BlockSpec((tm, D), lambda i: (i, 0))
    vec = pl.BlockSpec((1, D), lambda i: (0, 0))
    return pl.pallas_call(body, grid=(L // tm,), in_specs=[row, vec, vec, vec], out_specs=row,
                          out_shape=jax.ShapeDtypeStruct((L, D), BF16),
                          compiler_params=_params(("parallel",)), name=name)(x, nw, sc, sh)


def _gate_outputs(dx, below_refs, dy_ref, gs_ref):
    g_ref, y_ref = below_refs
    dy_ref[...] = (dx * g_ref[...]).astype(BF16)
    gs_ref[0:1, :] += jnp.sum(dx * y_ref[...].astype(F32), axis=0, keepdims=True)


def _modnorm_bwd(x, dh, dxo, nw, sc, gsum, below, *, name):
    L = x.shape[0]
    tm = min(L, 256)
    nb = 0 if below is None else 2

    def body(x_ref, dh_ref, dxo_ref, nw_ref, sc_ref, g_ref, *rest):
        dx_ref, s_ref = rest[nb:nb + 2]

        @pl.when(pl.program_id(0) == 0)
        def _():
            s_ref[...] = g_ref[...]
            if nb:
                rest[-1][...] = jnp.zeros_like(rest[-1])

        xv, dhv = x_ref[...], dh_ref[...]
        r = lax.rsqrt(jnp.mean(xv * xv, axis=-1, keepdims=True) + EPS)
        xhat = xv * r
        dxhat = dhv * (nw_ref[...] * (1.0 + sc_ref[...]))
        dx = dxo_ref[...] + r * (dxhat - xhat * jnp.mean(dxhat * xhat, axis=-1, keepdims=True))
        dx_ref[...] = dx
        s_ref[1:2, :] += jnp.sum(dhv * xhat, axis=0, keepdims=True) * (1.0 + sc_ref[...])
        s_ref[2:3, :] += jnp.sum(dhv * xhat, axis=0, keepdims=True) * nw_ref[...]
        s_ref[3:4, :] += jnp.sum(dhv, axis=0, keepdims=True)
        if nb:
            _gate_outputs(dx, rest[:nb], rest[-2], rest[-1])

    row = pl.BlockSpec((tm, D), lambda i: (i, 0))
    vec = pl.BlockSpec((1, D), lambda i: (0, 0))
    blk = pl.BlockSpec((8, D), lambda i: (0, 0))
    in_specs, out_specs = [row, row, row, vec, vec, blk], [row, blk]
    out_shape = [jax.ShapeDtypeStruct((L, D), F32), jax.ShapeDtypeStruct((8, D), F32)]
    if nb:
        in_specs, out_specs = in_specs + [vec, row], out_specs + [row, blk]
        out_shape += [jax.ShapeDtypeStruct((L, D), BF16), jax.ShapeDtypeStruct((8, D), F32)]
    return pl.pallas_call(body, grid=(L // tm,), in_specs=in_specs, out_specs=out_specs, out_shape=out_shape,
                          compiler_params=_params(("arbitrary",)), name=name)(x, dh, dxo, nw, sc, gsum, *(below or ()))


def _final_loss(x, fw, tgt, below, *, name):
    L = x.shape[0]
    tm = min(L, 256)

    def body(x_ref, fw_ref, t_ref, g_ref, y_ref, dx_ref, s_ref, dy_ref, gs_ref):
        @pl.when(pl.program_id(0) == 0)
        def _():
            s_ref[...] = jnp.zeros_like(s_ref)
            gs_ref[...] = jnp.zeros_like(gs_ref)

        xv = x_ref[...]
        r = lax.rsqrt(jnp.mean(xv * xv, axis=-1, keepdims=True) + EPS)
        xhat = xv * r
        diff = xhat * fw_ref[...] - t_ref[...]
        dout = diff * (1.0 / D)
        dxhat = dout * fw_ref[...]
        dx = r * (dxhat - xhat * jnp.mean(dxhat * xhat, axis=-1, keepdims=True))
        dx_ref[...] = dx
        s_ref[0:1, :] += jnp.sum(dout * xhat, axis=0, keepdims=True)
        s_ref[1:2, :] += jnp.zeros((1, D), F32) + 0.5 * jnp.sum(jnp.sum(diff * diff, axis=-1, keepdims=True) * (1.0 / D))
        _gate_outputs(dx, (g_ref, y_ref), dy_ref, gs_ref)

    row = pl.BlockSpec((tm, D), lambda i: (i, 0))
    vec = pl.BlockSpec((1, D), lambda i: (0, 0))
    blk = pl.BlockSpec((8, D), lambda i: (0, 0))
    return pl.pallas_call(body, grid=(L // tm,), in_specs=[row, vec, row, vec, row], out_specs=[row, blk, row, blk],
                          out_shape=[jax.ShapeDtypeStruct((L, D), F32), jax.ShapeDtypeStruct((8, D), F32),
                                     jax.ShapeDtypeStruct((L, D), BF16), jax.ShapeDtypeStruct((8, D), F32)],
                          compiler_params=_params(("arbitrary",)), name=name)(x, fw, tgt, *below)


def _shift_down(v, j):
    if j == 0:
        return v
    rolled = pltpu.roll(v, j, 0)
    row = lax.broadcasted_iota(jnp.int32, (8, v.shape[1]), 0)
    return jnp.concatenate([jnp.where(row >= j, rolled[0:8], 0.0), rolled[8:]], axis=0)


def _shift_up(v, j):
    if j == 0:
        return v
    n = v.shape[0]
    rolled = pltpu.roll(v, n - j, 0)
    row = lax.broadcasted_iota(jnp.int32, (8, v.shape[1]), 0)
    return jnp.concatenate([rolled[:n - 8], jnp.where(row < 8 - j, rolled[n - 8:], 0.0)], axis=0)


def _ssd_conv_fwd(zx, w, b, *, name):
    L = zx.shape[0]
    cb = 256
    k = w.shape[0]

    def body(x_ref, w_ref, b_ref, o_ref, p_ref):
        xv = x_ref[...].astype(F32)
        pre = b_ref[...] + xv * w_ref[k - 1:k, :]
        for j in range(1, k):
            pre = pre + _shift_down(xv, j) * w_ref[k - 1 - j:k - j, :]
        o_ref[...] = (pre * _sigmoid(pre)).astype(BF16)
        p_ref[...] = pre.astype(BF16)

    blk = pl.BlockSpec((L, cb), lambda i: (0, i))
    return pl.pallas_call(
        body, grid=(CONVD // cb,),
        in_specs=[pl.BlockSpec((L, cb), lambda i: (0, i + DI // cb)), pl.BlockSpec((k, cb), lambda i: (0, i)),
                  pl.BlockSpec((1, cb), lambda i: (0, i))],
        out_specs=[blk, blk], out_shape=[jax.ShapeDtypeStruct((L, CONVD), BF16)] * 2,
        compiler_params=_params(("parallel",)), name=name)(zx, w, b)


def _ssd_conv_bwd(zx, pre, dact, w, dzx, *, name):
    L = zx.shape[0]
    cb = 256
    k = w.shape[0]

    def body(x_ref, p_ref, da_ref, w_ref, _, dx_ref, s_ref):
        xv, pv = x_ref[...].astype(F32), p_ref[...].astype(F32)
        s = _sigmoid(pv)
        dpre = da_ref[...].astype(F32) * (s * (1.0 + pv * (1.0 - s)))
        s_ref[...] = jnp.zeros_like(s_ref)
        s_ref[k:k + 1, :] = jnp.sum(dpre, axis=0, keepdims=True)
        s_ref[k - 1:k, :] = jnp.sum(dpre * xv, axis=0, keepdims=True)
        dx = dpre * w_ref[k - 1:k, :]
        for j in range(1, k):
            later = _shift_up(dpre, j)
            dx = dx + later * w_ref[k - 1 - j:k - j, :]
            s_ref[k - 1 - j:k - j, :] = jnp.sum(later * xv, axis=0, keepdims=True)
        dx_ref[...] = dx.astype(BF16)

    blk = pl.BlockSpec((L, cb), lambda i: (0, i))
    return pl.pallas_call(
        body, grid=(CONVD // cb,),
        in_specs=[pl.BlockSpec((L, cb), lambda i: (0, i + DI // cb)), blk, blk, pl.BlockSpec((k, cb), lambda i: (0, i)), ANY],
        out_specs=[pl.BlockSpec((L, cb), lambda i: (0, i + DI // cb)), pl.BlockSpec((8, cb), lambda i: (0, i))],
        out_shape=[jax.ShapeDtypeStruct((L, ZX), BF16), jax.ShapeDtypeStruct((8, CONVD), F32)],
        input_output_aliases={4: 0}, compiler_params=_params(("parallel",)), name=name)(zx, pre, dact, w, dzx)


def _sc_fwd(proj, w, *, name):
    L = proj.shape[0]
    cb = 256
    nb = D // cb
    k = w.shape[0]

    def body(b_ref, c_ref, x_ref, w_ref, o_ref, v_ref):
        u = c_ref[...].astype(F32) * x_ref[...].astype(F32)
        v = u * w_ref[k - 1:k, :]
        for j in range(1, k):
            v = v + _shift_down(u, j) * w_ref[k - 1 - j:k - j, :]
        o_ref[...] = (b_ref[...].astype(F32) * v).astype(BF16)
        v_ref[...] = v.astype(BF16)

    blk = pl.BlockSpec((L, cb), lambda i: (0, i))
    return pl.pallas_call(
        body, grid=(nb,),
        in_specs=[blk, pl.BlockSpec((L, cb), lambda i: (0, i + nb)), pl.BlockSpec((L, cb), lambda i: (0, i + 2 * nb)),
                  pl.BlockSpec((k, cb), lambda i: (0, i))],
        out_specs=[blk, blk], out_shape=[jax.ShapeDtypeStruct((L, D), BF16)] * 2,
        compiler_params=_params(("parallel",)), name=name)(proj, proj, proj, w)


def _sc_bwd(proj, v, dyv, w, *, name):
    L = proj.shape[0]
    cb = 256
    nb = D // cb
    k = w.shape[0]

    def body(b_ref, c_ref, x_ref, v_ref, dy_ref, w_ref, dp_ref, s_ref):
        cv, xv = c_ref[...].astype(F32), x_ref[...].astype(F32)
        u = cv * xv
        dyv_ = dy_ref[...]
        dp_ref[0] = (dyv_ * v_ref[...].astype(F32)).astype(BF16)
        dv = dyv_ * b_ref[...].astype(F32)
        s_ref[...] = jnp.zeros_like(s_ref)
        s_ref[k - 1:k, :] = jnp.sum(dv * u, axis=0, keepdims=True)
        du = dv * w_ref[k - 1:k, :]
        for j in range(1, k):
            later = _shift_up(dv, j)
            du = du + later * w_ref[k - 1 - j:k - j, :]
            s_ref[k - 1 - j:k - j, :] = jnp.sum(later * u, axis=0, keepdims=True)
        dp_ref[1] = (du * xv).astype(BF16)
        dp_ref[2] = (du * cv).astype(BF16)

    blk = pl.BlockSpec((L, cb), lambda i: (0, i))
    return pl.pallas_call(
        body, grid=(nb,),
        in_specs=[blk, pl.BlockSpec((L, cb), lambda i: (0, i + nb)), pl.BlockSpec((L, cb), lambda i: (0, i + 2 * nb)),
                  blk, blk, pl.BlockSpec((k, cb), lambda i: (0, i))],
        out_specs=[pl.BlockSpec((3, L, cb), lambda i: (0, 0, i)), pl.BlockSpec((8, cb), lambda i: (0, i))],
        out_shape=[jax.ShapeDtypeStruct((3, L, D), BF16), jax.ShapeDtypeStruct((8, D), F32)],
        compiler_params=_params(("parallel",)), name=name)(proj, proj, proj, v, dyv, w)


def _pieces(v, n):
    out, rest = [], v
    for _ in range(n):
        out.append(rest.astype(BF16))
        rest = rest - out[-1].astype(F32)
    return out


def _cumsum_rows(mask, v):
    m = mask.astype(BF16)
    return _dot(jnp.concatenate([m, m, m], axis=1), jnp.concatenate(_pieces(v, 3), axis=0))


def _ssd_chunk_terms(dtr, prm):
    lane = lax.broadcasted_iota(jnp.int32, (CH, LANES), 1)
    valid = lane < NH
    xdt = dtr + prm[0:1, :]
    dt = jnp.where(valid, jnp.maximum(xdt, 0.0) + jnp.log1p(jnp.exp(-jnp.abs(xdt))), 0.0)
    A = -jnp.exp(prm[1:2, :])
    ri = lax.broadcasted_iota(jnp.int32, (CH, CH), 0)
    ci = lax.broadcasted_iota(jnp.int32, (CH, CH), 1)
    cs = _cumsum_rows(ri >= ci, dt * A)
    last = cs[CH - 1:CH, :]
    spread = (lax.broadcasted_iota(jnp.int32, (2 * LANES, DI), 1) // HP
              == lax.broadcasted_iota(jnp.int32, (2 * LANES, DI), 0) % LANES).astype(BF16)
    gather = ((lax.broadcasted_iota(jnp.int32, (LANES, 2 * DI), 1) % DI) // HP
              == lax.broadcasted_iota(jnp.int32, (LANES, 2 * DI), 0)).astype(BF16)
    return dict(valid=valid, xdt=xdt, dt=dt, A=A, cs=cs, csT=cs.T, last=last, ri=ri, ci=ci, ex=(spread, gather))


def _expand(v, ex):
    if v.shape[0] == 1:
        return _expand(jnp.broadcast_to(v, (8, LANES)), ex)[0:1, :]
    return _dot(jnp.concatenate(_pieces(v, 2), axis=1), ex[0])


def _head_sum(v, ex):
    if v.shape[0] == 1:
        return _head_sum(jnp.broadcast_to(v, (8, DI)), ex)[0:1, :]
    return _dot_nt(jnp.concatenate(_pieces(v, 2), axis=1), ex[1])


def _ssd_fwd(xbc, dtr, prm, *, name):
    L = xbc.shape[0]
    nc = L // CH

    def body(xbc_ref, dtr_ref, prm_ref, y_ref, sp_ref, st_ref):
        @pl.when(pl.program_id(0) == 0)
        def _():
            st_ref[...] = jnp.zeros_like(st_ref)

        prm_v = prm_ref[...]
        t = _ssd_chunk_terms(dtr_ref[...], prm_v)
        cs, csT, ex, causal = t["cs"], t["csT"], t["ex"], t["ri"] >= t["ci"]
        xs = xbc_ref[:, 0:DI].astype(F32)
        X = xs * _expand(t["dt"], ex)
        Xb = X.astype(BF16)
        Xd = (X * _expand(jnp.exp(t["last"] - cs), ex)).astype(BF16)
        Ex = _expand(jnp.exp(cs), ex)
        cdx = _expand(jnp.exp(t["last"]), ex)
        dskx = _expand(prm_v[2:3, :], ex)
        lane = lax.broadcasted_iota(jnp.int32, (CH, LANES), 1)
        sp_ref[0] = st_ref[...]
        for g in range(NG):
            Bg = xbc_ref[:, DI + g * NS:DI + (g + 1) * NS].astype(BF16)
            Cg = xbc_ref[:, DI + GW + g * NS:DI + GW + (g + 1) * NS].astype(BF16)
            G = _dot_nt(Cg, Bg)
            Sg = st_ref[:, g * GW:(g + 1) * GW]
            yoff = _dot(Cg, Sg.astype(BF16)) * Ex[:, g * GW:(g + 1) * GW]
            for j in range(GW // LANES):
                lo = g * GW + j * LANES
                Xp = Xb[:, lo:lo + LANES]
                yd = []
                for h in (lo // HP, lo // HP + 1):
                    seg = cs[:, h:h + 1] - csT[h:h + 1, :]
                    yd.append(_dot((G * jnp.where(causal, jnp.exp(seg), 0.0)).astype(BF16), Xp))
                y_ref[:, lo:lo + LANES] = (jnp.where(lane < HP, yd[0], yd[1]) + yoff[:, j * LANES:(j + 1) * LANES]
                                           + dskx[:, lo:lo + LANES] * xs[:, lo:lo + LANES]).astype(BF16)
            st_ref[:, g * GW:(g + 1) * GW] = Sg * cdx[:, g * GW:(g + 1) * GW] + _dot_tn(Bg, Xd[:, g * GW:(g + 1) * GW])

    return pl.pallas_call(
        body, grid=(nc,),
        in_specs=[pl.BlockSpec((CH, CONVD), lambda c: (c, 0)), pl.BlockSpec((CH, LANES), lambda c: (c, 0)),
                  pl.BlockSpec((8, LANES), lambda c: (0, 0))],
        out_specs=[pl.BlockSpec((CH, DI), lambda c: (c, 0)), pl.BlockSpec((1, NS, DI), lambda c: (c, 0, 0))],
        out_shape=[jax.ShapeDtypeStruct((L, DI), BF16), jax.ShapeDtypeStruct((nc, NS, DI), F32)],
        scratch_shapes=[pltpu.VMEM((NS, DI), F32)],
        compiler_params=_params(("arbitrary",)), name=name)(xbc, dtr, prm)


def _ssd_bwd(xbc, dtr, prm, dy, sprev, *, name):
    L = xbc.shape[0]
    nc = L // CH

    def body(xbc_ref, dtr_ref, prm_ref, dy_ref, sp_ref, dxbc_ref, ddtr_ref, s_ref, dst_ref, dx_scr, de_scr, dd_scr):
        step = pl.program_id(0)

        @pl.when(step == 0)
        def _():
            dst_ref[...] = jnp.zeros_like(dst_ref)
            s_ref[...] = jnp.zeros_like(s_ref)

        prm_v = prm_ref[...]
        t = _ssd_chunk_terms(dtr_ref[...], prm_v)
        cs, csT, ex, ri, ci = t["cs"], t["csT"], t["ex"], t["ri"], t["ci"]
        E = jnp.exp(cs)
        dec = jnp.exp(t["last"] - cs)
        cd = jnp.exp(t["last"])
        xs = xbc_ref[:, 0:DI].astype(F32)
        dtx = _expand(t["dt"], ex)
        X = xs * dtx
        Xb = X.astype(BF16)
        decx = _expand(dec, ex)
        Xd = (X * decx).astype(BF16)
        Ex = _expand(E, ex)
        cdx = _expand(cd, ex)
        dskx = _expand(prm_v[2:3, :], ex)
        lane = lax.broadcasted_iota(jnp.int32, (CH, LANES), 1)
        dcs = jnp.zeros((CH, LANES), F32)
        dcd_x = []
        for g in range(NG):
            gs = slice(g * GW, (g + 1) * GW)
            Bg = xbc_ref[:, DI + g * NS:DI + (g + 1) * NS].astype(BF16)
            Cg = xbc_ref[:, DI + GW + g * NS:DI + GW + (g + 1) * NS].astype(BF16)
            G = _dot_nt(Cg, Bg)
            GT = _dot_nt(Bg, Cg)
            Sg = sp_ref[0, :, gs]
            Sgb = Sg.astype(BF16)
            dyg = dy_ref[:, gs]
            de_scr[:, gs] = dyg * _dot(Cg, Sgb)
            dYo = (Ex[:, gs] * dyg).astype(BF16)
            dC = _dot_nt(dYo, Sgb)
            dS_in = _dot_tn(Cg, dYo)
            dStg = dst_ref[:, gs]
            dStb = dStg.astype(BF16)
            dXd = _dot(Bg, dStb)
            dB = _dot_nt(Xd[:, gs], dStb)
            dd_scr[:, gs] = dXd * X[:, gs]
            dXst = dXd * decx[:, gs]
            dG = jnp.zeros((CH, CH), F32)
            dGT = jnp.zeros((CH, CH), F32)
            for j in range(GW // LANES):
                lo = g * GW + j * LANES
                Xp = Xb[:, lo:lo + LANES]
                dyp = dy_ref[:, lo:lo + LANES]
                dXp = dXst[:, j * LANES:(j + 1) * LANES]
                for k, h in enumerate((lo // HP, lo // HP + 1)):
                    dyh = jnp.where((lane < HP) if k == 0 else (lane >= HP), dyp, 0.0).astype(BF16)
                    seg = cs[:, h:h + 1] - csT[h:h + 1, :]
                    Lm = jnp.where(ri >= ci, jnp.exp(seg), 0.0)
                    LmT = jnp.where(ci >= ri, jnp.exp(-seg), 0.0)
                    dM = _dot_nt(dyh, Xp)
                    dMT = _dot_nt(Xp, dyh)
                    MT = GT * LmT
                    rs = jnp.sum(dM * (G * Lm), axis=1, keepdims=True) - jnp.sum(dMT * MT, axis=1, keepdims=True)
                    dcs = dcs + jnp.where(lane == h, rs, 0.0)
                    dG = dG + dM * Lm
                    dGT = dGT + dMT * LmT
                    dXp = dXp + _dot(MT.astype(BF16), dyh)
                dx_scr[:, lo:lo + LANES] = dXp
            dxbc_ref[:, DI + g * NS:DI + (g + 1) * NS] = (dB + _dot(dGT.astype(BF16), Cg)).astype(BF16)
            dxbc_ref[:, DI + GW + g * NS:DI + GW + (g + 1) * NS] = (dC + _dot(dG.astype(BF16), Bg)).astype(BF16)
            dcd_x.append(jnp.sum(dStg * Sg, axis=0, keepdims=True))
            dst_ref[:, gs] = dStg * cdx[:, gs] + dS_in
        dX = dx_scr[...]
        dy = dy_ref[...]
        ddec = _head_sum(dd_scr[...], ex)
        dcd = _head_sum(jnp.concatenate(dcd_x, axis=1), ex)
        dcs = dcs + _head_sum(de_scr[...], ex) * E - ddec * dec
        row = lax.broadcasted_iota(jnp.int32, (CH, LANES), 0)
        dcs = dcs + jnp.where(row == CH - 1, jnp.sum(ddec * dec, axis=0, keepdims=True) + dcd * cd, 0.0)
        da = _cumsum_rows(ci >= ri, dcs)
        ddt = da * t["A"] + _head_sum(dX * xs, ex)
        ddtr = jnp.where(t["valid"], ddt * _sigmoid(t["xdt"]), 0.0)
        ddtr_ref[...] = ddtr
        dxbc_ref[:, 0:DI] = (dX * dtx + dskx * dy).astype(BF16)
        s_ref[0:1, :] += jnp.sum(da * t["dt"], axis=0, keepdims=True)
        s_ref[1:2, :] += _head_sum(jnp.sum(dy * xs, axis=0, keepdims=True), ex)
        s_ref[2:3, :] += jnp.sum(ddtr, axis=0, keepdims=True)

        @pl.when(step == nc - 1)
        def _():
            s_ref[0:1, :] = s_ref[0:1, :] * t["A"]

    rev = lambda c: (nc - 1 - c, 0)
    return pl.pallas_call(
        body, grid=(nc,),
        in_specs=[pl.BlockSpec((CH, CONVD), rev), pl.BlockSpec((CH, LANES), rev), pl.BlockSpec((8, LANES), lambda c: (0, 0)),
                  pl.BlockSpec((CH, DI), rev), pl.BlockSpec((1, NS, DI), lambda c: (nc - 1 - c, 0, 0))],
        out_specs=[pl.BlockSpec((CH, CONVD), rev), pl.BlockSpec((CH, LANES), rev), pl.BlockSpec((8, LANES), lambda c: (0, 0))],
        out_shape=[jax.ShapeDtypeStruct((L, CONVD), BF16), jax.ShapeDtypeStruct((L, LANES), F32),
                   jax.ShapeDtypeStruct((8, LANES), F32)],
        scratch_shapes=[pltpu.VMEM((NS, DI), F32), pltpu.VMEM((CH, DI), F32), pltpu.VMEM((CH, DI), F32),
                        pltpu.VMEM((CH, DI), F32)],
        compiler_params=_params(("arbitrary",)), name=name)(xbc, dtr, prm, dy, sprev)


def _gnorm_fwd(y, zx, nw, *, name):
    L = y.shape[0]
    tm = min(L, 256)

    def body(y_ref, z_ref, nw_ref, o_ref):
        z = z_ref[...].astype(F32)
        yg = y_ref[...].astype(F32) * (z * _sigmoid(z))
        for g in range(NG):
            v = yg[:, g * GW:(g + 1) * GW]
            r = lax.rsqrt(jnp.mean(v * v, axis=-1, keepdims=True) + EPS)
            o_ref[:, g * GW:(g + 1) * GW] = (v * r * nw_ref[:, g * GW:(g + 1) * GW]).astype(BF16)

    row = pl.BlockSpec((tm, DI), lambda i: (i, 0))
    return pl.pallas_call(body, grid=(L // tm,), in_specs=[row, row, pl.BlockSpec((1, DI), lambda i: (0, 0))],
                          out_specs=row, out_shape=jax.ShapeDtypeStruct((L, DI), BF16),
                          compiler_params=_params(("parallel",)), name=name)(y, zx, nw)


def _gnorm_bwd(y, zx, nw, dyn, *, name):
    L = y.shape[0]
    tm = min(L, 256)

    def body(y_ref, z_ref, nw_ref, dyn_ref, dy_ref, dz_ref, s_ref):
        @pl.when(pl.program_id(0) == 0)
        def _():
            s_ref[...] = jnp.zeros_like(s_ref)

        z, yv = z_ref[...].astype(F32), y_ref[...].astype(F32)
        sz = _sigmoid(z)
        gate = z * sz
        dgate_dz = sz * (1.0 + z * (1.0 - sz))
        for g in range(NG):
            gs = slice(g * GW, (g + 1) * GW)
            v = yv[:, gs] * gate[:, gs]
            r = lax.rsqrt(jnp.mean(v * v, axis=-1, keepdims=True) + EPS)
            vhat = v * r
            dn = dyn_ref[:, gs].astype(F32)
            s_ref[0:1, gs] += jnp.sum(dn * vhat, axis=0, keepdims=True)
            dvhat = dn * nw_ref[:, gs]
            dv = r * (dvhat - vhat * jnp.mean(dvhat * vhat, axis=-1, keepdims=True))
            dy_ref[:, gs] = dv * gate[:, gs]
            dz_ref[:, gs] = (dv * yv[:, gs] * dgate_dz[:, gs]).astype(BF16)

    row = pl.BlockSpec((tm, DI), lambda i: (i, 0))
    return pl.pallas_call(body, grid=(L // tm,), in_specs=[row, row, pl.BlockSpec((1, DI), lambda i: (0, 0)), row],
                          out_specs=[row, row, pl.BlockSpec((8, DI), lambda i: (0, 0))],
                          out_shape=[jax.ShapeDtypeStruct((L, DI), F32), jax.ShapeDtypeStruct((L, ZX), BF16),
                                     jax.ShapeDtypeStruct((8, DI), F32)],
                          compiler_params=_params(("arbitrary",)), name=name)(y, zx, nw, dyn)


def _adamw(w, g, m, v, *, name, g_row=0, w_row=0, rows=None, into=None, emit_g=False):
    lead = w.ndim == 3
    R, C = w.shape[-2:]
    rows = R if rows is None else rows
    tr = max([t for t in range(8, rows + 1, 8) if rows % t == 0 and t * C <= 256 * 1024], default=rows)
    assert g_row % tr == 0 and w_row % tr == 0, (name, g_row, w_row, tr)
    n_out = 4 if emit_g else 3

    def body(w_ref, g_ref, m_ref, v_ref, *rest):
        outs = rest[-n_out:]
        gv = g_ref[...]
        mn = ADAM_B1 * m_ref[...] + (1.0 - ADAM_B1) * gv
        vn = ADAM_B2 * v_ref[...] + (1.0 - ADAM_B2) * (gv * gv)
        m_hat = mn / (1.0 - ADAM_B1 ** ADAM_STEP)
        v_hat = vn / (1.0 - ADAM_B2 ** ADAM_STEP)
        d_ref, mo_ref, vo_ref = outs[-3:]
        d_ref[...] = -ADAM_LR * (m_hat / (jnp.sqrt(v_hat) + ADAM_EPS) + ADAM_WD * w_ref[...])
        mo_ref[...] = mn
        vo_ref[...] = vn
        if emit_g:
            outs[0][...] = gv

    blk = (pl.BlockSpec((None, tr, C), lambda i: (0, i + w_row // tr, 0)) if lead
           else pl.BlockSpec((tr, C), lambda i: (i + w_row // tr, 0)))
    args, in_specs, alias = [w, g, m, v], [blk, pl.BlockSpec((tr, C), lambda i: (i + g_row // tr, 0)), blk, blk], {}
    if into is not None:
        args, in_specs, alias = args + list(into), in_specs + [ANY] * n_out, {4 + k: k for k in range(n_out)}
    return pl.pallas_call(body, grid=(rows // tr,), in_specs=in_specs, out_specs=[blk] * n_out,
                          out_shape=[jax.ShapeDtypeStruct(w.shape, F32)] * n_out, input_output_aliases=alias,
                          compiler_params=_params(("parallel",)), name=name)(*args)


def _residual(acc, xv, gv):
    return xv + gv * acc, acc


def _like(buf):
    return jax.ShapeDtypeStruct(buf.shape, buf.dtype)


def _mlp_fwd(x, mod, nw, wb, up_row, down_row, tag, midway=None):
    sh, sc, g = mod
    h = _modnorm_fwd(x, nw, sc, sh, name=tag + "_norm")
    a = _matmul(h, wb, n=DFF, tm=TM_ALL, b_spec=pl.BlockSpec((None, D, 512), lambda mi, j: (j // 2, up_row // D, j % 2)),
                epi=lambda acc: (jnp.maximum(acc, 0.0),), out_dtypes=(BF16,), name=tag + "_up")
    if midway is not None:
        midway(a)
    xn, y = _matmul(a, wb, n=D, tm=TM_HALF, contract=_nn_split_sq,
                    b_spec=pl.BlockSpec((N_CHIPS, D, 512), lambda mi, j: (0, down_row // D, j)),
                    extras=(x, g), epi=_residual, out_dtypes=(F32, BF16), name=tag + "_down")
    return xn, (x, h, a, y)


def _mlp_bwd(dxo, dy, gsum, saved, mod, nw, wb, gb, up_row, down_row, below, tag):
    x, h, a, y = saved
    sh, sc, g = mod
    du = _matmul(dy, wb, n=DFF, tm=TM_ALL, contract=_nt,
                 b_spec=pl.BlockSpec((None, 512, D), lambda mi, j: (j // 2, down_row // 512 + j % 2, 0)),
                 extras=(a,), epi=lambda acc, av: (acc * (2.0 * av.astype(F32)),), out_dtypes=(BF16,), name=tag + "_dact")
    gb = _matmul_tn(a, dy, m=DFF, n=D, tm=D, tn=D, a_square=True, into=gb, out_struct=_like(wb),
                    out_spec=pl.BlockSpec((None, D, D), lambda mi, j: (mi, down_row // D, 0)), name=tag + "_ddown")
    dh = _matmul(du, wb, n=D, tm=TM_HALF, contract=_nt_split,
                 b_spec=pl.BlockSpec((N_CHIPS, 512, D), lambda mi, j: (0, up_row // 512 + j, 0)), name=tag + "_dh")
    gb = _matmul_tn(h, du, m=D, n=DFF, tm=D, into=gb, out_struct=_like(wb),
                    out_spec=pl.BlockSpec((None, D, 512), lambda mi, j: (j // 2, up_row // D, j % 2)), name=tag + "_dup")
    dx, sums, *nxt = _modnorm_bwd(x, dh, dxo, nw, sc, gsum, below, name=tag + "_dnorm")
    return dx, gb, sums, *nxt


def _ssd_fwd_scan(x, mod, nw, w_in_t, w_dt_t, conv_w, conv_b, prm, tag):
    sh, sc, g = mod
    h = _modnorm_fwd(x, nw, sc, sh, name=tag + "_norm")
    zx = _matmul(h, w_in_t, n=ZX, tm=TM_ALL, contract=_nt, out_dtypes=(BF16,), name=tag + "_in")
    dtr = _matmul(h, w_dt_t, n=LANES, tm=TM_ALL, contract=_nt, name=tag + "_in_dt")
    xbc, pre = _ssd_conv_fwd(zx, conv_w, conv_b, name=tag + "_conv")
    y, sprev = _ssd_fwd(xbc, dtr, prm, name=tag + "_scan")
    return h, zx, dtr, xbc, y, sprev, pre


def _ssd_fwd_out(x, mod, scan, gn_w, get_w_out, tag):
    sh, sc, g = mod
    h, zx, dtr, xbc, y, sprev, pre = scan
    yn = _gnorm_fwd(y, zx, gn_w, name=tag + "_gnorm")
    w_out = get_w_out(yn)
    xn, yo = _matmul(yn, w_out, n=D, tm=TM_HALF, contract=_nn_split,
                     b_spec=pl.BlockSpec((N_CHIPS, 512, 512), lambda mi, j: (0, 0, j)),
                     extras=(x, g), epi=_residual, out_dtypes=(F32, BF16), name=tag + "_out")
    return xn, (x, h, zx, dtr, xbc, y, sprev, yn, yo, pre)


def _ssd_bwd_out(dyo, saved, w_out, tag):
    x, h, zx, dtr, xbc, y, sprev, yn, yo, pre = saved
    dyn = _matmul(dyo, w_out, n=DI, tm=TM_ALL, contract=_nt, b_spec=pl.BlockSpec((None, 512, D), lambda mi, j: (j, 0, 0)),
                  out_dtypes=(BF16,), name=tag + "_dyn")
    g_out = _matmul_tn(yn, dyo, m=DI, n=D, tn=D, out_struct=_like(w_out),
                       out_spec=pl.BlockSpec((None, 512, D), lambda mi, j: (mi, 0, 0)), name=tag + "_dout")
    return dyn, g_out


def _ssd_bwd_rest(dxo, dy, dzx, gsum, saved, mod, nw, w_in_t, w_dt_t, conv_w, prm, tag):
    x, h, zx, dtr, xbc, y, sprev, yn, yo, pre = saved
    sh, sc, g = mod
    dxbc, ddtr, ssum = _ssd_bwd(xbc, dtr, prm, dy, sprev, name=tag + "_dscan")
    dzx, csum = _ssd_conv_bwd(zx, pre, dxbc, conv_w, dzx, name=tag + "_dconv")
    dh_dt = _matmul(ddtr, w_dt_t, n=D, tm=TM_ALL, name=tag + "_dh_dt")
    dh = _matmul(dzx, w_in_t, n=D, tm=TM_HALF, b_spec=pl.BlockSpec((ZX, 512), lambda mi, j: (0, j)), extras=(dh_dt,),
                 epi=lambda acc, e: (acc + e,), name=tag + "_dh")
    d_w_zx = _matmul_tn(h, dzx, m=D, n=ZX, tm=D, name=tag + "_din")
    d_w_dt = _matmul_tn(h, ddtr, m=D, n=LANES, tm=D, name=tag + "_din_dt")
    dx, sums = _modnorm_bwd(x, dh, dxo, nw, sc, gsum, None, name=tag + "_dnorm")
    return dx, d_w_zx, d_w_dt, sums, csum, ssum


def _sc_layer_fwd(x, mod, nw, w_sc_in, conv_w, wb, out_row, tag, midway=None):
    sh, sc, g = mod
    h = _modnorm_fwd(x, nw, sc, sh, name=tag + "_norm")
    proj = _matmul(h, w_sc_in, n=3 * D, tm=TM_ALL, tn=256, out_dtypes=(BF16,),
                   b_spec=pl.BlockSpec((None, D, 256), lambda mi, j: (j // 3, 0, j % 3)),
                   name=tag + "_in")
    if midway is not None:
        midway(proj)
    yv, v = _sc_fwd(proj, conv_w, name=tag + "_conv")
    xn, yo = _matmul(yv, wb, n=D, tm=TM_HALF, contract=_nn_split,
                     b_spec=pl.BlockSpec((N_CHIPS, 256, 512), lambda mi, j: (0, out_row // 256, j)),
                     extras=(x, g), epi=_residual, out_dtypes=(F32, BF16), name=tag + "_out")
    return xn, (x, h, proj, yv, yo, v)


def _sc_layer_bwd(dxo, dyo, gsum, saved, mod, nw, w_sc_in, conv_w, wb, gb, out_row, below, tag):
    x, h, proj, yv, yo, v = saved
    sh, sc, g = mod
    L = x.shape[0]
    dyv = _matmul(dyo, wb, n=D, tm=TM_ALL, tn=256, contract=_nt,
                  b_spec=pl.BlockSpec((None, 256, D), lambda mi, j: (j, out_row // 256, 0)), name=tag + "_dyv")
    gb = _matmul_tn(yv, dyo, m=D, n=D, tm=256, tn=D, into=gb, out_struct=_like(wb),
                    out_spec=pl.BlockSpec((None, 256, D), lambda mi, j: (mi, out_row // 256, 0)), name=tag + "_dout")
    dproj, csum = _sc_bwd(proj, v, dyv, conv_w, name=tag + "_dconv")
    tm = min(L, TM_HALF)
    dh = _matmul(dproj, w_sc_in, n=D, tm=tm, contract=_nt_sc_in, a_spec=pl.BlockSpec((3, tm, D), lambda mi, j: (0, mi, 0)),
                 b_spec=pl.BlockSpec((N_CHIPS, 512, SC_IN_SHARD), lambda mi, j: (0, j, 0)), name=tag + "_dh")
    g_sc_in = _matmul_tn(h, dproj, m=D, n=3 * D, tm=D, tn=256, b_spec=pl.BlockSpec((None, L, 256), lambda mi, j: (j // 4, 0, j % 4)),
                         out_spec=pl.BlockSpec((None, D, 256), lambda mi, j: (j // 3, 0, j % 3)),
                         out_struct=jax.ShapeDtypeStruct((N_CHIPS, D, SC_IN_SHARD), BF16), name=tag + "_din")
    dx, sums, *nxt = _modnorm_bwd(x, dh, dxo, nw, sc, gsum, below, name=tag + "_dnorm")
    return dx, gb, g_sc_in, sums, csum, *nxt


SUB_ROW = (0, 8, 16, 24)
SSD_CONV_ROW, GNORM_ROW, FINAL_ROW, SC_CONV_ROW, HEAD_ROW, SMALL_ROWS = 32, 56, 72, 80, 88, 96


def _all_gather_rows(blk, *, name):
    m_per, n = blk.shape

    def body(x_ref, out_ref, send_sems, recv_sems, local_sem):
        x, y, c = lax.axis_index("x"), lax.axis_index("y"), lax.axis_index("c")
        me, sibling = (x, y, c), (x, y, 1 - c)
        chips = [(1 - x, y), (x, 1 - y), (1 - x, 1 - y)]

        def rows(px, py, pc):
            return out_ref.at[pl.ds((4 * px + 2 * py + pc) * m_per, m_per), :]

        def copy(k, block, to, src=None):
            return pltpu.make_async_remote_copy(src_ref=rows(*block) if src is None else src, dst_ref=rows(*block),
                                                send_sem=send_sems.at[k], recv_sem=recv_sems.at[k], device_id=to,
                                                device_id_type=MESH)

        mine = pltpu.make_async_copy(x_ref, rows(*me), local_sem)
        mine.start()
        first = [copy(0, me, sibling, src=x_ref)] + [copy(1 + j, me, (*chip, c), src=x_ref) for j, chip in enumerate(chips)]
        for cp in first:
            cp.start()
        passed = [copy(4 + j, (*chip, c), sibling) for j, chip in enumerate(chips)]
        for j, chip in enumerate(chips):
            copy(1 + j, (*chip, c), me).wait_recv()
            passed[j].start()
        copy(0, sibling, me).wait_recv()
        for j, chip in enumerate(chips):
            copy(4 + j, (*chip, 1 - c), me).wait_recv()
        for cp in first + passed:
            cp.wait_send()
        mine.wait()

    return pl.pallas_call(
        body, out_shape=jax.ShapeDtypeStruct((N_DEV * m_per, n), blk.dtype),
        in_specs=[pl.BlockSpec(memory_space=pltpu.VMEM)], out_specs=pl.BlockSpec(memory_space=pltpu.VMEM),
        scratch_shapes=[pltpu.SemaphoreType.DMA((7,)), pltpu.SemaphoreType.DMA((7,)), pltpu.SemaphoreType.DMA],
        name=name)(blk)


def _half(ref, chip, c):
    r, n = ref.shape[1:]
    if r % 32 == 0:
        return ref.at[chip, pl.ds(c * (r // 2), r // 2), :]
    assert n % 256 == 0, ref.shape
    return ref.at[chip, :, pl.ds(c * (n // 2), n // 2)]


def _gather_copy(bufs, sends, recvs, b, k, chip, pc, to):
    piece = _half(bufs[b], 2 * chip[0] + chip[1], pc)
    return pltpu.make_async_remote_copy(src_ref=piece, dst_ref=piece, send_sem=sends.at[4 * b + k], recv_sem=recvs.at[4 * b + k],
                                        device_id=to, device_id_type=MESH)


def _split_call(body, bufs, sems_in, n_sems, *, name, after=(), token=False, lands=()):
    nb, na, nl, starts = len(bufs), len(after), len(lands), not sems_in

    def wrapped(*refs):
        sems = refs[nb + na:nb + na + 2] if starts else refs[nb:nb + 2]
        made = refs[nb + na + 2 + nb:nb + na + 2 + nb + nl] if starts else ()
        body(tuple(refs[:nb]) + tuple(made), sems[0], sems[1])
        if token:
            refs[-1][...] = jnp.zeros_like(refs[-1])

    out_shape = [pltpu.SemaphoreType.DMA((n_sems,)) for _ in range(2 if starts else 0)]
    out_specs = [SEM] * len(out_shape) + [ANY] * (nb + nl)
    alias = {b: len(out_shape) + b for b in range(nb)}
    out_shape += [jax.ShapeDtypeStruct(b.shape, b.dtype) for b in bufs] + list(lands)
    if token:
        out_shape.append(jax.ShapeDtypeStruct((8, LANES), F32))
        out_specs.append(pl.BlockSpec(memory_space=pltpu.VMEM))
    return pl.pallas_call(
        wrapped, out_shape=out_shape, in_specs=[ANY] * nb + [SEM] * len(sems_in) + [ANY] * na, out_specs=out_specs,
        input_output_aliases=alias,
        compiler_params=pltpu.CompilerParams(has_side_effects=pltpu.SideEffectType.DATAFLOW_SIDE_EFFECTING),
        name=name)(*bufs, *sems_in, *after)


def _gather_start(bufs, *, name, after=()):
    nb = len(bufs)

    def body(ins, sends, recvs):
        x, y, c = lax.axis_index("x"), lax.axis_index("y"), lax.axis_index("c")
        chips = [(1 - x, y), (x, 1 - y), (1 - x, 1 - y)]
        for b in range(nb):
            _gather_copy(ins, sends, recvs, b, 0, (x, y), c, (x, y, 1 - c)).start()
            for j, chip in enumerate(chips):
                _gather_copy(ins, sends, recvs, b, 1 + j, (x, y), c, (*chip, c)).start()

    out = _split_call(body, bufs, (), 4 * nb, name=name, after=after, token=True)
    return (out[0], out[1], out[2:2 + nb]), out[-1]


def _gather_wait_first(flight, *, name, after=()):
    sends, recvs, bufs = flight
    nb = len(bufs)

    def body(ins, sends_, recvs_):
        x, y, c = lax.axis_index("x"), lax.axis_index("y"), lax.axis_index("c")
        chips = [(1 - x, y), (x, 1 - y), (1 - x, 1 - y)]
        for b in range(nb):
            _gather_copy(ins, sends_, recvs_, b, 0, (x, y), c, (x, y, 1 - c)).wait_send()
            _gather_copy(ins, sends_, recvs_, b, 0, (x, y), 1 - c, (x, y, c)).wait_recv()
            for j, chip in enumerate(chips):
                _gather_copy(ins, sends_, recvs_, b, 1 + j, (x, y), c, (*chip, c)).wait_send()
                _gather_copy(ins, sends_, recvs_, b, 1 + j, chip, c, (x, y, c)).wait_recv()

    return _split_call(body, bufs, (sends, recvs), 4 * nb, name=name, after=after)


def _gather_forward(bufs, *, name):
    nb = len(bufs)

    def body(ins, sends, recvs):
        x, y, c = lax.axis_index("x"), lax.axis_index("y"), lax.axis_index("c")
        chips = [(1 - x, y), (x, 1 - y), (1 - x, 1 - y)]
        for b in range(nb):
            for j, chip in enumerate(chips):
                _gather_copy(ins, sends, recvs, b, 1 + j, chip, c, (x, y, 1 - c)).start()

    out = _split_call(body, bufs, (), 4 * nb, name=name)
    return out[0], out[1], out[2:2 + nb]


def _gather_wait_forward(flight, *, name, after=()):
    sends, recvs, bufs = flight
    nb = len(bufs)

    def body(ins, sends_, recvs_):
        x, y, c = lax.axis_index("x"), lax.axis_index("y"), lax.axis_index("c")
        chips = [(1 - x, y), (x, 1 - y), (1 - x, 1 - y)]
        for b in range(nb):
            for j, chip in enumerate(chips):
                _gather_copy(ins, sends_, recvs_, b, 1 + j, chip, c, (x, y, 1 - c)).wait_send()
                _gather_copy(ins, sends_, recvs_, b, 1 + j, chip, 1 - c, (x, y, c)).wait_recv()

    return _split_call(body, bufs, (sends, recvs), 4 * nb, name=name, after=after)


def _owner_copies(hs, lands, sends, recvs):
    x, y, c = lax.axis_index("x"), lax.axis_index("y"), lax.axis_index("c")
    chips = [(1 - x, y), (x, 1 - y), (1 - x, 1 - y)]
    return [pltpu.make_async_remote_copy(src_ref=hs[b].at[2 * cx + cy], dst_ref=lands[b].at[j], send_sem=sends.at[3 * b + j],
                                         recv_sem=recvs.at[3 * b + j], device_id=(cx, cy, c), device_id_type=MESH)
            for b in range(len(hs)) for j, (cx, cy) in enumerate(chips)]


def _owners_start(hs, *, name):
    nb = len(hs)
    lands = [jax.ShapeDtypeStruct((3,) + h.shape[1:], h.dtype) for h in hs]

    def body(refs, sends, recvs):
        for cp in _owner_copies(refs[:nb], refs[nb:], sends, recvs):
            cp.start()

    out = _split_call(body, list(hs), (), 3 * nb, name=name, token=True, lands=lands)
    return (out[0], out[1], out[2:2 + 2 * nb]), out[-1]


def _owners_wait(flight, *, name, after=()):
    sends, recvs, bufs = flight
    nb = len(bufs) // 2

    def body(refs, sends_, recvs_):
        for cp in _owner_copies(refs[:nb], refs[nb:], sends_, recvs_):
            cp.wait()

    out = _split_call(body, bufs, (sends, recvs), 3 * nb, name=name, after=after)
    return out[:nb], out[nb:]


def _sibling_copies(gs, lands, sends, recvs):
    x, y, c = lax.axis_index("x"), lax.axis_index("y"), lax.axis_index("c")
    copies = []
    for b in range(len(gs)):
        hr = gs[b].shape[1] // 2
        copies.append(pltpu.make_async_remote_copy(
            src_ref=gs[b].at[:, pl.ds((1 - c) * hr, hr), :], dst_ref=lands[b], send_sem=sends.at[b], recv_sem=recvs.at[b],
            device_id=(x, y, 1 - c), device_id_type=MESH))
    return copies


def _sibling_start(gs, *, name, after=()):
    nb = len(gs)
    lands = [jax.ShapeDtypeStruct((g.shape[0], g.shape[1] // 2, g.shape[2]), g.dtype) for g in gs]

    def body(refs, sends, recvs):
        for cp in _sibling_copies(refs[:nb], refs[nb:], sends, recvs):
            cp.start()

    out = _split_call(body, list(gs), (), nb, name=name, after=after, token=True, lands=lands)
    return (out[0], out[1], out[2:2 + 2 * nb]), out[-1]


def _sibling_wait(flight, *, name, after=()):
    sends, recvs, bufs = flight
    nb = len(bufs) // 2

    def body(refs, sends_, recvs_):
        for cp in _sibling_copies(refs[:nb], refs[nb:], sends_, recvs_):
            cp.wait()

    out = _split_call(body, bufs, (sends, recvs), nb, name=name, after=after)
    return out[:nb], out[nb:]


def _result_copies(ts, sends, recvs):
    x, y, c = lax.axis_index("x"), lax.axis_index("y"), lax.axis_index("c")
    return [pltpu.make_async_remote_copy(src_ref=ts[b].at[c], dst_ref=ts[b].at[c], send_sem=sends.at[b], recv_sem=recvs.at[b],
                                         device_id=(x, y, 1 - c), device_id_type=MESH) for b in range(len(ts))]


def _result_start(ts, *, name):
    def body(refs, sends, recvs):
        for cp in _result_copies(refs, sends, recvs):
            cp.start()

    out = _split_call(body, ts, (), len(ts), name=name, token=True)
    return (out[0], out[1], out[2:2 + len(ts)]), out[-1]


def _result_wait(flight, *, name, after=()):
    sends, recvs, bufs = flight

    def body(refs, sends_, recvs_):
        for cp in _result_copies(refs, sends_, recvs_):
            cp.wait()

    return _split_call(body, bufs, (sends, recvs), len(bufs), name=name, after=after)


def _row_tile(rows, cols):
    best = 16
    for t in range(16, rows + 1, 16):
        if rows % t == 0 and t * cols <= 640 * 1024:
            best = t
    assert rows % best == 0, (rows, cols)
    return best


def _add_sibling_half(g, recv, core, *, name):
    nk, r, n = g.shape
    hr = r // 2
    tr = _row_tile(hr, n)

    def body(c_ref, a_ref, b_ref, o_ref):
        o_ref[...] = (a_ref[...].astype(F32) + b_ref[...].astype(F32)).astype(BF16)

    grid_spec = pltpu.PrefetchScalarGridSpec(
        num_scalar_prefetch=1, grid=(nk, hr // tr),
        in_specs=[pl.BlockSpec((None, tr, n), lambda k, i, c_ref: (k, c_ref[0] * (hr // tr) + i, 0)),
                  pl.BlockSpec((None, tr, n), lambda k, i, c_ref: (k, i, 0))],
        out_specs=pl.BlockSpec((None, tr, n), lambda k, i, c_ref: (k, i, 0)))
    return pl.pallas_call(body, grid_spec=grid_spec, out_shape=jax.ShapeDtypeStruct((nk, hr, n), BF16),
                          compiler_params=_params(("parallel", "parallel")), name=name)(core, g, recv)


def _add_chip_sums(h, recv, chip_core, *, name):
    _, hr, n = h.shape
    tr = _row_tile(hr, n)

    def body(k_ref, a_ref, b_ref, o_ref):
        o_ref[...] = ((a_ref[...].astype(F32) + b_ref[0].astype(F32)) + b_ref[1].astype(F32)) + b_ref[2].astype(F32)

    grid_spec = pltpu.PrefetchScalarGridSpec(
        num_scalar_prefetch=1, grid=(hr // tr,),
        in_specs=[pl.BlockSpec((None, tr, n), lambda i, k_ref: (k_ref[0], i, 0)),
                  pl.BlockSpec((3, tr, n), lambda i, k_ref: (0, i, 0))],
        out_specs=pl.BlockSpec((None, tr, n), lambda i, k_ref: (k_ref[1], i, 0)))
    return pl.pallas_call(body, grid_spec=grid_spec, out_shape=jax.ShapeDtypeStruct((2, hr, n), F32),
                          compiler_params=_params(("parallel",)), name=name)(chip_core, h, recv)


def _sum_devices(g, *, name):
    nd, r, n = g.shape

    def body(g_ref, o_ref):
        acc = g_ref[0]
        for i in range(1, nd):
            acc = acc + g_ref[i]
        o_ref[...] = acc

    return pl.pallas_call(body, out_shape=jax.ShapeDtypeStruct((r, n), F32), name=name)(g)


def _own_slot(parts, chip, *, name):
    rows, cols = sum(w.shape[1] for w, _ in parts), parts[0][0].shape[2]
    buf, row0 = None, 0
    for p, (w, idx) in enumerate(parts):
        r = w.shape[1]
        tr = 256 if r % 256 == 0 else r
        assert row0 % tr == 0, (name, r, row0)

        def body(chip_ref, w_ref, *rest):
            rest[-1][...] = w_ref[...].astype(BF16)

        grid_spec = pltpu.PrefetchScalarGridSpec(
            num_scalar_prefetch=1, grid=(r // tr,),
            in_specs=[pl.BlockSpec((None, tr, cols), lambda i, c_ref, idx=idx: (idx, i, 0))] + ([] if buf is None else [ANY]),
            out_specs=pl.BlockSpec((None, tr, cols), lambda i, c_ref, row0=row0, tr=tr: (c_ref[0], row0 // tr + i, 0)))
        buf = pl.pallas_call(body, grid_spec=grid_spec, out_shape=jax.ShapeDtypeStruct((N_CHIPS, rows, cols), BF16),
                             input_output_aliases={} if buf is None else {2: 0}, compiler_params=_params(("parallel",)),
                             name=f"{name}{p}")(chip, w, *(() if buf is None else (buf,)))
        row0 += r
    return buf


def kernel(x, c, ada_w, ada_b, mix_norm_w, mlp_norm_w, mlp_up, mlp_down, ssd_in_w, ssd_conv_w, ssd_conv_b, ssd_dt_bias, ssd_A_log, ssd_D, ssd_norm_w, ssd_out_w, sc_in_w, sc_conv_w, sc_out_w, final_norm_w, loss_target, m_ada_w, m_ada_b, m_mix_norm_w, m_mlp_norm_w, m_mlp_up, m_mlp_down, m_ssd_in_w, m_ssd_conv_w, m_ssd_conv_b, m_ssd_dt_bias, m_ssd_A_log, m_ssd_D, m_ssd_norm_w, m_ssd_out_w, m_sc_in_w, m_sc_conv_w, m_sc_out_w, m_final_norm_w, v_ada_w, v_ada_b, v_mix_norm_w, v_mlp_norm_w, v_mlp_up, v_mlp_down, v_ssd_in_w, v_ssd_conv_w, v_ssd_conv_b, v_ssd_dt_bias, v_ssd_A_log, v_ssd_D, v_ssd_norm_w, v_ssd_out_w, v_sc_in_w, v_sc_conv_w, v_sc_out_w, v_final_norm_w):
    xi, yi, ci = lax.axis_index("x"), lax.axis_index("y"), lax.axis_index("c")
    chip = 2 * xi + yi
    dev = 2 * chip + ci
    n_ada = ada_w.shape[2]

    conv_flat = jnp.concatenate([ssd_conv_w.reshape(-1), sc_conv_w.reshape(-1), jnp.zeros((256,), F32)]).reshape(4, D)
    blk0 = jnp.concatenate([c, conv_flat, jnp.zeros((3, D), F32)], axis=0)
    got0 = _all_gather_rows(blk0, name="gather_cond").reshape(N_DEV, 8, D)
    c_all = got0[:, 0]
    conv_all = got0[0::2, 1:5].reshape(N_CHIPS, 4 * D)
    ssd_conv = jnp.moveaxis(conv_all[:, :4 * 768].reshape(N_CHIPS, 4, 768), 0, 1).reshape(4, CONVD)
    sc_conv = jnp.moveaxis(conv_all[:, 4 * 768:4 * 768 + 3 * 256].reshape(N_CHIPS, 3, 256), 0, 1).reshape(3, D)
    mod_shard = [_matmul(c_all, ada_w, n=n_ada, a_silu=True, b_spec=pl.BlockSpec((None, D, 512), lambda mi, j, i=i: (i, 0, j)),
                         extras=(lax.dynamic_slice(ada_b, (i, chip * n_ada), (1, n_ada)),),
                         epi=lambda acc, b: (acc + b,), name=f"ada_mod{i}") for i in range(2)]
    mod_all = _all_gather_rows(jnp.concatenate(mod_shard, axis=0), name="gather_mod")
    mod_all = mod_all.reshape(N_DEV, 2, N_DEV, n_ada)[0::2]
    mod = jnp.moveaxis(lax.dynamic_index_in_dim(mod_all, dev, axis=2, keepdims=False), 0, 1).reshape(2, 6, D)
    mods = [[mod[i, j:j + 1] for j in range(6)] for i in range(2)]

    up_row, down_row = 0, D
    chip1 = chip.reshape(1).astype(jnp.int32)
    a_bufs = [_own_slot([(jnp.swapaxes(ssd_in_w, 1, 2), 0)], chip1, name="slot_ssd_in")]
    b_bufs = [_own_slot([(ssd_out_w, 0)], chip1, name="slot_ssd_out"),
              _own_slot([(mlp_up, 0), (mlp_down, 0)], chip1, name="slot_mlp0_")]
    c_bufs = [_own_slot([(sc_in_w, 0)], chip1, name="slot_sc_in"), _own_slot([(sc_out_w, 0)], chip1, name="slot_sc_out")]
    d_bufs = [_own_slot([(mlp_up, 1), (mlp_down, 1)], chip1, name="slot_mlp1_")]
    fly_a, tok = _gather_start(a_bufs, name="gather_a_start", after=(mod,))
    fly_b, tok = _gather_start(b_bufs, name="gather_b_start", after=(tok,))
    fly_c, tok = _gather_start(c_bufs, name="gather_c_start", after=(tok,))
    fly_d, tok = _gather_start(d_bufs, name="gather_d_start", after=(tok,))

    row = lambda v: v.reshape(1, -1)
    xs, tgt = x[0], loss_target[0]
    prm = jnp.pad(jnp.concatenate([ssd_dt_bias, ssd_A_log, ssd_D, jnp.zeros((5, NH), F32)], axis=0), ((0, 0), (0, LANES - NH)))
    mix_nw = [row(mix_norm_w[i]) for i in range(2)]
    mlp_nw = [row(mlp_norm_w[i]) for i in range(2)]
    a_bufs = _gather_wait_first(fly_a, name="gather_a_landed", after=(tok,))
    (w_ssd_in,) = _gather_wait_forward(_gather_forward(a_bufs, name="gather_a_pass"), name="gather_a_done")
    w_in_t = w_ssd_in.reshape(N_CHIPS * SSD_IN_SHARD, D)
    w_dt_t = jnp.pad(w_in_t[ZX:], ((0, LANES - NH), (0, 0)))
    scan = _ssd_fwd_scan(xs, mods[0][0:3], mix_nw[0], w_in_t, w_dt_t, ssd_conv, ssd_conv_b, prm, "ssd")

    def land(flight, tag, after):
        return _gather_forward(_gather_wait_first(flight, name=f"gather_{tag}_landed", after=(after,)), name=f"gather_{tag}_pass")

    passed, got = {"b": land(fly_b, "b", scan[4])}, {}

    def done(tag, after):
        got[tag] = _gather_wait_forward(passed[tag], name=f"gather_{tag}_done", after=(after,))
        return got[tag]

    x1, s_ssd = _ssd_fwd_out(xs, mods[0][0:3], scan, ssd_norm_w, lambda yn: done("b", yn)[0], "ssd")
    w_ssd_out, w_b = got["b"]
    x2, s_mlp0 = _mlp_fwd(x1, mods[0][3:6], mlp_nw[0], w_b, up_row, down_row, "mlp0",
                          midway=lambda a: passed.update(c=land(fly_c, "c", a)))
    w_sc_in, w_sc_out = done("c", x2)
    x3, s_sc = _sc_layer_fwd(x2, mods[1][0:3], mix_nw[1], w_sc_in, sc_conv, w_sc_out, 0, "sc",
                             midway=lambda proj: passed.update(d=land(fly_d, "d", proj)))
    (w_mlp1,) = done("d", x3)
    x4, s_mlp1 = _mlp_fwd(x3, mods[1][3:6], mlp_nw[1], w_mlp1, up_row, down_row, "mlp1")

    core = ci.reshape(1).astype(jnp.int32)
    chip_core = jnp.stack([chip, ci]).astype(jnp.int32)

    def reduce_swap(gbufs, tag, after=()):
        return _sibling_start(gbufs, name=tag + "_sibling_start", after=after)

    def reduce_send(flight, tag, after):
        gs, sib = _sibling_wait(flight, name=tag + "_sibling_landed", after=after)
        hs = [_add_sibling_half(g, s, core, name=f"{tag}_add_sibling{b}") for b, (g, s) in enumerate(zip(gs, sib))]
        return _owners_start(hs, name=tag + "_owners_start")

    def reduce_sum(flight, tag, after):
        hs, lands = _owners_wait(flight, name=tag + "_owners_landed", after=after)
        ts = [_add_chip_sums(h, o, chip_core, name=f"{tag}_add_chips{b}") for b, (h, o) in enumerate(zip(hs, lands))]
        return _result_start(ts, name=tag + "_result_start")

    def reduce_done(flight, tag, after=()):
        return [t.reshape(-1, t.shape[2]) for t in _result_wait(flight, name=tag + "_result_landed", after=after)]

    dx4, fsum, dy, gs = _final_loss(x4, row(final_norm_w), tgt, (mods[1][5], s_mlp1[3]), name="final_loss")
    dx3, g_mlp1, sum_mlp1, dy, gs = _mlp_bwd(dx4, dy, gs, s_mlp1, mods[1][3:6], mlp_nw[1], w_mlp1, None, up_row, down_row,
                                             (mods[1][2], s_sc[4]), "mlp1")
    dx2, g_sc_out, g_sc_in, sum_sc, sc_csum, dy, gs = _sc_layer_bwd(dx3, dy, gs, s_sc, mods[1][0:3], mix_nw[1], w_sc_in,
                                                                    sc_conv, w_sc_out, None, 0, (mods[0][5], s_mlp0[3]), "sc")
    dx1, g_b, sum_mlp0, dy, gsum_ssd = _mlp_bwd(dx2, dy, gs, s_mlp0, mods[0][3:6], mlp_nw[0], w_b, None, up_row, down_row,
                                                (mods[0][2], s_ssd[8]), "mlp0")
    dyn, g_ssd_out = _ssd_bwd_out(dy, s_ssd, w_ssd_out, "ssd")
    fly_1, tok = reduce_swap([g_mlp1, g_sc_out, g_sc_in, g_b, g_ssd_out], "rs1")
    dy, dzx, gnsum = _gnorm_bwd(s_ssd[5], s_ssd[2], ssd_norm_w + tok[0:1, 0:1], dyn, name="ssd_dgnorm")
    fly_1, tok = reduce_send(fly_1, "rs1", (dy,))
    grad_x, d_w_zx, d_w_dt, sum_ssd, csum, ssum = _ssd_bwd_rest(
        dx1, dy, dzx, gsum_ssd, s_ssd, mods[0][0:3], mix_nw[0], w_in_t, w_dt_t, ssd_conv, prm + tok[0:1, 0:1], "ssd")
    fly_1, tok = reduce_sum(fly_1, "rs1", (grad_x,))

    def ssd_in_owner(k):
        lo, hi = k * SSD_IN_SHARD, (k + 1) * SSD_IN_SHARD
        if hi <= ZX:
            return d_w_zx[:, lo:hi]
        return jnp.concatenate([d_w_zx[:, lo:], d_w_dt[:, :hi - ZX]], axis=1)

    small = jnp.concatenate([sum_ssd + tok[0:1, 0:1], sum_mlp0, sum_sc, sum_mlp1, csum.reshape(24, D), gnsum.reshape(16, D),
                             fsum, sc_csum, jnp.pad(ssum, ((0, 0), (0, D - LANES)))], axis=0)
    small_all = _all_gather_rows(small, name="gather_small").reshape(N_DEV, SMALL_ROWS, D)
    fly_2, tok = reduce_swap([jnp.stack([ssd_in_owner(k) for k in range(N_CHIPS)]).astype(BF16)], "rs2", (small_all,))
    fly_2, tok = reduce_send(fly_2, "rs2", (tok,))
    t_mlp1, t_sc_out, t_sc_in, t_b, t_ssd_out = reduce_done(fly_1, "rs1", (tok,))
    small_all = small_all + tok[0:1, 0:1]
    tot = _sum_devices(small_all, name="sum_small")
    loss = tot[FINAL_ROW + 1, 0]
    mod_rows = [r + o for r in SUB_ROW for o in (3, 2, 0)]
    g_ada_b = jnp.stack([tot[r] for r in mod_rows]).reshape(2, 6 * D)
    g_mix_norm = jnp.stack([tot[SUB_ROW[0] + 1], tot[SUB_ROW[2] + 1]])
    g_mlp_norm = jnp.stack([tot[SUB_ROW[1] + 1], tot[SUB_ROW[3] + 1]])
    conv_sums = tot[SSD_CONV_ROW:SSD_CONV_ROW + 24].reshape(8, CONVD)
    g_ssd_conv_w = lax.dynamic_slice(conv_sums, (0, chip * 768), (4, 768))[None]
    g_ssd_conv_b = conv_sums[4:5]
    g_ssd_norm = tot[GNORM_ROW:GNORM_ROW + 2].reshape(1, DI)
    g_final = tot[FINAL_ROW]
    g_sc_conv_w = lax.dynamic_slice(tot[SC_CONV_ROW:SC_CONV_ROW + 3], (0, chip * 256), (3, 256))[None]
    g_a_log, g_d, g_dt_bias = (tot[HEAD_ROW + r:HEAD_ROW + r + 1, 0:NH] for r in range(3))
    c_pad = jnp.concatenate([c_all, jnp.zeros((8, D), F32)], axis=0)
    dmod_all = jnp.stack([small_all[:, r] for r in mod_rows], axis=1).reshape(N_DEV, 2, 6 * D)
    g_ada_w = []
    for i in range(2):
        dm = lax.dynamic_slice(dmod_all[:, i], (0, chip * n_ada), (N_DEV, n_ada))
        g_ada_w.append(_matmul_tn(c_pad, jnp.concatenate([dm, jnp.zeros_like(dm)], axis=0), m=D, n=n_ada, a_silu=True,
                                  name=f"ada_dw{i}"))

    big = dict(ada_w=[(g, 0) for g in g_ada_w], mlp_up=[(t_b, up_row), (t_mlp1, up_row)],
               mlp_down=[(t_b, down_row), (t_mlp1, down_row)], ssd_out_w=[(t_ssd_out, 0)], sc_out_w=[(t_sc_out, 0)],
               sc_in_w=[(t_sc_in, 0)], ssd_in_w=None)
    grads = dict(ada_b=g_ada_b, mix_norm_w=g_mix_norm, mlp_norm_w=g_mlp_norm, ssd_conv_w=g_ssd_conv_w,
                 ssd_conv_b=g_ssd_conv_b, ssd_dt_bias=g_dt_bias, ssd_A_log=g_a_log, ssd_D=g_d, ssd_norm_w=g_ssd_norm,
                 sc_conv_w=g_sc_conv_w, final_norm_w=g_final)
    weights = dict(ada_w=(ada_w, m_ada_w, v_ada_w), ada_b=(ada_b, m_ada_b, v_ada_b),
                   mix_norm_w=(mix_norm_w, m_mix_norm_w, v_mix_norm_w), mlp_norm_w=(mlp_norm_w, m_mlp_norm_w, v_mlp_norm_w),
                   mlp_up=(mlp_up, m_mlp_up, v_mlp_up), mlp_down=(mlp_down, m_mlp_down, v_mlp_down),
                   ssd_in_w=(ssd_in_w, m_ssd_in_w, v_ssd_in_w), ssd_conv_w=(ssd_conv_w, m_ssd_conv_w, v_ssd_conv_w),
                   ssd_conv_b=(ssd_conv_b, m_ssd_conv_b, v_ssd_conv_b), ssd_dt_bias=(ssd_dt_bias, m_ssd_dt_bias, v_ssd_dt_bias),
                   ssd_A_log=(ssd_A_log, m_ssd_A_log, v_ssd_A_log), ssd_D=(ssd_D, m_ssd_D, v_ssd_D),
                   ssd_norm_w=(ssd_norm_w, m_ssd_norm_w, v_ssd_norm_w), ssd_out_w=(ssd_out_w, m_ssd_out_w, v_ssd_out_w),
                   sc_in_w=(sc_in_w, m_sc_in_w, v_sc_in_w), sc_conv_w=(sc_conv_w, m_sc_conv_w, v_sc_conv_w),
                   sc_out_w=(sc_out_w, m_sc_out_w, v_sc_out_w), final_norm_w=(final_norm_w, m_final_norm_w, v_final_norm_w))
    def step(nm, parts):
        w, m, v = (t if t.shape[0] == 1 else t.reshape(-1, t.shape[-1]) for t in weights[nm])
        rows, outs = w.shape[-2] // len(parts), None
        for i, (gbuf, g_row) in enumerate(parts):
            outs = _adamw(w, gbuf, m, v, g_row=g_row, w_row=i * rows, rows=rows, into=outs, emit_g=True, name=f"adamw_{nm}{i}")
        return outs

    res = {}
    for nm, (w, m, v) in weights.items():
        two_d = (-1, w.shape[-1]) if w.ndim > 1 else (1, -1)
        if nm not in big:
            res[nm] = (grads[nm], *_adamw(w.reshape(two_d), grads[nm].reshape(two_d), m.reshape(two_d), v.reshape(two_d),
                                          name="adamw_" + nm))
        elif big[nm] is not None:
            res[nm] = step(nm, big[nm])
    fly_2, tok = reduce_sum(fly_2, "rs2", tuple(r[1] for r in res.values()))
    (t_ssd_in,) = reduce_done(fly_2, "rs2", (tok,))
    w_t, m_t, v_t = (jnp.swapaxes(t[0], 0, 1) for t in weights["ssd_in_w"])
    res["ssd_in_w"] = [jnp.swapaxes(o, 0, 1) for o in _adamw(w_t, t_ssd_in.T, m_t, v_t, emit_g=True, name="adamw_ssd_in_w")]
    outs = [[res[nm][k].reshape(weights[nm][0].shape) for nm in weights] for k in range(4)]
    return (loss, grad_x[None], *outs[0], *outs[1], *outs[2], *outs[3])
```

```python
import jax
import jax.numpy as jnp
from jax import lax
from jax.experimental import pallas as pl
from jax.experimental.pallas import tpu as pltpu

F32 = jnp.float32
BF16 = jnp.bfloat16
MESH = pl.DeviceIdType.MESH

D = 1024
DFF = 4096
DI = 2048
NH = 32
HP = 64
NG = 4
NS = 128
CH = 128
CONVD = DI + 2 * NG * NS
ZX = DI + CONVD
GW = NG * NS
LANES = 128
N_CHIPS = 4
N_DEV = 8
EPS = 1e-5
ADAM_LR, ADAM_B1, ADAM_B2, ADAM_EPS, ADAM_WD, ADAM_STEP = 1e-3, 0.9, 0.999, 1e-8, 0.01, 10
VMEM_LIMIT = 48 * 1024 * 1024
TM_ALL = 2048
TM_HALF = 1024
ANY = pl.BlockSpec(memory_space=pl.ANY)
SEM = pl.BlockSpec(memory_space=pltpu.SEMAPHORE)

SSD_IN_SHARD = 1288
SC_IN_SHARD = 768


def _params(sem=None):
    return pltpu.CompilerParams(dimension_semantics=sem, vmem_limit_bytes=VMEM_LIMIT)


def _sigmoid(v):
    return 0.5 * jnp.tanh(0.5 * v) + 0.5


def _dot(a, b, dims=((1,), (0,)), precision=None):
    return lax.dot_general(a, b, (dims, ((), ())), preferred_element_type=F32, precision=precision)


def _dot_nt(a, b):
    return _dot(a, b, ((1,), (1,)))


def _dot_tn(a, b):
    return _dot(a, b, ((0,), (0,)))


def _nn(av, bv):
    return _dot(av.astype(BF16), bv.astype(BF16))


def _nt(av, bv):
    return _dot_nt(av.astype(BF16), bv.astype(BF16))


def _nn_split(av, bv):
    return _dot(av.astype(BF16), bv.reshape(-1, bv.shape[2]))


def _nn_split_sq(av, bv):
    return _nn_split(av * av, bv)


def _nt_split(av, bv):
    kc = bv.shape[2]
    acc = _dot_nt(av[:, 0:kc].astype(BF16), bv[0])
    for s in range(1, bv.shape[0]):
        acc = acc + _dot_nt(av[:, s * kc:(s + 1) * kc].astype(BF16), bv[s])
    return acc


def _nt_sc_in(av, bv):
    q = 256
    acc = None
    for i in range(3 * D // q):
        a_blk = av[i // 4][:, (i % 4) * q:(i % 4 + 1) * q]
        b_blk = bv[i // 3][:, (i % 3) * q:(i % 3 + 1) * q]
        t = _dot_nt(a_blk, b_blk)
        acc = t if acc is None else acc + t
    return acc


def _matmul(a, b, *, name, n, contract=_nn, a_spec=None, b_spec=None, tm=512, tn=512, extras=(), epi=None,
            out_dtypes=(F32,), a_silu=False):
    M = a.shape[-2]
    tm, tn = min(tm, M), min(tn, n)
    assert M % tm == 0 and n % tn == 0, (name, M, n, tm, tn)
    n_ex = len(extras)
    if a_spec is None:
        a_spec = pl.BlockSpec((tm, a.shape[1]), lambda i, j: (i, 0))
    if b_spec is None:
        b_spec = (pl.BlockSpec((tn, b.shape[1]), lambda i, j: (j, 0)) if contract is _nt
                  else pl.BlockSpec((b.shape[0], tn), lambda i, j: (0, j)))

    def body(*refs):
        av = refs[0][...]
        if a_silu:
            av = av * _sigmoid(av)
        acc = contract(av, refs[1][...])
        res = epi(acc, *[r[...] for r in refs[2:2 + n_ex]]) if epi is not None else (acc,)
        for o_ref, r in zip(refs[2 + n_ex:], res, strict=True):
            o_ref[...] = r.astype(o_ref.dtype)

    in_specs = [a_spec, b_spec]
    for e in extras:
        in_specs.append(pl.BlockSpec((1, tn), lambda i, j: (0, j)) if e.shape[0] == 1 and M != 1
                        else pl.BlockSpec((tm, tn), lambda i, j: (i, j)))
    outs = pl.pallas_call(
        body, grid=(M // tm, n // tn), in_specs=in_specs,
        out_specs=[pl.BlockSpec((tm, tn), lambda i, j: (i, j)) for _ in out_dtypes],
        out_shape=[jax.ShapeDtypeStruct((M, n), dt) for dt in out_dtypes],
        compiler_params=_params(("parallel", "parallel")), name=name)(a, b, *extras)
    return outs if len(out_dtypes) > 1 else outs[0]


def _matmul_tn(a, b, *, name, m, n, tm=512, tn=512, a_spec=None, b_spec=None, out_spec=None, out_struct=None, into=None,
               a_silu=False, a_square=False):
    T = a.shape[-2]
    tm, tn = min(tm, m), min(tn, n)
    assert m % tm == 0 and n % tn == 0, (name, m, n, tm, tn)
    if a_spec is None:
        a_spec = pl.BlockSpec((T, tm), lambda i, j: (0, i))
    if b_spec is None:
        b_spec = pl.BlockSpec((T, tn), lambda i, j: (0, j))
    if out_spec is None:
        out_spec, out_struct = pl.BlockSpec((tm, tn), lambda i, j: (i, j)), jax.ShapeDtypeStruct((m, n), F32)

    def body(a_ref, b_ref, *rest):
        av = a_ref[...]
        if a_silu:
            av = av * _sigmoid(av)
        if a_square:
            av = av * av
        rest[-1][...] = _dot_tn(av.astype(BF16), b_ref[...].astype(BF16)).astype(rest[-1].dtype)

    args, in_specs, alias = [a, b], [a_spec, b_spec], {}
    if into is not None:
        args, in_specs, alias = args + [into], in_specs + [ANY], {2: 0}
    return pl.pallas_call(body, grid=(m // tm, n // tn), in_specs=in_specs, out_specs=out_spec, out_shape=out_struct,
                          input_output_aliases=alias, compiler_params=_params(("parallel", "parallel")), name=name)(*args)


def _modnorm_fwd(x, nw, sc, sh, *, name):
    L = x.shape[0]
    tm = min(L, 512)

    def body(x_ref, nw_ref, sc_ref, sh_ref, h_ref):
        xv = x_ref[...]
        r = lax.rsqrt(jnp.mean(xv * xv, axis=-1, keepdims=True) + EPS)
        h_ref[...] = ((xv * r * nw_ref[...]) * (1.0 + sc_ref[...]) + sh_ref[...]).astype(BF16)

    row = pl.BlockSpec((tm, D), lambda i: (i, 0))
    vec = pl.BlockSpec((1, D), lambda i: (0, 0))
    return pl.pallas_call(body, grid=(L // tm,), in_specs=[row, vec, vec, vec], out_specs=row,
                          out_shape=jax.ShapeDtypeStruct((L, D), BF16),
                          compiler_params=_params(("parallel",)), name=name)(x, nw, sc, sh)


def _gate_outputs(dx, below_refs, dy_ref, gs_ref):
    g_ref, y_ref = below_refs
    dy_ref[...] = (dx * g_ref[...]).astype(BF16)
    gs_ref[0:1, :] += jnp.sum(dx * y_ref[...].astype(F32), axis=0, keepdims=True)


def _modnorm_bwd(x, dh, dxo, nw, sc, gsum, below, *, name):
    L = x.shape[0]
    tm = min(L, 256)
    nb = 0 if below is None else 2

    def body(x_ref, dh_ref, dxo_ref, nw_ref, sc_ref, g_ref, *rest):
        dx_ref, s_ref = rest[nb:nb + 2]

        @pl.when(pl.program_id(0) == 0)
        def _():
            s_ref[...] = g_ref[...]
            if nb:
                rest[-1][...] = jnp.zeros_like(rest[-1])

        xv, dhv = x_ref[...], dh_ref[...]
        r = lax.rsqrt(jnp.mean(xv * xv, axis=-1, keepdims=True) + EPS)
        xhat = xv * r
        dxhat = dhv * (nw_ref[...] * (1.0 + sc_ref[...]))
        dx = dxo_ref[...] + r * (dxhat - xhat * jnp.mean(dxhat * xhat, axis=-1, keepdims=True))
        dx_ref[...] = dx
        s_ref[1:2, :] += jnp.sum(dhv * xhat, axis=0, keepdims=True) * (1.0 + sc_ref[...])
        s_ref[2:3, :] += jnp.sum(dhv * xhat, axis=0, keepdims=True) * nw_ref[...]
        s_ref[3:4, :] += jnp.sum(dhv, axis=0, keepdims=True)
        if nb:
            _gate_outputs(dx, rest[:nb], rest[-2], rest[-1])

    row = pl.BlockSpec((tm, D), lambda i: (i, 0))
    vec = pl.BlockSpec((1, D), lambda i: (0, 0))
    blk = pl.BlockSpec((8, D), lambda i: (0, 0))
    in_specs, out_specs = [row, row, row, vec, vec, blk], [row, blk]
    out_shape = [jax.ShapeDtypeStruct((L, D), F32), jax.ShapeDtypeStruct((8, D), F32)]
    if nb:
        in_specs, out_specs = in_specs + [vec, row], out_specs + [row, blk]
        out_shape += [jax.ShapeDtypeStruct((L, D), BF16), jax.ShapeDtypeStruct((8, D), F32)]
    return pl.pallas_call(body, grid=(L // tm,), in_specs=in_specs, out_specs=out_specs, out_shape=out_shape,
                          compiler_params=_params(("arbitrary",)), name=name)(x, dh, dxo, nw, sc, gsum, *(below or ()))


def _final_loss(x, fw, tgt, below, *, name):
    L = x.shape[0]
    tm = min(L, 256)

    def body(x_ref, fw_ref, t_ref, g_ref, y_ref, dx_ref, s_ref, dy_ref, gs_ref):
        @pl.when(pl.program_id(0) == 0)
        def _():
            s_ref[...] = jnp.zeros_like(s_ref)
            gs_ref[...] = jnp.zeros_like(gs_ref)

        xv = x_ref[...]
        r = lax.rsqrt(jnp.mean(xv * xv, axis=-1, keepdims=True) + EPS)
        xhat = xv * r
        diff = xhat * fw_ref[...] - t_ref[...]
        dout = diff * (1.0 / D)
        dxhat = dout * fw_ref[...]
        dx = r * (dxhat - xhat * jnp.mean(dxhat * xhat, axis=-1, keepdims=True))
        dx_ref[...] = dx
        s_ref[0:1, :] += jnp.sum(dout * xhat, axis=0, keepdims=True)
        s_ref[1:2, :] += jnp.zeros((1, D), F32) + 0.5 * jnp.sum(jnp.sum(diff * diff, axis=-1, keepdims=True) * (1.0 / D))
        _gate_outputs(dx, (g_ref, y_ref), dy_ref, gs_ref)

    row = pl.BlockSpec((tm, D), lambda i: (i, 0))
    vec = pl.BlockSpec((1, D), lambda i: (0, 0))
    blk = pl.BlockSpec((8, D), lambda i: (0, 0))
    return pl.pallas_call(body, grid=(L // tm,), in_specs=[row, vec, row, vec, row], out_specs=[row, blk, row, blk],
                          out_shape=[jax.ShapeDtypeStruct((L, D), F32), jax.ShapeDtypeStruct((8, D), F32),
                                     jax.ShapeDtypeStruct((L, D), BF16), jax.ShapeDtypeStruct((8, D), F32)],
                          compiler_params=_params(("arbitrary",)), name=name)(x, fw, tgt, *below)


def _shift_down(v, j):
    if j == 0:
        return v
    rolled = pltpu.roll(v, j, 0)
    row = lax.broadcasted_iota(jnp.int32, (8, v.shape[1]), 0)
    return jnp.concatenate([jnp.where(row >= j, rolled[0:8], 0.0), rolled[8:]], axis=0)


def _shift_up(v, j):
    if j == 0:
        return v
    n = v.shape[0]
    rolled = pltpu.roll(v, n - j, 0)
    row = lax.broadcasted_iota(jnp.int32, (8, v.shape[1]), 0)
    return jnp.concatenate([rolled[:n - 8], jnp.where(row < 8 - j, rolled[n - 8:], 0.0)], axis=0)


def _ssd_conv_fwd(zx, w, b, *, name):
    L = zx.shape[0]
    cb = 256
    k = w.shape[0]

    def body(x_ref, w_ref, b_ref, o_ref, p_ref):
        xv = x_ref[...].astype(F32)
        pre = b_ref[...] + xv * w_ref[k - 1:k, :]
        for j in range(1, k):
            pre = pre + _shift_down(xv, j) * w_ref[k - 1 - j:k - j, :]
        o_ref[...] = (pre * _sigmoid(pre)).astype(BF16)
        p_ref[...] = pre.astype(BF16)

    blk = pl.BlockSpec((L, cb), lambda i: (0, i))
    return pl.pallas_call(
        body, grid=(CONVD // cb,),
        in_specs=[pl.BlockSpec((L, cb), lambda i: (0, i + DI // cb)), pl.BlockSpec((k, cb), lambda i: (0, i)),
                  pl.BlockSpec((1, cb), lambda i: (0, i))],
        out_specs=[blk, blk], out_shape=[jax.ShapeDtypeStruct((L, CONVD), BF16)] * 2,
        compiler_params=_params(("parallel",)), name=name)(zx, w, b)


def _ssd_conv_bwd(zx, pre, dact, w, dzx, *, name):
    L = zx.shape[0]
    cb = 256
    k = w.shape[0]

    def body(x_ref, p_ref, da_ref, w_ref, _, dx_ref, s_ref):
        xv, pv = x_ref[...].astype(F32), p_ref[...].astype(F32)
        s = _sigmoid(pv)
        dpre = da_ref[...].astype(F32) * (s * (1.0 + pv * (1.0 - s)))
        s_ref[...] = jnp.zeros_like(s_ref)
        s_ref[k:k + 1, :] = jnp.sum(dpre, axis=0, keepdims=True)
        s_ref[k - 1:k, :] = jnp.sum(dpre * xv, axis=0, keepdims=True)
        dx = dpre * w_ref[k - 1:k, :]
        for j in range(1, k):
            later = _shift_up(dpre, j)
            dx = dx + later * w_ref[k - 1 - j:k - j, :]
            s_ref[k - 1 - j:k - j, :] = jnp.sum(later * xv, axis=0, keepdims=True)
        dx_ref[...] = dx.astype(BF16)

    blk = pl.BlockSpec((L, cb), lambda i: (0, i))
    return pl.pallas_call(
        body, grid=(CONVD // cb,),
        in_specs=[pl.BlockSpec((L, cb), lambda i: (0, i + DI // cb)), blk, blk, pl.BlockSpec((k, cb), lambda i: (0, i)), ANY],
        out_specs=[pl.BlockSpec((L, cb), lambda i: (0, i + DI // cb)), pl.BlockSpec((8, cb), lambda i: (0, i))],
        out_shape=[jax.ShapeDtypeStruct((L, ZX), BF16), jax.ShapeDtypeStruct((8, CONVD), F32)],
        input_output_aliases={4: 0}, compiler_params=_params(("parallel",)), name=name)(zx, pre, dact, w, dzx)


def _sc_fwd(proj, w, *, name):
    L = proj.shape[0]
    cb = 256
    nb = D // cb
    k = w.shape[0]

    def body(b_ref, c_ref, x_ref, w_ref, o_ref, v_ref):
        u = c_ref[...].astype(F32) * x_ref[...].astype(F32)
        v = u * w_ref[k - 1:k, :]
        for j in range(1, k):
            v = v + _shift_down(u, j) * w_ref[k - 1 - j:k - j, :]
        o_ref[...] = (b_ref[...].astype(F32) * v).astype(BF16)
        v_ref[...] = v.astype(BF16)

    blk = pl.BlockSpec((L, cb), lambda i: (0, i))
    return pl.pallas_call(
        body, grid=(nb,),
        in_specs=[blk, pl.BlockSpec((L, cb), lambda i: (0, i + nb)), pl.BlockSpec((L, cb), lambda i: (0, i + 2 * nb)),
                  pl.BlockSpec((k, cb), lambda i: (0, i))],
        out_specs=[blk, blk], out_shape=[jax.ShapeDtypeStruct((L, D), BF16)] * 2,
        compiler_params=_params(("parallel",)), name=name)(proj, proj, proj, w)


def _sc_bwd(proj, v, dyv, w, *, name):
    L = proj.shape[0]
    cb = 256
    nb = D // cb
    k = w.shape[0]

    def body(b_ref, c_ref, x_ref, v_ref, dy_ref, w_ref, dp_ref, s_ref):
        cv, xv = c_ref[...].astype(F32), x_ref[...].astype(F32)
        u = cv * xv
        dyv_ = dy_ref[...]
        dp_ref[0] = (dyv_ * v_ref[...].astype(F32)).astype(BF16)
        dv = dyv_ * b_ref[...].astype(F32)
        s_ref[...] = jnp.zeros_like(s_ref)
        s_ref[k - 1:k, :] = jnp.sum(dv * u, axis=0, keepdims=True)
        du = dv * w_ref[k - 1:k, :]
        for j in range(1, k):
            later = _shift_up(dv, j)
            du = du + later * w_ref[k - 1 - j:k - j, :]
            s_ref[k - 1 - j:k - j, :] = jnp.sum(later * u, axis=0, keepdims=True)
        dp_ref[1] = (du * xv).astype(BF16)
        dp_ref[2] = (du * cv).astype(BF16)

    blk = pl.BlockSpec((L, cb), lambda i: (0, i))
    return pl.pallas_call(
        body, grid=(nb,),
        in_specs=[blk, pl.BlockSpec((L, cb), lambda i: (0, i + nb)), pl.BlockSpec((L, cb), lambda i: (0, i + 2 * nb)),
                  blk, blk, pl.BlockSpec((k, cb), lambda i: (0, i))],
        out_specs=[pl.BlockSpec((3, L, cb), lambda i: (0, 0, i)), pl.BlockSpec((8, cb), lambda i: (0, i))],
        out_shape=[jax.ShapeDtypeStruct((3, L, D), BF16), jax.ShapeDtypeStruct((8, D), F32)],
        compiler_params=_params(("parallel",)), name=name)(proj, proj, proj, v, dyv, w)


def _pieces(v, n):
    out, rest = [], v
    for _ in range(n):
        out.append(rest.astype(BF16))
        rest = rest - out[-1].astype(F32)
    return out


def _cumsum_rows(mask, v):
    m = mask.astype(BF16)
    return _dot(jnp.concatenate([m, m, m], axis=1), jnp.concatenate(_pieces(v, 3), axis=0))


def _ssd_chunk_terms(dtr, prm):
    lane = lax.broadcasted_iota(jnp.int32, (CH, LANES), 1)
    valid = lane < NH
    xdt = dtr + prm[0:1, :]
    dt = jnp.where(valid, jnp.maximum(xdt, 0.0) + jnp.log1p(jnp.exp(-jnp.abs(xdt))), 0.0)
    A = -jnp.exp(prm[1:2, :])
    ri = lax.broadcasted_iota(jnp.int32, (CH, CH), 0)
    ci = lax.broadcasted_iota(jnp.int32, (CH, CH), 1)
    cs = _cumsum_rows(ri >= ci, dt * A)
    last = cs[CH - 1:CH, :]
    spread = (lax.broadcasted_iota(jnp.int32, (2 * LANES, DI), 1) // HP
              == lax.broadcasted_iota(jnp.int32, (2 * LANES, DI), 0) % LANES).astype(BF16)
    gather = ((lax.broadcasted_iota(jnp.int32, (LANES, 2 * DI), 1) % DI) // HP
              == lax.broadcasted_iota(jnp.int32, (LANES, 2 * DI), 0)).astype(BF16)
    return dict(valid=valid, xdt=xdt, dt=dt, A=A, cs=cs, csT=cs.T, last=last, ri=ri, ci=ci, ex=(spread, gather))


def _expand(v, ex):
    if v.shape[0] == 1:
        return _expand(jnp.broadcast_to(v, (8, LANES)), ex)[0:1, :]
    return _dot(jnp.concatenate(_pieces(v, 2), axis=1), ex[0])


def _head_sum(v, ex):
    if v.shape[0] == 1:
        return _head_sum(jnp.broadcast_to(v, (8, DI)), ex)[0:1, :]
    return _dot_nt(jnp.concatenate(_pieces(v, 2), axis=1), ex[1])


def _ssd_fwd(xbc, dtr, prm, *, name):
    L = xbc.shape[0]
    nc = L // CH

    def body(xbc_ref, dtr_ref, prm_ref, y_ref, sp_ref, st_ref):
        @pl.when(pl.program_id(0) == 0)
        def _():
            st_ref[...] = jnp.zeros_like(st_ref)

        prm_v = prm_ref[...]
        t = _ssd_chunk_terms(dtr_ref[...], prm_v)
        cs, csT, ex, causal = t["cs"], t["csT"], t["ex"], t["ri"] >= t["ci"]
        xs = xbc_ref[:, 0:DI].astype(F32)
        X = xs * _expand(t["dt"], ex)
        Xb = X.astype(BF16)
        Xd = (X * _expand(jnp.exp(t["last"] - cs), ex)).astype(BF16)
        Ex = _expand(jnp.exp(cs), ex)
        cdx = _expand(jnp.exp(t["last"]), ex)
        dskx = _expand(prm_v[2:3, :], ex)
        lane = lax.broadcasted_iota(jnp.int32, (CH, LANES), 1)
        sp_ref[0] = st_ref[...]
        for g in range(NG):
            Bg = xbc_ref[:, DI + g * NS:DI + (g + 1) * NS].astype(BF16)
            Cg = xbc_ref[:, DI + GW + g * NS:DI + GW + (g + 1) * NS].astype(BF16)
            G = _dot_nt(Cg, Bg)
            Sg = st_ref[:, g * GW:(g + 1) * GW]
            yoff = _dot(Cg, Sg.astype(BF16)) * Ex[:, g * GW:(g + 1) * GW]
            for j in range(GW // LANES):
                lo = g * GW + j * LANES
                Xp = Xb[:, lo:lo + LANES]
                yd = []
                for h in (lo // HP, lo // HP + 1):
                    seg = cs[:, h:h + 1] - csT[h:h + 1, :]
                    yd.append(_dot((G * jnp.where(causal, jnp.exp(seg), 0.0)).astype(BF16), Xp))
                y_ref[:, lo:lo + LANES] = (jnp.where(lane < HP, yd[0], yd[1]) + yoff[:, j * LANES:(j + 1) * LANES]
                                           + dskx[:, lo:lo + LANES] * xs[:, lo:lo + LANES]).astype(BF16)
            st_ref[:, g * GW:(g + 1) * GW] = Sg * cdx[:, g * GW:(g + 1) * GW] + _dot_tn(Bg, Xd[:, g * GW:(g + 1) * GW])

    return pl.pallas_call(
        body, grid=(nc,),
        in_specs=[pl.BlockSpec((CH, CONVD), lambda c: (c, 0)), pl.BlockSpec((CH, LANES), lambda c: (c, 0)),
                  pl.BlockSpec((8, LANES), lambda c: (0, 0))],
        out_specs=[pl.BlockSpec((CH, DI), lambda c: (c, 0)), pl.BlockSpec((1, NS, DI), lambda c: (c, 0, 0))],
        out_shape=[jax.ShapeDtypeStruct((L, DI), BF16), jax.ShapeDtypeStruct((nc, NS, DI), F32)],
        scratch_shapes=[pltpu.VMEM((NS, DI), F32)],
        compiler_params=_params(("arbitrary",)), name=name)(xbc, dtr, prm)


def _ssd_bwd(xbc, dtr, prm, dy, sprev, *, name):
    L = xbc.shape[0]
    nc = L // CH

    def body(xbc_ref, dtr_ref, prm_ref, dy_ref, sp_ref, dxbc_ref, ddtr_ref, s_ref, dst_ref, dx_scr, de_scr, dd_scr):
        step = pl.program_id(0)

        @pl.when(step == 0)
        def _():
            dst_ref[...] = jnp.zeros_like(dst_ref)
            s_ref[...] = jnp.zeros_like(s_ref)

        prm_v = prm_ref[...]
        t = _ssd_chunk_terms(dtr_ref[...], prm_v)
        cs, csT, ex, ri, ci = t["cs"], t["csT"], t["ex"], t["ri"], t["ci"]
        E = jnp.exp(cs)
        dec = jnp.exp(t["last"] - cs)
        cd = jnp.exp(t["last"])
        xs = xbc_ref[:, 0:DI].astype(F32)
        dtx = _expand(t["dt"], ex)
        X = xs * dtx
        Xb = X.astype(BF16)
        decx = _expand(dec, ex)
        Xd = (X * decx).astype(BF16)
        Ex = _expand(E, ex)
        cdx = _expand(cd, ex)
        dskx = _expand(prm_v[2:3, :], ex)
        lane = lax.broadcasted_iota(jnp.int32, (CH, LANES), 1)
        dcs = jnp.zeros((CH, LANES), F32)
        dcd_x = []
        for g in range(NG):
            gs = slice(g * GW, (g + 1) * GW)
            Bg = xbc_ref[:, DI + g * NS:DI + (g + 1) * NS].astype(BF16)
            Cg = xbc_ref[:, DI + GW + g * NS:DI + GW + (g + 1) * NS].astype(BF16)
            G = _dot_nt(Cg, Bg)
            GT = _dot_nt(Bg, Cg)
            Sg = sp_ref[0, :, gs]
            Sgb = Sg.astype(BF16)
            dyg = dy_ref[:, gs]
            de_scr[:, gs] = dyg * _dot(Cg, Sgb)
            dYo = (Ex[:, gs] * dyg).astype(BF16)
            dC = _dot_nt(dYo, Sgb)
            dS_in = _dot_tn(Cg, dYo)
            dStg = dst_ref[:, gs]
            dStb = dStg.astype(BF16)
            dXd = _dot(Bg, dStb)
            dB = _dot_nt(Xd[:, gs], dStb)
            dd_scr[:, gs] = dXd * X[:, gs]
            dXst = dXd * decx[:, gs]
            dG = jnp.zeros((CH, CH), F32)
            dGT = jnp.zeros((CH, CH), F32)
            for j in range(GW // LANES):
                lo = g * GW + j * LANES
                Xp = Xb[:, lo:lo + LANES]
                dyp = dy_ref[:, lo:lo + LANES]
                dXp = dXst[:, j * LANES:(j + 1) * LANES]
                for k, h in enumerate((lo // HP, lo // HP + 1)):
                    dyh = jnp.where((lane < HP) if k == 0 else (lane >= HP), dyp, 0.0).astype(BF16)
                    seg = cs[:, h:h + 1] - csT[h:h + 1, :]
                    Lm = jnp.where(ri >= ci, jnp.exp(seg), 0.0)
                    LmT = jnp.where(ci >= ri, jnp.exp(-seg), 0.0)
                    dM = _dot_nt(dyh, Xp)
                    dMT = _dot_nt(Xp, dyh)
                    MT = GT * LmT
                    rs = jnp.sum(dM * (G * Lm), axis=1, keepdims=True) - jnp.sum(dMT * MT, axis=1, keepdims=True)
                    dcs = dcs + jnp.where(lane == h, rs, 0.0)
                    dG = dG + dM * Lm
                    dGT = dGT + dMT * LmT
                    dXp = dXp + _dot(MT.astype(BF16), dyh)
                dx_scr[:, lo:lo + LANES] = dXp
            dxbc_ref[:, DI + g * NS:DI + (g + 1) * NS] = (dB + _dot(dGT.astype(BF16), Cg)).astype(BF16)
            dxbc_ref[:, DI + GW + g * NS:DI + GW + (g + 1) * NS] = (dC + _dot(dG.astype(BF16), Bg)).astype(BF16)
            dcd_x.append(jnp.sum(dStg * Sg, axis=0, keepdims=True))
            dst_ref[:, gs] = dStg * cdx[:, gs] + dS_in
        dX = dx_scr[...]
        dy = dy_ref[...]
        ddec = _head_sum(dd_scr[...], ex)
        dcd = _head_sum(jnp.concatenate(dcd_x, axis=1), ex)
        dcs = dcs + _head_sum(de_scr[...], ex) * E - ddec * dec
        row = lax.broadcasted_iota(jnp.int32, (CH, LANES), 0)
        dcs = dcs + jnp.where(row == CH - 1, jnp.sum(ddec * dec, axis=0, keepdims=True) + dcd * cd, 0.0)
        da = _cumsum_rows(ci >= ri, dcs)
        ddt = da * t["A"] + _head_sum(dX * xs, ex)
        ddtr = jnp.where(t["valid"], ddt * _sigmoid(t["xdt"]), 0.0)
        ddtr_ref[...] = ddtr
        dxbc_ref[:, 0:DI] = (dX * dtx + dskx * dy).astype(BF16)
        s_ref[0:1, :] += jnp.sum(da * t["dt"], axis=0, keepdims=True)
        s_ref[1:2, :] += _head_sum(jnp.sum(dy * xs, axis=0, keepdims=True), ex)
        s_ref[2:3, :] += jnp.sum(ddtr, axis=0, keepdims=True)

        @pl.when(step == nc - 1)
        def _():
            s_ref[0:1, :] = s_ref[0:1, :] * t["A"]

    rev = lambda c: (nc - 1 - c, 0)
    return pl.pallas_call(
        body, grid=(nc,),
        in_specs=[pl.BlockSpec((CH, CONVD), rev), pl.BlockSpec((CH, LANES), rev), pl.BlockSpec((8, LANES), lambda c: (0, 0)),
                  pl.BlockSpec((CH, DI), rev), pl.BlockSpec((1, NS, DI), lambda c: (nc - 1 - c, 0, 0))],
        out_specs=[pl.BlockSpec((CH, CONVD), rev), pl.BlockSpec((CH, LANES), rev), pl.BlockSpec((8, LANES), lambda c: (0, 0))],
        out_shape=[jax.ShapeDtypeStruct((L, CONVD), BF16), jax.ShapeDtypeStruct((L, LANES), F32),
                   jax.ShapeDtypeStruct((8, LANES), F32)],
        scratch_shapes=[pltpu.VMEM((NS, DI), F32), pltpu.VMEM((CH, DI), F32), pltpu.VMEM((CH, DI), F32),
                        pltpu.VMEM((CH, DI), F32)],
        compiler_params=_params(("arbitrary",)), name=name)(xbc, dtr, prm, dy, sprev)


def _gnorm_fwd(y, zx, nw, *, name):
    L = y.shape[0]
    tm = min(L, 256)

    def body(y_ref, z_ref, nw_ref, o_ref):
        z = z_ref[...].astype(F32)
        yg = y_ref[...].astype(F32) * (z * _sigmoid(z))
        for g in range(NG):
            v = yg[:, g * GW:(g + 1) * GW]
            r = lax.rsqrt(jnp.mean(v * v, axis=-1, keepdims=True) + EPS)
            o_ref[:, g * GW:(g + 1) * GW] = (v * r * nw_ref[:, g * GW:(g + 1) * GW]).astype(BF16)

    row = pl.BlockSpec((tm, DI), lambda i: (i, 0))
    return pl.pallas_call(body, grid=(L // tm,), in_specs=[row, row, pl.BlockSpec((1, DI), lambda i: (0, 0))],
                          out_specs=row, out_shape=jax.ShapeDtypeStruct((L, DI), BF16),
                          compiler_params=_params(("parallel",)), name=name)(y, zx, nw)


def _gnorm_bwd(y, zx, nw, dyn, *, name):
    L = y.shape[0]
    tm = min(L, 256)

    def body(y_ref, z_ref, nw_ref, dyn_ref, dy_ref, dz_ref, s_ref):
        @pl.when(pl.program_id(0) == 0)
        def _():
            s_ref[...] = jnp.zeros_like(s_ref)

        z, yv = z_ref[...].astype(F32), y_ref[...].astype(F32)
        sz = _sigmoid(z)
        gate = z * sz
        dgate_dz = sz * (1.0 + z * (1.0 - sz))
        for g in range(NG):
            gs = slice(g * GW, (g + 1) * GW)
            v = yv[:, gs] * gate[:, gs]
            r = lax.rsqrt(jnp.mean(v * v, axis=-1, keepdims=True) + EPS)
            vhat = v * r
            dn = dyn_ref[:, gs].astype(F32)
            s_ref[0:1, gs] += jnp.sum(dn * vhat, axis=0, keepdims=True)
            dvhat = dn * nw_ref[:, gs]
            dv = r * (dvhat - vhat * jnp.mean(dvhat * vhat, axis=-1, keepdims=True))
            dy_ref[:, gs] = dv * gate[:, gs]
            dz_ref[:, gs] = (dv * yv[:, gs] * dgate_dz[:, gs]).astype(BF16)

    row = pl.BlockSpec((tm, DI), lambda i: (i, 0))
    return pl.pallas_call(body, grid=(L // tm,), in_specs=[row, row, pl.BlockSpec((1, DI), lambda i: (0, 0)), row],
                          out_specs=[row, row, pl.BlockSpec((8, DI), lambda i: (0, 0))],
                          out_shape=[jax.ShapeDtypeStruct((L, DI), F32), jax.ShapeDtypeStruct((L, ZX), BF16),
                                     jax.ShapeDtypeStruct((8, DI), F32)],
                          compiler_params=_params(("arbitrary",)), name=name)(y, zx, nw, dyn)


def _adamw(w, g, m, v, *, name, g_row=0, w_row=0, rows=None, into=None, emit_g=False):
    lead = w.ndim == 3
    R, C = w.shape[-2:]
    rows = R if rows is None else rows
    tr = max([t for t in range(8, rows + 1, 8) if rows % t == 0 and t * C <= 256 * 1024], default=rows)
    assert g_row % tr == 0 and w_row % tr == 0, (name, g_row, w_row, tr)
    n_out = 4 if emit_g else 3

    def body(w_ref, g_ref, m_ref, v_ref, *rest):
        outs = rest[-n_out:]
        gv = g_ref[...]
        mn = ADAM_B1 * m_ref[...] + (1.0 - ADAM_B1) * gv
        vn = ADAM_B2 * v_ref[...] + (1.0 - ADAM_B2) * (gv * gv)
        m_hat = mn / (1.0 - ADAM_B1 ** ADAM_STEP)
        v_hat = vn / (1.0 - ADAM_B2 ** ADAM_STEP)
        d_ref, mo_ref, vo_ref = outs[-3:]
        d_ref[...] = -ADAM_LR * (m_hat / (jnp.sqrt(v_hat) + ADAM_EPS) + ADAM_WD * w_ref[...])
        mo_ref[...] = mn
        vo_ref[...] = vn
        if emit_g:
            outs[0][...] = gv

    blk = (pl.BlockSpec((None, tr, C), lambda i: (0, i + w_row // tr, 0)) if lead
           else pl.BlockSpec((tr, C), lambda i: (i + w_row // tr, 0)))
    args, in_specs, alias = [w, g, m, v], [blk, pl.BlockSpec((tr, C), lambda i: (i + g_row // tr, 0)), blk, blk], {}
    if into is not None:
        args, in_specs, alias = args + list(into), in_specs + [ANY] * n_out, {4 + k: k for k in range(n_out)}
    return pl.pallas_call(body, grid=(rows // tr,), in_specs=in_specs, out_specs=[blk] * n_out,
                          out_shape=[jax.ShapeDtypeStruct(w.shape, F32)] * n_out, input_output_aliases=alias,
                          compiler_params=_params(("parallel",)), name=name)(*args)


def _residual(acc, xv, gv):
    return xv + gv * acc, acc


def _like(buf):
    return jax.ShapeDtypeStruct(buf.shape, buf.dtype)


def _mlp_fwd(x, mod, nw, wb, up_row, down_row, tag, midway=None):
    sh, sc, g = mod
    h = _modnorm_fwd(x, nw, sc, sh, name=tag + "_norm")
    a = _matmul(h, wb, n=DFF, tm=TM_ALL, b_spec=pl.BlockSpec((None, D, 512), lambda mi, j: (j // 2, up_row // D, j % 2)),
                epi=lambda acc: (jnp.maximum(acc, 0.0),), out_dtypes=(BF16,), name=tag + "_up")
    if midway is not None:
        midway(a)
    xn, y = _matmul(a, wb, n=D, tm=TM_HALF, contract=_nn_split_sq,
                    b_spec=pl.BlockSpec((N_CHIPS, D, 512), lambda mi, j: (0, down_row // D, j)),
                    extras=(x, g), epi=_residual, out_dtypes=(F32, BF16), name=tag + "_down")
    return xn, (x, h, a, y)


def _mlp_bwd(dxo, dy, gsum, saved, mod, nw, wb, gb, up_row, down_row, below, tag):
    x, h, a, y = saved
    sh, sc, g = mod
    du = _matmul(dy, wb, n=DFF, tm=TM_ALL, contract=_nt,
                 b_spec=pl.BlockSpec((None, 512, D), lambda mi, j: (j // 2, down_row // 512 + j % 2, 0)),
                 extras=(a,), epi=lambda acc, av: (acc * (2.0 * av.astype(F32)),), out_dtypes=(BF16,), name=tag + "_dact")
    gb = _matmul_tn(a, dy, m=DFF, n=D, tm=D, tn=D, a_square=True, into=gb, out_struct=_like(wb),
                    out_spec=pl.BlockSpec((None, D, D), lambda mi, j: (mi, down_row // D, 0)), name=tag + "_ddown")
    dh = _matmul(du, wb, n=D, tm=TM_HALF, contract=_nt_split,
                 b_spec=pl.BlockSpec((N_CHIPS, 512, D), lambda mi, j: (0, up_row // 512 + j, 0)), name=tag + "_dh")
    gb = _matmul_tn(h, du, m=D, n=DFF, tm=D, into=gb, out_struct=_like(wb),
                    out_spec=pl.BlockSpec((None, D, 512), lambda mi, j: (j // 2, up_row // D, j % 2)), name=tag + "_dup")
    dx, sums, *nxt = _modnorm_bwd(x, dh, dxo, nw, sc, gsum, below, name=tag + "_dnorm")
    return dx, gb, sums, *nxt


def _ssd_fwd_scan(x, mod, nw, w_in_t, w_dt_t, conv_w, conv_b, prm, tag):
    sh, sc, g = mod
    h = _modnorm_fwd(x, nw, sc, sh, name=tag + "_norm")
    zx = _matmul(h, w_in_t, n=ZX, tm=TM_ALL, contract=_nt, out_dtypes=(BF16,), name=tag + "_in")
    dtr = _matmul(h, w_dt_t, n=LANES, tm=TM_ALL, contract=_nt, name=tag + "_in_dt")
    xbc, pre = _ssd_conv_fwd(zx, conv_w, conv_b, name=tag + "_conv")
    y, sprev = _ssd_fwd(xbc, dtr, prm, name=tag + "_scan")
    return h, zx, dtr, xbc, y, sprev, pre


def _ssd_fwd_out(x, mod, scan, gn_w, get_w_out, tag):
    sh, sc, g = mod
    h, zx, dtr, xbc, y, sprev, pre = scan
    yn = _gnorm_fwd(y, zx, gn_w, name=tag + "_gnorm")
    w_out = get_w_out(yn)
    xn, yo = _matmul(yn, w_out, n=D, tm=TM_HALF, contract=_nn_split,
                     b_spec=pl.BlockSpec((N_CHIPS, 512, 512), lambda mi, j: (0, 0, j)),
                     extras=(x, g), epi=_residual, out_dtypes=(F32, BF16), name=tag + "_out")
    return xn, (x, h, zx, dtr, xbc, y, sprev, yn, yo, pre)


def _ssd_bwd_out(dyo, saved, w_out, tag):
    x, h, zx, dtr, xbc, y, sprev, yn, yo, pre = saved
    dyn = _matmul(dyo, w_out, n=DI, tm=TM_ALL, contract=_nt, b_spec=pl.BlockSpec((None, 512, D), lambda mi, j: (j, 0, 0)),
                  out_dtypes=(BF16,), name=tag + "_dyn")
    g_out = _matmul_tn(yn, dyo, m=DI, n=D, tn=D, out_struct=_like(w_out),
                       out_spec=pl.BlockSpec((None, 512, D), lambda mi, j: (mi, 0, 0)), name=tag + "_dout")
    return dyn, g_out


def _ssd_bwd_rest(dxo, dy, dzx, gsum, saved, mod, nw, w_in_t, w_dt_t, conv_w, prm, tag):
    x, h, zx, dtr, xbc, y, sprev, yn, yo, pre = saved
    sh, sc, g = mod
    dxbc, ddtr, ssum = _ssd_bwd(xbc, dtr, prm, dy, sprev, name=tag + "_dscan")
    dzx, csum = _ssd_conv_bwd(zx, pre, dxbc, conv_w, dzx, name=tag + "_dconv")
    dh_dt = _matmul(ddtr, w_dt_t, n=D, tm=TM_ALL, name=tag + "_dh_dt")
    dh = _matmul(dzx, w_in_t, n=D, tm=TM_HALF, b_spec=pl.BlockSpec((ZX, 512), lambda mi, j: (0, j)), extras=(dh_dt,),
                 epi=lambda acc, e: (acc + e,), name=tag + "_dh")
    d_w_zx = _matmul_tn(h, dzx, m=D, n=ZX, tm=D, name=tag + "_din")
    d_w_dt = _matmul_tn(h, ddtr, m=D, n=LANES, tm=D, name=tag + "_din_dt")
    dx, sums = _modnorm_bwd(x, dh, dxo, nw, sc, gsum, None, name=tag + "_dnorm")
    return dx, d_w_zx, d_w_dt, sums, csum, ssum


def _sc_layer_fwd(x, mod, nw, w_sc_in, conv_w, wb, out_row, tag, midway=None):
    sh, sc, g = mod
    h = _modnorm_fwd(x, nw, sc, sh, name=tag + "_norm")
    proj = _matmul(h, w_sc_in, n=3 * D, tm=TM_ALL, tn=256, out_dtypes=(BF16,),
                   b_spec=pl.BlockSpec((None, D, 256), lambda mi, j: (j // 3, 0, j % 3)),
                   name=tag + "_in")
    if midway is not None:
        midway(proj)
    yv, v = _sc_fwd(proj, conv_w, name=tag + "_conv")
    xn, yo = _matmul(yv, wb, n=D, tm=TM_HALF, contract=_nn_split,
                     b_spec=pl.BlockSpec((N_CHIPS, 256, 512), lambda mi, j: (0, out_row // 256, j)),
                     extras=(x, g), epi=_residual, out_dtypes=(F32, BF16), name=tag + "_out")
    return xn, (x, h, proj, yv, yo, v)


def _sc_layer_bwd(dxo, dyo, gsum, saved, mod, nw, w_sc_in, conv_w, wb, gb, out_row, below, tag):
    x, h, proj, yv, yo, v = saved
    sh, sc, g = mod
    L = x.shape[0]
    dyv = _matmul(dyo, wb, n=D, tm=TM_ALL, tn=256, contract=_nt,
                  b_spec=pl.BlockSpec((None, 256, D), lambda mi, j: (j, out_row // 256, 0)), name=tag + "_dyv")
    gb = _matmul_tn(yv, dyo, m=D, n=D, tm=256, tn=D, into=gb, out_struct=_like(wb),
                    out_spec=pl.BlockSpec((None, 256, D), lambda mi, j: (mi, out_row // 256, 0)), name=tag + "_dout")
    dproj, csum = _sc_bwd(proj, v, dyv, conv_w, name=tag + "_dconv")
    tm = min(L, TM_HALF)
    dh = _matmul(dproj, w_sc_in, n=D, tm=tm, contract=_nt_sc_in, a_spec=pl.BlockSpec((3, tm, D), lambda mi, j: (0, mi, 0)),
                 b_spec=pl.BlockSpec((N_CHIPS, 512, SC_IN_SHARD), lambda mi, j: (0, j, 0)), name=tag + "_dh")
    g_sc_in = _matmul_tn(h, dproj, m=D, n=3 * D, tm=D, tn=256, b_spec=pl.BlockSpec((None, L, 256), lambda mi, j: (j // 4, 0, j % 4)),
                         out_spec=pl.BlockSpec((None, D, 256), lambda mi, j: (j // 3, 0, j % 3)),
                         out_struct=jax.ShapeDtypeStruct((N_CHIPS, D, SC_IN_SHARD), BF16), name=tag + "_din")
    dx, sums, *nxt = _modnorm_bwd(x, dh, dxo, nw, sc, gsum, below, name=tag + "_dnorm")
    return dx, gb, g_sc_in, sums, csum, *nxt


SUB_ROW = (0, 8, 16, 24)
SSD_CONV_ROW, GNORM_ROW, FINAL_ROW, SC_CONV_ROW, HEAD_ROW, SMALL_ROWS = 32, 48, 56, 64, 72, 80


def _all_gather_rows(blk, *, name):
    m_per, n = blk.shape

    def body(x_ref, out_ref, send_sems, recv_sems, local_sem):
        x, y, c = lax.axis_index("x"), lax.axis_index("y"), lax.axis_index("c")
        me, sibling = (x, y, c), (x, y, 1 - c)
        chips = [(1 - x, y), (x, 1 - y), (1 - x, 1 - y)]

        def rows(px, py, pc):
            return out_ref.at[pl.ds((4 * px + 2 * py + pc) * m_per, m_per), :]

        def copy(k, block, to, src=None):
            return pltpu.make_async_remote_copy(src_ref=rows(*block) if src is None else src, dst_ref=rows(*block),
                                                send_sem=send_sems.at[k], recv_sem=recv_sems.at[k], device_id=to,
                                                device_id_type=MESH)

        mine = pltpu.make_async_copy(x_ref, rows(*me), local_sem)
        mine.start()
        first = [copy(0, me, sibling, src=x_ref)] + [copy(1 + j, me, (*chip, c), src=x_ref) for j, chip in enumerate(chips)]
        for cp in first:
            cp.start()
        passed = [copy(4 + j, (*chip, c), sibling) for j, chip in enumerate(chips)]
        for j, chip in enumerate(chips):
            copy(1 + j, (*chip, c), me).wait_recv()
            passed[j].start()
        copy(0, sibling, me).wait_recv()
        for j, chip in enumerate(chips):
            copy(4 + j, (*chip, 1 - c), me).wait_recv()
        for cp in first + passed:
            cp.wait_send()
        mine.wait()

    return pl.pallas_call(
        body, out_shape=jax.ShapeDtypeStruct((N_DEV * m_per, n), blk.dtype),
        in_specs=[pl.BlockSpec(memory_space=pltpu.VMEM)], out_specs=pl.BlockSpec(memory_space=pltpu.VMEM),
        scratch_shapes=[pltpu.SemaphoreType.DMA((7,)), pltpu.SemaphoreType.DMA((7,)), pltpu.SemaphoreType.DMA],
        name=name)(blk)


def _half(ref, chip, c):
    r, n = ref.shape[1:]
    if r % 32 == 0:
        return ref.at[chip, pl.ds(c * (r // 2), r // 2), :]
    assert n % 256 == 0, ref.shape
    return ref.at[chip, :, pl.ds(c * (n // 2), n // 2)]


def _gather_copy(bufs, sends, recvs, b, k, chip, pc, to):
    piece = _half(bufs[b], 2 * chip[0] + chip[1], pc)
    return pltpu.make_async_remote_copy(src_ref=piece, dst_ref=piece, send_sem=sends.at[4 * b + k], recv_sem=recvs.at[4 * b + k],
                                        device_id=to, device_id_type=MESH)


def _split_call(body, bufs, sems_in, n_sems, *, name, after=(), token=False, lands=()):
    nb, na, nl, starts = len(bufs), len(after), len(lands), not sems_in

    def wrapped(*refs):
        sems = refs[nb + na:nb + na + 2] if starts else refs[nb:nb + 2]
        made = refs[nb + na + 2 + nb:nb + na + 2 + nb + nl] if starts else ()
        body(tuple(refs[:nb]) + tuple(made), sems[0], sems[1])
        if token:
            refs[-1][...] = jnp.zeros_like(refs[-1])

    out_shape = [pltpu.SemaphoreType.DMA((n_sems,)) for _ in range(2 if starts else 0)]
    out_specs = [SEM] * len(out_shape) + [ANY] * (nb + nl)
    alias = {b: len(out_shape) + b for b in range(nb)}
    out_shape += [jax.ShapeDtypeStruct(b.shape, b.dtype) for b in bufs] + list(lands)
    if token:
        out_shape.append(jax.ShapeDtypeStruct((8, LANES), F32))
        out_specs.append(pl.BlockSpec(memory_space=pltpu.VMEM))
    return pl.pallas_call(
        wrapped, out_shape=out_shape, in_specs=[ANY] * nb + [SEM] * len(sems_in) + [ANY] * na, out_specs=out_specs,
        input_output_aliases=alias,
        compiler_params=pltpu.CompilerParams(has_side_effects=pltpu.SideEffectType.DATAFLOW_SIDE_EFFECTING),
        name=name)(*bufs, *sems_in, *after)


def _gather_start(bufs, *, name, after=()):
    nb = len(bufs)

    def body(ins, sends, recvs):
        x, y, c = lax.axis_index("x"), lax.axis_index("y"), lax.axis_index("c")
        chips = [(1 - x, y), (x, 1 - y), (1 - x, 1 - y)]
        for b in range(nb):
            _gather_copy(ins, sends, recvs, b, 0, (x, y), c, (x, y, 1 - c)).start()
            for j, chip in enumerate(chips):
                _gather_copy(ins, sends, recvs, b, 1 + j, (x, y), c, (*chip, c)).start()

    out = _split_call(body, bufs, (), 4 * nb, name=name, after=after, token=True)
    return (out[0], out[1], out[2:2 + nb]), out[-1]


def _gather_wait_first(flight, *, name, after=()):
    sends, recvs, bufs = flight
    nb = len(bufs)

    def body(ins, sends_, recvs_):
        x, y, c = lax.axis_index("x"), lax.axis_index("y"), lax.axis_index("c")
        chips = [(1 - x, y), (x, 1 - y), (1 - x, 1 - y)]
        for b in range(nb):
            _gather_copy(ins, sends_, recvs_, b, 0, (x, y), c, (x, y, 1 - c)).wait_send()
            _gather_copy(ins, sends_, recvs_, b, 0, (x, y), 1 - c, (x, y, c)).wait_recv()
            for j, chip in enumerate(chips):
                _gather_copy(ins, sends_, recvs_, b, 1 + j, (x, y), c, (*chip, c)).wait_send()
                _gather_copy(ins, sends_, recvs_, b, 1 + j, chip, c, (x, y, c)).wait_recv()

    return _split_call(body, bufs, (sends, recvs), 4 * nb, name=name, after=after)


def _gather_forward(bufs, *, name):
    nb = len(bufs)

    def body(ins, sends, recvs):
        x, y, c = lax.axis_index("x"), lax.axis_index("y"), lax.axis_index("c")
        chips = [(1 - x, y), (x, 1 - y), (1 - x, 1 - y)]
        for b in range(nb):
            for j, chip in enumerate(chips):
                _gather_copy(ins, sends, recvs, b, 1 + j, chip, c, (x, y, 1 - c)).start()

    out = _split_call(body, bufs, (), 4 * nb, name=name)
    return out[0], out[1], out[2:2 + nb]


def _gather_wait_forward(flight, *, name, after=()):
    sends, recvs, bufs = flight
    nb = len(bufs)

    def body(ins, sends_, recvs_):
        x, y, c = lax.axis_index("x"), lax.axis_index("y"), lax.axis_index("c")
        chips = [(1 - x, y), (x, 1 - y), (1 - x, 1 - y)]
        for b in range(nb):
            for j, chip in enumerate(chips):
                _gather_copy(ins, sends_, recvs_, b, 1 + j, chip, c, (x, y, 1 - c)).wait_send()
                _gather_copy(ins, sends_, recvs_, b, 1 + j, chip, 1 - c, (x, y, c)).wait_recv()

    return _split_call(body, bufs, (sends, recvs), 4 * nb, name=name, after=after)


def _owner_copies(hs, lands, sends, recvs):
    x, y, c = lax.axis_index("x"), lax.axis_index("y"), lax.axis_index("c")
    chips = [(1 - x, y), (x, 1 - y), (1 - x, 1 - y)]
    return [pltpu.make_async_remote_copy(src_ref=hs[b].at[2 * cx + cy], dst_ref=lands[b].at[j], send_sem=sends.at[3 * b + j],
                                         recv_sem=recvs.at[3 * b + j], device_id=(cx, cy, c), device_id_type=MESH)
            for b in range(len(hs)) for j, (cx, cy) in enumerate(chips)]


def _owners_start(hs, *, name):
    nb = len(hs)
    lands = [jax.ShapeDtypeStruct((3,) + h.shape[1:], h.dtype) for h in hs]

    def body(refs, sends, recvs):
        for cp in _owner_copies(refs[:nb], refs[nb:], sends, recvs):
            cp.start()

    out = _split_call(body, list(hs), (), 3 * nb, name=name, token=True, lands=lands)
    return (out[0], out[1], out[2:2 + 2 * nb]), out[-1]


def _owners_wait(flight, *, name, after=()):
    sends, recvs, bufs = flight
    nb = len(bufs) // 2

    def body(refs, sends_, recvs_):
        for cp in _owner_copies(refs[:nb], refs[nb:], sends_, recvs_):
            cp.wait()

    out = _split_call(body, bufs, (sends, recvs), 3 * nb, name=name, after=after)
    return out[:nb], out[nb:]


def _sibling_copies(gs, lands, sends, recvs):
    x, y, c = lax.axis_index("x"), lax.axis_index("y"), lax.axis_index("c")
    copies = []
    for b in range(len(gs)):
        hr = gs[b].shape[1] // 2
        copies.append(pltpu.make_async_remote_copy(
            src_ref=gs[b].at[:, pl.ds((1 - c) * hr, hr), :], dst_ref=lands[b], send_sem=sends.at[b], recv_sem=recvs.at[b],
            device_id=(x, y, 1 - c), device_id_type=MESH))
    return copies


def _sibling_start(gs, *, name, after=()):
    nb = len(gs)
    lands = [jax.ShapeDtypeStruct((g.shape[0], g.shape[1] // 2, g.shape[2]), g.dtype) for g in gs]

    def body(refs, sends, recvs):
        for cp in _sibling_copies(refs[:nb], refs[nb:], sends, recvs):
            cp.start()

    out = _split_call(body, list(gs), (), nb, name=name, after=after, token=True, lands=lands)
    return (out[0], out[1], out[2:2 + 2 * nb]), out[-1]


def _sibling_wait(flight, *, name, after=()):
    sends, recvs, bufs = flight
    nb = len(bufs) // 2

    def body(refs, sends_, recvs_):
        for cp in _sibling_copies(refs[:nb], refs[nb:], sends_, recvs_):
            cp.wait()

    out = _split_call(body, bufs, (sends, recvs), nb, name=name, after=after)
    return out[:nb], out[nb:]


def _result_copies(ts, sends, recvs):
    x, y, c = lax.axis_index("x"), lax.axis_index("y"), lax.axis_index("c")
    return [pltpu.make_async_remote_copy(src_ref=ts[b].at[c], dst_ref=ts[b].at[c], send_sem=sends.at[b], recv_sem=recvs.at[b],
                                         device_id=(x, y, 1 - c), device_id_type=MESH) for b in range(len(ts))]


def _result_start(ts, *, name):
    def body(refs, sends, recvs):
        for cp in _result_copies(refs, sends, recvs):
            cp.start()

    out = _split_call(body, ts, (), len(ts), name=name, token=True)
    return (out[0], out[1], out[2:2 + len(ts)]), out[-1]


def _result_wait(flight, *, name, after=()):
    sends, recvs, bufs = flight

    def body(refs, sends_, recvs_):
        for cp in _result_copies(refs, sends_, recvs_):
            cp.wait()

    return _split_call(body, bufs, (sends, recvs), len(bufs), name=name, after=after)


def _row_tile(rows, cols):
    best = 16
    for t in range(16, rows + 1, 16):
        if rows % t == 0 and t * cols <= 640 * 1024:
            best = t
    assert rows % best == 0, (rows, cols)
    return best


def _add_sibling_half(g, recv, core, *, name):
    nk, r, n = g.shape
    hr = r // 2
    tr = _row_tile(hr, n)

    def body(c_ref, a_ref, b_ref, o_ref):
        o_ref[...] = (a_ref[...].astype(F32) + b_ref[...].astype(F32)).astype(BF16)

    grid_spec = pltpu.PrefetchScalarGridSpec(
        num_scalar_prefetch=1, grid=(nk, hr // tr),
        in_specs=[pl.BlockSpec((None, tr, n), lambda k, i, c_ref: (k, c_ref[0] * (hr // tr) + i, 0)),
                  pl.BlockSpec((None, tr, n), lambda k, i, c_ref: (k, i, 0))],
        out_specs=pl.BlockSpec((None, tr, n), lambda k, i, c_ref: (k, i, 0)))
    return pl.pallas_call(body, grid_spec=grid_spec, out_shape=jax.ShapeDtypeStruct((nk, hr, n), BF16),
                          compiler_params=_params(("parallel", "parallel")), name=name)(core, g, recv)


def _add_chip_sums(h, recv, chip_core, *, name):
    _, hr, n = h.shape
    tr = _row_tile(hr, n)

    def body(k_ref, a_ref, b_ref, o_ref):
        o_ref[...] = ((a_ref[...].astype(F32) + b_ref[0].astype(F32)) + b_ref[1].astype(F32)) + b_ref[2].astype(F32)

    grid_spec = pltpu.PrefetchScalarGridSpec(
        num_scalar_prefetch=1, grid=(hr // tr,),
        in_specs=[pl.BlockSpec((None, tr, n), lambda i, k_ref: (k_ref[0], i, 0)),
                  pl.BlockSpec((3, tr, n), lambda i, k_ref: (0, i, 0))],
        out_specs=pl.BlockSpec((None, tr, n), lambda i, k_ref: (k_ref[1], i, 0)))
    return pl.pallas_call(body, grid_spec=grid_spec, out_shape=jax.ShapeDtypeStruct((2, hr, n), F32),
                          compiler_params=_params(("parallel",)), name=name)(chip_core, h, recv)


def _sum_devices(g, *, name):
    nd, r, n = g.shape

    def body(g_ref, o_ref):
        acc = g_ref[0]
        for i in range(1, nd):
            acc = acc + g_ref[i]
        o_ref[...] = acc

    return pl.pallas_call(body, out_shape=jax.ShapeDtypeStruct((r, n), F32), name=name)(g)


def _own_slot(parts, chip, *, name):
    rows, cols = sum(w.shape[1] for w, _ in parts), parts[0][0].shape[2]
    buf, row0 = None, 0
    for p, (w, idx) in enumerate(parts):
        r = w.shape[1]
        tr = 256 if r % 256 == 0 else r
        assert row0 % tr == 0, (name, r, row0)

        def body(chip_ref, w_ref, *rest):
            rest[-1][...] = w_ref[...].astype(BF16)

        grid_spec = pltpu.PrefetchScalarGridSpec(
            num_scalar_prefetch=1, grid=(r // tr,),
            in_specs=[pl.BlockSpec((None, tr, cols), lambda i, c_ref, idx=idx: (idx, i, 0))] + ([] if buf is None else [ANY]),
            out_specs=pl.BlockSpec((None, tr, cols), lambda i, c_ref, row0=row0, tr=tr: (c_ref[0], row0 // tr + i, 0)))
        buf = pl.pallas_call(body, grid_spec=grid_spec, out_shape=jax.ShapeDtypeStruct((N_CHIPS, rows, cols), BF16),
                             input_output_aliases={} if buf is None else {2: 0}, compiler_params=_params(("parallel",)),
                             name=f"{name}{p}")(chip, w, *(() if buf is None else (buf,)))
        row0 += r
    return buf


def kernel(x, c, ada_w, ada_b, mix_norm_w, mlp_norm_w, mlp_up, mlp_down, ssd_in_w, ssd_conv_w, ssd_conv_b, ssd_dt_bias, ssd_A_log, ssd_D, ssd_norm_w, ssd_out_w, sc_in_w, sc_conv_w, sc_out_w, final_norm_w, loss_target, m_ada_w, m_ada_b, m_mix_norm_w, m_mlp_norm_w, m_mlp_up, m_mlp_down, m_ssd_in_w, m_ssd_conv_w, m_ssd_conv_b, m_ssd_dt_bias, m_ssd_A_log, m_ssd_D, m_ssd_norm_w, m_ssd_out_w, m_sc_in_w, m_sc_conv_w, m_sc_out_w, m_final_norm_w, v_ada_w, v_ada_b, v_mix_norm_w, v_mlp_norm_w, v_mlp_up, v_mlp_down, v_ssd_in_w, v_ssd_conv_w, v_ssd_conv_b, v_ssd_dt_bias, v_ssd_A_log, v_ssd_D, v_ssd_norm_w, v_ssd_out_w, v_sc_in_w, v_sc_conv_w, v_sc_out_w, v_final_norm_w):
    xi, yi, ci = lax.axis_index("x"), lax.axis_index("y"), lax.axis_index("c")
    chip = 2 * xi + yi
    dev = 2 * chip + ci
    n_ada = ada_w.shape[2]

    conv_flat = jnp.concatenate([ssd_conv_w.reshape(-1), sc_conv_w.reshape(-1), jnp.zeros((256,), F32)]).reshape(4, D)
    blk0 = jnp.concatenate([c, conv_flat, jnp.zeros((3, D), F32)], axis=0)
    got0 = _all_gather_rows(blk0, name="gather_cond").reshape(N_DEV, 8, D)
    c_all = got0[:, 0]
    conv_all = got0[0::2, 1:5].reshape(N_CHIPS, 4 * D)
    ssd_conv = jnp.moveaxis(conv_all[:, :4 * 768].reshape(N_CHIPS, 4, 768), 0, 1).reshape(4, CONVD)
    sc_conv = jnp.moveaxis(conv_all[:, 4 * 768:4 * 768 + 3 * 256].reshape(N_CHIPS, 3, 256), 0, 1).reshape(3, D)
    mod_shard = [_matmul(c_all, ada_w, n=n_ada, a_silu=True, b_spec=pl.BlockSpec((None, D, 512), lambda mi, j, i=i: (i, 0, j)),
                         extras=(lax.dynamic_slice(ada_b, (i, chip * n_ada), (1, n_ada)),),
                         epi=lambda acc, b: (acc + b,), name=f"ada_mod{i}") for i in range(2)]
    mod_all = _all_gather_rows(jnp.concatenate(mod_shard, axis=0), name="gather_mod")
    mod_all = mod_all.reshape(N_DEV, 2, N_DEV, n_ada)[0::2]
    mod = jnp.moveaxis(lax.dynamic_index_in_dim(mod_all, dev, axis=2, keepdims=False), 0, 1).reshape(2, 6, D)
    mods = [[mod[i, j:j + 1] for j in range(6)] for i in range(2)]

    up_row, down_row = 0, D
    chip1 = chip.reshape(1).astype(jnp.int32)
    a_bufs = [_own_slot([(jnp.swapaxes(ssd_in_w, 1, 2), 0)], chip1, name="slot_ssd_in")]
    b_bufs = [_own_slot([(ssd_out_w, 0)], chip1, name="slot_ssd_out"),
              _own_slot([(mlp_up, 0), (mlp_down, 0)], chip1, name="slot_mlp0_")]
    c_bufs = [_own_slot([(sc_in_w, 0)], chip1, name="slot_sc_in"), _own_slot([(sc_out_w, 0)], chip1, name="slot_sc_out")]
    d_bufs = [_own_slot([(mlp_up, 1), (mlp_down, 1)], chip1, name="slot_mlp1_")]
    fly_a, tok = _gather_start(a_bufs, name="gather_a_start", after=(mod,))
    fly_b, tok = _gather_start(b_bufs, name="gather_b_start", after=(tok,))
    fly_c, tok = _gather_start(c_bufs, name="gather_c_start", after=(tok,))
    fly_d, tok = _gather_start(d_bufs, name="gather_d_start", after=(tok,))

    row = lambda v: v.reshape(1, -1)
    xs, tgt = x[0], loss_target[0]
    prm = jnp.pad(jnp.concatenate([ssd_dt_bias, ssd_A_log, ssd_D, jnp.zeros((5, NH), F32)], axis=0), ((0, 0), (0, LANES - NH)))
    mix_nw = [row(mix_norm_w[i]) for i in range(2)]
    mlp_nw = [row(mlp_norm_w[i]) for i in range(2)]
    a_bufs = _gather_wait_first(fly_a, name="gather_a_landed", after=(tok,))
    (w_ssd_in,) = _gather_wait_forward(_gather_forward(a_bufs, name="gather_a_pass"), name="gather_a_done")
    w_in_t = w_ssd_in.reshape(N_CHIPS * SSD_IN_SHARD, D)
    w_dt_t = jnp.pad(w_in_t[ZX:], ((0, LANES - NH), (0, 0)))
    scan = _ssd_fwd_scan(xs, mods[0][0:3], mix_nw[0], w_in_t, w_dt_t, ssd_conv, ssd_conv_b, prm, "ssd")

    def land(flight, tag, after):
        return _gather_forward(_gather_wait_first(flight, name=f"gather_{tag}_landed", after=(after,)), name=f"gather_{tag}_pass")

    passed, got = {"b": land(fly_b, "b", scan[4])}, {}

    def done(tag, after):
        got[tag] = _gather_wait_forward(passed[tag], name=f"gather_{tag}_done", after=(after,))
        return got[tag]

    x1, s_ssd = _ssd_fwd_out(xs, mods[0][0:3], scan, ssd_norm_w, lambda yn: done("b", yn)[0], "ssd")
    w_ssd_out, w_b = got["b"]
    x2, s_mlp0 = _mlp_fwd(x1, mods[0][3:6], mlp_nw[0], w_b, up_row, down_row, "mlp0",
                          midway=lambda a: passed.update(c=land(fly_c, "c", a)))
    w_sc_in, w_sc_out = done("c", x2)
    x3, s_sc = _sc_layer_fwd(x2, mods[1][0:3], mix_nw[1], w_sc_in, sc_conv, w_sc_out, 0, "sc",
                             midway=lambda proj: passed.update(d=land(fly_d, "d", proj)))
    (w_mlp1,) = done("d", x3)
    x4, s_mlp1 = _mlp_fwd(x3, mods[1][3:6], mlp_nw[1], w_mlp1, up_row, down_row, "mlp1")

    core = ci.reshape(1).astype(jnp.int32)
    chip_core = jnp.stack([chip, ci]).astype(jnp.int32)

    def reduce_swap(gbufs, tag, after=()):
        return _sibling_start(gbufs, name=tag + "_sibling_start", after=after)

    def reduce_send(flight, tag, after):
        gs, sib = _sibling_wait(flight, name=tag + "_sibling_landed", after=after)
        hs = [_add_sibling_half(g, s, core, name=f"{tag}_add_sibling{b}") for b, (g, s) in enumerate(zip(gs, sib))]
        return _owners_start(hs, name=tag + "_owners_start")

    def reduce_sum(flight, tag, after):
        hs, lands = _owners_wait(flight, name=tag + "_owners_landed", after=after)
        ts = [_add_chip_sums(h, o, chip_core, name=f"{tag}_add_chips{b}") for b, (h, o) in enumerate(zip(hs, lands))]
        return _result_start(ts, name=tag + "_result_start")

    def reduce_done(flight, tag, after=()):
        return [t.reshape(-1, t.shape[2]) for t in _result_wait(flight, name=tag + "_result_landed", after=after)]

    dx4, fsum, dy, gs = _final_loss(x4, row(final_norm_w), tgt, (mods[1][5], s_mlp1[3]), name="final_loss")
    dx3, g_mlp1, sum_mlp1, dy, gs = _mlp_bwd(dx4, dy, gs, s_mlp1, mods[1][3:6], mlp_nw[1], w_mlp1, None, up_row, down_row,
                                             (mods[1][2], s_sc[4]), "mlp1")
    dx2, g_sc_out, g_sc_in, sum_sc, sc_csum, dy, gs = _sc_layer_bwd(dx3, dy, gs, s_sc, mods[1][0:3], mix_nw[1], w_sc_in,
                                                                    sc_conv, w_sc_out, None, 0, (mods[0][5], s_mlp0[3]), "sc")
    dx1, g_b, sum_mlp0, dy, gsum_ssd = _mlp_bwd(dx2, dy, gs, s_mlp0, mods[0][3:6], mlp_nw[0], w_b, None, up_row, down_row,
                                                (mods[0][2], s_ssd[8]), "mlp0")
    dyn, g_ssd_out = _ssd_bwd_out(dy, s_ssd, w_ssd_out, "ssd")
    fly_1, tok = reduce_swap([g_mlp1, g_sc_out, g_sc_in, g_b, g_ssd_out], "rs1")
    dy, dzx, gnsum = _gnorm_bwd(s_ssd[5], s_ssd[2], ssd_norm_w + tok[0:1, 0:1], dyn, name="ssd_dgnorm")
    fly_1, tok = reduce_send(fly_1, "rs1", (dy,))
    grad_x, d_w_zx, d_w_dt, sum_ssd, csum, ssum = _ssd_bwd_rest(
        dx1, dy, dzx, gsum_ssd, s_ssd, mods[0][0:3], mix_nw[0], w_in_t, w_dt_t, ssd_conv, prm + tok[0:1, 0:1], "ssd")
    fly_1, tok = reduce_sum(fly_1, "rs1", (grad_x,))

    def ssd_in_owner(k):
        lo, hi = k * SSD_IN_SHARD, (k + 1) * SSD_IN_SHARD
        if hi <= ZX:
            return d_w_zx[:, lo:hi]
        return jnp.concatenate([d_w_zx[:, lo:], d_w_dt[:, :hi - ZX]], axis=1)

    small = jnp.concatenate([sum_ssd + tok[0:1, 0:1], sum_mlp0, sum_sc, sum_mlp1, csum.reshape(24, D)[0:16], gnsum.reshape(16, D)[0:8],
                             fsum, sc_csum, jnp.pad(ssum, ((0, 0), (0, D - LANES)))], axis=0)
    small_all = _all_gather_rows(small, name="gather_small").reshape(N_DEV, SMALL_ROWS, D)
    fly_2, tok = reduce_swap([jnp.stack([ssd_in_owner(k) for k in range(N_CHIPS)]).astype(BF16)], "rs2", (small_all,))
    fly_2, tok = reduce_send(fly_2, "rs2", (tok,))
    t_mlp1, t_sc_out, t_sc_in, t_b, t_ssd_out = reduce_done(fly_1, "rs1", (tok,))
    small_all = small_all + tok[0:1, 0:1]
    tot = _sum_devices(small_all, name="sum_small")
    loss = tot[FINAL_ROW + 1, 0]
    mod_rows = [r + o for r in SUB_ROW for o in (3, 2, 0)]
    g_ada_b = jnp.stack([tot[r] for r in mod_rows]).reshape(2, 6 * D)
    g_mix_norm = jnp.stack([tot[SUB_ROW[0] + 1], tot[SUB_ROW[2] + 1]])
    g_mlp_norm = jnp.stack([tot[SUB_ROW[1] + 1], tot[SUB_ROW[3] + 1]])
    conv_sums = tot[SSD_CONV_ROW:SSD_CONV_ROW + 15].reshape(5, CONVD)
    g_ssd_conv_w = lax.dynamic_slice(conv_sums, (0, chip * 768), (4, 768))[None]
    g_ssd_conv_b = conv_sums[4:5]
    g_ssd_norm = tot[GNORM_ROW:GNORM_ROW + 2].reshape(1, DI)
    g_final = tot[FINAL_ROW]
    g_sc_conv_w = lax.dynamic_slice(tot[SC_CONV_ROW:SC_CONV_ROW + 3], (0, chip * 256), (3, 256))[None]
    g_a_log, g_d, g_dt_bias = (tot[HEAD_ROW + r:HEAD_ROW + r + 1, 0:NH] for r in range(3))
    c_pad = jnp.concatenate([c_all, jnp.zeros((8, D), F32)], axis=0)
    dmod_all = jnp.stack([small_all[:, r] for r in mod_rows], axis=1).reshape(N_DEV, 2, 6 * D)
    g_ada_w = []
    for i in range(2):
        dm = lax.dynamic_slice(dmod_all[:, i], (0, chip * n_ada), (N_DEV, n_ada))
        g_ada_w.append(_matmul_tn(c_pad, jnp.concatenate([dm, jnp.zeros_like(dm)], axis=0), m=D, n=n_ada, a_silu=True,
                                  name=f"ada_dw{i}"))

    big = dict(ada_w=[(g, 0) for g in g_ada_w], mlp_up=[(t_b, up_row), (t_mlp1, up_row)],
               mlp_down=[(t_b, down_row), (t_mlp1, down_row)], ssd_out_w=[(t_ssd_out, 0)], sc_out_w=[(t_sc_out, 0)],
               sc_in_w=[(t_sc_in, 0)], ssd_in_w=None)
    grads = dict(ada_b=g_ada_b, mix_norm_w=g_mix_norm, mlp_norm_w=g_mlp_norm, ssd_conv_w=g_ssd_conv_w,
                 ssd_conv_b=g_ssd_conv_b, ssd_dt_bias=g_dt_bias, ssd_A_log=g_a_log, ssd_D=g_d, ssd_norm_w=g_ssd_norm,
                 sc_conv_w=g_sc_conv_w, final_norm_w=g_final)
    weights = dict(ada_w=(ada_w, m_ada_w, v_ada_w), ada_b=(ada_b, m_ada_b, v_ada_b),
                   mix_norm_w=(mix_norm_w, m_mix_norm_w, v_mix_norm_w), mlp_norm_w=(mlp_norm_w, m_mlp_norm_w, v_mlp_norm_w),
                   mlp_up=(mlp_up, m_mlp_up, v_mlp_up), mlp_down=(mlp_down, m_mlp_down, v_mlp_down),
                   ssd_in_w=(ssd_in_w, m_ssd_in_w, v_ssd_in_w), ssd_conv_w=(ssd_conv_w, m_ssd_conv_w, v_ssd_conv_w),
                   ssd_conv_b=(ssd_conv_b, m_ssd_conv_b, v_ssd_conv_b), ssd_dt_bias=(ssd_dt_bias, m_ssd_dt_bias, v_ssd_dt_bias),
                   ssd_A_log=(ssd_A_log, m_ssd_A_log, v_ssd_A_log), ssd_D=(ssd_D, m_ssd_D, v_ssd_D),
                   ssd_norm_w=(ssd_norm_w, m_ssd_norm_w, v_ssd_norm_w), ssd_out_w=(ssd_out_w, m_ssd_out_w, v_ssd_out_w),
                   sc_in_w=(sc_in_w, m_sc_in_w, v_sc_in_w), sc_conv_w=(sc_conv_w, m_sc_conv_w, v_sc_conv_w),
                   sc_out_w=(sc_out_w, m_sc_out_w, v_sc_out_w), final_norm_w=(final_norm_w, m_final_norm_w, v_final_norm_w))
    def step(nm, parts):
        w, m, v = (t if t.shape[0] == 1 else t.reshape(-1, t.shape[-1]) for t in weights[nm])
        rows, outs = w.shape[-2] // len(parts), None
        for i, (gbuf, g_row) in enumerate(parts):
            outs = _adamw(w, gbuf, m, v, g_row=g_row, w_row=i * rows, rows=rows, into=outs, emit_g=True, name=f"adamw_{nm}{i}")
        return outs

    res = {}
    for nm, (w, m, v) in weights.items():
        two_d = (-1, w.shape[-1]) if w.ndim > 1 else (1, -1)
        if nm not in big:
            res[nm] = (grads[nm], *_adamw(w.reshape(two_d), grads[nm].reshape(two_d), m.reshape(two_d), v.reshape(two_d),
                                          name="adamw_" + nm))
        elif big[nm] is not None:
            res[nm] = step(nm, big[nm])
    fly_2, tok = reduce_sum(fly_2, "rs2", tuple(r[1] for r in res.values()))
    (t_ssd_in,) = reduce_done(fly_2, "rs2", (tok,))
    w_t, m_t, v_t = (jnp.swapaxes(t[0], 0, 1) for t in weights["ssd_in_w"])
    res["ssd_in_w"] = [jnp.swapaxes(o, 0, 1) for o in _adamw(w_t, t_ssd_in.T, m_t, v_t, emit_g=True, name="adamw_ssd_in_w")]
    outs = [[res[nm][k].reshape(weights[nm][0].shape) for nm in weights] for k in range(4)]
    return (loss, grad_x[None], *outs[0], *outs[1], *outs[2], *outs[3])
```

```python
import jax
import jax.numpy as jnp
from jax import lax
from jax.experimental import pallas as pl
from jax.experimental.pallas import tpu as pltpu

F32 = jnp.float32
BF16 = jnp.bfloat16
MESH = pl.DeviceIdType.MESH

D = 1024
DFF = 4096
DI = 2048
NH = 32
HP = 64
NG = 4
NS = 128
CH = 128
CONVD = DI + 2 * NG * NS
ZX = DI + CONVD
GW = NG * NS
LANES = 128
N_CHIPS = 4
N_DEV = 8
EPS = 1e-5
ADAM_LR, ADAM_B1, ADAM_B2, ADAM_EPS, ADAM_WD, ADAM_STEP = 1e-3, 0.9, 0.999, 1e-8, 0.01, 10
VMEM_LIMIT = 48 * 1024 * 1024
TM_ALL = 2048
TM_HALF = 1024
ANY = pl.BlockSpec(memory_space=pl.ANY)
SEM = pl.BlockSpec(memory_space=pltpu.SEMAPHORE)

SSD_IN_SHARD = 1288
SC_IN_SHARD = 768


def _params(sem=None):
    return pltpu.CompilerParams(dimension_semantics=sem, vmem_limit_bytes=VMEM_LIMIT)


def _sigmoid(v):
    return 0.5 * jnp.tanh(0.5 * v) + 0.5


def _dot(a, b, dims=((1,), (0,)), precision=None):
    return lax.dot_general(a, b, (dims, ((), ())), preferred_element_type=F32, precision=precision)


def _dot_nt(a, b):
    return _dot(a, b, ((1,), (1,)))


def _dot_tn(a, b):
    return _dot(a, b, ((0,), (0,)))


def _nn(av, bv):
    return _dot(av.astype(BF16), bv.astype(BF16))


def _nt(av, bv):
    return _dot_nt(av.astype(BF16), bv.astype(BF16))


def _nn_split(av, bv):
    return _dot(av.astype(BF16), bv.reshape(-1, bv.shape[2]))


def _nn_split_sq(av, bv):
    return _nn_split(av * av, bv)


def _nt_split(av, bv):
    kc = bv.shape[2]
    acc = _dot_nt(av[:, 0:kc].astype(BF16), bv[0])
    for s in range(1, bv.shape[0]):
        acc = acc + _dot_nt(av[:, s * kc:(s + 1) * kc].astype(BF16), bv[s])
    return acc


def _nt_sc_in(av, bv):
    q = 256
    acc = None
    for i in range(3 * D // q):
        a_blk = av[i // 4][:, (i % 4) * q:(i % 4 + 1) * q]
        b_blk = bv[i // 3][:, (i % 3) * q:(i % 3 + 1) * q]
        t = _dot_nt(a_blk, b_blk)
        acc = t if acc is None else acc + t
    return acc


def _matmul(a, b, *, name, n, contract=_nn, a_spec=None, b_spec=None, tm=512, tn=512, extras=(), epi=None,
            out_dtypes=(F32,), a_silu=False):
    M = a.shape[-2]
    tm, tn = min(tm, M), min(tn, n)
    assert M % tm == 0 and n % tn == 0, (name, M, n, tm, tn)
    n_ex = len(extras)
    if a_spec is None:
        a_spec = pl.BlockSpec((tm, a.shape[1]), lambda i, j: (i, 0))
    if b_spec is None:
        b_spec = (pl.BlockSpec((tn, b.shape[1]), lambda i, j: (j, 0)) if contract is _nt
                  else pl.BlockSpec((b.shape[0], tn), lambda i, j: (0, j)))

    def body(*refs):
        av = refs[0][...]
        if a_silu:
            av = av * _sigmoid(av)
        acc = contract(av, refs[1][...])
        res = epi(acc, *[r[...] for r in refs[2:2 + n_ex]]) if epi is not None else (acc,)
        for o_ref, r in zip(refs[2 + n_ex:], res, strict=True):
            o_ref[...] = r.astype(o_ref.dtype)

    in_specs = [a_spec, b_spec]
    for e in extras:
        in_specs.append(pl.BlockSpec((1, tn), lambda i, j: (0, j)) if e.shape[0] == 1 and M != 1
                        else pl.BlockSpec((tm, tn), lambda i, j: (i, j)))
    outs = pl.pallas_call(
        body, grid=(M // tm, n // tn), in_specs=in_specs,
        out_specs=[pl.BlockSpec((tm, tn), lambda i, j: (i, j)) for _ in out_dtypes],
        out_shape=[jax.ShapeDtypeStruct((M, n), dt) for dt in out_dtypes],
        compiler_params=_params(("parallel", "parallel")), name=name)(a, b, *extras)
    return outs if len(out_dtypes) > 1 else outs[0]


def _matmul_tn(a, b, *, name, m, n, tm=512, tn=512, a_spec=None, b_spec=None, out_spec=None, out_struct=None, into=None,
               a_silu=False, a_square=False):
    T = a.shape[-2]
    tm, tn = min(tm, m), min(tn, n)
    assert m % tm == 0 and n % tn == 0, (name, m, n, tm, tn)
    if a_spec is None:
        a_spec = pl.BlockSpec((T, tm), lambda i, j: (0, i))
    if b_spec is None:
        b_spec = pl.BlockSpec((T, tn), lambda i, j: (0, j))
    if out_spec is None:
        out_spec, out_struct = pl.BlockSpec((tm, tn), lambda i, j: (i, j)), jax.ShapeDtypeStruct((m, n), F32)

    def body(a_ref, b_ref, *rest):
        av = a_ref[...]
        if a_silu:
            av = av * _sigmoid(av)
        if a_square:
            av = av * av
        rest[-1][...] = _dot_tn(av.astype(BF16), b_ref[...].astype(BF16)).astype(rest[-1].dtype)

    args, in_specs, alias = [a, b], [a_spec, b_spec], {}
    if into is not None:
        args, in_specs, alias = args + [into], in_specs + [ANY], {2: 0}
    return pl.pallas_call(body, grid=(m // tm, n // tn), in_specs=in_specs, out_specs=out_spec, out_shape=out_struct,
                          input_output_aliases=alias, compiler_params=_params(("parallel", "parallel")), name=name)(*args)


def _modnorm_fwd(x, nw, sc, sh, *, name):
    L = x.shape[0]
    tm = min(L, 512)

    def body(x_ref, nw_ref, sc_ref, sh_ref, h_ref):
        xv = x_ref[...]
        r = lax.rsqrt(jnp.mean(xv * xv, axis=-1, keepdims=True) + EPS)
        h_ref[...] = ((xv * r * nw_ref[...]) * (1.0 + sc_ref[...]) + sh_ref[...]).astype(BF16)

    row = pl.BlockSpec((tm, D), lambda i: (i, 0))
    vec = pl.BlockSpec((1, D), lambda i: (0, 0))
    return pl.pallas_call(body, grid=(L // tm,), in_specs=[row, vec, vec, vec], out_specs=row,
                          out_shape=jax.ShapeDtypeStruct((L, D), BF16),
                          compiler_params=_params(("parallel",)), name=name)(x, nw, sc, sh)


def _gate_outputs(dx, below_refs, dy_ref, gs_ref):
    g_ref, y_ref = below_refs
    dy_ref[...] = (dx * g_ref[...]).astype(BF16)
    gs_ref[0:1, :] += jnp.sum(dx * y_ref[...].astype(F32), axis=0, keepdims=True)


def _modnorm_bwd(x, dh, dxo, nw, sc, gsum, below, *, name):
    L = x.shape[0]
    tm = min(L, 256)
    nb = 0 if below is None else 2

    def body(x_ref, dh_ref, dxo_ref, nw_ref, sc_ref, g_ref, *rest):
        dx_ref, s_ref = rest[nb:nb + 2]

        @pl.when(pl.program_id(0) == 0)
        def _():
            s_ref[...] = g_ref[...]
            if nb:
                rest[-1][...] = jnp.zeros_like(rest[-1])

        xv, dhv = x_ref[...], dh_ref[...].astype(F32)
        r = lax.rsqrt(jnp.mean(xv * xv, axis=-1, keepdims=True) + EPS)
        xhat = xv * r
        dxhat = dhv * (nw_ref[...] * (1.0 + sc_ref[...]))
        dx = dxo_ref[...] + r * (dxhat - xhat * jnp.mean(dxhat * xhat, axis=-1, keepdims=True))
        dx_ref[...] = dx
        s_ref[1:2, :] += jnp.sum(dhv * xhat, axis=0, keepdims=True) * (1.0 + sc_ref[...])
        s_ref[2:3, :] += jnp.sum(dhv * xhat, axis=0, keepdims=True) * nw_ref[...]
        s_ref[3:4, :] += jnp.sum(dhv, axis=0, keepdims=True)
        if nb:
            _gate_outputs(dx, rest[:nb], rest[-2], rest[-1])

    row = pl.BlockSpec((tm, D), lambda i: (i, 0))
    vec = pl.BlockSpec((1, D), lambda i: (0, 0))
    blk = pl.BlockSpec((8, D), lambda i: (0, 0))
    in_specs, out_specs = [row, row, row, vec, vec, blk], [row, blk]
    out_shape = [jax.ShapeDtypeStruct((L, D), F32), jax.ShapeDtypeStruct((8, D), F32)]
    if nb:
        in_specs, out_specs = in_specs + [vec, row], out_specs + [row, blk]
        out_shape += [jax.ShapeDtypeStruct((L, D), BF16), jax.ShapeDtypeStruct((8, D), F32)]
    return pl.pallas_call(body, grid=(L // tm,), in_specs=in_specs, out_specs=out_specs, out_shape=out_shape,
                          compiler_params=_params(("arbitrary",)), name=name)(x, dh, dxo, nw, sc, gsum, *(below or ()))


def _final_loss(x, fw, tgt, below, *, name):
    L = x.shape[0]
    tm = min(L, 256)

    def body(x_ref, fw_ref, t_ref, g_ref, y_ref, dx_ref, s_ref, dy_ref, gs_ref):
        @pl.when(pl.program_id(0) == 0)
        def _():
            s_ref[...] = jnp.zeros_like(s_ref)
            gs_ref[...] = jnp.zeros_like(gs_ref)

        xv = x_ref[...]
        r = lax.rsqrt(jnp.mean(xv * xv, axis=-1, keepdims=True) + EPS)
        xhat = xv * r
        diff = xhat * fw_ref[...] - t_ref[...]
        dout = diff * (1.0 / D)
        dxhat = dout * fw_ref[...]
        dx = r * (dxhat - xhat * jnp.mean(dxhat * xhat, axis=-1, keepdims=True))
        dx_ref[...] = dx
        s_ref[0:1, :] += jnp.sum(dout * xhat, axis=0, keepdims=True)
        s_ref[1:2, :] += jnp.zeros((1, D), F32) + 0.5 * jnp.sum(jnp.sum(diff * diff, axis=-1, keepdims=True) * (1.0 / D))
        _gate_outputs(dx, (g_ref, y_ref), dy_ref, gs_ref)

    row = pl.BlockSpec((tm, D), lambda i: (i, 0))
    vec = pl.BlockSpec((1, D), lambda i: (0, 0))
    blk = pl.BlockSpec((8, D), lambda i: (0, 0))
    return pl.pallas_call(body, grid=(L // tm,), in_specs=[row, vec, row, vec, row], out_specs=[row, blk, row, blk],
                          out_shape=[jax.ShapeDtypeStruct((L, D), F32), jax.ShapeDtypeStruct((8, D), F32),
                                     jax.ShapeDtypeStruct((L, D), BF16), jax.ShapeDtypeStruct((8, D), F32)],
                          compiler_params=_params(("arbitrary",)), name=name)(x, fw, tgt, *below)


def _shift_down(v, j):
    if j == 0:
        return v
    rolled = pltpu.roll(v, j, 0)
    row = lax.broadcasted_iota(jnp.int32, (8, v.shape[1]), 0)
    return jnp.concatenate([jnp.where(row >= j, rolled[0:8], 0.0), rolled[8:]], axis=0)


def _shift_up(v, j):
    if j == 0:
        return v
    n = v.shape[0]
    rolled = pltpu.roll(v, n - j, 0)
    row = lax.broadcasted_iota(jnp.int32, (8, v.shape[1]), 0)
    return jnp.concatenate([rolled[:n - 8], jnp.where(row < 8 - j, rolled[n - 8:], 0.0)], axis=0)


def _ssd_conv_fwd(zx, w, b, *, name):
    L = zx.shape[0]
    cb = 256
    k = w.shape[0]

    def body(x_ref, w_ref, b_ref, o_ref, p_ref):
        xv = x_ref[...].astype(F32)
        pre = b_ref[...] + xv * w_ref[k - 1:k, :]
        for j in range(1, k):
            pre = pre + _shift_down(xv, j) * w_ref[k - 1 - j:k - j, :]
        o_ref[...] = (pre * _sigmoid(pre)).astype(BF16)
        p_ref[...] = pre.astype(BF16)

    blk = pl.BlockSpec((L, cb), lambda i: (0, i))
    return pl.pallas_call(
        body, grid=(CONVD // cb,),
        in_specs=[pl.BlockSpec((L, cb), lambda i: (0, i + DI // cb)), pl.BlockSpec((k, cb), lambda i: (0, i)),
                  pl.BlockSpec((1, cb), lambda i: (0, i))],
        out_specs=[blk, blk], out_shape=[jax.ShapeDtypeStruct((L, CONVD), BF16)] * 2,
        compiler_params=_params(("parallel",)), name=name)(zx, w, b)


def _ssd_conv_bwd(zx, pre, dact, w, dzx, *, name):
    L = zx.shape[0]
    cb = 256
    k = w.shape[0]

    def body(x_ref, p_ref, da_ref, w_ref, _, dx_ref, s_ref):
        xv, pv = x_ref[...].astype(F32), p_ref[...].astype(F32)
        s = _sigmoid(pv)
        dpre = da_ref[...].astype(F32) * (s * (1.0 + pv * (1.0 - s)))
        s_ref[...] = jnp.zeros_like(s_ref)
        s_ref[k:k + 1, :] = jnp.sum(dpre, axis=0, keepdims=True)
        s_ref[k - 1:k, :] = jnp.sum(dpre * xv, axis=0, keepdims=True)
        dx = dpre * w_ref[k - 1:k, :]
        for j in range(1, k):
            later = _shift_up(dpre, j)
            dx = dx + later * w_ref[k - 1 - j:k - j, :]
            s_ref[k - 1 - j:k - j, :] = jnp.sum(later * xv, axis=0, keepdims=True)
        dx_ref[...] = dx.astype(BF16)

    blk = pl.BlockSpec((L, cb), lambda i: (0, i))
    return pl.pallas_call(
        body, grid=(CONVD // cb,),
        in_specs=[pl.BlockSpec((L, cb), lambda i: (0, i + DI // cb)), blk, blk, pl.BlockSpec((k, cb), lambda i: (0, i)), ANY],
        out_specs=[pl.BlockSpec((L, cb), lambda i: (0, i + DI // cb)), pl.BlockSpec((8, cb), lambda i: (0, i))],
        out_shape=[jax.ShapeDtypeStruct((L, ZX), BF16), jax.ShapeDtypeStruct((8, CONVD), F32)],
        input_output_aliases={4: 0}, compiler_params=_params(("parallel",)), name=name)(zx, pre, dact, w, dzx)


def _sc_fwd(proj, w, *, name):
    L = proj.shape[0]
    cb = 256
    nb = D // cb
    k = w.shape[0]

    def body(b_ref, c_ref, x_ref, w_ref, o_ref, v_ref):
        u = c_ref[...].astype(F32) * x_ref[...].astype(F32)
        v = u * w_ref[k - 1:k, :]
        for j in range(1, k):
            v = v + _shift_down(u, j) * w_ref[k - 1 - j:k - j, :]
        o_ref[...] = (b_ref[...].astype(F32) * v).astype(BF16)
        v_ref[...] = v.astype(BF16)

    blk = pl.BlockSpec((L, cb), lambda i: (0, i))
    return pl.pallas_call(
        body, grid=(nb,),
        in_specs=[blk, pl.BlockSpec((L, cb), lambda i: (0, i + nb)), pl.BlockSpec((L, cb), lambda i: (0, i + 2 * nb)),
                  pl.BlockSpec((k, cb), lambda i: (0, i))],
        out_specs=[blk, blk], out_shape=[jax.ShapeDtypeStruct((L, D), BF16)] * 2,
        compiler_params=_params(("parallel",)), name=name)(proj, proj, proj, w)


def _sc_bwd(proj, v, dyv, w, *, name):
    L = proj.shape[0]
    cb = 256
    nb = D // cb
    k = w.shape[0]

    def body(b_ref, c_ref, x_ref, v_ref, dy_ref, w_ref, dp_ref, s_ref):
        cv, xv = c_ref[...].astype(F32), x_ref[...].astype(F32)
        u = cv * xv
        dyv_ = dy_ref[...].astype(F32)
        dp_ref[0] = (dyv_ * v_ref[...].astype(F32)).astype(BF16)
        dv = dyv_ * b_ref[...].astype(F32)
        s_ref[...] = jnp.zeros_like(s_ref)
        s_ref[k - 1:k, :] = jnp.sum(dv * u, axis=0, keepdims=True)
        du = dv * w_ref[k - 1:k, :]
        for j in range(1, k):
            later = _shift_up(dv, j)
            du = du + later * w_ref[k - 1 - j:k - j, :]
            s_ref[k - 1 - j:k - j, :] = jnp.sum(later * u, axis=0, keepdims=True)
        dp_ref[1] = (du * xv).astype(BF16)
        dp_ref[2] = (du * cv).astype(BF16)

    blk = pl.BlockSpec((L, cb), lambda i: (0, i))
    return pl.pallas_call(
        body, grid=(nb,),
        in_specs=[blk, pl.BlockSpec((L, cb), lambda i: (0, i + nb)), pl.BlockSpec((L, cb), lambda i: (0, i + 2 * nb)),
                  blk, blk, pl.BlockSpec((k, cb), lambda i: (0, i))],
        out_specs=[pl.BlockSpec((3, L, cb), lambda i: (0, 0, i)), pl.BlockSpec((8, cb), lambda i: (0, i))],
        out_shape=[jax.ShapeDtypeStruct((3, L, D), BF16), jax.ShapeDtypeStruct((8, D), F32)],
        compiler_params=_params(("parallel",)), name=name)(proj, proj, proj, v, dyv, w)


def _pieces(v, n):
    out, rest = [], v
    for _ in range(n):
        out.append(rest.astype(BF16))
        rest = rest - out[-1].astype(F32)
    return out


def _cumsum_rows(mask, v):
    m = mask.astype(BF16)
    return _dot(jnp.concatenate([m, m, m], axis=1), jnp.concatenate(_pieces(v, 3), axis=0))


def _ssd_chunk_terms(dtr, prm):
    lane = lax.broadcasted_iota(jnp.int32, (CH, LANES), 1)
    valid = lane < NH
    xdt = dtr + prm[0:1, :]
    dt = jnp.where(valid, jnp.maximum(xdt, 0.0) + jnp.log1p(jnp.exp(-jnp.abs(xdt))), 0.0)
    A = -jnp.exp(prm[1:2, :])
    ri = lax.broadcasted_iota(jnp.int32, (CH, CH), 0)
    ci = lax.broadcasted_iota(jnp.int32, (CH, CH), 1)
    cs = _cumsum_rows(ri >= ci, dt * A)
    last = cs[CH - 1:CH, :]
    spread = (lax.broadcasted_iota(jnp.int32, (2 * LANES, DI), 1) // HP
              == lax.broadcasted_iota(jnp.int32, (2 * LANES, DI), 0) % LANES).astype(BF16)
    gather = ((lax.broadcasted_iota(jnp.int32, (LANES, 2 * DI), 1) % DI) // HP
              == lax.broadcasted_iota(jnp.int32, (LANES, 2 * DI), 0)).astype(BF16)
    return dict(valid=valid, xdt=xdt, dt=dt, A=A, cs=cs, csT=cs.T, last=last, ri=ri, ci=ci, ex=(spread, gather))


def _expand(v, ex):
    if v.shape[0] == 1:
        return _expand(jnp.broadcast_to(v, (8, LANES)), ex)[0:1, :]
    return _dot(jnp.concatenate(_pieces(v, 2), axis=1), ex[0])


def _head_sum(v, ex):
    if v.shape[0] == 1:
        return _head_sum(jnp.broadcast_to(v, (8, DI)), ex)[0:1, :]
    return _dot_nt(jnp.concatenate(_pieces(v, 2), axis=1), ex[1])


def _ssd_fwd(xbc, dtr, prm, *, name):
    L = xbc.shape[0]
    nc = L // CH

    def body(xbc_ref, dtr_ref, prm_ref, y_ref, sp_ref, st_ref):
        @pl.when(pl.program_id(0) == 0)
        def _():
            st_ref[...] = jnp.zeros_like(st_ref)

        prm_v = prm_ref[...]
        t = _ssd_chunk_terms(dtr_ref[...], prm_v)
        cs, csT, ex, causal = t["cs"], t["csT"], t["ex"], t["ri"] >= t["ci"]
        xs = xbc_ref[:, 0:DI].astype(F32)
        X = xs * _expand(t["dt"], ex)
        Xb = X.astype(BF16)
        Xd = (X * _expand(jnp.exp(t["last"] - cs), ex)).astype(BF16)
        Ex = _expand(jnp.exp(cs), ex)
        cdx = _expand(jnp.exp(t["last"]), ex)
        dskx = _expand(prm_v[2:3, :], ex)
        lane = lax.broadcasted_iota(jnp.int32, (CH, LANES), 1)
        sp_ref[0] = st_ref[...]
        for g in range(NG):
            Bg = xbc_ref[:, DI + g * NS:DI + (g + 1) * NS].astype(BF16)
            Cg = xbc_ref[:, DI + GW + g * NS:DI + GW + (g + 1) * NS].astype(BF16)
            G = _dot_nt(Cg, Bg)
            Sg = st_ref[:, g * GW:(g + 1) * GW]
            yoff = _dot(Cg, Sg.astype(BF16)) * Ex[:, g * GW:(g + 1) * GW]
            for j in range(GW // LANES):
                lo = g * GW + j * LANES
                Xp = Xb[:, lo:lo + LANES]
                yd = []
                for h in (lo // HP, lo // HP + 1):
                    seg = cs[:, h:h + 1] - csT[h:h + 1, :]
                    yd.append(_dot((G * jnp.where(causal, jnp.exp(seg), 0.0)).astype(BF16), Xp))
                y_ref[:, lo:lo + LANES] = (jnp.where(lane < HP, yd[0], yd[1]) + yoff[:, j * LANES:(j + 1) * LANES]
                                           + dskx[:, lo:lo + LANES] * xs[:, lo:lo + LANES]).astype(BF16)
            st_ref[:, g * GW:(g + 1) * GW] = Sg * cdx[:, g * GW:(g + 1) * GW] + _dot_tn(Bg, Xd[:, g * GW:(g + 1) * GW])

    return pl.pallas_call(
        body, grid=(nc,),
        in_specs=[pl.BlockSpec((CH, CONVD), lambda c: (c, 0)), pl.BlockSpec((CH, LANES), lambda c: (c, 0)),
                  pl.BlockSpec((8, LANES), lambda c: (0, 0))],
        out_specs=[pl.BlockSpec((CH, DI), lambda c: (c, 0)), pl.BlockSpec((1, NS, DI), lambda c: (c, 0, 0))],
        out_shape=[jax.ShapeDtypeStruct((L, DI), BF16), jax.ShapeDtypeStruct((nc, NS, DI), F32)],
        scratch_shapes=[pltpu.VMEM((NS, DI), F32)],
        compiler_params=_params(("arbitrary",)), name=name)(xbc, dtr, prm)


def _ssd_bwd(xbc, dtr, prm, dy, sprev, *, name):
    L = xbc.shape[0]
    nc = L // CH

    def body(xbc_ref, dtr_ref, prm_ref, dy_ref, sp_ref, dxbc_ref, ddtr_ref, s_ref, dst_ref, dx_scr, de_scr, dd_scr):
        step = pl.program_id(0)

        @pl.when(step == 0)
        def _():
            dst_ref[...] = jnp.zeros_like(dst_ref)
            s_ref[...] = jnp.zeros_like(s_ref)

        prm_v = prm_ref[...]
        t = _ssd_chunk_terms(dtr_ref[...], prm_v)
        cs, csT, ex, ri, ci = t["cs"], t["csT"], t["ex"], t["ri"], t["ci"]
        E = jnp.exp(cs)
        dec = jnp.exp(t["last"] - cs)
        cd = jnp.exp(t["last"])
        xs = xbc_ref[:, 0:DI].astype(F32)
        dtx = _expand(t["dt"], ex)
        X = xs * dtx
        Xb = X.astype(BF16)
        decx = _expand(dec, ex)
        Xd = (X * decx).astype(BF16)
        Ex = _expand(E, ex)
        cdx = _expand(cd, ex)
        dskx = _expand(prm_v[2:3, :], ex)
        lane = lax.broadcasted_iota(jnp.int32, (CH, LANES), 1)
        dcs = jnp.zeros((CH, LANES), F32)
        dcd_x = []
        for g in range(NG):
            gs = slice(g * GW, (g + 1) * GW)
            Bg = xbc_ref[:, DI + g * NS:DI + (g + 1) * NS].astype(BF16)
            Cg = xbc_ref[:, DI + GW + g * NS:DI + GW + (g + 1) * NS].astype(BF16)
            G = _dot_nt(Cg, Bg)
            GT = _dot_nt(Bg, Cg)
            Sg = sp_ref[0, :, gs]
            Sgb = Sg.astype(BF16)
            dyg = dy_ref[:, gs]
            de_scr[:, gs] = dyg * _dot(Cg, Sgb)
            dYo = (Ex[:, gs] * dyg).astype(BF16)
            dC = _dot_nt(dYo, Sgb)
            dS_in = _dot_tn(Cg, dYo)
            dStg = dst_ref[:, gs]
            dStb = dStg.astype(BF16)
            dXd = _dot(Bg, dStb)
            dB = _dot_nt(Xd[:, gs], dStb)
            dd_scr[:, gs] = dXd * X[:, gs]
            dXst = dXd * decx[:, gs]
            dG = jnp.zeros((CH, CH), F32)
            dGT = jnp.zeros((CH, CH), F32)
            for j in range(GW // LANES):
                lo = g * GW + j * LANES
                Xp = Xb[:, lo:lo + LANES]
                dyp = dy_ref[:, lo:lo + LANES]
                dXp = dXst[:, j * LANES:(j + 1) * LANES]
                for k, h in enumerate((lo // HP, lo // HP + 1)):
                    dyh = jnp.where((lane < HP) if k == 0 else (lane >= HP), dyp, 0.0).astype(BF16)
                    seg = cs[:, h:h + 1] - csT[h:h + 1, :]
                    Lm = jnp.where(ri >= ci, jnp.exp(seg), 0.0)
                    LmT = jnp.where(ci >= ri, jnp.exp(-seg), 0.0)
                    dM = _dot_nt(dyh, Xp)
                    dMT = _dot_nt(Xp, dyh)
                    MT = GT * LmT
                    rs = jnp.sum(dM * (G * Lm), axis=1, keepdims=True) - jnp.sum(dMT * MT, axis=1, keepdims=True)
                    dcs = dcs + jnp.where(lane == h, rs, 0.0)
                    dG = dG + dM * Lm
                    dGT = dGT + dMT * LmT
                    dXp = dXp + _dot(MT.astype(BF16), dyh)
                dx_scr[:, lo:lo + LANES] = dXp
            dxbc_ref[:, DI + g * NS:DI + (g + 1) * NS] = (dB + _dot(dGT.astype(BF16), Cg)).astype(BF16)
            dxbc_ref[:, DI + GW + g * NS:DI + GW + (g + 1) * NS] = (dC + _dot(dG.astype(BF16), Bg)).astype(BF16)
            dcd_x.append(jnp.sum(dStg * Sg, axis=0, keepdims=True))
            dst_ref[:, gs] = dStg * cdx[:, gs] + dS_in
        dX = dx_scr[...]
        dy = dy_ref[...]
        ddec = _head_sum(dd_scr[...], ex)
        dcd = _head_sum(jnp.concatenate(dcd_x, axis=1), ex)
        dcs = dcs + _head_sum(de_scr[...], ex) * E - ddec * dec
        row = lax.broadcasted_iota(jnp.int32, (CH, LANES), 0)
        dcs = dcs + jnp.where(row == CH - 1, jnp.sum(ddec * dec, axis=0, keepdims=True) + dcd * cd, 0.0)
        da = _cumsum_rows(ci >= ri, dcs)
        ddt = da * t["A"] + _head_sum(dX * xs, ex)
        ddtr = jnp.where(t["valid"], ddt * _sigmoid(t["xdt"]), 0.0)
        ddtr_ref[...] = ddtr
        dxbc_ref[:, 0:DI] = (dX * dtx + dskx * dy).astype(BF16)
        s_ref[0:1, :] += jnp.sum(da * t["dt"], axis=0, keepdims=True)
        s_ref[1:2, :] += _head_sum(jnp.sum(dy * xs, axis=0, keepdims=True), ex)
        s_ref[2:3, :] += jnp.sum(ddtr, axis=0, keepdims=True)

        @pl.when(step == nc - 1)
        def _():
            s_ref[0:1, :] = s_ref[0:1, :] * t["A"]

    rev = lambda c: (nc - 1 - c, 0)
    return pl.pallas_call(
        body, grid=(nc,),
        in_specs=[pl.BlockSpec((CH, CONVD), rev), pl.BlockSpec((CH, LANES), rev), pl.BlockSpec((8, LANES), lambda c: (0, 0)),
                  pl.BlockSpec((CH, DI), rev), pl.BlockSpec((1, NS, DI), lambda c: (nc - 1 - c, 0, 0))],
        out_specs=[pl.BlockSpec((CH, CONVD), rev), pl.BlockSpec((CH, LANES), rev), pl.BlockSpec((8, LANES), lambda c: (0, 0))],
        out_shape=[jax.ShapeDtypeStruct((L, CONVD), BF16), jax.ShapeDtypeStruct((L, LANES), F32),
                   jax.ShapeDtypeStruct((8, LANES), F32)],
        scratch_shapes=[pltpu.VMEM((NS, DI), F32), pltpu.VMEM((CH, DI), F32), pltpu.VMEM((CH, DI), F32),
                        pltpu.VMEM((CH, DI), F32)],
        compiler_params=_params(("arbitrary",)), name=name)(xbc, dtr, prm, dy, sprev)


def _gnorm_fwd(y, zx, nw, *, name):
    L = y.shape[0]
    tm = min(L, 256)

    def body(y_ref, z_ref, nw_ref, o_ref):
        z = z_ref[...].astype(F32)
        yg = y_ref[...].astype(F32) * (z * _sigmoid(z))
        for g in range(NG):
            v = yg[:, g * GW:(g + 1) * GW]
            r = lax.rsqrt(jnp.mean(v * v, axis=-1, keepdims=True) + EPS)
            o_ref[:, g * GW:(g + 1) * GW] = (v * r * nw_ref[:, g * GW:(g + 1) * GW]).astype(BF16)

    row = pl.BlockSpec((tm, DI), lambda i: (i, 0))
    return pl.pallas_call(body, grid=(L // tm,), in_specs=[row, row, pl.BlockSpec((1, DI), lambda i: (0, 0))],
                          out_specs=row, out_shape=jax.ShapeDtypeStruct((L, DI), BF16),
                          compiler_params=_params(("parallel",)), name=name)(y, zx, nw)


def _gnorm_bwd(y, zx, nw, dyn, *, name):
    L = y.shape[0]
    tm = min(L, 256)

    def body(y_ref, z_ref, nw_ref, dyn_ref, dy_ref, dz_ref, s_ref):
        @pl.when(pl.program_id(0) == 0)
        def _():
            s_ref[...] = jnp.zeros_like(s_ref)

        z, yv = z_ref[...].astype(F32), y_ref[...].astype(F32)
        sz = _sigmoid(z)
        gate = z * sz
        dgate_dz = sz * (1.0 + z * (1.0 - sz))
        for g in range(NG):
            gs = slice(g * GW, (g + 1) * GW)
            v = yv[:, gs] * gate[:, gs]
            r = lax.rsqrt(jnp.mean(v * v, axis=-1, keepdims=True) + EPS)
            vhat = v * r
            dn = dyn_ref[:, gs].astype(F32)
            s_ref[0:1, gs] += jnp.sum(dn * vhat, axis=0, keepdims=True)
            dvhat = dn * nw_ref[:, gs]
            dv = r * (dvhat - vhat * jnp.mean(dvhat * vhat, axis=-1, keepdims=True))
            dy_ref[:, gs] = dv * gate[:, gs]
            dz_ref[:, gs] = (dv * yv[:, gs] * dgate_dz[:, gs]).astype(BF16)

    row = pl.BlockSpec((tm, DI), lambda i: (i, 0))
    return pl.pallas_call(body, grid=(L // tm,), in_specs=[row, row, pl.BlockSpec((1, DI), lambda i: (0, 0)), row],
                          out_specs=[row, row, pl.BlockSpec((8, DI), lambda i: (0, 0))],
                          out_shape=[jax.ShapeDtypeStruct((L, DI), F32), jax.ShapeDtypeStruct((L, ZX), BF16),
                                     jax.ShapeDtypeStruct((8, DI), F32)],
                          compiler_params=_params(("arbitrary",)), name=name)(y, zx, nw, dyn)


def _adamw(w, g, m, v, *, name, g_row=0, w_row=0, rows=None, into=None, emit_g=False):
    lead = w.ndim == 3
    R, C = w.shape[-2:]
    rows = R if rows is None else rows
    tr = max([t for t in range(8, rows + 1, 8) if rows % t == 0 and t * C <= 256 * 1024], default=rows)
    assert g_row % tr == 0 and w_row % tr == 0, (name, g_row, w_row, tr)
    n_out = 4 if emit_g else 3

    def body(w_ref, g_ref, m_ref, v_ref, *rest):
        outs = rest[-n_out:]
        gv = g_ref[...]
        mn = ADAM_B1 * m_ref[...] + (1.0 - ADAM_B1) * gv
        vn = ADAM_B2 * v_ref[...] + (1.0 - ADAM_B2) * (gv * gv)
        m_hat = mn / (1.0 - ADAM_B1 ** ADAM_STEP)
        v_hat = vn / (1.0 - ADAM_B2 ** ADAM_STEP)
        d_ref, mo_ref, vo_ref = outs[-3:]
        d_ref[...] = -ADAM_LR * (m_hat / (jnp.sqrt(v_hat) + ADAM_EPS) + ADAM_WD * w_ref[...])
        mo_ref[...] = mn
        vo_ref[...] = vn
        if emit_g:
            outs[0][...] = gv

    blk = (pl.BlockSpec((None, tr, C), lambda i: (0, i + w_row // tr, 0)) if lead
           else pl.BlockSpec((tr, C), lambda i: (i + w_row // tr, 0)))
    args, in_specs, alias = [w, g, m, v], [blk, pl.BlockSpec((tr, C), lambda i: (i + g_row // tr, 0)), blk, blk], {}
    if into is not None:
        args, in_specs, alias = args + list(into), in_specs + [ANY] * n_out, {4 + k: k for k in range(n_out)}
    return pl.pallas_call(body, grid=(rows // tr,), in_specs=in_specs, out_specs=[blk] * n_out,
                          out_shape=[jax.ShapeDtypeStruct(w.shape, F32)] * n_out, input_output_aliases=alias,
                          compiler_params=_params(("parallel",)), name=name)(*args)


def _residual(acc, xv, gv):
    return xv + gv * acc, acc


def _like(buf):
    return jax.ShapeDtypeStruct(buf.shape, buf.dtype)


def _mlp_fwd(x, mod, nw, wb, up_row, down_row, tag, midway=None):
    sh, sc, g = mod
    h = _modnorm_fwd(x, nw, sc, sh, name=tag + "_norm")
    a = _matmul(h, wb, n=DFF, tm=TM_ALL, b_spec=pl.BlockSpec((None, D, 512), lambda mi, j: (j // 2, up_row // D, j % 2)),
                epi=lambda acc: (jnp.maximum(acc, 0.0),), out_dtypes=(BF16,), name=tag + "_up")
    if midway is not None:
        midway(a)
    xn, y = _matmul(a, wb, n=D, tm=TM_HALF, contract=_nn_split_sq,
                    b_spec=pl.BlockSpec((N_CHIPS, D, 512), lambda mi, j: (0, down_row // D, j)),
                    extras=(x, g), epi=_residual, out_dtypes=(F32, BF16), name=tag + "_down")
    return xn, (x, h, a, y)


def _mlp_bwd(dxo, dy, gsum, saved, mod, nw, wb, gb, up_row, down_row, below, tag):
    x, h, a, y = saved
    sh, sc, g = mod
    du = _matmul(dy, wb, n=DFF, tm=TM_ALL, contract=_nt,
                 b_spec=pl.BlockSpec((None, 512, D), lambda mi, j: (j // 2, down_row // 512 + j % 2, 0)),
                 extras=(a,), epi=lambda acc, av: (acc * (2.0 * av.astype(F32)),), out_dtypes=(BF16,), name=tag + "_dact")
    gb = _matmul_tn(a, dy, m=DFF, n=D, tm=D, tn=D, a_square=True, into=gb, out_struct=_like(wb),
                    out_spec=pl.BlockSpec((None, D, D), lambda mi, j: (mi, down_row // D, 0)), name=tag + "_ddown")
    dh = _matmul(du, wb, n=D, tm=TM_HALF, contract=_nt_split,
                 b_spec=pl.BlockSpec((N_CHIPS, 512, D), lambda mi, j: (0, up_row // 512 + j, 0)), out_dtypes=(BF16,),
                 name=tag + "_dh")
    gb = _matmul_tn(h, du, m=D, n=DFF, tm=D, into=gb, out_struct=_like(wb),
                    out_spec=pl.BlockSpec((None, D, 512), lambda mi, j: (j // 2, up_row // D, j % 2)), name=tag + "_dup")
    dx, sums, *nxt = _modnorm_bwd(x, dh, dxo, nw, sc, gsum, below, name=tag + "_dnorm")
    return dx, gb, sums, *nxt


def _ssd_fwd_scan(x, mod, nw, w_in_t, w_dt_t, conv_w, conv_b, prm, tag):
    sh, sc, g = mod
    h = _modnorm_fwd(x, nw, sc, sh, name=tag + "_norm")
    zx = _matmul(h, w_in_t, n=ZX, tm=TM_ALL, contract=_nt, out_dtypes=(BF16,), name=tag + "_in")
    dtr = _matmul(h, w_dt_t, n=LANES, tm=TM_ALL, contract=_nt, name=tag + "_in_dt")
    xbc, pre = _ssd_conv_fwd(zx, conv_w, conv_b, name=tag + "_conv")
    y, sprev = _ssd_fwd(xbc, dtr, prm, name=tag + "_scan")
    return h, zx, dtr, xbc, y, sprev, pre


def _ssd_fwd_out(x, mod, scan, gn_w, get_w_out, tag):
    sh, sc, g = mod
    h, zx, dtr, xbc, y, sprev, pre = scan
    yn = _gnorm_fwd(y, zx, gn_w, name=tag + "_gnorm")
    w_out = get_w_out(yn)
    xn, yo = _matmul(yn, w_out, n=D, tm=TM_HALF, contract=_nn_split,
                     b_spec=pl.BlockSpec((N_CHIPS, 512, 512), lambda mi, j: (0, 0, j)),
                     extras=(x, g), epi=_residual, out_dtypes=(F32, BF16), name=tag + "_out")
    return xn, (x, h, zx, dtr, xbc, y, sprev, yn, yo, pre)


def _ssd_bwd_out(dyo, saved, w_out, tag):
    x, h, zx, dtr, xbc, y, sprev, yn, yo, pre = saved
    dyn = _matmul(dyo, w_out, n=DI, tm=TM_ALL, contract=_nt, b_spec=pl.BlockSpec((None, 512, D), lambda mi, j: (j, 0, 0)),
                  out_dtypes=(BF16,), name=tag + "_dyn")
    g_out = _matmul_tn(yn, dyo, m=DI, n=D, tn=D, out_struct=_like(w_out),
                       out_spec=pl.BlockSpec((None, 512, D), lambda mi, j: (mi, 0, 0)), name=tag + "_dout")
    return dyn, g_out


def _ssd_bwd_rest(dxo, dy, dzx, gsum, saved, mod, nw, w_in_t, w_dt_t, conv_w, prm, tag):
    x, h, zx, dtr, xbc, y, sprev, yn, yo, pre = saved
    sh, sc, g = mod
    dxbc, ddtr, ssum = _ssd_bwd(xbc, dtr, prm, dy, sprev, name=tag + "_dscan")
    dzx, csum = _ssd_conv_bwd(zx, pre, dxbc, conv_w, dzx, name=tag + "_dconv")
    dh_dt = _matmul(ddtr, w_dt_t, n=D, tm=TM_ALL, name=tag + "_dh_dt")
    dh = _matmul(dzx, w_in_t, n=D, tm=TM_HALF, b_spec=pl.BlockSpec((ZX, 512), lambda mi, j: (0, j)), extras=(dh_dt,),
                 epi=lambda acc, e: (acc + e,), out_dtypes=(BF16,), name=tag + "_dh")
    d_w_zx = _matmul_tn(h, dzx, m=D, n=ZX, tm=D, name=tag + "_din")
    d_w_dt = _matmul_tn(h, ddtr, m=D, n=LANES, tm=D, name=tag + "_din_dt")
    dx, sums = _modnorm_bwd(x, dh, dxo, nw, sc, gsum, None, name=tag + "_dnorm")
    return dx, d_w_zx, d_w_dt, sums, csum, ssum


def _sc_layer_fwd(x, mod, nw, w_sc_in, conv_w, wb, out_row, tag, midway=None):
    sh, sc, g = mod
    h = _modnorm_fwd(x, nw, sc, sh, name=tag + "_norm")
    proj = _matmul(h, w_sc_in, n=3 * D, tm=TM_ALL, tn=256, out_dtypes=(BF16,),
                   b_spec=pl.BlockSpec((None, D, 256), lambda mi, j: (j // 3, 0, j % 3)),
                   name=tag + "_in")
    if midway is not None:
        midway(proj)
    yv, v = _sc_fwd(proj, conv_w, name=tag + "_conv")
    xn, yo = _matmul(yv, wb, n=D, tm=TM_HALF, contract=_nn_split,
                     b_spec=pl.BlockSpec((N_CHIPS, 256, 512), lambda mi, j: (0, out_row // 256, j)),
                     extras=(x, g), epi=_residual, out_dtypes=(F32, BF16), name=tag + "_out")
    return xn, (x, h, proj, yv, yo, v)


def _sc_layer_bwd(dxo, dyo, gsum, saved, mod, nw, w_sc_in, conv_w, wb, gb, out_row, below, tag):
    x, h, proj, yv, yo, v = saved
    sh, sc, g = mod
    L = x.shape[0]
    dyv = _matmul(dyo, wb, n=D, tm=TM_ALL, tn=256, contract=_nt,
                  b_spec=pl.BlockSpec((None, 256, D), lambda mi, j: (j, out_row // 256, 0)), out_dtypes=(BF16,),
                  name=tag + "_dyv")
    gb = _matmul_tn(yv, dyo, m=D, n=D, tm=256, tn=D, into=gb, out_struct=_like(wb),
                    out_spec=pl.BlockSpec((None, 256, D), lambda mi, j: (mi, out_row // 256, 0)), name=tag + "_dout")
    dproj, csum = _sc_bwd(proj, v, dyv, conv_w, name=tag + "_dconv")
    tm = min(L, TM_HALF)
    dh = _matmul(dproj, w_sc_in, n=D, tm=tm, contract=_nt_sc_in, a_spec=pl.BlockSpec((3, tm, D), lambda mi, j: (0, mi, 0)),
                 b_spec=pl.BlockSpec((N_CHIPS, 512, SC_IN_SHARD), lambda mi, j: (0, j, 0)), out_dtypes=(BF16,),
                 name=tag + "_dh")
    g_sc_in = _matmul_tn(h, dproj, m=D, n=3 * D, tm=D, tn=256, b_spec=pl.BlockSpec((None, L, 256), lambda mi, j: (j // 4, 0, j % 4)),
                         out_spec=pl.BlockSpec((None, D, 256), lambda mi, j: (j // 3, 0, j % 3)),
                         out_struct=jax.ShapeDtypeStruct((N_CHIPS, D, SC_IN_SHARD), BF16), name=tag + "_din")
    dx, sums, *nxt = _modnorm_bwd(x, dh, dxo, nw, sc, gsum, below, name=tag + "_dnorm")
    return dx, gb, g_sc_in, sums, csum, *nxt


SUB_ROW = (0, 8, 16, 24)
SSD_CONV_ROW, GNORM_ROW, FINAL_ROW, SC_CONV_ROW, HEAD_ROW, SMALL_ROWS = 32, 48, 56, 64, 72, 80


def _all_gather_rows(blk, *, name):
    m_per, n = blk.shape

    def body(x_ref, out_ref, send_sems, recv_sems, local_sem):
        x, y, c = lax.axis_index("x"), lax.axis_index("y"), lax.axis_index("c")
        me, sibling = (x, y, c), (x, y, 1 - c)
        chips = [(1 - x, y), (x, 1 - y), (1 - x, 1 - y)]

        def rows(px, py, pc):
            return out_ref.at[pl.ds((4 * px + 2 * py + pc) * m_per, m_per), :]

        def copy(k, block, to, src=None):
            return pltpu.make_async_remote_copy(src_ref=rows(*block) if src is None else src, dst_ref=rows(*block),
                                                send_sem=send_sems.at[k], recv_sem=recv_sems.at[k], device_id=to,
                                                device_id_type=MESH)

        mine = pltpu.make_async_copy(x_ref, rows(*me), local_sem)
        mine.start()
        first = [copy(0, me, sibling, src=x_ref)] + [copy(1 + j, me, (*chip, c), src=x_ref) for j, chip in enumerate(chips)]
        for cp in first:
            cp.start()
        passed = [copy(4 + j, (*chip, c), sibling) for j, chip in enumerate(chips)]
        for j, chip in enumerate(chips):
            copy(1 + j, (*chip, c), me).wait_recv()
            passed[j].start()
        copy(0, sibling, me).wait_recv()
        for j, chip in enumerate(chips):
            copy(4 + j, (*chip, 1 - c), me).wait_recv()
        for cp in first + passed:
            cp.wait_send()
        mine.wait()

    return pl.pallas_call(
        body, out_shape=jax.ShapeDtypeStruct((N_DEV * m_per, n), blk.dtype),
        in_specs=[pl.BlockSpec(memory_space=pltpu.VMEM)], out_specs=pl.BlockSpec(memory_space=pltpu.VMEM),
        scratch_shapes=[pltpu.SemaphoreType.DMA((7,)), pltpu.SemaphoreType.DMA((7,)), pltpu.SemaphoreType.DMA],
        name=name)(blk)


def _half(ref, chip, c):
    r, n = ref.shape[1:]
    if r % 32 == 0:
        return ref.at[chip, pl.ds(c * (r // 2), r // 2), :]
    assert n % 256 == 0, ref.shape
    return ref.at[chip, :, pl.ds(c * (n // 2), n // 2)]


def _gather_copy(bufs, sends, recvs, b, k, chip, pc, to):
    piece = _half(bufs[b], 2 * chip[0] + chip[1], pc)
    return pltpu.make_async_remote_copy(src_ref=piece, dst_ref=piece, send_sem=sends.at[4 * b + k], recv_sem=recvs.at[4 * b + k],
                                        device_id=to, device_id_type=MESH)


def _split_call(body, bufs, sems_in, n_sems, *, name, after=(), token=False, lands=()):
    nb, na, nl, starts = len(bufs), len(after), len(lands), not sems_in

    def wrapped(*refs):
        sems = refs[nb + na:nb + na + 2] if starts else refs[nb:nb + 2]
        made = refs[nb + na + 2 + nb:nb + na + 2 + nb + nl] if starts else ()
        body(tuple(refs[:nb]) + tuple(made), sems[0], sems[1])
        if token:
            refs[-1][...] = jnp.zeros_like(refs[-1])

    out_shape = [pltpu.SemaphoreType.DMA((n_sems,)) for _ in range(2 if starts else 0)]
    out_specs = [SEM] * len(out_shape) + [ANY] * (nb + nl)
    alias = {b: len(out_shape) + b for b in range(nb)}
    out_shape += [jax.ShapeDtypeStruct(b.shape, b.dtype) for b in bufs] + list(lands)
    if token:
        out_shape.append(jax.ShapeDtypeStruct((8, LANES), F32))
        out_specs.append(pl.BlockSpec(memory_space=pltpu.VMEM))
    return pl.pallas_call(
        wrapped, out_shape=out_shape, in_specs=[ANY] * nb + [SEM] * len(sems_in) + [ANY] * na, out_specs=out_specs,
        input_output_aliases=alias,
        compiler_params=pltpu.CompilerParams(has_side_effects=pltpu.SideEffectType.DATAFLOW_SIDE_EFFECTING),
        name=name)(*bufs, *sems_in, *after)


def _gather_start(bufs, *, name, after=()):
    nb = len(bufs)

    def body(ins, sends, recvs):
        x, y, c = lax.axis_index("x"), lax.axis_index("y"), lax.axis_index("c")
        chips = [(1 - x, y), (x, 1 - y), (1 - x, 1 - y)]
        for b in range(nb):
            _gather_copy(ins, sends, recvs, b, 0, (x, y), c, (x, y, 1 - c)).start()
            for j, chip in enumerate(chips):
                _gather_copy(ins, sends, recvs, b, 1 + j, (x, y), c, (*chip, c)).start()

    out = _split_call(body, bufs, (), 4 * nb, name=name, after=after, token=True)
    return (out[0], out[1], out[2:2 + nb]), out[-1]


def _gather_wait_first(flight, *, name, after=()):
    sends, recvs, bufs = flight
    nb = len(bufs)

    def body(ins, sends_, recvs_):
        x, y, c = lax.axis_index("x"), lax.axis_index("y"), lax.axis_index("c")
        chips = [(1 - x, y), (x, 1 - y), (1 - x, 1 - y)]
        for b in range(nb):
            _gather_copy(ins, sends_, recvs_, b, 0, (x, y), c, (x, y, 1 - c)).wait_send()
            _gather_copy(ins, sends_, recvs_, b, 0, (x, y), 1 - c, (x, y, c)).wait_recv()
            for j, chip in enumerate(chips):
                _gather_copy(ins, sends_, recvs_, b, 1 + j, (x, y), c, (*chip, c)).wait_send()
                _gather_copy(ins, sends_, recvs_, b, 1 + j, chip, c, (x, y, c)).wait_recv()

    return _split_call(body, bufs, (sends, recvs), 4 * nb, name=name, after=after)


def _gather_forward(bufs, *, name):
    nb = len(bufs)

    def body(ins, sends, recvs):
        x, y, c = lax.axis_index("x"), lax.axis_index("y"), lax.axis_index("c")
        chips = [(1 - x, y), (x, 1 - y), (1 - x, 1 - y)]
        for b in range(nb):
            for j, chip in enumerate(chips):
                _gather_copy(ins, sends, recvs, b, 1 + j, chip, c, (x, y, 1 - c)).start()

    out = _split_call(body, bufs, (), 4 * nb, name=name)
    return out[0], out[1], out[2:2 + nb]


def _gather_wait_forward(flight, *, name, after=()):
    sends, recvs, bufs = flight
    nb = len(bufs)

    def body(ins, sends_, recvs_):
        x, y, c = lax.axis_index("x"), lax.axis_index("y"), lax.axis_index("c")
        chips = [(1 - x, y), (x, 1 - y), (1 - x, 1 - y)]
        for b in range(nb):
            for j, chip in enumerate(chips):
                _gather_copy(ins, sends_, recvs_, b, 1 + j, chip, c, (x, y, 1 - c)).wait_send()
                _gather_copy(ins, sends_, recvs_, b, 1 + j, chip, 1 - c, (x, y, c)).wait_recv()

    return _split_call(body, bufs, (sends, recvs), 4 * nb, name=name, after=after)


def _owner_copies(hs, lands, sends, recvs):
    x, y, c = lax.axis_index("x"), lax.axis_index("y"), lax.axis_index("c")
    chips = [(1 - x, y), (x, 1 - y), (1 - x, 1 - y)]
    return [pltpu.make_async_remote_copy(src_ref=hs[b].at[2 * cx + cy], dst_ref=lands[b].at[j], send_sem=sends.at[3 * b + j],
                                         recv_sem=recvs.at[3 * b + j], device_id=(cx, cy, c), device_id_type=MESH)
            for b in range(len(hs)) for j, (cx, cy) in enumerate(chips)]


def _owners_start(hs, *, name):
    nb = len(hs)
    lands = [jax.ShapeDtypeStruct((3,) + h.shape[1:], h.dtype) for h in hs]

    def body(refs, sends, recvs):
        for cp in _owner_copies(refs[:nb], refs[nb:], sends, recvs):
            cp.start()

    out = _split_call(body, list(hs), (), 3 * nb, name=name, token=True, lands=lands)
    return (out[0], out[1], out[2:2 + 2 * nb]), out[-1]


def _owners_wait(flight, *, name, after=()):
    sends, recvs, bufs = flight
    nb = len(bufs) // 2

    def body(refs, sends_, recvs_):
        for cp in _owner_copies(refs[:nb], refs[nb:], sends_, recvs_):
            cp.wait()

    out = _split_call(body, bufs, (sends, recvs), 3 * nb, name=name, after=after)
    return out[:nb], out[nb:]


def _sibling_copies(gs, lands, sends, recvs):
    x, y, c = lax.axis_index("x"), lax.axis_index("y"), lax.axis_index("c")
    copies = []
    for b in range(len(gs)):
        hr = gs[b].shape[1] // 2
        copies.append(pltpu.make_async_remote_copy(
            src_ref=gs[b].at[:, pl.ds((1 - c) * hr, hr), :], dst_ref=lands[b], send_sem=sends.at[b], recv_sem=recvs.at[b],
            device_id=(x, y, 1 - c), device_id_type=MESH))
    return copies


def _sibling_start(gs, *, name, after=()):
    nb = len(gs)
    lands = [jax.ShapeDtypeStruct((g.shape[0], g.shape[1] // 2, g.shape[2]), g.dtype) for g in gs]

    def body(refs, sends, recvs):
        for cp in _sibling_copies(refs[:nb], refs[nb:], sends, recvs):
            cp.start()

    out = _split_call(body, list(gs), (), nb, name=name, after=after, token=True, lands=lands)
    return (out[0], out[1], out[2:2 + 2 * nb]), out[-1]


def _sibling_wait(flight, *, name, after=()):
    sends, recvs, bufs = flight
    nb = len(bufs) // 2

    def body(refs, sends_, recvs_):
        for cp in _sibling_copies(refs[:nb], refs[nb:], sends_, recvs_):
            cp.wait()

    out = _split_call(body, bufs, (sends, recvs), nb, name=name, after=after)
    return out[:nb], out[nb:]


def _result_copies(ts, sends, recvs):
    x, y, c = lax.axis_index("x"), lax.axis_index("y"), lax.axis_index("c")
    return [pltpu.make_async_remote_copy(src_ref=ts[b].at[c], dst_ref=ts[b].at[c], send_sem=sends.at[b], recv_sem=recvs.at[b],
                                         device_id=(x, y, 1 - c), device_id_type=MESH) for b in range(len(ts))]


def _result_start(ts, *, name):
    def body(refs, sends, recvs):
        for cp in _result_copies(refs, sends, recvs):
            cp.start()

    out = _split_call(body, ts, (), len(ts), name=name, token=True)
    return (out[0], out[1], out[2:2 + len(ts)]), out[-1]


def _result_wait(flight, *, name, after=()):
    sends, recvs, bufs = flight

    def body(refs, sends_, recvs_):
        for cp in _result_copies(refs, sends_, recvs_):
            cp.wait()

    return _split_call(body, bufs, (sends, recvs), len(bufs), name=name, after=after)


def _row_tile(rows, cols):
    best = 16
    for t in range(16, rows + 1, 16):
        if rows % t == 0 and t * cols <= 640 * 1024:
            best = t
    assert rows % best == 0, (rows, cols)
    return best


def _add_sibling_half(g, recv, core, *, name):
    nk, r, n = g.shape
    hr = r // 2
    tr = _row_tile(hr, n)

    def body(c_ref, a_ref, b_ref, o_ref):
        o_ref[...] = (a_ref[...].astype(F32) + b_ref[...].astype(F32)).astype(BF16)

    grid_spec = pltpu.PrefetchScalarGridSpec(
        num_scalar_prefetch=1, grid=(nk, hr // tr),
        in_specs=[pl.BlockSpec((None, tr, n), lambda k, i, c_ref: (k, c_ref[0] * (hr // tr) + i, 0)),
                  pl.BlockSpec((None, tr, n), lambda k, i, c_ref: (k, i, 0))],
        out_specs=pl.BlockSpec((None, tr, n), lambda k, i, c_ref: (k, i, 0)))
    return pl.pallas_call(body, grid_spec=grid_spec, out_shape=jax.ShapeDtypeStruct((nk, hr, n), BF16),
                          compiler_params=_params(("parallel", "parallel")), name=name)(core, g, recv)


def _add_chip_sums(h, recv, chip_core, *, name):
    _, hr, n = h.shape
    tr = _row_tile(hr, n)

    def body(k_ref, a_ref, b_ref, o_ref):
        o_ref[...] = ((a_ref[...].astype(F32) + b_ref[0].astype(F32)) + b_ref[1].astype(F32)) + b_ref[2].astype(F32)

    grid_spec = pltpu.PrefetchScalarGridSpec(
        num_scalar_prefetch=1, grid=(hr // tr,),
        in_specs=[pl.BlockSpec((None, tr, n), lambda i, k_ref: (k_ref[0], i, 0)),
                  pl.BlockSpec((3, tr, n), lambda i, k_ref: (0, i, 0))],
        out_specs=pl.BlockSpec((None, tr, n), lambda i, k_ref: (k_ref[1], i, 0)))
    return pl.pallas_call(body, grid_spec=grid_spec, out_shape=jax.ShapeDtypeStruct((2, hr, n), F32),
                          compiler_params=_params(("parallel",)), name=name)(chip_core, h, recv)


def _sum_devices(g, *, name):
    nd, r, n = g.shape

    def body(g_ref, o_ref):
        acc = g_ref[0]
        for i in range(1, nd):
            acc = acc + g_ref[i]
        o_ref[...] = acc

    return pl.pallas_call(body, out_shape=jax.ShapeDtypeStruct((r, n), F32), name=name)(g)


def _own_slot(parts, chip, *, name, after=()):
    rows, cols = sum(w.shape[1] for w, _ in parts), parts[0][0].shape[2]
    buf, row0 = None, 0
    for p, (w, idx) in enumerate(parts):
        r = w.shape[1]
        tr = 256 if r % 256 == 0 else r
        assert row0 % tr == 0, (name, r, row0)
        prev = () if buf is None else (buf,)

        def body(chip_ref, w_ref, *rest):
            rest[-1][...] = w_ref[...].astype(BF16)

        grid_spec = pltpu.PrefetchScalarGridSpec(
            num_scalar_prefetch=1, grid=(r // tr,),
            in_specs=[pl.BlockSpec((None, tr, cols), lambda i, c_ref, idx=idx: (idx, i, 0))] + [ANY] * (len(prev) + len(after)),
            out_specs=pl.BlockSpec((None, tr, cols), lambda i, c_ref, row0=row0, tr=tr: (c_ref[0], row0 // tr + i, 0)))
        buf = pl.pallas_call(body, grid_spec=grid_spec, out_shape=jax.ShapeDtypeStruct((N_CHIPS, rows, cols), BF16),
                             input_output_aliases={2: 0} if prev else {}, compiler_params=_params(("parallel",)),
                             name=f"{name}{p}")(chip, w, *prev, *after)
        row0 += r
    return buf


def kernel(x, c, ada_w, ada_b, mix_norm_w, mlp_norm_w, mlp_up, mlp_down, ssd_in_w, ssd_conv_w, ssd_conv_b, ssd_dt_bias, ssd_A_log, ssd_D, ssd_norm_w, ssd_out_w, sc_in_w, sc_conv_w, sc_out_w, final_norm_w, loss_target, m_ada_w, m_ada_b, m_mix_norm_w, m_mlp_norm_w, m_mlp_up, m_mlp_down, m_ssd_in_w, m_ssd_conv_w, m_ssd_conv_b, m_ssd_dt_bias, m_ssd_A_log, m_ssd_D, m_ssd_norm_w, m_ssd_out_w, m_sc_in_w, m_sc_conv_w, m_sc_out_w, m_final_norm_w, v_ada_w, v_ada_b, v_mix_norm_w, v_mlp_norm_w, v_mlp_up, v_mlp_down, v_ssd_in_w, v_ssd_conv_w, v_ssd_conv_b, v_ssd_dt_bias, v_ssd_A_log, v_ssd_D, v_ssd_norm_w, v_ssd_out_w, v_sc_in_w, v_sc_conv_w, v_sc_out_w, v_final_norm_w):
    xi, yi, ci = lax.axis_index("x"), lax.axis_index("y"), lax.axis_index("c")
    chip = 2 * xi + yi
    dev = 2 * chip + ci
    n_ada = ada_w.shape[2]

    conv_flat = jnp.concatenate([ssd_conv_w.reshape(-1), sc_conv_w.reshape(-1), jnp.zeros((256,), F32)]).reshape(4, D)
    blk0 = jnp.concatenate([c, conv_flat, jnp.zeros((3, D), F32)], axis=0)
    got0 = _all_gather_rows(blk0, name="gather_cond").reshape(N_DEV, 8, D)
    c_all = got0[:, 0]
    conv_all = got0[0::2, 1:5].reshape(N_CHIPS, 4 * D)
    ssd_conv = jnp.moveaxis(conv_all[:, :4 * 768].reshape(N_CHIPS, 4, 768), 0, 1).reshape(4, CONVD)
    sc_conv = jnp.moveaxis(conv_all[:, 4 * 768:4 * 768 + 3 * 256].reshape(N_CHIPS, 3, 256), 0, 1).reshape(3, D)
    mod_shard = [_matmul(c_all, ada_w, n=n_ada, a_silu=True, b_spec=pl.BlockSpec((None, D, 512), lambda mi, j, i=i: (i, 0, j)),
                         extras=(lax.dynamic_slice(ada_b, (i, chip * n_ada), (1, n_ada)),),
                         epi=lambda acc, b: (acc + b,), name=f"ada_mod{i}") for i in range(2)]
    mod_all = _all_gather_rows(jnp.concatenate(mod_shard, axis=0), name="gather_mod")
    mod_all = mod_all.reshape(N_DEV, 2, N_DEV, n_ada)[0::2]
    mod = jnp.moveaxis(lax.dynamic_index_in_dim(mod_all, dev, axis=2, keepdims=False), 0, 1).reshape(2, 6, D)
    mods = [[mod[i, j:j + 1] for j in range(6)] for i in range(2)]

    up_row, down_row = 0, D
    chip1 = chip.reshape(1).astype(jnp.int32)
    a_bufs = [_own_slot([(jnp.swapaxes(ssd_in_w, 1, 2), 0)], chip1, name="slot_ssd_in")]
    fly_a, tok = _gather_start(a_bufs, name="gather_a_start", after=(mod,))
    b_bufs = [_own_slot([(ssd_out_w, 0)], chip1, name="slot_ssd_out", after=(tok,)),
              _own_slot([(mlp_up, 0), (mlp_down, 0)], chip1, name="slot_mlp0_", after=(tok,))]
    fly_b, tok = _gather_start(b_bufs, name="gather_b_start", after=(tok,))
    c_bufs = [_own_slot([(sc_in_w, 0)], chip1, name="slot_sc_in", after=(tok,)),
              _own_slot([(sc_out_w, 0)], chip1, name="slot_sc_out", after=(tok,))]
    fly_c, tok = _gather_start(c_bufs, name="gather_c_start", after=(tok,))
    d_bufs = [_own_slot([(mlp_up, 1), (mlp_down, 1)], chip1, name="slot_mlp1_", after=(tok,))]
    fly_d, tok = _gather_start(d_bufs, name="gather_d_start", after=(tok,))

    row = lambda v: v.reshape(1, -1)
    xs, tgt = x[0], loss_target[0]
    prm = jnp.pad(jnp.concatenate([ssd_dt_bias, ssd_A_log, ssd_D, jnp.zeros((5, NH), F32)], axis=0), ((0, 0), (0, LANES - NH)))
    mix_nw = [row(mix_norm_w[i]) for i in range(2)]
    mlp_nw = [row(mlp_norm_w[i]) for i in range(2)]
    a_bufs = _gather_wait_first(fly_a, name="gather_a_landed", after=(tok,))
    (w_ssd_in,) = _gather_wait_forward(_gather_forward(a_bufs, name="gather_a_pass"), name="gather_a_done")
    w_in_t = w_ssd_in.reshape(N_CHIPS * SSD_IN_SHARD, D)
    w_dt_t = jnp.pad(w_in_t[ZX:], ((0, LANES - NH), (0, 0)))
    scan = _ssd_fwd_scan(xs, mods[0][0:3], mix_nw[0], w_in_t, w_dt_t, ssd_conv, ssd_conv_b, prm, "ssd")

    def land(flight, tag, after):
        return _gather_forward(_gather_wait_first(flight, name=f"gather_{tag}_landed", after=(after,)), name=f"gather_{tag}_pass")

    passed, got = {"b": land(fly_b, "b", scan[4])}, {}

    def done(tag, after):
        got[tag] = _gather_wait_forward(passed[tag], name=f"gather_{tag}_done", after=(after,))
        return got[tag]

    x1, s_ssd = _ssd_fwd_out(xs, mods[0][0:3], scan, ssd_norm_w, lambda yn: done("b", yn)[0], "ssd")
    w_ssd_out, w_b = got["b"]
    x2, s_mlp0 = _mlp_fwd(x1, mods[0][3:6], mlp_nw[0], w_b, up_row, down_row, "mlp0",
                          midway=lambda a: passed.update(c=land(fly_c, "c", a)))
    w_sc_in, w_sc_out = done("c", x2)
    x3, s_sc = _sc_layer_fwd(x2, mods[1][0:3], mix_nw[1], w_sc_in, sc_conv, w_sc_out, 0, "sc",
                             midway=lambda proj: passed.update(d=land(fly_d, "d", proj)))
    (w_mlp1,) = done("d", x3)
    x4, s_mlp1 = _mlp_fwd(x3, mods[1][3:6], mlp_nw[1], w_mlp1, up_row, down_row, "mlp1")

    core = ci.reshape(1).astype(jnp.int32)
    chip_core = jnp.stack([chip, ci]).astype(jnp.int32)

    def reduce_swap(gbufs, tag, after=()):
        return _sibling_start(gbufs, name=tag + "_sibling_start", after=after)

    def reduce_send(flight, tag, after):
        gs, sib = _sibling_wait(flight, name=tag + "_sibling_landed", after=after)
        hs = [_add_sibling_half(g, s, core, name=f"{tag}_add_sibling{b}") for b, (g, s) in enumerate(zip(gs, sib))]
        return _owners_start(hs, name=tag + "_owners_start")

    def reduce_sum(flight, tag, after):
        hs, lands = _owners_wait(flight, name=tag + "_owners_landed", after=after)
        ts = [_add_chip_sums(h, o, chip_core, name=f"{tag}_add_chips{b}") for b, (h, o) in enumerate(zip(hs, lands))]
        return _result_start(ts, name=tag + "_result_start")

    def reduce_done(flight, tag, after=()):
        return [t.reshape(-1, t.shape[2]) for t in _result_wait(flight, name=tag + "_result_landed", after=after)]

    dx4, fsum, dy, gs = _final_loss(x4, row(final_norm_w), tgt, (mods[1][5], s_mlp1[3]), name="final_loss")
    dx3, g_mlp1, sum_mlp1, dy, gs = _mlp_bwd(dx4, dy, gs, s_mlp1, mods[1][3:6], mlp_nw[1], w_mlp1, None, up_row, down_row,
                                             (mods[1][2], s_sc[4]), "mlp1")
    dx2, g_sc_out, g_sc_in, sum_sc, sc_csum, dy, gs = _sc_layer_bwd(dx3, dy, gs, s_sc, mods[1][0:3], mix_nw[1], w_sc_in,
                                                                    sc_conv, w_sc_out, None, 0, (mods[0][5], s_mlp0[3]), "sc")
    dx1, g_b, sum_mlp0, dy, gsum_ssd = _mlp_bwd(dx2, dy, gs, s_mlp0, mods[0][3:6], mlp_nw[0], w_b, None, up_row, down_row,
                                                (mods[0][2], s_ssd[8]), "mlp0")
    dyn, g_ssd_out = _ssd_bwd_out(dy, s_ssd, w_ssd_out, "ssd")
    fly_1, tok = reduce_swap([g_mlp1, g_sc_out, g_sc_in, g_b, g_ssd_out], "rs1")
    dy, dzx, gnsum = _gnorm_bwd(s_ssd[5], s_ssd[2], ssd_norm_w + tok[0:1, 0:1], dyn, name="ssd_dgnorm")
    fly_1, tok = reduce_send(fly_1, "rs1", (dy,))
    grad_x, d_w_zx, d_w_dt, sum_ssd, csum, ssum = _ssd_bwd_rest(
        dx1, dy, dzx, gsum_ssd, s_ssd, mods[0][0:3], mix_nw[0], w_in_t, w_dt_t, ssd_conv, prm + tok[0:1, 0:1], "ssd")
    fly_1, tok = reduce_sum(fly_1, "rs1", (grad_x,))

    def ssd_in_owner(k):
        lo, hi = k * SSD_IN_SHARD, (k + 1) * SSD_IN_SHARD
        if hi <= ZX:
            return d_w_zx[:, lo:hi]
        return jnp.concatenate([d_w_zx[:, lo:], d_w_dt[:, :hi - ZX]], axis=1)

    small = jnp.concatenate([sum_ssd + tok[0:1, 0:1], sum_mlp0, sum_sc, sum_mlp1, csum.reshape(24, D)[0:16], gnsum.reshape(16, D)[0:8],
                             fsum, sc_csum, jnp.pad(ssum, ((0, 0), (0, D - LANES)))], axis=0)
    small_all = _all_gather_rows(small, name="gather_small").reshape(N_DEV, SMALL_ROWS, D)
    fly_2, tok = reduce_swap([jnp.stack([ssd_in_owner(k) for k in range(N_CHIPS)]).astype(BF16)], "rs2", (small_all,))
    fly_2, tok = reduce_send(fly_2, "rs2", (tok,))
    t_mlp1, t_sc_out, t_sc_in, t_b, t_ssd_out = reduce_done(fly_1, "rs1", (tok,))
    small_all = small_all + tok[0:1, 0:1]
    tot = _sum_devices(small_all, name="sum_small")
    loss = tot[FINAL_ROW + 1, 0]
    mod_rows = [r + o for r in SUB_ROW for o in (3, 2, 0)]
    g_ada_b = jnp.stack([tot[r] for r in mod_rows]).reshape(2, 6 * D)
    g_mix_norm = jnp.stack([tot[SUB_ROW[0] + 1], tot[SUB_ROW[2] + 1]])
    g_mlp_norm = jnp.stack([tot[SUB_ROW[1] + 1], tot[SUB_ROW[3] + 1]])
    conv_sums = tot[SSD_CONV_ROW:SSD_CONV_ROW + 15].reshape(5, CONVD)
    g_ssd_conv_w = lax.dynamic_slice(conv_sums, (0, chip * 768), (4, 768))[None]
    g_ssd_conv_b = conv_sums[4:5]
    g_ssd_norm = tot[GNORM_ROW:GNORM_ROW + 2].reshape(1, DI)
    g_final = tot[FINAL_ROW]
    g_sc_conv_w = lax.dynamic_slice(tot[SC_CONV_ROW:SC_CONV_ROW + 3], (0, chip * 256), (3, 256))[None]
    g_a_log, g_d, g_dt_bias = (tot[HEAD_ROW + r:HEAD_ROW + r + 1, 0:NH] for r in range(3))
    c_pad = jnp.concatenate([c_all, jnp.zeros((8, D), F32)], axis=0)
    dmod_all = jnp.stack([small_all[:, r] for r in mod_rows], axis=1).reshape(N_DEV, 2, 6 * D)
    g_ada_w = []
    for i in range(2):
        dm = lax.dynamic_slice(dmod_all[:, i], (0, chip * n_ada), (N_DEV, n_ada))
        g_ada_w.append(_matmul_tn(c_pad, jnp.concatenate([dm, jnp.zeros_like(dm)], axis=0), m=D, n=n_ada, a_silu=True,
                                  name=f"ada_dw{i}"))

    big = dict(ada_w=[(g, 0) for g in g_ada_w], mlp_up=[(t_b, up_row), (t_mlp1, up_row)],
               mlp_down=[(t_b, down_row), (t_mlp1, down_row)], ssd_out_w=[(t_ssd_out, 0)], sc_out_w=[(t_sc_out, 0)],
               sc_in_w=[(t_sc_in, 0)], ssd_in_w=None)
    grads = dict(ada_b=g_ada_b, mix_norm_w=g_mix_norm, mlp_norm_w=g_mlp_norm, ssd_conv_w=g_ssd_conv_w,
                 ssd_conv_b=g_ssd_conv_b, ssd_dt_bias=g_dt_bias, ssd_A_log=g_a_log, ssd_D=g_d, ssd_norm_w=g_ssd_norm,
                 sc_conv_w=g_sc_conv_w, final_norm_w=g_final)
    weights = dict(ada_w=(ada_w, m_ada_w, v_ada_w), ada_b=(ada_b, m_ada_b, v_ada_b),
                   mix_norm_w=(mix_norm_w, m_mix_norm_w, v_mix_norm_w), mlp_norm_w=(mlp_norm_w, m_mlp_norm_w, v_mlp_norm_w),
                   mlp_up=(mlp_up, m_mlp_up, v_mlp_up), mlp_down=(mlp_down, m_mlp_down, v_mlp_down),
                   ssd_in_w=(ssd_in_w, m_ssd_in_w, v_ssd_in_w), ssd_conv_w=(ssd_conv_w, m_ssd_conv_w, v_ssd_conv_w),
                   ssd_conv_b=(ssd_conv_b, m_ssd_conv_b, v_ssd_conv_b), ssd_dt_bias=(ssd_dt_bias, m_ssd_dt_bias, v_ssd_dt_bias),
                   ssd_A_log=(ssd_A_log, m_ssd_A_log, v_ssd_A_log), ssd_D=(ssd_D, m_ssd_D, v_ssd_D),
                   ssd_norm_w=(ssd_norm_w, m_ssd_norm_w, v_ssd_norm_w), ssd_out_w=(ssd_out_w, m_ssd_out_w, v_ssd_out_w),
                   sc_in_w=(sc_in_w, m_sc_in_w, v_sc_in_w), sc_conv_w=(sc_conv_w, m_sc_conv_w, v_sc_conv_w),
                   sc_out_w=(sc_out_w, m_sc_out_w, v_sc_out_w), final_norm_w=(final_norm_w, m_final_norm_w, v_final_norm_w))
    def step(nm, parts):
        w, m, v = (t if t.shape[0] == 1 else t.reshape(-1, t.shape[-1]) for t in weights[nm])
        rows, outs = w.shape[-2] // len(parts), None
        for i, (gbuf, g_row) in enumerate(parts):
            outs = _adamw(w, gbuf, m, v, g_row=g_row, w_row=i * rows, rows=rows, into=outs, emit_g=True, name=f"adamw_{nm}{i}")
        return outs

    res = {}
    for nm, (w, m, v) in weights.items():
        two_d = (-1, w.shape[-1]) if w.ndim > 1 else (1, -1)
        if nm not in big:
            res[nm] = (grads[nm], *_adamw(w.reshape(two_d), grads[nm].reshape(two_d), m.reshape(two_d), v.reshape(two_d),
                                          name="adamw_" + nm))
        elif big[nm] is not None:
            res[nm] = step(nm, big[nm])
    fly_2, tok = reduce_sum(fly_2, "rs2", tuple(r[1] for r in res.values()))
    (t_ssd_in,) = reduce_done(fly_2, "rs2", (tok,))
    w_t, m_t, v_t = (jnp.swapaxes(t[0], 0, 1) for t in weights["ssd_in_w"])
    res["ssd_in_w"] = [jnp.swapaxes(o, 0, 1) for o in _adamw(w_t, t_ssd_in.T, m_t, v_t, emit_g=True, name="adamw_ssd_in_w")]
    outs = [[res[nm][k].reshape(weights[nm][0].shape) for nm in weights] for k in range(4)]
    return (loss, grad_x[None], *outs[0], *outs[1], *outs[2], *outs[3])
```

```python
import jax
import jax.numpy as jnp
from jax import lax
from jax.experimental import pallas as pl
from jax.experimental.pallas import tpu as pltpu

F32 = jnp.float32
BF16 = jnp.bfloat16
MESH = pl.DeviceIdType.MESH

D = 1024
DFF = 4096
DI = 2048
NH = 32
HP = 64
NG = 4
NS = 128
CH = 128
CONVD = DI + 2 * NG * NS
ZX = DI + CONVD
GW = NG * NS
LANES = 128
N_CHIPS = 4
N_DEV = 8
EPS = 1e-5
ADAM_LR, ADAM_B1, ADAM_B2, ADAM_EPS, ADAM_WD, ADAM_STEP = 1e-3, 0.9, 0.999, 1e-8, 0.01, 10
VMEM_LIMIT = 48 * 1024 * 1024
TM_ALL = 2048
TM_HALF = 1024
ANY = pl.BlockSpec(memory_space=pl.ANY)
SEM = pl.BlockSpec(memory_space=pltpu.SEMAPHORE)

SSD_IN_SHARD = 1288
SC_IN_SHARD = 768


def _params(sem=None):
    return pltpu.CompilerParams(dimension_semantics=sem, vmem_limit_bytes=VMEM_LIMIT)


def _sigmoid(v):
    return 0.5 * jnp.tanh(0.5 * v) + 0.5


def _dot(a, b, dims=((1,), (0,)), precision=None):
    return lax.dot_general(a, b, (dims, ((), ())), preferred_element_type=F32, precision=precision)


def _dot_nt(a, b):
    return _dot(a, b, ((1,), (1,)))


def _dot_tn(a, b):
    return _dot(a, b, ((0,), (0,)))


def _nn(av, bv):
    return _dot(av.astype(BF16), bv.astype(BF16))


def _nt(av, bv):
    return _dot_nt(av.astype(BF16), bv.astype(BF16))


def _nn_split(av, bv):
    return _dot(av.astype(BF16), bv.reshape(-1, bv.shape[2]))


def _nn_split_sq(av, bv):
    return _nn_split(av * av, bv)


def _nt_split(av, bv):
    kc = bv.shape[2]
    acc = _dot_nt(av[:, 0:kc].astype(BF16), bv[0])
    for s in range(1, bv.shape[0]):
        acc = acc + _dot_nt(av[:, s * kc:(s + 1) * kc].astype(BF16), bv[s])
    return acc


def _nt_sc_in(av, bv):
    q = 256
    acc = None
    for i in range(3 * D // q):
        a_blk = av[i // 4][:, (i % 4) * q:(i % 4 + 1) * q]
        b_blk = bv[i // 3][:, (i % 3) * q:(i % 3 + 1) * q]
        t = _dot_nt(a_blk, b_blk)
        acc = t if acc is None else acc + t
    return acc


def _matmul(a, b, *, name, n, contract=_nn, a_spec=None, b_spec=None, tm=512, tn=512, extras=(), epi=None,
            out_dtypes=(F32,), a_silu=False):
    M = a.shape[-2]
    tm, tn = min(tm, M), min(tn, n)
    assert M % tm == 0 and n % tn == 0, (name, M, n, tm, tn)
    n_ex = len(extras)
    if a_spec is None:
        a_spec = pl.BlockSpec((tm, a.shape[1]), lambda i, j: (i, 0))
    if b_spec is None:
        b_spec = (pl.BlockSpec((tn, b.shape[1]), lambda i, j: (j, 0)) if contract is _nt
                  else pl.BlockSpec((b.shape[0], tn), lambda i, j: (0, j)))

    def body(*refs):
        av = refs[0][...]
        if a_silu:
            av = av * _sigmoid(av)
        acc = contract(av, refs[1][...])
        res = epi(acc, *[r[...] for r in refs[2:2 + n_ex]]) if epi is not None else (acc,)
        for o_ref, r in zip(refs[2 + n_ex:], res, strict=True):
            o_ref[...] = r.astype(o_ref.dtype)

    in_specs = [a_spec, b_spec]
    for e in extras:
        in_specs.append(pl.BlockSpec((1, tn), lambda i, j: (0, j)) if e.shape[0] == 1 and M != 1
                        else pl.BlockSpec((tm, tn), lambda i, j: (i, j)))
    outs = pl.pallas_call(
        body, grid=(M // tm, n // tn), in_specs=in_specs,
        out_specs=[pl.BlockSpec((tm, tn), lambda i, j: (i, j)) for _ in out_dtypes],
        out_shape=[jax.ShapeDtypeStruct((M, n), dt) for dt in out_dtypes],
        compiler_params=_params(("parallel", "parallel")), name=name)(a, b, *extras)
    return outs if len(out_dtypes) > 1 else outs[0]


def _matmul_tn(a, b, *, name, m, n, tm=512, tn=512, a_spec=None, b_spec=None, out_spec=None, out_struct=None, into=None,
               a_silu=False, a_square=False):
    T = a.shape[-2]
    tm, tn = min(tm, m), min(tn, n)
    assert m % tm == 0 and n % tn == 0, (name, m, n, tm, tn)
    if a_spec is None:
        a_spec = pl.BlockSpec((T, tm), lambda i, j: (0, i))
    if b_spec is None:
        b_spec = pl.BlockSpec((T, tn), lambda i, j: (0, j))
    if out_spec is None:
        out_spec, out_struct = pl.BlockSpec((tm, tn), lambda i, j: (i, j)), jax.ShapeDtypeStruct((m, n), F32)

    def body(a_ref, b_ref, *rest):
        av = a_ref[...]
        if a_silu:
            av = av * _sigmoid(av)
        if a_square:
            av = av * av
        rest[-1][...] = _dot_tn(av.astype(BF16), b_ref[...].astype(BF16)).astype(rest[-1].dtype)

    args, in_specs, alias = [a, b], [a_spec, b_spec], {}
    if into is not None:
        args, in_specs, alias = args + [into], in_specs + [ANY], {2: 0}
    return pl.pallas_call(body, grid=(m // tm, n // tn), in_specs=in_specs, out_specs=out_spec, out_shape=out_struct,
                          input_output_aliases=alias, compiler_params=_params(("parallel", "parallel")), name=name)(*args)


def _modnorm_fwd(x, nw, sc, sh, *, name):
    L = x.shape[0]
    tm = min(L, 512)

    def body(x_ref, nw_ref, sc_ref, sh_ref, h_ref):
        xv = x_ref[...]
        r = lax.rsqrt(jnp.mean(xv * xv, axis=-1, keepdims=True) + EPS)
        h_ref[...] = ((xv * r * nw_ref[...]) * (1.0 + sc_ref[...]) + sh_ref[...]).astype(BF16)

    row = pl.BlockSpec((tm, D), lambda i: (i, 0))
    vec = pl.BlockSpec((1, D), lambda i: (0, 0))
    return pl.pallas_call(body, grid=(L // tm,), in_specs=[row, vec, vec, vec], out_specs=row,
                          out_shape=jax.ShapeDtypeStruct((L, D), BF16),
                          compiler_params=_params(("parallel",)), name=name)(x, nw, sc, sh)


def _gate_outputs(dx, below_refs, dy_ref, gs_ref):
    g_ref, y_ref = below_refs
    dy_ref[...] = (dx * g_ref[...]).astype(BF16)
    gs_ref[0:1, :] += jnp.sum(dx * y_ref[...].astype(F32), axis=0, keepdims=True)


def _modnorm_bwd(x, dh, dxo, nw, sc, gsum, below, *, name):
    L = x.shape[0]
    tm = min(L, 256)
    nb = 0 if below is None else 2

    def body(x_ref, dh_ref, dxo_ref, nw_ref, sc_ref, g_ref, *rest):
        dx_ref, s_ref = rest[nb:nb + 2]

        @pl.when(pl.program_id(0) == 0)
        def _():
            s_ref[...] = g_ref[...]
            if nb:
                rest[-1][...] = jnp.zeros_like(rest[-1])

        xv, dhv = x_ref[...], dh_ref[...].astype(F32)
        r = lax.rsqrt(jnp.mean(xv * xv, axis=-1, keepdims=True) + EPS)
        xhat = xv * r
        dxhat = dhv * (nw_ref[...] * (1.0 + sc_ref[...]))
        dx = dxo_ref[...] + r * (dxhat - xhat * jnp.mean(dxhat * xhat, axis=-1, keepdims=True))
        dx_ref[...] = dx
        s_ref[1:2, :] += jnp.sum(dhv * xhat, axis=0, keepdims=True) * (1.0 + sc_ref[...])
        s_ref[2:3, :] += jnp.sum(dhv * xhat, axis=0, keepdims=True) * nw_ref[...]
        s_ref[3:4, :] += jnp.sum(dhv, axis=0, keepdims=True)
        if nb:
            _gate_outputs(dx, rest[:nb], rest[-2], rest[-1])

    row = pl.BlockSpec((tm, D), lambda i: (i, 0))
    vec = pl.BlockSpec((1, D), lambda i: (0, 0))
    blk = pl.BlockSpec((8, D), lambda i: (0, 0))
    in_specs, out_specs = [row, row, row, vec, vec, blk], [row, blk]
    out_shape = [jax.ShapeDtypeStruct((L, D), F32), jax.ShapeDtypeStruct((8, D), F32)]
    if nb:
        in_specs, out_specs = in_specs + [vec, row], out_specs + [row, blk]
        out_shape += [jax.ShapeDtypeStruct((L, D), BF16), jax.ShapeDtypeStruct((8, D), F32)]
    return pl.pallas_call(body, grid=(L // tm,), in_specs=in_specs, out_specs=out_specs, out_shape=out_shape,
                          compiler_params=_params(("arbitrary",)), name=name)(x, dh, dxo, nw, sc, gsum, *(below or ()))


def _final_loss(x, fw, tgt, below, *, name):
    L = x.shape[0]
    tm = min(L, 256)

    def body(x_ref, fw_ref, t_ref, g_ref, y_ref, dx_ref, s_ref, dy_ref, gs_ref):
        @pl.when(pl.program_id(0) == 0)
        def _():
            s_ref[...] = jnp.zeros_like(s_ref)
            gs_ref[...] = jnp.zeros_like(gs_ref)

        xv = x_ref[...]
        r = lax.rsqrt(jnp.mean(xv * xv, axis=-1, keepdims=True) + EPS)
        xhat = xv * r
        diff = xhat * fw_ref[...] - t_ref[...]
        dout = diff * (1.0 / D)
        dxhat = dout * fw_ref[...]
        dx = r * (dxhat - xhat * jnp.mean(dxhat * xhat, axis=-1, keepdims=True))
        dx_ref[...] = dx
        s_ref[0:1, :] += jnp.sum(dout * xhat, axis=0, keepdims=True)
        s_ref[1:2, :] += jnp.zeros((1, D), F32) + 0.5 * jnp.sum(jnp.sum(diff * diff, axis=-1, keepdims=True) * (1.0 / D))
        _gate_outputs(dx, (g_ref, y_ref), dy_ref, gs_ref)

    row = pl.BlockSpec((tm, D), lambda i: (i, 0))
    vec = pl.BlockSpec((1, D), lambda i: (0, 0))
    blk = pl.BlockSpec((8, D), lambda i: (0, 0))
    return pl.pallas_call(body, grid=(L // tm,), in_specs=[row, vec, row, vec, row], out_specs=[row, blk, row, blk],
                          out_shape=[jax.ShapeDtypeStruct((L, D), F32), jax.ShapeDtypeStruct((8, D), F32),
                                     jax.ShapeDtypeStruct((L, D), BF16), jax.ShapeDtypeStruct((8, D), F32)],
                          compiler_params=_params(("arbitrary",)), name=name)(x, fw, tgt, *below)


def _shift_down(v, j):
    if j == 0:
        return v
    rolled = pltpu.roll(v, j, 0)
    row = lax.broadcasted_iota(jnp.int32, (8, v.shape[1]), 0)
    return jnp.concatenate([jnp.where(row >= j, rolled[0:8], 0.0), rolled[8:]], axis=0)


def _shift_up(v, j):
    if j == 0:
        return v
    n = v.shape[0]
    rolled = pltpu.roll(v, n - j, 0)
    row = lax.broadcasted_iota(jnp.int32, (8, v.shape[1]), 0)
    return jnp.concatenate([rolled[:n - 8], jnp.where(row < 8 - j, rolled[n - 8:], 0.0)], axis=0)


def _ssd_conv_fwd(zx, w, b, *, name):
    L = zx.shape[0]
    cb = 256
    k = w.shape[0]

    def body(x_ref, w_ref, b_ref, o_ref, p_ref):
        xv = x_ref[...].astype(F32)
        pre = b_ref[...] + xv * w_ref[k - 1:k, :]
        for j in range(1, k):
            pre = pre + _shift_down(xv, j) * w_ref[k - 1 - j:k - j, :]
        o_ref[...] = (pre * _sigmoid(pre)).astype(BF16)
        p_ref[...] = pre.astype(BF16)

    blk = pl.BlockSpec((L, cb), lambda i: (0, i))
    return pl.pallas_call(
        body, grid=(CONVD // cb,),
        in_specs=[pl.BlockSpec((L, cb), lambda i: (0, i + DI // cb)), pl.BlockSpec((k, cb), lambda i: (0, i)),
                  pl.BlockSpec((1, cb), lambda i: (0, i))],
        out_specs=[blk, blk], out_shape=[jax.ShapeDtypeStruct((L, CONVD), BF16)] * 2,
        compiler_params=_params(("parallel",)), name=name)(zx, w, b)


def _ssd_conv_bwd(zx, pre, dact, w, dzx, *, name):
    L = zx.shape[0]
    cb = 256
    k = w.shape[0]

    def body(x_ref, p_ref, da_ref, w_ref, _, dx_ref, s_ref):
        xv, pv = x_ref[...].astype(F32), p_ref[...].astype(F32)
        s = _sigmoid(pv)
        dpre = da_ref[...].astype(F32) * (s * (1.0 + pv * (1.0 - s)))
        s_ref[...] = jnp.zeros_like(s_ref)
        s_ref[k:k + 1, :] = jnp.sum(dpre, axis=0, keepdims=True)
        s_ref[k - 1:k, :] = jnp.sum(dpre * xv, axis=0, keepdims=True)
        dx = dpre * w_ref[k - 1:k, :]
        for j in range(1, k):
            later = _shift_up(dpre, j)
            dx = dx + later * w_ref[k - 1 - j:k - j, :]
            s_ref[k - 1 - j:k - j, :] = jnp.sum(later * xv, axis=0, keepdims=True)
        dx_ref[...] = dx.astype(BF16)

    blk = pl.BlockSpec((L, cb), lambda i: (0, i))
    return pl.pallas_call(
        body, grid=(CONVD // cb,),
        in_specs=[pl.BlockSpec((L, cb), lambda i: (0, i + DI // cb)), blk, blk, pl.BlockSpec((k, cb), lambda i: (0, i)), ANY],
        out_specs=[pl.BlockSpec((L, cb), lambda i: (0, i + DI // cb)), pl.BlockSpec((8, cb), lambda i: (0, i))],
        out_shape=[jax.ShapeDtypeStruct((L, ZX), BF16), jax.ShapeDtypeStruct((8, CONVD), F32)],
        input_output_aliases={4: 0}, compiler_params=_params(("parallel",)), name=name)(zx, pre, dact, w, dzx)


def _sc_fwd(proj, w, *, name):
    L = proj.shape[0]
    cb = 256
    nb = D // cb
    k = w.shape[0]

    def body(b_ref, c_ref, x_ref, w_ref, o_ref, v_ref):
        u = c_ref[...].astype(F32) * x_ref[...].astype(F32)
        v = u * w_ref[k - 1:k, :]
        for j in range(1, k):
            v = v + _shift_down(u, j) * w_ref[k - 1 - j:k - j, :]
        o_ref[...] = (b_ref[...].astype(F32) * v).astype(BF16)
        v_ref[...] = v.astype(BF16)

    blk = pl.BlockSpec((L, cb), lambda i: (0, i))
    return pl.pallas_call(
        body, grid=(nb,),
        in_specs=[blk, pl.BlockSpec((L, cb), lambda i: (0, i + nb)), pl.BlockSpec((L, cb), lambda i: (0, i + 2 * nb)),
                  pl.BlockSpec((k, cb), lambda i: (0, i))],
        out_specs=[blk, blk], out_shape=[jax.ShapeDtypeStruct((L, D), BF16)] * 2,
        compiler_params=_params(("parallel",)), name=name)(proj, proj, proj, w)


def _sc_bwd(proj, v, dyv, w, *, name):
    L = proj.shape[0]
    cb = 256
    nb = D // cb
    k = w.shape[0]

    def body(b_ref, c_ref, x_ref, v_ref, dy_ref, w_ref, dp_ref, s_ref):
        cv, xv = c_ref[...].astype(F32), x_ref[...].astype(F32)
        u = cv * xv
        dyv_ = dy_ref[...].astype(F32)
        dp_ref[0] = (dyv_ * v_ref[...].astype(F32)).astype(BF16)
        dv = dyv_ * b_ref[...].astype(F32)
        s_ref[...] = jnp.zeros_like(s_ref)
        s_ref[k - 1:k, :] = jnp.sum(dv * u, axis=0, keepdims=True)
        du = dv * w_ref[k - 1:k, :]
        for j in range(1, k):
            later = _shift_up(dv, j)
            du = du + later * w_ref[k - 1 - j:k - j, :]
            s_ref[k - 1 - j:k - j, :] = jnp.sum(later * u, axis=0, keepdims=True)
        dp_ref[1] = (du * xv).astype(BF16)
        dp_ref[2] = (du * cv).astype(BF16)

    blk = pl.BlockSpec((L, cb), lambda i: (0, i))
    return pl.pallas_call(
        body, grid=(nb,),
        in_specs=[blk, pl.BlockSpec((L, cb), lambda i: (0, i + nb)), pl.BlockSpec((L, cb), lambda i: (0, i + 2 * nb)),
                  blk, blk, pl.BlockSpec((k, cb), lambda i: (0, i))],
        out_specs=[pl.BlockSpec((3, L, cb), lambda i: (0, 0, i)), pl.BlockSpec((8, cb), lambda i: (0, i))],
        out_shape=[jax.ShapeDtypeStruct((3, L, D), BF16), jax.ShapeDtypeStruct((8, D), F32)],
        compiler_params=_params(("parallel",)), name=name)(proj, proj, proj, v, dyv, w)


def _pieces(v, n):
    out, rest = [], v
    for _ in range(n):
        out.append(rest.astype(BF16))
        rest = rest - out[-1].astype(F32)
    return out


def _cumsum_rows(mask, v):
    m = mask.astype(BF16)
    return _dot(jnp.concatenate([m, m, m], axis=1), jnp.concatenate(_pieces(v, 3), axis=0))


def _ssd_chunk_terms(dtr, prm):
    lane = lax.broadcasted_iota(jnp.int32, (CH, LANES), 1)
    valid = lane < NH
    xdt = dtr + prm[0:1, :]
    dt = jnp.where(valid, jnp.maximum(xdt, 0.0) + jnp.log1p(jnp.exp(-jnp.abs(xdt))), 0.0)
    A = -jnp.exp(prm[1:2, :])
    ri = lax.broadcasted_iota(jnp.int32, (CH, CH), 0)
    ci = lax.broadcasted_iota(jnp.int32, (CH, CH), 1)
    cs = _cumsum_rows(ri >= ci, dt * A)
    last = cs[CH - 1:CH, :]
    spread = (lax.broadcasted_iota(jnp.int32, (2 * LANES, DI), 1) // HP
              == lax.broadcasted_iota(jnp.int32, (2 * LANES, DI), 0) % LANES).astype(BF16)
    gather = ((lax.broadcasted_iota(jnp.int32, (LANES, 2 * DI), 1) % DI) // HP
              == lax.broadcasted_iota(jnp.int32, (LANES, 2 * DI), 0)).astype(BF16)
    return dict(valid=valid, xdt=xdt, dt=dt, A=A, cs=cs, csT=cs.T, last=last, ri=ri, ci=ci, ex=(spread, gather))


def _expand(v, ex):
    if v.shape[0] == 1:
        return _expand(jnp.broadcast_to(v, (8, LANES)), ex)[0:1, :]
    return _dot(jnp.concatenate(_pieces(v, 2), axis=1), ex[0])


def _head_sum(v, ex):
    if v.shape[0] == 1:
        return _head_sum(jnp.broadcast_to(v, (8, DI)), ex)[0:1, :]
    return _dot_nt(jnp.concatenate(_pieces(v, 2), axis=1), ex[1])


def _ssd_fwd(xbc, dtr, prm, *, name):
    L = xbc.shape[0]
    nc = L // CH

    def body(xbc_ref, dtr_ref, prm_ref, y_ref, sp_ref, st_ref):
        @pl.when(pl.program_id(0) == 0)
        def _():
            st_ref[...] = jnp.zeros_like(st_ref)

        prm_v = prm_ref[...]
        t = _ssd_chunk_terms(dtr_ref[...], prm_v)
        cs, csT, ex, causal = t["cs"], t["csT"], t["ex"], t["ri"] >= t["ci"]
        xs = xbc_ref[:, 0:DI].astype(F32)
        X = xs * _expand(t["dt"], ex)
        Xb = X.astype(BF16)
        Xd = (X * _expand(jnp.exp(t["last"] - cs), ex)).astype(BF16)
        Ex = _expand(jnp.exp(cs), ex)
        cdx = _expand(jnp.exp(t["last"]), ex)
        dskx = _expand(prm_v[2:3, :], ex)
        lane = lax.broadcasted_iota(jnp.int32, (CH, LANES), 1)
        sp_ref[0] = st_ref[...]
        for g in range(NG):
            Bg = xbc_ref[:, DI + g * NS:DI + (g + 1) * NS].astype(BF16)
            Cg = xbc_ref[:, DI + GW + g * NS:DI + GW + (g + 1) * NS].astype(BF16)
            G = _dot_nt(Cg, Bg)
            Sg = st_ref[:, g * GW:(g + 1) * GW]
            yoff = _dot(Cg, Sg.astype(BF16)) * Ex[:, g * GW:(g + 1) * GW]
            for j in range(GW // LANES):
                lo = g * GW + j * LANES
                Xp = Xb[:, lo:lo + LANES]
                yd = []
                for h in (lo // HP, lo // HP + 1):
                    seg = cs[:, h:h + 1] - csT[h:h + 1, :]
                    yd.append(_dot((G * jnp.where(causal, jnp.exp(seg), 0.0)).astype(BF16), Xp))
                y_ref[:, lo:lo + LANES] = (jnp.where(lane < HP, yd[0], yd[1]) + yoff[:, j * LANES:(j + 1) * LANES]
                                           + dskx[:, lo:lo + LANES] * xs[:, lo:lo + LANES]).astype(BF16)
            st_ref[:, g * GW:(g + 1) * GW] = Sg * cdx[:, g * GW:(g + 1) * GW] + _dot_tn(Bg, Xd[:, g * GW:(g + 1) * GW])

    return pl.pallas_call(
        body, grid=(nc,),
        in_specs=[pl.BlockSpec((CH, CONVD), lambda c: (c, 0)), pl.BlockSpec((CH, LANES), lambda c: (c, 0)),
                  pl.BlockSpec((8, LANES), lambda c: (0, 0))],
        out_specs=[pl.BlockSpec((CH, DI), lambda c: (c, 0)), pl.BlockSpec((1, NS, DI), lambda c: (c, 0, 0))],
        out_shape=[jax.ShapeDtypeStruct((L, DI), BF16), jax.ShapeDtypeStruct((nc, NS, DI), F32)],
        scratch_shapes=[pltpu.VMEM((NS, DI), F32)],
        compiler_params=_params(("arbitrary",)), name=name)(xbc, dtr, prm)


def _ssd_bwd(xbc, dtr, prm, dy, sprev, *, name):
    L = xbc.shape[0]
    nc = L // CH

    def body(xbc_ref, dtr_ref, prm_ref, dy_ref, sp_ref, dxbc_ref, ddtr_ref, s_ref, dst_ref, dx_scr, de_scr, dd_scr):
        step = pl.program_id(0)

        @pl.when(step == 0)
        def _():
            dst_ref[...] = jnp.zeros_like(dst_ref)
            s_ref[...] = jnp.zeros_like(s_ref)

        prm_v = prm_ref[...]
        t = _ssd_chunk_terms(dtr_ref[...], prm_v)
        cs, csT, ex, ri, ci = t["cs"], t["csT"], t["ex"], t["ri"], t["ci"]
        E = jnp.exp(cs)
        dec = jnp.exp(t["last"] - cs)
        cd = jnp.exp(t["last"])
        xs = xbc_ref[:, 0:DI].astype(F32)
        dtx = _expand(t["dt"], ex)
        X = xs * dtx
        Xb = X.astype(BF16)
        decx = _expand(dec, ex)
        Xd = (X * decx).astype(BF16)
        Ex = _expand(E, ex)
        cdx = _expand(cd, ex)
        dskx = _expand(prm_v[2:3, :], ex)
        lane = lax.broadcasted_iota(jnp.int32, (CH, LANES), 1)
        dcs = jnp.zeros((CH, LANES), F32)
        dcd_x = []
        for g in range(NG):
            gs = slice(g * GW, (g + 1) * GW)
            Bg = xbc_ref[:, DI + g * NS:DI + (g + 1) * NS].astype(BF16)
            Cg = xbc_ref[:, DI + GW + g * NS:DI + GW + (g + 1) * NS].astype(BF16)
            G = _dot_nt(Cg, Bg)
            GT = _dot_nt(Bg, Cg)
            Sg = sp_ref[0, :, gs]
            Sgb = Sg.astype(BF16)
            dyg = dy_ref[:, gs]
            de_scr[:, gs] = dyg * _dot(Cg, Sgb)
            dYo = (Ex[:, gs] * dyg).astype(BF16)
            dC = _dot_nt(dYo, Sgb)
            dS_in = _dot_tn(Cg, dYo)
            dStg = dst_ref[:, gs]
            dStb = dStg.astype(BF16)
            dXd = _dot(Bg, dStb)
            dB = _dot_nt(Xd[:, gs], dStb)
            dd_scr[:, gs] = dXd * X[:, gs]
            dXst = dXd * decx[:, gs]
            dG = jnp.zeros((CH, CH), F32)
            dGT = jnp.zeros((CH, CH), F32)
            for j in range(GW // LANES):
                lo = g * GW + j * LANES
                Xp = Xb[:, lo:lo + LANES]
                dyp = dy_ref[:, lo:lo + LANES]
                dXp = dXst[:, j * LANES:(j + 1) * LANES]
                for k, h in enumerate((lo // HP, lo // HP + 1)):
                    dyh = jnp.where((lane < HP) if k == 0 else (lane >= HP), dyp, 0.0).astype(BF16)
                    seg = cs[:, h:h + 1] - csT[h:h + 1, :]
                    Lm = jnp.where(ri >= ci, jnp.exp(seg), 0.0)
                    LmT = jnp.where(ci >= ri, jnp.exp(-seg), 0.0)
                    dM = _dot_nt(dyh, Xp)
                    dMT = _dot_nt(Xp, dyh)
                    MT = GT * LmT
                    rs = jnp.sum(dM * (G * Lm), axis=1, keepdims=True) - jnp.sum(dMT * MT, axis=1, keepdims=True)
                    dcs = dcs + jnp.where(lane == h, rs, 0.0)
                    dG = dG + dM * Lm
                    dGT = dGT + dMT * LmT
                    dXp = dXp + _dot(MT.astype(BF16), dyh)
                dx_scr[:, lo:lo + LANES] = dXp
            dxbc_ref[:, DI + g * NS:DI + (g + 1) * NS] = (dB + _dot(dGT.astype(BF16), Cg)).astype(BF16)
            dxbc_ref[:, DI + GW + g * NS:DI + GW + (g + 1) * NS] = (dC + _dot(dG.astype(BF16), Bg)).astype(BF16)
            dcd_x.append(jnp.sum(dStg * Sg, axis=0, keepdims=True))
            dst_ref[:, gs] = dStg * cdx[:, gs] + dS_in
        dX = dx_scr[...]
        dy = dy_ref[...]
        ddec = _head_sum(dd_scr[...], ex)
        dcd = _head_sum(jnp.concatenate(dcd_x, axis=1), ex)
        dcs = dcs + _head_sum(de_scr[...], ex) * E - ddec * dec
        row = lax.broadcasted_iota(jnp.int32, (CH, LANES), 0)
        dcs = dcs + jnp.where(row == CH - 1, jnp.sum(ddec * dec, axis=0, keepdims=True) + dcd * cd, 0.0)
        da = _cumsum_rows(ci >= ri, dcs)
        ddt = da * t["A"] + _head_sum(dX * xs, ex)
        ddtr = jnp.where(t["valid"], ddt * _sigmoid(t["xdt"]), 0.0)
        ddtr_ref[...] = ddtr
        dxbc_ref[:, 0:DI] = (dX * dtx + dskx * dy).astype(BF16)
        s_ref[0:1, :] += jnp.sum(da * t["dt"], axis=0, keepdims=True)
        s_ref[1:2, :] += _head_sum(jnp.sum(dy * xs, axis=0, keepdims=True), ex)
        s_ref[2:3, :] += jnp.sum(ddtr, axis=0, keepdims=True)

        @pl.when(step == nc - 1)
        def _():
            s_ref[0:1, :] = s_ref[0:1, :] * t["A"]

    rev = lambda c: (nc - 1 - c, 0)
    return pl.pallas_call(
        body, grid=(nc,),
        in_specs=[pl.BlockSpec((CH, CONVD), rev), pl.BlockSpec((CH, LANES), rev), pl.BlockSpec((8, LANES), lambda c: (0, 0)),
                  pl.BlockSpec((CH, DI), rev), pl.BlockSpec((1, NS, DI), lambda c: (nc - 1 - c, 0, 0))],
        out_specs=[pl.BlockSpec((CH, CONVD), rev), pl.BlockSpec((CH, LANES), rev), pl.BlockSpec((8, LANES), lambda c: (0, 0))],
        out_shape=[jax.ShapeDtypeStruct((L, CONVD), BF16), jax.ShapeDtypeStruct((L, LANES), F32),
                   jax.ShapeDtypeStruct((8, LANES), F32)],
        scratch_shapes=[pltpu.VMEM((NS, DI), F32), pltpu.VMEM((CH, DI), F32), pltpu.VMEM((CH, DI), F32),
                        pltpu.VMEM((CH, DI), F32)],
        compiler_params=_params(("arbitrary",)), name=name)(xbc, dtr, prm, dy, sprev)


def _gnorm_fwd(y, zx, nw, *, name):
    L = y.shape[0]
    tm = min(L, 256)

    def body(y_ref, z_ref, nw_ref, o_ref):
        z = z_ref[...].astype(F32)
        yg = y_ref[...].astype(F32) * (z * _sigmoid(z))
        for g in range(NG):
            v = yg[:, g * GW:(g + 1) * GW]
            r = lax.rsqrt(jnp.mean(v * v, axis=-1, keepdims=True) + EPS)
            o_ref[:, g * GW:(g + 1) * GW] = (v * r * nw_ref[:, g * GW:(g + 1) * GW]).astype(BF16)

    row = pl.BlockSpec((tm, DI), lambda i: (i, 0))
    return pl.pallas_call(body, grid=(L // tm,), in_specs=[row, row, pl.BlockSpec((1, DI), lambda i: (0, 0))],
                          out_specs=row, out_shape=jax.ShapeDtypeStruct((L, DI), BF16),
                          compiler_params=_params(("parallel",)), name=name)(y, zx, nw)


def _gnorm_bwd(y, zx, nw, dyn, *, name):
    L = y.shape[0]
    tm = min(L, 256)

    def body(y_ref, z_ref, nw_ref, dyn_ref, dy_ref, dz_ref, s_ref):
        @pl.when(pl.program_id(0) == 0)
        def _():
            s_ref[...] = jnp.zeros_like(s_ref)

        z, yv = z_ref[...].astype(F32), y_ref[...].astype(F32)
        sz = _sigmoid(z)
        gate = z * sz
        dgate_dz = sz * (1.0 + z * (1.0 - sz))
        for g in range(NG):
            gs = slice(g * GW, (g + 1) * GW)
            v = yv[:, gs] * gate[:, gs]
            r = lax.rsqrt(jnp.mean(v * v, axis=-1, keepdims=True) + EPS)
            vhat = v * r
            dn = dyn_ref[:, gs].astype(F32)
            s_ref[0:1, gs] += jnp.sum(dn * vhat, axis=0, keepdims=True)
            dvhat = dn * nw_ref[:, gs]
            dv = r * (dvhat - vhat * jnp.mean(dvhat * vhat, axis=-1, keepdims=True))
            dy_ref[:, gs] = dv * gate[:, gs]
            dz_ref[:, gs] = (dv * yv[:, gs] * dgate_dz[:, gs]).astype(BF16)

    row = pl.BlockSpec((tm, DI), lambda i: (i, 0))
    return pl.pallas_call(body, grid=(L // tm,), in_specs=[row, row, pl.BlockSpec((1, DI), lambda i: (0, 0)), row],
                          out_specs=[row, row, pl.BlockSpec((8, DI), lambda i: (0, 0))],
                          out_shape=[jax.ShapeDtypeStruct((L, DI), F32), jax.ShapeDtypeStruct((L, ZX), BF16),
                                     jax.ShapeDtypeStruct((8, DI), F32)],
                          compiler_params=_params(("arbitrary",)), name=name)(y, zx, nw, dyn)


def _adamw(w, g, m, v, *, name, g_row=0, w_row=0, rows=None, into=None, emit_g=False):
    lead = w.ndim == 3
    R, C = w.shape[-2:]
    rows = R if rows is None else rows
    tr = max([t for t in range(8, rows + 1, 8) if rows % t == 0 and t * C <= 256 * 1024], default=rows)
    assert g_row % tr == 0 and w_row % tr == 0, (name, g_row, w_row, tr)
    n_out = 4 if emit_g else 3

    def body(w_ref, g_ref, m_ref, v_ref, *rest):
        outs = rest[-n_out:]
        gv = g_ref[...]
        mn = ADAM_B1 * m_ref[...] + (1.0 - ADAM_B1) * gv
        vn = ADAM_B2 * v_ref[...] + (1.0 - ADAM_B2) * (gv * gv)
        m_hat = mn / (1.0 - ADAM_B1 ** ADAM_STEP)
        v_hat = vn / (1.0 - ADAM_B2 ** ADAM_STEP)
        d_ref, mo_ref, vo_ref = outs[-3:]
        d_ref[...] = -ADAM_LR * (m_hat / (jnp.sqrt(v_hat) + ADAM_EPS) + ADAM_WD * w_ref[...])
        mo_ref[...] = mn
        vo_ref[...] = vn
        if emit_g:
            outs[0][...] = gv

    blk = (pl.BlockSpec((None, tr, C), lambda i: (0, i + w_row // tr, 0)) if lead
           else pl.BlockSpec((tr, C), lambda i: (i + w_row // tr, 0)))
    args, in_specs, alias = [w, g, m, v], [blk, pl.BlockSpec((tr, C), lambda i: (i + g_row // tr, 0)), blk, blk], {}
    if into is not None:
        args, in_specs, alias = args + list(into), in_specs + [ANY] * n_out, {4 + k: k for k in range(n_out)}
    return pl.pallas_call(body, grid=(rows // tr,), in_specs=in_specs, out_specs=[blk] * n_out,
                          out_shape=[jax.ShapeDtypeStruct(w.shape, F32)] * n_out, input_output_aliases=alias,
                          compiler_params=_params(("parallel",)), name=name)(*args)


def _residual(acc, xv, gv):
    return xv + gv * acc, acc


def _like(buf):
    return jax.ShapeDtypeStruct(buf.shape, buf.dtype)


def _mlp_fwd(x, mod, nw, wb, up_row, down_row, tag, midway=None):
    sh, sc, g = mod
    h = _modnorm_fwd(x, nw, sc, sh, name=tag + "_norm")
    a = _matmul(h, wb, n=DFF, tm=TM_ALL, b_spec=pl.BlockSpec((None, D, 512), lambda mi, j: (j // 2, up_row // D, j % 2)),
                epi=lambda acc: (jnp.maximum(acc, 0.0),), out_dtypes=(BF16,), name=tag + "_up")
    if midway is not None:
        midway(a)
    xn, y = _matmul(a, wb, n=D, tm=TM_HALF, contract=_nn_split_sq,
                    b_spec=pl.BlockSpec((N_CHIPS, D, 512), lambda mi, j: (0, down_row // D, j)),
                    extras=(x, g), epi=_residual, out_dtypes=(F32, BF16), name=tag + "_down")
    return xn, (x, h, a, y)


def _mlp_bwd(dxo, dy, gsum, saved, mod, nw, wb, gb, up_row, down_row, below, tag):
    x, h, a, y = saved
    sh, sc, g = mod
    du = _matmul(dy, wb, n=DFF, tm=TM_ALL, contract=_nt,
                 b_spec=pl.BlockSpec((None, 512, D), lambda mi, j: (j // 2, down_row // 512 + j % 2, 0)),
                 extras=(a,), epi=lambda acc, av: (acc * (2.0 * av.astype(F32)),), out_dtypes=(BF16,), name=tag + "_dact")
    gb = _matmul_tn(a, dy, m=DFF, n=D, tm=D, tn=D, a_square=True, into=gb, out_struct=_like(wb),
                    out_spec=pl.BlockSpec((None, D, D), lambda mi, j: (mi, down_row // D, 0)), name=tag + "_ddown")
    dh = _matmul(du, wb, n=D, tm=TM_HALF, contract=_nt_split,
                 b_spec=pl.BlockSpec((N_CHIPS, 512, D), lambda mi, j: (0, up_row // 512 + j, 0)), out_dtypes=(BF16,),
                 name=tag + "_dh")
    gb = _matmul_tn(h, du, m=D, n=DFF, tm=D, into=gb, out_struct=_like(wb),
                    out_spec=pl.BlockSpec((None, D, 512), lambda mi, j: (j // 2, up_row // D, j % 2)), name=tag + "_dup")
    dx, sums, *nxt = _modnorm_bwd(x, dh, dxo, nw, sc, gsum, below, name=tag + "_dnorm")
    return dx, gb, sums, *nxt


def _ssd_fwd_scan(x, mod, nw, w_in_t, w_dt_t, conv_w, conv_b, prm, tag):
    sh, sc, g = mod
    h = _modnorm_fwd(x, nw, sc, sh, name=tag + "_norm")
    zx = _matmul(h, w_in_t, n=ZX, tm=TM_ALL, contract=_nt, out_dtypes=(BF16,), name=tag + "_in")
    dtr = _matmul(h, w_dt_t, n=LANES, tm=TM_ALL, contract=_nt, name=tag + "_in_dt")
    xbc, pre = _ssd_conv_fwd(zx, conv_w, conv_b, name=tag + "_conv")
    y, sprev = _ssd_fwd(xbc, dtr, prm, name=tag + "_scan")
    return h, zx, dtr, xbc, y, sprev, pre


def _ssd_fwd_out(x, mod, scan, gn_w, get_w_out, tag):
    sh, sc, g = mod
    h, zx, dtr, xbc, y, sprev, pre = scan
    yn = _gnorm_fwd(y, zx, gn_w, name=tag + "_gnorm")
    w_out = get_w_out(yn)
    xn, yo = _matmul(yn, w_out, n=D, tm=TM_HALF, contract=_nn_split,
                     b_spec=pl.BlockSpec((N_CHIPS, 512, 512), lambda mi, j: (0, 0, j)),
                     extras=(x, g), epi=_residual, out_dtypes=(F32, BF16), name=tag + "_out")
    return xn, (x, h, zx, dtr, xbc, y, sprev, yn, yo, pre)


def _ssd_bwd_out(dyo, saved, w_out, tag):
    x, h, zx, dtr, xbc, y, sprev, yn, yo, pre = saved
    dyn = _matmul(dyo, w_out, n=DI, tm=TM_ALL, contract=_nt, b_spec=pl.BlockSpec((None, 512, D), lambda mi, j: (j, 0, 0)),
                  out_dtypes=(BF16,), name=tag + "_dyn")
    g_out = _matmul_tn(yn, dyo, m=DI, n=D, tn=D, out_struct=_like(w_out),
                       out_spec=pl.BlockSpec((None, 512, D), lambda mi, j: (mi, 0, 0)), name=tag + "_dout")
    return dyn, g_out


def _ssd_bwd_rest(dxo, dy, dzx, gsum, saved, mod, nw, w_in_t, w_dt_t, conv_w, prm, tag):
    x, h, zx, dtr, xbc, y, sprev, yn, yo, pre = saved
    sh, sc, g = mod
    dxbc, ddtr, ssum = _ssd_bwd(xbc, dtr, prm, dy, sprev, name=tag + "_dscan")
    dzx, csum = _ssd_conv_bwd(zx, pre, dxbc, conv_w, dzx, name=tag + "_dconv")
    dh_dt = _matmul(ddtr, w_dt_t, n=D, tm=TM_ALL, name=tag + "_dh_dt")
    dh = _matmul(dzx, w_in_t, n=D, tm=TM_HALF, b_spec=pl.BlockSpec((ZX, 512), lambda mi, j: (0, j)), extras=(dh_dt,),
                 epi=lambda acc, e: (acc + e,), out_dtypes=(BF16,), name=tag + "_dh")
    d_w_zx = _matmul_tn(h, dzx, m=D, n=ZX, tm=D, name=tag + "_din")
    d_w_dt = _matmul_tn(h, ddtr, m=D, n=LANES, tm=D, name=tag + "_din_dt")
    dx, sums = _modnorm_bwd(x, dh, dxo, nw, sc, gsum, None, name=tag + "_dnorm")
    return dx, d_w_zx, d_w_dt, sums, csum, ssum


def _sc_layer_fwd(x, mod, nw, w_sc_in, conv_w, wb, out_row, tag, midway=None):
    sh, sc, g = mod
    h = _modnorm_fwd(x, nw, sc, sh, name=tag + "_norm")
    proj = _matmul(h, w_sc_in, n=3 * D, tm=TM_ALL, tn=256, out_dtypes=(BF16,),
                   b_spec=pl.BlockSpec((None, D, 256), lambda mi, j: (j // 3, 0, j % 3)),
                   name=tag + "_in")
    if midway is not None:
        midway(proj)
    yv, v = _sc_fwd(proj, conv_w, name=tag + "_conv")
    xn, yo = _matmul(yv, wb, n=D, tm=TM_HALF, contract=_nn_split,
                     b_spec=pl.BlockSpec((N_CHIPS, 256, 512), lambda mi, j: (0, out_row // 256, j)),
                     extras=(x, g), epi=_residual, out_dtypes=(F32, BF16), name=tag + "_out")
    return xn, (x, h, proj, yv, yo, v)


def _sc_layer_bwd(dxo, dyo, gsum, saved, mod, nw, w_sc_in, conv_w, wb, gb, out_row, below, tag):
    x, h, proj, yv, yo, v = saved
    sh, sc, g = mod
    L = x.shape[0]
    dyv = _matmul(dyo, wb, n=D, tm=TM_ALL, tn=256, contract=_nt,
                  b_spec=pl.BlockSpec((None, 256, D), lambda mi, j: (j, out_row // 256, 0)), out_dtypes=(BF16,),
                  name=tag + "_dyv")
    gb = _matmul_tn(yv, dyo, m=D, n=D, tm=256, tn=D, into=gb, out_struct=_like(wb),
                    out_spec=pl.BlockSpec((None, 256, D), lambda mi, j: (mi, out_row // 256, 0)), name=tag + "_dout")
    dproj, csum = _sc_bwd(proj, v, dyv, conv_w, name=tag + "_dconv")
    tm = min(L, TM_HALF)
    dh = _matmul(dproj, w_sc_in, n=D, tm=tm, contract=_nt_sc_in, a_spec=pl.BlockSpec((3, tm, D), lambda mi, j: (0, mi, 0)),
                 b_spec=pl.BlockSpec((N_CHIPS, 512, SC_IN_SHARD), lambda mi, j: (0, j, 0)), out_dtypes=(BF16,),
                 name=tag + "_dh")
    g_sc_in = _matmul_tn(h, dproj, m=D, n=3 * D, tm=D, tn=256, b_spec=pl.BlockSpec((None, L, 256), lambda mi, j: (j // 4, 0, j % 4)),
                         out_spec=pl.BlockSpec((None, D, 256), lambda mi, j: (j // 3, 0, j % 3)),
                         out_struct=jax.ShapeDtypeStruct((N_CHIPS, D, SC_IN_SHARD), BF16), name=tag + "_din")
    dx, sums, *nxt = _modnorm_bwd(x, dh, dxo, nw, sc, gsum, below, name=tag + "_dnorm")
    return dx, gb, g_sc_in, sums, csum, *nxt


SUB_ROW = (0, 8, 16, 24)
SSD_CONV_ROW, GNORM_ROW, FINAL_ROW, SC_CONV_ROW, HEAD_ROW, SMALL_ROWS = 32, 48, 56, 64, 72, 80


def _all_gather_rows(blk, *, name):
    m_per, n = blk.shape

    def body(x_ref, out_ref, send_sems, recv_sems, local_sem):
        x, y, c = lax.axis_index("x"), lax.axis_index("y"), lax.axis_index("c")
        me, sibling = (x, y, c), (x, y, 1 - c)
        chips = [(1 - x, y), (x, 1 - y), (1 - x, 1 - y)]

        def rows(px, py, pc):
            return out_ref.at[pl.ds((4 * px + 2 * py + pc) * m_per, m_per), :]

        def copy(k, block, to, src=None):
            return pltpu.make_async_remote_copy(src_ref=rows(*block) if src is None else src, dst_ref=rows(*block),
                                                send_sem=send_sems.at[k], recv_sem=recv_sems.at[k], device_id=to,
                                                device_id_type=MESH)

        mine = pltpu.make_async_copy(x_ref, rows(*me), local_sem)
        mine.start()
        first = [copy(0, me, sibling, src=x_ref)] + [copy(1 + j, me, (*chip, c), src=x_ref) for j, chip in enumerate(chips)]
        for cp in first:
            cp.start()
        passed = [copy(4 + j, (*chip, c), sibling) for j, chip in enumerate(chips)]
        for j, chip in enumerate(chips):
            copy(1 + j, (*chip, c), me).wait_recv()
            passed[j].start()
        copy(0, sibling, me).wait_recv()
        for j, chip in enumerate(chips):
            copy(4 + j, (*chip, 1 - c), me).wait_recv()
        for cp in first + passed:
            cp.wait_send()
        mine.wait()

    return pl.pallas_call(
        body, out_shape=jax.ShapeDtypeStruct((N_DEV * m_per, n), blk.dtype),
        in_specs=[pl.BlockSpec(memory_space=pltpu.VMEM)], out_specs=pl.BlockSpec(memory_space=pltpu.VMEM),
        scratch_shapes=[pltpu.SemaphoreType.DMA((7,)), pltpu.SemaphoreType.DMA((7,)), pltpu.SemaphoreType.DMA],
        name=name)(blk)


def _half(ref, chip, c):
    r, n = ref.shape[1:]
    if r % 32 == 0:
        return ref.at[chip, pl.ds(c * (r // 2), r // 2), :]
    assert n % 256 == 0, ref.shape
    return ref.at[chip, :, pl.ds(c * (n // 2), n // 2)]


def _gather_copy(bufs, sends, recvs, b, k, chip, pc, to):
    piece = _half(bufs[b], 2 * chip[0] + chip[1], pc)
    return pltpu.make_async_remote_copy(src_ref=piece, dst_ref=piece, send_sem=sends.at[4 * b + k], recv_sem=recvs.at[4 * b + k],
                                        device_id=to, device_id_type=MESH)


def _split_call(body, bufs, sems_in, n_sems, *, name, after=(), token=False, lands=()):
    nb, na, nl, starts = len(bufs), len(after), len(lands), not sems_in

    def wrapped(*refs):
        sems = refs[nb + na:nb + na + 2] if starts else refs[nb:nb + 2]
        made = refs[nb + na + 2 + nb:nb + na + 2 + nb + nl] if starts else ()
        body(tuple(refs[:nb]) + tuple(made), sems[0], sems[1])
        if token:
            refs[-1][...] = jnp.zeros_like(refs[-1])

    out_shape = [pltpu.SemaphoreType.DMA((n_sems,)) for _ in range(2 if starts else 0)]
    out_specs = [SEM] * len(out_shape) + [ANY] * (nb + nl)
    alias = {b: len(out_shape) + b for b in range(nb)}
    out_shape += [jax.ShapeDtypeStruct(b.shape, b.dtype) for b in bufs] + list(lands)
    if token:
        out_shape.append(jax.ShapeDtypeStruct((8, LANES), F32))
        out_specs.append(pl.BlockSpec(memory_space=pltpu.VMEM))
    return pl.pallas_call(
        wrapped, out_shape=out_shape, in_specs=[ANY] * nb + [SEM] * len(sems_in) + [ANY] * na, out_specs=out_specs,
        input_output_aliases=alias,
        compiler_params=pltpu.CompilerParams(has_side_effects=pltpu.SideEffectType.DATAFLOW_SIDE_EFFECTING),
        name=name)(*bufs, *sems_in, *after)


def _gather_start(bufs, *, name, after=()):
    nb = len(bufs)

    def body(ins, sends, recvs):
        x, y, c = lax.axis_index("x"), lax.axis_index("y"), lax.axis_index("c")
        chips = [(1 - x, y), (x, 1 - y), (1 - x, 1 - y)]
        for b in range(nb):
            _gather_copy(ins, sends, recvs, b, 0, (x, y), c, (x, y, 1 - c)).start()
            for j, chip in enumerate(chips):
                _gather_copy(ins, sends, recvs, b, 1 + j, (x, y), c, (*chip, c)).start()

    out = _split_call(body, bufs, (), 4 * nb, name=name, after=after, token=True)
    return (out[0], out[1], out[2:2 + nb]), out[-1]


def _gather_wait_first(flight, *, name, after=()):
    sends, recvs, bufs = flight
    nb = len(bufs)

    def body(ins, sends_, recvs_):
        x, y, c = lax.axis_index("x"), lax.axis_index("y"), lax.axis_index("c")
        chips = [(1 - x, y), (x, 1 - y), (1 - x, 1 - y)]
        for b in range(nb):
            _gather_copy(ins, sends_, recvs_, b, 0, (x, y), c, (x, y, 1 - c)).wait_send()
            _gather_copy(ins, sends_, recvs_, b, 0, (x, y), 1 - c, (x, y, c)).wait_recv()
            for j, chip in enumerate(chips):
                _gather_copy(ins, sends_, recvs_, b, 1 + j, (x, y), c, (*chip, c)).wait_send()
                _gather_copy(ins, sends_, recvs_, b, 1 + j, chip, c, (x, y, c)).wait_recv()

    return _split_call(body, bufs, (sends, recvs), 4 * nb, name=name, after=after)


def _gather_forward(bufs, *, name):
    nb = len(bufs)

    def body(ins, sends, recvs):
        x, y, c = lax.axis_index("x"), lax.axis_index("y"), lax.axis_index("c")
        chips = [(1 - x, y), (x, 1 - y), (1 - x, 1 - y)]
        for b in range(nb):
            for j, chip in enumerate(chips):
                _gather_copy(ins, sends, recvs, b, 1 + j, chip, c, (x, y, 1 - c)).start()

    out = _split_call(body, bufs, (), 4 * nb, name=name)
    return out[0], out[1], out[2:2 + nb]


def _gather_wait_forward(flight, *, name, after=()):
    sends, recvs, bufs = flight
    nb = len(bufs)

    def body(ins, sends_, recvs_):
        x, y, c = lax.axis_index("x"), lax.axis_index("y"), lax.axis_index("c")
        chips = [(1 - x, y), (x, 1 - y), (1 - x, 1 - y)]
        for b in range(nb):
            for j, chip in enumerate(chips):
                _gather_copy(ins, sends_, recvs_, b, 1 + j, chip, c, (x, y, 1 - c)).wait_send()
                _gather_copy(ins, sends_, recvs_, b, 1 + j, chip, 1 - c, (x, y, c)).wait_recv()

    return _split_call(body, bufs, (sends, recvs), 4 * nb, name=name, after=after)


def _owner_copies(hs, lands, sends, recvs):
    x, y, c = lax.axis_index("x"), lax.axis_index("y"), lax.axis_index("c")
    chips = [(1 - x, y), (x, 1 - y), (1 - x, 1 - y)]
    return [pltpu.make_async_remote_copy(src_ref=hs[b].at[2 * cx + cy], dst_ref=lands[b].at[j], send_sem=sends.at[3 * b + j],
                                         recv_sem=recvs.at[3 * b + j], device_id=(cx, cy, c), device_id_type=MESH)
            for b in range(len(hs)) for j, (cx, cy) in enumerate(chips)]


def _owners_start(hs, *, name):
    nb = len(hs)
    lands = [jax.ShapeDtypeStruct((3,) + h.shape[1:], h.dtype) for h in hs]

    def body(refs, sends, recvs):
        for cp in _owner_copies(refs[:nb], refs[nb:], sends, recvs):
            cp.start()

    out = _split_call(body, list(hs), (), 3 * nb, name=name, token=True, lands=lands)
    return (out[0], out[1], out[2:2 + 2 * nb]), out[-1]


def _owners_wait(flight, *, name, after=()):
    sends, recvs, bufs = flight
    nb = len(bufs) // 2

    def body(refs, sends_, recvs_):
        for cp in _owner_copies(refs[:nb], refs[nb:], sends_, recvs_):
            cp.wait()

    out = _split_call(body, bufs, (sends, recvs), 3 * nb, name=name, after=after)
    return out[:nb], out[nb:]


def _sibling_copies(gs, lands, sends, recvs):
    x, y, c = lax.axis_index("x"), lax.axis_index("y"), lax.axis_index("c")
    copies = []
    for b in range(len(gs)):
        hr = gs[b].shape[1] // 2
        copies.append(pltpu.make_async_remote_copy(
            src_ref=gs[b].at[:, pl.ds((1 - c) * hr, hr), :], dst_ref=lands[b], send_sem=sends.at[b], recv_sem=recvs.at[b],
            device_id=(x, y, 1 - c), device_id_type=MESH))
    return copies


def _sibling_start(gs, *, name, after=()):
    nb = len(gs)
    lands = [jax.ShapeDtypeStruct((g.shape[0], g.shape[1] // 2, g.shape[2]), g.dtype) for g in gs]

    def body(refs, sends, recvs):
        for cp in _sibling_copies(refs[:nb], refs[nb:], sends, recvs):
            cp.start()

    out = _split_call(body, list(gs), (), nb, name=name, after=after, token=True, lands=lands)
    return (out[0], out[1], out[2:2 + 2 * nb]), out[-1]


def _sibling_wait(flight, *, name, after=()):
    sends, recvs, bufs = flight
    nb = len(bufs) // 2

    def body(refs, sends_, recvs_):
        for cp in _sibling_copies(refs[:nb], refs[nb:], sends_, recvs_):
            cp.wait()

    out = _split_call(body, bufs, (sends, recvs), nb, name=name, after=after)
    return out[:nb], out[nb:]


def _result_copies(ts, sends, recvs):
    x, y, c = lax.axis_index("x"), lax.axis_index("y"), lax.axis_index("c")
    return [pltpu.make_async_remote_copy(src_ref=ts[b].at[c], dst_ref=ts[b].at[c], send_sem=sends.at[b], recv_sem=recvs.at[b],
                                         device_id=(x, y, 1 - c), device_id_type=MESH) for b in range(len(ts))]


def _result_start(ts, *, name):
    def body(refs, sends, recvs):
        for cp in _result_copies(refs, sends, recvs):
            cp.start()

    out = _split_call(body, ts, (), len(ts), name=name, token=True)
    return (out[0], out[1], out[2:2 + len(ts)]), out[-1]


def _result_wait(flight, *, name, after=()):
    sends, recvs, bufs = flight

    def body(refs, sends_, recvs_):
        for cp in _result_copies(refs, sends_, recvs_):
            cp.wait()

    return _split_call(body, bufs, (sends, recvs), len(bufs), name=name, after=after)


def _row_tile(rows, cols):
    best = 16
    for t in range(16, rows + 1, 16):
        if rows % t == 0 and t * cols <= 640 * 1024:
            best = t
    assert rows % best == 0, (rows, cols)
    return best


def _add_sibling_half(g, recv, core, *, name):
    nk, r, n = g.shape
    hr = r // 2
    tr = _row_tile(hr, n)

    def body(c_ref, a_ref, b_ref, o_ref):
        o_ref[...] = (a_ref[...].astype(F32) + b_ref[...].astype(F32)).astype(BF16)

    grid_spec = pltpu.PrefetchScalarGridSpec(
        num_scalar_prefetch=1, grid=(nk, hr // tr),
        in_specs=[pl.BlockSpec((None, tr, n), lambda k, i, c_ref: (k, c_ref[0] * (hr // tr) + i, 0)),
                  pl.BlockSpec((None, tr, n), lambda k, i, c_ref: (k, i, 0))],
        out_specs=pl.BlockSpec((None, tr, n), lambda k, i, c_ref: (k, i, 0)))
    return pl.pallas_call(body, grid_spec=grid_spec, out_shape=jax.ShapeDtypeStruct((nk, hr, n), BF16),
                          compiler_params=_params(("parallel", "parallel")), name=name)(core, g, recv)


def _add_chip_sums(h, recv, chip_core, *, name):
    _, hr, n = h.shape
    tr = _row_tile(hr, n)

    def body(k_ref, a_ref, b_ref, o_ref):
        o_ref[...] = ((a_ref[...].astype(F32) + b_ref[0].astype(F32)) + b_ref[1].astype(F32)) + b_ref[2].astype(F32)

    grid_spec = pltpu.PrefetchScalarGridSpec(
        num_scalar_prefetch=1, grid=(hr // tr,),
        in_specs=[pl.BlockSpec((None, tr, n), lambda i, k_ref: (k_ref[0], i, 0)),
                  pl.BlockSpec((3, tr, n), lambda i, k_ref: (0, i, 0))],
        out_specs=pl.BlockSpec((None, tr, n), lambda i, k_ref: (k_ref[1], i, 0)))
    return pl.pallas_call(body, grid_spec=grid_spec, out_shape=jax.ShapeDtypeStruct((2, hr, n), F32),
                          compiler_params=_params(("parallel",)), name=name)(chip_core, h, recv)


def _sum_devices(g, *, name):
    nd, r, n = g.shape

    def body(g_ref, o_ref):
        acc = g_ref[0]
        for i in range(1, nd):
            acc = acc + g_ref[i]
        o_ref[...] = acc

    return pl.pallas_call(body, out_shape=jax.ShapeDtypeStruct((r, n), F32), name=name)(g)


def _own_slot(parts, chip, *, name, after=()):
    rows, cols = sum(w.shape[1] for w, _ in parts), parts[0][0].shape[2]
    buf, row0 = None, 0
    for p, (w, idx) in enumerate(parts):
        r = w.shape[1]
        tr = 256 if r % 256 == 0 else r
        assert row0 % tr == 0, (name, r, row0)
        prev = () if buf is None else (buf,)

        def body(chip_ref, w_ref, *rest):
            rest[-1][...] = w_ref[...].astype(BF16)

        grid_spec = pltpu.PrefetchScalarGridSpec(
            num_scalar_prefetch=1, grid=(r // tr,),
            in_specs=[pl.BlockSpec((None, tr, cols), lambda i, c_ref, idx=idx: (idx, i, 0))] + [ANY] * (len(prev) + len(after)),
            out_specs=pl.BlockSpec((None, tr, cols), lambda i, c_ref, row0=row0, tr=tr: (c_ref[0], row0 // tr + i, 0)))
        buf = pl.pallas_call(body, grid_spec=grid_spec, out_shape=jax.ShapeDtypeStruct((N_CHIPS, rows, cols), BF16),
                             input_output_aliases={2: 0} if prev else {}, compiler_params=_params(("parallel",)),
                             name=f"{name}{p}")(chip, w, *prev, *after)
        row0 += r
    return buf


def kernel(x, c, ada_w, ada_b, mix_norm_w, mlp_norm_w, mlp_up, mlp_down, ssd_in_w, ssd_conv_w, ssd_conv_b, ssd_dt_bias, ssd_A_log, ssd_D, ssd_norm_w, ssd_out_w, sc_in_w, sc_conv_w, sc_out_w, final_norm_w, loss_target, m_ada_w, m_ada_b, m_mix_norm_w, m_mlp_norm_w, m_mlp_up, m_mlp_down, m_ssd_in_w, m_ssd_conv_w, m_ssd_conv_b, m_ssd_dt_bias, m_ssd_A_log, m_ssd_D, m_ssd_norm_w, m_ssd_out_w, m_sc_in_w, m_sc_conv_w, m_sc_out_w, m_final_norm_w, v_ada_w, v_ada_b, v_mix_norm_w, v_mlp_norm_w, v_mlp_up, v_mlp_down, v_ssd_in_w, v_ssd_conv_w, v_ssd_conv_b, v_ssd_dt_bias, v_ssd_A_log, v_ssd_D, v_ssd_norm_w, v_ssd_out_w, v_sc_in_w, v_sc_conv_w, v_sc_out_w, v_final_norm_w):
    xi, yi, ci = lax.axis_index("x"), lax.axis_index("y"), lax.axis_index("c")
    chip = 2 * xi + yi
    dev = 2 * chip + ci
    n_ada = ada_w.shape[2]

    conv_flat = jnp.concatenate([ssd_conv_w.reshape(-1), sc_conv_w.reshape(-1), jnp.zeros((256,), F32)]).reshape(4, D)
    blk0 = jnp.concatenate([c, conv_flat, jnp.zeros((3, D), F32)], axis=0)
    got0 = _all_gather_rows(blk0, name="gather_cond").reshape(N_DEV, 8, D)
    c_all = got0[:, 0]
    conv_all = got0[0::2, 1:5].reshape(N_CHIPS, 4 * D)
    ssd_conv = jnp.moveaxis(conv_all[:, :4 * 768].reshape(N_CHIPS, 4, 768), 0, 1).reshape(4, CONVD)
    sc_conv = jnp.moveaxis(conv_all[:, 4 * 768:4 * 768 + 3 * 256].reshape(N_CHIPS, 3, 256), 0, 1).reshape(3, D)
    mod_shard = [_matmul(c_all, ada_w, n=n_ada, a_silu=True, b_spec=pl.BlockSpec((None, D, 512), lambda mi, j, i=i: (i, 0, j)),
                         extras=(lax.dynamic_slice(ada_b, (i, chip * n_ada), (1, n_ada)),),
                         epi=lambda acc, b: (acc + b,), name=f"ada_mod{i}") for i in range(2)]
    mod_slot = lax.dynamic_update_slice(jnp.zeros((N_CHIPS, 2 * N_DEV, n_ada), F32), jnp.concatenate(mod_shard, axis=0)[None],
                                        (chip, 0, 0))

    up_row, down_row = 0, D
    chip1 = chip.reshape(1).astype(jnp.int32)
    a_bufs = [mod_slot, _own_slot([(jnp.swapaxes(ssd_in_w, 1, 2), 0)], chip1, name="slot_ssd_in")]
    fly_a, tok = _gather_start(a_bufs, name="gather_a_start")
    b_bufs = [_own_slot([(ssd_out_w, 0)], chip1, name="slot_ssd_out", after=(tok,)),
              _own_slot([(mlp_up, 0), (mlp_down, 0)], chip1, name="slot_mlp0_", after=(tok,))]
    fly_b, tok = _gather_start(b_bufs, name="gather_b_start", after=(tok,))
    c_bufs = [_own_slot([(sc_in_w, 0)], chip1, name="slot_sc_in", after=(tok,)),
              _own_slot([(sc_out_w, 0)], chip1, name="slot_sc_out", after=(tok,))]
    fly_c, tok = _gather_start(c_bufs, name="gather_c_start", after=(tok,))
    d_bufs = [_own_slot([(mlp_up, 1), (mlp_down, 1)], chip1, name="slot_mlp1_", after=(tok,))]
    fly_d, tok = _gather_start(d_bufs, name="gather_d_start", after=(tok,))

    row = lambda v: v.reshape(1, -1)
    xs, tgt = x[0], loss_target[0]
    prm = jnp.pad(jnp.concatenate([ssd_dt_bias, ssd_A_log, ssd_D, jnp.zeros((5, NH), F32)], axis=0), ((0, 0), (0, LANES - NH)))
    mix_nw = [row(mix_norm_w[i]) for i in range(2)]
    mlp_nw = [row(mlp_norm_w[i]) for i in range(2)]
    a_bufs = _gather_wait_first(fly_a, name="gather_a_landed", after=(tok,))
    mod_all, w_ssd_in = _gather_wait_forward(_gather_forward(a_bufs, name="gather_a_pass"), name="gather_a_done")
    mod = lax.dynamic_index_in_dim(mod_all.reshape(N_CHIPS, 2, N_DEV, n_ada), dev, axis=2, keepdims=False)
    mod = jnp.moveaxis(mod, 0, 1).reshape(2, 6, D)
    mods = [[mod[i, j:j + 1] for j in range(6)] for i in range(2)]
    w_in_t = w_ssd_in.reshape(N_CHIPS * SSD_IN_SHARD, D)
    w_dt_t = jnp.pad(w_in_t[ZX:], ((0, LANES - NH), (0, 0)))
    scan = _ssd_fwd_scan(xs, mods[0][0:3], mix_nw[0], w_in_t, w_dt_t, ssd_conv, ssd_conv_b, prm, "ssd")

    def land(flight, tag, after):
        return _gather_forward(_gather_wait_first(flight, name=f"gather_{tag}_landed", after=(after,)), name=f"gather_{tag}_pass")

    passed, got = {"b": land(fly_b, "b", scan[4])}, {}

    def done(tag, after):
        got[tag] = _gather_wait_forward(passed[tag], name=f"gather_{tag}_done", after=(after,))
        return got[tag]

    x1, s_ssd = _ssd_fwd_out(xs, mods[0][0:3], scan, ssd_norm_w, lambda yn: done("b", yn)[0], "ssd")
    w_ssd_out, w_b = got["b"]
    x2, s_mlp0 = _mlp_fwd(x1, mods[0][3:6], mlp_nw[0], w_b, up_row, down_row, "mlp0",
                          midway=lambda a: passed.update(c=land(fly_c, "c", a)))
    w_sc_in, w_sc_out = done("c", x2)
    x3, s_sc = _sc_layer_fwd(x2, mods[1][0:3], mix_nw[1], w_sc_in, sc_conv, w_sc_out, 0, "sc",
                             midway=lambda proj: passed.update(d=land(fly_d, "d", proj)))
    (w_mlp1,) = done("d", x3)
    x4, s_mlp1 = _mlp_fwd(x3, mods[1][3:6], mlp_nw[1], w_mlp1, up_row, down_row, "mlp1")

    core = ci.reshape(1).astype(jnp.int32)
    chip_core = jnp.stack([chip, ci]).astype(jnp.int32)

    def reduce_swap(gbufs, tag, after=()):
        return _sibling_start(gbufs, name=tag + "_sibling_start", after=after)

    def reduce_send(flight, tag, after):
        gs, sib = _sibling_wait(flight, name=tag + "_sibling_landed", after=after)
        hs = [_add_sibling_half(g, s, core, name=f"{tag}_add_sibling{b}") for b, (g, s) in enumerate(zip(gs, sib))]
        return _owners_start(hs, name=tag + "_owners_start")

    def reduce_sum(flight, tag, after):
        hs, lands = _owners_wait(flight, name=tag + "_owners_landed", after=after)
        ts = [_add_chip_sums(h, o, chip_core, name=f"{tag}_add_chips{b}") for b, (h, o) in enumerate(zip(hs, lands))]
        return _result_start(ts, name=tag + "_result_start")

    def reduce_done(flight, tag, after=()):
        return [t.reshape(-1, t.shape[2]) for t in _result_wait(flight, name=tag + "_result_landed", after=after)]

    dx4, fsum, dy, gs = _final_loss(x4, row(final_norm_w), tgt, (mods[1][5], s_mlp1[3]), name="final_loss")
    dx3, g_mlp1, sum_mlp1, dy, gs = _mlp_bwd(dx4, dy, gs, s_mlp1, mods[1][3:6], mlp_nw[1], w_mlp1, None, up_row, down_row,
                                             (mods[1][2], s_sc[4]), "mlp1")
    dx2, g_sc_out, g_sc_in, sum_sc, sc_csum, dy, gs = _sc_layer_bwd(dx3, dy, gs, s_sc, mods[1][0:3], mix_nw[1], w_sc_in,
                                                                    sc_conv, w_sc_out, None, 0, (mods[0][5], s_mlp0[3]), "sc")
    dx1, g_b, sum_mlp0, dy, gsum_ssd = _mlp_bwd(dx2, dy, gs, s_mlp0, mods[0][3:6], mlp_nw[0], w_b, None, up_row, down_row,
                                                (mods[0][2], s_ssd[8]), "mlp0")
    dyn, g_ssd_out = _ssd_bwd_out(dy, s_ssd, w_ssd_out, "ssd")
    fly_1, tok = reduce_swap([g_mlp1, g_sc_out, g_sc_in, g_b, g_ssd_out], "rs1")
    dy, dzx, gnsum = _gnorm_bwd(s_ssd[5], s_ssd[2], ssd_norm_w + tok[0:1, 0:1], dyn, name="ssd_dgnorm")
    fly_1, tok = reduce_send(fly_1, "rs1", (dy,))
    grad_x, d_w_zx, d_w_dt, sum_ssd, csum, ssum = _ssd_bwd_rest(
        dx1, dy, dzx, gsum_ssd, s_ssd, mods[0][0:3], mix_nw[0], w_in_t, w_dt_t, ssd_conv, prm + tok[0:1, 0:1], "ssd")
    fly_1, tok = reduce_sum(fly_1, "rs1", (grad_x,))

    def ssd_in_owner(k):
        lo, hi = k * SSD_IN_SHARD, (k + 1) * SSD_IN_SHARD
        if hi <= ZX:
            return d_w_zx[:, lo:hi]
        return jnp.concatenate([d_w_zx[:, lo:], d_w_dt[:, :hi - ZX]], axis=1)

    small = jnp.concatenate([sum_ssd + tok[0:1, 0:1], sum_mlp0, sum_sc, sum_mlp1, csum.reshape(24, D)[0:16], gnsum.reshape(16, D)[0:8],
                             fsum, sc_csum, jnp.pad(ssum, ((0, 0), (0, D - LANES)))], axis=0)
    small_all = _all_gather_rows(small, name="gather_small").reshape(N_DEV, SMALL_ROWS, D)
    fly_2, tok = reduce_swap([jnp.stack([ssd_in_owner(k) for k in range(N_CHIPS)]).astype(BF16)], "rs2", (small_all,))
    fly_2, tok = reduce_send(fly_2, "rs2", (tok,))
    t_mlp1, t_sc_out, t_sc_in, t_b, t_ssd_out = reduce_done(fly_1, "rs1", (tok,))
    small_all = small_all + tok[0:1, 0:1]
    tot = _sum_devices(small_all, name="sum_small")
    loss = tot[FINAL_ROW + 1, 0]
    mod_rows = [r + o for r in SUB_ROW for o in (3, 2, 0)]
    g_ada_b = jnp.stack([tot[r] for r in mod_rows]).reshape(2, 6 * D)
    g_mix_norm = jnp.stack([tot[SUB_ROW[0] + 1], tot[SUB_ROW[2] + 1]])
    g_mlp_norm = jnp.stack([tot[SUB_ROW[1] + 1], tot[SUB_ROW[3] + 1]])
    conv_sums = tot[SSD_CONV_ROW:SSD_CONV_ROW + 15].reshape(5, CONVD)
    g_ssd_conv_w = lax.dynamic_slice(conv_sums, (0, chip * 768), (4, 768))[None]
    g_ssd_conv_b = conv_sums[4:5]
    g_ssd_norm = tot[GNORM_ROW:GNORM_ROW + 2].reshape(1, DI)
    g_final = tot[FINAL_ROW]
    g_sc_conv_w = lax.dynamic_slice(tot[SC_CONV_ROW:SC_CONV_ROW + 3], (0, chip * 256), (3, 256))[None]
    g_a_log, g_d, g_dt_bias = (tot[HEAD_ROW + r:HEAD_ROW + r + 1, 0:NH] for r in range(3))
    c_pad = jnp.concatenate([c_all, jnp.zeros((8, D), F32)], axis=0)
    dmod_all = jnp.stack([small_all[:, r] for r in mod_rows], axis=1).reshape(N_DEV, 2, 6 * D)
    g_ada_w = []
    for i in range(2):
        dm = lax.dynamic_slice(dmod_all[:, i], (0, chip * n_ada), (N_DEV, n_ada))
        g_ada_w.append(_matmul_tn(c_pad, jnp.concatenate([dm, jnp.zeros_like(dm)], axis=0), m=D, n=n_ada, a_silu=True,
                                  name=f"ada_dw{i}"))

    big = dict(ada_w=[(g, 0) for g in g_ada_w], mlp_up=[(t_b, up_row), (t_mlp1, up_row)],
               mlp_down=[(t_b, down_row), (t_mlp1, down_row)], ssd_out_w=[(t_ssd_out, 0)], sc_out_w=[(t_sc_out, 0)],
               sc_in_w=[(t_sc_in, 0)], ssd_in_w=None)
    grads = dict(ada_b=g_ada_b, mix_norm_w=g_mix_norm, mlp_norm_w=g_mlp_norm, ssd_conv_w=g_ssd_conv_w,
                 ssd_conv_b=g_ssd_conv_b, ssd_dt_bias=g_dt_bias, ssd_A_log=g_a_log, ssd_D=g_d, ssd_norm_w=g_ssd_norm,
                 sc_conv_w=g_sc_conv_w, final_norm_w=g_final)
    weights = dict(ada_w=(ada_w, m_ada_w, v_ada_w), ada_b=(ada_b, m_ada_b, v_ada_b),
                   mix_norm_w=(mix_norm_w, m_mix_norm_w, v_mix_norm_w), mlp_norm_w=(mlp_norm_w, m_mlp_norm_w, v_mlp_norm_w),
                   mlp_up=(mlp_up, m_mlp_up, v_mlp_up), mlp_down=(mlp_down, m_mlp_down, v_mlp_down),
                   ssd_in_w=(ssd_in_w, m_ssd_in_w, v_ssd_in_w), ssd_conv_w=(ssd_conv_w, m_ssd_conv_w, v_ssd_conv_w),
                   ssd_conv_b=(ssd_conv_b, m_ssd_conv_b, v_ssd_conv_b), ssd_dt_bias=(ssd_dt_bias, m_ssd_dt_bias, v_ssd_dt_bias),
                   ssd_A_log=(ssd_A_log, m_ssd_A_log, v_ssd_A_log), ssd_D=(ssd_D, m_ssd_D, v_ssd_D),
                   ssd_norm_w=(ssd_norm_w, m_ssd_norm_w, v_ssd_norm_w), ssd_out_w=(ssd_out_w, m_ssd_out_w, v_ssd_out_w),
                   sc_in_w=(sc_in_w, m_sc_in_w, v_sc_in_w), sc_conv_w=(sc_conv_w, m_sc_conv_w, v_sc_conv_w),
                   sc_out_w=(sc_out_w, m_sc_out_w, v_sc_out_w), final_norm_w=(final_norm_w, m_final_norm_w, v_final_norm_w))
    def step(nm, parts):
        w, m, v = (t if t.shape[0] == 1 else t.reshape(-1, t.shape[-1]) for t in weights[nm])
        rows, outs = w.shape[-2] // len(parts), None
        for i, (gbuf, g_row) in enumerate(parts):
            outs = _adamw(w, gbuf, m, v, g_row=g_row, w_row=i * rows, rows=rows, into=outs, emit_g=True, name=f"adamw_{nm}{i}")
        return outs

    res = {}
    for nm, (w, m, v) in weights.items():
        two_d = (-1, w.shape[-1]) if w.ndim > 1 else (1, -1)
        if nm not in big:
            res[nm] = (grads[nm], *_adamw(w.reshape(two_d), grads[nm].reshape(two_d), m.reshape(two_d), v.reshape(two_d),
                                          name="adamw_" + nm))
        elif big[nm] is not None:
            res[nm] = step(nm, big[nm])
    fly_2, tok = reduce_sum(fly_2, "rs2", tuple(r[1] for r in res.values()))
    (t_ssd_in,) = reduce_done(fly_2, "rs2", (tok,))
    w_t, m_t, v_t = (jnp.swapaxes(t[0], 0, 1) for t in weights["ssd_in_w"])
    res["ssd_in_w"] = [jnp.swapaxes(o, 0, 1) for o in _adamw(w_t, t_ssd_in.T, m_t, v_t, emit_g=True, name="adamw_ssd_in_w")]
    outs = [[res[nm][k].reshape(weights[nm][0].shape) for nm in weights] for k in range(4)]
    return (loss, grad_x[None], *outs[0], *outs[1], *outs[2], *outs[3])
```

```python
import jax
import jax.numpy as jnp
from jax import lax
from jax.experimental import pallas as pl
from jax.experimental.pallas import tpu as pltpu

F32 = jnp.float32
BF16 = jnp.bfloat16
MESH = pl.DeviceIdType.MESH

D = 1024
DFF = 4096
DI = 2048
NH = 32
HP = 64
NG = 4
NS = 128
CH = 128
CONVD = DI + 2 * NG * NS
ZX = DI + CONVD
GW = NG * NS
LANES = 128
N_CHIPS = 4
N_DEV = 8
EPS = 1e-5
ADAM_LR, ADAM_B1, ADAM_B2, ADAM_EPS, ADAM_WD, ADAM_STEP = 1e-3, 0.9, 0.999, 1e-8, 0.01, 10
VMEM_LIMIT = 48 * 1024 * 1024
TM_ALL = 2048
TM_HALF = 1024
ANY = pl.BlockSpec(memory_space=pl.ANY)
SEM = pl.BlockSpec(memory_space=pltpu.SEMAPHORE)

SSD_IN_SHARD = 1288
SC_IN_SHARD = 768


def _params(sem=None):
    return pltpu.CompilerParams(dimension_semantics=sem, vmem_limit_bytes=VMEM_LIMIT)


def _sigmoid(v):
    return 0.5 * jnp.tanh(0.5 * v) + 0.5


def _dot(a, b, dims=((1,), (0,)), precision=None):
    return lax.dot_general(a, b, (dims, ((), ())), preferred_element_type=F32, precision=precision)


def _dot_nt(a, b):
    return _dot(a, b, ((1,), (1,)))


def _dot_tn(a, b):
    return _dot(a, b, ((0,), (0,)))


def _nn(av, bv):
    return _dot(av.astype(BF16), bv.astype(BF16))


def _nt(av, bv):
    return _dot_nt(av.astype(BF16), bv.astype(BF16))


def _nn_split(av, bv):
    return _dot(av.astype(BF16), bv.reshape(-1, bv.shape[2]))


def _nn_split_sq(av, bv):
    return _nn_split(av * av, bv)


def _nt_split(av, bv):
    kc = bv.shape[2]
    acc = _dot_nt(av[:, 0:kc].astype(BF16), bv[0])
    for s in range(1, bv.shape[0]):
        acc = acc + _dot_nt(av[:, s * kc:(s + 1) * kc].astype(BF16), bv[s])
    return acc


def _nt_sc_in(av, bv):
    q = 256
    acc = None
    for i in range(3 * D // q):
        a_blk = av[i // 4][:, (i % 4) * q:(i % 4 + 1) * q]
        b_blk = bv[i // 3][:, (i % 3) * q:(i % 3 + 1) * q]
        t = _dot_nt(a_blk, b_blk)
        acc = t if acc is None else acc + t
    return acc


def _matmul(a, b, *, name, n, contract=_nn, a_spec=None, b_spec=None, tm=512, tn=512, extras=(), epi=None,
            out_dtypes=(F32,), a_silu=False):
    M = a.shape[-2]
    tm, tn = min(tm, M), min(tn, n)
    assert M % tm == 0 and n % tn == 0, (name, M, n, tm, tn)
    n_ex = len(extras)
    if a_spec is None:
        a_spec = pl.BlockSpec((tm, a.shape[1]), lambda i, j: (i, 0))
    if b_spec is None:
        b_spec = (pl.BlockSpec((tn, b.shape[1]), lambda i, j: (j, 0)) if contract is _nt
                  else pl.BlockSpec((b.shape[0], tn), lambda i, j: (0, j)))

    def body(*refs):
        av = refs[0][...]
        if a_silu:
            av = av * _sigmoid(av)
        acc = contract(av, refs[1][...])
        res = epi(acc, *[r[...] for r in refs[2:2 + n_ex]]) if epi is not None else (acc,)
        for o_ref, r in zip(refs[2 + n_ex:], res, strict=True):
            o_ref[...] = r.astype(o_ref.dtype)

    in_specs = [a_spec, b_spec]
    for e in extras:
        in_specs.append(pl.BlockSpec((1, tn), lambda i, j: (0, j)) if e.shape[0] == 1 and M != 1
                        else pl.BlockSpec((tm, tn), lambda i, j: (i, j)))
    outs = pl.pallas_call(
        body, grid=(M // tm, n // tn), in_specs=in_specs,
        out_specs=[pl.BlockSpec((tm, tn), lambda i, j: (i, j)) for _ in out_dtypes],
        out_shape=[jax.ShapeDtypeStruct((M, n), dt) for dt in out_dtypes],
        compiler_params=_params(("parallel", "parallel")), name=name)(a, b, *extras)
    return outs if len(out_dtypes) > 1 else outs[0]


def _matmul_tn(a, b, *, name, m, n, tm=512, tn=512, a_spec=None, b_spec=None, out_spec=None, out_struct=None, into=None,
               a_silu=False, a_square=False):
    T = a.shape[-2]
    tm, tn = min(tm, m), min(tn, n)
    assert m % tm == 0 and n % tn == 0, (name, m, n, tm, tn)
    if a_spec is None:
        a_spec = pl.BlockSpec((T, tm), lambda i, j: (0, i))
    if b_spec is None:
        b_spec = pl.BlockSpec((T, tn), lambda i, j: (0, j))
    if out_spec is None:
        out_spec, out_struct = pl.BlockSpec((tm, tn), lambda i, j: (i, j)), jax.ShapeDtypeStruct((m, n), F32)

    def body(a_ref, b_ref, *rest):
        av = a_ref[...]
        if a_silu:
            av = av * _sigmoid(av)
        if a_square:
            av = av * av
        rest[-1][...] = _dot_tn(av.astype(BF16), b_ref[...].astype(BF16)).astype(rest[-1].dtype)

    args, in_specs, alias = [a, b], [a_spec, b_spec], {}
    if into is not None:
        args, in_specs, alias = args + [into], in_specs + [ANY], {2: 0}
    return pl.pallas_call(body, grid=(m // tm, n // tn), in_specs=in_specs, out_specs=out_spec, out_shape=out_struct,
                          input_output_aliases=alias, compiler_params=_params(("parallel", "parallel")), name=name)(*args)


def _modnorm_fwd(x, nw, sc, sh, *, name):
    L = x.shape[0]
    tm = min(L, 512)

    def body(x_ref, nw_ref, sc_ref, sh_ref, h_ref):
        xv = x_ref[...]
        r = lax.rsqrt(jnp.mean(xv * xv, axis=-1, keepdims=True) + EPS)
        h_ref[...] = ((xv * r * nw_ref[...]) * (1.0 + sc_ref[...]) + sh_ref[...]).astype(BF16)

    row = pl.BlockSpec((tm, D), lambda i: (i, 0))
    vec = pl.BlockSpec((1, D), lambda i: (0, 0))
    return pl.pallas_call(body, grid=(L // tm,), in_specs=[row, vec, vec, vec], out_specs=row,
                          out_shape=jax.ShapeDtypeStruct((L, D), BF16),
                          compiler_params=_params(("parallel",)), name=name)(x, nw, sc, sh)


def _gate_outputs(dx, below_refs, dy_ref, gs_ref):
    g_ref, y_ref = below_refs
    dy_ref[...] = (dx * g_ref[...]).astype(BF16)
    gs_ref[0:1, :] += jnp.sum(dx * y_ref[...].astype(F32), axis=0, keepdims=True)


def _modnorm_bwd(x, dh, dxo, nw, sc, gsum, below, *, name):
    L = x.shape[0]
    tm = min(L, 256)
    nb = 0 if below is None else 2

    def body(x_ref, dh_ref, dxo_ref, nw_ref, sc_ref, g_ref, *rest):
        dx_ref, s_ref = rest[nb:nb + 2]

        @pl.when(pl.program_id(0) == 0)
        def _():
            s_ref[...] = g_ref[...]
            if nb:
                rest[-1][...] = jnp.zeros_like(rest[-1])

        xv, dhv = x_ref[...], dh_ref[...].astype(F32)
        r = lax.rsqrt(jnp.mean(xv * xv, axis=-1, keepdims=True) + EPS)
        xhat = xv * r
        dxhat = dhv * (nw_ref[...] * (1.0 + sc_ref[...]))
        dx = dxo_ref[...] + r * (dxhat - xhat * jnp.mean(dxhat * xhat, axis=-1, keepdims=True))
        dx_ref[...] = dx
        s_ref[1:2, :] += jnp.sum(dhv * xhat, axis=0, keepdims=True) * (1.0 + sc_ref[...])
        s_ref[2:3, :] += jnp.sum(dhv * xhat, axis=0, keepdims=True) * nw_ref[...]
        s_ref[3:4, :] += jnp.sum(dhv, axis=0, keepdims=True)
        if nb:
            _gate_outputs(dx, rest[:nb], rest[-2], rest[-1])

    row = pl.BlockSpec((tm, D), lambda i: (i, 0))
    vec = pl.BlockSpec((1, D), lambda i: (0, 0))
    blk = pl.BlockSpec((8, D), lambda i: (0, 0))
    in_specs, out_specs = [row, row, row, vec, vec, blk], [row, blk]
    out_shape = [jax.ShapeDtypeStruct((L, D), F32), jax.ShapeDtypeStruct((8, D), F32)]
    if nb:
        in_specs, out_specs = in_specs + [vec, row], out_specs + [row, blk]
        out_shape += [jax.ShapeDtypeStruct((L, D), BF16), jax.ShapeDtypeStruct((8, D), F32)]
    return pl.pallas_call(body, grid=(L // tm,), in_specs=in_specs, out_specs=out_specs, out_shape=out_shape,
                          compiler_params=_params(("arbitrary",)), name=name)(x, dh, dxo, nw, sc, gsum, *(below or ()))


def _final_loss(x, fw, tgt, below, *, name):
    L = x.shape[0]
    tm = min(L, 256)

    def body(x_ref, fw_ref, t_ref, g_ref, y_ref, dx_ref, s_ref, dy_ref, gs_ref):
        @pl.when(pl.program_id(0) == 0)
        def _():
            s_ref[...] = jnp.zeros_like(s_ref)
            gs_ref[...] = jnp.zeros_like(gs_ref)

        xv = x_ref[...]
        r = lax.rsqrt(jnp.mean(xv * xv, axis=-1, keepdims=True) + EPS)
        xhat = xv * r
        diff = xhat * fw_ref[...] - t_ref[...]
        dout = diff * (1.0 / D)
        dxhat = dout * fw_ref[...]
        dx = r * (dxhat - xhat * jnp.mean(dxhat * xhat, axis=-1, keepdims=True))
        dx_ref[...] = dx
        s_ref[0:1, :] += jnp.sum(dout * xhat, axis=0, keepdims=True)
        s_ref[1:2, :] += jnp.zeros((1, D), F32) + 0.5 * jnp.sum(jnp.sum(diff * diff, axis=-1, keepdims=True) * (1.0 / D))
        _gate_outputs(dx, (g_ref, y_ref), dy_ref, gs_ref)

    row = pl.BlockSpec((tm, D), lambda i: (i, 0))
    vec = pl.BlockSpec((1, D), lambda i: (0, 0))
    blk = pl.BlockSpec((8, D), lambda i: (0, 0))
    return pl.pallas_call(body, grid=(L // tm,), in_specs=[row, vec, row, vec, row], out_specs=[row, blk, row, blk],
                          out_shape=[jax.ShapeDtypeStruct((L, D), F32), jax.ShapeDtypeStruct((8, D), F32),
                                     jax.ShapeDtypeStruct((L, D), BF16), jax.ShapeDtypeStruct((8, D), F32)],
                          compiler_params=_params(("arbitrary",)), name=name)(x, fw, tgt, *below)


def _shift_down(v, j):
    if j == 0:
        return v
    rolled = pltpu.roll(v, j, 0)
    row = lax.broadcasted_iota(jnp.int32, (8, v.shape[1]), 0)
    return jnp.concatenate([jnp.where(row >= j, rolled[0:8], 0.0), rolled[8:]], axis=0)


def _shift_up(v, j):
    if j == 0:
        return v
    n = v.shape[0]
    rolled = pltpu.roll(v, n - j, 0)
    row = lax.broadcasted_iota(jnp.int32, (8, v.shape[1]), 0)
    return jnp.concatenate([rolled[:n - 8], jnp.where(row < 8 - j, rolled[n - 8:], 0.0)], axis=0)


def _ssd_conv_fwd(zx, w, b, *, name):
    L = zx.shape[0]
    cb = 256
    k = w.shape[0]

    def body(x_ref, w_ref, b_ref, o_ref, p_ref):
        xv = x_ref[...].astype(F32)
        pre = b_ref[...] + xv * w_ref[k - 1:k, :]
        for j in range(1, k):
            pre = pre + _shift_down(xv, j) * w_ref[k - 1 - j:k - j, :]
        o_ref[...] = (pre * _sigmoid(pre)).astype(BF16)
        p_ref[...] = pre.astype(BF16)

    blk = pl.BlockSpec((L, cb), lambda i: (0, i))
    return pl.pallas_call(
        body, grid=(CONVD // cb,),
        in_specs=[pl.BlockSpec((L, cb), lambda i: (0, i + DI // cb)), pl.BlockSpec((k, cb), lambda i: (0, i)),
                  pl.BlockSpec((1, cb), lambda i: (0, i))],
        out_specs=[blk, blk], out_shape=[jax.ShapeDtypeStruct((L, CONVD), BF16)] * 2,
        compiler_params=_params(("parallel",)), name=name)(zx, w, b)


def _ssd_conv_bwd(zx, pre, dact, w, dzx, *, name):
    L = zx.shape[0]
    cb = 256
    k = w.shape[0]

    def body(x_ref, p_ref, da_ref, w_ref, _, dx_ref, s_ref):
        xv, pv = x_ref[...].astype(F32), p_ref[...].astype(F32)
        s = _sigmoid(pv)
        dpre = da_ref[...].astype(F32) * (s * (1.0 + pv * (1.0 - s)))
        s_ref[...] = jnp.zeros_like(s_ref)
        s_ref[k:k + 1, :] = jnp.sum(dpre, axis=0, keepdims=True)
        s_ref[k - 1:k, :] = jnp.sum(dpre * xv, axis=0, keepdims=True)
        dx = dpre * w_ref[k - 1:k, :]
        for j in range(1, k):
            later = _shift_up(dpre, j)
            dx = dx + later * w_ref[k - 1 - j:k - j, :]
            s_ref[k - 1 - j:k - j, :] = jnp.sum(later * xv, axis=0, keepdims=True)
        dx_ref[...] = dx.astype(BF16)

    blk = pl.BlockSpec((L, cb), lambda i: (0, i))
    return pl.pallas_call(
        body, grid=(CONVD // cb,),
        in_specs=[pl.BlockSpec((L, cb), lambda i: (0, i + DI // cb)), blk, blk, pl.BlockSpec((k, cb), lambda i: (0, i)), ANY],
        out_specs=[pl.BlockSpec((L, cb), lambda i: (0, i + DI // cb)), pl.BlockSpec((8, cb), lambda i: (0, i))],
        out_shape=[jax.ShapeDtypeStruct((L, ZX), BF16), jax.ShapeDtypeStruct((8, CONVD), F32)],
        input_output_aliases={4: 0}, compiler_params=_params(("parallel",)), name=name)(zx, pre, dact, w, dzx)


def _sc_fwd(proj, w, *, name):
    L = proj.shape[0]
    cb = 256
    nb = D // cb
    k = w.shape[0]

    def body(b_ref, c_ref, x_ref, w_ref, o_ref, v_ref):
        u = c_ref[...].astype(F32) * x_ref[...].astype(F32)
        v = u * w_ref[k - 1:k, :]
        for j in range(1, k):
            v = v + _shift_down(u, j) * w_ref[k - 1 - j:k - j, :]
        o_ref[...] = (b_ref[...].astype(F32) * v).astype(BF16)
        v_ref[...] = v.astype(BF16)

    blk = pl.BlockSpec((L, cb), lambda i: (0, i))
    return pl.pallas_call(
        body, grid=(nb,),
        in_specs=[blk, pl.BlockSpec((L, cb), lambda i: (0, i + nb)), pl.BlockSpec((L, cb), lambda i: (0, i + 2 * nb)),
                  pl.BlockSpec((k, cb), lambda i: (0, i))],
        out_specs=[blk, blk], out_shape=[jax.ShapeDtypeStruct((L, D), BF16)] * 2,
        compiler_params=_params(("parallel",)), name=name)(proj, proj, proj, w)


def _sc_bwd(proj, v, dyv, w, *, name):
    L = proj.shape[0]
    cb = 256
    nb = D // cb
    k = w.shape[0]

    def body(b_ref, c_ref, x_ref, v_ref, dy_ref, w_ref, dp_ref, s_ref):
        cv, xv = c_ref[...].astype(F32), x_ref[...].astype(F32)
        u = cv * xv
        dyv_ = dy_ref[...].astype(F32)
        dp_ref[0] = (dyv_ * v_ref[...].astype(F32)).astype(BF16)
        dv = dyv_ * b_ref[...].astype(F32)
        s_ref[...] = jnp.zeros_like(s_ref)
        s_ref[k - 1:k, :] = jnp.sum(dv * u, axis=0, keepdims=True)
        du = dv * w_ref[k - 1:k, :]
        for j in range(1, k):
            later = _shift_up(dv, j)
            du = du + later * w_ref[k - 1 - j:k - j, :]
            s_ref[k - 1 - j:k - j, :] = jnp.sum(later * u, axis=0, keepdims=True)
        dp_ref[1] = (du * xv).astype(BF16)
        dp_ref[2] = (du * cv).astype(BF16)

    blk = pl.BlockSpec((L, cb), lambda i: (0, i))
    return pl.pallas_call(
        body, grid=(nb,),
        in_specs=[blk, pl.BlockSpec((L, cb), lambda i: (0, i + nb)), pl.BlockSpec((L, cb), lambda i: (0, i + 2 * nb)),
                  blk, blk, pl.BlockSpec((k, cb), lambda i: (0, i))],
        out_specs=[pl.BlockSpec((3, L, cb), lambda i: (0, 0, i)), pl.BlockSpec((8, cb), lambda i: (0, i))],
        out_shape=[jax.ShapeDtypeStruct((3, L, D), BF16), jax.ShapeDtypeStruct((8, D), F32)],
        compiler_params=_params(("parallel",)), name=name)(proj, proj, proj, v, dyv, w)


def _pieces(v, n):
    out, rest = [], v
    for _ in range(n):
        out.append(rest.astype(BF16))
        rest = rest - out[-1].astype(F32)
    return out


def _cumsum_rows(mask, v):
    m = mask.astype(BF16)
    return _dot(jnp.concatenate([m, m, m], axis=1), jnp.concatenate(_pieces(v, 3), axis=0))


def _ssd_chunk_terms(dtr, prm):
    lane = lax.broadcasted_iota(jnp.int32, (CH, LANES), 1)
    valid = lane < NH
    xdt = dtr + prm[0:1, :]
    dt = jnp.where(valid, jnp.maximum(xdt, 0.0) + jnp.log1p(jnp.exp(-jnp.abs(xdt))), 0.0)
    A = -jnp.exp(prm[1:2, :])
    ri = lax.broadcasted_iota(jnp.int32, (CH, CH), 0)
    ci = lax.broadcasted_iota(jnp.int32, (CH, CH), 1)
    cs = _cumsum_rows(ri >= ci, dt * A)
    last = cs[CH - 1:CH, :]
    spread = (lax.broadcasted_iota(jnp.int32, (2 * LANES, DI), 1) // HP
              == lax.broadcasted_iota(jnp.int32, (2 * LANES, DI), 0) % LANES).astype(BF16)
    gather = ((lax.broadcasted_iota(jnp.int32, (LANES, 2 * DI), 1) % DI) // HP
              == lax.broadcasted_iota(jnp.int32, (LANES, 2 * DI), 0)).astype(BF16)
    return dict(valid=valid, xdt=xdt, dt=dt, A=A, cs=cs, csT=cs.T, last=last, ri=ri, ci=ci, ex=(spread, gather))


def _expand(v, ex):
    if v.shape[0] == 1:
        return _expand(jnp.broadcast_to(v, (8, LANES)), ex)[0:1, :]
    return _dot(jnp.concatenate(_pieces(v, 2), axis=1), ex[0])


def _head_sum(v, ex):
    if v.shape[0] == 1:
        return _head_sum(jnp.broadcast_to(v, (8, DI)), ex)[0:1, :]
    return _dot_nt(jnp.concatenate(_pieces(v, 2), axis=1), ex[1])


def _ssd_fwd(xbc, dtr, prm, *, name):
    L = xbc.shape[0]
    nc = L // CH

    def body(xbc_ref, dtr_ref, prm_ref, y_ref, sp_ref, st_ref):
        @pl.when(pl.program_id(0) == 0)
        def _():
            st_ref[...] = jnp.zeros_like(st_ref)

        prm_v = prm_ref[...]
        t = _ssd_chunk_terms(dtr_ref[...], prm_v)
        cs, csT, ex, causal = t["cs"], t["csT"], t["ex"], t["ri"] >= t["ci"]
        xs = xbc_ref[:, 0:DI].astype(F32)
        X = xs * _expand(t["dt"], ex)
        Xb = X.astype(BF16)
        Xd = (X * _expand(jnp.exp(t["last"] - cs), ex)).astype(BF16)
        Ex = _expand(jnp.exp(cs), ex)
        cdx = _expand(jnp.exp(t["last"]), ex)
        dskx = _expand(prm_v[2:3, :], ex)
        lane = lax.broadcasted_iota(jnp.int32, (CH, LANES), 1)
        sp_ref[0] = st_ref[...]
        for g in range(NG):
            Bg = xbc_ref[:, DI + g * NS:DI + (g + 1) * NS].astype(BF16)
            Cg = xbc_ref[:, DI + GW + g * NS:DI + GW + (g + 1) * NS].astype(BF16)
            G = _dot_nt(Cg, Bg)
            Sg = st_ref[:, g * GW:(g + 1) * GW]
            yoff = _dot(Cg, Sg.astype(BF16)) * Ex[:, g * GW:(g + 1) * GW]
            for j in range(GW // LANES):
                lo = g * GW + j * LANES
                Xp = Xb[:, lo:lo + LANES]
                yd = []
                for h in (lo // HP, lo // HP + 1):
                    seg = cs[:, h:h + 1] - csT[h:h + 1, :]
                    yd.append(_dot((G * jnp.where(causal, jnp.exp(seg), 0.0)).astype(BF16), Xp))
                y_ref[:, lo:lo + LANES] = (jnp.where(lane < HP, yd[0], yd[1]) + yoff[:, j * LANES:(j + 1) * LANES]
                                           + dskx[:, lo:lo + LANES] * xs[:, lo:lo + LANES]).astype(BF16)
            st_ref[:, g * GW:(g + 1) * GW] = Sg * cdx[:, g * GW:(g + 1) * GW] + _dot_tn(Bg, Xd[:, g * GW:(g + 1) * GW])

    return pl.pallas_call(
        body, grid=(nc,),
        in_specs=[pl.BlockSpec((CH, CONVD), lambda c: (c, 0)), pl.BlockSpec((CH, LANES), lambda c: (c, 0)),
                  pl.BlockSpec((8, LANES), lambda c: (0, 0))],
        out_specs=[pl.BlockSpec((CH, DI), lambda c: (c, 0)), pl.BlockSpec((1, NS, DI), lambda c: (c, 0, 0))],
        out_shape=[jax.ShapeDtypeStruct((L, DI), BF16), jax.ShapeDtypeStruct((nc, NS, DI), F32)],
        scratch_shapes=[pltpu.VMEM((NS, DI), F32)],
        compiler_params=_params(("arbitrary",)), name=name)(xbc, dtr, prm)


def _ssd_bwd(xbc, dtr, prm, dy, sprev, *, name):
    L = xbc.shape[0]
    nc = L // CH

    def body(xbc_ref, dtr_ref, prm_ref, dy_ref, sp_ref, dxbc_ref, ddtr_ref, s_ref, dst_ref, dx_scr, de_scr, dd_scr):
        step = pl.program_id(0)

        @pl.when(step == 0)
        def _():
            dst_ref[...] = jnp.zeros_like(dst_ref)
            s_ref[...] = jnp.zeros_like(s_ref)

        prm_v = prm_ref[...]
        t = _ssd_chunk_terms(dtr_ref[...], prm_v)
        cs, csT, ex, ri, ci = t["cs"], t["csT"], t["ex"], t["ri"], t["ci"]
        E = jnp.exp(cs)
        dec = jnp.exp(t["last"] - cs)
        cd = jnp.exp(t["last"])
        xs = xbc_ref[:, 0:DI].astype(F32)
        dtx = _expand(t["dt"], ex)
        X = xs * dtx
        Xb = X.astype(BF16)
        decx = _expand(dec, ex)
        Xd = (X * decx).astype(BF16)
        Ex = _expand(E, ex)
        cdx = _expand(cd, ex)
        dskx = _expand(prm_v[2:3, :], ex)
        lane = lax.broadcasted_iota(jnp.int32, (CH, LANES), 1)
        dcs = jnp.zeros((CH, LANES), F32)
        dcd_x = []
        for g in range(NG):
            gs = slice(g * GW, (g + 1) * GW)
            Bg = xbc_ref[:, DI + g * NS:DI + (g + 1) * NS].astype(BF16)
            Cg = xbc_ref[:, DI + GW + g * NS:DI + GW + (g + 1) * NS].astype(BF16)
            G = _dot_nt(Cg, Bg)
            GT = _dot_nt(Bg, Cg)
            Sg = sp_ref[0, :, gs]
            Sgb = Sg.astype(BF16)
            dyg = dy_ref[:, gs]
            de_scr[:, gs] = dyg * _dot(Cg, Sgb)
            dYo = (Ex[:, gs] * dyg).astype(BF16)
            dC = _dot_nt(dYo, Sgb)
            dS_in = _dot_tn(Cg, dYo)
            dStg = dst_ref[:, gs]
            dStb = dStg.astype(BF16)
            dXd = _dot(Bg, dStb)
            dB = _dot_nt(Xd[:, gs], dStb)
            dd_scr[:, gs] = dXd * X[:, gs]
            dXst = dXd * decx[:, gs]
            dG = jnp.zeros((CH, CH), F32)
            dGT = jnp.zeros((CH, CH), F32)
            for j in range(GW // LANES):
                lo = g * GW + j * LANES
                Xp = Xb[:, lo:lo + LANES]
                dyp = dy_ref[:, lo:lo + LANES]
                dXp = dXst[:, j * LANES:(j + 1) * LANES]
                for k, h in enumerate((lo // HP, lo // HP + 1)):
                    dyh = jnp.where((lane < HP) if k == 0 else (lane >= HP), dyp, 0.0).astype(BF16)
                    seg = cs[:, h:h + 1] - csT[h:h + 1, :]
                    Lm = jnp.where(ri >= ci, jnp.exp(seg), 0.0)
                    LmT = jnp.where(ci >= ri, jnp.exp(-seg), 0.0)
                    dM = _dot_nt(dyh, Xp)
                    dMT = _dot_nt(Xp, dyh)
                    MT = GT * LmT
                    rs = jnp.sum(dM * (G * Lm), axis=1, keepdims=True) - jnp.sum(dMT * MT, axis=1, keepdims=True)
                    dcs = dcs + jnp.where(lane == h, rs, 0.0)
                    dG = dG + dM * Lm
                    dGT = dGT + dMT * LmT
                    dXp = dXp + _dot(MT.astype(BF16), dyh)
                dx_scr[:, lo:lo + LANES] = dXp
            dxbc_ref[:, DI + g * NS:DI + (g + 1) * NS] = (dB + _dot(dGT.astype(BF16), Cg)).astype(BF16)
            dxbc_ref[:, DI + GW + g * NS:DI + GW + (g + 1) * NS] = (dC + _dot(dG.astype(BF16), Bg)).astype(BF16)
            dcd_x.append(jnp.sum(dStg * Sg, axis=0, keepdims=True))
            dst_ref[:, gs] = dStg * cdx[:, gs] + dS_in
        dX = dx_scr[...]
        dy = dy_ref[...]
        ddec = _head_sum(dd_scr[...], ex)
        dcd = _head_sum(jnp.concatenate(dcd_x, axis=1), ex)
        dcs = dcs + _head_sum(de_scr[...], ex) * E - ddec * dec
        row = lax.broadcasted_iota(jnp.int32, (CH, LANES), 0)
        dcs = dcs + jnp.where(row == CH - 1, jnp.sum(ddec * dec, axis=0, keepdims=True) + dcd * cd, 0.0)
        da = _cumsum_rows(ci >= ri, dcs)
        ddt = da * t["A"] + _head_sum(dX * xs, ex)
        ddtr = jnp.where(t["valid"], ddt * _sigmoid(t["xdt"]), 0.0)
        ddtr_ref[...] = ddtr
        dxbc_ref[:, 0:DI] = (dX * dtx + dskx * dy).astype(BF16)
        s_ref[0:1, :] += jnp.sum(da * t["dt"], axis=0, keepdims=True)
        s_ref[1:2, :] += _head_sum(jnp.sum(dy * xs, axis=0, keepdims=True), ex)
        s_ref[2:3, :] += jnp.sum(ddtr, axis=0, keepdims=True)

        @pl.when(step == nc - 1)
        def _():
            s_ref[0:1, :] = s_ref[0:1, :] * t["A"]

    rev = lambda c: (nc - 1 - c, 0)
    return pl.pallas_call(
        body, grid=(nc,),
        in_specs=[pl.BlockSpec((CH, CONVD), rev), pl.BlockSpec((CH, LANES), rev), pl.BlockSpec((8, LANES), lambda c: (0, 0)),
                  pl.BlockSpec((CH, DI), rev), pl.BlockSpec((1, NS, DI), lambda c: (nc - 1 - c, 0, 0))],
        out_specs=[pl.BlockSpec((CH, CONVD), rev), pl.BlockSpec((CH, LANES), rev), pl.BlockSpec((8, LANES), lambda c: (0, 0))],
        out_shape=[jax.ShapeDtypeStruct((L, CONVD), BF16), jax.ShapeDtypeStruct((L, LANES), F32),
                   jax.ShapeDtypeStruct((8, LANES), F32)],
        scratch_shapes=[pltpu.VMEM((NS, DI), F32), pltpu.VMEM((CH, DI), F32), pltpu.VMEM((CH, DI), F32),
                        pltpu.VMEM((CH, DI), F32)],
        compiler_params=_params(("arbitrary",)), name=name)(xbc, dtr, prm, dy, sprev)


def _gnorm_fwd(y, zx, nw, *, name):
    L = y.shape[0]
    tm = min(L, 256)

    def body(y_ref, z_ref, nw_ref, o_ref):
        z = z_ref[...].astype(F32)
        yg = y_ref[...].astype(F32) * (z * _sigmoid(z))
        for g in range(NG):
            v = yg[:, g * GW:(g + 1) * GW]
            r = lax.rsqrt(jnp.mean(v * v, axis=-1, keepdims=True) + EPS)
            o_ref[:, g * GW:(g + 1) * GW] = (v * r * nw_ref[:, g * GW:(g + 1) * GW]).astype(BF16)

    row = pl.BlockSpec((tm, DI), lambda i: (i, 0))
    return pl.pallas_call(body, grid=(L // tm,), in_specs=[row, row, pl.BlockSpec((1, DI), lambda i: (0, 0))],
                          out_specs=row, out_shape=jax.ShapeDtypeStruct((L, DI), BF16),
                          compiler_params=_params(("parallel",)), name=name)(y, zx, nw)


def _gnorm_bwd(y, zx, nw, dyn, *, name):
    L = y.shape[0]
    tm = min(L, 256)

    def body(y_ref, z_ref, nw_ref, dyn_ref, dy_ref, dz_ref, s_ref):
        @pl.when(pl.program_id(0) == 0)
        def _():
            s_ref[...] = jnp.zeros_like(s_ref)

        z, yv = z_ref[...].astype(F32), y_ref[...].astype(F32)
        sz = _sigmoid(z)
        gate = z * sz
        dgate_dz = sz * (1.0 + z * (1.0 - sz))
        for g in range(NG):
            gs = slice(g * GW, (g + 1) * GW)
            v = yv[:, gs] * gate[:, gs]
            r = lax.rsqrt(jnp.mean(v * v, axis=-1, keepdims=True) + EPS)
            vhat = v * r
            dn = dyn_ref[:, gs].astype(F32)
            s_ref[0:1, gs] += jnp.sum(dn * vhat, axis=0, keepdims=True)
            dvhat = dn * nw_ref[:, gs]
            dv = r * (dvhat - vhat * jnp.mean(dvhat * vhat, axis=-1, keepdims=True))
            dy_ref[:, gs] = dv * gate[:, gs]
            dz_ref[:, gs] = (dv * yv[:, gs] * dgate_dz[:, gs]).astype(BF16)

    row = pl.BlockSpec((tm, DI), lambda i: (i, 0))
    return pl.pallas_call(body, grid=(L // tm,), in_specs=[row, row, pl.BlockSpec((1, DI), lambda i: (0, 0)), row],
                          out_specs=[row, row, pl.BlockSpec((8, DI), lambda i: (0, 0))],
                          out_shape=[jax.ShapeDtypeStruct((L, DI), F32), jax.ShapeDtypeStruct((L, ZX), BF16),
                                     jax.ShapeDtypeStruct((8, DI), F32)],
                          compiler_params=_params(("arbitrary",)), name=name)(y, zx, nw, dyn)


def _adamw(w, g, m, v, *, name, g_row=0, w_row=0, rows=None, into=None, emit_g=False):
    lead = w.ndim == 3
    R, C = w.shape[-2:]
    rows = R if rows is None else rows
    tr = max([t for t in range(8, rows + 1, 8) if rows % t == 0 and t * C <= 256 * 1024], default=rows)
    assert g_row % tr == 0 and w_row % tr == 0, (name, g_row, w_row, tr)
    n_out = 4 if emit_g else 3

    def body(w_ref, g_ref, m_ref, v_ref, *rest):
        outs = rest[-n_out:]
        gv = g_ref[...]
        mn = ADAM_B1 * m_ref[...] + (1.0 - ADAM_B1) * gv
        vn = ADAM_B2 * v_ref[...] + (1.0 - ADAM_B2) * (gv * gv)
        m_hat = mn / (1.0 - ADAM_B1 ** ADAM_STEP)
        v_hat = vn / (1.0 - ADAM_B2 ** ADAM_STEP)
        d_ref, mo_ref, vo_ref = outs[-3:]
        d_ref[...] = -ADAM_LR * (m_hat / (jnp.sqrt(v_hat) + ADAM_EPS) + ADAM_WD * w_ref[...])
        mo_ref[...] = mn
        vo_ref[...] = vn
        if emit_g:
            outs[0][...] = gv

    blk = (pl.BlockSpec((None, tr, C), lambda i: (0, i + w_row // tr, 0)) if lead
           else pl.BlockSpec((tr, C), lambda i: (i + w_row // tr, 0)))
    args, in_specs, alias = [w, g, m, v], [blk, pl.BlockSpec((tr, C), lambda i: (i + g_row // tr, 0)), blk, blk], {}
    if into is not None:
        args, in_specs, alias = args + list(into), in_specs + [ANY] * n_out, {4 + k: k for k in range(n_out)}
    return pl.pallas_call(body, grid=(rows // tr,), in_specs=in_specs, out_specs=[blk] * n_out,
                          out_shape=[jax.ShapeDtypeStruct(w.shape, F32)] * n_out, input_output_aliases=alias,
                          compiler_params=_params(("parallel",)), name=name)(*args)


def _residual(acc, xv, gv):
    return xv + gv * acc, acc


def _like(buf):
    return jax.ShapeDtypeStruct(buf.shape, buf.dtype)


def _mlp_fwd(x, mod, nw, wb, up_row, down_row, tag, midway=None):
    sh, sc, g = mod
    h = _modnorm_fwd(x, nw, sc, sh, name=tag + "_norm")
    a = _matmul(h, wb, n=DFF, tm=TM_ALL, b_spec=pl.BlockSpec((None, D, 512), lambda mi, j: (j // 2, up_row // D, j % 2)),
                epi=lambda acc: (jnp.maximum(acc, 0.0),), out_dtypes=(BF16,), name=tag + "_up")
    if midway is not None:
        midway(a)
    xn, y = _matmul(a, wb, n=D, tm=TM_HALF, contract=_nn_split_sq,
                    b_spec=pl.BlockSpec((N_CHIPS, D, 512), lambda mi, j: (0, down_row // D, j)),
                    extras=(x, g), epi=_residual, out_dtypes=(F32, BF16), name=tag + "_down")
    return xn, (x, h, a, y)


def _mlp_bwd(dxo, dy, gsum, saved, mod, nw, wb, gb, up_row, down_row, below, tag):
    x, h, a, y = saved
    sh, sc, g = mod
    du = _matmul(dy, wb, n=DFF, tm=TM_ALL, contract=_nt,
                 b_spec=pl.BlockSpec((None, 512, D), lambda mi, j: (j // 2, down_row // 512 + j % 2, 0)),
                 extras=(a,), epi=lambda acc, av: (acc * (2.0 * av.astype(F32)),), out_dtypes=(BF16,), name=tag + "_dact")
    gb = _matmul_tn(a, dy, m=DFF, n=D, tm=D, tn=D, a_square=True, into=gb, out_struct=_like(wb),
                    out_spec=pl.BlockSpec((None, D, D), lambda mi, j: (mi, down_row // D, 0)), name=tag + "_ddown")
    dh = _matmul(du, wb, n=D, tm=TM_HALF, contract=_nt_split,
                 b_spec=pl.BlockSpec((N_CHIPS, 512, D), lambda mi, j: (0, up_row // 512 + j, 0)), out_dtypes=(BF16,),
                 name=tag + "_dh")
    gb = _matmul_tn(h, du, m=D, n=DFF, tm=D, into=gb, out_struct=_like(wb),
                    out_spec=pl.BlockSpec((None, D, 512), lambda mi, j: (j // 2, up_row // D, j % 2)), name=tag + "_dup")
    dx, sums, *nxt = _modnorm_bwd(x, dh, dxo, nw, sc, gsum, below, name=tag + "_dnorm")
    return dx, gb, sums, *nxt


def _ssd_fwd_scan(x, mod, nw, w_in_t, w_dt_t, conv_w, conv_b, prm, tag):
    sh, sc, g = mod
    h = _modnorm_fwd(x, nw, sc, sh, name=tag + "_norm")
    zx = _matmul(h, w_in_t, n=ZX, tm=TM_ALL, contract=_nt, out_dtypes=(BF16,), name=tag + "_in")
    dtr = _matmul(h, w_dt_t, n=LANES, tm=TM_ALL, contract=_nt, name=tag + "_in_dt")
    xbc, pre = _ssd_conv_fwd(zx, conv_w, conv_b, name=tag + "_conv")
    y, sprev = _ssd_fwd(xbc, dtr, prm, name=tag + "_scan")
    return h, zx, dtr, xbc, y, sprev, pre


def _ssd_fwd_out(x, mod, scan, gn_w, get_w_out, tag):
    sh, sc, g = mod
    h, zx, dtr, xbc, y, sprev, pre = scan
    yn = _gnorm_fwd(y, zx, gn_w, name=tag + "_gnorm")
    w_out = get_w_out(yn)
    xn, yo = _matmul(yn, w_out, n=D, tm=TM_HALF, contract=_nn_split,
                     b_spec=pl.BlockSpec((N_CHIPS, 512, 512), lambda mi, j: (0, 0, j)),
                     extras=(x, g), epi=_residual, out_dtypes=(F32, BF16), name=tag + "_out")
    return xn, (x, h, zx, dtr, xbc, y, sprev, yn, yo, pre)


def _ssd_bwd_out(dyo, saved, w_out, tag):
    x, h, zx, dtr, xbc, y, sprev, yn, yo, pre = saved
    dyn = _matmul(dyo, w_out, n=DI, tm=TM_ALL, contract=_nt, b_spec=pl.BlockSpec((None, 512, D), lambda mi, j: (j, 0, 0)),
                  out_dtypes=(BF16,), name=tag + "_dyn")
    g_out = _matmul_tn(yn, dyo, m=DI, n=D, tn=D, out_struct=_like(w_out),
                       out_spec=pl.BlockSpec((None, 512, D), lambda mi, j: (mi, 0, 0)), name=tag + "_dout")
    return dyn, g_out


def _ssd_bwd_rest(dxo, dy, dzx, gsum, saved, mod, nw, w_in_t, w_dt_t, conv_w, prm, tag):
    x, h, zx, dtr, xbc, y, sprev, yn, yo, pre = saved
    sh, sc, g = mod
    dxbc, ddtr, ssum = _ssd_bwd(xbc, dtr, prm, dy, sprev, name=tag + "_dscan")
    dzx, csum = _ssd_conv_bwd(zx, pre, dxbc, conv_w, dzx, name=tag + "_dconv")
    dh_dt = _matmul(ddtr, w_dt_t, n=D, tm=TM_ALL, name=tag + "_dh_dt")
    dh = _matmul(dzx, w_in_t, n=D, tm=TM_HALF, b_spec=pl.BlockSpec((ZX, 512), lambda mi, j: (0, j)), extras=(dh_dt,),
                 epi=lambda acc, e: (acc + e,), out_dtypes=(BF16,), name=tag + "_dh")
    d_w_zx = _matmul_tn(h, dzx, m=D, n=ZX, tm=D, name=tag + "_din")
    d_w_dt = _matmul_tn(h, ddtr, m=D, n=LANES, tm=D, name=tag + "_din_dt")
    dx, sums = _modnorm_bwd(x, dh, dxo, nw, sc, gsum, None, name=tag + "_dnorm")
    return dx, d_w_zx, d_w_dt, sums, csum, ssum


def _sc_layer_fwd(x, mod, nw, w_sc_in, conv_w, wb, out_row, tag, midway=None):
    sh, sc, g = mod
    h = _modnorm_fwd(x, nw, sc, sh, name=tag + "_norm")
    proj = _matmul(h, w_sc_in, n=3 * D, tm=TM_ALL, tn=256, out_dtypes=(BF16,),
                   b_spec=pl.BlockSpec((None, D, 256), lambda mi, j: (j // 3, 0, j % 3)),
                   name=tag + "_in")
    if midway is not None:
        midway(proj)
    yv, v = _sc_fwd(proj, conv_w, name=tag + "_conv")
    xn, yo = _matmul(yv, wb, n=D, tm=TM_HALF, contract=_nn_split,
                     b_spec=pl.BlockSpec((N_CHIPS, 256, 512), lambda mi, j: (0, out_row // 256, j)),
                     extras=(x, g), epi=_residual, out_dtypes=(F32, BF16), name=tag + "_out")
    return xn, (x, h, proj, yv, yo, v)


def _sc_layer_bwd(dxo, dyo, gsum, saved, mod, nw, w_sc_in, conv_w, wb, gb, out_row, below, tag):
    x, h, proj, yv, yo, v = saved
    sh, sc, g = mod
    L = x.shape[0]
    dyv = _matmul(dyo, wb, n=D, tm=TM_ALL, tn=256, contract=_nt,
                  b_spec=pl.BlockSpec((None, 256, D), lambda mi, j: (j, out_row // 256, 0)), out_dtypes=(BF16,),
                  name=tag + "_dyv")
    gb = _matmul_tn(yv, dyo, m=D, n=D, tm=256, tn=D, into=gb, out_struct=_like(wb),
                    out_spec=pl.BlockSpec((None, 256, D), lambda mi, j: (mi, out_row // 256, 0)), name=tag + "_dout")
    dproj, csum = _sc_bwd(proj, v, dyv, conv_w, name=tag + "_dconv")
    tm = min(L, TM_HALF)
    dh = _matmul(dproj, w_sc_in, n=D, tm=tm, contract=_nt_sc_in, a_spec=pl.BlockSpec((3, tm, D), lambda mi, j: (0, mi, 0)),
                 b_spec=pl.BlockSpec((N_CHIPS, 512, SC_IN_SHARD), lambda mi, j: (0, j, 0)), out_dtypes=(BF16,),
                 name=tag + "_dh")
    g_sc_in = _matmul_tn(h, dproj, m=D, n=3 * D, tm=D, tn=256, b_spec=pl.BlockSpec((None, L, 256), lambda mi, j: (j // 4, 0, j % 4)),
                         out_spec=pl.BlockSpec((None, D, 256), lambda mi, j: (j // 3, 0, j % 3)),
                         out_struct=jax.ShapeDtypeStruct((N_CHIPS, D, SC_IN_SHARD), BF16), name=tag + "_din")
    dx, sums, *nxt = _modnorm_bwd(x, dh, dxo, nw, sc, gsum, below, name=tag + "_dnorm")
    return dx, gb, g_sc_in, sums, csum, *nxt


SUB_ROW = (0, 8, 16, 24)
SSD_CONV_ROW, GNORM_ROW, FINAL_ROW, SC_CONV_ROW, HEAD_ROW, SMALL_ROWS = 32, 48, 56, 64, 72, 80


def _all_gather_rows(blk, *, name):
    m_per, n = blk.shape

    def body(x_ref, out_ref, send_sems, recv_sems, local_sem):
        x, y, c = lax.axis_index("x"), lax.axis_index("y"), lax.axis_index("c")
        me, sibling = (x, y, c), (x, y, 1 - c)
        chips = [(1 - x, y), (x, 1 - y), (1 - x, 1 - y)]

        def rows(px, py, pc):
            return out_ref.at[pl.ds((4 * px + 2 * py + pc) * m_per, m_per), :]

        def copy(k, block, to, src=None):
            return pltpu.make_async_remote_copy(src_ref=rows(*block) if src is None else src, dst_ref=rows(*block),
                                                send_sem=send_sems.at[k], recv_sem=recv_sems.at[k], device_id=to,
                                                device_id_type=MESH)

        mine = pltpu.make_async_copy(x_ref, rows(*me), local_sem)
        mine.start()
        first = [copy(0, me, sibling, src=x_ref)] + [copy(1 + j, me, (*chip, c), src=x_ref) for j, chip in enumerate(chips)]
        for cp in first:
            cp.start()
        passed = [copy(4 + j, (*chip, c), sibling) for j, chip in enumerate(chips)]
        for j, chip in enumerate(chips):
            copy(1 + j, (*chip, c), me).wait_recv()
            passed[j].start()
        copy(0, sibling, me).wait_recv()
        for j, chip in enumerate(chips):
            copy(4 + j, (*chip, 1 - c), me).wait_recv()
        for cp in first + passed:
            cp.wait_send()
        mine.wait()

    return pl.pallas_call(
        body, out_shape=jax.ShapeDtypeStruct((N_DEV * m_per, n), blk.dtype),
        in_specs=[pl.BlockSpec(memory_space=pltpu.VMEM)], out_specs=pl.BlockSpec(memory_space=pltpu.VMEM),
        scratch_shapes=[pltpu.SemaphoreType.DMA((7,)), pltpu.SemaphoreType.DMA((7,)), pltpu.SemaphoreType.DMA],
        name=name)(blk)


def _half(ref, chip, c):
    r, n = ref.shape[1:]
    if r % 32 == 0:
        return ref.at[chip, pl.ds(c * (r // 2), r // 2), :]
    assert n % 256 == 0, ref.shape
    return ref.at[chip, :, pl.ds(c * (n // 2), n // 2)]


def _gather_copy(bufs, sends, recvs, b, k, chip, pc, to):
    piece = _half(bufs[b], 2 * chip[0] + chip[1], pc)
    return pltpu.make_async_remote_copy(src_ref=piece, dst_ref=piece, send_sem=sends.at[4 * b + k], recv_sem=recvs.at[4 * b + k],
                                        device_id=to, device_id_type=MESH)


def _split_call(body, bufs, sems_in, n_sems, *, name, after=(), token=False, lands=()):
    nb, na, nl, starts = len(bufs), len(after), len(lands), not sems_in

    def wrapped(*refs):
        sems = refs[nb + na:nb + na + 2] if starts else refs[nb:nb + 2]
        made = refs[nb + na + 2 + nb:nb + na + 2 + nb + nl] if starts else ()
        body(tuple(refs[:nb]) + tuple(made), sems[0], sems[1])
        if token:
            refs[-1][...] = jnp.zeros_like(refs[-1])

    out_shape = [pltpu.SemaphoreType.DMA((n_sems,)) for _ in range(2 if starts else 0)]
    out_specs = [SEM] * len(out_shape) + [ANY] * (nb + nl)
    alias = {b: len(out_shape) + b for b in range(nb)}
    out_shape += [jax.ShapeDtypeStruct(b.shape, b.dtype) for b in bufs] + list(lands)
    if token:
        out_shape.append(jax.ShapeDtypeStruct((8, LANES), F32))
        out_specs.append(pl.BlockSpec(memory_space=pltpu.VMEM))
    return pl.pallas_call(
        wrapped, out_shape=out_shape, in_specs=[ANY] * nb + [SEM] * len(sems_in) + [ANY] * na, out_specs=out_specs,
        input_output_aliases=alias,
        compiler_params=pltpu.CompilerParams(has_side_effects=pltpu.SideEffectType.DATAFLOW_SIDE_EFFECTING),
        name=name)(*bufs, *sems_in, *after)


def _gather_start(bufs, *, name, after=()):
    nb = len(bufs)

    def body(ins, sends, recvs):
        x, y, c = lax.axis_index("x"), lax.axis_index("y"), lax.axis_index("c")
        chips = [(1 - x, y), (x, 1 - y), (1 - x, 1 - y)]
        for b in range(nb):
            _gather_copy(ins, sends, recvs, b, 0, (x, y), c, (x, y, 1 - c)).start()
            for j, chip in enumerate(chips):
                _gather_copy(ins, sends, recvs, b, 1 + j, (x, y), c, (*chip, c)).start()

    out = _split_call(body, bufs, (), 4 * nb, name=name, after=after, token=True)
    return (out[0], out[1], out[2:2 + nb]), out[-1]


def _gather_wait_first(flight, *, name, after=()):
    sends, recvs, bufs = flight
    nb = len(bufs)

    def body(ins, sends_, recvs_):
        x, y, c = lax.axis_index("x"), lax.axis_index("y"), lax.axis_index("c")
        chips = [(1 - x, y), (x, 1 - y), (1 - x, 1 - y)]
        for b in range(nb):
            _gather_copy(ins, sends_, recvs_, b, 0, (x, y), c, (x, y, 1 - c)).wait_send()
            _gather_copy(ins, sends_, recvs_, b, 0, (x, y), 1 - c, (x, y, c)).wait_recv()
            for j, chip in enumerate(chips):
                _gather_copy(ins, sends_, recvs_, b, 1 + j, (x, y), c, (*chip, c)).wait_send()
                _gather_copy(ins, sends_, recvs_, b, 1 + j, chip, c, (x, y, c)).wait_recv()

    return _split_call(body, bufs, (sends, recvs), 4 * nb, name=name, after=after)


def _gather_forward(bufs, *, name):
    nb = len(bufs)

    def body(ins, sends, recvs):
        x, y, c = lax.axis_index("x"), lax.axis_index("y"), lax.axis_index("c")
        chips = [(1 - x, y), (x, 1 - y), (1 - x, 1 - y)]
        for b in range(nb):
            for j, chip in enumerate(chips):
                _gather_copy(ins, sends, recvs, b, 1 + j, chip, c, (x, y, 1 - c)).start()

    out = _split_call(body, bufs, (), 4 * nb, name=name)
    return out[0], out[1], out[2:2 + nb]


def _gather_wait_forward(flight, *, name, after=()):
    sends, recvs, bufs = flight
    nb = len(bufs)

    def body(ins, sends_, recvs_):
        x, y, c = lax.axis_index("x"), lax.axis_index("y"), lax.axis_index("c")
        chips = [(1 - x, y), (x, 1 - y), (1 - x, 1 - y)]
        for b in range(nb):
            for j, chip in enumerate(chips):
                _gather_copy(ins, sends_, recvs_, b, 1 + j, chip, c, (x, y, 1 - c)).wait_send()
                _gather_copy(ins, sends_, recvs_, b, 1 + j, chip, 1 - c, (x, y, c)).wait_recv()

    return _split_call(body, bufs, (sends, recvs), 4 * nb, name=name, after=after)


def _owner_copies(hs, lands, sends, recvs):
    x, y, c = lax.axis_index("x"), lax.axis_index("y"), lax.axis_index("c")
    chips = [(1 - x, y), (x, 1 - y), (1 - x, 1 - y)]
    return [pltpu.make_async_remote_copy(src_ref=hs[b].at[2 * cx + cy], dst_ref=lands[b].at[j], send_sem=sends.at[3 * b + j],
                                         recv_sem=recvs.at[3 * b + j], device_id=(cx, cy, c), device_id_type=MESH)
            for b in range(len(hs)) for j, (cx, cy) in enumerate(chips)]


def _owners_start(hs, *, name):
    nb = len(hs)
    lands = [jax.ShapeDtypeStruct((3,) + h.shape[1:], h.dtype) for h in hs]

    def body(refs, sends, recvs):
        for cp in _owner_copies(refs[:nb], refs[nb:], sends, recvs):
            cp.start()

    out = _split_call(body, list(hs), (), 3 * nb, name=name, token=True, lands=lands)
    return (out[0], out[1], out[2:2 + 2 * nb]), out[-1]


def _owners_wait(flight, *, name, after=()):
    sends, recvs, bufs = flight
    nb = len(bufs) // 2

    def body(refs, sends_, recvs_):
        for cp in _owner_copies(refs[:nb], refs[nb:], sends_, recvs_):
            cp.wait()

    out = _split_call(body, bufs, (sends, recvs), 3 * nb, name=name, after=after)
    return out[:nb], out[nb:]


def _sibling_copies(gs, lands, sends, recvs):
    x, y, c = lax.axis_index("x"), lax.axis_index("y"), lax.axis_index("c")
    copies = []
    for b in range(len(gs)):
        hr = gs[b].shape[1] // 2
        copies.append(pltpu.make_async_remote_copy(
            src_ref=gs[b].at[:, pl.ds((1 - c) * hr, hr), :], dst_ref=lands[b], send_sem=sends.at[b], recv_sem=recvs.at[b],
            device_id=(x, y, 1 - c), device_id_type=MESH))
    return copies


def _sibling_start(gs, *, name, after=()):
    nb = len(gs)
    lands = [jax.ShapeDtypeStruct((g.shape[0], g.shape[1] // 2, g.shape[2]), g.dtype) for g in gs]

    def body(refs, sends, recvs):
        for cp in _sibling_copies(refs[:nb], refs[nb:], sends, recvs):
            cp.start()

    out = _split_call(body, list(gs), (), nb, name=name, after=after, token=True, lands=lands)
    return (out[0], out[1], out[2:2 + 2 * nb]), out[-1]


def _sibling_wait(flight, *, name, after=()):
    sends, recvs, bufs = flight
    nb = len(bufs) // 2

    def body(refs, sends_, recvs_):
        for cp in _sibling_copies(refs[:nb], refs[nb:], sends_, recvs_):
            cp.wait()

    out = _split_call(body, bufs, (sends, recvs), nb, name=name, after=after)
    return out[:nb], out[nb:]


def _result_copies(ts, sends, recvs):
    x, y, c = lax.axis_index("x"), lax.axis_index("y"), lax.axis_index("c")
    return [pltpu.make_async_remote_copy(src_ref=ts[b].at[c], dst_ref=ts[b].at[c], send_sem=sends.at[b], recv_sem=recvs.at[b],
                                         device_id=(x, y, 1 - c), device_id_type=MESH) for b in range(len(ts))]


def _result_start(ts, *, name):
    def body(refs, sends, recvs):
        for cp in _result_copies(refs, sends, recvs):
            cp.start()

    out = _split_call(body, ts, (), len(ts), name=name, token=True)
    return (out[0], out[1], out[2:2 + len(ts)]), out[-1]


def _result_wait(flight, *, name, after=()):
    sends, recvs, bufs = flight

    def body(refs, sends_, recvs_):
        for cp in _result_copies(refs, sends_, recvs_):
            cp.wait()

    return _split_call(body, bufs, (sends, recvs), len(bufs), name=name, after=after)


def _row_tile(rows, cols):
    best = 16
    for t in range(16, rows + 1, 16):
        if rows % t == 0 and t * cols <= 640 * 1024:
            best = t
    assert rows % best == 0, (rows, cols)
    return best


def _add_sibling_half(g, recv, core, *, name):
    nk, r, n = g.shape
    hr = r // 2
    tr = _row_tile(hr, n)

    def body(c_ref, a_ref, b_ref, o_ref):
        o_ref[...] = (a_ref[...].astype(F32) + b_ref[...].astype(F32)).astype(BF16)

    grid_spec = pltpu.PrefetchScalarGridSpec(
        num_scalar_prefetch=1, grid=(nk, hr // tr),
        in_specs=[pl.BlockSpec((None, tr, n), lambda k, i, c_ref: (k, c_ref[0] * (hr // tr) + i, 0)),
                  pl.BlockSpec((None, tr, n), lambda k, i, c_ref: (k, i, 0))],
        out_specs=pl.BlockSpec((None, tr, n), lambda k, i, c_ref: (k, i, 0)))
    return pl.pallas_call(body, grid_spec=grid_spec, out_shape=jax.ShapeDtypeStruct((nk, hr, n), BF16),
                          compiler_params=_params(("parallel", "parallel")), name=name)(core, g, recv)


def _add_chip_sums(h, recv, chip_core, *, name):
    _, hr, n = h.shape
    tr = _row_tile(hr, n)

    def body(k_ref, a_ref, b_ref, o_ref):
        o_ref[...] = ((a_ref[...].astype(F32) + b_ref[0].astype(F32)) + b_ref[1].astype(F32)) + b_ref[2].astype(F32)

    grid_spec = pltpu.PrefetchScalarGridSpec(
        num_scalar_prefetch=1, grid=(hr // tr,),
        in_specs=[pl.BlockSpec((None, tr, n), lambda i, k_ref: (k_ref[0], i, 0)),
                  pl.BlockSpec((3, tr, n), lambda i, k_ref: (0, i, 0))],
        out_specs=pl.BlockSpec((None, tr, n), lambda i, k_ref: (k_ref[1], i, 0)))
    return pl.pallas_call(body, grid_spec=grid_spec, out_shape=jax.ShapeDtypeStruct((2, hr, n), F32),
                          compiler_params=_params(("parallel",)), name=name)(chip_core, h, recv)


def _sum_devices(g, *, name):
    nd, r, n = g.shape

    def body(g_ref, o_ref):
        acc = g_ref[0]
        for i in range(1, nd):
            acc = acc + g_ref[i]
        o_ref[...] = acc

    return pl.pallas_call(body, out_shape=jax.ShapeDtypeStruct((r, n), F32), name=name)(g)


def _own_slot(parts, chip, *, name, after=()):
    rows, cols = sum(w.shape[1] for w, _ in parts), parts[0][0].shape[2]
    buf, row0 = None, 0
    for p, (w, idx) in enumerate(parts):
        r = w.shape[1]
        tr = 256 if r % 256 == 0 else r
        assert row0 % tr == 0, (name, r, row0)
        prev = () if buf is None else (buf,)

        def body(chip_ref, w_ref, *rest):
            rest[-1][...] = w_ref[...].astype(BF16)

        grid_spec = pltpu.PrefetchScalarGridSpec(
            num_scalar_prefetch=1, grid=(r // tr,),
            in_specs=[pl.BlockSpec((None, tr, cols), lambda i, c_ref, idx=idx: (idx, i, 0))] + [ANY] * (len(prev) + len(after)),
            out_specs=pl.BlockSpec((None, tr, cols), lambda i, c_ref, row0=row0, tr=tr: (c_ref[0], row0 // tr + i, 0)))
        buf = pl.pallas_call(body, grid_spec=grid_spec, out_shape=jax.ShapeDtypeStruct((N_CHIPS, rows, cols), BF16),
                             input_output_aliases={2: 0} if prev else {}, compiler_params=_params(("parallel",)),
                             name=f"{name}{p}")(chip, w, *prev, *after)
        row0 += r
    return buf


def kernel(x, c, ada_w, ada_b, mix_norm_w, mlp_norm_w, mlp_up, mlp_down, ssd_in_w, ssd_conv_w, ssd_conv_b, ssd_dt_bias, ssd_A_log, ssd_D, ssd_norm_w, ssd_out_w, sc_in_w, sc_conv_w, sc_out_w, final_norm_w, loss_target, m_ada_w, m_ada_b, m_mix_norm_w, m_mlp_norm_w, m_mlp_up, m_mlp_down, m_ssd_in_w, m_ssd_conv_w, m_ssd_conv_b, m_ssd_dt_bias, m_ssd_A_log, m_ssd_D, m_ssd_norm_w, m_ssd_out_w, m_sc_in_w, m_sc_conv_w, m_sc_out_w, m_final_norm_w, v_ada_w, v_ada_b, v_mix_norm_w, v_mlp_norm_w, v_mlp_up, v_mlp_down, v_ssd_in_w, v_ssd_conv_w, v_ssd_conv_b, v_ssd_dt_bias, v_ssd_A_log, v_ssd_D, v_ssd_norm_w, v_ssd_out_w, v_sc_in_w, v_sc_conv_w, v_sc_out_w, v_final_norm_w):
    xi, yi, ci = lax.axis_index("x"), lax.axis_index("y"), lax.axis_index("c")
    chip = 2 * xi + yi
    dev = 2 * chip + ci
    n_ada = ada_w.shape[2]

    conv_flat = jnp.concatenate([ssd_conv_w.reshape(-1), sc_conv_w.reshape(-1), jnp.zeros((256,), F32)]).reshape(4, D)
    blk0 = jnp.concatenate([c, conv_flat, jnp.zeros((3, D), F32)], axis=0)
    got0 = _all_gather_rows(blk0, name="gather_cond").reshape(N_DEV, 8, D)
    c_all = got0[:, 0]
    conv_all = got0[0::2, 1:5].reshape(N_CHIPS, 4 * D)
    ssd_conv = jnp.moveaxis(conv_all[:, :4 * 768].reshape(N_CHIPS, 4, 768), 0, 1).reshape(4, CONVD)
    sc_conv = jnp.moveaxis(conv_all[:, 4 * 768:4 * 768 + 3 * 256].reshape(N_CHIPS, 3, 256), 0, 1).reshape(3, D)
    mod_shard = [_matmul(c_all, ada_w, n=n_ada, a_silu=True, b_spec=pl.BlockSpec((None, D, 512), lambda mi, j, i=i: (i, 0, j)),
                         extras=(lax.dynamic_slice(ada_b, (i, chip * n_ada), (1, n_ada)),),
                         epi=lambda acc, b: (acc + b,), name=f"ada_mod{i}") for i in range(2)]
    mod_slot = lax.dynamic_update_slice(jnp.zeros((N_CHIPS, 2 * N_DEV, n_ada), F32), jnp.concatenate(mod_shard, axis=0)[None],
                                        (chip, 0, 0))

    up_row, down_row = 0, D
    chip1 = chip.reshape(1).astype(jnp.int32)
    a_bufs = [mod_slot, _own_slot([(jnp.swapaxes(ssd_in_w, 1, 2), 0)], chip1, name="slot_ssd_in")]
    fly_a, tok = _gather_start(a_bufs, name="gather_a_start")
    b_bufs = [_own_slot([(ssd_out_w, 0)], chip1, name="slot_ssd_out", after=(tok,)),
              _own_slot([(mlp_up, 0), (mlp_down, 0)], chip1, name="slot_mlp0_", after=(tok,))]
    fly_b, tok = _gather_start(b_bufs, name="gather_b_start", after=(tok,))
    c_bufs = [_own_slot([(sc_in_w, 0)], chip1, name="slot_sc_in", after=(tok,)),
              _own_slot([(sc_out_w, 0)], chip1, name="slot_sc_out", after=(tok,))]
    fly_c, tok = _gather_start(c_bufs, name="gather_c_start", after=(tok,))
    d_bufs = [_own_slot([(mlp_up, 1), (mlp_down, 1)], chip1, name="slot_mlp1_", after=(tok,))]
    fly_d, tok = _gather_start(d_bufs, name="gather_d_start", after=(tok,))

    row = lambda v: v.reshape(1, -1)
    xs, tgt = x[0], loss_target[0]
    prm = jnp.pad(jnp.concatenate([ssd_dt_bias, ssd_A_log, ssd_D, jnp.zeros((5, NH), F32)], axis=0), ((0, 0), (0, LANES - NH)))
    mix_nw = [row(mix_norm_w[i]) for i in range(2)]
    mlp_nw = [row(mlp_norm_w[i]) for i in range(2)]
    a_bufs = _gather_wait_first(fly_a, name="gather_a_landed", after=(tok,))
    mod_all, w_ssd_in = _gather_wait_forward(_gather_forward(a_bufs, name="gather_a_pass"), name="gather_a_done")
    mod = lax.dynamic_index_in_dim(mod_all.reshape(N_CHIPS, 2, N_DEV, n_ada), dev, axis=2, keepdims=False)
    mod = jnp.moveaxis(mod, 0, 1).reshape(2, 6, D)
    mods = [[mod[i, j:j + 1] for j in range(6)] for i in range(2)]
    w_in_t = w_ssd_in.reshape(N_CHIPS * SSD_IN_SHARD, D)
    w_dt_t = jnp.pad(w_in_t[ZX:], ((0, LANES - NH), (0, 0)))
    scan = _ssd_fwd_scan(xs, mods[0][0:3], mix_nw[0], w_in_t, w_dt_t, ssd_conv, ssd_conv_b, prm, "ssd")

    def land(flight, tag, after):
        return _gather_forward(_gather_wait_first(flight, name=f"gather_{tag}_landed", after=(after,)), name=f"gather_{tag}_pass")

    passed, got = {"b": land(fly_b, "b", scan[4])}, {}

    def done(tag, after):
        got[tag] = _gather_wait_forward(passed[tag], name=f"gather_{tag}_done", after=(after,))
        return got[tag]

    x1, s_ssd = _ssd_fwd_out(xs, mods[0][0:3], scan, ssd_norm_w, lambda yn: done("b", yn)[0], "ssd")
    w_ssd_out, w_b = got["b"]
    x2, s_mlp0 = _mlp_fwd(x1, mods[0][3:6], mlp_nw[0], w_b, up_row, down_row, "mlp0",
                          midway=lambda a: passed.update(c=land(fly_c, "c", a)))
    w_sc_in, w_sc_out = done("c", x2)
    x3, s_sc = _sc_layer_fwd(x2, mods[1][0:3], mix_nw[1], w_sc_in, sc_conv, w_sc_out, 0, "sc",
                             midway=lambda proj: passed.update(d=land(fly_d, "d", proj)))
    (w_mlp1,) = done("d", x3)
    x4, s_mlp1 = _mlp_fwd(x3, mods[1][3:6], mlp_nw[1], w_mlp1, up_row, down_row, "mlp1")

    core = ci.reshape(1).astype(jnp.int32)
    chip_core = jnp.stack([chip, ci]).astype(jnp.int32)

    def reduce_swap(gbufs, tag, after=()):
        return _sibling_start(gbufs, name=tag + "_sibling_start", after=after)

    def reduce_send(flight, tag, after):
        gs, sib = _sibling_wait(flight, name=tag + "_sibling_landed", after=after)
        hs = [_add_sibling_half(g, s, core, name=f"{tag}_add_sibling{b}") for b, (g, s) in enumerate(zip(gs, sib))]
        return _owners_start(hs, name=tag + "_owners_start")

    def reduce_sum(flight, tag, after):
        hs, lands = _owners_wait(flight, name=tag + "_owners_landed", after=after)
        ts = [_add_chip_sums(h, o, chip_core, name=f"{tag}_add_chips{b}") for b, (h, o) in enumerate(zip(hs, lands))]
        return _result_start(ts, name=tag + "_result_start")

    def reduce_done(flight, tag, after=()):
        return [t.reshape(-1, t.shape[2]) for t in _result_wait(flight, name=tag + "_result_landed", after=after)]

    dx4, fsum, dy, gs = _final_loss(x4, row(final_norm_w), tgt, (mods[1][5], s_mlp1[3]), name="final_loss")
    dx3, g_mlp1, sum_mlp1, dy, gs = _mlp_bwd(dx4, dy, gs, s_mlp1, mods[1][3:6], mlp_nw[1], w_mlp1, None, up_row, down_row,
                                             (mods[1][2], s_sc[4]), "mlp1")
    dx2, g_sc_out, g_sc_in, sum_sc, sc_csum, dy, gs = _sc_layer_bwd(dx3, dy, gs, s_sc, mods[1][0:3], mix_nw[1], w_sc_in,
                                                                    sc_conv, w_sc_out, None, 0, (mods[0][5], s_mlp0[3]), "sc")
    dx1, g_b, sum_mlp0, dy, gsum_ssd = _mlp_bwd(dx2, dy, gs, s_mlp0, mods[0][3:6], mlp_nw[0], w_b, None, up_row, down_row,
                                                (mods[0][2], s_ssd[8]), "mlp0")
    dyn, g_ssd_out = _ssd_bwd_out(dy, s_ssd, w_ssd_out, "ssd")
    fly_1, tok = reduce_swap([g_mlp1, g_sc_out, g_sc_in, g_b, g_ssd_out], "rs1")
    dy, dzx, gnsum = _gnorm_bwd(s_ssd[5], s_ssd[2], ssd_norm_w + tok[0:1, 0:1], dyn, name="ssd_dgnorm")
    fly_1, tok = reduce_send(fly_1, "rs1", (dy,))
    grad_x, d_w_zx, d_w_dt, sum_ssd, csum, ssum = _ssd_bwd_rest(
        dx1, dy, dzx, gsum_ssd, s_ssd, mods[0][0:3], mix_nw[0], w_in_t, w_dt_t, ssd_conv, prm + tok[0:1, 0:1], "ssd")

    def ssd_in_owner(k):
        lo, hi = k * SSD_IN_SHARD, (k + 1) * SSD_IN_SHARD
        if hi <= ZX:
            return d_w_zx[:, lo:hi]
        return jnp.concatenate([d_w_zx[:, lo:], d_w_dt[:, :hi - ZX]], axis=1)

    small = jnp.concatenate([sum_ssd, sum_mlp0, sum_sc, sum_mlp1, csum.reshape(24, D)[0:16], gnsum.reshape(16, D)[0:8],
                             fsum, sc_csum, jnp.pad(ssum, ((0, 0), (0, D - LANES)))], axis=0)
    small_slot = lax.dynamic_update_slice(jnp.zeros((N_CHIPS, 2 * SMALL_ROWS, D), F32), small[None], (chip, ci * SMALL_ROWS, 0))
    fly_s, tok = _gather_start([small_slot], name="gather_small_start")
    fly_1, tok = reduce_sum(fly_1, "rs1", (grad_x, tok))
    fly_s = _gather_forward(_gather_wait_first(fly_s, name="gather_small_landed", after=(tok,)), name="gather_small_pass")
    fly_2, tok = reduce_swap([jnp.stack([ssd_in_owner(k) for k in range(N_CHIPS)]).astype(BF16)], "rs2", (tok,))
    fly_2, tok = reduce_send(fly_2, "rs2", (tok,))
    (small_all,) = _gather_wait_forward(fly_s, name="gather_small_done", after=(tok,))
    t_mlp1, t_sc_out, t_sc_in, t_b, t_ssd_out = reduce_done(fly_1, "rs1", (tok,))
    small_all = small_all.reshape(N_DEV, SMALL_ROWS, D) + tok[0:1, 0:1]
    tot = _sum_devices(small_all, name="sum_small")
    loss = tot[FINAL_ROW + 1, 0]
    mod_rows = [r + o for r in SUB_ROW for o in (3, 2, 0)]
    g_ada_b = jnp.stack([tot[r] for r in mod_rows]).reshape(2, 6 * D)
    g_mix_norm = jnp.stack([tot[SUB_ROW[0] + 1], tot[SUB_ROW[2] + 1]])
    g_mlp_norm = jnp.stack([tot[SUB_ROW[1] + 1], tot[SUB_ROW[3] + 1]])
    conv_sums = tot[SSD_CONV_ROW:SSD_CONV_ROW + 15].reshape(5, CONVD)
    g_ssd_conv_w = lax.dynamic_slice(conv_sums, (0, chip * 768), (4, 768))[None]
    g_ssd_conv_b = conv_sums[4:5]
    g_ssd_norm = tot[GNORM_ROW:GNORM_ROW + 2].reshape(1, DI)
    g_final = tot[FINAL_ROW]
    g_sc_conv_w = lax.dynamic_slice(tot[SC_CONV_ROW:SC_CONV_ROW + 3], (0, chip * 256), (3, 256))[None]
    g_a_log, g_d, g_dt_bias = (tot[HEAD_ROW + r:HEAD_ROW + r + 1, 0:NH] for r in range(3))
    c_pad = jnp.concatenate([c_all, jnp.zeros((8, D), F32)], axis=0)
    dmod_all = jnp.stack([small_all[:, r] for r in mod_rows], axis=1).reshape(N_DEV, 2, 6 * D)
    g_ada_w = []
    for i in range(2):
        dm = lax.dynamic_slice(dmod_all[:, i], (0, chip * n_ada), (N_DEV, n_ada))
        g_ada_w.append(_matmul_tn(c_pad, jnp.concatenate([dm, jnp.zeros_like(dm)], axis=0), m=D, n=n_ada, a_silu=True,
                                  name=f"ada_dw{i}"))

    big = dict(ada_w=[(g, 0) for g in g_ada_w], mlp_up=[(t_b, up_row), (t_mlp1, up_row)],
               mlp_down=[(t_b, down_row), (t_mlp1, down_row)], ssd_out_w=[(t_ssd_out, 0)], sc_out_w=[(t_sc_out, 0)],
               sc_in_w=[(t_sc_in, 0)], ssd_in_w=None)
    grads = dict(ada_b=g_ada_b, mix_norm_w=g_mix_norm, mlp_norm_w=g_mlp_norm, ssd_conv_w=g_ssd_conv_w,
                 ssd_conv_b=g_ssd_conv_b, ssd_dt_bias=g_dt_bias, ssd_A_log=g_a_log, ssd_D=g_d, ssd_norm_w=g_ssd_norm,
                 sc_conv_w=g_sc_conv_w, final_norm_w=g_final)
    weights = dict(ada_w=(ada_w, m_ada_w, v_ada_w), ada_b=(ada_b, m_ada_b, v_ada_b),
                   mix_norm_w=(mix_norm_w, m_mix_norm_w, v_mix_norm_w), mlp_norm_w=(mlp_norm_w, m_mlp_norm_w, v_mlp_norm_w),
                   mlp_up=(mlp_up, m_mlp_up, v_mlp_up), mlp_down=(mlp_down, m_mlp_down, v_mlp_down),
                   ssd_in_w=(ssd_in_w, m_ssd_in_w, v_ssd_in_w), ssd_conv_w=(ssd_conv_w, m_ssd_conv_w, v_ssd_conv_w),
                   ssd_conv_b=(ssd_conv_b, m_ssd_conv_b, v_ssd_conv_b), ssd_dt_bias=(ssd_dt_bias, m_ssd_dt_bias, v_ssd_dt_bias),
                   ssd_A_log=(ssd_A_log, m_ssd_A_log, v_ssd_A_log), ssd_D=(ssd_D, m_ssd_D, v_ssd_D),
                   ssd_norm_w=(ssd_norm_w, m_ssd_norm_w, v_ssd_norm_w), ssd_out_w=(ssd_out_w, m_ssd_out_w, v_ssd_out_w),
                   sc_in_w=(sc_in_w, m_sc_in_w, v_sc_in_w), sc_conv_w=(sc_conv_w, m_sc_conv_w, v_sc_conv_w),
                   sc_out_w=(sc_out_w, m_sc_out_w, v_sc_out_w), final_norm_w=(final_norm_w, m_final_norm_w, v_final_norm_w))
    def step(nm, parts):
        w, m, v = (t if t.shape[0] == 1 else t.reshape(-1, t.shape[-1]) for t in weights[nm])
        rows, outs = w.shape[-2] // len(parts), None
        for i, (gbuf, g_row) in enumerate(parts):
            outs = _adamw(w, gbuf, m, v, g_row=g_row, w_row=i * rows, rows=rows, into=outs, emit_g=True, name=f"adamw_{nm}{i}")
        return outs

    res = {}
    for nm, (w, m, v) in weights.items():
        two_d = (-1, w.shape[-1]) if w.ndim > 1 else (1, -1)
        if nm not in big:
            res[nm] = (grads[nm], *_adamw(w.reshape(two_d), grads[nm].reshape(two_d), m.reshape(two_d), v.reshape(two_d),
                                          name="adamw_" + nm))
        elif big[nm] is not None:
            res[nm] = step(nm, big[nm])
    fly_2, tok = reduce_sum(fly_2, "rs2", tuple(r[1] for r in res.values()))
    (t_ssd_in,) = reduce_done(fly_2, "rs2", (tok,))
    w_t, m_t, v_t = (jnp.swapaxes(t[0], 0, 1) for t in weights["ssd_in_w"])
    res["ssd_in_w"] = [jnp.swapaxes(o, 0, 1) for o in _adamw(w_t, t_ssd_in.T, m_t, v_t, emit_g=True, name="adamw_ssd_in_w")]
    outs = [[res[nm][k].reshape(weights[nm][0].shape) for nm in weights] for k in range(4)]
    return (loss, grad_x[None], *outs[0], *outs[1], *outs[2], *outs[3])
```

```python
import jax
import jax.numpy as jnp
from jax import lax
from jax.experimental import pallas as pl
from jax.experimental.pallas import tpu as pltpu

F32 = jnp.float32
BF16 = jnp.bfloat16
MESH = pl.DeviceIdType.MESH

D = 1024
DFF = 4096
DI = 2048
NH = 32
HP = 64
NG = 4
NS = 128
CH = 128
CONVD = DI + 2 * NG * NS
ZX = DI + CONVD
GW = NG * NS
LANES = 128
N_CHIPS = 4
N_DEV = 8
EPS = 1e-5
ADAM_LR, ADAM_B1, ADAM_B2, ADAM_EPS, ADAM_WD, ADAM_STEP = 1e-3, 0.9, 0.999, 1e-8, 0.01, 10
VMEM_LIMIT = 48 * 1024 * 1024
TM_ALL = 2048
TM_HALF = 1024
ANY = pl.BlockSpec(memory_space=pl.ANY)
SEM = pl.BlockSpec(memory_space=pltpu.SEMAPHORE)

SSD_IN_SHARD = 1288
SC_IN_SHARD = 768


def _params(sem=None):
    return pltpu.CompilerParams(dimension_semantics=sem, vmem_limit_bytes=VMEM_LIMIT)


def _sigmoid(v):
    return 0.5 * jnp.tanh(0.5 * v) + 0.5


def _dot(a, b, dims=((1,), (0,)), precision=None):
    return lax.dot_general(a, b, (dims, ((), ())), preferred_element_type=F32, precision=precision)


def _dot_nt(a, b):
    return _dot(a, b, ((1,), (1,)))


def _dot_tn(a, b):
    return _dot(a, b, ((0,), (0,)))


def _nn(av, bv):
    return _dot(av.astype(BF16), bv.astype(BF16))


def _nt(av, bv):
    return _dot_nt(av.astype(BF16), bv.astype(BF16))


def _nn_split(av, bv):
    return _dot(av.astype(BF16), bv.reshape(-1, bv.shape[2]))


def _nn_split_sq(av, bv):
    return _nn_split(av * av, bv)


def _nt_split(av, bv):
    kc = bv.shape[2]
    acc = _dot_nt(av[:, 0:kc].astype(BF16), bv[0])
    for s in range(1, bv.shape[0]):
        acc = acc + _dot_nt(av[:, s * kc:(s + 1) * kc].astype(BF16), bv[s])
    return acc


def _nt_sc_in(av, bv):
    q = 256
    acc = None
    for i in range(3 * D // q):
        a_blk = av[i // 4][:, (i % 4) * q:(i % 4 + 1) * q]
        b_blk = bv[i // 3][:, (i % 3) * q:(i % 3 + 1) * q]
        t = _dot_nt(a_blk, b_blk)
        acc = t if acc is None else acc + t
    return acc


def _matmul(a, b, *, name, n, contract=_nn, a_spec=None, b_spec=None, tm=512, tn=512, extras=(), epi=None,
            out_dtypes=(F32,), a_silu=False):
    M = a.shape[-2]
    tm, tn = min(tm, M), min(tn, n)
    assert M % tm == 0 and n % tn == 0, (name, M, n, tm, tn)
    n_ex = len(extras)
    if a_spec is None:
        a_spec = pl.BlockSpec((tm, a.shape[1]), lambda i, j: (i, 0))
    if b_spec is None:
        b_spec = (pl.BlockSpec((tn, b.shape[1]), lambda i, j: (j, 0)) if contract is _nt
                  else pl.BlockSpec((b.shape[0], tn), lambda i, j: (0, j)))

    def body(*refs):
        av = refs[0][...]
        if a_silu:
            av = av * _sigmoid(av)
        acc = contract(av, refs[1][...])
        res = epi(acc, *[r[...] for r in refs[2:2 + n_ex]]) if epi is not None else (acc,)
        for o_ref, r in zip(refs[2 + n_ex:], res, strict=True):
            o_ref[...] = r.astype(o_ref.dtype)

    in_specs = [a_spec, b_spec]
    for e in extras:
        in_specs.append(pl.BlockSpec((1, tn), lambda i, j: (0, j)) if e.shape[0] == 1 and M != 1
                        else pl.BlockSpec((tm, tn), lambda i, j: (i, j)))
    outs = pl.pallas_call(
        body, grid=(M // tm, n // tn), in_specs=in_specs,
        out_specs=[pl.BlockSpec((tm, tn), lambda i, j: (i, j)) for _ in out_dtypes],
        out_shape=[jax.ShapeDtypeStruct((M, n), dt) for dt in out_dtypes],
        compiler_params=_params(("parallel", "parallel")), name=name)(a, b, *extras)
    return outs if len(out_dtypes) > 1 else outs[0]


def _matmul_tn(a, b, *, name, m, n, tm=512, tn=512, a_spec=None, b_spec=None, out_spec=None, out_struct=None, into=None,
               a_silu=False, a_square=False):
    T = a.shape[-2]
    tm, tn = min(tm, m), min(tn, n)
    assert m % tm == 0 and n % tn == 0, (name, m, n, tm, tn)
    if a_spec is None:
        a_spec = pl.BlockSpec((T, tm), lambda i, j: (0, i))
    if b_spec is None:
        b_spec = pl.BlockSpec((T, tn), lambda i, j: (0, j))
    if out_spec is None:
        out_spec, out_struct = pl.BlockSpec((tm, tn), lambda i, j: (i, j)), jax.ShapeDtypeStruct((m, n), F32)

    def body(a_ref, b_ref, *rest):
        av = a_ref[...]
        if a_silu:
            av = av * _sigmoid(av)
        if a_square:
            av = av * av
        rest[-1][...] = _dot_tn(av.astype(BF16), b_ref[...].astype(BF16)).astype(rest[-1].dtype)

    args, in_specs, alias = [a, b], [a_spec, b_spec], {}
    if into is not None:
        args, in_specs, alias = args + [into], in_specs + [ANY], {2: 0}
    return pl.pallas_call(body, grid=(m // tm, n // tn), in_specs=in_specs, out_specs=out_spec, out_shape=out_struct,
                          input_output_aliases=alias, compiler_params=_params(("parallel", "parallel")), name=name)(*args)


def _modnorm_fwd(x, nw, sc, sh, *, name):
    L = x.shape[0]
    tm = min(L, 512)

    def body(x_ref, nw_ref, sc_ref, sh_ref, h_ref):
        xv = x_ref[...]
        r = lax.rsqrt(jnp.mean(xv * xv, axis=-1, keepdims=True) + EPS)
        h_ref[...] = ((xv * r * nw_ref[...]) * (1.0 + sc_ref[...]) + sh_ref[...]).astype(BF16)

    row = pl.BlockSpec((tm, D), lambda i: (i, 0))
    vec = pl.BlockSpec((1, D), lambda i: (0, 0))
    return pl.pallas_call(body, grid=(L // tm,), in_specs=[row, vec, vec, vec], out_specs=row,
                          out_shape=jax.ShapeDtypeStruct((L, D), BF16),
                          compiler_params=_params(("parallel",)), name=name)(x, nw, sc, sh)


def _gate_outputs(dx, below_refs, dy_ref, gs_ref):
    g_ref, y_ref = below_refs
    dy_ref[...] = (dx * g_ref[...]).astype(BF16)
    gs_ref[0:1, :] += jnp.sum(dx * y_ref[...].astype(F32), axis=0, keepdims=True)


def _modnorm_bwd(x, dh, dxo, nw, sc, gsum, below, *, name):
    L = x.shape[0]
    tm = min(L, 256)
    nb = 0 if below is None else 2

    def body(x_ref, dh_ref, dxo_ref, nw_ref, sc_ref, g_ref, *rest):
        dx_ref, s_ref = rest[nb:nb + 2]

        @pl.when(pl.program_id(0) == 0)
        def _():
            s_ref[...] = g_ref[...]
            if nb:
                rest[-1][...] = jnp.zeros_like(rest[-1])

        xv, dhv = x_ref[...], dh_ref[...].astype(F32)
        r = lax.rsqrt(jnp.mean(xv * xv, axis=-1, keepdims=True) + EPS)
        xhat = xv * r
        dxhat = dhv * (nw_ref[...] * (1.0 + sc_ref[...]))
        dx = dxo_ref[...] + r * (dxhat - xhat * jnp.mean(dxhat * xhat, axis=-1, keepdims=True))
        dx_ref[...] = dx
        s_ref[1:2, :] += jnp.sum(dhv * xhat, axis=0, keepdims=True) * (1.0 + sc_ref[...])
        s_ref[2:3, :] += jnp.sum(dhv * xhat, axis=0, keepdims=True) * nw_ref[...]
        s_ref[3:4, :] += jnp.sum(dhv, axis=0, keepdims=True)
        if nb:
            _gate_outputs(dx, rest[:nb], rest[-2], rest[-1])

    row = pl.BlockSpec((tm, D), lambda i: (i, 0))
    vec = pl.BlockSpec((1, D), lambda i: (0, 0))
    blk = pl.BlockSpec((8, D), lambda i: (0, 0))
    in_specs, out_specs = [row, row, row, vec, vec, blk], [row, blk]
    out_shape = [jax.ShapeDtypeStruct((L, D), F32), jax.ShapeDtypeStruct((8, D), F32)]
    if nb:
        in_specs, out_specs = in_specs + [vec, row], out_specs + [row, blk]
        out_shape += [jax.ShapeDtypeStruct((L, D), BF16), jax.ShapeDtypeStruct((8, D), F32)]
    return pl.pallas_call(body, grid=(L // tm,), in_specs=in_specs, out_specs=out_specs, out_shape=out_shape,
                          compiler_params=_params(("arbitrary",)), name=name)(x, dh, dxo, nw, sc, gsum, *(below or ()))


def _final_loss(x, fw, tgt, below, *, name):
    L = x.shape[0]
    tm = min(L, 256)

    def body(x_ref, fw_ref, t_ref, g_ref, y_ref, dx_ref, s_ref, dy_ref, gs_ref):
        @pl.when(pl.program_id(0) == 0)
        def _():
            s_ref[...] = jnp.zeros_like(s_ref)
            gs_ref[...] = jnp.zeros_like(gs_ref)

        xv = x_ref[...]
        r = lax.rsqrt(jnp.mean(xv * xv, axis=-1, keepdims=True) + EPS)
        xhat = xv * r
        diff = xhat * fw_ref[...] - t_ref[...]
        dout = diff * (1.0 / D)
        dxhat = dout * fw_ref[...]
        dx = r * (dxhat - xhat * jnp.mean(dxhat * xhat, axis=-1, keepdims=True))
        dx_ref[...] = dx
        s_ref[0:1, :] += jnp.sum(dout * xhat, axis=0, keepdims=True)
        s_ref[1:2, :] += jnp.zeros((1, D), F32) + 0.5 * jnp.sum(jnp.sum(diff * diff, axis=-1, keepdims=True) * (1.0 / D))
        _gate_outputs(dx, (g_ref, y_ref), dy_ref, gs_ref)

    row = pl.BlockSpec((tm, D), lambda i: (i, 0))
    vec = pl.BlockSpec((1, D), lambda i: (0, 0))
    blk = pl.BlockSpec((8, D), lambda i: (0, 0))
    return pl.pallas_call(body, grid=(L // tm,), in_specs=[row, vec, row, vec, row], out_specs=[row, blk, row, blk],
                          out_shape=[jax.ShapeDtypeStruct((L, D), F32), jax.ShapeDtypeStruct((8, D), F32),
                                     jax.ShapeDtypeStruct((L, D), BF16), jax.ShapeDtypeStruct((8, D), F32)],
                          compiler_params=_params(("arbitrary",)), name=name)(x, fw, tgt, *below)


def _shift_down(v, j):
    if j == 0:
        return v
    rolled = pltpu.roll(v, j, 0)
    row = lax.broadcasted_iota(jnp.int32, (8, v.shape[1]), 0)
    return jnp.concatenate([jnp.where(row >= j, rolled[0:8], 0.0), rolled[8:]], axis=0)


def _shift_up(v, j):
    if j == 0:
        return v
    n = v.shape[0]
    rolled = pltpu.roll(v, n - j, 0)
    row = lax.broadcasted_iota(jnp.int32, (8, v.shape[1]), 0)
    return jnp.concatenate([rolled[:n - 8], jnp.where(row < 8 - j, rolled[n - 8:], 0.0)], axis=0)


def _ssd_conv_fwd(zx, w, b, *, name):
    L = zx.shape[0]
    cb = 256
    k = w.shape[0]

    def body(x_ref, w_ref, b_ref, o_ref, p_ref):
        xv = x_ref[...].astype(F32)
        pre = b_ref[...] + xv * w_ref[k - 1:k, :]
        for j in range(1, k):
            pre = pre + _shift_down(xv, j) * w_ref[k - 1 - j:k - j, :]
        o_ref[...] = (pre * _sigmoid(pre)).astype(BF16)
        p_ref[...] = pre.astype(BF16)

    blk = pl.BlockSpec((L, cb), lambda i: (0, i))
    return pl.pallas_call(
        body, grid=(CONVD // cb,),
        in_specs=[pl.BlockSpec((L, cb), lambda i: (0, i + DI // cb)), pl.BlockSpec((k, cb), lambda i: (0, i)),
                  pl.BlockSpec((1, cb), lambda i: (0, i))],
        out_specs=[blk, blk], out_shape=[jax.ShapeDtypeStruct((L, CONVD), BF16)] * 2,
        compiler_params=_params(("parallel",)), name=name)(zx, w, b)


def _ssd_conv_bwd(zx, pre, dact, w, dzx, *, name):
    L = zx.shape[0]
    cb = 256
    k = w.shape[0]

    def body(x_ref, p_ref, da_ref, w_ref, _, dx_ref, s_ref):
        xv, pv = x_ref[...].astype(F32), p_ref[...].astype(F32)
        s = _sigmoid(pv)
        dpre = da_ref[...].astype(F32) * (s * (1.0 + pv * (1.0 - s)))
        s_ref[...] = jnp.zeros_like(s_ref)
        s_ref[k:k + 1, :] = jnp.sum(dpre, axis=0, keepdims=True)
        s_ref[k - 1:k, :] = jnp.sum(dpre * xv, axis=0, keepdims=True)
        dx = dpre * w_ref[k - 1:k, :]
        for j in range(1, k):
            later = _shift_up(dpre, j)
            dx = dx + later * w_ref[k - 1 - j:k - j, :]
            s_ref[k - 1 - j:k - j, :] = jnp.sum(later * xv, axis=0, keepdims=True)
        dx_ref[...] = dx.astype(BF16)

    blk = pl.BlockSpec((L, cb), lambda i: (0, i))
    return pl.pallas_call(
        body, grid=(CONVD // cb,),
        in_specs=[pl.BlockSpec((L, cb), lambda i: (0, i + DI // cb)), blk, blk, pl.BlockSpec((k, cb), lambda i: (0, i)), ANY],
        out_specs=[pl.BlockSpec((L, cb), lambda i: (0, i + DI // cb)), pl.BlockSpec((8, cb), lambda i: (0, i))],
        out_shape=[jax.ShapeDtypeStruct((L, ZX), BF16), jax.ShapeDtypeStruct((8, CONVD), F32)],
        input_output_aliases={4: 0}, compiler_params=_params(("parallel",)), name=name)(zx, pre, dact, w, dzx)


def _sc_fwd(proj, w, *, name):
    L = proj.shape[0]
    cb = 256
    nb = D // cb
    k = w.shape[0]

    def body(b_ref, c_ref, x_ref, w_ref, o_ref, v_ref):
        u = c_ref[...].astype(F32) * x_ref[...].astype(F32)
        v = u * w_ref[k - 1:k, :]
        for j in range(1, k):
            v = v + _shift_down(u, j) * w_ref[k - 1 - j:k - j, :]
        o_ref[...] = (b_ref[...].astype(F32) * v).astype(BF16)
        v_ref[...] = v.astype(BF16)

    blk = pl.BlockSpec((L, cb), lambda i: (0, i))
    return pl.pallas_call(
        body, grid=(nb,),
        in_specs=[blk, pl.BlockSpec((L, cb), lambda i: (0, i + nb)), pl.BlockSpec((L, cb), lambda i: (0, i + 2 * nb)),
                  pl.BlockSpec((k, cb), lambda i: (0, i))],
        out_specs=[blk, blk], out_shape=[jax.ShapeDtypeStruct((L, D), BF16)] * 2,
        compiler_params=_params(("parallel",)), name=name)(proj, proj, proj, w)


def _sc_bwd(proj, v, dyv, w, *, name):
    L = proj.shape[0]
    cb = 256
    nb = D // cb
    k = w.shape[0]

    def body(b_ref, c_ref, x_ref, v_ref, dy_ref, w_ref, dp_ref, s_ref):
        cv, xv = c_ref[...].astype(F32), x_ref[...].astype(F32)
        u = cv * xv
        dyv_ = dy_ref[...].astype(F32)
        dp_ref[0] = (dyv_ * v_ref[...].astype(F32)).astype(BF16)
        dv = dyv_ * b_ref[...].astype(F32)
        s_ref[...] = jnp.zeros_like(s_ref)
        s_ref[k - 1:k, :] = jnp.sum(dv * u, axis=0, keepdims=True)
        du = dv * w_ref[k - 1:k, :]
        for j in range(1, k):
            later = _shift_up(dv, j)
            du = du + later * w_ref[k - 1 - j:k - j, :]
            s_ref[k - 1 - j:k - j, :] = jnp.sum(later * u, axis=0, keepdims=True)
        dp_ref[1] = (du * xv).astype(BF16)
        dp_ref[2] = (du * cv).astype(BF16)

    blk = pl.BlockSpec((L, cb), lambda i: (0, i))
    return pl.pallas_call(
        body, grid=(nb,),
        in_specs=[blk, pl.BlockSpec((L, cb), lambda i: (0, i + nb)), pl.BlockSpec((L, cb), lambda i: (0, i + 2 * nb)),
                  blk, blk, pl.BlockSpec((k, cb), lambda i: (0, i))],
        out_specs=[pl.BlockSpec((3, L, cb), lambda i: (0, 0, i)), pl.BlockSpec((8, cb), lambda i: (0, i))],
        out_shape=[jax.ShapeDtypeStruct((3, L, D), BF16), jax.ShapeDtypeStruct((8, D), F32)],
        compiler_params=_params(("parallel",)), name=name)(proj, proj, proj, v, dyv, w)


def _pieces(v, n):
    out, rest = [], v
    for _ in range(n):
        out.append(rest.astype(BF16))
        rest = rest - out[-1].astype(F32)
    return out


def _cumsum_rows(mask, v):
    m = mask.astype(BF16)
    return _dot(jnp.concatenate([m, m, m], axis=1), jnp.concatenate(_pieces(v, 3), axis=0))


def _ssd_chunk_terms(dtr, prm):
    lane = lax.broadcasted_iota(jnp.int32, (CH, LANES), 1)
    valid = lane < NH
    xdt = dtr + prm[0:1, :]
    dt = jnp.where(valid, jnp.maximum(xdt, 0.0) + jnp.log1p(jnp.exp(-jnp.abs(xdt))), 0.0)
    A = -jnp.exp(prm[1:2, :])
    ri = lax.broadcasted_iota(jnp.int32, (CH, CH), 0)
    ci = lax.broadcasted_iota(jnp.int32, (CH, CH), 1)
    cs = _cumsum_rows(ri >= ci, dt * A)
    last = cs[CH - 1:CH, :]
    spread = (lax.broadcasted_iota(jnp.int32, (2 * LANES, DI), 1) // HP
              == lax.broadcasted_iota(jnp.int32, (2 * LANES, DI), 0) % LANES).astype(BF16)
    gather = ((lax.broadcasted_iota(jnp.int32, (LANES, 2 * DI), 1) % DI) // HP
              == lax.broadcasted_iota(jnp.int32, (LANES, 2 * DI), 0)).astype(BF16)
    return dict(valid=valid, xdt=xdt, dt=dt, A=A, cs=cs, csT=cs.T, last=last, ri=ri, ci=ci, ex=(spread, gather))


def _expand(v, ex):
    if v.shape[0] == 1:
        return _expand(jnp.broadcast_to(v, (8, LANES)), ex)[0:1, :]
    return _dot(jnp.concatenate(_pieces(v, 2), axis=1), ex[0])


def _head_sum(v, ex):
    if v.shape[0] == 1:
        return _head_sum(jnp.broadcast_to(v, (8, DI)), ex)[0:1, :]
    return _dot_nt(jnp.concatenate(_pieces(v, 2), axis=1), ex[1])


def _ssd_fwd(xbc, dtr, prm, *, name):
    L = xbc.shape[0]
    nc = L // CH

    def body(xbc_ref, dtr_ref, prm_ref, y_ref, sp_ref, st_ref):
        @pl.when(pl.program_id(0) == 0)
        def _():
            st_ref[...] = jnp.zeros_like(st_ref)

        prm_v = prm_ref[...]
        t = _ssd_chunk_terms(dtr_ref[...], prm_v)
        cs, csT, ex, causal = t["cs"], t["csT"], t["ex"], t["ri"] >= t["ci"]
        xs = xbc_ref[:, 0:DI].astype(F32)
        X = xs * _expand(t["dt"], ex)
        Xb = X.astype(BF16)
        Xd = (X * _expand(jnp.exp(t["last"] - cs), ex)).astype(BF16)
        Ex = _expand(jnp.exp(cs), ex)
        cdx = _expand(jnp.exp(t["last"]), ex)
        dskx = _expand(prm_v[2:3, :], ex)
        lane = lax.broadcasted_iota(jnp.int32, (CH, LANES), 1)
        sp_ref[0] = st_ref[...]
        for g in range(NG):
            Bg = xbc_ref[:, DI + g * NS:DI + (g + 1) * NS].astype(BF16)
            Cg = xbc_ref[:, DI + GW + g * NS:DI + GW + (g + 1) * NS].astype(BF16)
            G = _dot_nt(Cg, Bg)
            Sg = st_ref[:, g * GW:(g + 1) * GW]
            yoff = _dot(Cg, Sg.astype(BF16)) * Ex[:, g * GW:(g + 1) * GW]
            for j in range(GW // LANES):
                lo = g * GW + j * LANES
                Xp = Xb[:, lo:lo + LANES]
                yd = []
                for h in (lo // HP, lo // HP + 1):
                    seg = cs[:, h:h + 1] - csT[h:h + 1, :]
                    yd.append(_dot((G * jnp.where(causal, jnp.exp(seg), 0.0)).astype(BF16), Xp))
                y_ref[:, lo:lo + LANES] = (jnp.where(lane < HP, yd[0], yd[1]) + yoff[:, j * LANES:(j + 1) * LANES]
                                           + dskx[:, lo:lo + LANES] * xs[:, lo:lo + LANES]).astype(BF16)
            st_ref[:, g * GW:(g + 1) * GW] = Sg * cdx[:, g * GW:(g + 1) * GW] + _dot_tn(Bg, Xd[:, g * GW:(g + 1) * GW])

    return pl.pallas_call(
        body, grid=(nc,),
        in_specs=[pl.BlockSpec((CH, CONVD), lambda c: (c, 0)), pl.BlockSpec((CH, LANES), lambda c: (c, 0)),
                  pl.BlockSpec((8, LANES), lambda c: (0, 0))],
        out_specs=[pl.BlockSpec((CH, DI), lambda c: (c, 0)), pl.BlockSpec((1, NS, DI), lambda c: (c, 0, 0))],
        out_shape=[jax.ShapeDtypeStruct((L, DI), BF16), jax.ShapeDtypeStruct((nc, NS, DI), F32)],
        scratch_shapes=[pltpu.VMEM((NS, DI), F32)],
        compiler_params=_params(("arbitrary",)), name=name)(xbc, dtr, prm)


def _ssd_bwd(xbc, dtr, prm, dy, sprev, *, name):
    L = xbc.shape[0]
    nc = L // CH

    def body(xbc_ref, dtr_ref, prm_ref, dy_ref, sp_ref, dxbc_ref, ddtr_ref, s_ref, dst_ref, dx_scr, de_scr, dd_scr):
        step = pl.program_id(0)

        @pl.when(step == 0)
        def _():
            dst_ref[...] = jnp.zeros_like(dst_ref)
            s_ref[...] = jnp.zeros_like(s_ref)

        prm_v = prm_ref[...]
        t = _ssd_chunk_terms(dtr_ref[...], prm_v)
        cs, csT, ex, ri, ci = t["cs"], t["csT"], t["ex"], t["ri"], t["ci"]
        E = jnp.exp(cs)
        dec = jnp.exp(t["last"] - cs)
        cd = jnp.exp(t["last"])
        xs = xbc_ref[:, 0:DI].astype(F32)
        dtx = _expand(t["dt"], ex)
        X = xs * dtx
        Xb = X.astype(BF16)
        decx = _expand(dec, ex)
        Xd = (X * decx).astype(BF16)
        Ex = _expand(E, ex)
        cdx = _expand(cd, ex)
        dskx = _expand(prm_v[2:3, :], ex)
        lane = lax.broadcasted_iota(jnp.int32, (CH, LANES), 1)
        dcs = jnp.zeros((CH, LANES), F32)
        dcd_x = []
        for g in range(NG):
            gs = slice(g * GW, (g + 1) * GW)
            Bg = xbc_ref[:, DI + g * NS:DI + (g + 1) * NS].astype(BF16)
            Cg = xbc_ref[:, DI + GW + g * NS:DI + GW + (g + 1) * NS].astype(BF16)
            G = _dot_nt(Cg, Bg)
            GT = _dot_nt(Bg, Cg)
            Sg = sp_ref[0, :, gs]
            Sgb = Sg.astype(BF16)
            dyg = dy_ref[:, gs]
            de_scr[:, gs] = dyg * _dot(Cg, Sgb)
            dYo = (Ex[:, gs] * dyg).astype(BF16)
            dC = _dot_nt(dYo, Sgb)
            dS_in = _dot_tn(Cg, dYo)
            dStg = dst_ref[:, gs]
            dStb = dStg.astype(BF16)
            dXd = _dot(Bg, dStb)
            dB = _dot_nt(Xd[:, gs], dStb)
            dd_scr[:, gs] = dXd * X[:, gs]
            dXst = dXd * decx[:, gs]
            dG = jnp.zeros((CH, CH), F32)
            dGT = jnp.zeros((CH, CH), F32)
            for j in range(GW // LANES):
                lo = g * GW + j * LANES
                Xp = Xb[:, lo:lo + LANES]
                dyp = dy_ref[:, lo:lo + LANES]
                dXp = dXst[:, j * LANES:(j + 1) * LANES]
                for k, h in enumerate((lo // HP, lo // HP + 1)):
                    dyh = jnp.where((lane < HP) if k == 0 else (lane >= HP), dyp, 0.0).astype(BF16)
                    seg = cs[:, h:h + 1] - csT[h:h + 1, :]
                    Lm = jnp.where(ri >= ci, jnp.exp(seg), 0.0)
                    LmT = jnp.where(ci >= ri, jnp.exp(-seg), 0.0)
                    dM = _dot_nt(dyh, Xp)
                    dMT = _dot_nt(Xp, dyh)
                    MT = GT * LmT
                    rs = jnp.sum(dM * (G * Lm), axis=1, keepdims=True) - jnp.sum(dMT * MT, axis=1, keepdims=True)
                    dcs = dcs + jnp.where(lane == h, rs, 0.0)
                    dG = dG + dM * Lm
                    dGT = dGT + dMT * LmT
                    dXp = dXp + _dot(MT.astype(BF16), dyh)
                dx_scr[:, lo:lo + LANES] = dXp
            dxbc_ref[:, DI + g * NS:DI + (g + 1) * NS] = (dB + _dot(dGT.astype(BF16), Cg)).astype(BF16)
            dxbc_ref[:, DI + GW + g * NS:DI + GW + (g + 1) * NS] = (dC + _dot(dG.astype(BF16), Bg)).astype(BF16)
            dcd_x.append(jnp.sum(dStg * Sg, axis=0, keepdims=True))
            dst_ref[:, gs] = dStg * cdx[:, gs] + dS_in
        dX = dx_scr[...]
        dy = dy_ref[...]
        ddec = _head_sum(dd_scr[...], ex)
        dcd = _head_sum(jnp.concatenate(dcd_x, axis=1), ex)
        dcs = dcs + _head_sum(de_scr[...], ex) * E - ddec * dec
        row = lax.broadcasted_iota(jnp.int32, (CH, LANES), 0)
        dcs = dcs + jnp.where(row == CH - 1, jnp.sum(ddec * dec, axis=0, keepdims=True) + dcd * cd, 0.0)
        da = _cumsum_rows(ci >= ri, dcs)
        ddt = da * t["A"] + _head_sum(dX * xs, ex)
        ddtr = jnp.where(t["valid"], ddt * _sigmoid(t["xdt"]), 0.0)
        ddtr_ref[...] = ddtr
        dxbc_ref[:, 0:DI] = (dX * dtx + dskx * dy).astype(BF16)
        s_ref[0:1, :] += jnp.sum(da * t["dt"], axis=0, keepdims=True)
        s_ref[1:2, :] += _head_sum(jnp.sum(dy * xs, axis=0, keepdims=True), ex)
        s_ref[2:3, :] += jnp.sum(ddtr, axis=0, keepdims=True)

        @pl.when(step == nc - 1)
        def _():
            s_ref[0:1, :] = s_ref[0:1, :] * t["A"]

    rev = lambda c: (nc - 1 - c, 0)
    return pl.pallas_call(
        body, grid=(nc,),
        in_specs=[pl.BlockSpec((CH, CONVD), rev), pl.BlockSpec((CH, LANES), rev), pl.BlockSpec((8, LANES), lambda c: (0, 0)),
                  pl.BlockSpec((CH, DI), rev), pl.BlockSpec((1, NS, DI), lambda c: (nc - 1 - c, 0, 0))],
        out_specs=[pl.BlockSpec((CH, CONVD), rev), pl.BlockSpec((CH, LANES), rev), pl.BlockSpec((8, LANES), lambda c: (0, 0))],
        out_shape=[jax.ShapeDtypeStruct((L, CONVD), BF16), jax.ShapeDtypeStruct((L, LANES), F32),
                   jax.ShapeDtypeStruct((8, LANES), F32)],
        scratch_shapes=[pltpu.VMEM((NS, DI), F32), pltpu.VMEM((CH, DI), F32), pltpu.VMEM((CH, DI), F32),
                        pltpu.VMEM((CH, DI), F32)],
        compiler_params=_params(("arbitrary",)), name=name)(xbc, dtr, prm, dy, sprev)


def _gnorm_fwd(y, zx, nw, *, name):
    L = y.shape[0]
    tm = min(L, 256)

    def body(y_ref, z_ref, nw_ref, o_ref):
        z = z_ref[...].astype(F32)
        yg = y_ref[...].astype(F32) * (z * _sigmoid(z))
        for g in range(NG):
            v = yg[:, g * GW:(g + 1) * GW]
            r = lax.rsqrt(jnp.mean(v * v, axis=-1, keepdims=True) + EPS)
            o_ref[:, g * GW:(g + 1) * GW] = (v * r * nw_ref[:, g * GW:(g + 1) * GW]).astype(BF16)

    row = pl.BlockSpec((tm, DI), lambda i: (i, 0))
    return pl.pallas_call(body, grid=(L // tm,), in_specs=[row, row, pl.BlockSpec((1, DI), lambda i: (0, 0))],
                          out_specs=row, out_shape=jax.ShapeDtypeStruct((L, DI), BF16),
                          compiler_params=_params(("parallel",)), name=name)(y, zx, nw)


def _gnorm_bwd(y, zx, nw, dyn, *, name):
    L = y.shape[0]
    tm = min(L, 256)

    def body(y_ref, z_ref, nw_ref, dyn_ref, dy_ref, dz_ref, s_ref):
        @pl.when(pl.program_id(0) == 0)
        def _():
            s_ref[...] = jnp.zeros_like(s_ref)

        z, yv = z_ref[...].astype(F32), y_ref[...].astype(F32)
        sz = _sigmoid(z)
        gate = z * sz
        dgate_dz = sz * (1.0 + z * (1.0 - sz))
        for g in range(NG):
            gs = slice(g * GW, (g + 1) * GW)
            v = yv[:, gs] * gate[:, gs]
            r = lax.rsqrt(jnp.mean(v * v, axis=-1, keepdims=True) + EPS)
            vhat = v * r
            dn = dyn_ref[:, gs].astype(F32)
            s_ref[0:1, gs] += jnp.sum(dn * vhat, axis=0, keepdims=True)
            dvhat = dn * nw_ref[:, gs]
            dv = r * (dvhat - vhat * jnp.mean(dvhat * vhat, axis=-1, keepdims=True))
            dy_ref[:, gs] = dv * gate[:, gs]
            dz_ref[:, gs] = (dv * yv[:, gs] * dgate_dz[:, gs]).astype(BF16)

    row = pl.BlockSpec((tm, DI), lambda i: (i, 0))
    return pl.pallas_call(body, grid=(L // tm,), in_specs=[row, row, pl.BlockSpec((1, DI), lambda i: (0, 0)), row],
                          out_specs=[row, row, pl.BlockSpec((8, DI), lambda i: (0, 0))],
                          out_shape=[jax.ShapeDtypeStruct((L, DI), F32), jax.ShapeDtypeStruct((L, ZX), BF16),
                                     jax.ShapeDtypeStruct((8, DI), F32)],
                          compiler_params=_params(("arbitrary",)), name=name)(y, zx, nw, dyn)


def _adamw(w, g, m, v, *, name, g_row=0, w_row=0, rows=None, into=None, emit_g=False):
    lead = w.ndim == 3
    R, C = w.shape[-2:]
    rows = R if rows is None else rows
    tr = max([t for t in range(8, rows + 1, 8) if rows % t == 0 and t * C <= 256 * 1024], default=rows)
    assert g_row % tr == 0 and w_row % tr == 0, (name, g_row, w_row, tr)
    n_out = 4 if emit_g else 3

    def body(w_ref, g_ref, m_ref, v_ref, *rest):
        outs = rest[-n_out:]
        gv = g_ref[...]
        mn = ADAM_B1 * m_ref[...] + (1.0 - ADAM_B1) * gv
        vn = ADAM_B2 * v_ref[...] + (1.0 - ADAM_B2) * (gv * gv)
        m_hat = mn / (1.0 - ADAM_B1 ** ADAM_STEP)
        v_hat = vn / (1.0 - ADAM_B2 ** ADAM_STEP)
        d_ref, mo_ref, vo_ref = outs[-3:]
        d_ref[...] = -ADAM_LR * (m_hat / (jnp.sqrt(v_hat) + ADAM_EPS) + ADAM_WD * w_ref[...])
        mo_ref[...] = mn
        vo_ref[...] = vn
        if emit_g:
            outs[0][...] = gv

    blk = (pl.BlockSpec((None, tr, C), lambda i: (0, i + w_row // tr, 0)) if lead
           else pl.BlockSpec((tr, C), lambda i: (i + w_row // tr, 0)))
    args, in_specs, alias = [w, g, m, v], [blk, pl.BlockSpec((tr, C), lambda i: (i + g_row // tr, 0)), blk, blk], {}
    if into is not None:
        args, in_specs, alias = args + list(into), in_specs + [ANY] * n_out, {4 + k: k for k in range(n_out)}
    return pl.pallas_call(body, grid=(rows // tr,), in_specs=in_specs, out_specs=[blk] * n_out,
                          out_shape=[jax.ShapeDtypeStruct(w.shape, F32)] * n_out, input_output_aliases=alias,
                          compiler_params=_params(("parallel",)), name=name)(*args)


def _residual(acc, xv, gv):
    return xv + gv * acc, acc


def _like(buf):
    return jax.ShapeDtypeStruct(buf.shape, buf.dtype)


def _mlp_fwd(x, mod, nw, wb, up_row, down_row, tag, midway=None):
    sh, sc, g = mod
    h = _modnorm_fwd(x, nw, sc, sh, name=tag + "_norm")
    a = _matmul(h, wb, n=DFF, tm=TM_ALL, b_spec=pl.BlockSpec((None, D, 512), lambda mi, j: (j // 2, up_row // D, j % 2)),
                epi=lambda acc: (jnp.maximum(acc, 0.0),), out_dtypes=(BF16,), name=tag + "_up")
    if midway is not None:
        midway(a)
    xn, y = _matmul(a, wb, n=D, tm=TM_HALF, contract=_nn_split_sq,
                    b_spec=pl.BlockSpec((N_CHIPS, D, 512), lambda mi, j: (0, down_row // D, j)),
                    extras=(x, g), epi=_residual, out_dtypes=(F32, BF16), name=tag + "_down")
    return xn, (x, h, a, y)


def _mlp_bwd(dxo, dy, gsum, saved, mod, nw, wb, gb, up_row, down_row, below, tag):
    x, h, a, y = saved
    sh, sc, g = mod
    du = _matmul(dy, wb, n=DFF, tm=TM_ALL, contract=_nt,
                 b_spec=pl.BlockSpec((None, 512, D), lambda mi, j: (j // 2, down_row // 512 + j % 2, 0)),
                 extras=(a,), epi=lambda acc, av: (acc * (2.0 * av.astype(F32)),), out_dtypes=(BF16,), name=tag + "_dact")
    gb = _matmul_tn(a, dy, m=DFF, n=D, tm=D, tn=D, a_square=True, into=gb, out_struct=_like(wb),
                    out_spec=pl.BlockSpec((None, D, D), lambda mi, j: (mi, down_row // D, 0)), name=tag + "_ddown")
    dh = _matmul(du, wb, n=D, tm=TM_HALF, contract=_nt_split,
                 b_spec=pl.BlockSpec((N_CHIPS, 512, D), lambda mi, j: (0, up_row // 512 + j, 0)), out_dtypes=(BF16,),
                 name=tag + "_dh")
    gb = _matmul_tn(h, du, m=D, n=DFF, tm=D, into=gb, out_struct=_like(wb),
                    out_spec=pl.BlockSpec((None, D, 512), lambda mi, j: (j // 2, up_row // D, j % 2)), name=tag + "_dup")
    dx, sums, *nxt = _modnorm_bwd(x, dh, dxo, nw, sc, gsum, below, name=tag + "_dnorm")
    return dx, gb, sums, *nxt


def _ssd_fwd_scan(x, mod, nw, w_in_t, w_dt_t, conv_w, conv_b, prm, tag):
    sh, sc, g = mod
    h = _modnorm_fwd(x, nw, sc, sh, name=tag + "_norm")
    zx = _matmul(h, w_in_t, n=ZX, tm=TM_ALL, contract=_nt, out_dtypes=(BF16,), name=tag + "_in")
    dtr = _matmul(h, w_dt_t, n=LANES, tm=TM_ALL, contract=_nt, name=tag + "_in_dt")
    xbc, pre = _ssd_conv_fwd(zx, conv_w, conv_b, name=tag + "_conv")
    y, sprev = _ssd_fwd(xbc, dtr, prm, name=tag + "_scan")
    return h, zx, dtr, xbc, y, sprev, pre


def _ssd_fwd_out(x, mod, scan, gn_w, get_w_out, tag):
    sh, sc, g = mod
    h, zx, dtr, xbc, y, sprev, pre = scan
    yn = _gnorm_fwd(y, zx, gn_w, name=tag + "_gnorm")
    w_out = get_w_out(yn)
    xn, yo = _matmul(yn, w_out, n=D, tm=TM_HALF, contract=_nn_split,
                     b_spec=pl.BlockSpec((N_CHIPS, 512, 512), lambda mi, j: (0, 0, j)),
                     extras=(x, g), epi=_residual, out_dtypes=(F32, BF16), name=tag + "_out")
    return xn, (x, h, zx, dtr, xbc, y, sprev, yn, yo, pre)


def _ssd_bwd_out(dyo, saved, w_out, tag):
    x, h, zx, dtr, xbc, y, sprev, yn, yo, pre = saved
    dyn = _matmul(dyo, w_out, n=DI, tm=TM_ALL, contract=_nt, b_spec=pl.BlockSpec((None, 512, D), lambda mi, j: (j, 0, 0)),
                  out_dtypes=(BF16,), name=tag + "_dyn")
    g_out = _matmul_tn(yn, dyo, m=DI, n=D, tn=D, out_struct=_like(w_out),
                       out_spec=pl.BlockSpec((None, 512, D), lambda mi, j: (mi, 0, 0)), name=tag + "_dout")
    return dyn, g_out


def _ssd_bwd_rest(dxo, dy, dzx, gsum, saved, mod, nw, w_in_t, w_dt_t, conv_w, prm, tag):
    x, h, zx, dtr, xbc, y, sprev, yn, yo, pre = saved
    sh, sc, g = mod
    dxbc, ddtr, ssum = _ssd_bwd(xbc, dtr, prm, dy, sprev, name=tag + "_dscan")
    dzx, csum = _ssd_conv_bwd(zx, pre, dxbc, conv_w, dzx, name=tag + "_dconv")
    dh_dt = _matmul(ddtr, w_dt_t, n=D, tm=TM_ALL, name=tag + "_dh_dt")
    dh = _matmul(dzx, w_in_t, n=D, tm=TM_HALF, b_spec=pl.BlockSpec((ZX, 512), lambda mi, j: (0, j)), extras=(dh_dt,),
                 epi=lambda acc, e: (acc + e,), out_dtypes=(BF16,), name=tag + "_dh")
    d_w_zx = _matmul_tn(h, dzx, m=D, n=ZX, tm=D, name=tag + "_din")
    d_w_dt = _matmul_tn(h, ddtr, m=D, n=LANES, tm=D, name=tag + "_din_dt")
    dx, sums = _modnorm_bwd(x, dh, dxo, nw, sc, gsum, None, name=tag + "_dnorm")
    return dx, d_w_zx, d_w_dt, sums, csum, ssum


def _sc_layer_fwd(x, mod, nw, w_sc_in, conv_w, wb, out_row, tag, midway=None):
    sh, sc, g = mod
    h = _modnorm_fwd(x, nw, sc, sh, name=tag + "_norm")
    proj = _matmul(h, w_sc_in, n=3 * D, tm=TM_ALL, tn=256, out_dtypes=(BF16,),
                   b_spec=pl.BlockSpec((None, D, 256), lambda mi, j: (j // 3, 0, j % 3)),
                   name=tag + "_in")
    if midway is not None:
        midway(proj)
    yv, v = _sc_fwd(proj, conv_w, name=tag + "_conv")
    xn, yo = _matmul(yv, wb, n=D, tm=TM_HALF, contract=_nn_split,
                     b_spec=pl.BlockSpec((N_CHIPS, 256, 512), lambda mi, j: (0, out_row // 256, j)),
                     extras=(x, g), epi=_residual, out_dtypes=(F32, BF16), name=tag + "_out")
    return xn, (x, h, proj, yv, yo, v)


def _sc_layer_bwd(dxo, dyo, gsum, saved, mod, nw, w_sc_in, conv_w, wb, gb, out_row, below, tag):
    x, h, proj, yv, yo, v = saved
    sh, sc, g = mod
    L = x.shape[0]
    dyv = _matmul(dyo, wb, n=D, tm=TM_ALL, tn=256, contract=_nt,
                  b_spec=pl.BlockSpec((None, 256, D), lambda mi, j: (j, out_row // 256, 0)), out_dtypes=(BF16,),
                  name=tag + "_dyv")
    gb = _matmul_tn(yv, dyo, m=D, n=D, tm=256, tn=D, into=gb, out_struct=_like(wb),
                    out_spec=pl.BlockSpec((None, 256, D), lambda mi, j: (mi, out_row // 256, 0)), name=tag + "_dout")
    dproj, csum = _sc_bwd(proj, v, dyv, conv_w, name=tag + "_dconv")
    tm = min(L, TM_HALF)
    dh = _matmul(dproj, w_sc_in, n=D, tm=tm, contract=_nt_sc_in, a_spec=pl.BlockSpec((3, tm, D), lambda mi, j: (0, mi, 0)),
                 b_spec=pl.BlockSpec((N_CHIPS, 512, SC_IN_SHARD), lambda mi, j: (0, j, 0)), out_dtypes=(BF16,),
                 name=tag + "_dh")
    g_sc_in = _matmul_tn(h, dproj, m=D, n=3 * D, tm=D, tn=256, b_spec=pl.BlockSpec((None, L, 256), lambda mi, j: (j // 4, 0, j % 4)),
                         out_spec=pl.BlockSpec((None, D, 256), lambda mi, j: (j // 3, 0, j % 3)),
                         out_struct=jax.ShapeDtypeStruct((N_CHIPS, D, SC_IN_SHARD), BF16), name=tag + "_din")
    dx, sums, *nxt = _modnorm_bwd(x, dh, dxo, nw, sc, gsum, below, name=tag + "_dnorm")
    return dx, gb, g_sc_in, sums, csum, *nxt


SUB_ROW = (0, 8, 16, 24)
SSD_CONV_ROW, GNORM_ROW, FINAL_ROW, SC_CONV_ROW, HEAD_ROW, SMALL_ROWS = 32, 48, 56, 64, 72, 80


def _all_gather_rows(blk, *, name):
    m_per, n = blk.shape

    def body(x_ref, out_ref, send_sems, recv_sems, local_sem):
        x, y, c = lax.axis_index("x"), lax.axis_index("y"), lax.axis_index("c")
        me, sibling = (x, y, c), (x, y, 1 - c)
        chips = [(1 - x, y), (x, 1 - y), (1 - x, 1 - y)]

        def rows(px, py, pc):
            return out_ref.at[pl.ds((4 * px + 2 * py + pc) * m_per, m_per), :]

        def copy(k, block, to, src=None):
            return pltpu.make_async_remote_copy(src_ref=rows(*block) if src is None else src, dst_ref=rows(*block),
                                                send_sem=send_sems.at[k], recv_sem=recv_sems.at[k], device_id=to,
                                                device_id_type=MESH)

        mine = pltpu.make_async_copy(x_ref, rows(*me), local_sem)
        mine.start()
        first = [copy(0, me, sibling, src=x_ref)] + [copy(1 + j, me, (*chip, c), src=x_ref) for j, chip in enumerate(chips)]
        for cp in first:
            cp.start()
        passed = [copy(4 + j, (*chip, c), sibling) for j, chip in enumerate(chips)]
        for j, chip in enumerate(chips):
            copy(1 + j, (*chip, c), me).wait_recv()
            passed[j].start()
        copy(0, sibling, me).wait_recv()
        for j, chip in enumerate(chips):
            copy(4 + j, (*chip, 1 - c), me).wait_recv()
        for cp in first + passed:
            cp.wait_send()
        mine.wait()

    return pl.pallas_call(
        body, out_shape=jax.ShapeDtypeStruct((N_DEV * m_per, n), blk.dtype),
        in_specs=[pl.BlockSpec(memory_space=pltpu.VMEM)], out_specs=pl.BlockSpec(memory_space=pltpu.VMEM),
        scratch_shapes=[pltpu.SemaphoreType.DMA((7,)), pltpu.SemaphoreType.DMA((7,)), pltpu.SemaphoreType.DMA],
        name=name)(blk)


def _half(ref, chip, c):
    r, n = ref.shape[1:]
    if r % 32 == 0:
        return ref.at[chip, pl.ds(c * (r // 2), r // 2), :]
    assert n % 256 == 0, ref.shape
    return ref.at[chip, :, pl.ds(c * (n // 2), n // 2)]


def _gather_copy(bufs, sends, recvs, b, k, chip, pc, to):
    piece = _half(bufs[b], 2 * chip[0] + chip[1], pc)
    return pltpu.make_async_remote_copy(src_ref=piece, dst_ref=piece, send_sem=sends.at[4 * b + k], recv_sem=recvs.at[4 * b + k],
                                        device_id=to, device_id_type=MESH)


def _split_call(body, bufs, sems_in, n_sems, *, name, after=(), token=False, lands=()):
    nb, na, nl, starts = len(bufs), len(after), len(lands), not sems_in

    def wrapped(*refs):
        sems = refs[nb + na:nb + na + 2] if starts else refs[nb:nb + 2]
        made = refs[nb + na + 2 + nb:nb + na + 2 + nb + nl] if starts else ()
        body(tuple(refs[:nb]) + tuple(made), sems[0], sems[1])
        if token:
            refs[-1][...] = jnp.zeros_like(refs[-1])

    out_shape = [pltpu.SemaphoreType.DMA((n_sems,)) for _ in range(2 if starts else 0)]
    out_specs = [SEM] * len(out_shape) + [ANY] * (nb + nl)
    alias = {b: len(out_shape) + b for b in range(nb)}
    out_shape += [jax.ShapeDtypeStruct(b.shape, b.dtype) for b in bufs] + list(lands)
    if token:
        out_shape.append(jax.ShapeDtypeStruct((8, LANES), F32))
        out_specs.append(pl.BlockSpec(memory_space=pltpu.VMEM))
    return pl.pallas_call(
        wrapped, out_shape=out_shape, in_specs=[ANY] * nb + [SEM] * len(sems_in) + [ANY] * na, out_specs=out_specs,
        input_output_aliases=alias,
        compiler_params=pltpu.CompilerParams(has_side_effects=pltpu.SideEffectType.DATAFLOW_SIDE_EFFECTING),
        name=name)(*bufs, *sems_in, *after)


def _gather_start(bufs, *, name, after=()):
    nb = len(bufs)

    def body(ins, sends, recvs):
        x, y, c = lax.axis_index("x"), lax.axis_index("y"), lax.axis_index("c")
        chips = [(1 - x, y), (x, 1 - y), (1 - x, 1 - y)]
        for b in range(nb):
            _gather_copy(ins, sends, recvs, b, 0, (x, y), c, (x, y, 1 - c)).start()
            for j, chip in enumerate(chips):
                _gather_copy(ins, sends, recvs, b, 1 + j, (x, y), c, (*chip, c)).start()

    out = _split_call(body, bufs, (), 4 * nb, name=name, after=after, token=True)
    return (out[0], out[1], out[2:2 + nb]), out[-1]


def _gather_wait_first(flight, *, name, after=()):
    sends, recvs, bufs = flight
    nb = len(bufs)

    def body(ins, sends_, recvs_):
        x, y, c = lax.axis_index("x"), lax.axis_index("y"), lax.axis_index("c")
        chips = [(1 - x, y), (x, 1 - y), (1 - x, 1 - y)]
        for b in range(nb):
            _gather_copy(ins, sends_, recvs_, b, 0, (x, y), c, (x, y, 1 - c)).wait_send()
            _gather_copy(ins, sends_, recvs_, b, 0, (x, y), 1 - c, (x, y, c)).wait_recv()
            for j, chip in enumerate(chips):
                _gather_copy(ins, sends_, recvs_, b, 1 + j, (x, y), c, (*chip, c)).wait_send()
                _gather_copy(ins, sends_, recvs_, b, 1 + j, chip, c, (x, y, c)).wait_recv()

    return _split_call(body, bufs, (sends, recvs), 4 * nb, name=name, after=after)


def _gather_forward(bufs, *, name):
    nb = len(bufs)

    def body(ins, sends, recvs):
        x, y, c = lax.axis_index("x"), lax.axis_index("y"), lax.axis_index("c")
        chips = [(1 - x, y), (x, 1 - y), (1 - x, 1 - y)]
        for b in range(nb):
            for j, chip in enumerate(chips):
                _gather_copy(ins, sends, recvs, b, 1 + j, chip, c, (x, y, 1 - c)).start()

    out = _split_call(body, bufs, (), 4 * nb, name=name)
    return out[0], out[1], out[2:2 + nb]


def _gather_wait_forward(flight, *, name, after=()):
    sends, recvs, bufs = flight
    nb = len(bufs)

    def body(ins, sends_, recvs_):
        x, y, c = lax.axis_index("x"), lax.axis_index("y"), lax.axis_index("c")
        chips = [(1 - x, y), (x, 1 - y), (1 - x, 1 - y)]
        for b in range(nb):
            for j, chip in enumerate(chips):
                _gather_copy(ins, sends_, recvs_, b, 1 + j, chip, c, (x, y, 1 - c)).wait_send()
                _gather_copy(ins, sends_, recvs_, b, 1 + j, chip, 1 - c, (x, y, c)).wait_recv()

    return _split_call(body, bufs, (sends, recvs), 4 * nb, name=name, after=after)


def _owner_copies(hs, lands, sends, recvs):
    x, y, c = lax.axis_index("x"), lax.axis_index("y"), lax.axis_index("c")
    chips = [(1 - x, y), (x, 1 - y), (1 - x, 1 - y)]
    return [pltpu.make_async_remote_copy(src_ref=hs[b].at[2 * cx + cy], dst_ref=lands[b].at[j], send_sem=sends.at[3 * b + j],
                                         recv_sem=recvs.at[3 * b + j], device_id=(cx, cy, c), device_id_type=MESH)
            for b in range(len(hs)) for j, (cx, cy) in enumerate(chips)]


def _owners_start(hs, *, name):
    nb = len(hs)
    lands = [jax.ShapeDtypeStruct((3,) + h.shape[1:], h.dtype) for h in hs]

    def body(refs, sends, recvs):
        for cp in _owner_copies(refs[:nb], refs[nb:], sends, recvs):
            cp.start()

    out = _split_call(body, list(hs), (), 3 * nb, name=name, token=True, lands=lands)
    return (out[0], out[1], out[2:2 + 2 * nb]), out[-1]


def _owners_wait(flight, *, name, after=()):
    sends, recvs, bufs = flight
    nb = len(bufs) // 2

    def body(refs, sends_, recvs_):
        for cp in _owner_copies(refs[:nb], refs[nb:], sends_, recvs_):
            cp.wait()

    out = _split_call(body, bufs, (sends, recvs), 3 * nb, name=name, after=after)
    return out[:nb], out[nb:]


def _sibling_copies(gs, lands, sends, recvs):
    x, y, c = lax.axis_index("x"), lax.axis_index("y"), lax.axis_index("c")
    copies = []
    for b in range(len(gs)):
        hr = gs[b].shape[1] // 2
        copies.append(pltpu.make_async_remote_copy(
            src_ref=gs[b].at[:, pl.ds((1 - c) * hr, hr), :], dst_ref=lands[b], send_sem=sends.at[b], recv_sem=recvs.at[b],
            device_id=(x, y, 1 - c), device_id_type=MESH))
    return copies


def _sibling_start(gs, *, name, after=()):
    nb = len(gs)
    lands = [jax.ShapeDtypeStruct((g.shape[0], g.shape[1] // 2, g.shape[2]), g.dtype) for g in gs]

    def body(refs, sends, recvs):
        for cp in _sibling_copies(refs[:nb], refs[nb:], sends, recvs):
            cp.start()

    out = _split_call(body, list(gs), (), nb, name=name, after=after, token=True, lands=lands)
    return (out[0], out[1], out[2:2 + 2 * nb]), out[-1]


def _sibling_wait(flight, *, name, after=()):
    sends, recvs, bufs = flight
    nb = len(bufs) // 2

    def body(refs, sends_, recvs_):
        for cp in _sibling_copies(refs[:nb], refs[nb:], sends_, recvs_):
            cp.wait()

    out = _split_call(body, bufs, (sends, recvs), nb, name=name, after=after)
    return out[:nb], out[nb:]


def _result_copies(ts, sends, recvs):
    x, y, c = lax.axis_index("x"), lax.axis_index("y"), lax.axis_index("c")
    return [pltpu.make_async_remote_copy(src_ref=ts[b].at[c], dst_ref=ts[b].at[c], send_sem=sends.at[b], recv_sem=recvs.at[b],
                                         device_id=(x, y, 1 - c), device_id_type=MESH) for b in range(len(ts))]


def _result_start(ts, *, name):
    def body(refs, sends, recvs):
        for cp in _result_copies(refs, sends, recvs):
            cp.start()

    out = _split_call(body, ts, (), len(ts), name=name, token=True)
    return (out[0], out[1], out[2:2 + len(ts)]), out[-1]


def _result_wait(flight, *, name, after=()):
    sends, recvs, bufs = flight

    def body(refs, sends_, recvs_):
        for cp in _result_copies(refs, sends_, recvs_):
            cp.wait()

    return _split_call(body, bufs, (sends, recvs), len(bufs), name=name, after=after)


def _row_tile(rows, cols):
    best = 16
    for t in range(16, rows + 1, 16):
        if rows % t == 0 and t * cols <= 640 * 1024:
            best = t
    assert rows % best == 0, (rows, cols)
    return best


def _add_sibling_half(g, recv, core, *, name):
    nk, r, n = g.shape
    hr = r // 2
    tr = _row_tile(hr, n)

    def body(c_ref, a_ref, b_ref, o_ref):
        o_ref[...] = (a_ref[...].astype(F32) + b_ref[...].astype(F32)).astype(BF16)

    grid_spec = pltpu.PrefetchScalarGridSpec(
        num_scalar_prefetch=1, grid=(nk, hr // tr),
        in_specs=[pl.BlockSpec((None, tr, n), lambda k, i, c_ref: (k, c_ref[0] * (hr // tr) + i, 0)),
                  pl.BlockSpec((None, tr, n), lambda k, i, c_ref: (k, i, 0))],
        out_specs=pl.BlockSpec((None, tr, n), lambda k, i, c_ref: (k, i, 0)))
    return pl.pallas_call(body, grid_spec=grid_spec, out_shape=jax.ShapeDtypeStruct((nk, hr, n), BF16),
                          compiler_params=_params(("parallel", "parallel")), name=name)(core, g, recv)


def _add_chip_sums(h, recv, chip_core, *, name):
    _, hr, n = h.shape
    tr = _row_tile(hr, n)

    def body(k_ref, a_ref, b_ref, o_ref):
        o_ref[...] = ((a_ref[...].astype(F32) + b_ref[0].astype(F32)) + b_ref[1].astype(F32)) + b_ref[2].astype(F32)

    grid_spec = pltpu.PrefetchScalarGridSpec(
        num_scalar_prefetch=1, grid=(hr // tr,),
        in_specs=[pl.BlockSpec((None, tr, n), lambda i, k_ref: (k_ref[0], i, 0)),
                  pl.BlockSpec((3, tr, n), lambda i, k_ref: (0, i, 0))],
        out_specs=pl.BlockSpec((None, tr, n), lambda i, k_ref: (k_ref[1], i, 0)))
    return pl.pallas_call(body, grid_spec=grid_spec, out_shape=jax.ShapeDtypeStruct((2, hr, n), F32),
                          compiler_params=_params(("parallel",)), name=name)(chip_core, h, recv)


def _sum_devices(g, *, name):
    nd, r, n = g.shape

    def body(g_ref, o_ref):
        acc = g_ref[0]
        for i in range(1, nd):
            acc = acc + g_ref[i]
        o_ref[...] = acc

    return pl.pallas_call(body, out_shape=jax.ShapeDtypeStruct((r, n), F32), name=name)(g)


def _own_slot(parts, chip, *, name, after=()):
    rows, cols = sum(w.shape[1] for w, _ in parts), parts[0][0].shape[2]
    buf, row0 = None, 0
    for p, (w, idx) in enumerate(parts):
        r = w.shape[1]
        tr = 256 if r % 256 == 0 else r
        assert row0 % tr == 0, (name, r, row0)
        prev = () if buf is None else (buf,)

        def body(chip_ref, w_ref, *rest):
            rest[-1][...] = w_ref[...].astype(BF16)

        grid_spec = pltpu.PrefetchScalarGridSpec(
            num_scalar_prefetch=1, grid=(r // tr,),
            in_specs=[pl.BlockSpec((None, tr, cols), lambda i, c_ref, idx=idx: (idx, i, 0))] + [ANY] * (len(prev) + len(after)),
            out_specs=pl.BlockSpec((None, tr, cols), lambda i, c_ref, row0=row0, tr=tr: (c_ref[0], row0 // tr + i, 0)))
        buf = pl.pallas_call(body, grid_spec=grid_spec, out_shape=jax.ShapeDtypeStruct((N_CHIPS, rows, cols), BF16),
                             input_output_aliases={2: 0} if prev else {}, compiler_params=_params(("parallel",)),
                             name=f"{name}{p}")(chip, w, *prev, *after)
        row0 += r
    return buf


def kernel(x, c, ada_w, ada_b, mix_norm_w, mlp_norm_w, mlp_up, mlp_down, ssd_in_w, ssd_conv_w, ssd_conv_b, ssd_dt_bias, ssd_A_log, ssd_D, ssd_norm_w, ssd_out_w, sc_in_w, sc_conv_w, sc_out_w, final_norm_w, loss_target, m_ada_w, m_ada_b, m_mix_norm_w, m_mlp_norm_w, m_mlp_up, m_mlp_down, m_ssd_in_w, m_ssd_conv_w, m_ssd_conv_b, m_ssd_dt_bias, m_ssd_A_log, m_ssd_D, m_ssd_norm_w, m_ssd_out_w, m_sc_in_w, m_sc_conv_w, m_sc_out_w, m_final_norm_w, v_ada_w, v_ada_b, v_mix_norm_w, v_mlp_norm_w, v_mlp_up, v_mlp_down, v_ssd_in_w, v_ssd_conv_w, v_ssd_conv_b, v_ssd_dt_bias, v_ssd_A_log, v_ssd_D, v_ssd_norm_w, v_ssd_out_w, v_sc_in_w, v_sc_conv_w, v_sc_out_w, v_final_norm_w):
    xi, yi, ci = lax.axis_index("x"), lax.axis_index("y"), lax.axis_index("c")
    chip = 2 * xi + yi
    dev = 2 * chip + ci
    n_ada = ada_w.shape[2]

    conv_flat = jnp.concatenate([ssd_conv_w.reshape(-1), sc_conv_w.reshape(-1), jnp.zeros((256,), F32)]).reshape(4, D)
    blk0 = jnp.concatenate([c, conv_flat, jnp.zeros((3, D), F32)], axis=0)
    got0 = _all_gather_rows(blk0, name="gather_cond").reshape(N_DEV, 8, D)
    c_all = got0[:, 0]
    conv_all = got0[0::2, 1:5].reshape(N_CHIPS, 4 * D)
    ssd_conv = jnp.moveaxis(conv_all[:, :4 * 768].reshape(N_CHIPS, 4, 768), 0, 1).reshape(4, CONVD)
    sc_conv = jnp.moveaxis(conv_all[:, 4 * 768:4 * 768 + 3 * 256].reshape(N_CHIPS, 3, 256), 0, 1).reshape(3, D)
    mod_shard = [_matmul(c_all, ada_w, n=n_ada, a_silu=True, b_spec=pl.BlockSpec((None, D, 512), lambda mi, j, i=i: (i, 0, j)),
                         extras=(lax.dynamic_slice(ada_b, (i, chip * n_ada), (1, n_ada)),),
                         epi=lambda acc, b: (acc + b,), name=f"ada_mod{i}") for i in range(2)]
    mod_slot = lax.dynamic_update_slice(jnp.zeros((N_CHIPS, 2 * N_DEV, n_ada), F32), jnp.concatenate(mod_shard, axis=0)[None],
                                        (chip, 0, 0))

    up_row, down_row = 0, D
    chip1 = chip.reshape(1).astype(jnp.int32)
    a_bufs = [mod_slot, _own_slot([(jnp.swapaxes(ssd_in_w, 1, 2), 0)], chip1, name="slot_ssd_in")]
    fly_a, tok = _gather_start(a_bufs, name="gather_a_start")
    b_bufs = [_own_slot([(ssd_out_w, 0)], chip1, name="slot_ssd_out", after=(tok,)),
              _own_slot([(mlp_up, 0), (mlp_down, 0)], chip1, name="slot_mlp0_", after=(tok,))]
    fly_b, tok = _gather_start(b_bufs, name="gather_b_start", after=(tok,))
    c_bufs = [_own_slot([(sc_in_w, 0)], chip1, name="slot_sc_in", after=(tok,)),
              _own_slot([(sc_out_w, 0)], chip1, name="slot_sc_out", after=(tok,))]
    fly_c, tok = _gather_start(c_bufs, name="gather_c_start", after=(tok,))
    d_bufs = [_own_slot([(mlp_up, 1), (mlp_down, 1)], chip1, name="slot_mlp1_", after=(tok,))]
    fly_d, tok = _gather_start(d_bufs, name="gather_d_start", after=(tok,))

    row = lambda v: v.reshape(1, -1)
    xs, tgt = x[0], loss_target[0]
    prm = jnp.pad(jnp.concatenate([ssd_dt_bias, ssd_A_log, ssd_D, jnp.zeros((5, NH), F32)], axis=0), ((0, 0), (0, LANES - NH)))
    mix_nw = [row(mix_norm_w[i]) for i in range(2)]
    mlp_nw = [row(mlp_norm_w[i]) for i in range(2)]
    a_bufs = _gather_wait_first(fly_a, name="gather_a_landed", after=(tok,))
    mod_all, w_ssd_in = _gather_wait_forward(_gather_forward(a_bufs, name="gather_a_pass"), name="gather_a_done")
    mod = lax.dynamic_index_in_dim(mod_all.reshape(N_CHIPS, 2, N_DEV, n_ada), dev, axis=2, keepdims=False)
    mod = jnp.moveaxis(mod, 0, 1).reshape(2, 6, D)
    mods = [[mod[i, j:j + 1] for j in range(6)] for i in range(2)]
    w_in_t = w_ssd_in.reshape(N_CHIPS * SSD_IN_SHARD, D)
    w_dt_t = jnp.pad(w_in_t[ZX:], ((0, LANES - NH), (0, 0)))
    scan = _ssd_fwd_scan(xs, mods[0][0:3], mix_nw[0], w_in_t, w_dt_t, ssd_conv, ssd_conv_b, prm, "ssd")

    def land(flight, tag, after):
        return _gather_forward(_gather_wait_first(flight, name=f"gather_{tag}_landed", after=(after,)), name=f"gather_{tag}_pass")

    passed, got = {"b": land(fly_b, "b", scan[4])}, {}

    def done(tag, after):
        got[tag] = _gather_wait_forward(passed[tag], name=f"gather_{tag}_done", after=(after,))
        return got[tag]

    x1, s_ssd = _ssd_fwd_out(xs, mods[0][0:3], scan, ssd_norm_w, lambda yn: done("b", yn)[0], "ssd")
    w_ssd_out, w_b = got["b"]
    x2, s_mlp0 = _mlp_fwd(x1, mods[0][3:6], mlp_nw[0], w_b, up_row, down_row, "mlp0",
                          midway=lambda a: passed.update(c=land(fly_c, "c", a)))
    w_sc_in, w_sc_out = done("c", x2)
    x3, s_sc = _sc_layer_fwd(x2, mods[1][0:3], mix_nw[1], w_sc_in, sc_conv, w_sc_out, 0, "sc",
                             midway=lambda proj: passed.update(d=land(fly_d, "d", proj)))
    (w_mlp1,) = done("d", x3)
    x4, s_mlp1 = _mlp_fwd(x3, mods[1][3:6], mlp_nw[1], w_mlp1, up_row, down_row, "mlp1")

    core = ci.reshape(1).astype(jnp.int32)
    chip_core = jnp.stack([chip, ci]).astype(jnp.int32)

    def reduce_swap(gbufs, tag, after=()):
        return _sibling_start(gbufs, name=tag + "_sibling_start", after=after)

    def reduce_send(flight, tag, after):
        gs, sib = _sibling_wait(flight, name=tag + "_sibling_landed", after=after)
        hs = [_add_sibling_half(g, s, core, name=f"{tag}_add_sibling{b}") for b, (g, s) in enumerate(zip(gs, sib))]
        return _owners_start(hs, name=tag + "_owners_start")

    def reduce_sum(flight, tag, after):
        hs, lands = _owners_wait(flight, name=tag + "_owners_landed", after=after)
        ts = [_add_chip_sums(h, o, chip_core, name=f"{tag}_add_chips{b}") for b, (h, o) in enumerate(zip(hs, lands))]
        return _result_start(ts, name=tag + "_result_start")

    def reduce_done(flight, tag, after=()):
        return [t.reshape(-1, t.shape[2]) for t in _result_wait(flight, name=tag + "_result_landed", after=after)]

    dx4, fsum, dy, gs = _final_loss(x4, row(final_norm_w), tgt, (mods[1][5], s_mlp1[3]), name="final_loss")
    dx3, g_mlp1, sum_mlp1, dy, gs = _mlp_bwd(dx4, dy, gs, s_mlp1, mods[1][3:6], mlp_nw[1], w_mlp1, None, up_row, down_row,
                                             (mods[1][2], s_sc[4]), "mlp1")
    dx2, g_sc_out, g_sc_in, sum_sc, sc_csum, dy, gs = _sc_layer_bwd(dx3, dy, gs, s_sc, mods[1][0:3], mix_nw[1], w_sc_in,
                                                                    sc_conv, w_sc_out, None, 0, (mods[0][5], s_mlp0[3]), "sc")
    dx1, g_b, sum_mlp0, dy, gsum_ssd = _mlp_bwd(dx2, dy, gs, s_mlp0, mods[0][3:6], mlp_nw[0], w_b, None, up_row, down_row,
                                                (mods[0][2], s_ssd[8]), "mlp0")
    dyn, g_ssd_out = _ssd_bwd_out(dy, s_ssd, w_ssd_out, "ssd")
    fly_1, tok = reduce_swap([g_mlp1, g_sc_out, g_sc_in, g_b, g_ssd_out], "rs1")
    dy, dzx, gnsum = _gnorm_bwd(s_ssd[5], s_ssd[2], ssd_norm_w + tok[0:1, 0:1], dyn, name="ssd_dgnorm")
    fly_1, tok = reduce_send(fly_1, "rs1", (dy,))
    grad_x, d_w_zx, d_w_dt, sum_ssd, csum, ssum = _ssd_bwd_rest(
        dx1, dy, dzx, gsum_ssd, s_ssd, mods[0][0:3], mix_nw[0], w_in_t, w_dt_t, ssd_conv, prm + tok[0:1, 0:1], "ssd")

    def ssd_in_owner(k):
        lo, hi = k * SSD_IN_SHARD, (k + 1) * SSD_IN_SHARD
        if hi <= ZX:
            return d_w_zx[:, lo:hi]
        return jnp.concatenate([d_w_zx[:, lo:], d_w_dt[:, :hi - ZX]], axis=1)

    small = jnp.concatenate([sum_ssd, sum_mlp0, sum_sc, sum_mlp1, csum.reshape(24, D)[0:16], gnsum.reshape(16, D)[0:8],
                             fsum, sc_csum, jnp.pad(ssum, ((0, 0), (0, D - LANES)))], axis=0)
    small_slot = lax.dynamic_update_slice(jnp.zeros((N_CHIPS, 2 * SMALL_ROWS, D), F32), small[None], (chip, ci * SMALL_ROWS, 0))
    fly_2, tok = reduce_swap([jnp.stack([ssd_in_owner(k) for k in range(N_CHIPS)]).astype(BF16)], "rs2")
    fly_s, tok = _gather_start([small_slot], name="gather_small_start", after=(tok,))
    fly_1, tok = reduce_sum(fly_1, "rs1", (grad_x, tok))
    fly_2, tok = reduce_send(fly_2, "rs2", (tok,))
    fly_s = _gather_forward(_gather_wait_first(fly_s, name="gather_small_landed", after=(tok,)), name="gather_small_pass")
    (small_all,) = _gather_wait_forward(fly_s, name="gather_small_done")
    t_mlp1, t_sc_out, t_sc_in, t_b, t_ssd_out = reduce_done(fly_1, "rs1", (small_all,))
    small_all = small_all.reshape(N_DEV, SMALL_ROWS, D) + tok[0:1, 0:1]
    tot = _sum_devices(small_all, name="sum_small")
    loss = tot[FINAL_ROW + 1, 0]
    mod_rows = [r + o for r in SUB_ROW for o in (3, 2, 0)]
    g_ada_b = jnp.stack([tot[r] for r in mod_rows]).reshape(2, 6 * D)
    g_mix_norm = jnp.stack([tot[SUB_ROW[0] + 1], tot[SUB_ROW[2] + 1]])
    g_mlp_norm = jnp.stack([tot[SUB_ROW[1] + 1], tot[SUB_ROW[3] + 1]])
    conv_sums = tot[SSD_CONV_ROW:SSD_CONV_ROW + 15].reshape(5, CONVD)
    g_ssd_conv_w = lax.dynamic_slice(conv_sums, (0, chip * 768), (4, 768))[None]
    g_ssd_conv_b = conv_sums[4:5]
    g_ssd_norm = tot[GNORM_ROW:GNORM_ROW + 2].reshape(1, DI)
    g_final = tot[FINAL_ROW]
    g_sc_conv_w = lax.dynamic_slice(tot[SC_CONV_ROW:SC_CONV_ROW + 3], (0, chip * 256), (3, 256))[None]
    g_a_log, g_d, g_dt_bias = (tot[HEAD_ROW + r:HEAD_ROW + r + 1, 0:NH] for r in range(3))
    c_pad = jnp.concatenate([c_all, jnp.zeros((8, D), F32)], axis=0)
    dmod_all = jnp.stack([small_all[:, r] for r in mod_rows], axis=1).reshape(N_DEV, 2, 6 * D)
    g_ada_w = []
    for i in range(2):
        dm = lax.dynamic_slice(dmod_all[:, i], (0, chip * n_ada), (N_DEV, n_ada))
        g_ada_w.append(_matmul_tn(c_pad, jnp.concatenate([dm, jnp.zeros_like(dm)], axis=0), m=D, n=n_ada, a_silu=True,
                                  name=f"ada_dw{i}"))

    big = dict(ada_w=[(g, 0) for g in g_ada_w], mlp_up=[(t_b, up_row), (t_mlp1, up_row)],
               mlp_down=[(t_b, down_row), (t_mlp1, down_row)], ssd_out_w=[(t_ssd_out, 0)], sc_out_w=[(t_sc_out, 0)],
               sc_in_w=[(t_sc_in, 0)], ssd_in_w=None)
    grads = dict(ada_b=g_ada_b, mix_norm_w=g_mix_norm, mlp_norm_w=g_mlp_norm, ssd_conv_w=g_ssd_conv_w,
                 ssd_conv_b=g_ssd_conv_b, ssd_dt_bias=g_dt_bias, ssd_A_log=g_a_log, ssd_D=g_d, ssd_norm_w=g_ssd_norm,
                 sc_conv_w=g_sc_conv_w, final_norm_w=g_final)
    weights = dict(ada_w=(ada_w, m_ada_w, v_ada_w), ada_b=(ada_b, m_ada_b, v_ada_b),
                   mix_norm_w=(mix_norm_w, m_mix_norm_w, v_mix_norm_w), mlp_norm_w=(mlp_norm_w, m_mlp_norm_w, v_mlp_norm_w),
                   mlp_up=(mlp_up, m_mlp_up, v_mlp_up), mlp_down=(mlp_down, m_mlp_down, v_mlp_down),
                   ssd_in_w=(ssd_in_w, m_ssd_in_w, v_ssd_in_w), ssd_conv_w=(ssd_conv_w, m_ssd_conv_w, v_ssd_conv_w),
                   ssd_conv_b=(ssd_conv_b, m_ssd_conv_b, v_ssd_conv_b), ssd_dt_bias=(ssd_dt_bias, m_ssd_dt_bias, v_ssd_dt_bias),
                   ssd_A_log=(ssd_A_log, m_ssd_A_log, v_ssd_A_log), ssd_D=(ssd_D, m_ssd_D, v_ssd_D),
                   ssd_norm_w=(ssd_norm_w, m_ssd_norm_w, v_ssd_norm_w), ssd_out_w=(ssd_out_w, m_ssd_out_w, v_ssd_out_w),
                   sc_in_w=(sc_in_w, m_sc_in_w, v_sc_in_w), sc_conv_w=(sc_conv_w, m_sc_conv_w, v_sc_conv_w),
                   sc_out_w=(sc_out_w, m_sc_out_w, v_sc_out_w), final_norm_w=(final_norm_w, m_final_norm_w, v_final_norm_w))
    def step(nm, parts):
        w, m, v = (t if t.shape[0] == 1 else t.reshape(-1, t.shape[-1]) for t in weights[nm])
        rows, outs = w.shape[-2] // len(parts), None
        for i, (gbuf, g_row) in enumerate(parts):
            outs = _adamw(w, gbuf, m, v, g_row=g_row, w_row=i * rows, rows=rows, into=outs, emit_g=True, name=f"adamw_{nm}{i}")
        return outs

    res = {}
    for nm, (w, m, v) in weights.items():
        two_d = (-1, w.shape[-1]) if w.ndim > 1 else (1, -1)
        if nm not in big:
            res[nm] = (grads[nm], *_adamw(w.reshape(two_d), grads[nm].reshape(two_d), m.reshape(two_d), v.reshape(two_d),
                                          name="adamw_" + nm))
        elif big[nm] is not None:
            res[nm] = step(nm, big[nm])
    fly_2, tok = reduce_sum(fly_2, "rs2", tuple(r[1] for r in res.values()))
    (t_ssd_in,) = reduce_done(fly_2, "rs2", (tok,))
    w_t, m_t, v_t = (jnp.swapaxes(t[0], 0, 1) for t in weights["ssd_in_w"])
    res["ssd_in_w"] = [jnp.swapaxes(o, 0, 1) for o in _adamw(w_t, t_ssd_in.T, m_t, v_t, emit_g=True, name="adamw_ssd_in_w")]
    outs = [[res[nm][k].reshape(weights[nm][0].shape) for nm in weights] for k in range(4)]
    return (loss, grad_x[None], *outs[0], *outs[1], *outs[2], *outs[3])
```

```python
import jax
import jax.numpy as jnp
from jax import lax
from jax.experimental import pallas as pl
from jax.experimental.pallas import tpu as pltpu

F32 = jnp.float32
BF16 = jnp.bfloat16
MESH = pl.DeviceIdType.MESH

D = 1024
DFF = 4096
DI = 2048
NH = 32
HP = 64
NG = 4
NS = 128
CH = 128
CONVD = DI + 2 * NG * NS
ZX = DI + CONVD
GW = NG * NS
LANES = 128
N_CHIPS = 4
N_DEV = 8
EPS = 1e-5
ADAM_LR, ADAM_B1, ADAM_B2, ADAM_EPS, ADAM_WD, ADAM_STEP = 1e-3, 0.9, 0.999, 1e-8, 0.01, 10
VMEM_LIMIT = 48 * 1024 * 1024
TM_ALL = 2048
TM_HALF = 1024
ANY = pl.BlockSpec(memory_space=pl.ANY)
SEM = pl.BlockSpec(memory_space=pltpu.SEMAPHORE)

SSD_IN_SHARD = 1288
SC_IN_SHARD = 768


def _params(sem=None):
    return pltpu.CompilerParams(dimension_semantics=sem, vmem_limit_bytes=VMEM_LIMIT)


def _sigmoid(v):
    return 0.5 * jnp.tanh(0.5 * v) + 0.5


def _dot(a, b, dims=((1,), (0,)), precision=None):
    return lax.dot_general(a, b, (dims, ((), ())), preferred_element_type=F32, precision=precision)


def _dot_nt(a, b):
    return _dot(a, b, ((1,), (1,)))


def _dot_tn(a, b):
    return _dot(a, b, ((0,), (0,)))


def _nn(av, bv):
    return _dot(av.astype(BF16), bv.astype(BF16))


def _nt(av, bv):
    return _dot_nt(av.astype(BF16), bv.astype(BF16))


def _nn_split(av, bv):
    return _dot(av.astype(BF16), bv.reshape(-1, bv.shape[2]))


def _nn_split_sq(av, bv):
    return _nn_split(av * av, bv)


def _nt_split(av, bv):
    kc = bv.shape[2]
    acc = _dot_nt(av[:, 0:kc].astype(BF16), bv[0])
    for s in range(1, bv.shape[0]):
        acc = acc + _dot_nt(av[:, s * kc:(s + 1) * kc].astype(BF16), bv[s])
    return acc


def _nt_sc_in(av, bv):
    q = 256
    acc = None
    for i in range(3 * D // q):
        a_blk = av[i // 4][:, (i % 4) * q:(i % 4 + 1) * q]
        b_blk = bv[i // 3][:, (i % 3) * q:(i % 3 + 1) * q]
        t = _dot_nt(a_blk, b_blk)
        acc = t if acc is None else acc + t
    return acc


def _matmul(a, b, *, name, n, contract=_nn, a_spec=None, b_spec=None, tm=512, tn=512, extras=(), epi=None,
            out_dtypes=(F32,), a_silu=False):
    M = a.shape[-2]
    tm, tn = min(tm, M), min(tn, n)
    assert M % tm == 0 and n % tn == 0, (name, M, n, tm, tn)
    n_ex = len(extras)
    if a_spec is None:
        a_spec = pl.BlockSpec((tm, a.shape[1]), lambda i, j: (i, 0))
    if b_spec is None:
        b_spec = (pl.BlockSpec((tn, b.shape[1]), lambda i, j: (j, 0)) if contract is _nt
                  else pl.BlockSpec((b.shape[0], tn), lambda i, j: (0, j)))

    def body(*refs):
        av = refs[0][...]
        if a_silu:
            av = av * _sigmoid(av)
        acc = contract(av, refs[1][...])
        res = epi(acc, *[r[...] for r in refs[2:2 + n_ex]]) if epi is not None else (acc,)
        for o_ref, r in zip(refs[2 + n_ex:], res, strict=True):
            o_ref[...] = r.astype(o_ref.dtype)

    in_specs = [a_spec, b_spec]
    for e in extras:
        in_specs.append(pl.BlockSpec((1, tn), lambda i, j: (0, j)) if e.shape[0] == 1 and M != 1
                        else pl.BlockSpec((tm, tn), lambda i, j: (i, j)))
    outs = pl.pallas_call(
        body, grid=(M // tm, n // tn), in_specs=in_specs,
        out_specs=[pl.BlockSpec((tm, tn), lambda i, j: (i, j)) for _ in out_dtypes],
        out_shape=[jax.ShapeDtypeStruct((M, n), dt) for dt in out_dtypes],
        compiler_params=_params(("parallel", "parallel")), name=name)(a, b, *extras)
    return outs if len(out_dtypes) > 1 else outs[0]


def _matmul_tn(a, b, *, name, m, n, tm=512, tn=512, a_spec=None, b_spec=None, out_spec=None, out_struct=None, into=None,
               a_silu=False, a_square=False):
    T = a.shape[-2]
    tm, tn = min(tm, m), min(tn, n)
    assert m % tm == 0 and n % tn == 0, (name, m, n, tm, tn)
    if a_spec is None:
        a_spec = pl.BlockSpec((T, tm), lambda i, j: (0, i))
    if b_spec is None:
        b_spec = pl.BlockSpec((T, tn), lambda i, j: (0, j))
    if out_spec is None:
        out_spec, out_struct = pl.BlockSpec((tm, tn), lambda i, j: (i, j)), jax.ShapeDtypeStruct((m, n), F32)

    def body(a_ref, b_ref, *rest):
        av = a_ref[...]
        if a_silu:
            av = av * _sigmoid(av)
        if a_square:
            av = av * av
        rest[-1][...] = _dot_tn(av.astype(BF16), b_ref[...].astype(BF16)).astype(rest[-1].dtype)

    args, in_specs, alias = [a, b], [a_spec, b_spec], {}
    if into is not None:
        args, in_specs, alias = args + [into], in_specs + [ANY], {2: 0}
    return pl.pallas_call(body, grid=(m // tm, n // tn), in_specs=in_specs, out_specs=out_spec, out_shape=out_struct,
                          input_output_aliases=alias, compiler_params=_params(("parallel", "parallel")), name=name)(*args)


def _modnorm_fwd(x, nw, sc, sh, *, name):
    L = x.shape[0]
    tm = min(L, 512)

    def body(x_ref, nw_ref, sc_ref, sh_ref, h_ref):
        xv = x_ref[...]
        r = lax.rsqrt(jnp.mean(xv * xv, axis=-1, keepdims=True) + EPS)
        h_ref[...] = ((xv * r * nw_ref[...]) * (1.0 + sc_ref[...]) + sh_ref[...]).astype(BF16)

    row = pl.BlockSpec((tm, D), lambda i: (i, 0))
    vec = pl.BlockSpec((1, D), lambda i: (0, 0))
    return pl.pallas_call(body, grid=(L // tm,), in_specs=[row, vec, vec, vec], out_specs=row,
                          out_shape=jax.ShapeDtypeStruct((L, D), BF16),
                          compiler_params=_params(("parallel",)), name=name)(x, nw, sc, sh)


def _gate_outputs(dx, below_refs, dy_ref, gs_ref):
    g_ref, y_ref = below_refs
    dy_ref[...] = (dx * g_ref[...]).astype(BF16)
    gs_ref[0:1, :] += jnp.sum(dx * y_ref[...].astype(F32), axis=0, keepdims=True)


def _modnorm_bwd(x, dh, dxo, nw, sc, gsum, below, *, name):
    L = x.shape[0]
    tm = min(L, 256)
    nb = 0 if below is None else 2

    def body(x_ref, dh_ref, dxo_ref, nw_ref, sc_ref, g_ref, *rest):
        dx_ref, s_ref = rest[nb:nb + 2]

        @pl.when(pl.program_id(0) == 0)
        def _():
            s_ref[...] = g_ref[...]
            if nb:
                rest[-1][...] = jnp.zeros_like(rest[-1])

        xv, dhv = x_ref[...], dh_ref[...].astype(F32)
        r = lax.rsqrt(jnp.mean(xv * xv, axis=-1, keepdims=True) + EPS)
        xhat = xv * r
        dxhat = dhv * (nw_ref[...] * (1.0 + sc_ref[...]))
        dx = dxo_ref[...] + r * (dxhat - xhat * jnp.mean(dxhat * xhat, axis=-1, keepdims=True))
        dx_ref[...] = dx
        s_ref[1:2, :] += jnp.sum(dhv * xhat, axis=0, keepdims=True) * (1.0 + sc_ref[...])
        s_ref[2:3, :] += jnp.sum(dhv * xhat, axis=0, keepdims=True) * nw_ref[...]
        s_ref[3:4, :] += jnp.sum(dhv, axis=0, keepdims=True)
        if nb:
            _gate_outputs(dx, rest[:nb], rest[-2], rest[-1])

    row = pl.BlockSpec((tm, D), lambda i: (i, 0))
    vec = pl.BlockSpec((1, D), lambda i: (0, 0))
    blk = pl.BlockSpec((8, D), lambda i: (0, 0))
    in_specs, out_specs = [row, row, row, vec, vec, blk], [row, blk]
    out_shape = [jax.ShapeDtypeStruct((L, D), F32), jax.ShapeDtypeStruct((8, D), F32)]
    if nb:
        in_specs, out_specs = in_specs + [vec, row], out_specs + [row, blk]
        out_shape += [jax.ShapeDtypeStruct((L, D), BF16), jax.ShapeDtypeStruct((8, D), F32)]
    return pl.pallas_call(body, grid=(L // tm,), in_specs=in_specs, out_specs=out_specs, out_shape=out_shape,
                          compiler_params=_params(("arbitrary",)), name=name)(x, dh, dxo, nw, sc, gsum, *(below or ()))


def _final_loss(x, fw, tgt, below, *, name):
    L = x.shape[0]
    tm = min(L, 256)

    def body(x_ref, fw_ref, t_ref, g_ref, y_ref, dx_ref, s_ref, dy_ref, gs_ref):
        @pl.when(pl.program_id(0) == 0)
        def _():
            s_ref[...] = jnp.zeros_like(s_ref)
            gs_ref[...] = jnp.zeros_like(gs_ref)

        xv = x_ref[...]
        r = lax.rsqrt(jnp.mean(xv * xv, axis=-1, keepdims=True) + EPS)
        xhat = xv * r
        diff = xhat * fw_ref[...] - t_ref[...]
        dout = diff * (1.0 / D)
        dxhat = dout * fw_ref[...]
        dx = r * (dxhat - xhat * jnp.mean(dxhat * xhat, axis=-1, keepdims=True))
        dx_ref[...] = dx
        s_ref[0:1, :] += jnp.sum(dout * xhat, axis=0, keepdims=True)
        s_ref[1:2, :] += jnp.zeros((1, D), F32) + 0.5 * jnp.sum(jnp.sum(diff * diff, axis=-1, keepdims=True) * (1.0 / D))
        _gate_outputs(dx, (g_ref, y_ref), dy_ref, gs_ref)

    row = pl.BlockSpec((tm, D), lambda i: (i, 0))
    vec = pl.BlockSpec((1, D), lambda i: (0, 0))
    blk = pl.BlockSpec((8, D), lambda i: (0, 0))
    return pl.pallas_call(body, grid=(L // tm,), in_specs=[row, vec, row, vec, row], out_specs=[row, blk, row, blk],
                          out_shape=[jax.ShapeDtypeStruct((L, D), F32), jax.ShapeDtypeStruct((8, D), F32),
                                     jax.ShapeDtypeStruct((L, D), BF16), jax.ShapeDtypeStruct((8, D), F32)],
                          compiler_params=_params(("arbitrary",)), name=name)(x, fw, tgt, *below)


def _shift_down(v, j):
    if j == 0:
        return v
    rolled = pltpu.roll(v, j, 0)
    row = lax.broadcasted_iota(jnp.int32, (8, v.shape[1]), 0)
    return jnp.concatenate([jnp.where(row >= j, rolled[0:8], 0.0), rolled[8:]], axis=0)


def _shift_up(v, j):
    if j == 0:
        return v
    n = v.shape[0]
    rolled = pltpu.roll(v, n - j, 0)
    row = lax.broadcasted_iota(jnp.int32, (8, v.shape[1]), 0)
    return jnp.concatenate([rolled[:n - 8], jnp.where(row < 8 - j, rolled[n - 8:], 0.0)], axis=0)


def _ssd_conv_fwd(zx, w, b, *, name):
    L = zx.shape[0]
    cb = 256
    k = w.shape[0]

    def body(x_ref, w_ref, b_ref, o_ref, p_ref):
        xv = x_ref[...].astype(F32)
        pre = b_ref[...] + xv * w_ref[k - 1:k, :]
        for j in range(1, k):
            pre = pre + _shift_down(xv, j) * w_ref[k - 1 - j:k - j, :]
        o_ref[...] = (pre * _sigmoid(pre)).astype(BF16)
        p_ref[...] = pre.astype(BF16)

    blk = pl.BlockSpec((L, cb), lambda i: (0, i))
    return pl.pallas_call(
        body, grid=(CONVD // cb,),
        in_specs=[pl.BlockSpec((L, cb), lambda i: (0, i + DI // cb)), pl.BlockSpec((k, cb), lambda i: (0, i)),
                  pl.BlockSpec((1, cb), lambda i: (0, i))],
        out_specs=[blk, blk], out_shape=[jax.ShapeDtypeStruct((L, CONVD), BF16)] * 2,
        compiler_params=_params(("parallel",)), name=name)(zx, w, b)


def _ssd_conv_bwd(zx, pre, dact, w, dzx, *, name):
    L = zx.shape[0]
    cb = 256
    k = w.shape[0]

    def body(x_ref, p_ref, da_ref, w_ref, _, dx_ref, s_ref):
        xv, pv = x_ref[...].astype(F32), p_ref[...].astype(F32)
        s = _sigmoid(pv)
        dpre = da_ref[...].astype(F32) * (s * (1.0 + pv * (1.0 - s)))
        s_ref[...] = jnp.zeros_like(s_ref)
        s_ref[k:k + 1, :] = jnp.sum(dpre, axis=0, keepdims=True)
        s_ref[k - 1:k, :] = jnp.sum(dpre * xv, axis=0, keepdims=True)
        dx = dpre * w_ref[k - 1:k, :]
        for j in range(1, k):
            later = _shift_up(dpre, j)
            dx = dx + later * w_ref[k - 1 - j:k - j, :]
            s_ref[k - 1 - j:k - j, :] = jnp.sum(later * xv, axis=0, keepdims=True)
        dx_ref[...] = dx.astype(BF16)

    blk = pl.BlockSpec((L, cb), lambda i: (0, i))
    return pl.pallas_call(
        body, grid=(CONVD // cb,),
        in_specs=[pl.BlockSpec((L, cb), lambda i: (0, i + DI // cb)), blk, blk, pl.BlockSpec((k, cb), lambda i: (0, i)), ANY],
        out_specs=[pl.BlockSpec((L, cb), lambda i: (0, i + DI // cb)), pl.BlockSpec((8, cb), lambda i: (0, i))],
        out_shape=[jax.ShapeDtypeStruct((L, ZX), BF16), jax.ShapeDtypeStruct((8, CONVD), F32)],
        input_output_aliases={4: 0}, compiler_params=_params(("parallel",)), name=name)(zx, pre, dact, w, dzx)


def _sc_fwd(proj, w, *, name):
    L = proj.shape[0]
    cb = 256
    nb = D // cb
    k = w.shape[0]

    def body(b_ref, c_ref, x_ref, w_ref, o_ref, v_ref):
        u = c_ref[...].astype(F32) * x_ref[...].astype(F32)
        v = u * w_ref[k - 1:k, :]
        for j in range(1, k):
            v = v + _shift_down(u, j) * w_ref[k - 1 - j:k - j, :]
        o_ref[...] = (b_ref[...].astype(F32) * v).astype(BF16)
        v_ref[...] = v.astype(BF16)

    blk = pl.BlockSpec((L, cb), lambda i: (0, i))
    return pl.pallas_call(
        body, grid=(nb,),
        in_specs=[blk, pl.BlockSpec((L, cb), lambda i: (0, i + nb)), pl.BlockSpec((L, cb), lambda i: (0, i + 2 * nb)),
                  pl.BlockSpec((k, cb), lambda i: (0, i))],
        out_specs=[blk, blk], out_shape=[jax.ShapeDtypeStruct((L, D), BF16)] * 2,
        compiler_params=_params(("parallel",)), name=name)(proj, proj, proj, w)


def _sc_bwd(proj, v, dyv, w, *, name):
    L = proj.shape[0]
    cb = 256
    nb = D // cb
    k = w.shape[0]

    def body(b_ref, c_ref, x_ref, v_ref, dy_ref, w_ref, dp_ref, s_ref):
        cv, xv = c_ref[...].astype(F32), x_ref[...].astype(F32)
        u = cv * xv
        dyv_ = dy_ref[...].astype(F32)
        dp_ref[0] = (dyv_ * v_ref[...].astype(F32)).astype(BF16)
        dv = dyv_ * b_ref[...].astype(F32)
        s_ref[...] = jnp.zeros_like(s_ref)
        s_ref[k - 1:k, :] = jnp.sum(dv * u, axis=0, keepdims=True)
        du = dv * w_ref[k - 1:k, :]
        for j in range(1, k):
            later = _shift_up(dv, j)
            du = du + later * w_ref[k - 1 - j:k - j, :]
            s_ref[k - 1 - j:k - j, :] = jnp.sum(later * u, axis=0, keepdims=True)
        dp_ref[1] = (du * xv).astype(BF16)
        dp_ref[2] = (du * cv).astype(BF16)

    blk = pl.BlockSpec((L, cb), lambda i: (0, i))
    return pl.pallas_call(
        body, grid=(nb,),
        in_specs=[blk, pl.BlockSpec((L, cb), lambda i: (0, i + nb)), pl.BlockSpec((L, cb), lambda i: (0, i + 2 * nb)),
                  blk, blk, pl.BlockSpec((k, cb), lambda i: (0, i))],
        out_specs=[pl.BlockSpec((3, L, cb), lambda i: (0, 0, i)), pl.BlockSpec((8, cb), lambda i: (0, i))],
        out_shape=[jax.ShapeDtypeStruct((3, L, D), BF16), jax.ShapeDtypeStruct((8, D), F32)],
        compiler_params=_params(("parallel",)), name=name)(proj, proj, proj, v, dyv, w)


def _pieces(v, n):
    out, rest = [], v
    for _ in range(n):
        out.append(rest.astype(BF16))
        rest = rest - out[-1].astype(F32)
    return out


def _cumsum_rows(mask, v):
    m = mask.astype(BF16)
    return _dot(jnp.concatenate([m, m, m], axis=1), jnp.concatenate(_pieces(v, 3), axis=0))


def _ssd_chunk_terms(dtr, prm):
    lane = lax.broadcasted_iota(jnp.int32, (CH, LANES), 1)
    valid = lane < NH
    xdt = dtr + prm[0:1, :]
    dt = jnp.where(valid, jnp.maximum(xdt, 0.0) + jnp.log1p(jnp.exp(-jnp.abs(xdt))), 0.0)
    A = -jnp.exp(prm[1:2, :])
    ri = lax.broadcasted_iota(jnp.int32, (CH, CH), 0)
    ci = lax.broadcasted_iota(jnp.int32, (CH, CH), 1)
    cs = _cumsum_rows(ri >= ci, dt * A)
    last = cs[CH - 1:CH, :]
    spread = (lax.broadcasted_iota(jnp.int32, (2 * LANES, DI), 1) // HP
              == lax.broadcasted_iota(jnp.int32, (2 * LANES, DI), 0) % LANES).astype(BF16)
    gather = ((lax.broadcasted_iota(jnp.int32, (LANES, 2 * DI), 1) % DI) // HP
              == lax.broadcasted_iota(jnp.int32, (LANES, 2 * DI), 0)).astype(BF16)
    return dict(valid=valid, xdt=xdt, dt=dt, A=A, cs=cs, csT=cs.T, last=last, ri=ri, ci=ci, ex=(spread, gather))


def _expand(v, ex):
    if v.shape[0] == 1:
        return _expand(jnp.broadcast_to(v, (8, LANES)), ex)[0:1, :]
    return _dot(jnp.concatenate(_pieces(v, 2), axis=1), ex[0])


def _head_sum(v, ex):
    if v.shape[0] == 1:
        return _head_sum(jnp.broadcast_to(v, (8, DI)), ex)[0:1, :]
    return _dot_nt(jnp.concatenate(_pieces(v, 2), axis=1), ex[1])


def _ssd_fwd(xbc, dtr, prm, *, name):
    L = xbc.shape[0]
    nc = L // CH

    def body(xbc_ref, dtr_ref, prm_ref, y_ref, sp_ref, st_ref):
        @pl.when(pl.program_id(0) == 0)
        def _():
            st_ref[...] = jnp.zeros_like(st_ref)

        prm_v = prm_ref[...]
        t = _ssd_chunk_terms(dtr_ref[...], prm_v)
        cs, csT, ex, causal = t["cs"], t["csT"], t["ex"], t["ri"] >= t["ci"]
        xs = xbc_ref[:, 0:DI].astype(F32)
        X = xs * _expand(t["dt"], ex)
        Xb = X.astype(BF16)
        Xd = (X * _expand(jnp.exp(t["last"] - cs), ex)).astype(BF16)
        Ex = _expand(jnp.exp(cs), ex)
        cdx = _expand(jnp.exp(t["last"]), ex)
        dskx = _expand(prm_v[2:3, :], ex)
        lane = lax.broadcasted_iota(jnp.int32, (CH, LANES), 1)
        sp_ref[0] = st_ref[...]
        for g in range(NG):
            Bg = xbc_ref[:, DI + g * NS:DI + (g + 1) * NS].astype(BF16)
            Cg = xbc_ref[:, DI + GW + g * NS:DI + GW + (g + 1) * NS].astype(BF16)
            G = _dot_nt(Cg, Bg)
            Sg = st_ref[:, g * GW:(g + 1) * GW]
            yoff = _dot(Cg, Sg.astype(BF16)) * Ex[:, g * GW:(g + 1) * GW]
            for j in range(GW // LANES):
                lo = g * GW + j * LANES
                Xp = Xb[:, lo:lo + LANES]
                yd = []
                for h in (lo // HP, lo // HP + 1):
                    seg = cs[:, h:h + 1] - csT[h:h + 1, :]
                    yd.append(_dot((G * jnp.where(causal, jnp.exp(seg), 0.0)).astype(BF16), Xp))
                y_ref[:, lo:lo + LANES] = (jnp.where(lane < HP, yd[0], yd[1]) + yoff[:, j * LANES:(j + 1) * LANES]
                                           + dskx[:, lo:lo + LANES] * xs[:, lo:lo + LANES]).astype(BF16)
            st_ref[:, g * GW:(g + 1) * GW] = Sg * cdx[:, g * GW:(g + 1) * GW] + _dot_tn(Bg, Xd[:, g * GW:(g + 1) * GW])

    return pl.pallas_call(
        body, grid=(nc,),
        in_specs=[pl.BlockSpec((CH, CONVD), lambda c: (c, 0)), pl.BlockSpec((CH, LANES), lambda c: (c, 0)),
                  pl.BlockSpec((8, LANES), lambda c: (0, 0))],
        out_specs=[pl.BlockSpec((CH, DI), lambda c: (c, 0)), pl.BlockSpec((1, NS, DI), lambda c: (c, 0, 0))],
        out_shape=[jax.ShapeDtypeStruct((L, DI), BF16), jax.ShapeDtypeStruct((nc, NS, DI), F32)],
        scratch_shapes=[pltpu.VMEM((NS, DI), F32)],
        compiler_params=_params(("arbitrary",)), name=name)(xbc, dtr, prm)


def _ssd_bwd(xbc, dtr, prm, dy, sprev, *, name):
    L = xbc.shape[0]
    nc = L // CH

    def body(xbc_ref, dtr_ref, prm_ref, dy_ref, sp_ref, dxbc_ref, ddtr_ref, s_ref, dst_ref, dx_scr, de_scr, dd_scr):
        step = pl.program_id(0)

        @pl.when(step == 0)
        def _():
            dst_ref[...] = jnp.zeros_like(dst_ref)
            s_ref[...] = jnp.zeros_like(s_ref)

        prm_v = prm_ref[...]
        t = _ssd_chunk_terms(dtr_ref[...], prm_v)
        cs, csT, ex, ri, ci = t["cs"], t["csT"], t["ex"], t["ri"], t["ci"]
        E = jnp.exp(cs)
        dec = jnp.exp(t["last"] - cs)
        cd = jnp.exp(t["last"])
        xs = xbc_ref[:, 0:DI].astype(F32)
        dtx = _expand(t["dt"], ex)
        X = xs * dtx
        Xb = X.astype(BF16)
        decx = _expand(dec, ex)
        Xd = (X * decx).astype(BF16)
        Ex = _expand(E, ex)
        cdx = _expand(cd, ex)
        dskx = _expand(prm_v[2:3, :], ex)
        lane = lax.broadcasted_iota(jnp.int32, (CH, LANES), 1)
        dcs = jnp.zeros((CH, LANES), F32)
        dcd_x = []
        for g in range(NG):
            gs = slice(g * GW, (g + 1) * GW)
            Bg = xbc_ref[:, DI + g * NS:DI + (g + 1) * NS].astype(BF16)
            Cg = xbc_ref[:, DI + GW + g * NS:DI + GW + (g + 1) * NS].astype(BF16)
            G = _dot_nt(Cg, Bg)
            GT = _dot_nt(Bg, Cg)
            Sg = sp_ref[0, :, gs]
            Sgb = Sg.astype(BF16)
            dyg = dy_ref[:, gs]
            de_scr[:, gs] = dyg * _dot(Cg, Sgb)
            dYo = (Ex[:, gs] * dyg).astype(BF16)
            dC = _dot_nt(dYo, Sgb)
            dS_in = _dot_tn(Cg, dYo)
            dStg = dst_ref[:, gs]
            dStb = dStg.astype(BF16)
            dXd = _dot(Bg, dStb)
            dB = _dot_nt(Xd[:, gs], dStb)
            dd_scr[:, gs] = dXd * X[:, gs]
            dXst = dXd * decx[:, gs]
            dG = jnp.zeros((CH, CH), F32)
            dGT = jnp.zeros((CH, CH), F32)
            for j in range(GW // LANES):
                lo = g * GW + j * LANES
                Xp = Xb[:, lo:lo + LANES]
                dyp = dy_ref[:, lo:lo + LANES]
                dXp = dXst[:, j * LANES:(j + 1) * LANES]
                for k, h in enumerate((lo // HP, lo // HP + 1)):
                    dyh = jnp.where((lane < HP) if k == 0 else (lane >= HP), dyp, 0.0).astype(BF16)
                    seg = cs[:, h:h + 1] - csT[h:h + 1, :]
                    Lm = jnp.where(ri >= ci, jnp.exp(seg), 0.0)
                    LmT = jnp.where(ci >= ri, jnp.exp(-seg), 0.0)
                    dM = _dot_nt(dyh, Xp)
                    dMT = _dot_nt(Xp, dyh)
                    MT = GT * LmT
                    rs = jnp.sum(dM * (G * Lm), axis=1, keepdims=True) - jnp.sum(dMT * MT, axis=1, keepdims=True)
                    dcs = dcs + jnp.where(lane == h, rs, 0.0)
                    dG = dG + dM * Lm
                    dGT = dGT + dMT * LmT
                    dXp = dXp + _dot(MT.astype(BF16), dyh)
                dx_scr[:, lo:lo + LANES] = dXp
            dxbc_ref[:, DI + g * NS:DI + (g + 1) * NS] = (dB + _dot(dGT.astype(BF16), Cg)).astype(BF16)
            dxbc_ref[:, DI + GW + g * NS:DI + GW + (g + 1) * NS] = (dC + _dot(dG.astype(BF16), Bg)).astype(BF16)
            dcd_x.append(jnp.sum(dStg * Sg, axis=0, keepdims=True))
            dst_ref[:, gs] = dStg * cdx[:, gs] + dS_in
        dX = dx_scr[...]
        dy = dy_ref[...]
        ddec = _head_sum(dd_scr[...], ex)
        dcd = _head_sum(jnp.concatenate(dcd_x, axis=1), ex)
        dcs = dcs + _head_sum(de_scr[...], ex) * E - ddec * dec
        row = lax.broadcasted_iota(jnp.int32, (CH, LANES), 0)
        dcs = dcs + jnp.where(row == CH - 1, jnp.sum(ddec * dec, axis=0, keepdims=True) + dcd * cd, 0.0)
        da = _cumsum_rows(ci >= ri, dcs)
        ddt = da * t["A"] + _head_sum(dX * xs, ex)
        ddtr = jnp.where(t["valid"], ddt * _sigmoid(t["xdt"]), 0.0)
        ddtr_ref[...] = ddtr
        dxbc_ref[:, 0:DI] = (dX * dtx + dskx * dy).astype(BF16)
        s_ref[0:1, :] += jnp.sum(da * t["dt"], axis=0, keepdims=True)
        s_ref[1:2, :] += _head_sum(jnp.sum(dy * xs, axis=0, keepdims=True), ex)
        s_ref[2:3, :] += jnp.sum(ddtr, axis=0, keepdims=True)

        @pl.when(step == nc - 1)
        def _():
            s_ref[0:1, :] = s_ref[0:1, :] * t["A"]

    rev = lambda c: (nc - 1 - c, 0)
    return pl.pallas_call(
        body, grid=(nc,),
        in_specs=[pl.BlockSpec((CH, CONVD), rev), pl.BlockSpec((CH, LANES), rev), pl.BlockSpec((8, LANES), lambda c: (0, 0)),
                  pl.BlockSpec((CH, DI), rev), pl.BlockSpec((1, NS, DI), lambda c: (nc - 1 - c, 0, 0))],
        out_specs=[pl.BlockSpec((CH, CONVD), rev), pl.BlockSpec((CH, LANES), rev), pl.BlockSpec((8, LANES), lambda c: (0, 0))],
        out_shape=[jax.ShapeDtypeStruct((L, CONVD), BF16), jax.ShapeDtypeStruct((L, LANES), F32),
                   jax.ShapeDtypeStruct((8, LANES), F32)],
        scratch_shapes=[pltpu.VMEM((NS, DI), F32), pltpu.VMEM((CH, DI), F32), pltpu.VMEM((CH, DI), F32),
                        pltpu.VMEM((CH, DI), F32)],
        compiler_params=_params(("arbitrary",)), name=name)(xbc, dtr, prm, dy, sprev)


def _gnorm_fwd(y, zx, nw, *, name):
    L = y.shape[0]
    tm = min(L, 256)

    def body(y_ref, z_ref, nw_ref, o_ref):
        z = z_ref[...].astype(F32)
        yg = y_ref[...].astype(F32) * (z * _sigmoid(z))
        for g in range(NG):
            v = yg[:, g * GW:(g + 1) * GW]
            r = lax.rsqrt(jnp.mean(v * v, axis=-1, keepdims=True) + EPS)
            o_ref[:, g * GW:(g + 1) * GW] = (v * r * nw_ref[:, g * GW:(g + 1) * GW]).astype(BF16)

    row = pl.BlockSpec((tm, DI), lambda i: (i, 0))
    return pl.pallas_call(body, grid=(L // tm,), in_specs=[row, row, pl.BlockSpec((1, DI), lambda i: (0, 0))],
                          out_specs=row, out_shape=jax.ShapeDtypeStruct((L, DI), BF16),
                          compiler_params=_params(("parallel",)), name=name)(y, zx, nw)


def _gnorm_bwd(y, zx, nw, dyn, *, name):
    L = y.shape[0]
    tm = min(L, 256)

    def body(y_ref, z_ref, nw_ref, dyn_ref, dy_ref, dz_ref, s_ref):
        @pl.when(pl.program_id(0) == 0)
        def _():
            s_ref[...] = jnp.zeros_like(s_ref)

        z, yv = z_ref[...].astype(F32), y_ref[...].astype(F32)
        sz = _sigmoid(z)
        gate = z * sz
        dgate_dz = sz * (1.0 + z * (1.0 - sz))
        for g in range(NG):
            gs = slice(g * GW, (g + 1) * GW)
            v = yv[:, gs] * gate[:, gs]
            r = lax.rsqrt(jnp.mean(v * v, axis=-1, keepdims=True) + EPS)
            vhat = v * r
            dn = dyn_ref[:, gs].astype(F32)
            s_ref[0:1, gs] += jnp.sum(dn * vhat, axis=0, keepdims=True)
            dvhat = dn * nw_ref[:, gs]
            dv = r * (dvhat - vhat * jnp.mean(dvhat * vhat, axis=-1, keepdims=True))
            dy_ref[:, gs] = dv * gate[:, gs]
            dz_ref[:, gs] = (dv * yv[:, gs] * dgate_dz[:, gs]).astype(BF16)

    row = pl.BlockSpec((tm, DI), lambda i: (i, 0))
    return pl.pallas_call(body, grid=(L // tm,), in_specs=[row, row, pl.BlockSpec((1, DI), lambda i: (0, 0)), row],
                          out_specs=[row, row, pl.BlockSpec((8, DI), lambda i: (0, 0))],
                          out_shape=[jax.ShapeDtypeStruct((L, DI), F32), jax.ShapeDtypeStruct((L, ZX), BF16),
                                     jax.ShapeDtypeStruct((8, DI), F32)],
                          compiler_params=_params(("arbitrary",)), name=name)(y, zx, nw, dyn)


def _adamw(w, g, m, v, *, name, g_row=0, w_row=0, rows=None, into=None, emit_g=False):
    lead = w.ndim == 3
    R, C = w.shape[-2:]
    rows = R if rows is None else rows
    tr = max([t for t in range(8, rows + 1, 8) if rows % t == 0 and t * C <= 256 * 1024], default=rows)
    assert g_row % tr == 0 and w_row % tr == 0, (name, g_row, w_row, tr)
    n_out = 4 if emit_g else 3

    def body(w_ref, g_ref, m_ref, v_ref, *rest):
        outs = rest[-n_out:]
        gv = g_ref[...]
        mn = ADAM_B1 * m_ref[...] + (1.0 - ADAM_B1) * gv
        vn = ADAM_B2 * v_ref[...] + (1.0 - ADAM_B2) * (gv * gv)
        m_hat = mn / (1.0 - ADAM_B1 ** ADAM_STEP)
        v_hat = vn / (1.0 - ADAM_B2 ** ADAM_STEP)
        d_ref, mo_ref, vo_ref = outs[-3:]
        d_ref[...] = -ADAM_LR * (m_hat / (jnp.sqrt(v_hat) + ADAM_EPS) + ADAM_WD * w_ref[...])
        mo_ref[...] = mn
        vo_ref[...] = vn
        if emit_g:
            outs[0][...] = gv

    blk = (pl.BlockSpec((None, tr, C), lambda i: (0, i + w_row // tr, 0)) if lead
           else pl.BlockSpec((tr, C), lambda i: (i + w_row // tr, 0)))
    args, in_specs, alias = [w, g, m, v], [blk, pl.BlockSpec((tr, C), lambda i: (i + g_row // tr, 0)), blk, blk], {}
    if into is not None:
        args, in_specs, alias = args + list(into), in_specs + [ANY] * n_out, {4 + k: k for k in range(n_out)}
    return pl.pallas_call(body, grid=(rows // tr,), in_specs=in_specs, out_specs=[blk] * n_out,
                          out_shape=[jax.ShapeDtypeStruct(w.shape, F32)] * n_out, input_output_aliases=alias,
                          compiler_params=_params(("parallel",)), name=name)(*args)


def _residual(acc, xv, gv):
    return xv + gv * acc, acc


def _like(buf):
    return jax.ShapeDtypeStruct(buf.shape, buf.dtype)


def _mlp_fwd(x, mod, nw, wb, up_row, down_row, tag, midway=None):
    sh, sc, g = mod
    h = _modnorm_fwd(x, nw, sc, sh, name=tag + "_norm")
    a = _matmul(h, wb, n=DFF, tm=TM_ALL, b_spec=pl.BlockSpec((None, D, 512), lambda mi, j: (j // 2, up_row // D, j % 2)),
                epi=lambda acc: (jnp.maximum(acc, 0.0),), out_dtypes=(BF16,), name=tag + "_up")
    if midway is not None:
        midway(a)
    xn, y = _matmul(a, wb, n=D, tm=TM_HALF, contract=_nn_split_sq,
                    b_spec=pl.BlockSpec((N_CHIPS, D, 512), lambda mi, j: (0, down_row // D, j)),
                    extras=(x, g), epi=_residual, out_dtypes=(F32, BF16), name=tag + "_down")
    return xn, (x, h, a, y)


def _mlp_bwd(dxo, dy, gsum, saved, mod, nw, wb, gb, up_row, down_row, below, tag):
    x, h, a, y = saved
    sh, sc, g = mod
    du = _matmul(dy, wb, n=DFF, tm=TM_ALL, contract=_nt,
                 b_spec=pl.BlockSpec((None, 512, D), lambda mi, j: (j // 2, down_row // 512 + j % 2, 0)),
                 extras=(a,), epi=lambda acc, av: (acc * (2.0 * av.astype(F32)),), out_dtypes=(BF16,), name=tag + "_dact")
    gb = _matmul_tn(a, dy, m=DFF, n=D, tm=D, tn=D, a_square=True, into=gb, out_struct=_like(wb),
                    out_spec=pl.BlockSpec((None, D, D), lambda mi, j: (mi, down_row // D, 0)), name=tag + "_ddown")
    dh = _matmul(du, wb, n=D, tm=TM_HALF, contract=_nt_split,
                 b_spec=pl.BlockSpec((N_CHIPS, 512, D), lambda mi, j: (0, up_row // 512 + j, 0)), out_dtypes=(BF16,),
                 name=tag + "_dh")
    gb = _matmul_tn(h, du, m=D, n=DFF, tm=D, into=gb, out_struct=_like(wb),
                    out_spec=pl.BlockSpec((None, D, 512), lambda mi, j: (j // 2, up_row // D, j % 2)), name=tag + "_dup")
    dx, sums, *nxt = _modnorm_bwd(x, dh, dxo, nw, sc, gsum, below, name=tag + "_dnorm")
    return dx, gb, sums, *nxt


def _ssd_fwd_scan(x, mod, nw, w_in_t, w_dt_t, conv_w, conv_b, prm, tag):
    sh, sc, g = mod
    h = _modnorm_fwd(x, nw, sc, sh, name=tag + "_norm")
    zx = _matmul(h, w_in_t, n=ZX, tm=TM_ALL, contract=_nt, out_dtypes=(BF16,), name=tag + "_in")
    dtr = _matmul(h, w_dt_t, n=LANES, tm=TM_ALL, contract=_nt, name=tag + "_in_dt")
    xbc, pre = _ssd_conv_fwd(zx, conv_w, conv_b, name=tag + "_conv")
    y, sprev = _ssd_fwd(xbc, dtr, prm, name=tag + "_scan")
    return h, zx, dtr, xbc, y, sprev, pre


def _ssd_fwd_out(x, mod, scan, gn_w, get_w_out, tag):
    sh, sc, g = mod
    h, zx, dtr, xbc, y, sprev, pre = scan
    yn = _gnorm_fwd(y, zx, gn_w, name=tag + "_gnorm")
    w_out = get_w_out(yn)
    xn, yo = _matmul(yn, w_out, n=D, tm=TM_HALF, contract=_nn_split,
                     b_spec=pl.BlockSpec((N_CHIPS, 512, 512), lambda mi, j: (0, 0, j)),
                     extras=(x, g), epi=_residual, out_dtypes=(F32, BF16), name=tag + "_out")
    return xn, (x, h, zx, dtr, xbc, y, sprev, yn, yo, pre)


def _ssd_bwd_out(dyo, saved, w_out, tag, after):
    x, h, zx, dtr, xbc, y, sprev, yn, yo, pre = saved
    dyn = _matmul(dyo, w_out, n=DI, tm=TM_ALL, contract=_nt, b_spec=pl.BlockSpec((None, 512, D), lambda mi, j: (j, 0, 0)),
                  extras=(jnp.broadcast_to(after[0:1, 0:1], (1, DI)),), epi=lambda acc, t: (acc + t,),
                  out_dtypes=(BF16,), name=tag + "_dyn")
    g_out = _matmul_tn(yn, dyo, m=DI, n=D, tn=D, out_struct=_like(w_out),
                       out_spec=pl.BlockSpec((None, 512, D), lambda mi, j: (mi, 0, 0)), name=tag + "_dout")
    return dyn, g_out


def _ssd_bwd_rest(dxo, dy, dzx, gsum, saved, mod, nw, w_in_t, w_dt_t, conv_w, prm, tag):
    x, h, zx, dtr, xbc, y, sprev, yn, yo, pre = saved
    sh, sc, g = mod
    dxbc, ddtr, ssum = _ssd_bwd(xbc, dtr, prm, dy, sprev, name=tag + "_dscan")
    dzx, csum = _ssd_conv_bwd(zx, pre, dxbc, conv_w, dzx, name=tag + "_dconv")
    dh_dt = _matmul(ddtr, w_dt_t, n=D, tm=TM_ALL, name=tag + "_dh_dt")
    dh = _matmul(dzx, w_in_t, n=D, tm=TM_HALF, b_spec=pl.BlockSpec((ZX, 512), lambda mi, j: (0, j)), extras=(dh_dt,),
                 epi=lambda acc, e: (acc + e,), out_dtypes=(BF16,), name=tag + "_dh")
    d_w_zx = _matmul_tn(h, dzx, m=D, n=ZX, tm=D, name=tag + "_din")
    d_w_dt = _matmul_tn(h, ddtr, m=D, n=LANES, tm=D, name=tag + "_din_dt")
    dx, sums = _modnorm_bwd(x, dh, dxo, nw, sc, gsum, None, name=tag + "_dnorm")
    return dx, d_w_zx, d_w_dt, sums, csum, ssum


def _sc_layer_fwd(x, mod, nw, w_sc_in, conv_w, wb, out_row, tag, midway=None):
    sh, sc, g = mod
    h = _modnorm_fwd(x, nw, sc, sh, name=tag + "_norm")
    proj = _matmul(h, w_sc_in, n=3 * D, tm=TM_ALL, tn=256, out_dtypes=(BF16,),
                   b_spec=pl.BlockSpec((None, D, 256), lambda mi, j: (j // 3, 0, j % 3)),
                   name=tag + "_in")
    if midway is not None:
        midway(proj)
    yv, v = _sc_fwd(proj, conv_w, name=tag + "_conv")
    xn, yo = _matmul(yv, wb, n=D, tm=TM_HALF, contract=_nn_split,
                     b_spec=pl.BlockSpec((N_CHIPS, 256, 512), lambda mi, j: (0, out_row // 256, j)),
                     extras=(x, g), epi=_residual, out_dtypes=(F32, BF16), name=tag + "_out")
    return xn, (x, h, proj, yv, yo, v)


def _sc_layer_bwd(dxo, dyo, gsum, saved, mod, nw, w_sc_in, conv_w, wb, gb, out_row, below, tag):
    x, h, proj, yv, yo, v = saved
    sh, sc, g = mod
    L = x.shape[0]
    dyv = _matmul(dyo, wb, n=D, tm=TM_ALL, tn=256, contract=_nt,
                  b_spec=pl.BlockSpec((None, 256, D), lambda mi, j: (j, out_row // 256, 0)), out_dtypes=(BF16,),
                  name=tag + "_dyv")
    gb = _matmul_tn(yv, dyo, m=D, n=D, tm=256, tn=D, into=gb, out_struct=_like(wb),
                    out_spec=pl.BlockSpec((None, 256, D), lambda mi, j: (mi, out_row // 256, 0)), name=tag + "_dout")
    dproj, csum = _sc_bwd(proj, v, dyv, conv_w, name=tag + "_dconv")
    tm = min(L, TM_HALF)
    dh = _matmul(dproj, w_sc_in, n=D, tm=tm, contract=_nt_sc_in, a_spec=pl.BlockSpec((3, tm, D), lambda mi, j: (0, mi, 0)),
                 b_spec=pl.BlockSpec((N_CHIPS, 512, SC_IN_SHARD), lambda mi, j: (0, j, 0)), out_dtypes=(BF16,),
                 name=tag + "_dh")
    g_sc_in = _matmul_tn(h, dproj, m=D, n=3 * D, tm=D, tn=256, b_spec=pl.BlockSpec((None, L, 256), lambda mi, j: (j // 4, 0, j % 4)),
                         out_spec=pl.BlockSpec((None, D, 256), lambda mi, j: (j // 3, 0, j % 3)),
                         out_struct=jax.ShapeDtypeStruct((N_CHIPS, D, SC_IN_SHARD), BF16), name=tag + "_din")
    dx, sums, *nxt = _modnorm_bwd(x, dh, dxo, nw, sc, gsum, below, name=tag + "_dnorm")
    return dx, gb, g_sc_in, sums, csum, *nxt


SUB_ROW = (0, 8, 16, 24)
SSD_CONV_ROW, GNORM_ROW, FINAL_ROW, SC_CONV_ROW, HEAD_ROW, SMALL_ROWS = 32, 48, 56, 64, 72, 80


def _all_gather_rows(blk, *, name):
    m_per, n = blk.shape

    def body(x_ref, out_ref, send_sems, recv_sems, local_sem):
        x, y, c = lax.axis_index("x"), lax.axis_index("y"), lax.axis_index("c")
        me, sibling = (x, y, c), (x, y, 1 - c)
        chips = [(1 - x, y), (x, 1 - y), (1 - x, 1 - y)]

        def rows(px, py, pc):
            return out_ref.at[pl.ds((4 * px + 2 * py + pc) * m_per, m_per), :]

        def copy(k, block, to, src=None):
            return pltpu.make_async_remote_copy(src_ref=rows(*block) if src is None else src, dst_ref=rows(*block),
                                                send_sem=send_sems.at[k], recv_sem=recv_sems.at[k], device_id=to,
                                                device_id_type=MESH)

        mine = pltpu.make_async_copy(x_ref, rows(*me), local_sem)
        mine.start()
        first = [copy(0, me, sibling, src=x_ref)] + [copy(1 + j, me, (*chip, c), src=x_ref) for j, chip in enumerate(chips)]
        for cp in first:
            cp.start()
        passed = [copy(4 + j, (*chip, c), sibling) for j, chip in enumerate(chips)]
        for j, chip in enumerate(chips):
            copy(1 + j, (*chip, c), me).wait_recv()
            passed[j].start()
        copy(0, sibling, me).wait_recv()
        for j, chip in enumerate(chips):
            copy(4 + j, (*chip, 1 - c), me).wait_recv()
        for cp in first + passed:
            cp.wait_send()
        mine.wait()

    return pl.pallas_call(
        body, out_shape=jax.ShapeDtypeStruct((N_DEV * m_per, n), blk.dtype),
        in_specs=[pl.BlockSpec(memory_space=pltpu.VMEM)], out_specs=pl.BlockSpec(memory_space=pltpu.VMEM),
        scratch_shapes=[pltpu.SemaphoreType.DMA((7,)), pltpu.SemaphoreType.DMA((7,)), pltpu.SemaphoreType.DMA],
        name=name)(blk)


def _half(ref, chip, c):
    r, n = ref.shape[1:]
    if r % 32 == 0:
        return ref.at[chip, pl.ds(c * (r // 2), r // 2), :]
    assert n % 256 == 0, ref.shape
    return ref.at[chip, :, pl.ds(c * (n // 2), n // 2)]


def _gather_copy(bufs, sends, recvs, b, k, chip, pc, to):
    piece = _half(bufs[b], 2 * chip[0] + chip[1], pc)
    return pltpu.make_async_remote_copy(src_ref=piece, dst_ref=piece, send_sem=sends.at[4 * b + k], recv_sem=recvs.at[4 * b + k],
                                        device_id=to, device_id_type=MESH)


def _split_call(body, bufs, sems_in, n_sems, *, name, after=(), token=False, lands=()):
    nb, na, nl, starts = len(bufs), len(after), len(lands), not sems_in

    def wrapped(*refs):
        sems = refs[nb + na:nb + na + 2] if starts else refs[nb:nb + 2]
        made = refs[nb + na + 2 + nb:nb + na + 2 + nb + nl] if starts else ()
        body(tuple(refs[:nb]) + tuple(made), sems[0], sems[1])
        if token:
            refs[-1][...] = jnp.zeros_like(refs[-1])

    out_shape = [pltpu.SemaphoreType.DMA((n_sems,)) for _ in range(2 if starts else 0)]
    out_specs = [SEM] * len(out_shape) + [ANY] * (nb + nl)
    alias = {b: len(out_shape) + b for b in range(nb)}
    out_shape += [jax.ShapeDtypeStruct(b.shape, b.dtype) for b in bufs] + list(lands)
    if token:
        out_shape.append(jax.ShapeDtypeStruct((8, LANES), F32))
        out_specs.append(pl.BlockSpec(memory_space=pltpu.VMEM))
    return pl.pallas_call(
        wrapped, out_shape=out_shape, in_specs=[ANY] * nb + [SEM] * len(sems_in) + [ANY] * na, out_specs=out_specs,
        input_output_aliases=alias,
        compiler_params=pltpu.CompilerParams(has_side_effects=pltpu.SideEffectType.DATAFLOW_SIDE_EFFECTING),
        name=name)(*bufs, *sems_in, *after)


def _gather_start(bufs, *, name, after=()):
    nb = len(bufs)

    def body(ins, sends, recvs):
        x, y, c = lax.axis_index("x"), lax.axis_index("y"), lax.axis_index("c")
        chips = [(1 - x, y), (x, 1 - y), (1 - x, 1 - y)]
        for b in range(nb):
            _gather_copy(ins, sends, recvs, b, 0, (x, y), c, (x, y, 1 - c)).start()
            for j, chip in enumerate(chips):
                _gather_copy(ins, sends, recvs, b, 1 + j, (x, y), c, (*chip, c)).start()

    out = _split_call(body, bufs, (), 4 * nb, name=name, after=after, token=True)
    return (out[0], out[1], out[2:2 + nb]), out[-1]


def _gather_wait_first(flight, *, name, after=()):
    sends, recvs, bufs = flight
    nb = len(bufs)

    def body(ins, sends_, recvs_):
        x, y, c = lax.axis_index("x"), lax.axis_index("y"), lax.axis_index("c")
        chips = [(1 - x, y), (x, 1 - y), (1 - x, 1 - y)]
        for b in range(nb):
            _gather_copy(ins, sends_, recvs_, b, 0, (x, y), c, (x, y, 1 - c)).wait_send()
            _gather_copy(ins, sends_, recvs_, b, 0, (x, y), 1 - c, (x, y, c)).wait_recv()
            for j, chip in enumerate(chips):
                _gather_copy(ins, sends_, recvs_, b, 1 + j, (x, y), c, (*chip, c)).wait_send()
                _gather_copy(ins, sends_, recvs_, b, 1 + j, chip, c, (x, y, c)).wait_recv()

    return _split_call(body, bufs, (sends, recvs), 4 * nb, name=name, after=after)


def _gather_forward(bufs, *, name):
    nb = len(bufs)

    def body(ins, sends, recvs):
        x, y, c = lax.axis_index("x"), lax.axis_index("y"), lax.axis_index("c")
        chips = [(1 - x, y), (x, 1 - y), (1 - x, 1 - y)]
        for b in range(nb):
            for j, chip in enumerate(chips):
                _gather_copy(ins, sends, recvs, b, 1 + j, chip, c, (x, y, 1 - c)).start()

    out = _split_call(body, bufs, (), 4 * nb, name=name)
    return out[0], out[1], out[2:2 + nb]


def _gather_wait_forward(flight, *, name, after=()):
    sends, recvs, bufs = flight
    nb = len(bufs)

    def body(ins, sends_, recvs_):
        x, y, c = lax.axis_index("x"), lax.axis_index("y"), lax.axis_index("c")
        chips = [(1 - x, y), (x, 1 - y), (1 - x, 1 - y)]
        for b in range(nb):
            for j, chip in enumerate(chips):
                _gather_copy(ins, sends_, recvs_, b, 1 + j, chip, c, (x, y, 1 - c)).wait_send()
                _gather_copy(ins, sends_, recvs_, b, 1 + j, chip, 1 - c, (x, y, c)).wait_recv()

    return _split_call(body, bufs, (sends, recvs), 4 * nb, name=name, after=after)


def _owner_copies(hs, lands, sends, recvs):
    x, y, c = lax.axis_index("x"), lax.axis_index("y"), lax.axis_index("c")
    chips = [(1 - x, y), (x, 1 - y), (1 - x, 1 - y)]
    return [pltpu.make_async_remote_copy(src_ref=hs[b].at[2 * cx + cy], dst_ref=lands[b].at[j], send_sem=sends.at[3 * b + j],
                                         recv_sem=recvs.at[3 * b + j], device_id=(cx, cy, c), device_id_type=MESH)
            for b in range(len(hs)) for j, (cx, cy) in enumerate(chips)]


def _owners_start(hs, *, name):
    nb = len(hs)
    lands = [jax.ShapeDtypeStruct((3,) + h.shape[1:], h.dtype) for h in hs]

    def body(refs, sends, recvs):
        for cp in _owner_copies(refs[:nb], refs[nb:], sends, recvs):
            cp.start()

    out = _split_call(body, list(hs), (), 3 * nb, name=name, token=True, lands=lands)
    return (out[0], out[1], out[2:2 + 2 * nb]), out[-1]


def _owners_wait(flight, *, name, after=()):
    sends, recvs, bufs = flight
    nb = len(bufs) // 2

    def body(refs, sends_, recvs_):
        for cp in _owner_copies(refs[:nb], refs[nb:], sends_, recvs_):
            cp.wait()

    out = _split_call(body, bufs, (sends, recvs), 3 * nb, name=name, after=after)
    return out[:nb], out[nb:]


def _sibling_copies(gs, lands, sends, recvs):
    x, y, c = lax.axis_index("x"), lax.axis_index("y"), lax.axis_index("c")
    copies = []
    for b in range(len(gs)):
        hr = gs[b].shape[1] // 2
        copies.append(pltpu.make_async_remote_copy(
            src_ref=gs[b].at[:, pl.ds((1 - c) * hr, hr), :], dst_ref=lands[b], send_sem=sends.at[b], recv_sem=recvs.at[b],
            device_id=(x, y, 1 - c), device_id_type=MESH))
    return copies


def _sibling_start(gs, *, name, after=()):
    nb = len(gs)
    lands = [jax.ShapeDtypeStruct((g.shape[0], g.shape[1] // 2, g.shape[2]), g.dtype) for g in gs]

    def body(refs, sends, recvs):
        for cp in _sibling_copies(refs[:nb], refs[nb:], sends, recvs):
            cp.start()

    out = _split_call(body, list(gs), (), nb, name=name, after=after, token=True, lands=lands)
    return (out[0], out[1], out[2:2 + 2 * nb]), out[-1]


def _sibling_wait(flight, *, name, after=()):
    sends, recvs, bufs = flight
    nb = len(bufs) // 2

    def body(refs, sends_, recvs_):
        for cp in _sibling_copies(refs[:nb], refs[nb:], sends_, recvs_):
            cp.wait()

    out = _split_call(body, bufs, (sends, recvs), nb, name=name, after=after)
    return out[:nb], out[nb:]


def _result_copies(ts, sends, recvs):
    x, y, c = lax.axis_index("x"), lax.axis_index("y"), lax.axis_index("c")
    return [pltpu.make_async_remote_copy(src_ref=ts[b].at[c], dst_ref=ts[b].at[c], send_sem=sends.at[b], recv_sem=recvs.at[b],
                                         device_id=(x, y, 1 - c), device_id_type=MESH) for b in range(len(ts))]


def _result_start(ts, *, name):
    def body(refs, sends, recvs):
        for cp in _result_copies(refs, sends, recvs):
            cp.start()

    out = _split_call(body, ts, (), len(ts), name=name, token=True)
    return (out[0], out[1], out[2:2 + len(ts)]), out[-1]


def _result_wait(flight, *, name, after=()):
    sends, recvs, bufs = flight

    def body(refs, sends_, recvs_):
        for cp in _result_copies(refs, sends_, recvs_):
            cp.wait()

    return _split_call(body, bufs, (sends, recvs), len(bufs), name=name, after=after)


def _row_tile(rows, cols):
    best = 16
    for t in range(16, rows + 1, 16):
        if rows % t == 0 and t * cols <= 640 * 1024:
            best = t
    assert rows % best == 0, (rows, cols)
    return best


def _add_sibling_half(g, recv, core, *, name):
    nk, r, n = g.shape
    hr = r // 2
    tr = _row_tile(hr, n)

    def body(c_ref, a_ref, b_ref, o_ref):
        o_ref[...] = (a_ref[...].astype(F32) + b_ref[...].astype(F32)).astype(BF16)

    grid_spec = pltpu.PrefetchScalarGridSpec(
        num_scalar_prefetch=1, grid=(nk, hr // tr),
        in_specs=[pl.BlockSpec((None, tr, n), lambda k, i, c_ref: (k, c_ref[0] * (hr // tr) + i, 0)),
                  pl.BlockSpec((None, tr, n), lambda k, i, c_ref: (k, i, 0))],
        out_specs=pl.BlockSpec((None, tr, n), lambda k, i, c_ref: (k, i, 0)))
    return pl.pallas_call(body, grid_spec=grid_spec, out_shape=jax.ShapeDtypeStruct((nk, hr, n), BF16),
                          compiler_params=_params(("parallel", "parallel")), name=name)(core, g, recv)


def _add_chip_sums(h, recv, chip_core, *, name):
    _, hr, n = h.shape
    tr = _row_tile(hr, n)

    def body(k_ref, a_ref, b_ref, o_ref):
        o_ref[...] = ((a_ref[...].astype(F32) + b_ref[0].astype(F32)) + b_ref[1].astype(F32)) + b_ref[2].astype(F32)

    grid_spec = pltpu.PrefetchScalarGridSpec(
        num_scalar_prefetch=1, grid=(hr // tr,),
        in_specs=[pl.BlockSpec((None, tr, n), lambda i, k_ref: (k_ref[0], i, 0)),
                  pl.BlockSpec((3, tr, n), lambda i, k_ref: (0, i, 0))],
        out_specs=pl.BlockSpec((None, tr, n), lambda i, k_ref: (k_ref[1], i, 0)))
    return pl.pallas_call(body, grid_spec=grid_spec, out_shape=jax.ShapeDtypeStruct((2, hr, n), F32),
                          compiler_params=_params(("parallel",)), name=name)(chip_core, h, recv)


def _sum_devices(g, *, name):
    nd, r, n = g.shape

    def body(g_ref, o_ref):
        acc = g_ref[0]
        for i in range(1, nd):
            acc = acc + g_ref[i]
        o_ref[...] = acc

    return pl.pallas_call(body, out_shape=jax.ShapeDtypeStruct((r, n), F32), name=name)(g)


def _own_slot(parts, chip, *, name, after=()):
    rows, cols = sum(w.shape[1] for w, _ in parts), parts[0][0].shape[2]
    buf, row0 = None, 0
    for p, (w, idx) in enumerate(parts):
        r = w.shape[1]
        tr = 256 if r % 256 == 0 else r
        assert row0 % tr == 0, (name, r, row0)
        prev = () if buf is None else (buf,)

        def body(chip_ref, w_ref, *rest):
            rest[-1][...] = w_ref[...].astype(BF16)

        grid_spec = pltpu.PrefetchScalarGridSpec(
            num_scalar_prefetch=1, grid=(r // tr,),
            in_specs=[pl.BlockSpec((None, tr, cols), lambda i, c_ref, idx=idx: (idx, i, 0))] + [ANY] * (len(prev) + len(after)),
            out_specs=pl.BlockSpec((None, tr, cols), lambda i, c_ref, row0=row0, tr=tr: (c_ref[0], row0 // tr + i, 0)))
        buf = pl.pallas_call(body, grid_spec=grid_spec, out_shape=jax.ShapeDtypeStruct((N_CHIPS, rows, cols), BF16),
                             input_output_aliases={2: 0} if prev else {}, compiler_params=_params(("parallel",)),
                             name=f"{name}{p}")(chip, w, *prev, *after)
        row0 += r
    return buf


def kernel(x, c, ada_w, ada_b, mix_norm_w, mlp_norm_w, mlp_up, mlp_down, ssd_in_w, ssd_conv_w, ssd_conv_b, ssd_dt_bias, ssd_A_log, ssd_D, ssd_norm_w, ssd_out_w, sc_in_w, sc_conv_w, sc_out_w, final_norm_w, loss_target, m_ada_w, m_ada_b, m_mix_norm_w, m_mlp_norm_w, m_mlp_up, m_mlp_down, m_ssd_in_w, m_ssd_conv_w, m_ssd_conv_b, m_ssd_dt_bias, m_ssd_A_log, m_ssd_D, m_ssd_norm_w, m_ssd_out_w, m_sc_in_w, m_sc_conv_w, m_sc_out_w, m_final_norm_w, v_ada_w, v_ada_b, v_mix_norm_w, v_mlp_norm_w, v_mlp_up, v_mlp_down, v_ssd_in_w, v_ssd_conv_w, v_ssd_conv_b, v_ssd_dt_bias, v_ssd_A_log, v_ssd_D, v_ssd_norm_w, v_ssd_out_w, v_sc_in_w, v_sc_conv_w, v_sc_out_w, v_final_norm_w):
    xi, yi, ci = lax.axis_index("x"), lax.axis_index("y"), lax.axis_index("c")
    chip = 2 * xi + yi
    dev = 2 * chip + ci
    n_ada = ada_w.shape[2]

    conv_flat = jnp.concatenate([ssd_conv_w.reshape(-1), sc_conv_w.reshape(-1), jnp.zeros((256,), F32)]).reshape(4, D)
    blk0 = jnp.concatenate([c, conv_flat, jnp.zeros((3, D), F32)], axis=0)
    got0 = _all_gather_rows(blk0, name="gather_cond").reshape(N_DEV, 8, D)
    c_all = got0[:, 0]
    conv_all = got0[0::2, 1:5].reshape(N_CHIPS, 4 * D)
    ssd_conv = jnp.moveaxis(conv_all[:, :4 * 768].reshape(N_CHIPS, 4, 768), 0, 1).reshape(4, CONVD)
    sc_conv = jnp.moveaxis(conv_all[:, 4 * 768:4 * 768 + 3 * 256].reshape(N_CHIPS, 3, 256), 0, 1).reshape(3, D)
    mod_shard = [_matmul(c_all, ada_w, n=n_ada, a_silu=True, b_spec=pl.BlockSpec((None, D, 512), lambda mi, j, i=i: (i, 0, j)),
                         extras=(lax.dynamic_slice(ada_b, (i, chip * n_ada), (1, n_ada)),),
                         epi=lambda acc, b: (acc + b,), name=f"ada_mod{i}") for i in range(2)]
    mod_slot = lax.dynamic_update_slice(jnp.zeros((N_CHIPS, 2 * N_DEV, n_ada), F32), jnp.concatenate(mod_shard, axis=0)[None],
                                        (chip, 0, 0))

    up_row, down_row = 0, D
    chip1 = chip.reshape(1).astype(jnp.int32)
    a_bufs = [mod_slot, _own_slot([(jnp.swapaxes(ssd_in_w, 1, 2), 0)], chip1, name="slot_ssd_in")]
    fly_a, tok = _gather_start(a_bufs, name="gather_a_start")
    b_bufs = [_own_slot([(ssd_out_w, 0)], chip1, name="slot_ssd_out", after=(tok,)),
              _own_slot([(mlp_up, 0), (mlp_down, 0)], chip1, name="slot_mlp0_", after=(tok,))]
    fly_b, tok = _gather_start(b_bufs, name="gather_b_start", after=(tok,))
    c_bufs = [_own_slot([(sc_in_w, 0)], chip1, name="slot_sc_in", after=(tok,)),
              _own_slot([(sc_out_w, 0)], chip1, name="slot_sc_out", after=(tok,))]
    fly_c, tok = _gather_start(c_bufs, name="gather_c_start", after=(tok,))
    d_bufs = [_own_slot([(mlp_up, 1), (mlp_down, 1)], chip1, name="slot_mlp1_", after=(tok,))]
    fly_d, tok = _gather_start(d_bufs, name="gather_d_start", after=(tok,))

    row = lambda v: v.reshape(1, -1)
    xs, tgt = x[0], loss_target[0]
    prm = jnp.pad(jnp.concatenate([ssd_dt_bias, ssd_A_log, ssd_D, jnp.zeros((5, NH), F32)], axis=0), ((0, 0), (0, LANES - NH)))
    mix_nw = [row(mix_norm_w[i]) for i in range(2)]
    mlp_nw = [row(mlp_norm_w[i]) for i in range(2)]
    a_bufs = _gather_wait_first(fly_a, name="gather_a_landed", after=(tok,))
    mod_all, w_ssd_in = _gather_wait_forward(_gather_forward(a_bufs, name="gather_a_pass"), name="gather_a_done")
    mod = lax.dynamic_index_in_dim(mod_all.reshape(N_CHIPS, 2, N_DEV, n_ada), dev, axis=2, keepdims=False)
    mod = jnp.moveaxis(mod, 0, 1).reshape(2, 6, D)
    mods = [[mod[i, j:j + 1] for j in range(6)] for i in range(2)]
    w_in_t = w_ssd_in.reshape(N_CHIPS * SSD_IN_SHARD, D)
    w_dt_t = jnp.pad(w_in_t[ZX:], ((0, LANES - NH), (0, 0)))
    scan = _ssd_fwd_scan(xs, mods[0][0:3], mix_nw[0], w_in_t, w_dt_t, ssd_conv, ssd_conv_b, prm, "ssd")

    def land(flight, tag, after):
        return _gather_forward(_gather_wait_first(flight, name=f"gather_{tag}_landed", after=(after,)), name=f"gather_{tag}_pass")

    passed, got = {"b": land(fly_b, "b", scan[4])}, {}

    def done(tag, after):
        got[tag] = _gather_wait_forward(passed[tag], name=f"gather_{tag}_done", after=(after,))
        return got[tag]

    x1, s_ssd = _ssd_fwd_out(xs, mods[0][0:3], scan, ssd_norm_w, lambda yn: done("b", yn)[0], "ssd")
    w_ssd_out, w_b = got["b"]
    x2, s_mlp0 = _mlp_fwd(x1, mods[0][3:6], mlp_nw[0], w_b, up_row, down_row, "mlp0",
                          midway=lambda a: passed.update(c=land(fly_c, "c", a)))
    w_sc_in, w_sc_out = done("c", x2)
    x3, s_sc = _sc_layer_fwd(x2, mods[1][0:3], mix_nw[1], w_sc_in, sc_conv, w_sc_out, 0, "sc",
                             midway=lambda proj: passed.update(d=land(fly_d, "d", proj)))
    (w_mlp1,) = done("d", x3)
    x4, s_mlp1 = _mlp_fwd(x3, mods[1][3:6], mlp_nw[1], w_mlp1, up_row, down_row, "mlp1")

    core = ci.reshape(1).astype(jnp.int32)
    chip_core = jnp.stack([chip, ci]).astype(jnp.int32)

    def reduce_swap(gbufs, tag, after=()):
        return _sibling_start(gbufs, name=tag + "_sibling_start", after=after)

    def reduce_send(flight, tag, after):
        gs, sib = _sibling_wait(flight, name=tag + "_sibling_landed", after=after)
        hs = [_add_sibling_half(g, s, core, name=f"{tag}_add_sibling{b}") for b, (g, s) in enumerate(zip(gs, sib))]
        return _owners_start(hs, name=tag + "_owners_start")

    def reduce_sum(flight, tag, after):
        hs, lands = _owners_wait(flight, name=tag + "_owners_landed", after=after)
        ts = [_add_chip_sums(h, o, chip_core, name=f"{tag}_add_chips{b}") for b, (h, o) in enumerate(zip(hs, lands))]
        return _result_start(ts, name=tag + "_result_start")

    def reduce_done(flight, tag, after=()):
        return [t.reshape(-1, t.shape[2]) for t in _result_wait(flight, name=tag + "_result_landed", after=after)]

    dx4, fsum, dy, gs = _final_loss(x4, row(final_norm_w), tgt, (mods[1][5], s_mlp1[3]), name="final_loss")
    dx3, g_mlp1, sum_mlp1, dy, gs = _mlp_bwd(dx4, dy, gs, s_mlp1, mods[1][3:6], mlp_nw[1], w_mlp1, None, up_row, down_row,
                                             (mods[1][2], s_sc[4]), "mlp1")
    dx2, g_sc_out, g_sc_in, sum_sc, sc_csum, dy, gs = _sc_layer_bwd(dx3, dy, gs, s_sc, mods[1][0:3], mix_nw[1], w_sc_in,
                                                                    sc_conv, w_sc_out, None, 0, (mods[0][5], s_mlp0[3]), "sc")
    dx1, g_b, sum_mlp0, dy, gsum_ssd = _mlp_bwd(dx2, dy, gs, s_mlp0, mods[0][3:6], mlp_nw[0], w_b, None, up_row, down_row,
                                                (mods[0][2], s_ssd[8]), "mlp0")
    fly_1, tok = reduce_swap([g_mlp1, g_sc_out, g_sc_in, g_b], "rs1")
    dyn, g_ssd_out = _ssd_bwd_out(dy, s_ssd, w_ssd_out, "ssd", tok)
    fly_1, tok = reduce_send(fly_1, "rs1", (g_ssd_out,))
    dy, dzx, gnsum = _gnorm_bwd(s_ssd[5], s_ssd[2], ssd_norm_w + tok[0:1, 0:1], dyn, name="ssd_dgnorm")
    grad_x, d_w_zx, d_w_dt, sum_ssd, csum, ssum = _ssd_bwd_rest(
        dx1, dy, dzx, gsum_ssd, s_ssd, mods[0][0:3], mix_nw[0], w_in_t, w_dt_t, ssd_conv, prm, "ssd")

    def ssd_in_owner(k):
        lo, hi = k * SSD_IN_SHARD, (k + 1) * SSD_IN_SHARD
        if hi <= ZX:
            return d_w_zx[:, lo:hi]
        return jnp.concatenate([d_w_zx[:, lo:], d_w_dt[:, :hi - ZX]], axis=1)

    small = jnp.concatenate([sum_ssd, sum_mlp0, sum_sc, sum_mlp1, csum.reshape(24, D)[0:16], gnsum.reshape(16, D)[0:8],
                             fsum, sc_csum, jnp.pad(ssum, ((0, 0), (0, D - LANES)))], axis=0)
    small_slot = lax.dynamic_update_slice(jnp.zeros((N_CHIPS, 2 * SMALL_ROWS, D), F32), small[None], (chip, ci * SMALL_ROWS, 0))
    fly_2, tok = reduce_swap([jnp.stack([ssd_in_owner(k) for k in range(N_CHIPS)]).astype(BF16), g_ssd_out], "rs2")
    fly_s, tok = _gather_start([small_slot], name="gather_small_start", after=(tok,))
    fly_1, tok = reduce_sum(fly_1, "rs1", (grad_x, tok))
    fly_2, tok = reduce_send(fly_2, "rs2", (tok,))
    fly_s = _gather_forward(_gather_wait_first(fly_s, name="gather_small_landed", after=(tok,)), name="gather_small_pass")
    (small_all,) = _gather_wait_forward(fly_s, name="gather_small_done")
    t_mlp1, t_sc_out, t_sc_in, t_b = reduce_done(fly_1, "rs1", (small_all,))
    small_all = small_all.reshape(N_DEV, SMALL_ROWS, D) + tok[0:1, 0:1]
    tot = _sum_devices(small_all, name="sum_small")
    loss = tot[FINAL_ROW + 1, 0]
    mod_rows = [r + o for r in SUB_ROW for o in (3, 2, 0)]
    g_ada_b = jnp.stack([tot[r] for r in mod_rows]).reshape(2, 6 * D)
    g_mix_norm = jnp.stack([tot[SUB_ROW[0] + 1], tot[SUB_ROW[2] + 1]])
    g_mlp_norm = jnp.stack([tot[SUB_ROW[1] + 1], tot[SUB_ROW[3] + 1]])
    conv_sums = tot[SSD_CONV_ROW:SSD_CONV_ROW + 15].reshape(5, CONVD)
    g_ssd_conv_w = lax.dynamic_slice(conv_sums, (0, chip * 768), (4, 768))[None]
    g_ssd_conv_b = conv_sums[4:5]
    g_ssd_norm = tot[GNORM_ROW:GNORM_ROW + 2].reshape(1, DI)
    g_final = tot[FINAL_ROW]
    g_sc_conv_w = lax.dynamic_slice(tot[SC_CONV_ROW:SC_CONV_ROW + 3], (0, chip * 256), (3, 256))[None]
    g_a_log, g_d, g_dt_bias = (tot[HEAD_ROW + r:HEAD_ROW + r + 1, 0:NH] for r in range(3))
    c_pad = jnp.concatenate([c_all, jnp.zeros((8, D), F32)], axis=0)
    dmod_all = jnp.stack([small_all[:, r] for r in mod_rows], axis=1).reshape(N_DEV, 2, 6 * D)
    g_ada_w = []
    for i in range(2):
        dm = lax.dynamic_slice(dmod_all[:, i], (0, chip * n_ada), (N_DEV, n_ada))
        g_ada_w.append(_matmul_tn(c_pad, jnp.concatenate([dm, jnp.zeros_like(dm)], axis=0), m=D, n=n_ada, a_silu=True,
                                  name=f"ada_dw{i}"))

    big = dict(ada_w=[(g, 0) for g in g_ada_w], mlp_up=[(t_b, up_row), (t_mlp1, up_row)],
               mlp_down=[(t_b, down_row), (t_mlp1, down_row)], ssd_out_w=None, sc_out_w=[(t_sc_out, 0)],
               sc_in_w=[(t_sc_in, 0)], ssd_in_w=None)
    grads = dict(ada_b=g_ada_b, mix_norm_w=g_mix_norm, mlp_norm_w=g_mlp_norm, ssd_conv_w=g_ssd_conv_w,
                 ssd_conv_b=g_ssd_conv_b, ssd_dt_bias=g_dt_bias, ssd_A_log=g_a_log, ssd_D=g_d, ssd_norm_w=g_ssd_norm,
                 sc_conv_w=g_sc_conv_w, final_norm_w=g_final)
    weights = dict(ada_w=(ada_w, m_ada_w, v_ada_w), ada_b=(ada_b, m_ada_b, v_ada_b),
                   mix_norm_w=(mix_norm_w, m_mix_norm_w, v_mix_norm_w), mlp_norm_w=(mlp_norm_w, m_mlp_norm_w, v_mlp_norm_w),
                   mlp_up=(mlp_up, m_mlp_up, v_mlp_up), mlp_down=(mlp_down, m_mlp_down, v_mlp_down),
                   ssd_in_w=(ssd_in_w, m_ssd_in_w, v_ssd_in_w), ssd_conv_w=(ssd_conv_w, m_ssd_conv_w, v_ssd_conv_w),
                   ssd_conv_b=(ssd_conv_b, m_ssd_conv_b, v_ssd_conv_b), ssd_dt_bias=(ssd_dt_bias, m_ssd_dt_bias, v_ssd_dt_bias),
                   ssd_A_log=(ssd_A_log, m_ssd_A_log, v_ssd_A_log), ssd_D=(ssd_D, m_ssd_D, v_ssd_D),
                   ssd_norm_w=(ssd_norm_w, m_ssd_norm_w, v_ssd_norm_w), ssd_out_w=(ssd_out_w, m_ssd_out_w, v_ssd_out_w),
                   sc_in_w=(sc_in_w, m_sc_in_w, v_sc_in_w), sc_conv_w=(sc_conv_w, m_sc_conv_w, v_sc_conv_w),
                   sc_out_w=(sc_out_w, m_sc_out_w, v_sc_out_w), final_norm_w=(final_norm_w, m_final_norm_w, v_final_norm_w))
    def step(nm, parts):
        w, m, v = (t if t.shape[0] == 1 else t.reshape(-1, t.shape[-1]) for t in weights[nm])
        rows, outs = w.shape[-2] // len(parts), None
        for i, (gbuf, g_row) in enumerate(parts):
            outs = _adamw(w, gbuf, m, v, g_row=g_row, w_row=i * rows, rows=rows, into=outs, emit_g=True, name=f"adamw_{nm}{i}")
        return outs

    res = {}
    for nm, (w, m, v) in weights.items():
        two_d = (-1, w.shape[-1]) if w.ndim > 1 else (1, -1)
        if nm not in big:
            res[nm] = (grads[nm], *_adamw(w.reshape(two_d), grads[nm].reshape(two_d), m.reshape(two_d), v.reshape(two_d),
                                          name="adamw_" + nm))
        elif big[nm] is not None:
            res[nm] = step(nm, big[nm])
    fly_2, tok = reduce_sum(fly_2, "rs2", tuple(r[1] for r in res.values()))
    t_ssd_in, t_ssd_out = reduce_done(fly_2, "rs2", (tok,))
    res["ssd_out_w"] = step("ssd_out_w", [(t_ssd_out, 0)])
    w_t, m_t, v_t = (jnp.swapaxes(t[0], 0, 1) for t in weights["ssd_in_w"])
    res["ssd_in_w"] = [jnp.swapaxes(o, 0, 1) for o in _adamw(w_t, t_ssd_in.T, m_t, v_t, emit_g=True, name="adamw_ssd_in_w")]
    outs = [[res[nm][k].reshape(weights[nm][0].shape) for nm in weights] for k in range(4)]
    return (loss, grad_x[None], *outs[0], *outs[1], *outs[2], *outs[3])
```

```python
import jax
import jax.numpy as jnp
from jax import lax
from jax.experimental import pallas as pl
from jax.experimental.pallas import tpu as pltpu

F32 = jnp.float32
BF16 = jnp.bfloat16
MESH = pl.DeviceIdType.MESH

D = 1024
DFF = 4096
DI = 2048
NH = 32
HP = 64
NG = 4
NS = 128
CH = 128
CONVD = DI + 2 * NG * NS
ZX = DI + CONVD
GW = NG * NS
LANES = 128
N_CHIPS = 4
N_DEV = 8
EPS = 1e-5
ADAM_LR, ADAM_B1, ADAM_B2, ADAM_EPS, ADAM_WD, ADAM_STEP = 1e-3, 0.9, 0.999, 1e-8, 0.01, 10
VMEM_LIMIT = 48 * 1024 * 1024
TM_ALL = 2048
TM_HALF = 1024
ANY = pl.BlockSpec(memory_space=pl.ANY)
SEM = pl.BlockSpec(memory_space=pltpu.SEMAPHORE)

SSD_IN_SHARD = 1288
SC_IN_SHARD = 768


def _params(sem=None):
    return pltpu.CompilerParams(dimension_semantics=sem, vmem_limit_bytes=VMEM_LIMIT)


def _sigmoid(v):
    return 0.5 * jnp.tanh(0.5 * v) + 0.5


def _dot(a, b, dims=((1,), (0,)), precision=None):
    return lax.dot_general(a, b, (dims, ((), ())), preferred_element_type=F32, precision=precision)


def _dot_nt(a, b):
    return _dot(a, b, ((1,), (1,)))


def _dot_tn(a, b):
    return _dot(a, b, ((0,), (0,)))


def _nn(av, bv):
    return _dot(av.astype(BF16), bv.astype(BF16))


def _nt(av, bv):
    return _dot_nt(av.astype(BF16), bv.astype(BF16))


def _nn_split(av, bv):
    return _dot(av.astype(BF16), bv.reshape(-1, bv.shape[2]))


def _nn_split_sq(av, bv):
    return _nn_split(av * av, bv)


def _nt_split(av, bv):
    kc = bv.shape[2]
    acc = _dot_nt(av[:, 0:kc].astype(BF16), bv[0])
    for s in range(1, bv.shape[0]):
        acc = acc + _dot_nt(av[:, s * kc:(s + 1) * kc].astype(BF16), bv[s])
    return acc


def _nt_sc_in(av, bv):
    q = 256
    acc = None
    for i in range(3 * D // q):
        a_blk = av[i // 4][:, (i % 4) * q:(i % 4 + 1) * q]
        b_blk = bv[i // 3][:, (i % 3) * q:(i % 3 + 1) * q]
        t = _dot_nt(a_blk, b_blk)
        acc = t if acc is None else acc + t
    return acc


def _matmul(a, b, *, name, n, contract=_nn, a_spec=None, b_spec=None, tm=512, tn=512, extras=(), epi=None,
            out_dtypes=(F32,), a_silu=False):
    M = a.shape[-2]
    tm, tn = min(tm, M), min(tn, n)
    assert M % tm == 0 and n % tn == 0, (name, M, n, tm, tn)
    n_ex = len(extras)
    if a_spec is None:
        a_spec = pl.BlockSpec((tm, a.shape[1]), lambda i, j: (i, 0))
    if b_spec is None:
        b_spec = (pl.BlockSpec((tn, b.shape[1]), lambda i, j: (j, 0)) if contract is _nt
                  else pl.BlockSpec((b.shape[0], tn), lambda i, j: (0, j)))

    def body(*refs):
        av = refs[0][...]
        if a_silu:
            av = av * _sigmoid(av)
        acc = contract(av, refs[1][...])
        res = epi(acc, *[r[...] for r in refs[2:2 + n_ex]]) if epi is not None else (acc,)
        for o_ref, r in zip(refs[2 + n_ex:], res, strict=True):
            o_ref[...] = r.astype(o_ref.dtype)

    in_specs = [a_spec, b_spec]
    for e in extras:
        in_specs.append(pl.BlockSpec((1, tn), lambda i, j: (0, j)) if e.shape[0] == 1 and M != 1
                        else pl.BlockSpec((tm, tn), lambda i, j: (i, j)))
    outs = pl.pallas_call(
        body, grid=(M // tm, n // tn), in_specs=in_specs,
        out_specs=[pl.BlockSpec((tm, tn), lambda i, j: (i, j)) for _ in out_dtypes],
        out_shape=[jax.ShapeDtypeStruct((M, n), dt) for dt in out_dtypes],
        compiler_params=_params(("parallel", "parallel")), name=name)(a, b, *extras)
    return outs if len(out_dtypes) > 1 else outs[0]


def _matmul_tn(a, b, *, name, m, n, tm=512, tn=512, a_spec=None, b_spec=None, out_spec=None, out_struct=None, into=None,
               a_silu=False, a_square=False):
    T = a.shape[-2]
    tm, tn = min(tm, m), min(tn, n)
    assert m % tm == 0 and n % tn == 0, (name, m, n, tm, tn)
    if a_spec is None:
        a_spec = pl.BlockSpec((T, tm), lambda i, j: (0, i))
    if b_spec is None:
        b_spec = pl.BlockSpec((T, tn), lambda i, j: (0, j))
    if out_spec is None:
        out_spec, out_struct = pl.BlockSpec((tm, tn), lambda i, j: (i, j)), jax.ShapeDtypeStruct((m, n), F32)

    def body(a_ref, b_ref, *rest):
        av = a_ref[...]
        if a_silu:
            av = av * _sigmoid(av)
        if a_square:
            av = av * av
        rest[-1][...] = _dot_tn(av.astype(BF16), b_ref[...].astype(BF16)).astype(rest[-1].dtype)

    args, in_specs, alias = [a, b], [a_spec, b_spec], {}
    if into is not None:
        args, in_specs, alias = args + [into], in_specs + [ANY], {2: 0}
    return pl.pallas_call(body, grid=(m // tm, n // tn), in_specs=in_specs, out_specs=out_spec, out_shape=out_struct,
                          input_output_aliases=alias, compiler_params=_params(("parallel", "parallel")), name=name)(*args)


def _modnorm_fwd(x, nw, sc, sh, *, name):
    L = x.shape[0]
    tm = min(L, 512)

    def body(x_ref, nw_ref, sc_ref, sh_ref, h_ref):
        xv = x_ref[...]
        r = lax.rsqrt(jnp.mean(xv * xv, axis=-1, keepdims=True) + EPS)
        h_ref[...] = ((xv * r * nw_ref[...]) * (1.0 + sc_ref[...]) + sh_ref[...]).astype(BF16)

    row = pl.BlockSpec((tm, D), lambda i: (i, 0))
    vec = pl.BlockSpec((1, D), lambda i: (0, 0))
    return pl.pallas_call(body, grid=(L // tm,), in_specs=[row, vec, vec, vec], out_specs=row,
                          out_shape=jax.ShapeDtypeStruct((L, D), BF16),
                          compiler_params=_params(("parallel",)), name=name)(x, nw, sc, sh)


def _gate_outputs(dx, below_refs, dy_ref, gs_ref):
    g_ref, y_ref = below_refs
    dy_ref[...] = (dx * g_ref[...]).astype(BF16)
    gs_ref[0:1, :] += jnp.sum(dx * y_ref[...].astype(F32), axis=0, keepdims=True)


def _modnorm_bwd(x, dh, dxo, nw, sc, gsum, below, *, name):
    L = x.shape[0]
    tm = min(L, 256)
    nb = 0 if below is None else 2

    def body(x_ref, dh_ref, dxo_ref, nw_ref, sc_ref, g_ref, *rest):
        dx_ref, s_ref = rest[nb:nb + 2]

        @pl.when(pl.program_id(0) == 0)
        def _():
            s_ref[...] = g_ref[...]
            if nb:
                rest[-1][...] = jnp.zeros_like(rest[-1])

        xv, dhv = x_ref[...], dh_ref[...].astype(F32)
        r = lax.rsqrt(jnp.mean(xv * xv, axis=-1, keepdims=True) + EPS)
        xhat = xv * r
        dxhat = dhv * (nw_ref[...] * (1.0 + sc_ref[...]))
        dx = dxo_ref[...] + r * (dxhat - xhat * jnp.mean(dxhat * xhat, axis=-1, keepdims=True))
        dx_ref[...] = dx
        s_ref[1:2, :] += jnp.sum(dhv * xhat, axis=0, keepdims=True) * (1.0 + sc_ref[...])
        s_ref[2:3, :] += jnp.sum(dhv * xhat, axis=0, keepdims=True) * nw_ref[...]
        s_ref[3:4, :] += jnp.sum(dhv, axis=0, keepdims=True)
        if nb:
            _gate_outputs(dx, rest[:nb], rest[-2], rest[-1])

    row = pl.BlockSpec((tm, D), lambda i: (i, 0))
    vec = pl.BlockSpec((1, D), lambda i: (0, 0))
    blk = pl.BlockSpec((8, D), lambda i: (0, 0))
    in_specs, out_specs = [row, row, row, vec, vec, blk], [row, blk]
    out_shape = [jax.ShapeDtypeStruct((L, D), F32), jax.ShapeDtypeStruct((8, D), F32)]
    if nb:
        in_specs, out_specs = in_specs + [vec, row], out_specs + [row, blk]
        out_shape += [jax.ShapeDtypeStruct((L, D), BF16), jax.ShapeDtypeStruct((8, D), F32)]
    return pl.pallas_call(body, grid=(L // tm,), in_specs=in_specs, out_specs=out_specs, out_shape=out_shape,
                          compiler_params=_params(("arbitrary",)), name=name)(x, dh, dxo, nw, sc, gsum, *(below or ()))


def _final_loss(x, fw, tgt, below, *, name):
    L = x.shape[0]
    tm = min(L, 256)

    def body(x_ref, fw_ref, t_ref, g_ref, y_ref, dx_ref, s_ref, dy_ref, gs_ref):
        @pl.when(pl.program_id(0) == 0)
        def _():
            s_ref[...] = jnp.zeros_like(s_ref)
            gs_ref[...] = jnp.zeros_like(gs_ref)

        xv = x_ref[...]
        r = lax.rsqrt(jnp.mean(xv * xv, axis=-1, keepdims=True) + EPS)
        xhat = xv * r
        diff = xhat * fw_ref[...] - t_ref[...]
        dout = diff * (1.0 / D)
        dxhat = dout * fw_ref[...]
        dx = r * (dxhat - xhat * jnp.mean(dxhat * xhat, axis=-1, keepdims=True))
        dx_ref[...] = dx
        s_ref[0:1, :] += jnp.sum(dout * xhat, axis=0, keepdims=True)
        s_ref[1:2, :] += jnp.zeros((1, D), F32) + 0.5 * jnp.sum(jnp.sum(diff * diff, axis=-1, keepdims=True) * (1.0 / D))
        _gate_outputs(dx, (g_ref, y_ref), dy_ref, gs_ref)

    row = pl.BlockSpec((tm, D), lambda i: (i, 0))
    vec = pl.BlockSpec((1, D), lambda i: (0, 0))
    blk = pl.BlockSpec((8, D), lambda i: (0, 0))
    return pl.pallas_call(body, grid=(L // tm,), in_specs=[row, vec, row, vec, row], out_specs=[row, blk, row, blk],
                          out_shape=[jax.ShapeDtypeStruct((L, D), F32), jax.ShapeDtypeStruct((8, D), F32),
                                     jax.ShapeDtypeStruct((L, D), BF16), jax.ShapeDtypeStruct((8, D), F32)],
                          compiler_params=_params(("arbitrary",)), name=name)(x, fw, tgt, *below)


def _shift_down(v, j):
    if j == 0:
        return v
    rolled = pltpu.roll(v, j, 0)
    row = lax.broadcasted_iota(jnp.int32, (8, v.shape[1]), 0)
    return jnp.concatenate([jnp.where(row >= j, rolled[0:8], 0.0), rolled[8:]], axis=0)


def _shift_up(v, j):
    if j == 0:
        return v
    n = v.shape[0]
    rolled = pltpu.roll(v, n - j, 0)
    row = lax.broadcasted_iota(jnp.int32, (8, v.shape[1]), 0)
    return jnp.concatenate([rolled[:n - 8], jnp.where(row < 8 - j, rolled[n - 8:], 0.0)], axis=0)


def _ssd_conv_fwd(zx, w, b, *, name):
    L = zx.shape[0]
    cb = 256
    k = w.shape[0]

    def body(x_ref, w_ref, b_ref, o_ref, p_ref):
        xv = x_ref[...].astype(F32)
        pre = b_ref[...] + xv * w_ref[k - 1:k, :]
        for j in range(1, k):
            pre = pre + _shift_down(xv, j) * w_ref[k - 1 - j:k - j, :]
        o_ref[...] = (pre * _sigmoid(pre)).astype(BF16)
        p_ref[...] = pre.astype(BF16)

    blk = pl.BlockSpec((L, cb), lambda i: (0, i))
    return pl.pallas_call(
        body, grid=(CONVD // cb,),
        in_specs=[pl.BlockSpec((L, cb), lambda i: (0, i + DI // cb)), pl.BlockSpec((k, cb), lambda i: (0, i)),
                  pl.BlockSpec((1, cb), lambda i: (0, i))],
        out_specs=[blk, blk], out_shape=[jax.ShapeDtypeStruct((L, CONVD), BF16)] * 2,
        compiler_params=_params(("parallel",)), name=name)(zx, w, b)


def _ssd_conv_bwd(zx, pre, dact, w, dzx, *, name):
    L = zx.shape[0]
    cb = 256
    k = w.shape[0]

    def body(x_ref, p_ref, da_ref, w_ref, _, dx_ref, s_ref):
        xv, pv = x_ref[...].astype(F32), p_ref[...].astype(F32)
        s = _sigmoid(pv)
        dpre = da_ref[...].astype(F32) * (s * (1.0 + pv * (1.0 - s)))
        s_ref[...] = jnp.zeros_like(s_ref)
        s_ref[k:k + 1, :] = jnp.sum(dpre, axis=0, keepdims=True)
        s_ref[k - 1:k, :] = jnp.sum(dpre * xv, axis=0, keepdims=True)
        dx = dpre * w_ref[k - 1:k, :]
        for j in range(1, k):
            later = _shift_up(dpre, j)
            dx = dx + later * w_ref[k - 1 - j:k - j, :]
            s_ref[k - 1 - j:k - j, :] = jnp.sum(later * xv, axis=0, keepdims=True)
        dx_ref[...] = dx.astype(BF16)

    blk = pl.BlockSpec((L, cb), lambda i: (0, i))
    return pl.pallas_call(
        body, grid=(CONVD // cb,),
        in_specs=[pl.BlockSpec((L, cb), lambda i: (0, i + DI // cb)), blk, blk, pl.BlockSpec((k, cb), lambda i: (0, i)), ANY],
        out_specs=[pl.BlockSpec((L, cb), lambda i: (0, i + DI // cb)), pl.BlockSpec((8, cb), lambda i: (0, i))],
        out_shape=[jax.ShapeDtypeStruct((L, ZX), BF16), jax.ShapeDtypeStruct((8, CONVD), F32)],
        input_output_aliases={4: 0}, compiler_params=_params(("parallel",)), name=name)(zx, pre, dact, w, dzx)


def _sc_fwd(proj, w, *, name):
    L = proj.shape[0]
    cb = 256
    nb = D // cb
    k = w.shape[0]

    def body(b_ref, c_ref, x_ref, w_ref, o_ref, v_ref):
        u = c_ref[...].astype(F32) * x_ref[...].astype(F32)
        v = u * w_ref[k - 1:k, :]
        for j in range(1, k):
            v = v + _shift_down(u, j) * w_ref[k - 1 - j:k - j, :]
        o_ref[...] = (b_ref[...].astype(F32) * v).astype(BF16)
        v_ref[...] = v.astype(BF16)

    blk = pl.BlockSpec((L, cb), lambda i: (0, i))
    return pl.pallas_call(
        body, grid=(nb,),
        in_specs=[blk, pl.BlockSpec((L, cb), lambda i: (0, i + nb)), pl.BlockSpec((L, cb), lambda i: (0, i + 2 * nb)),
                  pl.BlockSpec((k, cb), lambda i: (0, i))],
        out_specs=[blk, blk], out_shape=[jax.ShapeDtypeStruct((L, D), BF16)] * 2,
        compiler_params=_params(("parallel",)), name=name)(proj, proj, proj, w)


def _sc_bwd(proj, v, dyv, w, *, name):
    L = proj.shape[0]
    cb = 256
    nb = D // cb
    k = w.shape[0]

    def body(b_ref, c_ref, x_ref, v_ref, dy_ref, w_ref, dp_ref, s_ref):
        cv, xv = c_ref[...].astype(F32), x_ref[...].astype(F32)
        u = cv * xv
        dyv_ = dy_ref[...].astype(F32)
        dp_ref[0] = (dyv_ * v_ref[...].astype(F32)).astype(BF16)
        dv = dyv_ * b_ref[...].astype(F32)
        s_ref[...] = jnp.zeros_like(s_ref)
        s_ref[k - 1:k, :] = jnp.sum(dv * u, axis=0, keepdims=True)
        du = dv * w_ref[k - 1:k, :]
        for j in range(1, k):
            later = _shift_up(dv, j)
            du = du + later * w_ref[k - 1 - j:k - j, :]
            s_ref[k - 1 - j:k - j, :] = jnp.sum(later * u, axis=0, keepdims=True)
        dp_ref[1] = (du * xv).astype(BF16)
        dp_ref[2] = (du * cv).astype(BF16)

    blk = pl.BlockSpec((L, cb), lambda i: (0, i))
    return pl.pallas_call(
        body, grid=(nb,),
        in_specs=[blk, pl.BlockSpec((L, cb), lambda i: (0, i + nb)), pl.BlockSpec((L, cb), lambda i: (0, i + 2 * nb)),
                  blk, blk, pl.BlockSpec((k, cb), lambda i: (0, i))],
        out_specs=[pl.BlockSpec((3, L, cb), lambda i: (0, 0, i)), pl.BlockSpec((8, cb), lambda i: (0, i))],
        out_shape=[jax.ShapeDtypeStruct((3, L, D), BF16), jax.ShapeDtypeStruct((8, D), F32)],
        compiler_params=_params(("parallel",)), name=name)(proj, proj, proj, v, dyv, w)


def _pieces(v, n):
    out, rest = [], v
    for _ in range(n):
        out.append(rest.astype(BF16))
        rest = rest - out[-1].astype(F32)
    return out


def _cumsum_rows(mask, v):
    m = mask.astype(BF16)
    return _dot(jnp.concatenate([m, m, m], axis=1), jnp.concatenate(_pieces(v, 3), axis=0))


def _ssd_chunk_terms(dtr, prm):
    lane = lax.broadcasted_iota(jnp.int32, (CH, LANES), 1)
    valid = lane < NH
    xdt = dtr + prm[0:1, :]
    dt = jnp.where(valid, jnp.maximum(xdt, 0.0) + jnp.log1p(jnp.exp(-jnp.abs(xdt))), 0.0)
    A = -jnp.exp(prm[1:2, :])
    ri = lax.broadcasted_iota(jnp.int32, (CH, CH), 0)
    ci = lax.broadcasted_iota(jnp.int32, (CH, CH), 1)
    cs = _cumsum_rows(ri >= ci, dt * A)
    last = cs[CH - 1:CH, :]
    spread = (lax.broadcasted_iota(jnp.int32, (2 * LANES, DI), 1) // HP
              == lax.broadcasted_iota(jnp.int32, (2 * LANES, DI), 0) % LANES).astype(BF16)
    gather = ((lax.broadcasted_iota(jnp.int32, (LANES, 2 * DI), 1) % DI) // HP
              == lax.broadcasted_iota(jnp.int32, (LANES, 2 * DI), 0)).astype(BF16)
    return dict(valid=valid, xdt=xdt, dt=dt, A=A, cs=cs, csT=cs.T, last=last, ri=ri, ci=ci, ex=(spread, gather))


def _expand(v, ex):
    if v.shape[0] == 1:
        return _expand(jnp.broadcast_to(v, (8, LANES)), ex)[0:1, :]
    return _dot(jnp.concatenate(_pieces(v, 2), axis=1), ex[0])


def _head_sum(v, ex):
    if v.shape[0] == 1:
        return _head_sum(jnp.broadcast_to(v, (8, DI)), ex)[0:1, :]
    return _dot_nt(jnp.concatenate(_pieces(v, 2), axis=1), ex[1])


def _ssd_fwd(xbc, dtr, prm, *, name):
    L = xbc.shape[0]
    nc = L // CH

    def body(xbc_ref, dtr_ref, prm_ref, y_ref, sp_ref, st_ref):
        @pl.when(pl.program_id(0) == 0)
        def _():
            st_ref[...] = jnp.zeros_like(st_ref)

        prm_v = prm_ref[...]
        t = _ssd_chunk_terms(dtr_ref[...], prm_v)
        cs, csT, ex, causal = t["cs"], t["csT"], t["ex"], t["ri"] >= t["ci"]
        xs = xbc_ref[:, 0:DI].astype(F32)
        X = xs * _expand(t["dt"], ex)
        Xb = X.astype(BF16)
        Xd = (X * _expand(jnp.exp(t["last"] - cs), ex)).astype(BF16)
        Ex = _expand(jnp.exp(cs), ex)
        cdx = _expand(jnp.exp(t["last"]), ex)
        dskx = _expand(prm_v[2:3, :], ex)
        lane = lax.broadcasted_iota(jnp.int32, (CH, LANES), 1)
        sp_ref[0] = st_ref[...]
        for g in range(NG):
            Bg = xbc_ref[:, DI + g * NS:DI + (g + 1) * NS].astype(BF16)
            Cg = xbc_ref[:, DI + GW + g * NS:DI + GW + (g + 1) * NS].astype(BF16)
            G = _dot_nt(Cg, Bg)
            Sg = st_ref[:, g * GW:(g + 1) * GW]
            yoff = _dot(Cg, Sg.astype(BF16)) * Ex[:, g * GW:(g + 1) * GW]
            for j in range(GW // LANES):
                lo = g * GW + j * LANES
                Xp = Xb[:, lo:lo + LANES]
                yd = []
                for h in (lo // HP, lo // HP + 1):
                    seg = cs[:, h:h + 1] - csT[h:h + 1, :]
                    yd.append(_dot((G * jnp.where(causal, jnp.exp(seg), 0.0)).astype(BF16), Xp))
                y_ref[:, lo:lo + LANES] = (jnp.where(lane < HP, yd[0], yd[1]) + yoff[:, j * LANES:(j + 1) * LANES]
                                           + dskx[:, lo:lo + LANES] * xs[:, lo:lo + LANES]).astype(BF16)
            st_ref[:, g * GW:(g + 1) * GW] = Sg * cdx[:, g * GW:(g + 1) * GW] + _dot_tn(Bg, Xd[:, g * GW:(g + 1) * GW])

    return pl.pallas_call(
        body, grid=(nc,),
        in_specs=[pl.BlockSpec((CH, CONVD), lambda c: (c, 0)), pl.BlockSpec((CH, LANES), lambda c: (c, 0)),
                  pl.BlockSpec((8, LANES), lambda c: (0, 0))],
        out_specs=[pl.BlockSpec((CH, DI), lambda c: (c, 0)), pl.BlockSpec((1, NS, DI), lambda c: (c, 0, 0))],
        out_shape=[jax.ShapeDtypeStruct((L, DI), BF16), jax.ShapeDtypeStruct((nc, NS, DI), F32)],
        scratch_shapes=[pltpu.VMEM((NS, DI), F32)],
        compiler_params=_params(("arbitrary",)), name=name)(xbc, dtr, prm)


def _ssd_bwd(xbc, dtr, prm, dy, sprev, *, name):
    L = xbc.shape[0]
    nc = L // CH

    def body(xbc_ref, dtr_ref, prm_ref, dy_ref, sp_ref, dxbc_ref, ddtr_ref, s_ref, dst_ref, dx_scr, de_scr, dd_scr):
        step = pl.program_id(0)

        @pl.when(step == 0)
        def _():
            dst_ref[...] = jnp.zeros_like(dst_ref)
            s_ref[...] = jnp.zeros_like(s_ref)

        prm_v = prm_ref[...]
        t = _ssd_chunk_terms(dtr_ref[...], prm_v)
        cs, csT, ex, ri, ci = t["cs"], t["csT"], t["ex"], t["ri"], t["ci"]
        E = jnp.exp(cs)
        dec = jnp.exp(t["last"] - cs)
        cd = jnp.exp(t["last"])
        xs = xbc_ref[:, 0:DI].astype(F32)
        dtx = _expand(t["dt"], ex)
        X = xs * dtx
        Xb = X.astype(BF16)
        decx = _expand(dec, ex)
        Xd = (X * decx).astype(BF16)
        Ex = _expand(E, ex)
        cdx = _expand(cd, ex)
        dskx = _expand(prm_v[2:3, :], ex)
        lane = lax.broadcasted_iota(jnp.int32, (CH, LANES), 1)
        dcs = jnp.zeros((CH, LANES), F32)
        dcd_x = []
        for g in range(NG):
            gs = slice(g * GW, (g + 1) * GW)
            Bg = xbc_ref[:, DI + g * NS:DI + (g + 1) * NS].astype(BF16)
            Cg = xbc_ref[:, DI + GW + g * NS:DI + GW + (g + 1) * NS].astype(BF16)
            G = _dot_nt(Cg, Bg)
            GT = _dot_nt(Bg, Cg)
            Sg = sp_ref[0, :, gs]
            Sgb = Sg.astype(BF16)
            dyg = dy_ref[:, gs]
            de_scr[:, gs] = dyg * _dot(Cg, Sgb)
            dYo = (Ex[:, gs] * dyg).astype(BF16)
            dC = _dot_nt(dYo, Sgb)
            dS_in = _dot_tn(Cg, dYo)
            dStg = dst_ref[:, gs]
            dStb = dStg.astype(BF16)
            dXd = _dot(Bg, dStb)
            dB = _dot_nt(Xd[:, gs], dStb)
            dd_scr[:, gs] = dXd * X[:, gs]
            dXst = dXd * decx[:, gs]
            dG = jnp.zeros((CH, CH), F32)
            dGT = jnp.zeros((CH, CH), F32)
            for j in range(GW // LANES):
                lo = g * GW + j * LANES
                Xp = Xb[:, lo:lo + LANES]
                dyp = dy_ref[:, lo:lo + LANES]
                dXp = dXst[:, j * LANES:(j + 1) * LANES]
                for k, h in enumerate((lo // HP, lo // HP + 1)):
                    dyh = jnp.where((lane < HP) if k == 0 else (lane >= HP), dyp, 0.0).astype(BF16)
                    seg = cs[:, h:h + 1] - csT[h:h + 1, :]
                    Lm = jnp.where(ri >= ci, jnp.exp(seg), 0.0)
                    LmT = jnp.where(ci >= ri, jnp.exp(-seg), 0.0)
                    dM = _dot_nt(dyh, Xp)
                    dMT = _dot_nt(Xp, dyh)
                    MT = GT * LmT
                    rs = jnp.sum(dM * (G * Lm), axis=1, keepdims=True) - jnp.sum(dMT * MT, axis=1, keepdims=True)
                    dcs = dcs + jnp.where(lane == h, rs, 0.0)
                    dG = dG + dM * Lm
                    dGT = dGT + dMT * LmT
                    dXp = dXp + _dot(MT.astype(BF16), dyh)
                dx_scr[:, lo:lo + LANES] = dXp
            dxbc_ref[:, DI + g * NS:DI + (g + 1) * NS] = (dB + _dot(dGT.astype(BF16), Cg)).astype(BF16)
            dxbc_ref[:, DI + GW + g * NS:DI + GW + (g + 1) * NS] = (dC + _dot(dG.astype(BF16), Bg)).astype(BF16)
            dcd_x.append(jnp.sum(dStg * Sg, axis=0, keepdims=True))
            dst_ref[:, gs] = dStg * cdx[:, gs] + dS_in
        dX = dx_scr[...]
        dy = dy_ref[...]
        ddec = _head_sum(dd_scr[...], ex)
        dcd = _head_sum(jnp.concatenate(dcd_x, axis=1), ex)
        dcs = dcs + _head_sum(de_scr[...], ex) * E - ddec * dec
        row = lax.broadcasted_iota(jnp.int32, (CH, LANES), 0)
        dcs = dcs + jnp.where(row == CH - 1, jnp.sum(ddec * dec, axis=0, keepdims=True) + dcd * cd, 0.0)
        da = _cumsum_rows(ci >= ri, dcs)
        ddt = da * t["A"] + _head_sum(dX * xs, ex)
        ddtr = jnp.where(t["valid"], ddt * _sigmoid(t["xdt"]), 0.0)
        ddtr_ref[...] = ddtr
        dxbc_ref[:, 0:DI] = (dX * dtx + dskx * dy).astype(BF16)
        s_ref[0:1, :] += jnp.sum(da * t["dt"], axis=0, keepdims=True)
        s_ref[1:2, :] += _head_sum(jnp.sum(dy * xs, axis=0, keepdims=True), ex)
        s_ref[2:3, :] += jnp.sum(ddtr, axis=0, keepdims=True)

        @pl.when(step == nc - 1)
        def _():
            s_ref[0:1, :] = s_ref[0:1, :] * t["A"]

    rev = lambda c: (nc - 1 - c, 0)
    return pl.pallas_call(
        body, grid=(nc,),
        in_specs=[pl.BlockSpec((CH, CONVD), rev), pl.BlockSpec((CH, LANES), rev), pl.BlockSpec((8, LANES), lambda c: (0, 0)),
                  pl.BlockSpec((CH, DI), rev), pl.BlockSpec((1, NS, DI), lambda c: (nc - 1 - c, 0, 0))],
        out_specs=[pl.BlockSpec((CH, CONVD), rev), pl.BlockSpec((CH, LANES), rev), pl.BlockSpec((8, LANES), lambda c: (0, 0))],
        out_shape=[jax.ShapeDtypeStruct((L, CONVD), BF16), jax.ShapeDtypeStruct((L, LANES), F32),
                   jax.ShapeDtypeStruct((8, LANES), F32)],
        scratch_shapes=[pltpu.VMEM((NS, DI), F32), pltpu.VMEM((CH, DI), F32), pltpu.VMEM((CH, DI), F32),
                        pltpu.VMEM((CH, DI), F32)],
        compiler_params=_params(("arbitrary",)), name=name)(xbc, dtr, prm, dy, sprev)


def _gnorm_fwd(y, zx, nw, *, name):
    L = y.shape[0]
    tm = min(L, 256)

    def body(y_ref, z_ref, nw_ref, o_ref):
        z = z_ref[...].astype(F32)
        yg = y_ref[...].astype(F32) * (z * _sigmoid(z))
        for g in range(NG):
            v = yg[:, g * GW:(g + 1) * GW]
            r = lax.rsqrt(jnp.mean(v * v, axis=-1, keepdims=True) + EPS)
            o_ref[:, g * GW:(g + 1) * GW] = (v * r * nw_ref[:, g * GW:(g + 1) * GW]).astype(BF16)

    row = pl.BlockSpec((tm, DI), lambda i: (i, 0))
    return pl.pallas_call(body, grid=(L // tm,), in_specs=[row, row, pl.BlockSpec((1, DI), lambda i: (0, 0))],
                          out_specs=row, out_shape=jax.ShapeDtypeStruct((L, DI), BF16),
                          compiler_params=_params(("parallel",)), name=name)(y, zx, nw)


def _gnorm_bwd(y, zx, nw, dyn, *, name):
    L = y.shape[0]
    tm = min(L, 256)

    def body(y_ref, z_ref, nw_ref, dyn_ref, dy_ref, dz_ref, s_ref):
        @pl.when(pl.program_id(0) == 0)
        def _():
            s_ref[...] = jnp.zeros_like(s_ref)

        z, yv = z_ref[...].astype(F32), y_ref[...].astype(F32)
        sz = _sigmoid(z)
        gate = z * sz
        dgate_dz = sz * (1.0 + z * (1.0 - sz))
        for g in range(NG):
            gs = slice(g * GW, (g + 1) * GW)
            v = yv[:, gs] * gate[:, gs]
            r = lax.rsqrt(jnp.mean(v * v, axis=-1, keepdims=True) + EPS)
            vhat = v * r
            dn = dyn_ref[:, gs].astype(F32)
            s_ref[0:1, gs] += jnp.sum(dn * vhat, axis=0, keepdims=True)
            dvhat = dn * nw_ref[:, gs]
            dv = r * (dvhat - vhat * jnp.mean(dvhat * vhat, axis=-1, keepdims=True))
            dy_ref[:, gs] = dv * gate[:, gs]
            dz_ref[:, gs] = (dv * yv[:, gs] * dgate_dz[:, gs]).astype(BF16)

    row = pl.BlockSpec((tm, DI), lambda i: (i, 0))
    return pl.pallas_call(body, grid=(L // tm,), in_specs=[row, row, pl.BlockSpec((1, DI), lambda i: (0, 0)), row],
                          out_specs=[row, row, pl.BlockSpec((8, DI), lambda i: (0, 0))],
                          out_shape=[jax.ShapeDtypeStruct((L, DI), F32), jax.ShapeDtypeStruct((L, ZX), BF16),
                                     jax.ShapeDtypeStruct((8, DI), F32)],
                          compiler_params=_params(("arbitrary",)), name=name)(y, zx, nw, dyn)


def _adamw(w, g, m, v, *, name, g_row=0, w_row=0, rows=None, into=None, emit_g=False):
    lead = w.ndim == 3
    R, C = w.shape[-2:]
    rows = R if rows is None else rows
    tr = max([t for t in range(8, rows + 1, 8) if rows % t == 0 and t * C <= 256 * 1024], default=rows)
    assert g_row % tr == 0 and w_row % tr == 0, (name, g_row, w_row, tr)
    n_out = 4 if emit_g else 3

    def body(w_ref, g_ref, m_ref, v_ref, *rest):
        outs = rest[-n_out:]
        gv = g_ref[...]
        mn = ADAM_B1 * m_ref[...] + (1.0 - ADAM_B1) * gv
        vn = ADAM_B2 * v_ref[...] + (1.0 - ADAM_B2) * (gv * gv)
        m_hat = mn / (1.0 - ADAM_B1 ** ADAM_STEP)
        v_hat = vn / (1.0 - ADAM_B2 ** ADAM_STEP)
        d_ref, mo_ref, vo_ref = outs[-3:]
        d_ref[...] = -ADAM_LR * (m_hat / (jnp.sqrt(v_hat) + ADAM_EPS) + ADAM_WD * w_ref[...])
        mo_ref[...] = mn
        vo_ref[...] = vn
        if emit_g:
            outs[0][...] = gv

    blk = (pl.BlockSpec((None, tr, C), lambda i: (0, i + w_row // tr, 0)) if lead
           else pl.BlockSpec((tr, C), lambda i: (i + w_row // tr, 0)))
    args, in_specs, alias = [w, g, m, v], [blk, pl.BlockSpec((tr, C), lambda i: (i + g_row // tr, 0)), blk, blk], {}
    if into is not None:
        args, in_specs, alias = args + list(into), in_specs + [ANY] * n_out, {4 + k: k for k in range(n_out)}
    return pl.pallas_call(body, grid=(rows // tr,), in_specs=in_specs, out_specs=[blk] * n_out,
                          out_shape=[jax.ShapeDtypeStruct(w.shape, F32)] * n_out, input_output_aliases=alias,
                          compiler_params=_params(("parallel",)), name=name)(*args)


def _residual(acc, xv, gv):
    return xv + gv * acc, acc


def _like(buf):
    return jax.ShapeDtypeStruct(buf.shape, buf.dtype)


def _mlp_fwd(x, mod, nw, wb, up_row, down_row, tag, midway=None):
    sh, sc, g = mod
    h = _modnorm_fwd(x, nw, sc, sh, name=tag + "_norm")
    a = _matmul(h, wb, n=DFF, tm=TM_ALL, b_spec=pl.BlockSpec((None, D, 512), lambda mi, j: (j // 2, up_row // D, j % 2)),
                epi=lambda acc: (jnp.maximum(acc, 0.0),), out_dtypes=(BF16,), name=tag + "_up")
    if midway is not None:
        midway(a)
    xn, y = _matmul(a, wb, n=D, tm=TM_HALF, contract=_nn_split_sq,
                    b_spec=pl.BlockSpec((N_CHIPS, D, 512), lambda mi, j: (0, down_row // D, j)),
                    extras=(x, g), epi=_residual, out_dtypes=(F32, BF16), name=tag + "_down")
    return xn, (x, h, a, y)


def _mlp_bwd(dxo, dy, gsum, saved, mod, nw, wb, gb, up_row, down_row, below, tag):
    x, h, a, y = saved
    sh, sc, g = mod
    du = _matmul(dy, wb, n=DFF, tm=TM_ALL, contract=_nt,
                 b_spec=pl.BlockSpec((None, 512, D), lambda mi, j: (j // 2, down_row // 512 + j % 2, 0)),
                 extras=(a,), epi=lambda acc, av: (acc * (2.0 * av.astype(F32)),), out_dtypes=(BF16,), name=tag + "_dact")
    gb = _matmul_tn(a, dy, m=DFF, n=D, tm=D, tn=D, a_square=True, into=gb, out_struct=_like(wb),
                    out_spec=pl.BlockSpec((None, D, D), lambda mi, j: (mi, down_row // D, 0)), name=tag + "_ddown")
    dh = _matmul(du, wb, n=D, tm=TM_HALF, contract=_nt_split,
                 b_spec=pl.BlockSpec((N_CHIPS, 512, D), lambda mi, j: (0, up_row // 512 + j, 0)), out_dtypes=(BF16,),
                 name=tag + "_dh")
    gb = _matmul_tn(h, du, m=D, n=DFF, tm=D, into=gb, out_struct=_like(wb),
                    out_spec=pl.BlockSpec((None, D, 512), lambda mi, j: (j // 2, up_row // D, j % 2)), name=tag + "_dup")
    dx, sums, *nxt = _modnorm_bwd(x, dh, dxo, nw, sc, gsum, below, name=tag + "_dnorm")
    return dx, gb, sums, *nxt


def _ssd_fwd_scan(x, mod, nw, w_in_t, w_dt_t, conv_w, conv_b, prm, tag):
    sh, sc, g = mod
    h = _modnorm_fwd(x, nw, sc, sh, name=tag + "_norm")
    zx = _matmul(h, w_in_t, n=ZX, tm=TM_ALL, contract=_nt, out_dtypes=(BF16,), name=tag + "_in")
    dtr = _matmul(h, w_dt_t, n=LANES, tm=TM_ALL, contract=_nt, name=tag + "_in_dt")
    xbc, pre = _ssd_conv_fwd(zx, conv_w, conv_b, name=tag + "_conv")
    y, sprev = _ssd_fwd(xbc, dtr, prm, name=tag + "_scan")
    return h, zx, dtr, xbc, y, sprev, pre


def _ssd_fwd_out(x, mod, scan, gn_w, get_w_out, tag):
    sh, sc, g = mod
    h, zx, dtr, xbc, y, sprev, pre = scan
    yn = _gnorm_fwd(y, zx, gn_w, name=tag + "_gnorm")
    w_out = get_w_out(yn)
    xn, yo = _matmul(yn, w_out, n=D, tm=TM_HALF, contract=_nn_split,
                     b_spec=pl.BlockSpec((N_CHIPS, 512, 512), lambda mi, j: (0, 0, j)),
                     extras=(x, g), epi=_residual, out_dtypes=(F32, BF16), name=tag + "_out")
    return xn, (x, h, zx, dtr, xbc, y, sprev, yn, yo, pre)


def _ssd_bwd_out(dyo, saved, w_out, tag, after):
    x, h, zx, dtr, xbc, y, sprev, yn, yo, pre = saved
    dyn = _matmul(dyo, w_out, n=DI, tm=TM_ALL, contract=_nt, b_spec=pl.BlockSpec((None, 512, D), lambda mi, j: (j, 0, 0)),
                  extras=(jnp.broadcast_to(after[0:1, 0:1], (1, DI)),), epi=lambda acc, t: (acc + t,),
                  out_dtypes=(BF16,), name=tag + "_dyn")
    g_out = _matmul_tn(yn, dyo, m=DI, n=D, tn=D, out_struct=_like(w_out),
                       out_spec=pl.BlockSpec((None, 512, D), lambda mi, j: (mi, 0, 0)), name=tag + "_dout")
    return dyn, g_out


def _ssd_bwd_rest(dxo, dy, dzx, gsum, saved, mod, nw, w_in_t, w_dt_t, conv_w, prm, tag):
    x, h, zx, dtr, xbc, y, sprev, yn, yo, pre = saved
    sh, sc, g = mod
    dxbc, ddtr, ssum = _ssd_bwd(xbc, dtr, prm, dy, sprev, name=tag + "_dscan")
    dzx, csum = _ssd_conv_bwd(zx, pre, dxbc, conv_w, dzx, name=tag + "_dconv")
    dh_dt = _matmul(ddtr, w_dt_t, n=D, tm=TM_ALL, name=tag + "_dh_dt")
    dh = _matmul(dzx, w_in_t, n=D, tm=TM_HALF, b_spec=pl.BlockSpec((ZX, 512), lambda mi, j: (0, j)), extras=(dh_dt,),
                 epi=lambda acc, e: (acc + e,), out_dtypes=(BF16,), name=tag + "_dh")
    d_w_zx = _matmul_tn(h, dzx, m=D, n=ZX, tm=D, out_spec=pl.BlockSpec((D, 512), lambda i, j: (i, j)),
                        out_struct=jax.ShapeDtypeStruct((D, ZX), BF16), name=tag + "_din")
    d_w_dt = _matmul_tn(h, ddtr, m=D, n=LANES, tm=D, name=tag + "_din_dt")
    dx, sums = _modnorm_bwd(x, dh, dxo, nw, sc, gsum, None, name=tag + "_dnorm")
    return dx, d_w_zx, d_w_dt, sums, csum, ssum


def _sc_layer_fwd(x, mod, nw, w_sc_in, conv_w, wb, out_row, tag, midway=None):
    sh, sc, g = mod
    h = _modnorm_fwd(x, nw, sc, sh, name=tag + "_norm")
    proj = _matmul(h, w_sc_in, n=3 * D, tm=TM_ALL, tn=256, out_dtypes=(BF16,),
                   b_spec=pl.BlockSpec((None, D, 256), lambda mi, j: (j // 3, 0, j % 3)),
                   name=tag + "_in")
    if midway is not None:
        midway(proj)
    yv, v = _sc_fwd(proj, conv_w, name=tag + "_conv")
    xn, yo = _matmul(yv, wb, n=D, tm=TM_HALF, contract=_nn_split,
                     b_spec=pl.BlockSpec((N_CHIPS, 256, 512), lambda mi, j: (0, out_row // 256, j)),
                     extras=(x, g), epi=_residual, out_dtypes=(F32, BF16), name=tag + "_out")
    return xn, (x, h, proj, yv, yo, v)


def _sc_layer_bwd(dxo, dyo, gsum, saved, mod, nw, w_sc_in, conv_w, wb, gb, out_row, below, tag):
    x, h, proj, yv, yo, v = saved
    sh, sc, g = mod
    L = x.shape[0]
    dyv = _matmul(dyo, wb, n=D, tm=TM_ALL, tn=256, contract=_nt,
                  b_spec=pl.BlockSpec((None, 256, D), lambda mi, j: (j, out_row // 256, 0)), out_dtypes=(BF16,),
                  name=tag + "_dyv")
    gb = _matmul_tn(yv, dyo, m=D, n=D, tm=256, tn=D, into=gb, out_struct=_like(wb),
                    out_spec=pl.BlockSpec((None, 256, D), lambda mi, j: (mi, out_row // 256, 0)), name=tag + "_dout")
    dproj, csum = _sc_bwd(proj, v, dyv, conv_w, name=tag + "_dconv")
    tm = min(L, TM_HALF)
    dh = _matmul(dproj, w_sc_in, n=D, tm=tm, contract=_nt_sc_in, a_spec=pl.BlockSpec((3, tm, D), lambda mi, j: (0, mi, 0)),
                 b_spec=pl.BlockSpec((N_CHIPS, 512, SC_IN_SHARD), lambda mi, j: (0, j, 0)), out_dtypes=(BF16,),
                 name=tag + "_dh")
    g_sc_in = _matmul_tn(h, dproj, m=D, n=3 * D, tm=D, tn=256, b_spec=pl.BlockSpec((None, L, 256), lambda mi, j: (j // 4, 0, j % 4)),
                         out_spec=pl.BlockSpec((None, D, 256), lambda mi, j: (j // 3, 0, j % 3)),
                         out_struct=jax.ShapeDtypeStruct((N_CHIPS, D, SC_IN_SHARD), BF16), name=tag + "_din")
    dx, sums, *nxt = _modnorm_bwd(x, dh, dxo, nw, sc, gsum, below, name=tag + "_dnorm")
    return dx, gb, g_sc_in, sums, csum, *nxt


SUB_ROW = (0, 8, 16, 24)
SSD_CONV_ROW, GNORM_ROW, FINAL_ROW, SC_CONV_ROW, HEAD_ROW, SMALL_ROWS = 32, 48, 56, 64, 72, 80


def _all_gather_rows(blk, *, name):
    m_per, n = blk.shape

    def body(x_ref, out_ref, send_sems, recv_sems, local_sem):
        x, y, c = lax.axis_index("x"), lax.axis_index("y"), lax.axis_index("c")
        me, sibling = (x, y, c), (x, y, 1 - c)
        chips = [(1 - x, y), (x, 1 - y), (1 - x, 1 - y)]

        def rows(px, py, pc):
            return out_ref.at[pl.ds((4 * px + 2 * py + pc) * m_per, m_per), :]

        def copy(k, block, to, src=None):
            return pltpu.make_async_remote_copy(src_ref=rows(*block) if src is None else src, dst_ref=rows(*block),
                                                send_sem=send_sems.at[k], recv_sem=recv_sems.at[k], device_id=to,
                                                device_id_type=MESH)

        mine = pltpu.make_async_copy(x_ref, rows(*me), local_sem)
        mine.start()
        first = [copy(0, me, sibling, src=x_ref)] + [copy(1 + j, me, (*chip, c), src=x_ref) for j, chip in enumerate(chips)]
        for cp in first:
            cp.start()
        passed = [copy(4 + j, (*chip, c), sibling) for j, chip in enumerate(chips)]
        for j, chip in enumerate(chips):
            copy(1 + j, (*chip, c), me).wait_recv()
            passed[j].start()
        copy(0, sibling, me).wait_recv()
        for j, chip in enumerate(chips):
            copy(4 + j, (*chip, 1 - c), me).wait_recv()
        for cp in first + passed:
            cp.wait_send()
        mine.wait()

    return pl.pallas_call(
        body, out_shape=jax.ShapeDtypeStruct((N_DEV * m_per, n), blk.dtype),
        in_specs=[pl.BlockSpec(memory_space=pltpu.VMEM)], out_specs=pl.BlockSpec(memory_space=pltpu.VMEM),
        scratch_shapes=[pltpu.SemaphoreType.DMA((7,)), pltpu.SemaphoreType.DMA((7,)), pltpu.SemaphoreType.DMA],
        name=name)(blk)


def _half(ref, chip, c):
    r, n = ref.shape[1:]
    if r % 32 == 0:
        return ref.at[chip, pl.ds(c * (r // 2), r // 2), :]
    assert n % 256 == 0, ref.shape
    return ref.at[chip, :, pl.ds(c * (n // 2), n // 2)]


def _gather_copy(bufs, sends, recvs, b, k, chip, pc, to):
    piece = _half(bufs[b], 2 * chip[0] + chip[1], pc)
    return pltpu.make_async_remote_copy(src_ref=piece, dst_ref=piece, send_sem=sends.at[4 * b + k], recv_sem=recvs.at[4 * b + k],
                                        device_id=to, device_id_type=MESH)


def _split_call(body, bufs, sems_in, n_sems, *, name, after=(), token=False, lands=()):
    nb, na, nl, starts = len(bufs), len(after), len(lands), not sems_in

    def wrapped(*refs):
        sems = refs[nb + na:nb + na + 2] if starts else refs[nb:nb + 2]
        made = refs[nb + na + 2 + nb:nb + na + 2 + nb + nl] if starts else ()
        body(tuple(refs[:nb]) + tuple(made), sems[0], sems[1])
        if token:
            refs[-1][...] = jnp.zeros_like(refs[-1])

    out_shape = [pltpu.SemaphoreType.DMA((n_sems,)) for _ in range(2 if starts else 0)]
    out_specs = [SEM] * len(out_shape) + [ANY] * (nb + nl)
    alias = {b: len(out_shape) + b for b in range(nb)}
    out_shape += [jax.ShapeDtypeStruct(b.shape, b.dtype) for b in bufs] + list(lands)
    if token:
        out_shape.append(jax.ShapeDtypeStruct((8, LANES), F32))
        out_specs.append(pl.BlockSpec(memory_space=pltpu.VMEM))
    return pl.pallas_call(
        wrapped, out_shape=out_shape, in_specs=[ANY] * nb + [SEM] * len(sems_in) + [ANY] * na, out_specs=out_specs,
        input_output_aliases=alias,
        compiler_params=pltpu.CompilerParams(has_side_effects=pltpu.SideEffectType.DATAFLOW_SIDE_EFFECTING),
        name=name)(*bufs, *sems_in, *after)


def _gather_start(bufs, *, name, after=()):
    nb = len(bufs)

    def body(ins, sends, recvs):
        x, y, c = lax.axis_index("x"), lax.axis_index("y"), lax.axis_index("c")
        chips = [(1 - x, y), (x, 1 - y), (1 - x, 1 - y)]
        for b in range(nb):
            _gather_copy(ins, sends, recvs, b, 0, (x, y), c, (x, y, 1 - c)).start()
            for j, chip in enumerate(chips):
                _gather_copy(ins, sends, recvs, b, 1 + j, (x, y), c, (*chip, c)).start()

    out = _split_call(body, bufs, (), 4 * nb, name=name, after=after, token=True)
    return (out[0], out[1], out[2:2 + nb]), out[-1]


def _gather_wait_first(flight, *, name, after=()):
    sends, recvs, bufs = flight
    nb = len(bufs)

    def body(ins, sends_, recvs_):
        x, y, c = lax.axis_index("x"), lax.axis_index("y"), lax.axis_index("c")
        chips = [(1 - x, y), (x, 1 - y), (1 - x, 1 - y)]
        for b in range(nb):
            _gather_copy(ins, sends_, recvs_, b, 0, (x, y), c, (x, y, 1 - c)).wait_send()
            _gather_copy(ins, sends_, recvs_, b, 0, (x, y), 1 - c, (x, y, c)).wait_recv()
            for j, chip in enumerate(chips):
                _gather_copy(ins, sends_, recvs_, b, 1 + j, (x, y), c, (*chip, c)).wait_send()
                _gather_copy(ins, sends_, recvs_, b, 1 + j, chip, c, (x, y, c)).wait_recv()

    return _split_call(body, bufs, (sends, recvs), 4 * nb, name=name, after=after)


def _gather_forward(bufs, *, name):
    nb = len(bufs)

    def body(ins, sends, recvs):
        x, y, c = lax.axis_index("x"), lax.axis_index("y"), lax.axis_index("c")
        chips = [(1 - x, y), (x, 1 - y), (1 - x, 1 - y)]
        for b in range(nb):
            for j, chip in enumerate(chips):
                _gather_copy(ins, sends, recvs, b, 1 + j, chip, c, (x, y, 1 - c)).start()

    out = _split_call(body, bufs, (), 4 * nb, name=name)
    return out[0], out[1], out[2:2 + nb]


def _gather_wait_forward(flight, *, name, after=()):
    sends, recvs, bufs = flight
    nb = len(bufs)

    def body(ins, sends_, recvs_):
        x, y, c = lax.axis_index("x"), lax.axis_index("y"), lax.axis_index("c")
        chips = [(1 - x, y), (x, 1 - y), (1 - x, 1 - y)]
        for b in range(nb):
            for j, chip in enumerate(chips):
                _gather_copy(ins, sends_, recvs_, b, 1 + j, chip, c, (x, y, 1 - c)).wait_send()
                _gather_copy(ins, sends_, recvs_, b, 1 + j, chip, 1 - c, (x, y, c)).wait_recv()

    return _split_call(body, bufs, (sends, recvs), 4 * nb, name=name, after=after)


def _owner_copies(hs, lands, sends, recvs):
    x, y, c = lax.axis_index("x"), lax.axis_index("y"), lax.axis_index("c")
    chips = [(1 - x, y), (x, 1 - y), (1 - x, 1 - y)]
    return [pltpu.make_async_remote_copy(src_ref=hs[b].at[2 * cx + cy], dst_ref=lands[b].at[j], send_sem=sends.at[3 * b + j],
                                         recv_sem=recvs.at[3 * b + j], device_id=(cx, cy, c), device_id_type=MESH)
            for b in range(len(hs)) for j, (cx, cy) in enumerate(chips)]


def _owners_start(hs, *, name):
    nb = len(hs)
    lands = [jax.ShapeDtypeStruct((3,) + h.shape[1:], h.dtype) for h in hs]

    def body(refs, sends, recvs):
        for cp in _owner_copies(refs[:nb], refs[nb:], sends, recvs):
            cp.start()

    out = _split_call(body, list(hs), (), 3 * nb, name=name, token=True, lands=lands)
    return (out[0], out[1], out[2:2 + 2 * nb]), out[-1]


def _owners_wait(flight, *, name, after=()):
    sends, recvs, bufs = flight
    nb = len(bufs) // 2

    def body(refs, sends_, recvs_):
        for cp in _owner_copies(refs[:nb], refs[nb:], sends_, recvs_):
            cp.wait()

    out = _split_call(body, bufs, (sends, recvs), 3 * nb, name=name, after=after)
    return out[:nb], out[nb:]


def _sibling_copies(gs, lands, sends, recvs):
    x, y, c = lax.axis_index("x"), lax.axis_index("y"), lax.axis_index("c")
    copies = []
    for b in range(len(gs)):
        hr = gs[b].shape[1] // 2
        copies.append(pltpu.make_async_remote_copy(
            src_ref=gs[b].at[:, pl.ds((1 - c) * hr, hr), :], dst_ref=lands[b], send_sem=sends.at[b], recv_sem=recvs.at[b],
            device_id=(x, y, 1 - c), device_id_type=MESH))
    return copies


def _sibling_start(gs, *, name, after=()):
    nb = len(gs)
    lands = [jax.ShapeDtypeStruct((g.shape[0], g.shape[1] // 2, g.shape[2]), g.dtype) for g in gs]

    def body(refs, sends, recvs):
        for cp in _sibling_copies(refs[:nb], refs[nb:], sends, recvs):
            cp.start()

    out = _split_call(body, list(gs), (), nb, name=name, after=after, token=True, lands=lands)
    return (out[0], out[1], out[2:2 + 2 * nb]), out[-1]


def _sibling_wait(flight, *, name, after=()):
    sends, recvs, bufs = flight
    nb = len(bufs) // 2

    def body(refs, sends_, recvs_):
        for cp in _sibling_copies(refs[:nb], refs[nb:], sends_, recvs_):
            cp.wait()

    out = _split_call(body, bufs, (sends, recvs), nb, name=name, after=after)
    return out[:nb], out[nb:]


def _result_copies(ts, sends, recvs):
    x, y, c = lax.axis_index("x"), lax.axis_index("y"), lax.axis_index("c")
    return [pltpu.make_async_remote_copy(src_ref=ts[b].at[c], dst_ref=ts[b].at[c], send_sem=sends.at[b], recv_sem=recvs.at[b],
                                         device_id=(x, y, 1 - c), device_id_type=MESH) for b in range(len(ts))]


def _result_start(ts, *, name):
    def body(refs, sends, recvs):
        for cp in _result_copies(refs, sends, recvs):
            cp.start()

    out = _split_call(body, ts, (), len(ts), name=name, token=True)
    return (out[0], out[1], out[2:2 + len(ts)]), out[-1]


def _result_wait(flight, *, name, after=()):
    sends, recvs, bufs = flight

    def body(refs, sends_, recvs_):
        for cp in _result_copies(refs, sends_, recvs_):
            cp.wait()

    return _split_call(body, bufs, (sends, recvs), len(bufs), name=name, after=after)


def _row_tile(rows, cols):
    best = 16
    for t in range(16, rows + 1, 16):
        if rows % t == 0 and t * cols <= 640 * 1024:
            best = t
    assert rows % best == 0, (rows, cols)
    return best


def _add_sibling_half(g, recv, core, *, name):
    nk, r, n = g.shape
    hr = r // 2
    tr = _row_tile(hr, n)

    def body(c_ref, a_ref, b_ref, o_ref):
        o_ref[...] = (a_ref[...].astype(F32) + b_ref[...].astype(F32)).astype(BF16)

    grid_spec = pltpu.PrefetchScalarGridSpec(
        num_scalar_prefetch=1, grid=(nk, hr // tr),
        in_specs=[pl.BlockSpec((None, tr, n), lambda k, i, c_ref: (k, c_ref[0] * (hr // tr) + i, 0)),
                  pl.BlockSpec((None, tr, n), lambda k, i, c_ref: (k, i, 0))],
        out_specs=pl.BlockSpec((None, tr, n), lambda k, i, c_ref: (k, i, 0)))
    return pl.pallas_call(body, grid_spec=grid_spec, out_shape=jax.ShapeDtypeStruct((nk, hr, n), BF16),
                          compiler_params=_params(("parallel", "parallel")), name=name)(core, g, recv)


def _add_chip_sums(h, recv, chip_core, *, name):
    _, hr, n = h.shape
    tr = _row_tile(hr, n)

    def body(k_ref, a_ref, b_ref, o_ref):
        o_ref[...] = ((a_ref[...].astype(F32) + b_ref[0].astype(F32)) + b_ref[1].astype(F32)) + b_ref[2].astype(F32)

    grid_spec = pltpu.PrefetchScalarGridSpec(
        num_scalar_prefetch=1, grid=(hr // tr,),
        in_specs=[pl.BlockSpec((None, tr, n), lambda i, k_ref: (k_ref[0], i, 0)),
                  pl.BlockSpec((3, tr, n), lambda i, k_ref: (0, i, 0))],
        out_specs=pl.BlockSpec((None, tr, n), lambda i, k_ref: (k_ref[1], i, 0)))
    return pl.pallas_call(body, grid_spec=grid_spec, out_shape=jax.ShapeDtypeStruct((2, hr, n), F32),
                          compiler_params=_params(("parallel",)), name=name)(chip_core, h, recv)


def _sum_devices(g, *, name):
    nd, r, n = g.shape

    def body(g_ref, o_ref):
        acc = g_ref[0]
        for i in range(1, nd):
            acc = acc + g_ref[i]
        o_ref[...] = acc

    return pl.pallas_call(body, out_shape=jax.ShapeDtypeStruct((r, n), F32), name=name)(g)


def _own_slot(parts, chip, *, name, after=()):
    rows, cols = sum(w.shape[1] for w, _ in parts), parts[0][0].shape[2]
    buf, row0 = None, 0
    for p, (w, idx) in enumerate(parts):
        r = w.shape[1]
        tr = 256 if r % 256 == 0 else r
        assert row0 % tr == 0, (name, r, row0)
        prev = () if buf is None else (buf,)

        def body(chip_ref, w_ref, *rest):
            rest[-1][...] = w_ref[...].astype(BF16)

        grid_spec = pltpu.PrefetchScalarGridSpec(
            num_scalar_prefetch=1, grid=(r // tr,),
            in_specs=[pl.BlockSpec((None, tr, cols), lambda i, c_ref, idx=idx: (idx, i, 0))] + [ANY] * (len(prev) + len(after)),
            out_specs=pl.BlockSpec((None, tr, cols), lambda i, c_ref, row0=row0, tr=tr: (c_ref[0], row0 // tr + i, 0)))
        buf = pl.pallas_call(body, grid_spec=grid_spec, out_shape=jax.ShapeDtypeStruct((N_CHIPS, rows, cols), BF16),
                             input_output_aliases={2: 0} if prev else {}, compiler_params=_params(("parallel",)),
                             name=f"{name}{p}")(chip, w, *prev, *after)
        row0 += r
    return buf


def kernel(x, c, ada_w, ada_b, mix_norm_w, mlp_norm_w, mlp_up, mlp_down, ssd_in_w, ssd_conv_w, ssd_conv_b, ssd_dt_bias, ssd_A_log, ssd_D, ssd_norm_w, ssd_out_w, sc_in_w, sc_conv_w, sc_out_w, final_norm_w, loss_target, m_ada_w, m_ada_b, m_mix_norm_w, m_mlp_norm_w, m_mlp_up, m_mlp_down, m_ssd_in_w, m_ssd_conv_w, m_ssd_conv_b, m_ssd_dt_bias, m_ssd_A_log, m_ssd_D, m_ssd_norm_w, m_ssd_out_w, m_sc_in_w, m_sc_conv_w, m_sc_out_w, m_final_norm_w, v_ada_w, v_ada_b, v_mix_norm_w, v_mlp_norm_w, v_mlp_up, v_mlp_down, v_ssd_in_w, v_ssd_conv_w, v_ssd_conv_b, v_ssd_dt_bias, v_ssd_A_log, v_ssd_D, v_ssd_norm_w, v_ssd_out_w, v_sc_in_w, v_sc_conv_w, v_sc_out_w, v_final_norm_w):
    xi, yi, ci = lax.axis_index("x"), lax.axis_index("y"), lax.axis_index("c")
    chip = 2 * xi + yi
    dev = 2 * chip + ci
    n_ada = ada_w.shape[2]

    conv_flat = jnp.concatenate([ssd_conv_w.reshape(-1), sc_conv_w.reshape(-1), jnp.zeros((256,), F32)]).reshape(4, D)
    blk0 = jnp.concatenate([c, conv_flat, jnp.zeros((3, D), F32)], axis=0)
    got0 = _all_gather_rows(blk0, name="gather_cond").reshape(N_DEV, 8, D)
    c_all = got0[:, 0]
    conv_all = got0[0::2, 1:5].reshape(N_CHIPS, 4 * D)
    ssd_conv = jnp.moveaxis(conv_all[:, :4 * 768].reshape(N_CHIPS, 4, 768), 0, 1).reshape(4, CONVD)
    sc_conv = jnp.moveaxis(conv_all[:, 4 * 768:4 * 768 + 3 * 256].reshape(N_CHIPS, 3, 256), 0, 1).reshape(3, D)
    mod_shard = [_matmul(c_all, ada_w, n=n_ada, a_silu=True, b_spec=pl.BlockSpec((None, D, 512), lambda mi, j, i=i: (i, 0, j)),
                         extras=(lax.dynamic_slice(ada_b, (i, chip * n_ada), (1, n_ada)),),
                         epi=lambda acc, b: (acc + b,), name=f"ada_mod{i}") for i in range(2)]
    mod_slot = lax.dynamic_update_slice(jnp.zeros((N_CHIPS, 2 * N_DEV, n_ada), F32), jnp.concatenate(mod_shard, axis=0)[None],
                                        (chip, 0, 0))

    up_row, down_row = 0, D
    chip1 = chip.reshape(1).astype(jnp.int32)
    a_bufs = [mod_slot, _own_slot([(jnp.swapaxes(ssd_in_w, 1, 2), 0)], chip1, name="slot_ssd_in")]
    fly_a, tok = _gather_start(a_bufs, name="gather_a_start")
    b_bufs = [_own_slot([(ssd_out_w, 0)], chip1, name="slot_ssd_out", after=(tok,)),
              _own_slot([(mlp_up, 0), (mlp_down, 0)], chip1, name="slot_mlp0_", after=(tok,))]
    fly_b, tok = _gather_start(b_bufs, name="gather_b_start", after=(tok,))
    c_bufs = [_own_slot([(sc_in_w, 0)], chip1, name="slot_sc_in", after=(tok,)),
              _own_slot([(sc_out_w, 0)], chip1, name="slot_sc_out", after=(tok,))]
    fly_c, tok = _gather_start(c_bufs, name="gather_c_start", after=(tok,))
    d_bufs = [_own_slot([(mlp_up, 1), (mlp_down, 1)], chip1, name="slot_mlp1_", after=(tok,))]
    fly_d, tok = _gather_start(d_bufs, name="gather_d_start", after=(tok,))

    row = lambda v: v.reshape(1, -1)
    xs, tgt = x[0], loss_target[0]
    prm = jnp.pad(jnp.concatenate([ssd_dt_bias, ssd_A_log, ssd_D, jnp.zeros((5, NH), F32)], axis=0), ((0, 0), (0, LANES - NH)))
    mix_nw = [row(mix_norm_w[i]) for i in range(2)]
    mlp_nw = [row(mlp_norm_w[i]) for i in range(2)]
    a_bufs = _gather_wait_first(fly_a, name="gather_a_landed", after=(tok,))
    mod_all, w_ssd_in = _gather_wait_forward(_gather_forward(a_bufs, name="gather_a_pass"), name="gather_a_done")
    mod = lax.dynamic_index_in_dim(mod_all.reshape(N_CHIPS, 2, N_DEV, n_ada), dev, axis=2, keepdims=False)
    mod = jnp.moveaxis(mod, 0, 1).reshape(2, 6, D)
    mods = [[mod[i, j:j + 1] for j in range(6)] for i in range(2)]
    w_in_t = w_ssd_in.reshape(N_CHIPS * SSD_IN_SHARD, D)
    w_dt_t = jnp.pad(w_in_t[ZX:], ((0, LANES - NH), (0, 0)))
    scan = _ssd_fwd_scan(xs, mods[0][0:3], mix_nw[0], w_in_t, w_dt_t, ssd_conv, ssd_conv_b, prm, "ssd")

    def land(flight, tag, after):
        return _gather_forward(_gather_wait_first(flight, name=f"gather_{tag}_landed", after=(after,)), name=f"gather_{tag}_pass")

    passed, got = {"b": land(fly_b, "b", scan[4])}, {}

    def done(tag, after):
        got[tag] = _gather_wait_forward(passed[tag], name=f"gather_{tag}_done", after=(after,))
        return got[tag]

    x1, s_ssd = _ssd_fwd_out(xs, mods[0][0:3], scan, ssd_norm_w, lambda yn: done("b", yn)[0], "ssd")
    w_ssd_out, w_b = got["b"]
    x2, s_mlp0 = _mlp_fwd(x1, mods[0][3:6], mlp_nw[0], w_b, up_row, down_row, "mlp0",
                          midway=lambda a: passed.update(c=land(fly_c, "c", a)))
    w_sc_in, w_sc_out = done("c", x2)
    x3, s_sc = _sc_layer_fwd(x2, mods[1][0:3], mix_nw[1], w_sc_in, sc_conv, w_sc_out, 0, "sc",
                             midway=lambda proj: passed.update(d=land(fly_d, "d", proj)))
    (w_mlp1,) = done("d", x3)
    x4, s_mlp1 = _mlp_fwd(x3, mods[1][3:6], mlp_nw[1], w_mlp1, up_row, down_row, "mlp1")

    core = ci.reshape(1).astype(jnp.int32)
    chip_core = jnp.stack([chip, ci]).astype(jnp.int32)

    def reduce_swap(gbufs, tag, after=()):
        return _sibling_start(gbufs, name=tag + "_sibling_start", after=after)

    def reduce_send(flight, tag, after):
        gs, sib = _sibling_wait(flight, name=tag + "_sibling_landed", after=after)
        hs = [_add_sibling_half(g, s, core, name=f"{tag}_add_sibling{b}") for b, (g, s) in enumerate(zip(gs, sib))]
        return _owners_start(hs, name=tag + "_owners_start")

    def reduce_sum(flight, tag, after):
        hs, lands = _owners_wait(flight, name=tag + "_owners_landed", after=after)
        ts = [_add_chip_sums(h, o, chip_core, name=f"{tag}_add_chips{b}") for b, (h, o) in enumerate(zip(hs, lands))]
        return _result_start(ts, name=tag + "_result_start")

    def reduce_done(flight, tag, after=()):
        return [t.reshape(-1, t.shape[2]) for t in _result_wait(flight, name=tag + "_result_landed", after=after)]

    dx4, fsum, dy, gs = _final_loss(x4, row(final_norm_w), tgt, (mods[1][5], s_mlp1[3]), name="final_loss")
    dx3, g_mlp1, sum_mlp1, dy, gs = _mlp_bwd(dx4, dy, gs, s_mlp1, mods[1][3:6], mlp_nw[1], w_mlp1, None, up_row, down_row,
                                             (mods[1][2], s_sc[4]), "mlp1")
    dx2, g_sc_out, g_sc_in, sum_sc, sc_csum, dy, gs = _sc_layer_bwd(dx3, dy, gs, s_sc, mods[1][0:3], mix_nw[1], w_sc_in,
                                                                    sc_conv, w_sc_out, None, 0, (mods[0][5], s_mlp0[3]), "sc")
    dx1, g_b, sum_mlp0, dy, gsum_ssd = _mlp_bwd(dx2, dy, gs, s_mlp0, mods[0][3:6], mlp_nw[0], w_b, None, up_row, down_row,
                                                (mods[0][2], s_ssd[8]), "mlp0")
    fly_1, tok = reduce_swap([g_mlp1, g_sc_out, g_sc_in, g_b], "rs1")
    dyn, g_ssd_out = _ssd_bwd_out(dy, s_ssd, w_ssd_out, "ssd", tok)
    fly_1, tok = reduce_send(fly_1, "rs1", (g_ssd_out,))
    dy, dzx, gnsum = _gnorm_bwd(s_ssd[5], s_ssd[2], ssd_norm_w + tok[0:1, 0:1], dyn, name="ssd_dgnorm")
    grad_x, d_w_zx, d_w_dt, sum_ssd, csum, ssum = _ssd_bwd_rest(
        dx1, dy, dzx, gsum_ssd, s_ssd, mods[0][0:3], mix_nw[0], w_in_t, w_dt_t, ssd_conv, prm, "ssd")

    def ssd_in_owner(k):
        lo, hi = k * SSD_IN_SHARD, (k + 1) * SSD_IN_SHARD
        if hi <= ZX:
            return d_w_zx[:, lo:hi]
        return jnp.concatenate([d_w_zx[:, lo:], d_w_dt[:, :hi - ZX].astype(BF16)], axis=1)

    small = jnp.concatenate([sum_ssd, sum_mlp0, sum_sc, sum_mlp1, csum.reshape(24, D)[0:16], gnsum.reshape(16, D)[0:8],
                             fsum, sc_csum, jnp.pad(ssum, ((0, 0), (0, D - LANES)))], axis=0)
    small_slot = lax.dynamic_update_slice(jnp.zeros((N_CHIPS, 2 * SMALL_ROWS, D), F32), small[None], (chip, ci * SMALL_ROWS, 0))
    fly_2, tok = reduce_swap([jnp.stack([ssd_in_owner(k) for k in range(N_CHIPS)]), g_ssd_out], "rs2")
    fly_s, tok = _gather_start([small_slot], name="gather_small_start", after=(tok,))
    fly_1, tok = reduce_sum(fly_1, "rs1", (grad_x, tok))
    fly_2, tok = reduce_send(fly_2, "rs2", (tok,))
    fly_s = _gather_forward(_gather_wait_first(fly_s, name="gather_small_landed", after=(tok,)), name="gather_small_pass")
    (small_all,) = _gather_wait_forward(fly_s, name="gather_small_done")
    t_mlp1, t_sc_out, t_sc_in, t_b = reduce_done(fly_1, "rs1", (small_all,))
    small_all = small_all.reshape(N_DEV, SMALL_ROWS, D)
    mod_rows = [r + o for r in SUB_ROW for o in (3, 2, 0)]
    c_pad = jnp.concatenate([c_all, jnp.zeros((8, D), F32)], axis=0)
    dmod_all = jnp.stack([small_all[:, r] for r in mod_rows], axis=1).reshape(N_DEV, 2, 6 * D)
    g_ada_w = []
    for i in range(2):
        dm = lax.dynamic_slice(dmod_all[:, i], (0, chip * n_ada), (N_DEV, n_ada))
        g_ada_w.append(_matmul_tn(c_pad, jnp.concatenate([dm, jnp.zeros_like(dm)], axis=0), m=D, n=n_ada, a_silu=True,
                                  name=f"ada_dw{i}"))

    big = dict(ada_w=[(g, 0) for g in g_ada_w], mlp_up=[(t_b, up_row), (t_mlp1, up_row)],
               mlp_down=[(t_b, down_row), (t_mlp1, down_row)], ssd_out_w=None, sc_out_w=[(t_sc_out, 0)],
               sc_in_w=[(t_sc_in, 0)], ssd_in_w=None)
    weights = dict(ada_w=(ada_w, m_ada_w, v_ada_w), ada_b=(ada_b, m_ada_b, v_ada_b),
                   mix_norm_w=(mix_norm_w, m_mix_norm_w, v_mix_norm_w), mlp_norm_w=(mlp_norm_w, m_mlp_norm_w, v_mlp_norm_w),
                   mlp_up=(mlp_up, m_mlp_up, v_mlp_up), mlp_down=(mlp_down, m_mlp_down, v_mlp_down),
                   ssd_in_w=(ssd_in_w, m_ssd_in_w, v_ssd_in_w), ssd_conv_w=(ssd_conv_w, m_ssd_conv_w, v_ssd_conv_w),
                   ssd_conv_b=(ssd_conv_b, m_ssd_conv_b, v_ssd_conv_b), ssd_dt_bias=(ssd_dt_bias, m_ssd_dt_bias, v_ssd_dt_bias),
                   ssd_A_log=(ssd_A_log, m_ssd_A_log, v_ssd_A_log), ssd_D=(ssd_D, m_ssd_D, v_ssd_D),
                   ssd_norm_w=(ssd_norm_w, m_ssd_norm_w, v_ssd_norm_w), ssd_out_w=(ssd_out_w, m_ssd_out_w, v_ssd_out_w),
                   sc_in_w=(sc_in_w, m_sc_in_w, v_sc_in_w), sc_conv_w=(sc_conv_w, m_sc_conv_w, v_sc_conv_w),
                   sc_out_w=(sc_out_w, m_sc_out_w, v_sc_out_w), final_norm_w=(final_norm_w, m_final_norm_w, v_final_norm_w))
    def step(nm, parts):
        w, m, v = (t if t.shape[0] == 1 else t.reshape(-1, t.shape[-1]) for t in weights[nm])
        rows, outs = w.shape[-2] // len(parts), None
        for i, (gbuf, g_row) in enumerate(parts):
            outs = _adamw(w, gbuf, m, v, g_row=g_row, w_row=i * rows, rows=rows, into=outs, emit_g=True, name=f"adamw_{nm}{i}")
        return outs

    res = {nm: step(nm, parts) for nm, parts in big.items() if parts is not None}
    fly_2, tok = reduce_sum(fly_2, "rs2", tuple(r[1] for r in res.values()))
    tot = _sum_devices(small_all + tok[0:1, 0:1], name="sum_small")
    loss = tot[FINAL_ROW + 1, 0]
    conv_sums = tot[SSD_CONV_ROW:SSD_CONV_ROW + 15].reshape(5, CONVD)
    grads = dict(ada_b=jnp.stack([tot[r] for r in mod_rows]).reshape(2, 6 * D),
                 mix_norm_w=jnp.stack([tot[SUB_ROW[0] + 1], tot[SUB_ROW[2] + 1]]),
                 mlp_norm_w=jnp.stack([tot[SUB_ROW[1] + 1], tot[SUB_ROW[3] + 1]]),
                 ssd_conv_w=lax.dynamic_slice(conv_sums, (0, chip * 768), (4, 768))[None], ssd_conv_b=conv_sums[4:5],
                 ssd_dt_bias=tot[HEAD_ROW + 2:HEAD_ROW + 3, 0:NH], ssd_A_log=tot[HEAD_ROW:HEAD_ROW + 1, 0:NH],
                 ssd_D=tot[HEAD_ROW + 1:HEAD_ROW + 2, 0:NH], ssd_norm_w=tot[GNORM_ROW:GNORM_ROW + 2].reshape(1, DI),
                 sc_conv_w=lax.dynamic_slice(tot[SC_CONV_ROW:SC_CONV_ROW + 3], (0, chip * 256), (3, 256))[None],
                 final_norm_w=tot[FINAL_ROW])
    for nm, g in grads.items():
        w, m, v = weights[nm]
        two_d = (-1, w.shape[-1]) if w.ndim > 1 else (1, -1)
        res[nm] = (g, *_adamw(w.reshape(two_d), g.reshape(two_d), m.reshape(two_d), v.reshape(two_d), name="adamw_" + nm))
    t_ssd_in, t_ssd_out = reduce_done(fly_2, "rs2", tuple(res[nm][1] for nm in grads))
    res["ssd_out_w"] = step("ssd_out_w", [(t_ssd_out, 0)])
    w_t, m_t, v_t = (jnp.swapaxes(t[0], 0, 1) for t in weights["ssd_in_w"])
    res["ssd_in_w"] = [jnp.swapaxes(o, 0, 1) for o in _adamw(w_t, t_ssd_in.T, m_t, v_t, emit_g=True, name="adamw_ssd_in_w")]
    outs = [[res[nm][k].reshape(weights[nm][0].shape) for nm in weights] for k in range(4)]
    return (loss, grad_x[None], *outs[0], *outs[1], *outs[2], *outs[3])
```

```python
import jax
import jax.numpy as jnp
from jax import lax
from jax.experimental import pallas as pl
from jax.experimental.pallas import tpu as pltpu

F32 = jnp.float32
BF16 = jnp.bfloat16
MESH = pl.DeviceIdType.MESH

D = 1024
DFF = 4096
DI = 2048
NH = 32
HP = 64
NG = 4
NS = 128
CH = 128
CONVD = DI + 2 * NG * NS
ZX = DI + CONVD
GW = NG * NS
LANES = 128
N_CHIPS = 4
N_DEV = 8
EPS = 1e-5
ADAM_LR, ADAM_B1, ADAM_B2, ADAM_EPS, ADAM_WD, ADAM_STEP = 1e-3, 0.9, 0.999, 1e-8, 0.01, 10
VMEM_LIMIT = 48 * 1024 * 1024
TM_ALL = 2048
TM_HALF = 1024
ANY = pl.BlockSpec(memory_space=pl.ANY)
SEM = pl.BlockSpec(memory_space=pltpu.SEMAPHORE)

SSD_IN_SHARD = 1288
SC_IN_SHARD = 768


def _params(sem=None):
    return pltpu.CompilerParams(dimension_semantics=sem, vmem_limit_bytes=VMEM_LIMIT)


def _sigmoid(v):
    return 0.5 * jnp.tanh(0.5 * v) + 0.5


def _dot(a, b, dims=((1,), (0,)), precision=None):
    return lax.dot_general(a, b, (dims, ((), ())), preferred_element_type=F32, precision=precision)


def _dot_nt(a, b):
    return _dot(a, b, ((1,), (1,)))


def _dot_tn(a, b):
    return _dot(a, b, ((0,), (0,)))


def _nn(av, bv):
    return _dot(av.astype(BF16), bv.astype(BF16))


def _nt(av, bv):
    return _dot_nt(av.astype(BF16), bv.astype(BF16))


def _nn_split(av, bv):
    return _dot(av.astype(BF16), bv.reshape(-1, bv.shape[2]))


def _nn_split_sq(av, bv):
    return _nn_split(av * av, bv)


def _nt_split(av, bv):
    kc = bv.shape[2]
    acc = _dot_nt(av[:, 0:kc].astype(BF16), bv[0])
    for s in range(1, bv.shape[0]):
        acc = acc + _dot_nt(av[:, s * kc:(s + 1) * kc].astype(BF16), bv[s])
    return acc


def _nt_sc_in(av, bv):
    q = 256
    acc = None
    for i in range(3 * D // q):
        a_blk = av[i // 4][:, (i % 4) * q:(i % 4 + 1) * q]
        b_blk = bv[i // 3][:, (i % 3) * q:(i % 3 + 1) * q]
        t = _dot_nt(a_blk, b_blk)
        acc = t if acc is None else acc + t
    return acc


def _matmul(a, b, *, name, n, contract=_nn, a_spec=None, b_spec=None, tm=512, tn=512, extras=(), epi=None,
            out_dtypes=(F32,), a_silu=False):
    M = a.shape[-2]
    tm, tn = min(tm, M), min(tn, n)
    assert M % tm == 0 and n % tn == 0, (name, M, n, tm, tn)
    n_ex = len(extras)
    if a_spec is None:
        a_spec = pl.BlockSpec((tm, a.shape[1]), lambda i, j: (i, 0))
    if b_spec is None:
        b_spec = (pl.BlockSpec((tn, b.shape[1]), lambda i, j: (j, 0)) if contract is _nt
                  else pl.BlockSpec((b.shape[0], tn), lambda i, j: (0, j)))

    def body(*refs):
        av = refs[0][...]
        if a_silu:
            av = av * _sigmoid(av)
        acc = contract(av, refs[1][...])
        res = epi(acc, *[r[...] for r in refs[2:2 + n_ex]]) if epi is not None else (acc,)
        for o_ref, r in zip(refs[2 + n_ex:], res, strict=True):
            o_ref[...] = r.astype(o_ref.dtype)

    in_specs = [a_spec, b_spec]
    for e in extras:
        in_specs.append(pl.BlockSpec((1, tn), lambda i, j: (0, j)) if e.shape[0] == 1 and M != 1
                        else pl.BlockSpec((tm, tn), lambda i, j: (i, j)))
    outs = pl.pallas_call(
        body, grid=(M // tm, n // tn), in_specs=in_specs,
        out_specs=[pl.BlockSpec((tm, tn), lambda i, j: (i, j)) for _ in out_dtypes],
        out_shape=[jax.ShapeDtypeStruct((M, n), dt) for dt in out_dtypes],
        compiler_params=_params(("parallel", "parallel")), name=name)(a, b, *extras)
    return outs if len(out_dtypes) > 1 else outs[0]


def _matmul_tn(a, b, *, name, m, n, tm=512, tn=512, a_spec=None, b_spec=None, out_spec=None, out_struct=None, into=None,
               a_silu=False, a_square=False):
    T = a.shape[-2]
    tm, tn = min(tm, m), min(tn, n)
    assert m % tm == 0 and n % tn == 0, (name, m, n, tm, tn)
    if a_spec is None:
        a_spec = pl.BlockSpec((T, tm), lambda i, j: (0, i))
    if b_spec is None:
        b_spec = pl.BlockSpec((T, tn), lambda i, j: (0, j))
    if out_spec is None:
        out_spec, out_struct = pl.BlockSpec((tm, tn), lambda i, j: (i, j)), jax.ShapeDtypeStruct((m, n), F32)

    def body(a_ref, b_ref, *rest):
        av = a_ref[...]
        if a_silu:
            av = av * _sigmoid(av)
        if a_square:
            av = av * av
        rest[-1][...] = _dot_tn(av.astype(BF16), b_ref[...].astype(BF16)).astype(rest[-1].dtype)

    args, in_specs, alias = [a, b], [a_spec, b_spec], {}
    if into is not None:
        args, in_specs, alias = args + [into], in_specs + [ANY], {2: 0}
    return pl.pallas_call(body, grid=(m // tm, n // tn), in_specs=in_specs, out_specs=out_spec, out_shape=out_struct,
                          input_output_aliases=alias, compiler_params=_params(("parallel", "parallel")), name=name)(*args)


def _modnorm_fwd(x, nw, sc, sh, *, name):
    L = x.shape[0]
    tm = min(L, 512)

    def body(x_ref, nw_ref, sc_ref, sh_ref, h_ref):
        xv = x_ref[...]
        r = lax.rsqrt(jnp.mean(xv * xv, axis=-1, keepdims=True) + EPS)
        h_ref[...] = ((xv * r * nw_ref[...]) * (1.0 + sc_ref[...]) + sh_ref[...]).astype(BF16)

    row = pl.BlockSpec((tm, D), lambda i: (i, 0))
    vec = pl.BlockSpec((1, D), lambda i: (0, 0))
    return pl.pallas_call(body, grid=(L // tm,), in_specs=[row, vec, vec, vec], out_specs=row,
                          out_shape=jax.ShapeDtypeStruct((L, D), BF16),
                          compiler_params=_params(("parallel",)), name=name)(x, nw, sc, sh)


def _gate_outputs(dx, below_refs, dy_ref, gs_ref):
    g_ref, y_ref = below_refs
    dy_ref[...] = (dx * g_ref[...]).astype(BF16)
    gs_ref[0:1, :] += jnp.sum(dx * y_ref[...].astype(F32), axis=0, keepdims=True)


def _modnorm_bwd(x, dh, dxo, nw, sc, gsum, below, *, name):
    L = x.shape[0]
    tm = min(L, 256)
    nb = 0 if below is None else 2

    def body(x_ref, dh_ref, dxo_ref, nw_ref, sc_ref, g_ref, *rest):
        dx_ref, s_ref = rest[nb:nb + 2]

        @pl.when(pl.program_id(0) == 0)
        def _():
            s_ref[...] = g_ref[...]
            if nb:
                rest[-1][...] = jnp.zeros_like(rest[-1])

        xv, dhv = x_ref[...], dh_ref[...].astype(F32)
        r = lax.rsqrt(jnp.mean(xv * xv, axis=-1, keepdims=True) + EPS)
        xhat = xv * r
        dxhat = dhv * (nw_ref[...] * (1.0 + sc_ref[...]))
        dx = dxo_ref[...] + r * (dxhat - xhat * jnp.mean(dxhat * xhat, axis=-1, keepdims=True))
        dx_ref[...] = dx
        s_ref[1:2, :] += jnp.sum(dhv * xhat, axis=0, keepdims=True) * (1.0 + sc_ref[...])
        s_ref[2:3, :] += jnp.sum(dhv * xhat, axis=0, keepdims=True) * nw_ref[...]
        s_ref[3:4, :] += jnp.sum(dhv, axis=0, keepdims=True)
        if nb:
            _gate_outputs(dx, rest[:nb], rest[-2], rest[-1])

    row = pl.BlockSpec((tm, D), lambda i: (i, 0))
    vec = pl.BlockSpec((1, D), lambda i: (0, 0))
    blk = pl.BlockSpec((8, D), lambda i: (0, 0))
    in_specs, out_specs = [row, row, row, vec, vec, blk], [row, blk]
    out_shape = [jax.ShapeDtypeStruct((L, D), F32), jax.ShapeDtypeStruct((8, D), F32)]
    if nb:
        in_specs, out_specs = in_specs + [vec, row], out_specs + [row, blk]
        out_shape += [jax.ShapeDtypeStruct((L, D), BF16), jax.ShapeDtypeStruct((8, D), F32)]
    return pl.pallas_call(body, grid=(L // tm,), in_specs=in_specs, out_specs=out_specs, out_shape=out_shape,
                          compiler_params=_params(("arbitrary",)), name=name)(x, dh, dxo, nw, sc, gsum, *(below or ()))


def _final_loss(x, fw, tgt, below, *, name):
    L = x.shape[0]
    tm = min(L, 256)

    def body(x_ref, fw_ref, t_ref, g_ref, y_ref, dx_ref, s_ref, dy_ref, gs_ref):
        @pl.when(pl.program_id(0) == 0)
        def _():
            s_ref[...] = jnp.zeros_like(s_ref)
            gs_ref[...] = jnp.zeros_like(gs_ref)

        xv = x_ref[...]
        r = lax.rsqrt(jnp.mean(xv * xv, axis=-1, keepdims=True) + EPS)
        xhat = xv * r
        diff = xhat * fw_ref[...] - t_ref[...]
        dout = diff * (1.0 / D)
        dxhat = dout * fw_ref[...]
        dx = r * (dxhat - xhat * jnp.mean(dxhat * xhat, axis=-1, keepdims=True))
        dx_ref[...] = dx
        s_ref[0:1, :] += jnp.sum(dout * xhat, axis=0, keepdims=True)
        s_ref[1:2, :] += jnp.zeros((1, D), F32) + 0.5 * jnp.sum(jnp.sum(diff * diff, axis=-1, keepdims=True) * (1.0 / D))
        _gate_outputs(dx, (g_ref, y_ref), dy_ref, gs_ref)

    row = pl.BlockSpec((tm, D), lambda i: (i, 0))
    vec = pl.BlockSpec((1, D), lambda i: (0, 0))
    blk = pl.BlockSpec((8, D), lambda i: (0, 0))
    return pl.pallas_call(body, grid=(L // tm,), in_specs=[row, vec, row, vec, row], out_specs=[row, blk, row, blk],
                          out_shape=[jax.ShapeDtypeStruct((L, D), F32), jax.ShapeDtypeStruct((8, D), F32),
                                     jax.ShapeDtypeStruct((L, D), BF16), jax.ShapeDtypeStruct((8, D), F32)],
                          compiler_params=_params(("arbitrary",)), name=name)(x, fw, tgt, *below)


def _shift_down(v, j):
    if j == 0:
        return v
    rolled = pltpu.roll(v, j, 0)
    row = lax.broadcasted_iota(jnp.int32, (8, v.shape[1]), 0)
    return jnp.concatenate([jnp.where(row >= j, rolled[0:8], 0.0), rolled[8:]], axis=0)


def _shift_up(v, j):
    if j == 0:
        return v
    n = v.shape[0]
    rolled = pltpu.roll(v, n - j, 0)
    row = lax.broadcasted_iota(jnp.int32, (8, v.shape[1]), 0)
    return jnp.concatenate([rolled[:n - 8], jnp.where(row < 8 - j, rolled[n - 8:], 0.0)], axis=0)


def _ssd_conv_fwd(zx, w, b, *, name):
    L = zx.shape[0]
    cb = 256
    k = w.shape[0]

    def body(x_ref, w_ref, b_ref, o_ref, p_ref):
        xv = x_ref[...].astype(F32)
        pre = b_ref[...] + xv * w_ref[k - 1:k, :]
        for j in range(1, k):
            pre = pre + _shift_down(xv, j) * w_ref[k - 1 - j:k - j, :]
        o_ref[...] = (pre * _sigmoid(pre)).astype(BF16)
        p_ref[...] = pre.astype(BF16)

    blk = pl.BlockSpec((L, cb), lambda i: (0, i))
    return pl.pallas_call(
        body, grid=(CONVD // cb,),
        in_specs=[pl.BlockSpec((L, cb), lambda i: (0, i + DI // cb)), pl.BlockSpec((k, cb), lambda i: (0, i)),
                  pl.BlockSpec((1, cb), lambda i: (0, i))],
        out_specs=[blk, blk], out_shape=[jax.ShapeDtypeStruct((L, CONVD), BF16)] * 2,
        compiler_params=_params(("parallel",)), name=name)(zx, w, b)


def _ssd_conv_bwd(zx, pre, dact, w, dzx, *, name):
    L = zx.shape[0]
    cb = 256
    k = w.shape[0]

    def body(x_ref, p_ref, da_ref, w_ref, _, dx_ref, s_ref):
        xv, pv = x_ref[...].astype(F32), p_ref[...].astype(F32)
        s = _sigmoid(pv)
        dpre = da_ref[...].astype(F32) * (s * (1.0 + pv * (1.0 - s)))
        s_ref[...] = jnp.zeros_like(s_ref)
        s_ref[k:k + 1, :] = jnp.sum(dpre, axis=0, keepdims=True)
        s_ref[k - 1:k, :] = jnp.sum(dpre * xv, axis=0, keepdims=True)
        dx = dpre * w_ref[k - 1:k, :]
        for j in range(1, k):
            later = _shift_up(dpre, j)
            dx = dx + later * w_ref[k - 1 - j:k - j, :]
            s_ref[k - 1 - j:k - j, :] = jnp.sum(later * xv, axis=0, keepdims=True)
        dx_ref[...] = dx.astype(BF16)

    blk = pl.BlockSpec((L, cb), lambda i: (0, i))
    return pl.pallas_call(
        body, grid=(CONVD // cb,),
        in_specs=[pl.BlockSpec((L, cb), lambda i: (0, i + DI // cb)), blk, blk, pl.BlockSpec((k, cb), lambda i: (0, i)), ANY],
        out_specs=[pl.BlockSpec((L, cb), lambda i: (0, i + DI // cb)), pl.BlockSpec((8, cb), lambda i: (0, i))],
        out_shape=[jax.ShapeDtypeStruct((L, ZX), BF16), jax.ShapeDtypeStruct((8, CONVD), F32)],
        input_output_aliases={4: 0}, compiler_params=_params(("parallel",)), name=name)(zx, pre, dact, w, dzx)


def _sc_fwd(proj, w, *, name):
    L = proj.shape[0]
    cb = 256
    nb = D // cb
    k = w.shape[0]

    def body(b_ref, c_ref, x_ref, w_ref, o_ref, v_ref):
        u = c_ref[...].astype(F32) * x_ref[...].astype(F32)
        v = u * w_ref[k - 1:k, :]
        for j in range(1, k):
            v = v + _shift_down(u, j) * w_ref[k - 1 - j:k - j, :]
        o_ref[...] = (b_ref[...].astype(F32) * v).astype(BF16)
        v_ref[...] = v.astype(BF16)

    blk = pl.BlockSpec((L, cb), lambda i: (0, i))
    return pl.pallas_call(
        body, grid=(nb,),
        in_specs=[blk, pl.BlockSpec((L, cb), lambda i: (0, i + nb)), pl.BlockSpec((L, cb), lambda i: (0, i + 2 * nb)),
                  pl.BlockSpec((k, cb), lambda i: (0, i))],
        out_specs=[blk, blk], out_shape=[jax.ShapeDtypeStruct((L, D), BF16)] * 2,
        compiler_params=_params(("parallel",)), name=name)(proj, proj, proj, w)


def _sc_bwd(proj, v, dyv, w, *, name):
    L = proj.shape[0]
    cb = 256
    nb = D // cb
    k = w.shape[0]

    def body(b_ref, c_ref, x_ref, v_ref, dy_ref, w_ref, dp_ref, s_ref):
        cv, xv = c_ref[...].astype(F32), x_ref[...].astype(F32)
        u = cv * xv
        dyv_ = dy_ref[...].astype(F32)
        dp_ref[0] = (dyv_ * v_ref[...].astype(F32)).astype(BF16)
        dv = dyv_ * b_ref[...].astype(F32)
        s_ref[...] = jnp.zeros_like(s_ref)
        s_ref[k - 1:k, :] = jnp.sum(dv * u, axis=0, keepdims=True)
        du = dv * w_ref[k - 1:k, :]
        for j in range(1, k):
            later = _shift_up(dv, j)
            du = du + later * w_ref[k - 1 - j:k - j, :]
            s_ref[k - 1 - j:k - j, :] = jnp.sum(later * u, axis=0, keepdims=True)
        dp_ref[1] = (du * xv).astype(BF16)
        dp_ref[2] = (du * cv).astype(BF16)

    blk = pl.BlockSpec((L, cb), lambda i: (0, i))
    return pl.pallas_call(
        body, grid=(nb,),
        in_specs=[blk, pl.BlockSpec((L, cb), lambda i: (0, i + nb)), pl.BlockSpec((L, cb), lambda i: (0, i + 2 * nb)),
                  blk, blk, pl.BlockSpec((k, cb), lambda i: (0, i))],
        out_specs=[pl.BlockSpec((3, L, cb), lambda i: (0, 0, i)), pl.BlockSpec((8, cb), lambda i: (0, i))],
        out_shape=[jax.ShapeDtypeStruct((3, L, D), BF16), jax.ShapeDtypeStruct((8, D), F32)],
        compiler_params=_params(("parallel",)), name=name)(proj, proj, proj, v, dyv, w)


def _pieces(v, n):
    out, rest = [], v
    for _ in range(n):
        out.append(rest.astype(BF16))
        rest = rest - out[-1].astype(F32)
    return out


def _cumsum_rows(mask, v):
    m = mask.astype(BF16)
    return _dot(jnp.concatenate([m, m, m], axis=1), jnp.concatenate(_pieces(v, 3), axis=0))


def _ssd_chunk_terms(dtr, prm):
    lane = lax.broadcasted_iota(jnp.int32, (CH, LANES), 1)
    valid = lane < NH
    xdt = dtr + prm[0:1, :]
    dt = jnp.where(valid, jnp.maximum(xdt, 0.0) + jnp.log1p(jnp.exp(-jnp.abs(xdt))), 0.0)
    A = -jnp.exp(prm[1:2, :])
    ri = lax.broadcasted_iota(jnp.int32, (CH, CH), 0)
    ci = lax.broadcasted_iota(jnp.int32, (CH, CH), 1)
    cs = _cumsum_rows(ri >= ci, dt * A)
    last = cs[CH - 1:CH, :]
    spread = (lax.broadcasted_iota(jnp.int32, (2 * LANES, DI), 1) // HP
              == lax.broadcasted_iota(jnp.int32, (2 * LANES, DI), 0) % LANES).astype(BF16)
    gather = ((lax.broadcasted_iota(jnp.int32, (LANES, 2 * DI), 1) % DI) // HP
              == lax.broadcasted_iota(jnp.int32, (LANES, 2 * DI), 0)).astype(BF16)
    return dict(valid=valid, xdt=xdt, dt=dt, A=A, cs=cs, csT=cs.T, last=last, ri=ri, ci=ci, ex=(spread, gather))


def _expand(v, ex):
    if v.shape[0] == 1:
        return _expand(jnp.broadcast_to(v, (8, LANES)), ex)[0:1, :]
    return _dot(jnp.concatenate(_pieces(v, 2), axis=1), ex[0])


def _head_sum(v, ex):
    if v.shape[0] == 1:
        return _head_sum(jnp.broadcast_to(v, (8, DI)), ex)[0:1, :]
    return _dot_nt(jnp.concatenate(_pieces(v, 2), axis=1), ex[1])


def _ssd_fwd(xbc, dtr, prm, *, name):
    L = xbc.shape[0]
    nc = L // CH

    def body(xbc_ref, dtr_ref, prm_ref, y_ref, sp_ref, st_ref):
        @pl.when(pl.program_id(0) == 0)
        def _():
            st_ref[...] = jnp.zeros_like(st_ref)

        prm_v = prm_ref[...]
        t = _ssd_chunk_terms(dtr_ref[...], prm_v)
        cs, csT, ex, causal = t["cs"], t["csT"], t["ex"], t["ri"] >= t["ci"]
        xs = xbc_ref[:, 0:DI].astype(F32)
        X = xs * _expand(t["dt"], ex)
        Xb = X.astype(BF16)
        Xd = (X * _expand(jnp.exp(t["last"] - cs), ex)).astype(BF16)
        Ex = _expand(jnp.exp(cs), ex)
        cdx = _expand(jnp.exp(t["last"]), ex)
        dskx = _expand(prm_v[2:3, :], ex)
        lane = lax.broadcasted_iota(jnp.int32, (CH, LANES), 1)
        sp_ref[0] = st_ref[...]
        for g in range(NG):
            Bg = xbc_ref[:, DI + g * NS:DI + (g + 1) * NS].astype(BF16)
            Cg = xbc_ref[:, DI + GW + g * NS:DI + GW + (g + 1) * NS].astype(BF16)
            G = _dot_nt(Cg, Bg)
            Sg = st_ref[:, g * GW:(g + 1) * GW]
            yoff = _dot(Cg, Sg.astype(BF16)) * Ex[:, g * GW:(g + 1) * GW]
            for j in range(GW // LANES):
                lo = g * GW + j * LANES
                Xp = Xb[:, lo:lo + LANES]
                yd = []
                for h in (lo // HP, lo // HP + 1):
                    seg = cs[:, h:h + 1] - csT[h:h + 1, :]
                    yd.append(_dot((G * jnp.where(causal, jnp.exp(seg), 0.0)).astype(BF16), Xp))
                y_ref[:, lo:lo + LANES] = (jnp.where(lane < HP, yd[0], yd[1]) + yoff[:, j * LANES:(j + 1) * LANES]
                                           + dskx[:, lo:lo + LANES] * xs[:, lo:lo + LANES]).astype(BF16)
            st_ref[:, g * GW:(g + 1) * GW] = Sg * cdx[:, g * GW:(g + 1) * GW] + _dot_tn(Bg, Xd[:, g * GW:(g + 1) * GW])

    return pl.pallas_call(
        body, grid=(nc,),
        in_specs=[pl.BlockSpec((CH, CONVD), lambda c: (c, 0)), pl.BlockSpec((CH, LANES), lambda c: (c, 0)),
                  pl.BlockSpec((8, LANES), lambda c: (0, 0))],
        out_specs=[pl.BlockSpec((CH, DI), lambda c: (c, 0)), pl.BlockSpec((1, NS, DI), lambda c: (c, 0, 0))],
        out_shape=[jax.ShapeDtypeStruct((L, DI), BF16), jax.ShapeDtypeStruct((nc, NS, DI), F32)],
        scratch_shapes=[pltpu.VMEM((NS, DI), F32)],
        compiler_params=_params(("arbitrary",)), name=name)(xbc, dtr, prm)


def _ssd_bwd(xbc, dtr, prm, dy, sprev, *, name):
    L = xbc.shape[0]
    nc = L // CH

    def body(xbc_ref, dtr_ref, prm_ref, dy_ref, sp_ref, dxbc_ref, ddtr_ref, s_ref, dst_ref, dx_scr, de_scr, dd_scr):
        step = pl.program_id(0)

        @pl.when(step == 0)
        def _():
            dst_ref[...] = jnp.zeros_like(dst_ref)
            s_ref[...] = jnp.zeros_like(s_ref)

        prm_v = prm_ref[...]
        t = _ssd_chunk_terms(dtr_ref[...], prm_v)
        cs, csT, ex, ri, ci = t["cs"], t["csT"], t["ex"], t["ri"], t["ci"]
        E = jnp.exp(cs)
        dec = jnp.exp(t["last"] - cs)
        cd = jnp.exp(t["last"])
        xs = xbc_ref[:, 0:DI].astype(F32)
        dtx = _expand(t["dt"], ex)
        X = xs * dtx
        Xb = X.astype(BF16)
        decx = _expand(dec, ex)
        Xd = (X * decx).astype(BF16)
        Ex = _expand(E, ex)
        cdx = _expand(cd, ex)
        dskx = _expand(prm_v[2:3, :], ex)
        lane = lax.broadcasted_iota(jnp.int32, (CH, LANES), 1)
        dcs = jnp.zeros((CH, LANES), F32)
        dcd_x = []
        for g in range(NG):
            gs = slice(g * GW, (g + 1) * GW)
            Bg = xbc_ref[:, DI + g * NS:DI + (g + 1) * NS].astype(BF16)
            Cg = xbc_ref[:, DI + GW + g * NS:DI + GW + (g + 1) * NS].astype(BF16)
            G = _dot_nt(Cg, Bg)
            GT = _dot_nt(Bg, Cg)
            Sg = sp_ref[0, :, gs]
            Sgb = Sg.astype(BF16)
            dyg = dy_ref[:, gs]
            de_scr[:, gs] = dyg * _dot(Cg, Sgb)
            dYo = (Ex[:, gs] * dyg).astype(BF16)
            dC = _dot_nt(dYo, Sgb)
            dS_in = _dot_tn(Cg, dYo)
            dStg = dst_ref[:, gs]
            dStb = dStg.astype(BF16)
            dXd = _dot(Bg, dStb)
            dB = _dot_nt(Xd[:, gs], dStb)
            dd_scr[:, gs] = dXd * X[:, gs]
            dXst = dXd * decx[:, gs]
            dG = jnp.zeros((CH, CH), F32)
            dGT = jnp.zeros((CH, CH), F32)
            for j in range(GW // LANES):
                lo = g * GW + j * LANES
                Xp = Xb[:, lo:lo + LANES]
                dyp = dy_ref[:, lo:lo + LANES]
                dXp = dXst[:, j * LANES:(j + 1) * LANES]
                for k, h in enumerate((lo // HP, lo // HP + 1)):
                    dyh = jnp.where((lane < HP) if k == 0 else (lane >= HP), dyp, 0.0).astype(BF16)
                    seg = cs[:, h:h + 1] - csT[h:h + 1, :]
                    Lm = jnp.where(ri >= ci, jnp.exp(seg), 0.0)
                    LmT = jnp.where(ci >= ri, jnp.exp(-seg), 0.0)
                    dM = _dot_nt(dyh, Xp)
                    dMT = _dot_nt(Xp, dyh)
                    MT = GT * LmT
                    rs = jnp.sum(dM * (G * Lm), axis=1, keepdims=True) - jnp.sum(dMT * MT, axis=1, keepdims=True)
                    dcs = dcs + jnp.where(lane == h, rs, 0.0)
                    dG = dG + dM * Lm
                    dGT = dGT + dMT * LmT
                    dXp = dXp + _dot(MT.astype(BF16), dyh)
                dx_scr[:, lo:lo + LANES] = dXp
            dxbc_ref[:, DI + g * NS:DI + (g + 1) * NS] = (dB + _dot(dGT.astype(BF16), Cg)).astype(BF16)
            dxbc_ref[:, DI + GW + g * NS:DI + GW + (g + 1) * NS] = (dC + _dot(dG.astype(BF16), Bg)).astype(BF16)
            dcd_x.append(jnp.sum(dStg * Sg, axis=0, keepdims=True))
            dst_ref[:, gs] = dStg * cdx[:, gs] + dS_in
        dX = dx_scr[...]
        dy = dy_ref[...]
        ddec = _head_sum(dd_scr[...], ex)
        dcd = _head_sum(jnp.concatenate(dcd_x, axis=1), ex)
        dcs = dcs + _head_sum(de_scr[...], ex) * E - ddec * dec
        row = lax.broadcasted_iota(jnp.int32, (CH, LANES), 0)
        dcs = dcs + jnp.where(row == CH - 1, jnp.sum(ddec * dec, axis=0, keepdims=True) + dcd * cd, 0.0)
        da = _cumsum_rows(ci >= ri, dcs)
        ddt = da * t["A"] + _head_sum(dX * xs, ex)
        ddtr = jnp.where(t["valid"], ddt * _sigmoid(t["xdt"]), 0.0)
        ddtr_ref[...] = ddtr
        dxbc_ref[:, 0:DI] = (dX * dtx + dskx * dy).astype(BF16)
        s_ref[0:1, :] += jnp.sum(da * t["dt"], axis=0, keepdims=True)
        s_ref[1:2, :] += _head_sum(jnp.sum(dy * xs, axis=0, keepdims=True), ex)
        s_ref[2:3, :] += jnp.sum(ddtr, axis=0, keepdims=True)

        @pl.when(step == nc - 1)
        def _():
            s_ref[0:1, :] = s_ref[0:1, :] * t["A"]

    rev = lambda c: (nc - 1 - c, 0)
    return pl.pallas_call(
        body, grid=(nc,),
        in_specs=[pl.BlockSpec((CH, CONVD), rev), pl.BlockSpec((CH, LANES), rev), pl.BlockSpec((8, LANES), lambda c: (0, 0)),
                  pl.BlockSpec((CH, DI), rev), pl.BlockSpec((1, NS, DI), lambda c: (nc - 1 - c, 0, 0))],
        out_specs=[pl.BlockSpec((CH, CONVD), rev), pl.BlockSpec((CH, LANES), rev), pl.BlockSpec((8, LANES), lambda c: (0, 0))],
        out_shape=[jax.ShapeDtypeStruct((L, CONVD), BF16), jax.ShapeDtypeStruct((L, LANES), F32),
                   jax.ShapeDtypeStruct((8, LANES), F32)],
        scratch_shapes=[pltpu.VMEM((NS, DI), F32), pltpu.VMEM((CH, DI), F32), pltpu.VMEM((CH, DI), F32),
                        pltpu.VMEM((CH, DI), F32)],
        compiler_params=_params(("arbitrary",)), name=name)(xbc, dtr, prm, dy, sprev)


def _gnorm_fwd(y, zx, nw, *, name):
    L = y.shape[0]
    tm = min(L, 256)

    def body(y_ref, z_ref, nw_ref, o_ref):
        z = z_ref[...].astype(F32)
        yg = y_ref[...].astype(F32) * (z * _sigmoid(z))
        for g in range(NG):
            v = yg[:, g * GW:(g + 1) * GW]
            r = lax.rsqrt(jnp.mean(v * v, axis=-1, keepdims=True) + EPS)
            o_ref[:, g * GW:(g + 1) * GW] = (v * r * nw_ref[:, g * GW:(g + 1) * GW]).astype(BF16)

    row = pl.BlockSpec((tm, DI), lambda i: (i, 0))
    return pl.pallas_call(body, grid=(L // tm,), in_specs=[row, row, pl.BlockSpec((1, DI), lambda i: (0, 0))],
                          out_specs=row, out_shape=jax.ShapeDtypeStruct((L, DI), BF16),
                          compiler_params=_params(("parallel",)), name=name)(y, zx, nw)


def _gnorm_bwd(y, zx, nw, dyn, *, name):
    L = y.shape[0]
    tm = min(L, 256)

    def body(y_ref, z_ref, nw_ref, dyn_ref, dy_ref, dz_ref, s_ref):
        @pl.when(pl.program_id(0) == 0)
        def _():
            s_ref[...] = jnp.zeros_like(s_ref)

        z, yv = z_ref[...].astype(F32), y_ref[...].astype(F32)
        sz = _sigmoid(z)
        gate = z * sz
        dgate_dz = sz * (1.0 + z * (1.0 - sz))
        for g in range(NG):
            gs = slice(g * GW, (g + 1) * GW)
            v = yv[:, gs] * gate[:, gs]
            r = lax.rsqrt(jnp.mean(v * v, axis=-1, keepdims=True) + EPS)
            vhat = v * r
            dn = dyn_ref[:, gs].astype(F32)
            s_ref[0:1, gs] += jnp.sum(dn * vhat, axis=0, keepdims=True)
            dvhat = dn * nw_ref[:, gs]
            dv = r * (dvhat - vhat * jnp.mean(dvhat * vhat, axis=-1, keepdims=True))
            dy_ref[:, gs] = dv * gate[:, gs]
            dz_ref[:, gs] = (dv * yv[:, gs] * dgate_dz[:, gs]).astype(BF16)

    row = pl.BlockSpec((tm, DI), lambda i: (i, 0))
    return pl.pallas_call(body, grid=(L // tm,), in_specs=[row, row, pl.BlockSpec((1, DI), lambda i: (0, 0)), row],
                          out_specs=[row, row, pl.BlockSpec((8, DI), lambda i: (0, 0))],
                          out_shape=[jax.ShapeDtypeStruct((L, DI), F32), jax.ShapeDtypeStruct((L, ZX), BF16),
                                     jax.ShapeDtypeStruct((8, DI), F32)],
                          compiler_params=_params(("arbitrary",)), name=name)(y, zx, nw, dyn)


def _adamw(w, g, m, v, *, name, g_row=0, w_row=0, rows=None, into=None, emit_g=False):
    lead = w.ndim == 3
    R, C = w.shape[-2:]
    rows = R if rows is None else rows
    tr = max([t for t in range(8, rows + 1, 8) if rows % t == 0 and t * C <= 256 * 1024], default=rows)
    assert g_row % tr == 0 and w_row % tr == 0, (name, g_row, w_row, tr)
    n_out = 4 if emit_g else 3

    def body(w_ref, g_ref, m_ref, v_ref, *rest):
        outs = rest[-n_out:]
        gv = g_ref[...]
        mn = ADAM_B1 * m_ref[...] + (1.0 - ADAM_B1) * gv
        vn = ADAM_B2 * v_ref[...] + (1.0 - ADAM_B2) * (gv * gv)
        m_hat = mn / (1.0 - ADAM_B1 ** ADAM_STEP)
        v_hat = vn / (1.0 - ADAM_B2 ** ADAM_STEP)
        d_ref, mo_ref, vo_ref = outs[-3:]
        d_ref[...] = -ADAM_LR * (m_hat / (jnp.sqrt(v_hat) + ADAM_EPS) + ADAM_WD * w_ref[...])
        mo_ref[...] = mn
        vo_ref[...] = vn
        if emit_g:
            outs[0][...] = gv

    blk = (pl.BlockSpec((None, tr, C), lambda i: (0, i + w_row // tr, 0)) if lead
           else pl.BlockSpec((tr, C), lambda i: (i + w_row // tr, 0)))
    args, in_specs, alias = [w, g, m, v], [blk, pl.BlockSpec((tr, C), lambda i: (i + g_row // tr, 0)), blk, blk], {}
    if into is not None:
        args, in_specs, alias = args + list(into), in_specs + [ANY] * n_out, {4 + k: k for k in range(n_out)}
    return pl.pallas_call(body, grid=(rows // tr,), in_specs=in_specs, out_specs=[blk] * n_out,
                          out_shape=[jax.ShapeDtypeStruct(w.shape, F32)] * n_out, input_output_aliases=alias,
                          compiler_params=_params(("parallel",)), name=name)(*args)


def _residual(acc, xv, gv):
    return xv + gv * acc, acc


def _like(buf):
    return jax.ShapeDtypeStruct(buf.shape, buf.dtype)


def _mlp_fwd(x, mod, nw, wb, up_row, down_row, tag, midway=None):
    sh, sc, g = mod
    h = _modnorm_fwd(x, nw, sc, sh, name=tag + "_norm")
    a = _matmul(h, wb, n=DFF, tm=TM_ALL, b_spec=pl.BlockSpec((None, D, 512), lambda mi, j: (j // 2, up_row // D, j % 2)),
                epi=lambda acc: (jnp.maximum(acc, 0.0),), out_dtypes=(BF16,), name=tag + "_up")
    if midway is not None:
        midway(a)
    xn, y = _matmul(a, wb, n=D, tm=TM_HALF, contract=_nn_split_sq,
                    b_spec=pl.BlockSpec((N_CHIPS, D, 512), lambda mi, j: (0, down_row // D, j)),
                    extras=(x, g), epi=_residual, out_dtypes=(F32, BF16), name=tag + "_down")
    return xn, (x, h, a, y)


def _mlp_bwd(dxo, dy, gsum, saved, mod, nw, wb, gb, up_row, down_row, below, tag):
    x, h, a, y = saved
    sh, sc, g = mod
    du = _matmul(dy, wb, n=DFF, tm=TM_ALL, contract=_nt,
                 b_spec=pl.BlockSpec((None, 512, D), lambda mi, j: (j // 2, down_row // 512 + j % 2, 0)),
                 extras=(a,), epi=lambda acc, av: (acc * (2.0 * av.astype(F32)),), out_dtypes=(BF16,), name=tag + "_dact")
    gb = _matmul_tn(a, dy, m=DFF, n=D, tm=D, tn=D, a_square=True, into=gb, out_struct=_like(wb),
                    out_spec=pl.BlockSpec((None, D, D), lambda mi, j: (mi, down_row // D, 0)), name=tag + "_ddown")
    dh = _matmul(du, wb, n=D, tm=TM_HALF, contract=_nt_split,
                 b_spec=pl.BlockSpec((N_CHIPS, 512, D), lambda mi, j: (0, up_row // 512 + j, 0)), out_dtypes=(BF16,),
                 name=tag + "_dh")
    gb = _matmul_tn(h, du, m=D, n=DFF, tm=D, into=gb, out_struct=_like(wb),
                    out_spec=pl.BlockSpec((None, D, 512), lambda mi, j: (j // 2, up_row // D, j % 2)), name=tag + "_dup")
    dx, sums, *nxt = _modnorm_bwd(x, dh, dxo, nw, sc, gsum, below, name=tag + "_dnorm")
    return dx, gb, sums, *nxt


def _ssd_fwd_scan(x, mod, nw, w_in_t, w_dt_t, conv_w, conv_b, prm, tag):
    sh, sc, g = mod
    h = _modnorm_fwd(x, nw, sc, sh, name=tag + "_norm")
    zx = _matmul(h, w_in_t, n=ZX, tm=TM_ALL, contract=_nt, out_dtypes=(BF16,), name=tag + "_in")
    dtr = _matmul(h, w_dt_t, n=LANES, tm=TM_ALL, contract=_nt, name=tag + "_in_dt")
    xbc, pre = _ssd_conv_fwd(zx, conv_w, conv_b, name=tag + "_conv")
    y, sprev = _ssd_fwd(xbc, dtr, prm, name=tag + "_scan")
    return h, zx, dtr, xbc, y, sprev, pre


def _ssd_fwd_out(x, mod, scan, gn_w, get_w_out, tag):
    sh, sc, g = mod
    h, zx, dtr, xbc, y, sprev, pre = scan
    yn = _gnorm_fwd(y, zx, gn_w, name=tag + "_gnorm")
    w_out = get_w_out(yn)
    xn, yo = _matmul(yn, w_out, n=D, tm=TM_HALF, contract=_nn_split,
                     b_spec=pl.BlockSpec((N_CHIPS, 512, 512), lambda mi, j: (0, 0, j)),
                     extras=(x, g), epi=_residual, out_dtypes=(F32, BF16), name=tag + "_out")
    return xn, (x, h, zx, dtr, xbc, y, sprev, yn, yo, pre)


def _ssd_bwd_out(dyo, saved, w_out, tag, after):
    x, h, zx, dtr, xbc, y, sprev, yn, yo, pre = saved
    dyn = _matmul(dyo, w_out, n=DI, tm=TM_ALL, contract=_nt, b_spec=pl.BlockSpec((None, 512, D), lambda mi, j: (j, 0, 0)),
                  extras=(jnp.broadcast_to(after[0:1, 0:1], (1, DI)),), epi=lambda acc, t: (acc + t,),
                  out_dtypes=(BF16,), name=tag + "_dyn")
    g_out = _matmul_tn(yn, dyo, m=DI, n=D, tn=D, out_struct=_like(w_out),
                       out_spec=pl.BlockSpec((None, 512, D), lambda mi, j: (mi, 0, 0)), name=tag + "_dout")
    return dyn, g_out


def _ssd_bwd_rest(dxo, dy, dzx, gsum, saved, mod, nw, w_in_t, w_dt_t, conv_w, prm, tag):
    x, h, zx, dtr, xbc, y, sprev, yn, yo, pre = saved
    sh, sc, g = mod
    dxbc, ddtr, ssum = _ssd_bwd(xbc, dtr, prm, dy, sprev, name=tag + "_dscan")
    dzx, csum = _ssd_conv_bwd(zx, pre, dxbc, conv_w, dzx, name=tag + "_dconv")
    dh_dt = _matmul(ddtr, w_dt_t, n=D, tm=TM_ALL, name=tag + "_dh_dt")
    dh = _matmul(dzx, w_in_t, n=D, tm=TM_HALF, b_spec=pl.BlockSpec((ZX, 512), lambda mi, j: (0, j)), extras=(dh_dt,),
                 epi=lambda acc, e: (acc + e,), out_dtypes=(BF16,), name=tag + "_dh")
    d_w_zx = _matmul_tn(h, dzx, m=D, n=ZX, tm=D, out_spec=pl.BlockSpec((D, 512), lambda i, j: (i, j)),
                        out_struct=jax.ShapeDtypeStruct((D, ZX), BF16), name=tag + "_din")
    d_w_dt = _matmul_tn(h, ddtr, m=D, n=LANES, tm=D, name=tag + "_din_dt")
    dx, sums = _modnorm_bwd(x, dh, dxo, nw, sc, gsum, None, name=tag + "_dnorm")
    return dx, d_w_zx, d_w_dt, sums, csum, ssum


def _sc_layer_fwd(x, mod, nw, w_sc_in, conv_w, wb, out_row, tag, midway=None):
    sh, sc, g = mod
    h = _modnorm_fwd(x, nw, sc, sh, name=tag + "_norm")
    proj = _matmul(h, w_sc_in, n=3 * D, tm=TM_ALL, tn=256, out_dtypes=(BF16,),
                   b_spec=pl.BlockSpec((None, D, 256), lambda mi, j: (j // 3, 0, j % 3)),
                   name=tag + "_in")
    if midway is not None:
        midway(proj)
    yv, v = _sc_fwd(proj, conv_w, name=tag + "_conv")
    xn, yo = _matmul(yv, wb, n=D, tm=TM_HALF, contract=_nn_split,
                     b_spec=pl.BlockSpec((N_CHIPS, 256, 512), lambda mi, j: (0, out_row // 256, j)),
                     extras=(x, g), epi=_residual, out_dtypes=(F32, BF16), name=tag + "_out")
    return xn, (x, h, proj, yv, yo, v)


def _sc_layer_bwd(dxo, dyo, gsum, saved, mod, nw, w_sc_in, conv_w, wb, gb, out_row, below, tag):
    x, h, proj, yv, yo, v = saved
    sh, sc, g = mod
    L = x.shape[0]
    dyv = _matmul(dyo, wb, n=D, tm=TM_ALL, tn=256, contract=_nt,
                  b_spec=pl.BlockSpec((None, 256, D), lambda mi, j: (j, out_row // 256, 0)), out_dtypes=(BF16,),
                  name=tag + "_dyv")
    gb = _matmul_tn(yv, dyo, m=D, n=D, tm=256, tn=D, into=gb, out_struct=_like(wb),
                    out_spec=pl.BlockSpec((None, 256, D), lambda mi, j: (mi, out_row // 256, 0)), name=tag + "_dout")
    dproj, csum = _sc_bwd(proj, v, dyv, conv_w, name=tag + "_dconv")
    tm = min(L, TM_HALF)
    dh = _matmul(dproj, w_sc_in, n=D, tm=tm, contract=_nt_sc_in, a_spec=pl.BlockSpec((3, tm, D), lambda mi, j: (0, mi, 0)),
                 b_spec=pl.BlockSpec((N_CHIPS, 512, SC_IN_SHARD), lambda mi, j: (0, j, 0)), out_dtypes=(BF16,),
                 name=tag + "_dh")
    g_sc_in = _matmul_tn(h, dproj, m=D, n=3 * D, tm=D, tn=256, b_spec=pl.BlockSpec((None, L, 256), lambda mi, j: (j // 4, 0, j % 4)),
                         out_spec=pl.BlockSpec((None, D, 256), lambda mi, j: (j // 3, 0, j % 3)),
                         out_struct=jax.ShapeDtypeStruct((N_CHIPS, D, SC_IN_SHARD), BF16), name=tag + "_din")
    dx, sums, *nxt = _modnorm_bwd(x, dh, dxo, nw, sc, gsum, below, name=tag + "_dnorm")
    return dx, gb, g_sc_in, sums, csum, *nxt


SUB_ROW = (0, 8, 16, 24)
SSD_CONV_ROW, GNORM_ROW, FINAL_ROW, SC_CONV_ROW, HEAD_ROW, SMALL_ROWS = 32, 48, 56, 64, 72, 80


def _all_gather_rows(blk, *, name):
    m_per, n = blk.shape

    def body(x_ref, out_ref, send_sems, recv_sems, local_sem):
        x, y, c = lax.axis_index("x"), lax.axis_index("y"), lax.axis_index("c")
        me, sibling = (x, y, c), (x, y, 1 - c)
        chips = [(1 - x, y), (x, 1 - y), (1 - x, 1 - y)]

        def rows(px, py, pc):
            return out_ref.at[pl.ds((4 * px + 2 * py + pc) * m_per, m_per), :]

        def copy(k, block, to, src=None):
            return pltpu.make_async_remote_copy(src_ref=rows(*block) if src is None else src, dst_ref=rows(*block),
                                                send_sem=send_sems.at[k], recv_sem=recv_sems.at[k], device_id=to,
                                                device_id_type=MESH)

        mine = pltpu.make_async_copy(x_ref, rows(*me), local_sem)
        mine.start()
        first = [copy(0, me, sibling, src=x_ref)] + [copy(1 + j, me, (*chip, c), src=x_ref) for j, chip in enumerate(chips)]
        for cp in first:
            cp.start()
        passed = [copy(4 + j, (*chip, c), sibling) for j, chip in enumerate(chips)]
        for j, chip in enumerate(chips):
            copy(1 + j, (*chip, c), me).wait_recv()
            passed[j].start()
        copy(0, sibling, me).wait_recv()
        for j, chip in enumerate(chips):
            copy(4 + j, (*chip, 1 - c), me).wait_recv()
        for cp in first + passed:
            cp.wait_send()
        mine.wait()

    return pl.pallas_call(
        body, out_shape=jax.ShapeDtypeStruct((N_DEV * m_per, n), blk.dtype),
        in_specs=[pl.BlockSpec(memory_space=pltpu.VMEM)], out_specs=pl.BlockSpec(memory_space=pltpu.VMEM),
        scratch_shapes=[pltpu.SemaphoreType.DMA((7,)), pltpu.SemaphoreType.DMA((7,)), pltpu.SemaphoreType.DMA],
        name=name)(blk)


def _half(ref, chip, c):
    r, n = ref.shape[1:]
    if r % 32 == 0:
        return ref.at[chip, pl.ds(c * (r // 2), r // 2), :]
    assert n % 256 == 0, ref.shape
    return ref.at[chip, :, pl.ds(c * (n // 2), n // 2)]


def _gather_copy(bufs, sends, recvs, b, k, chip, pc, to):
    piece = _half(bufs[b], 2 * chip[0] + chip[1], pc)
    return pltpu.make_async_remote_copy(src_ref=piece, dst_ref=piece, send_sem=sends.at[4 * b + k], recv_sem=recvs.at[4 * b + k],
                                        device_id=to, device_id_type=MESH)


def _split_call(body, bufs, sems_in, n_sems, *, name, after=(), token=False, lands=()):
    nb, na, nl, starts = len(bufs), len(after), len(lands), not sems_in

    def wrapped(*refs):
        sems = refs[nb + na:nb + na + 2] if starts else refs[nb:nb + 2]
        made = refs[nb + na + 2 + nb:nb + na + 2 + nb + nl] if starts else ()
        body(tuple(refs[:nb]) + tuple(made), sems[0], sems[1])
        if token:
            refs[-1][...] = jnp.zeros_like(refs[-1])

    out_shape = [pltpu.SemaphoreType.DMA((n_sems,)) for _ in range(2 if starts else 0)]
    out_specs = [SEM] * len(out_shape) + [ANY] * (nb + nl)
    alias = {b: len(out_shape) + b for b in range(nb)}
    out_shape += [jax.ShapeDtypeStruct(b.shape, b.dtype) for b in bufs] + list(lands)
    if token:
        out_shape.append(jax.ShapeDtypeStruct((8, LANES), F32))
        out_specs.append(pl.BlockSpec(memory_space=pltpu.VMEM))
    return pl.pallas_call(
        wrapped, out_shape=out_shape, in_specs=[ANY] * nb + [SEM] * len(sems_in) + [ANY] * na, out_specs=out_specs,
        input_output_aliases=alias,
        compiler_params=pltpu.CompilerParams(has_side_effects=pltpu.SideEffectType.DATAFLOW_SIDE_EFFECTING),
        name=name)(*bufs, *sems_in, *after)


def _gather_start(bufs, *, name, after=()):
    nb = len(bufs)

    def body(ins, sends, recvs):
        x, y, c = lax.axis_index("x"), lax.axis_index("y"), lax.axis_index("c")
        chips = [(1 - x, y), (x, 1 - y), (1 - x, 1 - y)]
        for b in range(nb):
            _gather_copy(ins, sends, recvs, b, 0, (x, y), c, (x, y, 1 - c)).start()
            for j, chip in enumerate(chips):
                _gather_copy(ins, sends, recvs, b, 1 + j, (x, y), c, (*chip, c)).start()

    out = _split_call(body, bufs, (), 4 * nb, name=name, after=after, token=True)
    return (out[0], out[1], out[2:2 + nb]), out[-1]


def _gather_wait_first(flight, *, name, after=()):
    sends, recvs, bufs = flight
    nb = len(bufs)

    def body(ins, sends_, recvs_):
        x, y, c = lax.axis_index("x"), lax.axis_index("y"), lax.axis_index("c")
        chips = [(1 - x, y), (x, 1 - y), (1 - x, 1 - y)]
        for b in range(nb):
            _gather_copy(ins, sends_, recvs_, b, 0, (x, y), c, (x, y, 1 - c)).wait_send()
            _gather_copy(ins, sends_, recvs_, b, 0, (x, y), 1 - c, (x, y, c)).wait_recv()
            for j, chip in enumerate(chips):
                _gather_copy(ins, sends_, recvs_, b, 1 + j, (x, y), c, (*chip, c)).wait_send()
                _gather_copy(ins, sends_, recvs_, b, 1 + j, chip, c, (x, y, c)).wait_recv()

    return _split_call(body, bufs, (sends, recvs), 4 * nb, name=name, after=after)


def _gather_forward(bufs, *, name):
    nb = len(bufs)

    def body(ins, sends, recvs):
        x, y, c = lax.axis_index("x"), lax.axis_index("y"), lax.axis_index("c")
        chips = [(1 - x, y), (x, 1 - y), (1 - x, 1 - y)]
        for b in range(nb):
            for j, chip in enumerate(chips):
                _gather_copy(ins, sends, recvs, b, 1 + j, chip, c, (x, y, 1 - c)).start()

    out = _split_call(body, bufs, (), 4 * nb, name=name)
    return out[0], out[1], out[2:2 + nb]


def _gather_wait_forward(flight, *, name, after=()):
    sends, recvs, bufs = flight
    nb = len(bufs)

    def body(ins, sends_, recvs_):
        x, y, c = lax.axis_index("x"), lax.axis_index("y"), lax.axis_index("c")
        chips = [(1 - x, y), (x, 1 - y), (1 - x, 1 - y)]
        for b in range(nb):
            for j, chip in enumerate(chips):
                _gather_copy(ins, sends_, recvs_, b, 1 + j, chip, c, (x, y, 1 - c)).wait_send()
                _gather_copy(ins, sends_, recvs_, b, 1 + j, chip, 1 - c, (x, y, c)).wait_recv()

    return _split_call(body, bufs, (sends, recvs), 4 * nb, name=name, after=after)


def _owner_copies(hs, lands, sends, recvs):
    x, y, c = lax.axis_index("x"), lax.axis_index("y"), lax.axis_index("c")
    chips = [(1 - x, y), (x, 1 - y), (1 - x, 1 - y)]
    return [pltpu.make_async_remote_copy(src_ref=hs[b].at[2 * cx + cy], dst_ref=lands[b].at[j], send_sem=sends.at[3 * b + j],
                                         recv_sem=recvs.at[3 * b + j], device_id=(cx, cy, c), device_id_type=MESH)
            for b in range(len(hs)) for j, (cx, cy) in enumerate(chips)]


def _owners_start(hs, *, name):
    nb = len(hs)
    lands = [jax.ShapeDtypeStruct((3,) + h.shape[1:], h.dtype) for h in hs]

    def body(refs, sends, recvs):
        for cp in _owner_copies(refs[:nb], refs[nb:], sends, recvs):
            cp.start()

    out = _split_call(body, list(hs), (), 3 * nb, name=name, token=True, lands=lands)
    return (out[0], out[1], out[2:2 + 2 * nb]), out[-1]


def _owners_wait(flight, *, name, after=()):
    sends, recvs, bufs = flight
    nb = len(bufs) // 2

    def body(refs, sends_, recvs_):
        for cp in _owner_copies(refs[:nb], refs[nb:], sends_, recvs_):
            cp.wait()

    out = _split_call(body, bufs, (sends, recvs), 3 * nb, name=name, after=after)
    return out[:nb], out[nb:]


def _sibling_copies(gs, lands, sends, recvs):
    x, y, c = lax.axis_index("x"), lax.axis_index("y"), lax.axis_index("c")
    copies = []
    for b in range(len(gs)):
        hr = gs[b].shape[1] // 2
        copies.append(pltpu.make_async_remote_copy(
            src_ref=gs[b].at[:, pl.ds((1 - c) * hr, hr), :], dst_ref=lands[b], send_sem=sends.at[b], recv_sem=recvs.at[b],
            device_id=(x, y, 1 - c), device_id_type=MESH))
    return copies


def _sibling_start(gs, *, name, after=()):
    nb = len(gs)
    lands = [jax.ShapeDtypeStruct((g.shape[0], g.shape[1] // 2, g.shape[2]), g.dtype) for g in gs]

    def body(refs, sends, recvs):
        for cp in _sibling_copies(refs[:nb], refs[nb:], sends, recvs):
            cp.start()

    out = _split_call(body, list(gs), (), nb, name=name, after=after, token=True, lands=lands)
    return (out[0], out[1], out[2:2 + 2 * nb]), out[-1]


def _sibling_wait(flight, *, name, after=()):
    sends, recvs, bufs = flight
    nb = len(bufs) // 2

    def body(refs, sends_, recvs_):
        for cp in _sibling_copies(refs[:nb], refs[nb:], sends_, recvs_):
            cp.wait()

    out = _split_call(body, bufs, (sends, recvs), nb, name=name, after=after)
    return out[:nb], out[nb:]


def _result_copies(ts, sends, recvs):
    x, y, c = lax.axis_index("x"), lax.axis_index("y"), lax.axis_index("c")
    return [pltpu.make_async_remote_copy(src_ref=ts[b].at[c], dst_ref=ts[b].at[c], send_sem=sends.at[b], recv_sem=recvs.at[b],
                                         device_id=(x, y, 1 - c), device_id_type=MESH) for b in range(len(ts))]


def _result_start(ts, *, name):
    def body(refs, sends, recvs):
        for cp in _result_copies(refs, sends, recvs):
            cp.start()

    out = _split_call(body, ts, (), len(ts), name=name, token=True)
    return (out[0], out[1], out[2:2 + len(ts)]), out[-1]


def _result_wait(flight, *, name, after=()):
    sends, recvs, bufs = flight

    def body(refs, sends_, recvs_):
        for cp in _result_copies(refs, sends_, recvs_):
            cp.wait()

    return _split_call(body, bufs, (sends, recvs), len(bufs), name=name, after=after)


def _row_tile(rows, cols):
    best = 16
    for t in range(16, rows + 1, 16):
        if rows % t == 0 and t * cols <= 640 * 1024:
            best = t
    assert rows % best == 0, (rows, cols)
    return best


def _add_sibling_half(g, recv, core, *, name):
    nk, r, n = g.shape
    hr = r // 2
    tr = _row_tile(hr, n)

    def body(c_ref, a_ref, b_ref, o_ref):
        o_ref[...] = (a_ref[...].astype(F32) + b_ref[...].astype(F32)).astype(BF16)

    grid_spec = pltpu.PrefetchScalarGridSpec(
        num_scalar_prefetch=1, grid=(nk, hr // tr),
        in_specs=[pl.BlockSpec((None, tr, n), lambda k, i, c_ref: (k, c_ref[0] * (hr // tr) + i, 0)),
                  pl.BlockSpec((None, tr, n), lambda k, i, c_ref: (k, i, 0))],
        out_specs=pl.BlockSpec((None, tr, n), lambda k, i, c_ref: (k, i, 0)))
    return pl.pallas_call(body, grid_spec=grid_spec, out_shape=jax.ShapeDtypeStruct((nk, hr, n), BF16),
                          compiler_params=_params(("parallel", "parallel")), name=name)(core, g, recv)


def _add_chip_sums(h, recv, chip_core, *, name):
    _, hr, n = h.shape
    tr = _row_tile(hr, n)

    def body(k_ref, a_ref, b_ref, o_ref):
        o_ref[...] = ((a_ref[...].astype(F32) + b_ref[0].astype(F32)) + b_ref[1].astype(F32)) + b_ref[2].astype(F32)

    grid_spec = pltpu.PrefetchScalarGridSpec(
        num_scalar_prefetch=1, grid=(hr // tr,),
        in_specs=[pl.BlockSpec((None, tr, n), lambda i, k_ref: (k_ref[0], i, 0)),
                  pl.BlockSpec((3, tr, n), lambda i, k_ref: (0, i, 0))],
        out_specs=pl.BlockSpec((None, tr, n), lambda i, k_ref: (k_ref[1], i, 0)))
    return pl.pallas_call(body, grid_spec=grid_spec, out_shape=jax.ShapeDtypeStruct((2, hr, n), F32),
                          compiler_params=_params(("parallel",)), name=name)(chip_core, h, recv)


def _sum_devices(g, *, name, after=()):
    nd, r, n = g.shape

    def body(g_ref, *rest):
        acc = g_ref[0]
        for i in range(1, nd):
            acc = acc + g_ref[i]
        rest[-1][...] = acc

    return pl.pallas_call(body, out_shape=jax.ShapeDtypeStruct((r, n), F32),
                          in_specs=[pl.BlockSpec(memory_space=pltpu.VMEM)] + [ANY] * len(after), name=name)(g, *after)


def _own_slot(parts, chip, *, name, after=()):
    rows, cols = sum(w.shape[1] for w, _ in parts), parts[0][0].shape[2]
    buf, row0 = None, 0
    for p, (w, idx) in enumerate(parts):
        r = w.shape[1]
        tr = 256 if r % 256 == 0 else r
        assert row0 % tr == 0, (name, r, row0)
        prev = () if buf is None else (buf,)

        def body(chip_ref, w_ref, *rest):
            rest[-1][...] = w_ref[...].astype(BF16)

        grid_spec = pltpu.PrefetchScalarGridSpec(
            num_scalar_prefetch=1, grid=(r // tr,),
            in_specs=[pl.BlockSpec((None, tr, cols), lambda i, c_ref, idx=idx: (idx, i, 0))] + [ANY] * (len(prev) + len(after)),
            out_specs=pl.BlockSpec((None, tr, cols), lambda i, c_ref, row0=row0, tr=tr: (c_ref[0], row0 // tr + i, 0)))
        buf = pl.pallas_call(body, grid_spec=grid_spec, out_shape=jax.ShapeDtypeStruct((N_CHIPS, rows, cols), BF16),
                             input_output_aliases={2: 0} if prev else {}, compiler_params=_params(("parallel",)),
                             name=f"{name}{p}")(chip, w, *prev, *after)
        row0 += r
    return buf


def kernel(x, c, ada_w, ada_b, mix_norm_w, mlp_norm_w, mlp_up, mlp_down, ssd_in_w, ssd_conv_w, ssd_conv_b, ssd_dt_bias, ssd_A_log, ssd_D, ssd_norm_w, ssd_out_w, sc_in_w, sc_conv_w, sc_out_w, final_norm_w, loss_target, m_ada_w, m_ada_b, m_mix_norm_w, m_mlp_norm_w, m_mlp_up, m_mlp_down, m_ssd_in_w, m_ssd_conv_w, m_ssd_conv_b, m_ssd_dt_bias, m_ssd_A_log, m_ssd_D, m_ssd_norm_w, m_ssd_out_w, m_sc_in_w, m_sc_conv_w, m_sc_out_w, m_final_norm_w, v_ada_w, v_ada_b, v_mix_norm_w, v_mlp_norm_w, v_mlp_up, v_mlp_down, v_ssd_in_w, v_ssd_conv_w, v_ssd_conv_b, v_ssd_dt_bias, v_ssd_A_log, v_ssd_D, v_ssd_norm_w, v_ssd_out_w, v_sc_in_w, v_sc_conv_w, v_sc_out_w, v_final_norm_w):
    xi, yi, ci = lax.axis_index("x"), lax.axis_index("y"), lax.axis_index("c")
    chip = 2 * xi + yi
    dev = 2 * chip + ci
    n_ada = ada_w.shape[2]

    conv_flat = jnp.concatenate([ssd_conv_w.reshape(-1), sc_conv_w.reshape(-1), jnp.zeros((256,), F32)]).reshape(4, D)
    blk0 = jnp.concatenate([c, conv_flat, jnp.zeros((3, D), F32)], axis=0)
    got0 = _all_gather_rows(blk0, name="gather_cond").reshape(N_DEV, 8, D)
    c_all = got0[:, 0]
    conv_all = got0[0::2, 1:5].reshape(N_CHIPS, 4 * D)
    ssd_conv = jnp.moveaxis(conv_all[:, :4 * 768].reshape(N_CHIPS, 4, 768), 0, 1).reshape(4, CONVD)
    sc_conv = jnp.moveaxis(conv_all[:, 4 * 768:4 * 768 + 3 * 256].reshape(N_CHIPS, 3, 256), 0, 1).reshape(3, D)
    mod_shard = [_matmul(c_all, ada_w, n=n_ada, a_silu=True, b_spec=pl.BlockSpec((None, D, 512), lambda mi, j, i=i: (i, 0, j)),
                         extras=(lax.dynamic_slice(ada_b, (i, chip * n_ada), (1, n_ada)),),
                         epi=lambda acc, b: (acc + b,), name=f"ada_mod{i}") for i in range(2)]
    mod_slot = lax.dynamic_update_slice(jnp.zeros((N_CHIPS, 2 * N_DEV, n_ada), F32), jnp.concatenate(mod_shard, axis=0)[None],
                                        (chip, 0, 0))

    up_row, down_row = 0, D
    chip1 = chip.reshape(1).astype(jnp.int32)
    a_bufs = [mod_slot, _own_slot([(jnp.swapaxes(ssd_in_w, 1, 2), 0)], chip1, name="slot_ssd_in")]
    fly_a, tok = _gather_start(a_bufs, name="gather_a_start")
    b_bufs = [_own_slot([(ssd_out_w, 0)], chip1, name="slot_ssd_out", after=(tok,)),
              _own_slot([(mlp_up, 0), (mlp_down, 0)], chip1, name="slot_mlp0_", after=(tok,))]
    fly_b, tok = _gather_start(b_bufs, name="gather_b_start", after=(tok,))
    c_bufs = [_own_slot([(sc_in_w, 0)], chip1, name="slot_sc_in", after=(tok,)),
              _own_slot([(sc_out_w, 0)], chip1, name="slot_sc_out", after=(tok,))]
    fly_c, tok = _gather_start(c_bufs, name="gather_c_start", after=(tok,))
    d_bufs = [_own_slot([(mlp_up, 1), (mlp_down, 1)], chip1, name="slot_mlp1_", after=(tok,))]
    fly_d, tok = _gather_start(d_bufs, name="gather_d_start", after=(tok,))

    row = lambda v: v.reshape(1, -1)
    xs, tgt = x[0], loss_target[0]
    prm = jnp.pad(jnp.concatenate([ssd_dt_bias, ssd_A_log, ssd_D, jnp.zeros((5, NH), F32)], axis=0), ((0, 0), (0, LANES - NH)))
    mix_nw = [row(mix_norm_w[i]) for i in range(2)]
    mlp_nw = [row(mlp_norm_w[i]) for i in range(2)]
    a_bufs = _gather_wait_first(fly_a, name="gather_a_landed", after=(tok,))
    mod_all, w_ssd_in = _gather_wait_forward(_gather_forward(a_bufs, name="gather_a_pass"), name="gather_a_done")
    mod = lax.dynamic_index_in_dim(mod_all.reshape(N_CHIPS, 2, N_DEV, n_ada), dev, axis=2, keepdims=False)
    mod = jnp.moveaxis(mod, 0, 1).reshape(2, 6, D)
    mods = [[mod[i, j:j + 1] for j in range(6)] for i in range(2)]
    w_in_t = w_ssd_in.reshape(N_CHIPS * SSD_IN_SHARD, D)
    w_dt_t = jnp.pad(w_in_t[ZX:], ((0, LANES - NH), (0, 0)))
    scan = _ssd_fwd_scan(xs, mods[0][0:3], mix_nw[0], w_in_t, w_dt_t, ssd_conv, ssd_conv_b, prm, "ssd")

    def land(flight, tag, after):
        return _gather_forward(_gather_wait_first(flight, name=f"gather_{tag}_landed", after=(after,)), name=f"gather_{tag}_pass")

    passed, got = {"b": land(fly_b, "b", scan[4])}, {}

    def done(tag, after):
        got[tag] = _gather_wait_forward(passed[tag], name=f"gather_{tag}_done", after=(after,))
        return got[tag]

    x1, s_ssd = _ssd_fwd_out(xs, mods[0][0:3], scan, ssd_norm_w, lambda yn: done("b", yn)[0], "ssd")
    w_ssd_out, w_b = got["b"]
    x2, s_mlp0 = _mlp_fwd(x1, mods[0][3:6], mlp_nw[0], w_b, up_row, down_row, "mlp0",
                          midway=lambda a: passed.update(c=land(fly_c, "c", a)))
    w_sc_in, w_sc_out = done("c", x2)
    x3, s_sc = _sc_layer_fwd(x2, mods[1][0:3], mix_nw[1], w_sc_in, sc_conv, w_sc_out, 0, "sc",
                             midway=lambda proj: passed.update(d=land(fly_d, "d", proj)))
    (w_mlp1,) = done("d", x3)
    x4, s_mlp1 = _mlp_fwd(x3, mods[1][3:6], mlp_nw[1], w_mlp1, up_row, down_row, "mlp1")

    core = ci.reshape(1).astype(jnp.int32)
    chip_core = jnp.stack([chip, ci]).astype(jnp.int32)

    def reduce_swap(gbufs, tag, after=()):
        return _sibling_start(gbufs, name=tag + "_sibling_start", after=after)

    def reduce_send(flight, tag, after):
        gs, sib = _sibling_wait(flight, name=tag + "_sibling_landed", after=after)
        hs = [_add_sibling_half(g, s, core, name=f"{tag}_add_sibling{b}") for b, (g, s) in enumerate(zip(gs, sib))]
        return _owners_start(hs, name=tag + "_owners_start")

    def reduce_sum(flight, tag, after):
        hs, lands = _owners_wait(flight, name=tag + "_owners_landed", after=after)
        ts = [_add_chip_sums(h, o, chip_core, name=f"{tag}_add_chips{b}") for b, (h, o) in enumerate(zip(hs, lands))]
        return _result_start(ts, name=tag + "_result_start")

    def reduce_done(flight, tag, after=()):
        return [t.reshape(-1, t.shape[2]) for t in _result_wait(flight, name=tag + "_result_landed", after=after)]

    dx4, fsum, dy, gs = _final_loss(x4, row(final_norm_w), tgt, (mods[1][5], s_mlp1[3]), name="final_loss")
    dx3, g_mlp1, sum_mlp1, dy, gs = _mlp_bwd(dx4, dy, gs, s_mlp1, mods[1][3:6], mlp_nw[1], w_mlp1, None, up_row, down_row,
                                             (mods[1][2], s_sc[4]), "mlp1")
    dx2, g_sc_out, g_sc_in, sum_sc, sc_csum, dy, gs = _sc_layer_bwd(dx3, dy, gs, s_sc, mods[1][0:3], mix_nw[1], w_sc_in,
                                                                    sc_conv, w_sc_out, None, 0, (mods[0][5], s_mlp0[3]), "sc")
    dx1, g_b, sum_mlp0, dy, gsum_ssd = _mlp_bwd(dx2, dy, gs, s_mlp0, mods[0][3:6], mlp_nw[0], w_b, None, up_row, down_row,
                                                (mods[0][2], s_ssd[8]), "mlp0")
    fly_1, tok = reduce_swap([g_mlp1, g_sc_out, g_sc_in, g_b], "rs1")
    dyn, g_ssd_out = _ssd_bwd_out(dy, s_ssd, w_ssd_out, "ssd", tok)
    fly_1, tok = reduce_send(fly_1, "rs1", (g_ssd_out,))
    dy, dzx, gnsum = _gnorm_bwd(s_ssd[5], s_ssd[2], ssd_norm_w + tok[0:1, 0:1], dyn, name="ssd_dgnorm")
    grad_x, d_w_zx, d_w_dt, sum_ssd, csum, ssum = _ssd_bwd_rest(
        dx1, dy, dzx, gsum_ssd, s_ssd, mods[0][0:3], mix_nw[0], w_in_t, w_dt_t, ssd_conv, prm, "ssd")

    def ssd_in_owner(k):
        lo, hi = k * SSD_IN_SHARD, (k + 1) * SSD_IN_SHARD
        if hi <= ZX:
            return d_w_zx[:, lo:hi]
        return jnp.concatenate([d_w_zx[:, lo:], d_w_dt[:, :hi - ZX].astype(BF16)], axis=1)

    small = jnp.concatenate([sum_ssd, sum_mlp0, sum_sc, sum_mlp1, csum.reshape(24, D)[0:16], gnsum.reshape(16, D)[0:8],
                             fsum, sc_csum, jnp.pad(ssum, ((0, 0), (0, D - LANES)))], axis=0)
    small_slot = lax.dynamic_update_slice(jnp.zeros((N_CHIPS, 2 * SMALL_ROWS, D), F32), small[None], (chip, ci * SMALL_ROWS, 0))
    fly_2, tok = reduce_swap([jnp.stack([ssd_in_owner(k) for k in range(N_CHIPS)]), g_ssd_out], "rs2")
    fly_s, tok = _gather_start([small_slot], name="gather_small_start", after=(tok,))
    fly_1, tok = reduce_sum(fly_1, "rs1", (grad_x, tok))
    fly_2, tok = reduce_send(fly_2, "rs2", (tok,))
    fly_s = _gather_forward(_gather_wait_first(fly_s, name="gather_small_landed", after=(tok,)), name="gather_small_pass")
    (small_all,) = _gather_wait_forward(fly_s, name="gather_small_done")
    t_mlp1, t_sc_out, t_sc_in, t_b = reduce_done(fly_1, "rs1", (small_all,))
    small_all = small_all.reshape(N_DEV, SMALL_ROWS, D)
    mod_rows = [r + o for r in SUB_ROW for o in (3, 2, 0)]
    c_pad = jnp.concatenate([c_all, jnp.zeros((8, D), F32)], axis=0)
    dmod_all = jnp.stack([small_all[:, r] for r in mod_rows], axis=1).reshape(N_DEV, 2, 6 * D)
    g_ada_w = []
    for i in range(2):
        dm = lax.dynamic_slice(dmod_all[:, i], (0, chip * n_ada), (N_DEV, n_ada))
        g_ada_w.append(_matmul_tn(c_pad, jnp.concatenate([dm, jnp.zeros_like(dm)], axis=0), m=D, n=n_ada, a_silu=True,
                                  name=f"ada_dw{i}"))

    big = dict(ada_w=[(g, 0) for g in g_ada_w], mlp_up=[(t_b, up_row), (t_mlp1, up_row)],
               mlp_down=[(t_b, down_row), (t_mlp1, down_row)], ssd_out_w=None, sc_out_w=[(t_sc_out, 0)],
               sc_in_w=[(t_sc_in, 0)], ssd_in_w=None)
    weights = dict(ada_w=(ada_w, m_ada_w, v_ada_w), ada_b=(ada_b, m_ada_b, v_ada_b),
                   mix_norm_w=(mix_norm_w, m_mix_norm_w, v_mix_norm_w), mlp_norm_w=(mlp_norm_w, m_mlp_norm_w, v_mlp_norm_w),
                   mlp_up=(mlp_up, m_mlp_up, v_mlp_up), mlp_down=(mlp_down, m_mlp_down, v_mlp_down),
                   ssd_in_w=(ssd_in_w, m_ssd_in_w, v_ssd_in_w), ssd_conv_w=(ssd_conv_w, m_ssd_conv_w, v_ssd_conv_w),
                   ssd_conv_b=(ssd_conv_b, m_ssd_conv_b, v_ssd_conv_b), ssd_dt_bias=(ssd_dt_bias, m_ssd_dt_bias, v_ssd_dt_bias),
                   ssd_A_log=(ssd_A_log, m_ssd_A_log, v_ssd_A_log), ssd_D=(ssd_D, m_ssd_D, v_ssd_D),
                   ssd_norm_w=(ssd_norm_w, m_ssd_norm_w, v_ssd_norm_w), ssd_out_w=(ssd_out_w, m_ssd_out_w, v_ssd_out_w),
                   sc_in_w=(sc_in_w, m_sc_in_w, v_sc_in_w), sc_conv_w=(sc_conv_w, m_sc_conv_w, v_sc_conv_w),
                   sc_out_w=(sc_out_w, m_sc_out_w, v_sc_out_w), final_norm_w=(final_norm_w, m_final_norm_w, v_final_norm_w))
    def step(nm, parts):
        w, m, v = (t if t.shape[0] == 1 else t.reshape(-1, t.shape[-1]) for t in weights[nm])
        rows, outs = w.shape[-2] // len(parts), None
        for i, (gbuf, g_row) in enumerate(parts):
            outs = _adamw(w, gbuf, m, v, g_row=g_row, w_row=i * rows, rows=rows, into=outs, emit_g=True, name=f"adamw_{nm}{i}")
        return outs

    res = {nm: step(nm, parts) for nm, parts in big.items() if parts is not None}
    fly_2, tok = reduce_sum(fly_2, "rs2", tuple(r[1] for r in res.values()))
    tot = _sum_devices(small_all, name="sum_small", after=(tok,))
    loss = tot[FINAL_ROW + 1, 0]
    conv_sums = tot[SSD_CONV_ROW:SSD_CONV_ROW + 15].reshape(5, CONVD)
    grads = dict(ada_b=jnp.stack([tot[r] for r in mod_rows]).reshape(2, 6 * D),
                 mix_norm_w=jnp.stack([tot[SUB_ROW[0] + 1], tot[SUB_ROW[2] + 1]]),
                 mlp_norm_w=jnp.stack([tot[SUB_ROW[1] + 1], tot[SUB_ROW[3] + 1]]),
                 ssd_conv_w=lax.dynamic_slice(conv_sums, (0, chip * 768), (4, 768))[None], ssd_conv_b=conv_sums[4:5],
                 ssd_dt_bias=tot[HEAD_ROW + 2:HEAD_ROW + 3, 0:NH], ssd_A_log=tot[HEAD_ROW:HEAD_ROW + 1, 0:NH],
                 ssd_D=tot[HEAD_ROW + 1:HEAD_ROW + 2, 0:NH], ssd_norm_w=tot[GNORM_ROW:GNORM_ROW + 2].reshape(1, DI),
                 sc_conv_w=lax.dynamic_slice(tot[SC_CONV_ROW:SC_CONV_ROW + 3], (0, chip * 256), (3, 256))[None],
                 final_norm_w=tot[FINAL_ROW])
    for nm, g in grads.items():
        w, m, v = weights[nm]
        two_d = (-1, w.shape[-1]) if w.ndim > 1 else (1, -1)
        res[nm] = (g, *_adamw(w.reshape(two_d), g.reshape(two_d), m.reshape(two_d), v.reshape(two_d), name="adamw_" + nm))
    t_ssd_in, t_ssd_out = reduce_done(fly_2, "rs2", tuple(res[nm][1] for nm in grads))
    res["ssd_out_w"] = step("ssd_out_w", [(t_ssd_out, 0)])
    w_t, m_t, v_t = (jnp.swapaxes(t[0], 0, 1) for t in weights["ssd_in_w"])
    res["ssd_in_w"] = [jnp.swapaxes(o, 0, 1) for o in _adamw(w_t, t_ssd_in.T, m_t, v_t, emit_g=True, name="adamw_ssd_in_w")]
    outs = [[res[nm][k].reshape(weights[nm][0].shape) for nm in weights] for k in range(4)]
    return (loss, grad_x[None], *outs[0], *outs[1], *outs[2], *outs[3])
```

```python
import jax
import jax.numpy as jnp
from jax import lax
from jax.experimental import pallas as pl
from jax.experimental.pallas import tpu as pltpu

F32 = jnp.float32
BF16 = jnp.bfloat16
MESH = pl.DeviceIdType.MESH

D = 1024
DFF = 4096
DI = 2048
NH = 32
HP = 64
NG = 4
NS = 128
CH = 128
CONVD = DI + 2 * NG * NS
ZX = DI + CONVD
GW = NG * NS
LANES = 128
N_CHIPS = 4
N_DEV = 8
EPS = 1e-5
ADAM_LR, ADAM_B1, ADAM_B2, ADAM_EPS, ADAM_WD, ADAM_STEP = 1e-3, 0.9, 0.999, 1e-8, 0.01, 10
VMEM_LIMIT = 48 * 1024 * 1024
TM_ALL = 2048
TM_HALF = 1024
ANY = pl.BlockSpec(memory_space=pl.ANY)
SEM = pl.BlockSpec(memory_space=pltpu.SEMAPHORE)

SSD_IN_SHARD = 1288
SC_IN_SHARD = 768


def _params(sem=None):
    return pltpu.CompilerParams(dimension_semantics=sem, vmem_limit_bytes=VMEM_LIMIT)


def _sigmoid(v):
    return 0.5 * jnp.tanh(0.5 * v) + 0.5


def _dot(a, b, dims=((1,), (0,)), precision=None):
    return lax.dot_general(a, b, (dims, ((), ())), preferred_element_type=F32, precision=precision)


def _dot_nt(a, b):
    return _dot(a, b, ((1,), (1,)))


def _dot_tn(a, b):
    return _dot(a, b, ((0,), (0,)))


def _nn(av, bv):
    return _dot(av.astype(BF16), bv.astype(BF16))


def _nt(av, bv):
    return _dot_nt(av.astype(BF16), bv.astype(BF16))


def _nn_split(av, bv):
    return _dot(av.astype(BF16), bv.reshape(-1, bv.shape[2]))


def _nn_split_sq(av, bv):
    return _nn_split(av * av, bv)


def _nt_split(av, bv):
    kc = bv.shape[2]
    acc = _dot_nt(av[:, 0:kc].astype(BF16), bv[0])
    for s in range(1, bv.shape[0]):
        acc = acc + _dot_nt(av[:, s * kc:(s + 1) * kc].astype(BF16), bv[s])
    return acc


def _nt_sc_in(av, bv):
    q = 256
    acc = None
    for i in range(3 * D // q):
        a_blk = av[i // 4][:, (i % 4) * q:(i % 4 + 1) * q]
        b_blk = bv[i // 3][:, (i % 3) * q:(i % 3 + 1) * q]
        t = _dot_nt(a_blk, b_blk)
        acc = t if acc is None else acc + t
    return acc


def _matmul(a, b, *, name, n, contract=_nn, a_spec=None, b_spec=None, tm=512, tn=512, extras=(), epi=None,
            out_dtypes=(F32,), a_silu=False):
    M = a.shape[-2]
    tm, tn = min(tm, M), min(tn, n)
    assert M % tm == 0 and n % tn == 0, (name, M, n, tm, tn)
    n_ex = len(extras)
    if a_spec is None:
        a_spec = pl.BlockSpec((tm, a.shape[1]), lambda i, j: (i, 0))
    if b_spec is None:
        b_spec = (pl.BlockSpec((tn, b.shape[1]), lambda i, j: (j, 0)) if contract is _nt
                  else pl.BlockSpec((b.shape[0], tn), lambda i, j: (0, j)))

    def body(*refs):
        av = refs[0][...]
        if a_silu:
            av = av * _sigmoid(av)
        acc = contract(av, refs[1][...])
        res = epi(acc, *[r[...] for r in refs[2:2 + n_ex]]) if epi is not None else (acc,)
        for o_ref, r in zip(refs[2 + n_ex:], res, strict=True):
            o_ref[...] = r.astype(o_ref.dtype)

    in_specs = [a_spec, b_spec]
    for e in extras:
        in_specs.append(pl.BlockSpec((1, tn), lambda i, j: (0, j)) if e.shape[0] == 1 and M != 1
                        else pl.BlockSpec((tm, tn), lambda i, j: (i, j)))
    outs = pl.pallas_call(
        body, grid=(M // tm, n // tn), in_specs=in_specs,
        out_specs=[pl.BlockSpec((tm, tn), lambda i, j: (i, j)) for _ in out_dtypes],
        out_shape=[jax.ShapeDtypeStruct((M, n), dt) for dt in out_dtypes],
        compiler_params=_params(("parallel", "parallel")), name=name)(a, b, *extras)
    return outs if len(out_dtypes) > 1 else outs[0]


def _matmul_tn(a, b, *, name, m, n, tm=512, tn=512, a_spec=None, b_spec=None, out_spec=None, out_struct=None, into=None,
               a_silu=False, a_square=False):
    T = a.shape[-2]
    tm, tn = min(tm, m), min(tn, n)
    assert m % tm == 0 and n % tn == 0, (name, m, n, tm, tn)
    if a_spec is None:
        a_spec = pl.BlockSpec((T, tm), lambda i, j: (0, i))
    if b_spec is None:
        b_spec = pl.BlockSpec((T, tn), lambda i, j: (0, j))
    if out_spec is None:
        out_spec, out_struct = pl.BlockSpec((tm, tn), lambda i, j: (i, j)), jax.ShapeDtypeStruct((m, n), F32)

    def body(a_ref, b_ref, *rest):
        av = a_ref[...]
        if a_silu:
            av = av * _sigmoid(av)
        if a_square:
            av = av * av
        rest[-1][...] = _dot_tn(av.astype(BF16), b_ref[...].astype(BF16)).astype(rest[-1].dtype)

    args, in_specs, alias = [a, b], [a_spec, b_spec], {}
    if into is not None:
        args, in_specs, alias = args + [into], in_specs + [ANY], {2: 0}
    return pl.pallas_call(body, grid=(m // tm, n // tn), in_specs=in_specs, out_specs=out_spec, out_shape=out_struct,
                          input_output_aliases=alias, compiler_params=_params(("parallel", "parallel")), name=name)(*args)


def _modnorm_fwd(x, nw, sc, sh, *, name):
    L = x.shape[0]
    tm = min(L, 512)

    def body(x_ref, nw_ref, sc_ref, sh_ref, h_ref):
        xv = x_ref[...]
        r = lax.rsqrt(jnp.mean(xv * xv, axis=-1, keepdims=True) + EPS)
        h_ref[...] = ((xv * r * nw_ref[...]) * (1.0 + sc_ref[...]) + sh_ref[...]).astype(BF16)

    row = pl.BlockSpec((tm, D), lambda i: (i, 0))
    vec = pl.BlockSpec((1, D), lambda i: (0, 0))
    return pl.pallas_call(body, grid=(L // tm,), in_specs=[row, vec, vec, vec], out_specs=row,
                          out_shape=jax.ShapeDtypeStruct((L, D), BF16),
                          compiler_params=_params(("parallel",)), name=name)(x, nw, sc, sh)


def _gate_outputs(dx, below_refs, dy_ref, gs_ref):
    g_ref, y_ref = below_refs
    dy_ref[...] = (dx * g_ref[...]).astype(BF16)
    gs_ref[0:1, :] += jnp.sum(dx * y_ref[...].astype(F32), axis=0, keepdims=True)


def _modnorm_bwd(x, dh, dxo, nw, sc, gsum, below, *, name):
    L = x.shape[0]
    tm = min(L, 256)
    nb = 0 if below is None else 2

    def body(x_ref, dh_ref, dxo_ref, nw_ref, sc_ref, g_ref, *rest):
        dx_ref, s_ref = rest[nb:nb + 2]

        @pl.when(pl.program_id(0) == 0)
        def _():
            s_ref[...] = g_ref[...]
            if nb:
                rest[-1][...] = jnp.zeros_like(rest[-1])

        xv, dhv = x_ref[...], dh_ref[...].astype(F32)
        r = lax.rsqrt(jnp.mean(xv * xv, axis=-1, keepdims=True) + EPS)
        xhat = xv * r
        dxhat = dhv * (nw_ref[...] * (1.0 + sc_ref[...]))
        dx = dxo_ref[...] + r * (dxhat - xhat * jnp.mean(dxhat * xhat, axis=-1, keepdims=True))
        dx_ref[...] = dx
        s_ref[1:2, :] += jnp.sum(dhv * xhat, axis=0, keepdims=True) * (1.0 + sc_ref[...])
        s_ref[2:3, :] += jnp.sum(dhv * xhat, axis=0, keepdims=True) * nw_ref[...]
        s_ref[3:4, :] += jnp.sum(dhv, axis=0, keepdims=True)
        if nb:
            _gate_outputs(dx, rest[:nb], rest[-2], rest[-1])

    row = pl.BlockSpec((tm, D), lambda i: (i, 0))
    vec = pl.BlockSpec((1, D), lambda i: (0, 0))
    blk = pl.BlockSpec((8, D), lambda i: (0, 0))
    in_specs, out_specs = [row, row, row, vec, vec, blk], [row, blk]
    out_shape = [jax.ShapeDtypeStruct((L, D), F32), jax.ShapeDtypeStruct((8, D), F32)]
    if nb:
        in_specs, out_specs = in_specs + [vec, row], out_specs + [row, blk]
        out_shape += [jax.ShapeDtypeStruct((L, D), BF16), jax.ShapeDtypeStruct((8, D), F32)]
    return pl.pallas_call(body, grid=(L // tm,), in_specs=in_specs, out_specs=out_specs, out_shape=out_shape,
                          compiler_params=_params(("arbitrary",)), name=name)(x, dh, dxo, nw, sc, gsum, *(below or ()))


def _final_loss(x, fw, tgt, below, *, name):
    L = x.shape[0]
    tm = min(L, 256)

    def body(x_ref, fw_ref, t_ref, g_ref, y_ref, dx_ref, s_ref, dy_ref, gs_ref):
        @pl.when(pl.program_id(0) == 0)
        def _():
            s_ref[...] = jnp.zeros_like(s_ref)
            gs_ref[...] = jnp.zeros_like(gs_ref)

        xv = x_ref[...]
        r = lax.rsqrt(jnp.mean(xv * xv, axis=-1, keepdims=True) + EPS)
        xhat = xv * r
        diff = xhat * fw_ref[...] - t_ref[...]
        dout = diff * (1.0 / D)
        dxhat = dout * fw_ref[...]
        dx = r * (dxhat - xhat * jnp.mean(dxhat * xhat, axis=-1, keepdims=True))
        dx_ref[...] = dx
        s_ref[0:1, :] += jnp.sum(dout * xhat, axis=0, keepdims=True)
        s_ref[1:2, :] += jnp.zeros((1, D), F32) + 0.5 * jnp.sum(jnp.sum(diff * diff, axis=-1, keepdims=True) * (1.0 / D))
        _gate_outputs(dx, (g_ref, y_ref), dy_ref, gs_ref)

    row = pl.BlockSpec((tm, D), lambda i: (i, 0))
    vec = pl.BlockSpec((1, D), lambda i: (0, 0))
    blk = pl.BlockSpec((8, D), lambda i: (0, 0))
    return pl.pallas_call(body, grid=(L // tm,), in_specs=[row, vec, row, vec, row], out_specs=[row, blk, row, blk],
                          out_shape=[jax.ShapeDtypeStruct((L, D), F32), jax.ShapeDtypeStruct((8, D), F32),
                                     jax.ShapeDtypeStruct((L, D), BF16), jax.ShapeDtypeStruct((8, D), F32)],
                          compiler_params=_params(("arbitrary",)), name=name)(x, fw, tgt, *below)


def _shift_down(v, j):
    if j == 0:
        return v
    rolled = pltpu.roll(v, j, 0)
    row = lax.broadcasted_iota(jnp.int32, (8, v.shape[1]), 0)
    return jnp.concatenate([jnp.where(row >= j, rolled[0:8], 0.0), rolled[8:]], axis=0)


def _shift_up(v, j):
    if j == 0:
        return v
    n = v.shape[0]
    rolled = pltpu.roll(v, n - j, 0)
    row = lax.broadcasted_iota(jnp.int32, (8, v.shape[1]), 0)
    return jnp.concatenate([rolled[:n - 8], jnp.where(row < 8 - j, rolled[n - 8:], 0.0)], axis=0)


def _ssd_conv_fwd(zx, w, b, *, name):
    L = zx.shape[0]
    cb = 256
    k = w.shape[0]

    def body(x_ref, w_ref, b_ref, o_ref, p_ref):
        xv = x_ref[...].astype(F32)
        pre = b_ref[...] + xv * w_ref[k - 1:k, :]
        for j in range(1, k):
            pre = pre + _shift_down(xv, j) * w_ref[k - 1 - j:k - j, :]
        o_ref[...] = (pre * _sigmoid(pre)).astype(BF16)
        p_ref[...] = pre.astype(BF16)

    blk = pl.BlockSpec((L, cb), lambda i: (0, i))
    return pl.pallas_call(
        body, grid=(CONVD // cb,),
        in_specs=[pl.BlockSpec((L, cb), lambda i: (0, i + DI // cb)), pl.BlockSpec((k, cb), lambda i: (0, i)),
                  pl.BlockSpec((1, cb), lambda i: (0, i))],
        out_specs=[blk, blk], out_shape=[jax.ShapeDtypeStruct((L, CONVD), BF16)] * 2,
        compiler_params=_params(("parallel",)), name=name)(zx, w, b)


def _ssd_conv_bwd(zx, pre, dact, w, dzx, *, name):
    L = zx.shape[0]
    cb = 256
    k = w.shape[0]

    def body(x_ref, p_ref, da_ref, w_ref, _, dx_ref, s_ref):
        xv, pv = x_ref[...].astype(F32), p_ref[...].astype(F32)
        s = _sigmoid(pv)
        dpre = da_ref[...].astype(F32) * (s * (1.0 + pv * (1.0 - s)))
        s_ref[...] = jnp.zeros_like(s_ref)
        s_ref[k:k + 1, :] = jnp.sum(dpre, axis=0, keepdims=True)
        s_ref[k - 1:k, :] = jnp.sum(dpre * xv, axis=0, keepdims=True)
        dx = dpre * w_ref[k - 1:k, :]
        for j in range(1, k):
            later = _shift_up(dpre, j)
            dx = dx + later * w_ref[k - 1 - j:k - j, :]
            s_ref[k - 1 - j:k - j, :] = jnp.sum(later * xv, axis=0, keepdims=True)
        dx_ref[...] = dx.astype(BF16)

    blk = pl.BlockSpec((L, cb), lambda i: (0, i))
    return pl.pallas_call(
        body, grid=(CONVD // cb,),
        in_specs=[pl.BlockSpec((L, cb), lambda i: (0, i + DI // cb)), blk, blk, pl.BlockSpec((k, cb), lambda i: (0, i)), ANY],
        out_specs=[pl.BlockSpec((L, cb), lambda i: (0, i + DI // cb)), pl.BlockSpec((8, cb), lambda i: (0, i))],
        out_shape=[jax.ShapeDtypeStruct((L, ZX), BF16), jax.ShapeDtypeStruct((8, CONVD), F32)],
        input_output_aliases={4: 0}, compiler_params=_params(("parallel",)), name=name)(zx, pre, dact, w, dzx)


def _sc_fwd(proj, w, *, name):
    L = proj.shape[0]
    cb = 256
    nb = D // cb
    k = w.shape[0]

    def body(b_ref, c_ref, x_ref, w_ref, o_ref, v_ref):
        u = c_ref[...].astype(F32) * x_ref[...].astype(F32)
        v = u * w_ref[k - 1:k, :]
        for j in range(1, k):
            v = v + _shift_down(u, j) * w_ref[k - 1 - j:k - j, :]
        o_ref[...] = (b_ref[...].astype(F32) * v).astype(BF16)
        v_ref[...] = v.astype(BF16)

    blk = pl.BlockSpec((L, cb), lambda i: (0, i))
    return pl.pallas_call(
        body, grid=(nb,),
        in_specs=[blk, pl.BlockSpec((L, cb), lambda i: (0, i + nb)), pl.BlockSpec((L, cb), lambda i: (0, i + 2 * nb)),
                  pl.BlockSpec((k, cb), lambda i: (0, i))],
        out_specs=[blk, blk], out_shape=[jax.ShapeDtypeStruct((L, D), BF16)] * 2,
        compiler_params=_params(("parallel",)), name=name)(proj, proj, proj, w)


def _sc_bwd(proj, v, dyv, w, *, name):
    L = proj.shape[0]
    cb = 256
    nb = D // cb
    k = w.shape[0]

    def body(b_ref, c_ref, x_ref, v_ref, dy_ref, w_ref, dp_ref, s_ref):
        cv, xv = c_ref[...].astype(F32), x_ref[...].astype(F32)
        u = cv * xv
        dyv_ = dy_ref[...].astype(F32)
        dp_ref[0] = (dyv_ * v_ref[...].astype(F32)).astype(BF16)
        dv = dyv_ * b_ref[...].astype(F32)
        s_ref[...] = jnp.zeros_like(s_ref)
        s_ref[k - 1:k, :] = jnp.sum(dv * u, axis=0, keepdims=True)
        du = dv * w_ref[k - 1:k, :]
        for j in range(1, k):
            later = _shift_up(dv, j)
            du = du + later * w_ref[k - 1 - j:k - j, :]
            s_ref[k - 1 - j:k - j, :] = jnp.sum(later * u, axis=0, keepdims=True)
        dp_ref[1] = (du * xv).astype(BF16)
        dp_ref[2] = (du * cv).astype(BF16)

    blk = pl.BlockSpec((L, cb), lambda i: (0, i))
    return pl.pallas_call(
        body, grid=(nb,),
        in_specs=[blk, pl.BlockSpec((L, cb), lambda i: (0, i + nb)), pl.BlockSpec((L, cb), lambda i: (0, i + 2 * nb)),
                  blk, blk, pl.BlockSpec((k, cb), lambda i: (0, i))],
        out_specs=[pl.BlockSpec((3, L, cb), lambda i: (0, 0, i)), pl.BlockSpec((8, cb), lambda i: (0, i))],
        out_shape=[jax.ShapeDtypeStruct((3, L, D), BF16), jax.ShapeDtypeStruct((8, D), F32)],
        compiler_params=_params(("parallel",)), name=name)(proj, proj, proj, v, dyv, w)


def _pieces(v, n):
    out, rest = [], v
    for _ in range(n):
        out.append(rest.astype(BF16))
        rest = rest - out[-1].astype(F32)
    return out


def _cumsum_rows(mask, v):
    m = mask.astype(BF16)
    return _dot(jnp.concatenate([m, m, m], axis=1), jnp.concatenate(_pieces(v, 3), axis=0))


def _ssd_chunk_terms(dtr, prm):
    lane = lax.broadcasted_iota(jnp.int32, (CH, LANES), 1)
    valid = lane < NH
    xdt = dtr + prm[0:1, :]
    dt = jnp.where(valid, jnp.maximum(xdt, 0.0) + jnp.log1p(jnp.exp(-jnp.abs(xdt))), 0.0)
    A = -jnp.exp(prm[1:2, :])
    ri = lax.broadcasted_iota(jnp.int32, (CH, CH), 0)
    ci = lax.broadcasted_iota(jnp.int32, (CH, CH), 1)
    cs = _cumsum_rows(ri >= ci, dt * A)
    last = cs[CH - 1:CH, :]
    spread = (lax.broadcasted_iota(jnp.int32, (2 * LANES, DI), 1) // HP
              == lax.broadcasted_iota(jnp.int32, (2 * LANES, DI), 0) % LANES).astype(BF16)
    gather = ((lax.broadcasted_iota(jnp.int32, (LANES, 2 * DI), 1) % DI) // HP
              == lax.broadcasted_iota(jnp.int32, (LANES, 2 * DI), 0)).astype(BF16)
    return dict(valid=valid, xdt=xdt, dt=dt, A=A, cs=cs, csT=cs.T, last=last, ri=ri, ci=ci, ex=(spread, gather))


def _expand(v, ex):
    if v.shape[0] == 1:
        return _expand(jnp.broadcast_to(v, (8, LANES)), ex)[0:1, :]
    return _dot(jnp.concatenate(_pieces(v, 2), axis=1), ex[0])


def _head_sum(v, ex):
    if v.shape[0] == 1:
        return _head_sum(jnp.broadcast_to(v, (8, DI)), ex)[0:1, :]
    return _dot_nt(jnp.concatenate(_pieces(v, 2), axis=1), ex[1])


def _ssd_fwd(xbc, dtr, prm, *, name):
    L = xbc.shape[0]
    nc = L // CH

    def body(xbc_ref, dtr_ref, prm_ref, y_ref, sp_ref, st_ref):
        @pl.when(pl.program_id(0) == 0)
        def _():
            st_ref[...] = jnp.zeros_like(st_ref)

        prm_v = prm_ref[...]
        t = _ssd_chunk_terms(dtr_ref[...], prm_v)
        cs, csT, ex, causal = t["cs"], t["csT"], t["ex"], t["ri"] >= t["ci"]
        xs = xbc_ref[:, 0:DI].astype(F32)
        X = xs * _expand(t["dt"], ex)
        Xb = X.astype(BF16)
        Xd = (X * _expand(jnp.exp(t["last"] - cs), ex)).astype(BF16)
        Ex = _expand(jnp.exp(cs), ex)
        cdx = _expand(jnp.exp(t["last"]), ex)
        dskx = _expand(prm_v[2:3, :], ex)
        lane = lax.broadcasted_iota(jnp.int32, (CH, LANES), 1)
        sp_ref[0] = st_ref[...]
        for g in range(NG):
            Bg = xbc_ref[:, DI + g * NS:DI + (g + 1) * NS].astype(BF16)
            Cg = xbc_ref[:, DI + GW + g * NS:DI + GW + (g + 1) * NS].astype(BF16)
            G = _dot_nt(Cg, Bg)
            Sg = st_ref[:, g * GW:(g + 1) * GW]
            yoff = _dot(Cg, Sg.astype(BF16)) * Ex[:, g * GW:(g + 1) * GW]
            for j in range(GW // LANES):
                lo = g * GW + j * LANES
                Xp = Xb[:, lo:lo + LANES]
                yd = []
                for h in (lo // HP, lo // HP + 1):
                    seg = cs[:, h:h + 1] - csT[h:h + 1, :]
                    yd.append(_dot((G * jnp.where(causal, jnp.exp(seg), 0.0)).astype(BF16), Xp))
                y_ref[:, lo:lo + LANES] = (jnp.where(lane < HP, yd[0], yd[1]) + yoff[:, j * LANES:(j + 1) * LANES]
                                           + dskx[:, lo:lo + LANES] * xs[:, lo:lo + LANES]).astype(BF16)
            st_ref[:, g * GW:(g + 1) * GW] = Sg * cdx[:, g * GW:(g + 1) * GW] + _dot_tn(Bg, Xd[:, g * GW:(g + 1) * GW])

    return pl.pallas_call(
        body, grid=(nc,),
        in_specs=[pl.BlockSpec((CH, CONVD), lambda c: (c, 0)), pl.BlockSpec((CH, LANES), lambda c: (c, 0)),
                  pl.BlockSpec((8, LANES), lambda c: (0, 0))],
        out_specs=[pl.BlockSpec((CH, DI), lambda c: (c, 0)), pl.BlockSpec((1, NS, DI), lambda c: (c, 0, 0))],
        out_shape=[jax.ShapeDtypeStruct((L, DI), BF16), jax.ShapeDtypeStruct((nc, NS, DI), F32)],
        scratch_shapes=[pltpu.VMEM((NS, DI), F32)],
        compiler_params=_params(("arbitrary",)), name=name)(xbc, dtr, prm)


def _ssd_bwd(xbc, dtr, prm, dy, sprev, *, name):
    L = xbc.shape[0]
    nc = L // CH

    def body(xbc_ref, dtr_ref, prm_ref, dy_ref, sp_ref, dxbc_ref, ddtr_ref, s_ref, dst_ref, dx_scr, de_scr, dd_scr):
        step = pl.program_id(0)

        @pl.when(step == 0)
        def _():
            dst_ref[...] = jnp.zeros_like(dst_ref)
            s_ref[...] = jnp.zeros_like(s_ref)

        prm_v = prm_ref[...]
        t = _ssd_chunk_terms(dtr_ref[...], prm_v)
        cs, csT, ex, ri, ci = t["cs"], t["csT"], t["ex"], t["ri"], t["ci"]
        E = jnp.exp(cs)
        dec = jnp.exp(t["last"] - cs)
        cd = jnp.exp(t["last"])
        xs = xbc_ref[:, 0:DI].astype(F32)
        dtx = _expand(t["dt"], ex)
        X = xs * dtx
        Xb = X.astype(BF16)
        decx = _expand(dec, ex)
        Xd = (X * decx).astype(BF16)
        Ex = _expand(E, ex)
        cdx = _expand(cd, ex)
        dskx = _expand(prm_v[2:3, :], ex)
        lane = lax.broadcasted_iota(jnp.int32, (CH, LANES), 1)
        dcs = jnp.zeros((CH, LANES), F32)
        dcd_x = []
        for g in range(NG):
            gs = slice(g * GW, (g + 1) * GW)
            Bg = xbc_ref[:, DI + g * NS:DI + (g + 1) * NS].astype(BF16)
            Cg = xbc_ref[:, DI + GW + g * NS:DI + GW + (g + 1) * NS].astype(BF16)
            G = _dot_nt(Cg, Bg)
            GT = _dot_nt(Bg, Cg)
            Sg = sp_ref[0, :, gs]
            Sgb = Sg.astype(BF16)
            dyg = dy_ref[:, gs]
            de_scr[:, gs] = dyg * _dot(Cg, Sgb)
            dYo = (Ex[:, gs] * dyg).astype(BF16)
            dC = _dot_nt(dYo, Sgb)
            dS_in = _dot_tn(Cg, dYo)
            dStg = dst_ref[:, gs]
            dStb = dStg.astype(BF16)
            dXd = _dot(Bg, dStb)
            dB = _dot_nt(Xd[:, gs], dStb)
            dd_scr[:, gs] = dXd * X[:, gs]
            dXst = dXd * decx[:, gs]
            dG = jnp.zeros((CH, CH), F32)
            dGT = jnp.zeros((CH, CH), F32)
            for j in range(GW // LANES):
                lo = g * GW + j * LANES
                Xp = Xb[:, lo:lo + LANES]
                dyp = dy_ref[:, lo:lo + LANES]
                dXp = dXst[:, j * LANES:(j + 1) * LANES]
                for k, h in enumerate((lo // HP, lo // HP + 1)):
                    dyh = jnp.where((lane < HP) if k == 0 else (lane >= HP), dyp, 0.0).astype(BF16)
                    seg = cs[:, h:h + 1] - csT[h:h + 1, :]
                    Lm = jnp.where(ri >= ci, jnp.exp(seg), 0.0)
                    LmT = jnp.where(ci >= ri, jnp.exp(-seg), 0.0)
                    dM = _dot_nt(dyh, Xp)
                    dMT = _dot_nt(Xp, dyh)
                    MT = GT * LmT
                    rs = jnp.sum(dM * (G * Lm), axis=1, keepdims=True) - jnp.sum(dMT * MT, axis=1, keepdims=True)
                    dcs = dcs + jnp.where(lane == h, rs, 0.0)
                    dG = dG + dM * Lm
                    dGT = dGT + dMT * LmT
                    dXp = dXp + _dot(MT.astype(BF16), dyh)
                dx_scr[:, lo:lo + LANES] = dXp
            dxbc_ref[:, DI + g * NS:DI + (g + 1) * NS] = (dB + _dot(dGT.astype(BF16), Cg)).astype(BF16)
            dxbc_ref[:, DI + GW + g * NS:DI + GW + (g + 1) * NS] = (dC + _dot(dG.astype(BF16), Bg)).astype(BF16)
            dcd_x.append(jnp.sum(dStg * Sg, axis=0, keepdims=True))
            dst_ref[:, gs] = dStg * cdx[:, gs] + dS_in
        dX = dx_scr[...]
        dy = dy_ref[...]
        ddec = _head_sum(dd_scr[...], ex)
        dcd = _head_sum(jnp.concatenate(dcd_x, axis=1), ex)
        dcs = dcs + _head_sum(de_scr[...], ex) * E - ddec * dec
        row = lax.broadcasted_iota(jnp.int32, (CH, LANES), 0)
        dcs = dcs + jnp.where(row == CH - 1, jnp.sum(ddec * dec, axis=0, keepdims=True) + dcd * cd, 0.0)
        da = _cumsum_rows(ci >= ri, dcs)
        ddt = da * t["A"] + _head_sum(dX * xs, ex)
        ddtr = jnp.where(t["valid"], ddt * _sigmoid(t["xdt"]), 0.0)
        ddtr_ref[...] = ddtr
        dxbc_ref[:, 0:DI] = (dX * dtx + dskx * dy).astype(BF16)
        s_ref[0:1, :] += jnp.sum(da * t["dt"], axis=0, keepdims=True)
        s_ref[1:2, :] += _head_sum(jnp.sum(dy * xs, axis=0, keepdims=True), ex)
        s_ref[2:3, :] += jnp.sum(ddtr, axis=0, keepdims=True)

        @pl.when(step == nc - 1)
        def _():
            s_ref[0:1, :] = s_ref[0:1, :] * t["A"]

    rev = lambda c: (nc - 1 - c, 0)
    return pl.pallas_call(
        body, grid=(nc,),
        in_specs=[pl.BlockSpec((CH, CONVD), rev), pl.BlockSpec((CH, LANES), rev), pl.BlockSpec((8, LANES), lambda c: (0, 0)),
                  pl.BlockSpec((CH, DI), rev), pl.BlockSpec((1, NS, DI), lambda c: (nc - 1 - c, 0, 0))],
        out_specs=[pl.BlockSpec((CH, CONVD), rev), pl.BlockSpec((CH, LANES), rev), pl.BlockSpec((8, LANES), lambda c: (0, 0))],
        out_shape=[jax.ShapeDtypeStruct((L, CONVD), BF16), jax.ShapeDtypeStruct((L, LANES), F32),
                   jax.ShapeDtypeStruct((8, LANES), F32)],
        scratch_shapes=[pltpu.VMEM((NS, DI), F32), pltpu.VMEM((CH, DI), F32), pltpu.VMEM((CH, DI), F32),
                        pltpu.VMEM((CH, DI), F32)],
        compiler_params=_params(("arbitrary",)), name=name)(xbc, dtr, prm, dy, sprev)


def _gnorm_fwd(y, zx, nw, *, name):
    L = y.shape[0]
    tm = min(L, 256)

    def body(y_ref, z_ref, nw_ref, o_ref):
        z = z_ref[...].astype(F32)
        yg = y_ref[...].astype(F32) * (z * _sigmoid(z))
        for g in range(NG):
            v = yg[:, g * GW:(g + 1) * GW]
            r = lax.rsqrt(jnp.mean(v * v, axis=-1, keepdims=True) + EPS)
            o_ref[:, g * GW:(g + 1) * GW] = (v * r * nw_ref[:, g * GW:(g + 1) * GW]).astype(BF16)

    row = pl.BlockSpec((tm, DI), lambda i: (i, 0))
    return pl.pallas_call(body, grid=(L // tm,), in_specs=[row, row, pl.BlockSpec((1, DI), lambda i: (0, 0))],
                          out_specs=row, out_shape=jax.ShapeDtypeStruct((L, DI), BF16),
                          compiler_params=_params(("parallel",)), name=name)(y, zx, nw)


def _gnorm_bwd(y, zx, nw, dyn, *, name):
    L = y.shape[0]
    tm = min(L, 256)

    def body(y_ref, z_ref, nw_ref, dyn_ref, dy_ref, dz_ref, s_ref):
        @pl.when(pl.program_id(0) == 0)
        def _():
            s_ref[...] = jnp.zeros_like(s_ref)

        z, yv = z_ref[...].astype(F32), y_ref[...].astype(F32)
        sz = _sigmoid(z)
        gate = z * sz
        dgate_dz = sz * (1.0 + z * (1.0 - sz))
        for g in range(NG):
            gs = slice(g * GW, (g + 1) * GW)
            v = yv[:, gs] * gate[:, gs]
            r = lax.rsqrt(jnp.mean(v * v, axis=-1, keepdims=True) + EPS)
            vhat = v * r
            dn = dyn_ref[:, gs].astype(F32)
            s_ref[0:1, gs] += jnp.sum(dn * vhat, axis=0, keepdims=True)
            dvhat = dn * nw_ref[:, gs]
            dv = r * (dvhat - vhat * jnp.mean(dvhat * vhat, axis=-1, keepdims=True))
            dy_ref[:, gs] = dv * gate[:, gs]
            dz_ref[:, gs] = (dv * yv[:, gs] * dgate_dz[:, gs]).astype(BF16)

    row = pl.BlockSpec((tm, DI), lambda i: (i, 0))
    return pl.pallas_call(body, grid=(L // tm,), in_specs=[row, row, pl.BlockSpec((1, DI), lambda i: (0, 0)), row],
                          out_specs=[row, row, pl.BlockSpec((8, DI), lambda i: (0, 0))],
                          out_shape=[jax.ShapeDtypeStruct((L, DI), F32), jax.ShapeDtypeStruct((L, ZX), BF16),
                                     jax.ShapeDtypeStruct((8, DI), F32)],
                          compiler_params=_params(("arbitrary",)), name=name)(y, zx, nw, dyn)


def _adam_update(w_ref, g_ref, m_ref, v_ref, outs):
    gv = g_ref[...]
    mn = ADAM_B1 * m_ref[...] + (1.0 - ADAM_B1) * gv
    vn = ADAM_B2 * v_ref[...] + (1.0 - ADAM_B2) * (gv * gv)
    m_hat = mn / (1.0 - ADAM_B1 ** ADAM_STEP)
    v_hat = vn / (1.0 - ADAM_B2 ** ADAM_STEP)
    d_ref, mo_ref, vo_ref = outs
    d_ref[...] = -ADAM_LR * (m_hat / (jnp.sqrt(v_hat) + ADAM_EPS) + ADAM_WD * w_ref[...])
    mo_ref[...] = mn
    vo_ref[...] = vn


def _adamw_parts(w, parts, m, v, *, name):
    R, C = w.shape
    n_parts = len(parts)
    rows = R // n_parts
    tr = max([t for t in range(8, rows + 1, 8) if rows % t == 0 and t * C <= 256 * 1024], default=rows)
    steps = rows // tr
    assert all(g_row % tr == 0 for _, g_row in parts), (name, tr)

    def body(w_ref, m_ref, v_ref, *rest):
        g_refs, outs = rest[:n_parts], rest[n_parts:]
        for p, g_ref in enumerate(g_refs):
            @pl.when(pl.program_id(0) // steps == p)
            def _(g_ref=g_ref):
                _adam_update(w_ref, g_ref, m_ref, v_ref, outs[1:])
                outs[0][...] = g_ref[...]

    blk = pl.BlockSpec((tr, C), lambda i: (i, 0))
    g_specs = [pl.BlockSpec((tr, C), lambda i, p=p, g_row=g_row: (jnp.clip(i - p * steps, 0, steps - 1) + g_row // tr, 0))
               for p, (_, g_row) in enumerate(parts)]
    return pl.pallas_call(body, grid=(R // tr,), in_specs=[blk, blk, blk] + g_specs, out_specs=[blk] * 4,
                          out_shape=[jax.ShapeDtypeStruct(w.shape, F32)] * 4,
                          compiler_params=_params(("parallel",)), name=name)(w, m, v, *[g for g, _ in parts])


def _adamw(w, g, m, v, *, name, g_row=0, w_row=0, rows=None, into=None, emit_g=False):
    lead = w.ndim == 3
    R, C = w.shape[-2:]
    rows = R if rows is None else rows
    tr = max([t for t in range(8, rows + 1, 8) if rows % t == 0 and t * C <= 256 * 1024], default=rows)
    assert g_row % tr == 0 and w_row % tr == 0, (name, g_row, w_row, tr)
    n_out = 4 if emit_g else 3

    def body(w_ref, g_ref, m_ref, v_ref, *rest):
        outs = rest[-n_out:]
        _adam_update(w_ref, g_ref, m_ref, v_ref, outs[-3:])
        if emit_g:
            outs[0][...] = g_ref[...]

    blk = (pl.BlockSpec((None, tr, C), lambda i: (0, i + w_row // tr, 0)) if lead
           else pl.BlockSpec((tr, C), lambda i: (i + w_row // tr, 0)))
    args, in_specs, alias = [w, g, m, v], [blk, pl.BlockSpec((tr, C), lambda i: (i + g_row // tr, 0)), blk, blk], {}
    if into is not None:
        args, in_specs, alias = args + list(into), in_specs + [ANY] * n_out, {4 + k: k for k in range(n_out)}
    return pl.pallas_call(body, grid=(rows // tr,), in_specs=in_specs, out_specs=[blk] * n_out,
                          out_shape=[jax.ShapeDtypeStruct(w.shape, F32)] * n_out, input_output_aliases=alias,
                          compiler_params=_params(("parallel",)), name=name)(*args)


def _residual(acc, xv, gv):
    return xv + gv * acc, acc


def _like(buf):
    return jax.ShapeDtypeStruct(buf.shape, buf.dtype)


def _mlp_fwd(x, mod, nw, wb, up_row, down_row, tag, midway=None):
    sh, sc, g = mod
    h = _modnorm_fwd(x, nw, sc, sh, name=tag + "_norm")
    a = _matmul(h, wb, n=DFF, tm=TM_ALL, b_spec=pl.BlockSpec((None, D, 512), lambda mi, j: (j // 2, up_row // D, j % 2)),
                epi=lambda acc: (jnp.maximum(acc, 0.0),), out_dtypes=(BF16,), name=tag + "_up")
    if midway is not None:
        midway(a)
    xn, y = _matmul(a, wb, n=D, tm=TM_HALF, contract=_nn_split_sq,
                    b_spec=pl.BlockSpec((N_CHIPS, D, 512), lambda mi, j: (0, down_row // D, j)),
                    extras=(x, g), epi=_residual, out_dtypes=(F32, BF16), name=tag + "_down")
    return xn, (x, h, a, y)


def _mlp_bwd(dxo, dy, gsum, saved, mod, nw, wb, gb, up_row, down_row, below, tag):
    x, h, a, y = saved
    sh, sc, g = mod
    du = _matmul(dy, wb, n=DFF, tm=TM_ALL, contract=_nt,
                 b_spec=pl.BlockSpec((None, 512, D), lambda mi, j: (j // 2, down_row // 512 + j % 2, 0)),
                 extras=(a,), epi=lambda acc, av: (acc * (2.0 * av.astype(F32)),), out_dtypes=(BF16,), name=tag + "_dact")
    gb = _matmul_tn(a, dy, m=DFF, n=D, tm=D, tn=D, a_square=True, into=gb, out_struct=_like(wb),
                    out_spec=pl.BlockSpec((None, D, D), lambda mi, j: (mi, down_row // D, 0)), name=tag + "_ddown")
    dh = _matmul(du, wb, n=D, tm=TM_HALF, contract=_nt_split,
                 b_spec=pl.BlockSpec((N_CHIPS, 512, D), lambda mi, j: (0, up_row // 512 + j, 0)), out_dtypes=(BF16,),
                 name=tag + "_dh")
    gb = _matmul_tn(h, du, m=D, n=DFF, tm=D, into=gb, out_struct=_like(wb),
                    out_spec=pl.BlockSpec((None, D, 512), lambda mi, j: (j // 2, up_row // D, j % 2)), name=tag + "_dup")
    dx, sums, *nxt = _modnorm_bwd(x, dh, dxo, nw, sc, gsum, below, name=tag + "_dnorm")
    return dx, gb, sums, *nxt


def _ssd_fwd_scan(x, mod, nw, w_in_t, w_dt_t, conv_w, conv_b, prm, tag):
    sh, sc, g = mod
    h = _modnorm_fwd(x, nw, sc, sh, name=tag + "_norm")
    zx = _matmul(h, w_in_t, n=ZX, tm=TM_ALL, contract=_nt, out_dtypes=(BF16,), name=tag + "_in")
    dtr = _matmul(h, w_dt_t, n=LANES, tm=TM_ALL, contract=_nt, name=tag + "_in_dt")
    xbc, pre = _ssd_conv_fwd(zx, conv_w, conv_b, name=tag + "_conv")
    y, sprev = _ssd_fwd(xbc, dtr, prm, name=tag + "_scan")
    return h, zx, dtr, xbc, y, sprev, pre


def _ssd_fwd_out(x, mod, scan, gn_w, get_w_out, tag):
    sh, sc, g = mod
    h, zx, dtr, xbc, y, sprev, pre = scan
    yn = _gnorm_fwd(y, zx, gn_w, name=tag + "_gnorm")
    w_out = get_w_out(yn)
    xn, yo = _matmul(yn, w_out, n=D, tm=TM_HALF, contract=_nn_split,
                     b_spec=pl.BlockSpec((N_CHIPS, 512, 512), lambda mi, j: (0, 0, j)),
                     extras=(x, g), epi=_residual, out_dtypes=(F32, BF16), name=tag + "_out")
    return xn, (x, h, zx, dtr, xbc, y, sprev, yn, yo, pre)


def _ssd_bwd_out(dyo, saved, w_out, tag, after):
    x, h, zx, dtr, xbc, y, sprev, yn, yo, pre = saved
    dyn = _matmul(dyo, w_out, n=DI, tm=TM_ALL, contract=_nt, b_spec=pl.BlockSpec((None, 512, D), lambda mi, j: (j, 0, 0)),
                  extras=(jnp.broadcast_to(after[0:1, 0:1], (1, DI)),), epi=lambda acc, t: (acc + t,),
                  out_dtypes=(BF16,), name=tag + "_dyn")
    g_out = _matmul_tn(yn, dyo, m=DI, n=D, tn=D, out_struct=_like(w_out),
                       out_spec=pl.BlockSpec((None, 512, D), lambda mi, j: (mi, 0, 0)), name=tag + "_dout")
    return dyn, g_out


def _ssd_bwd_rest(dxo, dy, dzx, gsum, saved, mod, nw, w_in_t, w_dt_t, conv_w, prm, tag):
    x, h, zx, dtr, xbc, y, sprev, yn, yo, pre = saved
    sh, sc, g = mod
    dxbc, ddtr, ssum = _ssd_bwd(xbc, dtr, prm, dy, sprev, name=tag + "_dscan")
    dzx, csum = _ssd_conv_bwd(zx, pre, dxbc, conv_w, dzx, name=tag + "_dconv")
    dh_dt = _matmul(ddtr, w_dt_t, n=D, tm=TM_ALL, name=tag + "_dh_dt")
    dh = _matmul(dzx, w_in_t, n=D, tm=TM_HALF, b_spec=pl.BlockSpec((ZX, 512), lambda mi, j: (0, j)), extras=(dh_dt,),
                 epi=lambda acc, e: (acc + e,), out_dtypes=(BF16,), name=tag + "_dh")
    d_w_zx = _matmul_tn(h, dzx, m=D, n=ZX, tm=D, out_spec=pl.BlockSpec((D, 512), lambda i, j: (i, j)),
                        out_struct=jax.ShapeDtypeStruct((D, ZX), BF16), name=tag + "_din")
    d_w_dt = _matmul_tn(h, ddtr, m=D, n=LANES, tm=D, name=tag + "_din_dt")
    dx, sums = _modnorm_bwd(x, dh, dxo, nw, sc, gsum, None, name=tag + "_dnorm")
    return dx, d_w_zx, d_w_dt, sums, csum, ssum


def _sc_layer_fwd(x, mod, nw, w_sc_in, conv_w, wb, out_row, tag, midway=None):
    sh, sc, g = mod
    h = _modnorm_fwd(x, nw, sc, sh, name=tag + "_norm")
    proj = _matmul(h, w_sc_in, n=3 * D, tm=TM_ALL, tn=256, out_dtypes=(BF16,),
                   b_spec=pl.BlockSpec((None, D, 256), lambda mi, j: (j // 3, 0, j % 3)),
                   name=tag + "_in")
    if midway is not None:
        midway(proj)
    yv, v = _sc_fwd(proj, conv_w, name=tag + "_conv")
    xn, yo = _matmul(yv, wb, n=D, tm=TM_HALF, contract=_nn_split,
                     b_spec=pl.BlockSpec((N_CHIPS, 256, 512), lambda mi, j: (0, out_row // 256, j)),
                     extras=(x, g), epi=_residual, out_dtypes=(F32, BF16), name=tag + "_out")
    return xn, (x, h, proj, yv, yo, v)


def _sc_layer_bwd(dxo, dyo, gsum, saved, mod, nw, w_sc_in, conv_w, wb, gb, out_row, below, tag):
    x, h, proj, yv, yo, v = saved
    sh, sc, g = mod
    L = x.shape[0]
    dyv = _matmul(dyo, wb, n=D, tm=TM_ALL, tn=256, contract=_nt,
                  b_spec=pl.BlockSpec((None, 256, D), lambda mi, j: (j, out_row // 256, 0)), out_dtypes=(BF16,),
                  name=tag + "_dyv")
    gb = _matmul_tn(yv, dyo, m=D, n=D, tm=256, tn=D, into=gb, out_struct=_like(wb),
                    out_spec=pl.BlockSpec((None, 256, D), lambda mi, j: (mi, out_row // 256, 0)), name=tag + "_dout")
    dproj, csum = _sc_bwd(proj, v, dyv, conv_w, name=tag + "_dconv")
    tm = min(L, TM_HALF)
    dh = _matmul(dproj, w_sc_in, n=D, tm=tm, contract=_nt_sc_in, a_spec=pl.BlockSpec((3, tm, D), lambda mi, j: (0, mi, 0)),
                 b_spec=pl.BlockSpec((N_CHIPS, 512, SC_IN_SHARD), lambda mi, j: (0, j, 0)), out_dtypes=(BF16,),
                 name=tag + "_dh")
    g_sc_in = _matmul_tn(h, dproj, m=D, n=3 * D, tm=D, tn=256, b_spec=pl.BlockSpec((None, L, 256), lambda mi, j: (j // 4, 0, j % 4)),
                         out_spec=pl.BlockSpec((None, D, 256), lambda mi, j: (j // 3, 0, j % 3)),
                         out_struct=jax.ShapeDtypeStruct((N_CHIPS, D, SC_IN_SHARD), BF16), name=tag + "_din")
    dx, sums, *nxt = _modnorm_bwd(x, dh, dxo, nw, sc, gsum, below, name=tag + "_dnorm")
    return dx, gb, g_sc_in, sums, csum, *nxt


SUB_ROW = (0, 8, 16, 24)
SSD_CONV_ROW, GNORM_ROW, FINAL_ROW, SC_CONV_ROW, HEAD_ROW, SMALL_ROWS = 32, 48, 56, 64, 72, 80


def _all_gather_rows(blk, *, name):
    m_per, n = blk.shape

    def body(x_ref, out_ref, send_sems, recv_sems, local_sem):
        x, y, c = lax.axis_index("x"), lax.axis_index("y"), lax.axis_index("c")
        me, sibling = (x, y, c), (x, y, 1 - c)
        chips = [(1 - x, y), (x, 1 - y), (1 - x, 1 - y)]

        def rows(px, py, pc):
            return out_ref.at[pl.ds((4 * px + 2 * py + pc) * m_per, m_per), :]

        def copy(k, block, to, src=None):
            return pltpu.make_async_remote_copy(src_ref=rows(*block) if src is None else src, dst_ref=rows(*block),
                                                send_sem=send_sems.at[k], recv_sem=recv_sems.at[k], device_id=to,
                                                device_id_type=MESH)

        mine = pltpu.make_async_copy(x_ref, rows(*me), local_sem)
        mine.start()
        first = [copy(0, me, sibling, src=x_ref)] + [copy(1 + j, me, (*chip, c), src=x_ref) for j, chip in enumerate(chips)]
        for cp in first:
            cp.start()
        passed = [copy(4 + j, (*chip, c), sibling) for j, chip in enumerate(chips)]
        for j, chip in enumerate(chips):
            copy(1 + j, (*chip, c), me).wait_recv()
            passed[j].start()
        copy(0, sibling, me).wait_recv()
        for j, chip in enumerate(chips):
            copy(4 + j, (*chip, 1 - c), me).wait_recv()
        for cp in first + passed:
            cp.wait_send()
        mine.wait()

    return pl.pallas_call(
        body, out_shape=jax.ShapeDtypeStruct((N_DEV * m_per, n), blk.dtype),
        in_specs=[pl.BlockSpec(memory_space=pltpu.VMEM)], out_specs=pl.BlockSpec(memory_space=pltpu.VMEM),
        scratch_shapes=[pltpu.SemaphoreType.DMA((7,)), pltpu.SemaphoreType.DMA((7,)), pltpu.SemaphoreType.DMA],
        name=name)(blk)


def _half(ref, chip, c):
    r, n = ref.shape[1:]
    if r % 32 == 0:
        return ref.at[chip, pl.ds(c * (r // 2), r // 2), :]
    assert n % 256 == 0, ref.shape
    return ref.at[chip, :, pl.ds(c * (n // 2), n // 2)]


def _gather_copy(bufs, sends, recvs, b, k, chip, pc, to):
    piece = _half(bufs[b], 2 * chip[0] + chip[1], pc)
    return pltpu.make_async_remote_copy(src_ref=piece, dst_ref=piece, send_sem=sends.at[4 * b + k], recv_sem=recvs.at[4 * b + k],
                                        device_id=to, device_id_type=MESH)


def _split_call(body, bufs, sems_in, n_sems, *, name, after=(), token=False, lands=()):
    nb, na, nl, starts = len(bufs), len(after), len(lands), not sems_in

    def wrapped(*refs):
        sems = refs[nb + na:nb + na + 2] if starts else refs[nb:nb + 2]
        made = refs[nb + na + 2 + nb:nb + na + 2 + nb + nl] if starts else ()
        body(tuple(refs[:nb]) + tuple(made), sems[0], sems[1])
        if token:
            refs[-1][...] = jnp.zeros_like(refs[-1])

    out_shape = [pltpu.SemaphoreType.DMA((n_sems,)) for _ in range(2 if starts else 0)]
    out_specs = [SEM] * len(out_shape) + [ANY] * (nb + nl)
    alias = {b: len(out_shape) + b for b in range(nb)}
    out_shape += [jax.ShapeDtypeStruct(b.shape, b.dtype) for b in bufs] + list(lands)
    if token:
        out_shape.append(jax.ShapeDtypeStruct((8, LANES), F32))
        out_specs.append(pl.BlockSpec(memory_space=pltpu.VMEM))
    return pl.pallas_call(
        wrapped, out_shape=out_shape, in_specs=[ANY] * nb + [SEM] * len(sems_in) + [ANY] * na, out_specs=out_specs,
        input_output_aliases=alias,
        compiler_params=pltpu.CompilerParams(has_side_effects=pltpu.SideEffectType.DATAFLOW_SIDE_EFFECTING),
        name=name)(*bufs, *sems_in, *after)


def _gather_start(bufs, *, name, after=()):
    nb = len(bufs)

    def body(ins, sends, recvs):
        x, y, c = lax.axis_index("x"), lax.axis_index("y"), lax.axis_index("c")
        chips = [(1 - x, y), (x, 1 - y), (1 - x, 1 - y)]
        for b in range(nb):
            _gather_copy(ins, sends, recvs, b, 0, (x, y), c, (x, y, 1 - c)).start()
            for j, chip in enumerate(chips):
                _gather_copy(ins, sends, recvs, b, 1 + j, (x, y), c, (*chip, c)).start()

    out = _split_call(body, bufs, (), 4 * nb, name=name, after=after, token=True)
    return (out[0], out[1], out[2:2 + nb]), out[-1]


def _gather_wait_first(flight, *, name, after=()):
    sends, recvs, bufs = flight
    nb = len(bufs)

    def body(ins, sends_, recvs_):
        x, y, c = lax.axis_index("x"), lax.axis_index("y"), lax.axis_index("c")
        chips = [(1 - x, y), (x, 1 - y), (1 - x, 1 - y)]
        for b in range(nb):
            _gather_copy(ins, sends_, recvs_, b, 0, (x, y), c, (x, y, 1 - c)).wait_send()
            _gather_copy(ins, sends_, recvs_, b, 0, (x, y), 1 - c, (x, y, c)).wait_recv()
            for j, chip in enumerate(chips):
                _gather_copy(ins, sends_, recvs_, b, 1 + j, (x, y), c, (*chip, c)).wait_send()
                _gather_copy(ins, sends_, recvs_, b, 1 + j, chip, c, (x, y, c)).wait_recv()

    return _split_call(body, bufs, (sends, recvs), 4 * nb, name=name, after=after)


def _gather_forward(bufs, *, name):
    nb = len(bufs)

    def body(ins, sends, recvs):
        x, y, c = lax.axis_index("x"), lax.axis_index("y"), lax.axis_index("c")
        chips = [(1 - x, y), (x, 1 - y), (1 - x, 1 - y)]
        for b in range(nb):
            for j, chip in enumerate(chips):
                _gather_copy(ins, sends, recvs, b, 1 + j, chip, c, (x, y, 1 - c)).start()

    out = _split_call(body, bufs, (), 4 * nb, name=name)
    return out[0], out[1], out[2:2 + nb]


def _gather_wait_forward(flight, *, name, after=()):
    sends, recvs, bufs = flight
    nb = len(bufs)

    def body(ins, sends_, recvs_):
        x, y, c = lax.axis_index("x"), lax.axis_index("y"), lax.axis_index("c")
        chips = [(1 - x, y), (x, 1 - y), (1 - x, 1 - y)]
        for b in range(nb):
            for j, chip in enumerate(chips):
                _gather_copy(ins, sends_, recvs_, b, 1 + j, chip, c, (x, y, 1 - c)).wait_send()
                _gather_copy(ins, sends_, recvs_, b, 1 + j, chip, 1 - c, (x, y, c)).wait_recv()

    return _split_call(body, bufs, (sends, recvs), 4 * nb, name=name, after=after)


def _owner_copies(hs, lands, sends, recvs):
    x, y, c = lax.axis_index("x"), lax.axis_index("y"), lax.axis_index("c")
    chips = [(1 - x, y), (x, 1 - y), (1 - x, 1 - y)]
    return [pltpu.make_async_remote_copy(src_ref=hs[b].at[2 * cx + cy], dst_ref=lands[b].at[j], send_sem=sends.at[3 * b + j],
                                         recv_sem=recvs.at[3 * b + j], device_id=(cx, cy, c), device_id_type=MESH)
            for b in range(len(hs)) for j, (cx, cy) in enumerate(chips)]


def _owners_start(hs, *, name):
    nb = len(hs)
    lands = [jax.ShapeDtypeStruct((3,) + h.shape[1:], h.dtype) for h in hs]

    def body(refs, sends, recvs):
        for cp in _owner_copies(refs[:nb], refs[nb:], sends, recvs):
            cp.start()

    out = _split_call(body, list(hs), (), 3 * nb, name=name, token=True, lands=lands)
    return (out[0], out[1], out[2:2 + 2 * nb]), out[-1]


def _owners_wait(flight, *, name, after=()):
    sends, recvs, bufs = flight
    nb = len(bufs) // 2

    def body(refs, sends_, recvs_):
        for cp in _owner_copies(refs[:nb], refs[nb:], sends_, recvs_):
            cp.wait()

    out = _split_call(body, bufs, (sends, recvs), 3 * nb, name=name, after=after)
    return out[:nb], out[nb:]


def _sibling_copies(gs, lands, sends, recvs):
    x, y, c = lax.axis_index("x"), lax.axis_index("y"), lax.axis_index("c")
    copies = []
    for b in range(len(gs)):
        hr = gs[b].shape[1] // 2
        copies.append(pltpu.make_async_remote_copy(
            src_ref=gs[b].at[:, pl.ds((1 - c) * hr, hr), :], dst_ref=lands[b], send_sem=sends.at[b], recv_sem=recvs.at[b],
            device_id=(x, y, 1 - c), device_id_type=MESH))
    return copies


def _sibling_start(gs, *, name, after=()):
    nb = len(gs)
    lands = [jax.ShapeDtypeStruct((g.shape[0], g.shape[1] // 2, g.shape[2]), g.dtype) for g in gs]

    def body(refs, sends, recvs):
        for cp in _sibling_copies(refs[:nb], refs[nb:], sends, recvs):
            cp.start()

    out = _split_call(body, list(gs), (), nb, name=name, after=after, token=True, lands=lands)
    return (out[0], out[1], out[2:2 + 2 * nb]), out[-1]


def _sibling_wait(flight, *, name, after=()):
    sends, recvs, bufs = flight
    nb = len(bufs) // 2

    def body(refs, sends_, recvs_):
        for cp in _sibling_copies(refs[:nb], refs[nb:], sends_, recvs_):
            cp.wait()

    out = _split_call(body, bufs, (sends, recvs), nb, name=name, after=after)
    return out[:nb], out[nb:]


def _result_copies(ts, sends, recvs):
    x, y, c = lax.axis_index("x"), lax.axis_index("y"), lax.axis_index("c")
    return [pltpu.make_async_remote_copy(src_ref=ts[b].at[c], dst_ref=ts[b].at[c], send_sem=sends.at[b], recv_sem=recvs.at[b],
                                         device_id=(x, y, 1 - c), device_id_type=MESH) for b in range(len(ts))]


def _result_start(ts, *, name):
    def body(refs, sends, recvs):
        for cp in _result_copies(refs, sends, recvs):
            cp.start()

    out = _split_call(body, ts, (), len(ts), name=name, token=True)
    return (out[0], out[1], out[2:2 + len(ts)]), out[-1]


def _result_wait(flight, *, name, after=()):
    sends, recvs, bufs = flight

    def body(refs, sends_, recvs_):
        for cp in _result_copies(refs, sends_, recvs_):
            cp.wait()

    return _split_call(body, bufs, (sends, recvs), len(bufs), name=name, after=after)


def _row_tile(rows, cols):
    best = 16
    for t in range(16, rows + 1, 16):
        if rows % t == 0 and t * cols <= 640 * 1024:
            best = t
    assert rows % best == 0, (rows, cols)
    return best


def _add_sibling_half(g, recv, core, *, name):
    nk, r, n = g.shape
    hr = r // 2
    tr = _row_tile(hr, n)

    def body(c_ref, a_ref, b_ref, o_ref):
        o_ref[...] = (a_ref[...].astype(F32) + b_ref[...].astype(F32)).astype(BF16)

    grid_spec = pltpu.PrefetchScalarGridSpec(
        num_scalar_prefetch=1, grid=(nk, hr // tr),
        in_specs=[pl.BlockSpec((None, tr, n), lambda k, i, c_ref: (k, c_ref[0] * (hr // tr) + i, 0)),
                  pl.BlockSpec((None, tr, n), lambda k, i, c_ref: (k, i, 0))],
        out_specs=pl.BlockSpec((None, tr, n), lambda k, i, c_ref: (k, i, 0)))
    return pl.pallas_call(body, grid_spec=grid_spec, out_shape=jax.ShapeDtypeStruct((nk, hr, n), BF16),
                          compiler_params=_params(("parallel", "parallel")), name=name)(core, g, recv)


def _add_chip_sums(h, recv, chip_core, *, name):
    _, hr, n = h.shape
    tr = _row_tile(hr, n)

    def body(k_ref, a_ref, b_ref, o_ref):
        o_ref[...] = ((a_ref[...].astype(F32) + b_ref[0].astype(F32)) + b_ref[1].astype(F32)) + b_ref[2].astype(F32)

    grid_spec = pltpu.PrefetchScalarGridSpec(
        num_scalar_prefetch=1, grid=(hr // tr,),
        in_specs=[pl.BlockSpec((None, tr, n), lambda i, k_ref: (k_ref[0], i, 0)),
                  pl.BlockSpec((3, tr, n), lambda i, k_ref: (0, i, 0))],
        out_specs=pl.BlockSpec((None, tr, n), lambda i, k_ref: (k_ref[1], i, 0)))
    return pl.pallas_call(body, grid_spec=grid_spec, out_shape=jax.ShapeDtypeStruct((2, hr, n), F32),
                          compiler_params=_params(("parallel",)), name=name)(chip_core, h, recv)


def _sum_devices(g, *, name, after=()):
    nd, r, n = g.shape

    def body(g_ref, *rest):
        acc = g_ref[0]
        for i in range(1, nd):
            acc = acc + g_ref[i]
        rest[-1][...] = acc

    return pl.pallas_call(body, out_shape=jax.ShapeDtypeStruct((r, n), F32),
                          in_specs=[pl.BlockSpec(memory_space=pltpu.VMEM)] + [ANY] * len(after), name=name)(g, *after)


def _own_slot(parts, chip, *, name, after=()):
    rows, cols = sum(w.shape[1] for w, _ in parts), parts[0][0].shape[2]
    buf, row0 = None, 0
    for p, (w, idx) in enumerate(parts):
        r = w.shape[1]
        tr = 256 if r % 256 == 0 else r
        assert row0 % tr == 0, (name, r, row0)
        prev = () if buf is None else (buf,)

        def body(chip_ref, w_ref, *rest):
            rest[-1][...] = w_ref[...].astype(BF16)

        grid_spec = pltpu.PrefetchScalarGridSpec(
            num_scalar_prefetch=1, grid=(r // tr,),
            in_specs=[pl.BlockSpec((None, tr, cols), lambda i, c_ref, idx=idx: (idx, i, 0))] + [ANY] * (len(prev) + len(after)),
            out_specs=pl.BlockSpec((None, tr, cols), lambda i, c_ref, row0=row0, tr=tr: (c_ref[0], row0 // tr + i, 0)))
        buf = pl.pallas_call(body, grid_spec=grid_spec, out_shape=jax.ShapeDtypeStruct((N_CHIPS, rows, cols), BF16),
                             input_output_aliases={2: 0} if prev else {}, compiler_params=_params(("parallel",)),
                             name=f"{name}{p}")(chip, w, *prev, *after)
        row0 += r
    return buf


def kernel(x, c, ada_w, ada_b, mix_norm_w, mlp_norm_w, mlp_up, mlp_down, ssd_in_w, ssd_conv_w, ssd_conv_b, ssd_dt_bias, ssd_A_log, ssd_D, ssd_norm_w, ssd_out_w, sc_in_w, sc_conv_w, sc_out_w, final_norm_w, loss_target, m_ada_w, m_ada_b, m_mix_norm_w, m_mlp_norm_w, m_mlp_up, m_mlp_down, m_ssd_in_w, m_ssd_conv_w, m_ssd_conv_b, m_ssd_dt_bias, m_ssd_A_log, m_ssd_D, m_ssd_norm_w, m_ssd_out_w, m_sc_in_w, m_sc_conv_w, m_sc_out_w, m_final_norm_w, v_ada_w, v_ada_b, v_mix_norm_w, v_mlp_norm_w, v_mlp_up, v_mlp_down, v_ssd_in_w, v_ssd_conv_w, v_ssd_conv_b, v_ssd_dt_bias, v_ssd_A_log, v_ssd_D, v_ssd_norm_w, v_ssd_out_w, v_sc_in_w, v_sc_conv_w, v_sc_out_w, v_final_norm_w):
    xi, yi, ci = lax.axis_index("x"), lax.axis_index("y"), lax.axis_index("c")
    chip = 2 * xi + yi
    dev = 2 * chip + ci
    n_ada = ada_w.shape[2]

    conv_flat = jnp.concatenate([ssd_conv_w.reshape(-1), sc_conv_w.reshape(-1), jnp.zeros((256,), F32)]).reshape(4, D)
    blk0 = jnp.concatenate([c, conv_flat, jnp.zeros((3, D), F32)], axis=0)
    got0 = _all_gather_rows(blk0, name="gather_cond").reshape(N_DEV, 8, D)
    c_all = got0[:, 0]
    conv_all = got0[0::2, 1:5].reshape(N_CHIPS, 4 * D)
    ssd_conv = jnp.moveaxis(conv_all[:, :4 * 768].reshape(N_CHIPS, 4, 768), 0, 1).reshape(4, CONVD)
    sc_conv = jnp.moveaxis(conv_all[:, 4 * 768:4 * 768 + 3 * 256].reshape(N_CHIPS, 3, 256), 0, 1).reshape(3, D)
    mod_shard = [_matmul(c_all, ada_w, n=n_ada, a_silu=True, b_spec=pl.BlockSpec((None, D, 512), lambda mi, j, i=i: (i, 0, j)),
                         extras=(lax.dynamic_slice(ada_b, (i, chip * n_ada), (1, n_ada)),),
                         epi=lambda acc, b: (acc + b,), name=f"ada_mod{i}") for i in range(2)]
    mod_slot = lax.dynamic_update_slice(jnp.zeros((N_CHIPS, 2 * N_DEV, n_ada), F32), jnp.concatenate(mod_shard, axis=0)[None],
                                        (chip, 0, 0))

    up_row, down_row = 0, D
    chip1 = chip.reshape(1).astype(jnp.int32)
    a_bufs = [mod_slot, _own_slot([(jnp.swapaxes(ssd_in_w, 1, 2), 0)], chip1, name="slot_ssd_in")]
    fly_a, tok = _gather_start(a_bufs, name="gather_a_start")
    b_bufs = [_own_slot([(ssd_out_w, 0)], chip1, name="slot_ssd_out", after=(tok,)),
              _own_slot([(mlp_up, 0), (mlp_down, 0)], chip1, name="slot_mlp0_", after=(tok,))]
    fly_b, tok = _gather_start(b_bufs, name="gather_b_start", after=(tok,))
    c_bufs = [_own_slot([(sc_in_w, 0)], chip1, name="slot_sc_in", after=(tok,)),
              _own_slot([(sc_out_w, 0)], chip1, name="slot_sc_out", after=(tok,))]
    fly_c, tok = _gather_start(c_bufs, name="gather_c_start", after=(tok,))
    d_bufs = [_own_slot([(mlp_up, 1), (mlp_down, 1)], chip1, name="slot_mlp1_", after=(tok,))]
    fly_d, tok = _gather_start(d_bufs, name="gather_d_start", after=(tok,))

    row = lambda v: v.reshape(1, -1)
    xs, tgt = x[0], loss_target[0]
    prm = jnp.pad(jnp.concatenate([ssd_dt_bias, ssd_A_log, ssd_D, jnp.zeros((5, NH), F32)], axis=0), ((0, 0), (0, LANES - NH)))
    mix_nw = [row(mix_norm_w[i]) for i in range(2)]
    mlp_nw = [row(mlp_norm_w[i]) for i in range(2)]
    a_bufs = _gather_wait_first(fly_a, name="gather_a_landed", after=(tok,))
    mod_all, w_ssd_in = _gather_wait_forward(_gather_forward(a_bufs, name="gather_a_pass"), name="gather_a_done")
    mod = lax.dynamic_index_in_dim(mod_all.reshape(N_CHIPS, 2, N_DEV, n_ada), dev, axis=2, keepdims=False)
    mod = jnp.moveaxis(mod, 0, 1).reshape(2, 6, D)
    mods = [[mod[i, j:j + 1] for j in range(6)] for i in range(2)]
    w_in_t = w_ssd_in.reshape(N_CHIPS * SSD_IN_SHARD, D)
    w_dt_t = jnp.pad(w_in_t[ZX:], ((0, LANES - NH), (0, 0)))
    scan = _ssd_fwd_scan(xs, mods[0][0:3], mix_nw[0], w_in_t, w_dt_t, ssd_conv, ssd_conv_b, prm, "ssd")

    def land(flight, tag, after):
        return _gather_forward(_gather_wait_first(flight, name=f"gather_{tag}_landed", after=(after,)), name=f"gather_{tag}_pass")

    passed, got = {"b": land(fly_b, "b", scan[4])}, {}

    def done(tag, after):
        got[tag] = _gather_wait_forward(passed[tag], name=f"gather_{tag}_done", after=(after,))
        return got[tag]

    x1, s_ssd = _ssd_fwd_out(xs, mods[0][0:3], scan, ssd_norm_w, lambda yn: done("b", yn)[0], "ssd")
    w_ssd_out, w_b = got["b"]
    x2, s_mlp0 = _mlp_fwd(x1, mods[0][3:6], mlp_nw[0], w_b, up_row, down_row, "mlp0",
                          midway=lambda a: passed.update(c=land(fly_c, "c", a)))
    w_sc_in, w_sc_out = done("c", x2)
    x3, s_sc = _sc_layer_fwd(x2, mods[1][0:3], mix_nw[1], w_sc_in, sc_conv, w_sc_out, 0, "sc",
                             midway=lambda proj: passed.update(d=land(fly_d, "d", proj)))
    (w_mlp1,) = done("d", x3)
    x4, s_mlp1 = _mlp_fwd(x3, mods[1][3:6], mlp_nw[1], w_mlp1, up_row, down_row, "mlp1")

    core = ci.reshape(1).astype(jnp.int32)
    chip_core = jnp.stack([chip, ci]).astype(jnp.int32)

    def reduce_swap(gbufs, tag, after=()):
        return _sibling_start(gbufs, name=tag + "_sibling_start", after=after)

    def reduce_send(flight, tag, after):
        gs, sib = _sibling_wait(flight, name=tag + "_sibling_landed", after=after)
        hs = [_add_sibling_half(g, s, core, name=f"{tag}_add_sibling{b}") for b, (g, s) in enumerate(zip(gs, sib))]
        return _owners_start(hs, name=tag + "_owners_start")

    def reduce_sum(flight, tag, after):
        hs, lands = _owners_wait(flight, name=tag + "_owners_landed", after=after)
        ts = [_add_chip_sums(h, o, chip_core, name=f"{tag}_add_chips{b}") for b, (h, o) in enumerate(zip(hs, lands))]
        return _result_start(ts, name=tag + "_result_start")

    def reduce_done(flight, tag, after=()):
        return [t.reshape(-1, t.shape[2]) for t in _result_wait(flight, name=tag + "_result_landed", after=after)]

    dx4, fsum, dy, gs = _final_loss(x4, row(final_norm_w), tgt, (mods[1][5], s_mlp1[3]), name="final_loss")
    dx3, g_mlp1, sum_mlp1, dy, gs = _mlp_bwd(dx4, dy, gs, s_mlp1, mods[1][3:6], mlp_nw[1], w_mlp1, None, up_row, down_row,
                                             (mods[1][2], s_sc[4]), "mlp1")
    dx2, g_sc_out, g_sc_in, sum_sc, sc_csum, dy, gs = _sc_layer_bwd(dx3, dy, gs, s_sc, mods[1][0:3], mix_nw[1], w_sc_in,
                                                                    sc_conv, w_sc_out, None, 0, (mods[0][5], s_mlp0[3]), "sc")
    dx1, g_b, sum_mlp0, dy, gsum_ssd = _mlp_bwd(dx2, dy, gs, s_mlp0, mods[0][3:6], mlp_nw[0], w_b, None, up_row, down_row,
                                                (mods[0][2], s_ssd[8]), "mlp0")
    fly_1, tok = reduce_swap([g_mlp1, g_sc_out, g_sc_in, g_b], "rs1")
    dyn, g_ssd_out = _ssd_bwd_out(dy, s_ssd, w_ssd_out, "ssd", tok)
    fly_1, tok = reduce_send(fly_1, "rs1", (g_ssd_out,))
    dy, dzx, gnsum = _gnorm_bwd(s_ssd[5], s_ssd[2], ssd_norm_w + tok[0:1, 0:1], dyn, name="ssd_dgnorm")
    grad_x, d_w_zx, d_w_dt, sum_ssd, csum, ssum = _ssd_bwd_rest(
        dx1, dy, dzx, gsum_ssd, s_ssd, mods[0][0:3], mix_nw[0], w_in_t, w_dt_t, ssd_conv, prm, "ssd")

    def ssd_in_owner(k):
        lo, hi = k * SSD_IN_SHARD, (k + 1) * SSD_IN_SHARD
        if hi <= ZX:
            return d_w_zx[:, lo:hi]
        return jnp.concatenate([d_w_zx[:, lo:], d_w_dt[:, :hi - ZX].astype(BF16)], axis=1)

    small = jnp.concatenate([sum_ssd, sum_mlp0, sum_sc, sum_mlp1, csum.reshape(24, D)[0:16], gnsum.reshape(16, D)[0:8],
                             fsum, sc_csum, jnp.pad(ssum, ((0, 0), (0, D - LANES)))], axis=0)
    small_slot = lax.dynamic_update_slice(jnp.zeros((N_CHIPS, 2 * SMALL_ROWS, D), F32), small[None], (chip, ci * SMALL_ROWS, 0))
    fly_2, tok = reduce_swap([jnp.stack([ssd_in_owner(k) for k in range(N_CHIPS)]), g_ssd_out], "rs2")
    fly_s, tok = _gather_start([small_slot], name="gather_small_start", after=(tok,))
    fly_1, tok = reduce_sum(fly_1, "rs1", (grad_x, tok))
    fly_2, tok = reduce_send(fly_2, "rs2", (tok,))
    fly_s = _gather_forward(_gather_wait_first(fly_s, name="gather_small_landed", after=(tok,)), name="gather_small_pass")
    (small_all,) = _gather_wait_forward(fly_s, name="gather_small_done")
    t_mlp1, t_sc_out, t_sc_in, t_b = reduce_done(fly_1, "rs1", (small_all,))
    small_all = small_all.reshape(N_DEV, SMALL_ROWS, D)
    mod_rows = [r + o for r in SUB_ROW for o in (3, 2, 0)]
    c_pad = jnp.concatenate([c_all, jnp.zeros((8, D), F32)], axis=0)
    dmod_all = jnp.stack([small_all[:, r] for r in mod_rows], axis=1).reshape(N_DEV, 2, 6 * D)
    g_ada_w = []
    for i in range(2):
        dm = lax.dynamic_slice(dmod_all[:, i], (0, chip * n_ada), (N_DEV, n_ada))
        g_ada_w.append(_matmul_tn(c_pad, jnp.concatenate([dm, jnp.zeros_like(dm)], axis=0), m=D, n=n_ada, a_silu=True,
                                  name=f"ada_dw{i}"))

    big = dict(ada_w=[(g, 0) for g in g_ada_w], mlp_up=[(t_b, up_row), (t_mlp1, up_row)],
               mlp_down=[(t_b, down_row), (t_mlp1, down_row)], ssd_out_w=None, sc_out_w=[(t_sc_out, 0)],
               sc_in_w=[(t_sc_in, 0)], ssd_in_w=None)
    weights = dict(ada_w=(ada_w, m_ada_w, v_ada_w), ada_b=(ada_b, m_ada_b, v_ada_b),
                   mix_norm_w=(mix_norm_w, m_mix_norm_w, v_mix_norm_w), mlp_norm_w=(mlp_norm_w, m_mlp_norm_w, v_mlp_norm_w),
                   mlp_up=(mlp_up, m_mlp_up, v_mlp_up), mlp_down=(mlp_down, m_mlp_down, v_mlp_down),
                   ssd_in_w=(ssd_in_w, m_ssd_in_w, v_ssd_in_w), ssd_conv_w=(ssd_conv_w, m_ssd_conv_w, v_ssd_conv_w),
                   ssd_conv_b=(ssd_conv_b, m_ssd_conv_b, v_ssd_conv_b), ssd_dt_bias=(ssd_dt_bias, m_ssd_dt_bias, v_ssd_dt_bias),
                   ssd_A_log=(ssd_A_log, m_ssd_A_log, v_ssd_A_log), ssd_D=(ssd_D, m_ssd_D, v_ssd_D),
                   ssd_norm_w=(ssd_norm_w, m_ssd_norm_w, v_ssd_norm_w), ssd_out_w=(ssd_out_w, m_ssd_out_w, v_ssd_out_w),
                   sc_in_w=(sc_in_w, m_sc_in_w, v_sc_in_w), sc_conv_w=(sc_conv_w, m_sc_conv_w, v_sc_conv_w),
                   sc_out_w=(sc_out_w, m_sc_out_w, v_sc_out_w), final_norm_w=(final_norm_w, m_final_norm_w, v_final_norm_w))
    def step(nm, parts):
        w, m, v = (t if t.shape[0] == 1 else t.reshape(-1, t.shape[-1]) for t in weights[nm])
        if len(parts) > 1:
            return _adamw_parts(w, parts, m, v, name="adamw_" + nm)
        rows, outs = w.shape[-2] // len(parts), None
        for i, (gbuf, g_row) in enumerate(parts):
            outs = _adamw(w, gbuf, m, v, g_row=g_row, w_row=i * rows, rows=rows, into=outs, emit_g=True, name=f"adamw_{nm}{i}")
        return outs

    res = {nm: step(nm, parts) for nm, parts in big.items() if parts is not None}
    fly_2, tok = reduce_sum(fly_2, "rs2", tuple(r[1] for r in res.values()))
    tot = _sum_devices(small_all, name="sum_small", after=(tok,))
    loss = tot[FINAL_ROW + 1, 0]
    conv_sums = tot[SSD_CONV_ROW:SSD_CONV_ROW + 15].reshape(5, CONVD)
    grads = dict(ada_b=jnp.stack([tot[r] for r in mod_rows]).reshape(2, 6 * D),
                 mix_norm_w=jnp.stack([tot[SUB_ROW[0] + 1], tot[SUB_ROW[2] + 1]]),
                 mlp_norm_w=jnp.stack([tot[SUB_ROW[1] + 1], tot[SUB_ROW[3] + 1]]),
                 ssd_conv_w=lax.dynamic_slice(conv_sums, (0, chip * 768), (4, 768))[None], ssd_conv_b=conv_sums[4:5],
                 ssd_dt_bias=tot[HEAD_ROW + 2:HEAD_ROW + 3, 0:NH], ssd_A_log=tot[HEAD_ROW:HEAD_ROW + 1, 0:NH],
                 ssd_D=tot[HEAD_ROW + 1:HEAD_ROW + 2, 0:NH], ssd_norm_w=tot[GNORM_ROW:GNORM_ROW + 2].reshape(1, DI),
                 sc_conv_w=lax.dynamic_slice(tot[SC_CONV_ROW:SC_CONV_ROW + 3], (0, chip * 256), (3, 256))[None],
                 final_norm_w=tot[FINAL_ROW])
    for nm, g in grads.items():
        w, m, v = weights[nm]
        two_d = (-1, w.shape[-1]) if w.ndim > 1 else (1, -1)
        res[nm] = (g, *_adamw(w.reshape(two_d), g.reshape(two_d), m.reshape(two_d), v.reshape(two_d), name="adamw_" + nm))
    t_ssd_in, t_ssd_out = reduce_done(fly_2, "rs2", tuple(res[nm][1] for nm in grads))
    res["ssd_out_w"] = step("ssd_out_w", [(t_ssd_out, 0)])
    w_t, m_t, v_t = (jnp.swapaxes(t[0], 0, 1) for t in weights["ssd_in_w"])
    res["ssd_in_w"] = [jnp.swapaxes(o, 0, 1) for o in _adamw(w_t, t_ssd_in.T, m_t, v_t, emit_g=True, name="adamw_ssd_in_w")]
    outs = [[res[nm][k].reshape(weights[nm][0].shape) for nm in weights] for k in range(4)]
    return (loss, grad_x[None], *outs[0], *outs[1], *outs[2], *outs[3])
```

```python
import jax
import jax.numpy as jnp
from jax import lax
from jax.experimental import pallas as pl
from jax.experimental.pallas import tpu as pltpu

F32 = jnp.float32
BF16 = jnp.bfloat16
MESH = pl.DeviceIdType.MESH

D = 1024
DFF = 4096
DI = 2048
NH = 32
HP = 64
NG = 4
NS = 128
CH = 128
CONVD = DI + 2 * NG * NS
ZX = DI + CONVD
GW = NG * NS
LANES = 128
N_CHIPS = 4
N_DEV = 8
EPS = 1e-5
ADAM_LR, ADAM_B1, ADAM_B2, ADAM_EPS, ADAM_WD, ADAM_STEP = 1e-3, 0.9, 0.999, 1e-8, 0.01, 10
VMEM_LIMIT = 48 * 1024 * 1024
TM_ALL = 2048
TM_HALF = 1024
ANY = pl.BlockSpec(memory_space=pl.ANY)
SEM = pl.BlockSpec(memory_space=pltpu.SEMAPHORE)

SSD_IN_SHARD = 1288
SC_IN_SHARD = 768


def _params(sem=None):
    return pltpu.CompilerParams(dimension_semantics=sem, vmem_limit_bytes=VMEM_LIMIT)


def _sigmoid(v):
    return 0.5 * jnp.tanh(0.5 * v) + 0.5


def _dot(a, b, dims=((1,), (0,)), precision=None):
    return lax.dot_general(a, b, (dims, ((), ())), preferred_element_type=F32, precision=precision)


def _dot_nt(a, b):
    return _dot(a, b, ((1,), (1,)))


def _dot_tn(a, b):
    return _dot(a, b, ((0,), (0,)))


def _nn(av, bv):
    return _dot(av.astype(BF16), bv.astype(BF16))


def _nt(av, bv):
    return _dot_nt(av.astype(BF16), bv.astype(BF16))


def _nn_split(av, bv):
    return _dot(av.astype(BF16), bv.reshape(-1, bv.shape[2]))


def _nn_split_sq(av, bv):
    return _nn_split(av * av, bv)


def _nt_split(av, bv):
    kc = bv.shape[2]
    acc = _dot_nt(av[:, 0:kc].astype(BF16), bv[0])
    for s in range(1, bv.shape[0]):
        acc = acc + _dot_nt(av[:, s * kc:(s + 1) * kc].astype(BF16), bv[s])
    return acc


def _nt_sc_in(av, bv):
    q = 256
    acc = None
    for i in range(3 * D // q):
        a_blk = av[i // 4][:, (i % 4) * q:(i % 4 + 1) * q]
        b_blk = bv[i // 3][:, (i % 3) * q:(i % 3 + 1) * q]
        t = _dot_nt(a_blk, b_blk)
        acc = t if acc is None else acc + t
    return acc


def _matmul(a, b, *, name, n, contract=_nn, a_spec=None, b_spec=None, tm=512, tn=512, extras=(), epi=None,
            out_dtypes=(F32,), a_silu=False):
    M = a.shape[-2]
    tm, tn = min(tm, M), min(tn, n)
    assert M % tm == 0 and n % tn == 0, (name, M, n, tm, tn)
    n_ex = len(extras)
    if a_spec is None:
        a_spec = pl.BlockSpec((tm, a.shape[1]), lambda i, j: (i, 0))
    if b_spec is None:
        b_spec = (pl.BlockSpec((tn, b.shape[1]), lambda i, j: (j, 0)) if contract is _nt
                  else pl.BlockSpec((b.shape[0], tn), lambda i, j: (0, j)))

    def body(*refs):
        av = refs[0][...]
        if a_silu:
            av = av * _sigmoid(av)
        acc = contract(av, refs[1][...])
        res = epi(acc, *[r[...] for r in refs[2:2 + n_ex]]) if epi is not None else (acc,)
        for o_ref, r in zip(refs[2 + n_ex:], res, strict=True):
            o_ref[...] = r.astype(o_ref.dtype)

    in_specs = [a_spec, b_spec]
    for e in extras:
        in_specs.append(pl.BlockSpec((1, tn), lambda i, j: (0, j)) if e.shape[0] == 1 and M != 1
                        else pl.BlockSpec((tm, tn), lambda i, j: (i, j)))
    outs = pl.pallas_call(
        body, grid=(M // tm, n // tn), in_specs=in_specs,
        out_specs=[pl.BlockSpec((tm, tn), lambda i, j: (i, j)) for _ in out_dtypes],
        out_shape=[jax.ShapeDtypeStruct((M, n), dt) for dt in out_dtypes],
        compiler_params=_params(("parallel", "parallel")), name=name)(a, b, *extras)
    return outs if len(out_dtypes) > 1 else outs[0]


def _matmul_tn(a, b, *, name, m, n, tm=512, tn=512, a_spec=None, b_spec=None, out_spec=None, out_struct=None, into=None,
               a_silu=False, a_square=False):
    T = a.shape[-2]
    tm, tn = min(tm, m), min(tn, n)
    assert m % tm == 0 and n % tn == 0, (name, m, n, tm, tn)
    if a_spec is None:
        a_spec = pl.BlockSpec((T, tm), lambda i, j: (0, i))
    if b_spec is None:
        b_spec = pl.BlockSpec((T, tn), lambda i, j: (0, j))
    if out_spec is None:
        out_spec, out_struct = pl.BlockSpec((tm, tn), lambda i, j: (i, j)), jax.ShapeDtypeStruct((m, n), F32)

    def body(a_ref, b_ref, *rest):
        av = a_ref[...]
        if a_silu:
            av = av * _sigmoid(av)
        if a_square:
            av = av * av
        rest[-1][...] = _dot_tn(av.astype(BF16), b_ref[...].astype(BF16)).astype(rest[-1].dtype)

    args, in_specs, alias = [a, b], [a_spec, b_spec], {}
    if into is not None:
        args, in_specs, alias = args + [into], in_specs + [ANY], {2: 0}
    return pl.pallas_call(body, grid=(m // tm, n // tn), in_specs=in_specs, out_specs=out_spec, out_shape=out_struct,
                          input_output_aliases=alias, compiler_params=_params(("parallel", "parallel")), name=name)(*args)


def _modnorm_fwd(x, nw, sc, sh, *, name):
    L = x.shape[0]
    tm = min(L, 512)

    def body(x_ref, nw_ref, sc_ref, sh_ref, h_ref):
        xv = x_ref[...]
        r = lax.rsqrt(jnp.mean(xv * xv, axis=-1, keepdims=True) + EPS)
        h_ref[...] = ((xv * r * nw_ref[...]) * (1.0 + sc_ref[...]) + sh_ref[...]).astype(BF16)

    row = pl.BlockSpec((tm, D), lambda i: (i, 0))
    vec = pl.BlockSpec((1, D), lambda i: (0, 0))
    return pl.pallas_call(body, grid=(L // tm,), in_specs=[row, vec, vec, vec], out_specs=row,
                          out_shape=jax.ShapeDtypeStruct((L, D), BF16),
                          compiler_params=_params(("parallel",)), name=name)(x, nw, sc, sh)


def _gate_outputs(dx, below_refs, dy_ref, gs_ref):
    g_ref, y_ref = below_refs
    dy_ref[...] = (dx * g_ref[...]).astype(BF16)
    gs_ref[0:1, :] += jnp.sum(dx * y_ref[...].astype(F32), axis=0, keepdims=True)


def _modnorm_bwd(x, dh, dxo, nw, sc, gsum, below, *, name):
    L = x.shape[0]
    tm = min(L, 256)
    nb = 0 if below is None else 2

    def body(x_ref, dh_ref, dxo_ref, nw_ref, sc_ref, g_ref, *rest):
        dx_ref, s_ref = rest[nb:nb + 2]

        @pl.when(pl.program_id(0) == 0)
        def _():
            s_ref[...] = g_ref[...]
            if nb:
                rest[-1][...] = jnp.zeros_like(rest[-1])

        xv, dhv = x_ref[...], dh_ref[...].astype(F32)
        r = lax.rsqrt(jnp.mean(xv * xv, axis=-1, keepdims=True) + EPS)
        xhat = xv * r
        dxhat = dhv * (nw_ref[...] * (1.0 + sc_ref[...]))
        dx = dxo_ref[...] + r * (dxhat - xhat * jnp.mean(dxhat * xhat, axis=-1, keepdims=True))
        dx_ref[...] = dx
        s_ref[1:2, :] += jnp.sum(dhv * xhat, axis=0, keepdims=True) * (1.0 + sc_ref[...])
        s_ref[2:3, :] += jnp.sum(dhv * xhat, axis=0, keepdims=True) * nw_ref[...]
        s_ref[3:4, :] += jnp.sum(dhv, axis=0, keepdims=True)
        if nb:
            _gate_outputs(dx, rest[:nb], rest[-2], rest[-1])

    row = pl.BlockSpec((tm, D), lambda i: (i, 0))
    vec = pl.BlockSpec((1, D), lambda i: (0, 0))
    blk = pl.BlockSpec((8, D), lambda i: (0, 0))
    in_specs, out_specs = [row, row, row, vec, vec, blk], [row, blk]
    out_shape = [jax.ShapeDtypeStruct((L, D), F32), jax.ShapeDtypeStruct((8, D), F32)]
    if nb:
        in_specs, out_specs = in_specs + [vec, row], out_specs + [row, blk]
        out_shape += [jax.ShapeDtypeStruct((L, D), BF16), jax.ShapeDtypeStruct((8, D), F32)]
    return pl.pallas_call(body, grid=(L // tm,), in_specs=in_specs, out_specs=out_specs, out_shape=out_shape,
                          compiler_params=_params(("arbitrary",)), name=name)(x, dh, dxo, nw, sc, gsum, *(below or ()))


def _final_loss(x, fw, tgt, below, *, name):
    L = x.shape[0]
    tm = min(L, 256)

    def body(x_ref, fw_ref, t_ref, g_ref, y_ref, dx_ref, s_ref, dy_ref, gs_ref):
        @pl.when(pl.program_id(0) == 0)
        def _():
            s_ref[...] = jnp.zeros_like(s_ref)
            gs_ref[...] = jnp.zeros_like(gs_ref)

        xv = x_ref[...]
        r = lax.rsqrt(jnp.mean(xv * xv, axis=-1, keepdims=True) + EPS)
        xhat = xv * r
        diff = xhat * fw_ref[...] - t_ref[...]
        dout = diff * (1.0 / D)
        dxhat = dout * fw_ref[...]
        dx = r * (dxhat - xhat * jnp.mean(dxhat * xhat, axis=-1, keepdims=True))
        dx_ref[...] = dx
        s_ref[0:1, :] += jnp.sum(dout * xhat, axis=0, keepdims=True)
        s_ref[1:2, :] += jnp.zeros((1, D), F32) + 0.5 * jnp.sum(jnp.sum(diff * diff, axis=-1, keepdims=True) * (1.0 / D))
        _gate_outputs(dx, (g_ref, y_ref), dy_ref, gs_ref)

    row = pl.BlockSpec((tm, D), lambda i: (i, 0))
    vec = pl.BlockSpec((1, D), lambda i: (0, 0))
    blk = pl.BlockSpec((8, D), lambda i: (0, 0))
    return pl.pallas_call(body, grid=(L // tm,), in_specs=[row, vec, row, vec, row], out_specs=[row, blk, row, blk],
                          out_shape=[jax.ShapeDtypeStruct((L, D), F32), jax.ShapeDtypeStruct((8, D), F32),
                                     jax.ShapeDtypeStruct((L, D), BF16), jax.ShapeDtypeStruct((8, D), F32)],
                          compiler_params=_params(("arbitrary",)), name=name)(x, fw, tgt, *below)


def _shift_down(v, j):
    if j == 0:
        return v
    rolled = pltpu.roll(v, j, 0)
    row = lax.broadcasted_iota(jnp.int32, (8, v.shape[1]), 0)
    return jnp.concatenate([jnp.where(row >= j, rolled[0:8], 0.0), rolled[8:]], axis=0)


def _shift_up(v, j):
    if j == 0:
        return v
    n = v.shape[0]
    rolled = pltpu.roll(v, n - j, 0)
    row = lax.broadcasted_iota(jnp.int32, (8, v.shape[1]), 0)
    return jnp.concatenate([rolled[:n - 8], jnp.where(row < 8 - j, rolled[n - 8:], 0.0)], axis=0)


def _ssd_conv_fwd(zx, w, b, *, name):
    L = zx.shape[0]
    cb = 256
    k = w.shape[0]

    def body(x_ref, w_ref, b_ref, o_ref, p_ref):
        xv = x_ref[...].astype(F32)
        pre = b_ref[...] + xv * w_ref[k - 1:k, :]
        for j in range(1, k):
            pre = pre + _shift_down(xv, j) * w_ref[k - 1 - j:k - j, :]
        o_ref[...] = (pre * _sigmoid(pre)).astype(BF16)
        p_ref[...] = pre.astype(BF16)

    blk = pl.BlockSpec((L, cb), lambda i: (0, i))
    return pl.pallas_call(
        body, grid=(CONVD // cb,),
        in_specs=[pl.BlockSpec((L, cb), lambda i: (0, i + DI // cb)), pl.BlockSpec((k, cb), lambda i: (0, i)),
                  pl.BlockSpec((1, cb), lambda i: (0, i))],
        out_specs=[blk, blk], out_shape=[jax.ShapeDtypeStruct((L, CONVD), BF16)] * 2,
        compiler_params=_params(("parallel",)), name=name)(zx, w, b)


def _ssd_conv_bwd(zx, pre, dact, w, dzx, *, name):
    L = zx.shape[0]
    cb = 256
    k = w.shape[0]

    def body(x_ref, p_ref, da_ref, w_ref, _, dx_ref, s_ref):
        xv, pv = x_ref[...].astype(F32), p_ref[...].astype(F32)
        s = _sigmoid(pv)
        dpre = da_ref[...].astype(F32) * (s * (1.0 + pv * (1.0 - s)))
        s_ref[...] = jnp.zeros_like(s_ref)
        s_ref[k:k + 1, :] = jnp.sum(dpre, axis=0, keepdims=True)
        s_ref[k - 1:k, :] = jnp.sum(dpre * xv, axis=0, keepdims=True)
        dx = dpre * w_ref[k - 1:k, :]
        for j in range(1, k):
            later = _shift_up(dpre, j)
            dx = dx + later * w_ref[k - 1 - j:k - j, :]
            s_ref[k - 1 - j:k - j, :] = jnp.sum(later * xv, axis=0, keepdims=True)
        dx_ref[...] = dx.astype(BF16)

    blk = pl.BlockSpec((L, cb), lambda i: (0, i))
    return pl.pallas_call(
        body, grid=(CONVD // cb,),
        in_specs=[pl.BlockSpec((L, cb), lambda i: (0, i + DI // cb)), blk, blk, pl.BlockSpec((k, cb), lambda i: (0, i)), ANY],
        out_specs=[pl.BlockSpec((L, cb), lambda i: (0, i + DI // cb)), pl.BlockSpec((8, cb), lambda i: (0, i))],
        out_shape=[jax.ShapeDtypeStruct((L, ZX), BF16), jax.ShapeDtypeStruct((8, CONVD), F32)],
        input_output_aliases={4: 0}, compiler_params=_params(("parallel",)), name=name)(zx, pre, dact, w, dzx)


def _sc_fwd(proj, w, *, name):
    L = proj.shape[0]
    cb = 256
    nb = D // cb
    k = w.shape[0]

    def body(b_ref, c_ref, x_ref, w_ref, o_ref, v_ref):
        u = c_ref[...].astype(F32) * x_ref[...].astype(F32)
        v = u * w_ref[k - 1:k, :]
        for j in range(1, k):
            v = v + _shift_down(u, j) * w_ref[k - 1 - j:k - j, :]
        o_ref[...] = (b_ref[...].astype(F32) * v).astype(BF16)
        v_ref[...] = v.astype(BF16)

    blk = pl.BlockSpec((L, cb), lambda i: (0, i))
    return pl.pallas_call(
        body, grid=(nb,),
        in_specs=[blk, pl.BlockSpec((L, cb), lambda i: (0, i + nb)), pl.BlockSpec((L, cb), lambda i: (0, i + 2 * nb)),
                  pl.BlockSpec((k, cb), lambda i: (0, i))],
        out_specs=[blk, blk], out_shape=[jax.ShapeDtypeStruct((L, D), BF16)] * 2,
        compiler_params=_params(("parallel",)), name=name)(proj, proj, proj, w)


def _sc_bwd(proj, v, dyv, w, *, name):
    L = proj.shape[0]
    cb = 256
    nb = D // cb
    k = w.shape[0]

    def body(b_ref, c_ref, x_ref, v_ref, dy_ref, w_ref, dp_ref, s_ref):
        cv, xv = c_ref[...].astype(F32), x_ref[...].astype(F32)
        u = cv * xv
        dyv_ = dy_ref[...].astype(F32)
        dp_ref[0] = (dyv_ * v_ref[...].astype(F32)).astype(BF16)
        dv = dyv_ * b_ref[...].astype(F32)
        s_ref[...] = jnp.zeros_like(s_ref)
        s_ref[k - 1:k, :] = jnp.sum(dv * u, axis=0, keepdims=True)
        du = dv * w_ref[k - 1:k, :]
        for j in range(1, k):
            later = _shift_up(dv, j)
            du = du + later * w_ref[k - 1 - j:k - j, :]
            s_ref[k - 1 - j:k - j, :] = jnp.sum(later * u, axis=0, keepdims=True)
        dp_ref[1] = (du * xv).astype(BF16)
        dp_ref[2] = (du * cv).astype(BF16)

    blk = pl.BlockSpec((L, cb), lambda i: (0, i))
    return pl.pallas_call(
        body, grid=(nb,),
        in_specs=[blk, pl.BlockSpec((L, cb), lambda i: (0, i + nb)), pl.BlockSpec((L, cb), lambda i: (0, i + 2 * nb)),
                  blk, blk, pl.BlockSpec((k, cb), lambda i: (0, i))],
        out_specs=[pl.BlockSpec((3, L, cb), lambda i: (0, 0, i)), pl.BlockSpec((8, cb), lambda i: (0, i))],
        out_shape=[jax.ShapeDtypeStruct((3, L, D), BF16), jax.ShapeDtypeStruct((8, D), F32)],
        compiler_params=_params(("parallel",)), name=name)(proj, proj, proj, v, dyv, w)


def _pieces(v, n):
    out, rest = [], v
    for _ in range(n):
        out.append(rest.astype(BF16))
        rest = rest - out[-1].astype(F32)
    return out


def _cumsum_rows(mask, v):
    m = mask.astype(BF16)
    return _dot(jnp.concatenate([m, m, m], axis=1), jnp.concatenate(_pieces(v, 3), axis=0))


def _ssd_chunk_terms(dtr, prm):
    lane = lax.broadcasted_iota(jnp.int32, (CH, LANES), 1)
    valid = lane < NH
    xdt = dtr + prm[0:1, :]
    dt = jnp.where(valid, jnp.maximum(xdt, 0.0) + jnp.log1p(jnp.exp(-jnp.abs(xdt))), 0.0)
    A = -jnp.exp(prm[1:2, :])
    ri = lax.broadcasted_iota(jnp.int32, (CH, CH), 0)
    ci = lax.broadcasted_iota(jnp.int32, (CH, CH), 1)
    cs = _cumsum_rows(ri >= ci, dt * A)
    last = cs[CH - 1:CH, :]
    spread = (lax.broadcasted_iota(jnp.int32, (2 * LANES, DI), 1) // HP
              == lax.broadcasted_iota(jnp.int32, (2 * LANES, DI), 0) % LANES).astype(BF16)
    gather = ((lax.broadcasted_iota(jnp.int32, (LANES, 2 * DI), 1) % DI) // HP
              == lax.broadcasted_iota(jnp.int32, (LANES, 2 * DI), 0)).astype(BF16)
    return dict(valid=valid, xdt=xdt, dt=dt, A=A, cs=cs, csT=cs.T, last=last, ri=ri, ci=ci, ex=(spread, gather))


def _expand(v, ex):
    if v.shape[0] == 1:
        return _expand(jnp.broadcast_to(v, (8, LANES)), ex)[0:1, :]
    return _dot(jnp.concatenate(_pieces(v, 2), axis=1), ex[0])


def _head_sum(v, ex):
    if v.shape[0] == 1:
        return _head_sum(jnp.broadcast_to(v, (8, DI)), ex)[0:1, :]
    return _dot_nt(jnp.concatenate(_pieces(v, 2), axis=1), ex[1])


def _ssd_fwd(xbc, dtr, prm, *, name):
    L = xbc.shape[0]
    nc = L // CH

    def body(xbc_ref, dtr_ref, prm_ref, y_ref, sp_ref, st_ref):
        @pl.when(pl.program_id(0) == 0)
        def _():
            st_ref[...] = jnp.zeros_like(st_ref)

        prm_v = prm_ref[...]
        t = _ssd_chunk_terms(dtr_ref[...], prm_v)
        cs, csT, ex, causal = t["cs"], t["csT"], t["ex"], t["ri"] >= t["ci"]
        xs = xbc_ref[:, 0:DI].astype(F32)
        X = xs * _expand(t["dt"], ex)
        Xb = X.astype(BF16)
        Xd = (X * _expand(jnp.exp(t["last"] - cs), ex)).astype(BF16)
        Ex = _expand(jnp.exp(cs), ex)
        cdx = _expand(jnp.exp(t["last"]), ex)
        dskx = _expand(prm_v[2:3, :], ex)
        lane = lax.broadcasted_iota(jnp.int32, (CH, LANES), 1)
        sp_ref[0] = st_ref[...]
        for g in range(NG):
            Bg = xbc_ref[:, DI + g * NS:DI + (g + 1) * NS].astype(BF16)
            Cg = xbc_ref[:, DI + GW + g * NS:DI + GW + (g + 1) * NS].astype(BF16)
            G = _dot_nt(Cg, Bg)
            Sg = st_ref[:, g * GW:(g + 1) * GW]
            yoff = _dot(Cg, Sg.astype(BF16)) * Ex[:, g * GW:(g + 1) * GW]
            for j in range(GW // LANES):
                lo = g * GW + j * LANES
                Xp = Xb[:, lo:lo + LANES]
                yd = []
                for h in (lo // HP, lo // HP + 1):
                    seg = cs[:, h:h + 1] - csT[h:h + 1, :]
                    yd.append(_dot((G * jnp.where(causal, jnp.exp(seg), 0.0)).astype(BF16), Xp))
                y_ref[:, lo:lo + LANES] = (jnp.where(lane < HP, yd[0], yd[1]) + yoff[:, j * LANES:(j + 1) * LANES]
                                           + dskx[:, lo:lo + LANES] * xs[:, lo:lo + LANES]).astype(BF16)
            st_ref[:, g * GW:(g + 1) * GW] = Sg * cdx[:, g * GW:(g + 1) * GW] + _dot_tn(Bg, Xd[:, g * GW:(g + 1) * GW])

    return pl.pallas_call(
        body, grid=(nc,),
        in_specs=[pl.BlockSpec((CH, CONVD), lambda c: (c, 0)), pl.BlockSpec((CH, LANES), lambda c: (c, 0)),
                  pl.BlockSpec((8, LANES), lambda c: (0, 0))],
        out_specs=[pl.BlockSpec((CH, DI), lambda c: (c, 0)), pl.BlockSpec((1, NS, DI), lambda c: (c, 0, 0))],
        out_shape=[jax.ShapeDtypeStruct((L, DI), BF16), jax.ShapeDtypeStruct((nc, NS, DI), F32)],
        scratch_shapes=[pltpu.VMEM((NS, DI), F32)],
        compiler_params=_params(("arbitrary",)), name=name)(xbc, dtr, prm)


def _ssd_bwd(xbc, dtr, prm, dy, sprev, *, name):
    L = xbc.shape[0]
    nc = L // CH

    def body(xbc_ref, dtr_ref, prm_ref, dy_ref, sp_ref, dxbc_ref, ddtr_ref, s_ref, dst_ref, dx_scr, de_scr, dd_scr):
        step = pl.program_id(0)

        @pl.when(step == 0)
        def _():
            dst_ref[...] = jnp.zeros_like(dst_ref)
            s_ref[...] = jnp.zeros_like(s_ref)

        prm_v = prm_ref[...]
        t = _ssd_chunk_terms(dtr_ref[...], prm_v)
        cs, csT, ex, ri, ci = t["cs"], t["csT"], t["ex"], t["ri"], t["ci"]
        E = jnp.exp(cs)
        dec = jnp.exp(t["last"] - cs)
        cd = jnp.exp(t["last"])
        xs = xbc_ref[:, 0:DI].astype(F32)
        dtx = _expand(t["dt"], ex)
        X = xs * dtx
        Xb = X.astype(BF16)
        decx = _expand(dec, ex)
        Xd = (X * decx).astype(BF16)
        Ex = _expand(E, ex)
        cdx = _expand(cd, ex)
        dskx = _expand(prm_v[2:3, :], ex)
        lane = lax.broadcasted_iota(jnp.int32, (CH, LANES), 1)
        dcs = jnp.zeros((CH, LANES), F32)
        dcd_x = []
        for g in range(NG):
            gs = slice(g * GW, (g + 1) * GW)
            Bg = xbc_ref[:, DI + g * NS:DI + (g + 1) * NS].astype(BF16)
            Cg = xbc_ref[:, DI + GW + g * NS:DI + GW + (g + 1) * NS].astype(BF16)
            G = _dot_nt(Cg, Bg)
            GT = _dot_nt(Bg, Cg)
            Sg = sp_ref[0, :, gs]
            Sgb = Sg.astype(BF16)
            dyg = dy_ref[:, gs]
            de_scr[:, gs] = dyg * _dot(Cg, Sgb)
            dYo = (Ex[:, gs] * dyg).astype(BF16)
            dC = _dot_nt(dYo, Sgb)
            dS_in = _dot_tn(Cg, dYo)
            dStg = dst_ref[:, gs]
            dStb = dStg.astype(BF16)
            dXd = _dot(Bg, dStb)
            dB = _dot_nt(Xd[:, gs], dStb)
            dd_scr[:, gs] = dXd * X[:, gs]
            dXst = dXd * decx[:, gs]
            dG = jnp.zeros((CH, CH), F32)
            dGT = jnp.zeros((CH, CH), F32)
            for j in range(GW // LANES):
                lo = g * GW + j * LANES
                Xp = Xb[:, lo:lo + LANES]
                dyp = dy_ref[:, lo:lo + LANES]
                dXp = dXst[:, j * LANES:(j + 1) * LANES]
                for k, h in enumerate((lo // HP, lo // HP + 1)):
                    dyh = jnp.where((lane < HP) if k == 0 else (lane >= HP), dyp, 0.0).astype(BF16)
                    seg = cs[:, h:h + 1] - csT[h:h + 1, :]
                    Lm = jnp.where(ri >= ci, jnp.exp(seg), 0.0)
                    LmT = jnp.where(ci >= ri, jnp.exp(-seg), 0.0)
                    dM = _dot_nt(dyh, Xp)
                    dMT = _dot_nt(Xp, dyh)
                    MT = GT * LmT
                    rs = jnp.sum(dM * (G * Lm), axis=1, keepdims=True) - jnp.sum(dMT * MT, axis=1, keepdims=True)
                    dcs = dcs + jnp.where(lane == h, rs, 0.0)
                    dG = dG + dM * Lm
                    dGT = dGT + dMT * LmT
                    dXp = dXp + _dot(MT.astype(BF16), dyh)
                dx_scr[:, lo:lo + LANES] = dXp
            dxbc_ref[:, DI + g * NS:DI + (g + 1) * NS] = (dB + _dot(dGT.astype(BF16), Cg)).astype(BF16)
            dxbc_ref[:, DI + GW + g * NS:DI + GW + (g + 1) * NS] = (dC + _dot(dG.astype(BF16), Bg)).astype(BF16)
            dcd_x.append(jnp.sum(dStg * Sg, axis=0, keepdims=True))
            dst_ref[:, gs] = dStg * cdx[:, gs] + dS_in
        dX = dx_scr[...]
        dy = dy_ref[...]
        ddec = _head_sum(dd_scr[...], ex)
        dcd = _head_sum(jnp.concatenate(dcd_x, axis=1), ex)
        dcs = dcs + _head_sum(de_scr[...], ex) * E - ddec * dec
        row = lax.broadcasted_iota(jnp.int32, (CH, LANES), 0)
        dcs = dcs + jnp.where(row == CH - 1, jnp.sum(ddec * dec, axis=0, keepdims=True) + dcd * cd, 0.0)
        da = _cumsum_rows(ci >= ri, dcs)
        ddt = da * t["A"] + _head_sum(dX * xs, ex)
        ddtr = jnp.where(t["valid"], ddt * _sigmoid(t["xdt"]), 0.0)
        ddtr_ref[...] = ddtr
        dxbc_ref[:, 0:DI] = (dX * dtx + dskx * dy).astype(BF16)
        s_ref[0:1, :] += jnp.sum(da * t["dt"], axis=0, keepdims=True)
        s_ref[1:2, :] += _head_sum(jnp.sum(dy * xs, axis=0, keepdims=True), ex)
        s_ref[2:3, :] += jnp.sum(ddtr, axis=0, keepdims=True)

        @pl.when(step == nc - 1)
        def _():
            s_ref[0:1, :] = s_ref[0:1, :] * t["A"]

    rev = lambda c: (nc - 1 - c, 0)
    return pl.pallas_call(
        body, grid=(nc,),
        in_specs=[pl.BlockSpec((CH, CONVD), rev), pl.BlockSpec((CH, LANES), rev), pl.BlockSpec((8, LANES), lambda c: (0, 0)),
                  pl.BlockSpec((CH, DI), rev), pl.BlockSpec((1, NS, DI), lambda c: (nc - 1 - c, 0, 0))],
        out_specs=[pl.BlockSpec((CH, CONVD), rev), pl.BlockSpec((CH, LANES), rev), pl.BlockSpec((8, LANES), lambda c: (0, 0))],
        out_shape=[jax.ShapeDtypeStruct((L, CONVD), BF16), jax.ShapeDtypeStruct((L, LANES), F32),
                   jax.ShapeDtypeStruct((8, LANES), F32)],
        scratch_shapes=[pltpu.VMEM((NS, DI), F32), pltpu.VMEM((CH, DI), F32), pltpu.VMEM((CH, DI), F32),
                        pltpu.VMEM((CH, DI), F32)],
        compiler_params=_params(("arbitrary",)), name=name)(xbc, dtr, prm, dy, sprev)


def _gnorm_fwd(y, zx, nw, *, name):
    L = y.shape[0]
    tm = min(L, 256)

    def body(y_ref, z_ref, nw_ref, o_ref):
        z = z_ref[...].astype(F32)
        yg = y_ref[...].astype(F32) * (z * _sigmoid(z))
        for g in range(NG):
            v = yg[:, g * GW:(g + 1) * GW]
            r = lax.rsqrt(jnp.mean(v * v, axis=-1, keepdims=True) + EPS)
            o_ref[:, g * GW:(g + 1) * GW] = (v * r * nw_ref[:, g * GW:(g + 1) * GW]).astype(BF16)

    row = pl.BlockSpec((tm, DI), lambda i: (i, 0))
    return pl.pallas_call(body, grid=(L // tm,), in_specs=[row, row, pl.BlockSpec((1, DI), lambda i: (0, 0))],
                          out_specs=row, out_shape=jax.ShapeDtypeStruct((L, DI), BF16),
                          compiler_params=_params(("parallel",)), name=name)(y, zx, nw)


def _gnorm_bwd(y, zx, nw, dyn, *, name):
    L = y.shape[0]
    tm = min(L, 256)

    def body(y_ref, z_ref, nw_ref, dyn_ref, dy_ref, dz_ref, s_ref):
        @pl.when(pl.program_id(0) == 0)
        def _():
            s_ref[...] = jnp.zeros_like(s_ref)

        z, yv = z_ref[...].astype(F32), y_ref[...].astype(F32)
        sz = _sigmoid(z)
        gate = z * sz
        dgate_dz = sz * (1.0 + z * (1.0 - sz))
        for g in range(NG):
            gs = slice(g * GW, (g + 1) * GW)
            v = yv[:, gs] * gate[:, gs]
            r = lax.rsqrt(jnp.mean(v * v, axis=-1, keepdims=True) + EPS)
            vhat = v * r
            dn = dyn_ref[:, gs].astype(F32)
            s_ref[0:1, gs] += jnp.sum(dn * vhat, axis=0, keepdims=True)
            dvhat = dn * nw_ref[:, gs]
            dv = r * (dvhat - vhat * jnp.mean(dvhat * vhat, axis=-1, keepdims=True))
            dy_ref[:, gs] = dv * gate[:, gs]
            dz_ref[:, gs] = (dv * yv[:, gs] * dgate_dz[:, gs]).astype(BF16)

    row = pl.BlockSpec((tm, DI), lambda i: (i, 0))
    return pl.pallas_call(body, grid=(L // tm,), in_specs=[row, row, pl.BlockSpec((1, DI), lambda i: (0, 0)), row],
                          out_specs=[row, row, pl.BlockSpec((8, DI), lambda i: (0, 0))],
                          out_shape=[jax.ShapeDtypeStruct((L, DI), F32), jax.ShapeDtypeStruct((L, ZX), BF16),
                                     jax.ShapeDtypeStruct((8, DI), F32)],
                          compiler_params=_params(("arbitrary",)), name=name)(y, zx, nw, dyn)


def _adam_update(w_ref, g_ref, m_ref, v_ref, outs):
    gv = g_ref[...]
    mn = ADAM_B1 * m_ref[...] + (1.0 - ADAM_B1) * gv
    vn = ADAM_B2 * v_ref[...] + (1.0 - ADAM_B2) * (gv * gv)
    m_hat = mn / (1.0 - ADAM_B1 ** ADAM_STEP)
    v_hat = vn / (1.0 - ADAM_B2 ** ADAM_STEP)
    d_ref, mo_ref, vo_ref = outs
    d_ref[...] = -ADAM_LR * (m_hat / (jnp.sqrt(v_hat) + ADAM_EPS) + ADAM_WD * w_ref[...])
    mo_ref[...] = mn
    vo_ref[...] = vn


def _adamw_parts(w, parts, m, v, *, name):
    R, C = w.shape
    n_parts = len(parts)
    rows = R // n_parts
    tr = max([t for t in range(8, rows + 1, 8) if rows % t == 0 and t * C <= 256 * 1024], default=rows)
    steps = rows // tr
    assert all(g_row % tr == 0 for _, g_row in parts), (name, tr)

    def body(w_ref, m_ref, v_ref, *rest):
        g_refs, outs = rest[:n_parts], rest[n_parts:]
        for p, g_ref in enumerate(g_refs):
            @pl.when(pl.program_id(0) // steps == p)
            def _(g_ref=g_ref):
                _adam_update(w_ref, g_ref, m_ref, v_ref, outs[1:])
                outs[0][...] = g_ref[...]

    blk = pl.BlockSpec((tr, C), lambda i: (i, 0))
    g_specs = [pl.BlockSpec((tr, C), lambda i, p=p, g_row=g_row: (jnp.clip(i - p * steps, 0, steps - 1) + g_row // tr, 0))
               for p, (_, g_row) in enumerate(parts)]
    return pl.pallas_call(body, grid=(R // tr,), in_specs=[blk, blk, blk] + g_specs, out_specs=[blk] * 4,
                          out_shape=[jax.ShapeDtypeStruct(w.shape, F32)] * 4,
                          compiler_params=_params(("parallel",)), name=name)(w, m, v, *[g for g, _ in parts])


def _adamw(w, g, m, v, *, name, g_row=0, w_row=0, rows=None, into=None, emit_g=False):
    lead = w.ndim == 3
    R, C = w.shape[-2:]
    rows = R if rows is None else rows
    tr = max([t for t in range(8, rows + 1, 8) if rows % t == 0 and t * C <= 256 * 1024], default=rows)
    assert g_row % tr == 0 and w_row % tr == 0, (name, g_row, w_row, tr)
    n_out = 4 if emit_g else 3

    def body(w_ref, g_ref, m_ref, v_ref, *rest):
        outs = rest[-n_out:]
        _adam_update(w_ref, g_ref, m_ref, v_ref, outs[-3:])
        if emit_g:
            outs[0][...] = g_ref[...]

    blk = (pl.BlockSpec((None, tr, C), lambda i: (0, i + w_row // tr, 0)) if lead
           else pl.BlockSpec((tr, C), lambda i: (i + w_row // tr, 0)))
    args, in_specs, alias = [w, g, m, v], [blk, pl.BlockSpec((tr, C), lambda i: (i + g_row // tr, 0)), blk, blk], {}
    if into is not None:
        args, in_specs, alias = args + list(into), in_specs + [ANY] * n_out, {4 + k: k for k in range(n_out)}
    return pl.pallas_call(body, grid=(rows // tr,), in_specs=in_specs, out_specs=[blk] * n_out,
                          out_shape=[jax.ShapeDtypeStruct(w.shape, F32)] * n_out, input_output_aliases=alias,
                          compiler_params=_params(("parallel",)), name=name)(*args)


def _residual(acc, xv, gv):
    return xv + gv * acc, acc


def _like(buf):
    return jax.ShapeDtypeStruct(buf.shape, buf.dtype)


def _mlp_fwd(x, mod, nw, wb, up_row, down_row, tag, midway=None):
    sh, sc, g = mod
    h = _modnorm_fwd(x, nw, sc, sh, name=tag + "_norm")
    a = _matmul(h, wb, n=DFF, tm=TM_ALL, b_spec=pl.BlockSpec((None, D, 512), lambda mi, j: (j // 2, up_row // D, j % 2)),
                epi=lambda acc: (jnp.maximum(acc, 0.0),), out_dtypes=(BF16,), name=tag + "_up")
    if midway is not None:
        midway(a)
    xn, y = _matmul(a, wb, n=D, tm=TM_HALF, contract=_nn_split_sq,
                    b_spec=pl.BlockSpec((N_CHIPS, D, 512), lambda mi, j: (0, down_row // D, j)),
                    extras=(x, g), epi=_residual, out_dtypes=(F32, BF16), name=tag + "_down")
    return xn, (x, h, a, y)


def _mlp_bwd(dxo, dy, gsum, saved, mod, nw, wb, gb, up_row, down_row, below, tag):
    x, h, a, y = saved
    sh, sc, g = mod
    du = _matmul(dy, wb, n=DFF, tm=TM_ALL, contract=_nt,
                 b_spec=pl.BlockSpec((None, 512, D), lambda mi, j: (j // 2, down_row // 512 + j % 2, 0)),
                 extras=(a,), epi=lambda acc, av: (acc * (2.0 * av.astype(F32)),), out_dtypes=(BF16,), name=tag + "_dact")
    gb = _matmul_tn(a, dy, m=DFF, n=D, tm=D, tn=D, a_square=True, into=gb, out_struct=_like(wb),
                    out_spec=pl.BlockSpec((None, D, D), lambda mi, j: (mi, down_row // D, 0)), name=tag + "_ddown")
    dh = _matmul(du, wb, n=D, tm=TM_HALF, contract=_nt_split,
                 b_spec=pl.BlockSpec((N_CHIPS, 512, D), lambda mi, j: (0, up_row // 512 + j, 0)), out_dtypes=(BF16,),
                 name=tag + "_dh")
    gb = _matmul_tn(h, du, m=D, n=DFF, tm=D, into=gb, out_struct=_like(wb),
                    out_spec=pl.BlockSpec((None, D, 512), lambda mi, j: (j // 2, up_row // D, j % 2)), name=tag + "_dup")
    dx, sums, *nxt = _modnorm_bwd(x, dh, dxo, nw, sc, gsum, below, name=tag + "_dnorm")
    return dx, gb, sums, *nxt


def _ssd_fwd_scan(x, mod, nw, w_in_t, w_dt_t, conv_w, conv_b, prm, tag):
    sh, sc, g = mod
    h = _modnorm_fwd(x, nw, sc, sh, name=tag + "_norm")
    zx = _matmul(h, w_in_t, n=ZX, tm=TM_ALL, contract=_nt, out_dtypes=(BF16,), name=tag + "_in")
    dtr = _matmul(h, w_dt_t, n=LANES, tm=TM_ALL, contract=_nt, name=tag + "_in_dt")
    xbc, pre = _ssd_conv_fwd(zx, conv_w, conv_b, name=tag + "_conv")
    y, sprev = _ssd_fwd(xbc, dtr, prm, name=tag + "_scan")
    return h, zx, dtr, xbc, y, sprev, pre


def _ssd_fwd_out(x, mod, scan, gn_w, get_w_out, tag):
    sh, sc, g = mod
    h, zx, dtr, xbc, y, sprev, pre = scan
    yn = _gnorm_fwd(y, zx, gn_w, name=tag + "_gnorm")
    w_out = get_w_out(yn)
    xn, yo = _matmul(yn, w_out, n=D, tm=TM_HALF, contract=_nn_split,
                     b_spec=pl.BlockSpec((N_CHIPS, 512, 512), lambda mi, j: (0, 0, j)),
                     extras=(x, g), epi=_residual, out_dtypes=(F32, BF16), name=tag + "_out")
    return xn, (x, h, zx, dtr, xbc, y, sprev, yn, yo, pre)


def _ssd_bwd_out(dyo, saved, w_out, tag, after):
    x, h, zx, dtr, xbc, y, sprev, yn, yo, pre = saved
    dyn = _matmul(dyo, w_out, n=DI, tm=TM_ALL, contract=_nt, b_spec=pl.BlockSpec((None, 512, D), lambda mi, j: (j, 0, 0)),
                  extras=(jnp.broadcast_to(after[0:1, 0:1], (1, DI)),), epi=lambda acc, t: (acc + t,),
                  out_dtypes=(BF16,), name=tag + "_dyn")
    g_out = _matmul_tn(yn, dyo, m=DI, n=D, tn=D, out_struct=_like(w_out),
                       out_spec=pl.BlockSpec((None, 512, D), lambda mi, j: (mi, 0, 0)), name=tag + "_dout")
    return dyn, g_out


def _ssd_bwd_rest(dxo, dy, dzx, gsum, saved, mod, nw, w_in_t, w_dt_t, conv_w, prm, tag):
    x, h, zx, dtr, xbc, y, sprev, yn, yo, pre = saved
    sh, sc, g = mod
    dxbc, ddtr, ssum = _ssd_bwd(xbc, dtr, prm, dy, sprev, name=tag + "_dscan")
    dzx, csum = _ssd_conv_bwd(zx, pre, dxbc, conv_w, dzx, name=tag + "_dconv")
    dh_dt = _matmul(ddtr, w_dt_t, n=D, tm=TM_ALL, name=tag + "_dh_dt")
    dh = _matmul(dzx, w_in_t, n=D, tm=TM_HALF, b_spec=pl.BlockSpec((ZX, 512), lambda mi, j: (0, j)), extras=(dh_dt,),
                 epi=lambda acc, e: (acc + e,), out_dtypes=(BF16,), name=tag + "_dh")
    d_w_zx = _matmul_tn(h, dzx, m=D, n=ZX, tm=D, out_spec=pl.BlockSpec((D, 512), lambda i, j: (i, j)),
                        out_struct=jax.ShapeDtypeStruct((D, ZX), BF16), name=tag + "_din")
    d_w_dt = _matmul_tn(h, ddtr, m=D, n=LANES, tm=D, name=tag + "_din_dt")
    dx, sums = _modnorm_bwd(x, dh, dxo, nw, sc, gsum, None, name=tag + "_dnorm")
    return dx, d_w_zx, d_w_dt, sums, csum, ssum


def _sc_layer_fwd(x, mod, nw, w_sc_in, conv_w, wb, out_row, tag, midway=None):
    sh, sc, g = mod
    h = _modnorm_fwd(x, nw, sc, sh, name=tag + "_norm")
    proj = _matmul(h, w_sc_in, n=3 * D, tm=TM_ALL, tn=256, out_dtypes=(BF16,),
                   b_spec=pl.BlockSpec((None, D, 256), lambda mi, j: (j // 3, 0, j % 3)),
                   name=tag + "_in")
    if midway is not None:
        midway(proj)
    yv, v = _sc_fwd(proj, conv_w, name=tag + "_conv")
    xn, yo = _matmul(yv, wb, n=D, tm=TM_HALF, contract=_nn_split,
                     b_spec=pl.BlockSpec((N_CHIPS, 256, 512), lambda mi, j: (0, out_row // 256, j)),
                     extras=(x, g), epi=_residual, out_dtypes=(F32, BF16), name=tag + "_out")
    return xn, (x, h, proj, yv, yo, v)


def _sc_layer_bwd(dxo, dyo, gsum, saved, mod, nw, w_sc_in, conv_w, wb, gb, out_row, below, tag):
    x, h, proj, yv, yo, v = saved
    sh, sc, g = mod
    L = x.shape[0]
    dyv = _matmul(dyo, wb, n=D, tm=TM_ALL, tn=256, contract=_nt,
                  b_spec=pl.BlockSpec((None, 256, D), lambda mi, j: (j, out_row // 256, 0)), out_dtypes=(BF16,),
                  name=tag + "_dyv")
    gb = _matmul_tn(yv, dyo, m=D, n=D, tm=256, tn=D, into=gb, out_struct=_like(wb),
                    out_spec=pl.BlockSpec((None, 256, D), lambda mi, j: (mi, out_row // 256, 0)), name=tag + "_dout")
    dproj, csum = _sc_bwd(proj, v, dyv, conv_w, name=tag + "_dconv")
    tm = min(L, TM_HALF)
    dh = _matmul(dproj, w_sc_in, n=D, tm=tm, contract=_nt_sc_in, a_spec=pl.BlockSpec((3, tm, D), lambda mi, j: (0, mi, 0)),
                 b_spec=pl.BlockSpec((N_CHIPS, 512, SC_IN_SHARD), lambda mi, j: (0, j, 0)), out_dtypes=(BF16,),
                 name=tag + "_dh")
    g_sc_in = _matmul_tn(h, dproj, m=D, n=3 * D, tm=D, tn=256, b_spec=pl.BlockSpec((None, L, 256), lambda mi, j: (j // 4, 0, j % 4)),
                         out_spec=pl.BlockSpec((None, D, 256), lambda mi, j: (j // 3, 0, j % 3)),
                         out_struct=jax.ShapeDtypeStruct((N_CHIPS, D, SC_IN_SHARD), BF16), name=tag + "_din")
    dx, sums, *nxt = _modnorm_bwd(x, dh, dxo, nw, sc, gsum, below, name=tag + "_dnorm")
    return dx, gb, g_sc_in, sums, csum, *nxt


SUB_ROW = (0, 8, 16, 24)
SSD_CONV_ROW, GNORM_ROW, FINAL_ROW, SC_CONV_ROW, HEAD_ROW, SMALL_ROWS = 32, 48, 56, 64, 72, 80


def _all_gather_rows(blk, *, name):
    m_per, n = blk.shape

    def body(x_ref, out_ref, send_sems, recv_sems, local_sem):
        x, y, c = lax.axis_index("x"), lax.axis_index("y"), lax.axis_index("c")
        me, sibling = (x, y, c), (x, y, 1 - c)
        chips = [(1 - x, y), (x, 1 - y), (1 - x, 1 - y)]

        def rows(px, py, pc):
            return out_ref.at[pl.ds((4 * px + 2 * py + pc) * m_per, m_per), :]

        def copy(k, block, to, src=None):
            return pltpu.make_async_remote_copy(src_ref=rows(*block) if src is None else src, dst_ref=rows(*block),
                                                send_sem=send_sems.at[k], recv_sem=recv_sems.at[k], device_id=to,
                                                device_id_type=MESH)

        mine = pltpu.make_async_copy(x_ref, rows(*me), local_sem)
        mine.start()
        first = [copy(0, me, sibling, src=x_ref)] + [copy(1 + j, me, (*chip, c), src=x_ref) for j, chip in enumerate(chips)]
        for cp in first:
            cp.start()
        passed = [copy(4 + j, (*chip, c), sibling) for j, chip in enumerate(chips)]
        for j, chip in enumerate(chips):
            copy(1 + j, (*chip, c), me).wait_recv()
            passed[j].start()
        copy(0, sibling, me).wait_recv()
        for j, chip in enumerate(chips):
            copy(4 + j, (*chip, 1 - c), me).wait_recv()
        for cp in first + passed:
            cp.wait_send()
        mine.wait()

    return pl.pallas_call(
        body, out_shape=jax.ShapeDtypeStruct((N_DEV * m_per, n), blk.dtype),
        in_specs=[pl.BlockSpec(memory_space=pltpu.VMEM)], out_specs=pl.BlockSpec(memory_space=pltpu.VMEM),
        scratch_shapes=[pltpu.SemaphoreType.DMA((7,)), pltpu.SemaphoreType.DMA((7,)), pltpu.SemaphoreType.DMA],
        name=name)(blk)


def _half(ref, chip, c):
    r, n = ref.shape[1:]
    if r % 32 == 0:
        return ref.at[chip, pl.ds(c * (r // 2), r // 2), :]
    assert n % 256 == 0, ref.shape
    return ref.at[chip, :, pl.ds(c * (n // 2), n // 2)]


def _gather_copy(bufs, sends, recvs, b, k, chip, pc, to):
    piece = _half(bufs[b], 2 * chip[0] + chip[1], pc)
    return pltpu.make_async_remote_copy(src_ref=piece, dst_ref=piece, send_sem=sends.at[4 * b + k], recv_sem=recvs.at[4 * b + k],
                                        device_id=to, device_id_type=MESH)


def _split_call(body, bufs, sems_in, n_sems, *, name, after=(), token=False, lands=()):
    nb, na, nl, starts = len(bufs), len(after), len(lands), not sems_in

    def wrapped(*refs):
        sems = refs[nb + na:nb + na + 2] if starts else refs[nb:nb + 2]
        made = refs[nb + na + 2 + nb:nb + na + 2 + nb + nl] if starts else ()
        body(tuple(refs[:nb]) + tuple(made), sems[0], sems[1])
        if token:
            refs[-1][...] = jnp.zeros_like(refs[-1])

    out_shape = [pltpu.SemaphoreType.DMA((n_sems,)) for _ in range(2 if starts else 0)]
    out_specs = [SEM] * len(out_shape) + [ANY] * (nb + nl)
    alias = {b: len(out_shape) + b for b in range(nb)}
    out_shape += [jax.ShapeDtypeStruct(b.shape, b.dtype) for b in bufs] + list(lands)
    if token:
        out_shape.append(jax.ShapeDtypeStruct((8, LANES), F32))
        out_specs.append(pl.BlockSpec(memory_space=pltpu.VMEM))
    return pl.pallas_call(
        wrapped, out_shape=out_shape, in_specs=[ANY] * nb + [SEM] * len(sems_in) + [ANY] * na, out_specs=out_specs,
        input_output_aliases=alias,
        compiler_params=pltpu.CompilerParams(has_side_effects=pltpu.SideEffectType.DATAFLOW_SIDE_EFFECTING),
        name=name)(*bufs, *sems_in, *after)


def _gather_start(bufs, *, name, after=()):
    nb = len(bufs)

    def body(ins, sends, recvs):
        x, y, c = lax.axis_index("x"), lax.axis_index("y"), lax.axis_index("c")
        chips = [(1 - x, y), (x, 1 - y), (1 - x, 1 - y)]
        for b in range(nb):
            _gather_copy(ins, sends, recvs, b, 0, (x, y), c, (x, y, 1 - c)).start()
            for j, chip in enumerate(chips):
                _gather_copy(ins, sends, recvs, b, 1 + j, (x, y), c, (*chip, c)).start()

    out = _split_call(body, bufs, (), 4 * nb, name=name, after=after, token=True)
    return (out[0], out[1], out[2:2 + nb]), out[-1]


def _gather_wait_first(flight, *, name, after=()):
    sends, recvs, bufs = flight
    nb = len(bufs)

    def body(ins, sends_, recvs_):
        x, y, c = lax.axis_index("x"), lax.axis_index("y"), lax.axis_index("c")
        chips = [(1 - x, y), (x, 1 - y), (1 - x, 1 - y)]
        for b in range(nb):
            _gather_copy(ins, sends_, recvs_, b, 0, (x, y), c, (x, y, 1 - c)).wait_send()
            _gather_copy(ins, sends_, recvs_, b, 0, (x, y), 1 - c, (x, y, c)).wait_recv()
            for j, chip in enumerate(chips):
                _gather_copy(ins, sends_, recvs_, b, 1 + j, (x, y), c, (*chip, c)).wait_send()
                _gather_copy(ins, sends_, recvs_, b, 1 + j, chip, c, (x, y, c)).wait_recv()

    return _split_call(body, bufs, (sends, recvs), 4 * nb, name=name, after=after)


def _gather_forward(bufs, *, name):
    nb = len(bufs)

    def body(ins, sends, recvs):
        x, y, c = lax.axis_index("x"), lax.axis_index("y"), lax.axis_index("c")
        chips = [(1 - x, y), (x, 1 - y), (1 - x, 1 - y)]
        for b in range(nb):
            for j, chip in enumerate(chips):
                _gather_copy(ins, sends, recvs, b, 1 + j, chip, c, (x, y, 1 - c)).start()

    out = _split_call(body, bufs, (), 4 * nb, name=name)
    return out[0], out[1], out[2:2 + nb]


def _gather_wait_forward(flight, *, name, after=()):
    sends, recvs, bufs = flight
    nb = len(bufs)

    def body(ins, sends_, recvs_):
        x, y, c = lax.axis_index("x"), lax.axis_index("y"), lax.axis_index("c")
        chips = [(1 - x, y), (x, 1 - y), (1 - x, 1 - y)]
        for b in range(nb):
            for j, chip in enumerate(chips):
                _gather_copy(ins, sends_, recvs_, b, 1 + j, chip, c, (x, y, 1 - c)).wait_send()
                _gather_copy(ins, sends_, recvs_, b, 1 + j, chip, 1 - c, (x, y, c)).wait_recv()

    return _split_call(body, bufs, (sends, recvs), 4 * nb, name=name, after=after)


def _owner_copies(hs, lands, sends, recvs):
    x, y, c = lax.axis_index("x"), lax.axis_index("y"), lax.axis_index("c")
    chips = [(1 - x, y), (x, 1 - y), (1 - x, 1 - y)]
    return [pltpu.make_async_remote_copy(src_ref=hs[b].at[2 * cx + cy], dst_ref=lands[b].at[j], send_sem=sends.at[3 * b + j],
                                         recv_sem=recvs.at[3 * b + j], device_id=(cx, cy, c), device_id_type=MESH)
            for b in range(len(hs)) for j, (cx, cy) in enumerate(chips)]


def _owners_start(hs, *, name):
    nb = len(hs)
    lands = [jax.ShapeDtypeStruct((3,) + h.shape[1:], h.dtype) for h in hs]

    def body(refs, sends, recvs):
        for cp in _owner_copies(refs[:nb], refs[nb:], sends, recvs):
            cp.start()

    out = _split_call(body, list(hs), (), 3 * nb, name=name, token=True, lands=lands)
    return (out[0], out[1], out[2:2 + 2 * nb]), out[-1]


def _owners_wait(flight, *, name, after=()):
    sends, recvs, bufs = flight
    nb = len(bufs) // 2

    def body(refs, sends_, recvs_):
        for cp in _owner_copies(refs[:nb], refs[nb:], sends_, recvs_):
            cp.wait()

    out = _split_call(body, bufs, (sends, recvs), 3 * nb, name=name, after=after)
    return out[:nb], out[nb:]


def _sibling_copies(gs, lands, sends, recvs):
    x, y, c = lax.axis_index("x"), lax.axis_index("y"), lax.axis_index("c")
    copies = []
    for b in range(len(gs)):
        hr = gs[b].shape[1] // 2
        copies.append(pltpu.make_async_remote_copy(
            src_ref=gs[b].at[:, pl.ds((1 - c) * hr, hr), :], dst_ref=lands[b], send_sem=sends.at[b], recv_sem=recvs.at[b],
            device_id=(x, y, 1 - c), device_id_type=MESH))
    return copies


def _sibling_start(gs, *, name, after=()):
    nb = len(gs)
    lands = [jax.ShapeDtypeStruct((g.shape[0], g.shape[1] // 2, g.shape[2]), g.dtype) for g in gs]

    def body(refs, sends, recvs):
        for cp in _sibling_copies(refs[:nb], refs[nb:], sends, recvs):
            cp.start()

    out = _split_call(body, list(gs), (), nb, name=name, after=after, token=True, lands=lands)
    return (out[0], out[1], out[2:2 + 2 * nb]), out[-1]


def _sibling_wait(flight, *, name, after=()):
    sends, recvs, bufs = flight
    nb = len(bufs) // 2

    def body(refs, sends_, recvs_):
        for cp in _sibling_copies(refs[:nb], refs[nb:], sends_, recvs_):
            cp.wait()

    out = _split_call(body, bufs, (sends, recvs), nb, name=name, after=after)
    return out[:nb], out[nb:]


def _result_copies(ts, sends, recvs):
    x, y, c = lax.axis_index("x"), lax.axis_index("y"), lax.axis_index("c")
    return [pltpu.make_async_remote_copy(src_ref=ts[b].at[c], dst_ref=ts[b].at[c], send_sem=sends.at[b], recv_sem=recvs.at[b],
                                         device_id=(x, y, 1 - c), device_id_type=MESH) for b in range(len(ts))]


def _result_start(ts, *, name):
    def body(refs, sends, recvs):
        for cp in _result_copies(refs, sends, recvs):
            cp.start()

    out = _split_call(body, ts, (), len(ts), name=name, token=True)
    return (out[0], out[1], out[2:2 + len(ts)]), out[-1]


def _result_wait(flight, *, name, after=()):
    sends, recvs, bufs = flight

    def body(refs, sends_, recvs_):
        for cp in _result_copies(refs, sends_, recvs_):
            cp.wait()

    return _split_call(body, bufs, (sends, recvs), len(bufs), name=name, after=after)


def _row_tile(rows, cols):
    best = 16
    for t in range(16, rows + 1, 16):
        if rows % t == 0 and t * cols <= 640 * 1024:
            best = t
    assert rows % best == 0, (rows, cols)
    return best


def _add_sibling_half(g, recv, core, *, name):
    nk, r, n = g.shape
    hr = r // 2
    tr = _row_tile(hr, n)

    def body(c_ref, a_ref, b_ref, o_ref):
        o_ref[...] = (a_ref[...].astype(F32) + b_ref[...].astype(F32)).astype(BF16)

    grid_spec = pltpu.PrefetchScalarGridSpec(
        num_scalar_prefetch=1, grid=(nk, hr // tr),
        in_specs=[pl.BlockSpec((None, tr, n), lambda k, i, c_ref: (k, c_ref[0] * (hr // tr) + i, 0)),
                  pl.BlockSpec((None, tr, n), lambda k, i, c_ref: (k, i, 0))],
        out_specs=pl.BlockSpec((None, tr, n), lambda k, i, c_ref: (k, i, 0)))
    return pl.pallas_call(body, grid_spec=grid_spec, out_shape=jax.ShapeDtypeStruct((nk, hr, n), BF16),
                          compiler_params=_params(("parallel", "parallel")), name=name)(core, g, recv)


def _add_chip_sums(h, recv, chip_core, *, name):
    _, hr, n = h.shape
    tr = _row_tile(hr, n)

    def body(k_ref, a_ref, b_ref, o_ref):
        o_ref[...] = ((a_ref[...].astype(F32) + b_ref[0].astype(F32)) + b_ref[1].astype(F32)) + b_ref[2].astype(F32)

    grid_spec = pltpu.PrefetchScalarGridSpec(
        num_scalar_prefetch=1, grid=(hr // tr,),
        in_specs=[pl.BlockSpec((None, tr, n), lambda i, k_ref: (k_ref[0], i, 0)),
                  pl.BlockSpec((3, tr, n), lambda i, k_ref: (0, i, 0))],
        out_specs=pl.BlockSpec((None, tr, n), lambda i, k_ref: (k_ref[1], i, 0)))
    return pl.pallas_call(body, grid_spec=grid_spec, out_shape=jax.ShapeDtypeStruct((2, hr, n), F32),
                          compiler_params=_params(("parallel",)), name=name)(chip_core, h, recv)


def _sum_devices(g, *, name, after=()):
    nd, r, n = g.shape

    def body(g_ref, *rest):
        acc = g_ref[0]
        for i in range(1, nd):
            acc = acc + g_ref[i]
        rest[-1][...] = acc

    return pl.pallas_call(body, out_shape=jax.ShapeDtypeStruct((r, n), F32),
                          in_specs=[pl.BlockSpec(memory_space=pltpu.VMEM)] + [ANY] * len(after), name=name)(g, *after)


def _own_slot(parts, chip, *, name, after=()):
    rows, cols = sum(w.shape[1] for w, _ in parts), parts[0][0].shape[2]
    buf, row0 = None, 0
    for p, (w, idx) in enumerate(parts):
        r = w.shape[1]
        tr = 256 if r % 256 == 0 else r
        assert row0 % tr == 0, (name, r, row0)
        prev = () if buf is None else (buf,)

        def body(chip_ref, w_ref, *rest):
            rest[-1][...] = w_ref[...].astype(BF16)

        grid_spec = pltpu.PrefetchScalarGridSpec(
            num_scalar_prefetch=1, grid=(r // tr,),
            in_specs=[pl.BlockSpec((None, tr, cols), lambda i, c_ref, idx=idx: (idx, i, 0))] + [ANY] * (len(prev) + len(after)),
            out_specs=pl.BlockSpec((None, tr, cols), lambda i, c_ref, row0=row0, tr=tr: (c_ref[0], row0 // tr + i, 0)))
        buf = pl.pallas_call(body, grid_spec=grid_spec, out_shape=jax.ShapeDtypeStruct((N_CHIPS, rows, cols), BF16),
                             input_output_aliases={2: 0} if prev else {}, compiler_params=_params(("parallel",)),
                             name=f"{name}{p}")(chip, w, *prev, *after)
        row0 += r
    return buf


def kernel(x, c, ada_w, ada_b, mix_norm_w, mlp_norm_w, mlp_up, mlp_down, ssd_in_w, ssd_conv_w, ssd_conv_b, ssd_dt_bias, ssd_A_log, ssd_D, ssd_norm_w, ssd_out_w, sc_in_w, sc_conv_w, sc_out_w, final_norm_w, loss_target, m_ada_w, m_ada_b, m_mix_norm_w, m_mlp_norm_w, m_mlp_up, m_mlp_down, m_ssd_in_w, m_ssd_conv_w, m_ssd_conv_b, m_ssd_dt_bias, m_ssd_A_log, m_ssd_D, m_ssd_norm_w, m_ssd_out_w, m_sc_in_w, m_sc_conv_w, m_sc_out_w, m_final_norm_w, v_ada_w, v_ada_b, v_mix_norm_w, v_mlp_norm_w, v_mlp_up, v_mlp_down, v_ssd_in_w, v_ssd_conv_w, v_ssd_conv_b, v_ssd_dt_bias, v_ssd_A_log, v_ssd_D, v_ssd_norm_w, v_ssd_out_w, v_sc_in_w, v_sc_conv_w, v_sc_out_w, v_final_norm_w):
    xi, yi, ci = lax.axis_index("x"), lax.axis_index("y"), lax.axis_index("c")
    chip = 2 * xi + yi
    dev = 2 * chip + ci
    n_ada = ada_w.shape[2]

    conv_flat = jnp.concatenate([ssd_conv_w.reshape(-1), sc_conv_w.reshape(-1), jnp.zeros((256,), F32)]).reshape(4, D)
    blk0 = jnp.concatenate([c, conv_flat, jnp.zeros((3, D), F32)], axis=0)
    got0 = _all_gather_rows(blk0, name="gather_cond").reshape(N_DEV, 8, D)
    c_all = got0[:, 0]
    conv_all = got0[0::2, 1:5].reshape(N_CHIPS, 4 * D)
    ssd_conv = jnp.moveaxis(conv_all[:, :4 * 768].reshape(N_CHIPS, 4, 768), 0, 1).reshape(4, CONVD)
    sc_conv = jnp.moveaxis(conv_all[:, 4 * 768:4 * 768 + 3 * 256].reshape(N_CHIPS, 3, 256), 0, 1).reshape(3, D)
    ada_tiles = n_ada // 512
    mod_both = _matmul(c_all, ada_w, n=2 * n_ada, a_silu=True,
                       b_spec=pl.BlockSpec((None, D, 512), lambda mi, j: (j // ada_tiles, 0, j % ada_tiles)),
                       extras=(lax.dynamic_slice(ada_b, (0, chip * n_ada), (2, n_ada)).reshape(1, 2 * n_ada),),
                       epi=lambda acc, b: (acc + b,), name="ada_mod")
    mod_rows_all = jnp.concatenate([mod_both[:, :n_ada], mod_both[:, n_ada:]], axis=0)
    mod_slot = lax.dynamic_update_slice(jnp.zeros((N_CHIPS, 2 * N_DEV, n_ada), F32), mod_rows_all[None], (chip, 0, 0))

    up_row, down_row = 0, D
    chip1 = chip.reshape(1).astype(jnp.int32)
    a_bufs = [mod_slot, _own_slot([(jnp.swapaxes(ssd_in_w, 1, 2), 0)], chip1, name="slot_ssd_in")]
    fly_a, tok = _gather_start(a_bufs, name="gather_a_start")
    b_bufs = [_own_slot([(ssd_out_w, 0)], chip1, name="slot_ssd_out", after=(tok,)),
              _own_slot([(mlp_up, 0), (mlp_down, 0)], chip1, name="slot_mlp0_", after=(tok,))]
    fly_b, tok = _gather_start(b_bufs, name="gather_b_start", after=(tok,))
    c_bufs = [_own_slot([(sc_in_w, 0)], chip1, name="slot_sc_in", after=(tok,)),
              _own_slot([(sc_out_w, 0)], chip1, name="slot_sc_out", after=(tok,))]
    fly_c, tok = _gather_start(c_bufs, name="gather_c_start", after=(tok,))
    d_bufs = [_own_slot([(mlp_up, 1), (mlp_down, 1)], chip1, name="slot_mlp1_", after=(tok,))]
    fly_d, tok = _gather_start(d_bufs, name="gather_d_start", after=(tok,))

    row = lambda v: v.reshape(1, -1)
    xs, tgt = x[0], loss_target[0]
    prm = jnp.pad(jnp.concatenate([ssd_dt_bias, ssd_A_log, ssd_D, jnp.zeros((5, NH), F32)], axis=0), ((0, 0), (0, LANES - NH)))
    mix_nw = [row(mix_norm_w[i]) for i in range(2)]
    mlp_nw = [row(mlp_norm_w[i]) for i in range(2)]
    a_bufs = _gather_wait_first(fly_a, name="gather_a_landed", after=(tok,))
    mod_all, w_ssd_in = _gather_wait_forward(_gather_forward(a_bufs, name="gather_a_pass"), name="gather_a_done")
    mod = lax.dynamic_index_in_dim(mod_all.reshape(N_CHIPS, 2, N_DEV, n_ada), dev, axis=2, keepdims=False)
    mod = jnp.moveaxis(mod, 0, 1).reshape(2, 6, D)
    mods = [[mod[i, j:j + 1] for j in range(6)] for i in range(2)]
    w_in_t = w_ssd_in.reshape(N_CHIPS * SSD_IN_SHARD, D)
    w_dt_t = jnp.pad(w_in_t[ZX:], ((0, LANES - NH), (0, 0)))
    scan = _ssd_fwd_scan(xs, mods[0][0:3], mix_nw[0], w_in_t, w_dt_t, ssd_conv, ssd_conv_b, prm, "ssd")

    def land(flight, tag, after):
        return _gather_forward(_gather_wait_first(flight, name=f"gather_{tag}_landed", after=(after,)), name=f"gather_{tag}_pass")

    passed, got = {"b": land(fly_b, "b", scan[4])}, {}

    def done(tag, after):
        got[tag] = _gather_wait_forward(passed[tag], name=f"gather_{tag}_done", after=(after,))
        return got[tag]

    x1, s_ssd = _ssd_fwd_out(xs, mods[0][0:3], scan, ssd_norm_w, lambda yn: done("b", yn)[0], "ssd")
    w_ssd_out, w_b = got["b"]
    x2, s_mlp0 = _mlp_fwd(x1, mods[0][3:6], mlp_nw[0], w_b, up_row, down_row, "mlp0",
                          midway=lambda a: passed.update(c=land(fly_c, "c", a)))
    w_sc_in, w_sc_out = done("c", x2)
    x3, s_sc = _sc_layer_fwd(x2, mods[1][0:3], mix_nw[1], w_sc_in, sc_conv, w_sc_out, 0, "sc",
                             midway=lambda proj: passed.update(d=land(fly_d, "d", proj)))
    (w_mlp1,) = done("d", x3)
    x4, s_mlp1 = _mlp_fwd(x3, mods[1][3:6], mlp_nw[1], w_mlp1, up_row, down_row, "mlp1")

    core = ci.reshape(1).astype(jnp.int32)
    chip_core = jnp.stack([chip, ci]).astype(jnp.int32)

    def reduce_swap(gbufs, tag, after=()):
        return _sibling_start(gbufs, name=tag + "_sibling_start", after=after)

    def reduce_send(flight, tag, after):
        gs, sib = _sibling_wait(flight, name=tag + "_sibling_landed", after=after)
        hs = [_add_sibling_half(g, s, core, name=f"{tag}_add_sibling{b}") for b, (g, s) in enumerate(zip(gs, sib))]
        return _owners_start(hs, name=tag + "_owners_start")

    def reduce_sum(flight, tag, after):
        hs, lands = _owners_wait(flight, name=tag + "_owners_landed", after=after)
        ts = [_add_chip_sums(h, o, chip_core, name=f"{tag}_add_chips{b}") for b, (h, o) in enumerate(zip(hs, lands))]
        return _result_start(ts, name=tag + "_result_start")

    def reduce_done(flight, tag, after=()):
        return [t.reshape(-1, t.shape[2]) for t in _result_wait(flight, name=tag + "_result_landed", after=after)]

    dx4, fsum, dy, gs = _final_loss(x4, row(final_norm_w), tgt, (mods[1][5], s_mlp1[3]), name="final_loss")
    dx3, g_mlp1, sum_mlp1, dy, gs = _mlp_bwd(dx4, dy, gs, s_mlp1, mods[1][3:6], mlp_nw[1], w_mlp1, None, up_row, down_row,
                                             (mods[1][2], s_sc[4]), "mlp1")
    dx2, g_sc_out, g_sc_in, sum_sc, sc_csum, dy, gs = _sc_layer_bwd(dx3, dy, gs, s_sc, mods[1][0:3], mix_nw[1], w_sc_in,
                                                                    sc_conv, w_sc_out, None, 0, (mods[0][5], s_mlp0[3]), "sc")
    dx1, g_b, sum_mlp0, dy, gsum_ssd = _mlp_bwd(dx2, dy, gs, s_mlp0, mods[0][3:6], mlp_nw[0], w_b, None, up_row, down_row,
                                                (mods[0][2], s_ssd[8]), "mlp0")
    fly_1, tok = reduce_swap([g_mlp1, g_sc_out, g_sc_in, g_b], "rs1")
    dyn, g_ssd_out = _ssd_bwd_out(dy, s_ssd, w_ssd_out, "ssd", tok)
    fly_1, tok = reduce_send(fly_1, "rs1", (g_ssd_out,))
    dy, dzx, gnsum = _gnorm_bwd(s_ssd[5], s_ssd[2], ssd_norm_w + tok[0:1, 0:1], dyn, name="ssd_dgnorm")
    grad_x, d_w_zx, d_w_dt, sum_ssd, csum, ssum = _ssd_bwd_rest(
        dx1, dy, dzx, gsum_ssd, s_ssd, mods[0][0:3], mix_nw[0], w_in_t, w_dt_t, ssd_conv, prm, "ssd")

    def ssd_in_owner(k):
        lo, hi = k * SSD_IN_SHARD, (k + 1) * SSD_IN_SHARD
        if hi <= ZX:
            return d_w_zx[:, lo:hi]
        return jnp.concatenate([d_w_zx[:, lo:], d_w_dt[:, :hi - ZX].astype(BF16)], axis=1)

    small = jnp.concatenate([sum_ssd, sum_mlp0, sum_sc, sum_mlp1, csum.reshape(24, D)[0:16], gnsum.reshape(16, D)[0:8],
                             fsum, sc_csum, jnp.pad(ssum, ((0, 0), (0, D - LANES)))], axis=0)
    small_slot = lax.dynamic_update_slice(jnp.zeros((N_CHIPS, 2 * SMALL_ROWS, D), F32), small[None], (chip, ci * SMALL_ROWS, 0))
    fly_2, tok = reduce_swap([jnp.stack([ssd_in_owner(k) for k in range(N_CHIPS)]), g_ssd_out], "rs2")
    fly_s, tok = _gather_start([small_slot], name="gather_small_start", after=(tok,))
    fly_1, tok = reduce_sum(fly_1, "rs1", (grad_x, tok))
    fly_2, tok = reduce_send(fly_2, "rs2", (tok,))
    fly_s = _gather_forward(_gather_wait_first(fly_s, name="gather_small_landed", after=(tok,)), name="gather_small_pass")
    (small_all,) = _gather_wait_forward(fly_s, name="gather_small_done")
    t_mlp1, t_sc_out, t_sc_in, t_b = reduce_done(fly_1, "rs1", (small_all,))
    small_all = small_all.reshape(N_DEV, SMALL_ROWS, D)
    mod_rows = [r + o for r in SUB_ROW for o in (3, 2, 0)]
    c_pad = jnp.concatenate([c_all, jnp.zeros((8, D), F32)], axis=0)
    dmod_all = jnp.stack([small_all[:, r] for r in mod_rows], axis=1).reshape(N_DEV, 2, 6 * D)
    g_ada_w = []
    for i in range(2):
        dm = lax.dynamic_slice(dmod_all[:, i], (0, chip * n_ada), (N_DEV, n_ada))
        g_ada_w.append(_matmul_tn(c_pad, jnp.concatenate([dm, jnp.zeros_like(dm)], axis=0), m=D, n=n_ada, a_silu=True,
                                  name=f"ada_dw{i}"))

    big = dict(ada_w=[(g, 0) for g in g_ada_w], mlp_up=[(t_b, up_row), (t_mlp1, up_row)],
               mlp_down=[(t_b, down_row), (t_mlp1, down_row)], ssd_out_w=None, sc_out_w=[(t_sc_out, 0)],
               sc_in_w=[(t_sc_in, 0)], ssd_in_w=None)
    weights = dict(ada_w=(ada_w, m_ada_w, v_ada_w), ada_b=(ada_b, m_ada_b, v_ada_b),
                   mix_norm_w=(mix_norm_w, m_mix_norm_w, v_mix_norm_w), mlp_norm_w=(mlp_norm_w, m_mlp_norm_w, v_mlp_norm_w),
                   mlp_up=(mlp_up, m_mlp_up, v_mlp_up), mlp_down=(mlp_down, m_mlp_down, v_mlp_down),
                   ssd_in_w=(ssd_in_w, m_ssd_in_w, v_ssd_in_w), ssd_conv_w=(ssd_conv_w, m_ssd_conv_w, v_ssd_conv_w),
                   ssd_conv_b=(ssd_conv_b, m_ssd_conv_b, v_ssd_conv_b), ssd_dt_bias=(ssd_dt_bias, m_ssd_dt_bias, v_ssd_dt_bias),
                   ssd_A_log=(ssd_A_log, m_ssd_A_log, v_ssd_A_log), ssd_D=(ssd_D, m_ssd_D, v_ssd_D),
                   ssd_norm_w=(ssd_norm_w, m_ssd_norm_w, v_ssd_norm_w), ssd_out_w=(ssd_out_w, m_ssd_out_w, v_ssd_out_w),
                   sc_in_w=(sc_in_w, m_sc_in_w, v_sc_in_w), sc_conv_w=(sc_conv_w, m_sc_conv_w, v_sc_conv_w),
                   sc_out_w=(sc_out_w, m_sc_out_w, v_sc_out_w), final_norm_w=(final_norm_w, m_final_norm_w, v_final_norm_w))
    def step(nm, parts):
        w, m, v = (t if t.shape[0] == 1 else t.reshape(-1, t.shape[-1]) for t in weights[nm])
        if len(parts) > 1:
            return _adamw_parts(w, parts, m, v, name="adamw_" + nm)
        rows, outs = w.shape[-2] // len(parts), None
        for i, (gbuf, g_row) in enumerate(parts):
            outs = _adamw(w, gbuf, m, v, g_row=g_row, w_row=i * rows, rows=rows, into=outs, emit_g=True, name=f"adamw_{nm}{i}")
        return outs

    res = {nm: step(nm, parts) for nm, parts in big.items() if parts is not None}
    fly_2, tok = reduce_sum(fly_2, "rs2", tuple(r[1] for r in res.values()))
    tot = _sum_devices(small_all, name="sum_small", after=(tok,))
    loss = tot[FINAL_ROW + 1, 0]
    conv_sums = tot[SSD_CONV_ROW:SSD_CONV_ROW + 15].reshape(5, CONVD)
    grads = dict(ada_b=jnp.stack([tot[r] for r in mod_rows]).reshape(2, 6 * D),
                 mix_norm_w=jnp.stack([tot[SUB_ROW[0] + 1], tot[SUB_ROW[2] + 1]]),
                 mlp_norm_w=jnp.stack([tot[SUB_ROW[1] + 1], tot[SUB_ROW[3] + 1]]),
                 ssd_conv_w=lax.dynamic_slice(conv_sums, (0, chip * 768), (4, 768))[None], ssd_conv_b=conv_sums[4:5],
                 ssd_dt_bias=tot[HEAD_ROW + 2:HEAD_ROW + 3, 0:NH], ssd_A_log=tot[HEAD_ROW:HEAD_ROW + 1, 0:NH],
                 ssd_D=tot[HEAD_ROW + 1:HEAD_ROW + 2, 0:NH], ssd_norm_w=tot[GNORM_ROW:GNORM_ROW + 2].reshape(1, DI),
                 sc_conv_w=lax.dynamic_slice(tot[SC_CONV_ROW:SC_CONV_ROW + 3], (0, chip * 256), (3, 256))[None],
                 final_norm_w=tot[FINAL_ROW])
    for nm, g in grads.items():
        w, m, v = weights[nm]
        two_d = (-1, w.shape[-1]) if w.ndim > 1 else (1, -1)
        res[nm] = (g, *_adamw(w.reshape(two_d), g.reshape(two_d), m.reshape(two_d), v.reshape(two_d), name="adamw_" + nm))
    t_ssd_in, t_ssd_out = reduce_done(fly_2, "rs2", tuple(res[nm][1] for nm in grads))
    res["ssd_out_w"] = step("ssd_out_w", [(t_ssd_out, 0)])
    w_t, m_t, v_t = (jnp.swapaxes(t[0], 0, 1) for t in weights["ssd_in_w"])
    res["ssd_in_w"] = [jnp.swapaxes(o, 0, 1) for o in _adamw(w_t, t_ssd_in.T, m_t, v_t, emit_g=True, name="adamw_ssd_in_w")]
    outs = [[res[nm][k].reshape(weights[nm][0].shape) for nm in weights] for k in range(4)]
    return (loss, grad_x[None], *outs[0], *outs[1], *outs[2], *outs[3])
```
